```python
import jax, jax.numpy as jnp
from jax import lax
import numpy as np

D_MODEL = 2048
BATCH = 8
SEQ = 8192
DEPTH = 1

CHUNK = 64
LEFT_CHUNKS = 8
BAND = (LEFT_CHUNKS + 1) * CHUNK
ATT_HEADS = 16
ATT_HEAD_DIM = 64
ATT_WIDTH = ATT_HEADS * ATT_HEAD_DIM
REL_CLIP = 256
REL_FUTURE = CHUNK - 1
N_REL = REL_CLIP + REL_FUTURE + 1
SSD_HEADS = 16
SSD_HEAD_DIM = 64
SSD_WIDTH = SSD_HEADS * SSD_HEAD_DIM
SSD_GROUPS = 2
SSD_STATE = 128
SSD_CONV = 4
SSD_CHUNK = CHUNK
XBC_WIDTH = SSD_WIDTH + 2 * SSD_GROUPS * SSD_STATE
MIX_WIDTH = ATT_WIDTH + SSD_WIDTH
IN_COLS = 3 * ATT_WIDTH + SSD_WIDTH + XBC_WIDTH + SSD_HEADS
FFN_HIDDEN = -(-8 * D_MODEL // (3 * 256)) * 256
N_MOD = 6
EPS = 1e-6

kernel_name = "hymba_chunk_attn_ssd_adaln"


def rmsnorm(x, g):
    xf = x.astype(jnp.float32)
    y = xf * lax.rsqrt(jnp.mean(xf * xf, axis=-1, keepdims=True) + EPS)
    return (y * g.astype(jnp.float32)).astype(x.dtype)


def chunk_attention(q, k, v, rel_bias):
    b, s, h, dh = q.shape
    n_chunks = s // CHUNK
    pad = LEFT_CHUNKS * CHUNK
    k_pad = jnp.pad(k, ((0, 0), (pad, 0), (0, 0), (0, 0)))
    v_pad = jnp.pad(v, ((0, 0), (pad, 0), (0, 0), (0, 0)))
    q_loc = jnp.arange(CHUNK)[:, None] + pad
    k_loc = jnp.arange(BAND)[None, :]
    rel_idx = jnp.clip(q_loc - k_loc, -REL_FUTURE, REL_CLIP) + REL_FUTURE
    bias = rel_bias[:, rel_idx].astype(jnp.float32)
    scale = ATT_HEAD_DIM ** -0.5

    def one_chunk(i):
        start = i * CHUNK
        qc = lax.dynamic_slice_in_dim(q, start, CHUNK, axis=1)
        kc = lax.dynamic_slice_in_dim(k_pad, start, BAND, axis=1)
        vc = lax.dynamic_slice_in_dim(v_pad, start, BAND, axis=1)
        scores = jnp.einsum("bqhd,bkhd->bhqk", qc, kc).astype(jnp.float32) * scale + bias
        valid = (start - pad + jnp.arange(BAND)) >= 0
        scores = jnp.where(valid, scores, -jnp.inf)
        probs = jax.nn.softmax(scores, axis=-1).astype(vc.dtype)
        return jnp.einsum("bhqk,bkhd->bqhd", probs, vc)

    out = lax.map(one_chunk, jnp.arange(n_chunks))
    return jnp.moveaxis(out, 0, 1).reshape(b, s, h * dh)


def ssd_mixer(xbc_raw, z, dt_raw, conv_w, conv_b, dt_bias, a_log, d_skip, norm_g):
    b, s, ch = xbc_raw.shape
    xbc = lax.conv_general_dilated(
        xbc_raw, conv_w[:, None, :].astype(xbc_raw.dtype), window_strides=(1,),
        padding=[(SSD_CONV - 1, 0)], dimension_numbers=("NWC", "WIO", "NWC"),
        feature_group_count=ch) + conv_b
    xbc = jax.nn.silu(xbc)
    xs, bm, cm = jnp.split(xbc, [SSD_WIDTH, SSD_WIDTH + SSD_GROUPS * SSD_STATE], axis=-1)
    nc = s // SSD_CHUNK
    r = SSD_HEADS // SSD_GROUPS
    dt = jax.nn.softplus((dt_raw + dt_bias).astype(jnp.float32))
    a = -jnp.exp(a_log.astype(jnp.float32))
    x = xs.reshape(b, nc, SSD_CHUNK, SSD_GROUPS, r, SSD_HEAD_DIM)
    bm = bm.reshape(b, nc, SSD_CHUNK, SSD_GROUPS, SSD_STATE)
    cm = cm.reshape(b, nc, SSD_CHUNK, SSD_GROUPS, SSD_STATE)
    dt = dt.reshape(b, nc, SSD_CHUNK, SSD_GROUPS, r)
    xdt = x * dt[..., None].astype(x.dtype)
    a_dt = jnp.moveaxis(dt * a.reshape(SSD_GROUPS, r), 2, -1)
    cs = jnp.cumsum(a_dt, axis=-1)
    causal = jnp.tril(jnp.ones((SSD_CHUNK, SSD_CHUNK), dtype=bool))
    seg = jnp.exp(jnp.where(causal, cs[..., :, None] - cs[..., None, :], -jnp.inf))
    cb = jnp.einsum("bclgn,bcsgn->bcgls", cm, bm)
    y_diag = jnp.einsum("bcgls,bcgrls,bcsgrp->bclgrp", cb, seg, xdt)
    decay = jnp.exp(cs[..., -1:] - cs)
    states = jnp.einsum("bclgn,bcgrl,bclgrp->bcgrpn", bm, decay, xdt).astype(jnp.float32)
    chunk_decay = jnp.exp(cs[..., -1])

    def step(h, inp):
        st, dec = inp
        return dec[..., None, None] * h + st, h

    h0 = jnp.zeros((b, SSD_GROUPS, r, SSD_HEAD_DIM, SSD_STATE), jnp.float32)
    _, prev = lax.scan(step, h0, (jnp.moveaxis(states, 1, 0), jnp.moveaxis(chunk_decay, 1, 0)))
    prev = jnp.moveaxis(prev, 0, 1)
    y_off = jnp.einsum("bclgn,bcgrpn,bcgrl->bclgrp", cm, prev, jnp.exp(cs))
    y = y_diag + y_off + x * d_skip.reshape(SSD_GROUPS, r)[:, :, None]
    y = y.reshape(b, s, SSD_WIDTH).astype(xs.dtype)
    return rmsnorm(y * jax.nn.silu(z), norm_g)


def _fwd_setup_inputs(seed: int = 0) -> dict:
    key = jax.random.key(seed)
    ks = jax.random.split(key, 24)
    f32 = jnp.float32
    nrm = lambda k, shape, s: jax.random.normal(k, shape, f32) * s
    gain = lambda k, shape: 1.0 + 0.01 * jax.random.normal(k, shape, f32)
    dt0 = jnp.exp(jax.random.uniform(ks[10], (DEPTH, SSD_HEADS), f32,
                                     jnp.log(1e-3), jnp.log(1e-1)))
    return {
        "x": nrm(ks[0], (BATCH, SEQ, D_MODEL), 1.0),
        "c": nrm(ks[1], (BATCH, D_MODEL), 1.0),
        "w_ada": nrm(ks[2], (DEPTH, D_MODEL, N_MOD * D_MODEL), D_MODEL ** -0.5),
        "b_ada": nrm(ks[3], (DEPTH, N_MOD * D_MODEL), 0.01),
        "g_mix": gain(ks[4], (DEPTH, D_MODEL)),
        "w_in": nrm(ks[5], (DEPTH, D_MODEL, IN_COLS), D_MODEL ** -0.5),
        "rel_bias": nrm(ks[6], (DEPTH, ATT_HEADS, N_REL), 0.5),
        "conv_w": nrm(ks[7], (DEPTH, SSD_CONV, XBC_WIDTH), SSD_CONV ** -0.5),
        "conv_b": nrm(ks[8], (DEPTH, XBC_WIDTH), 0.01),
        "dt_bias": dt0 + jnp.log(-jnp.expm1(-dt0)),
        "a_log": jnp.log(jax.random.uniform(ks[11], (DEPTH, SSD_HEADS), f32, 1.0, 16.0)),
        "d_skip": gain(ks[12], (DEPTH, SSD_HEADS)),
        "g_att_out": gain(ks[13], (DEPTH, ATT_WIDTH)),
        "g_ssd_out": gain(ks[14], (DEPTH, SSD_WIDTH)),
        "w_out": nrm(ks[15], (DEPTH, MIX_WIDTH, D_MODEL), MIX_WIDTH ** -0.5),
        "g_ffn": gain(ks[16], (DEPTH, D_MODEL)),
        "w_gate": nrm(ks[17], (DEPTH, D_MODEL, FFN_HIDDEN), D_MODEL ** -0.5),
        "w_up": nrm(ks[18], (DEPTH, D_MODEL, FFN_HIDDEN), D_MODEL ** -0.5),
        "w_down": nrm(ks[19], (DEPTH, FFN_HIDDEN, D_MODEL), FFN_HIDDEN ** -0.5),
        "g_final": gain(ks[20], (D_MODEL,)),
    }


def _fwd_reference(x, c, w_ada, b_ada, g_mix, w_in, rel_bias, conv_w, conv_b, dt_bias, a_log,
              d_skip, g_att_out, g_ssd_out, w_out, g_ffn, w_gate, w_up, w_down, g_final):
    b, s, _ = x.shape
    cond = jax.nn.silu(c)
    splits = [ATT_WIDTH, 2 * ATT_WIDTH, 3 * ATT_WIDTH, 3 * ATT_WIDTH + SSD_WIDTH,
              3 * ATT_WIDTH + SSD_WIDTH + XBC_WIDTH]
    for l in range(DEPTH):
        mods = cond @ w_ada[l] + b_ada[l]
        sh1, sc1, gt1, sh2, sc2, gt2 = [m[:, None, :] for m in jnp.split(mods, N_MOD, axis=-1)]
        h = rmsnorm(x, g_mix[l]) * (1.0 + sc1) + sh1
        proj = h @ w_in[l]
        q, k, v, z, xbc, dt_raw = jnp.split(proj, splits, axis=-1)
        att = chunk_attention(q.reshape(b, s, ATT_HEADS, ATT_HEAD_DIM),
                              k.reshape(b, s, ATT_HEADS, ATT_HEAD_DIM),
                              v.reshape(b, s, ATT_HEADS, ATT_HEAD_DIM), rel_bias[l])
        att = rmsnorm(att, g_att_out[l])
        ssd = ssd_mixer(xbc, z, dt_raw, conv_w[l], conv_b[l], dt_bias[l], a_log[l],
                        d_skip[l], g_ssd_out[l])
        mix = jnp.concatenate([att, ssd], axis=-1) @ w_out[l]
        x = x + gt1 * mix
        h = rmsnorm(x, g_ffn[l]) * (1.0 + sc2) + sh2
        ffn = (jax.nn.silu(h @ w_gate[l]) * (h @ w_up[l])) @ w_down[l]
        x = x + gt2 * ffn
    return rmsnorm(x, g_final)


import jax as _jax
import jax.numpy as _jnp

TWIN_FORMAT = 'train_step'
FWD_PARAMS = ['x', 'c', 'w_ada', 'b_ada', 'g_mix', 'w_in', 'rel_bias', 'conv_w', 'conv_b', 'dt_bias', 'a_log', 'd_skip', 'g_att_out', 'g_ssd_out', 'w_out', 'g_ffn', 'w_gate', 'w_up', 'w_down', 'g_final']
TWIN_WEIGHTS = ['w_ada', 'b_ada', 'g_mix', 'w_in', 'rel_bias', 'conv_w', 'conv_b', 'dt_bias', 'a_log', 'd_skip', 'g_att_out', 'g_ssd_out', 'w_out', 'g_ffn', 'w_gate', 'w_up', 'w_down', 'g_final']
TWIN_DIFF_INPUT = 'x'
TWIN_INPUTS = ['x', 'c', 'w_ada', 'b_ada', 'g_mix', 'w_in', 'rel_bias', 'conv_w', 'conv_b', 'dt_bias', 'a_log', 'd_skip', 'g_att_out', 'g_ssd_out', 'w_out', 'g_ffn', 'w_gate', 'w_up', 'w_down', 'g_final', 'loss_target', 'm_w_ada', 'm_b_ada', 'm_g_mix', 'm_w_in', 'm_rel_bias', 'm_conv_w', 'm_conv_b', 'm_dt_bias', 'm_a_log', 'm_d_skip', 'm_g_att_out', 'm_g_ssd_out', 'm_w_out', 'm_g_ffn', 'm_w_gate', 'm_w_up', 'm_w_down', 'm_g_final', 'v_w_ada', 'v_b_ada', 'v_g_mix', 'v_w_in', 'v_rel_bias', 'v_conv_w', 'v_conv_b', 'v_dt_bias', 'v_a_log', 'v_d_skip', 'v_g_att_out', 'v_g_ssd_out', 'v_w_out', 'v_g_ffn', 'v_w_gate', 'v_w_up', 'v_w_down', 'v_g_final']
TWIN_OUTPUTS = ['loss', 'grad_x', 'grad_w_ada', 'grad_b_ada', 'grad_g_mix', 'grad_w_in', 'grad_rel_bias', 'grad_conv_w', 'grad_conv_b', 'grad_dt_bias', 'grad_a_log', 'grad_d_skip', 'grad_g_att_out', 'grad_g_ssd_out', 'grad_w_out', 'grad_g_ffn', 'grad_w_gate', 'grad_w_up', 'grad_w_down', 'grad_g_final', 'delta_w_ada', 'delta_b_ada', 'delta_g_mix', 'delta_w_in', 'delta_rel_bias', 'delta_conv_w', 'delta_conv_b', 'delta_dt_bias', 'delta_a_log', 'delta_d_skip', 'delta_g_att_out', 'delta_g_ssd_out', 'delta_w_out', 'delta_g_ffn', 'delta_w_gate', 'delta_w_up', 'delta_w_down', 'delta_g_final', 'new_m_w_ada', 'new_m_b_ada', 'new_m_g_mix', 'new_m_w_in', 'new_m_rel_bias', 'new_m_conv_w', 'new_m_conv_b', 'new_m_dt_bias', 'new_m_a_log', 'new_m_d_skip', 'new_m_g_att_out', 'new_m_g_ssd_out', 'new_m_w_out', 'new_m_g_ffn', 'new_m_w_gate', 'new_m_w_up', 'new_m_w_down', 'new_m_g_final', 'new_v_w_ada', 'new_v_b_ada', 'new_v_g_mix', 'new_v_w_in', 'new_v_rel_bias', 'new_v_conv_w', 'new_v_conv_b', 'new_v_dt_bias', 'new_v_a_log', 'new_v_d_skip', 'new_v_g_att_out', 'new_v_g_ssd_out', 'new_v_w_out', 'new_v_g_ffn', 'new_v_w_gate', 'new_v_w_up', 'new_v_w_down', 'new_v_g_final']
TWIN_LEAF_KINDS = {'loss': 'loss', 'grad_x': 'grad_x', 'grad_w_ada': 'grad_w', 'grad_b_ada': 'grad_w', 'grad_g_mix': 'grad_w', 'grad_w_in': 'grad_w', 'grad_rel_bias': 'grad_w', 'grad_conv_w': 'grad_w', 'grad_conv_b': 'grad_w', 'grad_dt_bias': 'grad_w', 'grad_a_log': 'grad_w', 'grad_d_skip': 'grad_w', 'grad_g_att_out': 'grad_w', 'grad_g_ssd_out': 'grad_w', 'grad_w_out': 'grad_w', 'grad_g_ffn': 'grad_w', 'grad_w_gate': 'grad_w', 'grad_w_up': 'grad_w', 'grad_w_down': 'grad_w', 'grad_g_final': 'grad_w', 'delta_w_ada': 'delta_w', 'delta_b_ada': 'delta_w', 'delta_g_mix': 'delta_w', 'delta_w_in': 'delta_w', 'delta_rel_bias': 'delta_w', 'delta_conv_w': 'delta_w', 'delta_conv_b': 'delta_w', 'delta_dt_bias': 'delta_w', 'delta_a_log': 'delta_w', 'delta_d_skip': 'delta_w', 'delta_g_att_out': 'delta_w', 'delta_g_ssd_out': 'delta_w', 'delta_w_out': 'delta_w', 'delta_g_ffn': 'delta_w', 'delta_w_gate': 'delta_w', 'delta_w_up': 'delta_w', 'delta_w_down': 'delta_w', 'delta_g_final': 'delta_w', 'new_m_w_ada': 'new_m', 'new_m_b_ada': 'new_m', 'new_m_g_mix': 'new_m', 'new_m_w_in': 'new_m', 'new_m_rel_bias': 'new_m', 'new_m_conv_w': 'new_m', 'new_m_conv_b': 'new_m', 'new_m_dt_bias': 'new_m', 'new_m_a_log': 'new_m', 'new_m_d_skip': 'new_m', 'new_m_g_att_out': 'new_m', 'new_m_g_ssd_out': 'new_m', 'new_m_w_out': 'new_m', 'new_m_g_ffn': 'new_m', 'new_m_w_gate': 'new_m', 'new_m_w_up': 'new_m', 'new_m_w_down': 'new_m', 'new_m_g_final': 'new_m', 'new_v_w_ada': 'new_v', 'new_v_b_ada': 'new_v', 'new_v_g_mix': 'new_v', 'new_v_w_in': 'new_v', 'new_v_rel_bias': 'new_v', 'new_v_conv_w': 'new_v', 'new_v_conv_b': 'new_v', 'new_v_dt_bias': 'new_v', 'new_v_a_log': 'new_v', 'new_v_d_skip': 'new_v', 'new_v_g_att_out': 'new_v', 'new_v_g_ssd_out': 'new_v', 'new_v_w_out': 'new_v', 'new_v_g_ffn': 'new_v', 'new_v_w_gate': 'new_v', 'new_v_w_up': 'new_v', 'new_v_w_down': 'new_v', 'new_v_g_final': 'new_v'}


def _forward(args):
    return _fwd_reference(*[args[k] for k in FWD_PARAMS])


def _output_shape():
    def fwd():
        inp = _fwd_setup_inputs(0)
        return _fwd_reference(*[inp[k] for k in FWD_PARAMS])
    out = _jax.eval_shape(fwd)
    return out.shape, out.dtype

N_MICROBATCH = 1
ADAM_LR = 0.001
ADAM_B1 = 0.9
ADAM_B2 = 0.999
ADAM_EPS = 1e-08
ADAM_WD = 0.01
ADAM_STEP = 10
PER_EXAMPLE_BATCH_AXIS = {'x': 0, 'c': 0, 'loss_target': 0}
SHARED_INPUTS = []
_WEIGHT_DTYPES = {'w_ada': _jnp.float32, 'b_ada': _jnp.float32, 'g_mix': _jnp.float32, 'w_in': _jnp.float32, 'rel_bias': _jnp.float32, 'conv_w': _jnp.float32, 'conv_b': _jnp.float32, 'dt_bias': _jnp.float32, 'a_log': _jnp.float32, 'd_skip': _jnp.float32, 'g_att_out': _jnp.float32, 'g_ssd_out': _jnp.float32, 'w_out': _jnp.float32, 'g_ffn': _jnp.float32, 'w_gate': _jnp.float32, 'w_up': _jnp.float32, 'w_down': _jnp.float32, 'g_final': _jnp.float32}
MOMENT_SCALE = {'w_ada': 5.651424e-02, 'b_ada': 1.007301e-01, 'g_mix': 7.636828e-02, 'w_in': 5.976872e-02, 'rel_bias': 1.098524e-02, 'conv_w': 6.199534e-02, 'conv_b': 5.910164e-02, 'dt_bias': 6.810803e-01, 'a_log': 2.954316e-01, 'd_skip': 2.495542e-01, 'g_att_out': 7.683306e-02, 'g_ssd_out': 6.716551e-02, 'w_out': 7.324455e-02, 'g_ffn': 8.389051e-02, 'w_gate': 3.816462e-02, 'w_up': 3.726673e-02, 'w_down': 6.184079e-02, 'g_final': 3.263486e+01}


def _to_microbatches(a, axis):
    t = _jnp.moveaxis(a, axis, 0)
    t = t.reshape((N_MICROBATCH, t.shape[0] // N_MICROBATCH) + t.shape[1:])
    return _jnp.moveaxis(t, 1, axis + 1)


def setup_inputs(seed: int = 0) -> dict:
    inp = _fwd_setup_inputs(seed)
    key = _jax.random.fold_in(_jax.random.key(seed), 7919)
    shape, _ = _output_shape()
    out = dict(inp)
    out["loss_target"] = _jax.random.normal(_jax.random.fold_in(key, 0), shape, _jnp.float32)
    for i, name in enumerate(TWIN_WEIGHTS):
        w = inp[name].astype(_jnp.float32)
        if MOMENT_SCALE is None:
            s = _jnp.sqrt(_jnp.mean(_jnp.square(w)) + 1e-30)
        else:
            s = MOMENT_SCALE[name]
        km, kv = _jax.random.split(_jax.random.fold_in(key, i + 1))
        out[name] = w
        out["m_" + name] = s * _jax.random.normal(km, w.shape, _jnp.float32)
        out["v_" + name] = (s * s) * _jax.random.uniform(kv, w.shape, _jnp.float32, 0.5, 1.5)
    if N_MICROBATCH > 1:
        for name, axis in PER_EXAMPLE_BATCH_AXIS.items():
            out[name] = _to_microbatches(out[name], axis)
    return {'x': out['x'], 'c': out['c'], 'w_ada': out['w_ada'], 'b_ada': out['b_ada'], 'g_mix': out['g_mix'], 'w_in': out['w_in'], 'rel_bias': out['rel_bias'], 'conv_w': out['conv_w'], 'conv_b': out['conv_b'], 'dt_bias': out['dt_bias'], 'a_log': out['a_log'], 'd_skip': out['d_skip'], 'g_att_out': out['g_att_out'], 'g_ssd_out': out['g_ssd_out'], 'w_out': out['w_out'], 'g_ffn': out['g_ffn'], 'w_gate': out['w_gate'], 'w_up': out['w_up'], 'w_down': out['w_down'], 'g_final': out['g_final'], 'loss_target': out['loss_target'], 'm_w_ada': out['m_w_ada'], 'm_b_ada': out['m_b_ada'], 'm_g_mix': out['m_g_mix'], 'm_w_in': out['m_w_in'], 'm_rel_bias': out['m_rel_bias'], 'm_conv_w': out['m_conv_w'], 'm_conv_b': out['m_conv_b'], 'm_dt_bias': out['m_dt_bias'], 'm_a_log': out['m_a_log'], 'm_d_skip': out['m_d_skip'], 'm_g_att_out': out['m_g_att_out'], 'm_g_ssd_out': out['m_g_ssd_out'], 'm_w_out': out['m_w_out'], 'm_g_ffn': out['m_g_ffn'], 'm_w_gate': out['m_w_gate'], 'm_w_up': out['m_w_up'], 'm_w_down': out['m_w_down'], 'm_g_final': out['m_g_final'], 'v_w_ada': out['v_w_ada'], 'v_b_ada': out['v_b_ada'], 'v_g_mix': out['v_g_mix'], 'v_w_in': out['v_w_in'], 'v_rel_bias': out['v_rel_bias'], 'v_conv_w': out['v_conv_w'], 'v_conv_b': out['v_conv_b'], 'v_dt_bias': out['v_dt_bias'], 'v_a_log': out['v_a_log'], 'v_d_skip': out['v_d_skip'], 'v_g_att_out': out['v_g_att_out'], 'v_g_ssd_out': out['v_g_ssd_out'], 'v_w_out': out['v_w_out'], 'v_g_ffn': out['v_g_ffn'], 'v_w_gate': out['v_w_gate'], 'v_w_up': out['v_w_up'], 'v_w_down': out['v_w_down'], 'v_g_final': out['v_g_final']}


def _loss(weights, diff, rest, loss_target):
    with _jax.named_scope("forward"):
        args = {**rest, TWIN_DIFF_INPUT: diff, **{k: w.astype(_WEIGHT_DTYPES[k]) for k, w in weights.items()}}
        y = _forward(args)
    with _jax.named_scope("loss_head"):
        err = _jnp.square(y.astype(_jnp.float32) - loss_target)
        return 0.5 * _jnp.sum(_jnp.mean(err, axis=-1)) if err.ndim else 0.5 * err


def _adamw(w, g, m, v):
    m = ADAM_B1 * m + (1.0 - ADAM_B1) * g
    v = ADAM_B2 * v + (1.0 - ADAM_B2) * _jnp.square(g)
    m_hat = m / (1.0 - ADAM_B1 ** ADAM_STEP)
    v_hat = v / (1.0 - ADAM_B2 ** ADAM_STEP)
    delta = -ADAM_LR * (m_hat / (_jnp.sqrt(v_hat) + ADAM_EPS) + ADAM_WD * w)
    return delta, m, v


def reference(x, c, w_ada, b_ada, g_mix, w_in, rel_bias, conv_w, conv_b, dt_bias, a_log, d_skip, g_att_out, g_ssd_out, w_out, g_ffn, w_gate, w_up, w_down, g_final, loss_target, m_w_ada, m_b_ada, m_g_mix, m_w_in, m_rel_bias, m_conv_w, m_conv_b, m_dt_bias, m_a_log, m_d_skip, m_g_att_out, m_g_ssd_out, m_w_out, m_g_ffn, m_w_gate, m_w_up, m_w_down, m_g_final, v_w_ada, v_b_ada, v_g_mix, v_w_in, v_rel_bias, v_conv_w, v_conv_b, v_dt_bias, v_a_log, v_d_skip, v_g_att_out, v_g_ssd_out, v_w_out, v_g_ffn, v_w_gate, v_w_up, v_w_down, v_g_final):
    given = dict(x=x, c=c, w_ada=w_ada, b_ada=b_ada, g_mix=g_mix, w_in=w_in, rel_bias=rel_bias, conv_w=conv_w, conv_b=conv_b, dt_bias=dt_bias, a_log=a_log, d_skip=d_skip, g_att_out=g_att_out, g_ssd_out=g_ssd_out, w_out=w_out, g_ffn=g_ffn, w_gate=w_gate, w_up=w_up, w_down=w_down, g_final=g_final, loss_target=loss_target, m_w_ada=m_w_ada, m_b_ada=m_b_ada, m_g_mix=m_g_mix, m_w_in=m_w_in, m_rel_bias=m_rel_bias, m_conv_w=m_conv_w, m_conv_b=m_conv_b, m_dt_bias=m_dt_bias, m_a_log=m_a_log, m_d_skip=m_d_skip, m_g_att_out=m_g_att_out, m_g_ssd_out=m_g_ssd_out, m_w_out=m_w_out, m_g_ffn=m_g_ffn, m_w_gate=m_w_gate, m_w_up=m_w_up, m_w_down=m_w_down, m_g_final=m_g_final, v_w_ada=v_w_ada, v_b_ada=v_b_ada, v_g_mix=v_g_mix, v_w_in=v_w_in, v_rel_bias=v_rel_bias, v_conv_w=v_conv_w, v_conv_b=v_conv_b, v_dt_bias=v_dt_bias, v_a_log=v_a_log, v_d_skip=v_d_skip, v_g_att_out=v_g_att_out, v_g_ssd_out=v_g_ssd_out, v_w_out=v_w_out, v_g_ffn=v_g_ffn, v_w_gate=v_w_gate, v_w_up=v_w_up, v_w_down=v_w_down, v_g_final=v_g_final)
    weights = {n: given[n] for n in TWIN_WEIGHTS}
    shared = {n: given[n] for n in SHARED_INPUTS}
    per_example = {n: given[n] for n in ['x', 'c']}
    grad_fn = _jax.value_and_grad(_loss, argnums=(0, 1))

    def one_microbatch(ex, loss_target):
        ex = dict(ex)
        diff = ex.pop(TWIN_DIFF_INPUT)
        return grad_fn(weights, diff, {**shared, **ex}, loss_target)

    if N_MICROBATCH == 1:
        loss, (grad_w, grad_x) = one_microbatch(per_example, given["loss_target"])
    else:
        def body(carry, xs):
            loss_sum, grad_sum = carry
            l_k, (gw_k, gx_k) = one_microbatch(xs[0], xs[1])
            with _jax.named_scope("update"):
                return (loss_sum + l_k, _jax.tree.map(_jnp.add, grad_sum, gw_k)), gx_k

        init = (_jnp.zeros((), _jnp.float32), _jax.tree.map(_jnp.zeros_like, weights))
        (loss, grad_w), grad_x = _jax.lax.scan(body, init, (per_example, given["loss_target"]))
    with _jax.named_scope("update"):
        delta_w, new_m, new_v = {}, {}, {}
        for n in TWIN_WEIGHTS:
            delta_w[n], new_m[n], new_v[n] = _adamw(weights[n], grad_w[n], given["m_" + n], given["v_" + n])
    return (loss, grad_x, *[grad_w[n] for n in TWIN_WEIGHTS], *[delta_w[n] for n in TWIN_WEIGHTS],
            *[new_m[n] for n in TWIN_WEIGHTS], *[new_v[n] for n in TWIN_WEIGHTS])
```

```python
import functools

import numpy as np
import jax
import jax.numpy as jnp
from jax import lax
from jax.experimental import pallas as pl
from jax.experimental.pallas import tpu as pltpu

f32 = jnp.float32
bf16 = jnp.bfloat16
HIGHEST = lax.Precision.HIGHEST
MESH = pl.DeviceIdType.MESH

D = 2048
CHUNK = 64
LEFT = 8
BAND = (LEFT + 1) * CHUNK
BANDP = 640
PADK = LEFT * CHUNK
NH = 16
HD = 64
ATT_W = NH * HD
SSD_W = 1024
NG = 2
NSTATE = 128
GW = SSD_W // NG
XBC = SSD_W + 2 * NG * NSTATE
N_REL = 320
REL_CLIP = 256
FFN = 5632
NSH = 4
FSH = FFN // NSH
IN_COLS = 5648
IN_SH = IN_COLS // NSH
IN_SHP = 1536
IN_A = 3 * ATT_W
IN_B = 2688
IN_P = IN_A + IN_B
EPS = 1e-6
N_DEV = 8

ADAM_LR = 0.001
ADAM_B1 = 0.9
ADAM_B2 = 0.999
ADAM_EPS = 1e-08
ADAM_WD = 0.01
ADAM_STEP = 10

VMEM_LIMIT = 56 * 1024 * 1024


def _params(sem):
    return pltpu.CompilerParams(dimension_semantics=sem, vmem_limit_bytes=VMEM_LIMIT)


def _sds(shape, dtype):
    return jax.ShapeDtypeStruct(shape, dtype)


def _fold8(v):
    r, w = v.shape
    return jnp.sum(v.reshape(r // 8, 8, w), axis=0)


def _sigmoid(v):
    return 1.0 / (1.0 + jnp.exp(-v))


def _softplus(v):
    return jnp.maximum(v, 0.0) + jnp.log(1.0 + jnp.exp(-jnp.abs(v)))


def _dot(a, b, ta=False, tb=False):
    dn = (((0 if ta else 1,), (1 if tb else 0,)), ((), ()))
    return lax.dot_general(a.astype(bf16), b.astype(bf16), dn, preferred_element_type=f32)


def _matmul(name, a, b, *, grid, a_spec, b_spec, o_spec, o_shape, o_dtype, acc_shape, ta=False, tb=False):
    nk = grid[2]

    def body(a_ref, b_ref, o_ref, acc_ref):
        p = _dot(a_ref[...], b_ref[...], ta, tb)
        if nk == 1:
            o_ref[...] = p.astype(o_ref.dtype)
        else:
            k = pl.program_id(2)

            @pl.when(k == 0)
            def _():
                acc_ref[...] = p

            @pl.when(k > 0)
            def _():
                acc_ref[...] += p

            @pl.when(k == nk - 1)
            def _():
                o_ref[...] = acc_ref[...].astype(o_ref.dtype)

    return pl.pallas_call(
        body, name=name, grid=grid, in_specs=[a_spec, b_spec], out_specs=o_spec,
        out_shape=_sds(o_shape, o_dtype), scratch_shapes=[pltpu.VMEM(acc_shape if nk > 1 else (8, 128), f32)],
        compiler_params=_params(("parallel", "parallel", "arbitrary")),
    )(a, b)


def _mm_nn_fullk(name, a, b, tm, tn, o_dtype):
    m, k = a.shape
    n = b.shape[1]
    return _matmul(name, a, b, grid=(m // tm, n // tn, 1),
                   a_spec=pl.BlockSpec((tm, k), lambda i, j, kk: (i, 0)),
                   b_spec=pl.BlockSpec((k, tn), lambda i, j, kk: (0, j)),
                   o_spec=pl.BlockSpec((tm, tn), lambda i, j, kk: (i, j)),
                   o_shape=(m, n), o_dtype=o_dtype, acc_shape=(tm, tn))


def _mm_nt(name, a, b, tm, tn, tk, o_dtype):
    m, k = a.shape
    n = b.shape[0]
    return _matmul(name, a, b, grid=(m // tm, n // tn, k // tk), tb=True,
                   a_spec=pl.BlockSpec((tm, tk), lambda i, j, kk: (i, kk)),
                   b_spec=pl.BlockSpec((tn, tk), lambda i, j, kk: (j, kk)),
                   o_spec=pl.BlockSpec((tm, tn), lambda i, j, kk: (i, j)),
                   o_shape=(m, n), o_dtype=o_dtype, acc_shape=(tm, tn))


def _mm_tn(name, a, b, tm, tn, tk, o_dtype):
    k, m = a.shape
    n = b.shape[1]
    return _matmul(name, a, b, grid=(m // tm, n // tn, k // tk), ta=True,
                   a_spec=pl.BlockSpec((tk, tm), lambda i, j, kk: (kk, i)),
                   b_spec=pl.BlockSpec((tk, tn), lambda i, j, kk: (kk, j)),
                   o_spec=pl.BlockSpec((tm, tn), lambda i, j, kk: (i, j)),
                   o_shape=(m, n), o_dtype=o_dtype, acc_shape=(tm, tn))


def _ffn_up(h2b, wg4, wu4, tm):
    s = h2b.shape[0]

    def body(h_ref, wg_ref, wu_ref, g_ref, u_ref, a_ref):
        h = h_ref[...]
        g = _dot(h, wg_ref[...])
        u = _dot(h, wu_ref[...])
        g_ref[...] = g
        u_ref[...] = u
        a_ref[...] = (g * _sigmoid(g) * u).astype(bf16)

    wspec = pl.BlockSpec((None, D, FSH), lambda k, i: (k, 0, 0))
    ospec = pl.BlockSpec((tm, FSH), lambda k, i: (i, k))
    return pl.pallas_call(
        body, name="ffn_up", grid=(NSH, s // tm),
        in_specs=[pl.BlockSpec((tm, D), lambda k, i: (i, 0)), wspec, wspec],
        out_specs=[ospec, ospec, ospec],
        out_shape=[_sds((s, FFN), f32), _sds((s, FFN), f32), _sds((s, FFN), bf16)],
        compiler_params=_params(("parallel", "parallel")),
    )(h2b, wg4, wu4)


def _ffn_down(act, wd4, tm):
    s = act.shape[0]
    return _matmul("ffn_down", act, wd4, grid=(s // tm, 1, NSH),
                   a_spec=pl.BlockSpec((tm, FSH), lambda i, j, k: (i, k)),
                   b_spec=pl.BlockSpec((None, FSH, D), lambda i, j, k: (k, 0, 0)),
                   o_spec=pl.BlockSpec((tm, D), lambda i, j, k: (i, 0)),
                   o_shape=(s, D), o_dtype=f32, acc_shape=(tm, D))


def _ffn_dact(dffn, wd4, gate, up, tm):
    s = dffn.shape[0]

    def body(d_ref, w_ref, g_ref, u_ref, dg_ref, du_ref):
        dact = _dot(d_ref[...], w_ref[...], tb=True)
        g = g_ref[...]
        sg = _sigmoid(g)
        dg_ref[...] = (dact * u_ref[...] * (sg * (1.0 + g * (1.0 - sg)))).astype(bf16)
        du_ref[...] = (dact * (g * sg)).astype(bf16)

    blk = pl.BlockSpec((tm, FSH), lambda k, i: (i, k))
    return pl.pallas_call(
        body, name="ffn_dact", grid=(NSH, s // tm),
        in_specs=[pl.BlockSpec((tm, D), lambda k, i: (i, 0)), pl.BlockSpec((None, FSH, D), lambda k, i: (k, 0, 0)), blk, blk],
        out_specs=[blk, blk], out_shape=[_sds((s, FFN), bf16), _sds((s, FFN), bf16)],
        compiler_params=_params(("parallel", "parallel")),
    )(dffn, wd4, gate, up)


def _ffn_dh(dgate, dup, wg4, wu4, tm):
    s = dgate.shape[0]

    def body(dg_ref, du_ref, wg_ref, wu_ref, o_ref, acc_ref):
        k = pl.program_id(1)
        p = _dot(dg_ref[...], wg_ref[...], tb=True) + _dot(du_ref[...], wu_ref[...], tb=True)

        @pl.when(k == 0)
        def _():
            acc_ref[...] = p

        @pl.when(k > 0)
        def _():
            acc_ref[...] += p

        @pl.when(k == NSH - 1)
        def _():
            o_ref[...] = acc_ref[...]

    aspec = pl.BlockSpec((tm, FSH), lambda i, k: (i, k))
    wspec = pl.BlockSpec((None, D, FSH), lambda i, k: (k, 0, 0))
    return pl.pallas_call(
        body, name="ffn_dh", grid=(s // tm, NSH), in_specs=[aspec, aspec, wspec, wspec],
        out_specs=pl.BlockSpec((tm, D), lambda i, k: (i, 0)), out_shape=_sds((s, D), f32),
        scratch_shapes=[pltpu.VMEM((tm, D), f32)], compiler_params=_params(("parallel", "arbitrary")),
    )(dgate, dup, wg4, wu4)


def _grad_cols4(name, h, dy, tm, tk):
    s = h.shape[0]
    return _matmul(name, h, dy, grid=(NSH, D // tm, s // tk), ta=True,
                   a_spec=pl.BlockSpec((tk, tm), lambda k, i, kk: (kk, i)),
                   b_spec=pl.BlockSpec((tk, FSH), lambda k, i, kk: (kk, k)),
                   o_spec=pl.BlockSpec((None, tm, FSH), lambda k, i, kk: (k, i, 0)),
                   o_shape=(NSH, D, FSH), o_dtype=bf16, acc_shape=(tm, FSH))


def _grad_wdown4(act, dffn, tn, tk):
    s = act.shape[0]
    return _matmul("grad_w_down", act, dffn, grid=(NSH, D // tn, s // tk), ta=True,
                   a_spec=pl.BlockSpec((tk, FSH), lambda k, j, kk: (kk, k)),
                   b_spec=pl.BlockSpec((tk, tn), lambda k, j, kk: (kk, j)),
                   o_spec=pl.BlockSpec((None, FSH, tn), lambda k, j, kk: (k, 0, j)),
                   o_shape=(NSH, FSH, D), o_dtype=bf16, acc_shape=(FSH, tn))


def _row_spec(w):
    return pl.BlockSpec((1, w), lambda i: (0, 0))


def _tile_spec(tm, w, col=0):
    return pl.BlockSpec((tm, w), lambda i: (i, col))


def _norm_mod(name, x, g, sc, sh, tm):
    s = x.shape[0]

    def body(x_ref, g_ref, sc_ref, sh_ref, o_ref):
        xv = x_ref[...]
        r = lax.rsqrt(jnp.mean(xv * xv, axis=-1, keepdims=True) + EPS)
        o_ref[...] = (xv * r * g_ref[...] * (1.0 + sc_ref[...]) + sh_ref[...]).astype(bf16)

    return pl.pallas_call(
        body, name=name, grid=(s // tm,), in_specs=[_tile_spec(tm, D), _row_spec(D), _row_spec(D), _row_spec(D)],
        out_specs=_tile_spec(tm, D), out_shape=_sds((s, D), bf16), compiler_params=_params(("parallel",)),
    )(x, g, sc, sh)


def _resid_norm_mod(x, gt, mix, g, sc, sh, tm):
    s = x.shape[0]

    def body(x_ref, gt_ref, m_ref, g_ref, sc_ref, sh_ref, x2_ref, h_ref):
        xv = x_ref[...] + gt_ref[...] * m_ref[...]
        x2_ref[...] = xv
        r = lax.rsqrt(jnp.mean(xv * xv, axis=-1, keepdims=True) + EPS)
        h_ref[...] = (xv * r * g_ref[...] * (1.0 + sc_ref[...]) + sh_ref[...]).astype(bf16)

    return pl.pallas_call(
        body, name="resid_norm_mod", grid=(s // tm,),
        in_specs=[_tile_spec(tm, D), _row_spec(D), _tile_spec(tm, D), _row_spec(D), _row_spec(D), _row_spec(D)],
        out_specs=[_tile_spec(tm, D), _tile_spec(tm, D)], out_shape=[_sds((s, D), f32), _sds((s, D), bf16)],
        compiler_params=_params(("parallel",)),
    )(x, gt, mix, g, sc, sh)


def _final_fwd_bwd(x2, ffn, gt2, g, tgt, tm):
    s = x2.shape[0]
    n = s // tm

    def body(x_ref, f_ref, gt_ref, g_ref, t_ref, dx_ref, df_ref, loss_ref, dg_ref, dgt_ref, a_loss, a_dg, a_dgt):
        i = pl.program_id(0)

        @pl.when(i == 0)
        def _():
            a_loss[...] = jnp.zeros_like(a_loss)
            a_dg[...] = jnp.zeros_like(a_dg)
            a_dgt[...] = jnp.zeros_like(a_dgt)

        fv = f_ref[...]
        gt = gt_ref[...]
        gv = g_ref[...]
        xv = x_ref[...] + gt * fv
        r = lax.rsqrt(jnp.mean(xv * xv, axis=-1, keepdims=True) + EPS)
        xh = xv * r
        e = xh * gv - t_ref[...]
        a_loss[...] += _fold8(e * e)
        dy = e * (1.0 / D)
        a_dg[...] += _fold8(dy * xh)
        t = dy * gv
        dx = r * (t - xh * jnp.mean(t * xh, axis=-1, keepdims=True))
        dx_ref[...] = dx
        a_dgt[...] += _fold8(dx * fv)
        df_ref[...] = (dx * gt).astype(bf16)

        @pl.when(i == n - 1)
        def _():
            tot = jnp.sum(jnp.sum(a_loss[...], axis=0, keepdims=True), axis=1, keepdims=True) * (0.5 / D)
            loss_ref[...] = jnp.broadcast_to(tot, (1, 128))
            dg_ref[...] = jnp.sum(a_dg[...], axis=0, keepdims=True)
            dgt_ref[...] = jnp.sum(a_dgt[...], axis=0, keepdims=True)

    return pl.pallas_call(
        body, name="final_fwd_bwd", grid=(n,),
        in_specs=[_tile_spec(tm, D), _tile_spec(tm, D), _row_spec(D), _row_spec(D), _tile_spec(tm, D)],
        out_specs=[_tile_spec(tm, D), _tile_spec(tm, D), _row_spec(128), _row_spec(D), _row_spec(D)],
        out_shape=[_sds((s, D), f32), _sds((s, D), bf16), _sds((1, 128), f32), _sds((1, D), f32), _sds((1, D), f32)],
        scratch_shapes=[pltpu.VMEM((8, D), f32)] * 3, compiler_params=_params(("arbitrary",)),
    )(x2, ffn, gt2, g, tgt)


def _norm_mod_bwd(name, dh, xin, g, sc, dres, tm, mix=None, gt=None):
    s = dh.shape[0]
    n = s // tm
    with_mix = mix is not None

    def body(*refs):
        if with_mix:
            dh_ref, x_ref, g_ref, sc_ref, dr_ref, m_ref, gt_ref, dx_ref, dm_ref, dsc_ref, dsh_ref, dg_ref, dgt_ref, a_sc, a_sh, a_g, a_gt = refs
        else:
            dh_ref, x_ref, g_ref, sc_ref, dr_ref, dx_ref, dsc_ref, dsh_ref, dg_ref, a_sc, a_sh, a_g = refs
        i = pl.program_id(0)

        @pl.when(i == 0)
        def _():
            a_sc[...] = jnp.zeros_like(a_sc)
            a_sh[...] = jnp.zeros_like(a_sh)
            a_g[...] = jnp.zeros_like(a_g)
            if with_mix:
                a_gt[...] = jnp.zeros_like(a_gt)

        dh = dh_ref[...]
        xv = x_ref[...]
        gv = g_ref[...]
        r = lax.rsqrt(jnp.mean(xv * xv, axis=-1, keepdims=True) + EPS)
        xh = xv * r
        a_sc[...] += _fold8(dh * xh * gv)
        a_sh[...] += _fold8(dh)
        dn = dh * (1.0 + sc_ref[...])
        a_g[...] += _fold8(dn * xh)
        t = dn * gv
        dx = dr_ref[...] + r * (t - xh * jnp.mean(t * xh, axis=-1, keepdims=True))
        dx_ref[...] = dx
        if with_mix:
            a_gt[...] += _fold8(dx * m_ref[...])
            dm_ref[...] = (dx * gt_ref[...]).astype(bf16)

        @pl.when(i == n - 1)
        def _():
            dsc_ref[...] = jnp.sum(a_sc[...], axis=0, keepdims=True)
            dsh_ref[...] = jnp.sum(a_sh[...], axis=0, keepdims=True)
            dg_ref[...] = jnp.sum(a_g[...], axis=0, keepdims=True)
            if with_mix:
                dgt_ref[...] = jnp.sum(a_gt[...], axis=0, keepdims=True)

    tile, row = _tile_spec(tm, D), _row_spec(D)
    if with_mix:
        ins, args = [tile, tile, row, row, tile, tile, row], (dh, xin, g, sc, dres, mix, gt)
        outs = [tile, tile, row, row, row, row]
        shapes = [_sds((s, D), f32), _sds((s, D), bf16)] + [_sds((1, D), f32)] * 4
        nacc = 4
    else:
        ins, args = [tile, tile, row, row, tile], (dh, xin, g, sc, dres)
        outs = [tile, row, row, row]
        shapes = [_sds((s, D), f32)] + [_sds((1, D), f32)] * 3
        nacc = 3
    return pl.pallas_call(
        body, name=name, grid=(n,), in_specs=ins, out_specs=outs, out_shape=shapes,
        scratch_shapes=[pltpu.VMEM((8, D), f32)] * nacc, compiler_params=_params(("arbitrary",)),
    )(*args)


def _mix_pre(att, y, proj2, g_att, g_ssd, tm):
    s = att.shape[0]

    def body(a_ref, y_ref, z_ref, ga_ref, gs_ref, o_ref):
        a = a_ref[...]
        ra = lax.rsqrt(jnp.mean(a * a, axis=-1, keepdims=True) + EPS)
        o_ref[:, 0:ATT_W] = (a * ra * ga_ref[...]).astype(bf16)
        z = z_ref[...]
        u = y_ref[...] * (z * _sigmoid(z))
        ru = lax.rsqrt(jnp.mean(u * u, axis=-1, keepdims=True) + EPS)
        o_ref[:, ATT_W:] = (u * ru * gs_ref[...]).astype(bf16)

    t = _tile_spec(tm, ATT_W)
    return pl.pallas_call(
        body, name="mix_pre", grid=(s // tm,), in_specs=[t, t, t, _row_spec(ATT_W), _row_spec(SSD_W)],
        out_specs=_tile_spec(tm, D), out_shape=_sds((s, D), bf16), compiler_params=_params(("parallel",)),
    )(att, y, proj2, g_att, g_ssd)


def _mix_pre_bwd(dmc, att, y, proj2, g_att, g_ssd, tm):
    s = att.shape[0]
    n = s // tm

    def body(da_ref, ds_ref, a_ref, y_ref, z_ref, ga_ref, gs_ref, datt_ref, dy_ref, dz_ref, dga_ref, dgs_ref, acc_a, acc_s):
        i = pl.program_id(0)

        @pl.when(i == 0)
        def _():
            acc_a[...] = jnp.zeros_like(acc_a)
            acc_s[...] = jnp.zeros_like(acc_s)

        a = a_ref[...]
        ra = lax.rsqrt(jnp.mean(a * a, axis=-1, keepdims=True) + EPS)
        ah = a * ra
        dan = da_ref[...]
        acc_a[...] += _fold8(dan * ah)
        t = dan * ga_ref[...]
        datt_ref[...] = (ra * (t - ah * jnp.mean(t * ah, axis=-1, keepdims=True))).astype(bf16)
        z = z_ref[...]
        yv = y_ref[...]
        sz = _sigmoid(z)
        sil = z * sz
        u = yv * sil
        ru = lax.rsqrt(jnp.mean(u * u, axis=-1, keepdims=True) + EPS)
        uh = u * ru
        dsn = ds_ref[...]
        acc_s[...] += _fold8(dsn * uh)
        t2 = dsn * gs_ref[...]
        du = ru * (t2 - uh * jnp.mean(t2 * uh, axis=-1, keepdims=True))
        dy_ref[...] = du * sil
        dz_ref[...] = (du * yv * (sz * (1.0 + z * (1.0 - sz)))).astype(bf16)

        @pl.when(i == n - 1)
        def _():
            dga_ref[...] = jnp.sum(acc_a[...], axis=0, keepdims=True)
            dgs_ref[...] = jnp.sum(acc_s[...], axis=0, keepdims=True)

    t = _tile_spec(tm, ATT_W)
    row = _row_spec(ATT_W)
    return pl.pallas_call(
        body, name="mix_pre_bwd", grid=(n,),
        in_specs=[_tile_spec(tm, ATT_W, 0), _tile_spec(tm, ATT_W, 1), t, t, t, row, row],
        out_specs=[t, t, t, row, row],
        out_shape=[_sds((s, ATT_W), bf16), _sds((s, SSD_W), f32), _sds((s, SSD_W), bf16), _sds((1, ATT_W), f32), _sds((1, SSD_W), f32)],
        scratch_shapes=[pltpu.VMEM((8, ATT_W), f32)] * 2, compiler_params=_params(("arbitrary",)),
    )(dmc, dmc, att, y, proj2, g_att, g_ssd)


def _softmax_rows(q, kb, bias, valid):
    sc = lax.dot_general(q, kb, (((1,), (1,)), ((), ())), preferred_element_type=f32) * (HD ** -0.5) + bias
    sc = jnp.where(valid, sc, -jnp.inf)
    e = jnp.exp(sc - jnp.max(sc, axis=-1, keepdims=True))
    return e / jnp.sum(e, axis=-1, keepdims=True)


def _attn_masks(r0):
    lane = lax.broadcasted_iota(jnp.int32, (CHUNK, 128), 1)
    kidx = lax.broadcasted_iota(jnp.int32, (CHUNK, BANDP), 1)
    valid = jnp.logical_and(kidx < BAND, r0 + kidx >= PADK)
    return lane < HD, valid


def _attn_fwd(qkv, bias):
    s = qkv.shape[0]
    nc = s // CHUNK
    npair = NH // 2

    def body(q_ref, k_ref, v_ref, b_ref, o_ref, kp, vp):
        zeros = jnp.zeros((PADK, 128), bf16)
        kp[0:PADK, :] = zeros
        vp[0:PADK, :] = zeros
        kp[PADK:PADK + s, :] = k_ref[...]
        vp[PADK:PADK + s, :] = v_ref[...]
        kp[PADK + s:, :] = jnp.zeros((CHUNK, 128), bf16)
        vp[PADK + s:, :] = jnp.zeros((CHUNK, 128), bf16)

        def chunk(i, carry):
            r0 = pl.multiple_of(i * CHUNK, CHUNK)
            first, valid = _attn_masks(r0)
            qc = q_ref[pl.ds(r0, CHUNK), :]
            kb = kp[pl.ds(r0, BANDP), :]
            vb = vp[pl.ds(r0, BANDP), :]
            zq = jnp.zeros_like(qc)
            outs = []
            for h in range(2):
                qm = jnp.where(first if h == 0 else jnp.logical_not(first), qc, zq)
                p = _softmax_rows(qm, kb, b_ref[h], valid)
                outs.append(jnp.dot(p.astype(bf16), vb, preferred_element_type=f32))
            o_ref[pl.ds(r0, CHUNK), :] = jnp.where(first, outs[0], outs[1])
            return carry

        lax.fori_loop(0, nc, chunk, 0)

    return pl.pallas_call(
        body, name="attn_fwd", grid=(npair,),
        in_specs=[pl.BlockSpec((s, 128), lambda p: (0, p)), pl.BlockSpec((s, 128), lambda p: (0, npair + p)),
                  pl.BlockSpec((s, 128), lambda p: (0, 2 * npair + p)), pl.BlockSpec((2, CHUNK, BANDP), lambda p: (p, 0, 0))],
        out_specs=pl.BlockSpec((s, 128), lambda p: (0, p)), out_shape=_sds((s, ATT_W), f32),
        scratch_shapes=[pltpu.VMEM((PADK + s + CHUNK, 128), bf16)] * 2, compiler_params=_params(("parallel",)),
    )(qkv, qkv, qkv, bias)


def _attn_bwd(qkv, datt, bias):
    s = qkv.shape[0]
    nc = s // CHUNK
    npair = NH // 2
    rows = PADK + s + CHUNK

    def body(q_ref, k_ref, v_ref, do_ref, b_ref, dq_ref, dk_ref, dv_ref, g_ref, kp, vp, dkp, dvp):
        zeros = jnp.zeros((PADK, 128), bf16)
        kp[0:PADK, :] = zeros
        vp[0:PADK, :] = zeros
        kp[PADK:PADK + s, :] = k_ref[...]
        vp[PADK:PADK + s, :] = v_ref[...]
        kp[PADK + s:, :] = jnp.zeros((CHUNK, 128), bf16)
        vp[PADK + s:, :] = jnp.zeros((CHUNK, 128), bf16)
        dkp[...] = jnp.zeros_like(dkp)
        dvp[...] = jnp.zeros_like(dvp)
        g_ref[...] = jnp.zeros_like(g_ref)

        def chunk(i, carry):
            r0 = pl.multiple_of(i * CHUNK, CHUNK)
            first, valid = _attn_masks(r0)
            qc = q_ref[pl.ds(r0, CHUNK), :]
            doc = do_ref[pl.ds(r0, CHUNK), :]
            kb = kp[pl.ds(r0, BANDP), :]
            vb = vp[pl.ds(r0, BANDP), :]
            zq = jnp.zeros_like(qc)
            dqs = []
            dk = jnp.zeros((BANDP, 128), f32)
            dv = jnp.zeros((BANDP, 128), f32)
            for h in range(2):
                sel = first if h == 0 else jnp.logical_not(first)
                qm = jnp.where(sel, qc, zq)
                dom = jnp.where(sel, doc, zq)
                p = _softmax_rows(qm, kb, b_ref[h], valid)
                dp = lax.dot_general(dom, vb, (((1,), (1,)), ((), ())), preferred_element_type=f32)
                ds = p * (dp - jnp.sum(p * dp, axis=-1, keepdims=True))
                g_ref[h] += ds
                dsb = ds.astype(bf16)
                dqs.append(jnp.dot(dsb, kb, preferred_element_type=f32))
                dk = dk + lax.dot_general(dsb, qm, (((0,), (0,)), ((), ())), preferred_element_type=f32)
                dv = dv + lax.dot_general(p.astype(bf16), dom, (((0,), (0,)), ((), ())), preferred_element_type=f32)
            dq_ref[pl.ds(r0, CHUNK), :] = (jnp.where(first, dqs[0], dqs[1]) * (HD ** -0.5)).astype(bf16)
            dkp[pl.ds(r0, BANDP), :] += dk * (HD ** -0.5)
            dvp[pl.ds(r0, BANDP), :] += dv
            return carry

        lax.fori_loop(0, nc, chunk, 0)
        dk_ref[...] = dkp[PADK:PADK + s, :].astype(bf16)
        dv_ref[...] = dvp[PADK:PADK + s, :].astype(bf16)

    col = lambda off: pl.BlockSpec((s, 128), lambda p: (0, off + p))
    return pl.pallas_call(
        body, name="attn_bwd", grid=(npair,),
        in_specs=[col(0), col(npair), col(2 * npair), col(0), pl.BlockSpec((2, CHUNK, BANDP), lambda p: (p, 0, 0))],
        out_specs=[col(0), col(0), col(0), pl.BlockSpec((2, CHUNK, BANDP), lambda p: (p, 0, 0))],
        out_shape=[_sds((s, ATT_W), bf16)] * 3 + [_sds((NH, CHUNK, BANDP), f32)],
        scratch_shapes=[pltpu.VMEM((rows, 128), bf16)] * 2 + [pltpu.VMEM((rows, 128), f32)] * 2,
        compiler_params=_params(("parallel",)),
    )(qkv, qkv, qkv, datt, bias)


def _rel_tables():
    onehot = np.zeros((BANDP, N_REL), np.float32)
    for j in range(BAND + CHUNK - 1):
        o = j - (CHUNK - 1)
        onehot[j, int(np.clip(PADK - o, -(CHUNK - 1), REL_CLIP)) + CHUNK - 1] = 1.0
    return onehot, np.ascontiguousarray(np.eye(CHUNK, dtype=np.float32)[::-1])


def _expand_bias(rel):
    ext = jnp.concatenate([jnp.broadcast_to(rel[:, N_REL - 1:], (NH, N_REL - 1)), rel[:, ::-1],
                           jnp.zeros((NH, BANDP - BAND + 1), f32)], axis=1)
    return jnp.stack([ext[:, CHUNK - 1 - q:CHUNK - 1 - q + BANDP] for q in range(CHUNK)], axis=1)


def _rel_bias_grad(gband):
    def body(g_ref, m_ref, flip_ref, o_ref, d2):
        for h in range(NH):
            rev = jnp.dot(flip_ref[...], g_ref[h], precision=HIGHEST, preferred_element_type=f32)
            rolled = pltpu.roll(rev, 0, 1, stride=1, stride_axis=0)
            d2[h:h + 1, :] = jnp.sum(rolled, axis=0, keepdims=True)
        o_ref[...] = jnp.dot(d2[...], m_ref[...], precision=HIGHEST, preferred_element_type=f32)

    onehot, flip = _rel_tables()
    return pl.pallas_call(
        body, name="rel_bias_grad", out_shape=_sds((NH, N_REL), f32), scratch_shapes=[pltpu.VMEM((NH, BANDP), f32)],
    )(gband, jnp.asarray(onehot), jnp.asarray(flip))


XBC_BLK = 512
XBC_COL0 = SSD_W // XBC_BLK
DT_COL = (SSD_W + XBC) // 128


def _conv_taps(ext, w_ref, b_ref, tm):
    n = ext.shape[0]
    pre = w_ref[3:4, :] * ext + b_ref[...]
    for j in range(3):
        pre = pre + w_ref[j:j + 1, :] * pltpu.roll(ext, 3 - j, 0)
    return pre


def _ssd_conv(proj2, conv_w, conv_b, tm):
    s = proj2.shape[0]
    nb = XBC // XBC_BLK

    def body(x_ref, p_ref, w_ref, b_ref, o_ref):
        i = pl.program_id(1)
        prev = jnp.where(i > 0, p_ref[...], 0.0)
        ext = jnp.concatenate([prev, x_ref[...]], axis=0)
        pre = _conv_taps(ext, w_ref, b_ref, tm)[8:8 + tm]
        o_ref[...] = pre * _sigmoid(pre)

    return pl.pallas_call(
        body, name="ssd_conv", grid=(nb, s // tm),
        in_specs=[pl.BlockSpec((tm, XBC_BLK), lambda j, i: (i, XBC_COL0 + j)),
                  pl.BlockSpec((8, XBC_BLK), lambda j, i: (jnp.maximum(i * (tm // 8) - 1, 0), XBC_COL0 + j)),
                  pl.BlockSpec((4, XBC_BLK), lambda j, i: (0, j)), pl.BlockSpec((1, XBC_BLK), lambda j, i: (0, j))],
        out_specs=pl.BlockSpec((tm, XBC_BLK), lambda j, i: (i, j)), out_shape=_sds((s, XBC), f32),
        compiler_params=_params(("parallel", "parallel")),
    )(proj2, proj2, conv_w, conv_b)


def _ssd_conv_bwd(dxbc, proj2, conv_w, conv_b, tm):
    s = proj2.shape[0]
    nb = XBC // XBC_BLK
    n = s // tm
    last8 = s // 8 - 1

    def body(x_ref, xp_ref, xn_ref, d_ref, dn_ref, w_ref, b_ref, o_ref, dw_ref, db_ref):
        i = pl.program_id(1)

        @pl.when(i == 0)
        def _():
            dw_ref[...] = jnp.zeros_like(dw_ref)
            db_ref[...] = jnp.zeros_like(db_ref)

        prev = jnp.where(i > 0, xp_ref[...], 0.0)
        ext = jnp.concatenate([prev, x_ref[...], xn_ref[...]], axis=0)
        pre = _conv_taps(ext, w_ref, b_ref, tm)
        sg = _sigmoid(pre)
        dnext = jnp.where(i < n - 1, dn_ref[...], 0.0)
        dext = jnp.concatenate([jnp.zeros((8, XBC_BLK), f32), d_ref[...], dnext], axis=0)
        dpre = dext * (sg * (1.0 + pre * (1.0 - sg)))
        rows = tm + 16
        dx = w_ref[3:4, :] * dpre
        for j in range(3):
            dx = dx + w_ref[j:j + 1, :] * pltpu.roll(dpre, rows - (3 - j), 0)
        o_ref[...] = dx[8:8 + tm].astype(bf16)
        dcur = dpre[8:8 + tm]
        db_ref[...] += jnp.sum(dcur, axis=0, keepdims=True)
        dw_ref[3:4, :] += jnp.sum(dcur * ext[8:8 + tm], axis=0, keepdims=True)
        for j in range(3):
            dw_ref[j:j + 1, :] += jnp.sum(dcur * pltpu.roll(ext, 3 - j, 0)[8:8 + tm], axis=0, keepdims=True)

    xcol = lambda j: XBC_COL0 + j
    return pl.pallas_call(
        body, name="ssd_conv_bwd", grid=(nb, n),
        in_specs=[pl.BlockSpec((tm, XBC_BLK), lambda j, i: (i, xcol(j))),
                  pl.BlockSpec((8, XBC_BLK), lambda j, i: (jnp.maximum(i * (tm // 8) - 1, 0), xcol(j))),
                  pl.BlockSpec((8, XBC_BLK), lambda j, i: (jnp.minimum((i + 1) * (tm // 8), last8), xcol(j))),
                  pl.BlockSpec((tm, XBC_BLK), lambda j, i: (i, j)),
                  pl.BlockSpec((8, XBC_BLK), lambda j, i: (jnp.minimum((i + 1) * (tm // 8), last8), j)),
                  pl.BlockSpec((4, XBC_BLK), lambda j, i: (0, j)), pl.BlockSpec((1, XBC_BLK), lambda j, i: (0, j))],
        out_specs=[pl.BlockSpec((tm, XBC_BLK), lambda j, i: (i, j)), pl.BlockSpec((4, XBC_BLK), lambda j, i: (0, j)),
                   pl.BlockSpec((1, XBC_BLK), lambda j, i: (0, j))],
        out_shape=[_sds((s, XBC), bf16), _sds((4, XBC), f32), _sds((1, XBC), f32)],
        compiler_params=_params(("parallel", "arbitrary")),
    )(proj2, proj2, proj2, dxbc, dxbc, conv_w, conv_b)


def _ssd_consts():
    ex = np.zeros((128, SSD_W), np.float32)
    for h in range(NH):
        ex[h, h * HD:(h + 1) * HD] = 1.0
    sel = np.zeros((8, 128), np.float32)
    for h in range(NH):
        sel[h // 2, h] = 1.0
    par = np.zeros((128, 128), np.float32)
    for r in range(128):
        for h in range(NH):
            par[r, h] = 1.0 if (h % 2) == (r // 64) else 0.0
    ones_blk = np.zeros((128, 128), np.float32)
    for r in range(128):
        ones_blk[r, (r // 64) * 64:(r // 64) * 64 + 64] = 1.0
    return ex, np.ascontiguousarray(ex.T), sel, par, ones_blk


def _ssd_common(xbc_ref, dtr_ref, a_ref, dtb_ref, ex_ref, sel_ref, par_ref):
    xs = xbc_ref[:, 0:SSD_W]
    dt = _softplus(dtr_ref[...] + dtb_ref[...])
    adt = dt * a_ref[...]
    r_i = lax.broadcasted_iota(jnp.int32, (CHUNK, CHUNK), 0)
    c_i = lax.broadcasted_iota(jnp.int32, (CHUNK, CHUNK), 1)
    tril = (r_i >= c_i).astype(f32)
    cs = jnp.dot(tril, adt, precision=HIGHEST, preferred_element_type=f32)
    cs2 = jnp.concatenate([cs, cs], axis=0) * par_ref[...]
    cstp = lax.dot_general(sel_ref[...], cs2, (((1,), (1,)), ((), ())), precision=HIGHEST, preferred_element_type=f32)
    ex = ex_ref[...]
    dt_full = jnp.dot(dt, ex, precision=HIGHEST, preferred_element_type=f32)
    cs_full = jnp.dot(cs, ex, precision=HIGHEST, preferred_element_type=f32)
    return xs, dt, cs, cstp, dt_full, cs_full


def _pair_mask():
    l_i = lax.broadcasted_iota(jnp.int32, (CHUNK, 128), 0)
    lane = lax.broadcasted_iota(jnp.int32, (CHUNK, 128), 1)
    return l_i >= (lane % CHUNK), lane < HD


def _block_diag(xp, first):
    z = jnp.zeros_like(xp)
    return jnp.concatenate([jnp.where(first, xp, z), jnp.where(first, z, xp)], axis=0)


def _ssd_fwd(xbc, proj2, a_row, dtb_row, dsk_full):
    s = xbc.shape[0]
    nc = s // CHUNK
    ex, ext, sel, par, ones_blk = _ssd_consts()

    def body(xbc_ref, dtr_ref, a_ref, dtb_ref, dsk_ref, ex_ref, sel_ref, par_ref, y_ref, hs_ref, hst):
        @pl.when(pl.program_id(0) == 0)
        def _():
            hst[...] = jnp.zeros_like(hst)

        hs_ref[...] = hst[...]
        xs, dt, cs, cstp, dt_full, cs_full = _ssd_common(xbc_ref, dtr_ref, a_ref, dtb_ref, ex_ref, sel_ref, par_ref)
        cs_last = cs_full[CHUNK - 1:CHUNK, :]
        xdt = xs * dt_full
        causal, first = _pair_mask()
        for g in range(NG):
            gl = slice(g * GW, (g + 1) * GW)
            bg = xbc_ref[:, SSD_W + g * NSTATE:SSD_W + (g + 1) * NSTATE].astype(bf16)
            cg = xbc_ref[:, SSD_W + NG * NSTATE + g * NSTATE:SSD_W + NG * NSTATE + (g + 1) * NSTATE].astype(bf16)
            cb2 = lax.dot_general(cg, jnp.concatenate([bg, bg], axis=0), (((1,), (1,)), ((), ())), preferred_element_type=f32)
            hg = hst[g]
            y0 = jnp.dot(cg, hg.astype(bf16), preferred_element_type=f32)
            yoff = jnp.exp(cs_full[:, gl]) * y0
            for j in range(GW // 128):
                pair = g * (GW // 128) + j
                pl_ = slice(pair * 128, (pair + 1) * 128)
                seg = jnp.exp(jnp.where(causal, cs_full[:, pl_] - cstp[pair:pair + 1, :], -jnp.inf))
                m = (cb2 * seg).astype(bf16)
                yd = jnp.dot(m, _block_diag(xdt[:, pl_].astype(bf16), first), preferred_element_type=f32)
                y_ref[:, pl_] = yd + yoff[:, j * 128:(j + 1) * 128] + xs[:, pl_] * dsk_ref[:, pl_]
            xdec = (xdt[:, gl] * jnp.exp(cs_last[:, gl] - cs_full[:, gl])).astype(bf16)
            st = lax.dot_general(bg, xdec, (((0,), (0,)), ((), ())), preferred_element_type=f32)
            hst[g] = jnp.exp(cs_last[:, gl]) * hg + st

    const = lambda shape: pl.BlockSpec(shape, lambda c: tuple(0 for _ in shape))
    return pl.pallas_call(
        body, name="ssd_fwd", grid=(nc,),
        in_specs=[pl.BlockSpec((CHUNK, XBC), lambda c: (c, 0)), pl.BlockSpec((CHUNK, 128), lambda c: (c, DT_COL)),
                  const((1, 128)), const((1, 128)), const((1, SSD_W)), const((128, SSD_W)), const((8, 128)), const((128, 128))],
        out_specs=[pl.BlockSpec((CHUNK, SSD_W), lambda c: (c, 0)), pl.BlockSpec((None, NG, NSTATE, GW), lambda c: (c, 0, 0, 0))],
        out_shape=[_sds((s, SSD_W), f32), _sds((nc, NG, NSTATE, GW), f32)],
        scratch_shapes=[pltpu.VMEM((NG, NSTATE, GW), f32)], compiler_params=_params(("arbitrary",)),
    )(xbc, proj2, a_row, dtb_row, dsk_full, jnp.asarray(ex), jnp.asarray(sel), jnp.asarray(par))


def _ssd_bwd(xbc, proj2, dy, hsave, a_row, dtb_row, dsk_full):
    s = xbc.shape[0]
    nc = s // CHUNK
    ex, ext, sel, par, ones_blk = _ssd_consts()

    def body(xbc_ref, dtr_ref, dy_ref, hs_ref, a_ref, dtb_ref, dsk_ref, ex_ref, ext_ref, sel_ref, par_ref, ob_ref,
             dxbc_ref, ddtr_ref, dd_ref, da_ref, ddtb_ref, dh, a_dd, a_da, a_dtb, dcs_lane, dcs_b, dxdt):
        step = pl.program_id(0)

        @pl.when(step == 0)
        def _():
            dh[...] = jnp.zeros_like(dh)
            a_dd[...] = jnp.zeros_like(a_dd)
            a_da[...] = jnp.zeros_like(a_da)
            a_dtb[...] = jnp.zeros_like(a_dtb)

        xs, dt, cs, cstp, dt_full, cs_full = _ssd_common(xbc_ref, dtr_ref, a_ref, dtb_ref, ex_ref, sel_ref, par_ref)
        cs_last = cs_full[CHUNK - 1:CHUNK, :]
        xdt = xs * dt_full
        dyv = dy_ref[...]
        a_dd[...] += _fold8(dyv * xs)
        causal, first = _pair_mask()
        ones_l = jnp.ones((CHUNK, 128), f32)
        for g in range(NG):
            gl = slice(g * GW, (g + 1) * GW)
            bcol = slice(SSD_W + g * NSTATE, SSD_W + (g + 1) * NSTATE)
            ccol = slice(SSD_W + NG * NSTATE + g * NSTATE, SSD_W + NG * NSTATE + (g + 1) * NSTATE)
            bg = xbc_ref[:, bcol].astype(bf16)
            cg = xbc_ref[:, ccol].astype(bf16)
            bg2 = jnp.concatenate([bg, bg], axis=0)
            cb2 = lax.dot_general(cg, bg2, (((1,), (1,)), ((), ())), preferred_element_type=f32)
            hg = hs_ref[g]
            hgb = hg.astype(bf16)
            dhg = dh[g]
            dhgb = dhg.astype(bf16)
            eg = jnp.exp(cs_full[:, gl])
            dec = jnp.exp(cs_last[:, gl] - cs_full[:, gl])
            gam = jnp.exp(cs_last[:, gl])
            dyg = dyv[:, gl]
            xdt_g = xdt[:, gl]
            y0 = jnp.dot(cg, hgb, preferred_element_type=f32)
            dy0 = (eg * dyg).astype(bf16)
            dcm = lax.dot_general(dy0, hgb, (((1,), (1,)), ((), ())), preferred_element_type=f32)
            dh_prev = gam * dhg + lax.dot_general(cg, dy0, (((0,), (0,)), ((), ())), preferred_element_type=f32)
            dgam = jnp.sum(dhg * hg, axis=0, keepdims=True) * gam
            dxdec = jnp.dot(bg, dhgb, preferred_element_type=f32)
            dbm = lax.dot_general((xdt_g * dec).astype(bf16), dhgb, (((1,), (1,)), ((), ())), preferred_element_type=f32)
            t = dxdec * xdt_g * dec
            dcs_lane[:, gl] = dyg * eg * y0 - t
            dcs_lane[CHUNK - 1:CHUNK, gl] += jnp.sum(t, axis=0, keepdims=True) + dgam
            dxdt[:, gl] = dxdec * dec
            dcb2 = jnp.zeros((CHUNK, 128), f32)
            for j in range(GW // 128):
                pair = g * (GW // 128) + j
                pl_ = slice(pair * 128, (pair + 1) * 128)
                seg = jnp.exp(jnp.where(causal, cs_full[:, pl_] - cstp[pair:pair + 1, :], -jnp.inf))
                m = cb2 * seg
                mb = m.astype(bf16)
                rhs = _block_diag(xdt[:, pl_].astype(bf16), first)
                dyp = dyv[:, pl_].astype(bf16)
                dm = lax.dot_general(dyp, rhs, (((1,), (1,)), ((), ())), preferred_element_type=f32)
                tt = lax.dot_general(mb, dyp, (((0,), (0,)), ((), ())), preferred_element_type=f32)
                dxdt[:, pl_] += jnp.where(first, tt[0:CHUNK], tt[CHUNK:])
                dcb2 = dcb2 + dm * seg
                w = dm * m
                rsum = jnp.dot(w, ob_ref[...], precision=HIGHEST, preferred_element_type=f32)
                t2 = lax.dot_general(w, ones_l, (((0,), (0,)), ((), ())), precision=HIGHEST, preferred_element_type=f32)
                dcs_b[:, pl_] = rsum - jnp.where(first, t2[0:CHUNK], t2[CHUNK:])
            dcb2b = dcb2.astype(bf16)
            dcm = dcm + jnp.dot(dcb2b, bg2, preferred_element_type=f32)
            t3 = lax.dot_general(dcb2b, cg, (((0,), (0,)), ((), ())), preferred_element_type=f32)
            dxbc_ref[:, bcol] = dbm + t3[0:CHUNK] + t3[CHUNK:]
            dxbc_ref[:, ccol] = dcm
            dh[g] = dh_prev
        dcs = jnp.dot(dcs_lane[...] + dcs_b[...] * (1.0 / HD), ext_ref[...], precision=HIGHEST, preferred_element_type=f32)
        r_i = lax.broadcasted_iota(jnp.int32, (CHUNK, CHUNK), 0)
        c_i = lax.broadcasted_iota(jnp.int32, (CHUNK, CHUNK), 1)
        triu = (r_i <= c_i).astype(f32)
        da_ = jnp.dot(triu, dcs, precision=HIGHEST, preferred_element_type=f32)
        dxdtv = dxdt[...]
        ddt = da_ * a_ref[...] + jnp.dot(dxdtv * xs, ext_ref[...], precision=HIGHEST, preferred_element_type=f32)
        a_da[...] += _fold8(da_ * dt)
        dxbc_ref[:, 0:SSD_W] = dyv * dsk_ref[...] + dxdtv * dt_full
        ddtr = ddt * _sigmoid(dtr_ref[...] + dtb_ref[...])
        ddtr_ref[...] = ddtr
        a_dtb[...] += _fold8(ddtr)

        @pl.when(step == nc - 1)
        def _():
            dd_ref[...] = jnp.sum(jnp.dot(a_dd[...], ext_ref[...], precision=HIGHEST, preferred_element_type=f32), axis=0, keepdims=True)
            da_ref[...] = jnp.sum(a_da[...], axis=0, keepdims=True)
            ddtb_ref[...] = jnp.sum(a_dtb[...], axis=0, keepdims=True)

    rev = lambda c: nc - 1 - c
    const = lambda shape: pl.BlockSpec(shape, lambda c: tuple(0 for _ in shape))
    return pl.pallas_call(
        body, name="ssd_bwd", grid=(nc,),
        in_specs=[pl.BlockSpec((CHUNK, XBC), lambda c: (rev(c), 0)), pl.BlockSpec((CHUNK, 128), lambda c: (rev(c), DT_COL)),
                  pl.BlockSpec((CHUNK, SSD_W), lambda c: (rev(c), 0)), pl.BlockSpec((None, NG, NSTATE, GW), lambda c: (rev(c), 0, 0, 0)),
                  const((1, 128)), const((1, 128)), const((1, SSD_W)), const((128, SSD_W)), const((SSD_W, 128)),
                  const((8, 128)), const((128, 128)), const((128, 128))],
        out_specs=[pl.BlockSpec((CHUNK, XBC), lambda c: (rev(c), 0)), pl.BlockSpec((CHUNK, 128), lambda c: (rev(c), 0)),
                   const((1, 128)), const((1, 128)), const((1, 128))],
        out_shape=[_sds((s, XBC), f32), _sds((s, 128), f32), _sds((1, 128), f32), _sds((1, 128), f32), _sds((1, 128), f32)],
        scratch_shapes=[pltpu.VMEM((NG, NSTATE, GW), f32), pltpu.VMEM((8, SSD_W), f32), pltpu.VMEM((8, 128), f32), pltpu.VMEM((8, 128), f32),
                        pltpu.VMEM((CHUNK, SSD_W), f32), pltpu.VMEM((CHUNK, SSD_W), f32), pltpu.VMEM((CHUNK, SSD_W), f32)],
        compiler_params=_params(("arbitrary",)),
    )(xbc, proj2, dy, hsave, a_row, dtb_row, dsk_full, jnp.asarray(ex), jnp.asarray(ext), jnp.asarray(sel), jnp.asarray(par),
      jnp.asarray(ones_blk))


def _local_step(x, tgt, mods, g_mix, win, rel, conv_w, conv_b, dt_bias, a_log, d_skip, g_att, g_ssd, wout, g_ffn,
                wg4, wu4, wd4, g_final):
    s = x.shape[0]
    tm_e = 256 if s % 256 == 0 else s
    tm_m = 512 if s % 512 == 0 else s
    tm_l = 1024 if s % 1024 == 0 else s
    tk = 512 if s % 512 == 0 else s
    sh1, sc1, gt1, sh2, sc2, gt2 = [mods[:, i * D:(i + 1) * D] for i in range(6)]

    h1b = _norm_mod("norm_mod_1", x, g_mix, sc1, sh1, tm_e)
    qkv = _mm_nn_fullk("proj_qkv", h1b, win[:, :IN_A], tm_m, 768, bf16)
    proj2 = _mm_nn_fullk("proj_zxbcdt", h1b, win[:, IN_A:], tm_m, 896, f32)
    bias = _expand_bias(rel)
    att = _attn_fwd(qkv, bias)
    xbc = _ssd_conv(proj2, conv_w, conv_b, tm_e)
    a_row = jnp.pad(-jnp.exp(a_log), ((0, 0), (0, 128 - NH)))
    dtb_row = jnp.pad(dt_bias, ((0, 0), (0, 128 - NH)))
    dsk_full = jnp.repeat(d_skip, HD, axis=1)
    y, hsave = _ssd_fwd(xbc, proj2, a_row, dtb_row, dsk_full)
    mixcat = _mix_pre(att, y, proj2, g_att, g_ssd, tm_e)
    mix = _mm_nn_fullk("proj_out", mixcat, wout, tm_m, 1024, f32)
    x2, h2b = _resid_norm_mod(x, gt1, mix, g_ffn, sc2, sh2, tm_e)
    gate, up, act = _ffn_up(h2b, wg4, wu4, tm_m)
    ffn = _ffn_down(act, wd4, tm_l)

    dx3, dffn, loss, dg_final, dgt2 = _final_fwd_bwd(x2, ffn, gt2, g_final, tgt, tm_e)
    gwd4 = _grad_wdown4(act, dffn, 1024, tk)
    dgate, dup = _ffn_dact(dffn, wd4, gate, up, tm_m)
    gwg4 = _grad_cols4("grad_w_gate", h2b, dgate, 1024, tk)
    gwu4 = _grad_cols4("grad_w_up", h2b, dup, 1024, tk)
    dh2 = _ffn_dh(dgate, dup, wg4, wu4, tm_m)
    dx2, dmix, dsc2, dsh2, dg_ffn, dgt1 = _norm_mod_bwd("norm_mod_bwd_2", dh2, x2, g_ffn, sc2, dx3, tm_e, mix=mix, gt=gt1)
    gwout = _mm_tn("grad_w_out", mixcat, dmix, 1024, 1024, tk, bf16)
    dmc = _mm_nt("dmixcat", dmix, wout, tm_m, 1024, D, f32)
    datt, dy, dz, dg_att, dg_ssd = _mix_pre_bwd(dmc, att, y, proj2, g_att, g_ssd, tm_e)
    dq, dk, dv, gband = _attn_bwd(qkv, datt, bias)
    drel = _rel_bias_grad(gband)
    dxbc, ddtr, dd_row, da_row, ddtb_row = _ssd_bwd(xbc, proj2, dy, hsave, a_row, dtb_row, dsk_full)
    dxbc_raw, dconv_w, dconv_b = _ssd_conv_bwd(dxbc, proj2, conv_w, conv_b, tm_e)
    dproj = jnp.concatenate([dq, dk, dv, dz, dxbc_raw, ddtr.astype(bf16)], axis=1)
    gwin = _mm_tn("grad_w_in", h1b, dproj, 1024, 1152, tk, bf16)
    dh1 = _mm_nt("dh1", dproj, win, tm_m, D, 1920, f32)
    grad_x, dsc1, dsh1, dg_mix = _norm_mod_bwd("norm_mod_bwd_1", dh1, x, g_mix, sc1, dx2, tm_e)

    dmods = jnp.concatenate([dsh1, dsc1, dgt1, dsh2, dsc2, dgt2], axis=1)
    dd_skip = dd_row[:, :NH]
    da_log = da_row[:, :NH] * a_row[:, :NH]
    small = dict(g_mix=dg_mix, conv_b=dconv_b, dt_bias=ddtb_row[:, :NH], a_log=da_log, d_skip=dd_skip, g_att_out=dg_att,
                 g_ssd_out=dg_ssd, g_ffn=dg_ffn, g_final=dg_final, rel_bias=drel, conv_w=dconv_w)
    big = dict(w_in=gwin, w_out=gwout, w_gate=gwg4, w_up=gwu4, w_down=gwd4)
    return loss[0, 0], grad_x, dmods, small, big


HBM = pl.BlockSpec(memory_space=pl.ANY)
VMEM = pl.BlockSpec(memory_space=pltpu.VMEM)


def _place():
    x, y, c = lax.axis_index("x"), lax.axis_index("y"), lax.axis_index("c")
    chips = [(1 - x, y), (x, 1 - y), (1 - x, 1 - y)]
    return x, y, c, chips


def _allgather8(name, payload):
    r = payload.shape[0]

    def body(x_ref, out_ref, send_sems, recv_sems, local_sem):
        x, y, c, chips = _place()
        me, sibling = (x, y, c), (x, y, 1 - c)

        def slot(px, py, pc):
            return out_ref.at[4 * px + 2 * py + pc]

        def copy(k, block, to, src=None):
            return pltpu.make_async_remote_copy(
                src_ref=slot(*block) if src is None else src, dst_ref=slot(*block),
                send_sem=send_sems.at[k], recv_sem=recv_sems.at[k], device_id=to, device_id_type=MESH)

        mine = pltpu.make_async_copy(x_ref, slot(*me), local_sem)
        mine.start()
        first = [copy(0, me, sibling, src=x_ref)]
        first += [copy(1 + j, me, (*chip, c), src=x_ref) for j, chip in enumerate(chips)]
        for cp in first:
            cp.start()
        passed = [copy(4 + j, (*chip, c), sibling) for j, chip in enumerate(chips)]
        for j, chip in enumerate(chips):
            copy(1 + j, (*chip, c), me).wait_recv()
            passed[j].start()
        copy(0, sibling, me).wait_recv()
        for j, chip in enumerate(chips):
            copy(4 + j, (*chip, 1 - c), me).wait_recv()
        for cp in first + passed:
            cp.wait_send()
        mine.wait()

    return pl.pallas_call(
        body, name=name, out_shape=_sds((N_DEV, r, 128), f32), in_specs=[VMEM], out_specs=VMEM,
        scratch_shapes=[pltpu.SemaphoreType.DMA((7,)), pltpu.SemaphoreType.DMA((7,)), pltpu.SemaphoreType.DMA],
    )(payload)


def _sum8(g):
    r = g.shape[1]

    def body(g_ref, o_ref):
        acc = g_ref[0]
        for i in range(1, N_DEV):
            acc = acc + g_ref[i]
        o_ref[...] = acc

    return pl.pallas_call(body, name="sum8", out_shape=_sds((r, 128), f32))(g)


def _gather_weights(shards):
    nw = len(shards)

    def body(*refs):
        ins, outs = refs[:nw], refs[nw:2 * nw]
        send_sems, recv_sems, local_sems = refs[2 * nw:]
        x, y, c, chips = _place()
        k = 2 * x + y
        sibling = (x, y, 1 - c)
        local = [pltpu.make_async_copy(ins[w], outs[w].at[k], local_sems.at[w]) for w in range(nw)]
        for cp in local:
            cp.start()

        def half(w, kk, hh, sem, to, src=None):
            dst = outs[w].at[kk, hh]
            return pltpu.make_async_remote_copy(src_ref=dst if src is None else src, dst_ref=dst, send_sem=send_sems.at[w, sem],
                                                recv_sem=recv_sems.at[w, sem], device_id=to, device_id_type=MESH)

        sends = [half(w, k, c, j, (*chip, c), src=ins[w].at[c]) for w in range(nw) for j, chip in enumerate(chips)]
        for cp in sends:
            cp.start()
        passed = []
        for j, (px, py) in enumerate(chips):
            for w in range(nw):
                half(w, 2 * px + py, c, j, (x, y, c)).wait_recv()
                fwd = half(w, 2 * px + py, c, 3 + j, sibling)
                fwd.start()
                passed.append(fwd)
        for j, (px, py) in enumerate(chips):
            for w in range(nw):
                half(w, 2 * px + py, 1 - c, 3 + j, (x, y, c)).wait_recv()
        for cp in sends + passed:
            cp.wait_send()
        for cp in local:
            cp.wait()

    return pl.pallas_call(
        body, name="gather_weights", out_shape=[_sds((NSH,) + s.shape, bf16) for s in shards],
        in_specs=[HBM] * nw, out_specs=[HBM] * nw,
        scratch_shapes=[pltpu.SemaphoreType.DMA((nw, 6)), pltpu.SemaphoreType.DMA((nw, 6)), pltpu.SemaphoreType.DMA((nw,))],
    )(*shards)


def _rs_pair_exchange(grads):
    nw = len(grads)

    def body(*refs):
        ins, own, got = refs[:nw], refs[nw:2 * nw], refs[2 * nw:3 * nw]
        send_sems, recv_sems, local_sems = refs[3 * nw:]
        x, y, c, _ = _place()
        local = [pltpu.make_async_copy(ins[w].at[:, pl.ds(c, 1)], own[w], local_sems.at[w]) for w in range(nw)]
        cps = [pltpu.make_async_remote_copy(src_ref=ins[w].at[:, pl.ds(1 - c, 1)], dst_ref=got[w], send_sem=send_sems.at[w],
                                            recv_sem=recv_sems.at[w], device_id=(x, y, 1 - c), device_id_type=MESH) for w in range(nw)]
        for cp in local + cps:
            cp.start()
        for cp in cps:
            cp.wait()
        for cp in local:
            cp.wait()

    shp = [_sds((NSH, 1) + g.shape[2:], bf16) for g in grads]
    return pl.pallas_call(
        body, name="rs_pair_exchange", out_shape=shp + shp, in_specs=[HBM] * nw, out_specs=[HBM] * (2 * nw),
        scratch_shapes=[pltpu.SemaphoreType.DMA((nw,)), pltpu.SemaphoreType.DMA((nw,)), pltpu.SemaphoreType.DMA((nw,))],
    )(*grads)


def _rs_chip_exchange(sums):
    nw = len(sums)

    def body(*refs):
        ins, outs = refs[:nw], refs[nw:2 * nw]
        send_sems, recv_sems, local_sems = refs[2 * nw:]
        x, y, c, chips = _place()
        k = 2 * x + y
        local = [pltpu.make_async_copy(ins[w].at[k], outs[w].at[k], local_sems.at[w]) for w in range(nw)]
        for cp in local:
            cp.start()
        cps = []
        for w in range(nw):
            for j, (px, py) in enumerate(chips):
                cps.append(pltpu.make_async_remote_copy(src_ref=ins[w].at[2 * px + py], dst_ref=outs[w].at[k], send_sem=send_sems.at[w, j],
                                                        recv_sem=recv_sems.at[w, j], device_id=(px, py, c), device_id_type=MESH))
        for cp in cps:
            cp.start()
        for w in range(nw):
            for j, (px, py) in enumerate(chips):
                pltpu.make_async_remote_copy(src_ref=ins[w].at[k], dst_ref=outs[w].at[2 * px + py], send_sem=send_sems.at[w, j],
                                             recv_sem=recv_sems.at[w, j], device_id=(px, py, c), device_id_type=MESH).wait_recv()
        for cp in cps:
            cp.wait_send()
        for cp in local:
            cp.wait()

    return pl.pallas_call(
        body, name="rs_chip_exchange", out_shape=[_sds(s.shape, bf16) for s in sums], in_specs=[HBM] * nw, out_specs=[HBM] * nw,
        scratch_shapes=[pltpu.SemaphoreType.DMA((nw, 3)), pltpu.SemaphoreType.DMA((nw, 3)), pltpu.SemaphoreType.DMA((nw,))],
    )(*sums)


def _rs_pair_gather(halves):
    nw = len(halves)

    def body(*refs):
        ins, outs = refs[:nw], refs[nw:2 * nw]
        send_sems, recv_sems, local_sems = refs[2 * nw:]
        x, y, c, _ = _place()
        local = [pltpu.make_async_copy(ins[w], outs[w].at[c], local_sems.at[w]) for w in range(nw)]
        cps = [pltpu.make_async_remote_copy(src_ref=ins[w], dst_ref=outs[w].at[c], send_sem=send_sems.at[w], recv_sem=recv_sems.at[w],
                                            device_id=(x, y, 1 - c), device_id_type=MESH) for w in range(nw)]
        for cp in local + cps:
            cp.start()
        for w in range(nw):
            pltpu.make_async_remote_copy(src_ref=ins[w], dst_ref=outs[w].at[1 - c], send_sem=send_sems.at[w], recv_sem=recv_sems.at[w],
                                         device_id=(x, y, 1 - c), device_id_type=MESH).wait_recv()
        for cp in cps:
            cp.wait_send()
        for cp in local:
            cp.wait()

    return pl.pallas_call(
        body, name="rs_pair_gather", out_shape=[_sds((2,) + h.shape, f32) for h in halves], in_specs=[HBM] * nw, out_specs=[HBM] * nw,
        scratch_shapes=[pltpu.SemaphoreType.DMA((nw,)), pltpu.SemaphoreType.DMA((nw,)), pltpu.SemaphoreType.DMA((nw,))],
    )(*halves)


def _row_tile(r, c, nbuf):
    budget = 24 * 1024 * 1024 // (2 * nbuf * 4 * c)
    t = 8
    while t * 2 <= budget and r % (t * 2) == 0:
        t *= 2
    return t


def _cast_bf16(name, a):
    r, c = a.shape
    tr = _row_tile(r, c, 2)

    def body(a_ref, o_ref):
        o_ref[...] = a_ref[...].astype(bf16)

    spec = pl.BlockSpec((tr, c), lambda i: (i, 0))
    return pl.pallas_call(body, name=name, grid=(r // tr,), in_specs=[spec], out_specs=spec, out_shape=_sds((r, c), bf16),
                          compiler_params=_params(("parallel",)))(a)


def _pair_sum(name, own, got):
    _, _, rh, c = own.shape
    tr = _row_tile(rh, c, 2)

    def body(a_ref, b_ref, o_ref):
        o_ref[...] = (a_ref[...].astype(f32) + b_ref[...].astype(f32)).astype(bf16)

    ispec = pl.BlockSpec((None, None, tr, c), lambda k, i: (k, 0, i, 0))
    return pl.pallas_call(body, name=name, grid=(NSH, rh // tr), in_specs=[ispec, ispec],
                          out_specs=pl.BlockSpec((None, tr, c), lambda k, i: (k, i, 0)), out_shape=_sds((NSH, rh, c), bf16),
                          compiler_params=_params(("parallel", "parallel")))(own, got)


def _chip_sum(name, parts):
    _, rh, c = parts.shape
    tr = _row_tile(rh, c, 3)

    def body(p_ref, o_ref):
        acc = p_ref[0].astype(f32)
        for k in range(1, NSH):
            acc = acc + p_ref[k].astype(f32)
        o_ref[...] = acc

    return pl.pallas_call(body, name=name, grid=(rh // tr,), in_specs=[pl.BlockSpec((NSH, tr, c), lambda i: (0, i, 0))],
                          out_specs=pl.BlockSpec((tr, c), lambda i: (i, 0)), out_shape=_sds((rh, c), f32),
                          compiler_params=_params(("parallel",)))(parts)


def _mods_part(cond16, w_ada, b_part):
    n = w_ada.shape[1]
    tn = 512

    def body(c_ref, w_ref, b_ref, o_ref):
        cv = c_ref[...]
        o_ref[...] = _dot(cv * _sigmoid(cv), w_ref[...]) + b_ref[...]

    return pl.pallas_call(
        body, name="mods_part", grid=(n // tn,),
        in_specs=[pl.BlockSpec((16, D), lambda j: (0, 0)), pl.BlockSpec((D, tn), lambda j: (0, j)), pl.BlockSpec((1, tn), lambda j: (0, j))],
        out_specs=pl.BlockSpec((16, tn), lambda j: (0, j)), out_shape=_sds((16, n), f32), compiler_params=_params(("parallel",)),
    )(cond16, w_ada, b_part)


def _grad_w_ada(cond16, dm16):
    n = dm16.shape[1]
    tr = 256

    def body(c_ref, d_ref, o_ref):
        cv = c_ref[...]
        o_ref[...] = _dot(cv * _sigmoid(cv), d_ref[...], ta=True)

    return pl.pallas_call(
        body, name="grad_w_ada", grid=(D // tr,),
        in_specs=[pl.BlockSpec((16, tr), lambda i: (0, i)), pl.BlockSpec((16, n), lambda i: (0, 0))],
        out_specs=pl.BlockSpec((tr, n), lambda i: (i, 0)), out_shape=_sds((D, n), f32), compiler_params=_params(("parallel",)),
    )(cond16, dm16)


def _adamw(name, w, g, m, v):
    r, c = w.shape
    tr = _row_tile(r, c, 7)

    def body(w_ref, g_ref, m_ref, v_ref, d_ref, nm_ref, nv_ref):
        gv = g_ref[...]
        nm = ADAM_B1 * m_ref[...] + (1.0 - ADAM_B1) * gv
        nv = ADAM_B2 * v_ref[...] + (1.0 - ADAM_B2) * (gv * gv)
        nm_ref[...] = nm
        nv_ref[...] = nv
        m_hat = nm / (1.0 - ADAM_B1 ** ADAM_STEP)
        v_hat = nv / (1.0 - ADAM_B2 ** ADAM_STEP)
        d_ref[...] = -ADAM_LR * (m_hat / (jnp.sqrt(v_hat) + ADAM_EPS) + ADAM_WD * w_ref[...])

    spec = pl.BlockSpec((tr, c), lambda i: (i, 0))
    return pl.pallas_call(body, name=name, grid=(r // tr,), in_specs=[spec] * 4, out_specs=[spec] * 3, out_shape=[_sds((r, c), f32)] * 3,
                          compiler_params=_params(("parallel",)))(w, g, m, v)


def _pack(parts, rows):
    flat = []
    for p in parts:
        p = p.reshape(-1)
        flat.append(jnp.pad(p, (0, (-p.shape[0]) % 128)))
    v = jnp.concatenate(flat)
    return jnp.pad(v, (0, rows * 128 - v.shape[0])).reshape(rows, 128)


def _unpack(packed, sizes):
    lead = packed.shape[:-2]
    flat = packed.reshape(lead + (-1,))
    out, off = [], 0
    for n in sizes:
        out.append(flat[..., off:off + n])
        off += n + (-n) % 128
    return out


BIG = ("w_in", "w_out", "w_gate", "w_up", "w_down")
SMALL = ("b_ada", "g_mix", "conv_b", "dt_bias", "a_log", "d_skip", "g_att_out", "g_ssd_out", "g_ffn", "g_final", "rel_bias", "conv_w")
ORDER = ("w_ada", "b_ada", "g_mix", "w_in", "rel_bias", "conv_w", "conv_b", "dt_bias", "a_log", "d_skip", "g_att_out", "g_ssd_out",
         "w_out", "g_ffn", "w_gate", "w_up", "w_down", "g_final")
REL_SH = N_REL // NSH
CONVW_SH = XBC // NSH
ADA_SH = 6 * D // NSH


def kernel(x, c, w_ada, b_ada, g_mix, w_in, rel_bias, conv_w, conv_b, dt_bias, a_log, d_skip, g_att_out, g_ssd_out, w_out, g_ffn, w_gate, w_up, w_down, g_final, loss_target, m_w_ada, m_b_ada, m_g_mix, m_w_in, m_rel_bias, m_conv_w, m_conv_b, m_dt_bias, m_a_log, m_d_skip, m_g_att_out, m_g_ssd_out, m_w_out, m_g_ffn, m_w_gate, m_w_up, m_w_down, m_g_final, v_w_ada, v_b_ada, v_g_mix, v_w_in, v_rel_bias, v_conv_w, v_conv_b, v_dt_bias, v_a_log, v_d_skip, v_g_att_out, v_g_ssd_out, v_w_out, v_g_ffn, v_w_gate, v_w_up, v_w_down, v_g_final):
    args = dict(locals())
    w = {n: args[n] for n in ORDER}
    m = {n: args["m_" + n] for n in ORDER}
    v = {n: args["v_" + n] for n in ORDER}
    ix, iy, ic = lax.axis_index("x"), lax.axis_index("y"), lax.axis_index("c")
    chip = 2 * ix + iy
    dev = 2 * chip + ic
    s = x.shape[1]

    shards = [
        _cast_bf16("cast_w_in", jnp.pad(w_in[0], ((0, 0), (0, IN_SHP - IN_SH)))).reshape(2, D // 2, IN_SHP),
        _cast_bf16("cast_w_out", w_out[0]).reshape(2, D // NSH // 2, D),
        _cast_bf16("cast_w_gate", w_gate[0]).reshape(2, D // 2, FSH),
        _cast_bf16("cast_w_up", w_up[0]).reshape(2, D // 2, FSH),
        _cast_bf16("cast_w_down", w_down[0]).reshape(2, FSH // 2, D),
    ]
    win4, wout4, wg4, wu4, wd4 = _gather_weights(shards)
    win = jnp.transpose(win4.reshape(NSH, D, IN_SHP)[:, :, :IN_SH], (1, 0, 2)).reshape(D, IN_COLS)
    win = jnp.pad(win, ((0, 0), (0, IN_P - IN_COLS)))
    wout = wout4.reshape(D, D)
    wg4 = wg4.reshape(NSH, D, FSH)
    wu4 = wu4.reshape(NSH, D, FSH)
    wd4 = wd4.reshape(NSH, FSH, D)

    g1 = _allgather8("gather_inputs", _pack([c[0], rel_bias[0], conv_w[0]], 40))
    c_all, rel_sh, convw_sh = _unpack(g1, [D, NH * REL_SH, 4 * CONVW_SH])
    rel_full = jnp.concatenate([rel_sh[2 * k].reshape(NH, REL_SH) for k in range(NSH)], axis=1)
    convw_full = jnp.concatenate([convw_sh[2 * k].reshape(4, CONVW_SH) for k in range(NSH)], axis=1)
    cond16 = jnp.pad(c_all, ((0, 8), (0, 0)))
    b_part = lax.dynamic_slice_in_dim(b_ada, chip * ADA_SH, ADA_SH, axis=1)
    mods_part = _mods_part(cond16, w_ada[0], b_part)[:N_DEV]
    g2 = _allgather8("gather_mods", mods_part.reshape(N_DEV * ADA_SH // 128, 128))
    mods_all = jnp.concatenate([g2[2 * k].reshape(N_DEV, ADA_SH) for k in range(NSH)], axis=1)
    mods = lax.dynamic_slice_in_dim(mods_all, dev, 1, axis=0)

    loss, grad_x, dmods, small, big = _local_step(
        x[0], loss_target[0], mods, g_mix, win, rel_full, convw_full, conv_b, dt_bias, a_log, d_skip, g_att_out, g_ssd_out, wout, g_ffn,
        wg4, wu4, wd4, g_final[None, :])

    small_names = ("g_mix", "conv_b", "dt_bias", "a_log", "d_skip", "g_att_out", "g_ssd_out", "g_ffn", "g_final", "rel_bias", "conv_w")
    g3 = _allgather8("gather_small_grads", _pack([dmods] + [small[n] for n in small_names], 264))
    sizes = [6 * D] + [int(np.prod(small[n].shape)) for n in small_names]
    dmods_all = _unpack(g3, sizes)[0]
    summed = _unpack(_sum8(g3), sizes)
    grads = {"b_ada": summed[0].reshape(1, 6 * D)}
    for n, val in zip(small_names, summed[1:]):
        grads[n] = val.reshape(small[n].shape)
    grads["rel_bias"] = lax.dynamic_slice_in_dim(grads["rel_bias"], chip * REL_SH, REL_SH, axis=1)
    grads["conv_w"] = lax.dynamic_slice_in_dim(grads["conv_w"], chip * CONVW_SH, CONVW_SH, axis=1)
    grads["g_final"] = grads["g_final"].reshape(D)
    dm16 = jnp.pad(lax.dynamic_slice_in_dim(dmods_all, chip * ADA_SH, ADA_SH, axis=1), ((0, 8), (0, 0)))
    grads["w_ada"] = _grad_w_ada(cond16, dm16)

    gwin4 = jnp.stack([jnp.pad(big["w_in"][:, k * IN_SH:(k + 1) * IN_SH], ((0, 0), (0, IN_SHP - IN_SH))) for k in range(NSH)])
    stacked = [gwin4, big["w_out"].reshape(NSH, D // NSH, D), big["w_gate"], big["w_up"], big["w_down"]]
    stacked = [g.reshape(NSH, 2, g.shape[1] // 2, g.shape[2]) for g in stacked]
    res = _rs_pair_exchange(stacked)
    own, got = res[:len(BIG)], res[len(BIG):]
    sums = [_pair_sum("pair_sum_" + n, o, g) for n, o, g in zip(BIG, own, got)]
    parts = _rs_chip_exchange(sums)
    halves = [_chip_sum("chip_sum_" + n, p) for n, p in zip(BIG, parts)]
    full = _rs_pair_gather(halves)
    for n, f in zip(BIG, full):
        grads[n] = f.reshape(2 * f.shape[1], f.shape[2])
    grads["w_in"] = grads["w_in"][:, :IN_SH]

    delta, new_m, new_v = {}, {}, {}
    for n in ("w_ada",) + BIG:
        delta[n], new_m[n], new_v[n] = _adamw("adamw_" + n, w[n][0], grads[n], m[n][0], v[n][0])
    sw = _pack([w[n] for n in SMALL], 200)
    sg = _pack([grads[n] for n in SMALL], 200)
    sm = _pack([m[n] for n in SMALL], 200)
    sv = _pack([v[n] for n in SMALL], 200)
    ssz = [int(np.prod(w[n].shape)) for n in SMALL]
    for dst, packed in zip((delta, new_m, new_v), _adamw("adamw_small", sw, sg, sm, sv)):
        for n, val in zip(SMALL, _unpack(packed, ssz)):
            dst[n] = val

    def shaped(d, n):
        return d[n].reshape(w[n].shape)

    total = lax.psum(loss, ("x", "y", "c"))
    return (total, grad_x[None], *[shaped(grads, n) for n in ORDER], *[shaped(delta, n) for n in ORDER],
            *[shaped(new_m, n) for n in ORDER], *[shaped(new_v, n) for n in ORDER])
```

```python
import functools

import numpy as np
import jax
import jax.numpy as jnp
from jax import lax
from jax.experimental import pallas as pl
from jax.experimental.pallas import tpu as pltpu

f32 = jnp.float32
bf16 = jnp.bfloat16
HIGHEST = lax.Precision.HIGHEST
MESH = pl.DeviceIdType.MESH

D = 2048
CHUNK = 64
LEFT = 8
BAND = (LEFT + 1) * CHUNK
BANDP = 640
PADK = LEFT * CHUNK
NH = 16
HD = 64
ATT_W = NH * HD
SSD_W = 1024
NG = 2
NSTATE = 128
GW = SSD_W // NG
XBC = SSD_W + 2 * NG * NSTATE
N_REL = 320
REL_CLIP = 256
FFN = 5632
NSH = 4
FSH = FFN // NSH
IN_COLS = 5648
IN_SH = IN_COLS // NSH
IN_SHP = 1536
IN_A = 3 * ATT_W
IN_B = 2688
IN_P = IN_A + IN_B
EPS = 1e-6
N_DEV = 8

ADAM_LR = 0.001
ADAM_B1 = 0.9
ADAM_B2 = 0.999
ADAM_EPS = 1e-08
ADAM_WD = 0.01
ADAM_STEP = 10

VMEM_LIMIT = 56 * 1024 * 1024


def _params(sem):
    return pltpu.CompilerParams(dimension_semantics=sem, vmem_limit_bytes=VMEM_LIMIT)


def _sds(shape, dtype):
    return jax.ShapeDtypeStruct(shape, dtype)


def _fold8(v):
    r, w = v.shape
    return jnp.sum(v.reshape(r // 8, 8, w), axis=0)


def _sigmoid(v):
    return 1.0 / (1.0 + jnp.exp(-v))


def _softplus(v):
    return jnp.maximum(v, 0.0) + jnp.log(1.0 + jnp.exp(-jnp.abs(v)))


def _dot(a, b, ta=False, tb=False):
    dn = (((0 if ta else 1,), (1 if tb else 0,)), ((), ()))
    return lax.dot_general(a.astype(bf16), b.astype(bf16), dn, preferred_element_type=f32)


def _matmul(name, a, b, *, grid, a_spec, b_spec, o_spec, o_shape, o_dtype, acc_shape, ta=False, tb=False):
    nk = grid[2]

    def body(a_ref, b_ref, o_ref, acc_ref):
        p = _dot(a_ref[...], b_ref[...], ta, tb)
        if nk == 1:
            o_ref[...] = p.astype(o_ref.dtype)
        else:
            k = pl.program_id(2)

            @pl.when(k == 0)
            def _():
                acc_ref[...] = p

            @pl.when(k > 0)
            def _():
                acc_ref[...] += p

            @pl.when(k == nk - 1)
            def _():
                o_ref[...] = acc_ref[...].astype(o_ref.dtype)

    return pl.pallas_call(
        body, name=name, grid=grid, in_specs=[a_spec, b_spec], out_specs=o_spec,
        out_shape=_sds(o_shape, o_dtype), scratch_shapes=[pltpu.VMEM(acc_shape if nk > 1 else (8, 128), f32)],
        compiler_params=_params(("parallel", "parallel", "arbitrary")),
    )(a, b)


def _mm_nn_fullk(name, a, b, tm, tn, o_dtype):
    m, k = a.shape
    n = b.shape[1]
    return _matmul(name, a, b, grid=(m // tm, n // tn, 1),
                   a_spec=pl.BlockSpec((tm, k), lambda i, j, kk: (i, 0)),
                   b_spec=pl.BlockSpec((k, tn), lambda i, j, kk: (0, j)),
                   o_spec=pl.BlockSpec((tm, tn), lambda i, j, kk: (i, j)),
                   o_shape=(m, n), o_dtype=o_dtype, acc_shape=(tm, tn))


def _mm_nt(name, a, b, tm, tn, tk, o_dtype):
    m, k = a.shape
    n = b.shape[0]
    return _matmul(name, a, b, grid=(m // tm, n // tn, k // tk), tb=True,
                   a_spec=pl.BlockSpec((tm, tk), lambda i, j, kk: (i, kk)),
                   b_spec=pl.BlockSpec((tn, tk), lambda i, j, kk: (j, kk)),
                   o_spec=pl.BlockSpec((tm, tn), lambda i, j, kk: (i, j)),
                   o_shape=(m, n), o_dtype=o_dtype, acc_shape=(tm, tn))


def _mm_tn(name, a, b, tm, tn, tk, o_dtype):
    k, m = a.shape
    n = b.shape[1]
    return _matmul(name, a, b, grid=(m // tm, n // tn, k // tk), ta=True,
                   a_spec=pl.BlockSpec((tk, tm), lambda i, j, kk: (kk, i)),
                   b_spec=pl.BlockSpec((tk, tn), lambda i, j, kk: (kk, j)),
                   o_spec=pl.BlockSpec((tm, tn), lambda i, j, kk: (i, j)),
                   o_shape=(m, n), o_dtype=o_dtype, acc_shape=(tm, tn))


def _ffn_up(h2b, wg4, wu4, tm):
    s = h2b.shape[0]

    def body(h_ref, wg_ref, wu_ref, g_ref, u_ref, a_ref):
        h = h_ref[...]
        g = _dot(h, wg_ref[...])
        u = _dot(h, wu_ref[...])
        g_ref[...] = g
        u_ref[...] = u
        a_ref[...] = (g * _sigmoid(g) * u).astype(bf16)

    wspec = pl.BlockSpec((None, D, FSH), lambda k, i: (k, 0, 0))
    ospec = pl.BlockSpec((tm, FSH), lambda k, i: (i, k))
    return pl.pallas_call(
        body, name="ffn_up", grid=(NSH, s // tm),
        in_specs=[pl.BlockSpec((tm, D), lambda k, i: (i, 0)), wspec, wspec],
        out_specs=[ospec, ospec, ospec],
        out_shape=[_sds((s, FFN), f32), _sds((s, FFN), f32), _sds((s, FFN), bf16)],
        compiler_params=_params(("parallel", "parallel")),
    )(h2b, wg4, wu4)


def _ffn_down(act, wd4, tm):
    s = act.shape[0]
    return _matmul("ffn_down", act, wd4, grid=(s // tm, 1, NSH),
                   a_spec=pl.BlockSpec((tm, FSH), lambda i, j, k: (i, k)),
                   b_spec=pl.BlockSpec((None, FSH, D), lambda i, j, k: (k, 0, 0)),
                   o_spec=pl.BlockSpec((tm, D), lambda i, j, k: (i, 0)),
                   o_shape=(s, D), o_dtype=f32, acc_shape=(tm, D))


def _ffn_dact(dffn, wd4, gate, up, tm):
    s = dffn.shape[0]

    def body(d_ref, w_ref, g_ref, u_ref, dg_ref, du_ref):
        dact = _dot(d_ref[...], w_ref[...], tb=True)
        g = g_ref[...]
        sg = _sigmoid(g)
        dg_ref[...] = (dact * u_ref[...] * (sg * (1.0 + g * (1.0 - sg)))).astype(bf16)
        du_ref[...] = (dact * (g * sg)).astype(bf16)

    blk = pl.BlockSpec((tm, FSH), lambda k, i: (i, k))
    return pl.pallas_call(
        body, name="ffn_dact", grid=(NSH, s // tm),
        in_specs=[pl.BlockSpec((tm, D), lambda k, i: (i, 0)), pl.BlockSpec((None, FSH, D), lambda k, i: (k, 0, 0)), blk, blk],
        out_specs=[blk, blk], out_shape=[_sds((s, FFN), bf16), _sds((s, FFN), bf16)],
        compiler_params=_params(("parallel", "parallel")),
    )(dffn, wd4, gate, up)


def _ffn_dh(dgate, dup, wg4, wu4, tm):
    s = dgate.shape[0]

    def body(dg_ref, du_ref, wg_ref, wu_ref, o_ref, acc_ref):
        k = pl.program_id(1)
        p = _dot(dg_ref[...], wg_ref[...], tb=True) + _dot(du_ref[...], wu_ref[...], tb=True)

        @pl.when(k == 0)
        def _():
            acc_ref[...] = p

        @pl.when(k > 0)
        def _():
            acc_ref[...] += p

        @pl.when(k == NSH - 1)
        def _():
            o_ref[...] = acc_ref[...]

    aspec = pl.BlockSpec((tm, FSH), lambda i, k: (i, k))
    wspec = pl.BlockSpec((None, D, FSH), lambda i, k: (k, 0, 0))
    return pl.pallas_call(
        body, name="ffn_dh", grid=(s // tm, NSH), in_specs=[aspec, aspec, wspec, wspec],
        out_specs=pl.BlockSpec((tm, D), lambda i, k: (i, 0)), out_shape=_sds((s, D), f32),
        scratch_shapes=[pltpu.VMEM((tm, D), f32)], compiler_params=_params(("parallel", "arbitrary")),
    )(dgate, dup, wg4, wu4)


def _grad_cols4(name, h, dy, tm, tk):
    s = h.shape[0]
    return _matmul(name, h, dy, grid=(NSH, D // tm, s // tk), ta=True,
                   a_spec=pl.BlockSpec((tk, tm), lambda k, i, kk: (kk, i)),
                   b_spec=pl.BlockSpec((tk, FSH), lambda k, i, kk: (kk, k)),
                   o_spec=pl.BlockSpec((None, tm, FSH), lambda k, i, kk: (k, i, 0)),
                   o_shape=(NSH, D, FSH), o_dtype=bf16, acc_shape=(tm, FSH))


def _grad_wdown4(act, dffn, tn, tk):
    s = act.shape[0]
    return _matmul("grad_w_down", act, dffn, grid=(NSH, D // tn, s // tk), ta=True,
                   a_spec=pl.BlockSpec((tk, FSH), lambda k, j, kk: (kk, k)),
                   b_spec=pl.BlockSpec((tk, tn), lambda k, j, kk: (kk, j)),
                   o_spec=pl.BlockSpec((None, FSH, tn), lambda k, j, kk: (k, 0, j)),
                   o_shape=(NSH, FSH, D), o_dtype=bf16, acc_shape=(FSH, tn))


def _row_spec(w):
    return pl.BlockSpec((1, w), lambda i: (0, 0))


def _tile_spec(tm, w, col=0):
    return pl.BlockSpec((tm, w), lambda i: (i, col))


def _norm_mod(name, x, g, sc, sh, tm):
    s = x.shape[0]

    def body(x_ref, g_ref, sc_ref, sh_ref, o_ref):
        xv = x_ref[...]
        r = lax.rsqrt(jnp.mean(xv * xv, axis=-1, keepdims=True) + EPS)
        o_ref[...] = (xv * r * g_ref[...] * (1.0 + sc_ref[...]) + sh_ref[...]).astype(bf16)

    return pl.pallas_call(
        body, name=name, grid=(s // tm,), in_specs=[_tile_spec(tm, D), _row_spec(D), _row_spec(D), _row_spec(D)],
        out_specs=_tile_spec(tm, D), out_shape=_sds((s, D), bf16), compiler_params=_params(("parallel",)),
    )(x, g, sc, sh)


def _resid_norm_mod(x, gt, mix, g, sc, sh, tm):
    s = x.shape[0]

    def body(x_ref, gt_ref, m_ref, g_ref, sc_ref, sh_ref, x2_ref, h_ref):
        xv = x_ref[...] + gt_ref[...] * m_ref[...]
        x2_ref[...] = xv
        r = lax.rsqrt(jnp.mean(xv * xv, axis=-1, keepdims=True) + EPS)
        h_ref[...] = (xv * r * g_ref[...] * (1.0 + sc_ref[...]) + sh_ref[...]).astype(bf16)

    return pl.pallas_call(
        body, name="resid_norm_mod", grid=(s // tm,),
        in_specs=[_tile_spec(tm, D), _row_spec(D), _tile_spec(tm, D), _row_spec(D), _row_spec(D), _row_spec(D)],
        out_specs=[_tile_spec(tm, D), _tile_spec(tm, D)], out_shape=[_sds((s, D), f32), _sds((s, D), bf16)],
        compiler_params=_params(("parallel",)),
    )(x, gt, mix, g, sc, sh)


def _final_fwd_bwd(x2, ffn, gt2, g, tgt, tm):
    s = x2.shape[0]
    n = s // tm

    def body(x_ref, f_ref, gt_ref, g_ref, t_ref, dx_ref, df_ref, loss_ref, dg_ref, dgt_ref, a_loss, a_dg, a_dgt):
        i = pl.program_id(0)

        @pl.when(i == 0)
        def _():
            a_loss[...] = jnp.zeros_like(a_loss)
            a_dg[...] = jnp.zeros_like(a_dg)
            a_dgt[...] = jnp.zeros_like(a_dgt)

        fv = f_ref[...]
        gt = gt_ref[...]
        gv = g_ref[...]
        xv = x_ref[...] + gt * fv
        r = lax.rsqrt(jnp.mean(xv * xv, axis=-1, keepdims=True) + EPS)
        xh = xv * r
        e = xh * gv - t_ref[...]
        a_loss[...] += _fold8(e * e)
        dy = e * (1.0 / D)
        a_dg[...] += _fold8(dy * xh)
        t = dy * gv
        dx = r * (t - xh * jnp.mean(t * xh, axis=-1, keepdims=True))
        dx_ref[...] = dx
        a_dgt[...] += _fold8(dx * fv)
        df_ref[...] = (dx * gt).astype(bf16)

        @pl.when(i == n - 1)
        def _():
            tot = jnp.sum(jnp.sum(a_loss[...], axis=0, keepdims=True), axis=1, keepdims=True) * (0.5 / D)
            loss_ref[...] = jnp.broadcast_to(tot, (1, 128))
            dg_ref[...] = jnp.sum(a_dg[...], axis=0, keepdims=True)
            dgt_ref[...] = jnp.sum(a_dgt[...], axis=0, keepdims=True)

    return pl.pallas_call(
        body, name="final_fwd_bwd", grid=(n,),
        in_specs=[_tile_spec(tm, D), _tile_spec(tm, D), _row_spec(D), _row_spec(D), _tile_spec(tm, D)],
        out_specs=[_tile_spec(tm, D), _tile_spec(tm, D), _row_spec(128), _row_spec(D), _row_spec(D)],
        out_shape=[_sds((s, D), f32), _sds((s, D), bf16), _sds((1, 128), f32), _sds((1, D), f32), _sds((1, D), f32)],
        scratch_shapes=[pltpu.VMEM((8, D), f32)] * 3, compiler_params=_params(("arbitrary",)),
    )(x2, ffn, gt2, g, tgt)


def _norm_mod_bwd(name, dh, xin, g, sc, dres, tm, mix=None, gt=None):
    s = dh.shape[0]
    n = s // tm
    with_mix = mix is not None

    def body(*refs):
        if with_mix:
            dh_ref, x_ref, g_ref, sc_ref, dr_ref, m_ref, gt_ref, dx_ref, dm_ref, dsc_ref, dsh_ref, dg_ref, dgt_ref, a_sc, a_sh, a_g, a_gt = refs
        else:
            dh_ref, x_ref, g_ref, sc_ref, dr_ref, dx_ref, dsc_ref, dsh_ref, dg_ref, a_sc, a_sh, a_g = refs
        i = pl.program_id(0)

        @pl.when(i == 0)
        def _():
            a_sc[...] = jnp.zeros_like(a_sc)
            a_sh[...] = jnp.zeros_like(a_sh)
            a_g[...] = jnp.zeros_like(a_g)
            if with_mix:
                a_gt[...] = jnp.zeros_like(a_gt)

        dh = dh_ref[...]
        xv = x_ref[...]
        gv = g_ref[...]
        r = lax.rsqrt(jnp.mean(xv * xv, axis=-1, keepdims=True) + EPS)
        xh = xv * r
        a_sc[...] += _fold8(dh * xh * gv)
        a_sh[...] += _fold8(dh)
        dn = dh * (1.0 + sc_ref[...])
        a_g[...] += _fold8(dn * xh)
        t = dn * gv
        dx = dr_ref[...] + r * (t - xh * jnp.mean(t * xh, axis=-1, keepdims=True))
        dx_ref[...] = dx
        if with_mix:
            a_gt[...] += _fold8(dx * m_ref[...])
            dm_ref[...] = (dx * gt_ref[...]).astype(bf16)

        @pl.when(i == n - 1)
        def _():
            dsc_ref[...] = jnp.sum(a_sc[...], axis=0, keepdims=True)
            dsh_ref[...] = jnp.sum(a_sh[...], axis=0, keepdims=True)
            dg_ref[...] = jnp.sum(a_g[...], axis=0, keepdims=True)
            if with_mix:
                dgt_ref[...] = jnp.sum(a_gt[...], axis=0, keepdims=True)

    tile, row = _tile_spec(tm, D), _row_spec(D)
    if with_mix:
        ins, args = [tile, tile, row, row, tile, tile, row], (dh, xin, g, sc, dres, mix, gt)
        outs = [tile, tile, row, row, row, row]
        shapes = [_sds((s, D), f32), _sds((s, D), bf16)] + [_sds((1, D), f32)] * 4
        nacc = 4
    else:
        ins, args = [tile, tile, row, row, tile], (dh, xin, g, sc, dres)
        outs = [tile, row, row, row]
        shapes = [_sds((s, D), f32)] + [_sds((1, D), f32)] * 3
        nacc = 3
    return pl.pallas_call(
        body, name=name, grid=(n,), in_specs=ins, out_specs=outs, out_shape=shapes,
        scratch_shapes=[pltpu.VMEM((8, D), f32)] * nacc, compiler_params=_params(("arbitrary",)),
    )(*args)


def _mix_pre(att, y, proj2, g_att, g_ssd, tm):
    s = att.shape[0]

    def body(a_ref, y_ref, z_ref, ga_ref, gs_ref, o_ref):
        a = a_ref[...]
        ra = lax.rsqrt(jnp.mean(a * a, axis=-1, keepdims=True) + EPS)
        o_ref[:, 0:ATT_W] = (a * ra * ga_ref[...]).astype(bf16)
        z = z_ref[...]
        u = y_ref[...] * (z * _sigmoid(z))
        ru = lax.rsqrt(jnp.mean(u * u, axis=-1, keepdims=True) + EPS)
        o_ref[:, ATT_W:] = (u * ru * gs_ref[...]).astype(bf16)

    t = _tile_spec(tm, ATT_W)
    return pl.pallas_call(
        body, name="mix_pre", grid=(s // tm,), in_specs=[t, t, t, _row_spec(ATT_W), _row_spec(SSD_W)],
        out_specs=_tile_spec(tm, D), out_shape=_sds((s, D), bf16), compiler_params=_params(("parallel",)),
    )(att, y, proj2, g_att, g_ssd)


def _mix_pre_bwd(dmc, att, y, proj2, g_att, g_ssd, tm):
    s = att.shape[0]
    n = s // tm

    def body(da_ref, ds_ref, a_ref, y_ref, z_ref, ga_ref, gs_ref, datt_ref, dy_ref, dz_ref, dga_ref, dgs_ref, acc_a, acc_s):
        i = pl.program_id(0)

        @pl.when(i == 0)
        def _():
            acc_a[...] = jnp.zeros_like(acc_a)
            acc_s[...] = jnp.zeros_like(acc_s)

        a = a_ref[...]
        ra = lax.rsqrt(jnp.mean(a * a, axis=-1, keepdims=True) + EPS)
        ah = a * ra
        dan = da_ref[...]
        acc_a[...] += _fold8(dan * ah)
        t = dan * ga_ref[...]
        datt_ref[...] = (ra * (t - ah * jnp.mean(t * ah, axis=-1, keepdims=True))).astype(bf16)
        z = z_ref[...]
        yv = y_ref[...]
        sz = _sigmoid(z)
        sil = z * sz
        u = yv * sil
        ru = lax.rsqrt(jnp.mean(u * u, axis=-1, keepdims=True) + EPS)
        uh = u * ru
        dsn = ds_ref[...]
        acc_s[...] += _fold8(dsn * uh)
        t2 = dsn * gs_ref[...]
        du = ru * (t2 - uh * jnp.mean(t2 * uh, axis=-1, keepdims=True))
        dy_ref[...] = du * sil
        dz_ref[...] = (du * yv * (sz * (1.0 + z * (1.0 - sz)))).astype(bf16)

        @pl.when(i == n - 1)
        def _():
            dga_ref[...] = jnp.sum(acc_a[...], axis=0, keepdims=True)
            dgs_ref[...] = jnp.sum(acc_s[...], axis=0, keepdims=True)

    t = _tile_spec(tm, ATT_W)
    row = _row_spec(ATT_W)
    return pl.pallas_call(
        body, name="mix_pre_bwd", grid=(n,),
        in_specs=[_tile_spec(tm, ATT_W, 0), _tile_spec(tm, ATT_W, 1), t, t, t, row, row],
        out_specs=[t, t, t, row, row],
        out_shape=[_sds((s, ATT_W), bf16), _sds((s, SSD_W), f32), _sds((s, SSD_W), bf16), _sds((1, ATT_W), f32), _sds((1, SSD_W), f32)],
        scratch_shapes=[pltpu.VMEM((8, ATT_W), f32)] * 2, compiler_params=_params(("arbitrary",)),
    )(dmc, dmc, att, y, proj2, g_att, g_ssd)


def _softmax_rows(q, kb, bias, valid):
    sc = lax.dot_general(q, kb, (((1,), (1,)), ((), ())), preferred_element_type=f32) * (HD ** -0.5) + bias
    sc = jnp.where(valid, sc, -jnp.inf)
    e = jnp.exp(sc - jnp.max(sc, axis=-1, keepdims=True))
    return e / jnp.sum(e, axis=-1, keepdims=True)


def _attn_masks(r0):
    lane = lax.broadcasted_iota(jnp.int32, (CHUNK, 128), 1)
    kidx = lax.broadcasted_iota(jnp.int32, (CHUNK, BANDP), 1)
    valid = jnp.logical_and(kidx < BAND, r0 + kidx >= PADK)
    return lane < HD, valid


def _attn_fwd(qkv, bias):
    s = qkv.shape[0]
    nc = s // CHUNK
    npair = NH // 2

    def body(q_ref, k_ref, v_ref, b_ref, o_ref, kp, vp):
        zeros = jnp.zeros((PADK, 128), bf16)
        kp[0:PADK, :] = zeros
        vp[0:PADK, :] = zeros
        kp[PADK:PADK + s, :] = k_ref[...]
        vp[PADK:PADK + s, :] = v_ref[...]
        kp[PADK + s:, :] = jnp.zeros((CHUNK, 128), bf16)
        vp[PADK + s:, :] = jnp.zeros((CHUNK, 128), bf16)

        def chunk(i, carry):
            r0 = pl.multiple_of(i * CHUNK, CHUNK)
            first, valid = _attn_masks(r0)
            qc = q_ref[pl.ds(r0, CHUNK), :]
            kb = kp[pl.ds(r0, BANDP), :]
            vb = vp[pl.ds(r0, BANDP), :]
            zq = jnp.zeros_like(qc)
            outs = []
            for h in range(2):
                qm = jnp.where(first if h == 0 else jnp.logical_not(first), qc, zq)
                p = _softmax_rows(qm, kb, b_ref[h], valid)
                outs.append(jnp.dot(p.astype(bf16), vb, preferred_element_type=f32))
            o_ref[pl.ds(r0, CHUNK), :] = jnp.where(first, outs[0], outs[1])
            return carry

        lax.fori_loop(0, nc, chunk, 0)

    return pl.pallas_call(
        body, name="attn_fwd", grid=(npair,),
        in_specs=[pl.BlockSpec((s, 128), lambda p: (0, p)), pl.BlockSpec((s, 128), lambda p: (0, npair + p)),
                  pl.BlockSpec((s, 128), lambda p: (0, 2 * npair + p)), pl.BlockSpec((2, CHUNK, BANDP), lambda p: (p, 0, 0))],
        out_specs=pl.BlockSpec((s, 128), lambda p: (0, p)), out_shape=_sds((s, ATT_W), f32),
        scratch_shapes=[pltpu.VMEM((PADK + s + CHUNK, 128), bf16)] * 2, compiler_params=_params(("parallel",)),
    )(qkv, qkv, qkv, bias)


def _attn_bwd(qkv, datt, bias):
    s = qkv.shape[0]
    nc = s // CHUNK
    npair = NH // 2
    rows = PADK + s + CHUNK

    def body(q_ref, k_ref, v_ref, do_ref, b_ref, dq_ref, dk_ref, dv_ref, g_ref, kp, vp, dkp, dvp):
        zeros = jnp.zeros((PADK, 128), bf16)
        kp[0:PADK, :] = zeros
        vp[0:PADK, :] = zeros
        kp[PADK:PADK + s, :] = k_ref[...]
        vp[PADK:PADK + s, :] = v_ref[...]
        kp[PADK + s:, :] = jnp.zeros((CHUNK, 128), bf16)
        vp[PADK + s:, :] = jnp.zeros((CHUNK, 128), bf16)
        dkp[...] = jnp.zeros_like(dkp)
        dvp[...] = jnp.zeros_like(dvp)
        g_ref[...] = jnp.zeros_like(g_ref)

        def chunk(i, carry):
            r0 = pl.multiple_of(i * CHUNK, CHUNK)
            first, valid = _attn_masks(r0)
            qc = q_ref[pl.ds(r0, CHUNK), :]
            doc = do_ref[pl.ds(r0, CHUNK), :]
            kb = kp[pl.ds(r0, BANDP), :]
            vb = vp[pl.ds(r0, BANDP), :]
            zq = jnp.zeros_like(qc)
            dqs = []
            dk = jnp.zeros((BANDP, 128), f32)
            dv = jnp.zeros((BANDP, 128), f32)
            for h in range(2):
                sel = first if h == 0 else jnp.logical_not(first)
                qm = jnp.where(sel, qc, zq)
                dom = jnp.where(sel, doc, zq)
                p = _softmax_rows(qm, kb, b_ref[h], valid)
                dp = lax.dot_general(dom, vb, (((1,), (1,)), ((), ())), preferred_element_type=f32)
                ds = p * (dp - jnp.sum(p * dp, axis=-1, keepdims=True))
                g_ref[h] += ds
                dsb = ds.astype(bf16)
                dqs.append(jnp.dot(dsb, kb, preferred_element_type=f32))
                dk = dk + lax.dot_general(dsb, qm, (((0,), (0,)), ((), ())), preferred_element_type=f32)
                dv = dv + lax.dot_general(p.astype(bf16), dom, (((0,), (0,)), ((), ())), preferred_element_type=f32)
            dq_ref[pl.ds(r0, CHUNK), :] = (jnp.where(first, dqs[0], dqs[1]) * (HD ** -0.5)).astype(bf16)
            dkp[pl.ds(r0, BANDP), :] += dk * (HD ** -0.5)
            dvp[pl.ds(r0, BANDP), :] += dv
            return carry

        lax.fori_loop(0, nc, chunk, 0)
        dk_ref[...] = dkp[PADK:PADK + s, :].astype(bf16)
        dv_ref[...] = dvp[PADK:PADK + s, :].astype(bf16)

    col = lambda off: pl.BlockSpec((s, 128), lambda p: (0, off + p))
    return pl.pallas_call(
        body, name="attn_bwd", grid=(npair,),
        in_specs=[col(0), col(npair), col(2 * npair), col(0), pl.BlockSpec((2, CHUNK, BANDP), lambda p: (p, 0, 0))],
        out_specs=[col(0), col(0), col(0), pl.BlockSpec((2, CHUNK, BANDP), lambda p: (p, 0, 0))],
        out_shape=[_sds((s, ATT_W), bf16)] * 3 + [_sds((NH, CHUNK, BANDP), f32)],
        scratch_shapes=[pltpu.VMEM((rows, 128), bf16)] * 2 + [pltpu.VMEM((rows, 128), f32)] * 2,
        compiler_params=_params(("parallel",)),
    )(qkv, qkv, qkv, datt, bias)


def _rel_tables():
    onehot = np.zeros((BANDP, N_REL), np.float32)
    for j in range(BAND + CHUNK - 1):
        o = j - (CHUNK - 1)
        onehot[j, int(np.clip(PADK - o, -(CHUNK - 1), REL_CLIP)) + CHUNK - 1] = 1.0
    return onehot, np.ascontiguousarray(np.eye(CHUNK, dtype=np.float32)[::-1])


def _expand_bias(rel):
    ext = jnp.concatenate([jnp.broadcast_to(rel[:, N_REL - 1:], (NH, N_REL - 1)), rel[:, ::-1],
                           jnp.zeros((NH, BANDP - BAND + 1), f32)], axis=1)
    return jnp.stack([ext[:, CHUNK - 1 - q:CHUNK - 1 - q + BANDP] for q in range(CHUNK)], axis=1)


def _rel_bias_grad(gband):
    def body(g_ref, m_ref, flip_ref, o_ref, d2):
        for h in range(NH):
            rev = jnp.dot(flip_ref[...], g_ref[h], precision=HIGHEST, preferred_element_type=f32)
            rolled = pltpu.roll(rev, 0, 1, stride=1, stride_axis=0)
            d2[h:h + 1, :] = jnp.sum(rolled, axis=0, keepdims=True)
        o_ref[...] = jnp.dot(d2[...], m_ref[...], precision=HIGHEST, preferred_element_type=f32)

    onehot, flip = _rel_tables()
    return pl.pallas_call(
        body, name="rel_bias_grad", out_shape=_sds((NH, N_REL), f32), scratch_shapes=[pltpu.VMEM((NH, BANDP), f32)],
    )(gband, jnp.asarray(onehot), jnp.asarray(flip))


XBC_BLK = 512
XBC_COL0 = SSD_W // XBC_BLK
DT_COL = (SSD_W + XBC) // 128


def _conv_taps(ext, w_ref, b_ref, tm):
    n = ext.shape[0]
    pre = w_ref[3:4, :] * ext + b_ref[...]
    for j in range(3):
        pre = pre + w_ref[j:j + 1, :] * pltpu.roll(ext, 3 - j, 0)
    return pre


def _ssd_conv(proj2, conv_w, conv_b, tm):
    s = proj2.shape[0]
    nb = XBC // XBC_BLK

    def body(x_ref, p_ref, w_ref, b_ref, o_ref):
        i = pl.program_id(1)
        prev = jnp.where(i > 0, p_ref[...], 0.0)
        ext = jnp.concatenate([prev, x_ref[...]], axis=0)
        pre = _conv_taps(ext, w_ref, b_ref, tm)[8:8 + tm]
        o_ref[...] = pre * _sigmoid(pre)

    return pl.pallas_call(
        body, name="ssd_conv", grid=(nb, s // tm),
        in_specs=[pl.BlockSpec((tm, XBC_BLK), lambda j, i: (i, XBC_COL0 + j)),
                  pl.BlockSpec((8, XBC_BLK), lambda j, i: (jnp.maximum(i * (tm // 8) - 1, 0), XBC_COL0 + j)),
                  pl.BlockSpec((4, XBC_BLK), lambda j, i: (0, j)), pl.BlockSpec((1, XBC_BLK), lambda j, i: (0, j))],
        out_specs=pl.BlockSpec((tm, XBC_BLK), lambda j, i: (i, j)), out_shape=_sds((s, XBC), f32),
        compiler_params=_params(("parallel", "parallel")),
    )(proj2, proj2, conv_w, conv_b)


def _ssd_conv_bwd(dxbc, proj2, conv_w, conv_b, tm):
    s = proj2.shape[0]
    nb = XBC // XBC_BLK
    n = s // tm
    last8 = s // 8 - 1

    def body(x_ref, xp_ref, xn_ref, d_ref, dn_ref, w_ref, b_ref, o_ref, dw_ref, db_ref):
        i = pl.program_id(1)

        @pl.when(i == 0)
        def _():
            dw_ref[...] = jnp.zeros_like(dw_ref)
            db_ref[...] = jnp.zeros_like(db_ref)

        prev = jnp.where(i > 0, xp_ref[...], 0.0)
        ext = jnp.concatenate([prev, x_ref[...], xn_ref[...]], axis=0)
        pre = _conv_taps(ext, w_ref, b_ref, tm)
        sg = _sigmoid(pre)
        dnext = jnp.where(i < n - 1, dn_ref[...], 0.0)
        dext = jnp.concatenate([jnp.zeros((8, XBC_BLK), f32), d_ref[...], dnext], axis=0)
        dpre = dext * (sg * (1.0 + pre * (1.0 - sg)))
        rows = tm + 16
        dx = w_ref[3:4, :] * dpre
        for j in range(3):
            dx = dx + w_ref[j:j + 1, :] * pltpu.roll(dpre, rows - (3 - j), 0)
        o_ref[...] = dx[8:8 + tm].astype(bf16)
        dcur = dpre[8:8 + tm]
        db_ref[...] += jnp.sum(dcur, axis=0, keepdims=True)
        dw_ref[3:4, :] += jnp.sum(dcur * ext[8:8 + tm], axis=0, keepdims=True)
        for j in range(3):
            dw_ref[j:j + 1, :] += jnp.sum(dcur * pltpu.roll(ext, 3 - j, 0)[8:8 + tm], axis=0, keepdims=True)

    xcol = lambda j: XBC_COL0 + j
    return pl.pallas_call(
        body, name="ssd_conv_bwd", grid=(nb, n),
        in_specs=[pl.BlockSpec((tm, XBC_BLK), lambda j, i: (i, xcol(j))),
                  pl.BlockSpec((8, XBC_BLK), lambda j, i: (jnp.maximum(i * (tm // 8) - 1, 0), xcol(j))),
                  pl.BlockSpec((8, XBC_BLK), lambda j, i: (jnp.minimum((i + 1) * (tm // 8), last8), xcol(j))),
                  pl.BlockSpec((tm, XBC_BLK), lambda j, i: (i, j)),
                  pl.BlockSpec((8, XBC_BLK), lambda j, i: (jnp.minimum((i + 1) * (tm // 8), last8), j)),
                  pl.BlockSpec((4, XBC_BLK), lambda j, i: (0, j)), pl.BlockSpec((1, XBC_BLK), lambda j, i: (0, j))],
        out_specs=[pl.BlockSpec((tm, XBC_BLK), lambda j, i: (i, j)), pl.BlockSpec((4, XBC_BLK), lambda j, i: (0, j)),
                   pl.BlockSpec((1, XBC_BLK), lambda j, i: (0, j))],
        out_shape=[_sds((s, XBC), bf16), _sds((4, XBC), f32), _sds((1, XBC), f32)],
        compiler_params=_params(("parallel", "arbitrary")),
    )(proj2, proj2, proj2, dxbc, dxbc, conv_w, conv_b)


def _ssd_consts():
    ex = np.zeros((128, SSD_W), np.float32)
    for h in range(NH):
        ex[h, h * HD:(h + 1) * HD] = 1.0
    sel = np.zeros((8, 128), np.float32)
    for h in range(NH):
        sel[h // 2, h] = 1.0
    par = np.zeros((128, 128), np.float32)
    for r in range(128):
        for h in range(NH):
            par[r, h] = 1.0 if (h % 2) == (r // 64) else 0.0
    ones_blk = np.zeros((128, 128), np.float32)
    for r in range(128):
        ones_blk[r, (r // 64) * 64:(r // 64) * 64 + 64] = 1.0
    return ex, np.ascontiguousarray(ex.T), sel, par, ones_blk


def _ssd_common(xbc_ref, dtr_ref, a_ref, dtb_ref, ex_ref, sel_ref, par_ref):
    xs = xbc_ref[:, 0:SSD_W]
    dt = _softplus(dtr_ref[...] + dtb_ref[...])
    adt = dt * a_ref[...]
    r_i = lax.broadcasted_iota(jnp.int32, (CHUNK, CHUNK), 0)
    c_i = lax.broadcasted_iota(jnp.int32, (CHUNK, CHUNK), 1)
    tril = (r_i >= c_i).astype(f32)
    cs = jnp.dot(tril, adt, precision=HIGHEST, preferred_element_type=f32)
    cs2 = jnp.concatenate([cs, cs], axis=0) * par_ref[...]
    cstp = lax.dot_general(sel_ref[...], cs2, (((1,), (1,)), ((), ())), precision=HIGHEST, preferred_element_type=f32)
    ex = ex_ref[...]
    dt_full = jnp.dot(dt, ex, precision=HIGHEST, preferred_element_type=f32)
    cs_full = jnp.dot(cs, ex, precision=HIGHEST, preferred_element_type=f32)
    return xs, dt, cs, cstp, dt_full, cs_full


def _pair_mask():
    l_i = lax.broadcasted_iota(jnp.int32, (CHUNK, 128), 0)
    lane = lax.broadcasted_iota(jnp.int32, (CHUNK, 128), 1)
    return l_i >= (lane % CHUNK), lane < HD


def _block_diag(xp, first):
    z = jnp.zeros_like(xp)
    return jnp.concatenate([jnp.where(first, xp, z), jnp.where(first, z, xp)], axis=0)


def _ssd_fwd(xbc, proj2, a_row, dtb_row, dsk_full):
    s = xbc.shape[0]
    nc = s // CHUNK
    ex, ext, sel, par, ones_blk = _ssd_consts()

    def body(xbc_ref, dtr_ref, a_ref, dtb_ref, dsk_ref, ex_ref, sel_ref, par_ref, y_ref, hs_ref, hst):
        @pl.when(pl.program_id(0) == 0)
        def _():
            hst[...] = jnp.zeros_like(hst)

        hs_ref[...] = hst[...]
        xs, dt, cs, cstp, dt_full, cs_full = _ssd_common(xbc_ref, dtr_ref, a_ref, dtb_ref, ex_ref, sel_ref, par_ref)
        cs_last = cs_full[CHUNK - 1:CHUNK, :]
        xdt = xs * dt_full
        causal, first = _pair_mask()
        for g in range(NG):
            gl = slice(g * GW, (g + 1) * GW)
            bg = xbc_ref[:, SSD_W + g * NSTATE:SSD_W + (g + 1) * NSTATE].astype(bf16)
            cg = xbc_ref[:, SSD_W + NG * NSTATE + g * NSTATE:SSD_W + NG * NSTATE + (g + 1) * NSTATE].astype(bf16)
            cb2 = lax.dot_general(cg, jnp.concatenate([bg, bg], axis=0), (((1,), (1,)), ((), ())), preferred_element_type=f32)
            hg = hst[g]
            y0 = jnp.dot(cg, hg.astype(bf16), preferred_element_type=f32)
            yoff = jnp.exp(cs_full[:, gl]) * y0
            for j in range(GW // 128):
                pair = g * (GW // 128) + j
                pl_ = slice(pair * 128, (pair + 1) * 128)
                seg = jnp.exp(jnp.where(causal, cs_full[:, pl_] - cstp[pair:pair + 1, :], -jnp.inf))
                m = (cb2 * seg).astype(bf16)
                yd = jnp.dot(m, _block_diag(xdt[:, pl_].astype(bf16), first), preferred_element_type=f32)
                y_ref[:, pl_] = yd + yoff[:, j * 128:(j + 1) * 128] + xs[:, pl_] * dsk_ref[:, pl_]
            xdec = (xdt[:, gl] * jnp.exp(cs_last[:, gl] - cs_full[:, gl])).astype(bf16)
            st = lax.dot_general(bg, xdec, (((0,), (0,)), ((), ())), preferred_element_type=f32)
            hst[g] = jnp.exp(cs_last[:, gl]) * hg + st

    const = lambda shape: pl.BlockSpec(shape, lambda c: tuple(0 for _ in shape))
    return pl.pallas_call(
        body, name="ssd_fwd", grid=(nc,),
        in_specs=[pl.BlockSpec((CHUNK, XBC), lambda c: (c, 0)), pl.BlockSpec((CHUNK, 128), lambda c: (c, DT_COL)),
                  const((1, 128)), const((1, 128)), const((1, SSD_W)), const((128, SSD_W)), const((8, 128)), const((128, 128))],
        out_specs=[pl.BlockSpec((CHUNK, SSD_W), lambda c: (c, 0)), pl.BlockSpec((None, NG, NSTATE, GW), lambda c: (c, 0, 0, 0))],
        out_shape=[_sds((s, SSD_W), f32), _sds((nc, NG, NSTATE, GW), f32)],
        scratch_shapes=[pltpu.VMEM((NG, NSTATE, GW), f32)], compiler_params=_params(("arbitrary",)),
    )(xbc, proj2, a_row, dtb_row, dsk_full, jnp.asarray(ex), jnp.asarray(sel), jnp.asarray(par))


def _ssd_bwd(xbc, proj2, dy, hsave, a_row, dtb_row, dsk_full):
    s = xbc.shape[0]
    nc = s // CHUNK
    ex, ext, sel, par, ones_blk = _ssd_consts()

    def body(xbc_ref, dtr_ref, dy_ref, hs_ref, a_ref, dtb_ref, dsk_ref, ex_ref, ext_ref, sel_ref, par_ref, ob_ref,
             dxbc_ref, ddtr_ref, dd_ref, da_ref, ddtb_ref, dh, a_dd, a_da, a_dtb, dcs_lane, dcs_b, dxdt):
        step = pl.program_id(0)

        @pl.when(step == 0)
        def _():
            dh[...] = jnp.zeros_like(dh)
            a_dd[...] = jnp.zeros_like(a_dd)
            a_da[...] = jnp.zeros_like(a_da)
            a_dtb[...] = jnp.zeros_like(a_dtb)

        xs, dt, cs, cstp, dt_full, cs_full = _ssd_common(xbc_ref, dtr_ref, a_ref, dtb_ref, ex_ref, sel_ref, par_ref)
        cs_last = cs_full[CHUNK - 1:CHUNK, :]
        xdt = xs * dt_full
        dyv = dy_ref[...]
        a_dd[...] += _fold8(dyv * xs)
        causal, first = _pair_mask()
        ones_l = jnp.ones((CHUNK, 128), f32)
        for g in range(NG):
            gl = slice(g * GW, (g + 1) * GW)
            bcol = slice(SSD_W + g * NSTATE, SSD_W + (g + 1) * NSTATE)
            ccol = slice(SSD_W + NG * NSTATE + g * NSTATE, SSD_W + NG * NSTATE + (g + 1) * NSTATE)
            bg = xbc_ref[:, bcol].astype(bf16)
            cg = xbc_ref[:, ccol].astype(bf16)
            bg2 = jnp.concatenate([bg, bg], axis=0)
            cb2 = lax.dot_general(cg, bg2, (((1,), (1,)), ((), ())), preferred_element_type=f32)
            hg = hs_ref[g]
            hgb = hg.astype(bf16)
            dhg = dh[g]
            dhgb = dhg.astype(bf16)
            eg = jnp.exp(cs_full[:, gl])
            dec = jnp.exp(cs_last[:, gl] - cs_full[:, gl])
            gam = jnp.exp(cs_last[:, gl])
            dyg = dyv[:, gl]
            xdt_g = xdt[:, gl]
            y0 = jnp.dot(cg, hgb, preferred_element_type=f32)
            dy0 = (eg * dyg).astype(bf16)
            dcm = lax.dot_general(dy0, hgb, (((1,), (1,)), ((), ())), preferred_element_type=f32)
            dh_prev = gam * dhg + lax.dot_general(cg, dy0, (((0,), (0,)), ((), ())), preferred_element_type=f32)
            dgam = jnp.sum(dhg * hg, axis=0, keepdims=True) * gam
            dxdec = jnp.dot(bg, dhgb, preferred_element_type=f32)
            dbm = lax.dot_general((xdt_g * dec).astype(bf16), dhgb, (((1,), (1,)), ((), ())), preferred_element_type=f32)
            t = dxdec * xdt_g * dec
            dcs_lane[:, gl] = dyg * eg * y0 - t
            dcs_lane[CHUNK - 1:CHUNK, gl] += jnp.sum(t, axis=0, keepdims=True) + dgam
            dxdt[:, gl] = dxdec * dec
            dcb2 = jnp.zeros((CHUNK, 128), f32)
            for j in range(GW // 128):
                pair = g * (GW // 128) + j
                pl_ = slice(pair * 128, (pair + 1) * 128)
                seg = jnp.exp(jnp.where(causal, cs_full[:, pl_] - cstp[pair:pair + 1, :], -jnp.inf))
                m = cb2 * seg
                mb = m.astype(bf16)
                rhs = _block_diag(xdt[:, pl_].astype(bf16), first)
                dyp = dyv[:, pl_].astype(bf16)
                dm = lax.dot_general(dyp, rhs, (((1,), (1,)), ((), ())), preferred_element_type=f32)
                tt = lax.dot_general(mb, dyp, (((0,), (0,)), ((), ())), preferred_element_type=f32)
                dxdt[:, pl_] += jnp.where(first, tt[0:CHUNK], tt[CHUNK:])
                dcb2 = dcb2 + dm * seg
                w = dm * m
                rsum = jnp.dot(w, ob_ref[...], precision=HIGHEST, preferred_element_type=f32)
                t2 = lax.dot_general(w, ones_l, (((0,), (0,)), ((), ())), precision=HIGHEST, preferred_element_type=f32)
                dcs_b[:, pl_] = rsum - jnp.where(first, t2[0:CHUNK], t2[CHUNK:])
            dcb2b = dcb2.astype(bf16)
            dcm = dcm + jnp.dot(dcb2b, bg2, preferred_element_type=f32)
            t3 = lax.dot_general(dcb2b, cg, (((0,), (0,)), ((), ())), preferred_element_type=f32)
            dxbc_ref[:, bcol] = dbm + t3[0:CHUNK] + t3[CHUNK:]
            dxbc_ref[:, ccol] = dcm
            dh[g] = dh_prev
        dcs = jnp.dot(dcs_lane[...] + dcs_b[...] * (1.0 / HD), ext_ref[...], precision=HIGHEST, preferred_element_type=f32)
        r_i = lax.broadcasted_iota(jnp.int32, (CHUNK, CHUNK), 0)
        c_i = lax.broadcasted_iota(jnp.int32, (CHUNK, CHUNK), 1)
        triu = (r_i <= c_i).astype(f32)
        da_ = jnp.dot(triu, dcs, precision=HIGHEST, preferred_element_type=f32)
        dxdtv = dxdt[...]
        ddt = da_ * a_ref[...] + jnp.dot(dxdtv * xs, ext_ref[...], precision=HIGHEST, preferred_element_type=f32)
        a_da[...] += _fold8(da_ * dt)
        dxbc_ref[:, 0:SSD_W] = dyv * dsk_ref[...] + dxdtv * dt_full
        ddtr = ddt * _sigmoid(dtr_ref[...] + dtb_ref[...])
        ddtr_ref[...] = ddtr
        a_dtb[...] += _fold8(ddtr)

        @pl.when(step == nc - 1)
        def _():
            dd_ref[...] = jnp.sum(jnp.dot(a_dd[...], ext_ref[...], precision=HIGHEST, preferred_element_type=f32), axis=0, keepdims=True)
            da_ref[...] = jnp.sum(a_da[...], axis=0, keepdims=True)
            ddtb_ref[...] = jnp.sum(a_dtb[...], axis=0, keepdims=True)

    rev = lambda c: nc - 1 - c
    const = lambda shape: pl.BlockSpec(shape, lambda c: tuple(0 for _ in shape))
    return pl.pallas_call(
        body, name="ssd_bwd", grid=(nc,),
        in_specs=[pl.BlockSpec((CHUNK, XBC), lambda c: (rev(c), 0)), pl.BlockSpec((CHUNK, 128), lambda c: (rev(c), DT_COL)),
                  pl.BlockSpec((CHUNK, SSD_W), lambda c: (rev(c), 0)), pl.BlockSpec((None, NG, NSTATE, GW), lambda c: (rev(c), 0, 0, 0)),
                  const((1, 128)), const((1, 128)), const((1, SSD_W)), const((128, SSD_W)), const((SSD_W, 128)),
                  const((8, 128)), const((128, 128)), const((128, 128))],
        out_specs=[pl.BlockSpec((CHUNK, XBC), lambda c: (rev(c), 0)), pl.BlockSpec((CHUNK, 128), lambda c: (rev(c), 0)),
                   const((1, 128)), const((1, 128)), const((1, 128))],
        out_shape=[_sds((s, XBC), f32), _sds((s, 128), f32), _sds((1, 128), f32), _sds((1, 128), f32), _sds((1, 128), f32)],
        scratch_shapes=[pltpu.VMEM((NG, NSTATE, GW), f32), pltpu.VMEM((8, SSD_W), f32), pltpu.VMEM((8, 128), f32), pltpu.VMEM((8, 128), f32),
                        pltpu.VMEM((CHUNK, SSD_W), f32), pltpu.VMEM((CHUNK, SSD_W), f32), pltpu.VMEM((CHUNK, SSD_W), f32)],
        compiler_params=_params(("arbitrary",)),
    )(xbc, proj2, dy, hsave, a_row, dtb_row, dsk_full, jnp.asarray(ex), jnp.asarray(ext), jnp.asarray(sel), jnp.asarray(par),
      jnp.asarray(ones_blk))


def _local_step(x, tgt, mods, g_mix, win, rel, conv_w, conv_b, dt_bias, a_log, d_skip, g_att, g_ssd, wout, g_ffn,
                wg4, wu4, wd4, g_final):
    s = x.shape[0]
    tm_e = 256 if s % 256 == 0 else s
    tm_m = 512 if s % 512 == 0 else s
    tm_l = 1024 if s % 1024 == 0 else s
    tk = 512 if s % 512 == 0 else s
    sh1, sc1, gt1, sh2, sc2, gt2 = [mods[:, i * D:(i + 1) * D] for i in range(6)]

    h1b = _norm_mod("norm_mod_1", x, g_mix, sc1, sh1, tm_e)
    qkv = _mm_nn_fullk("proj_qkv", h1b, win[:, :IN_A], tm_m, 768, bf16)
    proj2 = _mm_nn_fullk("proj_zxbcdt", h1b, win[:, IN_A:], tm_m, 896, f32)
    bias = _expand_bias(rel)
    att = _attn_fwd(qkv, bias)
    xbc = _ssd_conv(proj2, conv_w, conv_b, tm_e)
    a_row = jnp.pad(-jnp.exp(a_log), ((0, 0), (0, 128 - NH)))
    dtb_row = jnp.pad(dt_bias, ((0, 0), (0, 128 - NH)))
    dsk_full = jnp.repeat(d_skip, HD, axis=1)
    y, hsave = _ssd_fwd(xbc, proj2, a_row, dtb_row, dsk_full)
    mixcat = _mix_pre(att, y, proj2, g_att, g_ssd, tm_e)
    mix = _mm_nn_fullk("proj_out", mixcat, wout, tm_m, 1024, f32)
    x2, h2b = _resid_norm_mod(x, gt1, mix, g_ffn, sc2, sh2, tm_e)
    gate, up, act = _ffn_up(h2b, wg4, wu4, tm_m)
    ffn = _ffn_down(act, wd4, tm_l)

    dx3, dffn, loss, dg_final, dgt2 = _final_fwd_bwd(x2, ffn, gt2, g_final, tgt, tm_e)
    gwd4 = _grad_wdown4(act, dffn, 1024, tk)
    dgate, dup = _ffn_dact(dffn, wd4, gate, up, tm_m)
    gwg4 = _grad_cols4("grad_w_gate", h2b, dgate, 1024, tk)
    gwu4 = _grad_cols4("grad_w_up", h2b, dup, 1024, tk)
    dh2 = _ffn_dh(dgate, dup, wg4, wu4, tm_m)
    dx2, dmix, dsc2, dsh2, dg_ffn, dgt1 = _norm_mod_bwd("norm_mod_bwd_2", dh2, x2, g_ffn, sc2, dx3, tm_e, mix=mix, gt=gt1)
    gwout = _mm_tn("grad_w_out", mixcat, dmix, 1024, 1024, tk, bf16)
    dmc = _mm_nt("dmixcat", dmix, wout, tm_m, 1024, D, f32)
    datt, dy, dz, dg_att, dg_ssd = _mix_pre_bwd(dmc, att, y, proj2, g_att, g_ssd, tm_e)
    dq, dk, dv, gband = _attn_bwd(qkv, datt, bias)
    drel = _rel_bias_grad(gband)
    dxbc, ddtr, dd_row, da_row, ddtb_row = _ssd_bwd(xbc, proj2, dy, hsave, a_row, dtb_row, dsk_full)
    dxbc_raw, dconv_w, dconv_b = _ssd_conv_bwd(dxbc, proj2, conv_w, conv_b, tm_e)
    dproj = jnp.concatenate([dq, dk, dv, dz, dxbc_raw, ddtr.astype(bf16)], axis=1)
    gwin = _mm_tn("grad_w_in", h1b, dproj, 1024, 1152, tk, bf16)
    dh1 = _mm_nt("dh1", dproj, win, tm_m, D, 1920, f32)
    grad_x, dsc1, dsh1, dg_mix = _norm_mod_bwd("norm_mod_bwd_1", dh1, x, g_mix, sc1, dx2, tm_e)

    dmods = jnp.concatenate([dsh1, dsc1, dgt1, dsh2, dsc2, dgt2], axis=1)
    dd_skip = dd_row[:, :NH]
    da_log = da_row[:, :NH] * a_row[:, :NH]
    small = dict(g_mix=dg_mix, conv_b=dconv_b, dt_bias=ddtb_row[:, :NH], a_log=da_log, d_skip=dd_skip, g_att_out=dg_att,
                 g_ssd_out=dg_ssd, g_ffn=dg_ffn, g_final=dg_final, rel_bias=drel, conv_w=dconv_w)
    big = dict(w_in=gwin, w_out=gwout, w_gate=gwg4, w_up=gwu4, w_down=gwd4)
    return loss[0, 0], grad_x, dmods, small, big


HBM = pl.BlockSpec(memory_space=pl.ANY)
VMEM = pl.BlockSpec(memory_space=pltpu.VMEM)


def _place():
    x, y, c = lax.axis_index("x"), lax.axis_index("y"), lax.axis_index("c")
    chips = [(1 - x, y), (x, 1 - y), (1 - x, 1 - y)]
    return x, y, c, chips


def _allgather8(name, payload):
    r = payload.shape[0]

    def body(x_ref, out_ref, send_sems, recv_sems, local_sem):
        x, y, c, chips = _place()
        me, sibling = (x, y, c), (x, y, 1 - c)

        def slot(px, py, pc):
            return out_ref.at[4 * px + 2 * py + pc]

        def copy(k, block, to, src=None):
            return pltpu.make_async_remote_copy(
                src_ref=slot(*block) if src is None else src, dst_ref=slot(*block),
                send_sem=send_sems.at[k], recv_sem=recv_sems.at[k], device_id=to, device_id_type=MESH)

        mine = pltpu.make_async_copy(x_ref, slot(*me), local_sem)
        mine.start()
        first = [copy(0, me, sibling, src=x_ref)]
        first += [copy(1 + j, me, (*chip, c), src=x_ref) for j, chip in enumerate(chips)]
        for cp in first:
            cp.start()
        passed = [copy(4 + j, (*chip, c), sibling) for j, chip in enumerate(chips)]
        for j, chip in enumerate(chips):
            copy(1 + j, (*chip, c), me).wait_recv()
            passed[j].start()
        copy(0, sibling, me).wait_recv()
        for j, chip in enumerate(chips):
            copy(4 + j, (*chip, 1 - c), me).wait_recv()
        for cp in first + passed:
            cp.wait_send()
        mine.wait()

    return pl.pallas_call(
        body, name=name, out_shape=_sds((N_DEV, r, 128), f32), in_specs=[VMEM], out_specs=VMEM,
        scratch_shapes=[pltpu.SemaphoreType.DMA((7,)), pltpu.SemaphoreType.DMA((7,)), pltpu.SemaphoreType.DMA],
    )(payload)


def _sum8(g):
    r = g.shape[1]

    def body(g_ref, o_ref):
        acc = g_ref[0]
        for i in range(1, N_DEV):
            acc = acc + g_ref[i]
        o_ref[...] = acc

    return pl.pallas_call(body, name="sum8", out_shape=_sds((r, 128), f32))(g)


def _gather_weights(shards):
    nw = len(shards)

    def body(*refs):
        ins, outs = refs[:nw], refs[nw:2 * nw]
        st_a, st_b, st_c = refs[2 * nw:3 * nw], refs[3 * nw:4 * nw], refs[4 * nw:5 * nw]
        send_sems, recv_sems, load_sems, store_sems = refs[5 * nw:]
        x, y, c, chips = _place()
        k = 2 * x + y
        sibling = (x, y, 1 - c)

        def half(w, kk, hh, sem, to, src):
            return pltpu.make_async_remote_copy(src_ref=src, dst_ref=outs[w].at[kk, hh], send_sem=send_sems.at[w, sem],
                                                recv_sem=recv_sems.at[w, sem], device_id=to, device_id_type=MESH)

        ld_a = [pltpu.make_async_copy(ins[w].at[c], st_a[w], load_sems.at[w, 0]) for w in range(nw)]
        ld_b = [pltpu.make_async_copy(ins[w].at[1 - c], st_b[w], load_sems.at[w, 1]) for w in range(nw)]
        for cp in ld_a + ld_b:
            cp.start()
        sends, stores = [], []
        for w in range(nw):
            ld_a[w].wait()
            for j, chip in enumerate(chips):
                sends.append(half(w, k, c, j, (*chip, c), st_a[w]))
                sends[-1].start()
            stores.append(pltpu.make_async_copy(st_a[w], outs[w].at[k, c], store_sems.at[w, 0]))
            stores[-1].start()
        st_own = []
        for w in range(nw):
            ld_b[w].wait()
            st_own.append(pltpu.make_async_copy(st_b[w], outs[w].at[k, 1 - c], store_sems.at[w, 1]))
            st_own[-1].start()
        for cp in st_own:
            cp.wait()
        fwds = {}
        for j, (px, py) in enumerate(chips):
            kq = 2 * px + py
            for w in range(nw):
                slot = st_b[w] if j % 2 == 0 else st_c[w]
                half(w, kq, c, j, (x, y, c), slot).wait_recv()
                if j == 2:
                    fwds[w, 0].wait_send()
                ld = pltpu.make_async_copy(outs[w].at[kq, c], slot, load_sems.at[w, 2 + j])
                ld.start()
                ld.wait()
                fwds[w, j] = half(w, kq, c, 3 + j, sibling, slot)
                fwds[w, j].start()
        for j, (px, py) in enumerate(chips):
            for w in range(nw):
                half(w, 2 * px + py, 1 - c, 3 + j, (x, y, c), st_c[w]).wait_recv()
        for cp in sends:
            cp.wait_send()
        for w in range(nw):
            fwds[w, 1].wait_send()
            fwds[w, 2].wait_send()
        for cp in stores:
            cp.wait()

    stage = [pltpu.VMEM(s.shape[1:], bf16) for s in shards]
    return pl.pallas_call(
        body, name="gather_weights", out_shape=[_sds((NSH,) + s.shape, bf16) for s in shards],
        in_specs=[HBM] * nw, out_specs=[HBM] * nw,
        scratch_shapes=stage * 3 + [pltpu.SemaphoreType.DMA((nw, 6)), pltpu.SemaphoreType.DMA((nw, 6)), pltpu.SemaphoreType.DMA((nw, 5)),
                                    pltpu.SemaphoreType.DMA((nw, 2))],
        compiler_params=pltpu.CompilerParams(vmem_limit_bytes=VMEM_LIMIT),
    )(*shards)


def _rs_pair_exchange(grads):
    nw = len(grads)

    def body(*refs):
        ins, got, stage = refs[:nw], refs[nw:2 * nw], refs[2 * nw:3 * nw]
        send_sems, recv_sems, load_sems = refs[3 * nw:]
        x, y, c, _ = _place()

        def load(w, kk):
            return pltpu.make_async_copy(ins[w].at[kk, 1 - c], stage[w].at[kk % 2], load_sems.at[w, kk])

        def send(w, kk):
            return pltpu.make_async_remote_copy(src_ref=stage[w].at[kk % 2], dst_ref=got[w].at[kk], send_sem=send_sems.at[w, kk],
                                                recv_sem=recv_sems.at[w, kk], device_id=(x, y, 1 - c), device_id_type=MESH)

        for kk in range(2):
            for w in range(nw):
                load(w, kk).start()
        for kk in range(NSH):
            for w in range(nw):
                load(w, kk).wait()
                send(w, kk).start()
            if kk + 2 < NSH:
                for w in range(nw):
                    send(w, kk).wait_send()
                    load(w, kk + 2).start()
        for kk in range(NSH - 2, NSH):
            for w in range(nw):
                send(w, kk).wait_send()
        for kk in range(NSH):
            for w in range(nw):
                send(w, kk).wait_recv()

    return pl.pallas_call(
        body, name="rs_pair_exchange", out_shape=[_sds((NSH,) + g.shape[2:], bf16) for g in grads], in_specs=[HBM] * nw, out_specs=[HBM] * nw,
        scratch_shapes=[pltpu.VMEM((2,) + g.shape[2:], bf16) for g in grads]
        + [pltpu.SemaphoreType.DMA((nw, NSH)), pltpu.SemaphoreType.DMA((nw, NSH)), pltpu.SemaphoreType.DMA((nw, NSH))],
        compiler_params=pltpu.CompilerParams(vmem_limit_bytes=VMEM_LIMIT),
    )(*grads)


def _rs_chip_exchange(sums):
    nw = len(sums)

    def body(*refs):
        ins, outs, stage = refs[:nw], refs[nw:2 * nw], refs[2 * nw:3 * nw]
        send_sems, recv_sems, load_sems, local_sems = refs[3 * nw:]
        x, y, c, chips = _place()
        k = 2 * x + y
        slabs = [2 * px + py for px, py in chips] + [k]

        def load(w, j):
            return pltpu.make_async_copy(ins[w].at[slabs[j]], stage[w].at[slabs[j]], load_sems.at[w, j])

        for j in range(NSH):
            for w in range(nw):
                load(w, j).start()
        cps = []
        for j, (px, py) in enumerate(chips):
            for w in range(nw):
                load(w, j).wait()
                cps.append(pltpu.make_async_remote_copy(src_ref=stage[w].at[slabs[j]], dst_ref=outs[w].at[k], send_sem=send_sems.at[w, j],
                                                        recv_sem=recv_sems.at[w, j], device_id=(px, py, c), device_id_type=MESH))
                cps[-1].start()
        local = []
        for w in range(nw):
            load(w, NSH - 1).wait()
            local.append(pltpu.make_async_copy(stage[w].at[k], outs[w].at[k], local_sems.at[w]))
            local[-1].start()
        for w in range(nw):
            for j, (px, py) in enumerate(chips):
                pltpu.make_async_remote_copy(src_ref=stage[w].at[k], dst_ref=outs[w].at[2 * px + py], send_sem=send_sems.at[w, j],
                                             recv_sem=recv_sems.at[w, j], device_id=(px, py, c), device_id_type=MESH).wait_recv()
        for cp in cps:
            cp.wait_send()
        for cp in local:
            cp.wait()

    return pl.pallas_call(
        body, name="rs_chip_exchange", out_shape=[_sds(s.shape, bf16) for s in sums], in_specs=[HBM] * nw, out_specs=[HBM] * nw,
        scratch_shapes=[pltpu.VMEM(s.shape, bf16) for s in sums]
        + [pltpu.SemaphoreType.DMA((nw, 3)), pltpu.SemaphoreType.DMA((nw, 3)), pltpu.SemaphoreType.DMA((nw, NSH)), pltpu.SemaphoreType.DMA((nw,))],
        compiler_params=pltpu.CompilerParams(vmem_limit_bytes=VMEM_LIMIT),
    )(*sums)


def _rs_pair_gather(halves):
    nw = len(halves)

    def body(*refs):
        ins, outs, stage = refs[:nw], refs[nw:2 * nw], refs[2 * nw:3 * nw]
        send_sems, recv_sems, local_sems, stage_sems = refs[3 * nw:]
        x, y, c, _ = _place()
        loads = [pltpu.make_async_copy(ins[w], stage[w], stage_sems.at[w]) for w in range(nw)]
        for cp in loads:
            cp.start()
        local, cps = [], []
        for w in range(nw):
            loads[w].wait()
            local.append(pltpu.make_async_copy(stage[w], outs[w].at[c], local_sems.at[w]))
            cps.append(pltpu.make_async_remote_copy(src_ref=stage[w], dst_ref=outs[w].at[c], send_sem=send_sems.at[w],
                                                    recv_sem=recv_sems.at[w], device_id=(x, y, 1 - c), device_id_type=MESH))
            local[w].start()
            cps[w].start()
        for w in range(nw):
            pltpu.make_async_remote_copy(src_ref=stage[w], dst_ref=outs[w].at[1 - c], send_sem=send_sems.at[w], recv_sem=recv_sems.at[w],
                                         device_id=(x, y, 1 - c), device_id_type=MESH).wait_recv()
        for cp in cps:
            cp.wait_send()
        for cp in local:
            cp.wait()

    return pl.pallas_call(
        body, name="rs_pair_gather", out_shape=[_sds((2,) + h.shape, f32) for h in halves], in_specs=[HBM] * nw, out_specs=[HBM] * nw,
        scratch_shapes=[pltpu.VMEM(h.shape, f32) for h in halves]
        + [pltpu.SemaphoreType.DMA((nw,)), pltpu.SemaphoreType.DMA((nw,)), pltpu.SemaphoreType.DMA((nw,)), pltpu.SemaphoreType.DMA((nw,))],
        compiler_params=pltpu.CompilerParams(vmem_limit_bytes=VMEM_LIMIT),
    )(*halves)


def _row_tile(r, c, nbuf):
    budget = 24 * 1024 * 1024 // (2 * nbuf * 4 * c)
    t = 8
    while t * 2 <= budget and r % (t * 2) == 0:
        t *= 2
    return t


def _cast_bf16(name, a):
    r, c = a.shape
    tr = _row_tile(r, c, 2)

    def body(a_ref, o_ref):
        o_ref[...] = a_ref[...].astype(bf16)

    spec = pl.BlockSpec((tr, c), lambda i: (i, 0))
    return pl.pallas_call(body, name=name, grid=(r // tr,), in_specs=[spec], out_specs=spec, out_shape=_sds((r, c), bf16),
                          compiler_params=_params(("parallel",)))(a)


def _pair_sum(name, core, grads, got):
    _, _, rh, c = grads.shape
    tr = _row_tile(rh, c, 2)

    def body(c_ref, a_ref, b_ref, o_ref):
        o_ref[...] = (a_ref[...].astype(f32) + b_ref[...].astype(f32)).astype(bf16)

    spec = pl.BlockSpec((None, tr, c), lambda k, i, c_ref: (k, i, 0))
    return pl.pallas_call(
        body, name=name, out_shape=_sds((NSH, rh, c), bf16),
        grid_spec=pltpu.PrefetchScalarGridSpec(
            num_scalar_prefetch=1, grid=(NSH, rh // tr),
            in_specs=[pl.BlockSpec((None, None, tr, c), lambda k, i, c_ref: (k, c_ref[0], i, 0)), spec], out_specs=spec),
        compiler_params=_params(("parallel", "parallel")))(core, grads, got)


def _chip_sum(name, parts):
    _, rh, c = parts.shape
    tr = _row_tile(rh, c, 3)

    def body(p_ref, o_ref):
        acc = p_ref[0].astype(f32)
        for k in range(1, NSH):
            acc = acc + p_ref[k].astype(f32)
        o_ref[...] = acc

    return pl.pallas_call(body, name=name, grid=(rh // tr,), in_specs=[pl.BlockSpec((NSH, tr, c), lambda i: (0, i, 0))],
                          out_specs=pl.BlockSpec((tr, c), lambda i: (i, 0)), out_shape=_sds((rh, c), f32),
                          compiler_params=_params(("parallel",)))(parts)


def _mods_part(cond16, w_ada, b_part):
    n = w_ada.shape[1]
    tn = 512

    def body(c_ref, w_ref, b_ref, o_ref):
        cv = c_ref[...]
        o_ref[...] = _dot(cv * _sigmoid(cv), w_ref[...]) + b_ref[...]

    return pl.pallas_call(
        body, name="mods_part", grid=(n // tn,),
        in_specs=[pl.BlockSpec((16, D), lambda j: (0, 0)), pl.BlockSpec((D, tn), lambda j: (0, j)), pl.BlockSpec((1, tn), lambda j: (0, j))],
        out_specs=pl.BlockSpec((16, tn), lambda j: (0, j)), out_shape=_sds((16, n), f32), compiler_params=_params(("parallel",)),
    )(cond16, w_ada, b_part)


def _grad_w_ada(cond16, dm16):
    n = dm16.shape[1]
    tr = 256

    def body(c_ref, d_ref, o_ref):
        cv = c_ref[...]
        o_ref[...] = _dot(cv * _sigmoid(cv), d_ref[...], ta=True)

    return pl.pallas_call(
        body, name="grad_w_ada", grid=(D // tr,),
        in_specs=[pl.BlockSpec((16, tr), lambda i: (0, i)), pl.BlockSpec((16, n), lambda i: (0, 0))],
        out_specs=pl.BlockSpec((tr, n), lambda i: (i, 0)), out_shape=_sds((D, n), f32), compiler_params=_params(("parallel",)),
    )(cond16, dm16)


def _adamw(name, w, g, m, v):
    r, c = w.shape
    tr = _row_tile(r, c, 7)

    def body(w_ref, g_ref, m_ref, v_ref, d_ref, nm_ref, nv_ref):
        gv = g_ref[...]
        nm = ADAM_B1 * m_ref[...] + (1.0 - ADAM_B1) * gv
        nv = ADAM_B2 * v_ref[...] + (1.0 - ADAM_B2) * (gv * gv)
        nm_ref[...] = nm
        nv_ref[...] = nv
        m_hat = nm / (1.0 - ADAM_B1 ** ADAM_STEP)
        v_hat = nv / (1.0 - ADAM_B2 ** ADAM_STEP)
        d_ref[...] = -ADAM_LR * (m_hat / (jnp.sqrt(v_hat) + ADAM_EPS) + ADAM_WD * w_ref[...])

    spec = pl.BlockSpec((tr, c), lambda i: (i, 0))
    return pl.pallas_call(body, name=name, grid=(r // tr,), in_specs=[spec] * 4, out_specs=[spec] * 3, out_shape=[_sds((r, c), f32)] * 3,
                          compiler_params=_params(("parallel",)))(w, g, m, v)


def _pack(parts, rows):
    flat = []
    for p in parts:
        p = p.reshape(-1)
        flat.append(jnp.pad(p, (0, (-p.shape[0]) % 128)))
    v = jnp.concatenate(flat)
    return jnp.pad(v, (0, rows * 128 - v.shape[0])).reshape(rows, 128)


def _unpack(packed, sizes):
    lead = packed.shape[:-2]
    flat = packed.reshape(lead + (-1,))
    out, off = [], 0
    for n in sizes:
        out.append(flat[..., off:off + n])
        off += n + (-n) % 128
    return out


BIG = ("w_in", "w_out", "w_gate", "w_up", "w_down")
SMALL = ("b_ada", "g_mix", "conv_b", "dt_bias", "a_log", "d_skip", "g_att_out", "g_ssd_out", "g_ffn", "g_final", "rel_bias", "conv_w")
ORDER = ("w_ada", "b_ada", "g_mix", "w_in", "rel_bias", "conv_w", "conv_b", "dt_bias", "a_log", "d_skip", "g_att_out", "g_ssd_out",
         "w_out", "g_ffn", "w_gate", "w_up", "w_down", "g_final")
REL_SH = N_REL // NSH
CONVW_SH = XBC // NSH
ADA_SH = 6 * D // NSH


def kernel(x, c, w_ada, b_ada, g_mix, w_in, rel_bias, conv_w, conv_b, dt_bias, a_log, d_skip, g_att_out, g_ssd_out, w_out, g_ffn, w_gate, w_up, w_down, g_final, loss_target, m_w_ada, m_b_ada, m_g_mix, m_w_in, m_rel_bias, m_conv_w, m_conv_b, m_dt_bias, m_a_log, m_d_skip, m_g_att_out, m_g_ssd_out, m_w_out, m_g_ffn, m_w_gate, m_w_up, m_w_down, m_g_final, v_w_ada, v_b_ada, v_g_mix, v_w_in, v_rel_bias, v_conv_w, v_conv_b, v_dt_bias, v_a_log, v_d_skip, v_g_att_out, v_g_ssd_out, v_w_out, v_g_ffn, v_w_gate, v_w_up, v_w_down, v_g_final):
    args = dict(locals())
    w = {n: args[n] for n in ORDER}
    m = {n: args["m_" + n] for n in ORDER}
    v = {n: args["v_" + n] for n in ORDER}
    ix, iy, ic = lax.axis_index("x"), lax.axis_index("y"), lax.axis_index("c")
    chip = 2 * ix + iy
    dev = 2 * chip + ic
    s = x.shape[1]

    shards = [
        _cast_bf16("cast_w_in", jnp.pad(w_in[0], ((0, 0), (0, IN_SHP - IN_SH)))).reshape(2, D // 2, IN_SHP),
        _cast_bf16("cast_w_out", w_out[0]).reshape(2, D // NSH // 2, D),
        _cast_bf16("cast_w_gate", w_gate[0]).reshape(2, D // 2, FSH),
        _cast_bf16("cast_w_up", w_up[0]).reshape(2, D // 2, FSH),
        _cast_bf16("cast_w_down", w_down[0]).reshape(2, FSH // 2, D),
    ]
    win4, wout4, wg4, wu4, wd4 = _gather_weights(shards)
    win = jnp.transpose(win4.reshape(NSH, D, IN_SHP)[:, :, :IN_SH], (1, 0, 2)).reshape(D, IN_COLS)
    win = jnp.pad(win, ((0, 0), (0, IN_P - IN_COLS)))
    wout = wout4.reshape(D, D)
    wg4 = wg4.reshape(NSH, D, FSH)
    wu4 = wu4.reshape(NSH, D, FSH)
    wd4 = wd4.reshape(NSH, FSH, D)

    g1 = _allgather8("gather_inputs", _pack([c[0], rel_bias[0], conv_w[0]], 40))
    c_all, rel_sh, convw_sh = _unpack(g1, [D, NH * REL_SH, 4 * CONVW_SH])
    rel_full = jnp.concatenate([rel_sh[2 * k].reshape(NH, REL_SH) for k in range(NSH)], axis=1)
    convw_full = jnp.concatenate([convw_sh[2 * k].reshape(4, CONVW_SH) for k in range(NSH)], axis=1)
    cond16 = jnp.pad(c_all, ((0, 8), (0, 0)))
    b_part = lax.dynamic_slice_in_dim(b_ada, chip * ADA_SH, ADA_SH, axis=1)
    mods_part = _mods_part(cond16, w_ada[0], b_part)[:N_DEV]
    g2 = _allgather8("gather_mods", mods_part.reshape(N_DEV * ADA_SH // 128, 128))
    mods_all = jnp.concatenate([g2[2 * k].reshape(N_DEV, ADA_SH) for k in range(NSH)], axis=1)
    mods = lax.dynamic_slice_in_dim(mods_all, dev, 1, axis=0)

    loss, grad_x, dmods, small, big = _local_step(
        x[0], loss_target[0], mods, g_mix, win, rel_full, convw_full, conv_b, dt_bias, a_log, d_skip, g_att_out, g_ssd_out, wout, g_ffn,
        wg4, wu4, wd4, g_final[None, :])

    small_names = ("g_mix", "conv_b", "dt_bias", "a_log", "d_skip", "g_att_out", "g_ssd_out", "g_ffn", "g_final", "rel_bias", "conv_w")
    g3 = _allgather8("gather_small_grads", _pack([dmods] + [small[n] for n in small_names], 264))
    sizes = [6 * D] + [int(np.prod(small[n].shape)) for n in small_names]
    dmods_all = _unpack(g3, sizes)[0]
    summed = _unpack(_sum8(g3), sizes)
    grads = {"b_ada": summed[0].reshape(1, 6 * D)}
    for n, val in zip(small_names, summed[1:]):
        grads[n] = val.reshape(small[n].shape)
    grads["rel_bias"] = lax.dynamic_slice_in_dim(grads["rel_bias"], chip * REL_SH, REL_SH, axis=1)
    grads["conv_w"] = lax.dynamic_slice_in_dim(grads["conv_w"], chip * CONVW_SH, CONVW_SH, axis=1)
    grads["g_final"] = grads["g_final"].reshape(D)
    dm16 = jnp.pad(lax.dynamic_slice_in_dim(dmods_all, chip * ADA_SH, ADA_SH, axis=1), ((0, 8), (0, 0)))
    grads["w_ada"] = _grad_w_ada(cond16, dm16)

    gwin4 = jnp.stack([jnp.pad(big["w_in"][:, k * IN_SH:(k + 1) * IN_SH], ((0, 0), (0, IN_SHP - IN_SH))) for k in range(NSH)])
    stacked = [gwin4, big["w_out"].reshape(NSH, D // NSH, D), big["w_gate"], big["w_up"], big["w_down"]]
    stacked = [g.reshape(NSH, 2, g.shape[1] // 2, g.shape[2]) for g in stacked]
    got = _rs_pair_exchange(stacked)
    core = jnp.reshape(ic, (1,)).astype(jnp.int32)
    sums = [_pair_sum("pair_sum_" + n, core, o, g) for n, o, g in zip(BIG, stacked, got)]
    parts = _rs_chip_exchange(sums)
    halves = [_chip_sum("chip_sum_" + n, p) for n, p in zip(BIG, parts)]
    full = _rs_pair_gather(halves)
    for n, f in zip(BIG, full):
        grads[n] = f.reshape(2 * f.shape[1], f.shape[2])
    grads["w_in"] = grads["w_in"][:, :IN_SH]

    delta, new_m, new_v = {}, {}, {}
    for n in ("w_ada",) + BIG:
        delta[n], new_m[n], new_v[n] = _adamw("adamw_" + n, w[n][0], grads[n], m[n][0], v[n][0])
    sw = _pack([w[n] for n in SMALL], 200)
    sg = _pack([grads[n] for n in SMALL], 200)
    sm = _pack([m[n] for n in SMALL], 200)
    sv = _pack([v[n] for n in SMALL], 200)
    ssz = [int(np.prod(w[n].shape)) for n in SMALL]
    for dst, packed in zip((delta, new_m, new_v), _adamw("adamw_small", sw, sg, sm, sv)):
        for n, val in zip(SMALL, _unpack(packed, ssz)):
            dst[n] = val

    def shaped(d, n):
        return d[n].reshape(w[n].shape)

    total = lax.psum(loss, ("x", "y", "c"))
    return (total, grad_x[None], *[shaped(grads, n) for n in ORDER], *[shaped(delta, n) for n in ORDER],
            *[shaped(new_m, n) for n in ORDER], *[shaped(new_v, n) for n in ORDER])
```

```python
import functools

import numpy as np
import jax
import jax.numpy as jnp
from jax import lax
from jax.experimental import pallas as pl
from jax.experimental.pallas import tpu as pltpu

f32 = jnp.float32
bf16 = jnp.bfloat16
HIGHEST = lax.Precision.HIGHEST
MESH = pl.DeviceIdType.MESH

D = 2048
CHUNK = 64
LEFT = 8
BAND = (LEFT + 1) * CHUNK
BANDP = 640
PADK = LEFT * CHUNK
NH = 16
HD = 64
ATT_W = NH * HD
SSD_W = 1024
NG = 2
NSTATE = 128
GW = SSD_W // NG
XBC = SSD_W + 2 * NG * NSTATE
N_REL = 320
REL_CLIP = 256
FFN = 5632
NSH = 4
FSH = FFN // NSH
IN_COLS = 5648
IN_SH = IN_COLS // NSH
IN_SHP = 1536
IN_A = 3 * ATT_W
IN_B = 2688
IN_P = IN_A + IN_B
EPS = 1e-6
N_DEV = 8

ADAM_LR = 0.001
ADAM_B1 = 0.9
ADAM_B2 = 0.999
ADAM_EPS = 1e-08
ADAM_WD = 0.01
ADAM_STEP = 10

VMEM_LIMIT = 56 * 1024 * 1024


def _params(sem):
    return pltpu.CompilerParams(dimension_semantics=sem, vmem_limit_bytes=VMEM_LIMIT)


def _sds(shape, dtype):
    return jax.ShapeDtypeStruct(shape, dtype)


def _fold8(v):
    r, w = v.shape
    return jnp.sum(v.reshape(r // 8, 8, w), axis=0)


def _sigmoid(v):
    return 1.0 / (1.0 + jnp.exp(-v))


def _softplus(v):
    return jnp.maximum(v, 0.0) + jnp.log(1.0 + jnp.exp(-jnp.abs(v)))


def _dot(a, b, ta=False, tb=False):
    dn = (((0 if ta else 1,), (1 if tb else 0,)), ((), ()))
    return lax.dot_general(a.astype(bf16), b.astype(bf16), dn, preferred_element_type=f32)


def _matmul(name, a, b, *, grid, a_spec, b_spec, o_spec, o_shape, o_dtype, acc_shape, ta=False, tb=False):
    nk = grid[2]

    def body(a_ref, b_ref, o_ref, acc_ref):
        p = _dot(a_ref[...], b_ref[...], ta, tb)
        if nk == 1:
            o_ref[...] = p.astype(o_ref.dtype)
        else:
            k = pl.program_id(2)

            @pl.when(k == 0)
            def _():
                acc_ref[...] = p

            @pl.when(k > 0)
            def _():
                acc_ref[...] += p

            @pl.when(k == nk - 1)
            def _():
                o_ref[...] = acc_ref[...].astype(o_ref.dtype)

    return pl.pallas_call(
        body, name=name, grid=grid, in_specs=[a_spec, b_spec], out_specs=o_spec,
        out_shape=_sds(o_shape, o_dtype), scratch_shapes=[pltpu.VMEM(acc_shape if nk > 1 else (8, 128), f32)],
        compiler_params=_params(("parallel", "parallel", "arbitrary")),
    )(a, b)


def _mm_nn_fullk(name, a, b, tm, tn, o_dtype):
    m, k = a.shape
    n = b.shape[1]
    return _matmul(name, a, b, grid=(m // tm, n // tn, 1),
                   a_spec=pl.BlockSpec((tm, k), lambda i, j, kk: (i, 0)),
                   b_spec=pl.BlockSpec((k, tn), lambda i, j, kk: (0, j)),
                   o_spec=pl.BlockSpec((tm, tn), lambda i, j, kk: (i, j)),
                   o_shape=(m, n), o_dtype=o_dtype, acc_shape=(tm, tn))


def _mm_nt(name, a, b, tm, tn, tk, o_dtype):
    m, k = a.shape
    n = b.shape[0]
    return _matmul(name, a, b, grid=(m // tm, n // tn, k // tk), tb=True,
                   a_spec=pl.BlockSpec((tm, tk), lambda i, j, kk: (i, kk)),
                   b_spec=pl.BlockSpec((tn, tk), lambda i, j, kk: (j, kk)),
                   o_spec=pl.BlockSpec((tm, tn), lambda i, j, kk: (i, j)),
                   o_shape=(m, n), o_dtype=o_dtype, acc_shape=(tm, tn))


def _mm_tn(name, a, b, tm, tn, tk, o_dtype):
    k, m = a.shape
    n = b.shape[1]
    return _matmul(name, a, b, grid=(m // tm, n // tn, k // tk), ta=True,
                   a_spec=pl.BlockSpec((tk, tm), lambda i, j, kk: (kk, i)),
                   b_spec=pl.BlockSpec((tk, tn), lambda i, j, kk: (kk, j)),
                   o_spec=pl.BlockSpec((tm, tn), lambda i, j, kk: (i, j)),
                   o_shape=(m, n), o_dtype=o_dtype, acc_shape=(tm, tn))


def _ffn_up(h2b, wg4, wu4, tm):
    s = h2b.shape[0]

    def body(h_ref, wg_ref, wu_ref, g_ref, u_ref, a_ref):
        h = h_ref[...]
        g = _dot(h, wg_ref[...])
        u = _dot(h, wu_ref[...])
        g_ref[...] = g
        u_ref[...] = u
        a_ref[...] = (g * _sigmoid(g) * u).astype(bf16)

    wspec = pl.BlockSpec((None, D, FSH), lambda k, i: (k, 0, 0))
    ospec = pl.BlockSpec((tm, FSH), lambda k, i: (i, k))
    return pl.pallas_call(
        body, name="ffn_up", grid=(NSH, s // tm),
        in_specs=[pl.BlockSpec((tm, D), lambda k, i: (i, 0)), wspec, wspec],
        out_specs=[ospec, ospec, ospec],
        out_shape=[_sds((s, FFN), f32), _sds((s, FFN), f32), _sds((s, FFN), bf16)],
        compiler_params=_params(("parallel", "parallel")),
    )(h2b, wg4, wu4)


def _ffn_down(act, wd4, tm):
    s = act.shape[0]
    return _matmul("ffn_down", act, wd4, grid=(s // tm, 1, NSH),
                   a_spec=pl.BlockSpec((tm, FSH), lambda i, j, k: (i, k)),
                   b_spec=pl.BlockSpec((None, FSH, D), lambda i, j, k: (k, 0, 0)),
                   o_spec=pl.BlockSpec((tm, D), lambda i, j, k: (i, 0)),
                   o_shape=(s, D), o_dtype=f32, acc_shape=(tm, D))


def _ffn_dact(dffn, wd4, gate, up, tm):
    s = dffn.shape[0]

    def body(d_ref, w_ref, g_ref, u_ref, dg_ref, du_ref):
        dact = _dot(d_ref[...], w_ref[...], tb=True)
        g = g_ref[...]
        sg = _sigmoid(g)
        dg_ref[...] = (dact * u_ref[...] * (sg * (1.0 + g * (1.0 - sg)))).astype(bf16)
        du_ref[...] = (dact * (g * sg)).astype(bf16)

    blk = pl.BlockSpec((tm, FSH), lambda k, i: (i, k))
    return pl.pallas_call(
        body, name="ffn_dact", grid=(NSH, s // tm),
        in_specs=[pl.BlockSpec((tm, D), lambda k, i: (i, 0)), pl.BlockSpec((None, FSH, D), lambda k, i: (k, 0, 0)), blk, blk],
        out_specs=[blk, blk], out_shape=[_sds((s, FFN), bf16), _sds((s, FFN), bf16)],
        compiler_params=_params(("parallel", "parallel")),
    )(dffn, wd4, gate, up)


def _ffn_dh(dgate, dup, wg4, wu4, tm):
    s = dgate.shape[0]

    def body(dg_ref, du_ref, wg_ref, wu_ref, o_ref, acc_ref):
        k = pl.program_id(1)
        p = _dot(dg_ref[...], wg_ref[...], tb=True) + _dot(du_ref[...], wu_ref[...], tb=True)

        @pl.when(k == 0)
        def _():
            acc_ref[...] = p

        @pl.when(k > 0)
        def _():
            acc_ref[...] += p

        @pl.when(k == NSH - 1)
        def _():
            o_ref[...] = acc_ref[...]

    aspec = pl.BlockSpec((tm, FSH), lambda i, k: (i, k))
    wspec = pl.BlockSpec((None, D, FSH), lambda i, k: (k, 0, 0))
    return pl.pallas_call(
        body, name="ffn_dh", grid=(s // tm, NSH), in_specs=[aspec, aspec, wspec, wspec],
        out_specs=pl.BlockSpec((tm, D), lambda i, k: (i, 0)), out_shape=_sds((s, D), f32),
        scratch_shapes=[pltpu.VMEM((tm, D), f32)], compiler_params=_params(("parallel", "arbitrary")),
    )(dgate, dup, wg4, wu4)


def _grad_cols4(name, h, dy, tm, tk):
    s = h.shape[0]
    return _matmul(name, h, dy, grid=(NSH, D // tm, s // tk), ta=True,
                   a_spec=pl.BlockSpec((tk, tm), lambda k, i, kk: (kk, i)),
                   b_spec=pl.BlockSpec((tk, FSH), lambda k, i, kk: (kk, k)),
                   o_spec=pl.BlockSpec((None, tm, FSH), lambda k, i, kk: (k, i, 0)),
                   o_shape=(NSH, D, FSH), o_dtype=bf16, acc_shape=(tm, FSH))


def _grad_wdown4(act, dffn, tn, tk):
    s = act.shape[0]
    return _matmul("grad_w_down", act, dffn, grid=(NSH, D // tn, s // tk), ta=True,
                   a_spec=pl.BlockSpec((tk, FSH), lambda k, j, kk: (kk, k)),
                   b_spec=pl.BlockSpec((tk, tn), lambda k, j, kk: (kk, j)),
                   o_spec=pl.BlockSpec((None, FSH, tn), lambda k, j, kk: (k, 0, j)),
                   o_shape=(NSH, FSH, D), o_dtype=bf16, acc_shape=(FSH, tn))


def _row_spec(w):
    return pl.BlockSpec((1, w), lambda i: (0, 0))


def _tile_spec(tm, w, col=0):
    return pl.BlockSpec((tm, w), lambda i: (i, col))


def _norm_mod(name, x, g, sc, sh, tm):
    s = x.shape[0]

    def body(x_ref, g_ref, sc_ref, sh_ref, o_ref):
        xv = x_ref[...]
        r = lax.rsqrt(jnp.mean(xv * xv, axis=-1, keepdims=True) + EPS)
        o_ref[...] = (xv * r * g_ref[...] * (1.0 + sc_ref[...]) + sh_ref[...]).astype(bf16)

    return pl.pallas_call(
        body, name=name, grid=(s // tm,), in_specs=[_tile_spec(tm, D), _row_spec(D), _row_spec(D), _row_spec(D)],
        out_specs=_tile_spec(tm, D), out_shape=_sds((s, D), bf16), compiler_params=_params(("parallel",)),
    )(x, g, sc, sh)


def _resid_norm_mod(x, gt, mix, g, sc, sh, tm):
    s = x.shape[0]

    def body(x_ref, gt_ref, m_ref, g_ref, sc_ref, sh_ref, x2_ref, h_ref):
        xv = x_ref[...] + gt_ref[...] * m_ref[...]
        x2_ref[...] = xv
        r = lax.rsqrt(jnp.mean(xv * xv, axis=-1, keepdims=True) + EPS)
        h_ref[...] = (xv * r * g_ref[...] * (1.0 + sc_ref[...]) + sh_ref[...]).astype(bf16)

    return pl.pallas_call(
        body, name="resid_norm_mod", grid=(s // tm,),
        in_specs=[_tile_spec(tm, D), _row_spec(D), _tile_spec(tm, D), _row_spec(D), _row_spec(D), _row_spec(D)],
        out_specs=[_tile_spec(tm, D), _tile_spec(tm, D)], out_shape=[_sds((s, D), f32), _sds((s, D), bf16)],
        compiler_params=_params(("parallel",)),
    )(x, gt, mix, g, sc, sh)


def _final_fwd_bwd(x2, ffn, gt2, g, tgt, tm):
    s = x2.shape[0]
    n = s // tm

    def body(x_ref, f_ref, gt_ref, g_ref, t_ref, dx_ref, df_ref, loss_ref, dg_ref, dgt_ref, a_loss, a_dg, a_dgt):
        i = pl.program_id(0)

        @pl.when(i == 0)
        def _():
            a_loss[...] = jnp.zeros_like(a_loss)
            a_dg[...] = jnp.zeros_like(a_dg)
            a_dgt[...] = jnp.zeros_like(a_dgt)

        fv = f_ref[...]
        gt = gt_ref[...]
        gv = g_ref[...]
        xv = x_ref[...] + gt * fv
        r = lax.rsqrt(jnp.mean(xv * xv, axis=-1, keepdims=True) + EPS)
        xh = xv * r
        e = xh * gv - t_ref[...]
        a_loss[...] += _fold8(e * e)
        dy = e * (1.0 / D)
        a_dg[...] += _fold8(dy * xh)
        t = dy * gv
        dx = r * (t - xh * jnp.mean(t * xh, axis=-1, keepdims=True))
        dx_ref[...] = dx
        a_dgt[...] += _fold8(dx * fv)
        df_ref[...] = (dx * gt).astype(bf16)

        @pl.when(i == n - 1)
        def _():
            tot = jnp.sum(jnp.sum(a_loss[...], axis=0, keepdims=True), axis=1, keepdims=True) * (0.5 / D)
            loss_ref[...] = jnp.broadcast_to(tot, (1, 128))
            dg_ref[...] = jnp.sum(a_dg[...], axis=0, keepdims=True)
            dgt_ref[...] = jnp.sum(a_dgt[...], axis=0, keepdims=True)

    return pl.pallas_call(
        body, name="final_fwd_bwd", grid=(n,),
        in_specs=[_tile_spec(tm, D), _tile_spec(tm, D), _row_spec(D), _row_spec(D), _tile_spec(tm, D)],
        out_specs=[_tile_spec(tm, D), _tile_spec(tm, D), _row_spec(128), _row_spec(D), _row_spec(D)],
        out_shape=[_sds((s, D), f32), _sds((s, D), bf16), _sds((1, 128), f32), _sds((1, D), f32), _sds((1, D), f32)],
        scratch_shapes=[pltpu.VMEM((8, D), f32)] * 3, compiler_params=_params(("arbitrary",)),
    )(x2, ffn, gt2, g, tgt)


def _norm_mod_bwd(name, dh, xin, g, sc, dres, tm, mix=None, gt=None):
    s = dh.shape[0]
    n = s // tm
    with_mix = mix is not None

    def body(*refs):
        if with_mix:
            dh_ref, x_ref, g_ref, sc_ref, dr_ref, m_ref, gt_ref, dx_ref, dm_ref, dsc_ref, dsh_ref, dg_ref, dgt_ref, a_sc, a_sh, a_g, a_gt = refs
        else:
            dh_ref, x_ref, g_ref, sc_ref, dr_ref, dx_ref, dsc_ref, dsh_ref, dg_ref, a_sc, a_sh, a_g = refs
        i = pl.program_id(0)

        @pl.when(i == 0)
        def _():
            a_sc[...] = jnp.zeros_like(a_sc)
            a_sh[...] = jnp.zeros_like(a_sh)
            a_g[...] = jnp.zeros_like(a_g)
            if with_mix:
                a_gt[...] = jnp.zeros_like(a_gt)

        dh = dh_ref[...]
        xv = x_ref[...]
        gv = g_ref[...]
        r = lax.rsqrt(jnp.mean(xv * xv, axis=-1, keepdims=True) + EPS)
        xh = xv * r
        a_sc[...] += _fold8(dh * xh * gv)
        a_sh[...] += _fold8(dh)
        dn = dh * (1.0 + sc_ref[...])
        a_g[...] += _fold8(dn * xh)
        t = dn * gv
        dx = dr_ref[...] + r * (t - xh * jnp.mean(t * xh, axis=-1, keepdims=True))
        dx_ref[...] = dx
        if with_mix:
            a_gt[...] += _fold8(dx * m_ref[...])
            dm_ref[...] = (dx * gt_ref[...]).astype(bf16)

        @pl.when(i == n - 1)
        def _():
            dsc_ref[...] = jnp.sum(a_sc[...], axis=0, keepdims=True)
            dsh_ref[...] = jnp.sum(a_sh[...], axis=0, keepdims=True)
            dg_ref[...] = jnp.sum(a_g[...], axis=0, keepdims=True)
            if with_mix:
                dgt_ref[...] = jnp.sum(a_gt[...], axis=0, keepdims=True)

    tile, row = _tile_spec(tm, D), _row_spec(D)
    if with_mix:
        ins, args = [tile, tile, row, row, tile, tile, row], (dh, xin, g, sc, dres, mix, gt)
        outs = [tile, tile, row, row, row, row]
        shapes = [_sds((s, D), f32), _sds((s, D), bf16)] + [_sds((1, D), f32)] * 4
        nacc = 4
    else:
        ins, args = [tile, tile, row, row, tile], (dh, xin, g, sc, dres)
        outs = [tile, row, row, row]
        shapes = [_sds((s, D), f32)] + [_sds((1, D), f32)] * 3
        nacc = 3
    return pl.pallas_call(
        body, name=name, grid=(n,), in_specs=ins, out_specs=outs, out_shape=shapes,
        scratch_shapes=[pltpu.VMEM((8, D), f32)] * nacc, compiler_params=_params(("arbitrary",)),
    )(*args)


def _mix_pre(att, y, proj2, g_att, g_ssd, tm):
    s = att.shape[0]

    def body(a_ref, y_ref, z_ref, ga_ref, gs_ref, o_ref):
        a = a_ref[...]
        ra = lax.rsqrt(jnp.mean(a * a, axis=-1, keepdims=True) + EPS)
        o_ref[:, 0:ATT_W] = (a * ra * ga_ref[...]).astype(bf16)
        z = z_ref[...]
        u = y_ref[...] * (z * _sigmoid(z))
        ru = lax.rsqrt(jnp.mean(u * u, axis=-1, keepdims=True) + EPS)
        o_ref[:, ATT_W:] = (u * ru * gs_ref[...]).astype(bf16)

    t = _tile_spec(tm, ATT_W)
    return pl.pallas_call(
        body, name="mix_pre", grid=(s // tm,), in_specs=[t, t, t, _row_spec(ATT_W), _row_spec(SSD_W)],
        out_specs=_tile_spec(tm, D), out_shape=_sds((s, D), bf16), compiler_params=_params(("parallel",)),
    )(att, y, proj2, g_att, g_ssd)


def _mix_pre_bwd(dmc, att, y, proj2, g_att, g_ssd, tm):
    s = att.shape[0]
    n = s // tm

    def body(da_ref, ds_ref, a_ref, y_ref, z_ref, ga_ref, gs_ref, datt_ref, dy_ref, dz_ref, dga_ref, dgs_ref, acc_a, acc_s):
        i = pl.program_id(0)

        @pl.when(i == 0)
        def _():
            acc_a[...] = jnp.zeros_like(acc_a)
            acc_s[...] = jnp.zeros_like(acc_s)

        a = a_ref[...]
        ra = lax.rsqrt(jnp.mean(a * a, axis=-1, keepdims=True) + EPS)
        ah = a * ra
        dan = da_ref[...]
        acc_a[...] += _fold8(dan * ah)
        t = dan * ga_ref[...]
        datt_ref[...] = (ra * (t - ah * jnp.mean(t * ah, axis=-1, keepdims=True))).astype(bf16)
        z = z_ref[...]
        yv = y_ref[...]
        sz = _sigmoid(z)
        sil = z * sz
        u = yv * sil
        ru = lax.rsqrt(jnp.mean(u * u, axis=-1, keepdims=True) + EPS)
        uh = u * ru
        dsn = ds_ref[...]
        acc_s[...] += _fold8(dsn * uh)
        t2 = dsn * gs_ref[...]
        du = ru * (t2 - uh * jnp.mean(t2 * uh, axis=-1, keepdims=True))
        dy_ref[...] = du * sil
        dz_ref[...] = (du * yv * (sz * (1.0 + z * (1.0 - sz)))).astype(bf16)

        @pl.when(i == n - 1)
        def _():
            dga_ref[...] = jnp.sum(acc_a[...], axis=0, keepdims=True)
            dgs_ref[...] = jnp.sum(acc_s[...], axis=0, keepdims=True)

    t = _tile_spec(tm, ATT_W)
    row = _row_spec(ATT_W)
    return pl.pallas_call(
        body, name="mix_pre_bwd", grid=(n,),
        in_specs=[_tile_spec(tm, ATT_W, 0), _tile_spec(tm, ATT_W, 1), t, t, t, row, row],
        out_specs=[t, t, t, row, row],
        out_shape=[_sds((s, ATT_W), bf16), _sds((s, SSD_W), f32), _sds((s, SSD_W), bf16), _sds((1, ATT_W), f32), _sds((1, SSD_W), f32)],
        scratch_shapes=[pltpu.VMEM((8, ATT_W), f32)] * 2, compiler_params=_params(("arbitrary",)),
    )(dmc, dmc, att, y, proj2, g_att, g_ssd)


ATT_GROUP = 2


def _pair_rows(qc):
    two = jnp.concatenate([qc, qc], axis=0)
    r = lax.broadcasted_iota(jnp.int32, (2 * CHUNK, 128), 0)
    l = lax.broadcasted_iota(jnp.int32, (2 * CHUNK, 128), 1)
    return jnp.where((r < CHUNK) == (l < HD), two, jnp.zeros_like(two))


def _pair_scores(wt, kb, bias, r0):
    sc = lax.dot_general(wt, kb, (((1,), (1,)), ((), ())), preferred_element_type=f32) * (HD ** -0.5) + bias
    kidx = lax.broadcasted_iota(jnp.int32, sc.shape, 1)
    return jnp.where(r0 + kidx >= PADK, sc, -jnp.inf)


def _softmax_lanes(sc):
    e = jnp.exp(sc - jnp.max(sc, axis=-1, keepdims=True))
    return e / jnp.sum(e, axis=-1, keepdims=True)


def _pair_diag(r):
    lane = lax.broadcasted_iota(jnp.int32, (CHUNK, 128), 1)
    return jnp.where(lane < HD, r[0:CHUNK], r[CHUNK:])


def _pad_keys(k_ref, kp, s):
    kp[0:PADK, :] = jnp.zeros((PADK, 128), bf16)
    kp[PADK:PADK + s, :] = k_ref[...]
    kp[PADK + s:, :] = jnp.zeros((CHUNK, 128), bf16)


def _attn_fwd(qkv, bias2):
    s = qkv.shape[0]
    nc = s // CHUNK
    npair = NH // 2

    def body(q_ref, k_ref, v_ref, b_ref, o_ref, kp, vp):
        _pad_keys(k_ref, kp, s)
        _pad_keys(v_ref, vp, s)

        def group(g, carry):
            r0s = [pl.multiple_of((g * ATT_GROUP + u) * CHUNK, CHUNK) for u in range(ATT_GROUP)]
            scs = [_pair_scores(_pair_rows(q_ref[pl.ds(r0, CHUNK), :]), kp[pl.ds(r0, BANDP), :], b_ref[...], r0) for r0 in r0s]
            ps = [_softmax_lanes(sc).astype(bf16) for sc in scs]
            for r0, p in zip(r0s, ps):
                o_ref[pl.ds(r0, CHUNK), :] = _pair_diag(jnp.dot(p, vp[pl.ds(r0, BANDP), :], preferred_element_type=f32))
            return carry

        lax.fori_loop(0, nc // ATT_GROUP, group, 0)

    return pl.pallas_call(
        body, name="attn_fwd", grid=(npair,),
        in_specs=[pl.BlockSpec((s, 128), lambda p: (0, p)), pl.BlockSpec((s, 128), lambda p: (0, npair + p)),
                  pl.BlockSpec((s, 128), lambda p: (0, 2 * npair + p)), pl.BlockSpec((None, 2 * CHUNK, BANDP), lambda p: (p, 0, 0))],
        out_specs=pl.BlockSpec((s, 128), lambda p: (0, p)), out_shape=_sds((s, ATT_W), f32),
        scratch_shapes=[pltpu.VMEM((PADK + s + CHUNK, 128), bf16)] * 2, compiler_params=_params(("parallel",)),
    )(qkv, qkv, qkv, bias2)


def _attn_bwd(qkv, datt, bias2, bias2t):
    s = qkv.shape[0]
    nc = s // CHUNK
    npair = NH // 2
    rows = PADK + s + CHUNK
    nt = (((1,), (1,)), ((), ()))

    def body(q_ref, k_ref, v_ref, do_ref, b_ref, bt_ref, dq_ref, dk_ref, dv_ref, g_ref, kp, vp, dkp, dvp):
        _pad_keys(k_ref, kp, s)
        _pad_keys(v_ref, vp, s)
        dkp[...] = jnp.zeros_like(dkp)
        dvp[...] = jnp.zeros_like(dvp)
        g_ref[...] = jnp.zeros_like(g_ref)

        def group(g, carry):
            r0s = [pl.multiple_of((g * ATT_GROUP + u) * CHUNK, CHUNK) for u in range(ATT_GROUP)]
            wts = [_pair_rows(q_ref[pl.ds(r0, CHUNK), :]) for r0 in r0s]
            dos = [_pair_rows(do_ref[pl.ds(r0, CHUNK), :]) for r0 in r0s]
            scs = [_pair_scores(wt, kp[pl.ds(r0, BANDP), :], b_ref[...], r0) for wt, r0 in zip(wts, r0s)]
            dps = [lax.dot_general(do, vp[pl.ds(r0, BANDP), :], nt, preferred_element_type=f32) for do, r0 in zip(dos, r0s)]
            scts, dpts = [], []
            for wt, do, r0 in zip(wts, dos, r0s):
                sct = lax.dot_general(kp[pl.ds(r0, BANDP), :], wt, nt, preferred_element_type=f32) * (HD ** -0.5) + bt_ref[...]
                kidx = lax.broadcasted_iota(jnp.int32, sct.shape, 0)
                scts.append(jnp.where(r0 + kidx >= PADK, sct, -jnp.inf))
                dpts.append(lax.dot_general(vp[pl.ds(r0, BANDP), :], do, nt, preferred_element_type=f32))
            for r0, sc, dp in zip(r0s, scs, dps):
                p = _softmax_lanes(sc)
                ds = p * (dp - jnp.sum(p * dp, axis=-1, keepdims=True))
                g_ref[...] += ds
                dq = jnp.dot(ds.astype(bf16), kp[pl.ds(r0, BANDP), :], preferred_element_type=f32)
                dq_ref[pl.ds(r0, CHUNK), :] = (_pair_diag(dq) * (HD ** -0.5)).astype(bf16)
            for r0, wt, do, sct, dpt in zip(r0s, wts, dos, scts, dpts):
                e = jnp.exp(sct - jnp.max(sct, axis=0, keepdims=True))
                pt = e / jnp.sum(e, axis=0, keepdims=True)
                dst = pt * (dpt - jnp.sum(pt * dpt, axis=0, keepdims=True))
                dkp[pl.ds(r0, BANDP), :] += jnp.dot(dst.astype(bf16), wt, preferred_element_type=f32) * (HD ** -0.5)
                dvp[pl.ds(r0, BANDP), :] += jnp.dot(pt.astype(bf16), do, preferred_element_type=f32)
            return carry

        lax.fori_loop(0, nc // ATT_GROUP, group, 0)
        dk_ref[...] = dkp[PADK:PADK + s, :].astype(bf16)
        dv_ref[...] = dvp[PADK:PADK + s, :].astype(bf16)

    col = lambda off: pl.BlockSpec((s, 128), lambda p: (0, off + p))
    return pl.pallas_call(
        body, name="attn_bwd", grid=(npair,),
        in_specs=[col(0), col(npair), col(2 * npair), col(0), pl.BlockSpec((None, 2 * CHUNK, BANDP), lambda p: (p, 0, 0)),
                  pl.BlockSpec((None, BANDP, 2 * CHUNK), lambda p: (p, 0, 0))],
        out_specs=[col(0), col(0), col(0), pl.BlockSpec((None, 2 * CHUNK, BANDP), lambda p: (p, 0, 0))],
        out_shape=[_sds((s, ATT_W), bf16)] * 3 + [_sds((npair, 2 * CHUNK, BANDP), f32)],
        scratch_shapes=[pltpu.VMEM((rows, 128), bf16)] * 2 + [pltpu.VMEM((rows, 128), f32)] * 2,
        compiler_params=_params(("parallel",)),
    )(qkv, qkv, qkv, datt, bias2, bias2t)


def _rel_tables():
    onehot = np.zeros((BANDP, N_REL), np.float32)
    for j in range(BAND + CHUNK - 1):
        o = j - (CHUNK - 1)
        onehot[j, int(np.clip(PADK - o, -(CHUNK - 1), REL_CLIP)) + CHUNK - 1] = 1.0
    return onehot, np.ascontiguousarray(np.eye(CHUNK, dtype=np.float32)[::-1])


def _expand_bias(rel):
    ext = jnp.concatenate([jnp.broadcast_to(rel[:, N_REL - 1:], (NH, N_REL - 1)), rel[:, ::-1],
                           jnp.zeros((NH, BANDP - BAND + 1), f32)], axis=1)
    band = jnp.stack([ext[:, CHUNK - 1 - q:CHUNK - 1 - q + BANDP] for q in range(CHUNK)], axis=1)
    band = jnp.where(np.arange(BANDP) < BAND, band, -jnp.inf)
    return band.reshape(NH // 2, 2 * CHUNK, BANDP)


def _rel_bias_grad(gband):
    def body(g_ref, m_ref, flip_ref, o_ref, d2):
        for h in range(NH):
            rev = jnp.dot(flip_ref[...], g_ref[h], precision=HIGHEST, preferred_element_type=f32)
            rolled = pltpu.roll(rev, 0, 1, stride=1, stride_axis=0)
            d2[h:h + 1, :] = jnp.sum(rolled, axis=0, keepdims=True)
        o_ref[...] = jnp.dot(d2[...], m_ref[...], precision=HIGHEST, preferred_element_type=f32)

    onehot, flip = _rel_tables()
    return pl.pallas_call(
        body, name="rel_bias_grad", out_shape=_sds((NH, N_REL), f32), scratch_shapes=[pltpu.VMEM((NH, BANDP), f32)],
    )(gband, jnp.asarray(onehot), jnp.asarray(flip))


XBC_BLK = 512
XBC_COL0 = SSD_W // XBC_BLK
DT_COL = (SSD_W + XBC) // 128


def _conv_taps(ext, w_ref, b_ref, tm):
    n = ext.shape[0]
    pre = w_ref[3:4, :] * ext + b_ref[...]
    for j in range(3):
        pre = pre + w_ref[j:j + 1, :] * pltpu.roll(ext, 3 - j, 0)
    return pre


def _ssd_conv(proj2, conv_w, conv_b, tm):
    s = proj2.shape[0]
    nb = XBC // XBC_BLK

    def body(x_ref, p_ref, w_ref, b_ref, o_ref):
        i = pl.program_id(1)
        prev = jnp.where(i > 0, p_ref[...], 0.0)
        ext = jnp.concatenate([prev, x_ref[...]], axis=0)
        pre = _conv_taps(ext, w_ref, b_ref, tm)[8:8 + tm]
        o_ref[...] = pre * _sigmoid(pre)

    return pl.pallas_call(
        body, name="ssd_conv", grid=(nb, s // tm),
        in_specs=[pl.BlockSpec((tm, XBC_BLK), lambda j, i: (i, XBC_COL0 + j)),
                  pl.BlockSpec((8, XBC_BLK), lambda j, i: (jnp.maximum(i * (tm // 8) - 1, 0), XBC_COL0 + j)),
                  pl.BlockSpec((4, XBC_BLK), lambda j, i: (0, j)), pl.BlockSpec((1, XBC_BLK), lambda j, i: (0, j))],
        out_specs=pl.BlockSpec((tm, XBC_BLK), lambda j, i: (i, j)), out_shape=_sds((s, XBC), f32),
        compiler_params=_params(("parallel", "parallel")),
    )(proj2, proj2, conv_w, conv_b)


def _ssd_conv_bwd(dxbc, proj2, conv_w, conv_b, tm):
    s = proj2.shape[0]
    nb = XBC // XBC_BLK
    n = s // tm
    last8 = s // 8 - 1

    def body(x_ref, xp_ref, xn_ref, d_ref, dn_ref, w_ref, b_ref, o_ref, dw_ref, db_ref):
        i = pl.program_id(1)

        @pl.when(i == 0)
        def _():
            dw_ref[...] = jnp.zeros_like(dw_ref)
            db_ref[...] = jnp.zeros_like(db_ref)

        prev = jnp.where(i > 0, xp_ref[...], 0.0)
        ext = jnp.concatenate([prev, x_ref[...], xn_ref[...]], axis=0)
        pre = _conv_taps(ext, w_ref, b_ref, tm)
        sg = _sigmoid(pre)
        dnext = jnp.where(i < n - 1, dn_ref[...], 0.0)
        dext = jnp.concatenate([jnp.zeros((8, XBC_BLK), f32), d_ref[...], dnext], axis=0)
        dpre = dext * (sg * (1.0 + pre * (1.0 - sg)))
        rows = tm + 16
        dx = w_ref[3:4, :] * dpre
        for j in range(3):
            dx = dx + w_ref[j:j + 1, :] * pltpu.roll(dpre, rows - (3 - j), 0)
        o_ref[...] = dx[8:8 + tm].astype(bf16)
        dcur = dpre[8:8 + tm]
        db_ref[...] += jnp.sum(dcur, axis=0, keepdims=True)
        dw_ref[3:4, :] += jnp.sum(dcur * ext[8:8 + tm], axis=0, keepdims=True)
        for j in range(3):
            dw_ref[j:j + 1, :] += jnp.sum(dcur * pltpu.roll(ext, 3 - j, 0)[8:8 + tm], axis=0, keepdims=True)

    xcol = lambda j: XBC_COL0 + j
    return pl.pallas_call(
        body, name="ssd_conv_bwd", grid=(nb, n),
        in_specs=[pl.BlockSpec((tm, XBC_BLK), lambda j, i: (i, xcol(j))),
                  pl.BlockSpec((8, XBC_BLK), lambda j, i: (jnp.maximum(i * (tm // 8) - 1, 0), xcol(j))),
                  pl.BlockSpec((8, XBC_BLK), lambda j, i: (jnp.minimum((i + 1) * (tm // 8), last8), xcol(j))),
                  pl.BlockSpec((tm, XBC_BLK), lambda j, i: (i, j)),
                  pl.BlockSpec((8, XBC_BLK), lambda j, i: (jnp.minimum((i + 1) * (tm // 8), last8), j)),
                  pl.BlockSpec((4, XBC_BLK), lambda j, i: (0, j)), pl.BlockSpec((1, XBC_BLK), lambda j, i: (0, j))],
        out_specs=[pl.BlockSpec((tm, XBC_BLK), lambda j, i: (i, j)), pl.BlockSpec((4, XBC_BLK), lambda j, i: (0, j)),
                   pl.BlockSpec((1, XBC_BLK), lambda j, i: (0, j))],
        out_shape=[_sds((s, XBC), bf16), _sds((4, XBC), f32), _sds((1, XBC), f32)],
        compiler_params=_params(("parallel", "arbitrary")),
    )(proj2, proj2, proj2, dxbc, dxbc, conv_w, conv_b)


def _ssd_consts():
    ex = np.zeros((128, SSD_W), np.float32)
    for h in range(NH):
        ex[h, h * HD:(h + 1) * HD] = 1.0
    sel = np.zeros((8, 128), np.float32)
    for h in range(NH):
        sel[h // 2, h] = 1.0
    par = np.zeros((128, 128), np.float32)
    for r in range(128):
        for h in range(NH):
            par[r, h] = 1.0 if (h % 2) == (r // 64) else 0.0
    ones_blk = np.zeros((128, 128), np.float32)
    for r in range(128):
        ones_blk[r, (r // 64) * 64:(r // 64) * 64 + 64] = 1.0
    return ex, np.ascontiguousarray(ex.T), sel, par, ones_blk


def _ssd_common(xbc_ref, dtr_ref, a_ref, dtb_ref, ex_ref, sel_ref, par_ref):
    xs = xbc_ref[:, 0:SSD_W]
    dt = _softplus(dtr_ref[...] + dtb_ref[...])
    adt = dt * a_ref[...]
    r_i = lax.broadcasted_iota(jnp.int32, (CHUNK, CHUNK), 0)
    c_i = lax.broadcasted_iota(jnp.int32, (CHUNK, CHUNK), 1)
    tril = (r_i >= c_i).astype(f32)
    cs = jnp.dot(tril, adt, precision=HIGHEST, preferred_element_type=f32)
    cs2 = jnp.concatenate([cs, cs], axis=0) * par_ref[...]
    cstp = lax.dot_general(sel_ref[...], cs2, (((1,), (1,)), ((), ())), precision=HIGHEST, preferred_element_type=f32)
    ex = ex_ref[...]
    dt_full = jnp.dot(dt, ex, precision=HIGHEST, preferred_element_type=f32)
    cs_full = jnp.dot(cs, ex, precision=HIGHEST, preferred_element_type=f32)
    return xs, dt, cs, cstp, dt_full, cs_full


def _pair_mask():
    l_i = lax.broadcasted_iota(jnp.int32, (CHUNK, 128), 0)
    lane = lax.broadcasted_iota(jnp.int32, (CHUNK, 128), 1)
    return l_i >= (lane % CHUNK), lane < HD


def _block_diag(xp, first):
    z = jnp.zeros_like(xp)
    return jnp.concatenate([jnp.where(first, xp, z), jnp.where(first, z, xp)], axis=0)


def _ssd_fwd(xbc, proj2, a_row, dtb_row, dsk_full):
    s = xbc.shape[0]
    nc = s // CHUNK
    ex, ext, sel, par, ones_blk = _ssd_consts()

    def body(xbc_ref, dtr_ref, a_ref, dtb_ref, dsk_ref, ex_ref, sel_ref, par_ref, y_ref, hs_ref, hst):
        @pl.when(pl.program_id(0) == 0)
        def _():
            hst[...] = jnp.zeros_like(hst)

        hs_ref[...] = hst[...]
        xs, dt, cs, cstp, dt_full, cs_full = _ssd_common(xbc_ref, dtr_ref, a_ref, dtb_ref, ex_ref, sel_ref, par_ref)
        cs_last = cs_full[CHUNK - 1:CHUNK, :]
        xdt = xs * dt_full
        causal, first = _pair_mask()
        for g in range(NG):
            gl = slice(g * GW, (g + 1) * GW)
            bg = xbc_ref[:, SSD_W + g * NSTATE:SSD_W + (g + 1) * NSTATE].astype(bf16)
            cg = xbc_ref[:, SSD_W + NG * NSTATE + g * NSTATE:SSD_W + NG * NSTATE + (g + 1) * NSTATE].astype(bf16)
            cb2 = lax.dot_general(cg, jnp.concatenate([bg, bg], axis=0), (((1,), (1,)), ((), ())), preferred_element_type=f32)
            hg = hst[g]
            y0 = jnp.dot(cg, hg.astype(bf16), preferred_element_type=f32)
            yoff = jnp.exp(cs_full[:, gl]) * y0
            for j in range(GW // 128):
                pair = g * (GW // 128) + j
                pl_ = slice(pair * 128, (pair + 1) * 128)
                seg = jnp.exp(jnp.where(causal, cs_full[:, pl_] - cstp[pair:pair + 1, :], -jnp.inf))
                m = (cb2 * seg).astype(bf16)
                yd = jnp.dot(m, _block_diag(xdt[:, pl_].astype(bf16), first), preferred_element_type=f32)
                y_ref[:, pl_] = yd + yoff[:, j * 128:(j + 1) * 128] + xs[:, pl_] * dsk_ref[:, pl_]
            xdec = (xdt[:, gl] * jnp.exp(cs_last[:, gl] - cs_full[:, gl])).astype(bf16)
            st = lax.dot_general(bg, xdec, (((0,), (0,)), ((), ())), preferred_element_type=f32)
            hst[g] = jnp.exp(cs_last[:, gl]) * hg + st

    const = lambda shape: pl.BlockSpec(shape, lambda c: tuple(0 for _ in shape))
    return pl.pallas_call(
        body, name="ssd_fwd", grid=(nc,),
        in_specs=[pl.BlockSpec((CHUNK, XBC), lambda c: (c, 0)), pl.BlockSpec((CHUNK, 128), lambda c: (c, DT_COL)),
                  const((1, 128)), const((1, 128)), const((1, SSD_W)), const((128, SSD_W)), const((8, 128)), const((128, 128))],
        out_specs=[pl.BlockSpec((CHUNK, SSD_W), lambda c: (c, 0)), pl.BlockSpec((None, NG, NSTATE, GW), lambda c: (c, 0, 0, 0))],
        out_shape=[_sds((s, SSD_W), f32), _sds((nc, NG, NSTATE, GW), f32)],
        scratch_shapes=[pltpu.VMEM((NG, NSTATE, GW), f32)], compiler_params=_params(("arbitrary",)),
    )(xbc, proj2, a_row, dtb_row, dsk_full, jnp.asarray(ex), jnp.asarray(sel), jnp.asarray(par))


def _ssd_bwd(xbc, proj2, dy, hsave, a_row, dtb_row, dsk_full):
    s = xbc.shape[0]
    nc = s // CHUNK
    ex, ext, sel, par, ones_blk = _ssd_consts()

    def body(xbc_ref, dtr_ref, dy_ref, hs_ref, a_ref, dtb_ref, dsk_ref, ex_ref, ext_ref, sel_ref, par_ref, ob_ref,
             dxbc_ref, ddtr_ref, dd_ref, da_ref, ddtb_ref, dh, a_dd, a_da, a_dtb, dcs_lane, dcs_b, dxdt):
        step = pl.program_id(0)

        @pl.when(step == 0)
        def _():
            dh[...] = jnp.zeros_like(dh)
            a_dd[...] = jnp.zeros_like(a_dd)
            a_da[...] = jnp.zeros_like(a_da)
            a_dtb[...] = jnp.zeros_like(a_dtb)

        xs, dt, cs, cstp, dt_full, cs_full = _ssd_common(xbc_ref, dtr_ref, a_ref, dtb_ref, ex_ref, sel_ref, par_ref)
        cs_last = cs_full[CHUNK - 1:CHUNK, :]
        xdt = xs * dt_full
        dyv = dy_ref[...]
        a_dd[...] += _fold8(dyv * xs)
        causal, first = _pair_mask()
        ones_l = jnp.ones((CHUNK, 128), f32)
        for g in range(NG):
            gl = slice(g * GW, (g + 1) * GW)
            bcol = slice(SSD_W + g * NSTATE, SSD_W + (g + 1) * NSTATE)
            ccol = slice(SSD_W + NG * NSTATE + g * NSTATE, SSD_W + NG * NSTATE + (g + 1) * NSTATE)
            bg = xbc_ref[:, bcol].astype(bf16)
            cg = xbc_ref[:, ccol].astype(bf16)
            bg2 = jnp.concatenate([bg, bg], axis=0)
            cb2 = lax.dot_general(cg, bg2, (((1,), (1,)), ((), ())), preferred_element_type=f32)
            hg = hs_ref[g]
            hgb = hg.astype(bf16)
            dhg = dh[g]
            dhgb = dhg.astype(bf16)
            eg = jnp.exp(cs_full[:, gl])
            dec = jnp.exp(cs_last[:, gl] - cs_full[:, gl])
            gam = jnp.exp(cs_last[:, gl])
            dyg = dyv[:, gl]
            xdt_g = xdt[:, gl]
            y0 = jnp.dot(cg, hgb, preferred_element_type=f32)
            dy0 = (eg * dyg).astype(bf16)
            dcm = lax.dot_general(dy0, hgb, (((1,), (1,)), ((), ())), preferred_element_type=f32)
            dh_prev = gam * dhg + lax.dot_general(cg, dy0, (((0,), (0,)), ((), ())), preferred_element_type=f32)
            dgam = jnp.sum(dhg * hg, axis=0, keepdims=True) * gam
            dxdec = jnp.dot(bg, dhgb, preferred_element_type=f32)
            dbm = lax.dot_general((xdt_g * dec).astype(bf16), dhgb, (((1,), (1,)), ((), ())), preferred_element_type=f32)
            t = dxdec * xdt_g * dec
            dcs_lane[:, gl] = dyg * eg * y0 - t
            dcs_lane[CHUNK - 1:CHUNK, gl] += jnp.sum(t, axis=0, keepdims=True) + dgam
            dxdt[:, gl] = dxdec * dec
            dcb2 = jnp.zeros((CHUNK, 128), f32)
            for j in range(GW // 128):
                pair = g * (GW // 128) + j
                pl_ = slice(pair * 128, (pair + 1) * 128)
                seg = jnp.exp(jnp.where(causal, cs_full[:, pl_] - cstp[pair:pair + 1, :], -jnp.inf))
                m = cb2 * seg
                mb = m.astype(bf16)
                rhs = _block_diag(xdt[:, pl_].astype(bf16), first)
                dyp = dyv[:, pl_].astype(bf16)
                dm = lax.dot_general(dyp, rhs, (((1,), (1,)), ((), ())), preferred_element_type=f32)
                tt = lax.dot_general(mb, dyp, (((0,), (0,)), ((), ())), preferred_element_type=f32)
                dxdt[:, pl_] += jnp.where(first, tt[0:CHUNK], tt[CHUNK:])
                dcb2 = dcb2 + dm * seg
                w = dm * m
                rsum = jnp.dot(w, ob_ref[...], precision=HIGHEST, preferred_element_type=f32)
                t2 = lax.dot_general(w, ones_l, (((0,), (0,)), ((), ())), precision=HIGHEST, preferred_element_type=f32)
                dcs_b[:, pl_] = rsum - jnp.where(first, t2[0:CHUNK], t2[CHUNK:])
            dcb2b = dcb2.astype(bf16)
            dcm = dcm + jnp.dot(dcb2b, bg2, preferred_element_type=f32)
            t3 = lax.dot_general(dcb2b, cg, (((0,), (0,)), ((), ())), preferred_element_type=f32)
            dxbc_ref[:, bcol] = dbm + t3[0:CHUNK] + t3[CHUNK:]
            dxbc_ref[:, ccol] = dcm
            dh[g] = dh_prev
        dcs = jnp.dot(dcs_lane[...] + dcs_b[...] * (1.0 / HD), ext_ref[...], precision=HIGHEST, preferred_element_type=f32)
        r_i = lax.broadcasted_iota(jnp.int32, (CHUNK, CHUNK), 0)
        c_i = lax.broadcasted_iota(jnp.int32, (CHUNK, CHUNK), 1)
        triu = (r_i <= c_i).astype(f32)
        da_ = jnp.dot(triu, dcs, precision=HIGHEST, preferred_element_type=f32)
        dxdtv = dxdt[...]
        ddt = da_ * a_ref[...] + jnp.dot(dxdtv * xs, ext_ref[...], precision=HIGHEST, preferred_element_type=f32)
        a_da[...] += _fold8(da_ * dt)
        dxbc_ref[:, 0:SSD_W] = dyv * dsk_ref[...] + dxdtv * dt_full
        ddtr = ddt * _sigmoid(dtr_ref[...] + dtb_ref[...])
        ddtr_ref[...] = ddtr
        a_dtb[...] += _fold8(ddtr)

        @pl.when(step == nc - 1)
        def _():
            dd_ref[...] = jnp.sum(jnp.dot(a_dd[...], ext_ref[...], precision=HIGHEST, preferred_element_type=f32), axis=0, keepdims=True)
            da_ref[...] = jnp.sum(a_da[...], axis=0, keepdims=True)
            ddtb_ref[...] = jnp.sum(a_dtb[...], axis=0, keepdims=True)

    rev = lambda c: nc - 1 - c
    const = lambda shape: pl.BlockSpec(shape, lambda c: tuple(0 for _ in shape))
    return pl.pallas_call(
        body, name="ssd_bwd", grid=(nc,),
        in_specs=[pl.BlockSpec((CHUNK, XBC), lambda c: (rev(c), 0)), pl.BlockSpec((CHUNK, 128), lambda c: (rev(c), DT_COL)),
                  pl.BlockSpec((CHUNK, SSD_W), lambda c: (rev(c), 0)), pl.BlockSpec((None, NG, NSTATE, GW), lambda c: (rev(c), 0, 0, 0)),
                  const((1, 128)), const((1, 128)), const((1, SSD_W)), const((128, SSD_W)), const((SSD_W, 128)),
                  const((8, 128)), const((128, 128)), const((128, 128))],
        out_specs=[pl.BlockSpec((CHUNK, XBC), lambda c: (rev(c), 0)), pl.BlockSpec((CHUNK, 128), lambda c: (rev(c), 0)),
                   const((1, 128)), const((1, 128)), const((1, 128))],
        out_shape=[_sds((s, XBC), f32), _sds((s, 128), f32), _sds((1, 128), f32), _sds((1, 128), f32), _sds((1, 128), f32)],
        scratch_shapes=[pltpu.VMEM((NG, NSTATE, GW), f32), pltpu.VMEM((8, SSD_W), f32), pltpu.VMEM((8, 128), f32), pltpu.VMEM((8, 128), f32),
                        pltpu.VMEM((CHUNK, SSD_W), f32), pltpu.VMEM((CHUNK, SSD_W), f32), pltpu.VMEM((CHUNK, SSD_W), f32)],
        compiler_params=_params(("arbitrary",)),
    )(xbc, proj2, dy, hsave, a_row, dtb_row, dsk_full, jnp.asarray(ex), jnp.asarray(ext), jnp.asarray(sel), jnp.asarray(par),
      jnp.asarray(ones_blk))


def _local_step(x, tgt, mods, g_mix, win, rel, conv_w, conv_b, dt_bias, a_log, d_skip, g_att, g_ssd, wout, g_ffn,
                wg4, wu4, wd4, g_final):
    s = x.shape[0]
    tm_e = 256 if s % 256 == 0 else s
    tm_m = 512 if s % 512 == 0 else s
    tm_l = 1024 if s % 1024 == 0 else s
    tk = 512 if s % 512 == 0 else s
    sh1, sc1, gt1, sh2, sc2, gt2 = [mods[:, i * D:(i + 1) * D] for i in range(6)]

    h1b = _norm_mod("norm_mod_1", x, g_mix, sc1, sh1, tm_e)
    qkv = _mm_nn_fullk("proj_qkv", h1b, win[:, :IN_A], tm_m, 768, bf16)
    proj2 = _mm_nn_fullk("proj_zxbcdt", h1b, win[:, IN_A:], tm_m, 896, f32)
    bias = _expand_bias(rel)
    att = _attn_fwd(qkv, bias)
    xbc = _ssd_conv(proj2, conv_w, conv_b, tm_e)
    a_row = jnp.pad(-jnp.exp(a_log), ((0, 0), (0, 128 - NH)))
    dtb_row = jnp.pad(dt_bias, ((0, 0), (0, 128 - NH)))
    dsk_full = jnp.repeat(d_skip, HD, axis=1)
    y, hsave = _ssd_fwd(xbc, proj2, a_row, dtb_row, dsk_full)
    mixcat = _mix_pre(att, y, proj2, g_att, g_ssd, tm_e)
    mix = _mm_nn_fullk("proj_out", mixcat, wout, tm_m, 1024, f32)
    x2, h2b = _resid_norm_mod(x, gt1, mix, g_ffn, sc2, sh2, tm_e)
    gate, up, act = _ffn_up(h2b, wg4, wu4, tm_m)
    ffn = _ffn_down(act, wd4, tm_l)

    dx3, dffn, loss, dg_final, dgt2 = _final_fwd_bwd(x2, ffn, gt2, g_final, tgt, tm_e)
    gwd4 = _grad_wdown4(act, dffn, 1024, tk)
    dgate, dup = _ffn_dact(dffn, wd4, gate, up, tm_m)
    gwg4 = _grad_cols4("grad_w_gate", h2b, dgate, 1024, tk)
    gwu4 = _grad_cols4("grad_w_up", h2b, dup, 1024, tk)
    dh2 = _ffn_dh(dgate, dup, wg4, wu4, tm_m)
    dx2, dmix, dsc2, dsh2, dg_ffn, dgt1 = _norm_mod_bwd("norm_mod_bwd_2", dh2, x2, g_ffn, sc2, dx3, tm_e, mix=mix, gt=gt1)
    gwout = _mm_tn("grad_w_out", mixcat, dmix, 1024, 1024, tk, bf16)
    dmc = _mm_nt("dmixcat", dmix, wout, tm_m, 1024, D, f32)
    datt, dy, dz, dg_att, dg_ssd = _mix_pre_bwd(dmc, att, y, proj2, g_att, g_ssd, tm_e)
    dq, dk, dv, gband = _attn_bwd(qkv, datt, bias, jnp.transpose(bias, (0, 2, 1)))
    drel = _rel_bias_grad(gband.reshape(NH, CHUNK, BANDP))
    dxbc, ddtr, dd_row, da_row, ddtb_row = _ssd_bwd(xbc, proj2, dy, hsave, a_row, dtb_row, dsk_full)
    dxbc_raw, dconv_w, dconv_b = _ssd_conv_bwd(dxbc, proj2, conv_w, conv_b, tm_e)
    dproj = jnp.concatenate([dq, dk, dv, dz, dxbc_raw, ddtr.astype(bf16)], axis=1)
    gwin = _mm_tn("grad_w_in", h1b, dproj, 1024, 1152, tk, bf16)
    dh1 = _mm_nt("dh1", dproj, win, tm_m, D, 1920, f32)
    grad_x, dsc1, dsh1, dg_mix = _norm_mod_bwd("norm_mod_bwd_1", dh1, x, g_mix, sc1, dx2, tm_e)

    dmods = jnp.concatenate([dsh1, dsc1, dgt1, dsh2, dsc2, dgt2], axis=1)
    dd_skip = dd_row[:, :NH]
    da_log = da_row[:, :NH] * a_row[:, :NH]
    small = dict(g_mix=dg_mix, conv_b=dconv_b, dt_bias=ddtb_row[:, :NH], a_log=da_log, d_skip=dd_skip, g_att_out=dg_att,
                 g_ssd_out=dg_ssd, g_ffn=dg_ffn, g_final=dg_final, rel_bias=drel, conv_w=dconv_w)
    big = dict(w_in=gwin, w_out=gwout, w_gate=gwg4, w_up=gwu4, w_down=gwd4)
    return loss[0, 0], grad_x, dmods, small, big


HBM = pl.BlockSpec(memory_space=pl.ANY)
VMEM = pl.BlockSpec(memory_space=pltpu.VMEM)


def _place():
    x, y, c = lax.axis_index("x"), lax.axis_index("y"), lax.axis_index("c")
    chips = [(1 - x, y), (x, 1 - y), (1 - x, 1 - y)]
    return x, y, c, chips


def _allgather8(name, payload):
    r = payload.shape[0]

    def body(x_ref, out_ref, send_sems, recv_sems, local_sem):
        x, y, c, chips = _place()
        me, sibling = (x, y, c), (x, y, 1 - c)

        def slot(px, py, pc):
            return out_ref.at[4 * px + 2 * py + pc]

        def copy(k, block, to, src=None):
            return pltpu.make_async_remote_copy(
                src_ref=slot(*block) if src is None else src, dst_ref=slot(*block),
                send_sem=send_sems.at[k], recv_sem=recv_sems.at[k], device_id=to, device_id_type=MESH)

        mine = pltpu.make_async_copy(x_ref, slot(*me), local_sem)
        mine.start()
        first = [copy(0, me, sibling, src=x_ref)]
        first += [copy(1 + j, me, (*chip, c), src=x_ref) for j, chip in enumerate(chips)]
        for cp in first:
            cp.start()
        passed = [copy(4 + j, (*chip, c), sibling) for j, chip in enumerate(chips)]
        for j, chip in enumerate(chips):
            copy(1 + j, (*chip, c), me).wait_recv()
            passed[j].start()
        copy(0, sibling, me).wait_recv()
        for j, chip in enumerate(chips):
            copy(4 + j, (*chip, 1 - c), me).wait_recv()
        for cp in first + passed:
            cp.wait_send()
        mine.wait()

    return pl.pallas_call(
        body, name=name, out_shape=_sds((N_DEV, r, 128), f32), in_specs=[VMEM], out_specs=VMEM,
        scratch_shapes=[pltpu.SemaphoreType.DMA((7,)), pltpu.SemaphoreType.DMA((7,)), pltpu.SemaphoreType.DMA],
    )(payload)


def _sum8(g):
    r = g.shape[1]

    def body(g_ref, o_ref):
        acc = g_ref[0]
        for i in range(1, N_DEV):
            acc = acc + g_ref[i]
        o_ref[...] = acc

    return pl.pallas_call(body, name="sum8", out_shape=_sds((r, 128), f32))(g)


def _gather_weights(shards):
    nw = len(shards)

    def body(*refs):
        ins, outs = refs[:nw], refs[nw:2 * nw]
        st_a, st_b, st_c = refs[2 * nw:3 * nw], refs[3 * nw:4 * nw], refs[4 * nw:5 * nw]
        send_sems, recv_sems, load_sems, store_sems = refs[5 * nw:]
        x, y, c, chips = _place()
        k = 2 * x + y
        sibling = (x, y, 1 - c)

        def half(w, kk, hh, sem, to, src):
            return pltpu.make_async_remote_copy(src_ref=src, dst_ref=outs[w].at[kk, hh], send_sem=send_sems.at[w, sem],
                                                recv_sem=recv_sems.at[w, sem], device_id=to, device_id_type=MESH)

        ld_a = [pltpu.make_async_copy(ins[w].at[c], st_a[w], load_sems.at[w, 0]) for w in range(nw)]
        ld_b = [pltpu.make_async_copy(ins[w].at[1 - c], st_b[w], load_sems.at[w, 1]) for w in range(nw)]
        for cp in ld_a + ld_b:
            cp.start()
        sends, stores = [], []
        for w in range(nw):
            ld_a[w].wait()
            for j, chip in enumerate(chips):
                sends.append(half(w, k, c, j, (*chip, c), st_a[w]))
                sends[-1].start()
            stores.append(pltpu.make_async_copy(st_a[w], outs[w].at[k, c], store_sems.at[w, 0]))
            stores[-1].start()
        st_own = []
        for w in range(nw):
            ld_b[w].wait()
            st_own.append(pltpu.make_async_copy(st_b[w], outs[w].at[k, 1 - c], store_sems.at[w, 1]))
            st_own[-1].start()
        for cp in st_own:
            cp.wait()
        fwds = {}
        for j, (px, py) in enumerate(chips):
            kq = 2 * px + py
            for w in range(nw):
                slot = st_b[w] if j % 2 == 0 else st_c[w]
                half(w, kq, c, j, (x, y, c), slot).wait_recv()
                if j == 2:
                    fwds[w, 0].wait_send()
                ld = pltpu.make_async_copy(outs[w].at[kq, c], slot, load_sems.at[w, 2 + j])
                ld.start()
                ld.wait()
                fwds[w, j] = half(w, kq, c, 3 + j, sibling, slot)
                fwds[w, j].start()
        for j, (px, py) in enumerate(chips):
            for w in range(nw):
                half(w, 2 * px + py, 1 - c, 3 + j, (x, y, c), st_c[w]).wait_recv()
        for cp in sends:
            cp.wait_send()
        for w in range(nw):
            fwds[w, 1].wait_send()
            fwds[w, 2].wait_send()
        for cp in stores:
            cp.wait()

    stage = [pltpu.VMEM(s.shape[1:], bf16) for s in shards]
    return pl.pallas_call(
        body, name="gather_weights", out_shape=[_sds((NSH,) + s.shape, bf16) for s in shards],
        in_specs=[HBM] * nw, out_specs=[HBM] * nw,
        scratch_shapes=stage * 3 + [pltpu.SemaphoreType.DMA((nw, 6)), pltpu.SemaphoreType.DMA((nw, 6)), pltpu.SemaphoreType.DMA((nw, 5)),
                                    pltpu.SemaphoreType.DMA((nw, 2))],
        compiler_params=pltpu.CompilerParams(vmem_limit_bytes=VMEM_LIMIT),
    )(*shards)


def _rs_pair_exchange(grads):
    nw = len(grads)

    def body(*refs):
        ins, got, stage = refs[:nw], refs[nw:2 * nw], refs[2 * nw:3 * nw]
        send_sems, recv_sems, load_sems = refs[3 * nw:]
        x, y, c, _ = _place()

        def load(w, kk):
            return pltpu.make_async_copy(ins[w].at[kk, 1 - c], stage[w].at[kk % 2], load_sems.at[w, kk])

        def send(w, kk):
            return pltpu.make_async_remote_copy(src_ref=stage[w].at[kk % 2], dst_ref=got[w].at[kk], send_sem=send_sems.at[w, kk],
                                                recv_sem=recv_sems.at[w, kk], device_id=(x, y, 1 - c), device_id_type=MESH)

        for kk in range(2):
            for w in range(nw):
                load(w, kk).start()
        for kk in range(NSH):
            for w in range(nw):
                load(w, kk).wait()
                send(w, kk).start()
            if kk + 2 < NSH:
                for w in range(nw):
                    send(w, kk).wait_send()
                    load(w, kk + 2).start()
        for kk in range(NSH - 2, NSH):
            for w in range(nw):
                send(w, kk).wait_send()
        for kk in range(NSH):
            for w in range(nw):
                send(w, kk).wait_recv()

    return pl.pallas_call(
        body, name="rs_pair_exchange", out_shape=[_sds((NSH,) + g.shape[2:], bf16) for g in grads], in_specs=[HBM] * nw, out_specs=[HBM] * nw,
        scratch_shapes=[pltpu.VMEM((2,) + g.shape[2:], bf16) for g in grads]
        + [pltpu.SemaphoreType.DMA((nw, NSH)), pltpu.SemaphoreType.DMA((nw, NSH)), pltpu.SemaphoreType.DMA((nw, NSH))],
        compiler_params=pltpu.CompilerParams(vmem_limit_bytes=VMEM_LIMIT),
    )(*grads)


def _rs_chip_exchange(sums):
    nw = len(sums)

    def body(*refs):
        ins, outs, stage = refs[:nw], refs[nw:2 * nw], refs[2 * nw:3 * nw]
        send_sems, recv_sems, load_sems, local_sems = refs[3 * nw:]
        x, y, c, chips = _place()
        k = 2 * x + y
        slabs = [2 * px + py for px, py in chips] + [k]

        def load(w, j):
            return pltpu.make_async_copy(ins[w].at[slabs[j]], stage[w].at[slabs[j]], load_sems.at[w, j])

        for j in range(NSH):
            for w in range(nw):
                load(w, j).start()
        cps = []
        for j, (px, py) in enumerate(chips):
            for w in range(nw):
                load(w, j).wait()
                cps.append(pltpu.make_async_remote_copy(src_ref=stage[w].at[slabs[j]], dst_ref=outs[w].at[k], send_sem=send_sems.at[w, j],
                                                        recv_sem=recv_sems.at[w, j], device_id=(px, py, c), device_id_type=MESH))
                cps[-1].start()
        local = []
        for w in range(nw):
            load(w, NSH - 1).wait()
            local.append(pltpu.make_async_copy(stage[w].at[k], outs[w].at[k], local_sems.at[w]))
            local[-1].start()
        for w in range(nw):
            for j, (px, py) in enumerate(chips):
                pltpu.make_async_remote_copy(src_ref=stage[w].at[k], dst_ref=outs[w].at[2 * px + py], send_sem=send_sems.at[w, j],
                                             recv_sem=recv_sems.at[w, j], device_id=(px, py, c), device_id_type=MESH).wait_recv()
        for cp in cps:
            cp.wait_send()
        for cp in local:
            cp.wait()

    return pl.pallas_call(
        body, name="rs_chip_exchange", out_shape=[_sds(s.shape, bf16) for s in sums], in_specs=[HBM] * nw, out_specs=[HBM] * nw,
        scratch_shapes=[pltpu.VMEM(s.shape, bf16) for s in sums]
        + [pltpu.SemaphoreType.DMA((nw, 3)), pltpu.SemaphoreType.DMA((nw, 3)), pltpu.SemaphoreType.DMA((nw, NSH)), pltpu.SemaphoreType.DMA((nw,))],
        compiler_params=pltpu.CompilerParams(vmem_limit_bytes=VMEM_LIMIT),
    )(*sums)


def _rs_pair_gather(halves):
    nw = len(halves)

    def body(*refs):
        ins, outs, stage = refs[:nw], refs[nw:2 * nw], refs[2 * nw:3 * nw]
        send_sems, recv_sems, local_sems, stage_sems = refs[3 * nw:]
        x, y, c, _ = _place()
        loads = [pltpu.make_async_copy(ins[w], stage[w], stage_sems.at[w]) for w in range(nw)]
        for cp in loads:
            cp.start()
        local, cps = [], []
        for w in range(nw):
            loads[w].wait()
            local.append(pltpu.make_async_copy(stage[w], outs[w].at[c], local_sems.at[w]))
            cps.append(pltpu.make_async_remote_copy(src_ref=stage[w], dst_ref=outs[w].at[c], send_sem=send_sems.at[w],
                                                    recv_sem=recv_sems.at[w], device_id=(x, y, 1 - c), device_id_type=MESH))
            local[w].start()
            cps[w].start()
        for w in range(nw):
            pltpu.make_async_remote_copy(src_ref=stage[w], dst_ref=outs[w].at[1 - c], send_sem=send_sems.at[w], recv_sem=recv_sems.at[w],
                                         device_id=(x, y, 1 - c), device_id_type=MESH).wait_recv()
        for cp in cps:
            cp.wait_send()
        for cp in local:
            cp.wait()

    return pl.pallas_call(
        body, name="rs_pair_gather", out_shape=[_sds((2,) + h.shape, f32) for h in halves], in_specs=[HBM] * nw, out_specs=[HBM] * nw,
        scratch_shapes=[pltpu.VMEM(h.shape, f32) for h in halves]
        + [pltpu.SemaphoreType.DMA((nw,)), pltpu.SemaphoreType.DMA((nw,)), pltpu.SemaphoreType.DMA((nw,)), pltpu.SemaphoreType.DMA((nw,))],
        compiler_params=pltpu.CompilerParams(vmem_limit_bytes=VMEM_LIMIT),
    )(*halves)


def _row_tile(r, c, nbuf):
    budget = 24 * 1024 * 1024 // (2 * nbuf * 4 * c)
    t = 8
    while t * 2 <= budget and r % (t * 2) == 0:
        t *= 2
    return t


def _cast_bf16(name, a):
    r, c = a.shape
    tr = _row_tile(r, c, 2)

    def body(a_ref, o_ref):
        o_ref[...] = a_ref[...].astype(bf16)

    spec = pl.BlockSpec((tr, c), lambda i: (i, 0))
    return pl.pallas_call(body, name=name, grid=(r // tr,), in_specs=[spec], out_specs=spec, out_shape=_sds((r, c), bf16),
                          compiler_params=_params(("parallel",)))(a)


def _pair_sum(name, core, grads, got):
    _, _, rh, c = grads.shape
    tr = _row_tile(rh, c, 2)

    def body(c_ref, a_ref, b_ref, o_ref):
        o_ref[...] = (a_ref[...].astype(f32) + b_ref[...].astype(f32)).astype(bf16)

    spec = pl.BlockSpec((None, tr, c), lambda k, i, c_ref: (k, i, 0))
    return pl.pallas_call(
        body, name=name, out_shape=_sds((NSH, rh, c), bf16),
        grid_spec=pltpu.PrefetchScalarGridSpec(
            num_scalar_prefetch=1, grid=(NSH, rh // tr),
            in_specs=[pl.BlockSpec((None, None, tr, c), lambda k, i, c_ref: (k, c_ref[0], i, 0)), spec], out_specs=spec),
        compiler_params=_params(("parallel", "parallel")))(core, grads, got)


def _chip_sum(name, parts):
    _, rh, c = parts.shape
    tr = _row_tile(rh, c, 3)

    def body(p_ref, o_ref):
        acc = p_ref[0].astype(f32)
        for k in range(1, NSH):
            acc = acc + p_ref[k].astype(f32)
        o_ref[...] = acc

    return pl.pallas_call(body, name=name, grid=(rh // tr,), in_specs=[pl.BlockSpec((NSH, tr, c), lambda i: (0, i, 0))],
                          out_specs=pl.BlockSpec((tr, c), lambda i: (i, 0)), out_shape=_sds((rh, c), f32),
                          compiler_params=_params(("parallel",)))(parts)


def _mods_part(cond16, w_ada, b_part):
    n = w_ada.shape[1]
    tn = 512

    def body(c_ref, w_ref, b_ref, o_ref):
        cv = c_ref[...]
        o_ref[...] = _dot(cv * _sigmoid(cv), w_ref[...]) + b_ref[...]

    return pl.pallas_call(
        body, name="mods_part", grid=(n // tn,),
        in_specs=[pl.BlockSpec((16, D), lambda j: (0, 0)), pl.BlockSpec((D, tn), lambda j: (0, j)), pl.BlockSpec((1, tn), lambda j: (0, j))],
        out_specs=pl.BlockSpec((16, tn), lambda j: (0, j)), out_shape=_sds((16, n), f32), compiler_params=_params(("parallel",)),
    )(cond16, w_ada, b_part)


def _grad_w_ada(cond16, dm16):
    n = dm16.shape[1]
    tr = 256

    def body(c_ref, d_ref, o_ref):
        cv = c_ref[...]
        o_ref[...] = _dot(cv * _sigmoid(cv), d_ref[...], ta=True)

    return pl.pallas_call(
        body, name="grad_w_ada", grid=(D // tr,),
        in_specs=[pl.BlockSpec((16, tr), lambda i: (0, i)), pl.BlockSpec((16, n), lambda i: (0, 0))],
        out_specs=pl.BlockSpec((tr, n), lambda i: (i, 0)), out_shape=_sds((D, n), f32), compiler_params=_params(("parallel",)),
    )(cond16, dm16)


def _adamw(name, w, g, m, v):
    r, c = w.shape
    tr = _row_tile(r, c, 7)

    def body(w_ref, g_ref, m_ref, v_ref, d_ref, nm_ref, nv_ref):
        gv = g_ref[...]
        nm = ADAM_B1 * m_ref[...] + (1.0 - ADAM_B1) * gv
        nv = ADAM_B2 * v_ref[...] + (1.0 - ADAM_B2) * (gv * gv)
        nm_ref[...] = nm
        nv_ref[...] = nv
        m_hat = nm / (1.0 - ADAM_B1 ** ADAM_STEP)
        v_hat = nv / (1.0 - ADAM_B2 ** ADAM_STEP)
        d_ref[...] = -ADAM_LR * (m_hat / (jnp.sqrt(v_hat) + ADAM_EPS) + ADAM_WD * w_ref[...])

    spec = pl.BlockSpec((tr, c), lambda i: (i, 0))
    return pl.pallas_call(body, name=name, grid=(r // tr,), in_specs=[spec] * 4, out_specs=[spec] * 3, out_shape=[_sds((r, c), f32)] * 3,
                          compiler_params=_params(("parallel",)))(w, g, m, v)


def _pack(parts, rows):
    flat = []
    for p in parts:
        p = p.reshape(-1)
        flat.append(jnp.pad(p, (0, (-p.shape[0]) % 128)))
    v = jnp.concatenate(flat)
    return jnp.pad(v, (0, rows * 128 - v.shape[0])).reshape(rows, 128)


def _unpack(packed, sizes):
    lead = packed.shape[:-2]
    flat = packed.reshape(lead + (-1,))
    out, off = [], 0
    for n in sizes:
        out.append(flat[..., off:off + n])
        off += n + (-n) % 128
    return out


BIG = ("w_in", "w_out", "w_gate", "w_up", "w_down")
SMALL = ("b_ada", "g_mix", "conv_b", "dt_bias", "a_log", "d_skip", "g_att_out", "g_ssd_out", "g_ffn", "g_final", "rel_bias", "conv_w")
ORDER = ("w_ada", "b_ada", "g_mix", "w_in", "rel_bias", "conv_w", "conv_b", "dt_bias", "a_log", "d_skip", "g_att_out", "g_ssd_out",
         "w_out", "g_ffn", "w_gate", "w_up", "w_down", "g_final")
REL_SH = N_REL // NSH
CONVW_SH = XBC // NSH
ADA_SH = 6 * D // NSH


def kernel(x, c, w_ada, b_ada, g_mix, w_in, rel_bias, conv_w, conv_b, dt_bias, a_log, d_skip, g_att_out, g_ssd_out, w_out, g_ffn, w_gate, w_up, w_down, g_final, loss_target, m_w_ada, m_b_ada, m_g_mix, m_w_in, m_rel_bias, m_conv_w, m_conv_b, m_dt_bias, m_a_log, m_d_skip, m_g_att_out, m_g_ssd_out, m_w_out, m_g_ffn, m_w_gate, m_w_up, m_w_down, m_g_final, v_w_ada, v_b_ada, v_g_mix, v_w_in, v_rel_bias, v_conv_w, v_conv_b, v_dt_bias, v_a_log, v_d_skip, v_g_att_out, v_g_ssd_out, v_w_out, v_g_ffn, v_w_gate, v_w_up, v_w_down, v_g_final):
    args = dict(locals())
    w = {n: args[n] for n in ORDER}
    m = {n: args["m_" + n] for n in ORDER}
    v = {n: args["v_" + n] for n in ORDER}
    ix, iy, ic = lax.axis_index("x"), lax.axis_index("y"), lax.axis_index("c")
    chip = 2 * ix + iy
    dev = 2 * chip + ic
    s = x.shape[1]

    shards = [
        _cast_bf16("cast_w_in", jnp.pad(w_in[0], ((0, 0), (0, IN_SHP - IN_SH)))).reshape(2, D // 2, IN_SHP),
        _cast_bf16("cast_w_out", w_out[0]).reshape(2, D // NSH // 2, D),
        _cast_bf16("cast_w_gate", w_gate[0]).reshape(2, D // 2, FSH),
        _cast_bf16("cast_w_up", w_up[0]).reshape(2, D // 2, FSH),
        _cast_bf16("cast_w_down", w_down[0]).reshape(2, FSH // 2, D),
    ]
    win4, wout4, wg4, wu4, wd4 = _gather_weights(shards)
    win = jnp.transpose(win4.reshape(NSH, D, IN_SHP)[:, :, :IN_SH], (1, 0, 2)).reshape(D, IN_COLS)
    win = jnp.pad(win, ((0, 0), (0, IN_P - IN_COLS)))
    wout = wout4.reshape(D, D)
    wg4 = wg4.reshape(NSH, D, FSH)
    wu4 = wu4.reshape(NSH, D, FSH)
    wd4 = wd4.reshape(NSH, FSH, D)

    g1 = _allgather8("gather_inputs", _pack([c[0], rel_bias[0], conv_w[0]], 40))
    c_all, rel_sh, convw_sh = _unpack(g1, [D, NH * REL_SH, 4 * CONVW_SH])
    rel_full = jnp.concatenate([rel_sh[2 * k].reshape(NH, REL_SH) for k in range(NSH)], axis=1)
    convw_full = jnp.concatenate([convw_sh[2 * k].reshape(4, CONVW_SH) for k in range(NSH)], axis=1)
    cond16 = jnp.pad(c_all, ((0, 8), (0, 0)))
    b_part = lax.dynamic_slice_in_dim(b_ada, chip * ADA_SH, ADA_SH, axis=1)
    mods_part = _mods_part(cond16, w_ada[0], b_part)[:N_DEV]
    g2 = _allgather8("gather_mods", mods_part.reshape(N_DEV * ADA_SH // 128, 128))
    mods_all = jnp.concatenate([g2[2 * k].reshape(N_DEV, ADA_SH) for k in range(NSH)], axis=1)
    mods = lax.dynamic_slice_in_dim(mods_all, dev, 1, axis=0)

    loss, grad_x, dmods, small, big = _local_step(
        x[0], loss_target[0], mods, g_mix, win, rel_full, convw_full, conv_b, dt_bias, a_log, d_skip, g_att_out, g_ssd_out, wout, g_ffn,
        wg4, wu4, wd4, g_final[None, :])

    small_names = ("g_mix", "conv_b", "dt_bias", "a_log", "d_skip", "g_att_out", "g_ssd_out", "g_ffn", "g_final", "rel_bias", "conv_w")
    g3 = _allgather8("gather_small_grads", _pack([dmods] + [small[n] for n in small_names], 264))
    sizes = [6 * D] + [int(np.prod(small[n].shape)) for n in small_names]
    dmods_all = _unpack(g3, sizes)[0]
    summed = _unpack(_sum8(g3), sizes)
    grads = {"b_ada": summed[0].reshape(1, 6 * D)}
    for n, val in zip(small_names, summed[1:]):
        grads[n] = val.reshape(small[n].shape)
    grads["rel_bias"] = lax.dynamic_slice_in_dim(grads["rel_bias"], chip * REL_SH, REL_SH, axis=1)
    grads["conv_w"] = lax.dynamic_slice_in_dim(grads["conv_w"], chip * CONVW_SH, CONVW_SH, axis=1)
    grads["g_final"] = grads["g_final"].reshape(D)
    dm16 = jnp.pad(lax.dynamic_slice_in_dim(dmods_all, chip * ADA_SH, ADA_SH, axis=1), ((0, 8), (0, 0)))
    grads["w_ada"] = _grad_w_ada(cond16, dm16)

    gwin4 = jnp.stack([jnp.pad(big["w_in"][:, k * IN_SH:(k + 1) * IN_SH], ((0, 0), (0, IN_SHP - IN_SH))) for k in range(NSH)])
    stacked = [gwin4, big["w_out"].reshape(NSH, D // NSH, D), big["w_gate"], big["w_up"], big["w_down"]]
    stacked = [g.reshape(NSH, 2, g.shape[1] // 2, g.shape[2]) for g in stacked]
    got = _rs_pair_exchange(stacked)
    core = jnp.reshape(ic, (1,)).astype(jnp.int32)
    sums = [_pair_sum("pair_sum_" + n, core, o, g) for n, o, g in zip(BIG, stacked, got)]
    parts = _rs_chip_exchange(sums)
    halves = [_chip_sum("chip_sum_" + n, p) for n, p in zip(BIG, parts)]
    full = _rs_pair_gather(halves)
    for n, f in zip(BIG, full):
        grads[n] = f.reshape(2 * f.shape[1], f.shape[2])
    grads["w_in"] = grads["w_in"][:, :IN_SH]

    delta, new_m, new_v = {}, {}, {}
    for n in ("w_ada",) + BIG:
        delta[n], new_m[n], new_v[n] = _adamw("adamw_" + n, w[n][0], grads[n], m[n][0], v[n][0])
    sw = _pack([w[n] for n in SMALL], 200)
    sg = _pack([grads[n] for n in SMALL], 200)
    sm = _pack([m[n] for n in SMALL], 200)
    sv = _pack([v[n] for n in SMALL], 200)
    ssz = [int(np.prod(w[n].shape)) for n in SMALL]
    for dst, packed in zip((delta, new_m, new_v), _adamw("adamw_small", sw, sg, sm, sv)):
        for n, val in zip(SMALL, _unpack(packed, ssz)):
            dst[n] = val

    def shaped(d, n):
        return d[n].reshape(w[n].shape)

    total = lax.psum(loss, ("x", "y", "c"))
    return (total, grad_x[None], *[shaped(grads, n) for n in ORDER], *[shaped(delta, n) for n in ORDER],
            *[shaped(new_m, n) for n in ORDER], *[shaped(new_v, n) for n in ORDER])
```

```python
import functools

import numpy as np
import jax
import jax.numpy as jnp
from jax import lax
from jax.experimental import pallas as pl
from jax.experimental.pallas import tpu as pltpu

f32 = jnp.float32
bf16 = jnp.bfloat16
HIGHEST = lax.Precision.HIGHEST
MESH = pl.DeviceIdType.MESH

D = 2048
CHUNK = 64
LEFT = 8
BAND = (LEFT + 1) * CHUNK
BANDP = 640
PADK = LEFT * CHUNK
NH = 16
HD = 64
ATT_W = NH * HD
SSD_W = 1024
NG = 2
NSTATE = 128
GW = SSD_W // NG
XBC = SSD_W + 2 * NG * NSTATE
N_REL = 320
REL_CLIP = 256
FFN = 5632
NSH = 4
FSH = FFN // NSH
IN_COLS = 5648
IN_SH = IN_COLS // NSH
IN_SHP = 1536
IN_A = 3 * ATT_W
IN_B = 2688
IN_P = IN_A + IN_B
EPS = 1e-6
N_DEV = 8

ADAM_LR = 0.001
ADAM_B1 = 0.9
ADAM_B2 = 0.999
ADAM_EPS = 1e-08
ADAM_WD = 0.01
ADAM_STEP = 10

VMEM_LIMIT = 56 * 1024 * 1024


def _params(sem):
    return pltpu.CompilerParams(dimension_semantics=sem, vmem_limit_bytes=VMEM_LIMIT)


def _sds(shape, dtype):
    return jax.ShapeDtypeStruct(shape, dtype)


def _fold8(v):
    r, w = v.shape
    return jnp.sum(v.reshape(r // 8, 8, w), axis=0)


def _sigmoid(v):
    return 1.0 / (1.0 + jnp.exp(-v))


def _softplus(v):
    return jnp.maximum(v, 0.0) + jnp.log(1.0 + jnp.exp(-jnp.abs(v)))


def _dot(a, b, ta=False, tb=False):
    dn = (((0 if ta else 1,), (1 if tb else 0,)), ((), ()))
    return lax.dot_general(a.astype(bf16), b.astype(bf16), dn, preferred_element_type=f32)


def _matmul(name, a, b, *, grid, a_spec, b_spec, o_spec, o_shape, o_dtype, acc_shape, ta=False, tb=False):
    nk = grid[2]

    def body(a_ref, b_ref, o_ref, acc_ref):
        p = _dot(a_ref[...], b_ref[...], ta, tb)
        if nk == 1:
            o_ref[...] = p.astype(o_ref.dtype)
        else:
            k = pl.program_id(2)

            @pl.when(k == 0)
            def _():
                acc_ref[...] = p

            @pl.when(jnp.logical_and(k > 0, k < nk - 1))
            def _():
                acc_ref[...] += p

            @pl.when(k == nk - 1)
            def _():
                o_ref[...] = (acc_ref[...] + p).astype(o_ref.dtype)

    return pl.pallas_call(
        body, name=name, grid=grid, in_specs=[a_spec, b_spec], out_specs=o_spec,
        out_shape=_sds(o_shape, o_dtype), scratch_shapes=[pltpu.VMEM(acc_shape if nk > 1 else (8, 128), f32)],
        compiler_params=_params(("parallel", "parallel", "arbitrary")),
    )(a, b)


def _mm_nn_fullk(name, a, b, tm, tn, o_dtype):
    m, k = a.shape
    n = b.shape[1]
    return _matmul(name, a, b, grid=(m // tm, n // tn, 1),
                   a_spec=pl.BlockSpec((tm, k), lambda i, j, kk: (i, 0)),
                   b_spec=pl.BlockSpec((k, tn), lambda i, j, kk: (0, j)),
                   o_spec=pl.BlockSpec((tm, tn), lambda i, j, kk: (i, j)),
                   o_shape=(m, n), o_dtype=o_dtype, acc_shape=(tm, tn))


def _mm_nt(name, a, b, tm, tn, tk, o_dtype):
    m, k = a.shape
    n = b.shape[0]
    return _matmul(name, a, b, grid=(m // tm, n // tn, k // tk), tb=True,
                   a_spec=pl.BlockSpec((tm, tk), lambda i, j, kk: (i, kk)),
                   b_spec=pl.BlockSpec((tn, tk), lambda i, j, kk: (j, kk)),
                   o_spec=pl.BlockSpec((tm, tn), lambda i, j, kk: (i, j)),
                   o_shape=(m, n), o_dtype=o_dtype, acc_shape=(tm, tn))


def _mm_tn(name, a, b, tm, tn, tk, o_dtype):
    k, m = a.shape
    n = b.shape[1]
    return _matmul(name, a, b, grid=(m // tm, n // tn, k // tk), ta=True,
                   a_spec=pl.BlockSpec((tk, tm), lambda i, j, kk: (kk, i)),
                   b_spec=pl.BlockSpec((tk, tn), lambda i, j, kk: (kk, j)),
                   o_spec=pl.BlockSpec((tm, tn), lambda i, j, kk: (i, j)),
                   o_shape=(m, n), o_dtype=o_dtype, acc_shape=(tm, tn))


def _ffn_up(h2b, wg4, wu4, tm):
    s = h2b.shape[0]

    def body(h_ref, wg_ref, wu_ref, g_ref, u_ref, a_ref):
        h = h_ref[...]
        g = _dot(h, wg_ref[...])
        u = _dot(h, wu_ref[...])
        g_ref[...] = g
        u_ref[...] = u
        a_ref[...] = (g * _sigmoid(g) * u).astype(bf16)

    wspec = pl.BlockSpec((None, D, FSH), lambda k, i: (k, 0, 0))
    ospec = pl.BlockSpec((tm, FSH), lambda k, i: (i, k))
    return pl.pallas_call(
        body, name="ffn_up", grid=(NSH, s // tm),
        in_specs=[pl.BlockSpec((tm, D), lambda k, i: (i, 0)), wspec, wspec],
        out_specs=[ospec, ospec, ospec],
        out_shape=[_sds((s, FFN), f32), _sds((s, FFN), f32), _sds((s, FFN), bf16)],
        compiler_params=_params(("parallel", "parallel")),
    )(h2b, wg4, wu4)


def _ffn_down(act, wd4, tm):
    s = act.shape[0]
    return _matmul("ffn_down", act, wd4, grid=(s // tm, 1, NSH),
                   a_spec=pl.BlockSpec((tm, FSH), lambda i, j, k: (i, k)),
                   b_spec=pl.BlockSpec((None, FSH, D), lambda i, j, k: (k, 0, 0)),
                   o_spec=pl.BlockSpec((tm, D), lambda i, j, k: (i, 0)),
                   o_shape=(s, D), o_dtype=f32, acc_shape=(tm, D))


def _ffn_dact(dffn, wd4, gate, up, tm):
    s = dffn.shape[0]

    def body(d_ref, w_ref, g_ref, u_ref, dg_ref, du_ref):
        dact = _dot(d_ref[...], w_ref[...], tb=True)
        g = g_ref[...]
        sg = _sigmoid(g)
        dg_ref[...] = (dact * u_ref[...] * (sg * (1.0 + g * (1.0 - sg)))).astype(bf16)
        du_ref[...] = (dact * (g * sg)).astype(bf16)

    blk = pl.BlockSpec((tm, FSH), lambda k, i: (i, k))
    return pl.pallas_call(
        body, name="ffn_dact", grid=(NSH, s // tm),
        in_specs=[pl.BlockSpec((tm, D), lambda k, i: (i, 0)), pl.BlockSpec((None, FSH, D), lambda k, i: (k, 0, 0)), blk, blk],
        out_specs=[blk, blk], out_shape=[_sds((s, FFN), bf16), _sds((s, FFN), bf16)],
        compiler_params=_params(("parallel", "parallel")),
    )(dffn, wd4, gate, up)


def _ffn_dh(dgate, dup, wg4, wu4, tm):
    s = dgate.shape[0]

    def body(dg_ref, du_ref, wg_ref, wu_ref, o_ref, acc_ref):
        k = pl.program_id(1)
        p = _dot(dg_ref[...], wg_ref[...], tb=True) + _dot(du_ref[...], wu_ref[...], tb=True)

        @pl.when(k == 0)
        def _():
            acc_ref[...] = p

        @pl.when(jnp.logical_and(k > 0, k < NSH - 1))
        def _():
            acc_ref[...] += p

        @pl.when(k == NSH - 1)
        def _():
            o_ref[...] = acc_ref[...] + p

    aspec = pl.BlockSpec((tm, FSH), lambda i, k: (i, k))
    wspec = pl.BlockSpec((None, D, FSH), lambda i, k: (k, 0, 0))
    return pl.pallas_call(
        body, name="ffn_dh", grid=(s // tm, NSH), in_specs=[aspec, aspec, wspec, wspec],
        out_specs=pl.BlockSpec((tm, D), lambda i, k: (i, 0)), out_shape=_sds((s, D), f32),
        scratch_shapes=[pltpu.VMEM((tm, D), f32)], compiler_params=_params(("parallel", "arbitrary")),
    )(dgate, dup, wg4, wu4)


def _grad_cols4(name, h, dy, tm, tk):
    s = h.shape[0]
    return _matmul(name, h, dy, grid=(NSH, D // tm, s // tk), ta=True,
                   a_spec=pl.BlockSpec((tk, tm), lambda k, i, kk: (kk, i)),
                   b_spec=pl.BlockSpec((tk, FSH), lambda k, i, kk: (kk, k)),
                   o_spec=pl.BlockSpec((None, tm, FSH), lambda k, i, kk: (k, i, 0)),
                   o_shape=(NSH, D, FSH), o_dtype=bf16, acc_shape=(tm, FSH))


def _grad_wdown4(act, dffn, tn, tk):
    s = act.shape[0]
    return _matmul("grad_w_down", act, dffn, grid=(NSH, D // tn, s // tk), ta=True,
                   a_spec=pl.BlockSpec((tk, FSH), lambda k, j, kk: (kk, k)),
                   b_spec=pl.BlockSpec((tk, tn), lambda k, j, kk: (kk, j)),
                   o_spec=pl.BlockSpec((None, FSH, tn), lambda k, j, kk: (k, 0, j)),
                   o_shape=(NSH, FSH, D), o_dtype=bf16, acc_shape=(FSH, tn))


def _row_spec(w):
    return pl.BlockSpec((1, w), lambda i: (0, 0))


def _tile_spec(tm, w, col=0):
    return pl.BlockSpec((tm, w), lambda i: (i, col))


def _norm_mod(name, x, g, sc, sh, tm):
    s = x.shape[0]

    def body(x_ref, g_ref, sc_ref, sh_ref, o_ref):
        xv = x_ref[...]
        r = lax.rsqrt(jnp.mean(xv * xv, axis=-1, keepdims=True) + EPS)
        o_ref[...] = (xv * r * g_ref[...] * (1.0 + sc_ref[...]) + sh_ref[...]).astype(bf16)

    return pl.pallas_call(
        body, name=name, grid=(s // tm,), in_specs=[_tile_spec(tm, D), _row_spec(D), _row_spec(D), _row_spec(D)],
        out_specs=_tile_spec(tm, D), out_shape=_sds((s, D), bf16), compiler_params=_params(("parallel",)),
    )(x, g, sc, sh)


def _resid_norm_mod(x, gt, mix, g, sc, sh, tm):
    s = x.shape[0]

    def body(x_ref, gt_ref, m_ref, g_ref, sc_ref, sh_ref, x2_ref, h_ref):
        xv = x_ref[...] + gt_ref[...] * m_ref[...]
        x2_ref[...] = xv
        r = lax.rsqrt(jnp.mean(xv * xv, axis=-1, keepdims=True) + EPS)
        h_ref[...] = (xv * r * g_ref[...] * (1.0 + sc_ref[...]) + sh_ref[...]).astype(bf16)

    return pl.pallas_call(
        body, name="resid_norm_mod", grid=(s // tm,),
        in_specs=[_tile_spec(tm, D), _row_spec(D), _tile_spec(tm, D), _row_spec(D), _row_spec(D), _row_spec(D)],
        out_specs=[_tile_spec(tm, D), _tile_spec(tm, D)], out_shape=[_sds((s, D), f32), _sds((s, D), bf16)],
        compiler_params=_params(("parallel",)),
    )(x, gt, mix, g, sc, sh)


def _final_fwd_bwd(x2, ffn, gt2, g, tgt, tm):
    s = x2.shape[0]
    n = s // tm

    def body(x_ref, f_ref, gt_ref, g_ref, t_ref, dx_ref, df_ref, loss_ref, dg_ref, dgt_ref, a_loss, a_dg, a_dgt):
        i = pl.program_id(0)

        @pl.when(i == 0)
        def _():
            a_loss[...] = jnp.zeros_like(a_loss)
            a_dg[...] = jnp.zeros_like(a_dg)
            a_dgt[...] = jnp.zeros_like(a_dgt)

        fv = f_ref[...]
        gt = gt_ref[...]
        gv = g_ref[...]
        xv = x_ref[...] + gt * fv
        r = lax.rsqrt(jnp.mean(xv * xv, axis=-1, keepdims=True) + EPS)
        xh = xv * r
        e = xh * gv - t_ref[...]
        a_loss[...] += _fold8(e * e)
        dy = e * (1.0 / D)
        a_dg[...] += _fold8(dy * xh)
        t = dy * gv
        dx = r * (t - xh * jnp.mean(t * xh, axis=-1, keepdims=True))
        dx_ref[...] = dx
        a_dgt[...] += _fold8(dx * fv)
        df_ref[...] = (dx * gt).astype(bf16)

        @pl.when(i == n - 1)
        def _():
            tot = jnp.sum(jnp.sum(a_loss[...], axis=0, keepdims=True), axis=1, keepdims=True) * (0.5 / D)
            loss_ref[...] = jnp.broadcast_to(tot, (1, 128))
            dg_ref[...] = jnp.sum(a_dg[...], axis=0, keepdims=True)
            dgt_ref[...] = jnp.sum(a_dgt[...], axis=0, keepdims=True)

    return pl.pallas_call(
        body, name="final_fwd_bwd", grid=(n,),
        in_specs=[_tile_spec(tm, D), _tile_spec(tm, D), _row_spec(D), _row_spec(D), _tile_spec(tm, D)],
        out_specs=[_tile_spec(tm, D), _tile_spec(tm, D), _row_spec(128), _row_spec(D), _row_spec(D)],
        out_shape=[_sds((s, D), f32), _sds((s, D), bf16), _sds((1, 128), f32), _sds((1, D), f32), _sds((1, D), f32)],
        scratch_shapes=[pltpu.VMEM((8, D), f32)] * 3, compiler_params=_params(("arbitrary",)),
    )(x2, ffn, gt2, g, tgt)


def _norm_mod_bwd(name, dh, xin, g, sc, dres, tm, mix=None, gt=None):
    s = dh.shape[0]
    n = s // tm
    with_mix = mix is not None

    def body(*refs):
        if with_mix:
            dh_ref, x_ref, g_ref, sc_ref, dr_ref, m_ref, gt_ref, dx_ref, dm_ref, dsc_ref, dsh_ref, dg_ref, dgt_ref, a_sc, a_sh, a_g, a_gt = refs
        else:
            dh_ref, x_ref, g_ref, sc_ref, dr_ref, dx_ref, dsc_ref, dsh_ref, dg_ref, a_sc, a_sh, a_g = refs
        i = pl.program_id(0)

        @pl.when(i == 0)
        def _():
            a_sc[...] = jnp.zeros_like(a_sc)
            a_sh[...] = jnp.zeros_like(a_sh)
            a_g[...] = jnp.zeros_like(a_g)
            if with_mix:
                a_gt[...] = jnp.zeros_like(a_gt)

        dh = dh_ref[...]
        xv = x_ref[...]
        gv = g_ref[...]
        r = lax.rsqrt(jnp.mean(xv * xv, axis=-1, keepdims=True) + EPS)
        xh = xv * r
        a_sc[...] += _fold8(dh * xh * gv)
        a_sh[...] += _fold8(dh)
        dn = dh * (1.0 + sc_ref[...])
        a_g[...] += _fold8(dn * xh)
        t = dn * gv
        dx = dr_ref[...] + r * (t - xh * jnp.mean(t * xh, axis=-1, keepdims=True))
        dx_ref[...] = dx
        if with_mix:
            a_gt[...] += _fold8(dx * m_ref[...])
            dm_ref[...] = (dx * gt_ref[...]).astype(bf16)

        @pl.when(i == n - 1)
        def _():
            dsc_ref[...] = jnp.sum(a_sc[...], axis=0, keepdims=True)
            dsh_ref[...] = jnp.sum(a_sh[...], axis=0, keepdims=True)
            dg_ref[...] = jnp.sum(a_g[...], axis=0, keepdims=True)
            if with_mix:
                dgt_ref[...] = jnp.sum(a_gt[...], axis=0, keepdims=True)

    tile, row = _tile_spec(tm, D), _row_spec(D)
    if with_mix:
        ins, args = [tile, tile, row, row, tile, tile, row], (dh, xin, g, sc, dres, mix, gt)
        outs = [tile, tile, row, row, row, row]
        shapes = [_sds((s, D), f32), _sds((s, D), bf16)] + [_sds((1, D), f32)] * 4
        nacc = 4
    else:
        ins, args = [tile, tile, row, row, tile], (dh, xin, g, sc, dres)
        outs = [tile, row, row, row]
        shapes = [_sds((s, D), f32)] + [_sds((1, D), f32)] * 3
        nacc = 3
    return pl.pallas_call(
        body, name=name, grid=(n,), in_specs=ins, out_specs=outs, out_shape=shapes,
        scratch_shapes=[pltpu.VMEM((8, D), f32)] * nacc, compiler_params=_params(("arbitrary",)),
    )(*args)


def _mix_pre(att, y, proj2, g_att, g_ssd, tm):
    s = att.shape[0]

    def body(a_ref, y_ref, z_ref, ga_ref, gs_ref, o_ref):
        a = a_ref[...]
        ra = lax.rsqrt(jnp.mean(a * a, axis=-1, keepdims=True) + EPS)
        o_ref[:, 0:ATT_W] = (a * ra * ga_ref[...]).astype(bf16)
        z = z_ref[...]
        u = y_ref[...] * (z * _sigmoid(z))
        ru = lax.rsqrt(jnp.mean(u * u, axis=-1, keepdims=True) + EPS)
        o_ref[:, ATT_W:] = (u * ru * gs_ref[...]).astype(bf16)

    t = _tile_spec(tm, ATT_W)
    return pl.pallas_call(
        body, name="mix_pre", grid=(s // tm,), in_specs=[t, t, t, _row_spec(ATT_W), _row_spec(SSD_W)],
        out_specs=_tile_spec(tm, D), out_shape=_sds((s, D), bf16), compiler_params=_params(("parallel",)),
    )(att, y, proj2, g_att, g_ssd)


def _mix_pre_bwd(dmc, att, y, proj2, g_att, g_ssd, tm):
    s = att.shape[0]
    n = s // tm

    def body(da_ref, ds_ref, a_ref, y_ref, z_ref, ga_ref, gs_ref, datt_ref, dy_ref, dz_ref, dga_ref, dgs_ref, acc_a, acc_s):
        i = pl.program_id(0)

        @pl.when(i == 0)
        def _():
            acc_a[...] = jnp.zeros_like(acc_a)
            acc_s[...] = jnp.zeros_like(acc_s)

        a = a_ref[...]
        ra = lax.rsqrt(jnp.mean(a * a, axis=-1, keepdims=True) + EPS)
        ah = a * ra
        dan = da_ref[...]
        acc_a[...] += _fold8(dan * ah)
        t = dan * ga_ref[...]
        datt_ref[...] = (ra * (t - ah * jnp.mean(t * ah, axis=-1, keepdims=True))).astype(bf16)
        z = z_ref[...]
        yv = y_ref[...]
        sz = _sigmoid(z)
        sil = z * sz
        u = yv * sil
        ru = lax.rsqrt(jnp.mean(u * u, axis=-1, keepdims=True) + EPS)
        uh = u * ru
        dsn = ds_ref[...]
        acc_s[...] += _fold8(dsn * uh)
        t2 = dsn * gs_ref[...]
        du = ru * (t2 - uh * jnp.mean(t2 * uh, axis=-1, keepdims=True))
        dy_ref[...] = du * sil
        dz_ref[...] = (du * yv * (sz * (1.0 + z * (1.0 - sz)))).astype(bf16)

        @pl.when(i == n - 1)
        def _():
            dga_ref[...] = jnp.sum(acc_a[...], axis=0, keepdims=True)
            dgs_ref[...] = jnp.sum(acc_s[...], axis=0, keepdims=True)

    t = _tile_spec(tm, ATT_W)
    row = _row_spec(ATT_W)
    return pl.pallas_call(
        body, name="mix_pre_bwd", grid=(n,),
        in_specs=[_tile_spec(tm, ATT_W, 0), _tile_spec(tm, ATT_W, 1), t, t, t, row, row],
        out_specs=[t, t, t, row, row],
        out_shape=[_sds((s, ATT_W), bf16), _sds((s, SSD_W), f32), _sds((s, SSD_W), bf16), _sds((1, ATT_W), f32), _sds((1, SSD_W), f32)],
        scratch_shapes=[pltpu.VMEM((8, ATT_W), f32)] * 2, compiler_params=_params(("arbitrary",)),
    )(dmc, dmc, att, y, proj2, g_att, g_ssd)


ATT_GROUP = 2


def _pair_rows(qc):
    two = jnp.concatenate([qc, qc], axis=0)
    r = lax.broadcasted_iota(jnp.int32, (2 * CHUNK, 128), 0)
    l = lax.broadcasted_iota(jnp.int32, (2 * CHUNK, 128), 1)
    return jnp.where((r < CHUNK) == (l < HD), two, jnp.zeros_like(two))


def _pair_scores(wt, kb, bias, r0):
    sc = lax.dot_general(wt, kb, (((1,), (1,)), ((), ())), preferred_element_type=f32) * (HD ** -0.5) + bias
    kidx = lax.broadcasted_iota(jnp.int32, sc.shape, 1)
    return jnp.where(r0 + kidx >= PADK, sc, -jnp.inf)


def _softmax_lanes(sc):
    e = jnp.exp(sc - jnp.max(sc, axis=-1, keepdims=True))
    return e / jnp.sum(e, axis=-1, keepdims=True)


def _pair_diag(r):
    lane = lax.broadcasted_iota(jnp.int32, (CHUNK, 128), 1)
    return jnp.where(lane < HD, r[0:CHUNK], r[CHUNK:])


def _pad_keys(k_ref, kp, s):
    kp[0:PADK, :] = jnp.zeros((PADK, 128), bf16)
    kp[PADK:PADK + s, :] = k_ref[...]
    kp[PADK + s:, :] = jnp.zeros((CHUNK, 128), bf16)


def _attn_fwd(qkv, bias2):
    s = qkv.shape[0]
    nc = s // CHUNK
    npair = NH // 2

    def body(q_ref, k_ref, v_ref, b_ref, o_ref, kp, vp):
        _pad_keys(k_ref, kp, s)
        _pad_keys(v_ref, vp, s)

        def group(g, carry):
            r0s = [pl.multiple_of((g * ATT_GROUP + u) * CHUNK, CHUNK) for u in range(ATT_GROUP)]
            scs = [_pair_scores(_pair_rows(q_ref[pl.ds(r0, CHUNK), :]), kp[pl.ds(r0, BANDP), :], b_ref[...], r0) for r0 in r0s]
            ps = [_softmax_lanes(sc).astype(bf16) for sc in scs]
            for r0, p in zip(r0s, ps):
                o_ref[pl.ds(r0, CHUNK), :] = _pair_diag(jnp.dot(p, vp[pl.ds(r0, BANDP), :], preferred_element_type=f32))
            return carry

        lax.fori_loop(0, nc // ATT_GROUP, group, 0)

    return pl.pallas_call(
        body, name="attn_fwd", grid=(npair,),
        in_specs=[pl.BlockSpec((s, 128), lambda p: (0, p)), pl.BlockSpec((s, 128), lambda p: (0, npair + p)),
                  pl.BlockSpec((s, 128), lambda p: (0, 2 * npair + p)), pl.BlockSpec((None, 2 * CHUNK, BANDP), lambda p: (p, 0, 0))],
        out_specs=pl.BlockSpec((s, 128), lambda p: (0, p)), out_shape=_sds((s, ATT_W), f32),
        scratch_shapes=[pltpu.VMEM((PADK + s + CHUNK, 128), bf16)] * 2, compiler_params=_params(("parallel",)),
    )(qkv, qkv, qkv, bias2)


def _attn_bwd(qkv, datt, bias2, bias2t):
    s = qkv.shape[0]
    nc = s // CHUNK
    npair = NH // 2
    rows = PADK + s + CHUNK
    nt = (((1,), (1,)), ((), ()))

    def body(q_ref, k_ref, v_ref, do_ref, b_ref, bt_ref, dq_ref, dk_ref, dv_ref, g_ref, kp, vp, dkp, dvp):
        _pad_keys(k_ref, kp, s)
        _pad_keys(v_ref, vp, s)
        dkp[...] = jnp.zeros_like(dkp)
        dvp[...] = jnp.zeros_like(dvp)
        g_ref[...] = jnp.zeros_like(g_ref)

        def group(g, carry):
            r0s = [pl.multiple_of((g * ATT_GROUP + u) * CHUNK, CHUNK) for u in range(ATT_GROUP)]
            wts = [_pair_rows(q_ref[pl.ds(r0, CHUNK), :]) for r0 in r0s]
            dos = [_pair_rows(do_ref[pl.ds(r0, CHUNK), :]) for r0 in r0s]
            scs = [_pair_scores(wt, kp[pl.ds(r0, BANDP), :], b_ref[...], r0) for wt, r0 in zip(wts, r0s)]
            dps = [lax.dot_general(do, vp[pl.ds(r0, BANDP), :], nt, preferred_element_type=f32) for do, r0 in zip(dos, r0s)]
            scts, dpts = [], []
            for wt, do, r0 in zip(wts, dos, r0s):
                sct = lax.dot_general(kp[pl.ds(r0, BANDP), :], wt, nt, preferred_element_type=f32) * (HD ** -0.5) + bt_ref[...]
                kidx = lax.broadcasted_iota(jnp.int32, sct.shape, 0)
                scts.append(jnp.where(r0 + kidx >= PADK, sct, -jnp.inf))
                dpts.append(lax.dot_general(vp[pl.ds(r0, BANDP), :], do, nt, preferred_element_type=f32))
            for r0, sc, dp in zip(r0s, scs, dps):
                p = _softmax_lanes(sc)
                ds = p * (dp - jnp.sum(p * dp, axis=-1, keepdims=True))
                g_ref[...] += ds
                dq = jnp.dot(ds.astype(bf16), kp[pl.ds(r0, BANDP), :], preferred_element_type=f32)
                dq_ref[pl.ds(r0, CHUNK), :] = (_pair_diag(dq) * (HD ** -0.5)).astype(bf16)
            for r0, wt, do, sct, dpt in zip(r0s, wts, dos, scts, dpts):
                e = jnp.exp(sct - jnp.max(sct, axis=0, keepdims=True))
                pt = e / jnp.sum(e, axis=0, keepdims=True)
                dst = pt * (dpt - jnp.sum(pt * dpt, axis=0, keepdims=True))
                dkp[pl.ds(r0, BANDP), :] += jnp.dot(dst.astype(bf16), wt, preferred_element_type=f32) * (HD ** -0.5)
                dvp[pl.ds(r0, BANDP), :] += jnp.dot(pt.astype(bf16), do, preferred_element_type=f32)
            return carry

        lax.fori_loop(0, nc // ATT_GROUP, group, 0)
        dk_ref[...] = dkp[PADK:PADK + s, :].astype(bf16)
        dv_ref[...] = dvp[PADK:PADK + s, :].astype(bf16)

    col = lambda off: pl.BlockSpec((s, 128), lambda p: (0, off + p))
    return pl.pallas_call(
        body, name="attn_bwd", grid=(npair,),
        in_specs=[col(0), col(npair), col(2 * npair), col(0), pl.BlockSpec((None, 2 * CHUNK, BANDP), lambda p: (p, 0, 0)),
                  pl.BlockSpec((None, BANDP, 2 * CHUNK), lambda p: (p, 0, 0))],
        out_specs=[col(0), col(0), col(0), pl.BlockSpec((None, 2 * CHUNK, BANDP), lambda p: (p, 0, 0))],
        out_shape=[_sds((s, ATT_W), bf16)] * 3 + [_sds((npair, 2 * CHUNK, BANDP), f32)],
        scratch_shapes=[pltpu.VMEM((rows, 128), bf16)] * 2 + [pltpu.VMEM((rows, 128), f32)] * 2,
        compiler_params=_params(("parallel",)),
    )(qkv, qkv, qkv, datt, bias2, bias2t)


def _rel_tables():
    onehot = np.zeros((BANDP, N_REL), np.float32)
    for j in range(BAND + CHUNK - 1):
        o = j - (CHUNK - 1)
        onehot[j, int(np.clip(PADK - o, -(CHUNK - 1), REL_CLIP)) + CHUNK - 1] = 1.0
    return onehot, np.ascontiguousarray(np.eye(CHUNK, dtype=np.float32)[::-1])


def _expand_bias(rel):
    ext = jnp.concatenate([jnp.broadcast_to(rel[:, N_REL - 1:], (NH, N_REL - 1)), rel[:, ::-1],
                           jnp.zeros((NH, BANDP - BAND + 1), f32)], axis=1)
    band = jnp.stack([ext[:, CHUNK - 1 - q:CHUNK - 1 - q + BANDP] for q in range(CHUNK)], axis=1)
    band = jnp.where(np.arange(BANDP) < BAND, band, -jnp.inf)
    return band.reshape(NH // 2, 2 * CHUNK, BANDP)


def _rel_bias_grad(gband):
    def body(g_ref, m_ref, flip_ref, o_ref, d2):
        for h in range(NH):
            rev = jnp.dot(flip_ref[...], g_ref[h], precision=HIGHEST, preferred_element_type=f32)
            rolled = pltpu.roll(rev, 0, 1, stride=1, stride_axis=0)
            d2[h:h + 1, :] = jnp.sum(rolled, axis=0, keepdims=True)
        o_ref[...] = jnp.dot(d2[...], m_ref[...], precision=HIGHEST, preferred_element_type=f32)

    onehot, flip = _rel_tables()
    return pl.pallas_call(
        body, name="rel_bias_grad", out_shape=_sds((NH, N_REL), f32), scratch_shapes=[pltpu.VMEM((NH, BANDP), f32)],
    )(gband, jnp.asarray(onehot), jnp.asarray(flip))


XBC_BLK = 512
XBC_COL0 = SSD_W // XBC_BLK
DT_COL = (SSD_W + XBC) // 128


def _conv_taps(ext, w_ref, b_ref, tm):
    n = ext.shape[0]
    pre = w_ref[3:4, :] * ext + b_ref[...]
    for j in range(3):
        pre = pre + w_ref[j:j + 1, :] * pltpu.roll(ext, 3 - j, 0)
    return pre


def _ssd_conv(proj2, conv_w, conv_b, tm):
    s = proj2.shape[0]
    nb = XBC // XBC_BLK

    def body(x_ref, p_ref, w_ref, b_ref, o_ref):
        i = pl.program_id(1)
        prev = jnp.where(i > 0, p_ref[...], 0.0)
        ext = jnp.concatenate([prev, x_ref[...]], axis=0)
        pre = _conv_taps(ext, w_ref, b_ref, tm)[8:8 + tm]
        o_ref[...] = pre * _sigmoid(pre)

    return pl.pallas_call(
        body, name="ssd_conv", grid=(nb, s // tm),
        in_specs=[pl.BlockSpec((tm, XBC_BLK), lambda j, i: (i, XBC_COL0 + j)),
                  pl.BlockSpec((8, XBC_BLK), lambda j, i: (jnp.maximum(i * (tm // 8) - 1, 0), XBC_COL0 + j)),
                  pl.BlockSpec((4, XBC_BLK), lambda j, i: (0, j)), pl.BlockSpec((1, XBC_BLK), lambda j, i: (0, j))],
        out_specs=pl.BlockSpec((tm, XBC_BLK), lambda j, i: (i, j)), out_shape=_sds((s, XBC), f32),
        compiler_params=_params(("parallel", "parallel")),
    )(proj2, proj2, conv_w, conv_b)


def _ssd_conv_bwd(dxbc, proj2, conv_w, conv_b, tm):
    s = proj2.shape[0]
    nb = XBC // XBC_BLK
    n = s // tm
    last8 = s // 8 - 1

    def body(x_ref, xp_ref, xn_ref, d_ref, dn_ref, w_ref, b_ref, o_ref, dw_ref, db_ref):
        i = pl.program_id(1)

        @pl.when(i == 0)
        def _():
            dw_ref[...] = jnp.zeros_like(dw_ref)
            db_ref[...] = jnp.zeros_like(db_ref)

        prev = jnp.where(i > 0, xp_ref[...], 0.0)
        ext = jnp.concatenate([prev, x_ref[...], xn_ref[...]], axis=0)
        pre = _conv_taps(ext, w_ref, b_ref, tm)
        sg = _sigmoid(pre)
        dnext = jnp.where(i < n - 1, dn_ref[...], 0.0)
        dext = jnp.concatenate([jnp.zeros((8, XBC_BLK), f32), d_ref[...], dnext], axis=0)
        dpre = dext * (sg * (1.0 + pre * (1.0 - sg)))
        rows = tm + 16
        dx = w_ref[3:4, :] * dpre
        for j in range(3):
            dx = dx + w_ref[j:j + 1, :] * pltpu.roll(dpre, rows - (3 - j), 0)
        o_ref[...] = dx[8:8 + tm].astype(bf16)
        dcur = dpre[8:8 + tm]
        db_ref[...] += jnp.sum(dcur, axis=0, keepdims=True)
        dw_ref[3:4, :] += jnp.sum(dcur * ext[8:8 + tm], axis=0, keepdims=True)
        for j in range(3):
            dw_ref[j:j + 1, :] += jnp.sum(dcur * pltpu.roll(ext, 3 - j, 0)[8:8 + tm], axis=0, keepdims=True)

    xcol = lambda j: XBC_COL0 + j
    return pl.pallas_call(
        body, name="ssd_conv_bwd", grid=(nb, n),
        in_specs=[pl.BlockSpec((tm, XBC_BLK), lambda j, i: (i, xcol(j))),
                  pl.BlockSpec((8, XBC_BLK), lambda j, i: (jnp.maximum(i * (tm // 8) - 1, 0), xcol(j))),
                  pl.BlockSpec((8, XBC_BLK), lambda j, i: (jnp.minimum((i + 1) * (tm // 8), last8), xcol(j))),
                  pl.BlockSpec((tm, XBC_BLK), lambda j, i: (i, j)),
                  pl.BlockSpec((8, XBC_BLK), lambda j, i: (jnp.minimum((i + 1) * (tm // 8), last8), j)),
                  pl.BlockSpec((4, XBC_BLK), lambda j, i: (0, j)), pl.BlockSpec((1, XBC_BLK), lambda j, i: (0, j))],
        out_specs=[pl.BlockSpec((tm, XBC_BLK), lambda j, i: (i, j)), pl.BlockSpec((4, XBC_BLK), lambda j, i: (0, j)),
                   pl.BlockSpec((1, XBC_BLK), lambda j, i: (0, j))],
        out_shape=[_sds((s, XBC), bf16), _sds((4, XBC), f32), _sds((1, XBC), f32)],
        compiler_params=_params(("parallel", "arbitrary")),
    )(proj2, proj2, proj2, dxbc, dxbc, conv_w, conv_b)


def _ssd_consts():
    ex = np.zeros((128, SSD_W), np.float32)
    for h in range(NH):
        ex[h, h * HD:(h + 1) * HD] = 1.0
    sel = np.zeros((8, 128), np.float32)
    for h in range(NH):
        sel[h // 2, h] = 1.0
    par = np.zeros((128, 128), np.float32)
    for r in range(128):
        for h in range(NH):
            par[r, h] = 1.0 if (h % 2) == (r // 64) else 0.0
    ones_blk = np.zeros((128, 128), np.float32)
    for r in range(128):
        ones_blk[r, (r // 64) * 64:(r // 64) * 64 + 64] = 1.0
    return ex, np.ascontiguousarray(ex.T), sel, par, ones_blk


def _ssd_common(xbc_ref, dtr_ref, a_ref, dtb_ref, ex_ref, sel_ref, par_ref):
    xs = xbc_ref[:, 0:SSD_W]
    dt = _softplus(dtr_ref[...] + dtb_ref[...])
    adt = dt * a_ref[...]
    r_i = lax.broadcasted_iota(jnp.int32, (CHUNK, CHUNK), 0)
    c_i = lax.broadcasted_iota(jnp.int32, (CHUNK, CHUNK), 1)
    tril = (r_i >= c_i).astype(f32)
    cs = jnp.dot(tril, adt, precision=HIGHEST, preferred_element_type=f32)
    cs2 = jnp.concatenate([cs, cs], axis=0) * par_ref[...]
    cstp = lax.dot_general(sel_ref[...], cs2, (((1,), (1,)), ((), ())), precision=HIGHEST, preferred_element_type=f32)
    ex = ex_ref[...]
    dt_full = jnp.dot(dt, ex, precision=HIGHEST, preferred_element_type=f32)
    cs_full = jnp.dot(cs, ex, precision=HIGHEST, preferred_element_type=f32)
    return xs, dt, cs, cstp, dt_full, cs_full


def _pair_mask():
    l_i = lax.broadcasted_iota(jnp.int32, (CHUNK, 128), 0)
    lane = lax.broadcasted_iota(jnp.int32, (CHUNK, 128), 1)
    return l_i >= (lane % CHUNK), lane < HD


def _block_diag(xp, first):
    z = jnp.zeros_like(xp)
    return jnp.concatenate([jnp.where(first, xp, z), jnp.where(first, z, xp)], axis=0)


def _ssd_fwd(xbc, proj2, a_row, dtb_row, dsk_full):
    s = xbc.shape[0]
    nc = s // CHUNK
    ex, ext, sel, par, ones_blk = _ssd_consts()

    def body(xbc_ref, dtr_ref, a_ref, dtb_ref, dsk_ref, ex_ref, sel_ref, par_ref, y_ref, hs_ref, hst):
        @pl.when(pl.program_id(0) == 0)
        def _():
            hst[...] = jnp.zeros_like(hst)

        hs_ref[...] = hst[...]
        xs, dt, cs, cstp, dt_full, cs_full = _ssd_common(xbc_ref, dtr_ref, a_ref, dtb_ref, ex_ref, sel_ref, par_ref)
        cs_last = cs_full[CHUNK - 1:CHUNK, :]
        xdt = xs * dt_full
        causal, first = _pair_mask()
        for g in range(NG):
            gl = slice(g * GW, (g + 1) * GW)
            bg = xbc_ref[:, SSD_W + g * NSTATE:SSD_W + (g + 1) * NSTATE].astype(bf16)
            cg = xbc_ref[:, SSD_W + NG * NSTATE + g * NSTATE:SSD_W + NG * NSTATE + (g + 1) * NSTATE].astype(bf16)
            cb2 = lax.dot_general(cg, jnp.concatenate([bg, bg], axis=0), (((1,), (1,)), ((), ())), preferred_element_type=f32)
            hg = hst[g]
            y0 = jnp.dot(cg, hg.astype(bf16), preferred_element_type=f32)
            yoff = jnp.exp(cs_full[:, gl]) * y0
            for j in range(GW // 128):
                pair = g * (GW // 128) + j
                pl_ = slice(pair * 128, (pair + 1) * 128)
                seg = jnp.exp(jnp.where(causal, cs_full[:, pl_] - cstp[pair:pair + 1, :], -jnp.inf))
                m = (cb2 * seg).astype(bf16)
                yd = jnp.dot(m, _block_diag(xdt[:, pl_].astype(bf16), first), preferred_element_type=f32)
                y_ref[:, pl_] = yd + yoff[:, j * 128:(j + 1) * 128] + xs[:, pl_] * dsk_ref[:, pl_]
            xdec = (xdt[:, gl] * jnp.exp(cs_last[:, gl] - cs_full[:, gl])).astype(bf16)
            st = lax.dot_general(bg, xdec, (((0,), (0,)), ((), ())), preferred_element_type=f32)
            hst[g] = jnp.exp(cs_last[:, gl]) * hg + st

    const = lambda shape: pl.BlockSpec(shape, lambda c: tuple(0 for _ in shape))
    return pl.pallas_call(
        body, name="ssd_fwd", grid=(nc,),
        in_specs=[pl.BlockSpec((CHUNK, XBC), lambda c: (c, 0)), pl.BlockSpec((CHUNK, 128), lambda c: (c, DT_COL)),
                  const((1, 128)), const((1, 128)), const((1, SSD_W)), const((128, SSD_W)), const((8, 128)), const((128, 128))],
        out_specs=[pl.BlockSpec((CHUNK, SSD_W), lambda c: (c, 0)), pl.BlockSpec((None, NG, NSTATE, GW), lambda c: (c, 0, 0, 0))],
        out_shape=[_sds((s, SSD_W), f32), _sds((nc, NG, NSTATE, GW), f32)],
        scratch_shapes=[pltpu.VMEM((NG, NSTATE, GW), f32)], compiler_params=_params(("arbitrary",)),
    )(xbc, proj2, a_row, dtb_row, dsk_full, jnp.asarray(ex), jnp.asarray(sel), jnp.asarray(par))


def _ssd_bwd(xbc, proj2, dy, hsave, a_row, dtb_row, dsk_full):
    s = xbc.shape[0]
    nc = s // CHUNK
    ex, ext, sel, par, ones_blk = _ssd_consts()

    def body(xbc_ref, dtr_ref, dy_ref, hs_ref, a_ref, dtb_ref, dsk_ref, ex_ref, ext_ref, sel_ref, par_ref, ob_ref,
             dxbc_ref, ddtr_ref, dd_ref, da_ref, ddtb_ref, dh, a_dd, a_da, a_dtb, dcs_lane, dcs_b, dxdt):
        step = pl.program_id(0)

        @pl.when(step == 0)
        def _():
            dh[...] = jnp.zeros_like(dh)
            a_dd[...] = jnp.zeros_like(a_dd)
            a_da[...] = jnp.zeros_like(a_da)
            a_dtb[...] = jnp.zeros_like(a_dtb)

        xs, dt, cs, cstp, dt_full, cs_full = _ssd_common(xbc_ref, dtr_ref, a_ref, dtb_ref, ex_ref, sel_ref, par_ref)
        cs_last = cs_full[CHUNK - 1:CHUNK, :]
        xdt = xs * dt_full
        dyv = dy_ref[...]
        a_dd[...] += _fold8(dyv * xs)
        causal, first = _pair_mask()
        ones_l = jnp.ones((CHUNK, 128), f32)
        for g in range(NG):
            gl = slice(g * GW, (g + 1) * GW)
            bcol = slice(SSD_W + g * NSTATE, SSD_W + (g + 1) * NSTATE)
            ccol = slice(SSD_W + NG * NSTATE + g * NSTATE, SSD_W + NG * NSTATE + (g + 1) * NSTATE)
            bg = xbc_ref[:, bcol].astype(bf16)
            cg = xbc_ref[:, ccol].astype(bf16)
            bg2 = jnp.concatenate([bg, bg], axis=0)
            cb2 = lax.dot_general(cg, bg2, (((1,), (1,)), ((), ())), preferred_element_type=f32)
            hg = hs_ref[g]
            hgb = hg.astype(bf16)
            dhg = dh[g]
            dhgb = dhg.astype(bf16)
            eg = jnp.exp(cs_full[:, gl])
            dec = jnp.exp(cs_last[:, gl] - cs_full[:, gl])
            gam = jnp.exp(cs_last[:, gl])
            dyg = dyv[:, gl]
            xdt_g = xdt[:, gl]
            y0 = jnp.dot(cg, hgb, preferred_element_type=f32)
            dy0 = (eg * dyg).astype(bf16)
            dcm = lax.dot_general(dy0, hgb, (((1,), (1,)), ((), ())), preferred_element_type=f32)
            dh_prev = gam * dhg + lax.dot_general(cg, dy0, (((0,), (0,)), ((), ())), preferred_element_type=f32)
            dgam = jnp.sum(dhg * hg, axis=0, keepdims=True) * gam
            dxdec = jnp.dot(bg, dhgb, preferred_element_type=f32)
            dbm = lax.dot_general((xdt_g * dec).astype(bf16), dhgb, (((1,), (1,)), ((), ())), preferred_element_type=f32)
            t = dxdec * xdt_g * dec
            dcs_lane[:, gl] = dyg * eg * y0 - t
            dcs_lane[CHUNK - 1:CHUNK, gl] += jnp.sum(t, axis=0, keepdims=True) + dgam
            dxdt[:, gl] = dxdec * dec
            dcb2 = jnp.zeros((CHUNK, 128), f32)
            for j in range(GW // 128):
                pair = g * (GW // 128) + j
                pl_ = slice(pair * 128, (pair + 1) * 128)
                seg = jnp.exp(jnp.where(causal, cs_full[:, pl_] - cstp[pair:pair + 1, :], -jnp.inf))
                m = cb2 * seg
                mb = m.astype(bf16)
                rhs = _block_diag(xdt[:, pl_].astype(bf16), first)
                dyp = dyv[:, pl_].astype(bf16)
                dm = lax.dot_general(dyp, rhs, (((1,), (1,)), ((), ())), preferred_element_type=f32)
                tt = lax.dot_general(mb, dyp, (((0,), (0,)), ((), ())), preferred_element_type=f32)
                dxdt[:, pl_] += jnp.where(first, tt[0:CHUNK], tt[CHUNK:])
                dcb2 = dcb2 + dm * seg
                w = dm * m
                rsum = jnp.dot(w, ob_ref[...], precision=HIGHEST, preferred_element_type=f32)
                t2 = lax.dot_general(w, ones_l, (((0,), (0,)), ((), ())), precision=HIGHEST, preferred_element_type=f32)
                dcs_b[:, pl_] = rsum - jnp.where(first, t2[0:CHUNK], t2[CHUNK:])
            dcb2b = dcb2.astype(bf16)
            dcm = dcm + jnp.dot(dcb2b, bg2, preferred_element_type=f32)
            t3 = lax.dot_general(dcb2b, cg, (((0,), (0,)), ((), ())), preferred_element_type=f32)
            dxbc_ref[:, bcol] = dbm + t3[0:CHUNK] + t3[CHUNK:]
            dxbc_ref[:, ccol] = dcm
            dh[g] = dh_prev
        dcs = jnp.dot(dcs_lane[...] + dcs_b[...] * (1.0 / HD), ext_ref[...], precision=HIGHEST, preferred_element_type=f32)
        r_i = lax.broadcasted_iota(jnp.int32, (CHUNK, CHUNK), 0)
        c_i = lax.broadcasted_iota(jnp.int32, (CHUNK, CHUNK), 1)
        triu = (r_i <= c_i).astype(f32)
        da_ = jnp.dot(triu, dcs, precision=HIGHEST, preferred_element_type=f32)
        dxdtv = dxdt[...]
        ddt = da_ * a_ref[...] + jnp.dot(dxdtv * xs, ext_ref[...], precision=HIGHEST, preferred_element_type=f32)
        a_da[...] += _fold8(da_ * dt)
        dxbc_ref[:, 0:SSD_W] = dyv * dsk_ref[...] + dxdtv * dt_full
        ddtr = ddt * _sigmoid(dtr_ref[...] + dtb_ref[...])
        ddtr_ref[...] = ddtr
        a_dtb[...] += _fold8(ddtr)

        @pl.when(step == nc - 1)
        def _():
            dd_ref[...] = jnp.sum(jnp.dot(a_dd[...], ext_ref[...], precision=HIGHEST, preferred_element_type=f32), axis=0, keepdims=True)
            da_ref[...] = jnp.sum(a_da[...], axis=0, keepdims=True)
            ddtb_ref[...] = jnp.sum(a_dtb[...], axis=0, keepdims=True)

    rev = lambda c: nc - 1 - c
    const = lambda shape: pl.BlockSpec(shape, lambda c: tuple(0 for _ in shape))
    return pl.pallas_call(
        body, name="ssd_bwd", grid=(nc,),
        in_specs=[pl.BlockSpec((CHUNK, XBC), lambda c: (rev(c), 0)), pl.BlockSpec((CHUNK, 128), lambda c: (rev(c), DT_COL)),
                  pl.BlockSpec((CHUNK, SSD_W), lambda c: (rev(c), 0)), pl.BlockSpec((None, NG, NSTATE, GW), lambda c: (rev(c), 0, 0, 0)),
                  const((1, 128)), const((1, 128)), const((1, SSD_W)), const((128, SSD_W)), const((SSD_W, 128)),
                  const((8, 128)), const((128, 128)), const((128, 128))],
        out_specs=[pl.BlockSpec((CHUNK, XBC), lambda c: (rev(c), 0)), pl.BlockSpec((CHUNK, 128), lambda c: (rev(c), 0)),
                   const((1, 128)), const((1, 128)), const((1, 128))],
        out_shape=[_sds((s, XBC), f32), _sds((s, 128), f32), _sds((1, 128), f32), _sds((1, 128), f32), _sds((1, 128), f32)],
        scratch_shapes=[pltpu.VMEM((NG, NSTATE, GW), f32), pltpu.VMEM((8, SSD_W), f32), pltpu.VMEM((8, 128), f32), pltpu.VMEM((8, 128), f32),
                        pltpu.VMEM((CHUNK, SSD_W), f32), pltpu.VMEM((CHUNK, SSD_W), f32), pltpu.VMEM((CHUNK, SSD_W), f32)],
        compiler_params=_params(("arbitrary",)),
    )(xbc, proj2, dy, hsave, a_row, dtb_row, dsk_full, jnp.asarray(ex), jnp.asarray(ext), jnp.asarray(sel), jnp.asarray(par),
      jnp.asarray(ones_blk))


def _local_step(x, tgt, mods, g_mix, win, rel, conv_w, conv_b, dt_bias, a_log, d_skip, g_att, g_ssd, wout, g_ffn,
                wg4, wu4, wd4, g_final):
    s = x.shape[0]
    tm_e = 256 if s % 256 == 0 else s
    tm_m = 512 if s % 512 == 0 else s
    tm_l = 1024 if s % 1024 == 0 else s
    tk = 2048 if s % 2048 == 0 else s
    sh1, sc1, gt1, sh2, sc2, gt2 = [mods[:, i * D:(i + 1) * D] for i in range(6)]

    h1b = _norm_mod("norm_mod_1", x, g_mix, sc1, sh1, tm_e)
    qkv = _mm_nn_fullk("proj_qkv", h1b, win[:, :IN_A], tm_m, 768, bf16)
    proj2 = _mm_nn_fullk("proj_zxbcdt", h1b, win[:, IN_A:], tm_m, 896, f32)
    bias = _expand_bias(rel)
    att = _attn_fwd(qkv, bias)
    xbc = _ssd_conv(proj2, conv_w, conv_b, tm_e)
    a_row = jnp.pad(-jnp.exp(a_log), ((0, 0), (0, 128 - NH)))
    dtb_row = jnp.pad(dt_bias, ((0, 0), (0, 128 - NH)))
    dsk_full = jnp.repeat(d_skip, HD, axis=1)
    y, hsave = _ssd_fwd(xbc, proj2, a_row, dtb_row, dsk_full)
    mixcat = _mix_pre(att, y, proj2, g_att, g_ssd, tm_e)
    mix = _mm_nn_fullk("proj_out", mixcat, wout, tm_m, 1024, f32)
    x2, h2b = _resid_norm_mod(x, gt1, mix, g_ffn, sc2, sh2, tm_e)
    gate, up, act = _ffn_up(h2b, wg4, wu4, tm_m)
    ffn = _ffn_down(act, wd4, tm_l)

    dx3, dffn, loss, dg_final, dgt2 = _final_fwd_bwd(x2, ffn, gt2, g_final, tgt, tm_e)
    gwd4 = _grad_wdown4(act, dffn, 1024, tk)
    dgate, dup = _ffn_dact(dffn, wd4, gate, up, tm_m)
    gwg4 = _grad_cols4("grad_w_gate", h2b, dgate, 1024, tk)
    gwu4 = _grad_cols4("grad_w_up", h2b, dup, 1024, tk)
    dh2 = _ffn_dh(dgate, dup, wg4, wu4, tm_m)
    dx2, dmix, dsc2, dsh2, dg_ffn, dgt1 = _norm_mod_bwd("norm_mod_bwd_2", dh2, x2, g_ffn, sc2, dx3, tm_e, mix=mix, gt=gt1)
    gwout = _mm_tn("grad_w_out", mixcat, dmix, 1024, 1024, tk, bf16)
    dmc = _mm_nt("dmixcat", dmix, wout, tm_m, 1024, D, f32)
    datt, dy, dz, dg_att, dg_ssd = _mix_pre_bwd(dmc, att, y, proj2, g_att, g_ssd, tm_e)
    dq, dk, dv, gband = _attn_bwd(qkv, datt, bias, jnp.transpose(bias, (0, 2, 1)))
    drel = _rel_bias_grad(gband.reshape(NH, CHUNK, BANDP))
    dxbc, ddtr, dd_row, da_row, ddtb_row = _ssd_bwd(xbc, proj2, dy, hsave, a_row, dtb_row, dsk_full)
    dxbc_raw, dconv_w, dconv_b = _ssd_conv_bwd(dxbc, proj2, conv_w, conv_b, tm_e)
    dproj = jnp.concatenate([dq, dk, dv, dz, dxbc_raw, ddtr.astype(bf16)], axis=1)
    gwin = _mm_tn("grad_w_in", h1b, dproj, 1024, 1152, tk, bf16)
    dh1 = _mm_nt("dh1", dproj, win, tm_m, D, 1920, f32)
    grad_x, dsc1, dsh1, dg_mix = _norm_mod_bwd("norm_mod_bwd_1", dh1, x, g_mix, sc1, dx2, tm_e)

    dmods = jnp.concatenate([dsh1, dsc1, dgt1, dsh2, dsc2, dgt2], axis=1)
    dd_skip = dd_row[:, :NH]
    da_log = da_row[:, :NH] * a_row[:, :NH]
    small = dict(g_mix=dg_mix, conv_b=dconv_b, dt_bias=ddtb_row[:, :NH], a_log=da_log, d_skip=dd_skip, g_att_out=dg_att,
                 g_ssd_out=dg_ssd, g_ffn=dg_ffn, g_final=dg_final, rel_bias=drel, conv_w=dconv_w)
    big = dict(w_in=gwin, w_out=gwout, w_gate=gwg4, w_up=gwu4, w_down=gwd4)
    return loss[0, 0], grad_x, dmods, small, big


HBM = pl.BlockSpec(memory_space=pl.ANY)
VMEM = pl.BlockSpec(memory_space=pltpu.VMEM)


def _place():
    x, y, c = lax.axis_index("x"), lax.axis_index("y"), lax.axis_index("c")
    chips = [(1 - x, y), (x, 1 - y), (1 - x, 1 - y)]
    return x, y, c, chips


def _allgather8(name, payload):
    r = payload.shape[0]

    def body(x_ref, out_ref, send_sems, recv_sems, local_sem):
        x, y, c, chips = _place()
        me, sibling = (x, y, c), (x, y, 1 - c)

        def slot(px, py, pc):
            return out_ref.at[4 * px + 2 * py + pc]

        def copy(k, block, to, src=None):
            return pltpu.make_async_remote_copy(
                src_ref=slot(*block) if src is None else src, dst_ref=slot(*block),
                send_sem=send_sems.at[k], recv_sem=recv_sems.at[k], device_id=to, device_id_type=MESH)

        mine = pltpu.make_async_copy(x_ref, slot(*me), local_sem)
        mine.start()
        first = [copy(0, me, sibling, src=x_ref)]
        first += [copy(1 + j, me, (*chip, c), src=x_ref) for j, chip in enumerate(chips)]
        for cp in first:
            cp.start()
        passed = [copy(4 + j, (*chip, c), sibling) for j, chip in enumerate(chips)]
        for j, chip in enumerate(chips):
            copy(1 + j, (*chip, c), me).wait_recv()
            passed[j].start()
        copy(0, sibling, me).wait_recv()
        for j, chip in enumerate(chips):
            copy(4 + j, (*chip, 1 - c), me).wait_recv()
        for cp in first + passed:
            cp.wait_send()
        mine.wait()

    return pl.pallas_call(
        body, name=name, out_shape=_sds((N_DEV, r, 128), f32), in_specs=[VMEM], out_specs=VMEM,
        scratch_shapes=[pltpu.SemaphoreType.DMA((7,)), pltpu.SemaphoreType.DMA((7,)), pltpu.SemaphoreType.DMA],
    )(payload)


def _sum8(g):
    r = g.shape[1]

    def body(g_ref, o_ref):
        acc = g_ref[0]
        for i in range(1, N_DEV):
            acc = acc + g_ref[i]
        o_ref[...] = acc

    return pl.pallas_call(body, name="sum8", out_shape=_sds((r, 128), f32))(g)


def _gather_weights(shards):
    nw = len(shards)

    def body(*refs):
        ins, outs = refs[:nw], refs[nw:2 * nw]
        st_a, st_b, st_c = refs[2 * nw:3 * nw], refs[3 * nw:4 * nw], refs[4 * nw:5 * nw]
        send_sems, recv_sems, load_sems, store_sems = refs[5 * nw:]
        x, y, c, chips = _place()
        k = 2 * x + y
        sibling = (x, y, 1 - c)

        def half(w, kk, hh, sem, to, src):
            return pltpu.make_async_remote_copy(src_ref=src, dst_ref=outs[w].at[kk, hh], send_sem=send_sems.at[w, sem],
                                                recv_sem=recv_sems.at[w, sem], device_id=to, device_id_type=MESH)

        ld_a = [pltpu.make_async_copy(ins[w].at[c], st_a[w], load_sems.at[w, 0]) for w in range(nw)]
        ld_b = [pltpu.make_async_copy(ins[w].at[1 - c], st_b[w], load_sems.at[w, 1]) for w in range(nw)]
        for cp in ld_a + ld_b:
            cp.start()
        sends, stores = [], []
        for w in range(nw):
            ld_a[w].wait()
            for j, chip in enumerate(chips):
                sends.append(half(w, k, c, j, (*chip, c), st_a[w]))
                sends[-1].start()
            stores.append(pltpu.make_async_copy(st_a[w], outs[w].at[k, c], store_sems.at[w, 0]))
            stores[-1].start()
        st_own = []
        for w in range(nw):
            ld_b[w].wait()
            st_own.append(pltpu.make_async_copy(st_b[w], outs[w].at[k, 1 - c], store_sems.at[w, 1]))
            st_own[-1].start()
        for cp in st_own:
            cp.wait()
        fwds = {}
        for j, (px, py) in enumerate(chips):
            kq = 2 * px + py
            for w in range(nw):
                slot = st_b[w] if j % 2 == 0 else st_c[w]
                half(w, kq, c, j, (x, y, c), slot).wait_recv()
                if j == 2:
                    fwds[w, 0].wait_send()
                ld = pltpu.make_async_copy(outs[w].at[kq, c], slot, load_sems.at[w, 2 + j])
                ld.start()
                ld.wait()
                fwds[w, j] = half(w, kq, c, 3 + j, sibling, slot)
                fwds[w, j].start()
        for j, (px, py) in enumerate(chips):
            for w in range(nw):
                half(w, 2 * px + py, 1 - c, 3 + j, (x, y, c), st_c[w]).wait_recv()
        for cp in sends:
            cp.wait_send()
        for w in range(nw):
            fwds[w, 1].wait_send()
            fwds[w, 2].wait_send()
        for cp in stores:
            cp.wait()

    stage = [pltpu.VMEM(s.shape[1:], bf16) for s in shards]
    return pl.pallas_call(
        body, name="gather_weights", out_shape=[_sds((NSH,) + s.shape, bf16) for s in shards],
        in_specs=[HBM] * nw, out_specs=[HBM] * nw,
        scratch_shapes=stage * 3 + [pltpu.SemaphoreType.DMA((nw, 6)), pltpu.SemaphoreType.DMA((nw, 6)), pltpu.SemaphoreType.DMA((nw, 5)),
                                    pltpu.SemaphoreType.DMA((nw, 2))],
        compiler_params=pltpu.CompilerParams(vmem_limit_bytes=VMEM_LIMIT),
    )(*shards)


def _rs_pair_exchange(grads):
    nw = len(grads)

    def body(*refs):
        ins, got, stage = refs[:nw], refs[nw:2 * nw], refs[2 * nw:3 * nw]
        send_sems, recv_sems, load_sems = refs[3 * nw:]
        x, y, c, _ = _place()

        def load(w, kk):
            return pltpu.make_async_copy(ins[w].at[kk, 1 - c], stage[w].at[kk % 2], load_sems.at[w, kk])

        def send(w, kk):
            return pltpu.make_async_remote_copy(src_ref=stage[w].at[kk % 2], dst_ref=got[w].at[kk], send_sem=send_sems.at[w, kk],
                                                recv_sem=recv_sems.at[w, kk], device_id=(x, y, 1 - c), device_id_type=MESH)

        for kk in range(2):
            for w in range(nw):
                load(w, kk).start()
        for kk in range(NSH):
            for w in range(nw):
                load(w, kk).wait()
                send(w, kk).start()
            if kk + 2 < NSH:
                for w in range(nw):
                    send(w, kk).wait_send()
                    load(w, kk + 2).start()
        for kk in range(NSH - 2, NSH):
            for w in range(nw):
                send(w, kk).wait_send()
        for kk in range(NSH):
            for w in range(nw):
                send(w, kk).wait_recv()

    return pl.pallas_call(
        body, name="rs_pair_exchange", out_shape=[_sds((NSH,) + g.shape[2:], bf16) for g in grads], in_specs=[HBM] * nw, out_specs=[HBM] * nw,
        scratch_shapes=[pltpu.VMEM((2,) + g.shape[2:], bf16) for g in grads]
        + [pltpu.SemaphoreType.DMA((nw, NSH)), pltpu.SemaphoreType.DMA((nw, NSH)), pltpu.SemaphoreType.DMA((nw, NSH))],
        compiler_params=pltpu.CompilerParams(vmem_limit_bytes=VMEM_LIMIT),
    )(*grads)


def _rs_chip_exchange(sums):
    nw = len(sums)

    def body(*refs):
        ins, outs, stage = refs[:nw], refs[nw:2 * nw], refs[2 * nw:3 * nw]
        send_sems, recv_sems, load_sems, local_sems = refs[3 * nw:]
        x, y, c, chips = _place()
        k = 2 * x + y
        slabs = [2 * px + py for px, py in chips] + [k]

        def load(w, j):
            return pltpu.make_async_copy(ins[w].at[slabs[j]], stage[w].at[slabs[j]], load_sems.at[w, j])

        for j in range(NSH):
            for w in range(nw):
                load(w, j).start()
        cps = []
        for j, (px, py) in enumerate(chips):
            for w in range(nw):
                load(w, j).wait()
                cps.append(pltpu.make_async_remote_copy(src_ref=stage[w].at[slabs[j]], dst_ref=outs[w].at[k], send_sem=send_sems.at[w, j],
                                                        recv_sem=recv_sems.at[w, j], device_id=(px, py, c), device_id_type=MESH))
                cps[-1].start()
        local = []
        for w in range(nw):
            load(w, NSH - 1).wait()
            local.append(pltpu.make_async_copy(stage[w].at[k], outs[w].at[k], local_sems.at[w]))
            local[-1].start()
        for w in range(nw):
            for j, (px, py) in enumerate(chips):
                pltpu.make_async_remote_copy(src_ref=stage[w].at[k], dst_ref=outs[w].at[2 * px + py], send_sem=send_sems.at[w, j],
                                             recv_sem=recv_sems.at[w, j], device_id=(px, py, c), device_id_type=MESH).wait_recv()
        for cp in cps:
            cp.wait_send()
        for cp in local:
            cp.wait()

    return pl.pallas_call(
        body, name="rs_chip_exchange", out_shape=[_sds(s.shape, bf16) for s in sums], in_specs=[HBM] * nw, out_specs=[HBM] * nw,
        scratch_shapes=[pltpu.VMEM(s.shape, bf16) for s in sums]
        + [pltpu.SemaphoreType.DMA((nw, 3)), pltpu.SemaphoreType.DMA((nw, 3)), pltpu.SemaphoreType.DMA((nw, NSH)), pltpu.SemaphoreType.DMA((nw,))],
        compiler_params=pltpu.CompilerParams(vmem_limit_bytes=VMEM_LIMIT),
    )(*sums)


def _rs_pair_gather(halves):
    nw = len(halves)

    def body(*refs):
        ins, outs, stage = refs[:nw], refs[nw:2 * nw], refs[2 * nw:3 * nw]
        send_sems, recv_sems, local_sems, stage_sems = refs[3 * nw:]
        x, y, c, _ = _place()
        loads = [pltpu.make_async_copy(ins[w], stage[w], stage_sems.at[w]) for w in range(nw)]
        for cp in loads:
            cp.start()
        local, cps = [], []
        for w in range(nw):
            loads[w].wait()
            local.append(pltpu.make_async_copy(stage[w], outs[w].at[c], local_sems.at[w]))
            cps.append(pltpu.make_async_remote_copy(src_ref=stage[w], dst_ref=outs[w].at[c], send_sem=send_sems.at[w],
                                                    recv_sem=recv_sems.at[w], device_id=(x, y, 1 - c), device_id_type=MESH))
            local[w].start()
            cps[w].start()
        for w in range(nw):
            pltpu.make_async_remote_copy(src_ref=stage[w], dst_ref=outs[w].at[1 - c], send_sem=send_sems.at[w], recv_sem=recv_sems.at[w],
                                         device_id=(x, y, 1 - c), device_id_type=MESH).wait_recv()
        for cp in cps:
            cp.wait_send()
        for cp in local:
            cp.wait()

    return pl.pallas_call(
        body, name="rs_pair_gather", out_shape=[_sds((2,) + h.shape, f32) for h in halves], in_specs=[HBM] * nw, out_specs=[HBM] * nw,
        scratch_shapes=[pltpu.VMEM(h.shape, f32) for h in halves]
        + [pltpu.SemaphoreType.DMA((nw,)), pltpu.SemaphoreType.DMA((nw,)), pltpu.SemaphoreType.DMA((nw,)), pltpu.SemaphoreType.DMA((nw,))],
        compiler_params=pltpu.CompilerParams(vmem_limit_bytes=VMEM_LIMIT),
    )(*halves)


def _row_tile(r, c, nbuf):
    budget = 24 * 1024 * 1024 // (2 * nbuf * 4 * c)
    t = 8
    while t * 2 <= budget and r % (t * 2) == 0:
        t *= 2
    return t


def _cast_bf16(name, a):
    r, c = a.shape
    tr = _row_tile(r, c, 2)

    def body(a_ref, o_ref):
        o_ref[...] = a_ref[...].astype(bf16)

    spec = pl.BlockSpec((tr, c), lambda i: (i, 0))
    return pl.pallas_call(body, name=name, grid=(r // tr,), in_specs=[spec], out_specs=spec, out_shape=_sds((r, c), bf16),
                          compiler_params=_params(("parallel",)))(a)


def _pair_sum(name, core, grads, got):
    _, _, rh, c = grads.shape
    tr = _row_tile(rh, c, 2)

    def body(c_ref, a_ref, b_ref, o_ref):
        o_ref[...] = (a_ref[...].astype(f32) + b_ref[...].astype(f32)).astype(bf16)

    spec = pl.BlockSpec((None, tr, c), lambda k, i, c_ref: (k, i, 0))
    return pl.pallas_call(
        body, name=name, out_shape=_sds((NSH, rh, c), bf16),
        grid_spec=pltpu.PrefetchScalarGridSpec(
            num_scalar_prefetch=1, grid=(NSH, rh // tr),
            in_specs=[pl.BlockSpec((None, None, tr, c), lambda k, i, c_ref: (k, c_ref[0], i, 0)), spec], out_specs=spec),
        compiler_params=_params(("parallel", "parallel")))(core, grads, got)


def _chip_sum(name, parts):
    _, rh, c = parts.shape
    tr = _row_tile(rh, c, 3)

    def body(p_ref, o_ref):
        acc = p_ref[0].astype(f32)
        for k in range(1, NSH):
            acc = acc + p_ref[k].astype(f32)
        o_ref[...] = acc

    return pl.pallas_call(body, name=name, grid=(rh // tr,), in_specs=[pl.BlockSpec((NSH, tr, c), lambda i: (0, i, 0))],
                          out_specs=pl.BlockSpec((tr, c), lambda i: (i, 0)), out_shape=_sds((rh, c), f32),
                          compiler_params=_params(("parallel",)))(parts)


def _mods_part(cond16, w_ada, b_part):
    n = w_ada.shape[1]
    tn = 512

    def body(c_ref, w_ref, b_ref, o_ref):
        cv = c_ref[...]
        o_ref[...] = _dot(cv * _sigmoid(cv), w_ref[...]) + b_ref[...]

    return pl.pallas_call(
        body, name="mods_part", grid=(n // tn,),
        in_specs=[pl.BlockSpec((16, D), lambda j: (0, 0)), pl.BlockSpec((D, tn), lambda j: (0, j)), pl.BlockSpec((1, tn), lambda j: (0, j))],
        out_specs=pl.BlockSpec((16, tn), lambda j: (0, j)), out_shape=_sds((16, n), f32), compiler_params=_params(("parallel",)),
    )(cond16, w_ada, b_part)


def _grad_w_ada(cond16, dm16):
    n = dm16.shape[1]
    tr = 256

    def body(c_ref, d_ref, o_ref):
        cv = c_ref[...]
        o_ref[...] = _dot(cv * _sigmoid(cv), d_ref[...], ta=True)

    return pl.pallas_call(
        body, name="grad_w_ada", grid=(D // tr,),
        in_specs=[pl.BlockSpec((16, tr), lambda i: (0, i)), pl.BlockSpec((16, n), lambda i: (0, 0))],
        out_specs=pl.BlockSpec((tr, n), lambda i: (i, 0)), out_shape=_sds((D, n), f32), compiler_params=_params(("parallel",)),
    )(cond16, dm16)


def _adamw(name, w, g, m, v):
    r, c = w.shape
    tr = _row_tile(r, c, 7)

    def body(w_ref, g_ref, m_ref, v_ref, d_ref, nm_ref, nv_ref):
        gv = g_ref[...]
        nm = ADAM_B1 * m_ref[...] + (1.0 - ADAM_B1) * gv
        nv = ADAM_B2 * v_ref[...] + (1.0 - ADAM_B2) * (gv * gv)
        nm_ref[...] = nm
        nv_ref[...] = nv
        m_hat = nm / (1.0 - ADAM_B1 ** ADAM_STEP)
        v_hat = nv / (1.0 - ADAM_B2 ** ADAM_STEP)
        d_ref[...] = -ADAM_LR * (m_hat / (jnp.sqrt(v_hat) + ADAM_EPS) + ADAM_WD * w_ref[...])

    spec = pl.BlockSpec((tr, c), lambda i: (i, 0))
    return pl.pallas_call(body, name=name, grid=(r // tr,), in_specs=[spec] * 4, out_specs=[spec] * 3, out_shape=[_sds((r, c), f32)] * 3,
                          compiler_params=_params(("parallel",)))(w, g, m, v)


def _pack(parts, rows):
    flat = []
    for p in parts:
        p = p.reshape(-1)
        flat.append(jnp.pad(p, (0, (-p.shape[0]) % 128)))
    v = jnp.concatenate(flat)
    return jnp.pad(v, (0, rows * 128 - v.shape[0])).reshape(rows, 128)


def _unpack(packed, sizes):
    lead = packed.shape[:-2]
    flat = packed.reshape(lead + (-1,))
    out, off = [], 0
    for n in sizes:
        out.append(flat[..., off:off + n])
        off += n + (-n) % 128
    return out


BIG = ("w_in", "w_out", "w_gate", "w_up", "w_down")
SMALL = ("b_ada", "g_mix", "conv_b", "dt_bias", "a_log", "d_skip", "g_att_out", "g_ssd_out", "g_ffn", "g_final", "rel_bias", "conv_w")
ORDER = ("w_ada", "b_ada", "g_mix", "w_in", "rel_bias", "conv_w", "conv_b", "dt_bias", "a_log", "d_skip", "g_att_out", "g_ssd_out",
         "w_out", "g_ffn", "w_gate", "w_up", "w_down", "g_final")
REL_SH = N_REL // NSH
CONVW_SH = XBC // NSH
ADA_SH = 6 * D // NSH


def kernel(x, c, w_ada, b_ada, g_mix, w_in, rel_bias, conv_w, conv_b, dt_bias, a_log, d_skip, g_att_out, g_ssd_out, w_out, g_ffn, w_gate, w_up, w_down, g_final, loss_target, m_w_ada, m_b_ada, m_g_mix, m_w_in, m_rel_bias, m_conv_w, m_conv_b, m_dt_bias, m_a_log, m_d_skip, m_g_att_out, m_g_ssd_out, m_w_out, m_g_ffn, m_w_gate, m_w_up, m_w_down, m_g_final, v_w_ada, v_b_ada, v_g_mix, v_w_in, v_rel_bias, v_conv_w, v_conv_b, v_dt_bias, v_a_log, v_d_skip, v_g_att_out, v_g_ssd_out, v_w_out, v_g_ffn, v_w_gate, v_w_up, v_w_down, v_g_final):
    args = dict(locals())
    w = {n: args[n] for n in ORDER}
    m = {n: args["m_" + n] for n in ORDER}
    v = {n: args["v_" + n] for n in ORDER}
    ix, iy, ic = lax.axis_index("x"), lax.axis_index("y"), lax.axis_index("c")
    chip = 2 * ix + iy
    dev = 2 * chip + ic
    s = x.shape[1]

    shards = [
        _cast_bf16("cast_w_in", jnp.pad(w_in[0], ((0, 0), (0, IN_SHP - IN_SH)))).reshape(2, D // 2, IN_SHP),
        _cast_bf16("cast_w_out", w_out[0]).reshape(2, D // NSH // 2, D),
        _cast_bf16("cast_w_gate", w_gate[0]).reshape(2, D // 2, FSH),
        _cast_bf16("cast_w_up", w_up[0]).reshape(2, D // 2, FSH),
        _cast_bf16("cast_w_down", w_down[0]).reshape(2, FSH // 2, D),
    ]
    win4, wout4, wg4, wu4, wd4 = _gather_weights(shards)
    win = jnp.transpose(win4.reshape(NSH, D, IN_SHP)[:, :, :IN_SH], (1, 0, 2)).reshape(D, IN_COLS)
    win = jnp.pad(win, ((0, 0), (0, IN_P - IN_COLS)))
    wout = wout4.reshape(D, D)
    wg4 = wg4.reshape(NSH, D, FSH)
    wu4 = wu4.reshape(NSH, D, FSH)
    wd4 = wd4.reshape(NSH, FSH, D)

    g1 = _allgather8("gather_inputs", _pack([c[0], rel_bias[0], conv_w[0]], 40))
    c_all, rel_sh, convw_sh = _unpack(g1, [D, NH * REL_SH, 4 * CONVW_SH])
    rel_full = jnp.concatenate([rel_sh[2 * k].reshape(NH, REL_SH) for k in range(NSH)], axis=1)
    convw_full = jnp.concatenate([convw_sh[2 * k].reshape(4, CONVW_SH) for k in range(NSH)], axis=1)
    cond16 = jnp.pad(c_all, ((0, 8), (0, 0)))
    b_part = lax.dynamic_slice_in_dim(b_ada, chip * ADA_SH, ADA_SH, axis=1)
    mods_part = _mods_part(cond16, w_ada[0], b_part)[:N_DEV]
    g2 = _allgather8("gather_mods", mods_part.reshape(N_DEV * ADA_SH // 128, 128))
    mods_all = jnp.concatenate([g2[2 * k].reshape(N_DEV, ADA_SH) for k in range(NSH)], axis=1)
    mods = lax.dynamic_slice_in_dim(mods_all, dev, 1, axis=0)

    loss, grad_x, dmods, small, big = _local_step(
        x[0], loss_target[0], mods, g_mix, win, rel_full, convw_full, conv_b, dt_bias, a_log, d_skip, g_att_out, g_ssd_out, wout, g_ffn,
        wg4, wu4, wd4, g_final[None, :])

    small_names = ("g_mix", "conv_b", "dt_bias", "a_log", "d_skip", "g_att_out", "g_ssd_out", "g_ffn", "g_final", "rel_bias", "conv_w")
    g3 = _allgather8("gather_small_grads", _pack([dmods] + [small[n] for n in small_names], 264))
    sizes = [6 * D] + [int(np.prod(small[n].shape)) for n in small_names]
    dmods_all = _unpack(g3, sizes)[0]
    summed = _unpack(_sum8(g3), sizes)
    grads = {"b_ada": summed[0].reshape(1, 6 * D)}
    for n, val in zip(small_names, summed[1:]):
        grads[n] = val.reshape(small[n].shape)
    grads["rel_bias"] = lax.dynamic_slice_in_dim(grads["rel_bias"], chip * REL_SH, REL_SH, axis=1)
    grads["conv_w"] = lax.dynamic_slice_in_dim(grads["conv_w"], chip * CONVW_SH, CONVW_SH, axis=1)
    grads["g_final"] = grads["g_final"].reshape(D)
    dm16 = jnp.pad(lax.dynamic_slice_in_dim(dmods_all, chip * ADA_SH, ADA_SH, axis=1), ((0, 8), (0, 0)))
    grads["w_ada"] = _grad_w_ada(cond16, dm16)

    gwin4 = jnp.stack([jnp.pad(big["w_in"][:, k * IN_SH:(k + 1) * IN_SH], ((0, 0), (0, IN_SHP - IN_SH))) for k in range(NSH)])
    stacked = [gwin4, big["w_out"].reshape(NSH, D // NSH, D), big["w_gate"], big["w_up"], big["w_down"]]
    stacked = [g.reshape(NSH, 2, g.shape[1] // 2, g.shape[2]) for g in stacked]
    got = _rs_pair_exchange(stacked)
    core = jnp.reshape(ic, (1,)).astype(jnp.int32)
    sums = [_pair_sum("pair_sum_" + n, core, o, g) for n, o, g in zip(BIG, stacked, got)]
    parts = _rs_chip_exchange(sums)
    halves = [_chip_sum("chip_sum_" + n, p) for n, p in zip(BIG, parts)]
    full = _rs_pair_gather(halves)
    for n, f in zip(BIG, full):
        grads[n] = f.reshape(2 * f.shape[1], f.shape[2])
    grads["w_in"] = grads["w_in"][:, :IN_SH]

    delta, new_m, new_v = {}, {}, {}
    for n in ("w_ada",) + BIG:
        delta[n], new_m[n], new_v[n] = _adamw("adamw_" + n, w[n][0], grads[n], m[n][0], v[n][0])
    sw = _pack([w[n] for n in SMALL], 200)
    sg = _pack([grads[n] for n in SMALL], 200)
    sm = _pack([m[n] for n in SMALL], 200)
    sv = _pack([v[n] for n in SMALL], 200)
    ssz = [int(np.prod(w[n].shape)) for n in SMALL]
    for dst, packed in zip((delta, new_m, new_v), _adamw("adamw_small", sw, sg, sm, sv)):
        for n, val in zip(SMALL, _unpack(packed, ssz)):
            dst[n] = val

    def shaped(d, n):
        return d[n].reshape(w[n].shape)

    total = lax.psum(loss, ("x", "y", "c"))
    return (total, grad_x[None], *[shaped(grads, n) for n in ORDER], *[shaped(delta, n) for n in ORDER],
            *[shaped(new_m, n) for n in ORDER], *[shaped(new_v, n) for n in ORDER])
```

```python
import functools

import numpy as np
import jax
import jax.numpy as jnp
from jax import lax
from jax.experimental import pallas as pl
from jax.experimental.pallas import tpu as pltpu

f32 = jnp.float32
bf16 = jnp.bfloat16
HIGHEST = lax.Precision.HIGHEST
MESH = pl.DeviceIdType.MESH

D = 2048
CHUNK = 64
LEFT = 8
BAND = (LEFT + 1) * CHUNK
BANDP = 640
PADK = LEFT * CHUNK
NH = 16
HD = 64
ATT_W = NH * HD
SSD_W = 1024
NG = 2
NSTATE = 128
GW = SSD_W // NG
XBC = SSD_W + 2 * NG * NSTATE
N_REL = 320
REL_CLIP = 256
FFN = 5632
NSH = 4
FSH = FFN // NSH
IN_COLS = 5648
IN_SH = IN_COLS // NSH
IN_SHP = 1536
IN_A = 3 * ATT_W
IN_B = 2688
IN_P = IN_A + IN_B
EPS = 1e-6
N_DEV = 8

ADAM_LR = 0.001
ADAM_B1 = 0.9
ADAM_B2 = 0.999
ADAM_EPS = 1e-08
ADAM_WD = 0.01
ADAM_STEP = 10

VMEM_LIMIT = 56 * 1024 * 1024


def _params(sem):
    return pltpu.CompilerParams(dimension_semantics=sem, vmem_limit_bytes=VMEM_LIMIT)


def _sds(shape, dtype):
    return jax.ShapeDtypeStruct(shape, dtype)


def _fold8(v):
    r, w = v.shape
    return jnp.sum(v.reshape(r // 8, 8, w), axis=0)


def _sigmoid(v):
    return 1.0 / (1.0 + jnp.exp(-v))


def _softplus(v):
    return jnp.maximum(v, 0.0) + jnp.log(1.0 + jnp.exp(-jnp.abs(v)))


def _dot(a, b, ta=False, tb=False):
    dn = (((0 if ta else 1,), (1 if tb else 0,)), ((), ()))
    return lax.dot_general(a.astype(bf16), b.astype(bf16), dn, preferred_element_type=f32)


def _matmul(name, a, b, *, grid, a_spec, b_spec, o_spec, o_shape, o_dtype, acc_shape, ta=False, tb=False):
    nk = grid[2]

    def body(a_ref, b_ref, o_ref, acc_ref):
        p = _dot(a_ref[...], b_ref[...], ta, tb)
        if nk == 1:
            o_ref[...] = p.astype(o_ref.dtype)
        else:
            k = pl.program_id(2)

            @pl.when(k == 0)
            def _():
                acc_ref[...] = p

            @pl.when(jnp.logical_and(k > 0, k < nk - 1))
            def _():
                acc_ref[...] += p

            @pl.when(k == nk - 1)
            def _():
                o_ref[...] = (acc_ref[...] + p).astype(o_ref.dtype)

    return pl.pallas_call(
        body, name=name, grid=grid, in_specs=[a_spec, b_spec], out_specs=o_spec,
        out_shape=_sds(o_shape, o_dtype), scratch_shapes=[pltpu.VMEM(acc_shape if nk > 1 else (8, 128), f32)],
        compiler_params=_params(("parallel", "parallel", "arbitrary")),
    )(a, b)


def _mm_nn_fullk(name, a, b, tm, tn, o_dtype):
    m, k = a.shape
    n = b.shape[1]
    return _matmul(name, a, b, grid=(m // tm, n // tn, 1),
                   a_spec=pl.BlockSpec((tm, k), lambda i, j, kk: (i, 0)),
                   b_spec=pl.BlockSpec((k, tn), lambda i, j, kk: (0, j)),
                   o_spec=pl.BlockSpec((tm, tn), lambda i, j, kk: (i, j)),
                   o_shape=(m, n), o_dtype=o_dtype, acc_shape=(tm, tn))


def _mm_nt(name, a, b, tm, tn, tk, o_dtype):
    m, k = a.shape
    n = b.shape[0]
    return _matmul(name, a, b, grid=(m // tm, n // tn, k // tk), tb=True,
                   a_spec=pl.BlockSpec((tm, tk), lambda i, j, kk: (i, kk)),
                   b_spec=pl.BlockSpec((tn, tk), lambda i, j, kk: (j, kk)),
                   o_spec=pl.BlockSpec((tm, tn), lambda i, j, kk: (i, j)),
                   o_shape=(m, n), o_dtype=o_dtype, acc_shape=(tm, tn))


def _mm_tn(name, a, b, tm, tn, tk, o_dtype):
    k, m = a.shape
    n = b.shape[1]
    return _matmul(name, a, b, grid=(m // tm, n // tn, k // tk), ta=True,
                   a_spec=pl.BlockSpec((tk, tm), lambda i, j, kk: (kk, i)),
                   b_spec=pl.BlockSpec((tk, tn), lambda i, j, kk: (kk, j)),
                   o_spec=pl.BlockSpec((tm, tn), lambda i, j, kk: (i, j)),
                   o_shape=(m, n), o_dtype=o_dtype, acc_shape=(tm, tn))


def _ffn_up(h2b, wg4, wu4, tm):
    s = h2b.shape[0]

    def body(h_ref, wg_ref, wu_ref, g_ref, u_ref, a_ref):
        h = h_ref[...]
        g = _dot(h, wg_ref[...])
        u = _dot(h, wu_ref[...])
        g_ref[...] = g
        u_ref[...] = u
        a_ref[...] = (g * _sigmoid(g) * u).astype(bf16)

    wspec = pl.BlockSpec((None, D, FSH), lambda k, i: (k, 0, 0))
    ospec = pl.BlockSpec((tm, FSH), lambda k, i: (i, k))
    return pl.pallas_call(
        body, name="ffn_up", grid=(NSH, s // tm),
        in_specs=[pl.BlockSpec((tm, D), lambda k, i: (i, 0)), wspec, wspec],
        out_specs=[ospec, ospec, ospec],
        out_shape=[_sds((s, FFN), f32), _sds((s, FFN), f32), _sds((s, FFN), bf16)],
        compiler_params=_params(("parallel", "parallel")),
    )(h2b, wg4, wu4)


def _ffn_down(act, wd4, tm):
    s = act.shape[0]
    return _matmul("ffn_down", act, wd4, grid=(s // tm, 1, NSH),
                   a_spec=pl.BlockSpec((tm, FSH), lambda i, j, k: (i, k)),
                   b_spec=pl.BlockSpec((None, FSH, D), lambda i, j, k: (k, 0, 0)),
                   o_spec=pl.BlockSpec((tm, D), lambda i, j, k: (i, 0)),
                   o_shape=(s, D), o_dtype=f32, acc_shape=(tm, D))


def _ffn_dact(dffn, wd4, gate, up, tm):
    s = dffn.shape[0]

    def body(d_ref, w_ref, g_ref, u_ref, dg_ref, du_ref):
        dact = _dot(d_ref[...], w_ref[...], tb=True)
        g = g_ref[...]
        sg = _sigmoid(g)
        dg_ref[...] = (dact * u_ref[...] * (sg * (1.0 + g * (1.0 - sg)))).astype(bf16)
        du_ref[...] = (dact * (g * sg)).astype(bf16)

    blk = pl.BlockSpec((tm, FSH), lambda k, i: (i, k))
    return pl.pallas_call(
        body, name="ffn_dact", grid=(NSH, s // tm),
        in_specs=[pl.BlockSpec((tm, D), lambda k, i: (i, 0)), pl.BlockSpec((None, FSH, D), lambda k, i: (k, 0, 0)), blk, blk],
        out_specs=[blk, blk], out_shape=[_sds((s, FFN), bf16), _sds((s, FFN), bf16)],
        compiler_params=_params(("parallel", "parallel")),
    )(dffn, wd4, gate, up)


def _ffn_dh(dgate, dup, wg4, wu4, tm):
    s = dgate.shape[0]

    def body(dg_ref, du_ref, wg_ref, wu_ref, o_ref, acc_ref):
        k = pl.program_id(1)
        p = _dot(dg_ref[...], wg_ref[...], tb=True) + _dot(du_ref[...], wu_ref[...], tb=True)

        @pl.when(k == 0)
        def _():
            acc_ref[...] = p

        @pl.when(jnp.logical_and(k > 0, k < NSH - 1))
        def _():
            acc_ref[...] += p

        @pl.when(k == NSH - 1)
        def _():
            o_ref[...] = acc_ref[...] + p

    aspec = pl.BlockSpec((tm, FSH), lambda i, k: (i, k))
    wspec = pl.BlockSpec((None, D, FSH), lambda i, k: (k, 0, 0))
    return pl.pallas_call(
        body, name="ffn_dh", grid=(s // tm, NSH), in_specs=[aspec, aspec, wspec, wspec],
        out_specs=pl.BlockSpec((tm, D), lambda i, k: (i, 0)), out_shape=_sds((s, D), f32),
        scratch_shapes=[pltpu.VMEM((tm, D), f32)], compiler_params=_params(("parallel", "arbitrary")),
    )(dgate, dup, wg4, wu4)


def _grad_cols4(name, h, dy, tm, tk):
    s = h.shape[0]
    return _matmul(name, h, dy, grid=(NSH, D // tm, s // tk), ta=True,
                   a_spec=pl.BlockSpec((tk, tm), lambda k, i, kk: (kk, i)),
                   b_spec=pl.BlockSpec((tk, FSH), lambda k, i, kk: (kk, k)),
                   o_spec=pl.BlockSpec((None, tm, FSH), lambda k, i, kk: (k, i, 0)),
                   o_shape=(NSH, D, FSH), o_dtype=bf16, acc_shape=(tm, FSH))


def _grad_wdown4(act, dffn, tn, tk):
    s = act.shape[0]
    return _matmul("grad_w_down", act, dffn, grid=(NSH, D // tn, s // tk), ta=True,
                   a_spec=pl.BlockSpec((tk, FSH), lambda k, j, kk: (kk, k)),
                   b_spec=pl.BlockSpec((tk, tn), lambda k, j, kk: (kk, j)),
                   o_spec=pl.BlockSpec((None, FSH, tn), lambda k, j, kk: (k, 0, j)),
                   o_shape=(NSH, FSH, D), o_dtype=bf16, acc_shape=(FSH, tn))


def _row_spec(w):
    return pl.BlockSpec((1, w), lambda i: (0, 0))


def _tile_spec(tm, w, col=0):
    return pl.BlockSpec((tm, w), lambda i: (i, col))


def _norm_mod(name, x, g, sc, sh, tm):
    s = x.shape[0]

    def body(x_ref, g_ref, sc_ref, sh_ref, o_ref):
        xv = x_ref[...]
        r = lax.rsqrt(jnp.mean(xv * xv, axis=-1, keepdims=True) + EPS)
        o_ref[...] = (xv * r * g_ref[...] * (1.0 + sc_ref[...]) + sh_ref[...]).astype(bf16)

    return pl.pallas_call(
        body, name=name, grid=(s // tm,), in_specs=[_tile_spec(tm, D), _row_spec(D), _row_spec(D), _row_spec(D)],
        out_specs=_tile_spec(tm, D), out_shape=_sds((s, D), bf16), compiler_params=_params(("parallel",)),
    )(x, g, sc, sh)


def _resid_norm_mod(x, gt, mix, g, sc, sh, tm):
    s = x.shape[0]

    def body(x_ref, gt_ref, m_ref, g_ref, sc_ref, sh_ref, x2_ref, h_ref):
        xv = x_ref[...] + gt_ref[...] * m_ref[...]
        x2_ref[...] = xv
        r = lax.rsqrt(jnp.mean(xv * xv, axis=-1, keepdims=True) + EPS)
        h_ref[...] = (xv * r * g_ref[...] * (1.0 + sc_ref[...]) + sh_ref[...]).astype(bf16)

    return pl.pallas_call(
        body, name="resid_norm_mod", grid=(s // tm,),
        in_specs=[_tile_spec(tm, D), _row_spec(D), _tile_spec(tm, D), _row_spec(D), _row_spec(D), _row_spec(D)],
        out_specs=[_tile_spec(tm, D), _tile_spec(tm, D)], out_shape=[_sds((s, D), f32), _sds((s, D), bf16)],
        compiler_params=_params(("parallel",)),
    )(x, gt, mix, g, sc, sh)


def _final_fwd_bwd(x2, ffn, gt2, g, tgt, tm):
    s = x2.shape[0]
    n = s // tm

    def body(x_ref, f_ref, gt_ref, g_ref, t_ref, dx_ref, df_ref, loss_ref, dg_ref, dgt_ref, a_loss, a_dg, a_dgt):
        i = pl.program_id(0)

        @pl.when(i == 0)
        def _():
            a_loss[...] = jnp.zeros_like(a_loss)
            a_dg[...] = jnp.zeros_like(a_dg)
            a_dgt[...] = jnp.zeros_like(a_dgt)

        fv = f_ref[...]
        gt = gt_ref[...]
        gv = g_ref[...]
        xv = x_ref[...] + gt * fv
        r = lax.rsqrt(jnp.mean(xv * xv, axis=-1, keepdims=True) + EPS)
        xh = xv * r
        e = xh * gv - t_ref[...]
        a_loss[...] += _fold8(e * e)
        dy = e * (1.0 / D)
        a_dg[...] += _fold8(dy * xh)
        t = dy * gv
        dx = r * (t - xh * jnp.mean(t * xh, axis=-1, keepdims=True))
        dx_ref[...] = dx
        a_dgt[...] += _fold8(dx * fv)
        df_ref[...] = (dx * gt).astype(bf16)

        @pl.when(i == n - 1)
        def _():
            tot = jnp.sum(jnp.sum(a_loss[...], axis=0, keepdims=True), axis=1, keepdims=True) * (0.5 / D)
            loss_ref[...] = jnp.broadcast_to(tot, (1, 128))
            dg_ref[...] = jnp.sum(a_dg[...], axis=0, keepdims=True)
            dgt_ref[...] = jnp.sum(a_dgt[...], axis=0, keepdims=True)

    return pl.pallas_call(
        body, name="final_fwd_bwd", grid=(n,),
        in_specs=[_tile_spec(tm, D), _tile_spec(tm, D), _row_spec(D), _row_spec(D), _tile_spec(tm, D)],
        out_specs=[_tile_spec(tm, D), _tile_spec(tm, D), _row_spec(128), _row_spec(D), _row_spec(D)],
        out_shape=[_sds((s, D), f32), _sds((s, D), bf16), _sds((1, 128), f32), _sds((1, D), f32), _sds((1, D), f32)],
        scratch_shapes=[pltpu.VMEM((8, D), f32)] * 3, compiler_params=_params(("arbitrary",)),
    )(x2, ffn, gt2, g, tgt)


def _norm_mod_bwd(name, dh, xin, g, sc, dres, tm, mix=None, gt=None):
    s = dh.shape[0]
    n = s // tm
    with_mix = mix is not None

    def body(*refs):
        if with_mix:
            dh_ref, x_ref, g_ref, sc_ref, dr_ref, m_ref, gt_ref, dx_ref, dm_ref, dsc_ref, dsh_ref, dg_ref, dgt_ref, a_sc, a_sh, a_g, a_gt = refs
        else:
            dh_ref, x_ref, g_ref, sc_ref, dr_ref, dx_ref, dsc_ref, dsh_ref, dg_ref, a_sc, a_sh, a_g = refs
        i = pl.program_id(0)

        @pl.when(i == 0)
        def _():
            a_sc[...] = jnp.zeros_like(a_sc)
            a_sh[...] = jnp.zeros_like(a_sh)
            a_g[...] = jnp.zeros_like(a_g)
            if with_mix:
                a_gt[...] = jnp.zeros_like(a_gt)

        dh = dh_ref[...]
        xv = x_ref[...]
        gv = g_ref[...]
        r = lax.rsqrt(jnp.mean(xv * xv, axis=-1, keepdims=True) + EPS)
        xh = xv * r
        a_sc[...] += _fold8(dh * xh * gv)
        a_sh[...] += _fold8(dh)
        dn = dh * (1.0 + sc_ref[...])
        a_g[...] += _fold8(dn * xh)
        t = dn * gv
        dx = dr_ref[...] + r * (t - xh * jnp.mean(t * xh, axis=-1, keepdims=True))
        dx_ref[...] = dx
        if with_mix:
            a_gt[...] += _fold8(dx * m_ref[...])
            dm_ref[...] = (dx * gt_ref[...]).astype(bf16)

        @pl.when(i == n - 1)
        def _():
            dsc_ref[...] = jnp.sum(a_sc[...], axis=0, keepdims=True)
            dsh_ref[...] = jnp.sum(a_sh[...], axis=0, keepdims=True)
            dg_ref[...] = jnp.sum(a_g[...], axis=0, keepdims=True)
            if with_mix:
                dgt_ref[...] = jnp.sum(a_gt[...], axis=0, keepdims=True)

    tile, row = _tile_spec(tm, D), _row_spec(D)
    if with_mix:
        ins, args = [tile, tile, row, row, tile, tile, row], (dh, xin, g, sc, dres, mix, gt)
        outs = [tile, tile, row, row, row, row]
        shapes = [_sds((s, D), f32), _sds((s, D), bf16)] + [_sds((1, D), f32)] * 4
        nacc = 4
    else:
        ins, args = [tile, tile, row, row, tile], (dh, xin, g, sc, dres)
        outs = [tile, row, row, row]
        shapes = [_sds((s, D), f32)] + [_sds((1, D), f32)] * 3
        nacc = 3
    return pl.pallas_call(
        body, name=name, grid=(n,), in_specs=ins, out_specs=outs, out_shape=shapes,
        scratch_shapes=[pltpu.VMEM((8, D), f32)] * nacc, compiler_params=_params(("arbitrary",)),
    )(*args)


def _mix_pre(att, y, proj2, g_att, g_ssd, tm):
    s = att.shape[0]

    def body(a_ref, y_ref, z_ref, ga_ref, gs_ref, o_ref):
        a = a_ref[...]
        ra = lax.rsqrt(jnp.mean(a * a, axis=-1, keepdims=True) + EPS)
        o_ref[:, 0:ATT_W] = (a * ra * ga_ref[...]).astype(bf16)
        z = z_ref[...]
        u = y_ref[...] * (z * _sigmoid(z))
        ru = lax.rsqrt(jnp.mean(u * u, axis=-1, keepdims=True) + EPS)
        o_ref[:, ATT_W:] = (u * ru * gs_ref[...]).astype(bf16)

    t = _tile_spec(tm, ATT_W)
    return pl.pallas_call(
        body, name="mix_pre", grid=(s // tm,), in_specs=[t, t, t, _row_spec(ATT_W), _row_spec(SSD_W)],
        out_specs=_tile_spec(tm, D), out_shape=_sds((s, D), bf16), compiler_params=_params(("parallel",)),
    )(att, y, proj2, g_att, g_ssd)


def _mix_pre_bwd(dmc, att, y, proj2, g_att, g_ssd, tm):
    s = att.shape[0]
    n = s // tm

    def body(da_ref, ds_ref, a_ref, y_ref, z_ref, ga_ref, gs_ref, datt_ref, dy_ref, dz_ref, dga_ref, dgs_ref, acc_a, acc_s):
        i = pl.program_id(0)

        @pl.when(i == 0)
        def _():
            acc_a[...] = jnp.zeros_like(acc_a)
            acc_s[...] = jnp.zeros_like(acc_s)

        a = a_ref[...]
        ra = lax.rsqrt(jnp.mean(a * a, axis=-1, keepdims=True) + EPS)
        ah = a * ra
        dan = da_ref[...]
        acc_a[...] += _fold8(dan * ah)
        t = dan * ga_ref[...]
        datt_ref[...] = (ra * (t - ah * jnp.mean(t * ah, axis=-1, keepdims=True))).astype(bf16)
        z = z_ref[...]
        yv = y_ref[...]
        sz = _sigmoid(z)
        sil = z * sz
        u = yv * sil
        ru = lax.rsqrt(jnp.mean(u * u, axis=-1, keepdims=True) + EPS)
        uh = u * ru
        dsn = ds_ref[...]
        acc_s[...] += _fold8(dsn * uh)
        t2 = dsn * gs_ref[...]
        du = ru * (t2 - uh * jnp.mean(t2 * uh, axis=-1, keepdims=True))
        dy_ref[...] = du * sil
        dz_ref[...] = (du * yv * (sz * (1.0 + z * (1.0 - sz)))).astype(bf16)

        @pl.when(i == n - 1)
        def _():
            dga_ref[...] = jnp.sum(acc_a[...], axis=0, keepdims=True)
            dgs_ref[...] = jnp.sum(acc_s[...], axis=0, keepdims=True)

    t = _tile_spec(tm, ATT_W)
    row = _row_spec(ATT_W)
    return pl.pallas_call(
        body, name="mix_pre_bwd", grid=(n,),
        in_specs=[_tile_spec(tm, ATT_W, 0), _tile_spec(tm, ATT_W, 1), t, t, t, row, row],
        out_specs=[t, t, t, row, row],
        out_shape=[_sds((s, ATT_W), bf16), _sds((s, SSD_W), f32), _sds((s, SSD_W), bf16), _sds((1, ATT_W), f32), _sds((1, SSD_W), f32)],
        scratch_shapes=[pltpu.VMEM((8, ATT_W), f32)] * 2, compiler_params=_params(("arbitrary",)),
    )(dmc, dmc, att, y, proj2, g_att, g_ssd)


ATT_GROUP = 2


def _pair_rows(qc):
    two = jnp.concatenate([qc, qc], axis=0)
    r = lax.broadcasted_iota(jnp.int32, (2 * CHUNK, 128), 0)
    l = lax.broadcasted_iota(jnp.int32, (2 * CHUNK, 128), 1)
    return jnp.where((r < CHUNK) == (l < HD), two, jnp.zeros_like(two))


def _pair_scores(wt, kb, bias, r0):
    sc = lax.dot_general(wt, kb, (((1,), (1,)), ((), ())), preferred_element_type=f32) * (HD ** -0.5) + bias
    kidx = lax.broadcasted_iota(jnp.int32, sc.shape, 1)
    return jnp.where(r0 + kidx >= PADK, sc, -jnp.inf)


def _softmax_lanes(sc):
    e = jnp.exp(sc - jnp.max(sc, axis=-1, keepdims=True))
    return e / jnp.sum(e, axis=-1, keepdims=True)


def _pair_diag(r):
    lane = lax.broadcasted_iota(jnp.int32, (CHUNK, 128), 1)
    return jnp.where(lane < HD, r[0:CHUNK], r[CHUNK:])


def _pad_keys(k_ref, kp, s):
    kp[0:PADK, :] = jnp.zeros((PADK, 128), bf16)
    kp[PADK:PADK + s, :] = k_ref[...]
    kp[PADK + s:, :] = jnp.zeros((CHUNK, 128), bf16)


def _attn_fwd(qkv, bias2):
    s = qkv.shape[0]
    nc = s // CHUNK
    npair = NH // 2

    def body(q_ref, k_ref, v_ref, b_ref, o_ref, kp, vp):
        _pad_keys(k_ref, kp, s)
        _pad_keys(v_ref, vp, s)

        def group(g, carry):
            r0s = [pl.multiple_of((g * ATT_GROUP + u) * CHUNK, CHUNK) for u in range(ATT_GROUP)]
            scs = [_pair_scores(_pair_rows(q_ref[pl.ds(r0, CHUNK), :]), kp[pl.ds(r0, BANDP), :], b_ref[...], r0) for r0 in r0s]
            ps = [_softmax_lanes(sc).astype(bf16) for sc in scs]
            for r0, p in zip(r0s, ps):
                o_ref[pl.ds(r0, CHUNK), :] = _pair_diag(jnp.dot(p, vp[pl.ds(r0, BANDP), :], preferred_element_type=f32))
            return carry

        lax.fori_loop(0, nc // ATT_GROUP, group, 0)

    return pl.pallas_call(
        body, name="attn_fwd", grid=(npair,),
        in_specs=[pl.BlockSpec((s, 128), lambda p: (0, p)), pl.BlockSpec((s, 128), lambda p: (0, npair + p)),
                  pl.BlockSpec((s, 128), lambda p: (0, 2 * npair + p)), pl.BlockSpec((None, 2 * CHUNK, BANDP), lambda p: (p, 0, 0))],
        out_specs=pl.BlockSpec((s, 128), lambda p: (0, p)), out_shape=_sds((s, ATT_W), f32),
        scratch_shapes=[pltpu.VMEM((PADK + s + CHUNK, 128), bf16)] * 2, compiler_params=_params(("parallel",)),
    )(qkv, qkv, qkv, bias2)


def _attn_bwd(qkv, datt, bias2, bias2t):
    s = qkv.shape[0]
    nc = s // CHUNK
    npair = NH // 2
    rows = PADK + s + CHUNK
    nt = (((1,), (1,)), ((), ()))

    def body(q_ref, k_ref, v_ref, do_ref, b_ref, bt_ref, dq_ref, dk_ref, dv_ref, g_ref, kp, vp, dkp, dvp):
        _pad_keys(k_ref, kp, s)
        _pad_keys(v_ref, vp, s)
        dkp[...] = jnp.zeros_like(dkp)
        dvp[...] = jnp.zeros_like(dvp)
        g_ref[...] = jnp.zeros_like(g_ref)

        def group(g, carry):
            r0s = [pl.multiple_of((g * ATT_GROUP + u) * CHUNK, CHUNK) for u in range(ATT_GROUP)]
            wts = [_pair_rows(q_ref[pl.ds(r0, CHUNK), :]) for r0 in r0s]
            dos = [_pair_rows(do_ref[pl.ds(r0, CHUNK), :]) for r0 in r0s]
            scs = [_pair_scores(wt, kp[pl.ds(r0, BANDP), :], b_ref[...], r0) for wt, r0 in zip(wts, r0s)]
            dps = [lax.dot_general(do, vp[pl.ds(r0, BANDP), :], nt, preferred_element_type=f32) for do, r0 in zip(dos, r0s)]
            scts, dpts = [], []
            for wt, do, r0 in zip(wts, dos, r0s):
                sct = lax.dot_general(kp[pl.ds(r0, BANDP), :], wt, nt, preferred_element_type=f32) * (HD ** -0.5) + bt_ref[...]
                kidx = lax.broadcasted_iota(jnp.int32, sct.shape, 0)
                scts.append(jnp.where(r0 + kidx >= PADK, sct, -jnp.inf))
                dpts.append(lax.dot_general(vp[pl.ds(r0, BANDP), :], do, nt, preferred_element_type=f32))
            for r0, sc, dp in zip(r0s, scs, dps):
                p = _softmax_lanes(sc)
                ds = p * (dp - jnp.sum(p * dp, axis=-1, keepdims=True))
                g_ref[...] += ds
                dq = jnp.dot(ds.astype(bf16), kp[pl.ds(r0, BANDP), :], preferred_element_type=f32)
                dq_ref[pl.ds(r0, CHUNK), :] = (_pair_diag(dq) * (HD ** -0.5)).astype(bf16)
            for r0, wt, do, sct, dpt in zip(r0s, wts, dos, scts, dpts):
                e = jnp.exp(sct - jnp.max(sct, axis=0, keepdims=True))
                pt = e / jnp.sum(e, axis=0, keepdims=True)
                dst = pt * (dpt - jnp.sum(pt * dpt, axis=0, keepdims=True))
                dkp[pl.ds(r0, BANDP), :] += jnp.dot(dst.astype(bf16), wt, preferred_element_type=f32) * (HD ** -0.5)
                dvp[pl.ds(r0, BANDP), :] += jnp.dot(pt.astype(bf16), do, preferred_element_type=f32)
            return carry

        lax.fori_loop(0, nc // ATT_GROUP, group, 0)
        dk_ref[...] = dkp[PADK:PADK + s, :].astype(bf16)
        dv_ref[...] = dvp[PADK:PADK + s, :].astype(bf16)

    col = lambda off: pl.BlockSpec((s, 128), lambda p: (0, off + p))
    return pl.pallas_call(
        body, name="attn_bwd", grid=(npair,),
        in_specs=[col(0), col(npair), col(2 * npair), col(0), pl.BlockSpec((None, 2 * CHUNK, BANDP), lambda p: (p, 0, 0)),
                  pl.BlockSpec((None, BANDP, 2 * CHUNK), lambda p: (p, 0, 0))],
        out_specs=[col(0), col(0), col(0), pl.BlockSpec((None, 2 * CHUNK, BANDP), lambda p: (p, 0, 0))],
        out_shape=[_sds((s, ATT_W), bf16)] * 3 + [_sds((npair, 2 * CHUNK, BANDP), f32)],
        scratch_shapes=[pltpu.VMEM((rows, 128), bf16)] * 2 + [pltpu.VMEM((rows, 128), f32)] * 2,
        compiler_params=_params(("parallel",)),
    )(qkv, qkv, qkv, datt, bias2, bias2t)


def _rel_tables():
    onehot = np.zeros((BANDP, N_REL), np.float32)
    for j in range(BAND + CHUNK - 1):
        o = j - (CHUNK - 1)
        onehot[j, int(np.clip(PADK - o, -(CHUNK - 1), REL_CLIP)) + CHUNK - 1] = 1.0
    return onehot, np.ascontiguousarray(np.eye(CHUNK, dtype=np.float32)[::-1])


def _expand_bias(rel):
    ext = jnp.concatenate([jnp.broadcast_to(rel[:, N_REL - 1:], (NH, N_REL - 1)), rel[:, ::-1],
                           jnp.zeros((NH, BANDP - BAND + 1), f32)], axis=1)
    band = jnp.stack([ext[:, CHUNK - 1 - q:CHUNK - 1 - q + BANDP] for q in range(CHUNK)], axis=1)
    band = jnp.where(np.arange(BANDP) < BAND, band, -jnp.inf)
    return band.reshape(NH // 2, 2 * CHUNK, BANDP)


def _rel_bias_grad(gband):
    def body(g_ref, m_ref, flip_ref, o_ref, d2):
        for h in range(NH):
            rev = jnp.dot(flip_ref[...], g_ref[h], precision=HIGHEST, preferred_element_type=f32)
            rolled = pltpu.roll(rev, 0, 1, stride=1, stride_axis=0)
            d2[h:h + 1, :] = jnp.sum(rolled, axis=0, keepdims=True)
        o_ref[...] = jnp.dot(d2[...], m_ref[...], precision=HIGHEST, preferred_element_type=f32)

    onehot, flip = _rel_tables()
    return pl.pallas_call(
        body, name="rel_bias_grad", out_shape=_sds((NH, N_REL), f32), scratch_shapes=[pltpu.VMEM((NH, BANDP), f32)],
    )(gband, jnp.asarray(onehot), jnp.asarray(flip))


XBC_BLK = 512
XBC_COL0 = SSD_W // XBC_BLK
DT_COL = (SSD_W + XBC) // 128


def _conv_taps(ext, w_ref, b_ref, tm):
    n = ext.shape[0]
    pre = w_ref[3:4, :] * ext + b_ref[...]
    for j in range(3):
        pre = pre + w_ref[j:j + 1, :] * pltpu.roll(ext, 3 - j, 0)
    return pre


def _ssd_conv(proj2, conv_w, conv_b, tm):
    s = proj2.shape[0]
    nb = XBC // XBC_BLK

    def body(x_ref, p_ref, w_ref, b_ref, o_ref):
        i = pl.program_id(1)
        prev = jnp.where(i > 0, p_ref[...], 0.0)
        ext = jnp.concatenate([prev, x_ref[...]], axis=0)
        pre = _conv_taps(ext, w_ref, b_ref, tm)[8:8 + tm]
        o_ref[...] = pre * _sigmoid(pre)

    return pl.pallas_call(
        body, name="ssd_conv", grid=(nb, s // tm),
        in_specs=[pl.BlockSpec((tm, XBC_BLK), lambda j, i: (i, XBC_COL0 + j)),
                  pl.BlockSpec((8, XBC_BLK), lambda j, i: (jnp.maximum(i * (tm // 8) - 1, 0), XBC_COL0 + j)),
                  pl.BlockSpec((4, XBC_BLK), lambda j, i: (0, j)), pl.BlockSpec((1, XBC_BLK), lambda j, i: (0, j))],
        out_specs=pl.BlockSpec((tm, XBC_BLK), lambda j, i: (i, j)), out_shape=_sds((s, XBC), f32),
        compiler_params=_params(("parallel", "parallel")),
    )(proj2, proj2, conv_w, conv_b)


def _ssd_conv_bwd(dxbc, proj2, conv_w, conv_b, tm):
    s = proj2.shape[0]
    nb = XBC // XBC_BLK
    n = s // tm
    last8 = s // 8 - 1

    def body(x_ref, xp_ref, xn_ref, d_ref, dn_ref, w_ref, b_ref, o_ref, dw_ref, db_ref):
        i = pl.program_id(1)

        @pl.when(i == 0)
        def _():
            dw_ref[...] = jnp.zeros_like(dw_ref)
            db_ref[...] = jnp.zeros_like(db_ref)

        prev = jnp.where(i > 0, xp_ref[...], 0.0)
        ext = jnp.concatenate([prev, x_ref[...], xn_ref[...]], axis=0)
        pre = _conv_taps(ext, w_ref, b_ref, tm)
        sg = _sigmoid(pre)
        dnext = jnp.where(i < n - 1, dn_ref[...], 0.0)
        dext = jnp.concatenate([jnp.zeros((8, XBC_BLK), f32), d_ref[...], dnext], axis=0)
        dpre = dext * (sg * (1.0 + pre * (1.0 - sg)))
        rows = tm + 16
        dx = w_ref[3:4, :] * dpre
        for j in range(3):
            dx = dx + w_ref[j:j + 1, :] * pltpu.roll(dpre, rows - (3 - j), 0)
        o_ref[...] = dx[8:8 + tm].astype(bf16)
        dcur = dpre[8:8 + tm]
        db_ref[...] += jnp.sum(dcur, axis=0, keepdims=True)
        dw_ref[3:4, :] += jnp.sum(dcur * ext[8:8 + tm], axis=0, keepdims=True)
        for j in range(3):
            dw_ref[j:j + 1, :] += jnp.sum(dcur * pltpu.roll(ext, 3 - j, 0)[8:8 + tm], axis=0, keepdims=True)

    xcol = lambda j: XBC_COL0 + j
    return pl.pallas_call(
        body, name="ssd_conv_bwd", grid=(nb, n),
        in_specs=[pl.BlockSpec((tm, XBC_BLK), lambda j, i: (i, xcol(j))),
                  pl.BlockSpec((8, XBC_BLK), lambda j, i: (jnp.maximum(i * (tm // 8) - 1, 0), xcol(j))),
                  pl.BlockSpec((8, XBC_BLK), lambda j, i: (jnp.minimum((i + 1) * (tm // 8), last8), xcol(j))),
                  pl.BlockSpec((tm, XBC_BLK), lambda j, i: (i, j)),
                  pl.BlockSpec((8, XBC_BLK), lambda j, i: (jnp.minimum((i + 1) * (tm // 8), last8), j)),
                  pl.BlockSpec((4, XBC_BLK), lambda j, i: (0, j)), pl.BlockSpec((1, XBC_BLK), lambda j, i: (0, j))],
        out_specs=[pl.BlockSpec((tm, XBC_BLK), lambda j, i: (i, j)), pl.BlockSpec((4, XBC_BLK), lambda j, i: (0, j)),
                   pl.BlockSpec((1, XBC_BLK), lambda j, i: (0, j))],
        out_shape=[_sds((s, XBC), bf16), _sds((4, XBC), f32), _sds((1, XBC), f32)],
        compiler_params=_params(("parallel", "arbitrary")),
    )(proj2, proj2, proj2, dxbc, dxbc, conv_w, conv_b)


def _ssd_consts():
    ex = np.zeros((128, SSD_W), np.float32)
    for h in range(NH):
        ex[h, h * HD:(h + 1) * HD] = 1.0
    sel = np.zeros((8, 128), np.float32)
    for h in range(NH):
        sel[h // 2, h] = 1.0
    par = np.zeros((128, 128), np.float32)
    for r in range(128):
        for h in range(NH):
            par[r, h] = 1.0 if (h % 2) == (r // 64) else 0.0
    ones_blk = np.zeros((128, 128), np.float32)
    for r in range(128):
        ones_blk[r, (r // 64) * 64:(r // 64) * 64 + 64] = 1.0
    return ex, np.ascontiguousarray(ex.T), sel, par, ones_blk


def _ssd_common(xbc_ref, dtr_ref, a_ref, dtb_ref, ex_ref, sel_ref, par_ref):
    xs = xbc_ref[:, 0:SSD_W]
    dt = _softplus(dtr_ref[...] + dtb_ref[...])
    adt = dt * a_ref[...]
    r_i = lax.broadcasted_iota(jnp.int32, (CHUNK, CHUNK), 0)
    c_i = lax.broadcasted_iota(jnp.int32, (CHUNK, CHUNK), 1)
    tril = (r_i >= c_i).astype(f32)
    cs = jnp.dot(tril, adt, precision=HIGHEST, preferred_element_type=f32)
    cs2 = jnp.concatenate([cs, cs], axis=0) * par_ref[...]
    cstp = lax.dot_general(sel_ref[...], cs2, (((1,), (1,)), ((), ())), precision=HIGHEST, preferred_element_type=f32)
    ex = ex_ref[...]
    dt_full = jnp.dot(dt, ex, precision=HIGHEST, preferred_element_type=f32)
    cs_full = jnp.dot(cs, ex, precision=HIGHEST, preferred_element_type=f32)
    return xs, dt, cs, cstp, dt_full, cs_full


def _pair_mask():
    l_i = lax.broadcasted_iota(jnp.int32, (CHUNK, 128), 0)
    lane = lax.broadcasted_iota(jnp.int32, (CHUNK, 128), 1)
    return l_i >= (lane % CHUNK), lane < HD


def _block_diag(xp, first):
    z = jnp.zeros_like(xp)
    return jnp.concatenate([jnp.where(first, xp, z), jnp.where(first, z, xp)], axis=0)


def _ssd_fwd(xbc, proj2, a_row, dtb_row, dsk_full):
    s = xbc.shape[0]
    nc = s // CHUNK
    ex, ext, sel, par, ones_blk = _ssd_consts()

    def body(xbc_ref, dtr_ref, a_ref, dtb_ref, dsk_ref, ex_ref, sel_ref, par_ref, y_ref, hs_ref, hst):
        @pl.when(pl.program_id(0) == 0)
        def _():
            hst[...] = jnp.zeros_like(hst)

        hs_ref[...] = hst[...]
        xs, dt, cs, cstp, dt_full, cs_full = _ssd_common(xbc_ref, dtr_ref, a_ref, dtb_ref, ex_ref, sel_ref, par_ref)
        cs_last = cs_full[CHUNK - 1:CHUNK, :]
        xdt = xs * dt_full
        causal, first = _pair_mask()
        for g in range(NG):
            gl = slice(g * GW, (g + 1) * GW)
            bg = xbc_ref[:, SSD_W + g * NSTATE:SSD_W + (g + 1) * NSTATE].astype(bf16)
            cg = xbc_ref[:, SSD_W + NG * NSTATE + g * NSTATE:SSD_W + NG * NSTATE + (g + 1) * NSTATE].astype(bf16)
            cb2 = lax.dot_general(cg, jnp.concatenate([bg, bg], axis=0), (((1,), (1,)), ((), ())), preferred_element_type=f32)
            hg = hst[g]
            y0 = jnp.dot(cg, hg.astype(bf16), preferred_element_type=f32)
            yoff = jnp.exp(cs_full[:, gl]) * y0
            for j in range(GW // 128):
                pair = g * (GW // 128) + j
                pl_ = slice(pair * 128, (pair + 1) * 128)
                seg = jnp.exp(jnp.where(causal, cs_full[:, pl_] - cstp[pair:pair + 1, :], -jnp.inf))
                m = (cb2 * seg).astype(bf16)
                yd = jnp.dot(m, _block_diag(xdt[:, pl_].astype(bf16), first), preferred_element_type=f32)
                y_ref[:, pl_] = yd + yoff[:, j * 128:(j + 1) * 128] + xs[:, pl_] * dsk_ref[:, pl_]
            xdec = (xdt[:, gl] * jnp.exp(cs_last[:, gl] - cs_full[:, gl])).astype(bf16)
            st = lax.dot_general(bg, xdec, (((0,), (0,)), ((), ())), preferred_element_type=f32)
            hst[g] = jnp.exp(cs_last[:, gl]) * hg + st

    const = lambda shape: pl.BlockSpec(shape, lambda c: tuple(0 for _ in shape))
    return pl.pallas_call(
        body, name="ssd_fwd", grid=(nc,),
        in_specs=[pl.BlockSpec((CHUNK, XBC), lambda c: (c, 0)), pl.BlockSpec((CHUNK, 128), lambda c: (c, DT_COL)),
                  const((1, 128)), const((1, 128)), const((1, SSD_W)), const((128, SSD_W)), const((8, 128)), const((128, 128))],
        out_specs=[pl.BlockSpec((CHUNK, SSD_W), lambda c: (c, 0)), pl.BlockSpec((None, NG, NSTATE, GW), lambda c: (c, 0, 0, 0))],
        out_shape=[_sds((s, SSD_W), f32), _sds((nc, NG, NSTATE, GW), f32)],
        scratch_shapes=[pltpu.VMEM((NG, NSTATE, GW), f32)], compiler_params=_params(("arbitrary",)),
    )(xbc, proj2, a_row, dtb_row, dsk_full, jnp.asarray(ex), jnp.asarray(sel), jnp.asarray(par))


def _ssd_bwd(xbc, proj2, dy, hsave, a_row, dtb_row, dsk_full):
    s = xbc.shape[0]
    nc = s // CHUNK
    ex, ext, sel, par, ones_blk = _ssd_consts()

    def body(xbc_ref, dtr_ref, dy_ref, hs_ref, a_ref, dtb_ref, dsk_ref, ex_ref, ext_ref, sel_ref, par_ref, ob_ref,
             dxbc_ref, ddtr_ref, dd_ref, da_ref, ddtb_ref, dh, a_dd, a_da, a_dtb, dcs_lane, dcs_b, dxdt):
        step = pl.program_id(0)

        @pl.when(step == 0)
        def _():
            dh[...] = jnp.zeros_like(dh)
            a_dd[...] = jnp.zeros_like(a_dd)
            a_da[...] = jnp.zeros_like(a_da)
            a_dtb[...] = jnp.zeros_like(a_dtb)

        xs, dt, cs, cstp, dt_full, cs_full = _ssd_common(xbc_ref, dtr_ref, a_ref, dtb_ref, ex_ref, sel_ref, par_ref)
        cs_last = cs_full[CHUNK - 1:CHUNK, :]
        xdt = xs * dt_full
        dyv = dy_ref[...]
        a_dd[...] += _fold8(dyv * xs)
        causal, first = _pair_mask()
        ones_l = jnp.ones((CHUNK, 128), f32)
        for g in range(NG):
            gl = slice(g * GW, (g + 1) * GW)
            bcol = slice(SSD_W + g * NSTATE, SSD_W + (g + 1) * NSTATE)
            ccol = slice(SSD_W + NG * NSTATE + g * NSTATE, SSD_W + NG * NSTATE + (g + 1) * NSTATE)
            bg = xbc_ref[:, bcol].astype(bf16)
            cg = xbc_ref[:, ccol].astype(bf16)
            bg2 = jnp.concatenate([bg, bg], axis=0)
            cb2 = lax.dot_general(cg, bg2, (((1,), (1,)), ((), ())), preferred_element_type=f32)
            hg = hs_ref[g]
            hgb = hg.astype(bf16)
            dhg = dh[g]
            dhgb = dhg.astype(bf16)
            eg = jnp.exp(cs_full[:, gl])
            dec = jnp.exp(cs_last[:, gl] - cs_full[:, gl])
            gam = jnp.exp(cs_last[:, gl])
            dyg = dyv[:, gl]
            xdt_g = xdt[:, gl]
            y0 = jnp.dot(cg, hgb, preferred_element_type=f32)
            dy0 = (eg * dyg).astype(bf16)
            dcm = lax.dot_general(dy0, hgb, (((1,), (1,)), ((), ())), preferred_element_type=f32)
            dh_prev = gam * dhg + lax.dot_general(cg, dy0, (((0,), (0,)), ((), ())), preferred_element_type=f32)
            dgam = jnp.sum(dhg * hg, axis=0, keepdims=True) * gam
            dxdec = jnp.dot(bg, dhgb, preferred_element_type=f32)
            dbm = lax.dot_general((xdt_g * dec).astype(bf16), dhgb, (((1,), (1,)), ((), ())), preferred_element_type=f32)
            t = dxdec * xdt_g * dec
            dcs_lane[:, gl] = dyg * eg * y0 - t
            dcs_lane[CHUNK - 1:CHUNK, gl] += jnp.sum(t, axis=0, keepdims=True) + dgam
            dxdt[:, gl] = dxdec * dec
            dcb2 = jnp.zeros((CHUNK, 128), f32)
            for j in range(GW // 128):
                pair = g * (GW // 128) + j
                pl_ = slice(pair * 128, (pair + 1) * 128)
                seg = jnp.exp(jnp.where(causal, cs_full[:, pl_] - cstp[pair:pair + 1, :], -jnp.inf))
                m = cb2 * seg
                mb = m.astype(bf16)
                rhs = _block_diag(xdt[:, pl_].astype(bf16), first)
                dyp = dyv[:, pl_].astype(bf16)
                dm = lax.dot_general(dyp, rhs, (((1,), (1,)), ((), ())), preferred_element_type=f32)
                tt = lax.dot_general(mb, dyp, (((0,), (0,)), ((), ())), preferred_element_type=f32)
                dxdt[:, pl_] += jnp.where(first, tt[0:CHUNK], tt[CHUNK:])
                dcb2 = dcb2 + dm * seg
                w = dm * m
                rsum = jnp.dot(w, ob_ref[...], precision=HIGHEST, preferred_element_type=f32)
                t2 = lax.dot_general(w, ones_l, (((0,), (0,)), ((), ())), precision=HIGHEST, preferred_element_type=f32)
                dcs_b[:, pl_] = rsum - jnp.where(first, t2[0:CHUNK], t2[CHUNK:])
            dcb2b = dcb2.astype(bf16)
            dcm = dcm + jnp.dot(dcb2b, bg2, preferred_element_type=f32)
            t3 = lax.dot_general(dcb2b, cg, (((0,), (0,)), ((), ())), preferred_element_type=f32)
            dxbc_ref[:, bcol] = dbm + t3[0:CHUNK] + t3[CHUNK:]
            dxbc_ref[:, ccol] = dcm
            dh[g] = dh_prev
        dcs = jnp.dot(dcs_lane[...] + dcs_b[...] * (1.0 / HD), ext_ref[...], precision=HIGHEST, preferred_element_type=f32)
        r_i = lax.broadcasted_iota(jnp.int32, (CHUNK, CHUNK), 0)
        c_i = lax.broadcasted_iota(jnp.int32, (CHUNK, CHUNK), 1)
        triu = (r_i <= c_i).astype(f32)
        da_ = jnp.dot(triu, dcs, precision=HIGHEST, preferred_element_type=f32)
        dxdtv = dxdt[...]
        ddt = da_ * a_ref[...] + jnp.dot(dxdtv * xs, ext_ref[...], precision=HIGHEST, preferred_element_type=f32)
        a_da[...] += _fold8(da_ * dt)
        dxbc_ref[:, 0:SSD_W] = dyv * dsk_ref[...] + dxdtv * dt_full
        ddtr = ddt * _sigmoid(dtr_ref[...] + dtb_ref[...])
        ddtr_ref[...] = ddtr
        a_dtb[...] += _fold8(ddtr)

        @pl.when(step == nc - 1)
        def _():
            dd_ref[...] = jnp.sum(jnp.dot(a_dd[...], ext_ref[...], precision=HIGHEST, preferred_element_type=f32), axis=0, keepdims=True)
            da_ref[...] = jnp.sum(a_da[...], axis=0, keepdims=True)
            ddtb_ref[...] = jnp.sum(a_dtb[...], axis=0, keepdims=True)

    rev = lambda c: nc - 1 - c
    const = lambda shape: pl.BlockSpec(shape, lambda c: tuple(0 for _ in shape))
    return pl.pallas_call(
        body, name="ssd_bwd", grid=(nc,),
        in_specs=[pl.BlockSpec((CHUNK, XBC), lambda c: (rev(c), 0)), pl.BlockSpec((CHUNK, 128), lambda c: (rev(c), DT_COL)),
                  pl.BlockSpec((CHUNK, SSD_W), lambda c: (rev(c), 0)), pl.BlockSpec((None, NG, NSTATE, GW), lambda c: (rev(c), 0, 0, 0)),
                  const((1, 128)), const((1, 128)), const((1, SSD_W)), const((128, SSD_W)), const((SSD_W, 128)),
                  const((8, 128)), const((128, 128)), const((128, 128))],
        out_specs=[pl.BlockSpec((CHUNK, XBC), lambda c: (rev(c), 0)), pl.BlockSpec((CHUNK, 128), lambda c: (rev(c), 0)),
                   const((1, 128)), const((1, 128)), const((1, 128))],
        out_shape=[_sds((s, XBC), f32), _sds((s, 128), f32), _sds((1, 128), f32), _sds((1, 128), f32), _sds((1, 128), f32)],
        scratch_shapes=[pltpu.VMEM((NG, NSTATE, GW), f32), pltpu.VMEM((8, SSD_W), f32), pltpu.VMEM((8, 128), f32), pltpu.VMEM((8, 128), f32),
                        pltpu.VMEM((CHUNK, SSD_W), f32), pltpu.VMEM((CHUNK, SSD_W), f32), pltpu.VMEM((CHUNK, SSD_W), f32)],
        compiler_params=_params(("arbitrary",)),
    )(xbc, proj2, dy, hsave, a_row, dtb_row, dsk_full, jnp.asarray(ex), jnp.asarray(ext), jnp.asarray(sel), jnp.asarray(par),
      jnp.asarray(ones_blk))


def _local_step(x, tgt, mods, g_mix, rel, conv_w, conv_b, dt_bias, a_log, d_skip, g_att, g_ssd, g_ffn, g_final, weights):
    s = x.shape[0]
    tm_e = 256 if s % 256 == 0 else s
    tm_m = 512 if s % 512 == 0 else s
    tm_l = 1024 if s % 1024 == 0 else s
    tk = 2048 if s % 2048 == 0 else s
    sh1, sc1, gt1, sh2, sc2, gt2 = [mods[:, i * D:(i + 1) * D] for i in range(6)]

    h1b = _norm_mod("norm_mod_1", x, g_mix, sc1, sh1, tm_e)
    win = weights.w_in(h1b)
    qkv = _mm_nn_fullk("proj_qkv", h1b, win[:, :IN_A], tm_m, 768, bf16)
    proj2 = _mm_nn_fullk("proj_zxbcdt", h1b, win[:, IN_A:], tm_m, 896, f32)
    bias = _expand_bias(rel)
    att = _attn_fwd(qkv, bias)
    xbc = _ssd_conv(proj2, conv_w, conv_b, tm_e)
    a_row = jnp.pad(-jnp.exp(a_log), ((0, 0), (0, 128 - NH)))
    dtb_row = jnp.pad(dt_bias, ((0, 0), (0, 128 - NH)))
    dsk_full = jnp.repeat(d_skip, HD, axis=1)
    y, hsave = _ssd_fwd(xbc, proj2, a_row, dtb_row, dsk_full)
    mixcat = _mix_pre(att, y, proj2, g_att, g_ssd, tm_e)
    wout = weights.w_out(mixcat)
    mix = _mm_nn_fullk("proj_out", mixcat, wout, tm_m, 1024, f32)
    x2, h2b = _resid_norm_mod(x, gt1, mix, g_ffn, sc2, sh2, tm_e)
    wg4, wu4, wd4 = weights.ffn(h2b)
    gate, up, act = _ffn_up(h2b, wg4, wu4, tm_m)
    ffn = _ffn_down(act, wd4, tm_l)

    dx3, dffn, loss, dg_final, dgt2 = _final_fwd_bwd(x2, ffn, gt2, g_final, tgt, tm_e)
    weights.grad(("w_down",), [_grad_wdown4(act, dffn, 1024, tk)])
    dgate, dup = _ffn_dact(dffn, wd4, gate, up, tm_m)
    weights.grad(("w_gate", "w_up"), [_grad_cols4("grad_w_gate", h2b, dgate, 1024, tk), _grad_cols4("grad_w_up", h2b, dup, 1024, tk)])
    dh2 = _ffn_dh(dgate, dup, wg4, wu4, tm_m)
    dx2, dmix, dsc2, dsh2, dg_ffn, dgt1 = _norm_mod_bwd("norm_mod_bwd_2", dh2, x2, g_ffn, sc2, dx3, tm_e, mix=mix, gt=gt1)
    weights.grad(("w_out",), [_mm_tn("grad_w_out", mixcat, dmix, 1024, 1024, tk, bf16).reshape(NSH, D // NSH, D)])
    dmc = _mm_nt("dmixcat", dmix, wout, tm_m, 1024, D, f32)
    datt, dy, dz, dg_att, dg_ssd = _mix_pre_bwd(dmc, att, y, proj2, g_att, g_ssd, tm_e)
    dq, dk, dv, gband = _attn_bwd(qkv, datt, bias, jnp.transpose(bias, (0, 2, 1)))
    drel = _rel_bias_grad(gband.reshape(NH, CHUNK, BANDP))
    dxbc, ddtr, dd_row, da_row, ddtb_row = _ssd_bwd(xbc, proj2, dy, hsave, a_row, dtb_row, dsk_full)
    dxbc_raw, dconv_w, dconv_b = _ssd_conv_bwd(dxbc, proj2, conv_w, conv_b, tm_e)
    dproj = jnp.concatenate([dq, dk, dv, dz, dxbc_raw, ddtr.astype(bf16)], axis=1)
    gwin = _mm_tn("grad_w_in", h1b, dproj, 1024, 1152, tk, bf16)
    weights.grad(("w_in",), [jnp.stack([jnp.pad(gwin[:, k * IN_SH:(k + 1) * IN_SH], ((0, 0), (0, IN_SHP - IN_SH))) for k in range(NSH)])])
    dh1 = _mm_nt("dh1", dproj, win, tm_m, D, 1920, f32)
    grad_x, dsc1, dsh1, dg_mix = _norm_mod_bwd("norm_mod_bwd_1", dh1, x, g_mix, sc1, dx2, tm_e)

    dmods = jnp.concatenate([dsh1, dsc1, dgt1, dsh2, dsc2, dgt2], axis=1)
    dd_skip = dd_row[:, :NH]
    da_log = da_row[:, :NH] * a_row[:, :NH]
    small = dict(g_mix=dg_mix, conv_b=dconv_b, dt_bias=ddtb_row[:, :NH], a_log=da_log, d_skip=dd_skip, g_att_out=dg_att,
                 g_ssd_out=dg_ssd, g_ffn=dg_ffn, g_final=dg_final, rel_bias=drel, conv_w=dconv_w)
    return loss[0, 0], grad_x, dmods, small


HBM = pl.BlockSpec(memory_space=pl.ANY)
VMEM = pl.BlockSpec(memory_space=pltpu.VMEM)


def _place():
    x, y, c = lax.axis_index("x"), lax.axis_index("y"), lax.axis_index("c")
    chips = [(1 - x, y), (x, 1 - y), (1 - x, 1 - y)]
    return x, y, c, chips


def _allgather8(name, payload):
    r = payload.shape[0]

    def body(x_ref, out_ref, send_sems, recv_sems, local_sem):
        x, y, c, chips = _place()
        me, sibling = (x, y, c), (x, y, 1 - c)

        def slot(px, py, pc):
            return out_ref.at[4 * px + 2 * py + pc]

        def copy(k, block, to, src=None):
            return pltpu.make_async_remote_copy(
                src_ref=slot(*block) if src is None else src, dst_ref=slot(*block),
                send_sem=send_sems.at[k], recv_sem=recv_sems.at[k], device_id=to, device_id_type=MESH)

        mine = pltpu.make_async_copy(x_ref, slot(*me), local_sem)
        mine.start()
        first = [copy(0, me, sibling, src=x_ref)]
        first += [copy(1 + j, me, (*chip, c), src=x_ref) for j, chip in enumerate(chips)]
        for cp in first:
            cp.start()
        passed = [copy(4 + j, (*chip, c), sibling) for j, chip in enumerate(chips)]
        for j, chip in enumerate(chips):
            copy(1 + j, (*chip, c), me).wait_recv()
            passed[j].start()
        copy(0, sibling, me).wait_recv()
        for j, chip in enumerate(chips):
            copy(4 + j, (*chip, 1 - c), me).wait_recv()
        for cp in first + passed:
            cp.wait_send()
        mine.wait()

    return pl.pallas_call(
        body, name=name, out_shape=_sds((N_DEV, r, 128), f32), in_specs=[VMEM], out_specs=VMEM,
        scratch_shapes=[pltpu.SemaphoreType.DMA((7,)), pltpu.SemaphoreType.DMA((7,)), pltpu.SemaphoreType.DMA],
    )(payload)


def _sum8(g):
    r = g.shape[1]

    def body(g_ref, o_ref):
        acc = g_ref[0]
        for i in range(1, N_DEV):
            acc = acc + g_ref[i]
        o_ref[...] = acc

    return pl.pallas_call(body, name="sum8", out_shape=_sds((r, 128), f32))(g)


SEM = pl.BlockSpec(memory_space=pltpu.SEMAPHORE)
EFFECT = pltpu.SideEffectType.DATAFLOW_SIDE_EFFECTING


def _gather_copies(ins, lands, send_sems, recv_sems):
    x, y, c, chips = _place()
    k = 2 * x + y
    starts, recvs = [], []
    for w in range(len(ins)):
        for j, (px, py) in enumerate(chips):
            def mk(dst):
                return pltpu.make_async_remote_copy(src_ref=ins[w].at[c], dst_ref=dst, send_sem=send_sems[w].at[j],
                                                    recv_sem=recv_sems[w].at[j], device_id=(px, py, c), device_id_type=MESH)
            starts.append(mk(lands[w].at[k, c]))
            recvs.append(mk(lands[w].at[2 * px + py, c]))
    return starts, recvs


def _reduce_copies(ins, lands, send_sems, recv_sems):
    x, y, c, chips = _place()
    k = 2 * x + y
    starts, recvs = [], []
    for w in range(len(ins)):
        for j, (px, py) in enumerate(chips):
            def mk(dst):
                return pltpu.make_async_remote_copy(src_ref=ins[w].at[2 * px + py], dst_ref=dst, send_sem=send_sems[w].at[j],
                                                    recv_sem=recv_sems[w].at[j], device_id=(px, py, c), device_id_type=MESH)
            starts.append(mk(lands[w].at[k]))
            recvs.append(mk(lands[w].at[2 * px + py]))
    return starts, recvs


def _split_start(name, copies, srcs, land_shapes):
    nw = len(srcs)

    def body(*refs):
        starts, _ = copies(refs[:nw], refs[nw:2 * nw], refs[2 * nw:3 * nw], refs[3 * nw:4 * nw])
        for cp in starts:
            cp.start()
        refs[6 * nw][...] = jnp.zeros((8, 128), f32)

    sems = [pltpu.SemaphoreType.DMA((3,))] * nw
    bufs = [pltpu.HBM(s.shape, bf16) for s in srcs] + [pltpu.HBM(s, bf16) for s in land_shapes]
    res = pl.pallas_call(
        body, name=name, out_shape=sems + sems + bufs + [_sds((8, 128), f32)],
        in_specs=[HBM] * (2 * nw), out_specs=[SEM] * (2 * nw) + [HBM] * (2 * nw) + [VMEM],
        input_output_aliases={i: 2 * nw + i for i in range(2 * nw)},
        compiler_params=pltpu.CompilerParams(has_side_effects=EFFECT),
    )(*[pltpu.with_memory_space_constraint(s, pltpu.HBM) for s in srcs],
      *[pltpu.with_memory_space_constraint(lax.empty(s, bf16), pltpu.HBM) for s in land_shapes])
    return res[:nw], res[nw:2 * nw], res[2 * nw:3 * nw], res[3 * nw:4 * nw], res[4 * nw]


def _split_wait(name, copies, send_sems, recv_sems, srcs, lands, after):
    nw = len(srcs)

    def body(*refs):
        starts, recvs = copies(refs[:nw], refs[nw:2 * nw], refs[2 * nw:3 * nw], refs[3 * nw:4 * nw])
        for s_, r_ in zip(starts, recvs):
            s_.wait_send()
            r_.wait_recv()

    bufs = [pltpu.HBM(s.shape, bf16) for s in srcs] + [pltpu.HBM(l.shape, bf16) for l in lands]
    res = pl.pallas_call(
        body, name=name, out_shape=bufs, in_specs=[HBM] * (2 * nw) + [SEM] * (2 * nw) + [HBM], out_specs=[HBM] * (2 * nw),
        input_output_aliases={i: i for i in range(2 * nw)},
        compiler_params=pltpu.CompilerParams(has_side_effects=EFFECT),
    )(*srcs, *lands, *send_sems, *recv_sems, after)
    return res[:nw], res[nw:]


def _gather_forward(name, shards, lands):
    nw = len(shards)

    def body(*refs):
        ins, lands_in, outs = refs[:nw], refs[nw:2 * nw], refs[2 * nw:3 * nw]
        st_a, st_b, st_c = refs[3 * nw:4 * nw], refs[4 * nw:5 * nw], refs[5 * nw:6 * nw]
        send_sems, recv_sems, load_sems, store_sems = refs[6 * nw:]
        x, y, c, chips = _place()
        k = 2 * x + y
        sibling = (x, y, 1 - c)
        ld_a = [pltpu.make_async_copy(ins[w].at[c], st_a[w], load_sems.at[w, 0]) for w in range(nw)]
        ld_b = [pltpu.make_async_copy(ins[w].at[1 - c], st_b[w], load_sems.at[w, 1]) for w in range(nw)]
        for cp in ld_a + ld_b:
            cp.start()
        st_own = []
        for w in range(nw):
            ld_a[w].wait()
            st_own.append(pltpu.make_async_copy(st_a[w], outs[w].at[k, c], store_sems.at[w, 0]))
            st_own[-1].start()
        for w in range(nw):
            ld_b[w].wait()
            st_own.append(pltpu.make_async_copy(st_b[w], outs[w].at[k, 1 - c], store_sems.at[w, 1]))
            st_own[-1].start()
        for cp in st_own:
            cp.wait()
        fwds = {}
        for j, (px, py) in enumerate(chips):
            kq = 2 * px + py
            for w in range(nw):
                slot = st_b[w] if j % 2 == 0 else st_c[w]
                if j == 2:
                    fwds[w, 0].wait_send()
                ld = pltpu.make_async_copy(lands_in[w].at[kq, c], slot, load_sems.at[w, 2 + j])
                ld.start()
                ld.wait()
                fwds[w, j] = pltpu.make_async_remote_copy(src_ref=slot, dst_ref=outs[w].at[kq, c], send_sem=send_sems.at[w, j],
                                                          recv_sem=recv_sems.at[w, j], device_id=sibling, device_id_type=MESH)
                fwds[w, j].start()
        for j, (px, py) in enumerate(chips):
            for w in range(nw):
                pltpu.make_async_remote_copy(src_ref=st_c[w], dst_ref=outs[w].at[2 * px + py, 1 - c], send_sem=send_sems.at[w, j],
                                             recv_sem=recv_sems.at[w, j], device_id=sibling, device_id_type=MESH).wait_recv()
        for w in range(nw):
            fwds[w, 1].wait_send()
            fwds[w, 2].wait_send()

    stage = [pltpu.VMEM(s.shape[1:], bf16) for s in shards]
    return pl.pallas_call(
        body, name=name, out_shape=[_sds(l.shape, bf16) for l in lands],
        in_specs=[HBM] * (2 * nw), out_specs=[HBM] * nw, input_output_aliases={nw + w: w for w in range(nw)},
        scratch_shapes=stage * 3 + [pltpu.SemaphoreType.DMA((nw, 3)), pltpu.SemaphoreType.DMA((nw, 3)), pltpu.SemaphoreType.DMA((nw, 5)),
                                    pltpu.SemaphoreType.DMA((nw, 2))],
        compiler_params=pltpu.CompilerParams(vmem_limit_bytes=VMEM_LIMIT),
    )(*shards, *lands)


def _rs_pair_exchange(name, grads):
    nw = len(grads)

    def body(*refs):
        ins, got, stage = refs[:nw], refs[nw:2 * nw], refs[2 * nw:3 * nw]
        send_sems, recv_sems, load_sems = refs[3 * nw:]
        x, y, c, _ = _place()

        def load(w, kk):
            return pltpu.make_async_copy(ins[w].at[kk, 1 - c], stage[w].at[kk % 2], load_sems.at[w, kk])

        def send(w, kk):
            return pltpu.make_async_remote_copy(src_ref=stage[w].at[kk % 2], dst_ref=got[w].at[kk], send_sem=send_sems.at[w, kk],
                                                recv_sem=recv_sems.at[w, kk], device_id=(x, y, 1 - c), device_id_type=MESH)

        for kk in range(2):
            for w in range(nw):
                load(w, kk).start()
        for kk in range(NSH):
            for w in range(nw):
                load(w, kk).wait()
                send(w, kk).start()
            if kk + 2 < NSH:
                for w in range(nw):
                    send(w, kk).wait_send()
                    load(w, kk + 2).start()
        for kk in range(NSH - 2, NSH):
            for w in range(nw):
                send(w, kk).wait_send()
        for kk in range(NSH):
            for w in range(nw):
                send(w, kk).wait_recv()

    return pl.pallas_call(
        body, name=name, out_shape=[_sds((NSH,) + g.shape[2:], bf16) for g in grads], in_specs=[HBM] * nw, out_specs=[HBM] * nw,
        scratch_shapes=[pltpu.VMEM((2,) + g.shape[2:], bf16) for g in grads]
        + [pltpu.SemaphoreType.DMA((nw, NSH)), pltpu.SemaphoreType.DMA((nw, NSH)), pltpu.SemaphoreType.DMA((nw, NSH))],
        compiler_params=pltpu.CompilerParams(vmem_limit_bytes=VMEM_LIMIT),
    )(*grads)


def _rs_pair_gather(name, halves):
    nw = len(halves)

    def body(*refs):
        ins, outs, stage = refs[:nw], refs[nw:2 * nw], refs[2 * nw:3 * nw]
        send_sems, recv_sems, local_sems, stage_sems = refs[3 * nw:]
        x, y, c, _ = _place()
        loads = [pltpu.make_async_copy(ins[w], stage[w], stage_sems.at[w]) for w in range(nw)]
        for cp in loads:
            cp.start()
        local, cps = [], []
        for w in range(nw):
            loads[w].wait()
            local.append(pltpu.make_async_copy(stage[w], outs[w].at[c], local_sems.at[w]))
            cps.append(pltpu.make_async_remote_copy(src_ref=stage[w], dst_ref=outs[w].at[c], send_sem=send_sems.at[w],
                                                    recv_sem=recv_sems.at[w], device_id=(x, y, 1 - c), device_id_type=MESH))
            local[w].start()
            cps[w].start()
        for w in range(nw):
            pltpu.make_async_remote_copy(src_ref=stage[w], dst_ref=outs[w].at[1 - c], send_sem=send_sems.at[w], recv_sem=recv_sems.at[w],
                                         device_id=(x, y, 1 - c), device_id_type=MESH).wait_recv()
        for cp in cps:
            cp.wait_send()
        for cp in local:
            cp.wait()

    return pl.pallas_call(
        body, name=name, out_shape=[_sds((2,) + h.shape, f32) for h in halves], in_specs=[HBM] * nw, out_specs=[HBM] * nw,
        scratch_shapes=[pltpu.VMEM(h.shape, f32) for h in halves]
        + [pltpu.SemaphoreType.DMA((nw,)), pltpu.SemaphoreType.DMA((nw,)), pltpu.SemaphoreType.DMA((nw,)), pltpu.SemaphoreType.DMA((nw,))],
        compiler_params=pltpu.CompilerParams(vmem_limit_bytes=VMEM_LIMIT),
    )(*halves)


def _row_tile(r, c, nbuf):
    budget = 24 * 1024 * 1024 // (2 * nbuf * 4 * c)
    t = 8
    while t * 2 <= budget and r % (t * 2) == 0:
        t *= 2
    return t


def _cast_bf16(name, a):
    r, c = a.shape
    tr = _row_tile(r, c, 2)

    def body(a_ref, o_ref):
        o_ref[...] = a_ref[...].astype(bf16)

    spec = pl.BlockSpec((tr, c), lambda i: (i, 0))
    return pl.pallas_call(body, name=name, grid=(r // tr,), in_specs=[spec], out_specs=spec, out_shape=_sds((r, c), bf16),
                          compiler_params=_params(("parallel",)))(a)


def _pair_sum(name, core, grads, got):
    _, _, rh, c = grads.shape
    tr = _row_tile(rh, c, 2)

    def body(c_ref, a_ref, b_ref, o_ref):
        o_ref[...] = (a_ref[...].astype(f32) + b_ref[...].astype(f32)).astype(bf16)

    spec = pl.BlockSpec((None, tr, c), lambda k, i, c_ref: (k, i, 0))
    return pl.pallas_call(
        body, name=name, out_shape=_sds((NSH, rh, c), bf16),
        grid_spec=pltpu.PrefetchScalarGridSpec(
            num_scalar_prefetch=1, grid=(NSH, rh // tr),
            in_specs=[pl.BlockSpec((None, None, tr, c), lambda k, i, c_ref: (k, c_ref[0], i, 0)), spec], out_specs=spec),
        compiler_params=_params(("parallel", "parallel")))(core, grads, got)


def _chip_sum(name, chip, sums, lands):
    _, rh, c = sums.shape
    tr = _row_tile(rh, c, 4)

    def body(k_ref, own_ref, l_ref, o_ref):
        own = own_ref[...].astype(f32)
        acc = None
        for j in range(NSH):
            term = jnp.where(k_ref[0] == j, own, l_ref[j].astype(f32))
            acc = term if acc is None else acc + term
        o_ref[...] = acc

    return pl.pallas_call(
        body, name=name, out_shape=_sds((rh, c), f32),
        grid_spec=pltpu.PrefetchScalarGridSpec(
            num_scalar_prefetch=1, grid=(rh // tr,),
            in_specs=[pl.BlockSpec((None, tr, c), lambda i, k_ref: (k_ref[0], i, 0)), pl.BlockSpec((NSH, tr, c), lambda i, k_ref: (0, i, 0))],
            out_specs=pl.BlockSpec((tr, c), lambda i, k_ref: (i, 0))),
        compiler_params=_params(("parallel",)))(chip, sums, lands)


def _mods_part(cond16, w_ada, b_part):
    n = w_ada.shape[1]
    tn = 512

    def body(c_ref, w_ref, b_ref, o_ref):
        cv = c_ref[...]
        o_ref[...] = _dot(cv * _sigmoid(cv), w_ref[...]) + b_ref[...]

    return pl.pallas_call(
        body, name="mods_part", grid=(n // tn,),
        in_specs=[pl.BlockSpec((16, D), lambda j: (0, 0)), pl.BlockSpec((D, tn), lambda j: (0, j)), pl.BlockSpec((1, tn), lambda j: (0, j))],
        out_specs=pl.BlockSpec((16, tn), lambda j: (0, j)), out_shape=_sds((16, n), f32), compiler_params=_params(("parallel",)),
    )(cond16, w_ada, b_part)


def _grad_w_ada(cond16, dm16):
    n = dm16.shape[1]
    tr = 256

    def body(c_ref, d_ref, o_ref):
        cv = c_ref[...]
        o_ref[...] = _dot(cv * _sigmoid(cv), d_ref[...], ta=True)

    return pl.pallas_call(
        body, name="grad_w_ada", grid=(D // tr,),
        in_specs=[pl.BlockSpec((16, tr), lambda i: (0, i)), pl.BlockSpec((16, n), lambda i: (0, 0))],
        out_specs=pl.BlockSpec((tr, n), lambda i: (i, 0)), out_shape=_sds((D, n), f32), compiler_params=_params(("parallel",)),
    )(cond16, dm16)


def _adamw(name, w, g, m, v):
    r, c = w.shape
    tr = _row_tile(r, c, 7)

    def body(w_ref, g_ref, m_ref, v_ref, d_ref, nm_ref, nv_ref):
        gv = g_ref[...]
        nm = ADAM_B1 * m_ref[...] + (1.0 - ADAM_B1) * gv
        nv = ADAM_B2 * v_ref[...] + (1.0 - ADAM_B2) * (gv * gv)
        nm_ref[...] = nm
        nv_ref[...] = nv
        m_hat = nm / (1.0 - ADAM_B1 ** ADAM_STEP)
        v_hat = nv / (1.0 - ADAM_B2 ** ADAM_STEP)
        d_ref[...] = -ADAM_LR * (m_hat / (jnp.sqrt(v_hat) + ADAM_EPS) + ADAM_WD * w_ref[...])

    spec = pl.BlockSpec((tr, c), lambda i: (i, 0))
    return pl.pallas_call(body, name=name, grid=(r // tr,), in_specs=[spec] * 4, out_specs=[spec] * 3, out_shape=[_sds((r, c), f32)] * 3,
                          compiler_params=_params(("parallel",)))(w, g, m, v)


def _pack(parts, rows):
    flat = []
    for p in parts:
        p = p.reshape(-1)
        flat.append(jnp.pad(p, (0, (-p.shape[0]) % 128)))
    v = jnp.concatenate(flat)
    return jnp.pad(v, (0, rows * 128 - v.shape[0])).reshape(rows, 128)


def _unpack(packed, sizes):
    lead = packed.shape[:-2]
    flat = packed.reshape(lead + (-1,))
    out, off = [], 0
    for n in sizes:
        out.append(flat[..., off:off + n])
        off += n + (-n) % 128
    return out


BIG = ("w_in", "w_out", "w_gate", "w_up", "w_down")
SMALL = ("b_ada", "g_mix", "conv_b", "dt_bias", "a_log", "d_skip", "g_att_out", "g_ssd_out", "g_ffn", "g_final", "rel_bias", "conv_w")
ORDER = ("w_ada", "b_ada", "g_mix", "w_in", "rel_bias", "conv_w", "conv_b", "dt_bias", "a_log", "d_skip", "g_att_out", "g_ssd_out",
         "w_out", "g_ffn", "w_gate", "w_up", "w_down", "g_final")
REL_SH = N_REL // NSH
CONVW_SH = XBC // NSH
ADA_SH = 6 * D // NSH


class _Exchange:
    def __init__(self, shards, core, chip):
        ssem, rsem, thru, lands, _ = _split_start("gather_start", _gather_copies, shards, [(NSH,) + s.shape for s in shards])
        self.gathered = {n: (ssem[i], rsem[i], thru[i], lands[i]) for i, n in enumerate(BIG)}
        self.core, self.chip = core, chip
        self.pending = []

    def _whole(self, names, after):
        ssem, rsem, thru, lands = zip(*[self.gathered[n] for n in names])
        tag = "_".join(names)
        thru, lands = _split_wait("gather_wait_" + tag, _gather_copies, ssem, rsem, thru, lands, after)
        return _gather_forward("gather_forward_" + tag, thru, lands)

    def w_in(self, after):
        (win4,) = self._whole(("w_in",), after)
        win = jnp.transpose(win4.reshape(NSH, D, IN_SHP)[:, :, :IN_SH], (1, 0, 2)).reshape(D, IN_COLS)
        return jnp.pad(win, ((0, 0), (0, IN_P - IN_COLS)))

    def w_out(self, after):
        (wout4,) = self._whole(("w_out",), after)
        return wout4.reshape(D, D)

    def ffn(self, after):
        wg4, wu4, wd4 = self._whole(("w_gate", "w_up", "w_down"), after)
        return wg4.reshape(NSH, D, FSH), wu4.reshape(NSH, D, FSH), wd4.reshape(NSH, FSH, D)

    def grad(self, names, grads):
        tag = "_".join(names)
        stacked = [g.reshape(NSH, 2, g.shape[1] // 2, g.shape[2]) for g in grads]
        got = _rs_pair_exchange("rs_pair_exchange_" + tag, stacked)
        sums = [_pair_sum("pair_sum_" + n, self.core, o, g) for n, o, g in zip(names, stacked, got)]
        self.pending.append((names, _split_start("rs_start_" + tag, _reduce_copies, sums, [s.shape for s in sums])))

    def finish(self, after):
        grads = {}
        for names, (ssem, rsem, sums, lands, _) in self.pending:
            tag = "_".join(names)
            sums, lands = _split_wait("rs_wait_" + tag, _reduce_copies, ssem, rsem, sums, lands, after)
            halves = [_chip_sum("chip_sum_" + n, self.chip, sm, ld) for n, sm, ld in zip(names, sums, lands)]
            for n, f in zip(names, _rs_pair_gather("rs_pair_gather_" + tag, halves)):
                grads[n] = f.reshape(2 * f.shape[1], f.shape[2])
        return grads


def kernel(x, c, w_ada, b_ada, g_mix, w_in, rel_bias, conv_w, conv_b, dt_bias, a_log, d_skip, g_att_out, g_ssd_out, w_out, g_ffn, w_gate, w_up, w_down, g_final, loss_target, m_w_ada, m_b_ada, m_g_mix, m_w_in, m_rel_bias, m_conv_w, m_conv_b, m_dt_bias, m_a_log, m_d_skip, m_g_att_out, m_g_ssd_out, m_w_out, m_g_ffn, m_w_gate, m_w_up, m_w_down, m_g_final, v_w_ada, v_b_ada, v_g_mix, v_w_in, v_rel_bias, v_conv_w, v_conv_b, v_dt_bias, v_a_log, v_d_skip, v_g_att_out, v_g_ssd_out, v_w_out, v_g_ffn, v_w_gate, v_w_up, v_w_down, v_g_final):
    args = dict(locals())
    w = {n: args[n] for n in ORDER}
    m = {n: args["m_" + n] for n in ORDER}
    v = {n: args["v_" + n] for n in ORDER}
    ix, iy, ic = lax.axis_index("x"), lax.axis_index("y"), lax.axis_index("c")
    chip = 2 * ix + iy
    dev = 2 * chip + ic
    s = x.shape[1]

    shards = [
        _cast_bf16("cast_w_in", jnp.pad(w_in[0], ((0, 0), (0, IN_SHP - IN_SH)))).reshape(2, D // 2, IN_SHP),
        _cast_bf16("cast_w_out", w_out[0]).reshape(2, D // NSH // 2, D),
        _cast_bf16("cast_w_gate", w_gate[0]).reshape(2, D // 2, FSH),
        _cast_bf16("cast_w_up", w_up[0]).reshape(2, D // 2, FSH),
        _cast_bf16("cast_w_down", w_down[0]).reshape(2, FSH // 2, D),
    ]
    exchange = _Exchange(shards, jnp.reshape(ic, (1,)).astype(jnp.int32), jnp.reshape(chip, (1,)).astype(jnp.int32))

    g1 = _allgather8("gather_inputs", _pack([c[0], rel_bias[0], conv_w[0]], 40))
    c_all, rel_sh, convw_sh = _unpack(g1, [D, NH * REL_SH, 4 * CONVW_SH])
    rel_full = jnp.concatenate([rel_sh[2 * k].reshape(NH, REL_SH) for k in range(NSH)], axis=1)
    convw_full = jnp.concatenate([convw_sh[2 * k].reshape(4, CONVW_SH) for k in range(NSH)], axis=1)
    cond16 = jnp.pad(c_all, ((0, 8), (0, 0)))
    b_part = lax.dynamic_slice_in_dim(b_ada, chip * ADA_SH, ADA_SH, axis=1)
    mods_part = _mods_part(cond16, w_ada[0], b_part)[:N_DEV]
    g2 = _allgather8("gather_mods", mods_part.reshape(N_DEV * ADA_SH // 128, 128))
    mods_all = jnp.concatenate([g2[2 * k].reshape(N_DEV, ADA_SH) for k in range(NSH)], axis=1)
    mods = lax.dynamic_slice_in_dim(mods_all, dev, 1, axis=0)

    loss, grad_x, dmods, small = _local_step(
        x[0], loss_target[0], mods, g_mix, rel_full, convw_full, conv_b, dt_bias, a_log, d_skip, g_att_out, g_ssd_out, g_ffn,
        g_final[None, :], exchange)

    small_names = ("g_mix", "conv_b", "dt_bias", "a_log", "d_skip", "g_att_out", "g_ssd_out", "g_ffn", "g_final", "rel_bias", "conv_w")
    g3 = _allgather8("gather_small_grads", _pack([dmods] + [small[n] for n in small_names], 264))
    sizes = [6 * D] + [int(np.prod(small[n].shape)) for n in small_names]
    dmods_all = _unpack(g3, sizes)[0]
    summed = _unpack(_sum8(g3), sizes)
    grads = {"b_ada": summed[0].reshape(1, 6 * D)}
    for n, val in zip(small_names, summed[1:]):
        grads[n] = val.reshape(small[n].shape)
    grads["rel_bias"] = lax.dynamic_slice_in_dim(grads["rel_bias"], chip * REL_SH, REL_SH, axis=1)
    grads["conv_w"] = lax.dynamic_slice_in_dim(grads["conv_w"], chip * CONVW_SH, CONVW_SH, axis=1)
    grads["g_final"] = grads["g_final"].reshape(D)
    dm16 = jnp.pad(lax.dynamic_slice_in_dim(dmods_all, chip * ADA_SH, ADA_SH, axis=1), ((0, 8), (0, 0)))
    grads["w_ada"] = _grad_w_ada(cond16, dm16)

    delta, new_m, new_v = {}, {}, {}
    delta["w_ada"], new_m["w_ada"], new_v["w_ada"] = _adamw("adamw_w_ada", w_ada[0], grads["w_ada"], m_w_ada[0], v_w_ada[0])
    grads.update(exchange.finish(grad_x))
    grads["w_in"] = grads["w_in"][:, :IN_SH]
    for n in BIG:
        delta[n], new_m[n], new_v[n] = _adamw("adamw_" + n, w[n][0], grads[n], m[n][0], v[n][0])
    sw = _pack([w[n] for n in SMALL], 200)
    sg = _pack([grads[n] for n in SMALL], 200)
    sm = _pack([m[n] for n in SMALL], 200)
    sv = _pack([v[n] for n in SMALL], 200)
    ssz = [int(np.prod(w[n].shape)) for n in SMALL]
    for dst, packed in zip((delta, new_m, new_v), _adamw("adamw_small", sw, sg, sm, sv)):
        for n, val in zip(SMALL, _unpack(packed, ssz)):
            dst[n] = val

    def shaped(d, n):
        return d[n].reshape(w[n].shape)

    total = lax.psum(loss, ("x", "y", "c"))
    return (total, grad_x[None], *[shaped(grads, n) for n in ORDER], *[shaped(delta, n) for n in ORDER],
            *[shaped(new_m, n) for n in ORDER], *[shaped(new_v, n) for n in ORDER])
```

```python
import functools

import numpy as np
import jax
import jax.numpy as jnp
from jax import lax
from jax.experimental import pallas as pl
from jax.experimental.pallas import tpu as pltpu

f32 = jnp.float32
bf16 = jnp.bfloat16
HIGHEST = lax.Precision.HIGHEST
MESH = pl.DeviceIdType.MESH

D = 2048
CHUNK = 64
LEFT = 8
BAND = (LEFT + 1) * CHUNK
BANDP = 640
PADK = LEFT * CHUNK
NH = 16
HD = 64
ATT_W = NH * HD
SSD_W = 1024
NG = 2
NSTATE = 128
GW = SSD_W // NG
XBC = SSD_W + 2 * NG * NSTATE
N_REL = 320
REL_CLIP = 256
FFN = 5632
NSH = 4
FSH = FFN // NSH
IN_COLS = 5648
IN_SH = IN_COLS // NSH
IN_SHP = 1536
IN_A = 3 * ATT_W
IN_B = 2688
IN_P = IN_A + IN_B
EPS = 1e-6
N_DEV = 8

ADAM_LR = 0.001
ADAM_B1 = 0.9
ADAM_B2 = 0.999
ADAM_EPS = 1e-08
ADAM_WD = 0.01
ADAM_STEP = 10

VMEM_LIMIT = 56 * 1024 * 1024


def _params(sem):
    return pltpu.CompilerParams(dimension_semantics=sem, vmem_limit_bytes=VMEM_LIMIT)


def _sds(shape, dtype):
    return jax.ShapeDtypeStruct(shape, dtype)


def _fold8(v):
    r, w = v.shape
    return jnp.sum(v.reshape(r // 8, 8, w), axis=0)


def _sigmoid(v):
    return 1.0 / (1.0 + jnp.exp(-v))


def _softplus(v):
    return jnp.maximum(v, 0.0) + jnp.log(1.0 + jnp.exp(-jnp.abs(v)))


def _dot(a, b, ta=False, tb=False):
    dn = (((0 if ta else 1,), (1 if tb else 0,)), ((), ()))
    return lax.dot_general(a.astype(bf16), b.astype(bf16), dn, preferred_element_type=f32)


def _matmul(name, a, b, *, grid, a_spec, b_spec, o_spec, o_shape, o_dtype, acc_shape, ta=False, tb=False):
    nk = grid[2]

    def body(a_ref, b_ref, o_ref, acc_ref):
        p = _dot(a_ref[...], b_ref[...], ta, tb)
        if nk == 1:
            o_ref[...] = p.astype(o_ref.dtype)
        else:
            k = pl.program_id(2)

            @pl.when(k == 0)
            def _():
                acc_ref[...] = p

            @pl.when(jnp.logical_and(k > 0, k < nk - 1))
            def _():
                acc_ref[...] += p

            @pl.when(k == nk - 1)
            def _():
                o_ref[...] = (acc_ref[...] + p).astype(o_ref.dtype)

    return pl.pallas_call(
        body, name=name, grid=grid, in_specs=[a_spec, b_spec], out_specs=o_spec,
        out_shape=_sds(o_shape, o_dtype), scratch_shapes=[pltpu.VMEM(acc_shape if nk > 1 else (8, 128), f32)],
        compiler_params=_params(("parallel", "parallel", "arbitrary")),
    )(a, b)


def _mm_nn_fullk(name, a, b, tm, tn, o_dtype, n=None):
    m, k = a.shape
    n = b.shape[1] if n is None else n
    return _matmul(name, a, b, grid=(m // tm, n // tn, 1),
                   a_spec=pl.BlockSpec((tm, k), lambda i, j, kk: (i, 0)),
                   b_spec=pl.BlockSpec((k, tn), lambda i, j, kk: (0, j)),
                   o_spec=pl.BlockSpec((tm, tn), lambda i, j, kk: (i, j)),
                   o_shape=(m, n), o_dtype=o_dtype, acc_shape=(tm, tn))


def _mm_nt(name, a, b, tm, tn, tk, o_dtype):
    m, k = a.shape
    n = b.shape[0]
    return _matmul(name, a, b, grid=(m // tm, n // tn, k // tk), tb=True,
                   a_spec=pl.BlockSpec((tm, tk), lambda i, j, kk: (i, kk)),
                   b_spec=pl.BlockSpec((tn, tk), lambda i, j, kk: (j, kk)),
                   o_spec=pl.BlockSpec((tm, tn), lambda i, j, kk: (i, j)),
                   o_shape=(m, n), o_dtype=o_dtype, acc_shape=(tm, tn))


def _mm_tn(name, a, b, tm, tn, tk, o_dtype):
    k, m = a.shape
    n = b.shape[1]
    return _matmul(name, a, b, grid=(m // tm, n // tn, k // tk), ta=True,
                   a_spec=pl.BlockSpec((tk, tm), lambda i, j, kk: (kk, i)),
                   b_spec=pl.BlockSpec((tk, tn), lambda i, j, kk: (kk, j)),
                   o_spec=pl.BlockSpec((tm, tn), lambda i, j, kk: (i, j)),
                   o_shape=(m, n), o_dtype=o_dtype, acc_shape=(tm, tn))


def _ffn_up(h2b, wg4, wu4, tm):
    s = h2b.shape[0]

    def body(h_ref, wg_ref, wu_ref, g_ref, u_ref, a_ref):
        h = h_ref[...]
        g = _dot(h, wg_ref[...])
        u = _dot(h, wu_ref[...])
        g_ref[...] = g
        u_ref[...] = u
        a_ref[...] = (g * _sigmoid(g) * u).astype(bf16)

    wspec = pl.BlockSpec((None, D, FSH), lambda k, i: (k, 0, 0))
    ospec = pl.BlockSpec((tm, FSH), lambda k, i: (i, k))
    return pl.pallas_call(
        body, name="ffn_up", grid=(NSH, s // tm),
        in_specs=[pl.BlockSpec((tm, D), lambda k, i: (i, 0)), wspec, wspec],
        out_specs=[ospec, ospec, ospec],
        out_shape=[_sds((s, FFN), f32), _sds((s, FFN), f32), _sds((s, FFN), bf16)],
        compiler_params=_params(("parallel", "parallel")),
    )(h2b, wg4, wu4)


def _ffn_down(act, wd4, tm):
    s = act.shape[0]
    return _matmul("ffn_down", act, wd4, grid=(s // tm, 1, NSH),
                   a_spec=pl.BlockSpec((tm, FSH), lambda i, j, k: (i, k)),
                   b_spec=pl.BlockSpec((None, FSH, D), lambda i, j, k: (k, 0, 0)),
                   o_spec=pl.BlockSpec((tm, D), lambda i, j, k: (i, 0)),
                   o_shape=(s, D), o_dtype=f32, acc_shape=(tm, D))


def _ffn_dact(dffn, wd4, gate, up, tm):
    s = dffn.shape[0]

    def body(d_ref, w_ref, g_ref, u_ref, dg_ref, du_ref):
        dact = _dot(d_ref[...], w_ref[...], tb=True)
        g = g_ref[...]
        sg = _sigmoid(g)
        dg_ref[...] = (dact * u_ref[...] * (sg * (1.0 + g * (1.0 - sg)))).astype(bf16)
        du_ref[...] = (dact * (g * sg)).astype(bf16)

    blk = pl.BlockSpec((tm, FSH), lambda k, i: (i, k))
    return pl.pallas_call(
        body, name="ffn_dact", grid=(NSH, s // tm),
        in_specs=[pl.BlockSpec((tm, D), lambda k, i: (i, 0)), pl.BlockSpec((None, FSH, D), lambda k, i: (k, 0, 0)), blk, blk],
        out_specs=[blk, blk], out_shape=[_sds((s, FFN), bf16), _sds((s, FFN), bf16)],
        compiler_params=_params(("parallel", "parallel")),
    )(dffn, wd4, gate, up)


def _ffn_dh(dgate, dup, wg4, wu4, tm):
    s = dgate.shape[0]

    def body(dg_ref, du_ref, wg_ref, wu_ref, o_ref, acc_ref):
        k = pl.program_id(1)
        p = _dot(dg_ref[...], wg_ref[...], tb=True) + _dot(du_ref[...], wu_ref[...], tb=True)

        @pl.when(k == 0)
        def _():
            acc_ref[...] = p

        @pl.when(jnp.logical_and(k > 0, k < NSH - 1))
        def _():
            acc_ref[...] += p

        @pl.when(k == NSH - 1)
        def _():
            o_ref[...] = acc_ref[...] + p

    aspec = pl.BlockSpec((tm, FSH), lambda i, k: (i, k))
    wspec = pl.BlockSpec((None, D, FSH), lambda i, k: (k, 0, 0))
    return pl.pallas_call(
        body, name="ffn_dh", grid=(s // tm, NSH), in_specs=[aspec, aspec, wspec, wspec],
        out_specs=pl.BlockSpec((tm, D), lambda i, k: (i, 0)), out_shape=_sds((s, D), f32),
        scratch_shapes=[pltpu.VMEM((tm, D), f32)], compiler_params=_params(("parallel", "arbitrary")),
    )(dgate, dup, wg4, wu4)


def _grad_cols4(name, h, dy, tm, tk):
    s = h.shape[0]
    return _matmul(name, h, dy, grid=(NSH, D // tm, s // tk), ta=True,
                   a_spec=pl.BlockSpec((tk, tm), lambda k, i, kk: (kk, i)),
                   b_spec=pl.BlockSpec((tk, FSH), lambda k, i, kk: (kk, k)),
                   o_spec=pl.BlockSpec((None, tm, FSH), lambda k, i, kk: (k, i, 0)),
                   o_shape=(NSH, D, FSH), o_dtype=bf16, acc_shape=(tm, FSH))


def _grad_wdown4(act, dffn, tn, tk):
    s = act.shape[0]
    return _matmul("grad_w_down", act, dffn, grid=(NSH, D // tn, s // tk), ta=True,
                   a_spec=pl.BlockSpec((tk, FSH), lambda k, j, kk: (kk, k)),
                   b_spec=pl.BlockSpec((tk, tn), lambda k, j, kk: (kk, j)),
                   o_spec=pl.BlockSpec((None, FSH, tn), lambda k, j, kk: (k, 0, j)),
                   o_shape=(NSH, FSH, D), o_dtype=bf16, acc_shape=(FSH, tn))


def _row_spec(w):
    return pl.BlockSpec((1, w), lambda i: (0, 0))


def _tile_spec(tm, w, col=0):
    return pl.BlockSpec((tm, w), lambda i: (i, col))


def _norm_mod(name, x, g, sc, sh, tm):
    s = x.shape[0]

    def body(x_ref, g_ref, sc_ref, sh_ref, o_ref):
        xv = x_ref[...]
        r = lax.rsqrt(jnp.mean(xv * xv, axis=-1, keepdims=True) + EPS)
        o_ref[...] = (xv * r * g_ref[...] * (1.0 + sc_ref[...]) + sh_ref[...]).astype(bf16)

    return pl.pallas_call(
        body, name=name, grid=(s // tm,), in_specs=[_tile_spec(tm, D), _row_spec(D), _row_spec(D), _row_spec(D)],
        out_specs=_tile_spec(tm, D), out_shape=_sds((s, D), bf16), compiler_params=_params(("parallel",)),
    )(x, g, sc, sh)


def _resid_norm_mod(x, gt, mix, g, sc, sh, tm):
    s = x.shape[0]

    def body(x_ref, gt_ref, m_ref, g_ref, sc_ref, sh_ref, x2_ref, h_ref):
        xv = x_ref[...] + gt_ref[...] * m_ref[...]
        x2_ref[...] = xv
        r = lax.rsqrt(jnp.mean(xv * xv, axis=-1, keepdims=True) + EPS)
        h_ref[...] = (xv * r * g_ref[...] * (1.0 + sc_ref[...]) + sh_ref[...]).astype(bf16)

    return pl.pallas_call(
        body, name="resid_norm_mod", grid=(s // tm,),
        in_specs=[_tile_spec(tm, D), _row_spec(D), _tile_spec(tm, D), _row_spec(D), _row_spec(D), _row_spec(D)],
        out_specs=[_tile_spec(tm, D), _tile_spec(tm, D)], out_shape=[_sds((s, D), f32), _sds((s, D), bf16)],
        compiler_params=_params(("parallel",)),
    )(x, gt, mix, g, sc, sh)


def _final_fwd_bwd(x2, ffn, gt2, g, tgt, tm):
    s = x2.shape[0]
    n = s // tm

    def body(x_ref, f_ref, gt_ref, g_ref, t_ref, dx_ref, df_ref, loss_ref, dg_ref, dgt_ref, a_loss, a_dg, a_dgt):
        i = pl.program_id(0)

        @pl.when(i == 0)
        def _():
            a_loss[...] = jnp.zeros_like(a_loss)
            a_dg[...] = jnp.zeros_like(a_dg)
            a_dgt[...] = jnp.zeros_like(a_dgt)

        fv = f_ref[...]
        gt = gt_ref[...]
        gv = g_ref[...]
        xv = x_ref[...] + gt * fv
        r = lax.rsqrt(jnp.mean(xv * xv, axis=-1, keepdims=True) + EPS)
        xh = xv * r
        e = xh * gv - t_ref[...]
        a_loss[...] += _fold8(e * e)
        dy = e * (1.0 / D)
        a_dg[...] += _fold8(dy * xh)
        t = dy * gv
        dx = r * (t - xh * jnp.mean(t * xh, axis=-1, keepdims=True))
        dx_ref[...] = dx
        a_dgt[...] += _fold8(dx * fv)
        df_ref[...] = (dx * gt).astype(bf16)

        @pl.when(i == n - 1)
        def _():
            tot = jnp.sum(jnp.sum(a_loss[...], axis=0, keepdims=True), axis=1, keepdims=True) * (0.5 / D)
            loss_ref[...] = jnp.broadcast_to(tot, (1, 128))
            dg_ref[...] = jnp.sum(a_dg[...], axis=0, keepdims=True)
            dgt_ref[...] = jnp.sum(a_dgt[...], axis=0, keepdims=True)

    return pl.pallas_call(
        body, name="final_fwd_bwd", grid=(n,),
        in_specs=[_tile_spec(tm, D), _tile_spec(tm, D), _row_spec(D), _row_spec(D), _tile_spec(tm, D)],
        out_specs=[_tile_spec(tm, D), _tile_spec(tm, D), _row_spec(128), _row_spec(D), _row_spec(D)],
        out_shape=[_sds((s, D), f32), _sds((s, D), bf16), _sds((1, 128), f32), _sds((1, D), f32), _sds((1, D), f32)],
        scratch_shapes=[pltpu.VMEM((8, D), f32)] * 3, compiler_params=_params(("arbitrary",)),
    )(x2, ffn, gt2, g, tgt)


def _norm_mod_bwd(name, dh, xin, g, sc, dres, tm, mix=None, gt=None):
    s = dh.shape[0]
    n = s // tm
    with_mix = mix is not None

    def body(*refs):
        if with_mix:
            dh_ref, x_ref, g_ref, sc_ref, dr_ref, m_ref, gt_ref, dx_ref, dm_ref, dsc_ref, dsh_ref, dg_ref, dgt_ref, a_sc, a_sh, a_g, a_gt = refs
        else:
            dh_ref, x_ref, g_ref, sc_ref, dr_ref, dx_ref, dsc_ref, dsh_ref, dg_ref, a_sc, a_sh, a_g = refs
        i = pl.program_id(0)

        @pl.when(i == 0)
        def _():
            a_sc[...] = jnp.zeros_like(a_sc)
            a_sh[...] = jnp.zeros_like(a_sh)
            a_g[...] = jnp.zeros_like(a_g)
            if with_mix:
                a_gt[...] = jnp.zeros_like(a_gt)

        dh = dh_ref[...]
        xv = x_ref[...]
        gv = g_ref[...]
        r = lax.rsqrt(jnp.mean(xv * xv, axis=-1, keepdims=True) + EPS)
        xh = xv * r
        a_sc[...] += _fold8(dh * xh * gv)
        a_sh[...] += _fold8(dh)
        dn = dh * (1.0 + sc_ref[...])
        a_g[...] += _fold8(dn * xh)
        t = dn * gv
        dx = dr_ref[...] + r * (t - xh * jnp.mean(t * xh, axis=-1, keepdims=True))
        dx_ref[...] = dx
        if with_mix:
            a_gt[...] += _fold8(dx * m_ref[...])
            dm_ref[...] = (dx * gt_ref[...]).astype(bf16)

        @pl.when(i == n - 1)
        def _():
            dsc_ref[...] = jnp.sum(a_sc[...], axis=0, keepdims=True)
            dsh_ref[...] = jnp.sum(a_sh[...], axis=0, keepdims=True)
            dg_ref[...] = jnp.sum(a_g[...], axis=0, keepdims=True)
            if with_mix:
                dgt_ref[...] = jnp.sum(a_gt[...], axis=0, keepdims=True)

    tile, row = _tile_spec(tm, D), _row_spec(D)
    if with_mix:
        ins, args = [tile, tile, row, row, tile, tile, row], (dh, xin, g, sc, dres, mix, gt)
        outs = [tile, tile, row, row, row, row]
        shapes = [_sds((s, D), f32), _sds((s, D), bf16)] + [_sds((1, D), f32)] * 4
        nacc = 4
    else:
        ins, args = [tile, tile, row, row, tile], (dh, xin, g, sc, dres)
        outs = [tile, row, row, row]
        shapes = [_sds((s, D), f32)] + [_sds((1, D), f32)] * 3
        nacc = 3
    return pl.pallas_call(
        body, name=name, grid=(n,), in_specs=ins, out_specs=outs, out_shape=shapes,
        scratch_shapes=[pltpu.VMEM((8, D), f32)] * nacc, compiler_params=_params(("arbitrary",)),
    )(*args)


def _mix_pre(att, y, proj2, g_att, g_ssd, tm):
    s = att.shape[0]

    def body(a_ref, y_ref, z_ref, ga_ref, gs_ref, o_ref):
        a = a_ref[...]
        ra = lax.rsqrt(jnp.mean(a * a, axis=-1, keepdims=True) + EPS)
        o_ref[:, 0:ATT_W] = (a * ra * ga_ref[...]).astype(bf16)
        z = z_ref[...]
        u = y_ref[...] * (z * _sigmoid(z))
        ru = lax.rsqrt(jnp.mean(u * u, axis=-1, keepdims=True) + EPS)
        o_ref[:, ATT_W:] = (u * ru * gs_ref[...]).astype(bf16)

    t = _tile_spec(tm, ATT_W)
    return pl.pallas_call(
        body, name="mix_pre", grid=(s // tm,), in_specs=[t, t, t, _row_spec(ATT_W), _row_spec(SSD_W)],
        out_specs=_tile_spec(tm, D), out_shape=_sds((s, D), bf16), compiler_params=_params(("parallel",)),
    )(att, y, proj2, g_att, g_ssd)


def _mix_pre_bwd(dmc, att, y, proj2, g_att, g_ssd, tm):
    s = att.shape[0]
    n = s // tm

    def body(da_ref, ds_ref, a_ref, y_ref, z_ref, ga_ref, gs_ref, datt_ref, dy_ref, dz_ref, dga_ref, dgs_ref, acc_a, acc_s):
        i = pl.program_id(0)

        @pl.when(i == 0)
        def _():
            acc_a[...] = jnp.zeros_like(acc_a)
            acc_s[...] = jnp.zeros_like(acc_s)

        a = a_ref[...]
        ra = lax.rsqrt(jnp.mean(a * a, axis=-1, keepdims=True) + EPS)
        ah = a * ra
        dan = da_ref[...]
        acc_a[...] += _fold8(dan * ah)
        t = dan * ga_ref[...]
        datt_ref[...] = (ra * (t - ah * jnp.mean(t * ah, axis=-1, keepdims=True))).astype(bf16)
        z = z_ref[...]
        yv = y_ref[...]
        sz = _sigmoid(z)
        sil = z * sz
        u = yv * sil
        ru = lax.rsqrt(jnp.mean(u * u, axis=-1, keepdims=True) + EPS)
        uh = u * ru
        dsn = ds_ref[...]
        acc_s[...] += _fold8(dsn * uh)
        t2 = dsn * gs_ref[...]
        du = ru * (t2 - uh * jnp.mean(t2 * uh, axis=-1, keepdims=True))
        dy_ref[...] = du * sil
        dz_ref[...] = (du * yv * (sz * (1.0 + z * (1.0 - sz)))).astype(bf16)

        @pl.when(i == n - 1)
        def _():
            dga_ref[...] = jnp.sum(acc_a[...], axis=0, keepdims=True)
            dgs_ref[...] = jnp.sum(acc_s[...], axis=0, keepdims=True)

    t = _tile_spec(tm, ATT_W)
    row = _row_spec(ATT_W)
    return pl.pallas_call(
        body, name="mix_pre_bwd", grid=(n,),
        in_specs=[_tile_spec(tm, ATT_W, 0), _tile_spec(tm, ATT_W, 1), t, t, t, row, row],
        out_specs=[t, t, t, row, row],
        out_shape=[_sds((s, ATT_W), bf16), _sds((s, SSD_W), f32), _sds((s, SSD_W), bf16), _sds((1, ATT_W), f32), _sds((1, SSD_W), f32)],
        scratch_shapes=[pltpu.VMEM((8, ATT_W), f32)] * 2, compiler_params=_params(("arbitrary",)),
    )(dmc, dmc, att, y, proj2, g_att, g_ssd)


ATT_GROUP = 2


def _pair_rows(qc):
    two = jnp.concatenate([qc, qc], axis=0)
    r = lax.broadcasted_iota(jnp.int32, (2 * CHUNK, 128), 0)
    l = lax.broadcasted_iota(jnp.int32, (2 * CHUNK, 128), 1)
    return jnp.where((r < CHUNK) == (l < HD), two, jnp.zeros_like(two))


def _pair_scores(wt, kb, bias, r0):
    sc = lax.dot_general(wt, kb, (((1,), (1,)), ((), ())), preferred_element_type=f32) * (HD ** -0.5) + bias
    kidx = lax.broadcasted_iota(jnp.int32, sc.shape, 1)
    return jnp.where(r0 + kidx >= PADK, sc, -jnp.inf)


def _softmax_lanes(sc):
    e = jnp.exp(sc - jnp.max(sc, axis=-1, keepdims=True))
    return e / jnp.sum(e, axis=-1, keepdims=True)


def _pair_diag(r):
    lane = lax.broadcasted_iota(jnp.int32, (CHUNK, 128), 1)
    return jnp.where(lane < HD, r[0:CHUNK], r[CHUNK:])


def _pad_keys(k_ref, kp, s):
    kp[0:PADK, :] = jnp.zeros((PADK, 128), bf16)
    kp[PADK:PADK + s, :] = k_ref[...]
    kp[PADK + s:, :] = jnp.zeros((CHUNK, 128), bf16)


def _attn_fwd(qkv, bias2):
    s = qkv.shape[0]
    nc = s // CHUNK
    npair = NH // 2

    def body(q_ref, k_ref, v_ref, b_ref, o_ref, kp, vp):
        _pad_keys(k_ref, kp, s)
        _pad_keys(v_ref, vp, s)

        def group(g, carry):
            r0s = [pl.multiple_of((g * ATT_GROUP + u) * CHUNK, CHUNK) for u in range(ATT_GROUP)]
            scs = [_pair_scores(_pair_rows(q_ref[pl.ds(r0, CHUNK), :]), kp[pl.ds(r0, BANDP), :], b_ref[...], r0) for r0 in r0s]
            ps = [_softmax_lanes(sc).astype(bf16) for sc in scs]
            for r0, p in zip(r0s, ps):
                o_ref[pl.ds(r0, CHUNK), :] = _pair_diag(jnp.dot(p, vp[pl.ds(r0, BANDP), :], preferred_element_type=f32))
            return carry

        lax.fori_loop(0, nc // ATT_GROUP, group, 0)

    return pl.pallas_call(
        body, name="attn_fwd", grid=(npair,),
        in_specs=[pl.BlockSpec((s, 128), lambda p: (0, p)), pl.BlockSpec((s, 128), lambda p: (0, npair + p)),
                  pl.BlockSpec((s, 128), lambda p: (0, 2 * npair + p)), pl.BlockSpec((None, 2 * CHUNK, BANDP), lambda p: (p, 0, 0))],
        out_specs=pl.BlockSpec((s, 128), lambda p: (0, p)), out_shape=_sds((s, ATT_W), f32),
        scratch_shapes=[pltpu.VMEM((PADK + s + CHUNK, 128), bf16)] * 2, compiler_params=_params(("parallel",)),
    )(qkv, qkv, qkv, bias2)


def _attn_bwd(qkv, datt, bias2, bias2t):
    s = qkv.shape[0]
    nc = s // CHUNK
    npair = NH // 2
    rows = PADK + s + CHUNK
    nt = (((1,), (1,)), ((), ()))

    def body(q_ref, k_ref, v_ref, do_ref, b_ref, bt_ref, dq_ref, dk_ref, dv_ref, g_ref, kp, vp, dkp, dvp):
        _pad_keys(k_ref, kp, s)
        _pad_keys(v_ref, vp, s)
        dkp[...] = jnp.zeros_like(dkp)
        dvp[...] = jnp.zeros_like(dvp)
        g_ref[...] = jnp.zeros_like(g_ref)

        def group(g, carry):
            r0s = [pl.multiple_of((g * ATT_GROUP + u) * CHUNK, CHUNK) for u in range(ATT_GROUP)]
            wts = [_pair_rows(q_ref[pl.ds(r0, CHUNK), :]) for r0 in r0s]
            dos = [_pair_rows(do_ref[pl.ds(r0, CHUNK), :]) for r0 in r0s]
            scs = [_pair_scores(wt, kp[pl.ds(r0, BANDP), :], b_ref[...], r0) for wt, r0 in zip(wts, r0s)]
            dps = [lax.dot_general(do, vp[pl.ds(r0, BANDP), :], nt, preferred_element_type=f32) for do, r0 in zip(dos, r0s)]
            scts, dpts = [], []
            for wt, do, r0 in zip(wts, dos, r0s):
                sct = lax.dot_general(kp[pl.ds(r0, BANDP), :], wt, nt, preferred_element_type=f32) * (HD ** -0.5) + bt_ref[...]
                kidx = lax.broadcasted_iota(jnp.int32, sct.shape, 0)
                scts.append(jnp.where(r0 + kidx >= PADK, sct, -jnp.inf))
                dpts.append(lax.dot_general(vp[pl.ds(r0, BANDP), :], do, nt, preferred_element_type=f32))
            for r0, sc, dp in zip(r0s, scs, dps):
                p = _softmax_lanes(sc)
                ds = p * (dp - jnp.sum(p * dp, axis=-1, keepdims=True))
                g_ref[...] += ds
                dq = jnp.dot(ds.astype(bf16), kp[pl.ds(r0, BANDP), :], preferred_element_type=f32)
                dq_ref[pl.ds(r0, CHUNK), :] = (_pair_diag(dq) * (HD ** -0.5)).astype(bf16)
            for r0, wt, do, sct, dpt in zip(r0s, wts, dos, scts, dpts):
                e = jnp.exp(sct - jnp.max(sct, axis=0, keepdims=True))
                pt = e / jnp.sum(e, axis=0, keepdims=True)
                dst = pt * (dpt - jnp.sum(pt * dpt, axis=0, keepdims=True))
                dkp[pl.ds(r0, BANDP), :] += jnp.dot(dst.astype(bf16), wt, preferred_element_type=f32) * (HD ** -0.5)
                dvp[pl.ds(r0, BANDP), :] += jnp.dot(pt.astype(bf16), do, preferred_element_type=f32)
            return carry

        lax.fori_loop(0, nc // ATT_GROUP, group, 0)
        dk_ref[...] = dkp[PADK:PADK + s, :].astype(bf16)
        dv_ref[...] = dvp[PADK:PADK + s, :].astype(bf16)

    col = lambda off: pl.BlockSpec((s, 128), lambda p: (0, off + p))
    return pl.pallas_call(
        body, name="attn_bwd", grid=(npair,),
        in_specs=[col(0), col(npair), col(2 * npair), col(0), pl.BlockSpec((None, 2 * CHUNK, BANDP), lambda p: (p, 0, 0)),
                  pl.BlockSpec((None, BANDP, 2 * CHUNK), lambda p: (p, 0, 0))],
        out_specs=[col(0), col(0), col(0), pl.BlockSpec((None, 2 * CHUNK, BANDP), lambda p: (p, 0, 0))],
        out_shape=[_sds((s, ATT_W), bf16)] * 3 + [_sds((npair, 2 * CHUNK, BANDP), f32)],
        scratch_shapes=[pltpu.VMEM((rows, 128), bf16)] * 2 + [pltpu.VMEM((rows, 128), f32)] * 2,
        compiler_params=_params(("parallel",)),
    )(qkv, qkv, qkv, datt, bias2, bias2t)


def _rel_tables():
    onehot = np.zeros((BANDP, N_REL), np.float32)
    for j in range(BAND + CHUNK - 1):
        o = j - (CHUNK - 1)
        onehot[j, int(np.clip(PADK - o, -(CHUNK - 1), REL_CLIP)) + CHUNK - 1] = 1.0
    return onehot, np.ascontiguousarray(np.eye(CHUNK, dtype=np.float32)[::-1])


def _expand_bias(rel):
    ext = jnp.concatenate([jnp.broadcast_to(rel[:, N_REL - 1:], (NH, N_REL - 1)), rel[:, ::-1],
                           jnp.zeros((NH, BANDP - BAND + 1), f32)], axis=1)
    band = jnp.stack([ext[:, CHUNK - 1 - q:CHUNK - 1 - q + BANDP] for q in range(CHUNK)], axis=1)
    band = jnp.where(np.arange(BANDP) < BAND, band, -jnp.inf)
    return band.reshape(NH // 2, 2 * CHUNK, BANDP)


def _rel_bias_grad(gband):
    def body(g_ref, m_ref, flip_ref, o_ref, d2):
        for h in range(NH):
            rev = jnp.dot(flip_ref[...], g_ref[h], precision=HIGHEST, preferred_element_type=f32)
            rolled = pltpu.roll(rev, 0, 1, stride=1, stride_axis=0)
            d2[h:h + 1, :] = jnp.sum(rolled, axis=0, keepdims=True)
        o_ref[...] = jnp.dot(d2[...], m_ref[...], precision=HIGHEST, preferred_element_type=f32)

    onehot, flip = _rel_tables()
    return pl.pallas_call(
        body, name="rel_bias_grad", out_shape=_sds((NH, N_REL), f32), scratch_shapes=[pltpu.VMEM((NH, BANDP), f32)],
    )(gband, jnp.asarray(onehot), jnp.asarray(flip))


XBC_BLK = 512
XBC_COL0 = SSD_W // XBC_BLK
DT_COL = (SSD_W + XBC) // 128


def _conv_taps(ext, w_ref, b_ref, tm):
    n = ext.shape[0]
    pre = w_ref[3:4, :] * ext + b_ref[...]
    for j in range(3):
        pre = pre + w_ref[j:j + 1, :] * pltpu.roll(ext, 3 - j, 0)
    return pre


def _ssd_conv(proj2, conv_w, conv_b, tm):
    s = proj2.shape[0]
    nb = XBC // XBC_BLK

    def body(x_ref, p_ref, w_ref, b_ref, o_ref):
        i = pl.program_id(1)
        prev = jnp.where(i > 0, p_ref[...], 0.0)
        ext = jnp.concatenate([prev, x_ref[...]], axis=0)
        pre = _conv_taps(ext, w_ref, b_ref, tm)[8:8 + tm]
        o_ref[...] = pre * _sigmoid(pre)

    return pl.pallas_call(
        body, name="ssd_conv", grid=(nb, s // tm),
        in_specs=[pl.BlockSpec((tm, XBC_BLK), lambda j, i: (i, XBC_COL0 + j)),
                  pl.BlockSpec((8, XBC_BLK), lambda j, i: (jnp.maximum(i * (tm // 8) - 1, 0), XBC_COL0 + j)),
                  pl.BlockSpec((4, XBC_BLK), lambda j, i: (0, j)), pl.BlockSpec((1, XBC_BLK), lambda j, i: (0, j))],
        out_specs=pl.BlockSpec((tm, XBC_BLK), lambda j, i: (i, j)), out_shape=_sds((s, XBC), f32),
        compiler_params=_params(("parallel", "parallel")),
    )(proj2, proj2, conv_w, conv_b)


def _ssd_conv_bwd(dxbc, proj2, conv_w, conv_b, tm):
    s = proj2.shape[0]
    nb = XBC // XBC_BLK
    n = s // tm
    last8 = s // 8 - 1

    def body(x_ref, xp_ref, xn_ref, d_ref, dn_ref, w_ref, b_ref, o_ref, dw_ref, db_ref):
        i = pl.program_id(1)

        @pl.when(i == 0)
        def _():
            dw_ref[...] = jnp.zeros_like(dw_ref)
            db_ref[...] = jnp.zeros_like(db_ref)

        prev = jnp.where(i > 0, xp_ref[...], 0.0)
        ext = jnp.concatenate([prev, x_ref[...], xn_ref[...]], axis=0)
        pre = _conv_taps(ext, w_ref, b_ref, tm)
        sg = _sigmoid(pre)
        dnext = jnp.where(i < n - 1, dn_ref[...], 0.0)
        dext = jnp.concatenate([jnp.zeros((8, XBC_BLK), f32), d_ref[...], dnext], axis=0)
        dpre = dext * (sg * (1.0 + pre * (1.0 - sg)))
        rows = tm + 16
        dx = w_ref[3:4, :] * dpre
        for j in range(3):
            dx = dx + w_ref[j:j + 1, :] * pltpu.roll(dpre, rows - (3 - j), 0)
        o_ref[...] = dx[8:8 + tm].astype(bf16)
        dcur = dpre[8:8 + tm]
        db_ref[...] += jnp.sum(dcur, axis=0, keepdims=True)
        dw_ref[3:4, :] += jnp.sum(dcur * ext[8:8 + tm], axis=0, keepdims=True)
        for j in range(3):
            dw_ref[j:j + 1, :] += jnp.sum(dcur * pltpu.roll(ext, 3 - j, 0)[8:8 + tm], axis=0, keepdims=True)

    xcol = lambda j: XBC_COL0 + j
    return pl.pallas_call(
        body, name="ssd_conv_bwd", grid=(nb, n),
        in_specs=[pl.BlockSpec((tm, XBC_BLK), lambda j, i: (i, xcol(j))),
                  pl.BlockSpec((8, XBC_BLK), lambda j, i: (jnp.maximum(i * (tm // 8) - 1, 0), xcol(j))),
                  pl.BlockSpec((8, XBC_BLK), lambda j, i: (jnp.minimum((i + 1) * (tm // 8), last8), xcol(j))),
                  pl.BlockSpec((tm, XBC_BLK), lambda j, i: (i, j)),
                  pl.BlockSpec((8, XBC_BLK), lambda j, i: (jnp.minimum((i + 1) * (tm // 8), last8), j)),
                  pl.BlockSpec((4, XBC_BLK), lambda j, i: (0, j)), pl.BlockSpec((1, XBC_BLK), lambda j, i: (0, j))],
        out_specs=[pl.BlockSpec((tm, XBC_BLK), lambda j, i: (i, j)), pl.BlockSpec((4, XBC_BLK), lambda j, i: (0, j)),
                   pl.BlockSpec((1, XBC_BLK), lambda j, i: (0, j))],
        out_shape=[_sds((s, XBC), bf16), _sds((4, XBC), f32), _sds((1, XBC), f32)],
        compiler_params=_params(("parallel", "arbitrary")),
    )(proj2, proj2, proj2, dxbc, dxbc, conv_w, conv_b)


def _ssd_consts():
    ex = np.zeros((128, SSD_W), np.float32)
    for h in range(NH):
        ex[h, h * HD:(h + 1) * HD] = 1.0
    sel = np.zeros((8, 128), np.float32)
    for h in range(NH):
        sel[h // 2, h] = 1.0
    par = np.zeros((128, 128), np.float32)
    for r in range(128):
        for h in range(NH):
            par[r, h] = 1.0 if (h % 2) == (r // 64) else 0.0
    ones_blk = np.zeros((128, 128), np.float32)
    for r in range(128):
        ones_blk[r, (r // 64) * 64:(r // 64) * 64 + 64] = 1.0
    return ex, np.ascontiguousarray(ex.T), sel, par, ones_blk


def _ssd_common(xbc_ref, dtr_ref, a_ref, dtb_ref, ex_ref, sel_ref, par_ref):
    xs = xbc_ref[:, 0:SSD_W]
    dt = _softplus(dtr_ref[...] + dtb_ref[...])
    adt = dt * a_ref[...]
    r_i = lax.broadcasted_iota(jnp.int32, (CHUNK, CHUNK), 0)
    c_i = lax.broadcasted_iota(jnp.int32, (CHUNK, CHUNK), 1)
    tril = (r_i >= c_i).astype(f32)
    cs = jnp.dot(tril, adt, precision=HIGHEST, preferred_element_type=f32)
    cs2 = jnp.concatenate([cs, cs], axis=0) * par_ref[...]
    cstp = lax.dot_general(sel_ref[...], cs2, (((1,), (1,)), ((), ())), precision=HIGHEST, preferred_element_type=f32)
    ex = ex_ref[...]
    dt_full = jnp.dot(dt, ex, precision=HIGHEST, preferred_element_type=f32)
    cs_full = jnp.dot(cs, ex, precision=HIGHEST, preferred_element_type=f32)
    return xs, dt, cs, cstp, dt_full, cs_full


def _pair_mask():
    l_i = lax.broadcasted_iota(jnp.int32, (CHUNK, 128), 0)
    lane = lax.broadcasted_iota(jnp.int32, (CHUNK, 128), 1)
    return l_i >= (lane % CHUNK), lane < HD


def _block_diag(xp, first):
    z = jnp.zeros_like(xp)
    return jnp.concatenate([jnp.where(first, xp, z), jnp.where(first, z, xp)], axis=0)


def _ssd_fwd(xbc, proj2, a_row, dtb_row, dsk_full):
    s = xbc.shape[0]
    nc = s // CHUNK
    ex, ext, sel, par, ones_blk = _ssd_consts()

    def body(xbc_ref, dtr_ref, a_ref, dtb_ref, dsk_ref, ex_ref, sel_ref, par_ref, y_ref, hs_ref, hst):
        @pl.when(pl.program_id(0) == 0)
        def _():
            hst[...] = jnp.zeros_like(hst)

        hs_ref[...] = hst[...]
        xs, dt, cs, cstp, dt_full, cs_full = _ssd_common(xbc_ref, dtr_ref, a_ref, dtb_ref, ex_ref, sel_ref, par_ref)
        cs_last = cs_full[CHUNK - 1:CHUNK, :]
        xdt = xs * dt_full
        causal, first = _pair_mask()
        for g in range(NG):
            gl = slice(g * GW, (g + 1) * GW)
            bg = xbc_ref[:, SSD_W + g * NSTATE:SSD_W + (g + 1) * NSTATE].astype(bf16)
            cg = xbc_ref[:, SSD_W + NG * NSTATE + g * NSTATE:SSD_W + NG * NSTATE + (g + 1) * NSTATE].astype(bf16)
            cb2 = lax.dot_general(cg, jnp.concatenate([bg, bg], axis=0), (((1,), (1,)), ((), ())), preferred_element_type=f32)
            hg = hst[g]
            y0 = jnp.dot(cg, hg.astype(bf16), preferred_element_type=f32)
            yoff = jnp.exp(cs_full[:, gl]) * y0
            for j in range(GW // 128):
                pair = g * (GW // 128) + j
                pl_ = slice(pair * 128, (pair + 1) * 128)
                seg = jnp.exp(jnp.where(causal, cs_full[:, pl_] - cstp[pair:pair + 1, :], -jnp.inf))
                m = (cb2 * seg).astype(bf16)
                yd = jnp.dot(m, _block_diag(xdt[:, pl_].astype(bf16), first), preferred_element_type=f32)
                y_ref[:, pl_] = yd + yoff[:, j * 128:(j + 1) * 128] + xs[:, pl_] * dsk_ref[:, pl_]
            xdec = (xdt[:, gl] * jnp.exp(cs_last[:, gl] - cs_full[:, gl])).astype(bf16)
            st = lax.dot_general(bg, xdec, (((0,), (0,)), ((), ())), preferred_element_type=f32)
            hst[g] = jnp.exp(cs_last[:, gl]) * hg + st

    const = lambda shape: pl.BlockSpec(shape, lambda c: tuple(0 for _ in shape))
    return pl.pallas_call(
        body, name="ssd_fwd", grid=(nc,),
        in_specs=[pl.BlockSpec((CHUNK, XBC), lambda c: (c, 0)), pl.BlockSpec((CHUNK, 128), lambda c: (c, DT_COL)),
                  const((1, 128)), const((1, 128)), const((1, SSD_W)), const((128, SSD_W)), const((8, 128)), const((128, 128))],
        out_specs=[pl.BlockSpec((CHUNK, SSD_W), lambda c: (c, 0)), pl.BlockSpec((None, NG, NSTATE, GW), lambda c: (c, 0, 0, 0))],
        out_shape=[_sds((s, SSD_W), f32), _sds((nc, NG, NSTATE, GW), f32)],
        scratch_shapes=[pltpu.VMEM((NG, NSTATE, GW), f32)], compiler_params=_params(("arbitrary",)),
    )(xbc, proj2, a_row, dtb_row, dsk_full, jnp.asarray(ex), jnp.asarray(sel), jnp.asarray(par))


def _ssd_bwd(xbc, proj2, dy, hsave, a_row, dtb_row, dsk_full):
    s = xbc.shape[0]
    nc = s // CHUNK
    ex, ext, sel, par, ones_blk = _ssd_consts()

    def body(xbc_ref, dtr_ref, dy_ref, hs_ref, a_ref, dtb_ref, dsk_ref, ex_ref, ext_ref, sel_ref, par_ref, ob_ref,
             dxbc_ref, ddtr_ref, dd_ref, da_ref, ddtb_ref, dh, a_dd, a_da, a_dtb, dcs_lane, dcs_b, dxdt):
        step = pl.program_id(0)

        @pl.when(step == 0)
        def _():
            dh[...] = jnp.zeros_like(dh)
            a_dd[...] = jnp.zeros_like(a_dd)
            a_da[...] = jnp.zeros_like(a_da)
            a_dtb[...] = jnp.zeros_like(a_dtb)

        xs, dt, cs, cstp, dt_full, cs_full = _ssd_common(xbc_ref, dtr_ref, a_ref, dtb_ref, ex_ref, sel_ref, par_ref)
        cs_last = cs_full[CHUNK - 1:CHUNK, :]
        xdt = xs * dt_full
        dyv = dy_ref[...]
        a_dd[...] += _fold8(dyv * xs)
        causal, first = _pair_mask()
        ones_l = jnp.ones((CHUNK, 128), f32)
        for g in range(NG):
            gl = slice(g * GW, (g + 1) * GW)
            bcol = slice(SSD_W + g * NSTATE, SSD_W + (g + 1) * NSTATE)
            ccol = slice(SSD_W + NG * NSTATE + g * NSTATE, SSD_W + NG * NSTATE + (g + 1) * NSTATE)
            bg = xbc_ref[:, bcol].astype(bf16)
            cg = xbc_ref[:, ccol].astype(bf16)
            bg2 = jnp.concatenate([bg, bg], axis=0)
            cb2 = lax.dot_general(cg, bg2, (((1,), (1,)), ((), ())), preferred_element_type=f32)
            hg = hs_ref[g]
            hgb = hg.astype(bf16)
            dhg = dh[g]
            dhgb = dhg.astype(bf16)
            eg = jnp.exp(cs_full[:, gl])
            dec = jnp.exp(cs_last[:, gl] - cs_full[:, gl])
            gam = jnp.exp(cs_last[:, gl])
            dyg = dyv[:, gl]
            xdt_g = xdt[:, gl]
            y0 = jnp.dot(cg, hgb, preferred_element_type=f32)
            dy0 = (eg * dyg).astype(bf16)
            dcm = lax.dot_general(dy0, hgb, (((1,), (1,)), ((), ())), preferred_element_type=f32)
            dh_prev = gam * dhg + lax.dot_general(cg, dy0, (((0,), (0,)), ((), ())), preferred_element_type=f32)
            dgam = jnp.sum(dhg * hg, axis=0, keepdims=True) * gam
            dxdec = jnp.dot(bg, dhgb, preferred_element_type=f32)
            dbm = lax.dot_general((xdt_g * dec).astype(bf16), dhgb, (((1,), (1,)), ((), ())), preferred_element_type=f32)
            t = dxdec * xdt_g * dec
            dcs_lane[:, gl] = dyg * eg * y0 - t
            dcs_lane[CHUNK - 1:CHUNK, gl] += jnp.sum(t, axis=0, keepdims=True) + dgam
            dxdt[:, gl] = dxdec * dec
            dcb2 = jnp.zeros((CHUNK, 128), f32)
            for j in range(GW // 128):
                pair = g * (GW // 128) + j
                pl_ = slice(pair * 128, (pair + 1) * 128)
                seg = jnp.exp(jnp.where(causal, cs_full[:, pl_] - cstp[pair:pair + 1, :], -jnp.inf))
                m = cb2 * seg
                mb = m.astype(bf16)
                rhs = _block_diag(xdt[:, pl_].astype(bf16), first)
                dyp = dyv[:, pl_].astype(bf16)
                dm = lax.dot_general(dyp, rhs, (((1,), (1,)), ((), ())), preferred_element_type=f32)
                tt = lax.dot_general(mb, dyp, (((0,), (0,)), ((), ())), preferred_element_type=f32)
                dxdt[:, pl_] += jnp.where(first, tt[0:CHUNK], tt[CHUNK:])
                dcb2 = dcb2 + dm * seg
                w = dm * m
                rsum = jnp.dot(w, ob_ref[...], precision=HIGHEST, preferred_element_type=f32)
                t2 = lax.dot_general(w, ones_l, (((0,), (0,)), ((), ())), precision=HIGHEST, preferred_element_type=f32)
                dcs_b[:, pl_] = rsum - jnp.where(first, t2[0:CHUNK], t2[CHUNK:])
            dcb2b = dcb2.astype(bf16)
            dcm = dcm + jnp.dot(dcb2b, bg2, preferred_element_type=f32)
            t3 = lax.dot_general(dcb2b, cg, (((0,), (0,)), ((), ())), preferred_element_type=f32)
            dxbc_ref[:, bcol] = dbm + t3[0:CHUNK] + t3[CHUNK:]
            dxbc_ref[:, ccol] = dcm
            dh[g] = dh_prev
        dcs = jnp.dot(dcs_lane[...] + dcs_b[...] * (1.0 / HD), ext_ref[...], precision=HIGHEST, preferred_element_type=f32)
        r_i = lax.broadcasted_iota(jnp.int32, (CHUNK, CHUNK), 0)
        c_i = lax.broadcasted_iota(jnp.int32, (CHUNK, CHUNK), 1)
        triu = (r_i <= c_i).astype(f32)
        da_ = jnp.dot(triu, dcs, precision=HIGHEST, preferred_element_type=f32)
        dxdtv = dxdt[...]
        ddt = da_ * a_ref[...] + jnp.dot(dxdtv * xs, ext_ref[...], precision=HIGHEST, preferred_element_type=f32)
        a_da[...] += _fold8(da_ * dt)
        dxbc_ref[:, 0:SSD_W] = dyv * dsk_ref[...] + dxdtv * dt_full
        ddtr = ddt * _sigmoid(dtr_ref[...] + dtb_ref[...])
        ddtr_ref[...] = ddtr
        a_dtb[...] += _fold8(ddtr)

        @pl.when(step == nc - 1)
        def _():
            dd_ref[...] = jnp.sum(jnp.dot(a_dd[...], ext_ref[...], precision=HIGHEST, preferred_element_type=f32), axis=0, keepdims=True)
            da_ref[...] = jnp.sum(a_da[...], axis=0, keepdims=True)
            ddtb_ref[...] = jnp.sum(a_dtb[...], axis=0, keepdims=True)

    rev = lambda c: nc - 1 - c
    const = lambda shape: pl.BlockSpec(shape, lambda c: tuple(0 for _ in shape))
    return pl.pallas_call(
        body, name="ssd_bwd", grid=(nc,),
        in_specs=[pl.BlockSpec((CHUNK, XBC), lambda c: (rev(c), 0)), pl.BlockSpec((CHUNK, 128), lambda c: (rev(c), DT_COL)),
                  pl.BlockSpec((CHUNK, SSD_W), lambda c: (rev(c), 0)), pl.BlockSpec((None, NG, NSTATE, GW), lambda c: (rev(c), 0, 0, 0)),
                  const((1, 128)), const((1, 128)), const((1, SSD_W)), const((128, SSD_W)), const((SSD_W, 128)),
                  const((8, 128)), const((128, 128)), const((128, 128))],
        out_specs=[pl.BlockSpec((CHUNK, XBC), lambda c: (rev(c), 0)), pl.BlockSpec((CHUNK, 128), lambda c: (rev(c), 0)),
                   const((1, 128)), const((1, 128)), const((1, 128))],
        out_shape=[_sds((s, XBC), f32), _sds((s, 128), f32), _sds((1, 128), f32), _sds((1, 128), f32), _sds((1, 128), f32)],
        scratch_shapes=[pltpu.VMEM((NG, NSTATE, GW), f32), pltpu.VMEM((8, SSD_W), f32), pltpu.VMEM((8, 128), f32), pltpu.VMEM((8, 128), f32),
                        pltpu.VMEM((CHUNK, SSD_W), f32), pltpu.VMEM((CHUNK, SSD_W), f32), pltpu.VMEM((CHUNK, SSD_W), f32)],
        compiler_params=_params(("arbitrary",)),
    )(xbc, proj2, dy, hsave, a_row, dtb_row, dsk_full, jnp.asarray(ex), jnp.asarray(ext), jnp.asarray(sel), jnp.asarray(par),
      jnp.asarray(ones_blk))


def _after(value, token):
    return value if token is None else lax.optimization_barrier((value, token))[0]


def _local_step(x, tgt, mods, g_mix, rel, conv_w, conv_b, dt_bias, a_log, d_skip, g_att, g_ssd, g_ffn, g_final, weights):
    s = x.shape[0]
    tm_e = 256 if s % 256 == 0 else s
    tm_m = 512 if s % 512 == 0 else s
    tm_l = 1024 if s % 1024 == 0 else s
    tk = 2048 if s % 2048 == 0 else s
    sh1, sc1, gt1, sh2, sc2, gt2 = [mods[:, i * D:(i + 1) * D] for i in range(6)]

    h1b = _norm_mod("norm_mod_1", x, g_mix, sc1, sh1, tm_e)
    win, win_b = weights.w_in(h1b)
    qkv = _mm_nn_fullk("proj_qkv", h1b, win, tm_m, 768, bf16, n=IN_A)
    proj2 = _mm_nn_fullk("proj_zxbcdt", h1b, win_b, tm_m, 896, f32)
    bias = _expand_bias(rel)
    att = _attn_fwd(qkv, bias)
    xbc = _ssd_conv(proj2, conv_w, conv_b, tm_e)
    a_row = jnp.pad(-jnp.exp(a_log), ((0, 0), (0, 128 - NH)))
    dtb_row = jnp.pad(dt_bias, ((0, 0), (0, 128 - NH)))
    dsk_full = jnp.repeat(d_skip, HD, axis=1)
    y, hsave = _ssd_fwd(xbc, proj2, a_row, dtb_row, dsk_full)
    mixcat = _mix_pre(att, y, proj2, g_att, g_ssd, tm_e)
    wout = weights.w_out(mixcat)
    mix = _mm_nn_fullk("proj_out", mixcat, wout, tm_m, 1024, f32)
    x2, h2b = _resid_norm_mod(x, gt1, mix, g_ffn, sc2, sh2, tm_e)
    wg4, wu4, wd4 = weights.ffn(h2b)
    gate, up, act = _ffn_up(h2b, wg4, wu4, tm_m)
    ffn = _ffn_down(act, wd4, tm_l)

    dx3, dffn, loss, dg_final, dgt2 = _final_fwd_bwd(x2, ffn, gt2, g_final, tgt, tm_e)
    dffn = _after(dffn, weights.grad(("w_down",), [_grad_wdown4(act, dffn, 1024, tk)]))
    dgate, dup = _ffn_dact(dffn, wd4, gate, up, tm_m)
    dgate = _after(dgate, weights.grad(("w_gate", "w_up"), [_grad_cols4("grad_w_gate", h2b, dgate, 1024, tk),
                                                            _grad_cols4("grad_w_up", h2b, dup, 1024, tk)]))
    dh2 = _ffn_dh(dgate, dup, wg4, wu4, tm_m)
    dx2, dmix, dsc2, dsh2, dg_ffn, dgt1 = _norm_mod_bwd("norm_mod_bwd_2", dh2, x2, g_ffn, sc2, dx3, tm_e, mix=mix, gt=gt1)
    dmix = _after(dmix, weights.grad(("w_out",), [_mm_tn("grad_w_out", mixcat, dmix, 1024, 1024, tk, bf16).reshape(NSH, D // NSH, D)]))
    dmc = _mm_nt("dmixcat", dmix, wout, tm_m, 1024, D, f32)
    datt, dy, dz, dg_att, dg_ssd = _mix_pre_bwd(dmc, att, y, proj2, g_att, g_ssd, tm_e)
    dq, dk, dv, gband = _attn_bwd(qkv, datt, bias, jnp.transpose(bias, (0, 2, 1)))
    drel = _rel_bias_grad(gband.reshape(NH, CHUNK, BANDP))
    dxbc, ddtr, dd_row, da_row, ddtb_row = _ssd_bwd(xbc, proj2, dy, hsave, a_row, dtb_row, dsk_full)
    dxbc_raw, dconv_w, dconv_b = _ssd_conv_bwd(dxbc, proj2, conv_w, conv_b, tm_e)
    dproj = jnp.concatenate([dq, dk, dv, dz, dxbc_raw, ddtr.astype(bf16)], axis=1)
    gwin = _mm_tn("grad_w_in", h1b, dproj, 1024, 1152, tk, bf16)
    gwin4 = jnp.stack([jnp.pad(gwin[:, k * IN_SH:(k + 1) * IN_SH], ((0, 0), (0, IN_SHP - IN_SH))) for k in range(NSH)])
    dproj = _after(dproj, weights.grad(("w_in",), [gwin4]))
    dh1 = _mm_nt("dh1", dproj, win, tm_m, D, 1920, f32)
    grad_x, dsc1, dsh1, dg_mix = _norm_mod_bwd("norm_mod_bwd_1", dh1, x, g_mix, sc1, dx2, tm_e)

    dmods = jnp.concatenate([dsh1, dsc1, dgt1, dsh2, dsc2, dgt2], axis=1)
    dd_skip = dd_row[:, :NH]
    da_log = da_row[:, :NH] * a_row[:, :NH]
    small = dict(g_mix=dg_mix, conv_b=dconv_b, dt_bias=ddtb_row[:, :NH], a_log=da_log, d_skip=dd_skip, g_att_out=dg_att,
                 g_ssd_out=dg_ssd, g_ffn=dg_ffn, g_final=dg_final, rel_bias=drel, conv_w=dconv_w)
    return loss[0, 0], grad_x, dmods, small


HBM = pl.BlockSpec(memory_space=pl.ANY)
VMEM = pl.BlockSpec(memory_space=pltpu.VMEM)


def _place():
    x, y, c = lax.axis_index("x"), lax.axis_index("y"), lax.axis_index("c")
    chips = [(1 - x, y), (x, 1 - y), (1 - x, 1 - y)]
    return x, y, c, chips


def _allgather8(name, payload):
    r = payload.shape[0]

    def body(x_ref, out_ref, send_sems, recv_sems, local_sem):
        x, y, c, chips = _place()
        me, sibling = (x, y, c), (x, y, 1 - c)

        def slot(px, py, pc):
            return out_ref.at[4 * px + 2 * py + pc]

        def copy(k, block, to, src=None):
            return pltpu.make_async_remote_copy(
                src_ref=slot(*block) if src is None else src, dst_ref=slot(*block),
                send_sem=send_sems.at[k], recv_sem=recv_sems.at[k], device_id=to, device_id_type=MESH)

        mine = pltpu.make_async_copy(x_ref, slot(*me), local_sem)
        mine.start()
        first = [copy(0, me, sibling, src=x_ref)]
        first += [copy(1 + j, me, (*chip, c), src=x_ref) for j, chip in enumerate(chips)]
        for cp in first:
            cp.start()
        passed = [copy(4 + j, (*chip, c), sibling) for j, chip in enumerate(chips)]
        for j, chip in enumerate(chips):
            copy(1 + j, (*chip, c), me).wait_recv()
            passed[j].start()
        copy(0, sibling, me).wait_recv()
        for j, chip in enumerate(chips):
            copy(4 + j, (*chip, 1 - c), me).wait_recv()
        for cp in first + passed:
            cp.wait_send()
        mine.wait()

    return pl.pallas_call(
        body, name=name, out_shape=_sds((N_DEV, r, 128), f32), in_specs=[VMEM], out_specs=VMEM,
        scratch_shapes=[pltpu.SemaphoreType.DMA((7,)), pltpu.SemaphoreType.DMA((7,)), pltpu.SemaphoreType.DMA],
    )(payload)


def _sum8(g):
    r = g.shape[1]

    def body(g_ref, o_ref):
        acc = g_ref[0]
        for i in range(1, N_DEV):
            acc = acc + g_ref[i]
        o_ref[...] = acc

    return pl.pallas_call(body, name="sum8", out_shape=_sds((r, 128), f32))(g)


SEM = pl.BlockSpec(memory_space=pltpu.SEMAPHORE)
EFFECT = pltpu.SideEffectType.DATAFLOW_SIDE_EFFECTING


def _gather_copies(ins, lands, send_sems, recv_sems):
    x, y, c, chips = _place()
    k = 2 * x + y
    starts, recvs = [], []
    for w in range(len(ins)):
        for j, (px, py) in enumerate(chips):
            def mk(dst):
                return pltpu.make_async_remote_copy(src_ref=ins[w].at[c], dst_ref=dst, send_sem=send_sems[w].at[j],
                                                    recv_sem=recv_sems[w].at[j], device_id=(px, py, c), device_id_type=MESH)
            starts.append(mk(lands[w].at[k, c]))
            recvs.append(mk(lands[w].at[2 * px + py, c]))
    return starts, recvs


def _reduce_copies(ins, lands, send_sems, recv_sems):
    x, y, c, chips = _place()
    k = 2 * x + y
    starts, recvs = [], []
    for w in range(len(ins)):
        for j, (px, py) in enumerate(chips):
            def mk(dst):
                return pltpu.make_async_remote_copy(src_ref=ins[w].at[2 * px + py], dst_ref=dst, send_sem=send_sems[w].at[j],
                                                    recv_sem=recv_sems[w].at[j], device_id=(px, py, c), device_id_type=MESH)
            starts.append(mk(lands[w].at[k]))
            recvs.append(mk(lands[w].at[2 * px + py]))
    return starts, recvs


def _split_start(name, copies, srcs, land_shapes):
    nw = len(srcs)

    def body(*refs):
        starts, _ = copies(refs[:nw], refs[nw:2 * nw], refs[2 * nw:3 * nw], refs[3 * nw:4 * nw])
        for cp in starts:
            cp.start()
        refs[6 * nw][...] = jnp.zeros((8, 128), f32)

    sems = [pltpu.SemaphoreType.DMA((3,))] * nw
    bufs = [pltpu.HBM(s.shape, bf16) for s in srcs] + [pltpu.HBM(s, bf16) for s in land_shapes]
    res = pl.pallas_call(
        body, name=name, out_shape=sems + sems + bufs + [_sds((8, 128), f32)],
        in_specs=[HBM] * (2 * nw), out_specs=[SEM] * (2 * nw) + [HBM] * (2 * nw) + [VMEM],
        input_output_aliases={i: 2 * nw + i for i in range(2 * nw)},
        compiler_params=pltpu.CompilerParams(has_side_effects=EFFECT),
    )(*[pltpu.with_memory_space_constraint(s, pltpu.HBM) for s in srcs],
      *[pltpu.with_memory_space_constraint(lax.empty(s, bf16), pltpu.HBM) for s in land_shapes])
    return res[:nw], res[nw:2 * nw], res[2 * nw:3 * nw], res[3 * nw:4 * nw], res[4 * nw]


def _split_wait(name, copies, send_sems, recv_sems, srcs, lands, after):
    nw = len(srcs)

    def body(*refs):
        starts, recvs = copies(refs[:nw], refs[nw:2 * nw], refs[2 * nw:3 * nw], refs[3 * nw:4 * nw])
        for s_, r_ in zip(starts, recvs):
            s_.wait_send()
            r_.wait_recv()

    bufs = [pltpu.HBM(s.shape, bf16) for s in srcs] + [pltpu.HBM(l.shape, bf16) for l in lands]
    res = pl.pallas_call(
        body, name=name, out_shape=bufs, in_specs=[HBM] * (2 * nw) + [SEM] * (2 * nw) + [HBM], out_specs=[HBM] * (2 * nw),
        input_output_aliases={i: i for i in range(2 * nw)},
        compiler_params=pltpu.CompilerParams(has_side_effects=EFFECT),
    )(*srcs, *lands, *send_sems, *recv_sems, after)
    return res[:nw], res[nw:]


def _gather_forward(name, shards, lands):
    nw = len(shards)

    def body(*refs):
        ins, lands_in, outs = refs[:nw], refs[nw:2 * nw], refs[2 * nw:3 * nw]
        st_a, st_b, st_c = refs[3 * nw:4 * nw], refs[4 * nw:5 * nw], refs[5 * nw:6 * nw]
        send_sems, recv_sems, load_sems, store_sems = refs[6 * nw:]
        x, y, c, chips = _place()
        k = 2 * x + y
        sibling = (x, y, 1 - c)
        ld_a = [pltpu.make_async_copy(ins[w].at[c], st_a[w], load_sems.at[w, 0]) for w in range(nw)]
        ld_b = [pltpu.make_async_copy(ins[w].at[1 - c], st_b[w], load_sems.at[w, 1]) for w in range(nw)]
        for cp in ld_a + ld_b:
            cp.start()
        st_own = []
        for w in range(nw):
            ld_a[w].wait()
            st_own.append(pltpu.make_async_copy(st_a[w], outs[w].at[k, c], store_sems.at[w, 0]))
            st_own[-1].start()
        for w in range(nw):
            ld_b[w].wait()
            st_own.append(pltpu.make_async_copy(st_b[w], outs[w].at[k, 1 - c], store_sems.at[w, 1]))
            st_own[-1].start()
        for cp in st_own:
            cp.wait()
        fwds = {}
        for j, (px, py) in enumerate(chips):
            kq = 2 * px + py
            for w in range(nw):
                slot = st_b[w] if j % 2 == 0 else st_c[w]
                if j == 2:
                    fwds[w, 0].wait_send()
                ld = pltpu.make_async_copy(lands_in[w].at[kq, c], slot, load_sems.at[w, 2 + j])
                ld.start()
                ld.wait()
                fwds[w, j] = pltpu.make_async_remote_copy(src_ref=slot, dst_ref=outs[w].at[kq, c], send_sem=send_sems.at[w, j],
                                                          recv_sem=recv_sems.at[w, j], device_id=sibling, device_id_type=MESH)
                fwds[w, j].start()
        for j, (px, py) in enumerate(chips):
            for w in range(nw):
                pltpu.make_async_remote_copy(src_ref=st_c[w], dst_ref=outs[w].at[2 * px + py, 1 - c], send_sem=send_sems.at[w, j],
                                             recv_sem=recv_sems.at[w, j], device_id=sibling, device_id_type=MESH).wait_recv()
        for w in range(nw):
            fwds[w, 1].wait_send()
            fwds[w, 2].wait_send()

    stage = [pltpu.VMEM(s.shape[1:], bf16) for s in shards]
    return pl.pallas_call(
        body, name=name, out_shape=[_sds(l.shape, bf16) for l in lands],
        in_specs=[HBM] * (2 * nw), out_specs=[HBM] * nw, input_output_aliases={nw + w: w for w in range(nw)},
        scratch_shapes=stage * 3 + [pltpu.SemaphoreType.DMA((nw, 3)), pltpu.SemaphoreType.DMA((nw, 3)), pltpu.SemaphoreType.DMA((nw, 5)),
                                    pltpu.SemaphoreType.DMA((nw, 2))],
        compiler_params=pltpu.CompilerParams(vmem_limit_bytes=VMEM_LIMIT),
    )(*shards, *lands)


def _rs_pair_exchange(name, grads):
    nw = len(grads)

    def body(*refs):
        ins, got, stage = refs[:nw], refs[nw:2 * nw], refs[2 * nw:3 * nw]
        send_sems, recv_sems, load_sems = refs[3 * nw:]
        x, y, c, _ = _place()

        def load(w, kk):
            return pltpu.make_async_copy(ins[w].at[kk, 1 - c], stage[w].at[kk % 2], load_sems.at[w, kk])

        def send(w, kk):
            return pltpu.make_async_remote_copy(src_ref=stage[w].at[kk % 2], dst_ref=got[w].at[kk], send_sem=send_sems.at[w, kk],
                                                recv_sem=recv_sems.at[w, kk], device_id=(x, y, 1 - c), device_id_type=MESH)

        for kk in range(2):
            for w in range(nw):
                load(w, kk).start()
        for kk in range(NSH):
            for w in range(nw):
                load(w, kk).wait()
                send(w, kk).start()
            if kk + 2 < NSH:
                for w in range(nw):
                    send(w, kk).wait_send()
                    load(w, kk + 2).start()
        for kk in range(NSH - 2, NSH):
            for w in range(nw):
                send(w, kk).wait_send()
        for kk in range(NSH):
            for w in range(nw):
                send(w, kk).wait_recv()

    return pl.pallas_call(
        body, name=name, out_shape=[_sds((NSH,) + g.shape[2:], bf16) for g in grads], in_specs=[HBM] * nw, out_specs=[HBM] * nw,
        scratch_shapes=[pltpu.VMEM((2,) + g.shape[2:], bf16) for g in grads]
        + [pltpu.SemaphoreType.DMA((nw, NSH)), pltpu.SemaphoreType.DMA((nw, NSH)), pltpu.SemaphoreType.DMA((nw, NSH))],
        compiler_params=pltpu.CompilerParams(vmem_limit_bytes=VMEM_LIMIT),
    )(*grads)


def _rs_pair_gather(name, halves):
    nw = len(halves)

    def body(*refs):
        ins, outs, stage = refs[:nw], refs[nw:2 * nw], refs[2 * nw:3 * nw]
        send_sems, recv_sems, local_sems, stage_sems = refs[3 * nw:]
        x, y, c, _ = _place()
        loads = [pltpu.make_async_copy(ins[w], stage[w], stage_sems.at[w]) for w in range(nw)]
        for cp in loads:
            cp.start()
        local, cps = [], []
        for w in range(nw):
            loads[w].wait()
            local.append(pltpu.make_async_copy(stage[w], outs[w].at[c], local_sems.at[w]))
            cps.append(pltpu.make_async_remote_copy(src_ref=stage[w], dst_ref=outs[w].at[c], send_sem=send_sems.at[w],
                                                    recv_sem=recv_sems.at[w], device_id=(x, y, 1 - c), device_id_type=MESH))
            local[w].start()
            cps[w].start()
        for w in range(nw):
            pltpu.make_async_remote_copy(src_ref=stage[w], dst_ref=outs[w].at[1 - c], send_sem=send_sems.at[w], recv_sem=recv_sems.at[w],
                                         device_id=(x, y, 1 - c), device_id_type=MESH).wait_recv()
        for cp in cps:
            cp.wait_send()
        for cp in local:
            cp.wait()

    return pl.pallas_call(
        body, name=name, out_shape=[_sds((2,) + h.shape, f32) for h in halves], in_specs=[HBM] * nw, out_specs=[HBM] * nw,
        scratch_shapes=[pltpu.VMEM(h.shape, f32) for h in halves]
        + [pltpu.SemaphoreType.DMA((nw,)), pltpu.SemaphoreType.DMA((nw,)), pltpu.SemaphoreType.DMA((nw,)), pltpu.SemaphoreType.DMA((nw,))],
        compiler_params=pltpu.CompilerParams(vmem_limit_bytes=VMEM_LIMIT),
    )(*halves)


def _row_tile(r, c, nbuf):
    budget = 24 * 1024 * 1024 // (2 * nbuf * 4 * c)
    t = 8
    while t * 2 <= budget and r % (t * 2) == 0:
        t *= 2
    return t


def _cast_bf16(name, a):
    r, c = a.shape
    tr = _row_tile(r, c, 2)

    def body(a_ref, o_ref):
        o_ref[...] = a_ref[...].astype(bf16)

    spec = pl.BlockSpec((tr, c), lambda i: (i, 0))
    return pl.pallas_call(body, name=name, grid=(r // tr,), in_specs=[spec], out_specs=spec, out_shape=_sds((r, c), bf16),
                          compiler_params=_params(("parallel",)))(a)


def _w_in_columns(win4):
    tr = 256

    def body(a_ref, o_ref, ob_ref):
        for k in range(NSH):
            o_ref[:, IN_SH * k:IN_SH * (k + 1)] = a_ref[k][:, :IN_SH]
        o_ref[:, IN_COLS:] = jnp.zeros((tr, IN_P - IN_COLS), bf16)
        ob_ref[...] = o_ref[:, IN_A:]

    return pl.pallas_call(
        body, name="w_in_columns", grid=(D // tr,), in_specs=[pl.BlockSpec((NSH, tr, IN_SHP), lambda i: (0, i, 0))],
        out_specs=[pl.BlockSpec((tr, IN_P), lambda i: (i, 0)), pl.BlockSpec((tr, IN_B), lambda i: (i, 0))],
        out_shape=[_sds((D, IN_P), bf16), _sds((D, IN_B), bf16)], compiler_params=_params(("parallel",)))(win4)


def _pair_sum(name, core, grads, got):
    _, _, rh, c = grads.shape
    tr = _row_tile(rh, c, 2)

    def body(c_ref, a_ref, b_ref, o_ref):
        o_ref[...] = (a_ref[...].astype(f32) + b_ref[...].astype(f32)).astype(bf16)

    spec = pl.BlockSpec((None, tr, c), lambda k, i, c_ref: (k, i, 0))
    return pl.pallas_call(
        body, name=name, out_shape=_sds((NSH, rh, c), bf16),
        grid_spec=pltpu.PrefetchScalarGridSpec(
            num_scalar_prefetch=1, grid=(NSH, rh // tr),
            in_specs=[pl.BlockSpec((None, None, tr, c), lambda k, i, c_ref: (k, c_ref[0], i, 0)), spec], out_specs=spec),
        compiler_params=_params(("parallel", "parallel")))(core, grads, got)


def _chip_sum(name, chip, sums, lands):
    _, rh, c = sums.shape
    tr = _row_tile(rh, c, 4)

    def body(k_ref, own_ref, l_ref, o_ref):
        own = own_ref[...].astype(f32)
        acc = None
        for j in range(NSH):
            term = jnp.where(k_ref[0] == j, own, l_ref[j].astype(f32))
            acc = term if acc is None else acc + term
        o_ref[...] = acc

    return pl.pallas_call(
        body, name=name, out_shape=_sds((rh, c), f32),
        grid_spec=pltpu.PrefetchScalarGridSpec(
            num_scalar_prefetch=1, grid=(rh // tr,),
            in_specs=[pl.BlockSpec((None, tr, c), lambda i, k_ref: (k_ref[0], i, 0)), pl.BlockSpec((NSH, tr, c), lambda i, k_ref: (0, i, 0))],
            out_specs=pl.BlockSpec((tr, c), lambda i, k_ref: (i, 0))),
        compiler_params=_params(("parallel",)))(chip, sums, lands)


def _mods_part(cond16, w_ada, b_part):
    n = w_ada.shape[1]
    tn = 512

    def body(c_ref, w_ref, b_ref, o_ref):
        cv = c_ref[...]
        o_ref[...] = _dot(cv * _sigmoid(cv), w_ref[...]) + b_ref[...]

    return pl.pallas_call(
        body, name="mods_part", grid=(n // tn,),
        in_specs=[pl.BlockSpec((16, D), lambda j: (0, 0)), pl.BlockSpec((D, tn), lambda j: (0, j)), pl.BlockSpec((1, tn), lambda j: (0, j))],
        out_specs=pl.BlockSpec((16, tn), lambda j: (0, j)), out_shape=_sds((16, n), f32), compiler_params=_params(("parallel",)),
    )(cond16, w_ada, b_part)


def _grad_w_ada(cond16, dm16):
    n = dm16.shape[1]
    tr = 256

    def body(c_ref, d_ref, o_ref):
        cv = c_ref[...]
        o_ref[...] = _dot(cv * _sigmoid(cv), d_ref[...], ta=True)

    return pl.pallas_call(
        body, name="grad_w_ada", grid=(D // tr,),
        in_specs=[pl.BlockSpec((16, tr), lambda i: (0, i)), pl.BlockSpec((16, n), lambda i: (0, 0))],
        out_specs=pl.BlockSpec((tr, n), lambda i: (i, 0)), out_shape=_sds((D, n), f32), compiler_params=_params(("parallel",)),
    )(cond16, dm16)


def _adamw(name, w, g, m, v):
    r, c = w.shape
    tr = _row_tile(r, c, 7)

    def body(w_ref, g_ref, m_ref, v_ref, d_ref, nm_ref, nv_ref):
        gv = g_ref[...]
        nm = ADAM_B1 * m_ref[...] + (1.0 - ADAM_B1) * gv
        nv = ADAM_B2 * v_ref[...] + (1.0 - ADAM_B2) * (gv * gv)
        nm_ref[...] = nm
        nv_ref[...] = nv
        m_hat = nm / (1.0 - ADAM_B1 ** ADAM_STEP)
        v_hat = nv / (1.0 - ADAM_B2 ** ADAM_STEP)
        d_ref[...] = -ADAM_LR * (m_hat / (jnp.sqrt(v_hat) + ADAM_EPS) + ADAM_WD * w_ref[...])

    spec = pl.BlockSpec((tr, c), lambda i: (i, 0))
    return pl.pallas_call(body, name=name, grid=(r // tr,), in_specs=[spec] * 4, out_specs=[spec] * 3, out_shape=[_sds((r, c), f32)] * 3,
                          compiler_params=_params(("parallel",)))(w, g, m, v)


def _pack(parts, rows):
    flat = []
    for p in parts:
        p = p.reshape(-1)
        flat.append(jnp.pad(p, (0, (-p.shape[0]) % 128)))
    v = jnp.concatenate(flat)
    return jnp.pad(v, (0, rows * 128 - v.shape[0])).reshape(rows, 128)


def _unpack(packed, sizes):
    lead = packed.shape[:-2]
    flat = packed.reshape(lead + (-1,))
    out, off = [], 0
    for n in sizes:
        out.append(flat[..., off:off + n])
        off += n + (-n) % 128
    return out


BIG = ("w_in", "w_out", "w_gate", "w_up", "w_down")
SMALL = ("b_ada", "g_mix", "conv_b", "dt_bias", "a_log", "d_skip", "g_att_out", "g_ssd_out", "g_ffn", "g_final", "rel_bias", "conv_w")
ORDER = ("w_ada", "b_ada", "g_mix", "w_in", "rel_bias", "conv_w", "conv_b", "dt_bias", "a_log", "d_skip", "g_att_out", "g_ssd_out",
         "w_out", "g_ffn", "w_gate", "w_up", "w_down", "g_final")
REL_SH = N_REL // NSH
CONVW_SH = XBC // NSH
ADA_SH = 6 * D // NSH


class _Exchange:
    def __init__(self, shards, core, chip):
        ssem, rsem, thru, lands, self.token = _split_start("gather_start", _gather_copies, shards, [(NSH,) + s.shape for s in shards])
        self.gathered = {n: (ssem[i], rsem[i], thru[i], lands[i]) for i, n in enumerate(BIG)}
        self.core, self.chip = core, chip
        self.pending = []

    def _whole(self, names, after):
        ssem, rsem, thru, lands = zip(*[self.gathered[n] for n in names])
        tag = "_".join(names)
        thru, lands = _split_wait("gather_wait_" + tag, _gather_copies, ssem, rsem, thru, lands, after)
        return _gather_forward("gather_forward_" + tag, thru, lands)

    def w_in(self, after):
        (win4,) = self._whole(("w_in",), after)
        return _w_in_columns(win4.reshape(NSH, D, IN_SHP))

    def w_out(self, after):
        (wout4,) = self._whole(("w_out",), after)
        return wout4.reshape(D, D)

    def ffn(self, after):
        wg4, wu4, wd4 = self._whole(("w_gate", "w_up", "w_down"), after)
        return wg4.reshape(NSH, D, FSH), wu4.reshape(NSH, D, FSH), wd4.reshape(NSH, FSH, D)

    def grad(self, names, grads):
        tag = "_".join(names)
        stacked = [g.reshape(NSH, 2, g.shape[1] // 2, g.shape[2]) for g in grads]
        got = _rs_pair_exchange("rs_pair_exchange_" + tag, stacked)
        sums = [_pair_sum("pair_sum_" + n, self.core, o, g) for n, o, g in zip(names, stacked, got)]
        self.pending.append((names, _split_start("rs_start_" + tag, _reduce_copies, sums, [s.shape for s in sums])))
        return self.pending[-1][1][4]

    def finish(self, after):
        grads = {}
        for names, (ssem, rsem, sums, lands, _) in self.pending:
            tag = "_".join(names)
            sums, lands = _split_wait("rs_wait_" + tag, _reduce_copies, ssem, rsem, sums, lands, after)
            halves = [_chip_sum("chip_sum_" + n, self.chip, sm, ld) for n, sm, ld in zip(names, sums, lands)]
            for n, f in zip(names, _rs_pair_gather("rs_pair_gather_" + tag, halves)):
                grads[n] = f.reshape(2 * f.shape[1], f.shape[2])
        return grads


def kernel(x, c, w_ada, b_ada, g_mix, w_in, rel_bias, conv_w, conv_b, dt_bias, a_log, d_skip, g_att_out, g_ssd_out, w_out, g_ffn, w_gate, w_up, w_down, g_final, loss_target, m_w_ada, m_b_ada, m_g_mix, m_w_in, m_rel_bias, m_conv_w, m_conv_b, m_dt_bias, m_a_log, m_d_skip, m_g_att_out, m_g_ssd_out, m_w_out, m_g_ffn, m_w_gate, m_w_up, m_w_down, m_g_final, v_w_ada, v_b_ada, v_g_mix, v_w_in, v_rel_bias, v_conv_w, v_conv_b, v_dt_bias, v_a_log, v_d_skip, v_g_att_out, v_g_ssd_out, v_w_out, v_g_ffn, v_w_gate, v_w_up, v_w_down, v_g_final):
    args = dict(locals())
    w = {n: args[n] for n in ORDER}
    m = {n: args["m_" + n] for n in ORDER}
    v = {n: args["v_" + n] for n in ORDER}
    ix, iy, ic = lax.axis_index("x"), lax.axis_index("y"), lax.axis_index("c")
    chip = 2 * ix + iy
    dev = 2 * chip + ic
    s = x.shape[1]

    shards = [
        _cast_bf16("cast_w_in", jnp.pad(w_in[0], ((0, 0), (0, IN_SHP - IN_SH)))).reshape(2, D // 2, IN_SHP),
        _cast_bf16("cast_w_out", w_out[0]).reshape(2, D // NSH // 2, D),
        _cast_bf16("cast_w_gate", w_gate[0]).reshape(2, D // 2, FSH),
        _cast_bf16("cast_w_up", w_up[0]).reshape(2, D // 2, FSH),
        _cast_bf16("cast_w_down", w_down[0]).reshape(2, FSH // 2, D),
    ]
    exchange = _Exchange(shards, jnp.reshape(ic, (1,)).astype(jnp.int32), jnp.reshape(chip, (1,)).astype(jnp.int32))

    g1 = _allgather8("gather_inputs", _after(_pack([c[0], rel_bias[0], conv_w[0]], 40), exchange.token))
    c_all, rel_sh, convw_sh = _unpack(g1, [D, NH * REL_SH, 4 * CONVW_SH])
    rel_full = jnp.concatenate([rel_sh[2 * k].reshape(NH, REL_SH) for k in range(NSH)], axis=1)
    convw_full = jnp.concatenate([convw_sh[2 * k].reshape(4, CONVW_SH) for k in range(NSH)], axis=1)
    cond16 = jnp.pad(c_all, ((0, 8), (0, 0)))
    b_part = lax.dynamic_slice_in_dim(b_ada, chip * ADA_SH, ADA_SH, axis=1)
    mods_part = _mods_part(cond16, w_ada[0], b_part)[:N_DEV]
    g2 = _allgather8("gather_mods", mods_part.reshape(N_DEV * ADA_SH // 128, 128))
    mods_all = jnp.concatenate([g2[2 * k].reshape(N_DEV, ADA_SH) for k in range(NSH)], axis=1)
    mods = lax.dynamic_slice_in_dim(mods_all, dev, 1, axis=0)

    loss, grad_x, dmods, small = _local_step(
        x[0], loss_target[0], mods, g_mix, rel_full, convw_full, conv_b, dt_bias, a_log, d_skip, g_att_out, g_ssd_out, g_ffn,
        g_final[None, :], exchange)

    small_names = ("g_mix", "conv_b", "dt_bias", "a_log", "d_skip", "g_att_out", "g_ssd_out", "g_ffn", "g_final", "rel_bias", "conv_w")
    g3 = _allgather8("gather_small_grads", _pack([dmods] + [small[n] for n in small_names], 264))
    sizes = [6 * D] + [int(np.prod(small[n].shape)) for n in small_names]
    dmods_all = _unpack(g3, sizes)[0]
    summed = _unpack(_sum8(g3), sizes)
    grads = {"b_ada": summed[0].reshape(1, 6 * D)}
    for n, val in zip(small_names, summed[1:]):
        grads[n] = val.reshape(small[n].shape)
    grads["rel_bias"] = lax.dynamic_slice_in_dim(grads["rel_bias"], chip * REL_SH, REL_SH, axis=1)
    grads["conv_w"] = lax.dynamic_slice_in_dim(grads["conv_w"], chip * CONVW_SH, CONVW_SH, axis=1)
    grads["g_final"] = grads["g_final"].reshape(D)
    dm16 = jnp.pad(lax.dynamic_slice_in_dim(dmods_all, chip * ADA_SH, ADA_SH, axis=1), ((0, 8), (0, 0)))
    grads["w_ada"] = _grad_w_ada(cond16, dm16)

    delta, new_m, new_v = {}, {}, {}
    delta["w_ada"], new_m["w_ada"], new_v["w_ada"] = _adamw("adamw_w_ada", w_ada[0], grads["w_ada"], m_w_ada[0], v_w_ada[0])
    grads.update(exchange.finish(grad_x))
    grads["w_in"] = grads["w_in"][:, :IN_SH]
    for n in BIG:
        delta[n], new_m[n], new_v[n] = _adamw("adamw_" + n, w[n][0], grads[n], m[n][0], v[n][0])
    sw = _pack([w[n] for n in SMALL], 200)
    sg = _pack([grads[n] for n in SMALL], 200)
    sm = _pack([m[n] for n in SMALL], 200)
    sv = _pack([v[n] for n in SMALL], 200)
    ssz = [int(np.prod(w[n].shape)) for n in SMALL]
    for dst, packed in zip((delta, new_m, new_v), _adamw("adamw_small", sw, sg, sm, sv)):
        for n, val in zip(SMALL, _unpack(packed, ssz)):
            dst[n] = val

    def shaped(d, n):
        return d[n].reshape(w[n].shape)

    total = lax.psum(loss, ("x", "y", "c"))
    return (total, grad_x[None], *[shaped(grads, n) for n in ORDER], *[shaped(delta, n) for n in ORDER],
            *[shaped(new_m, n) for n in ORDER], *[shaped(new_v, n) for n in ORDER])
```

```python
import functools

import numpy as np
import jax
import jax.numpy as jnp
from jax import lax
from jax.experimental import pallas as pl
from jax.experimental.pallas import tpu as pltpu

f32 = jnp.float32
bf16 = jnp.bfloat16
HIGHEST = lax.Precision.HIGHEST
MESH = pl.DeviceIdType.MESH

D = 2048
CHUNK = 64
LEFT = 8
BAND = (LEFT + 1) * CHUNK
BANDP = 640
PADK = LEFT * CHUNK
NH = 16
HD = 64
ATT_W = NH * HD
SSD_W = 1024
NG = 2
NSTATE = 128
GW = SSD_W // NG
XBC = SSD_W + 2 * NG * NSTATE
N_REL = 320
REL_CLIP = 256
FFN = 5632
NSH = 4
FSH = FFN // NSH
IN_COLS = 5648
IN_SH = IN_COLS // NSH
IN_SHP = 1536
IN_A = 3 * ATT_W
IN_B = 2688
IN_P = IN_A + IN_B
EPS = 1e-6
N_DEV = 8

ADAM_LR = 0.001
ADAM_B1 = 0.9
ADAM_B2 = 0.999
ADAM_EPS = 1e-08
ADAM_WD = 0.01
ADAM_STEP = 10

VMEM_LIMIT = 56 * 1024 * 1024


def _params(sem):
    return pltpu.CompilerParams(dimension_semantics=sem, vmem_limit_bytes=VMEM_LIMIT)


def _sds(shape, dtype):
    return jax.ShapeDtypeStruct(shape, dtype)


def _fold8(v):
    r, w = v.shape
    return jnp.sum(v.reshape(r // 8, 8, w), axis=0)


def _sigmoid(v):
    return 1.0 / (1.0 + jnp.exp(-v))


def _softplus(v):
    return jnp.maximum(v, 0.0) + jnp.log(1.0 + jnp.exp(-jnp.abs(v)))


def _dot(a, b, ta=False, tb=False):
    dn = (((0 if ta else 1,), (1 if tb else 0,)), ((), ()))
    return lax.dot_general(a.astype(bf16), b.astype(bf16), dn, preferred_element_type=f32)


def _dep_args(dep, ngrid):
    if dep is None:
        return [], []
    return [pl.BlockSpec((8, 128), lambda *_: (0, 0))], [dep]


def _matmul(name, a, b, *, grid, a_spec, b_spec, o_spec, o_shape, o_dtype, acc_shape, ta=False, tb=False, dep=None):
    nk = grid[2]
    dep_specs, dep_ops = _dep_args(dep, 3)

    def body(a_ref, b_ref, *rest):
        o_ref, acc_ref = rest[-2:]
        p = _dot(a_ref[...], b_ref[...], ta, tb)
        if nk == 1:
            o_ref[...] = p.astype(o_ref.dtype)
        else:
            k = pl.program_id(2)

            @pl.when(k == 0)
            def _():
                acc_ref[...] = p

            @pl.when(jnp.logical_and(k > 0, k < nk - 1))
            def _():
                acc_ref[...] += p

            @pl.when(k == nk - 1)
            def _():
                o_ref[...] = (acc_ref[...] + p).astype(o_ref.dtype)

    return pl.pallas_call(
        body, name=name, grid=grid, in_specs=[a_spec, b_spec] + dep_specs, out_specs=o_spec,
        out_shape=_sds(o_shape, o_dtype), scratch_shapes=[pltpu.VMEM(acc_shape if nk > 1 else (8, 128), f32)],
        compiler_params=_params(("parallel", "parallel", "arbitrary")),
    )(a, b, *dep_ops)


def _mm_nn_fullk(name, a, b, tm, tn, o_dtype, n=None):
    m, k = a.shape
    n = b.shape[1] if n is None else n
    return _matmul(name, a, b, grid=(m // tm, n // tn, 1),
                   a_spec=pl.BlockSpec((tm, k), lambda i, j, kk: (i, 0)),
                   b_spec=pl.BlockSpec((k, tn), lambda i, j, kk: (0, j)),
                   o_spec=pl.BlockSpec((tm, tn), lambda i, j, kk: (i, j)),
                   o_shape=(m, n), o_dtype=o_dtype, acc_shape=(tm, tn))


def _mm_nt(name, a, b, tm, tn, tk, o_dtype, dep=None):
    m, k = a.shape
    n = b.shape[0]
    return _matmul(name, a, b, grid=(m // tm, n // tn, k // tk), tb=True, dep=dep,
                   a_spec=pl.BlockSpec((tm, tk), lambda i, j, kk: (i, kk)),
                   b_spec=pl.BlockSpec((tn, tk), lambda i, j, kk: (j, kk)),
                   o_spec=pl.BlockSpec((tm, tn), lambda i, j, kk: (i, j)),
                   o_shape=(m, n), o_dtype=o_dtype, acc_shape=(tm, tn))


def _mm_tn(name, a, b, tm, tn, tk, o_dtype):
    k, m = a.shape
    n = b.shape[1]
    return _matmul(name, a, b, grid=(m // tm, n // tn, k // tk), ta=True,
                   a_spec=pl.BlockSpec((tk, tm), lambda i, j, kk: (kk, i)),
                   b_spec=pl.BlockSpec((tk, tn), lambda i, j, kk: (kk, j)),
                   o_spec=pl.BlockSpec((tm, tn), lambda i, j, kk: (i, j)),
                   o_shape=(m, n), o_dtype=o_dtype, acc_shape=(tm, tn))


def _ffn_up(h2b, wg4, wu4, tm):
    s = h2b.shape[0]

    def body(h_ref, wg_ref, wu_ref, g_ref, u_ref, a_ref):
        h = h_ref[...]
        g = _dot(h, wg_ref[...])
        u = _dot(h, wu_ref[...])
        g_ref[...] = g
        u_ref[...] = u
        a_ref[...] = (g * _sigmoid(g) * u).astype(bf16)

    wspec = pl.BlockSpec((None, D, FSH), lambda k, i: (k, 0, 0))
    ospec = pl.BlockSpec((tm, FSH), lambda k, i: (i, k))
    return pl.pallas_call(
        body, name="ffn_up", grid=(NSH, s // tm),
        in_specs=[pl.BlockSpec((tm, D), lambda k, i: (i, 0)), wspec, wspec],
        out_specs=[ospec, ospec, ospec],
        out_shape=[_sds((s, FFN), f32), _sds((s, FFN), f32), _sds((s, FFN), bf16)],
        compiler_params=_params(("parallel", "parallel")),
    )(h2b, wg4, wu4)


def _ffn_down(act, wd4, tm):
    s = act.shape[0]
    return _matmul("ffn_down", act, wd4, grid=(s // tm, 1, NSH),
                   a_spec=pl.BlockSpec((tm, FSH), lambda i, j, k: (i, k)),
                   b_spec=pl.BlockSpec((None, FSH, D), lambda i, j, k: (k, 0, 0)),
                   o_spec=pl.BlockSpec((tm, D), lambda i, j, k: (i, 0)),
                   o_shape=(s, D), o_dtype=f32, acc_shape=(tm, D))


def _ffn_dact(dffn, wd4, gate, up, tm, dep=None):
    s = dffn.shape[0]
    dep_specs, dep_ops = _dep_args(dep, 2)

    def body(d_ref, w_ref, g_ref, u_ref, *rest):
        dg_ref, du_ref = rest[-2:]
        dact = _dot(d_ref[...], w_ref[...], tb=True)
        g = g_ref[...]
        sg = _sigmoid(g)
        dg_ref[...] = (dact * u_ref[...] * (sg * (1.0 + g * (1.0 - sg)))).astype(bf16)
        du_ref[...] = (dact * (g * sg)).astype(bf16)

    blk = pl.BlockSpec((tm, FSH), lambda k, i: (i, k))
    return pl.pallas_call(
        body, name="ffn_dact", grid=(NSH, s // tm),
        in_specs=[pl.BlockSpec((tm, D), lambda k, i: (i, 0)), pl.BlockSpec((None, FSH, D), lambda k, i: (k, 0, 0)), blk, blk] + dep_specs,
        out_specs=[blk, blk], out_shape=[_sds((s, FFN), bf16), _sds((s, FFN), bf16)],
        compiler_params=_params(("parallel", "parallel")),
    )(dffn, wd4, gate, up, *dep_ops)


def _ffn_dh(dgate, dup, wg4, wu4, tm, dep=None):
    s = dgate.shape[0]
    dep_specs, dep_ops = _dep_args(dep, 2)

    def body(dg_ref, du_ref, wg_ref, wu_ref, *rest):
        o_ref, acc_ref = rest[-2:]
        k = pl.program_id(1)
        p = _dot(dg_ref[...], wg_ref[...], tb=True) + _dot(du_ref[...], wu_ref[...], tb=True)

        @pl.when(k == 0)
        def _():
            acc_ref[...] = p

        @pl.when(jnp.logical_and(k > 0, k < NSH - 1))
        def _():
            acc_ref[...] += p

        @pl.when(k == NSH - 1)
        def _():
            o_ref[...] = acc_ref[...] + p

    aspec = pl.BlockSpec((tm, FSH), lambda i, k: (i, k))
    wspec = pl.BlockSpec((None, D, FSH), lambda i, k: (k, 0, 0))
    return pl.pallas_call(
        body, name="ffn_dh", grid=(s // tm, NSH), in_specs=[aspec, aspec, wspec, wspec] + dep_specs,
        out_specs=pl.BlockSpec((tm, D), lambda i, k: (i, 0)), out_shape=_sds((s, D), f32),
        scratch_shapes=[pltpu.VMEM((tm, D), f32)], compiler_params=_params(("parallel", "arbitrary")),
    )(dgate, dup, wg4, wu4, *dep_ops)


def _grad_cols4(name, h, dy, tm, tk):
    s = h.shape[0]
    return _matmul(name, h, dy, grid=(NSH, D // tm, s // tk), ta=True,
                   a_spec=pl.BlockSpec((tk, tm), lambda k, i, kk: (kk, i)),
                   b_spec=pl.BlockSpec((tk, FSH), lambda k, i, kk: (kk, k)),
                   o_spec=pl.BlockSpec((None, tm, FSH), lambda k, i, kk: (k, i, 0)),
                   o_shape=(NSH, D, FSH), o_dtype=bf16, acc_shape=(tm, FSH))


def _grad_wdown4(act, dffn, tn, tk):
    s = act.shape[0]
    return _matmul("grad_w_down", act, dffn, grid=(NSH, D // tn, s // tk), ta=True,
                   a_spec=pl.BlockSpec((tk, FSH), lambda k, j, kk: (kk, k)),
                   b_spec=pl.BlockSpec((tk, tn), lambda k, j, kk: (kk, j)),
                   o_spec=pl.BlockSpec((None, FSH, tn), lambda k, j, kk: (k, 0, j)),
                   o_shape=(NSH, FSH, D), o_dtype=bf16, acc_shape=(FSH, tn))


def _row_spec(w):
    return pl.BlockSpec((1, w), lambda i: (0, 0))


def _tile_spec(tm, w, col=0):
    return pl.BlockSpec((tm, w), lambda i: (i, col))


def _norm_mod(name, x, g, sc, sh, tm):
    s = x.shape[0]

    def body(x_ref, g_ref, sc_ref, sh_ref, o_ref):
        xv = x_ref[...]
        r = lax.rsqrt(jnp.mean(xv * xv, axis=-1, keepdims=True) + EPS)
        o_ref[...] = (xv * r * g_ref[...] * (1.0 + sc_ref[...]) + sh_ref[...]).astype(bf16)

    return pl.pallas_call(
        body, name=name, grid=(s // tm,), in_specs=[_tile_spec(tm, D), _row_spec(D), _row_spec(D), _row_spec(D)],
        out_specs=_tile_spec(tm, D), out_shape=_sds((s, D), bf16), compiler_params=_params(("parallel",)),
    )(x, g, sc, sh)


def _resid_norm_mod(x, gt, mix, g, sc, sh, tm):
    s = x.shape[0]

    def body(x_ref, gt_ref, m_ref, g_ref, sc_ref, sh_ref, x2_ref, h_ref):
        xv = x_ref[...] + gt_ref[...] * m_ref[...]
        x2_ref[...] = xv
        r = lax.rsqrt(jnp.mean(xv * xv, axis=-1, keepdims=True) + EPS)
        h_ref[...] = (xv * r * g_ref[...] * (1.0 + sc_ref[...]) + sh_ref[...]).astype(bf16)

    return pl.pallas_call(
        body, name="resid_norm_mod", grid=(s // tm,),
        in_specs=[_tile_spec(tm, D), _row_spec(D), _tile_spec(tm, D), _row_spec(D), _row_spec(D), _row_spec(D)],
        out_specs=[_tile_spec(tm, D), _tile_spec(tm, D)], out_shape=[_sds((s, D), f32), _sds((s, D), bf16)],
        compiler_params=_params(("parallel",)),
    )(x, gt, mix, g, sc, sh)


def _final_fwd_bwd(x2, ffn, gt2, g, tgt, tm):
    s = x2.shape[0]
    n = s // tm

    def body(x_ref, f_ref, gt_ref, g_ref, t_ref, dx_ref, df_ref, loss_ref, dg_ref, dgt_ref, a_loss, a_dg, a_dgt):
        i = pl.program_id(0)

        @pl.when(i == 0)
        def _():
            a_loss[...] = jnp.zeros_like(a_loss)
            a_dg[...] = jnp.zeros_like(a_dg)
            a_dgt[...] = jnp.zeros_like(a_dgt)

        fv = f_ref[...]
        gt = gt_ref[...]
        gv = g_ref[...]
        xv = x_ref[...] + gt * fv
        r = lax.rsqrt(jnp.mean(xv * xv, axis=-1, keepdims=True) + EPS)
        xh = xv * r
        e = xh * gv - t_ref[...]
        a_loss[...] += _fold8(e * e)
        dy = e * (1.0 / D)
        a_dg[...] += _fold8(dy * xh)
        t = dy * gv
        dx = r * (t - xh * jnp.mean(t * xh, axis=-1, keepdims=True))
        dx_ref[...] = dx
        a_dgt[...] += _fold8(dx * fv)
        df_ref[...] = (dx * gt).astype(bf16)

        @pl.when(i == n - 1)
        def _():
            tot = jnp.sum(jnp.sum(a_loss[...], axis=0, keepdims=True), axis=1, keepdims=True) * (0.5 / D)
            loss_ref[...] = jnp.broadcast_to(tot, (1, 128))
            dg_ref[...] = jnp.sum(a_dg[...], axis=0, keepdims=True)
            dgt_ref[...] = jnp.sum(a_dgt[...], axis=0, keepdims=True)

    return pl.pallas_call(
        body, name="final_fwd_bwd", grid=(n,),
        in_specs=[_tile_spec(tm, D), _tile_spec(tm, D), _row_spec(D), _row_spec(D), _tile_spec(tm, D)],
        out_specs=[_tile_spec(tm, D), _tile_spec(tm, D), _row_spec(128), _row_spec(D), _row_spec(D)],
        out_shape=[_sds((s, D), f32), _sds((s, D), bf16), _sds((1, 128), f32), _sds((1, D), f32), _sds((1, D), f32)],
        scratch_shapes=[pltpu.VMEM((8, D), f32)] * 3, compiler_params=_params(("arbitrary",)),
    )(x2, ffn, gt2, g, tgt)


def _norm_mod_bwd(name, dh, xin, g, sc, dres, tm, mix=None, gt=None):
    s = dh.shape[0]
    n = s // tm
    with_mix = mix is not None

    def body(*refs):
        if with_mix:
            dh_ref, x_ref, g_ref, sc_ref, dr_ref, m_ref, gt_ref, dx_ref, dm_ref, dsc_ref, dsh_ref, dg_ref, dgt_ref, a_sc, a_sh, a_g, a_gt = refs
        else:
            dh_ref, x_ref, g_ref, sc_ref, dr_ref, dx_ref, dsc_ref, dsh_ref, dg_ref, a_sc, a_sh, a_g = refs
        i = pl.program_id(0)

        @pl.when(i == 0)
        def _():
            a_sc[...] = jnp.zeros_like(a_sc)
            a_sh[...] = jnp.zeros_like(a_sh)
            a_g[...] = jnp.zeros_like(a_g)
            if with_mix:
                a_gt[...] = jnp.zeros_like(a_gt)

        dh = dh_ref[...]
        xv = x_ref[...]
        gv = g_ref[...]
        r = lax.rsqrt(jnp.mean(xv * xv, axis=-1, keepdims=True) + EPS)
        xh = xv * r
        a_sc[...] += _fold8(dh * xh * gv)
        a_sh[...] += _fold8(dh)
        dn = dh * (1.0 + sc_ref[...])
        a_g[...] += _fold8(dn * xh)
        t = dn * gv
        dx = dr_ref[...] + r * (t - xh * jnp.mean(t * xh, axis=-1, keepdims=True))
        dx_ref[...] = dx
        if with_mix:
            a_gt[...] += _fold8(dx * m_ref[...])
            dm_ref[...] = (dx * gt_ref[...]).astype(bf16)

        @pl.when(i == n - 1)
        def _():
            dsc_ref[...] = jnp.sum(a_sc[...], axis=0, keepdims=True)
            dsh_ref[...] = jnp.sum(a_sh[...], axis=0, keepdims=True)
            dg_ref[...] = jnp.sum(a_g[...], axis=0, keepdims=True)
            if with_mix:
                dgt_ref[...] = jnp.sum(a_gt[...], axis=0, keepdims=True)

    tile, row = _tile_spec(tm, D), _row_spec(D)
    if with_mix:
        ins, args = [tile, tile, row, row, tile, tile, row], (dh, xin, g, sc, dres, mix, gt)
        outs = [tile, tile, row, row, row, row]
        shapes = [_sds((s, D), f32), _sds((s, D), bf16)] + [_sds((1, D), f32)] * 4
        nacc = 4
    else:
        ins, args = [tile, tile, row, row, tile], (dh, xin, g, sc, dres)
        outs = [tile, row, row, row]
        shapes = [_sds((s, D), f32)] + [_sds((1, D), f32)] * 3
        nacc = 3
    return pl.pallas_call(
        body, name=name, grid=(n,), in_specs=ins, out_specs=outs, out_shape=shapes,
        scratch_shapes=[pltpu.VMEM((8, D), f32)] * nacc, compiler_params=_params(("arbitrary",)),
    )(*args)


def _mix_pre(att, y, proj2, g_att, g_ssd, tm):
    s = att.shape[0]

    def body(a_ref, y_ref, z_ref, ga_ref, gs_ref, o_ref):
        a = a_ref[...]
        ra = lax.rsqrt(jnp.mean(a * a, axis=-1, keepdims=True) + EPS)
        o_ref[:, 0:ATT_W] = (a * ra * ga_ref[...]).astype(bf16)
        z = z_ref[...]
        u = y_ref[...] * (z * _sigmoid(z))
        ru = lax.rsqrt(jnp.mean(u * u, axis=-1, keepdims=True) + EPS)
        o_ref[:, ATT_W:] = (u * ru * gs_ref[...]).astype(bf16)

    t = _tile_spec(tm, ATT_W)
    return pl.pallas_call(
        body, name="mix_pre", grid=(s // tm,), in_specs=[t, t, t, _row_spec(ATT_W), _row_spec(SSD_W)],
        out_specs=_tile_spec(tm, D), out_shape=_sds((s, D), bf16), compiler_params=_params(("parallel",)),
    )(att, y, proj2, g_att, g_ssd)


def _mix_pre_bwd(dmc, att, y, proj2, g_att, g_ssd, tm):
    s = att.shape[0]
    n = s // tm

    def body(da_ref, ds_ref, a_ref, y_ref, z_ref, ga_ref, gs_ref, datt_ref, dy_ref, dz_ref, dga_ref, dgs_ref, acc_a, acc_s):
        i = pl.program_id(0)

        @pl.when(i == 0)
        def _():
            acc_a[...] = jnp.zeros_like(acc_a)
            acc_s[...] = jnp.zeros_like(acc_s)

        a = a_ref[...]
        ra = lax.rsqrt(jnp.mean(a * a, axis=-1, keepdims=True) + EPS)
        ah = a * ra
        dan = da_ref[...]
        acc_a[...] += _fold8(dan * ah)
        t = dan * ga_ref[...]
        datt_ref[...] = (ra * (t - ah * jnp.mean(t * ah, axis=-1, keepdims=True))).astype(bf16)
        z = z_ref[...]
        yv = y_ref[...]
        sz = _sigmoid(z)
        sil = z * sz
        u = yv * sil
        ru = lax.rsqrt(jnp.mean(u * u, axis=-1, keepdims=True) + EPS)
        uh = u * ru
        dsn = ds_ref[...]
        acc_s[...] += _fold8(dsn * uh)
        t2 = dsn * gs_ref[...]
        du = ru * (t2 - uh * jnp.mean(t2 * uh, axis=-1, keepdims=True))
        dy_ref[...] = du * sil
        dz_ref[...] = (du * yv * (sz * (1.0 + z * (1.0 - sz)))).astype(bf16)

        @pl.when(i == n - 1)
        def _():
            dga_ref[...] = jnp.sum(acc_a[...], axis=0, keepdims=True)
            dgs_ref[...] = jnp.sum(acc_s[...], axis=0, keepdims=True)

    t = _tile_spec(tm, ATT_W)
    row = _row_spec(ATT_W)
    return pl.pallas_call(
        body, name="mix_pre_bwd", grid=(n,),
        in_specs=[_tile_spec(tm, ATT_W, 0), _tile_spec(tm, ATT_W, 1), t, t, t, row, row],
        out_specs=[t, t, t, row, row],
        out_shape=[_sds((s, ATT_W), bf16), _sds((s, SSD_W), f32), _sds((s, SSD_W), bf16), _sds((1, ATT_W), f32), _sds((1, SSD_W), f32)],
        scratch_shapes=[pltpu.VMEM((8, ATT_W), f32)] * 2, compiler_params=_params(("arbitrary",)),
    )(dmc, dmc, att, y, proj2, g_att, g_ssd)


ATT_GROUP = 2


def _pair_rows(qc):
    two = jnp.concatenate([qc, qc], axis=0)
    r = lax.broadcasted_iota(jnp.int32, (2 * CHUNK, 128), 0)
    l = lax.broadcasted_iota(jnp.int32, (2 * CHUNK, 128), 1)
    return jnp.where((r < CHUNK) == (l < HD), two, jnp.zeros_like(two))


def _pair_scores(wt, kb, bias, r0):
    sc = lax.dot_general(wt, kb, (((1,), (1,)), ((), ())), preferred_element_type=f32) * (HD ** -0.5) + bias
    kidx = lax.broadcasted_iota(jnp.int32, sc.shape, 1)
    return jnp.where(r0 + kidx >= PADK, sc, -jnp.inf)


def _softmax_lanes(sc):
    e = jnp.exp(sc - jnp.max(sc, axis=-1, keepdims=True))
    return e / jnp.sum(e, axis=-1, keepdims=True)


def _pair_diag(r):
    lane = lax.broadcasted_iota(jnp.int32, (CHUNK, 128), 1)
    return jnp.where(lane < HD, r[0:CHUNK], r[CHUNK:])


def _pad_keys(k_ref, kp, s):
    kp[0:PADK, :] = jnp.zeros((PADK, 128), bf16)
    kp[PADK:PADK + s, :] = k_ref[...]
    kp[PADK + s:, :] = jnp.zeros((CHUNK, 128), bf16)


def _attn_fwd(qkv, bias2):
    s = qkv.shape[0]
    nc = s // CHUNK
    npair = NH // 2

    def body(q_ref, k_ref, v_ref, b_ref, o_ref, kp, vp):
        _pad_keys(k_ref, kp, s)
        _pad_keys(v_ref, vp, s)

        def group(g, carry):
            r0s = [pl.multiple_of((g * ATT_GROUP + u) * CHUNK, CHUNK) for u in range(ATT_GROUP)]
            scs = [_pair_scores(_pair_rows(q_ref[pl.ds(r0, CHUNK), :]), kp[pl.ds(r0, BANDP), :], b_ref[...], r0) for r0 in r0s]
            ps = [_softmax_lanes(sc).astype(bf16) for sc in scs]
            for r0, p in zip(r0s, ps):
                o_ref[pl.ds(r0, CHUNK), :] = _pair_diag(jnp.dot(p, vp[pl.ds(r0, BANDP), :], preferred_element_type=f32))
            return carry

        lax.fori_loop(0, nc // ATT_GROUP, group, 0)

    return pl.pallas_call(
        body, name="attn_fwd", grid=(npair,),
        in_specs=[pl.BlockSpec((s, 128), lambda p: (0, p)), pl.BlockSpec((s, 128), lambda p: (0, npair + p)),
                  pl.BlockSpec((s, 128), lambda p: (0, 2 * npair + p)), pl.BlockSpec((None, 2 * CHUNK, BANDP), lambda p: (p, 0, 0))],
        out_specs=pl.BlockSpec((s, 128), lambda p: (0, p)), out_shape=_sds((s, ATT_W), f32),
        scratch_shapes=[pltpu.VMEM((PADK + s + CHUNK, 128), bf16)] * 2, compiler_params=_params(("parallel",)),
    )(qkv, qkv, qkv, bias2)


def _attn_bwd(qkv, datt, bias2, bias2t):
    s = qkv.shape[0]
    nc = s // CHUNK
    npair = NH // 2
    rows = PADK + s + CHUNK
    nt = (((1,), (1,)), ((), ()))

    def body(q_ref, k_ref, v_ref, do_ref, b_ref, bt_ref, dq_ref, dk_ref, dv_ref, g_ref, kp, vp, dkp, dvp):
        _pad_keys(k_ref, kp, s)
        _pad_keys(v_ref, vp, s)
        dkp[...] = jnp.zeros_like(dkp)
        dvp[...] = jnp.zeros_like(dvp)
        g_ref[...] = jnp.zeros_like(g_ref)

        def group(g, carry):
            r0s = [pl.multiple_of((g * ATT_GROUP + u) * CHUNK, CHUNK) for u in range(ATT_GROUP)]
            wts = [_pair_rows(q_ref[pl.ds(r0, CHUNK), :]) for r0 in r0s]
            dos = [_pair_rows(do_ref[pl.ds(r0, CHUNK), :]) for r0 in r0s]
            scs = [_pair_scores(wt, kp[pl.ds(r0, BANDP), :], b_ref[...], r0) for wt, r0 in zip(wts, r0s)]
            dps = [lax.dot_general(do, vp[pl.ds(r0, BANDP), :], nt, preferred_element_type=f32) for do, r0 in zip(dos, r0s)]
            scts, dpts = [], []
            for wt, do, r0 in zip(wts, dos, r0s):
                sct = lax.dot_general(kp[pl.ds(r0, BANDP), :], wt, nt, preferred_element_type=f32) * (HD ** -0.5) + bt_ref[...]
                kidx = lax.broadcasted_iota(jnp.int32, sct.shape, 0)
                scts.append(jnp.where(r0 + kidx >= PADK, sct, -jnp.inf))
                dpts.append(lax.dot_general(vp[pl.ds(r0, BANDP), :], do, nt, preferred_element_type=f32))
            for r0, sc, dp in zip(r0s, scs, dps):
                p = _softmax_lanes(sc)
                ds = p * (dp - jnp.sum(p * dp, axis=-1, keepdims=True))
                g_ref[...] += ds
                dq = jnp.dot(ds.astype(bf16), kp[pl.ds(r0, BANDP), :], preferred_element_type=f32)
                dq_ref[pl.ds(r0, CHUNK), :] = (_pair_diag(dq) * (HD ** -0.5)).astype(bf16)
            for r0, wt, do, sct, dpt in zip(r0s, wts, dos, scts, dpts):
                e = jnp.exp(sct - jnp.max(sct, axis=0, keepdims=True))
                pt = e / jnp.sum(e, axis=0, keepdims=True)
                dst = pt * (dpt - jnp.sum(pt * dpt, axis=0, keepdims=True))
                dkp[pl.ds(r0, BANDP), :] += jnp.dot(dst.astype(bf16), wt, preferred_element_type=f32) * (HD ** -0.5)
                dvp[pl.ds(r0, BANDP), :] += jnp.dot(pt.astype(bf16), do, preferred_element_type=f32)
            return carry

        lax.fori_loop(0, nc // ATT_GROUP, group, 0)
        dk_ref[...] = dkp[PADK:PADK + s, :].astype(bf16)
        dv_ref[...] = dvp[PADK:PADK + s, :].astype(bf16)

    col = lambda off: pl.BlockSpec((s, 128), lambda p: (0, off + p))
    return pl.pallas_call(
        body, name="attn_bwd", grid=(npair,),
        in_specs=[col(0), col(npair), col(2 * npair), col(0), pl.BlockSpec((None, 2 * CHUNK, BANDP), lambda p: (p, 0, 0)),
                  pl.BlockSpec((None, BANDP, 2 * CHUNK), lambda p: (p, 0, 0))],
        out_specs=[col(0), col(0), col(0), pl.BlockSpec((None, 2 * CHUNK, BANDP), lambda p: (p, 0, 0))],
        out_shape=[_sds((s, ATT_W), bf16)] * 3 + [_sds((npair, 2 * CHUNK, BANDP), f32)],
        scratch_shapes=[pltpu.VMEM((rows, 128), bf16)] * 2 + [pltpu.VMEM((rows, 128), f32)] * 2,
        compiler_params=_params(("parallel",)),
    )(qkv, qkv, qkv, datt, bias2, bias2t)


def _rel_tables():
    onehot = np.zeros((BANDP, N_REL), np.float32)
    for j in range(BAND + CHUNK - 1):
        o = j - (CHUNK - 1)
        onehot[j, int(np.clip(PADK - o, -(CHUNK - 1), REL_CLIP)) + CHUNK - 1] = 1.0
    return onehot, np.ascontiguousarray(np.eye(CHUNK, dtype=np.float32)[::-1])


def _expand_bias(rel):
    ext = jnp.concatenate([jnp.broadcast_to(rel[:, N_REL - 1:], (NH, N_REL - 1)), rel[:, ::-1],
                           jnp.zeros((NH, BANDP - BAND + 1), f32)], axis=1)
    band = jnp.stack([ext[:, CHUNK - 1 - q:CHUNK - 1 - q + BANDP] for q in range(CHUNK)], axis=1)
    band = jnp.where(np.arange(BANDP) < BAND, band, -jnp.inf)
    return band.reshape(NH // 2, 2 * CHUNK, BANDP)


def _rel_bias_grad(gband):
    def body(g_ref, m_ref, flip_ref, o_ref, d2):
        for h in range(NH):
            rev = jnp.dot(flip_ref[...], g_ref[h], precision=HIGHEST, preferred_element_type=f32)
            rolled = pltpu.roll(rev, 0, 1, stride=1, stride_axis=0)
            d2[h:h + 1, :] = jnp.sum(rolled, axis=0, keepdims=True)
        o_ref[...] = jnp.dot(d2[...], m_ref[...], precision=HIGHEST, preferred_element_type=f32)

    onehot, flip = _rel_tables()
    return pl.pallas_call(
        body, name="rel_bias_grad", out_shape=_sds((NH, N_REL), f32), scratch_shapes=[pltpu.VMEM((NH, BANDP), f32)],
    )(gband, jnp.asarray(onehot), jnp.asarray(flip))


XBC_BLK = 512
XBC_COL0 = SSD_W // XBC_BLK
DT_COL = (SSD_W + XBC) // 128


def _conv_taps(ext, w_ref, b_ref, tm):
    n = ext.shape[0]
    pre = w_ref[3:4, :] * ext + b_ref[...]
    for j in range(3):
        pre = pre + w_ref[j:j + 1, :] * pltpu.roll(ext, 3 - j, 0)
    return pre


def _ssd_conv(proj2, conv_w, conv_b, tm):
    s = proj2.shape[0]
    nb = XBC // XBC_BLK

    def body(x_ref, p_ref, w_ref, b_ref, o_ref):
        i = pl.program_id(1)
        prev = jnp.where(i > 0, p_ref[...], 0.0)
        ext = jnp.concatenate([prev, x_ref[...]], axis=0)
        pre = _conv_taps(ext, w_ref, b_ref, tm)[8:8 + tm]
        o_ref[...] = pre * _sigmoid(pre)

    return pl.pallas_call(
        body, name="ssd_conv", grid=(nb, s // tm),
        in_specs=[pl.BlockSpec((tm, XBC_BLK), lambda j, i: (i, XBC_COL0 + j)),
                  pl.BlockSpec((8, XBC_BLK), lambda j, i: (jnp.maximum(i * (tm // 8) - 1, 0), XBC_COL0 + j)),
                  pl.BlockSpec((4, XBC_BLK), lambda j, i: (0, j)), pl.BlockSpec((1, XBC_BLK), lambda j, i: (0, j))],
        out_specs=pl.BlockSpec((tm, XBC_BLK), lambda j, i: (i, j)), out_shape=_sds((s, XBC), f32),
        compiler_params=_params(("parallel", "parallel")),
    )(proj2, proj2, conv_w, conv_b)


def _ssd_conv_bwd(dxbc, proj2, conv_w, conv_b, tm):
    s = proj2.shape[0]
    nb = XBC // XBC_BLK
    n = s // tm
    last8 = s // 8 - 1

    def body(x_ref, xp_ref, xn_ref, d_ref, dn_ref, w_ref, b_ref, o_ref, dw_ref, db_ref):
        i = pl.program_id(1)

        @pl.when(i == 0)
        def _():
            dw_ref[...] = jnp.zeros_like(dw_ref)
            db_ref[...] = jnp.zeros_like(db_ref)

        prev = jnp.where(i > 0, xp_ref[...], 0.0)
        ext = jnp.concatenate([prev, x_ref[...], xn_ref[...]], axis=0)
        pre = _conv_taps(ext, w_ref, b_ref, tm)
        sg = _sigmoid(pre)
        dnext = jnp.where(i < n - 1, dn_ref[...], 0.0)
        dext = jnp.concatenate([jnp.zeros((8, XBC_BLK), f32), d_ref[...], dnext], axis=0)
        dpre = dext * (sg * (1.0 + pre * (1.0 - sg)))
        rows = tm + 16
        dx = w_ref[3:4, :] * dpre
        for j in range(3):
            dx = dx + w_ref[j:j + 1, :] * pltpu.roll(dpre, rows - (3 - j), 0)
        o_ref[...] = dx[8:8 + tm].astype(bf16)
        dcur = dpre[8:8 + tm]
        db_ref[...] += jnp.sum(dcur, axis=0, keepdims=True)
        dw_ref[3:4, :] += jnp.sum(dcur * ext[8:8 + tm], axis=0, keepdims=True)
        for j in range(3):
            dw_ref[j:j + 1, :] += jnp.sum(dcur * pltpu.roll(ext, 3 - j, 0)[8:8 + tm], axis=0, keepdims=True)

    xcol = lambda j: XBC_COL0 + j
    return pl.pallas_call(
        body, name="ssd_conv_bwd", grid=(nb, n),
        in_specs=[pl.BlockSpec((tm, XBC_BLK), lambda j, i: (i, xcol(j))),
                  pl.BlockSpec((8, XBC_BLK), lambda j, i: (jnp.maximum(i * (tm // 8) - 1, 0), xcol(j))),
                  pl.BlockSpec((8, XBC_BLK), lambda j, i: (jnp.minimum((i + 1) * (tm // 8), last8), xcol(j))),
                  pl.BlockSpec((tm, XBC_BLK), lambda j, i: (i, j)),
                  pl.BlockSpec((8, XBC_BLK), lambda j, i: (jnp.minimum((i + 1) * (tm // 8), last8), j)),
                  pl.BlockSpec((4, XBC_BLK), lambda j, i: (0, j)), pl.BlockSpec((1, XBC_BLK), lambda j, i: (0, j))],
        out_specs=[pl.BlockSpec((tm, XBC_BLK), lambda j, i: (i, j)), pl.BlockSpec((4, XBC_BLK), lambda j, i: (0, j)),
                   pl.BlockSpec((1, XBC_BLK), lambda j, i: (0, j))],
        out_shape=[_sds((s, XBC), bf16), _sds((4, XBC), f32), _sds((1, XBC), f32)],
        compiler_params=_params(("parallel", "arbitrary")),
    )(proj2, proj2, proj2, dxbc, dxbc, conv_w, conv_b)


def _ssd_consts():
    ex = np.zeros((128, SSD_W), np.float32)
    for h in range(NH):
        ex[h, h * HD:(h + 1) * HD] = 1.0
    sel = np.zeros((8, 128), np.float32)
    for h in range(NH):
        sel[h // 2, h] = 1.0
    par = np.zeros((128, 128), np.float32)
    for r in range(128):
        for h in range(NH):
            par[r, h] = 1.0 if (h % 2) == (r // 64) else 0.0
    ones_blk = np.zeros((128, 128), np.float32)
    for r in range(128):
        ones_blk[r, (r // 64) * 64:(r // 64) * 64 + 64] = 1.0
    return ex, np.ascontiguousarray(ex.T), sel, par, ones_blk


def _ssd_common(xbc_ref, dtr_ref, a_ref, dtb_ref, ex_ref, sel_ref, par_ref):
    xs = xbc_ref[:, 0:SSD_W]
    dt = _softplus(dtr_ref[...] + dtb_ref[...])
    adt = dt * a_ref[...]
    r_i = lax.broadcasted_iota(jnp.int32, (CHUNK, CHUNK), 0)
    c_i = lax.broadcasted_iota(jnp.int32, (CHUNK, CHUNK), 1)
    tril = (r_i >= c_i).astype(f32)
    cs = jnp.dot(tril, adt, precision=HIGHEST, preferred_element_type=f32)
    cs2 = jnp.concatenate([cs, cs], axis=0) * par_ref[...]
    cstp = lax.dot_general(sel_ref[...], cs2, (((1,), (1,)), ((), ())), precision=HIGHEST, preferred_element_type=f32)
    ex = ex_ref[...]
    dt_full = jnp.dot(dt, ex, precision=HIGHEST, preferred_element_type=f32)
    cs_full = jnp.dot(cs, ex, precision=HIGHEST, preferred_element_type=f32)
    return xs, dt, cs, cstp, dt_full, cs_full


def _pair_mask():
    l_i = lax.broadcasted_iota(jnp.int32, (CHUNK, 128), 0)
    lane = lax.broadcasted_iota(jnp.int32, (CHUNK, 128), 1)
    return l_i >= (lane % CHUNK), lane < HD


def _block_diag(xp, first):
    z = jnp.zeros_like(xp)
    return jnp.concatenate([jnp.where(first, xp, z), jnp.where(first, z, xp)], axis=0)


def _ssd_fwd(xbc, proj2, a_row, dtb_row, dsk_full):
    s = xbc.shape[0]
    nc = s // CHUNK
    ex, ext, sel, par, ones_blk = _ssd_consts()

    def body(xbc_ref, dtr_ref, a_ref, dtb_ref, dsk_ref, ex_ref, sel_ref, par_ref, y_ref, hs_ref, hst):
        @pl.when(pl.program_id(0) == 0)
        def _():
            hst[...] = jnp.zeros_like(hst)

        hs_ref[...] = hst[...]
        xs, dt, cs, cstp, dt_full, cs_full = _ssd_common(xbc_ref, dtr_ref, a_ref, dtb_ref, ex_ref, sel_ref, par_ref)
        cs_last = cs_full[CHUNK - 1:CHUNK, :]
        xdt = xs * dt_full
        causal, first = _pair_mask()
        for g in range(NG):
            gl = slice(g * GW, (g + 1) * GW)
            bg = xbc_ref[:, SSD_W + g * NSTATE:SSD_W + (g + 1) * NSTATE].astype(bf16)
            cg = xbc_ref[:, SSD_W + NG * NSTATE + g * NSTATE:SSD_W + NG * NSTATE + (g + 1) * NSTATE].astype(bf16)
            cb2 = lax.dot_general(cg, jnp.concatenate([bg, bg], axis=0), (((1,), (1,)), ((), ())), preferred_element_type=f32)
            hg = hst[g]
            y0 = jnp.dot(cg, hg.astype(bf16), preferred_element_type=f32)
            yoff = jnp.exp(cs_full[:, gl]) * y0
            for j in range(GW // 128):
                pair = g * (GW // 128) + j
                pl_ = slice(pair * 128, (pair + 1) * 128)
                seg = jnp.exp(jnp.where(causal, cs_full[:, pl_] - cstp[pair:pair + 1, :], -jnp.inf))
                m = (cb2 * seg).astype(bf16)
                yd = jnp.dot(m, _block_diag(xdt[:, pl_].astype(bf16), first), preferred_element_type=f32)
                y_ref[:, pl_] = yd + yoff[:, j * 128:(j + 1) * 128] + xs[:, pl_] * dsk_ref[:, pl_]
            xdec = (xdt[:, gl] * jnp.exp(cs_last[:, gl] - cs_full[:, gl])).astype(bf16)
            st = lax.dot_general(bg, xdec, (((0,), (0,)), ((), ())), preferred_element_type=f32)
            hst[g] = jnp.exp(cs_last[:, gl]) * hg + st

    const = lambda shape: pl.BlockSpec(shape, lambda c: tuple(0 for _ in shape))
    return pl.pallas_call(
        body, name="ssd_fwd", grid=(nc,),
        in_specs=[pl.BlockSpec((CHUNK, XBC), lambda c: (c, 0)), pl.BlockSpec((CHUNK, 128), lambda c: (c, DT_COL)),
                  const((1, 128)), const((1, 128)), const((1, SSD_W)), const((128, SSD_W)), const((8, 128)), const((128, 128))],
        out_specs=[pl.BlockSpec((CHUNK, SSD_W), lambda c: (c, 0)), pl.BlockSpec((None, NG, NSTATE, GW), lambda c: (c, 0, 0, 0))],
        out_shape=[_sds((s, SSD_W), f32), _sds((nc, NG, NSTATE, GW), f32)],
        scratch_shapes=[pltpu.VMEM((NG, NSTATE, GW), f32)], compiler_params=_params(("arbitrary",)),
    )(xbc, proj2, a_row, dtb_row, dsk_full, jnp.asarray(ex), jnp.asarray(sel), jnp.asarray(par))


def _ssd_bwd(xbc, proj2, dy, hsave, a_row, dtb_row, dsk_full):
    s = xbc.shape[0]
    nc = s // CHUNK
    ex, ext, sel, par, ones_blk = _ssd_consts()

    def body(xbc_ref, dtr_ref, dy_ref, hs_ref, a_ref, dtb_ref, dsk_ref, ex_ref, ext_ref, sel_ref, par_ref, ob_ref,
             dxbc_ref, ddtr_ref, dd_ref, da_ref, ddtb_ref, dh, a_dd, a_da, a_dtb, dcs_lane, dcs_b, dxdt):
        step = pl.program_id(0)

        @pl.when(step == 0)
        def _():
            dh[...] = jnp.zeros_like(dh)
            a_dd[...] = jnp.zeros_like(a_dd)
            a_da[...] = jnp.zeros_like(a_da)
            a_dtb[...] = jnp.zeros_like(a_dtb)

        xs, dt, cs, cstp, dt_full, cs_full = _ssd_common(xbc_ref, dtr_ref, a_ref, dtb_ref, ex_ref, sel_ref, par_ref)
        cs_last = cs_full[CHUNK - 1:CHUNK, :]
        xdt = xs * dt_full
        dyv = dy_ref[...]
        a_dd[...] += _fold8(dyv * xs)
        causal, first = _pair_mask()
        ones_l = jnp.ones((CHUNK, 128), f32)
        for g in range(NG):
            gl = slice(g * GW, (g + 1) * GW)
            bcol = slice(SSD_W + g * NSTATE, SSD_W + (g + 1) * NSTATE)
            ccol = slice(SSD_W + NG * NSTATE + g * NSTATE, SSD_W + NG * NSTATE + (g + 1) * NSTATE)
            bg = xbc_ref[:, bcol].astype(bf16)
            cg = xbc_ref[:, ccol].astype(bf16)
            bg2 = jnp.concatenate([bg, bg], axis=0)
            cb2 = lax.dot_general(cg, bg2, (((1,), (1,)), ((), ())), preferred_element_type=f32)
            hg = hs_ref[g]
            hgb = hg.astype(bf16)
            dhg = dh[g]
            dhgb = dhg.astype(bf16)
            eg = jnp.exp(cs_full[:, gl])
            dec = jnp.exp(cs_last[:, gl] - cs_full[:, gl])
            gam = jnp.exp(cs_last[:, gl])
            dyg = dyv[:, gl]
            xdt_g = xdt[:, gl]
            y0 = jnp.dot(cg, hgb, preferred_element_type=f32)
            dy0 = (eg * dyg).astype(bf16)
            dcm = lax.dot_general(dy0, hgb, (((1,), (1,)), ((), ())), preferred_element_type=f32)
            dh_prev = gam * dhg + lax.dot_general(cg, dy0, (((0,), (0,)), ((), ())), preferred_element_type=f32)
            dgam = jnp.sum(dhg * hg, axis=0, keepdims=True) * gam
            dxdec = jnp.dot(bg, dhgb, preferred_element_type=f32)
            dbm = lax.dot_general((xdt_g * dec).astype(bf16), dhgb, (((1,), (1,)), ((), ())), preferred_element_type=f32)
            t = dxdec * xdt_g * dec
            dcs_lane[:, gl] = dyg * eg * y0 - t
            dcs_lane[CHUNK - 1:CHUNK, gl] += jnp.sum(t, axis=0, keepdims=True) + dgam
            dxdt[:, gl] = dxdec * dec
            dcb2 = jnp.zeros((CHUNK, 128), f32)
            for j in range(GW // 128):
                pair = g * (GW // 128) + j
                pl_ = slice(pair * 128, (pair + 1) * 128)
                seg = jnp.exp(jnp.where(causal, cs_full[:, pl_] - cstp[pair:pair + 1, :], -jnp.inf))
                m = cb2 * seg
                mb = m.astype(bf16)
                rhs = _block_diag(xdt[:, pl_].astype(bf16), first)
                dyp = dyv[:, pl_].astype(bf16)
                dm = lax.dot_general(dyp, rhs, (((1,), (1,)), ((), ())), preferred_element_type=f32)
                tt = lax.dot_general(mb, dyp, (((0,), (0,)), ((), ())), preferred_element_type=f32)
                dxdt[:, pl_] += jnp.where(first, tt[0:CHUNK], tt[CHUNK:])
                dcb2 = dcb2 + dm * seg
                w = dm * m
                rsum = jnp.dot(w, ob_ref[...], precision=HIGHEST, preferred_element_type=f32)
                t2 = lax.dot_general(w, ones_l, (((0,), (0,)), ((), ())), precision=HIGHEST, preferred_element_type=f32)
                dcs_b[:, pl_] = rsum - jnp.where(first, t2[0:CHUNK], t2[CHUNK:])
            dcb2b = dcb2.astype(bf16)
            dcm = dcm + jnp.dot(dcb2b, bg2, preferred_element_type=f32)
            t3 = lax.dot_general(dcb2b, cg, (((0,), (0,)), ((), ())), preferred_element_type=f32)
            dxbc_ref[:, bcol] = dbm + t3[0:CHUNK] + t3[CHUNK:]
            dxbc_ref[:, ccol] = dcm
            dh[g] = dh_prev
        dcs = jnp.dot(dcs_lane[...] + dcs_b[...] * (1.0 / HD), ext_ref[...], precision=HIGHEST, preferred_element_type=f32)
        r_i = lax.broadcasted_iota(jnp.int32, (CHUNK, CHUNK), 0)
        c_i = lax.broadcasted_iota(jnp.int32, (CHUNK, CHUNK), 1)
        triu = (r_i <= c_i).astype(f32)
        da_ = jnp.dot(triu, dcs, precision=HIGHEST, preferred_element_type=f32)
        dxdtv = dxdt[...]
        ddt = da_ * a_ref[...] + jnp.dot(dxdtv * xs, ext_ref[...], precision=HIGHEST, preferred_element_type=f32)
        a_da[...] += _fold8(da_ * dt)
        dxbc_ref[:, 0:SSD_W] = dyv * dsk_ref[...] + dxdtv * dt_full
        ddtr = ddt * _sigmoid(dtr_ref[...] + dtb_ref[...])
        ddtr_ref[...] = ddtr
        a_dtb[...] += _fold8(ddtr)

        @pl.when(step == nc - 1)
        def _():
            dd_ref[...] = jnp.sum(jnp.dot(a_dd[...], ext_ref[...], precision=HIGHEST, preferred_element_type=f32), axis=0, keepdims=True)
            da_ref[...] = jnp.sum(a_da[...], axis=0, keepdims=True)
            ddtb_ref[...] = jnp.sum(a_dtb[...], axis=0, keepdims=True)

    rev = lambda c: nc - 1 - c
    const = lambda shape: pl.BlockSpec(shape, lambda c: tuple(0 for _ in shape))
    return pl.pallas_call(
        body, name="ssd_bwd", grid=(nc,),
        in_specs=[pl.BlockSpec((CHUNK, XBC), lambda c: (rev(c), 0)), pl.BlockSpec((CHUNK, 128), lambda c: (rev(c), DT_COL)),
                  pl.BlockSpec((CHUNK, SSD_W), lambda c: (rev(c), 0)), pl.BlockSpec((None, NG, NSTATE, GW), lambda c: (rev(c), 0, 0, 0)),
                  const((1, 128)), const((1, 128)), const((1, SSD_W)), const((128, SSD_W)), const((SSD_W, 128)),
                  const((8, 128)), const((128, 128)), const((128, 128))],
        out_specs=[pl.BlockSpec((CHUNK, XBC), lambda c: (rev(c), 0)), pl.BlockSpec((CHUNK, 128), lambda c: (rev(c), 0)),
                   const((1, 128)), const((1, 128)), const((1, 128))],
        out_shape=[_sds((s, XBC), f32), _sds((s, 128), f32), _sds((1, 128), f32), _sds((1, 128), f32), _sds((1, 128), f32)],
        scratch_shapes=[pltpu.VMEM((NG, NSTATE, GW), f32), pltpu.VMEM((8, SSD_W), f32), pltpu.VMEM((8, 128), f32), pltpu.VMEM((8, 128), f32),
                        pltpu.VMEM((CHUNK, SSD_W), f32), pltpu.VMEM((CHUNK, SSD_W), f32), pltpu.VMEM((CHUNK, SSD_W), f32)],
        compiler_params=_params(("arbitrary",)),
    )(xbc, proj2, dy, hsave, a_row, dtb_row, dsk_full, jnp.asarray(ex), jnp.asarray(ext), jnp.asarray(sel), jnp.asarray(par),
      jnp.asarray(ones_blk))


def _local_step(x, tgt, mods, g_mix, rel, conv_w, conv_b, dt_bias, a_log, d_skip, g_att, g_ssd, g_ffn, g_final, weights):
    s = x.shape[0]
    tm_e = 256 if s % 256 == 0 else s
    tm_m = 512 if s % 512 == 0 else s
    tm_l = 1024 if s % 1024 == 0 else s
    tk = 2048 if s % 2048 == 0 else s
    sh1, sc1, gt1, sh2, sc2, gt2 = [mods[:, i * D:(i + 1) * D] for i in range(6)]

    h1b = _norm_mod("norm_mod_1", x, g_mix, sc1, sh1, tm_e)
    win, win_b = weights.w_in(h1b)
    qkv = _mm_nn_fullk("proj_qkv", h1b, win, tm_m, 768, bf16, n=IN_A)
    proj2 = _mm_nn_fullk("proj_zxbcdt", h1b, win_b, tm_m, 896, f32)
    bias = _expand_bias(rel)
    att = _attn_fwd(qkv, bias)
    xbc = _ssd_conv(proj2, conv_w, conv_b, tm_e)
    a_row = jnp.pad(-jnp.exp(a_log), ((0, 0), (0, 128 - NH)))
    dtb_row = jnp.pad(dt_bias, ((0, 0), (0, 128 - NH)))
    dsk_full = jnp.repeat(d_skip, HD, axis=1)
    y, hsave = _ssd_fwd(xbc, proj2, a_row, dtb_row, dsk_full)
    mixcat = _mix_pre(att, y, proj2, g_att, g_ssd, tm_e)
    wout = weights.w_out(mixcat)
    mix = _mm_nn_fullk("proj_out", mixcat, wout, tm_m, 1024, f32)
    x2, h2b = _resid_norm_mod(x, gt1, mix, g_ffn, sc2, sh2, tm_e)
    wg4, wu4, wd4 = weights.ffn(h2b)
    gate, up, act = _ffn_up(h2b, wg4, wu4, tm_m)
    ffn = _ffn_down(act, wd4, tm_l)

    dx3, dffn, loss, dg_final, dgt2 = _final_fwd_bwd(x2, ffn, gt2, g_final, tgt, tm_e)
    tok = weights.grad(("w_down",), [_grad_wdown4(act, dffn, 1024, tk)])
    dgate, dup = _ffn_dact(dffn, wd4, gate, up, tm_m, dep=tok)
    tok = weights.grad(("w_gate", "w_up"), [_grad_cols4("grad_w_gate", h2b, dgate, 1024, tk), _grad_cols4("grad_w_up", h2b, dup, 1024, tk)])
    dh2 = _ffn_dh(dgate, dup, wg4, wu4, tm_m, dep=tok)
    dx2, dmix, dsc2, dsh2, dg_ffn, dgt1 = _norm_mod_bwd("norm_mod_bwd_2", dh2, x2, g_ffn, sc2, dx3, tm_e, mix=mix, gt=gt1)
    tok = weights.grad(("w_out",), [_mm_tn("grad_w_out", mixcat, dmix, 1024, 1024, tk, bf16).reshape(NSH, D // NSH, D)])
    dmc = _mm_nt("dmixcat", dmix, wout, tm_m, 1024, D, f32, dep=tok)
    datt, dy, dz, dg_att, dg_ssd = _mix_pre_bwd(dmc, att, y, proj2, g_att, g_ssd, tm_e)
    dq, dk, dv, gband = _attn_bwd(qkv, datt, bias, jnp.transpose(bias, (0, 2, 1)))
    drel = _rel_bias_grad(gband.reshape(NH, CHUNK, BANDP))
    dxbc, ddtr, dd_row, da_row, ddtb_row = _ssd_bwd(xbc, proj2, dy, hsave, a_row, dtb_row, dsk_full)
    dxbc_raw, dconv_w, dconv_b = _ssd_conv_bwd(dxbc, proj2, conv_w, conv_b, tm_e)
    dproj = jnp.concatenate([dq, dk, dv, dz, dxbc_raw, ddtr.astype(bf16)], axis=1)
    gwin = _mm_tn("grad_w_in", h1b, dproj, 1024, 1152, tk, bf16)
    gwin4 = jnp.stack([jnp.pad(gwin[:, k * IN_SH:(k + 1) * IN_SH], ((0, 0), (0, IN_SHP - IN_SH))) for k in range(NSH)])
    tok = weights.grad(("w_in",), [gwin4])
    dh1 = _mm_nt("dh1", dproj, win, tm_m, D, 1920, f32, dep=tok)
    grad_x, dsc1, dsh1, dg_mix = _norm_mod_bwd("norm_mod_bwd_1", dh1, x, g_mix, sc1, dx2, tm_e)

    dmods = jnp.concatenate([dsh1, dsc1, dgt1, dsh2, dsc2, dgt2], axis=1)
    dd_skip = dd_row[:, :NH]
    da_log = da_row[:, :NH] * a_row[:, :NH]
    small = dict(g_mix=dg_mix, conv_b=dconv_b, dt_bias=ddtb_row[:, :NH], a_log=da_log, d_skip=dd_skip, g_att_out=dg_att,
                 g_ssd_out=dg_ssd, g_ffn=dg_ffn, g_final=dg_final, rel_bias=drel, conv_w=dconv_w)
    return loss[0, 0], grad_x, dmods, small


HBM = pl.BlockSpec(memory_space=pl.ANY)
VMEM = pl.BlockSpec(memory_space=pltpu.VMEM)


def _place():
    x, y, c = lax.axis_index("x"), lax.axis_index("y"), lax.axis_index("c")
    chips = [(1 - x, y), (x, 1 - y), (1 - x, 1 - y)]
    return x, y, c, chips


def _allgather8(name, payload, dep=None):
    r = payload.shape[0]
    deps = [] if dep is None else [dep]

    def body(x_ref, *rest):
        out_ref, send_sems, recv_sems, local_sem = rest[-4:]
        x, y, c, chips = _place()
        me, sibling = (x, y, c), (x, y, 1 - c)

        def slot(px, py, pc):
            return out_ref.at[4 * px + 2 * py + pc]

        def copy(k, block, to, src=None):
            return pltpu.make_async_remote_copy(
                src_ref=slot(*block) if src is None else src, dst_ref=slot(*block),
                send_sem=send_sems.at[k], recv_sem=recv_sems.at[k], device_id=to, device_id_type=MESH)

        mine = pltpu.make_async_copy(x_ref, slot(*me), local_sem)
        mine.start()
        first = [copy(0, me, sibling, src=x_ref)]
        first += [copy(1 + j, me, (*chip, c), src=x_ref) for j, chip in enumerate(chips)]
        for cp in first:
            cp.start()
        passed = [copy(4 + j, (*chip, c), sibling) for j, chip in enumerate(chips)]
        for j, chip in enumerate(chips):
            copy(1 + j, (*chip, c), me).wait_recv()
            passed[j].start()
        copy(0, sibling, me).wait_recv()
        for j, chip in enumerate(chips):
            copy(4 + j, (*chip, 1 - c), me).wait_recv()
        for cp in first + passed:
            cp.wait_send()
        mine.wait()

    return pl.pallas_call(
        body, name=name, out_shape=_sds((N_DEV, r, 128), f32), in_specs=[VMEM] * (1 + len(deps)), out_specs=VMEM,
        scratch_shapes=[pltpu.SemaphoreType.DMA((7,)), pltpu.SemaphoreType.DMA((7,)), pltpu.SemaphoreType.DMA],
    )(payload, *deps)


def _sum8(g):
    r = g.shape[1]

    def body(g_ref, o_ref):
        acc = g_ref[0]
        for i in range(1, N_DEV):
            acc = acc + g_ref[i]
        o_ref[...] = acc

    return pl.pallas_call(body, name="sum8", out_shape=_sds((r, 128), f32))(g)


SEM = pl.BlockSpec(memory_space=pltpu.SEMAPHORE)
EFFECT = pltpu.SideEffectType.DATAFLOW_SIDE_EFFECTING


def _gather_copies(ins, lands, send_sems, recv_sems):
    x, y, c, chips = _place()
    k = 2 * x + y
    starts, recvs = [], []
    for w in range(len(ins)):
        for j, (px, py) in enumerate(chips):
            def mk(dst):
                return pltpu.make_async_remote_copy(src_ref=ins[w].at[c], dst_ref=dst, send_sem=send_sems[w].at[j],
                                                    recv_sem=recv_sems[w].at[j], device_id=(px, py, c), device_id_type=MESH)
            starts.append(mk(lands[w].at[k, c]))
            recvs.append(mk(lands[w].at[2 * px + py, c]))
    return starts, recvs


def _reduce_copies(ins, lands, send_sems, recv_sems):
    x, y, c, chips = _place()
    k = 2 * x + y
    starts, recvs = [], []
    for w in range(len(ins)):
        for j, (px, py) in enumerate(chips):
            def mk(dst):
                return pltpu.make_async_remote_copy(src_ref=ins[w].at[2 * px + py], dst_ref=dst, send_sem=send_sems[w].at[j],
                                                    recv_sem=recv_sems[w].at[j], device_id=(px, py, c), device_id_type=MESH)
            starts.append(mk(lands[w].at[k]))
            recvs.append(mk(lands[w].at[2 * px + py]))
    return starts, recvs


def _split_start(name, copies, srcs, land_shapes):
    nw = len(srcs)

    def body(*refs):
        starts, _ = copies(refs[:nw], refs[nw:2 * nw], refs[2 * nw:3 * nw], refs[3 * nw:4 * nw])
        for cp in starts:
            cp.start()
        refs[6 * nw][...] = jnp.zeros((8, 128), f32)

    sems = [pltpu.SemaphoreType.DMA((3,))] * nw
    bufs = [pltpu.HBM(s.shape, bf16) for s in srcs] + [pltpu.HBM(s, bf16) for s in land_shapes]
    res = pl.pallas_call(
        body, name=name, out_shape=sems + sems + bufs + [_sds((8, 128), f32)],
        in_specs=[HBM] * (2 * nw), out_specs=[SEM] * (2 * nw) + [HBM] * (2 * nw) + [VMEM],
        input_output_aliases={i: 2 * nw + i for i in range(2 * nw)},
        compiler_params=pltpu.CompilerParams(has_side_effects=EFFECT),
    )(*[pltpu.with_memory_space_constraint(s, pltpu.HBM) for s in srcs],
      *[pltpu.with_memory_space_constraint(lax.empty(s, bf16), pltpu.HBM) for s in land_shapes])
    return res[:nw], res[nw:2 * nw], res[2 * nw:3 * nw], res[3 * nw:4 * nw], res[4 * nw]


def _split_wait(name, copies, send_sems, recv_sems, srcs, lands, after):
    nw = len(srcs)

    def body(*refs):
        starts, recvs = copies(refs[:nw], refs[nw:2 * nw], refs[2 * nw:3 * nw], refs[3 * nw:4 * nw])
        for s_, r_ in zip(starts, recvs):
            s_.wait_send()
            r_.wait_recv()

    bufs = [pltpu.HBM(s.shape, bf16) for s in srcs] + [pltpu.HBM(l.shape, bf16) for l in lands]
    res = pl.pallas_call(
        body, name=name, out_shape=bufs, in_specs=[HBM] * (2 * nw) + [SEM] * (2 * nw) + [HBM], out_specs=[HBM] * (2 * nw),
        input_output_aliases={i: i for i in range(2 * nw)},
        compiler_params=pltpu.CompilerParams(has_side_effects=EFFECT),
    )(*srcs, *lands, *send_sems, *recv_sems, after)
    return res[:nw], res[nw:]


def _gather_forward(name, shards, lands):
    nw = len(shards)

    def body(*refs):
        ins, lands_in, outs = refs[:nw], refs[nw:2 * nw], refs[2 * nw:3 * nw]
        st_a, st_b, st_c = refs[3 * nw:4 * nw], refs[4 * nw:5 * nw], refs[5 * nw:6 * nw]
        send_sems, recv_sems, load_sems, store_sems = refs[6 * nw:]
        x, y, c, chips = _place()
        k = 2 * x + y
        sibling = (x, y, 1 - c)
        ld_a = [pltpu.make_async_copy(ins[w].at[c], st_a[w], load_sems.at[w, 0]) for w in range(nw)]
        ld_b = [pltpu.make_async_copy(ins[w].at[1 - c], st_b[w], load_sems.at[w, 1]) for w in range(nw)]
        for cp in ld_a + ld_b:
            cp.start()
        st_own = []
        for w in range(nw):
            ld_a[w].wait()
            st_own.append(pltpu.make_async_copy(st_a[w], outs[w].at[k, c], store_sems.at[w, 0]))
            st_own[-1].start()
        for w in range(nw):
            ld_b[w].wait()
            st_own.append(pltpu.make_async_copy(st_b[w], outs[w].at[k, 1 - c], store_sems.at[w, 1]))
            st_own[-1].start()
        for cp in st_own:
            cp.wait()
        fwds = {}
        for j, (px, py) in enumerate(chips):
            kq = 2 * px + py
            for w in range(nw):
                slot = st_b[w] if j % 2 == 0 else st_c[w]
                if j == 2:
                    fwds[w, 0].wait_send()
                ld = pltpu.make_async_copy(lands_in[w].at[kq, c], slot, load_sems.at[w, 2 + j])
                ld.start()
                ld.wait()
                fwds[w, j] = pltpu.make_async_remote_copy(src_ref=slot, dst_ref=outs[w].at[kq, c], send_sem=send_sems.at[w, j],
                                                          recv_sem=recv_sems.at[w, j], device_id=sibling, device_id_type=MESH)
                fwds[w, j].start()
        for j, (px, py) in enumerate(chips):
            for w in range(nw):
                pltpu.make_async_remote_copy(src_ref=st_c[w], dst_ref=outs[w].at[2 * px + py, 1 - c], send_sem=send_sems.at[w, j],
                                             recv_sem=recv_sems.at[w, j], device_id=sibling, device_id_type=MESH).wait_recv()
        for w in range(nw):
            fwds[w, 1].wait_send()
            fwds[w, 2].wait_send()

    stage = [pltpu.VMEM(s.shape[1:], bf16) for s in shards]
    return pl.pallas_call(
        body, name=name, out_shape=[_sds(l.shape, bf16) for l in lands],
        in_specs=[HBM] * (2 * nw), out_specs=[HBM] * nw, input_output_aliases={nw + w: w for w in range(nw)},
        scratch_shapes=stage * 3 + [pltpu.SemaphoreType.DMA((nw, 3)), pltpu.SemaphoreType.DMA((nw, 3)), pltpu.SemaphoreType.DMA((nw, 5)),
                                    pltpu.SemaphoreType.DMA((nw, 2))],
        compiler_params=pltpu.CompilerParams(vmem_limit_bytes=VMEM_LIMIT),
    )(*shards, *lands)


def _rs_pair_exchange(name, grads):
    nw = len(grads)

    def body(*refs):
        ins, got, stage = refs[:nw], refs[nw:2 * nw], refs[2 * nw:3 * nw]
        send_sems, recv_sems, load_sems = refs[3 * nw:]
        x, y, c, _ = _place()

        def load(w, kk):
            return pltpu.make_async_copy(ins[w].at[kk, 1 - c], stage[w].at[kk % 2], load_sems.at[w, kk])

        def send(w, kk):
            return pltpu.make_async_remote_copy(src_ref=stage[w].at[kk % 2], dst_ref=got[w].at[kk], send_sem=send_sems.at[w, kk],
                                                recv_sem=recv_sems.at[w, kk], device_id=(x, y, 1 - c), device_id_type=MESH)

        for kk in range(2):
            for w in range(nw):
                load(w, kk).start()
        for kk in range(NSH):
            for w in range(nw):
                load(w, kk).wait()
                send(w, kk).start()
            if kk + 2 < NSH:
                for w in range(nw):
                    send(w, kk).wait_send()
                    load(w, kk + 2).start()
        for kk in range(NSH - 2, NSH):
            for w in range(nw):
                send(w, kk).wait_send()
        for kk in range(NSH):
            for w in range(nw):
                send(w, kk).wait_recv()

    return pl.pallas_call(
        body, name=name, out_shape=[_sds((NSH,) + g.shape[2:], bf16) for g in grads], in_specs=[HBM] * nw, out_specs=[HBM] * nw,
        scratch_shapes=[pltpu.VMEM((2,) + g.shape[2:], bf16) for g in grads]
        + [pltpu.SemaphoreType.DMA((nw, NSH)), pltpu.SemaphoreType.DMA((nw, NSH)), pltpu.SemaphoreType.DMA((nw, NSH))],
        compiler_params=pltpu.CompilerParams(vmem_limit_bytes=VMEM_LIMIT),
    )(*grads)


def _rs_pair_gather(name, halves):
    nw = len(halves)

    def body(*refs):
        ins, outs, stage = refs[:nw], refs[nw:2 * nw], refs[2 * nw:3 * nw]
        send_sems, recv_sems, local_sems, stage_sems = refs[3 * nw:]
        x, y, c, _ = _place()
        loads = [pltpu.make_async_copy(ins[w], stage[w], stage_sems.at[w]) for w in range(nw)]
        for cp in loads:
            cp.start()
        local, cps = [], []
        for w in range(nw):
            loads[w].wait()
            local.append(pltpu.make_async_copy(stage[w], outs[w].at[c], local_sems.at[w]))
            cps.append(pltpu.make_async_remote_copy(src_ref=stage[w], dst_ref=outs[w].at[c], send_sem=send_sems.at[w],
                                                    recv_sem=recv_sems.at[w], device_id=(x, y, 1 - c), device_id_type=MESH))
            local[w].start()
            cps[w].start()
        for w in range(nw):
            pltpu.make_async_remote_copy(src_ref=stage[w], dst_ref=outs[w].at[1 - c], send_sem=send_sems.at[w], recv_sem=recv_sems.at[w],
                                         device_id=(x, y, 1 - c), device_id_type=MESH).wait_recv()
        for cp in cps:
            cp.wait_send()
        for cp in local:
            cp.wait()

    return pl.pallas_call(
        body, name=name, out_shape=[_sds((2,) + h.shape, f32) for h in halves], in_specs=[HBM] * nw, out_specs=[HBM] * nw,
        scratch_shapes=[pltpu.VMEM(h.shape, f32) for h in halves]
        + [pltpu.SemaphoreType.DMA((nw,)), pltpu.SemaphoreType.DMA((nw,)), pltpu.SemaphoreType.DMA((nw,)), pltpu.SemaphoreType.DMA((nw,))],
        compiler_params=pltpu.CompilerParams(vmem_limit_bytes=VMEM_LIMIT),
    )(*halves)


def _row_tile(r, c, nbuf):
    budget = 24 * 1024 * 1024 // (2 * nbuf * 4 * c)
    t = 8
    while t * 2 <= budget and r % (t * 2) == 0:
        t *= 2
    return t


def _cast_bf16(name, a):
    r, c = a.shape
    tr = _row_tile(r, c, 2)

    def body(a_ref, o_ref):
        o_ref[...] = a_ref[...].astype(bf16)

    spec = pl.BlockSpec((tr, c), lambda i: (i, 0))
    return pl.pallas_call(body, name=name, grid=(r // tr,), in_specs=[spec], out_specs=spec, out_shape=_sds((r, c), bf16),
                          compiler_params=_params(("parallel",)))(a)


def _w_in_columns(win4):
    tr = 256

    def body(a_ref, o_ref, ob_ref):
        for k in range(NSH):
            o_ref[:, IN_SH * k:IN_SH * (k + 1)] = a_ref[k][:, :IN_SH]
        o_ref[:, IN_COLS:] = jnp.zeros((tr, IN_P - IN_COLS), bf16)
        ob_ref[...] = o_ref[:, IN_A:]

    return pl.pallas_call(
        body, name="w_in_columns", grid=(D // tr,), in_specs=[pl.BlockSpec((NSH, tr, IN_SHP), lambda i: (0, i, 0))],
        out_specs=[pl.BlockSpec((tr, IN_P), lambda i: (i, 0)), pl.BlockSpec((tr, IN_B), lambda i: (i, 0))],
        out_shape=[_sds((D, IN_P), bf16), _sds((D, IN_B), bf16)], compiler_params=_params(("parallel",)))(win4)


def _pair_sum(name, core, grads, got):
    _, _, rh, c = grads.shape
    tr = _row_tile(rh, c, 2)

    def body(c_ref, a_ref, b_ref, o_ref):
        o_ref[...] = (a_ref[...].astype(f32) + b_ref[...].astype(f32)).astype(bf16)

    spec = pl.BlockSpec((None, tr, c), lambda k, i, c_ref: (k, i, 0))
    return pl.pallas_call(
        body, name=name, out_shape=_sds((NSH, rh, c), bf16),
        grid_spec=pltpu.PrefetchScalarGridSpec(
            num_scalar_prefetch=1, grid=(NSH, rh // tr),
            in_specs=[pl.BlockSpec((None, None, tr, c), lambda k, i, c_ref: (k, c_ref[0], i, 0)), spec], out_specs=spec),
        compiler_params=_params(("parallel", "parallel")))(core, grads, got)


def _chip_sum(name, chip, sums, lands):
    _, rh, c = sums.shape
    tr = _row_tile(rh, c, 4)

    def body(k_ref, own_ref, l_ref, o_ref):
        own = own_ref[...].astype(f32)
        acc = None
        for j in range(NSH):
            term = jnp.where(k_ref[0] == j, own, l_ref[j].astype(f32))
            acc = term if acc is None else acc + term
        o_ref[...] = acc

    return pl.pallas_call(
        body, name=name, out_shape=_sds((rh, c), f32),
        grid_spec=pltpu.PrefetchScalarGridSpec(
            num_scalar_prefetch=1, grid=(rh // tr,),
            in_specs=[pl.BlockSpec((None, tr, c), lambda i, k_ref: (k_ref[0], i, 0)), pl.BlockSpec((NSH, tr, c), lambda i, k_ref: (0, i, 0))],
            out_specs=pl.BlockSpec((tr, c), lambda i, k_ref: (i, 0))),
        compiler_params=_params(("parallel",)))(chip, sums, lands)


def _mods_part(cond16, w_ada, b_part):
    n = w_ada.shape[1]
    tn = 512

    def body(c_ref, w_ref, b_ref, o_ref):
        cv = c_ref[...]
        o_ref[...] = _dot(cv * _sigmoid(cv), w_ref[...]) + b_ref[...]

    return pl.pallas_call(
        body, name="mods_part", grid=(n // tn,),
        in_specs=[pl.BlockSpec((16, D), lambda j: (0, 0)), pl.BlockSpec((D, tn), lambda j: (0, j)), pl.BlockSpec((1, tn), lambda j: (0, j))],
        out_specs=pl.BlockSpec((16, tn), lambda j: (0, j)), out_shape=_sds((16, n), f32), compiler_params=_params(("parallel",)),
    )(cond16, w_ada, b_part)


def _grad_w_ada(cond16, dm16):
    n = dm16.shape[1]
    tr = 256

    def body(c_ref, d_ref, o_ref):
        cv = c_ref[...]
        o_ref[...] = _dot(cv * _sigmoid(cv), d_ref[...], ta=True)

    return pl.pallas_call(
        body, name="grad_w_ada", grid=(D // tr,),
        in_specs=[pl.BlockSpec((16, tr), lambda i: (0, i)), pl.BlockSpec((16, n), lambda i: (0, 0))],
        out_specs=pl.BlockSpec((tr, n), lambda i: (i, 0)), out_shape=_sds((D, n), f32), compiler_params=_params(("parallel",)),
    )(cond16, dm16)


def _adamw(name, w, g, m, v):
    r, c = w.shape
    tr = _row_tile(r, c, 7)

    def body(w_ref, g_ref, m_ref, v_ref, d_ref, nm_ref, nv_ref):
        gv = g_ref[...]
        nm = ADAM_B1 * m_ref[...] + (1.0 - ADAM_B1) * gv
        nv = ADAM_B2 * v_ref[...] + (1.0 - ADAM_B2) * (gv * gv)
        nm_ref[...] = nm
        nv_ref[...] = nv
        m_hat = nm / (1.0 - ADAM_B1 ** ADAM_STEP)
        v_hat = nv / (1.0 - ADAM_B2 ** ADAM_STEP)
        d_ref[...] = -ADAM_LR * (m_hat / (jnp.sqrt(v_hat) + ADAM_EPS) + ADAM_WD * w_ref[...])

    spec = pl.BlockSpec((tr, c), lambda i: (i, 0))
    return pl.pallas_call(body, name=name, grid=(r // tr,), in_specs=[spec] * 4, out_specs=[spec] * 3, out_shape=[_sds((r, c), f32)] * 3,
                          compiler_params=_params(("parallel",)))(w, g, m, v)


def _pack(parts, rows):
    flat = []
    for p in parts:
        p = p.reshape(-1)
        flat.append(jnp.pad(p, (0, (-p.shape[0]) % 128)))
    v = jnp.concatenate(flat)
    return jnp.pad(v, (0, rows * 128 - v.shape[0])).reshape(rows, 128)


def _unpack(packed, sizes):
    lead = packed.shape[:-2]
    flat = packed.reshape(lead + (-1,))
    out, off = [], 0
    for n in sizes:
        out.append(flat[..., off:off + n])
        off += n + (-n) % 128
    return out


BIG = ("w_in", "w_out", "w_gate", "w_up", "w_down")
SMALL = ("b_ada", "g_mix", "conv_b", "dt_bias", "a_log", "d_skip", "g_att_out", "g_ssd_out", "g_ffn", "g_final", "rel_bias", "conv_w")
ORDER = ("w_ada", "b_ada", "g_mix", "w_in", "rel_bias", "conv_w", "conv_b", "dt_bias", "a_log", "d_skip", "g_att_out", "g_ssd_out",
         "w_out", "g_ffn", "w_gate", "w_up", "w_down", "g_final")
REL_SH = N_REL // NSH
CONVW_SH = XBC // NSH
ADA_SH = 6 * D // NSH


class _Exchange:
    def __init__(self, shards, core, chip):
        ssem, rsem, thru, lands, self.token = _split_start("gather_start", _gather_copies, shards, [(NSH,) + s.shape for s in shards])
        self.gathered = {n: (ssem[i], rsem[i], thru[i], lands[i]) for i, n in enumerate(BIG)}
        self.core, self.chip = core, chip
        self.pending = []

    def _whole(self, names, after):
        ssem, rsem, thru, lands = zip(*[self.gathered[n] for n in names])
        tag = "_".join(names)
        thru, lands = _split_wait("gather_wait_" + tag, _gather_copies, ssem, rsem, thru, lands, after)
        return _gather_forward("gather_forward_" + tag, thru, lands)

    def w_in(self, after):
        (win4,) = self._whole(("w_in",), after)
        return _w_in_columns(win4.reshape(NSH, D, IN_SHP))

    def w_out(self, after):
        (wout4,) = self._whole(("w_out",), after)
        return wout4.reshape(D, D)

    def ffn(self, after):
        wg4, wu4, wd4 = self._whole(("w_gate", "w_up", "w_down"), after)
        return wg4.reshape(NSH, D, FSH), wu4.reshape(NSH, D, FSH), wd4.reshape(NSH, FSH, D)

    def grad(self, names, grads):
        tag = "_".join(names)
        stacked = [g.reshape(NSH, 2, g.shape[1] // 2, g.shape[2]) for g in grads]
        got = _rs_pair_exchange("rs_pair_exchange_" + tag, stacked)
        sums = [_pair_sum("pair_sum_" + n, self.core, o, g) for n, o, g in zip(names, stacked, got)]
        self.pending.append((names, _split_start("rs_start_" + tag, _reduce_copies, sums, [s.shape for s in sums])))
        return self.pending[-1][1][4]

    def finish(self, after):
        grads = {}
        for names, (ssem, rsem, sums, lands, _) in self.pending:
            tag = "_".join(names)
            sums, lands = _split_wait("rs_wait_" + tag, _reduce_copies, ssem, rsem, sums, lands, after)
            halves = [_chip_sum("chip_sum_" + n, self.chip, sm, ld) for n, sm, ld in zip(names, sums, lands)]
            for n, f in zip(names, _rs_pair_gather("rs_pair_gather_" + tag, halves)):
                grads[n] = f.reshape(2 * f.shape[1], f.shape[2])
        return grads


def kernel(x, c, w_ada, b_ada, g_mix, w_in, rel_bias, conv_w, conv_b, dt_bias, a_log, d_skip, g_att_out, g_ssd_out, w_out, g_ffn, w_gate, w_up, w_down, g_final, loss_target, m_w_ada, m_b_ada, m_g_mix, m_w_in, m_rel_bias, m_conv_w, m_conv_b, m_dt_bias, m_a_log, m_d_skip, m_g_att_out, m_g_ssd_out, m_w_out, m_g_ffn, m_w_gate, m_w_up, m_w_down, m_g_final, v_w_ada, v_b_ada, v_g_mix, v_w_in, v_rel_bias, v_conv_w, v_conv_b, v_dt_bias, v_a_log, v_d_skip, v_g_att_out, v_g_ssd_out, v_w_out, v_g_ffn, v_w_gate, v_w_up, v_w_down, v_g_final):
    args = dict(locals())
    w = {n: args[n] for n in ORDER}
    m = {n: args["m_" + n] for n in ORDER}
    v = {n: args["v_" + n] for n in ORDER}
    ix, iy, ic = lax.axis_index("x"), lax.axis_index("y"), lax.axis_index("c")
    chip = 2 * ix + iy
    dev = 2 * chip + ic
    s = x.shape[1]

    shards = [
        _cast_bf16("cast_w_in", jnp.pad(w_in[0], ((0, 0), (0, IN_SHP - IN_SH)))).reshape(2, D // 2, IN_SHP),
        _cast_bf16("cast_w_out", w_out[0]).reshape(2, D // NSH // 2, D),
        _cast_bf16("cast_w_gate", w_gate[0]).reshape(2, D // 2, FSH),
        _cast_bf16("cast_w_up", w_up[0]).reshape(2, D // 2, FSH),
        _cast_bf16("cast_w_down", w_down[0]).reshape(2, FSH // 2, D),
    ]
    exchange = _Exchange(shards, jnp.reshape(ic, (1,)).astype(jnp.int32), jnp.reshape(chip, (1,)).astype(jnp.int32))

    g1 = _allgather8("gather_inputs", _pack([c[0], rel_bias[0], conv_w[0]], 40), dep=exchange.token)
    c_all, rel_sh, convw_sh = _unpack(g1, [D, NH * REL_SH, 4 * CONVW_SH])
    rel_full = jnp.concatenate([rel_sh[2 * k].reshape(NH, REL_SH) for k in range(NSH)], axis=1)
    convw_full = jnp.concatenate([convw_sh[2 * k].reshape(4, CONVW_SH) for k in range(NSH)], axis=1)
    cond16 = jnp.pad(c_all, ((0, 8), (0, 0)))
    b_part = lax.dynamic_slice_in_dim(b_ada, chip * ADA_SH, ADA_SH, axis=1)
    mods_part = _mods_part(cond16, w_ada[0], b_part)[:N_DEV]
    g2 = _allgather8("gather_mods", mods_part.reshape(N_DEV * ADA_SH // 128, 128))
    mods_all = jnp.concatenate([g2[2 * k].reshape(N_DEV, ADA_SH) for k in range(NSH)], axis=1)
    mods = lax.dynamic_slice_in_dim(mods_all, dev, 1, axis=0)

    loss, grad_x, dmods, small = _local_step(
        x[0], loss_target[0], mods, g_mix, rel_full, convw_full, conv_b, dt_bias, a_log, d_skip, g_att_out, g_ssd_out, g_ffn,
        g_final[None, :], exchange)

    small_names = ("g_mix", "conv_b", "dt_bias", "a_log", "d_skip", "g_att_out", "g_ssd_out", "g_ffn", "g_final", "rel_bias", "conv_w")
    g3 = _allgather8("gather_small_grads", _pack([dmods] + [small[n] for n in small_names], 264))
    sizes = [6 * D] + [int(np.prod(small[n].shape)) for n in small_names]
    dmods_all = _unpack(g3, sizes)[0]
    summed = _unpack(_sum8(g3), sizes)
    grads = {"b_ada": summed[0].reshape(1, 6 * D)}
    for n, val in zip(small_names, summed[1:]):
        grads[n] = val.reshape(small[n].shape)
    grads["rel_bias"] = lax.dynamic_slice_in_dim(grads["rel_bias"], chip * REL_SH, REL_SH, axis=1)
    grads["conv_w"] = lax.dynamic_slice_in_dim(grads["conv_w"], chip * CONVW_SH, CONVW_SH, axis=1)
    grads["g_final"] = grads["g_final"].reshape(D)
    dm16 = jnp.pad(lax.dynamic_slice_in_dim(dmods_all, chip * ADA_SH, ADA_SH, axis=1), ((0, 8), (0, 0)))
    grads["w_ada"] = _grad_w_ada(cond16, dm16)

    delta, new_m, new_v = {}, {}, {}
    delta["w_ada"], new_m["w_ada"], new_v["w_ada"] = _adamw("adamw_w_ada", w_ada[0], grads["w_ada"], m_w_ada[0], v_w_ada[0])
    grads.update(exchange.finish(grad_x))
    grads["w_in"] = grads["w_in"][:, :IN_SH]
    for n in BIG:
        delta[n], new_m[n], new_v[n] = _adamw("adamw_" + n, w[n][0], grads[n], m[n][0], v[n][0])
    sw = _pack([w[n] for n in SMALL], 200)
    sg = _pack([grads[n] for n in SMALL], 200)
    sm = _pack([m[n] for n in SMALL], 200)
    sv = _pack([v[n] for n in SMALL], 200)
    ssz = [int(np.prod(w[n].shape)) for n in SMALL]
    for dst, packed in zip((delta, new_m, new_v), _adamw("adamw_small", sw, sg, sm, sv)):
        for n, val in zip(SMALL, _unpack(packed, ssz)):
            dst[n] = val

    def shaped(d, n):
        return d[n].reshape(w[n].shape)

    total = lax.psum(loss, ("x", "y", "c"))
    return (total, grad_x[None], *[shaped(grads, n) for n in ORDER], *[shaped(delta, n) for n in ORDER],
            *[shaped(new_m, n) for n in ORDER], *[shaped(new_v, n) for n in ORDER])
```

```python
import functools

import numpy as np
import jax
import jax.numpy as jnp
from jax import lax
from jax.experimental import pallas as pl
from jax.experimental.pallas import tpu as pltpu

f32 = jnp.float32
bf16 = jnp.bfloat16
HIGHEST = lax.Precision.HIGHEST
MESH = pl.DeviceIdType.MESH

D = 2048
CHUNK = 64
LEFT = 8
BAND = (LEFT + 1) * CHUNK
BANDP = 640
PADK = LEFT * CHUNK
NH = 16
HD = 64
ATT_W = NH * HD
SSD_W = 1024
NG = 2
NSTATE = 128
GW = SSD_W // NG
XBC = SSD_W + 2 * NG * NSTATE
N_REL = 320
REL_CLIP = 256
FFN = 5632
NSH = 4
FSH = FFN // NSH
IN_COLS = 5648
IN_SH = IN_COLS // NSH
IN_SHP = 1536
IN_A = 3 * ATT_W
IN_B = 2688
IN_P = IN_A + IN_B
EPS = 1e-6
N_DEV = 8

ADAM_LR = 0.001
ADAM_B1 = 0.9
ADAM_B2 = 0.999
ADAM_EPS = 1e-08
ADAM_WD = 0.01
ADAM_STEP = 10

VMEM_LIMIT = 56 * 1024 * 1024


def _params(sem):
    return pltpu.CompilerParams(dimension_semantics=sem, vmem_limit_bytes=VMEM_LIMIT)


def _sds(shape, dtype):
    return jax.ShapeDtypeStruct(shape, dtype)


def _fold8(v):
    r, w = v.shape
    return jnp.sum(v.reshape(r // 8, 8, w), axis=0)


def _sigmoid(v):
    return 1.0 / (1.0 + jnp.exp(-v))


def _softplus(v):
    return jnp.maximum(v, 0.0) + jnp.log(1.0 + jnp.exp(-jnp.abs(v)))


def _dot(a, b, ta=False, tb=False):
    dn = (((0 if ta else 1,), (1 if tb else 0,)), ((), ()))
    return lax.dot_general(a.astype(bf16), b.astype(bf16), dn, preferred_element_type=f32)


def _dep_args(dep, ngrid):
    if dep is None:
        return [], []
    return [pl.BlockSpec((8, 128), lambda *_: (0, 0))], [dep]


def _matmul(name, a, b, *, grid, a_spec, b_spec, o_spec, o_shape, o_dtype, acc_shape, ta=False, tb=False, dep=None):
    nk = grid[2]
    dep_specs, dep_ops = _dep_args(dep, 3)

    def body(a_ref, b_ref, *rest):
        o_ref, acc_ref = rest[-2:]
        p = _dot(a_ref[...], b_ref[...], ta, tb)
        if nk == 1:
            o_ref[...] = p.astype(o_ref.dtype)
        else:
            k = pl.program_id(2)

            @pl.when(k == 0)
            def _():
                acc_ref[...] = p

            @pl.when(jnp.logical_and(k > 0, k < nk - 1))
            def _():
                acc_ref[...] += p

            @pl.when(k == nk - 1)
            def _():
                o_ref[...] = (acc_ref[...] + p).astype(o_ref.dtype)

    return pl.pallas_call(
        body, name=name, grid=grid, in_specs=[a_spec, b_spec] + dep_specs, out_specs=o_spec,
        out_shape=_sds(o_shape, o_dtype), scratch_shapes=[pltpu.VMEM(acc_shape if nk > 1 else (8, 128), f32)],
        compiler_params=_params(("parallel", "parallel", "arbitrary")),
    )(a, b, *dep_ops)


def _mm_nn_fullk(name, a, b, tm, tn, o_dtype, n=None):
    m, k = a.shape
    n = b.shape[1] if n is None else n
    return _matmul(name, a, b, grid=(m // tm, n // tn, 1),
                   a_spec=pl.BlockSpec((tm, k), lambda i, j, kk: (i, 0)),
                   b_spec=pl.BlockSpec((k, tn), lambda i, j, kk: (0, j)),
                   o_spec=pl.BlockSpec((tm, tn), lambda i, j, kk: (i, j)),
                   o_shape=(m, n), o_dtype=o_dtype, acc_shape=(tm, tn))


def _mm_nt(name, a, b, tm, tn, tk, o_dtype, dep=None):
    m, k = a.shape
    n = b.shape[0]
    return _matmul(name, a, b, grid=(m // tm, n // tn, k // tk), tb=True, dep=dep,
                   a_spec=pl.BlockSpec((tm, tk), lambda i, j, kk: (i, kk)),
                   b_spec=pl.BlockSpec((tn, tk), lambda i, j, kk: (j, kk)),
                   o_spec=pl.BlockSpec((tm, tn), lambda i, j, kk: (i, j)),
                   o_shape=(m, n), o_dtype=o_dtype, acc_shape=(tm, tn))


def _mm_tn(name, a, b, tm, tn, tk, o_dtype):
    k, m = a.shape
    n = b.shape[1]
    return _matmul(name, a, b, grid=(m // tm, n // tn, k // tk), ta=True,
                   a_spec=pl.BlockSpec((tk, tm), lambda i, j, kk: (kk, i)),
                   b_spec=pl.BlockSpec((tk, tn), lambda i, j, kk: (kk, j)),
                   o_spec=pl.BlockSpec((tm, tn), lambda i, j, kk: (i, j)),
                   o_shape=(m, n), o_dtype=o_dtype, acc_shape=(tm, tn))


def _ffn_up(h2b, wg4, wu4, tm):
    s = h2b.shape[0]

    def body(h_ref, wg_ref, wu_ref, g_ref, u_ref, a_ref):
        h = h_ref[...]
        g = _dot(h, wg_ref[...])
        u = _dot(h, wu_ref[...])
        g_ref[...] = g
        u_ref[...] = u
        a_ref[...] = (g * _sigmoid(g) * u).astype(bf16)

    wspec = pl.BlockSpec((None, D, FSH), lambda k, i: (k, 0, 0))
    ospec = pl.BlockSpec((tm, FSH), lambda k, i: (i, k))
    return pl.pallas_call(
        body, name="ffn_up", grid=(NSH, s // tm),
        in_specs=[pl.BlockSpec((tm, D), lambda k, i: (i, 0)), wspec, wspec],
        out_specs=[ospec, ospec, ospec],
        out_shape=[_sds((s, FFN), f32), _sds((s, FFN), f32), _sds((s, FFN), bf16)],
        compiler_params=_params(("parallel", "parallel")),
    )(h2b, wg4, wu4)


def _ffn_down(act, wd4, tm):
    s = act.shape[0]
    return _matmul("ffn_down", act, wd4, grid=(s // tm, 1, NSH),
                   a_spec=pl.BlockSpec((tm, FSH), lambda i, j, k: (i, k)),
                   b_spec=pl.BlockSpec((None, FSH, D), lambda i, j, k: (k, 0, 0)),
                   o_spec=pl.BlockSpec((tm, D), lambda i, j, k: (i, 0)),
                   o_shape=(s, D), o_dtype=f32, acc_shape=(tm, D))


def _ffn_dact(dffn, wd4, gate, up, tm, dep=None):
    s = dffn.shape[0]
    dep_specs, dep_ops = _dep_args(dep, 2)

    def body(d_ref, w_ref, g_ref, u_ref, *rest):
        dg_ref, du_ref = rest[-2:]
        dact = _dot(d_ref[...], w_ref[...], tb=True)
        g = g_ref[...]
        sg = _sigmoid(g)
        dg_ref[...] = (dact * u_ref[...] * (sg * (1.0 + g * (1.0 - sg)))).astype(bf16)
        du_ref[...] = (dact * (g * sg)).astype(bf16)

    blk = pl.BlockSpec((tm, FSH), lambda k, i: (i, k))
    return pl.pallas_call(
        body, name="ffn_dact", grid=(NSH, s // tm),
        in_specs=[pl.BlockSpec((tm, D), lambda k, i: (i, 0)), pl.BlockSpec((None, FSH, D), lambda k, i: (k, 0, 0)), blk, blk] + dep_specs,
        out_specs=[blk, blk], out_shape=[_sds((s, FFN), bf16), _sds((s, FFN), bf16)],
        compiler_params=_params(("parallel", "parallel")),
    )(dffn, wd4, gate, up, *dep_ops)


def _ffn_dh(dgate, dup, wg4, wu4, tm, dep=None):
    s = dgate.shape[0]
    dep_specs, dep_ops = _dep_args(dep, 2)

    def body(dg_ref, du_ref, wg_ref, wu_ref, *rest):
        o_ref, acc_ref = rest[-2:]
        k = pl.program_id(1)
        p = _dot(dg_ref[...], wg_ref[...], tb=True) + _dot(du_ref[...], wu_ref[...], tb=True)

        @pl.when(k == 0)
        def _():
            acc_ref[...] = p

        @pl.when(jnp.logical_and(k > 0, k < NSH - 1))
        def _():
            acc_ref[...] += p

        @pl.when(k == NSH - 1)
        def _():
            o_ref[...] = acc_ref[...] + p

    aspec = pl.BlockSpec((tm, FSH), lambda i, k: (i, k))
    wspec = pl.BlockSpec((None, D, FSH), lambda i, k: (k, 0, 0))
    return pl.pallas_call(
        body, name="ffn_dh", grid=(s // tm, NSH), in_specs=[aspec, aspec, wspec, wspec] + dep_specs,
        out_specs=pl.BlockSpec((tm, D), lambda i, k: (i, 0)), out_shape=_sds((s, D), f32),
        scratch_shapes=[pltpu.VMEM((tm, D), f32)], compiler_params=_params(("parallel", "arbitrary")),
    )(dgate, dup, wg4, wu4, *dep_ops)


def _grad_cols4(name, h, dy, tm, tk):
    s = h.shape[0]
    return _matmul(name, h, dy, grid=(NSH, D // tm, s // tk), ta=True,
                   a_spec=pl.BlockSpec((tk, tm), lambda k, i, kk: (kk, i)),
                   b_spec=pl.BlockSpec((tk, FSH), lambda k, i, kk: (kk, k)),
                   o_spec=pl.BlockSpec((None, tm, FSH), lambda k, i, kk: (k, i, 0)),
                   o_shape=(NSH, D, FSH), o_dtype=bf16, acc_shape=(tm, FSH))


def _grad_wdown4(act, dffn, tn, tk):
    s = act.shape[0]
    return _matmul("grad_w_down", act, dffn, grid=(NSH, D // tn, s // tk), ta=True,
                   a_spec=pl.BlockSpec((tk, FSH), lambda k, j, kk: (kk, k)),
                   b_spec=pl.BlockSpec((tk, tn), lambda k, j, kk: (kk, j)),
                   o_spec=pl.BlockSpec((None, FSH, tn), lambda k, j, kk: (k, 0, j)),
                   o_shape=(NSH, FSH, D), o_dtype=bf16, acc_shape=(FSH, tn))


def _row_spec(w):
    return pl.BlockSpec((1, w), lambda i: (0, 0))


def _tile_spec(tm, w, col=0):
    return pl.BlockSpec((tm, w), lambda i: (i, col))


def _norm_mod(name, x, g, sc, sh, tm):
    s = x.shape[0]

    def body(x_ref, g_ref, sc_ref, sh_ref, o_ref):
        xv = x_ref[...]
        r = lax.rsqrt(jnp.mean(xv * xv, axis=-1, keepdims=True) + EPS)
        o_ref[...] = (xv * r * g_ref[...] * (1.0 + sc_ref[...]) + sh_ref[...]).astype(bf16)

    return pl.pallas_call(
        body, name=name, grid=(s // tm,), in_specs=[_tile_spec(tm, D), _row_spec(D), _row_spec(D), _row_spec(D)],
        out_specs=_tile_spec(tm, D), out_shape=_sds((s, D), bf16), compiler_params=_params(("parallel",)),
    )(x, g, sc, sh)


def _resid_norm_mod(x, gt, mix, g, sc, sh, tm):
    s = x.shape[0]

    def body(x_ref, gt_ref, m_ref, g_ref, sc_ref, sh_ref, x2_ref, h_ref):
        xv = x_ref[...] + gt_ref[...] * m_ref[...]
        x2_ref[...] = xv
        r = lax.rsqrt(jnp.mean(xv * xv, axis=-1, keepdims=True) + EPS)
        h_ref[...] = (xv * r * g_ref[...] * (1.0 + sc_ref[...]) + sh_ref[...]).astype(bf16)

    return pl.pallas_call(
        body, name="resid_norm_mod", grid=(s // tm,),
        in_specs=[_tile_spec(tm, D), _row_spec(D), _tile_spec(tm, D), _row_spec(D), _row_spec(D), _row_spec(D)],
        out_specs=[_tile_spec(tm, D), _tile_spec(tm, D)], out_shape=[_sds((s, D), f32), _sds((s, D), bf16)],
        compiler_params=_params(("parallel",)),
    )(x, gt, mix, g, sc, sh)


def _final_fwd_bwd(x2, ffn, gt2, g, tgt, tm):
    s = x2.shape[0]
    n = s // tm

    def body(x_ref, f_ref, gt_ref, g_ref, t_ref, dx_ref, df_ref, loss_ref, dg_ref, dgt_ref, a_loss, a_dg, a_dgt):
        i = pl.program_id(0)

        @pl.when(i == 0)
        def _():
            a_loss[...] = jnp.zeros_like(a_loss)
            a_dg[...] = jnp.zeros_like(a_dg)
            a_dgt[...] = jnp.zeros_like(a_dgt)

        fv = f_ref[...]
        gt = gt_ref[...]
        gv = g_ref[...]
        xv = x_ref[...] + gt * fv
        r = lax.rsqrt(jnp.mean(xv * xv, axis=-1, keepdims=True) + EPS)
        xh = xv * r
        e = xh * gv - t_ref[...]
        a_loss[...] += _fold8(e * e)
        dy = e * (1.0 / D)
        a_dg[...] += _fold8(dy * xh)
        t = dy * gv
        dx = r * (t - xh * jnp.mean(t * xh, axis=-1, keepdims=True))
        dx_ref[...] = dx
        a_dgt[...] += _fold8(dx * fv)
        df_ref[...] = (dx * gt).astype(bf16)

        @pl.when(i == n - 1)
        def _():
            tot = jnp.sum(jnp.sum(a_loss[...], axis=0, keepdims=True), axis=1, keepdims=True) * (0.5 / D)
            loss_ref[...] = jnp.broadcast_to(tot, (1, 128))
            dg_ref[...] = jnp.sum(a_dg[...], axis=0, keepdims=True)
            dgt_ref[...] = jnp.sum(a_dgt[...], axis=0, keepdims=True)

    return pl.pallas_call(
        body, name="final_fwd_bwd", grid=(n,),
        in_specs=[_tile_spec(tm, D), _tile_spec(tm, D), _row_spec(D), _row_spec(D), _tile_spec(tm, D)],
        out_specs=[_tile_spec(tm, D), _tile_spec(tm, D), _row_spec(128), _row_spec(D), _row_spec(D)],
        out_shape=[_sds((s, D), f32), _sds((s, D), bf16), _sds((1, 128), f32), _sds((1, D), f32), _sds((1, D), f32)],
        scratch_shapes=[pltpu.VMEM((8, D), f32)] * 3, compiler_params=_params(("arbitrary",)),
    )(x2, ffn, gt2, g, tgt)


def _norm_mod_bwd(name, dh, xin, g, sc, dres, tm, mix=None, gt=None):
    s = dh.shape[0]
    n = s // tm
    with_mix = mix is not None

    def body(*refs):
        if with_mix:
            dh_ref, x_ref, g_ref, sc_ref, dr_ref, m_ref, gt_ref, dx_ref, dm_ref, dsc_ref, dsh_ref, dg_ref, dgt_ref, a_sc, a_sh, a_g, a_gt = refs
        else:
            dh_ref, x_ref, g_ref, sc_ref, dr_ref, dx_ref, dsc_ref, dsh_ref, dg_ref, a_sc, a_sh, a_g = refs
        i = pl.program_id(0)

        @pl.when(i == 0)
        def _():
            a_sc[...] = jnp.zeros_like(a_sc)
            a_sh[...] = jnp.zeros_like(a_sh)
            a_g[...] = jnp.zeros_like(a_g)
            if with_mix:
                a_gt[...] = jnp.zeros_like(a_gt)

        dh = dh_ref[...]
        xv = x_ref[...]
        gv = g_ref[...]
        r = lax.rsqrt(jnp.mean(xv * xv, axis=-1, keepdims=True) + EPS)
        xh = xv * r
        a_sc[...] += _fold8(dh * xh * gv)
        a_sh[...] += _fold8(dh)
        dn = dh * (1.0 + sc_ref[...])
        a_g[...] += _fold8(dn * xh)
        t = dn * gv
        dx = dr_ref[...] + r * (t - xh * jnp.mean(t * xh, axis=-1, keepdims=True))
        dx_ref[...] = dx
        if with_mix:
            a_gt[...] += _fold8(dx * m_ref[...])
            dm_ref[...] = (dx * gt_ref[...]).astype(bf16)

        @pl.when(i == n - 1)
        def _():
            dsc_ref[...] = jnp.sum(a_sc[...], axis=0, keepdims=True)
            dsh_ref[...] = jnp.sum(a_sh[...], axis=0, keepdims=True)
            dg_ref[...] = jnp.sum(a_g[...], axis=0, keepdims=True)
            if with_mix:
                dgt_ref[...] = jnp.sum(a_gt[...], axis=0, keepdims=True)

    tile, row = _tile_spec(tm, D), _row_spec(D)
    if with_mix:
        ins, args = [tile, tile, row, row, tile, tile, row], (dh, xin, g, sc, dres, mix, gt)
        outs = [tile, tile, row, row, row, row]
        shapes = [_sds((s, D), f32), _sds((s, D), bf16)] + [_sds((1, D), f32)] * 4
        nacc = 4
    else:
        ins, args = [tile, tile, row, row, tile], (dh, xin, g, sc, dres)
        outs = [tile, row, row, row]
        shapes = [_sds((s, D), f32)] + [_sds((1, D), f32)] * 3
        nacc = 3
    return pl.pallas_call(
        body, name=name, grid=(n,), in_specs=ins, out_specs=outs, out_shape=shapes,
        scratch_shapes=[pltpu.VMEM((8, D), f32)] * nacc, compiler_params=_params(("arbitrary",)),
    )(*args)


def _mix_pre(att, y, proj2, g_att, g_ssd, tm):
    s = att.shape[0]

    def body(a_ref, y_ref, z_ref, ga_ref, gs_ref, o_ref):
        a = a_ref[...]
        ra = lax.rsqrt(jnp.mean(a * a, axis=-1, keepdims=True) + EPS)
        o_ref[:, 0:ATT_W] = (a * ra * ga_ref[...]).astype(bf16)
        z = z_ref[...]
        u = y_ref[...] * (z * _sigmoid(z))
        ru = lax.rsqrt(jnp.mean(u * u, axis=-1, keepdims=True) + EPS)
        o_ref[:, ATT_W:] = (u * ru * gs_ref[...]).astype(bf16)

    t = _tile_spec(tm, ATT_W)
    return pl.pallas_call(
        body, name="mix_pre", grid=(s // tm,), in_specs=[t, t, t, _row_spec(ATT_W), _row_spec(SSD_W)],
        out_specs=_tile_spec(tm, D), out_shape=_sds((s, D), bf16), compiler_params=_params(("parallel",)),
    )(att, y, proj2, g_att, g_ssd)


def _mix_pre_bwd(dmc, att, y, proj2, g_att, g_ssd, tm):
    s = att.shape[0]
    n = s // tm

    def body(da_ref, ds_ref, a_ref, y_ref, z_ref, ga_ref, gs_ref, datt_ref, dy_ref, dz_ref, dga_ref, dgs_ref, acc_a, acc_s):
        i = pl.program_id(0)

        @pl.when(i == 0)
        def _():
            acc_a[...] = jnp.zeros_like(acc_a)
            acc_s[...] = jnp.zeros_like(acc_s)

        a = a_ref[...]
        ra = lax.rsqrt(jnp.mean(a * a, axis=-1, keepdims=True) + EPS)
        ah = a * ra
        dan = da_ref[...]
        acc_a[...] += _fold8(dan * ah)
        t = dan * ga_ref[...]
        datt_ref[...] = (ra * (t - ah * jnp.mean(t * ah, axis=-1, keepdims=True))).astype(bf16)
        z = z_ref[...]
        yv = y_ref[...]
        sz = _sigmoid(z)
        sil = z * sz
        u = yv * sil
        ru = lax.rsqrt(jnp.mean(u * u, axis=-1, keepdims=True) + EPS)
        uh = u * ru
        dsn = ds_ref[...]
        acc_s[...] += _fold8(dsn * uh)
        t2 = dsn * gs_ref[...]
        du = ru * (t2 - uh * jnp.mean(t2 * uh, axis=-1, keepdims=True))
        dy_ref[...] = du * sil
        dz_ref[...] = (du * yv * (sz * (1.0 + z * (1.0 - sz)))).astype(bf16)

        @pl.when(i == n - 1)
        def _():
            dga_ref[...] = jnp.sum(acc_a[...], axis=0, keepdims=True)
            dgs_ref[...] = jnp.sum(acc_s[...], axis=0, keepdims=True)

    t = _tile_spec(tm, ATT_W)
    row = _row_spec(ATT_W)
    return pl.pallas_call(
        body, name="mix_pre_bwd", grid=(n,),
        in_specs=[_tile_spec(tm, ATT_W, 0), _tile_spec(tm, ATT_W, 1), t, t, t, row, row],
        out_specs=[t, t, t, row, row],
        out_shape=[_sds((s, ATT_W), bf16), _sds((s, SSD_W), f32), _sds((s, SSD_W), bf16), _sds((1, ATT_W), f32), _sds((1, SSD_W), f32)],
        scratch_shapes=[pltpu.VMEM((8, ATT_W), f32)] * 2, compiler_params=_params(("arbitrary",)),
    )(dmc, dmc, att, y, proj2, g_att, g_ssd)


ATT_GROUP = 2


def _pair_rows(qc):
    two = jnp.concatenate([qc, qc], axis=0)
    r = lax.broadcasted_iota(jnp.int32, (2 * CHUNK, 128), 0)
    l = lax.broadcasted_iota(jnp.int32, (2 * CHUNK, 128), 1)
    return jnp.where((r < CHUNK) == (l < HD), two, jnp.zeros_like(two))


def _pair_scores(wt, kb, bias, r0):
    sc = lax.dot_general(wt, kb, (((1,), (1,)), ((), ())), preferred_element_type=f32) * (HD ** -0.5) + bias
    kidx = lax.broadcasted_iota(jnp.int32, sc.shape, 1)
    return jnp.where(r0 + kidx >= PADK, sc, -jnp.inf)


def _softmax_lanes(sc):
    e = jnp.exp(sc - jnp.max(sc, axis=-1, keepdims=True))
    return e / jnp.sum(e, axis=-1, keepdims=True)


def _pair_diag(r):
    lane = lax.broadcasted_iota(jnp.int32, (CHUNK, 128), 1)
    return jnp.where(lane < HD, r[0:CHUNK], r[CHUNK:])


def _pad_keys(k_ref, kp, s):
    kp[0:PADK, :] = jnp.zeros((PADK, 128), bf16)
    kp[PADK:PADK + s, :] = k_ref[...]
    kp[PADK + s:, :] = jnp.zeros((CHUNK, 128), bf16)


def _attn_fwd(qkv, bias2):
    s = qkv.shape[0]
    nc = s // CHUNK
    npair = NH // 2

    def body(q_ref, k_ref, v_ref, b_ref, o_ref, kp, vp):
        _pad_keys(k_ref, kp, s)
        _pad_keys(v_ref, vp, s)

        def group(g, carry):
            r0s = [pl.multiple_of((g * ATT_GROUP + u) * CHUNK, CHUNK) for u in range(ATT_GROUP)]
            scs = [_pair_scores(_pair_rows(q_ref[pl.ds(r0, CHUNK), :]), kp[pl.ds(r0, BANDP), :], b_ref[...], r0) for r0 in r0s]
            ps = [_softmax_lanes(sc).astype(bf16) for sc in scs]
            for r0, p in zip(r0s, ps):
                o_ref[pl.ds(r0, CHUNK), :] = _pair_diag(jnp.dot(p, vp[pl.ds(r0, BANDP), :], preferred_element_type=f32))
            return carry

        lax.fori_loop(0, nc // ATT_GROUP, group, 0)

    return pl.pallas_call(
        body, name="attn_fwd", grid=(npair,),
        in_specs=[pl.BlockSpec((s, 128), lambda p: (0, p)), pl.BlockSpec((s, 128), lambda p: (0, npair + p)),
                  pl.BlockSpec((s, 128), lambda p: (0, 2 * npair + p)), pl.BlockSpec((None, 2 * CHUNK, BANDP), lambda p: (p, 0, 0))],
        out_specs=pl.BlockSpec((s, 128), lambda p: (0, p)), out_shape=_sds((s, ATT_W), f32),
        scratch_shapes=[pltpu.VMEM((PADK + s + CHUNK, 128), bf16)] * 2, compiler_params=_params(("parallel",)),
    )(qkv, qkv, qkv, bias2)


def _attn_bwd(qkv, datt, bias2, bias2t):
    s = qkv.shape[0]
    nc = s // CHUNK
    npair = NH // 2
    rows = PADK + s + CHUNK
    nt = (((1,), (1,)), ((), ()))

    def body(q_ref, k_ref, v_ref, do_ref, b_ref, bt_ref, dq_ref, dk_ref, dv_ref, g_ref, kp, vp, dkp, dvp):
        _pad_keys(k_ref, kp, s)
        _pad_keys(v_ref, vp, s)
        dkp[...] = jnp.zeros_like(dkp)
        dvp[...] = jnp.zeros_like(dvp)
        g_ref[...] = jnp.zeros_like(g_ref)

        def group(g, carry):
            r0s = [pl.multiple_of((g * ATT_GROUP + u) * CHUNK, CHUNK) for u in range(ATT_GROUP)]
            wts = [_pair_rows(q_ref[pl.ds(r0, CHUNK), :]) for r0 in r0s]
            dos = [_pair_rows(do_ref[pl.ds(r0, CHUNK), :]) for r0 in r0s]
            scs = [_pair_scores(wt, kp[pl.ds(r0, BANDP), :], b_ref[...], r0) for wt, r0 in zip(wts, r0s)]
            dps = [lax.dot_general(do, vp[pl.ds(r0, BANDP), :], nt, preferred_element_type=f32) for do, r0 in zip(dos, r0s)]
            scts, dpts = [], []
            for wt, do, r0 in zip(wts, dos, r0s):
                sct = lax.dot_general(kp[pl.ds(r0, BANDP), :], wt, nt, preferred_element_type=f32) * (HD ** -0.5) + bt_ref[...]
                kidx = lax.broadcasted_iota(jnp.int32, sct.shape, 0)
                scts.append(jnp.where(r0 + kidx >= PADK, sct, -jnp.inf))
                dpts.append(lax.dot_general(vp[pl.ds(r0, BANDP), :], do, nt, preferred_element_type=f32))
            for r0, sc, dp in zip(r0s, scs, dps):
                p = _softmax_lanes(sc)
                ds = p * (dp - jnp.sum(p * dp, axis=-1, keepdims=True))
                g_ref[...] += ds
                dq = jnp.dot(ds.astype(bf16), kp[pl.ds(r0, BANDP), :], preferred_element_type=f32)
                dq_ref[pl.ds(r0, CHUNK), :] = (_pair_diag(dq) * (HD ** -0.5)).astype(bf16)
            for r0, wt, do, sct, dpt in zip(r0s, wts, dos, scts, dpts):
                e = jnp.exp(sct - jnp.max(sct, axis=0, keepdims=True))
                pt = e / jnp.sum(e, axis=0, keepdims=True)
                dst = pt * (dpt - jnp.sum(pt * dpt, axis=0, keepdims=True))
                dkp[pl.ds(r0, BANDP), :] += jnp.dot(dst.astype(bf16), wt, preferred_element_type=f32) * (HD ** -0.5)
                dvp[pl.ds(r0, BANDP), :] += jnp.dot(pt.astype(bf16), do, preferred_element_type=f32)
            return carry

        lax.fori_loop(0, nc // ATT_GROUP, group, 0)
        dk_ref[...] = dkp[PADK:PADK + s, :].astype(bf16)
        dv_ref[...] = dvp[PADK:PADK + s, :].astype(bf16)

    col = lambda off: pl.BlockSpec((s, 128), lambda p: (0, off + p))
    return pl.pallas_call(
        body, name="attn_bwd", grid=(npair,),
        in_specs=[col(0), col(npair), col(2 * npair), col(0), pl.BlockSpec((None, 2 * CHUNK, BANDP), lambda p: (p, 0, 0)),
                  pl.BlockSpec((None, BANDP, 2 * CHUNK), lambda p: (p, 0, 0))],
        out_specs=[col(0), col(0), col(0), pl.BlockSpec((None, 2 * CHUNK, BANDP), lambda p: (p, 0, 0))],
        out_shape=[_sds((s, ATT_W), bf16)] * 3 + [_sds((npair, 2 * CHUNK, BANDP), f32)],
        scratch_shapes=[pltpu.VMEM((rows, 128), bf16)] * 2 + [pltpu.VMEM((rows, 128), f32)] * 2,
        compiler_params=_params(("parallel",)),
    )(qkv, qkv, qkv, datt, bias2, bias2t)


def _rel_tables():
    onehot = np.zeros((BANDP, N_REL), np.float32)
    for j in range(BAND + CHUNK - 1):
        o = j - (CHUNK - 1)
        onehot[j, int(np.clip(PADK - o, -(CHUNK - 1), REL_CLIP)) + CHUNK - 1] = 1.0
    return onehot, np.ascontiguousarray(np.eye(CHUNK, dtype=np.float32)[::-1])


def _expand_bias(rel):
    ext = jnp.concatenate([jnp.broadcast_to(rel[:, N_REL - 1:], (NH, N_REL - 1)), rel[:, ::-1],
                           jnp.zeros((NH, BANDP - BAND + 1), f32)], axis=1)
    band = jnp.stack([ext[:, CHUNK - 1 - q:CHUNK - 1 - q + BANDP] for q in range(CHUNK)], axis=1)
    band = jnp.where(np.arange(BANDP) < BAND, band, -jnp.inf)
    return band.reshape(NH // 2, 2 * CHUNK, BANDP)


def _rel_bias_grad(gband):
    def body(g_ref, m_ref, flip_ref, o_ref, d2):
        for h in range(NH):
            rev = jnp.dot(flip_ref[...], g_ref[h], precision=HIGHEST, preferred_element_type=f32)
            rolled = pltpu.roll(rev, 0, 1, stride=1, stride_axis=0)
            d2[h:h + 1, :] = jnp.sum(rolled, axis=0, keepdims=True)
        o_ref[...] = jnp.dot(d2[...], m_ref[...], precision=HIGHEST, preferred_element_type=f32)

    onehot, flip = _rel_tables()
    return pl.pallas_call(
        body, name="rel_bias_grad", out_shape=_sds((NH, N_REL), f32), scratch_shapes=[pltpu.VMEM((NH, BANDP), f32)],
    )(gband, jnp.asarray(onehot), jnp.asarray(flip))


XBC_BLK = 512
XBC_COL0 = SSD_W // XBC_BLK
DT_COL = (SSD_W + XBC) // 128


def _conv_taps(ext, w_ref, b_ref, tm):
    n = ext.shape[0]
    pre = w_ref[3:4, :] * ext + b_ref[...]
    for j in range(3):
        pre = pre + w_ref[j:j + 1, :] * pltpu.roll(ext, 3 - j, 0)
    return pre


def _ssd_conv(proj2, conv_w, conv_b, tm):
    s = proj2.shape[0]
    nb = XBC // XBC_BLK

    def body(x_ref, p_ref, w_ref, b_ref, o_ref):
        i = pl.program_id(1)
        prev = jnp.where(i > 0, p_ref[...], 0.0)
        ext = jnp.concatenate([prev, x_ref[...]], axis=0)
        pre = _conv_taps(ext, w_ref, b_ref, tm)[8:8 + tm]
        o_ref[...] = pre * _sigmoid(pre)

    return pl.pallas_call(
        body, name="ssd_conv", grid=(nb, s // tm),
        in_specs=[pl.BlockSpec((tm, XBC_BLK), lambda j, i: (i, XBC_COL0 + j)),
                  pl.BlockSpec((8, XBC_BLK), lambda j, i: (jnp.maximum(i * (tm // 8) - 1, 0), XBC_COL0 + j)),
                  pl.BlockSpec((4, XBC_BLK), lambda j, i: (0, j)), pl.BlockSpec((1, XBC_BLK), lambda j, i: (0, j))],
        out_specs=pl.BlockSpec((tm, XBC_BLK), lambda j, i: (i, j)), out_shape=_sds((s, XBC), f32),
        compiler_params=_params(("parallel", "parallel")),
    )(proj2, proj2, conv_w, conv_b)


def _ssd_conv_bwd(dxbc, proj2, conv_w, conv_b, tm):
    s = proj2.shape[0]
    nb = XBC // XBC_BLK
    n = s // tm
    last8 = s // 8 - 1

    def body(x_ref, xp_ref, xn_ref, d_ref, dn_ref, w_ref, b_ref, o_ref, dw_ref, db_ref):
        i = pl.program_id(1)

        @pl.when(i == 0)
        def _():
            dw_ref[...] = jnp.zeros_like(dw_ref)
            db_ref[...] = jnp.zeros_like(db_ref)

        prev = jnp.where(i > 0, xp_ref[...], 0.0)
        ext = jnp.concatenate([prev, x_ref[...], xn_ref[...]], axis=0)
        pre = _conv_taps(ext, w_ref, b_ref, tm)
        sg = _sigmoid(pre)
        dnext = jnp.where(i < n - 1, dn_ref[...], 0.0)
        dext = jnp.concatenate([jnp.zeros((8, XBC_BLK), f32), d_ref[...], dnext], axis=0)
        dpre = dext * (sg * (1.0 + pre * (1.0 - sg)))
        rows = tm + 16
        dx = w_ref[3:4, :] * dpre
        for j in range(3):
            dx = dx + w_ref[j:j + 1, :] * pltpu.roll(dpre, rows - (3 - j), 0)
        o_ref[...] = dx[8:8 + tm].astype(bf16)
        dcur = dpre[8:8 + tm]
        db_ref[...] += jnp.sum(dcur, axis=0, keepdims=True)
        dw_ref[3:4, :] += jnp.sum(dcur * ext[8:8 + tm], axis=0, keepdims=True)
        for j in range(3):
            dw_ref[j:j + 1, :] += jnp.sum(dcur * pltpu.roll(ext, 3 - j, 0)[8:8 + tm], axis=0, keepdims=True)

    xcol = lambda j: XBC_COL0 + j
    return pl.pallas_call(
        body, name="ssd_conv_bwd", grid=(nb, n),
        in_specs=[pl.BlockSpec((tm, XBC_BLK), lambda j, i: (i, xcol(j))),
                  pl.BlockSpec((8, XBC_BLK), lambda j, i: (jnp.maximum(i * (tm // 8) - 1, 0), xcol(j))),
                  pl.BlockSpec((8, XBC_BLK), lambda j, i: (jnp.minimum((i + 1) * (tm // 8), last8), xcol(j))),
                  pl.BlockSpec((tm, XBC_BLK), lambda j, i: (i, j)),
                  pl.BlockSpec((8, XBC_BLK), lambda j, i: (jnp.minimum((i + 1) * (tm // 8), last8), j)),
                  pl.BlockSpec((4, XBC_BLK), lambda j, i: (0, j)), pl.BlockSpec((1, XBC_BLK), lambda j, i: (0, j))],
        out_specs=[pl.BlockSpec((tm, XBC_BLK), lambda j, i: (i, j)), pl.BlockSpec((4, XBC_BLK), lambda j, i: (0, j)),
                   pl.BlockSpec((1, XBC_BLK), lambda j, i: (0, j))],
        out_shape=[_sds((s, XBC), bf16), _sds((4, XBC), f32), _sds((1, XBC), f32)],
        compiler_params=_params(("parallel", "arbitrary")),
    )(proj2, proj2, proj2, dxbc, dxbc, conv_w, conv_b)


def _ssd_consts():
    ex = np.zeros((128, SSD_W), np.float32)
    for h in range(NH):
        ex[h, h * HD:(h + 1) * HD] = 1.0
    sel = np.zeros((8, 128), np.float32)
    for h in range(NH):
        sel[h // 2, h] = 1.0
    par = np.zeros((128, 128), np.float32)
    for r in range(128):
        for h in range(NH):
            par[r, h] = 1.0 if (h % 2) == (r // 64) else 0.0
    ones_blk = np.zeros((128, 128), np.float32)
    for r in range(128):
        ones_blk[r, (r // 64) * 64:(r // 64) * 64 + 64] = 1.0
    return ex, np.ascontiguousarray(ex.T), sel, par, ones_blk


def _ssd_common(xbc_ref, dtr_ref, a_ref, dtb_ref, ex_ref, sel_ref, par_ref):
    xs = xbc_ref[:, 0:SSD_W]
    dt = _softplus(dtr_ref[...] + dtb_ref[...])
    adt = dt * a_ref[...]
    r_i = lax.broadcasted_iota(jnp.int32, (CHUNK, CHUNK), 0)
    c_i = lax.broadcasted_iota(jnp.int32, (CHUNK, CHUNK), 1)
    tril = (r_i >= c_i).astype(f32)
    cs = jnp.dot(tril, adt, precision=HIGHEST, preferred_element_type=f32)
    cs2 = jnp.concatenate([cs, cs], axis=0) * par_ref[...]
    cstp = lax.dot_general(sel_ref[...], cs2, (((1,), (1,)), ((), ())), precision=HIGHEST, preferred_element_type=f32)
    ex = ex_ref[...]
    dt_full = jnp.dot(dt, ex, precision=HIGHEST, preferred_element_type=f32)
    cs_full = jnp.dot(cs, ex, precision=HIGHEST, preferred_element_type=f32)
    return xs, dt, cs, cstp, dt_full, cs_full


def _pair_mask():
    l_i = lax.broadcasted_iota(jnp.int32, (CHUNK, 128), 0)
    lane = lax.broadcasted_iota(jnp.int32, (CHUNK, 128), 1)
    return l_i >= (lane % CHUNK), lane < HD


def _block_diag(xp, first):
    z = jnp.zeros_like(xp)
    return jnp.concatenate([jnp.where(first, xp, z), jnp.where(first, z, xp)], axis=0)


def _ssd_fwd(xbc, proj2, a_row, dtb_row, dsk_full):
    s = xbc.shape[0]
    nc = s // CHUNK
    ex, ext, sel, par, ones_blk = _ssd_consts()

    def body(xbc_ref, dtr_ref, a_ref, dtb_ref, dsk_ref, ex_ref, sel_ref, par_ref, y_ref, hs_ref, hst):
        @pl.when(pl.program_id(0) == 0)
        def _():
            hst[...] = jnp.zeros_like(hst)

        hs_ref[...] = hst[...]
        xs, dt, cs, cstp, dt_full, cs_full = _ssd_common(xbc_ref, dtr_ref, a_ref, dtb_ref, ex_ref, sel_ref, par_ref)
        cs_last = cs_full[CHUNK - 1:CHUNK, :]
        xdt = xs * dt_full
        causal, first = _pair_mask()
        for g in range(NG):
            gl = slice(g * GW, (g + 1) * GW)
            bg = xbc_ref[:, SSD_W + g * NSTATE:SSD_W + (g + 1) * NSTATE].astype(bf16)
            cg = xbc_ref[:, SSD_W + NG * NSTATE + g * NSTATE:SSD_W + NG * NSTATE + (g + 1) * NSTATE].astype(bf16)
            cb2 = lax.dot_general(cg, jnp.concatenate([bg, bg], axis=0), (((1,), (1,)), ((), ())), preferred_element_type=f32)
            hg = hst[g]
            y0 = jnp.dot(cg, hg.astype(bf16), preferred_element_type=f32)
            yoff = jnp.exp(cs_full[:, gl]) * y0
            for j in range(GW // 128):
                pair = g * (GW // 128) + j
                pl_ = slice(pair * 128, (pair + 1) * 128)
                seg = jnp.exp(jnp.where(causal, cs_full[:, pl_] - cstp[pair:pair + 1, :], -jnp.inf))
                m = (cb2 * seg).astype(bf16)
                yd = jnp.dot(m, _block_diag(xdt[:, pl_].astype(bf16), first), preferred_element_type=f32)
                y_ref[:, pl_] = yd + yoff[:, j * 128:(j + 1) * 128] + xs[:, pl_] * dsk_ref[:, pl_]
            xdec = (xdt[:, gl] * jnp.exp(cs_last[:, gl] - cs_full[:, gl])).astype(bf16)
            st = lax.dot_general(bg, xdec, (((0,), (0,)), ((), ())), preferred_element_type=f32)
            hst[g] = jnp.exp(cs_last[:, gl]) * hg + st

    const = lambda shape: pl.BlockSpec(shape, lambda c: tuple(0 for _ in shape))
    return pl.pallas_call(
        body, name="ssd_fwd", grid=(nc,),
        in_specs=[pl.BlockSpec((CHUNK, XBC), lambda c: (c, 0)), pl.BlockSpec((CHUNK, 128), lambda c: (c, DT_COL)),
                  const((1, 128)), const((1, 128)), const((1, SSD_W)), const((128, SSD_W)), const((8, 128)), const((128, 128))],
        out_specs=[pl.BlockSpec((CHUNK, SSD_W), lambda c: (c, 0)), pl.BlockSpec((None, NG, NSTATE, GW), lambda c: (c, 0, 0, 0))],
        out_shape=[_sds((s, SSD_W), f32), _sds((nc, NG, NSTATE, GW), f32)],
        scratch_shapes=[pltpu.VMEM((NG, NSTATE, GW), f32)], compiler_params=_params(("arbitrary",)),
    )(xbc, proj2, a_row, dtb_row, dsk_full, jnp.asarray(ex), jnp.asarray(sel), jnp.asarray(par))


def _ssd_bwd(xbc, proj2, dy, hsave, a_row, dtb_row, dsk_full):
    s = xbc.shape[0]
    nc = s // CHUNK
    ex, ext, sel, par, ones_blk = _ssd_consts()

    def body(xbc_ref, dtr_ref, dy_ref, hs_ref, a_ref, dtb_ref, dsk_ref, ex_ref, ext_ref, sel_ref, par_ref, ob_ref,
             dxbc_ref, ddtr_ref, dd_ref, da_ref, ddtb_ref, dh, a_dd, a_da, a_dtb, dcs_lane, dcs_b, dxdt):
        step = pl.program_id(0)

        @pl.when(step == 0)
        def _():
            dh[...] = jnp.zeros_like(dh)
            a_dd[...] = jnp.zeros_like(a_dd)
            a_da[...] = jnp.zeros_like(a_da)
            a_dtb[...] = jnp.zeros_like(a_dtb)

        xs, dt, cs, cstp, dt_full, cs_full = _ssd_common(xbc_ref, dtr_ref, a_ref, dtb_ref, ex_ref, sel_ref, par_ref)
        cs_last = cs_full[CHUNK - 1:CHUNK, :]
        xdt = xs * dt_full
        dyv = dy_ref[...]
        a_dd[...] += _fold8(dyv * xs)
        causal, first = _pair_mask()
        ones_l = jnp.ones((CHUNK, 128), f32)
        for g in range(NG):
            gl = slice(g * GW, (g + 1) * GW)
            bcol = slice(SSD_W + g * NSTATE, SSD_W + (g + 1) * NSTATE)
            ccol = slice(SSD_W + NG * NSTATE + g * NSTATE, SSD_W + NG * NSTATE + (g + 1) * NSTATE)
            bg = xbc_ref[:, bcol].astype(bf16)
            cg = xbc_ref[:, ccol].astype(bf16)
            bg2 = jnp.concatenate([bg, bg], axis=0)
            cb2 = lax.dot_general(cg, bg2, (((1,), (1,)), ((), ())), preferred_element_type=f32)
            hg = hs_ref[g]
            hgb = hg.astype(bf16)
            dhg = dh[g]
            dhgb = dhg.astype(bf16)
            eg = jnp.exp(cs_full[:, gl])
            dec = jnp.exp(cs_last[:, gl] - cs_full[:, gl])
            gam = jnp.exp(cs_last[:, gl])
            dyg = dyv[:, gl]
            xdt_g = xdt[:, gl]
            y0 = jnp.dot(cg, hgb, preferred_element_type=f32)
            dy0 = (eg * dyg).astype(bf16)
            dcm = lax.dot_general(dy0, hgb, (((1,), (1,)), ((), ())), preferred_element_type=f32)
            dh_prev = gam * dhg + lax.dot_general(cg, dy0, (((0,), (0,)), ((), ())), preferred_element_type=f32)
            dgam = jnp.sum(dhg * hg, axis=0, keepdims=True) * gam
            dxdec = jnp.dot(bg, dhgb, preferred_element_type=f32)
            dbm = lax.dot_general((xdt_g * dec).astype(bf16), dhgb, (((1,), (1,)), ((), ())), preferred_element_type=f32)
            t = dxdec * xdt_g * dec
            dcs_lane[:, gl] = dyg * eg * y0 - t
            dcs_lane[CHUNK - 1:CHUNK, gl] += jnp.sum(t, axis=0, keepdims=True) + dgam
            dxdt[:, gl] = dxdec * dec
            dcb2 = jnp.zeros((CHUNK, 128), f32)
            for j in range(GW // 128):
                pair = g * (GW // 128) + j
                pl_ = slice(pair * 128, (pair + 1) * 128)
                seg = jnp.exp(jnp.where(causal, cs_full[:, pl_] - cstp[pair:pair + 1, :], -jnp.inf))
                m = cb2 * seg
                mb = m.astype(bf16)
                rhs = _block_diag(xdt[:, pl_].astype(bf16), first)
                dyp = dyv[:, pl_].astype(bf16)
                dm = lax.dot_general(dyp, rhs, (((1,), (1,)), ((), ())), preferred_element_type=f32)
                tt = lax.dot_general(mb, dyp, (((0,), (0,)), ((), ())), preferred_element_type=f32)
                dxdt[:, pl_] += jnp.where(first, tt[0:CHUNK], tt[CHUNK:])
                dcb2 = dcb2 + dm * seg
                w = dm * m
                rsum = jnp.dot(w, ob_ref[...], precision=HIGHEST, preferred_element_type=f32)
                t2 = lax.dot_general(w, ones_l, (((0,), (0,)), ((), ())), precision=HIGHEST, preferred_element_type=f32)
                dcs_b[:, pl_] = rsum - jnp.where(first, t2[0:CHUNK], t2[CHUNK:])
            dcb2b = dcb2.astype(bf16)
            dcm = dcm + jnp.dot(dcb2b, bg2, preferred_element_type=f32)
            t3 = lax.dot_general(dcb2b, cg, (((0,), (0,)), ((), ())), preferred_element_type=f32)
            dxbc_ref[:, bcol] = dbm + t3[0:CHUNK] + t3[CHUNK:]
            dxbc_ref[:, ccol] = dcm
            dh[g] = dh_prev
        dcs = jnp.dot(dcs_lane[...] + dcs_b[...] * (1.0 / HD), ext_ref[...], precision=HIGHEST, preferred_element_type=f32)
        r_i = lax.broadcasted_iota(jnp.int32, (CHUNK, CHUNK), 0)
        c_i = lax.broadcasted_iota(jnp.int32, (CHUNK, CHUNK), 1)
        triu = (r_i <= c_i).astype(f32)
        da_ = jnp.dot(triu, dcs, precision=HIGHEST, preferred_element_type=f32)
        dxdtv = dxdt[...]
        ddt = da_ * a_ref[...] + jnp.dot(dxdtv * xs, ext_ref[...], precision=HIGHEST, preferred_element_type=f32)
        a_da[...] += _fold8(da_ * dt)
        dxbc_ref[:, 0:SSD_W] = dyv * dsk_ref[...] + dxdtv * dt_full
        ddtr = ddt * _sigmoid(dtr_ref[...] + dtb_ref[...])
        ddtr_ref[...] = ddtr
        a_dtb[...] += _fold8(ddtr)

        @pl.when(step == nc - 1)
        def _():
            dd_ref[...] = jnp.sum(jnp.dot(a_dd[...], ext_ref[...], precision=HIGHEST, preferred_element_type=f32), axis=0, keepdims=True)
            da_ref[...] = jnp.sum(a_da[...], axis=0, keepdims=True)
            ddtb_ref[...] = jnp.sum(a_dtb[...], axis=0, keepdims=True)

    rev = lambda c: nc - 1 - c
    const = lambda shape: pl.BlockSpec(shape, lambda c: tuple(0 for _ in shape))
    return pl.pallas_call(
        body, name="ssd_bwd", grid=(nc,),
        in_specs=[pl.BlockSpec((CHUNK, XBC), lambda c: (rev(c), 0)), pl.BlockSpec((CHUNK, 128), lambda c: (rev(c), DT_COL)),
                  pl.BlockSpec((CHUNK, SSD_W), lambda c: (rev(c), 0)), pl.BlockSpec((None, NG, NSTATE, GW), lambda c: (rev(c), 0, 0, 0)),
                  const((1, 128)), const((1, 128)), const((1, SSD_W)), const((128, SSD_W)), const((SSD_W, 128)),
                  const((8, 128)), const((128, 128)), const((128, 128))],
        out_specs=[pl.BlockSpec((CHUNK, XBC), lambda c: (rev(c), 0)), pl.BlockSpec((CHUNK, 128), lambda c: (rev(c), 0)),
                   const((1, 128)), const((1, 128)), const((1, 128))],
        out_shape=[_sds((s, XBC), f32), _sds((s, 128), f32), _sds((1, 128), f32), _sds((1, 128), f32), _sds((1, 128), f32)],
        scratch_shapes=[pltpu.VMEM((NG, NSTATE, GW), f32), pltpu.VMEM((8, SSD_W), f32), pltpu.VMEM((8, 128), f32), pltpu.VMEM((8, 128), f32),
                        pltpu.VMEM((CHUNK, SSD_W), f32), pltpu.VMEM((CHUNK, SSD_W), f32), pltpu.VMEM((CHUNK, SSD_W), f32)],
        compiler_params=_params(("arbitrary",)),
    )(xbc, proj2, dy, hsave, a_row, dtb_row, dsk_full, jnp.asarray(ex), jnp.asarray(ext), jnp.asarray(sel), jnp.asarray(par),
      jnp.asarray(ones_blk))


def _local_step(x, tgt, mods, g_mix, rel, conv_w, conv_b, dt_bias, a_log, d_skip, g_att, g_ssd, g_ffn, g_final, weights):
    s = x.shape[0]
    tm_e = 256 if s % 256 == 0 else s
    tm_m = 512 if s % 512 == 0 else s
    tm_l = 1024 if s % 1024 == 0 else s
    tk = 2048 if s % 2048 == 0 else s
    sh1, sc1, gt1, sh2, sc2, gt2 = [mods[:, i * D:(i + 1) * D] for i in range(6)]

    h1b = _norm_mod("norm_mod_1", x, g_mix, sc1, sh1, tm_e)
    win, win_b = weights.w_in(h1b)
    qkv = _mm_nn_fullk("proj_qkv", h1b, win, tm_m, 768, bf16, n=IN_A)
    proj2 = _mm_nn_fullk("proj_zxbcdt", h1b, win_b, tm_m, 896, f32)
    bias = _expand_bias(rel)
    att = _attn_fwd(qkv, bias)
    xbc = _ssd_conv(proj2, conv_w, conv_b, tm_e)
    a_row = jnp.pad(-jnp.exp(a_log), ((0, 0), (0, 128 - NH)))
    dtb_row = jnp.pad(dt_bias, ((0, 0), (0, 128 - NH)))
    dsk_full = jnp.repeat(d_skip, HD, axis=1)
    y, hsave = _ssd_fwd(xbc, proj2, a_row, dtb_row, dsk_full)
    mixcat = _mix_pre(att, y, proj2, g_att, g_ssd, tm_e)
    wout = weights.w_out(mixcat)
    mix = _mm_nn_fullk("proj_out", mixcat, wout, tm_m, 1024, f32)
    x2, h2b = _resid_norm_mod(x, gt1, mix, g_ffn, sc2, sh2, tm_e)
    wg4, wu4, wd4 = weights.ffn(h2b)
    gate, up, act = _ffn_up(h2b, wg4, wu4, tm_m)
    ffn = _ffn_down(act, wd4, tm_l)

    dx3, dffn, loss, dg_final, dgt2 = _final_fwd_bwd(x2, ffn, gt2, g_final, tgt, tm_e)
    tok = weights.grad(("w_down",), [_grad_wdown4(act, dffn, 1024, tk)])
    dgate, dup = _ffn_dact(dffn, wd4, gate, up, tm_m, dep=tok)
    tok = weights.grad(("w_gate", "w_up"), [_grad_cols4("grad_w_gate", h2b, dgate, 1024, tk), _grad_cols4("grad_w_up", h2b, dup, 1024, tk)])
    dh2 = _ffn_dh(dgate, dup, wg4, wu4, tm_m, dep=tok)
    dx2, dmix, dsc2, dsh2, dg_ffn, dgt1 = _norm_mod_bwd("norm_mod_bwd_2", dh2, x2, g_ffn, sc2, dx3, tm_e, mix=mix, gt=gt1)
    tok = weights.grad(("w_out",), [_mm_tn("grad_w_out", mixcat, dmix, 1024, 1024, tk, bf16).reshape(NSH, D // NSH, D)])
    dmc = _mm_nt("dmixcat", dmix, wout, tm_m, 1024, D, f32, dep=tok)
    datt, dy, dz, dg_att, dg_ssd = _mix_pre_bwd(dmc, att, y, proj2, g_att, g_ssd, tm_e)
    dq, dk, dv, gband = _attn_bwd(qkv, datt, bias, jnp.transpose(bias, (0, 2, 1)))
    drel = _rel_bias_grad(gband.reshape(NH, CHUNK, BANDP))
    dxbc, ddtr, dd_row, da_row, ddtb_row = _ssd_bwd(xbc, proj2, dy, hsave, a_row, dtb_row, dsk_full)
    dxbc_raw, dconv_w, dconv_b = _ssd_conv_bwd(dxbc, proj2, conv_w, conv_b, tm_e)
    dproj = jnp.concatenate([dq, dk, dv, dz, dxbc_raw, ddtr.astype(bf16)], axis=1)
    gwin = _mm_tn("grad_w_in", h1b, dproj, 1024, 1152, tk, bf16)
    gwin4 = jnp.stack([jnp.pad(gwin[:, k * IN_SH:(k + 1) * IN_SH], ((0, 0), (0, IN_SHP - IN_SH))) for k in range(NSH)])
    tok = weights.grad(("w_in",), [gwin4])
    dh1 = _mm_nt("dh1", dproj, win, tm_m, D, 1920, f32, dep=tok)
    grad_x, dsc1, dsh1, dg_mix = _norm_mod_bwd("norm_mod_bwd_1", dh1, x, g_mix, sc1, dx2, tm_e)

    dmods = jnp.concatenate([dsh1, dsc1, dgt1, dsh2, dsc2, dgt2], axis=1)
    dd_skip = dd_row[:, :NH]
    da_log = da_row[:, :NH] * a_row[:, :NH]
    small = dict(g_mix=dg_mix, conv_b=dconv_b, dt_bias=ddtb_row[:, :NH], a_log=da_log, d_skip=dd_skip, g_att_out=dg_att,
                 g_ssd_out=dg_ssd, g_ffn=dg_ffn, g_final=dg_final, rel_bias=drel, conv_w=dconv_w)
    return loss[0, 0], grad_x, dmods, small


HBM = pl.BlockSpec(memory_space=pl.ANY)
VMEM = pl.BlockSpec(memory_space=pltpu.VMEM)


def _place():
    x, y, c = lax.axis_index("x"), lax.axis_index("y"), lax.axis_index("c")
    chips = [(1 - x, y), (x, 1 - y), (1 - x, 1 - y)]
    return x, y, c, chips


def _allgather8(name, payload, dep=None):
    r = payload.shape[0]
    deps = [] if dep is None else [dep]

    def body(x_ref, *rest):
        out_ref, send_sems, recv_sems, local_sem = rest[-4:]
        x, y, c, chips = _place()
        me, sibling = (x, y, c), (x, y, 1 - c)

        def slot(px, py, pc):
            return out_ref.at[4 * px + 2 * py + pc]

        def copy(k, block, to, src=None):
            return pltpu.make_async_remote_copy(
                src_ref=slot(*block) if src is None else src, dst_ref=slot(*block),
                send_sem=send_sems.at[k], recv_sem=recv_sems.at[k], device_id=to, device_id_type=MESH)

        mine = pltpu.make_async_copy(x_ref, slot(*me), local_sem)
        mine.start()
        first = [copy(0, me, sibling, src=x_ref)]
        first += [copy(1 + j, me, (*chip, c), src=x_ref) for j, chip in enumerate(chips)]
        for cp in first:
            cp.start()
        passed = [copy(4 + j, (*chip, c), sibling) for j, chip in enumerate(chips)]
        for j, chip in enumerate(chips):
            copy(1 + j, (*chip, c), me).wait_recv()
            passed[j].start()
        copy(0, sibling, me).wait_recv()
        for j, chip in enumerate(chips):
            copy(4 + j, (*chip, 1 - c), me).wait_recv()
        for cp in first + passed:
            cp.wait_send()
        mine.wait()

    return pl.pallas_call(
        body, name=name, out_shape=_sds((N_DEV, r, 128), f32), in_specs=[VMEM] * (1 + len(deps)), out_specs=VMEM,
        scratch_shapes=[pltpu.SemaphoreType.DMA((7,)), pltpu.SemaphoreType.DMA((7,)), pltpu.SemaphoreType.DMA],
    )(payload, *deps)


def _sum8(g):
    r = g.shape[1]

    def body(g_ref, o_ref):
        acc = g_ref[0]
        for i in range(1, N_DEV):
            acc = acc + g_ref[i]
        o_ref[...] = acc

    return pl.pallas_call(body, name="sum8", out_shape=_sds((r, 128), f32))(g)


SEM = pl.BlockSpec(memory_space=pltpu.SEMAPHORE)
EFFECT = pltpu.SideEffectType.DATAFLOW_SIDE_EFFECTING


def _gather_copies(ins, lands, send_sems, recv_sems):
    x, y, c, chips = _place()
    k = 2 * x + y
    starts, recvs = [], []
    for w in range(len(ins)):
        for j, (px, py) in enumerate(chips):
            def mk(dst):
                return pltpu.make_async_remote_copy(src_ref=ins[w].at[c], dst_ref=dst, send_sem=send_sems[w].at[j],
                                                    recv_sem=recv_sems[w].at[j], device_id=(px, py, c), device_id_type=MESH)
            starts.append(mk(lands[w].at[k, c]))
            recvs.append(mk(lands[w].at[2 * px + py, c]))
    return starts, recvs


def _reduce_copies(ins, lands, send_sems, recv_sems):
    x, y, c, chips = _place()
    k = 2 * x + y
    starts, recvs = [], []
    for w in range(len(ins)):
        for j, (px, py) in enumerate(chips):
            def mk(dst):
                return pltpu.make_async_remote_copy(src_ref=ins[w].at[2 * px + py], dst_ref=dst, send_sem=send_sems[w].at[j],
                                                    recv_sem=recv_sems[w].at[j], device_id=(px, py, c), device_id_type=MESH)
            starts.append(mk(lands[w].at[k]))
            recvs.append(mk(lands[w].at[2 * px + py]))
    return starts, recvs


def _split_start(name, copies, srcs, land_shapes):
    nw = len(srcs)

    def body(*refs):
        starts, _ = copies(refs[:nw], refs[nw:2 * nw], refs[2 * nw:3 * nw], refs[3 * nw:4 * nw])
        for cp in starts:
            cp.start()
        refs[6 * nw][...] = jnp.zeros((8, 128), f32)

    sems = [pltpu.SemaphoreType.DMA((3,))] * nw
    bufs = [pltpu.HBM(s.shape, bf16) for s in srcs] + [pltpu.HBM(s, bf16) for s in land_shapes]
    res = pl.pallas_call(
        body, name=name, out_shape=sems + sems + bufs + [_sds((8, 128), f32)],
        in_specs=[HBM] * (2 * nw), out_specs=[SEM] * (2 * nw) + [HBM] * (2 * nw) + [VMEM],
        input_output_aliases={i: 2 * nw + i for i in range(2 * nw)},
        compiler_params=pltpu.CompilerParams(has_side_effects=EFFECT),
    )(*[pltpu.with_memory_space_constraint(s, pltpu.HBM) for s in srcs],
      *[pltpu.with_memory_space_constraint(lax.empty(s, bf16), pltpu.HBM) for s in land_shapes])
    return res[:nw], res[nw:2 * nw], res[2 * nw:3 * nw], res[3 * nw:4 * nw], res[4 * nw]


def _split_wait(name, copies, send_sems, recv_sems, srcs, lands, after):
    nw = len(srcs)

    def body(*refs):
        starts, recvs = copies(refs[:nw], refs[nw:2 * nw], refs[2 * nw:3 * nw], refs[3 * nw:4 * nw])
        for s_, r_ in zip(starts, recvs):
            s_.wait_send()
            r_.wait_recv()

    bufs = [pltpu.HBM(s.shape, bf16) for s in srcs] + [pltpu.HBM(l.shape, bf16) for l in lands]
    res = pl.pallas_call(
        body, name=name, out_shape=bufs, in_specs=[HBM] * (2 * nw) + [SEM] * (2 * nw) + [HBM], out_specs=[HBM] * (2 * nw),
        input_output_aliases={i: i for i in range(2 * nw)},
        compiler_params=pltpu.CompilerParams(has_side_effects=EFFECT),
    )(*srcs, *lands, *send_sems, *recv_sems, after)
    return res[:nw], res[nw:]


def _gather_forward(name, shards, lands):
    nw = len(shards)

    def body(*refs):
        ins, lands_in, outs = refs[:nw], refs[nw:2 * nw], refs[2 * nw:3 * nw]
        st_a, st_b, st_c = refs[3 * nw:4 * nw], refs[4 * nw:5 * nw], refs[5 * nw:6 * nw]
        send_sems, recv_sems, load_sems, store_sems = refs[6 * nw:]
        x, y, c, chips = _place()
        k = 2 * x + y
        sibling = (x, y, 1 - c)
        ld_a = [pltpu.make_async_copy(ins[w].at[c], st_a[w], load_sems.at[w, 0]) for w in range(nw)]
        ld_b = [pltpu.make_async_copy(ins[w].at[1 - c], st_b[w], load_sems.at[w, 1]) for w in range(nw)]
        for cp in ld_a + ld_b:
            cp.start()
        st_own = []
        for w in range(nw):
            ld_a[w].wait()
            st_own.append(pltpu.make_async_copy(st_a[w], outs[w].at[k, c], store_sems.at[w, 0]))
            st_own[-1].start()
        for w in range(nw):
            ld_b[w].wait()
            st_own.append(pltpu.make_async_copy(st_b[w], outs[w].at[k, 1 - c], store_sems.at[w, 1]))
            st_own[-1].start()
        for cp in st_own:
            cp.wait()
        fwds = {}
        for j, (px, py) in enumerate(chips):
            kq = 2 * px + py
            for w in range(nw):
                slot = st_b[w] if j % 2 == 0 else st_c[w]
                if j == 2:
                    fwds[w, 0].wait_send()
                ld = pltpu.make_async_copy(lands_in[w].at[kq, c], slot, load_sems.at[w, 2 + j])
                ld.start()
                ld.wait()
                fwds[w, j] = pltpu.make_async_remote_copy(src_ref=slot, dst_ref=outs[w].at[kq, c], send_sem=send_sems.at[w, j],
                                                          recv_sem=recv_sems.at[w, j], device_id=sibling, device_id_type=MESH)
                fwds[w, j].start()
        for j, (px, py) in enumerate(chips):
            for w in range(nw):
                pltpu.make_async_remote_copy(src_ref=st_c[w], dst_ref=outs[w].at[2 * px + py, 1 - c], send_sem=send_sems.at[w, j],
                                             recv_sem=recv_sems.at[w, j], device_id=sibling, device_id_type=MESH).wait_recv()
        for w in range(nw):
            fwds[w, 1].wait_send()
            fwds[w, 2].wait_send()

    stage = [pltpu.VMEM(s.shape[1:], bf16) for s in shards]
    return pl.pallas_call(
        body, name=name, out_shape=[_sds(l.shape, bf16) for l in lands],
        in_specs=[HBM] * (2 * nw), out_specs=[HBM] * nw, input_output_aliases={nw + w: w for w in range(nw)},
        scratch_shapes=stage * 3 + [pltpu.SemaphoreType.DMA((nw, 3)), pltpu.SemaphoreType.DMA((nw, 3)), pltpu.SemaphoreType.DMA((nw, 5)),
                                    pltpu.SemaphoreType.DMA((nw, 2))],
        compiler_params=pltpu.CompilerParams(vmem_limit_bytes=VMEM_LIMIT),
    )(*shards, *lands)


def _rs_pair_exchange(name, grads):
    nw = len(grads)

    def body(*refs):
        ins, got, stage = refs[:nw], refs[nw:2 * nw], refs[2 * nw:3 * nw]
        send_sems, recv_sems, load_sems = refs[3 * nw:]
        x, y, c, _ = _place()

        def load(w, kk):
            return pltpu.make_async_copy(ins[w].at[kk, 1 - c], stage[w].at[kk % 2], load_sems.at[w, kk])

        def send(w, kk):
            return pltpu.make_async_remote_copy(src_ref=stage[w].at[kk % 2], dst_ref=got[w].at[kk], send_sem=send_sems.at[w, kk],
                                                recv_sem=recv_sems.at[w, kk], device_id=(x, y, 1 - c), device_id_type=MESH)

        for kk in range(2):
            for w in range(nw):
                load(w, kk).start()
        for kk in range(NSH):
            for w in range(nw):
                load(w, kk).wait()
                send(w, kk).start()
            if kk + 2 < NSH:
                for w in range(nw):
                    send(w, kk).wait_send()
                    load(w, kk + 2).start()
        for kk in range(NSH - 2, NSH):
            for w in range(nw):
                send(w, kk).wait_send()
        for kk in range(NSH):
            for w in range(nw):
                send(w, kk).wait_recv()

    return pl.pallas_call(
        body, name=name, out_shape=[_sds((NSH,) + g.shape[2:], bf16) for g in grads], in_specs=[HBM] * nw, out_specs=[HBM] * nw,
        scratch_shapes=[pltpu.VMEM((2,) + g.shape[2:], bf16) for g in grads]
        + [pltpu.SemaphoreType.DMA((nw, NSH)), pltpu.SemaphoreType.DMA((nw, NSH)), pltpu.SemaphoreType.DMA((nw, NSH))],
        compiler_params=pltpu.CompilerParams(vmem_limit_bytes=VMEM_LIMIT),
    )(*grads)


def _rs_pair_gather(name, halves):
    nw = len(halves)

    def body(*refs):
        ins, outs, stage = refs[:nw], refs[nw:2 * nw], refs[2 * nw:3 * nw]
        send_sems, recv_sems, local_sems, stage_sems = refs[3 * nw:]
        x, y, c, _ = _place()
        loads = [pltpu.make_async_copy(ins[w], stage[w], stage_sems.at[w]) for w in range(nw)]
        for cp in loads:
            cp.start()
        local, cps = [], []
        for w in range(nw):
            loads[w].wait()
            local.append(pltpu.make_async_copy(stage[w], outs[w].at[c], local_sems.at[w]))
            cps.append(pltpu.make_async_remote_copy(src_ref=stage[w], dst_ref=outs[w].at[c], send_sem=send_sems.at[w],
                                                    recv_sem=recv_sems.at[w], device_id=(x, y, 1 - c), device_id_type=MESH))
            local[w].start()
            cps[w].start()
        for w in range(nw):
            pltpu.make_async_remote_copy(src_ref=stage[w], dst_ref=outs[w].at[1 - c], send_sem=send_sems.at[w], recv_sem=recv_sems.at[w],
                                         device_id=(x, y, 1 - c), device_id_type=MESH).wait_recv()
        for cp in cps:
            cp.wait_send()
        for cp in local:
            cp.wait()

    return pl.pallas_call(
        body, name=name, out_shape=[_sds((2,) + h.shape, f32) for h in halves], in_specs=[HBM] * nw, out_specs=[HBM] * nw,
        scratch_shapes=[pltpu.VMEM(h.shape, f32) for h in halves]
        + [pltpu.SemaphoreType.DMA((nw,)), pltpu.SemaphoreType.DMA((nw,)), pltpu.SemaphoreType.DMA((nw,)), pltpu.SemaphoreType.DMA((nw,))],
        compiler_params=pltpu.CompilerParams(vmem_limit_bytes=VMEM_LIMIT),
    )(*halves)


def _row_tile(r, c, nbuf):
    budget = 24 * 1024 * 1024 // (2 * nbuf * 4 * c)
    t = 8
    while t * 2 <= budget and r % (t * 2) == 0:
        t *= 2
    return t


def _cast_bf16(name, a, dep=None):
    r, c = a.shape
    tr = _row_tile(r, c, 2)
    dep_specs, dep_ops = _dep_args(dep, 1)

    def body(a_ref, *rest):
        rest[-1][...] = a_ref[...].astype(bf16)

    spec = pl.BlockSpec((tr, c), lambda i: (i, 0))
    return pl.pallas_call(body, name=name, grid=(r // tr,), in_specs=[spec] + dep_specs, out_specs=spec, out_shape=_sds((r, c), bf16),
                          compiler_params=_params(("parallel",)))(a, *dep_ops)


def _w_in_columns(win4):
    tr = 256

    def body(a_ref, o_ref, ob_ref):
        for k in range(NSH):
            o_ref[:, IN_SH * k:IN_SH * (k + 1)] = a_ref[k][:, :IN_SH]
        o_ref[:, IN_COLS:] = jnp.zeros((tr, IN_P - IN_COLS), bf16)
        ob_ref[...] = o_ref[:, IN_A:]

    return pl.pallas_call(
        body, name="w_in_columns", grid=(D // tr,), in_specs=[pl.BlockSpec((NSH, tr, IN_SHP), lambda i: (0, i, 0))],
        out_specs=[pl.BlockSpec((tr, IN_P), lambda i: (i, 0)), pl.BlockSpec((tr, IN_B), lambda i: (i, 0))],
        out_shape=[_sds((D, IN_P), bf16), _sds((D, IN_B), bf16)], compiler_params=_params(("parallel",)))(win4)


def _pair_sum(name, core, grads, got):
    _, _, rh, c = grads.shape
    tr = _row_tile(rh, c, 2)

    def body(c_ref, a_ref, b_ref, o_ref):
        o_ref[...] = (a_ref[...].astype(f32) + b_ref[...].astype(f32)).astype(bf16)

    spec = pl.BlockSpec((None, tr, c), lambda k, i, c_ref: (k, i, 0))
    return pl.pallas_call(
        body, name=name, out_shape=_sds((NSH, rh, c), bf16),
        grid_spec=pltpu.PrefetchScalarGridSpec(
            num_scalar_prefetch=1, grid=(NSH, rh // tr),
            in_specs=[pl.BlockSpec((None, None, tr, c), lambda k, i, c_ref: (k, c_ref[0], i, 0)), spec], out_specs=spec),
        compiler_params=_params(("parallel", "parallel")))(core, grads, got)


def _chip_sum(name, chip, sums, lands):
    _, rh, c = sums.shape
    tr = _row_tile(rh, c, 4)

    def body(k_ref, own_ref, l_ref, o_ref):
        own = own_ref[...].astype(f32)
        acc = None
        for j in range(NSH):
            term = jnp.where(k_ref[0] == j, own, l_ref[j].astype(f32))
            acc = term if acc is None else acc + term
        o_ref[...] = acc

    return pl.pallas_call(
        body, name=name, out_shape=_sds((rh, c), f32),
        grid_spec=pltpu.PrefetchScalarGridSpec(
            num_scalar_prefetch=1, grid=(rh // tr,),
            in_specs=[pl.BlockSpec((None, tr, c), lambda i, k_ref: (k_ref[0], i, 0)), pl.BlockSpec((NSH, tr, c), lambda i, k_ref: (0, i, 0))],
            out_specs=pl.BlockSpec((tr, c), lambda i, k_ref: (i, 0))),
        compiler_params=_params(("parallel",)))(chip, sums, lands)


def _mods_part(cond16, w_ada, b_part):
    n = w_ada.shape[1]
    tn = 512

    def body(c_ref, w_ref, b_ref, o_ref):
        cv = c_ref[...]
        o_ref[...] = _dot(cv * _sigmoid(cv), w_ref[...]) + b_ref[...]

    return pl.pallas_call(
        body, name="mods_part", grid=(n // tn,),
        in_specs=[pl.BlockSpec((16, D), lambda j: (0, 0)), pl.BlockSpec((D, tn), lambda j: (0, j)), pl.BlockSpec((1, tn), lambda j: (0, j))],
        out_specs=pl.BlockSpec((16, tn), lambda j: (0, j)), out_shape=_sds((16, n), f32), compiler_params=_params(("parallel",)),
    )(cond16, w_ada, b_part)


def _grad_w_ada(cond16, dm16):
    n = dm16.shape[1]
    tr = 256

    def body(c_ref, d_ref, o_ref):
        cv = c_ref[...]
        o_ref[...] = _dot(cv * _sigmoid(cv), d_ref[...], ta=True)

    return pl.pallas_call(
        body, name="grad_w_ada", grid=(D // tr,),
        in_specs=[pl.BlockSpec((16, tr), lambda i: (0, i)), pl.BlockSpec((16, n), lambda i: (0, 0))],
        out_specs=pl.BlockSpec((tr, n), lambda i: (i, 0)), out_shape=_sds((D, n), f32), compiler_params=_params(("parallel",)),
    )(cond16, dm16)


def _adamw(name, w, g, m, v):
    r, c = w.shape
    tr = _row_tile(r, c, 7)

    def body(w_ref, g_ref, m_ref, v_ref, d_ref, nm_ref, nv_ref):
        gv = g_ref[...]
        nm = ADAM_B1 * m_ref[...] + (1.0 - ADAM_B1) * gv
        nv = ADAM_B2 * v_ref[...] + (1.0 - ADAM_B2) * (gv * gv)
        nm_ref[...] = nm
        nv_ref[...] = nv
        m_hat = nm / (1.0 - ADAM_B1 ** ADAM_STEP)
        v_hat = nv / (1.0 - ADAM_B2 ** ADAM_STEP)
        d_ref[...] = -ADAM_LR * (m_hat / (jnp.sqrt(v_hat) + ADAM_EPS) + ADAM_WD * w_ref[...])

    spec = pl.BlockSpec((tr, c), lambda i: (i, 0))
    return pl.pallas_call(body, name=name, grid=(r // tr,), in_specs=[spec] * 4, out_specs=[spec] * 3, out_shape=[_sds((r, c), f32)] * 3,
                          compiler_params=_params(("parallel",)))(w, g, m, v)


def _pack(parts, rows):
    flat = []
    for p in parts:
        p = p.reshape(-1)
        flat.append(jnp.pad(p, (0, (-p.shape[0]) % 128)))
    v = jnp.concatenate(flat)
    return jnp.pad(v, (0, rows * 128 - v.shape[0])).reshape(rows, 128)


def _unpack(packed, sizes):
    lead = packed.shape[:-2]
    flat = packed.reshape(lead + (-1,))
    out, off = [], 0
    for n in sizes:
        out.append(flat[..., off:off + n])
        off += n + (-n) % 128
    return out


BIG = ("w_in", "w_out", "w_gate", "w_up", "w_down")
SMALL = ("b_ada", "g_mix", "conv_b", "dt_bias", "a_log", "d_skip", "g_att_out", "g_ssd_out", "g_ffn", "g_final", "rel_bias", "conv_w")
ORDER = ("w_ada", "b_ada", "g_mix", "w_in", "rel_bias", "conv_w", "conv_b", "dt_bias", "a_log", "d_skip", "g_att_out", "g_ssd_out",
         "w_out", "g_ffn", "w_gate", "w_up", "w_down", "g_final")
REL_SH = N_REL // NSH
CONVW_SH = XBC // NSH
ADA_SH = 6 * D // NSH


class _Exchange:
    def __init__(self, core, chip):
        self.core, self.chip = core, chip
        self.gathered = {}
        self.pending = []

    def gather(self, names, shards):
        ssem, rsem, thru, lands, token = _split_start("gather_start_" + "_".join(names), _gather_copies, shards,
                                                      [(NSH,) + s.shape for s in shards])
        self.gathered.update({n: (ssem[i], rsem[i], thru[i], lands[i]) for i, n in enumerate(names)})
        return token

    def _whole(self, names, after):
        ssem, rsem, thru, lands = zip(*[self.gathered[n] for n in names])
        tag = "_".join(names)
        thru, lands = _split_wait("gather_wait_" + tag, _gather_copies, ssem, rsem, thru, lands, after)
        return _gather_forward("gather_forward_" + tag, thru, lands)

    def w_in(self, after):
        (win4,) = self._whole(("w_in",), after)
        return _w_in_columns(win4.reshape(NSH, D, IN_SHP))

    def w_out(self, after):
        (wout4,) = self._whole(("w_out",), after)
        return wout4.reshape(D, D)

    def ffn(self, after):
        wg4, wu4, wd4 = self._whole(("w_gate", "w_up", "w_down"), after)
        return wg4.reshape(NSH, D, FSH), wu4.reshape(NSH, D, FSH), wd4.reshape(NSH, FSH, D)

    def grad(self, names, grads):
        tag = "_".join(names)
        stacked = [g.reshape(NSH, 2, g.shape[1] // 2, g.shape[2]) for g in grads]
        got = _rs_pair_exchange("rs_pair_exchange_" + tag, stacked)
        sums = [_pair_sum("pair_sum_" + n, self.core, o, g) for n, o, g in zip(names, stacked, got)]
        self.pending.append((names, _split_start("rs_start_" + tag, _reduce_copies, sums, [s.shape for s in sums])))
        return self.pending[-1][1][4]

    def finish(self, after):
        grads = {}
        for names, (ssem, rsem, sums, lands, _) in self.pending:
            tag = "_".join(names)
            sums, lands = _split_wait("rs_wait_" + tag, _reduce_copies, ssem, rsem, sums, lands, after)
            halves = [_chip_sum("chip_sum_" + n, self.chip, sm, ld) for n, sm, ld in zip(names, sums, lands)]
            for n, f in zip(names, _rs_pair_gather("rs_pair_gather_" + tag, halves)):
                grads[n] = f.reshape(2 * f.shape[1], f.shape[2])
        return grads


def kernel(x, c, w_ada, b_ada, g_mix, w_in, rel_bias, conv_w, conv_b, dt_bias, a_log, d_skip, g_att_out, g_ssd_out, w_out, g_ffn, w_gate, w_up, w_down, g_final, loss_target, m_w_ada, m_b_ada, m_g_mix, m_w_in, m_rel_bias, m_conv_w, m_conv_b, m_dt_bias, m_a_log, m_d_skip, m_g_att_out, m_g_ssd_out, m_w_out, m_g_ffn, m_w_gate, m_w_up, m_w_down, m_g_final, v_w_ada, v_b_ada, v_g_mix, v_w_in, v_rel_bias, v_conv_w, v_conv_b, v_dt_bias, v_a_log, v_d_skip, v_g_att_out, v_g_ssd_out, v_w_out, v_g_ffn, v_w_gate, v_w_up, v_w_down, v_g_final):
    args = dict(locals())
    w = {n: args[n] for n in ORDER}
    m = {n: args["m_" + n] for n in ORDER}
    v = {n: args["v_" + n] for n in ORDER}
    ix, iy, ic = lax.axis_index("x"), lax.axis_index("y"), lax.axis_index("c")
    chip = 2 * ix + iy
    dev = 2 * chip + ic
    s = x.shape[1]

    g1 = _allgather8("gather_inputs", _pack([c[0], rel_bias[0], conv_w[0]], 40))
    c_all, rel_sh, convw_sh = _unpack(g1, [D, NH * REL_SH, 4 * CONVW_SH])
    rel_full = jnp.concatenate([rel_sh[2 * k].reshape(NH, REL_SH) for k in range(NSH)], axis=1)
    convw_full = jnp.concatenate([convw_sh[2 * k].reshape(4, CONVW_SH) for k in range(NSH)], axis=1)
    cond16 = jnp.pad(c_all, ((0, 8), (0, 0)))
    b_part = lax.dynamic_slice_in_dim(b_ada, chip * ADA_SH, ADA_SH, axis=1)
    mods_part = _mods_part(cond16, w_ada[0], b_part)[:N_DEV]
    g2 = _allgather8("gather_mods", mods_part.reshape(N_DEV * ADA_SH // 128, 128))
    mods_all = jnp.concatenate([g2[2 * k].reshape(N_DEV, ADA_SH) for k in range(NSH)], axis=1)
    mods = lax.dynamic_slice_in_dim(mods_all, dev, 1, axis=0)

    exchange = _Exchange(jnp.reshape(ic, (1,)).astype(jnp.int32), jnp.reshape(chip, (1,)).astype(jnp.int32))
    shard_in = _cast_bf16("cast_w_in", jnp.pad(w_in[0], ((0, 0), (0, IN_SHP - IN_SH))), dep=g2[0, :8]).reshape(2, D // 2, IN_SHP)
    tok = exchange.gather(("w_in",), [shard_in])
    tok = exchange.gather(("w_out", "w_gate", "w_up", "w_down"), [
        _cast_bf16("cast_w_out", w_out[0], dep=tok).reshape(2, D // NSH // 2, D),
        _cast_bf16("cast_w_gate", w_gate[0], dep=tok).reshape(2, D // 2, FSH),
        _cast_bf16("cast_w_up", w_up[0], dep=tok).reshape(2, D // 2, FSH),
        _cast_bf16("cast_w_down", w_down[0], dep=tok).reshape(2, FSH // 2, D)])
    mods = mods + tok[:1, :1]

    loss, grad_x, dmods, small = _local_step(
        x[0], loss_target[0], mods, g_mix, rel_full, convw_full, conv_b, dt_bias, a_log, d_skip, g_att_out, g_ssd_out, g_ffn,
        g_final[None, :], exchange)

    small_names = ("g_mix", "conv_b", "dt_bias", "a_log", "d_skip", "g_att_out", "g_ssd_out", "g_ffn", "g_final", "rel_bias", "conv_w")
    g3 = _allgather8("gather_small_grads", _pack([dmods] + [small[n] for n in small_names], 264))
    sizes = [6 * D] + [int(np.prod(small[n].shape)) for n in small_names]
    dmods_all = _unpack(g3, sizes)[0]
    summed = _unpack(_sum8(g3), sizes)
    grads = {"b_ada": summed[0].reshape(1, 6 * D)}
    for n, val in zip(small_names, summed[1:]):
        grads[n] = val.reshape(small[n].shape)
    grads["rel_bias"] = lax.dynamic_slice_in_dim(grads["rel_bias"], chip * REL_SH, REL_SH, axis=1)
    grads["conv_w"] = lax.dynamic_slice_in_dim(grads["conv_w"], chip * CONVW_SH, CONVW_SH, axis=1)
    grads["g_final"] = grads["g_final"].reshape(D)
    dm16 = jnp.pad(lax.dynamic_slice_in_dim(dmods_all, chip * ADA_SH, ADA_SH, axis=1), ((0, 8), (0, 0)))
    grads["w_ada"] = _grad_w_ada(cond16, dm16)

    delta, new_m, new_v = {}, {}, {}
    delta["w_ada"], new_m["w_ada"], new_v["w_ada"] = _adamw("adamw_w_ada", w_ada[0], grads["w_ada"], m_w_ada[0], v_w_ada[0])
    grads.update(exchange.finish(grad_x))
    grads["w_in"] = grads["w_in"][:, :IN_SH]
    for n in BIG:
        delta[n], new_m[n], new_v[n] = _adamw("adamw_" + n, w[n][0], grads[n], m[n][0], v[n][0])
    sw = _pack([w[n] for n in SMALL], 200)
    sg = _pack([grads[n] for n in SMALL], 200)
    sm = _pack([m[n] for n in SMALL], 200)
    sv = _pack([v[n] for n in SMALL], 200)
    ssz = [int(np.prod(w[n].shape)) for n in SMALL]
    for dst, packed in zip((delta, new_m, new_v), _adamw("adamw_small", sw, sg, sm, sv)):
        for n, val in zip(SMALL, _unpack(packed, ssz)):
            dst[n] = val

    def shaped(d, n):
        return d[n].reshape(w[n].shape)

    total = lax.psum(loss, ("x", "y", "c"))
    return (total, grad_x[None], *[shaped(grads, n) for n in ORDER], *[shaped(delta, n) for n in ORDER],
            *[shaped(new_m, n) for n in ORDER], *[shaped(new_v, n) for n in ORDER])
```

```python
import functools

import numpy as np
import jax
import jax.numpy as jnp
from jax import lax
from jax.experimental import pallas as pl
from jax.experimental.pallas import tpu as pltpu

f32 = jnp.float32
bf16 = jnp.bfloat16
HIGHEST = lax.Precision.HIGHEST
MESH = pl.DeviceIdType.MESH

D = 2048
CHUNK = 64
LEFT = 8
BAND = (LEFT + 1) * CHUNK
BANDP = 640
PADK = LEFT * CHUNK
NH = 16
HD = 64
ATT_W = NH * HD
SSD_W = 1024
NG = 2
NSTATE = 128
GW = SSD_W // NG
XBC = SSD_W + 2 * NG * NSTATE
N_REL = 320
REL_CLIP = 256
FFN = 5632
NSH = 4
FSH = FFN // NSH
IN_COLS = 5648
IN_SH = IN_COLS // NSH
IN_SHP = 1536
IN_A = 3 * ATT_W
IN_B = 2688
IN_P = IN_A + IN_B
EPS = 1e-6
N_DEV = 8

ADAM_LR = 0.001
ADAM_B1 = 0.9
ADAM_B2 = 0.999
ADAM_EPS = 1e-08
ADAM_WD = 0.01
ADAM_STEP = 10

VMEM_LIMIT = 56 * 1024 * 1024


def _params(sem):
    return pltpu.CompilerParams(dimension_semantics=sem, vmem_limit_bytes=VMEM_LIMIT)


def _sds(shape, dtype):
    return jax.ShapeDtypeStruct(shape, dtype)


def _fold8(v):
    r, w = v.shape
    return jnp.sum(v.reshape(r // 8, 8, w), axis=0)


def _sigmoid(v):
    return 1.0 / (1.0 + jnp.exp(-v))


def _softplus(v):
    return jnp.maximum(v, 0.0) + jnp.log(1.0 + jnp.exp(-jnp.abs(v)))


def _dot(a, b, ta=False, tb=False):
    dn = (((0 if ta else 1,), (1 if tb else 0,)), ((), ()))
    return lax.dot_general(a.astype(bf16), b.astype(bf16), dn, preferred_element_type=f32)


def _dep_args(dep, ngrid):
    if dep is None:
        return [], []
    return [pl.BlockSpec((8, 128), lambda *_: (0, 0))], [dep]


def _dot01(a, b, ta=False, tb=False, exact="b"):
    dn = (((0 if ta else 1,), (1 if tb else 0,)), ((), ()))
    x = a if exact == "b" else b
    hi = x.astype(bf16)
    r = x - hi.astype(f32)
    mid = r.astype(bf16)
    lo = (r - mid.astype(f32)).astype(bf16)
    if exact == "b":
        m = b.astype(bf16)
        return sum(lax.dot_general(p, m, dn, preferred_element_type=f32) for p in (hi, mid, lo))
    m = a.astype(bf16)
    return sum(lax.dot_general(m, p, dn, preferred_element_type=f32) for p in (hi, mid, lo))


def _matmul(name, a, b, *, grid, a_spec, b_spec, o_spec, o_shape, o_dtype, acc_shape, ta=False, tb=False, dep=None):
    nk = grid[2]
    dep_specs, dep_ops = _dep_args(dep, 3)

    def body(a_ref, b_ref, *rest):
        o_ref, acc_ref = rest[-2:]
        p = _dot(a_ref[...], b_ref[...], ta, tb)
        if nk == 1:
            o_ref[...] = p.astype(o_ref.dtype)
        else:
            k = pl.program_id(2)

            @pl.when(k == 0)
            def _():
                acc_ref[...] = p

            @pl.when(jnp.logical_and(k > 0, k < nk - 1))
            def _():
                acc_ref[...] += p

            @pl.when(k == nk - 1)
            def _():
                o_ref[...] = (acc_ref[...] + p).astype(o_ref.dtype)

    return pl.pallas_call(
        body, name=name, grid=grid, in_specs=[a_spec, b_spec] + dep_specs, out_specs=o_spec,
        out_shape=_sds(o_shape, o_dtype), scratch_shapes=[pltpu.VMEM(acc_shape if nk > 1 else (8, 128), f32)],
        compiler_params=_params(("parallel", "parallel", "arbitrary")),
    )(a, b, *dep_ops)


def _mm_nn_fullk(name, a, b, tm, tn, o_dtype, n=None):
    m, k = a.shape
    n = b.shape[1] if n is None else n
    return _matmul(name, a, b, grid=(m // tm, n // tn, 1),
                   a_spec=pl.BlockSpec((tm, k), lambda i, j, kk: (i, 0)),
                   b_spec=pl.BlockSpec((k, tn), lambda i, j, kk: (0, j)),
                   o_spec=pl.BlockSpec((tm, tn), lambda i, j, kk: (i, j)),
                   o_shape=(m, n), o_dtype=o_dtype, acc_shape=(tm, tn))


def _mm_nt(name, a, b, tm, tn, tk, o_dtype, dep=None):
    m, k = a.shape
    n = b.shape[0]
    return _matmul(name, a, b, grid=(m // tm, n // tn, k // tk), tb=True, dep=dep,
                   a_spec=pl.BlockSpec((tm, tk), lambda i, j, kk: (i, kk)),
                   b_spec=pl.BlockSpec((tn, tk), lambda i, j, kk: (j, kk)),
                   o_spec=pl.BlockSpec((tm, tn), lambda i, j, kk: (i, j)),
                   o_shape=(m, n), o_dtype=o_dtype, acc_shape=(tm, tn))


def _mm_tn(name, a, b, tm, tn, tk, o_dtype):
    k, m = a.shape
    n = b.shape[1]
    return _matmul(name, a, b, grid=(m // tm, n // tn, k // tk), ta=True,
                   a_spec=pl.BlockSpec((tk, tm), lambda i, j, kk: (kk, i)),
                   b_spec=pl.BlockSpec((tk, tn), lambda i, j, kk: (kk, j)),
                   o_spec=pl.BlockSpec((tm, tn), lambda i, j, kk: (i, j)),
                   o_shape=(m, n), o_dtype=o_dtype, acc_shape=(tm, tn))


def _ffn_up(h2b, wg4, wu4, tm):
    s = h2b.shape[0]

    def body(h_ref, wg_ref, wu_ref, g_ref, u_ref, a_ref):
        h = h_ref[...]
        g = _dot(h, wg_ref[...])
        u = _dot(h, wu_ref[...])
        g_ref[...] = g
        u_ref[...] = u
        a_ref[...] = (g * _sigmoid(g) * u).astype(bf16)

    wspec = pl.BlockSpec((None, D, FSH), lambda k, i: (k, 0, 0))
    ospec = pl.BlockSpec((tm, FSH), lambda k, i: (i, k))
    return pl.pallas_call(
        body, name="ffn_up", grid=(NSH, s // tm),
        in_specs=[pl.BlockSpec((tm, D), lambda k, i: (i, 0)), wspec, wspec],
        out_specs=[ospec, ospec, ospec],
        out_shape=[_sds((s, FFN), f32), _sds((s, FFN), f32), _sds((s, FFN), bf16)],
        compiler_params=_params(("parallel", "parallel")),
    )(h2b, wg4, wu4)


def _ffn_down(act, wd4, tm):
    s = act.shape[0]
    return _matmul("ffn_down", act, wd4, grid=(s // tm, 1, NSH),
                   a_spec=pl.BlockSpec((tm, FSH), lambda i, j, k: (i, k)),
                   b_spec=pl.BlockSpec((None, FSH, D), lambda i, j, k: (k, 0, 0)),
                   o_spec=pl.BlockSpec((tm, D), lambda i, j, k: (i, 0)),
                   o_shape=(s, D), o_dtype=f32, acc_shape=(tm, D))


def _ffn_dact(dffn, wd4, gate, up, tm, dep=None):
    s = dffn.shape[0]
    dep_specs, dep_ops = _dep_args(dep, 2)

    def body(d_ref, w_ref, g_ref, u_ref, *rest):
        dg_ref, du_ref = rest[-2:]
        dact = _dot(d_ref[...], w_ref[...], tb=True)
        g = g_ref[...]
        sg = _sigmoid(g)
        dg_ref[...] = (dact * u_ref[...] * (sg * (1.0 + g * (1.0 - sg)))).astype(bf16)
        du_ref[...] = (dact * (g * sg)).astype(bf16)

    blk = pl.BlockSpec((tm, FSH), lambda k, i: (i, k))
    return pl.pallas_call(
        body, name="ffn_dact", grid=(NSH, s // tm),
        in_specs=[pl.BlockSpec((tm, D), lambda k, i: (i, 0)), pl.BlockSpec((None, FSH, D), lambda k, i: (k, 0, 0)), blk, blk] + dep_specs,
        out_specs=[blk, blk], out_shape=[_sds((s, FFN), bf16), _sds((s, FFN), bf16)],
        compiler_params=_params(("parallel", "parallel")),
    )(dffn, wd4, gate, up, *dep_ops)


def _ffn_dh(dgate, dup, wg4, wu4, tm, dep=None):
    s = dgate.shape[0]
    dep_specs, dep_ops = _dep_args(dep, 2)

    def body(dg_ref, du_ref, wg_ref, wu_ref, *rest):
        o_ref, acc_ref = rest[-2:]
        k = pl.program_id(1)
        p = _dot(dg_ref[...], wg_ref[...], tb=True) + _dot(du_ref[...], wu_ref[...], tb=True)

        @pl.when(k == 0)
        def _():
            acc_ref[...] = p

        @pl.when(jnp.logical_and(k > 0, k < NSH - 1))
        def _():
            acc_ref[...] += p

        @pl.when(k == NSH - 1)
        def _():
            o_ref[...] = acc_ref[...] + p

    aspec = pl.BlockSpec((tm, FSH), lambda i, k: (i, k))
    wspec = pl.BlockSpec((None, D, FSH), lambda i, k: (k, 0, 0))
    return pl.pallas_call(
        body, name="ffn_dh", grid=(s // tm, NSH), in_specs=[aspec, aspec, wspec, wspec] + dep_specs,
        out_specs=pl.BlockSpec((tm, D), lambda i, k: (i, 0)), out_shape=_sds((s, D), f32),
        scratch_shapes=[pltpu.VMEM((tm, D), f32)], compiler_params=_params(("parallel", "arbitrary")),
    )(dgate, dup, wg4, wu4, *dep_ops)


def _grad_cols4(name, h, dy, tm, tk):
    s = h.shape[0]
    return _matmul(name, h, dy, grid=(NSH, D // tm, s // tk), ta=True,
                   a_spec=pl.BlockSpec((tk, tm), lambda k, i, kk: (kk, i)),
                   b_spec=pl.BlockSpec((tk, FSH), lambda k, i, kk: (kk, k)),
                   o_spec=pl.BlockSpec((None, tm, FSH), lambda k, i, kk: (k, i, 0)),
                   o_shape=(NSH, D, FSH), o_dtype=bf16, acc_shape=(tm, FSH))


def _grad_wdown4(act, dffn, tn, tk):
    s = act.shape[0]
    return _matmul("grad_w_down", act, dffn, grid=(NSH, D // tn, s // tk), ta=True,
                   a_spec=pl.BlockSpec((tk, FSH), lambda k, j, kk: (kk, k)),
                   b_spec=pl.BlockSpec((tk, tn), lambda k, j, kk: (kk, j)),
                   o_spec=pl.BlockSpec((None, FSH, tn), lambda k, j, kk: (k, 0, j)),
                   o_shape=(NSH, FSH, D), o_dtype=bf16, acc_shape=(FSH, tn))


def _row_spec(w):
    return pl.BlockSpec((1, w), lambda i: (0, 0))


def _tile_spec(tm, w, col=0):
    return pl.BlockSpec((tm, w), lambda i: (i, col))


def _norm_mod(name, x, g, sc, sh, tm):
    s = x.shape[0]

    def body(x_ref, g_ref, sc_ref, sh_ref, o_ref):
        xv = x_ref[...]
        r = lax.rsqrt(jnp.mean(xv * xv, axis=-1, keepdims=True) + EPS)
        o_ref[...] = (xv * r * g_ref[...] * (1.0 + sc_ref[...]) + sh_ref[...]).astype(bf16)

    return pl.pallas_call(
        body, name=name, grid=(s // tm,), in_specs=[_tile_spec(tm, D), _row_spec(D), _row_spec(D), _row_spec(D)],
        out_specs=_tile_spec(tm, D), out_shape=_sds((s, D), bf16), compiler_params=_params(("parallel",)),
    )(x, g, sc, sh)


def _resid_norm_mod(x, gt, mix, g, sc, sh, tm):
    s = x.shape[0]

    def body(x_ref, gt_ref, m_ref, g_ref, sc_ref, sh_ref, x2_ref, h_ref):
        xv = x_ref[...] + gt_ref[...] * m_ref[...]
        x2_ref[...] = xv
        r = lax.rsqrt(jnp.mean(xv * xv, axis=-1, keepdims=True) + EPS)
        h_ref[...] = (xv * r * g_ref[...] * (1.0 + sc_ref[...]) + sh_ref[...]).astype(bf16)

    return pl.pallas_call(
        body, name="resid_norm_mod", grid=(s // tm,),
        in_specs=[_tile_spec(tm, D), _row_spec(D), _tile_spec(tm, D), _row_spec(D), _row_spec(D), _row_spec(D)],
        out_specs=[_tile_spec(tm, D), _tile_spec(tm, D)], out_shape=[_sds((s, D), f32), _sds((s, D), bf16)],
        compiler_params=_params(("parallel",)),
    )(x, gt, mix, g, sc, sh)


def _final_fwd_bwd(x2, ffn, gt2, g, tgt, tm):
    s = x2.shape[0]
    n = s // tm

    def body(x_ref, f_ref, gt_ref, g_ref, t_ref, dx_ref, df_ref, loss_ref, dg_ref, dgt_ref, a_loss, a_dg, a_dgt):
        i = pl.program_id(0)

        @pl.when(i == 0)
        def _():
            a_loss[...] = jnp.zeros_like(a_loss)
            a_dg[...] = jnp.zeros_like(a_dg)
            a_dgt[...] = jnp.zeros_like(a_dgt)

        fv = f_ref[...]
        gt = gt_ref[...]
        gv = g_ref[...]
        xv = x_ref[...] + gt * fv
        r = lax.rsqrt(jnp.mean(xv * xv, axis=-1, keepdims=True) + EPS)
        xh = xv * r
        e = xh * gv - t_ref[...]
        a_loss[...] += _fold8(e * e)
        dy = e * (1.0 / D)
        a_dg[...] += _fold8(dy * xh)
        t = dy * gv
        dx = r * (t - xh * jnp.mean(t * xh, axis=-1, keepdims=True))
        dx_ref[...] = dx
        a_dgt[...] += _fold8(dx * fv)
        df_ref[...] = (dx * gt).astype(bf16)

        @pl.when(i == n - 1)
        def _():
            tot = jnp.sum(jnp.sum(a_loss[...], axis=0, keepdims=True), axis=1, keepdims=True) * (0.5 / D)
            loss_ref[...] = jnp.broadcast_to(tot, (1, 128))
            dg_ref[...] = jnp.sum(a_dg[...], axis=0, keepdims=True)
            dgt_ref[...] = jnp.sum(a_dgt[...], axis=0, keepdims=True)

    return pl.pallas_call(
        body, name="final_fwd_bwd", grid=(n,),
        in_specs=[_tile_spec(tm, D), _tile_spec(tm, D), _row_spec(D), _row_spec(D), _tile_spec(tm, D)],
        out_specs=[_tile_spec(tm, D), _tile_spec(tm, D), _row_spec(128), _row_spec(D), _row_spec(D)],
        out_shape=[_sds((s, D), f32), _sds((s, D), bf16), _sds((1, 128), f32), _sds((1, D), f32), _sds((1, D), f32)],
        scratch_shapes=[pltpu.VMEM((8, D), f32)] * 3, compiler_params=_params(("arbitrary",)),
    )(x2, ffn, gt2, g, tgt)


def _norm_mod_bwd(name, dh, xin, g, sc, dres, tm, mix=None, gt=None):
    s = dh.shape[0]
    n = s // tm
    with_mix = mix is not None

    def body(*refs):
        if with_mix:
            dh_ref, x_ref, g_ref, sc_ref, dr_ref, m_ref, gt_ref, dx_ref, dm_ref, dsc_ref, dsh_ref, dg_ref, dgt_ref, a_sc, a_sh, a_g, a_gt = refs
        else:
            dh_ref, x_ref, g_ref, sc_ref, dr_ref, dx_ref, dsc_ref, dsh_ref, dg_ref, a_sc, a_sh, a_g = refs
        i = pl.program_id(0)

        @pl.when(i == 0)
        def _():
            a_sc[...] = jnp.zeros_like(a_sc)
            a_sh[...] = jnp.zeros_like(a_sh)
            a_g[...] = jnp.zeros_like(a_g)
            if with_mix:
                a_gt[...] = jnp.zeros_like(a_gt)

        dh = dh_ref[...]
        xv = x_ref[...]
        gv = g_ref[...]
        r = lax.rsqrt(jnp.mean(xv * xv, axis=-1, keepdims=True) + EPS)
        xh = xv * r
        a_sc[...] += _fold8(dh * xh * gv)
        a_sh[...] += _fold8(dh)
        dn = dh * (1.0 + sc_ref[...])
        a_g[...] += _fold8(dn * xh)
        t = dn * gv
        dx = dr_ref[...] + r * (t - xh * jnp.mean(t * xh, axis=-1, keepdims=True))
        dx_ref[...] = dx
        if with_mix:
            a_gt[...] += _fold8(dx * m_ref[...])
            dm_ref[...] = (dx * gt_ref[...]).astype(bf16)

        @pl.when(i == n - 1)
        def _():
            dsc_ref[...] = jnp.sum(a_sc[...], axis=0, keepdims=True)
            dsh_ref[...] = jnp.sum(a_sh[...], axis=0, keepdims=True)
            dg_ref[...] = jnp.sum(a_g[...], axis=0, keepdims=True)
            if with_mix:
                dgt_ref[...] = jnp.sum(a_gt[...], axis=0, keepdims=True)

    tile, row = _tile_spec(tm, D), _row_spec(D)
    if with_mix:
        ins, args = [tile, tile, row, row, tile, tile, row], (dh, xin, g, sc, dres, mix, gt)
        outs = [tile, tile, row, row, row, row]
        shapes = [_sds((s, D), f32), _sds((s, D), bf16)] + [_sds((1, D), f32)] * 4
        nacc = 4
    else:
        ins, args = [tile, tile, row, row, tile], (dh, xin, g, sc, dres)
        outs = [tile, row, row, row]
        shapes = [_sds((s, D), f32)] + [_sds((1, D), f32)] * 3
        nacc = 3
    return pl.pallas_call(
        body, name=name, grid=(n,), in_specs=ins, out_specs=outs, out_shape=shapes,
        scratch_shapes=[pltpu.VMEM((8, D), f32)] * nacc, compiler_params=_params(("arbitrary",)),
    )(*args)


def _mix_pre(att, y, proj2, g_att, g_ssd, tm):
    s = att.shape[0]

    def body(a_ref, y_ref, z_ref, ga_ref, gs_ref, o_ref):
        a = a_ref[...]
        ra = lax.rsqrt(jnp.mean(a * a, axis=-1, keepdims=True) + EPS)
        o_ref[:, 0:ATT_W] = (a * ra * ga_ref[...]).astype(bf16)
        z = z_ref[...]
        u = y_ref[...] * (z * _sigmoid(z))
        ru = lax.rsqrt(jnp.mean(u * u, axis=-1, keepdims=True) + EPS)
        o_ref[:, ATT_W:] = (u * ru * gs_ref[...]).astype(bf16)

    t = _tile_spec(tm, ATT_W)
    return pl.pallas_call(
        body, name="mix_pre", grid=(s // tm,), in_specs=[t, t, t, _row_spec(ATT_W), _row_spec(SSD_W)],
        out_specs=_tile_spec(tm, D), out_shape=_sds((s, D), bf16), compiler_params=_params(("parallel",)),
    )(att, y, proj2, g_att, g_ssd)


def _mix_pre_bwd(dmc, att, y, proj2, g_att, g_ssd, tm):
    s = att.shape[0]
    n = s // tm

    def body(da_ref, ds_ref, a_ref, y_ref, z_ref, ga_ref, gs_ref, datt_ref, dy_ref, dz_ref, dga_ref, dgs_ref, acc_a, acc_s):
        i = pl.program_id(0)

        @pl.when(i == 0)
        def _():
            acc_a[...] = jnp.zeros_like(acc_a)
            acc_s[...] = jnp.zeros_like(acc_s)

        a = a_ref[...]
        ra = lax.rsqrt(jnp.mean(a * a, axis=-1, keepdims=True) + EPS)
        ah = a * ra
        dan = da_ref[...]
        acc_a[...] += _fold8(dan * ah)
        t = dan * ga_ref[...]
        datt_ref[...] = (ra * (t - ah * jnp.mean(t * ah, axis=-1, keepdims=True))).astype(bf16)
        z = z_ref[...]
        yv = y_ref[...]
        sz = _sigmoid(z)
        sil = z * sz
        u = yv * sil
        ru = lax.rsqrt(jnp.mean(u * u, axis=-1, keepdims=True) + EPS)
        uh = u * ru
        dsn = ds_ref[...]
        acc_s[...] += _fold8(dsn * uh)
        t2 = dsn * gs_ref[...]
        du = ru * (t2 - uh * jnp.mean(t2 * uh, axis=-1, keepdims=True))
        dy_ref[...] = du * sil
        dz_ref[...] = (du * yv * (sz * (1.0 + z * (1.0 - sz)))).astype(bf16)

        @pl.when(i == n - 1)
        def _():
            dga_ref[...] = jnp.sum(acc_a[...], axis=0, keepdims=True)
            dgs_ref[...] = jnp.sum(acc_s[...], axis=0, keepdims=True)

    t = _tile_spec(tm, ATT_W)
    row = _row_spec(ATT_W)
    return pl.pallas_call(
        body, name="mix_pre_bwd", grid=(n,),
        in_specs=[_tile_spec(tm, ATT_W, 0), _tile_spec(tm, ATT_W, 1), t, t, t, row, row],
        out_specs=[t, t, t, row, row],
        out_shape=[_sds((s, ATT_W), bf16), _sds((s, SSD_W), f32), _sds((s, SSD_W), bf16), _sds((1, ATT_W), f32), _sds((1, SSD_W), f32)],
        scratch_shapes=[pltpu.VMEM((8, ATT_W), f32)] * 2, compiler_params=_params(("arbitrary",)),
    )(dmc, dmc, att, y, proj2, g_att, g_ssd)


ATT_GROUP = 4
ATT_GROUP_FWD = 4


def _pair_rows(qc):
    two = jnp.concatenate([qc, qc], axis=0)
    r = lax.broadcasted_iota(jnp.int32, (2 * CHUNK, 128), 0)
    l = lax.broadcasted_iota(jnp.int32, (2 * CHUNK, 128), 1)
    return jnp.where((r < CHUNK) == (l < HD), two, jnp.zeros_like(two))


def _scaled(q):
    return q * jnp.asarray(HD ** -0.5, q.dtype)


def _pair_scores(wt, kb, bias, r0, masked):
    sc = lax.dot_general(wt, kb, (((1,), (1,)), ((), ())), preferred_element_type=f32) + bias
    if not masked:
        return sc
    kidx = lax.broadcasted_iota(jnp.int32, sc.shape, 1)
    return jnp.where(r0 + kidx >= PADK, sc, -jnp.inf)


def _softmax(sc, axis):
    e = jnp.exp(sc - jnp.max(sc, axis=axis, keepdims=True))
    return e * (1.0 / jnp.sum(e, axis=axis, keepdims=True))


def _chunk_loops(nc, group, per_trip):
    n_masked = min(-(-LEFT // per_trip), nc // per_trip)

    def run(masked):
        def step(g, carry):
            group(g, masked)
            return carry
        return step

    lax.fori_loop(0, n_masked, run(True), 0)
    lax.fori_loop(n_masked, nc // per_trip, run(False), 0)


def _pair_diag(r):
    lane = lax.broadcasted_iota(jnp.int32, (CHUNK, 128), 1)
    return jnp.where(lane < HD, r[0:CHUNK], r[CHUNK:])


def _pad_keys(k_ref, kp, s):
    kp[0:PADK, :] = jnp.zeros((PADK, 128), bf16)
    kp[PADK:PADK + s, :] = k_ref[...]
    kp[PADK + s:, :] = jnp.zeros((CHUNK, 128), bf16)


def _attn_fwd(qkv, bias2):
    s = qkv.shape[0]
    nc = s // CHUNK
    npair = NH // 2

    def body(q_ref, k_ref, v_ref, b_ref, o_ref, kp, vp):
        _pad_keys(k_ref, kp, s)
        _pad_keys(v_ref, vp, s)

        def group(g, masked):
            r0s = [pl.multiple_of((g * ATT_GROUP_FWD + u) * CHUNK, CHUNK) for u in range(ATT_GROUP_FWD)]
            scs = [_pair_scores(_pair_rows(_scaled(q_ref[pl.ds(r0, CHUNK), :])), kp[pl.ds(r0, BANDP), :], b_ref[...], r0, masked)
                   for r0 in r0s]
            ps = [_softmax(sc, -1).astype(bf16) for sc in scs]
            for r0, p in zip(r0s, ps):
                o_ref[pl.ds(r0, CHUNK), :] = _pair_diag(jnp.dot(p, vp[pl.ds(r0, BANDP), :], preferred_element_type=f32))

        _chunk_loops(nc, group, ATT_GROUP_FWD)

    return pl.pallas_call(
        body, name="attn_fwd", grid=(npair,),
        in_specs=[pl.BlockSpec((s, 128), lambda p: (0, p)), pl.BlockSpec((s, 128), lambda p: (0, npair + p)),
                  pl.BlockSpec((s, 128), lambda p: (0, 2 * npair + p)), pl.BlockSpec((None, 2 * CHUNK, BANDP), lambda p: (p, 0, 0))],
        out_specs=pl.BlockSpec((s, 128), lambda p: (0, p)), out_shape=_sds((s, ATT_W), f32),
        scratch_shapes=[pltpu.VMEM((PADK + s + CHUNK, 128), bf16)] * 2, compiler_params=_params(("parallel",)),
    )(qkv, qkv, qkv, bias2)


def _attn_bwd(qkv, datt, bias2, bias2t):
    s = qkv.shape[0]
    nc = s // CHUNK
    npair = NH // 2
    rows = PADK + s + CHUNK
    nt = (((1,), (1,)), ((), ()))

    def body(q_ref, k_ref, v_ref, do_ref, b_ref, bt_ref, dq_ref, dk_ref, dv_ref, g_ref, kp, vp, dkp, dvp):
        _pad_keys(k_ref, kp, s)
        _pad_keys(v_ref, vp, s)
        dkp[...] = jnp.zeros_like(dkp)
        dvp[...] = jnp.zeros_like(dvp)
        g_ref[...] = jnp.zeros_like(g_ref)

        def group(g, masked):
            r0s = [pl.multiple_of((g * ATT_GROUP + u) * CHUNK, CHUNK) for u in range(ATT_GROUP)]
            wts = [_pair_rows(_scaled(q_ref[pl.ds(r0, CHUNK), :])) for r0 in r0s]
            dos = [_pair_rows(do_ref[pl.ds(r0, CHUNK), :]) for r0 in r0s]
            scs = [_pair_scores(wt, kp[pl.ds(r0, BANDP), :], b_ref[...], r0, masked) for wt, r0 in zip(wts, r0s)]
            dps = [lax.dot_general(do, vp[pl.ds(r0, BANDP), :], nt, preferred_element_type=f32) for do, r0 in zip(dos, r0s)]
            scts, dpts = [], []
            for wt, do, r0 in zip(wts, dos, r0s):
                sct = lax.dot_general(kp[pl.ds(r0, BANDP), :], wt, nt, preferred_element_type=f32) + bt_ref[...]
                if masked:
                    kidx = lax.broadcasted_iota(jnp.int32, sct.shape, 0)
                    sct = jnp.where(r0 + kidx >= PADK, sct, -jnp.inf)
                scts.append(sct)
                dpts.append(lax.dot_general(vp[pl.ds(r0, BANDP), :], do, nt, preferred_element_type=f32))
            for r0, sc, dp in zip(r0s, scs, dps):
                p = _softmax(sc, -1)
                ds = p * (dp - jnp.sum(p * dp, axis=-1, keepdims=True))
                g_ref[...] += ds
                dq = jnp.dot(ds.astype(bf16), kp[pl.ds(r0, BANDP), :], preferred_element_type=f32)
                dq_ref[pl.ds(r0, CHUNK), :] = (_pair_diag(dq) * (HD ** -0.5)).astype(bf16)
            for r0, wt, do, sct, dpt in zip(r0s, wts, dos, scts, dpts):
                pt = _softmax(sct, 0)
                dst = pt * (dpt - jnp.sum(pt * dpt, axis=0, keepdims=True))
                dkp[pl.ds(r0, BANDP), :] += jnp.dot(dst.astype(bf16), wt, preferred_element_type=f32)
                dvp[pl.ds(r0, BANDP), :] += jnp.dot(pt.astype(bf16), do, preferred_element_type=f32)

        _chunk_loops(nc, group, ATT_GROUP)
        dk_ref[...] = dkp[PADK:PADK + s, :].astype(bf16)
        dv_ref[...] = dvp[PADK:PADK + s, :].astype(bf16)

    col = lambda off: pl.BlockSpec((s, 128), lambda p: (0, off + p))
    return pl.pallas_call(
        body, name="attn_bwd", grid=(npair,),
        in_specs=[col(0), col(npair), col(2 * npair), col(0), pl.BlockSpec((None, 2 * CHUNK, BANDP), lambda p: (p, 0, 0)),
                  pl.BlockSpec((None, BANDP, 2 * CHUNK), lambda p: (p, 0, 0))],
        out_specs=[col(0), col(0), col(0), pl.BlockSpec((None, 2 * CHUNK, BANDP), lambda p: (p, 0, 0))],
        out_shape=[_sds((s, ATT_W), bf16)] * 3 + [_sds((npair, 2 * CHUNK, BANDP), f32)],
        scratch_shapes=[pltpu.VMEM((rows, 128), bf16)] * 2 + [pltpu.VMEM((rows, 128), f32)] * 2,
        compiler_params=_params(("parallel",)),
    )(qkv, qkv, qkv, datt, bias2, bias2t)


def _rel_tables():
    onehot = np.zeros((BANDP, N_REL), np.float32)
    for j in range(BAND + CHUNK - 1):
        o = j - (CHUNK - 1)
        onehot[j, int(np.clip(PADK - o, -(CHUNK - 1), REL_CLIP)) + CHUNK - 1] = 1.0
    return onehot, np.ascontiguousarray(np.eye(CHUNK, dtype=np.float32)[::-1])


def _expand_bias(rel):
    ext = jnp.concatenate([jnp.broadcast_to(rel[:, N_REL - 1:], (NH, N_REL - 1)), rel[:, ::-1],
                           jnp.zeros((NH, BANDP - BAND + 1), f32)], axis=1)
    band = jnp.stack([ext[:, CHUNK - 1 - q:CHUNK - 1 - q + BANDP] for q in range(CHUNK)], axis=1)
    band = jnp.where(np.arange(BANDP) < BAND, band, -jnp.inf)
    return band.reshape(NH // 2, 2 * CHUNK, BANDP)


def _rel_bias_grad(gband):
    def body(g_ref, m_ref, flip_ref, o_ref, d2):
        for h in range(NH):
            rev = jnp.dot(flip_ref[...], g_ref[h], precision=HIGHEST, preferred_element_type=f32)
            rolled = pltpu.roll(rev, 0, 1, stride=1, stride_axis=0)
            d2[h:h + 1, :] = jnp.sum(rolled, axis=0, keepdims=True)
        o_ref[...] = jnp.dot(d2[...], m_ref[...], precision=HIGHEST, preferred_element_type=f32)

    onehot, flip = _rel_tables()
    return pl.pallas_call(
        body, name="rel_bias_grad", out_shape=_sds((NH, N_REL), f32), scratch_shapes=[pltpu.VMEM((NH, BANDP), f32)],
    )(gband, jnp.asarray(onehot), jnp.asarray(flip))


XBC_BLK = 512
XBC_COL0 = SSD_W // XBC_BLK
DT_COL = (SSD_W + XBC) // 128


def _conv_taps(ext, w_ref, b_ref, tm):
    n = ext.shape[0]
    pre = w_ref[3:4, :] * ext + b_ref[...]
    for j in range(3):
        pre = pre + w_ref[j:j + 1, :] * pltpu.roll(ext, 3 - j, 0)
    return pre


def _ssd_conv(proj2, conv_w, conv_b, tm):
    s = proj2.shape[0]
    nb = XBC // XBC_BLK

    def body(x_ref, p_ref, w_ref, b_ref, o_ref):
        i = pl.program_id(1)
        prev = jnp.where(i > 0, p_ref[...], 0.0)
        ext = jnp.concatenate([prev, x_ref[...]], axis=0)
        pre = _conv_taps(ext, w_ref, b_ref, tm)[8:8 + tm]
        o_ref[...] = pre * _sigmoid(pre)

    return pl.pallas_call(
        body, name="ssd_conv", grid=(nb, s // tm),
        in_specs=[pl.BlockSpec((tm, XBC_BLK), lambda j, i: (i, XBC_COL0 + j)),
                  pl.BlockSpec((8, XBC_BLK), lambda j, i: (jnp.maximum(i * (tm // 8) - 1, 0), XBC_COL0 + j)),
                  pl.BlockSpec((4, XBC_BLK), lambda j, i: (0, j)), pl.BlockSpec((1, XBC_BLK), lambda j, i: (0, j))],
        out_specs=pl.BlockSpec((tm, XBC_BLK), lambda j, i: (i, j)), out_shape=_sds((s, XBC), f32),
        compiler_params=_params(("parallel", "parallel")),
    )(proj2, proj2, conv_w, conv_b)


def _ssd_conv_bwd(dxbc, proj2, conv_w, conv_b, tm):
    s = proj2.shape[0]
    nb = XBC // XBC_BLK
    n = s // tm
    last8 = s // 8 - 1

    def body(x_ref, xp_ref, xn_ref, d_ref, dn_ref, w_ref, b_ref, o_ref, dw_ref, db_ref):
        i = pl.program_id(1)

        @pl.when(i == 0)
        def _():
            dw_ref[...] = jnp.zeros_like(dw_ref)
            db_ref[...] = jnp.zeros_like(db_ref)

        prev = jnp.where(i > 0, xp_ref[...], 0.0)
        ext = jnp.concatenate([prev, x_ref[...], xn_ref[...]], axis=0)
        pre = _conv_taps(ext, w_ref, b_ref, tm)
        sg = _sigmoid(pre)
        dnext = jnp.where(i < n - 1, dn_ref[...], 0.0)
        dext = jnp.concatenate([jnp.zeros((8, XBC_BLK), f32), d_ref[...], dnext], axis=0)
        dpre = dext * (sg * (1.0 + pre * (1.0 - sg)))
        rows = tm + 16
        dx = w_ref[3:4, :] * dpre
        for j in range(3):
            dx = dx + w_ref[j:j + 1, :] * pltpu.roll(dpre, rows - (3 - j), 0)
        o_ref[...] = dx[8:8 + tm].astype(bf16)
        dcur = dpre[8:8 + tm]
        db_ref[...] += jnp.sum(dcur, axis=0, keepdims=True)
        dw_ref[3:4, :] += jnp.sum(dcur * ext[8:8 + tm], axis=0, keepdims=True)
        for j in range(3):
            dw_ref[j:j + 1, :] += jnp.sum(dcur * pltpu.roll(ext, 3 - j, 0)[8:8 + tm], axis=0, keepdims=True)

    xcol = lambda j: XBC_COL0 + j
    return pl.pallas_call(
        body, name="ssd_conv_bwd", grid=(nb, n),
        in_specs=[pl.BlockSpec((tm, XBC_BLK), lambda j, i: (i, xcol(j))),
                  pl.BlockSpec((8, XBC_BLK), lambda j, i: (jnp.maximum(i * (tm // 8) - 1, 0), xcol(j))),
                  pl.BlockSpec((8, XBC_BLK), lambda j, i: (jnp.minimum((i + 1) * (tm // 8), last8), xcol(j))),
                  pl.BlockSpec((tm, XBC_BLK), lambda j, i: (i, j)),
                  pl.BlockSpec((8, XBC_BLK), lambda j, i: (jnp.minimum((i + 1) * (tm // 8), last8), j)),
                  pl.BlockSpec((4, XBC_BLK), lambda j, i: (0, j)), pl.BlockSpec((1, XBC_BLK), lambda j, i: (0, j))],
        out_specs=[pl.BlockSpec((tm, XBC_BLK), lambda j, i: (i, j)), pl.BlockSpec((4, XBC_BLK), lambda j, i: (0, j)),
                   pl.BlockSpec((1, XBC_BLK), lambda j, i: (0, j))],
        out_shape=[_sds((s, XBC), bf16), _sds((4, XBC), f32), _sds((1, XBC), f32)],
        compiler_params=_params(("parallel", "arbitrary")),
    )(proj2, proj2, proj2, dxbc, dxbc, conv_w, conv_b)


def _ssd_consts():
    ex = np.zeros((128, SSD_W), np.float32)
    for h in range(NH):
        ex[h, h * HD:(h + 1) * HD] = 1.0
    sel = np.zeros((8, 128), np.float32)
    for h in range(NH):
        sel[h // 2, h] = 1.0
    par = np.zeros((128, 128), np.float32)
    for r in range(128):
        for h in range(NH):
            par[r, h] = 1.0 if (h % 2) == (r // 64) else 0.0
    ones_blk = np.zeros((128, 128), np.float32)
    for r in range(128):
        ones_blk[r, (r // 64) * 64:(r // 64) * 64 + 64] = 1.0
    return ex, np.ascontiguousarray(ex.T), sel, par, ones_blk


def _ssd_common(xbc_ref, dtr_ref, a_ref, dtb_ref, ex_ref, sel_ref, par_ref):
    xs = xbc_ref[:, 0:SSD_W]
    dt = _softplus(dtr_ref[...] + dtb_ref[...])
    adt = dt * a_ref[...]
    r_i = lax.broadcasted_iota(jnp.int32, (CHUNK, CHUNK), 0)
    c_i = lax.broadcasted_iota(jnp.int32, (CHUNK, CHUNK), 1)
    tril = (r_i >= c_i).astype(f32)
    cs = _dot01(tril, adt, exact="a")
    cs2 = jnp.concatenate([cs, cs], axis=0) * par_ref[...]
    cstp = _dot01(sel_ref[...], cs2, tb=True, exact="a")
    ex = ex_ref[...]
    dt_full = _dot01(dt, ex)
    cs_full = _dot01(cs, ex)
    return xs, dt, cs, cstp, dt_full, cs_full


def _pair_mask():
    l_i = lax.broadcasted_iota(jnp.int32, (CHUNK, 128), 0)
    lane = lax.broadcasted_iota(jnp.int32, (CHUNK, 128), 1)
    return l_i >= (lane % CHUNK), lane < HD


def _block_diag(xp, first):
    z = jnp.zeros_like(xp)
    return jnp.concatenate([jnp.where(first, xp, z), jnp.where(first, z, xp)], axis=0)


def _ssd_fwd(xbc, proj2, a_row, dtb_row, dsk_full):
    s = xbc.shape[0]
    nc = s // CHUNK
    ex, ext, sel, par, ones_blk = _ssd_consts()

    def body(xbc_ref, dtr_ref, a_ref, dtb_ref, dsk_ref, ex_ref, sel_ref, par_ref, y_ref, hs_ref, hst):
        @pl.when(pl.program_id(0) == 0)
        def _():
            hst[...] = jnp.zeros_like(hst)

        hs_ref[...] = hst[...]
        xs, dt, cs, cstp, dt_full, cs_full = _ssd_common(xbc_ref, dtr_ref, a_ref, dtb_ref, ex_ref, sel_ref, par_ref)
        cs_last = cs_full[CHUNK - 1:CHUNK, :]
        xdt = xs * dt_full
        causal, first = _pair_mask()
        for g in range(NG):
            gl = slice(g * GW, (g + 1) * GW)
            bg = xbc_ref[:, SSD_W + g * NSTATE:SSD_W + (g + 1) * NSTATE].astype(bf16)
            cg = xbc_ref[:, SSD_W + NG * NSTATE + g * NSTATE:SSD_W + NG * NSTATE + (g + 1) * NSTATE].astype(bf16)
            cb2 = lax.dot_general(cg, jnp.concatenate([bg, bg], axis=0), (((1,), (1,)), ((), ())), preferred_element_type=f32)
            hg = hst[g]
            y0 = jnp.dot(cg, hg.astype(bf16), preferred_element_type=f32)
            yoff = jnp.exp(cs_full[:, gl]) * y0
            for j in range(GW // 128):
                pair = g * (GW // 128) + j
                pl_ = slice(pair * 128, (pair + 1) * 128)
                seg = jnp.exp(jnp.where(causal, cs_full[:, pl_] - cstp[pair:pair + 1, :], -jnp.inf))
                m = (cb2 * seg).astype(bf16)
                yd = jnp.dot(m, _block_diag(xdt[:, pl_].astype(bf16), first), preferred_element_type=f32)
                y_ref[:, pl_] = yd + yoff[:, j * 128:(j + 1) * 128] + xs[:, pl_] * dsk_ref[:, pl_]
            xdec = (xdt[:, gl] * jnp.exp(cs_last[:, gl] - cs_full[:, gl])).astype(bf16)
            st = lax.dot_general(bg, xdec, (((0,), (0,)), ((), ())), preferred_element_type=f32)
            hst[g] = jnp.exp(cs_last[:, gl]) * hg + st

    const = lambda shape: pl.BlockSpec(shape, lambda c: tuple(0 for _ in shape))
    return pl.pallas_call(
        body, name="ssd_fwd", grid=(nc,),
        in_specs=[pl.BlockSpec((CHUNK, XBC), lambda c: (c, 0)), pl.BlockSpec((CHUNK, 128), lambda c: (c, DT_COL)),
                  const((1, 128)), const((1, 128)), const((1, SSD_W)), const((128, SSD_W)), const((8, 128)), const((128, 128))],
        out_specs=[pl.BlockSpec((CHUNK, SSD_W), lambda c: (c, 0)), pl.BlockSpec((None, NG, NSTATE, GW), lambda c: (c, 0, 0, 0))],
        out_shape=[_sds((s, SSD_W), f32), _sds((nc, NG, NSTATE, GW), f32)],
        scratch_shapes=[pltpu.VMEM((NG, NSTATE, GW), f32)], compiler_params=_params(("arbitrary",)),
    )(xbc, proj2, a_row, dtb_row, dsk_full, jnp.asarray(ex), jnp.asarray(sel), jnp.asarray(par))


def _ssd_bwd(xbc, proj2, dy, hsave, a_row, dtb_row, dsk_full):
    s = xbc.shape[0]
    nc = s // CHUNK
    ex, ext, sel, par, ones_blk = _ssd_consts()

    def body(xbc_ref, dtr_ref, dy_ref, hs_ref, a_ref, dtb_ref, dsk_ref, ex_ref, ext_ref, sel_ref, par_ref, ob_ref,
             dxbc_ref, ddtr_ref, dd_ref, da_ref, ddtb_ref, dh, a_dd, a_da, a_dtb, dcs_lane, dcs_b, dxdt):
        step = pl.program_id(0)

        @pl.when(step == 0)
        def _():
            dh[...] = jnp.zeros_like(dh)
            a_dd[...] = jnp.zeros_like(a_dd)
            a_da[...] = jnp.zeros_like(a_da)
            a_dtb[...] = jnp.zeros_like(a_dtb)

        xs, dt, cs, cstp, dt_full, cs_full = _ssd_common(xbc_ref, dtr_ref, a_ref, dtb_ref, ex_ref, sel_ref, par_ref)
        cs_last = cs_full[CHUNK - 1:CHUNK, :]
        xdt = xs * dt_full
        dyv = dy_ref[...]
        a_dd[...] += _fold8(dyv * xs)
        causal, first = _pair_mask()
        ones_l = jnp.ones((CHUNK, 128), f32)
        for g in range(NG):
            gl = slice(g * GW, (g + 1) * GW)
            bcol = slice(SSD_W + g * NSTATE, SSD_W + (g + 1) * NSTATE)
            ccol = slice(SSD_W + NG * NSTATE + g * NSTATE, SSD_W + NG * NSTATE + (g + 1) * NSTATE)
            bg = xbc_ref[:, bcol].astype(bf16)
            cg = xbc_ref[:, ccol].astype(bf16)
            bg2 = jnp.concatenate([bg, bg], axis=0)
            cb2 = lax.dot_general(cg, bg2, (((1,), (1,)), ((), ())), preferred_element_type=f32)
            hg = hs_ref[g]
            hgb = hg.astype(bf16)
            dhg = dh[g]
            dhgb = dhg.astype(bf16)
            eg = jnp.exp(cs_full[:, gl])
            dec = jnp.exp(cs_last[:, gl] - cs_full[:, gl])
            gam = jnp.exp(cs_last[:, gl])
            dyg = dyv[:, gl]
            xdt_g = xdt[:, gl]
            y0 = jnp.dot(cg, hgb, preferred_element_type=f32)
            dy0 = (eg * dyg).astype(bf16)
            dcm = lax.dot_general(dy0, hgb, (((1,), (1,)), ((), ())), preferred_element_type=f32)
            dh_prev = gam * dhg + lax.dot_general(cg, dy0, (((0,), (0,)), ((), ())), preferred_element_type=f32)
            dgam = jnp.sum(dhg * hg, axis=0, keepdims=True) * gam
            dxdec = jnp.dot(bg, dhgb, preferred_element_type=f32)
            dbm = lax.dot_general((xdt_g * dec).astype(bf16), dhgb, (((1,), (1,)), ((), ())), preferred_element_type=f32)
            t = dxdec * xdt_g * dec
            dcs_lane[:, gl] = dyg * eg * y0 - t
            dcs_lane[CHUNK - 1:CHUNK, gl] += jnp.sum(t, axis=0, keepdims=True) + dgam
            dxdt[:, gl] = dxdec * dec
            dcb2 = jnp.zeros((CHUNK, 128), f32)
            for j in range(GW // 128):
                pair = g * (GW // 128) + j
                pl_ = slice(pair * 128, (pair + 1) * 128)
                seg = jnp.exp(jnp.where(causal, cs_full[:, pl_] - cstp[pair:pair + 1, :], -jnp.inf))
                m = cb2 * seg
                mb = m.astype(bf16)
                rhs = _block_diag(xdt[:, pl_].astype(bf16), first)
                dyp = dyv[:, pl_].astype(bf16)
                dm = lax.dot_general(dyp, rhs, (((1,), (1,)), ((), ())), preferred_element_type=f32)
                tt = lax.dot_general(mb, dyp, (((0,), (0,)), ((), ())), preferred_element_type=f32)
                dxdt[:, pl_] += jnp.where(first, tt[0:CHUNK], tt[CHUNK:])
                dcb2 = dcb2 + dm * seg
                w = dm * m
                rsum = _dot01(w, ob_ref[...])
                t2 = _dot01(w, ones_l, ta=True)
                dcs_b[:, pl_] = rsum - jnp.where(first, t2[0:CHUNK], t2[CHUNK:])
            dcb2b = dcb2.astype(bf16)
            dcm = dcm + jnp.dot(dcb2b, bg2, preferred_element_type=f32)
            t3 = lax.dot_general(dcb2b, cg, (((0,), (0,)), ((), ())), preferred_element_type=f32)
            dxbc_ref[:, bcol] = dbm + t3[0:CHUNK] + t3[CHUNK:]
            dxbc_ref[:, ccol] = dcm
            dh[g] = dh_prev
        dcs = _dot01(dcs_lane[...] + dcs_b[...] * (1.0 / HD), ext_ref[...])
        r_i = lax.broadcasted_iota(jnp.int32, (CHUNK, CHUNK), 0)
        c_i = lax.broadcasted_iota(jnp.int32, (CHUNK, CHUNK), 1)
        triu = (r_i <= c_i).astype(f32)
        da_ = _dot01(triu, dcs, exact="a")
        dxdtv = dxdt[...]
        ddt = da_ * a_ref[...] + _dot01(dxdtv * xs, ext_ref[...])
        a_da[...] += _fold8(da_ * dt)
        dxbc_ref[:, 0:SSD_W] = dyv * dsk_ref[...] + dxdtv * dt_full
        ddtr = ddt * _sigmoid(dtr_ref[...] + dtb_ref[...])
        ddtr_ref[...] = ddtr
        a_dtb[...] += _fold8(ddtr)

        @pl.when(step == nc - 1)
        def _():
            dd_ref[...] = jnp.sum(jnp.dot(a_dd[...], ext_ref[...], precision=HIGHEST, preferred_element_type=f32), axis=0, keepdims=True)
            da_ref[...] = jnp.sum(a_da[...], axis=0, keepdims=True)
            ddtb_ref[...] = jnp.sum(a_dtb[...], axis=0, keepdims=True)

    rev = lambda c: nc - 1 - c
    const = lambda shape: pl.BlockSpec(shape, lambda c: tuple(0 for _ in shape))
    return pl.pallas_call(
        body, name="ssd_bwd", grid=(nc,),
        in_specs=[pl.BlockSpec((CHUNK, XBC), lambda c: (rev(c), 0)), pl.BlockSpec((CHUNK, 128), lambda c: (rev(c), DT_COL)),
                  pl.BlockSpec((CHUNK, SSD_W), lambda c: (rev(c), 0)), pl.BlockSpec((None, NG, NSTATE, GW), lambda c: (rev(c), 0, 0, 0)),
                  const((1, 128)), const((1, 128)), const((1, SSD_W)), const((128, SSD_W)), const((SSD_W, 128)),
                  const((8, 128)), const((128, 128)), const((128, 128))],
        out_specs=[pl.BlockSpec((CHUNK, XBC), lambda c: (rev(c), 0)), pl.BlockSpec((CHUNK, 128), lambda c: (rev(c), 0)),
                   const((1, 128)), const((1, 128)), const((1, 128))],
        out_shape=[_sds((s, XBC), f32), _sds((s, 128), f32), _sds((1, 128), f32), _sds((1, 128), f32), _sds((1, 128), f32)],
        scratch_shapes=[pltpu.VMEM((NG, NSTATE, GW), f32), pltpu.VMEM((8, SSD_W), f32), pltpu.VMEM((8, 128), f32), pltpu.VMEM((8, 128), f32),
                        pltpu.VMEM((CHUNK, SSD_W), f32), pltpu.VMEM((CHUNK, SSD_W), f32), pltpu.VMEM((CHUNK, SSD_W), f32)],
        compiler_params=_params(("arbitrary",)),
    )(xbc, proj2, dy, hsave, a_row, dtb_row, dsk_full, jnp.asarray(ex), jnp.asarray(ext), jnp.asarray(sel), jnp.asarray(par),
      jnp.asarray(ones_blk))


def _local_step(x, tgt, mods, g_mix, rel, conv_w, conv_b, dt_bias, a_log, d_skip, g_att, g_ssd, g_ffn, g_final, weights):
    s = x.shape[0]
    tm_e = 256 if s % 256 == 0 else s
    tm_m = 512 if s % 512 == 0 else s
    tm_l = 1024 if s % 1024 == 0 else s
    tk = 2048 if s % 2048 == 0 else s
    sh1, sc1, gt1, sh2, sc2, gt2 = [mods[:, i * D:(i + 1) * D] for i in range(6)]

    h1b = _norm_mod("norm_mod_1", x, g_mix, sc1, sh1, tm_e)
    win, win_b = weights.w_in(h1b)
    qkv = _mm_nn_fullk("proj_qkv", h1b, win, tm_m, 768, bf16, n=IN_A)
    proj2 = _mm_nn_fullk("proj_zxbcdt", h1b, win_b, tm_m, 896, f32)
    bias = _expand_bias(rel)
    att = _attn_fwd(qkv, bias)
    xbc = _ssd_conv(proj2, conv_w, conv_b, tm_e)
    a_row = jnp.pad(-jnp.exp(a_log), ((0, 0), (0, 128 - NH)))
    dtb_row = jnp.pad(dt_bias, ((0, 0), (0, 128 - NH)))
    dsk_full = jnp.repeat(d_skip, HD, axis=1)
    y, hsave = _ssd_fwd(xbc, proj2, a_row, dtb_row, dsk_full)
    mixcat = _mix_pre(att, y, proj2, g_att, g_ssd, tm_e)
    wout = weights.w_out(mixcat)
    mix = _mm_nn_fullk("proj_out", mixcat, wout, tm_m, 1024, f32)
    x2, h2b = _resid_norm_mod(x, gt1, mix, g_ffn, sc2, sh2, tm_e)
    wg4, wu4, wd4 = weights.ffn(h2b)
    gate, up, act = _ffn_up(h2b, wg4, wu4, tm_m)
    ffn = _ffn_down(act, wd4, tm_l)

    dx3, dffn, loss, dg_final, dgt2 = _final_fwd_bwd(x2, ffn, gt2, g_final, tgt, tm_e)
    tok = weights.grad(("w_down",), [_grad_wdown4(act, dffn, 1024, tk)])
    dgate, dup = _ffn_dact(dffn, wd4, gate, up, tm_m, dep=tok)
    tok = weights.grad(("w_gate", "w_up"), [_grad_cols4("grad_w_gate", h2b, dgate, 1024, tk), _grad_cols4("grad_w_up", h2b, dup, 1024, tk)])
    dh2 = _ffn_dh(dgate, dup, wg4, wu4, tm_m, dep=tok)
    dx2, dmix, dsc2, dsh2, dg_ffn, dgt1 = _norm_mod_bwd("norm_mod_bwd_2", dh2, x2, g_ffn, sc2, dx3, tm_e, mix=mix, gt=gt1)
    tok = weights.grad(("w_out",), [_mm_tn("grad_w_out", mixcat, dmix, 1024, 1024, tk, bf16).reshape(NSH, D // NSH, D)])
    dmc = _mm_nt("dmixcat", dmix, wout, tm_m, 1024, D, f32, dep=tok)
    datt, dy, dz, dg_att, dg_ssd = _mix_pre_bwd(dmc, att, y, proj2, g_att, g_ssd, tm_e)
    dq, dk, dv, gband = _attn_bwd(qkv, datt, bias, jnp.transpose(bias, (0, 2, 1)))
    drel = _rel_bias_grad(gband.reshape(NH, CHUNK, BANDP))
    dxbc, ddtr, dd_row, da_row, ddtb_row = _ssd_bwd(xbc, proj2, dy, hsave, a_row, dtb_row, dsk_full)
    dxbc_raw, dconv_w, dconv_b = _ssd_conv_bwd(dxbc, proj2, conv_w, conv_b, tm_e)
    dproj = jnp.concatenate([dq, dk, dv, dz, dxbc_raw, ddtr.astype(bf16)], axis=1)
    gwin = _mm_tn("grad_w_in", h1b, dproj, 1024, 1152, tk, bf16)
    gwin4 = jnp.stack([jnp.pad(gwin[:, k * IN_SH:(k + 1) * IN_SH], ((0, 0), (0, IN_SHP - IN_SH))) for k in range(NSH)])
    tok = weights.grad(("w_in",), [gwin4])
    dh1 = _mm_nt("dh1", dproj, win, tm_m, D, 1920, f32, dep=tok)
    grad_x, dsc1, dsh1, dg_mix = _norm_mod_bwd("norm_mod_bwd_1", dh1, x, g_mix, sc1, dx2, tm_e)

    dmods = jnp.concatenate([dsh1, dsc1, dgt1, dsh2, dsc2, dgt2], axis=1)
    dd_skip = dd_row[:, :NH]
    da_log = da_row[:, :NH] * a_row[:, :NH]
    small = dict(g_mix=dg_mix, conv_b=dconv_b, dt_bias=ddtb_row[:, :NH], a_log=da_log, d_skip=dd_skip, g_att_out=dg_att,
                 g_ssd_out=dg_ssd, g_ffn=dg_ffn, g_final=dg_final, rel_bias=drel, conv_w=dconv_w)
    return loss[0, 0], grad_x, dmods, small


HBM = pl.BlockSpec(memory_space=pl.ANY)
VMEM = pl.BlockSpec(memory_space=pltpu.VMEM)


def _place():
    x, y, c = lax.axis_index("x"), lax.axis_index("y"), lax.axis_index("c")
    chips = [(1 - x, y), (x, 1 - y), (1 - x, 1 - y)]
    return x, y, c, chips


def _allgather8(name, payload, dep=None):
    r = payload.shape[0]
    deps = [] if dep is None else [dep]

    def body(x_ref, *rest):
        out_ref, send_sems, recv_sems, local_sem = rest[-4:]
        x, y, c, chips = _place()
        me, sibling = (x, y, c), (x, y, 1 - c)

        def slot(px, py, pc):
            return out_ref.at[4 * px + 2 * py + pc]

        def copy(k, block, to, src=None):
            return pltpu.make_async_remote_copy(
                src_ref=slot(*block) if src is None else src, dst_ref=slot(*block),
                send_sem=send_sems.at[k], recv_sem=recv_sems.at[k], device_id=to, device_id_type=MESH)

        mine = pltpu.make_async_copy(x_ref, slot(*me), local_sem)
        mine.start()
        first = [copy(0, me, sibling, src=x_ref)]
        first += [copy(1 + j, me, (*chip, c), src=x_ref) for j, chip in enumerate(chips)]
        for cp in first:
            cp.start()
        passed = [copy(4 + j, (*chip, c), sibling) for j, chip in enumerate(chips)]
        for j, chip in enumerate(chips):
            copy(1 + j, (*chip, c), me).wait_recv()
            passed[j].start()
        copy(0, sibling, me).wait_recv()
        for j, chip in enumerate(chips):
            copy(4 + j, (*chip, 1 - c), me).wait_recv()
        for cp in first + passed:
            cp.wait_send()
        mine.wait()

    return pl.pallas_call(
        body, name=name, out_shape=_sds((N_DEV, r, 128), f32), in_specs=[VMEM] * (1 + len(deps)), out_specs=VMEM,
        scratch_shapes=[pltpu.SemaphoreType.DMA((7,)), pltpu.SemaphoreType.DMA((7,)), pltpu.SemaphoreType.DMA],
    )(payload, *deps)


def _sum8(g):
    r = g.shape[1]

    def body(g_ref, o_ref):
        acc = g_ref[0]
        for i in range(1, N_DEV):
            acc = acc + g_ref[i]
        o_ref[...] = acc

    return pl.pallas_call(body, name="sum8", out_shape=_sds((r, 128), f32))(g)


SEM = pl.BlockSpec(memory_space=pltpu.SEMAPHORE)
EFFECT = pltpu.SideEffectType.DATAFLOW_SIDE_EFFECTING


def _gather_copies(ins, lands, send_sems, recv_sems):
    x, y, c, chips = _place()
    k = 2 * x + y
    starts, recvs = [], []
    for w in range(len(ins)):
        for j, (px, py) in enumerate(chips):
            def mk(dst):
                return pltpu.make_async_remote_copy(src_ref=ins[w].at[c], dst_ref=dst, send_sem=send_sems[w].at[j],
                                                    recv_sem=recv_sems[w].at[j], device_id=(px, py, c), device_id_type=MESH)
            starts.append(mk(lands[w].at[k, c]))
            recvs.append(mk(lands[w].at[2 * px + py, c]))
    return starts, recvs


def _reduce_copies(ins, lands, send_sems, recv_sems):
    x, y, c, chips = _place()
    k = 2 * x + y
    starts, recvs = [], []
    for w in range(len(ins)):
        for j, (px, py) in enumerate(chips):
            def mk(dst):
                return pltpu.make_async_remote_copy(src_ref=ins[w].at[2 * px + py], dst_ref=dst, send_sem=send_sems[w].at[j],
                                                    recv_sem=recv_sems[w].at[j], device_id=(px, py, c), device_id_type=MESH)
            starts.append(mk(lands[w].at[k]))
            recvs.append(mk(lands[w].at[2 * px + py]))
    return starts, recvs


def _split_start(name, copies, srcs, land_shapes):
    nw = len(srcs)

    def body(*refs):
        starts, _ = copies(refs[:nw], refs[nw:2 * nw], refs[2 * nw:3 * nw], refs[3 * nw:4 * nw])
        for cp in starts:
            cp.start()
        refs[6 * nw][...] = jnp.zeros((8, 128), f32)

    sems = [pltpu.SemaphoreType.DMA((3,))] * nw
    bufs = [pltpu.HBM(s.shape, bf16) for s in srcs] + [pltpu.HBM(s, bf16) for s in land_shapes]
    res = pl.pallas_call(
        body, name=name, out_shape=sems + sems + bufs + [_sds((8, 128), f32)],
        in_specs=[HBM] * (2 * nw), out_specs=[SEM] * (2 * nw) + [HBM] * (2 * nw) + [VMEM],
        input_output_aliases={i: 2 * nw + i for i in range(2 * nw)},
        compiler_params=pltpu.CompilerParams(has_side_effects=EFFECT),
    )(*[pltpu.with_memory_space_constraint(s, pltpu.HBM) for s in srcs],
      *[pltpu.with_memory_space_constraint(lax.empty(s, bf16), pltpu.HBM) for s in land_shapes])
    return res[:nw], res[nw:2 * nw], res[2 * nw:3 * nw], res[3 * nw:4 * nw], res[4 * nw]


def _split_wait(name, copies, send_sems, recv_sems, srcs, lands, after):
    nw = len(srcs)

    def body(*refs):
        starts, recvs = copies(refs[:nw], refs[nw:2 * nw], refs[2 * nw:3 * nw], refs[3 * nw:4 * nw])
        for s_, r_ in zip(starts, recvs):
            s_.wait_send()
            r_.wait_recv()

    bufs = [pltpu.HBM(s.shape, bf16) for s in srcs] + [pltpu.HBM(l.shape, bf16) for l in lands]
    res = pl.pallas_call(
        body, name=name, out_shape=bufs, in_specs=[HBM] * (2 * nw) + [SEM] * (2 * nw) + [HBM], out_specs=[HBM] * (2 * nw),
        input_output_aliases={i: i for i in range(2 * nw)},
        compiler_params=pltpu.CompilerParams(has_side_effects=EFFECT),
    )(*srcs, *lands, *send_sems, *recv_sems, after)
    return res[:nw], res[nw:]


def _gather_forward(name, shards, lands):
    nw = len(shards)

    def body(*refs):
        ins, lands_in, outs = refs[:nw], refs[nw:2 * nw], refs[2 * nw:3 * nw]
        st_a, st_b, st_c = refs[3 * nw:4 * nw], refs[4 * nw:5 * nw], refs[5 * nw:6 * nw]
        send_sems, recv_sems, load_sems, store_sems = refs[6 * nw:]
        x, y, c, chips = _place()
        k = 2 * x + y
        sibling = (x, y, 1 - c)
        ld_a = [pltpu.make_async_copy(ins[w].at[c], st_a[w], load_sems.at[w, 0]) for w in range(nw)]
        ld_b = [pltpu.make_async_copy(ins[w].at[1 - c], st_b[w], load_sems.at[w, 1]) for w in range(nw)]
        for cp in ld_a + ld_b:
            cp.start()
        st_own = []
        for w in range(nw):
            ld_a[w].wait()
            st_own.append(pltpu.make_async_copy(st_a[w], outs[w].at[k, c], store_sems.at[w, 0]))
            st_own[-1].start()
        for w in range(nw):
            ld_b[w].wait()
            st_own.append(pltpu.make_async_copy(st_b[w], outs[w].at[k, 1 - c], store_sems.at[w, 1]))
            st_own[-1].start()
        for cp in st_own:
            cp.wait()
        fwds = {}
        for j, (px, py) in enumerate(chips):
            kq = 2 * px + py
            for w in range(nw):
                slot = st_b[w] if j % 2 == 0 else st_c[w]
                if j == 2:
                    fwds[w, 0].wait_send()
                ld = pltpu.make_async_copy(lands_in[w].at[kq, c], slot, load_sems.at[w, 2 + j])
                ld.start()
                ld.wait()
                fwds[w, j] = pltpu.make_async_remote_copy(src_ref=slot, dst_ref=outs[w].at[kq, c], send_sem=send_sems.at[w, j],
                                                          recv_sem=recv_sems.at[w, j], device_id=sibling, device_id_type=MESH)
                fwds[w, j].start()
        for j, (px, py) in enumerate(chips):
            for w in range(nw):
                pltpu.make_async_remote_copy(src_ref=st_c[w], dst_ref=outs[w].at[2 * px + py, 1 - c], send_sem=send_sems.at[w, j],
                                             recv_sem=recv_sems.at[w, j], device_id=sibling, device_id_type=MESH).wait_recv()
        for w in range(nw):
            fwds[w, 1].wait_send()
            fwds[w, 2].wait_send()

    stage = [pltpu.VMEM(s.shape[1:], bf16) for s in shards]
    return pl.pallas_call(
        body, name=name, out_shape=[_sds(l.shape, bf16) for l in lands],
        in_specs=[HBM] * (2 * nw), out_specs=[HBM] * nw, input_output_aliases={nw + w: w for w in range(nw)},
        scratch_shapes=stage * 3 + [pltpu.SemaphoreType.DMA((nw, 3)), pltpu.SemaphoreType.DMA((nw, 3)), pltpu.SemaphoreType.DMA((nw, 5)),
                                    pltpu.SemaphoreType.DMA((nw, 2))],
        compiler_params=pltpu.CompilerParams(vmem_limit_bytes=VMEM_LIMIT),
    )(*shards, *lands)


def _rs_pair_exchange(name, grads):
    nw = len(grads)

    def body(*refs):
        ins, got, stage = refs[:nw], refs[nw:2 * nw], refs[2 * nw:3 * nw]
        send_sems, recv_sems, load_sems = refs[3 * nw:]
        x, y, c, _ = _place()

        def load(w, kk):
            return pltpu.make_async_copy(ins[w].at[kk, 1 - c], stage[w].at[kk % 2], load_sems.at[w, kk])

        def send(w, kk):
            return pltpu.make_async_remote_copy(src_ref=stage[w].at[kk % 2], dst_ref=got[w].at[kk], send_sem=send_sems.at[w, kk],
                                                recv_sem=recv_sems.at[w, kk], device_id=(x, y, 1 - c), device_id_type=MESH)

        for kk in range(2):
            for w in range(nw):
                load(w, kk).start()
        for kk in range(NSH):
            for w in range(nw):
                load(w, kk).wait()
                send(w, kk).start()
            if kk + 2 < NSH:
                for w in range(nw):
                    send(w, kk).wait_send()
                    load(w, kk + 2).start()
        for kk in range(NSH - 2, NSH):
            for w in range(nw):
                send(w, kk).wait_send()
        for kk in range(NSH):
            for w in range(nw):
                send(w, kk).wait_recv()

    return pl.pallas_call(
        body, name=name, out_shape=[_sds((NSH,) + g.shape[2:], bf16) for g in grads], in_specs=[HBM] * nw, out_specs=[HBM] * nw,
        scratch_shapes=[pltpu.VMEM((2,) + g.shape[2:], bf16) for g in grads]
        + [pltpu.SemaphoreType.DMA((nw, NSH)), pltpu.SemaphoreType.DMA((nw, NSH)), pltpu.SemaphoreType.DMA((nw, NSH))],
        compiler_params=pltpu.CompilerParams(vmem_limit_bytes=VMEM_LIMIT),
    )(*grads)


def _rs_pair_gather(name, halves):
    nw = len(halves)

    def body(*refs):
        ins, outs, stage = refs[:nw], refs[nw:2 * nw], refs[2 * nw:3 * nw]
        send_sems, recv_sems, local_sems, stage_sems = refs[3 * nw:]
        x, y, c, _ = _place()
        loads = [pltpu.make_async_copy(ins[w], stage[w], stage_sems.at[w]) for w in range(nw)]
        for cp in loads:
            cp.start()
        local, cps = [], []
        for w in range(nw):
            loads[w].wait()
            local.append(pltpu.make_async_copy(stage[w], outs[w].at[c], local_sems.at[w]))
            cps.append(pltpu.make_async_remote_copy(src_ref=stage[w], dst_ref=outs[w].at[c], send_sem=send_sems.at[w],
                                                    recv_sem=recv_sems.at[w], device_id=(x, y, 1 - c), device_id_type=MESH))
            local[w].start()
            cps[w].start()
        for w in range(nw):
            pltpu.make_async_remote_copy(src_ref=stage[w], dst_ref=outs[w].at[1 - c], send_sem=send_sems.at[w], recv_sem=recv_sems.at[w],
                                         device_id=(x, y, 1 - c), device_id_type=MESH).wait_recv()
        for cp in cps:
            cp.wait_send()
        for cp in local:
            cp.wait()

    return pl.pallas_call(
        body, name=name, out_shape=[_sds((2,) + h.shape, f32) for h in halves], in_specs=[HBM] * nw, out_specs=[HBM] * nw,
        scratch_shapes=[pltpu.VMEM(h.shape, f32) for h in halves]
        + [pltpu.SemaphoreType.DMA((nw,)), pltpu.SemaphoreType.DMA((nw,)), pltpu.SemaphoreType.DMA((nw,)), pltpu.SemaphoreType.DMA((nw,))],
        compiler_params=pltpu.CompilerParams(vmem_limit_bytes=VMEM_LIMIT),
    )(*halves)


def _row_tile(r, c, nbuf):
    budget = 24 * 1024 * 1024 // (2 * nbuf * 4 * c)
    t = 8
    while t * 2 <= budget and r % (t * 2) == 0:
        t *= 2
    return t


def _cast_bf16(name, a, dep=None):
    r, c = a.shape
    tr = _row_tile(r, c, 2)
    dep_specs, dep_ops = _dep_args(dep, 1)

    def body(a_ref, *rest):
        rest[-1][...] = a_ref[...].astype(bf16)

    spec = pl.BlockSpec((tr, c), lambda i: (i, 0))
    return pl.pallas_call(body, name=name, grid=(r // tr,), in_specs=[spec] + dep_specs, out_specs=spec, out_shape=_sds((r, c), bf16),
                          compiler_params=_params(("parallel",)))(a, *dep_ops)


def _w_in_columns(win4):
    tr = 256

    def body(a_ref, o_ref, ob_ref):
        for k in range(NSH):
            o_ref[:, IN_SH * k:IN_SH * (k + 1)] = a_ref[k][:, :IN_SH]
        o_ref[:, IN_COLS:] = jnp.zeros((tr, IN_P - IN_COLS), bf16)
        ob_ref[...] = o_ref[:, IN_A:]

    return pl.pallas_call(
        body, name="w_in_columns", grid=(D // tr,), in_specs=[pl.BlockSpec((NSH, tr, IN_SHP), lambda i: (0, i, 0))],
        out_specs=[pl.BlockSpec((tr, IN_P), lambda i: (i, 0)), pl.BlockSpec((tr, IN_B), lambda i: (i, 0))],
        out_shape=[_sds((D, IN_P), bf16), _sds((D, IN_B), bf16)], compiler_params=_params(("parallel",)))(win4)


def _pair_sum(name, core, grads, got):
    _, _, rh, c = grads.shape
    tr = _row_tile(rh, c, 2)

    def body(c_ref, a_ref, b_ref, o_ref):
        o_ref[...] = (a_ref[...].astype(f32) + b_ref[...].astype(f32)).astype(bf16)

    spec = pl.BlockSpec((None, tr, c), lambda k, i, c_ref: (k, i, 0))
    return pl.pallas_call(
        body, name=name, out_shape=_sds((NSH, rh, c), bf16),
        grid_spec=pltpu.PrefetchScalarGridSpec(
            num_scalar_prefetch=1, grid=(NSH, rh // tr),
            in_specs=[pl.BlockSpec((None, None, tr, c), lambda k, i, c_ref: (k, c_ref[0], i, 0)), spec], out_specs=spec),
        compiler_params=_params(("parallel", "parallel")))(core, grads, got)


def _chip_sum(name, chip, sums, lands):
    _, rh, c = sums.shape
    tr = _row_tile(rh, c, 4)

    def body(k_ref, own_ref, l_ref, o_ref):
        own = own_ref[...].astype(f32)
        acc = None
        for j in range(NSH):
            term = jnp.where(k_ref[0] == j, own, l_ref[j].astype(f32))
            acc = term if acc is None else acc + term
        o_ref[...] = acc

    return pl.pallas_call(
        body, name=name, out_shape=_sds((rh, c), f32),
        grid_spec=pltpu.PrefetchScalarGridSpec(
            num_scalar_prefetch=1, grid=(rh // tr,),
            in_specs=[pl.BlockSpec((None, tr, c), lambda i, k_ref: (k_ref[0], i, 0)), pl.BlockSpec((NSH, tr, c), lambda i, k_ref: (0, i, 0))],
            out_specs=pl.BlockSpec((tr, c), lambda i, k_ref: (i, 0))),
        compiler_params=_params(("parallel",)))(chip, sums, lands)


def _mods_part(cond16, w_ada, b_part):
    n = w_ada.shape[1]
    tn = 512

    def body(c_ref, w_ref, b_ref, o_ref):
        cv = c_ref[...]
        o_ref[...] = _dot(cv * _sigmoid(cv), w_ref[...]) + b_ref[...]

    return pl.pallas_call(
        body, name="mods_part", grid=(n // tn,),
        in_specs=[pl.BlockSpec((16, D), lambda j: (0, 0)), pl.BlockSpec((D, tn), lambda j: (0, j)), pl.BlockSpec((1, tn), lambda j: (0, j))],
        out_specs=pl.BlockSpec((16, tn), lambda j: (0, j)), out_shape=_sds((16, n), f32), compiler_params=_params(("parallel",)),
    )(cond16, w_ada, b_part)


def _grad_w_ada(cond16, dm16):
    n = dm16.shape[1]
    tr = 256

    def body(c_ref, d_ref, o_ref):
        cv = c_ref[...]
        o_ref[...] = _dot(cv * _sigmoid(cv), d_ref[...], ta=True)

    return pl.pallas_call(
        body, name="grad_w_ada", grid=(D // tr,),
        in_specs=[pl.BlockSpec((16, tr), lambda i: (0, i)), pl.BlockSpec((16, n), lambda i: (0, 0))],
        out_specs=pl.BlockSpec((tr, n), lambda i: (i, 0)), out_shape=_sds((D, n), f32), compiler_params=_params(("parallel",)),
    )(cond16, dm16)


def _adamw(name, w, g, m, v):
    r, c = w.shape
    tr = _row_tile(r, c, 7)

    def body(w_ref, g_ref, m_ref, v_ref, d_ref, nm_ref, nv_ref):
        gv = g_ref[...]
        nm = ADAM_B1 * m_ref[...] + (1.0 - ADAM_B1) * gv
        nv = ADAM_B2 * v_ref[...] + (1.0 - ADAM_B2) * (gv * gv)
        nm_ref[...] = nm
        nv_ref[...] = nv
        m_hat = nm / (1.0 - ADAM_B1 ** ADAM_STEP)
        v_hat = nv / (1.0 - ADAM_B2 ** ADAM_STEP)
        d_ref[...] = -ADAM_LR * (m_hat / (jnp.sqrt(v_hat) + ADAM_EPS) + ADAM_WD * w_ref[...])

    spec = pl.BlockSpec((tr, c), lambda i: (i, 0))
    return pl.pallas_call(body, name=name, grid=(r // tr,), in_specs=[spec] * 4, out_specs=[spec] * 3, out_shape=[_sds((r, c), f32)] * 3,
                          compiler_params=_params(("parallel",)))(w, g, m, v)


def _pack(parts, rows):
    flat = []
    for p in parts:
        p = p.reshape(-1)
        flat.append(jnp.pad(p, (0, (-p.shape[0]) % 128)))
    v = jnp.concatenate(flat)
    return jnp.pad(v, (0, rows * 128 - v.shape[0])).reshape(rows, 128)


def _unpack(packed, sizes):
    lead = packed.shape[:-2]
    flat = packed.reshape(lead + (-1,))
    out, off = [], 0
    for n in sizes:
        out.append(flat[..., off:off + n])
        off += n + (-n) % 128
    return out


BIG = ("w_in", "w_out", "w_gate", "w_up", "w_down")
SMALL = ("b_ada", "g_mix", "conv_b", "dt_bias", "a_log", "d_skip", "g_att_out", "g_ssd_out", "g_ffn", "g_final", "rel_bias", "conv_w")
ORDER = ("w_ada", "b_ada", "g_mix", "w_in", "rel_bias", "conv_w", "conv_b", "dt_bias", "a_log", "d_skip", "g_att_out", "g_ssd_out",
         "w_out", "g_ffn", "w_gate", "w_up", "w_down", "g_final")
REL_SH = N_REL // NSH
CONVW_SH = XBC // NSH
ADA_SH = 6 * D // NSH


class _Exchange:
    def __init__(self, core, chip):
        self.core, self.chip = core, chip
        self.gathered = {}
        self.pending = []

    def gather(self, names, shards):
        ssem, rsem, thru, lands, token = _split_start("gather_start_" + "_".join(names), _gather_copies, shards,
                                                      [(NSH,) + s.shape for s in shards])
        self.gathered.update({n: (ssem[i], rsem[i], thru[i], lands[i]) for i, n in enumerate(names)})
        return token

    def _whole(self, names, after):
        ssem, rsem, thru, lands = zip(*[self.gathered[n] for n in names])
        tag = "_".join(names)
        thru, lands = _split_wait("gather_wait_" + tag, _gather_copies, ssem, rsem, thru, lands, after)
        return _gather_forward("gather_forward_" + tag, thru, lands)

    def w_in(self, after):
        (win4,) = self._whole(("w_in",), after)
        return _w_in_columns(win4.reshape(NSH, D, IN_SHP))

    def w_out(self, after):
        (wout4,) = self._whole(("w_out",), after)
        return wout4.reshape(D, D)

    def ffn(self, after):
        wg4, wu4, wd4 = self._whole(("w_gate", "w_up", "w_down"), after)
        return wg4.reshape(NSH, D, FSH), wu4.reshape(NSH, D, FSH), wd4.reshape(NSH, FSH, D)

    def grad(self, names, grads):
        tag = "_".join(names)
        stacked = [g.reshape(NSH, 2, g.shape[1] // 2, g.shape[2]) for g in grads]
        got = _rs_pair_exchange("rs_pair_exchange_" + tag, stacked)
        sums = [_pair_sum("pair_sum_" + n, self.core, o, g) for n, o, g in zip(names, stacked, got)]
        self.pending.append((names, _split_start("rs_start_" + tag, _reduce_copies, sums, [s.shape for s in sums])))
        return self.pending[-1][1][4]

    def finish(self, after):
        grads = {}
        for names, (ssem, rsem, sums, lands, _) in self.pending:
            tag = "_".join(names)
            sums, lands = _split_wait("rs_wait_" + tag, _reduce_copies, ssem, rsem, sums, lands, after)
            halves = [_chip_sum("chip_sum_" + n, self.chip, sm, ld) for n, sm, ld in zip(names, sums, lands)]
            for n, f in zip(names, _rs_pair_gather("rs_pair_gather_" + tag, halves)):
                grads[n] = f.reshape(2 * f.shape[1], f.shape[2])
        return grads


def kernel(x, c, w_ada, b_ada, g_mix, w_in, rel_bias, conv_w, conv_b, dt_bias, a_log, d_skip, g_att_out, g_ssd_out, w_out, g_ffn, w_gate, w_up, w_down, g_final, loss_target, m_w_ada, m_b_ada, m_g_mix, m_w_in, m_rel_bias, m_conv_w, m_conv_b, m_dt_bias, m_a_log, m_d_skip, m_g_att_out, m_g_ssd_out, m_w_out, m_g_ffn, m_w_gate, m_w_up, m_w_down, m_g_final, v_w_ada, v_b_ada, v_g_mix, v_w_in, v_rel_bias, v_conv_w, v_conv_b, v_dt_bias, v_a_log, v_d_skip, v_g_att_out, v_g_ssd_out, v_w_out, v_g_ffn, v_w_gate, v_w_up, v_w_down, v_g_final):
    args = dict(locals())
    w = {n: args[n] for n in ORDER}
    m = {n: args["m_" + n] for n in ORDER}
    v = {n: args["v_" + n] for n in ORDER}
    ix, iy, ic = lax.axis_index("x"), lax.axis_index("y"), lax.axis_index("c")
    chip = 2 * ix + iy
    dev = 2 * chip + ic
    s = x.shape[1]

    g1 = _allgather8("gather_inputs", _pack([c[0], rel_bias[0], conv_w[0]], 40))
    c_all, rel_sh, convw_sh = _unpack(g1, [D, NH * REL_SH, 4 * CONVW_SH])
    rel_full = jnp.concatenate([rel_sh[2 * k].reshape(NH, REL_SH) for k in range(NSH)], axis=1)
    convw_full = jnp.concatenate([convw_sh[2 * k].reshape(4, CONVW_SH) for k in range(NSH)], axis=1)
    cond16 = jnp.pad(c_all, ((0, 8), (0, 0)))
    b_part = lax.dynamic_slice_in_dim(b_ada, chip * ADA_SH, ADA_SH, axis=1)
    mods_part = _mods_part(cond16, w_ada[0], b_part)[:N_DEV]
    g2 = _allgather8("gather_mods", mods_part.reshape(N_DEV * ADA_SH // 128, 128))
    mods_all = jnp.concatenate([g2[2 * k].reshape(N_DEV, ADA_SH) for k in range(NSH)], axis=1)
    mods = lax.dynamic_slice_in_dim(mods_all, dev, 1, axis=0)

    exchange = _Exchange(jnp.reshape(ic, (1,)).astype(jnp.int32), jnp.reshape(chip, (1,)).astype(jnp.int32))
    shard_in = _cast_bf16("cast_w_in", jnp.pad(w_in[0], ((0, 0), (0, IN_SHP - IN_SH))), dep=g2[0, :8]).reshape(2, D // 2, IN_SHP)
    tok = exchange.gather(("w_in",), [shard_in])
    tok = exchange.gather(("w_out", "w_gate", "w_up", "w_down"), [
        _cast_bf16("cast_w_out", w_out[0], dep=tok).reshape(2, D // NSH // 2, D),
        _cast_bf16("cast_w_gate", w_gate[0], dep=tok).reshape(2, D // 2, FSH),
        _cast_bf16("cast_w_up", w_up[0], dep=tok).reshape(2, D // 2, FSH),
        _cast_bf16("cast_w_down", w_down[0], dep=tok).reshape(2, FSH // 2, D)])
    mods = mods + tok[:1, :1]

    loss, grad_x, dmods, small = _local_step(
        x[0], loss_target[0], mods, g_mix, rel_full, convw_full, conv_b, dt_bias, a_log, d_skip, g_att_out, g_ssd_out, g_ffn,
        g_final[None, :], exchange)

    small_names = ("g_mix", "conv_b", "dt_bias", "a_log", "d_skip", "g_att_out", "g_ssd_out", "g_ffn", "g_final", "rel_bias", "conv_w")
    g3 = _allgather8("gather_small_grads", _pack([dmods] + [small[n] for n in small_names], 264))
    sizes = [6 * D] + [int(np.prod(small[n].shape)) for n in small_names]
    dmods_all = _unpack(g3, sizes)[0]
    summed = _unpack(_sum8(g3), sizes)
    grads = {"b_ada": summed[0].reshape(1, 6 * D)}
    for n, val in zip(small_names, summed[1:]):
        grads[n] = val.reshape(small[n].shape)
    grads["rel_bias"] = lax.dynamic_slice_in_dim(grads["rel_bias"], chip * REL_SH, REL_SH, axis=1)
    grads["conv_w"] = lax.dynamic_slice_in_dim(grads["conv_w"], chip * CONVW_SH, CONVW_SH, axis=1)
    grads["g_final"] = grads["g_final"].reshape(D)
    dm16 = jnp.pad(lax.dynamic_slice_in_dim(dmods_all, chip * ADA_SH, ADA_SH, axis=1), ((0, 8), (0, 0)))
    grads["w_ada"] = _grad_w_ada(cond16, dm16)

    delta, new_m, new_v = {}, {}, {}
    delta["w_ada"], new_m["w_ada"], new_v["w_ada"] = _adamw("adamw_w_ada", w_ada[0], grads["w_ada"], m_w_ada[0], v_w_ada[0])
    grads.update(exchange.finish(grad_x))
    grads["w_in"] = grads["w_in"][:, :IN_SH]
    for n in BIG:
        delta[n], new_m[n], new_v[n] = _adamw("adamw_" + n, w[n][0], grads[n], m[n][0], v[n][0])
    sw = _pack([w[n] for n in SMALL], 200)
    sg = _pack([grads[n] for n in SMALL], 200)
    sm = _pack([m[n] for n in SMALL], 200)
    sv = _pack([v[n] for n in SMALL], 200)
    ssz = [int(np.prod(w[n].shape)) for n in SMALL]
    for dst, packed in zip((delta, new_m, new_v), _adamw("adamw_small", sw, sg, sm, sv)):
        for n, val in zip(SMALL, _unpack(packed, ssz)):
            dst[n] = val

    def shaped(d, n):
        return d[n].reshape(w[n].shape)

    total = lax.psum(loss, ("x", "y", "c"))
    return (total, grad_x[None], *[shaped(grads, n) for n in ORDER], *[shaped(delta, n) for n in ORDER],
            *[shaped(new_m, n) for n in ORDER], *[shaped(new_v, n) for n in ORDER])
```

```python
import functools

import numpy as np
import jax
import jax.numpy as jnp
from jax import lax
from jax.experimental import pallas as pl
from jax.experimental.pallas import tpu as pltpu

f32 = jnp.float32
bf16 = jnp.bfloat16
HIGHEST = lax.Precision.HIGHEST
MESH = pl.DeviceIdType.MESH

D = 2048
CHUNK = 64
LEFT = 8
BAND = (LEFT + 1) * CHUNK
BANDP = 640
PADK = LEFT * CHUNK
NH = 16
HD = 64
ATT_W = NH * HD
SSD_W = 1024
NG = 2
NSTATE = 128
GW = SSD_W // NG
XBC = SSD_W + 2 * NG * NSTATE
N_REL = 320
REL_CLIP = 256
FFN = 5632
NSH = 4
FSH = FFN // NSH
IN_COLS = 5648
IN_SH = IN_COLS // NSH
IN_SHP = 1536
IN_A = 3 * ATT_W
IN_B = 2688
IN_P = IN_A + IN_B
EPS = 1e-6
N_DEV = 8

ADAM_LR = 0.001
ADAM_B1 = 0.9
ADAM_B2 = 0.999
ADAM_EPS = 1e-08
ADAM_WD = 0.01
ADAM_STEP = 10

VMEM_LIMIT = 56 * 1024 * 1024


def _params(sem):
    return pltpu.CompilerParams(dimension_semantics=sem, vmem_limit_bytes=VMEM_LIMIT)


def _sds(shape, dtype):
    return jax.ShapeDtypeStruct(shape, dtype)


def _fold8(v):
    r, w = v.shape
    return jnp.sum(v.reshape(r // 8, 8, w), axis=0)


STRIP = 16


def _strips(tm, fn):
    def step(j, carry):
        fn(pl.ds(pl.multiple_of(j * STRIP, STRIP), STRIP))
        return carry
    lax.fori_loop(0, tm // STRIP, step, 0, unroll=4)


def _sigmoid(v):
    return 1.0 / (1.0 + jnp.exp(-v))


def _softplus(v):
    return jnp.maximum(v, 0.0) + jnp.log(1.0 + jnp.exp(-jnp.abs(v)))


def _dot(a, b, ta=False, tb=False):
    dn = (((0 if ta else 1,), (1 if tb else 0,)), ((), ()))
    return lax.dot_general(a.astype(bf16), b.astype(bf16), dn, preferred_element_type=f32)


def _dep_args(dep, ngrid):
    if dep is None:
        return [], []
    return [pl.BlockSpec((8, 128), lambda *_: (0, 0))], [dep]


def _dot01(a, b, ta=False, tb=False, exact="b"):
    dn = (((0 if ta else 1,), (1 if tb else 0,)), ((), ()))
    x = a if exact == "b" else b
    hi = x.astype(bf16)
    r = x - hi.astype(f32)
    mid = r.astype(bf16)
    lo = (r - mid.astype(f32)).astype(bf16)
    if exact == "b":
        m = b.astype(bf16)
        return sum(lax.dot_general(p, m, dn, preferred_element_type=f32) for p in (hi, mid, lo))
    m = a.astype(bf16)
    return sum(lax.dot_general(m, p, dn, preferred_element_type=f32) for p in (hi, mid, lo))


def _matmul(name, a, b, *, grid, a_spec, b_spec, o_spec, o_shape, o_dtype, acc_shape, ta=False, tb=False, dep=None):
    nk = grid[2]
    dep_specs, dep_ops = _dep_args(dep, 3)

    def body(a_ref, b_ref, *rest):
        o_ref, acc_ref = rest[-2:]
        p = _dot(a_ref[...], b_ref[...], ta, tb)
        if nk == 1:
            o_ref[...] = p.astype(o_ref.dtype)
        else:
            k = pl.program_id(2)

            @pl.when(k == 0)
            def _():
                acc_ref[...] = p

            @pl.when(jnp.logical_and(k > 0, k < nk - 1))
            def _():
                acc_ref[...] += p

            @pl.when(k == nk - 1)
            def _():
                o_ref[...] = (acc_ref[...] + p).astype(o_ref.dtype)

    return pl.pallas_call(
        body, name=name, grid=grid, in_specs=[a_spec, b_spec] + dep_specs, out_specs=o_spec,
        out_shape=_sds(o_shape, o_dtype), scratch_shapes=[pltpu.VMEM(acc_shape if nk > 1 else (8, 128), f32)],
        compiler_params=_params(("parallel", "parallel", "arbitrary")),
    )(a, b, *dep_ops)


def _mm_nn_fullk(name, a, b, tm, tn, o_dtype, n=None):
    m, k = a.shape
    n = b.shape[1] if n is None else n
    return _matmul(name, a, b, grid=(m // tm, n // tn, 1),
                   a_spec=pl.BlockSpec((tm, k), lambda i, j, kk: (i, 0)),
                   b_spec=pl.BlockSpec((k, tn), lambda i, j, kk: (0, j)),
                   o_spec=pl.BlockSpec((tm, tn), lambda i, j, kk: (i, j)),
                   o_shape=(m, n), o_dtype=o_dtype, acc_shape=(tm, tn))


def _mm_nt(name, a, b, tm, tn, tk, o_dtype, dep=None):
    m, k = a.shape
    n = b.shape[0]
    return _matmul(name, a, b, grid=(m // tm, n // tn, k // tk), tb=True, dep=dep,
                   a_spec=pl.BlockSpec((tm, tk), lambda i, j, kk: (i, kk)),
                   b_spec=pl.BlockSpec((tn, tk), lambda i, j, kk: (j, kk)),
                   o_spec=pl.BlockSpec((tm, tn), lambda i, j, kk: (i, j)),
                   o_shape=(m, n), o_dtype=o_dtype, acc_shape=(tm, tn))


def _mm_tn(name, a, b, tm, tn, tk, o_dtype):
    k, m = a.shape
    n = b.shape[1]
    return _matmul(name, a, b, grid=(m // tm, n // tn, k // tk), ta=True,
                   a_spec=pl.BlockSpec((tk, tm), lambda i, j, kk: (kk, i)),
                   b_spec=pl.BlockSpec((tk, tn), lambda i, j, kk: (kk, j)),
                   o_spec=pl.BlockSpec((tm, tn), lambda i, j, kk: (i, j)),
                   o_shape=(m, n), o_dtype=o_dtype, acc_shape=(tm, tn))


FSH_PARTS = (slice(0, 640), slice(640, FSH))


def _ffn_up(h2b, wg4, wu4, tm):
    s = h2b.shape[0]

    def body(h_ref, wg_ref, wu_ref, g_ref, u_ref, a_ref):
        h = h_ref[...]
        for cols in FSH_PARTS:
            g = _dot(h, wg_ref[:, cols])
            u = _dot(h, wu_ref[:, cols])
            g_ref[:, cols] = g
            u_ref[:, cols] = u
            a_ref[:, cols] = (g * _sigmoid(g) * u).astype(bf16)

    wspec = pl.BlockSpec((None, D, FSH), lambda k, i: (k, 0, 0))
    ospec = pl.BlockSpec((tm, FSH), lambda k, i: (i, k))
    return pl.pallas_call(
        body, name="ffn_up", grid=(NSH, s // tm),
        in_specs=[pl.BlockSpec((tm, D), lambda k, i: (i, 0)), wspec, wspec],
        out_specs=[ospec, ospec, ospec],
        out_shape=[_sds((s, FFN), f32), _sds((s, FFN), f32), _sds((s, FFN), bf16)],
        compiler_params=_params(("parallel", "parallel")),
    )(h2b, wg4, wu4)


def _ffn_down(act, wd4, tm):
    s = act.shape[0]
    return _matmul("ffn_down", act, wd4, grid=(s // tm, 1, NSH),
                   a_spec=pl.BlockSpec((tm, FSH), lambda i, j, k: (i, k)),
                   b_spec=pl.BlockSpec((None, FSH, D), lambda i, j, k: (k, 0, 0)),
                   o_spec=pl.BlockSpec((tm, D), lambda i, j, k: (i, 0)),
                   o_shape=(s, D), o_dtype=f32, acc_shape=(tm, D))


def _ffn_dact(dffn, wd4, gate, up, tm, dep=None):
    s = dffn.shape[0]
    dep_specs, dep_ops = _dep_args(dep, 2)

    def body(d_ref, w_ref, g_ref, u_ref, *rest):
        dg_ref, du_ref = rest[-2:]
        d = d_ref[...]
        for cols in FSH_PARTS:
            dact = _dot(d, w_ref[cols, :], tb=True)
            g = g_ref[:, cols]
            sg = _sigmoid(g)
            dg_ref[:, cols] = (dact * u_ref[:, cols] * (sg * (1.0 + g * (1.0 - sg)))).astype(bf16)
            du_ref[:, cols] = (dact * (g * sg)).astype(bf16)

    blk = pl.BlockSpec((tm, FSH), lambda k, i: (i, k))
    return pl.pallas_call(
        body, name="ffn_dact", grid=(NSH, s // tm),
        in_specs=[pl.BlockSpec((tm, D), lambda k, i: (i, 0)), pl.BlockSpec((None, FSH, D), lambda k, i: (k, 0, 0)), blk, blk] + dep_specs,
        out_specs=[blk, blk], out_shape=[_sds((s, FFN), bf16), _sds((s, FFN), bf16)],
        compiler_params=_params(("parallel", "parallel")),
    )(dffn, wd4, gate, up, *dep_ops)


def _ffn_dh(dgate, dup, wg4, wu4, tm, dep=None):
    s = dgate.shape[0]
    dep_specs, dep_ops = _dep_args(dep, 2)

    def body(dg_ref, du_ref, wg_ref, wu_ref, *rest):
        o_ref, acc_ref = rest[-2:]
        k = pl.program_id(1)
        p = _dot(dg_ref[...], wg_ref[...], tb=True) + _dot(du_ref[...], wu_ref[...], tb=True)

        @pl.when(k == 0)
        def _():
            acc_ref[...] = p

        @pl.when(jnp.logical_and(k > 0, k < NSH - 1))
        def _():
            acc_ref[...] += p

        @pl.when(k == NSH - 1)
        def _():
            o_ref[...] = acc_ref[...] + p

    aspec = pl.BlockSpec((tm, FSH), lambda i, k: (i, k))
    wspec = pl.BlockSpec((None, D, FSH), lambda i, k: (k, 0, 0))
    return pl.pallas_call(
        body, name="ffn_dh", grid=(s // tm, NSH), in_specs=[aspec, aspec, wspec, wspec] + dep_specs,
        out_specs=pl.BlockSpec((tm, D), lambda i, k: (i, 0)), out_shape=_sds((s, D), f32),
        scratch_shapes=[pltpu.VMEM((tm, D), f32)], compiler_params=_params(("parallel", "arbitrary")),
    )(dgate, dup, wg4, wu4, *dep_ops)


def _grad_cols4(name, h, dy, tm, tk):
    s = h.shape[0]
    return _matmul(name, h, dy, grid=(NSH, D // tm, s // tk), ta=True,
                   a_spec=pl.BlockSpec((tk, tm), lambda k, i, kk: (kk, i)),
                   b_spec=pl.BlockSpec((tk, FSH), lambda k, i, kk: (kk, k)),
                   o_spec=pl.BlockSpec((None, tm, FSH), lambda k, i, kk: (k, i, 0)),
                   o_shape=(NSH, D, FSH), o_dtype=bf16, acc_shape=(tm, FSH))


def _grad_wdown4(act, dffn, tn, tk):
    s = act.shape[0]
    return _matmul("grad_w_down", act, dffn, grid=(NSH, D // tn, s // tk), ta=True,
                   a_spec=pl.BlockSpec((tk, FSH), lambda k, j, kk: (kk, k)),
                   b_spec=pl.BlockSpec((tk, tn), lambda k, j, kk: (kk, j)),
                   o_spec=pl.BlockSpec((None, FSH, tn), lambda k, j, kk: (k, 0, j)),
                   o_shape=(NSH, FSH, D), o_dtype=bf16, acc_shape=(FSH, tn))


def _row_spec(w):
    return pl.BlockSpec((1, w), lambda i: (0, 0))


def _tile_spec(tm, w, col=0):
    return pl.BlockSpec((tm, w), lambda i: (i, col))


def _norm_mod(name, x, g, sc, sh, tm):
    s = x.shape[0]

    def body(x_ref, g_ref, sc_ref, sh_ref, o_ref):
        def strip(rows):
            xv = x_ref[rows, :]
            r = lax.rsqrt(jnp.mean(xv * xv, axis=-1, keepdims=True) + EPS)
            o_ref[rows, :] = (xv * r * g_ref[...] * (1.0 + sc_ref[...]) + sh_ref[...]).astype(bf16)

        _strips(tm, strip)

    return pl.pallas_call(
        body, name=name, grid=(s // tm,), in_specs=[_tile_spec(tm, D), _row_spec(D), _row_spec(D), _row_spec(D)],
        out_specs=_tile_spec(tm, D), out_shape=_sds((s, D), bf16), compiler_params=_params(("parallel",)),
    )(x, g, sc, sh)


def _resid_norm_mod(x, gt, mix, g, sc, sh, tm):
    s = x.shape[0]

    def body(x_ref, gt_ref, m_ref, g_ref, sc_ref, sh_ref, x2_ref, h_ref):
        def strip(rows):
            xv = x_ref[rows, :] + gt_ref[...] * m_ref[rows, :]
            x2_ref[rows, :] = xv
            r = lax.rsqrt(jnp.mean(xv * xv, axis=-1, keepdims=True) + EPS)
            h_ref[rows, :] = (xv * r * g_ref[...] * (1.0 + sc_ref[...]) + sh_ref[...]).astype(bf16)

        _strips(tm, strip)

    return pl.pallas_call(
        body, name="resid_norm_mod", grid=(s // tm,),
        in_specs=[_tile_spec(tm, D), _row_spec(D), _tile_spec(tm, D), _row_spec(D), _row_spec(D), _row_spec(D)],
        out_specs=[_tile_spec(tm, D), _tile_spec(tm, D)], out_shape=[_sds((s, D), f32), _sds((s, D), bf16)],
        compiler_params=_params(("parallel",)),
    )(x, gt, mix, g, sc, sh)


def _final_fwd_bwd(x2, ffn, gt2, g, tgt, tm):
    s = x2.shape[0]
    n = s // tm

    def body(x_ref, f_ref, gt_ref, g_ref, t_ref, dx_ref, df_ref, loss_ref, dg_ref, dgt_ref, a_loss, a_dg, a_dgt):
        i = pl.program_id(0)

        @pl.when(i == 0)
        def _():
            a_loss[...] = jnp.zeros_like(a_loss)
            a_dg[...] = jnp.zeros_like(a_dg)
            a_dgt[...] = jnp.zeros_like(a_dgt)

        def strip(rows):
            fv = f_ref[rows, :]
            gt = gt_ref[...]
            gv = g_ref[...]
            xv = x_ref[rows, :] + gt * fv
            r = lax.rsqrt(jnp.mean(xv * xv, axis=-1, keepdims=True) + EPS)
            xh = xv * r
            e = xh * gv - t_ref[rows, :]
            a_loss[...] += _fold8(e * e)
            dy = e * (1.0 / D)
            a_dg[...] += _fold8(dy * xh)
            t = dy * gv
            dx = r * (t - xh * jnp.mean(t * xh, axis=-1, keepdims=True))
            dx_ref[rows, :] = dx
            a_dgt[...] += _fold8(dx * fv)
            df_ref[rows, :] = (dx * gt).astype(bf16)

        _strips(tm, strip)

        @pl.when(i == n - 1)
        def _():
            tot = jnp.sum(jnp.sum(a_loss[...], axis=0, keepdims=True), axis=1, keepdims=True) * (0.5 / D)
            loss_ref[...] = jnp.broadcast_to(tot, (1, 128))
            dg_ref[...] = jnp.sum(a_dg[...], axis=0, keepdims=True)
            dgt_ref[...] = jnp.sum(a_dgt[...], axis=0, keepdims=True)

    return pl.pallas_call(
        body, name="final_fwd_bwd", grid=(n,),
        in_specs=[_tile_spec(tm, D), _tile_spec(tm, D), _row_spec(D), _row_spec(D), _tile_spec(tm, D)],
        out_specs=[_tile_spec(tm, D), _tile_spec(tm, D), _row_spec(128), _row_spec(D), _row_spec(D)],
        out_shape=[_sds((s, D), f32), _sds((s, D), bf16), _sds((1, 128), f32), _sds((1, D), f32), _sds((1, D), f32)],
        scratch_shapes=[pltpu.VMEM((8, D), f32)] * 3, compiler_params=_params(("arbitrary",)),
    )(x2, ffn, gt2, g, tgt)


def _norm_mod_bwd(name, dh, xin, g, sc, dres, tm, mix=None, gt=None):
    s = dh.shape[0]
    n = s // tm
    with_mix = mix is not None

    def body(*refs):
        if with_mix:
            dh_ref, x_ref, g_ref, sc_ref, dr_ref, m_ref, gt_ref, dx_ref, dm_ref, dsc_ref, dsh_ref, dg_ref, dgt_ref, a_sc, a_sh, a_g, a_gt = refs
        else:
            dh_ref, x_ref, g_ref, sc_ref, dr_ref, dx_ref, dsc_ref, dsh_ref, dg_ref, a_sc, a_sh, a_g = refs
        i = pl.program_id(0)

        @pl.when(i == 0)
        def _():
            a_sc[...] = jnp.zeros_like(a_sc)
            a_sh[...] = jnp.zeros_like(a_sh)
            a_g[...] = jnp.zeros_like(a_g)
            if with_mix:
                a_gt[...] = jnp.zeros_like(a_gt)

        def strip(rows):
            dh = dh_ref[rows, :]
            xv = x_ref[rows, :]
            gv = g_ref[...]
            r = lax.rsqrt(jnp.mean(xv * xv, axis=-1, keepdims=True) + EPS)
            xh = xv * r
            a_sc[...] += _fold8(dh * xh * gv)
            a_sh[...] += _fold8(dh)
            dn = dh * (1.0 + sc_ref[...])
            a_g[...] += _fold8(dn * xh)
            t = dn * gv
            dx = dr_ref[rows, :] + r * (t - xh * jnp.mean(t * xh, axis=-1, keepdims=True))
            dx_ref[rows, :] = dx
            if with_mix:
                a_gt[...] += _fold8(dx * m_ref[rows, :])
                dm_ref[rows, :] = (dx * gt_ref[...]).astype(bf16)

        _strips(tm, strip)

        @pl.when(i == n - 1)
        def _():
            dsc_ref[...] = jnp.sum(a_sc[...], axis=0, keepdims=True)
            dsh_ref[...] = jnp.sum(a_sh[...], axis=0, keepdims=True)
            dg_ref[...] = jnp.sum(a_g[...], axis=0, keepdims=True)
            if with_mix:
                dgt_ref[...] = jnp.sum(a_gt[...], axis=0, keepdims=True)

    tile, row = _tile_spec(tm, D), _row_spec(D)
    if with_mix:
        ins, args = [tile, tile, row, row, tile, tile, row], (dh, xin, g, sc, dres, mix, gt)
        outs = [tile, tile, row, row, row, row]
        shapes = [_sds((s, D), f32), _sds((s, D), bf16)] + [_sds((1, D), f32)] * 4
        nacc = 4
    else:
        ins, args = [tile, tile, row, row, tile], (dh, xin, g, sc, dres)
        outs = [tile, row, row, row]
        shapes = [_sds((s, D), f32)] + [_sds((1, D), f32)] * 3
        nacc = 3
    return pl.pallas_call(
        body, name=name, grid=(n,), in_specs=ins, out_specs=outs, out_shape=shapes,
        scratch_shapes=[pltpu.VMEM((8, D), f32)] * nacc, compiler_params=_params(("arbitrary",)),
    )(*args)


def _mix_pre(att, y, proj2, g_att, g_ssd, tm):
    s = att.shape[0]

    def body(a_ref, y_ref, z_ref, ga_ref, gs_ref, o_ref):
        def strip(rows):
            a = a_ref[rows, :]
            ra = lax.rsqrt(jnp.mean(a * a, axis=-1, keepdims=True) + EPS)
            o_ref[rows, 0:ATT_W] = (a * ra * ga_ref[...]).astype(bf16)
            z = z_ref[rows, :]
            u = y_ref[rows, :] * (z * _sigmoid(z))
            ru = lax.rsqrt(jnp.mean(u * u, axis=-1, keepdims=True) + EPS)
            o_ref[rows, ATT_W:] = (u * ru * gs_ref[...]).astype(bf16)

        _strips(tm, strip)

    t = _tile_spec(tm, ATT_W)
    return pl.pallas_call(
        body, name="mix_pre", grid=(s // tm,), in_specs=[t, t, t, _row_spec(ATT_W), _row_spec(SSD_W)],
        out_specs=_tile_spec(tm, D), out_shape=_sds((s, D), bf16), compiler_params=_params(("parallel",)),
    )(att, y, proj2, g_att, g_ssd)


def _mix_pre_bwd(dmc, att, y, proj2, g_att, g_ssd, tm):
    s = att.shape[0]
    n = s // tm

    def body(da_ref, ds_ref, a_ref, y_ref, z_ref, ga_ref, gs_ref, datt_ref, dy_ref, dz_ref, dga_ref, dgs_ref, acc_a, acc_s):
        i = pl.program_id(0)

        @pl.when(i == 0)
        def _():
            acc_a[...] = jnp.zeros_like(acc_a)
            acc_s[...] = jnp.zeros_like(acc_s)

        def strip(rows):
            a = a_ref[rows, :]
            ra = lax.rsqrt(jnp.mean(a * a, axis=-1, keepdims=True) + EPS)
            ah = a * ra
            dan = da_ref[rows, :]
            acc_a[...] += _fold8(dan * ah)
            t = dan * ga_ref[...]
            datt_ref[rows, :] = (ra * (t - ah * jnp.mean(t * ah, axis=-1, keepdims=True))).astype(bf16)
            z = z_ref[rows, :]
            yv = y_ref[rows, :]
            sz = _sigmoid(z)
            sil = z * sz
            u = yv * sil
            ru = lax.rsqrt(jnp.mean(u * u, axis=-1, keepdims=True) + EPS)
            uh = u * ru
            dsn = ds_ref[rows, :]
            acc_s[...] += _fold8(dsn * uh)
            t2 = dsn * gs_ref[...]
            du = ru * (t2 - uh * jnp.mean(t2 * uh, axis=-1, keepdims=True))
            dy_ref[rows, :] = du * sil
            dz_ref[rows, :] = (du * yv * (sz * (1.0 + z * (1.0 - sz)))).astype(bf16)

        _strips(tm, strip)

        @pl.when(i == n - 1)
        def _():
            dga_ref[...] = jnp.sum(acc_a[...], axis=0, keepdims=True)
            dgs_ref[...] = jnp.sum(acc_s[...], axis=0, keepdims=True)

    t = _tile_spec(tm, ATT_W)
    row = _row_spec(ATT_W)
    return pl.pallas_call(
        body, name="mix_pre_bwd", grid=(n,),
        in_specs=[_tile_spec(tm, ATT_W, 0), _tile_spec(tm, ATT_W, 1), t, t, t, row, row],
        out_specs=[t, t, t, row, row],
        out_shape=[_sds((s, ATT_W), bf16), _sds((s, SSD_W), f32), _sds((s, SSD_W), bf16), _sds((1, ATT_W), f32), _sds((1, SSD_W), f32)],
        scratch_shapes=[pltpu.VMEM((8, ATT_W), f32)] * 2, compiler_params=_params(("arbitrary",)),
    )(dmc, dmc, att, y, proj2, g_att, g_ssd)


ATT_GROUP = 4
ATT_GROUP_FWD = 4


def _pair_rows(qc):
    two = jnp.concatenate([qc, qc], axis=0)
    r = lax.broadcasted_iota(jnp.int32, (2 * CHUNK, 128), 0)
    l = lax.broadcasted_iota(jnp.int32, (2 * CHUNK, 128), 1)
    return jnp.where((r < CHUNK) == (l < HD), two, jnp.zeros_like(two))


def _scaled(q):
    return q * jnp.asarray(HD ** -0.5, q.dtype)


def _pair_scores(wt, kb, bias, r0, masked):
    sc = lax.dot_general(wt, kb, (((1,), (1,)), ((), ())), preferred_element_type=f32) + bias
    if not masked:
        return sc
    kidx = lax.broadcasted_iota(jnp.int32, sc.shape, 1)
    return jnp.where(r0 + kidx >= PADK, sc, -jnp.inf)


def _softmax(sc, axis):
    e = jnp.exp(sc - jnp.max(sc, axis=axis, keepdims=True))
    return e * (1.0 / jnp.sum(e, axis=axis, keepdims=True))


def _chunk_loops(nc, group, per_trip):
    n_masked = min(-(-LEFT // per_trip), nc // per_trip)

    def run(masked):
        def step(g, carry):
            group(g, masked)
            return carry
        return step

    lax.fori_loop(0, n_masked, run(True), 0)
    lax.fori_loop(n_masked, nc // per_trip, run(False), 0)


def _pair_diag(r):
    lane = lax.broadcasted_iota(jnp.int32, (CHUNK, 128), 1)
    return jnp.where(lane < HD, r[0:CHUNK], r[CHUNK:])


def _pad_keys(k_ref, kp, s):
    kp[0:PADK, :] = jnp.zeros((PADK, 128), bf16)
    kp[PADK:PADK + s, :] = k_ref[...]
    kp[PADK + s:, :] = jnp.zeros((CHUNK, 128), bf16)


def _attn_fwd(qkv, bias2):
    s = qkv.shape[0]
    nc = s // CHUNK
    npair = NH // 2

    def body(q_ref, k_ref, v_ref, b_ref, o_ref, kp, vp):
        _pad_keys(k_ref, kp, s)
        _pad_keys(v_ref, vp, s)

        def group(g, masked):
            r0s = [pl.multiple_of((g * ATT_GROUP_FWD + u) * CHUNK, CHUNK) for u in range(ATT_GROUP_FWD)]
            scs = [_pair_scores(_pair_rows(_scaled(q_ref[pl.ds(r0, CHUNK), :])), kp[pl.ds(r0, BANDP), :], b_ref[...], r0, masked)
                   for r0 in r0s]
            ps = [_softmax(sc, -1).astype(bf16) for sc in scs]
            for r0, p in zip(r0s, ps):
                o_ref[pl.ds(r0, CHUNK), :] = _pair_diag(jnp.dot(p, vp[pl.ds(r0, BANDP), :], preferred_element_type=f32))

        _chunk_loops(nc, group, ATT_GROUP_FWD)

    return pl.pallas_call(
        body, name="attn_fwd", grid=(npair,),
        in_specs=[pl.BlockSpec((s, 128), lambda p: (0, p)), pl.BlockSpec((s, 128), lambda p: (0, npair + p)),
                  pl.BlockSpec((s, 128), lambda p: (0, 2 * npair + p)), pl.BlockSpec((None, 2 * CHUNK, BANDP), lambda p: (p, 0, 0))],
        out_specs=pl.BlockSpec((s, 128), lambda p: (0, p)), out_shape=_sds((s, ATT_W), f32),
        scratch_shapes=[pltpu.VMEM((PADK + s + CHUNK, 128), bf16)] * 2, compiler_params=_params(("parallel",)),
    )(qkv, qkv, qkv, bias2)


def _attn_bwd(qkv, datt, bias2, bias2t):
    s = qkv.shape[0]
    nc = s // CHUNK
    npair = NH // 2
    rows = PADK + s + CHUNK
    nt = (((1,), (1,)), ((), ()))

    def body(q_ref, k_ref, v_ref, do_ref, b_ref, bt_ref, dq_ref, dk_ref, dv_ref, g_ref, kp, vp, dkp, dvp):
        _pad_keys(k_ref, kp, s)
        _pad_keys(v_ref, vp, s)
        dkp[...] = jnp.zeros_like(dkp)
        dvp[...] = jnp.zeros_like(dvp)
        g_ref[...] = jnp.zeros_like(g_ref)

        def group(g, masked):
            r0s = [pl.multiple_of((g * ATT_GROUP + u) * CHUNK, CHUNK) for u in range(ATT_GROUP)]
            wts = [_pair_rows(_scaled(q_ref[pl.ds(r0, CHUNK), :])) for r0 in r0s]
            dos = [_pair_rows(do_ref[pl.ds(r0, CHUNK), :]) for r0 in r0s]
            scs = [_pair_scores(wt, kp[pl.ds(r0, BANDP), :], b_ref[...], r0, masked) for wt, r0 in zip(wts, r0s)]
            dps = [lax.dot_general(do, vp[pl.ds(r0, BANDP), :], nt, preferred_element_type=f32) for do, r0 in zip(dos, r0s)]
            scts, dpts = [], []
            for wt, do, r0 in zip(wts, dos, r0s):
                sct = lax.dot_general(kp[pl.ds(r0, BANDP), :], wt, nt, preferred_element_type=f32) + bt_ref[...]
                if masked:
                    kidx = lax.broadcasted_iota(jnp.int32, sct.shape, 0)
                    sct = jnp.where(r0 + kidx >= PADK, sct, -jnp.inf)
                scts.append(sct)
                dpts.append(lax.dot_general(vp[pl.ds(r0, BANDP), :], do, nt, preferred_element_type=f32))
            for r0, sc, dp in zip(r0s, scs, dps):
                p = _softmax(sc, -1)
                ds = p * (dp - jnp.sum(p * dp, axis=-1, keepdims=True))
                g_ref[...] += ds
                dq = jnp.dot(ds.astype(bf16), kp[pl.ds(r0, BANDP), :], preferred_element_type=f32)
                dq_ref[pl.ds(r0, CHUNK), :] = (_pair_diag(dq) * (HD ** -0.5)).astype(bf16)
            for r0, wt, do, sct, dpt in zip(r0s, wts, dos, scts, dpts):
                pt = _softmax(sct, 0)
                dst = pt * (dpt - jnp.sum(pt * dpt, axis=0, keepdims=True))
                dkp[pl.ds(r0, BANDP), :] += jnp.dot(dst.astype(bf16), wt, preferred_element_type=f32)
                dvp[pl.ds(r0, BANDP), :] += jnp.dot(pt.astype(bf16), do, preferred_element_type=f32)

        _chunk_loops(nc, group, ATT_GROUP)
        dk_ref[...] = dkp[PADK:PADK + s, :].astype(bf16)
        dv_ref[...] = dvp[PADK:PADK + s, :].astype(bf16)

    col = lambda off: pl.BlockSpec((s, 128), lambda p: (0, off + p))
    return pl.pallas_call(
        body, name="attn_bwd", grid=(npair,),
        in_specs=[col(0), col(npair), col(2 * npair), col(0), pl.BlockSpec((None, 2 * CHUNK, BANDP), lambda p: (p, 0, 0)),
                  pl.BlockSpec((None, BANDP, 2 * CHUNK), lambda p: (p, 0, 0))],
        out_specs=[col(0), col(0), col(0), pl.BlockSpec((None, 2 * CHUNK, BANDP), lambda p: (p, 0, 0))],
        out_shape=[_sds((s, ATT_W), bf16)] * 3 + [_sds((npair, 2 * CHUNK, BANDP), f32)],
        scratch_shapes=[pltpu.VMEM((rows, 128), bf16)] * 2 + [pltpu.VMEM((rows, 128), f32)] * 2,
        compiler_params=_params(("parallel",)),
    )(qkv, qkv, qkv, datt, bias2, bias2t)


def _rel_tables():
    onehot = np.zeros((BANDP, N_REL), np.float32)
    for j in range(BAND + CHUNK - 1):
        o = j - (CHUNK - 1)
        onehot[j, int(np.clip(PADK - o, -(CHUNK - 1), REL_CLIP)) + CHUNK - 1] = 1.0
    return onehot, np.ascontiguousarray(np.eye(CHUNK, dtype=np.float32)[::-1])


def _expand_bias(rel):
    ext = jnp.concatenate([jnp.broadcast_to(rel[:, N_REL - 1:], (NH, N_REL - 1)), rel[:, ::-1],
                           jnp.zeros((NH, BANDP - BAND + 1), f32)], axis=1)
    band = jnp.stack([ext[:, CHUNK - 1 - q:CHUNK - 1 - q + BANDP] for q in range(CHUNK)], axis=1)
    band = jnp.where(np.arange(BANDP) < BAND, band, -jnp.inf)
    return band.reshape(NH // 2, 2 * CHUNK, BANDP)


def _rel_bias_grad(gband):
    def body(g_ref, m_ref, flip_ref, o_ref, d2):
        for h in range(NH):
            rev = jnp.dot(flip_ref[...], g_ref[h], precision=HIGHEST, preferred_element_type=f32)
            rolled = pltpu.roll(rev, 0, 1, stride=1, stride_axis=0)
            d2[h:h + 1, :] = jnp.sum(rolled, axis=0, keepdims=True)
        o_ref[...] = jnp.dot(d2[...], m_ref[...], precision=HIGHEST, preferred_element_type=f32)

    onehot, flip = _rel_tables()
    return pl.pallas_call(
        body, name="rel_bias_grad", out_shape=_sds((NH, N_REL), f32), scratch_shapes=[pltpu.VMEM((NH, BANDP), f32)],
    )(gband, jnp.asarray(onehot), jnp.asarray(flip))


XBC_BLK = 512
XBC_COL0 = SSD_W // XBC_BLK
DT_COL = (SSD_W + XBC) // 128


def _conv_taps(ext, w_ref, b_ref, tm):
    n = ext.shape[0]
    pre = w_ref[3:4, :] * ext + b_ref[...]
    for j in range(3):
        pre = pre + w_ref[j:j + 1, :] * pltpu.roll(ext, 3 - j, 0)
    return pre


def _ssd_conv(proj2, conv_w, conv_b, tm):
    s = proj2.shape[0]
    nb = XBC // XBC_BLK

    def body(x_ref, p_ref, w_ref, b_ref, o_ref):
        i = pl.program_id(1)
        prev = jnp.where(i > 0, p_ref[...], 0.0)
        ext = jnp.concatenate([prev, x_ref[...]], axis=0)
        pre = _conv_taps(ext, w_ref, b_ref, tm)[8:8 + tm]
        o_ref[...] = pre * _sigmoid(pre)

    return pl.pallas_call(
        body, name="ssd_conv", grid=(nb, s // tm),
        in_specs=[pl.BlockSpec((tm, XBC_BLK), lambda j, i: (i, XBC_COL0 + j)),
                  pl.BlockSpec((8, XBC_BLK), lambda j, i: (jnp.maximum(i * (tm // 8) - 1, 0), XBC_COL0 + j)),
                  pl.BlockSpec((4, XBC_BLK), lambda j, i: (0, j)), pl.BlockSpec((1, XBC_BLK), lambda j, i: (0, j))],
        out_specs=pl.BlockSpec((tm, XBC_BLK), lambda j, i: (i, j)), out_shape=_sds((s, XBC), f32),
        compiler_params=_params(("parallel", "parallel")),
    )(proj2, proj2, conv_w, conv_b)


def _ssd_conv_bwd(dxbc, proj2, conv_w, conv_b, tm):
    s = proj2.shape[0]
    nb = XBC // XBC_BLK
    n = s // tm
    last8 = s // 8 - 1

    def body(x_ref, xp_ref, xn_ref, d_ref, dn_ref, w_ref, b_ref, o_ref, dw_ref, db_ref):
        i = pl.program_id(1)

        @pl.when(i == 0)
        def _():
            dw_ref[...] = jnp.zeros_like(dw_ref)
            db_ref[...] = jnp.zeros_like(db_ref)

        prev = jnp.where(i > 0, xp_ref[...], 0.0)
        ext = jnp.concatenate([prev, x_ref[...], xn_ref[...]], axis=0)
        pre = _conv_taps(ext, w_ref, b_ref, tm)
        sg = _sigmoid(pre)
        dnext = jnp.where(i < n - 1, dn_ref[...], 0.0)
        dext = jnp.concatenate([jnp.zeros((8, XBC_BLK), f32), d_ref[...], dnext], axis=0)
        dpre = dext * (sg * (1.0 + pre * (1.0 - sg)))
        rows = tm + 16
        dx = w_ref[3:4, :] * dpre
        for j in range(3):
            dx = dx + w_ref[j:j + 1, :] * pltpu.roll(dpre, rows - (3 - j), 0)
        o_ref[...] = dx[8:8 + tm].astype(bf16)
        dcur = dpre[8:8 + tm]
        db_ref[...] += jnp.sum(dcur, axis=0, keepdims=True)
        dw_ref[3:4, :] += jnp.sum(dcur * ext[8:8 + tm], axis=0, keepdims=True)
        for j in range(3):
            dw_ref[j:j + 1, :] += jnp.sum(dcur * pltpu.roll(ext, 3 - j, 0)[8:8 + tm], axis=0, keepdims=True)

    xcol = lambda j: XBC_COL0 + j
    return pl.pallas_call(
        body, name="ssd_conv_bwd", grid=(nb, n),
        in_specs=[pl.BlockSpec((tm, XBC_BLK), lambda j, i: (i, xcol(j))),
                  pl.BlockSpec((8, XBC_BLK), lambda j, i: (jnp.maximum(i * (tm // 8) - 1, 0), xcol(j))),
                  pl.BlockSpec((8, XBC_BLK), lambda j, i: (jnp.minimum((i + 1) * (tm // 8), last8), xcol(j))),
                  pl.BlockSpec((tm, XBC_BLK), lambda j, i: (i, j)),
                  pl.BlockSpec((8, XBC_BLK), lambda j, i: (jnp.minimum((i + 1) * (tm // 8), last8), j)),
                  pl.BlockSpec((4, XBC_BLK), lambda j, i: (0, j)), pl.BlockSpec((1, XBC_BLK), lambda j, i: (0, j))],
        out_specs=[pl.BlockSpec((tm, XBC_BLK), lambda j, i: (i, j)), pl.BlockSpec((4, XBC_BLK), lambda j, i: (0, j)),
                   pl.BlockSpec((1, XBC_BLK), lambda j, i: (0, j))],
        out_shape=[_sds((s, XBC), bf16), _sds((4, XBC), f32), _sds((1, XBC), f32)],
        compiler_params=_params(("parallel", "arbitrary")),
    )(proj2, proj2, proj2, dxbc, dxbc, conv_w, conv_b)


def _ssd_consts():
    ex = np.zeros((128, SSD_W), np.float32)
    for h in range(NH):
        ex[h, h * HD:(h + 1) * HD] = 1.0
    sel = np.zeros((8, 128), np.float32)
    for h in range(NH):
        sel[h // 2, h] = 1.0
    par = np.zeros((128, 128), np.float32)
    for r in range(128):
        for h in range(NH):
            par[r, h] = 1.0 if (h % 2) == (r // 64) else 0.0
    ones_blk = np.zeros((128, 128), np.float32)
    for r in range(128):
        ones_blk[r, (r // 64) * 64:(r // 64) * 64 + 64] = 1.0
    return ex, np.ascontiguousarray(ex.T), sel, par, ones_blk


def _ssd_common(xbc_ref, dtr_ref, a_ref, dtb_ref, ex_ref, sel_ref, par_ref):
    xs = xbc_ref[:, 0:SSD_W]
    dt = _softplus(dtr_ref[...] + dtb_ref[...])
    adt = dt * a_ref[...]
    r_i = lax.broadcasted_iota(jnp.int32, (CHUNK, CHUNK), 0)
    c_i = lax.broadcasted_iota(jnp.int32, (CHUNK, CHUNK), 1)
    tril = (r_i >= c_i).astype(f32)
    cs = _dot01(tril, adt, exact="a")
    cs2 = jnp.concatenate([cs, cs], axis=0) * par_ref[...]
    cstp = _dot01(sel_ref[...], cs2, tb=True, exact="a")
    ex = ex_ref[...]
    dt_full = _dot01(dt, ex)
    cs_full = _dot01(cs, ex)
    return xs, dt, cs, cstp, dt_full, cs_full


def _pair_mask():
    l_i = lax.broadcasted_iota(jnp.int32, (CHUNK, 128), 0)
    lane = lax.broadcasted_iota(jnp.int32, (CHUNK, 128), 1)
    return l_i >= (lane % CHUNK), lane < HD


def _block_diag(xp, first):
    z = jnp.zeros_like(xp)
    return jnp.concatenate([jnp.where(first, xp, z), jnp.where(first, z, xp)], axis=0)


def _ssd_fwd(xbc, proj2, a_row, dtb_row, dsk_full):
    s = xbc.shape[0]
    nc = s // CHUNK
    ex, ext, sel, par, ones_blk = _ssd_consts()

    def body(xbc_ref, dtr_ref, a_ref, dtb_ref, dsk_ref, ex_ref, sel_ref, par_ref, y_ref, hs_ref, hst):
        @pl.when(pl.program_id(0) == 0)
        def _():
            hst[...] = jnp.zeros_like(hst)

        hs_ref[...] = hst[...]
        xs, dt, cs, cstp, dt_full, cs_full = _ssd_common(xbc_ref, dtr_ref, a_ref, dtb_ref, ex_ref, sel_ref, par_ref)
        cs_last = cs_full[CHUNK - 1:CHUNK, :]
        xdt = xs * dt_full
        causal, first = _pair_mask()
        for g in range(NG):
            gl = slice(g * GW, (g + 1) * GW)
            bg = xbc_ref[:, SSD_W + g * NSTATE:SSD_W + (g + 1) * NSTATE].astype(bf16)
            cg = xbc_ref[:, SSD_W + NG * NSTATE + g * NSTATE:SSD_W + NG * NSTATE + (g + 1) * NSTATE].astype(bf16)
            cb2 = lax.dot_general(cg, jnp.concatenate([bg, bg], axis=0), (((1,), (1,)), ((), ())), preferred_element_type=f32)
            hg = hst[g]
            y0 = jnp.dot(cg, hg.astype(bf16), preferred_element_type=f32)
            yoff = jnp.exp(cs_full[:, gl]) * y0
            for j in range(GW // 128):
                pair = g * (GW // 128) + j
                pl_ = slice(pair * 128, (pair + 1) * 128)
                seg = jnp.exp(jnp.where(causal, cs_full[:, pl_] - cstp[pair:pair + 1, :], -jnp.inf))
                m = (cb2 * seg).astype(bf16)
                yd = jnp.dot(m, _block_diag(xdt[:, pl_].astype(bf16), first), preferred_element_type=f32)
                y_ref[:, pl_] = yd + yoff[:, j * 128:(j + 1) * 128] + xs[:, pl_] * dsk_ref[:, pl_]
            xdec = (xdt[:, gl] * jnp.exp(cs_last[:, gl] - cs_full[:, gl])).astype(bf16)
            st = lax.dot_general(bg, xdec, (((0,), (0,)), ((), ())), preferred_element_type=f32)
            hst[g] = jnp.exp(cs_last[:, gl]) * hg + st

    const = lambda shape: pl.BlockSpec(shape, lambda c: tuple(0 for _ in shape))
    return pl.pallas_call(
        body, name="ssd_fwd", grid=(nc,),
        in_specs=[pl.BlockSpec((CHUNK, XBC), lambda c: (c, 0)), pl.BlockSpec((CHUNK, 128), lambda c: (c, DT_COL)),
                  const((1, 128)), const((1, 128)), const((1, SSD_W)), const((128, SSD_W)), const((8, 128)), const((128, 128))],
        out_specs=[pl.BlockSpec((CHUNK, SSD_W), lambda c: (c, 0)), pl.BlockSpec((None, NG, NSTATE, GW), lambda c: (c, 0, 0, 0))],
        out_shape=[_sds((s, SSD_W), f32), _sds((nc, NG, NSTATE, GW), f32)],
        scratch_shapes=[pltpu.VMEM((NG, NSTATE, GW), f32)], compiler_params=_params(("arbitrary",)),
    )(xbc, proj2, a_row, dtb_row, dsk_full, jnp.asarray(ex), jnp.asarray(sel), jnp.asarray(par))


def _ssd_bwd(xbc, proj2, dy, hsave, a_row, dtb_row, dsk_full):
    s = xbc.shape[0]
    nc = s // CHUNK
    ex, ext, sel, par, ones_blk = _ssd_consts()

    def body(xbc_ref, dtr_ref, dy_ref, hs_ref, a_ref, dtb_ref, dsk_ref, ex_ref, ext_ref, sel_ref, par_ref, ob_ref,
             dxbc_ref, ddtr_ref, dd_ref, da_ref, ddtb_ref, dh, a_dd, a_da, a_dtb, dcs_lane, dcs_b, dxdt):
        step = pl.program_id(0)

        @pl.when(step == 0)
        def _():
            dh[...] = jnp.zeros_like(dh)
            a_dd[...] = jnp.zeros_like(a_dd)
            a_da[...] = jnp.zeros_like(a_da)
            a_dtb[...] = jnp.zeros_like(a_dtb)

        xs, dt, cs, cstp, dt_full, cs_full = _ssd_common(xbc_ref, dtr_ref, a_ref, dtb_ref, ex_ref, sel_ref, par_ref)
        cs_last = cs_full[CHUNK - 1:CHUNK, :]
        xdt = xs * dt_full
        dyv = dy_ref[...]
        a_dd[...] += _fold8(dyv * xs)
        causal, first = _pair_mask()
        ones_l = jnp.ones((CHUNK, 128), f32)
        for g in range(NG):
            gl = slice(g * GW, (g + 1) * GW)
            bcol = slice(SSD_W + g * NSTATE, SSD_W + (g + 1) * NSTATE)
            ccol = slice(SSD_W + NG * NSTATE + g * NSTATE, SSD_W + NG * NSTATE + (g + 1) * NSTATE)
            bg = xbc_ref[:, bcol].astype(bf16)
            cg = xbc_ref[:, ccol].astype(bf16)
            bg2 = jnp.concatenate([bg, bg], axis=0)
            cb2 = lax.dot_general(cg, bg2, (((1,), (1,)), ((), ())), preferred_element_type=f32)
            hg = hs_ref[g]
            hgb = hg.astype(bf16)
            dhg = dh[g]
            dhgb = dhg.astype(bf16)
            eg = jnp.exp(cs_full[:, gl])
            dec = jnp.exp(cs_last[:, gl] - cs_full[:, gl])
            gam = jnp.exp(cs_last[:, gl])
            dyg = dyv[:, gl]
            xdt_g = xdt[:, gl]
            y0 = jnp.dot(cg, hgb, preferred_element_type=f32)
            dy0 = (eg * dyg).astype(bf16)
            dcm = lax.dot_general(dy0, hgb, (((1,), (1,)), ((), ())), preferred_element_type=f32)
            dh_prev = gam * dhg + lax.dot_general(cg, dy0, (((0,), (0,)), ((), ())), preferred_element_type=f32)
            dgam = jnp.sum(dhg * hg, axis=0, keepdims=True) * gam
            dxdec = jnp.dot(bg, dhgb, preferred_element_type=f32)
            dbm = lax.dot_general((xdt_g * dec).astype(bf16), dhgb, (((1,), (1,)), ((), ())), preferred_element_type=f32)
            t = dxdec * xdt_g * dec
            dcs_lane[:, gl] = dyg * eg * y0 - t
            dcs_lane[CHUNK - 1:CHUNK, gl] += jnp.sum(t, axis=0, keepdims=True) + dgam
            dxdt[:, gl] = dxdec * dec
            dcb2 = jnp.zeros((CHUNK, 128), f32)
            for j in range(GW // 128):
                pair = g * (GW // 128) + j
                pl_ = slice(pair * 128, (pair + 1) * 128)
                seg = jnp.exp(jnp.where(causal, cs_full[:, pl_] - cstp[pair:pair + 1, :], -jnp.inf))
                m = cb2 * seg
                mb = m.astype(bf16)
                rhs = _block_diag(xdt[:, pl_].astype(bf16), first)
                dyp = dyv[:, pl_].astype(bf16)
                dm = lax.dot_general(dyp, rhs, (((1,), (1,)), ((), ())), preferred_element_type=f32)
                tt = lax.dot_general(mb, dyp, (((0,), (0,)), ((), ())), preferred_element_type=f32)
                dxdt[:, pl_] += jnp.where(first, tt[0:CHUNK], tt[CHUNK:])
                dcb2 = dcb2 + dm * seg
                w = dm * m
                rsum = _dot01(w, ob_ref[...])
                t2 = _dot01(w, ones_l, ta=True)
                dcs_b[:, pl_] = rsum - jnp.where(first, t2[0:CHUNK], t2[CHUNK:])
            dcb2b = dcb2.astype(bf16)
            dcm = dcm + jnp.dot(dcb2b, bg2, preferred_element_type=f32)
            t3 = lax.dot_general(dcb2b, cg, (((0,), (0,)), ((), ())), preferred_element_type=f32)
            dxbc_ref[:, bcol] = dbm + t3[0:CHUNK] + t3[CHUNK:]
            dxbc_ref[:, ccol] = dcm
            dh[g] = dh_prev
        dcs = _dot01(dcs_lane[...] + dcs_b[...] * (1.0 / HD), ext_ref[...])
        r_i = lax.broadcasted_iota(jnp.int32, (CHUNK, CHUNK), 0)
        c_i = lax.broadcasted_iota(jnp.int32, (CHUNK, CHUNK), 1)
        triu = (r_i <= c_i).astype(f32)
        da_ = _dot01(triu, dcs, exact="a")
        dxdtv = dxdt[...]
        ddt = da_ * a_ref[...] + _dot01(dxdtv * xs, ext_ref[...])
        a_da[...] += _fold8(da_ * dt)
        dxbc_ref[:, 0:SSD_W] = dyv * dsk_ref[...] + dxdtv * dt_full
        ddtr = ddt * _sigmoid(dtr_ref[...] + dtb_ref[...])
        ddtr_ref[...] = ddtr
        a_dtb[...] += _fold8(ddtr)

        @pl.when(step == nc - 1)
        def _():
            dd_ref[...] = jnp.sum(jnp.dot(a_dd[...], ext_ref[...], precision=HIGHEST, preferred_element_type=f32), axis=0, keepdims=True)
            da_ref[...] = jnp.sum(a_da[...], axis=0, keepdims=True)
            ddtb_ref[...] = jnp.sum(a_dtb[...], axis=0, keepdims=True)

    rev = lambda c: nc - 1 - c
    const = lambda shape: pl.BlockSpec(shape, lambda c: tuple(0 for _ in shape))
    return pl.pallas_call(
        body, name="ssd_bwd", grid=(nc,),
        in_specs=[pl.BlockSpec((CHUNK, XBC), lambda c: (rev(c), 0)), pl.BlockSpec((CHUNK, 128), lambda c: (rev(c), DT_COL)),
                  pl.BlockSpec((CHUNK, SSD_W), lambda c: (rev(c), 0)), pl.BlockSpec((None, NG, NSTATE, GW), lambda c: (rev(c), 0, 0, 0)),
                  const((1, 128)), const((1, 128)), const((1, SSD_W)), const((128, SSD_W)), const((SSD_W, 128)),
                  const((8, 128)), const((128, 128)), const((128, 128))],
        out_specs=[pl.BlockSpec((CHUNK, XBC), lambda c: (rev(c), 0)), pl.BlockSpec((CHUNK, 128), lambda c: (rev(c), 0)),
                   const((1, 128)), const((1, 128)), const((1, 128))],
        out_shape=[_sds((s, XBC), f32), _sds((s, 128), f32), _sds((1, 128), f32), _sds((1, 128), f32), _sds((1, 128), f32)],
        scratch_shapes=[pltpu.VMEM((NG, NSTATE, GW), f32), pltpu.VMEM((8, SSD_W), f32), pltpu.VMEM((8, 128), f32), pltpu.VMEM((8, 128), f32),
                        pltpu.VMEM((CHUNK, SSD_W), f32), pltpu.VMEM((CHUNK, SSD_W), f32), pltpu.VMEM((CHUNK, SSD_W), f32)],
        compiler_params=_params(("arbitrary",)),
    )(xbc, proj2, dy, hsave, a_row, dtb_row, dsk_full, jnp.asarray(ex), jnp.asarray(ext), jnp.asarray(sel), jnp.asarray(par),
      jnp.asarray(ones_blk))


def _local_step(x, tgt, mods, g_mix, rel, conv_w, conv_b, dt_bias, a_log, d_skip, g_att, g_ssd, g_ffn, g_final, weights):
    s = x.shape[0]
    tm_e = 256 if s % 256 == 0 else s
    tm_m = 512 if s % 512 == 0 else s
    tm_l = 1024 if s % 1024 == 0 else s
    tk = 2048 if s % 2048 == 0 else s
    sh1, sc1, gt1, sh2, sc2, gt2 = [mods[:, i * D:(i + 1) * D] for i in range(6)]

    h1b = _norm_mod("norm_mod_1", x, g_mix, sc1, sh1, tm_e)
    win, win_b = weights.w_in(h1b)
    qkv = _mm_nn_fullk("proj_qkv", h1b, win, tm_l, 768, bf16, n=IN_A)
    proj2 = _mm_nn_fullk("proj_zxbcdt", h1b, win_b, tm_l, 896, f32)
    bias = _expand_bias(rel)
    att = _attn_fwd(qkv, bias)
    xbc = _ssd_conv(proj2, conv_w, conv_b, tm_e)
    a_row = jnp.pad(-jnp.exp(a_log), ((0, 0), (0, 128 - NH)))
    dtb_row = jnp.pad(dt_bias, ((0, 0), (0, 128 - NH)))
    dsk_full = jnp.repeat(d_skip, HD, axis=1)
    y, hsave = _ssd_fwd(xbc, proj2, a_row, dtb_row, dsk_full)
    mixcat = _mix_pre(att, y, proj2, g_att, g_ssd, tm_e)
    wout = weights.w_out(mixcat)
    mix = _mm_nn_fullk("proj_out", mixcat, wout, tm_l, 1024, f32)
    x2, h2b = _resid_norm_mod(x, gt1, mix, g_ffn, sc2, sh2, tm_e)
    wg4, wu4, wd4 = weights.ffn(h2b)
    gate, up, act = _ffn_up(h2b, wg4, wu4, tm_m)
    ffn = _ffn_down(act, wd4, tm_l)

    dx3, dffn, loss, dg_final, dgt2 = _final_fwd_bwd(x2, ffn, gt2, g_final, tgt, tm_e)
    tok = weights.grad(("w_down",), [_grad_wdown4(act, dffn, 1024, tk)])
    dgate, dup = _ffn_dact(dffn, wd4, gate, up, tm_m, dep=tok)
    tok = weights.grad(("w_gate", "w_up"), [_grad_cols4("grad_w_gate", h2b, dgate, 1024, tk), _grad_cols4("grad_w_up", h2b, dup, 1024, tk)])
    dh2 = _ffn_dh(dgate, dup, wg4, wu4, tm_m, dep=tok)
    dx2, dmix, dsc2, dsh2, dg_ffn, dgt1 = _norm_mod_bwd("norm_mod_bwd_2", dh2, x2, g_ffn, sc2, dx3, tm_e, mix=mix, gt=gt1)
    tok = weights.grad(("w_out",), [_mm_tn("grad_w_out", mixcat, dmix, 1024, 1024, tk, bf16).reshape(NSH, D // NSH, D)])
    dmc = _mm_nt("dmixcat", dmix, wout, tm_l, 1024, D, f32, dep=tok)
    datt, dy, dz, dg_att, dg_ssd = _mix_pre_bwd(dmc, att, y, proj2, g_att, g_ssd, tm_e)
    dq, dk, dv, gband = _attn_bwd(qkv, datt, bias, jnp.transpose(bias, (0, 2, 1)))
    drel = _rel_bias_grad(gband.reshape(NH, CHUNK, BANDP))
    dxbc, ddtr, dd_row, da_row, ddtb_row = _ssd_bwd(xbc, proj2, dy, hsave, a_row, dtb_row, dsk_full)
    dxbc_raw, dconv_w, dconv_b = _ssd_conv_bwd(dxbc, proj2, conv_w, conv_b, tm_e)
    dproj = jnp.concatenate([dq, dk, dv, dz, dxbc_raw, ddtr.astype(bf16)], axis=1)
    gwin = _mm_tn("grad_w_in", h1b, dproj, 1024, 1152, tk, bf16)
    gwin4 = jnp.stack([jnp.pad(gwin[:, k * IN_SH:(k + 1) * IN_SH], ((0, 0), (0, IN_SHP - IN_SH))) for k in range(NSH)])
    tok = weights.grad(("w_in",), [gwin4])
    dh1 = _mm_nt("dh1", dproj, win, tm_l, 1024, 1920, f32, dep=tok)
    grad_x, dsc1, dsh1, dg_mix = _norm_mod_bwd("norm_mod_bwd_1", dh1, x, g_mix, sc1, dx2, tm_e)

    dmods = jnp.concatenate([dsh1, dsc1, dgt1, dsh2, dsc2, dgt2], axis=1)
    dd_skip = dd_row[:, :NH]
    da_log = da_row[:, :NH] * a_row[:, :NH]
    small = dict(g_mix=dg_mix, conv_b=dconv_b, dt_bias=ddtb_row[:, :NH], a_log=da_log, d_skip=dd_skip, g_att_out=dg_att,
                 g_ssd_out=dg_ssd, g_ffn=dg_ffn, g_final=dg_final, rel_bias=drel, conv_w=dconv_w)
    return loss[0, 0], grad_x, dmods, small


HBM = pl.BlockSpec(memory_space=pl.ANY)
VMEM = pl.BlockSpec(memory_space=pltpu.VMEM)


def _place():
    x, y, c = lax.axis_index("x"), lax.axis_index("y"), lax.axis_index("c")
    chips = [(1 - x, y), (x, 1 - y), (1 - x, 1 - y)]
    return x, y, c, chips


def _allgather8(name, payload, dep=None):
    r = payload.shape[0]
    deps = [] if dep is None else [dep]

    def body(x_ref, *rest):
        out_ref, send_sems, recv_sems, local_sem = rest[-4:]
        x, y, c, chips = _place()
        me, sibling = (x, y, c), (x, y, 1 - c)

        def slot(px, py, pc):
            return out_ref.at[4 * px + 2 * py + pc]

        def copy(k, block, to, src=None):
            return pltpu.make_async_remote_copy(
                src_ref=slot(*block) if src is None else src, dst_ref=slot(*block),
                send_sem=send_sems.at[k], recv_sem=recv_sems.at[k], device_id=to, device_id_type=MESH)

        mine = pltpu.make_async_copy(x_ref, slot(*me), local_sem)
        mine.start()
        first = [copy(0, me, sibling, src=x_ref)]
        first += [copy(1 + j, me, (*chip, c), src=x_ref) for j, chip in enumerate(chips)]
        for cp in first:
            cp.start()
        passed = [copy(4 + j, (*chip, c), sibling) for j, chip in enumerate(chips)]
        for j, chip in enumerate(chips):
            copy(1 + j, (*chip, c), me).wait_recv()
            passed[j].start()
        copy(0, sibling, me).wait_recv()
        for j, chip in enumerate(chips):
            copy(4 + j, (*chip, 1 - c), me).wait_recv()
        for cp in first + passed:
            cp.wait_send()
        mine.wait()

    return pl.pallas_call(
        body, name=name, out_shape=_sds((N_DEV, r, 128), f32), in_specs=[VMEM] * (1 + len(deps)), out_specs=VMEM,
        scratch_shapes=[pltpu.SemaphoreType.DMA((7,)), pltpu.SemaphoreType.DMA((7,)), pltpu.SemaphoreType.DMA],
    )(payload, *deps)


def _sum8(g):
    r = g.shape[1]

    def body(g_ref, o_ref):
        acc = g_ref[0]
        for i in range(1, N_DEV):
            acc = acc + g_ref[i]
        o_ref[...] = acc

    return pl.pallas_call(body, name="sum8", out_shape=_sds((r, 128), f32))(g)


SEM = pl.BlockSpec(memory_space=pltpu.SEMAPHORE)
EFFECT = pltpu.SideEffectType.DATAFLOW_SIDE_EFFECTING


def _gather_copies(ins, lands, send_sems, recv_sems):
    x, y, c, chips = _place()
    k = 2 * x + y
    starts, recvs = [], []
    for w in range(len(ins)):
        for j, (px, py) in enumerate(chips):
            def mk(dst):
                return pltpu.make_async_remote_copy(src_ref=ins[w].at[c], dst_ref=dst, send_sem=send_sems[w].at[j],
                                                    recv_sem=recv_sems[w].at[j], device_id=(px, py, c), device_id_type=MESH)
            starts.append(mk(lands[w].at[k, c]))
            recvs.append(mk(lands[w].at[2 * px + py, c]))
    return starts, recvs


def _reduce_copies(ins, lands, send_sems, recv_sems):
    x, y, c, chips = _place()
    k = 2 * x + y
    starts, recvs = [], []
    for w in range(len(ins)):
        for j, (px, py) in enumerate(chips):
            def mk(dst):
                return pltpu.make_async_remote_copy(src_ref=ins[w].at[2 * px + py], dst_ref=dst, send_sem=send_sems[w].at[j],
                                                    recv_sem=recv_sems[w].at[j], device_id=(px, py, c), device_id_type=MESH)
            starts.append(mk(lands[w].at[k]))
            recvs.append(mk(lands[w].at[2 * px + py]))
    return starts, recvs


def _split_start(name, copies, srcs, land_shapes):
    nw = len(srcs)

    def body(*refs):
        starts, _ = copies(refs[:nw], refs[nw:2 * nw], refs[2 * nw:3 * nw], refs[3 * nw:4 * nw])
        for cp in starts:
            cp.start()
        refs[6 * nw][...] = jnp.zeros((8, 128), f32)

    sems = [pltpu.SemaphoreType.DMA((3,))] * nw
    bufs = [pltpu.HBM(s.shape, bf16) for s in srcs] + [pltpu.HBM(s, bf16) for s in land_shapes]
    res = pl.pallas_call(
        body, name=name, out_shape=sems + sems + bufs + [_sds((8, 128), f32)],
        in_specs=[HBM] * (2 * nw), out_specs=[SEM] * (2 * nw) + [HBM] * (2 * nw) + [VMEM],
        input_output_aliases={i: 2 * nw + i for i in range(2 * nw)},
        compiler_params=pltpu.CompilerParams(has_side_effects=EFFECT),
    )(*[pltpu.with_memory_space_constraint(s, pltpu.HBM) for s in srcs],
      *[pltpu.with_memory_space_constraint(lax.empty(s, bf16), pltpu.HBM) for s in land_shapes])
    return res[:nw], res[nw:2 * nw], res[2 * nw:3 * nw], res[3 * nw:4 * nw], res[4 * nw]


def _split_wait(name, copies, send_sems, recv_sems, srcs, lands, after):
    nw = len(srcs)

    def body(*refs):
        starts, recvs = copies(refs[:nw], refs[nw:2 * nw], refs[2 * nw:3 * nw], refs[3 * nw:4 * nw])
        for s_, r_ in zip(starts, recvs):
            s_.wait_send()
            r_.wait_recv()

    bufs = [pltpu.HBM(s.shape, bf16) for s in srcs] + [pltpu.HBM(l.shape, bf16) for l in lands]
    res = pl.pallas_call(
        body, name=name, out_shape=bufs, in_specs=[HBM] * (2 * nw) + [SEM] * (2 * nw) + [HBM], out_specs=[HBM] * (2 * nw),
        input_output_aliases={i: i for i in range(2 * nw)},
        compiler_params=pltpu.CompilerParams(has_side_effects=EFFECT),
    )(*srcs, *lands, *send_sems, *recv_sems, after)
    return res[:nw], res[nw:]


def _gather_forward(name, shards, lands):
    nw = len(shards)

    def body(*refs):
        ins, lands_in, outs = refs[:nw], refs[nw:2 * nw], refs[2 * nw:3 * nw]
        st_a, st_b, st_c = refs[3 * nw:4 * nw], refs[4 * nw:5 * nw], refs[5 * nw:6 * nw]
        send_sems, recv_sems, load_sems, store_sems = refs[6 * nw:]
        x, y, c, chips = _place()
        k = 2 * x + y
        sibling = (x, y, 1 - c)
        ld_a = [pltpu.make_async_copy(ins[w].at[c], st_a[w], load_sems.at[w, 0]) for w in range(nw)]
        ld_b = [pltpu.make_async_copy(ins[w].at[1 - c], st_b[w], load_sems.at[w, 1]) for w in range(nw)]
        for cp in ld_a + ld_b:
            cp.start()
        st_own = []
        for w in range(nw):
            ld_a[w].wait()
            st_own.append(pltpu.make_async_copy(st_a[w], outs[w].at[k, c], store_sems.at[w, 0]))
            st_own[-1].start()
        for w in range(nw):
            ld_b[w].wait()
            st_own.append(pltpu.make_async_copy(st_b[w], outs[w].at[k, 1 - c], store_sems.at[w, 1]))
            st_own[-1].start()
        for cp in st_own:
            cp.wait()
        fwds = {}
        for j, (px, py) in enumerate(chips):
            kq = 2 * px + py
            for w in range(nw):
                slot = st_b[w] if j % 2 == 0 else st_c[w]
                if j == 2:
                    fwds[w, 0].wait_send()
                ld = pltpu.make_async_copy(lands_in[w].at[kq, c], slot, load_sems.at[w, 2 + j])
                ld.start()
                ld.wait()
                fwds[w, j] = pltpu.make_async_remote_copy(src_ref=slot, dst_ref=outs[w].at[kq, c], send_sem=send_sems.at[w, j],
                                                          recv_sem=recv_sems.at[w, j], device_id=sibling, device_id_type=MESH)
                fwds[w, j].start()
        for j, (px, py) in enumerate(chips):
            for w in range(nw):
                pltpu.make_async_remote_copy(src_ref=st_c[w], dst_ref=outs[w].at[2 * px + py, 1 - c], send_sem=send_sems.at[w, j],
                                             recv_sem=recv_sems.at[w, j], device_id=sibling, device_id_type=MESH).wait_recv()
        for w in range(nw):
            fwds[w, 1].wait_send()
            fwds[w, 2].wait_send()

    stage = [pltpu.VMEM(s.shape[1:], bf16) for s in shards]
    return pl.pallas_call(
        body, name=name, out_shape=[_sds(l.shape, bf16) for l in lands],
        in_specs=[HBM] * (2 * nw), out_specs=[HBM] * nw, input_output_aliases={nw + w: w for w in range(nw)},
        scratch_shapes=stage * 3 + [pltpu.SemaphoreType.DMA((nw, 3)), pltpu.SemaphoreType.DMA((nw, 3)), pltpu.SemaphoreType.DMA((nw, 5)),
                                    pltpu.SemaphoreType.DMA((nw, 2))],
        compiler_params=pltpu.CompilerParams(vmem_limit_bytes=VMEM_LIMIT),
    )(*shards, *lands)


def _rs_pair_exchange(name, grads):
    nw = len(grads)

    def body(*refs):
        ins, got, stage = refs[:nw], refs[nw:2 * nw], refs[2 * nw:3 * nw]
        send_sems, recv_sems, load_sems = refs[3 * nw:]
        x, y, c, _ = _place()

        def load(w, kk):
            return pltpu.make_async_copy(ins[w].at[kk, 1 - c], stage[w].at[kk % 2], load_sems.at[w, kk])

        def send(w, kk):
            return pltpu.make_async_remote_copy(src_ref=stage[w].at[kk % 2], dst_ref=got[w].at[kk], send_sem=send_sems.at[w, kk],
                                                recv_sem=recv_sems.at[w, kk], device_id=(x, y, 1 - c), device_id_type=MESH)

        for kk in range(2):
            for w in range(nw):
                load(w, kk).start()
        for kk in range(NSH):
            for w in range(nw):
                load(w, kk).wait()
                send(w, kk).start()
            if kk + 2 < NSH:
                for w in range(nw):
                    send(w, kk).wait_send()
                    load(w, kk + 2).start()
        for kk in range(NSH - 2, NSH):
            for w in range(nw):
                send(w, kk).wait_send()
        for kk in range(NSH):
            for w in range(nw):
                send(w, kk).wait_recv()

    return pl.pallas_call(
        body, name=name, out_shape=[_sds((NSH,) + g.shape[2:], bf16) for g in grads], in_specs=[HBM] * nw, out_specs=[HBM] * nw,
        scratch_shapes=[pltpu.VMEM((2,) + g.shape[2:], bf16) for g in grads]
        + [pltpu.SemaphoreType.DMA((nw, NSH)), pltpu.SemaphoreType.DMA((nw, NSH)), pltpu.SemaphoreType.DMA((nw, NSH))],
        compiler_params=pltpu.CompilerParams(vmem_limit_bytes=VMEM_LIMIT),
    )(*grads)


def _rs_pair_gather(name, halves):
    nw = len(halves)

    def body(*refs):
        ins, outs, stage = refs[:nw], refs[nw:2 * nw], refs[2 * nw:3 * nw]
        send_sems, recv_sems, local_sems, stage_sems = refs[3 * nw:]
        x, y, c, _ = _place()
        loads = [pltpu.make_async_copy(ins[w], stage[w], stage_sems.at[w]) for w in range(nw)]
        for cp in loads:
            cp.start()
        local, cps = [], []
        for w in range(nw):
            loads[w].wait()
            local.append(pltpu.make_async_copy(stage[w], outs[w].at[c], local_sems.at[w]))
            cps.append(pltpu.make_async_remote_copy(src_ref=stage[w], dst_ref=outs[w].at[c], send_sem=send_sems.at[w],
                                                    recv_sem=recv_sems.at[w], device_id=(x, y, 1 - c), device_id_type=MESH))
            local[w].start()
            cps[w].start()
        for w in range(nw):
            pltpu.make_async_remote_copy(src_ref=stage[w], dst_ref=outs[w].at[1 - c], send_sem=send_sems.at[w], recv_sem=recv_sems.at[w],
                                         device_id=(x, y, 1 - c), device_id_type=MESH).wait_recv()
        for cp in cps:
            cp.wait_send()
        for cp in local:
            cp.wait()

    return pl.pallas_call(
        body, name=name, out_shape=[_sds((2,) + h.shape, f32) for h in halves], in_specs=[HBM] * nw, out_specs=[HBM] * nw,
        scratch_shapes=[pltpu.VMEM(h.shape, f32) for h in halves]
        + [pltpu.SemaphoreType.DMA((nw,)), pltpu.SemaphoreType.DMA((nw,)), pltpu.SemaphoreType.DMA((nw,)), pltpu.SemaphoreType.DMA((nw,))],
        compiler_params=pltpu.CompilerParams(vmem_limit_bytes=VMEM_LIMIT),
    )(*halves)


def _row_tile(r, c, nbuf):
    budget = 24 * 1024 * 1024 // (2 * nbuf * 4 * c)
    t = 8
    while t * 2 <= budget and r % (t * 2) == 0:
        t *= 2
    return t


def _cast_bf16(name, a, dep=None):
    r, c = a.shape
    tr = _row_tile(r, c, 2)
    dep_specs, dep_ops = _dep_args(dep, 1)

    def body(a_ref, *rest):
        rest[-1][...] = a_ref[...].astype(bf16)

    spec = pl.BlockSpec((tr, c), lambda i: (i, 0))
    return pl.pallas_call(body, name=name, grid=(r // tr,), in_specs=[spec] + dep_specs, out_specs=spec, out_shape=_sds((r, c), bf16),
                          compiler_params=_params(("parallel",)))(a, *dep_ops)


def _w_in_columns(win4):
    tr = 256

    def body(a_ref, o_ref, ob_ref):
        for k in range(NSH):
            o_ref[:, IN_SH * k:IN_SH * (k + 1)] = a_ref[k][:, :IN_SH]
        o_ref[:, IN_COLS:] = jnp.zeros((tr, IN_P - IN_COLS), bf16)
        ob_ref[...] = o_ref[:, IN_A:]

    return pl.pallas_call(
        body, name="w_in_columns", grid=(D // tr,), in_specs=[pl.BlockSpec((NSH, tr, IN_SHP), lambda i: (0, i, 0))],
        out_specs=[pl.BlockSpec((tr, IN_P), lambda i: (i, 0)), pl.BlockSpec((tr, IN_B), lambda i: (i, 0))],
        out_shape=[_sds((D, IN_P), bf16), _sds((D, IN_B), bf16)], compiler_params=_params(("parallel",)))(win4)


def _pair_sum(name, core, grads, got):
    _, _, rh, c = grads.shape
    tr = _row_tile(rh, c, 2)

    def body(c_ref, a_ref, b_ref, o_ref):
        o_ref[...] = (a_ref[...].astype(f32) + b_ref[...].astype(f32)).astype(bf16)

    spec = pl.BlockSpec((None, tr, c), lambda k, i, c_ref: (k, i, 0))
    return pl.pallas_call(
        body, name=name, out_shape=_sds((NSH, rh, c), bf16),
        grid_spec=pltpu.PrefetchScalarGridSpec(
            num_scalar_prefetch=1, grid=(NSH, rh // tr),
            in_specs=[pl.BlockSpec((None, None, tr, c), lambda k, i, c_ref: (k, c_ref[0], i, 0)), spec], out_specs=spec),
        compiler_params=_params(("parallel", "parallel")))(core, grads, got)


def _chip_sum(name, chip, sums, lands):
    _, rh, c = sums.shape
    tr = _row_tile(rh, c, 4)

    def body(k_ref, own_ref, l_ref, o_ref):
        own = own_ref[...].astype(f32)
        acc = None
        for j in range(NSH):
            term = jnp.where(k_ref[0] == j, own, l_ref[j].astype(f32))
            acc = term if acc is None else acc + term
        o_ref[...] = acc

    return pl.pallas_call(
        body, name=name, out_shape=_sds((rh, c), f32),
        grid_spec=pltpu.PrefetchScalarGridSpec(
            num_scalar_prefetch=1, grid=(rh // tr,),
            in_specs=[pl.BlockSpec((None, tr, c), lambda i, k_ref: (k_ref[0], i, 0)), pl.BlockSpec((NSH, tr, c), lambda i, k_ref: (0, i, 0))],
            out_specs=pl.BlockSpec((tr, c), lambda i, k_ref: (i, 0))),
        compiler_params=_params(("parallel",)))(chip, sums, lands)


def _mods_part(cond16, w_ada, b_part):
    n = w_ada.shape[1]
    tn = 512

    def body(c_ref, w_ref, b_ref, o_ref):
        cv = c_ref[...]
        o_ref[...] = _dot(cv * _sigmoid(cv), w_ref[...]) + b_ref[...]

    return pl.pallas_call(
        body, name="mods_part", grid=(n // tn,),
        in_specs=[pl.BlockSpec((16, D), lambda j: (0, 0)), pl.BlockSpec((D, tn), lambda j: (0, j)), pl.BlockSpec((1, tn), lambda j: (0, j))],
        out_specs=pl.BlockSpec((16, tn), lambda j: (0, j)), out_shape=_sds((16, n), f32), compiler_params=_params(("parallel",)),
    )(cond16, w_ada, b_part)


def _grad_w_ada(cond16, dm16):
    n = dm16.shape[1]
    tr = 256

    def body(c_ref, d_ref, o_ref):
        cv = c_ref[...]
        o_ref[...] = _dot(cv * _sigmoid(cv), d_ref[...], ta=True)

    return pl.pallas_call(
        body, name="grad_w_ada", grid=(D // tr,),
        in_specs=[pl.BlockSpec((16, tr), lambda i: (0, i)), pl.BlockSpec((16, n), lambda i: (0, 0))],
        out_specs=pl.BlockSpec((tr, n), lambda i: (i, 0)), out_shape=_sds((D, n), f32), compiler_params=_params(("parallel",)),
    )(cond16, dm16)


def _adamw(name, w, g, m, v):
    r, c = w.shape
    tr = _row_tile(r, c, 7)

    def body(w_ref, g_ref, m_ref, v_ref, d_ref, nm_ref, nv_ref):
        gv = g_ref[...]
        nm = ADAM_B1 * m_ref[...] + (1.0 - ADAM_B1) * gv
        nv = ADAM_B2 * v_ref[...] + (1.0 - ADAM_B2) * (gv * gv)
        nm_ref[...] = nm
        nv_ref[...] = nv
        m_hat = nm / (1.0 - ADAM_B1 ** ADAM_STEP)
        v_hat = nv / (1.0 - ADAM_B2 ** ADAM_STEP)
        d_ref[...] = -ADAM_LR * (m_hat / (jnp.sqrt(v_hat) + ADAM_EPS) + ADAM_WD * w_ref[...])

    spec = pl.BlockSpec((tr, c), lambda i: (i, 0))
    return pl.pallas_call(body, name=name, grid=(r // tr,), in_specs=[spec] * 4, out_specs=[spec] * 3, out_shape=[_sds((r, c), f32)] * 3,
                          compiler_params=_params(("parallel",)))(w, g, m, v)


def _pack(parts, rows):
    flat = []
    for p in parts:
        p = p.reshape(-1)
        flat.append(jnp.pad(p, (0, (-p.shape[0]) % 128)))
    v = jnp.concatenate(flat)
    return jnp.pad(v, (0, rows * 128 - v.shape[0])).reshape(rows, 128)


def _unpack(packed, sizes):
    lead = packed.shape[:-2]
    flat = packed.reshape(lead + (-1,))
    out, off = [], 0
    for n in sizes:
        out.append(flat[..., off:off + n])
        off += n + (-n) % 128
    return out


BIG = ("w_in", "w_out", "w_gate", "w_up", "w_down")
SMALL = ("b_ada", "g_mix", "conv_b", "dt_bias", "a_log", "d_skip", "g_att_out", "g_ssd_out", "g_ffn", "g_final", "rel_bias", "conv_w")
ORDER = ("w_ada", "b_ada", "g_mix", "w_in", "rel_bias", "conv_w", "conv_b", "dt_bias", "a_log", "d_skip", "g_att_out", "g_ssd_out",
         "w_out", "g_ffn", "w_gate", "w_up", "w_down", "g_final")
REL_SH = N_REL // NSH
CONVW_SH = XBC // NSH
ADA_SH = 6 * D // NSH


class _Exchange:
    def __init__(self, core, chip):
        self.core, self.chip = core, chip
        self.gathered = {}
        self.pending = []

    def gather(self, names, shards):
        ssem, rsem, thru, lands, token = _split_start("gather_start_" + "_".join(names), _gather_copies, shards,
                                                      [(NSH,) + s.shape for s in shards])
        self.gathered.update({n: (ssem[i], rsem[i], thru[i], lands[i]) for i, n in enumerate(names)})
        return token

    def _whole(self, names, after):
        ssem, rsem, thru, lands = zip(*[self.gathered[n] for n in names])
        tag = "_".join(names)
        thru, lands = _split_wait("gather_wait_" + tag, _gather_copies, ssem, rsem, thru, lands, after)
        return _gather_forward("gather_forward_" + tag, thru, lands)

    def w_in(self, after):
        (win4,) = self._whole(("w_in",), after)
        return _w_in_columns(win4.reshape(NSH, D, IN_SHP))

    def w_out(self, after):
        (wout4,) = self._whole(("w_out",), after)
        return wout4.reshape(D, D)

    def ffn(self, after):
        wg4, wu4, wd4 = self._whole(("w_gate", "w_up", "w_down"), after)
        return wg4.reshape(NSH, D, FSH), wu4.reshape(NSH, D, FSH), wd4.reshape(NSH, FSH, D)

    def grad(self, names, grads):
        tag = "_".join(names)
        stacked = [g.reshape(NSH, 2, g.shape[1] // 2, g.shape[2]) for g in grads]
        got = _rs_pair_exchange("rs_pair_exchange_" + tag, stacked)
        sums = [_pair_sum("pair_sum_" + n, self.core, o, g) for n, o, g in zip(names, stacked, got)]
        self.pending.append((names, _split_start("rs_start_" + tag, _reduce_copies, sums, [s.shape for s in sums])))
        return self.pending[-1][1][4]

    def finish(self, after):
        grads = {}
        for names, (ssem, rsem, sums, lands, _) in self.pending:
            tag = "_".join(names)
            sums, lands = _split_wait("rs_wait_" + tag, _reduce_copies, ssem, rsem, sums, lands, after)
            halves = [_chip_sum("chip_sum_" + n, self.chip, sm, ld) for n, sm, ld in zip(names, sums, lands)]
            for n, f in zip(names, _rs_pair_gather("rs_pair_gather_" + tag, halves)):
                grads[n] = f.reshape(2 * f.shape[1], f.shape[2])
        return grads


def kernel(x, c, w_ada, b_ada, g_mix, w_in, rel_bias, conv_w, conv_b, dt_bias, a_log, d_skip, g_att_out, g_ssd_out, w_out, g_ffn, w_gate, w_up, w_down, g_final, loss_target, m_w_ada, m_b_ada, m_g_mix, m_w_in, m_rel_bias, m_conv_w, m_conv_b, m_dt_bias, m_a_log, m_d_skip, m_g_att_out, m_g_ssd_out, m_w_out, m_g_ffn, m_w_gate, m_w_up, m_w_down, m_g_final, v_w_ada, v_b_ada, v_g_mix, v_w_in, v_rel_bias, v_conv_w, v_conv_b, v_dt_bias, v_a_log, v_d_skip, v_g_att_out, v_g_ssd_out, v_w_out, v_g_ffn, v_w_gate, v_w_up, v_w_down, v_g_final):
    args = dict(locals())
    w = {n: args[n] for n in ORDER}
    m = {n: args["m_" + n] for n in ORDER}
    v = {n: args["v_" + n] for n in ORDER}
    ix, iy, ic = lax.axis_index("x"), lax.axis_index("y"), lax.axis_index("c")
    chip = 2 * ix + iy
    dev = 2 * chip + ic
    s = x.shape[1]

    g1 = _allgather8("gather_inputs", _pack([c[0], rel_bias[0], conv_w[0]], 40))
    c_all, rel_sh, convw_sh = _unpack(g1, [D, NH * REL_SH, 4 * CONVW_SH])
    rel_full = jnp.concatenate([rel_sh[2 * k].reshape(NH, REL_SH) for k in range(NSH)], axis=1)
    convw_full = jnp.concatenate([convw_sh[2 * k].reshape(4, CONVW_SH) for k in range(NSH)], axis=1)
    cond16 = jnp.pad(c_all, ((0, 8), (0, 0)))
    b_part = lax.dynamic_slice_in_dim(b_ada, chip * ADA_SH, ADA_SH, axis=1)
    mods_part = _mods_part(cond16, w_ada[0], b_part)[:N_DEV]
    g2 = _allgather8("gather_mods", mods_part.reshape(N_DEV * ADA_SH // 128, 128))
    mods_all = jnp.concatenate([g2[2 * k].reshape(N_DEV, ADA_SH) for k in range(NSH)], axis=1)
    mods = lax.dynamic_slice_in_dim(mods_all, dev, 1, axis=0)

    exchange = _Exchange(jnp.reshape(ic, (1,)).astype(jnp.int32), jnp.reshape(chip, (1,)).astype(jnp.int32))
    shard_in = _cast_bf16("cast_w_in", jnp.pad(w_in[0], ((0, 0), (0, IN_SHP - IN_SH))), dep=g2[0, :8]).reshape(2, D // 2, IN_SHP)
    tok = exchange.gather(("w_in",), [shard_in])
    tok = exchange.gather(("w_out", "w_gate", "w_up", "w_down"), [
        _cast_bf16("cast_w_out", w_out[0], dep=tok).reshape(2, D // NSH // 2, D),
        _cast_bf16("cast_w_gate", w_gate[0], dep=tok).reshape(2, D // 2, FSH),
        _cast_bf16("cast_w_up", w_up[0], dep=tok).reshape(2, D // 2, FSH),
        _cast_bf16("cast_w_down", w_down[0], dep=tok).reshape(2, FSH // 2, D)])
    mods = mods + tok[:1, :1]

    loss, grad_x, dmods, small = _local_step(
        x[0], loss_target[0], mods, g_mix, rel_full, convw_full, conv_b, dt_bias, a_log, d_skip, g_att_out, g_ssd_out, g_ffn,
        g_final[None, :], exchange)

    small_names = ("g_mix", "conv_b", "dt_bias", "a_log", "d_skip", "g_att_out", "g_ssd_out", "g_ffn", "g_final", "rel_bias", "conv_w")
    g3 = _allgather8("gather_small_grads", _pack([dmods] + [small[n] for n in small_names], 264))
    sizes = [6 * D] + [int(np.prod(small[n].shape)) for n in small_names]
    dmods_all = _unpack(g3, sizes)[0]
    summed = _unpack(_sum8(g3), sizes)
    grads = {"b_ada": summed[0].reshape(1, 6 * D)}
    for n, val in zip(small_names, summed[1:]):
        grads[n] = val.reshape(small[n].shape)
    grads["rel_bias"] = lax.dynamic_slice_in_dim(grads["rel_bias"], chip * REL_SH, REL_SH, axis=1)
    grads["conv_w"] = lax.dynamic_slice_in_dim(grads["conv_w"], chip * CONVW_SH, CONVW_SH, axis=1)
    grads["g_final"] = grads["g_final"].reshape(D)
    dm16 = jnp.pad(lax.dynamic_slice_in_dim(dmods_all, chip * ADA_SH, ADA_SH, axis=1), ((0, 8), (0, 0)))
    grads["w_ada"] = _grad_w_ada(cond16, dm16)

    delta, new_m, new_v = {}, {}, {}
    delta["w_ada"], new_m["w_ada"], new_v["w_ada"] = _adamw("adamw_w_ada", w_ada[0], grads["w_ada"], m_w_ada[0], v_w_ada[0])
    grads.update(exchange.finish(grad_x))
    grads["w_in"] = grads["w_in"][:, :IN_SH]
    for n in BIG:
        delta[n], new_m[n], new_v[n] = _adamw("adamw_" + n, w[n][0], grads[n], m[n][0], v[n][0])
    sw = _pack([w[n] for n in SMALL], 200)
    sg = _pack([grads[n] for n in SMALL], 200)
    sm = _pack([m[n] for n in SMALL], 200)
    sv = _pack([v[n] for n in SMALL], 200)
    ssz = [int(np.prod(w[n].shape)) for n in SMALL]
    for dst, packed in zip((delta, new_m, new_v), _adamw("adamw_small", sw, sg, sm, sv)):
        for n, val in zip(SMALL, _unpack(packed, ssz)):
            dst[n] = val

    def shaped(d, n):
        return d[n].reshape(w[n].shape)

    total = lax.psum(loss, ("x", "y", "c"))
    return (total, grad_x[None], *[shaped(grads, n) for n in ORDER], *[shaped(delta, n) for n in ORDER],
            *[shaped(new_m, n) for n in ORDER], *[shaped(new_v, n) for n in ORDER])
```

```python
import functools

import numpy as np
import jax
import jax.numpy as jnp
from jax import lax
from jax.experimental import pallas as pl
from jax.experimental.pallas import tpu as pltpu

f32 = jnp.float32
bf16 = jnp.bfloat16
HIGHEST = lax.Precision.HIGHEST
MESH = pl.DeviceIdType.MESH

D = 2048
CHUNK = 64
LEFT = 8
BAND = (LEFT + 1) * CHUNK
BANDP = 640
PADK = LEFT * CHUNK
NH = 16
HD = 64
ATT_W = NH * HD
SSD_W = 1024
NG = 2
NSTATE = 128
GW = SSD_W // NG
XBC = SSD_W + 2 * NG * NSTATE
N_REL = 320
REL_CLIP = 256
FFN = 5632
NSH = 4
FSH = FFN // NSH
IN_COLS = 5648
IN_SH = IN_COLS // NSH
IN_SHP = 1536
IN_A = 3 * ATT_W
IN_B = 2688
IN_P = IN_A + IN_B
EPS = 1e-6
N_DEV = 8

ADAM_LR = 0.001
ADAM_B1 = 0.9
ADAM_B2 = 0.999
ADAM_EPS = 1e-08
ADAM_WD = 0.01
ADAM_STEP = 10

VMEM_LIMIT = 56 * 1024 * 1024


def _params(sem):
    return pltpu.CompilerParams(dimension_semantics=sem, vmem_limit_bytes=VMEM_LIMIT)


def _sds(shape, dtype):
    return jax.ShapeDtypeStruct(shape, dtype)


def _fold8(v):
    r, w = v.shape
    return jnp.sum(v.reshape(r // 8, 8, w), axis=0)


STRIP = 16


def _strips(tm, fn):
    def step(j, carry):
        fn(pl.ds(pl.multiple_of(j * STRIP, STRIP), STRIP))
        return carry
    lax.fori_loop(0, tm // STRIP, step, 0, unroll=4)


def _sigmoid(v):
    return 1.0 / (1.0 + jnp.exp(-v))


def _softplus(v):
    return jnp.maximum(v, 0.0) + jnp.log(1.0 + jnp.exp(-jnp.abs(v)))


def _dot(a, b, ta=False, tb=False):
    dn = (((0 if ta else 1,), (1 if tb else 0,)), ((), ()))
    return lax.dot_general(a.astype(bf16), b.astype(bf16), dn, preferred_element_type=f32)


def _dep_args(dep, ngrid):
    if dep is None:
        return [], []
    return [pl.BlockSpec((8, 128), lambda *_: (0, 0))], [dep]


def _dot01(a, b, ta=False, tb=False, exact="b"):
    dn = (((0 if ta else 1,), (1 if tb else 0,)), ((), ()))
    x = a if exact == "b" else b
    hi = x.astype(bf16)
    r = x - hi.astype(f32)
    mid = r.astype(bf16)
    lo = (r - mid.astype(f32)).astype(bf16)
    if exact == "b":
        m = b.astype(bf16)
        return sum(lax.dot_general(p, m, dn, preferred_element_type=f32) for p in (hi, mid, lo))
    m = a.astype(bf16)
    return sum(lax.dot_general(m, p, dn, preferred_element_type=f32) for p in (hi, mid, lo))


def _matmul(name, a, b, *, grid, a_spec, b_spec, o_spec, o_shape, o_dtype, acc_shape, ta=False, tb=False, dep=None):
    nk = grid[2]
    dep_specs, dep_ops = _dep_args(dep, 3)

    def body(a_ref, b_ref, *rest):
        o_ref, acc_ref = rest[-2:]
        p = _dot(a_ref[...], b_ref[...], ta, tb)
        if nk == 1:
            o_ref[...] = p.astype(o_ref.dtype)
        else:
            k = pl.program_id(2)

            @pl.when(k == 0)
            def _():
                acc_ref[...] = p

            @pl.when(jnp.logical_and(k > 0, k < nk - 1))
            def _():
                acc_ref[...] += p

            @pl.when(k == nk - 1)
            def _():
                o_ref[...] = (acc_ref[...] + p).astype(o_ref.dtype)

    return pl.pallas_call(
        body, name=name, grid=grid, in_specs=[a_spec, b_spec] + dep_specs, out_specs=o_spec,
        out_shape=_sds(o_shape, o_dtype), scratch_shapes=[pltpu.VMEM(acc_shape if nk > 1 else (8, 128), f32)],
        compiler_params=_params(("parallel", "parallel", "arbitrary")),
    )(a, b, *dep_ops)


def _mm_nn_fullk(name, a, b, tm, tn, o_dtype, n=None):
    m, k = a.shape
    n = b.shape[1] if n is None else n
    return _matmul(name, a, b, grid=(m // tm, n // tn, 1),
                   a_spec=pl.BlockSpec((tm, k), lambda i, j, kk: (i, 0)),
                   b_spec=pl.BlockSpec((k, tn), lambda i, j, kk: (0, j)),
                   o_spec=pl.BlockSpec((tm, tn), lambda i, j, kk: (i, j)),
                   o_shape=(m, n), o_dtype=o_dtype, acc_shape=(tm, tn))


def _mm_nt(name, a, b, tm, tn, tk, o_dtype, dep=None):
    m, k = a.shape
    n = b.shape[0]
    return _matmul(name, a, b, grid=(m // tm, n // tn, k // tk), tb=True, dep=dep,
                   a_spec=pl.BlockSpec((tm, tk), lambda i, j, kk: (i, kk)),
                   b_spec=pl.BlockSpec((tn, tk), lambda i, j, kk: (j, kk)),
                   o_spec=pl.BlockSpec((tm, tn), lambda i, j, kk: (i, j)),
                   o_shape=(m, n), o_dtype=o_dtype, acc_shape=(tm, tn))


def _mm_tn(name, a, b, tm, tn, tk, o_dtype):
    k, m = a.shape
    n = b.shape[1]
    return _matmul(name, a, b, grid=(m // tm, n // tn, k // tk), ta=True,
                   a_spec=pl.BlockSpec((tk, tm), lambda i, j, kk: (kk, i)),
                   b_spec=pl.BlockSpec((tk, tn), lambda i, j, kk: (kk, j)),
                   o_spec=pl.BlockSpec((tm, tn), lambda i, j, kk: (i, j)),
                   o_shape=(m, n), o_dtype=o_dtype, acc_shape=(tm, tn))


FSH_PARTS = (slice(0, 640), slice(640, FSH))


def _ffn_up(h2b, wg4, wu4, tm):
    s = h2b.shape[0]

    def body(h_ref, wg_ref, wu_ref, a_ref, s_ref, ud_ref):
        h = h_ref[...]
        for cols in FSH_PARTS:
            g = _dot(h, wg_ref[:, cols])
            u = _dot(h, wu_ref[:, cols])
            sg = _sigmoid(g)
            sil = g * sg
            a_ref[:, cols] = (sil * u).astype(bf16)
            s_ref[:, cols] = sil.astype(bf16)
            ud_ref[:, cols] = (u * (sg * (1.0 + g * (1.0 - sg)))).astype(bf16)

    wspec = pl.BlockSpec((None, D, FSH), lambda k, i: (k, 0, 0))
    ospec = pl.BlockSpec((tm, FSH), lambda k, i: (i, k))
    return pl.pallas_call(
        body, name="ffn_up", grid=(NSH, s // tm),
        in_specs=[pl.BlockSpec((tm, D), lambda k, i: (i, 0)), wspec, wspec],
        out_specs=[ospec, ospec, ospec], out_shape=[_sds((s, FFN), bf16)] * 3,
        compiler_params=_params(("parallel", "parallel")),
    )(h2b, wg4, wu4)


def _ffn_down(act, wd4, tm):
    s = act.shape[0]
    return _matmul("ffn_down", act, wd4, grid=(s // tm, 1, NSH),
                   a_spec=pl.BlockSpec((tm, FSH), lambda i, j, k: (i, k)),
                   b_spec=pl.BlockSpec((None, FSH, D), lambda i, j, k: (k, 0, 0)),
                   o_spec=pl.BlockSpec((tm, D), lambda i, j, k: (i, 0)),
                   o_shape=(s, D), o_dtype=f32, acc_shape=(tm, D))


def _ffn_dact(dffn, wd4, sil, ud, tm, dep=None):
    s = dffn.shape[0]
    dep_specs, dep_ops = _dep_args(dep, 2)

    def body(d_ref, w_ref, s_ref, ud_ref, *rest):
        dg_ref, du_ref = rest[-2:]
        d = d_ref[...]
        for cols in FSH_PARTS:
            dact = _dot(d, w_ref[cols, :], tb=True)
            dg_ref[:, cols] = (dact * ud_ref[:, cols].astype(f32)).astype(bf16)
            du_ref[:, cols] = (dact * s_ref[:, cols].astype(f32)).astype(bf16)

    blk = pl.BlockSpec((tm, FSH), lambda k, i: (i, k))
    return pl.pallas_call(
        body, name="ffn_dact", grid=(NSH, s // tm),
        in_specs=[pl.BlockSpec((tm, D), lambda k, i: (i, 0)), pl.BlockSpec((None, FSH, D), lambda k, i: (k, 0, 0)), blk, blk] + dep_specs,
        out_specs=[blk, blk], out_shape=[_sds((s, FFN), bf16), _sds((s, FFN), bf16)],
        compiler_params=_params(("parallel", "parallel")),
    )(dffn, wd4, sil, ud, *dep_ops)


def _ffn_dh(dgate, dup, wg4, wu4, tm, dep=None):
    s = dgate.shape[0]
    dep_specs, dep_ops = _dep_args(dep, 2)

    def body(dg_ref, du_ref, wg_ref, wu_ref, *rest):
        o_ref, acc_ref = rest[-2:]
        k = pl.program_id(1)
        p = _dot(dg_ref[...], wg_ref[...], tb=True) + _dot(du_ref[...], wu_ref[...], tb=True)

        @pl.when(k == 0)
        def _():
            acc_ref[...] = p

        @pl.when(jnp.logical_and(k > 0, k < NSH - 1))
        def _():
            acc_ref[...] += p

        @pl.when(k == NSH - 1)
        def _():
            o_ref[...] = acc_ref[...] + p

    aspec = pl.BlockSpec((tm, FSH), lambda i, k: (i, k))
    wspec = pl.BlockSpec((None, D, FSH), lambda i, k: (k, 0, 0))
    return pl.pallas_call(
        body, name="ffn_dh", grid=(s // tm, NSH), in_specs=[aspec, aspec, wspec, wspec] + dep_specs,
        out_specs=pl.BlockSpec((tm, D), lambda i, k: (i, 0)), out_shape=_sds((s, D), f32),
        scratch_shapes=[pltpu.VMEM((tm, D), f32)], compiler_params=_params(("parallel", "arbitrary")),
    )(dgate, dup, wg4, wu4, *dep_ops)


def _grad_cols4(name, h, dy, tm, tk):
    s = h.shape[0]
    return _matmul(name, h, dy, grid=(NSH, D // tm, s // tk), ta=True,
                   a_spec=pl.BlockSpec((tk, tm), lambda k, i, kk: (kk, i)),
                   b_spec=pl.BlockSpec((tk, FSH), lambda k, i, kk: (kk, k)),
                   o_spec=pl.BlockSpec((None, tm, FSH), lambda k, i, kk: (k, i, 0)),
                   o_shape=(NSH, D, FSH), o_dtype=bf16, acc_shape=(tm, FSH))


def _grad_wdown4(act, dffn, tn, tk):
    s = act.shape[0]
    return _matmul("grad_w_down", act, dffn, grid=(NSH, D // tn, s // tk), ta=True,
                   a_spec=pl.BlockSpec((tk, FSH), lambda k, j, kk: (kk, k)),
                   b_spec=pl.BlockSpec((tk, tn), lambda k, j, kk: (kk, j)),
                   o_spec=pl.BlockSpec((None, FSH, tn), lambda k, j, kk: (k, 0, j)),
                   o_shape=(NSH, FSH, D), o_dtype=bf16, acc_shape=(FSH, tn))


def _row_spec(w):
    return pl.BlockSpec((1, w), lambda i: (0, 0))


def _tile_spec(tm, w, col=0):
    return pl.BlockSpec((tm, w), lambda i: (i, col))


def _norm_mod(name, x, g, sc, sh, tm):
    s = x.shape[0]

    def body(x_ref, g_ref, sc_ref, sh_ref, o_ref):
        def strip(rows):
            xv = x_ref[rows, :]
            r = lax.rsqrt(jnp.mean(xv * xv, axis=-1, keepdims=True) + EPS)
            o_ref[rows, :] = (xv * r * g_ref[...] * (1.0 + sc_ref[...]) + sh_ref[...]).astype(bf16)

        _strips(tm, strip)

    return pl.pallas_call(
        body, name=name, grid=(s // tm,), in_specs=[_tile_spec(tm, D), _row_spec(D), _row_spec(D), _row_spec(D)],
        out_specs=_tile_spec(tm, D), out_shape=_sds((s, D), bf16), compiler_params=_params(("parallel",)),
    )(x, g, sc, sh)


def _resid_norm_mod(x, gt, mix, g, sc, sh, tm):
    s = x.shape[0]

    def body(x_ref, gt_ref, m_ref, g_ref, sc_ref, sh_ref, x2_ref, h_ref):
        def strip(rows):
            xv = x_ref[rows, :] + gt_ref[...] * m_ref[rows, :]
            x2_ref[rows, :] = xv
            r = lax.rsqrt(jnp.mean(xv * xv, axis=-1, keepdims=True) + EPS)
            h_ref[rows, :] = (xv * r * g_ref[...] * (1.0 + sc_ref[...]) + sh_ref[...]).astype(bf16)

        _strips(tm, strip)

    return pl.pallas_call(
        body, name="resid_norm_mod", grid=(s // tm,),
        in_specs=[_tile_spec(tm, D), _row_spec(D), _tile_spec(tm, D), _row_spec(D), _row_spec(D), _row_spec(D)],
        out_specs=[_tile_spec(tm, D), _tile_spec(tm, D)], out_shape=[_sds((s, D), f32), _sds((s, D), bf16)],
        compiler_params=_params(("parallel",)),
    )(x, gt, mix, g, sc, sh)


def _final_fwd_bwd(x2, ffn, gt2, g, tgt, tm):
    s = x2.shape[0]
    n = s // tm

    def body(x_ref, f_ref, gt_ref, g_ref, t_ref, dx_ref, df_ref, loss_ref, dg_ref, dgt_ref, a_loss, a_dg, a_dgt):
        i = pl.program_id(0)

        @pl.when(i == 0)
        def _():
            a_loss[...] = jnp.zeros_like(a_loss)
            a_dg[...] = jnp.zeros_like(a_dg)
            a_dgt[...] = jnp.zeros_like(a_dgt)

        def strip(rows):
            fv = f_ref[rows, :]
            gt = gt_ref[...]
            gv = g_ref[...]
            xv = x_ref[rows, :] + gt * fv
            r = lax.rsqrt(jnp.mean(xv * xv, axis=-1, keepdims=True) + EPS)
            xh = xv * r
            e = xh * gv - t_ref[rows, :]
            a_loss[...] += _fold8(e * e)
            dy = e * (1.0 / D)
            a_dg[...] += _fold8(dy * xh)
            t = dy * gv
            dx = r * (t - xh * jnp.mean(t * xh, axis=-1, keepdims=True))
            dx_ref[rows, :] = dx
            a_dgt[...] += _fold8(dx * fv)
            df_ref[rows, :] = (dx * gt).astype(bf16)

        _strips(tm, strip)

        @pl.when(i == n - 1)
        def _():
            tot = jnp.sum(jnp.sum(a_loss[...], axis=0, keepdims=True), axis=1, keepdims=True) * (0.5 / D)
            loss_ref[...] = jnp.broadcast_to(tot, (1, 128))
            dg_ref[...] = jnp.sum(a_dg[...], axis=0, keepdims=True)
            dgt_ref[...] = jnp.sum(a_dgt[...], axis=0, keepdims=True)

    return pl.pallas_call(
        body, name="final_fwd_bwd", grid=(n,),
        in_specs=[_tile_spec(tm, D), _tile_spec(tm, D), _row_spec(D), _row_spec(D), _tile_spec(tm, D)],
        out_specs=[_tile_spec(tm, D), _tile_spec(tm, D), _row_spec(128), _row_spec(D), _row_spec(D)],
        out_shape=[_sds((s, D), f32), _sds((s, D), bf16), _sds((1, 128), f32), _sds((1, D), f32), _sds((1, D), f32)],
        scratch_shapes=[pltpu.VMEM((8, D), f32)] * 3, compiler_params=_params(("arbitrary",)),
    )(x2, ffn, gt2, g, tgt)


def _norm_mod_bwd(name, dh, xin, g, sc, dres, tm, mix=None, gt=None):
    s = dh.shape[0]
    n = s // tm
    with_mix = mix is not None

    def body(*refs):
        if with_mix:
            dh_ref, x_ref, g_ref, sc_ref, dr_ref, m_ref, gt_ref, dx_ref, dm_ref, dsc_ref, dsh_ref, dg_ref, dgt_ref, a_sc, a_sh, a_g, a_gt = refs
        else:
            dh_ref, x_ref, g_ref, sc_ref, dr_ref, dx_ref, dsc_ref, dsh_ref, dg_ref, a_sc, a_sh, a_g = refs
        i = pl.program_id(0)

        @pl.when(i == 0)
        def _():
            a_sc[...] = jnp.zeros_like(a_sc)
            a_sh[...] = jnp.zeros_like(a_sh)
            a_g[...] = jnp.zeros_like(a_g)
            if with_mix:
                a_gt[...] = jnp.zeros_like(a_gt)

        def strip(rows):
            dh = dh_ref[rows, :]
            xv = x_ref[rows, :]
            gv = g_ref[...]
            r = lax.rsqrt(jnp.mean(xv * xv, axis=-1, keepdims=True) + EPS)
            xh = xv * r
            a_sc[...] += _fold8(dh * xh * gv)
            a_sh[...] += _fold8(dh)
            dn = dh * (1.0 + sc_ref[...])
            a_g[...] += _fold8(dn * xh)
            t = dn * gv
            dx = dr_ref[rows, :] + r * (t - xh * jnp.mean(t * xh, axis=-1, keepdims=True))
            dx_ref[rows, :] = dx
            if with_mix:
                a_gt[...] += _fold8(dx * m_ref[rows, :])
                dm_ref[rows, :] = (dx * gt_ref[...]).astype(bf16)

        _strips(tm, strip)

        @pl.when(i == n - 1)
        def _():
            dsc_ref[...] = jnp.sum(a_sc[...], axis=0, keepdims=True)
            dsh_ref[...] = jnp.sum(a_sh[...], axis=0, keepdims=True)
            dg_ref[...] = jnp.sum(a_g[...], axis=0, keepdims=True)
            if with_mix:
                dgt_ref[...] = jnp.sum(a_gt[...], axis=0, keepdims=True)

    tile, row = _tile_spec(tm, D), _row_spec(D)
    if with_mix:
        ins, args = [tile, tile, row, row, tile, tile, row], (dh, xin, g, sc, dres, mix, gt)
        outs = [tile, tile, row, row, row, row]
        shapes = [_sds((s, D), f32), _sds((s, D), bf16)] + [_sds((1, D), f32)] * 4
        nacc = 4
    else:
        ins, args = [tile, tile, row, row, tile], (dh, xin, g, sc, dres)
        outs = [tile, row, row, row]
        shapes = [_sds((s, D), f32)] + [_sds((1, D), f32)] * 3
        nacc = 3
    return pl.pallas_call(
        body, name=name, grid=(n,), in_specs=ins, out_specs=outs, out_shape=shapes,
        scratch_shapes=[pltpu.VMEM((8, D), f32)] * nacc, compiler_params=_params(("arbitrary",)),
    )(*args)


def _mix_pre(att, y, proj2, g_att, g_ssd, tm):
    s = att.shape[0]

    def body(a_ref, y_ref, z_ref, ga_ref, gs_ref, o_ref):
        def strip(rows):
            a = a_ref[rows, :]
            ra = lax.rsqrt(jnp.mean(a * a, axis=-1, keepdims=True) + EPS)
            o_ref[rows, 0:ATT_W] = (a * ra * ga_ref[...]).astype(bf16)
            z = z_ref[rows, :]
            u = y_ref[rows, :] * (z * _sigmoid(z))
            ru = lax.rsqrt(jnp.mean(u * u, axis=-1, keepdims=True) + EPS)
            o_ref[rows, ATT_W:] = (u * ru * gs_ref[...]).astype(bf16)

        _strips(tm, strip)

    t = _tile_spec(tm, ATT_W)
    return pl.pallas_call(
        body, name="mix_pre", grid=(s // tm,), in_specs=[t, t, t, _row_spec(ATT_W), _row_spec(SSD_W)],
        out_specs=_tile_spec(tm, D), out_shape=_sds((s, D), bf16), compiler_params=_params(("parallel",)),
    )(att, y, proj2, g_att, g_ssd)


def _mix_pre_bwd(dmc, att, y, proj2, g_att, g_ssd, tm):
    s = att.shape[0]
    n = s // tm

    def body(da_ref, ds_ref, a_ref, y_ref, z_ref, ga_ref, gs_ref, datt_ref, dy_ref, dz_ref, dga_ref, dgs_ref, acc_a, acc_s):
        i = pl.program_id(0)

        @pl.when(i == 0)
        def _():
            acc_a[...] = jnp.zeros_like(acc_a)
            acc_s[...] = jnp.zeros_like(acc_s)

        def strip(rows):
            a = a_ref[rows, :]
            ra = lax.rsqrt(jnp.mean(a * a, axis=-1, keepdims=True) + EPS)
            ah = a * ra
            dan = da_ref[rows, :]
            acc_a[...] += _fold8(dan * ah)
            t = dan * ga_ref[...]
            datt_ref[rows, :] = (ra * (t - ah * jnp.mean(t * ah, axis=-1, keepdims=True))).astype(bf16)
            z = z_ref[rows, :]
            yv = y_ref[rows, :]
            sz = _sigmoid(z)
            sil = z * sz
            u = yv * sil
            ru = lax.rsqrt(jnp.mean(u * u, axis=-1, keepdims=True) + EPS)
            uh = u * ru
            dsn = ds_ref[rows, :]
            acc_s[...] += _fold8(dsn * uh)
            t2 = dsn * gs_ref[...]
            du = ru * (t2 - uh * jnp.mean(t2 * uh, axis=-1, keepdims=True))
            dy_ref[rows, :] = du * sil
            dz_ref[rows, :] = (du * yv * (sz * (1.0 + z * (1.0 - sz)))).astype(bf16)

        _strips(tm, strip)

        @pl.when(i == n - 1)
        def _():
            dga_ref[...] = jnp.sum(acc_a[...], axis=0, keepdims=True)
            dgs_ref[...] = jnp.sum(acc_s[...], axis=0, keepdims=True)

    t = _tile_spec(tm, ATT_W)
    row = _row_spec(ATT_W)
    return pl.pallas_call(
        body, name="mix_pre_bwd", grid=(n,),
        in_specs=[_tile_spec(tm, ATT_W, 0), _tile_spec(tm, ATT_W, 1), t, t, t, row, row],
        out_specs=[t, t, t, row, row],
        out_shape=[_sds((s, ATT_W), bf16), _sds((s, SSD_W), f32), _sds((s, SSD_W), bf16), _sds((1, ATT_W), f32), _sds((1, SSD_W), f32)],
        scratch_shapes=[pltpu.VMEM((8, ATT_W), f32)] * 2, compiler_params=_params(("arbitrary",)),
    )(dmc, dmc, att, y, proj2, g_att, g_ssd)


ATT_GROUP = 4
ATT_GROUP_FWD = 4


def _pair_rows(qc):
    two = jnp.concatenate([qc, qc], axis=0)
    r = lax.broadcasted_iota(jnp.int32, (2 * CHUNK, 128), 0)
    l = lax.broadcasted_iota(jnp.int32, (2 * CHUNK, 128), 1)
    return jnp.where((r < CHUNK) == (l < HD), two, jnp.zeros_like(two))


def _scaled(q):
    return q * jnp.asarray(HD ** -0.5, q.dtype)


def _pair_scores(wt, kb, bias, r0, masked):
    sc = lax.dot_general(wt, kb, (((1,), (1,)), ((), ())), preferred_element_type=f32) + bias
    if not masked:
        return sc
    kidx = lax.broadcasted_iota(jnp.int32, sc.shape, 1)
    return jnp.where(r0 + kidx >= PADK, sc, -jnp.inf)


def _softmax(sc, axis):
    e = jnp.exp(sc - jnp.max(sc, axis=axis, keepdims=True))
    return e * (1.0 / jnp.sum(e, axis=axis, keepdims=True))


def _chunk_loops(nc, group, per_trip):
    n_masked = min(-(-LEFT // per_trip), nc // per_trip)

    def run(masked):
        def step(g, carry):
            group(g, masked)
            return carry
        return step

    lax.fori_loop(0, n_masked, run(True), 0)
    lax.fori_loop(n_masked, nc // per_trip, run(False), 0)


def _pair_diag(r):
    lane = lax.broadcasted_iota(jnp.int32, (CHUNK, 128), 1)
    return jnp.where(lane < HD, r[0:CHUNK], r[CHUNK:])


def _pad_keys(k_ref, kp, s):
    kp[0:PADK, :] = jnp.zeros((PADK, 128), bf16)
    kp[PADK:PADK + s, :] = k_ref[...]
    kp[PADK + s:, :] = jnp.zeros((CHUNK, 128), bf16)


def _attn_fwd(qkv, bias2):
    s = qkv.shape[0]
    nc = s // CHUNK
    npair = NH // 2

    def body(q_ref, k_ref, v_ref, b_ref, o_ref, kp, vp):
        _pad_keys(k_ref, kp, s)
        _pad_keys(v_ref, vp, s)

        def group(g, masked):
            r0s = [pl.multiple_of((g * ATT_GROUP_FWD + u) * CHUNK, CHUNK) for u in range(ATT_GROUP_FWD)]
            scs = [_pair_scores(_pair_rows(_scaled(q_ref[pl.ds(r0, CHUNK), :])), kp[pl.ds(r0, BANDP), :], b_ref[...], r0, masked)
                   for r0 in r0s]
            ps = [_softmax(sc, -1).astype(bf16) for sc in scs]
            for r0, p in zip(r0s, ps):
                o_ref[pl.ds(r0, CHUNK), :] = _pair_diag(jnp.dot(p, vp[pl.ds(r0, BANDP), :], preferred_element_type=f32))

        _chunk_loops(nc, group, ATT_GROUP_FWD)

    return pl.pallas_call(
        body, name="attn_fwd", grid=(npair,),
        in_specs=[pl.BlockSpec((s, 128), lambda p: (0, p)), pl.BlockSpec((s, 128), lambda p: (0, npair + p)),
                  pl.BlockSpec((s, 128), lambda p: (0, 2 * npair + p)), pl.BlockSpec((None, 2 * CHUNK, BANDP), lambda p: (p, 0, 0))],
        out_specs=pl.BlockSpec((s, 128), lambda p: (0, p)), out_shape=_sds((s, ATT_W), f32),
        scratch_shapes=[pltpu.VMEM((PADK + s + CHUNK, 128), bf16)] * 2, compiler_params=_params(("parallel",)),
    )(qkv, qkv, qkv, bias2)


def _attn_bwd(qkv, datt, bias2, bias2t):
    s = qkv.shape[0]
    nc = s // CHUNK
    npair = NH // 2
    rows = PADK + s + CHUNK
    nt = (((1,), (1,)), ((), ()))

    def body(q_ref, k_ref, v_ref, do_ref, b_ref, bt_ref, dq_ref, dk_ref, dv_ref, g_ref, kp, vp, dkp, dvp):
        _pad_keys(k_ref, kp, s)
        _pad_keys(v_ref, vp, s)
        dkp[...] = jnp.zeros_like(dkp)
        dvp[...] = jnp.zeros_like(dvp)
        g_ref[...] = jnp.zeros_like(g_ref)

        def group(g, masked):
            r0s = [pl.multiple_of((g * ATT_GROUP + u) * CHUNK, CHUNK) for u in range(ATT_GROUP)]
            wts = [_pair_rows(_scaled(q_ref[pl.ds(r0, CHUNK), :])) for r0 in r0s]
            dos = [_pair_rows(do_ref[pl.ds(r0, CHUNK), :]) for r0 in r0s]
            scs = [_pair_scores(wt, kp[pl.ds(r0, BANDP), :], b_ref[...], r0, masked) for wt, r0 in zip(wts, r0s)]
            dps = [lax.dot_general(do, vp[pl.ds(r0, BANDP), :], nt, preferred_element_type=f32) for do, r0 in zip(dos, r0s)]
            scts, dpts = [], []
            for wt, do, r0 in zip(wts, dos, r0s):
                sct = lax.dot_general(kp[pl.ds(r0, BANDP), :], wt, nt, preferred_element_type=f32) + bt_ref[...]
                if masked:
                    kidx = lax.broadcasted_iota(jnp.int32, sct.shape, 0)
                    sct = jnp.where(r0 + kidx >= PADK, sct, -jnp.inf)
                scts.append(sct)
                dpts.append(lax.dot_general(vp[pl.ds(r0, BANDP), :], do, nt, preferred_element_type=f32))
            for r0, sc, dp in zip(r0s, scs, dps):
                p = _softmax(sc, -1)
                ds = p * (dp - jnp.sum(p * dp, axis=-1, keepdims=True))
                g_ref[...] += ds
                dq = jnp.dot(ds.astype(bf16), kp[pl.ds(r0, BANDP), :], preferred_element_type=f32)
                dq_ref[pl.ds(r0, CHUNK), :] = (_pair_diag(dq) * (HD ** -0.5)).astype(bf16)
            for r0, wt, do, sct, dpt in zip(r0s, wts, dos, scts, dpts):
                pt = _softmax(sct, 0)
                dst = pt * (dpt - jnp.sum(pt * dpt, axis=0, keepdims=True))
                dkp[pl.ds(r0, BANDP), :] += jnp.dot(dst.astype(bf16), wt, preferred_element_type=f32)
                dvp[pl.ds(r0, BANDP), :] += jnp.dot(pt.astype(bf16), do, preferred_element_type=f32)

        _chunk_loops(nc, group, ATT_GROUP)
        dk_ref[...] = dkp[PADK:PADK + s, :].astype(bf16)
        dv_ref[...] = dvp[PADK:PADK + s, :].astype(bf16)

    col = lambda off: pl.BlockSpec((s, 128), lambda p: (0, off + p))
    return pl.pallas_call(
        body, name="attn_bwd", grid=(npair,),
        in_specs=[col(0), col(npair), col(2 * npair), col(0), pl.BlockSpec((None, 2 * CHUNK, BANDP), lambda p: (p, 0, 0)),
                  pl.BlockSpec((None, BANDP, 2 * CHUNK), lambda p: (p, 0, 0))],
        out_specs=[col(0), col(0), col(0), pl.BlockSpec((None, 2 * CHUNK, BANDP), lambda p: (p, 0, 0))],
        out_shape=[_sds((s, ATT_W), bf16)] * 3 + [_sds((npair, 2 * CHUNK, BANDP), f32)],
        scratch_shapes=[pltpu.VMEM((rows, 128), bf16)] * 2 + [pltpu.VMEM((rows, 128), f32)] * 2,
        compiler_params=_params(("parallel",)),
    )(qkv, qkv, qkv, datt, bias2, bias2t)


def _rel_tables():
    onehot = np.zeros((BANDP, N_REL), np.float32)
    for j in range(BAND + CHUNK - 1):
        o = j - (CHUNK - 1)
        onehot[j, int(np.clip(PADK - o, -(CHUNK - 1), REL_CLIP)) + CHUNK - 1] = 1.0
    return onehot, np.ascontiguousarray(np.eye(CHUNK, dtype=np.float32)[::-1])


def _expand_bias(rel):
    ext = jnp.concatenate([jnp.broadcast_to(rel[:, N_REL - 1:], (NH, N_REL - 1)), rel[:, ::-1],
                           jnp.zeros((NH, BANDP - BAND + 1), f32)], axis=1)
    band = jnp.stack([ext[:, CHUNK - 1 - q:CHUNK - 1 - q + BANDP] for q in range(CHUNK)], axis=1)
    band = jnp.where(np.arange(BANDP) < BAND, band, -jnp.inf)
    return band.reshape(NH // 2, 2 * CHUNK, BANDP)


def _rel_bias_grad(gband):
    def body(g_ref, m_ref, flip_ref, o_ref, d2):
        for h in range(NH):
            rev = jnp.dot(flip_ref[...], g_ref[h], precision=HIGHEST, preferred_element_type=f32)
            rolled = pltpu.roll(rev, 0, 1, stride=1, stride_axis=0)
            d2[h:h + 1, :] = jnp.sum(rolled, axis=0, keepdims=True)
        o_ref[...] = jnp.dot(d2[...], m_ref[...], precision=HIGHEST, preferred_element_type=f32)

    onehot, flip = _rel_tables()
    return pl.pallas_call(
        body, name="rel_bias_grad", out_shape=_sds((NH, N_REL), f32), scratch_shapes=[pltpu.VMEM((NH, BANDP), f32)],
    )(gband, jnp.asarray(onehot), jnp.asarray(flip))


XBC_BLK = 512
XBC_COL0 = SSD_W // XBC_BLK
DT_COL = (SSD_W + XBC) // 128


def _conv_taps(ext, w_ref, b_ref, tm):
    n = ext.shape[0]
    pre = w_ref[3:4, :] * ext + b_ref[...]
    for j in range(3):
        pre = pre + w_ref[j:j + 1, :] * pltpu.roll(ext, 3 - j, 0)
    return pre


def _ssd_conv(proj2, conv_w, conv_b, tm):
    s = proj2.shape[0]
    nb = XBC // XBC_BLK

    def body(x_ref, p_ref, w_ref, b_ref, o_ref):
        i = pl.program_id(1)
        prev = jnp.where(i > 0, p_ref[...], 0.0)
        ext = jnp.concatenate([prev, x_ref[...]], axis=0)
        pre = _conv_taps(ext, w_ref, b_ref, tm)[8:8 + tm]
        o_ref[...] = pre * _sigmoid(pre)

    return pl.pallas_call(
        body, name="ssd_conv", grid=(nb, s // tm),
        in_specs=[pl.BlockSpec((tm, XBC_BLK), lambda j, i: (i, XBC_COL0 + j)),
                  pl.BlockSpec((8, XBC_BLK), lambda j, i: (jnp.maximum(i * (tm // 8) - 1, 0), XBC_COL0 + j)),
                  pl.BlockSpec((4, XBC_BLK), lambda j, i: (0, j)), pl.BlockSpec((1, XBC_BLK), lambda j, i: (0, j))],
        out_specs=pl.BlockSpec((tm, XBC_BLK), lambda j, i: (i, j)), out_shape=_sds((s, XBC), f32),
        compiler_params=_params(("parallel", "parallel")),
    )(proj2, proj2, conv_w, conv_b)


def _ssd_conv_bwd(dxbc, proj2, conv_w, conv_b, tm):
    s = proj2.shape[0]
    nb = XBC // XBC_BLK
    n = s // tm
    last8 = s // 8 - 1

    def body(x_ref, xp_ref, xn_ref, d_ref, dn_ref, w_ref, b_ref, o_ref, dw_ref, db_ref):
        i = pl.program_id(1)

        @pl.when(i == 0)
        def _():
            dw_ref[...] = jnp.zeros_like(dw_ref)
            db_ref[...] = jnp.zeros_like(db_ref)

        prev = jnp.where(i > 0, xp_ref[...], 0.0)
        ext = jnp.concatenate([prev, x_ref[...], xn_ref[...]], axis=0)
        pre = _conv_taps(ext, w_ref, b_ref, tm)
        sg = _sigmoid(pre)
        dnext = jnp.where(i < n - 1, dn_ref[...], 0.0)
        dext = jnp.concatenate([jnp.zeros((8, XBC_BLK), f32), d_ref[...], dnext], axis=0)
        dpre = dext * (sg * (1.0 + pre * (1.0 - sg)))
        rows = tm + 16
        dx = w_ref[3:4, :] * dpre
        for j in range(3):
            dx = dx + w_ref[j:j + 1, :] * pltpu.roll(dpre, rows - (3 - j), 0)
        o_ref[...] = dx[8:8 + tm].astype(bf16)
        dcur = dpre[8:8 + tm]
        db_ref[...] += jnp.sum(dcur, axis=0, keepdims=True)
        dw_ref[3:4, :] += jnp.sum(dcur * ext[8:8 + tm], axis=0, keepdims=True)
        for j in range(3):
            dw_ref[j:j + 1, :] += jnp.sum(dcur * pltpu.roll(ext, 3 - j, 0)[8:8 + tm], axis=0, keepdims=True)

    xcol = lambda j: XBC_COL0 + j
    return pl.pallas_call(
        body, name="ssd_conv_bwd", grid=(nb, n),
        in_specs=[pl.BlockSpec((tm, XBC_BLK), lambda j, i: (i, xcol(j))),
                  pl.BlockSpec((8, XBC_BLK), lambda j, i: (jnp.maximum(i * (tm // 8) - 1, 0), xcol(j))),
                  pl.BlockSpec((8, XBC_BLK), lambda j, i: (jnp.minimum((i + 1) * (tm // 8), last8), xcol(j))),
                  pl.BlockSpec((tm, XBC_BLK), lambda j, i: (i, j)),
                  pl.BlockSpec((8, XBC_BLK), lambda j, i: (jnp.minimum((i + 1) * (tm // 8), last8), j)),
                  pl.BlockSpec((4, XBC_BLK), lambda j, i: (0, j)), pl.BlockSpec((1, XBC_BLK), lambda j, i: (0, j))],
        out_specs=[pl.BlockSpec((tm, XBC_BLK), lambda j, i: (i, j)), pl.BlockSpec((4, XBC_BLK), lambda j, i: (0, j)),
                   pl.BlockSpec((1, XBC_BLK), lambda j, i: (0, j))],
        out_shape=[_sds((s, XBC), bf16), _sds((4, XBC), f32), _sds((1, XBC), f32)],
        compiler_params=_params(("parallel", "arbitrary")),
    )(proj2, proj2, proj2, dxbc, dxbc, conv_w, conv_b)


def _ssd_consts():
    ex = np.zeros((128, SSD_W), np.float32)
    for h in range(NH):
        ex[h, h * HD:(h + 1) * HD] = 1.0
    sel = np.zeros((8, 128), np.float32)
    for h in range(NH):
        sel[h // 2, h] = 1.0
    par = np.zeros((128, 128), np.float32)
    for r in range(128):
        for h in range(NH):
            par[r, h] = 1.0 if (h % 2) == (r // 64) else 0.0
    ones_blk = np.zeros((128, 128), np.float32)
    for r in range(128):
        ones_blk[r, (r // 64) * 64:(r // 64) * 64 + 64] = 1.0
    return ex, np.ascontiguousarray(ex.T), sel, par, ones_blk


def _ssd_common(xbc_ref, dtr_ref, a_ref, dtb_ref, ex_ref, sel_ref, par_ref):
    xs = xbc_ref[:, 0:SSD_W]
    dt = _softplus(dtr_ref[...] + dtb_ref[...])
    adt = dt * a_ref[...]
    r_i = lax.broadcasted_iota(jnp.int32, (CHUNK, CHUNK), 0)
    c_i = lax.broadcasted_iota(jnp.int32, (CHUNK, CHUNK), 1)
    tril = (r_i >= c_i).astype(f32)
    cs = _dot01(tril, adt, exact="a")
    cs2 = jnp.concatenate([cs, cs], axis=0) * par_ref[...]
    cstp = _dot01(sel_ref[...], cs2, tb=True, exact="a")
    ex = ex_ref[...]
    dt_full = _dot01(dt, ex)
    cs_full = _dot01(cs, ex)
    return xs, dt, cs, cstp, dt_full, cs_full


def _pair_mask():
    l_i = lax.broadcasted_iota(jnp.int32, (CHUNK, 128), 0)
    lane = lax.broadcasted_iota(jnp.int32, (CHUNK, 128), 1)
    return l_i >= (lane % CHUNK), lane < HD


def _block_diag(xp, first):
    z = jnp.zeros_like(xp)
    return jnp.concatenate([jnp.where(first, xp, z), jnp.where(first, z, xp)], axis=0)


def _ssd_fwd(xbc, proj2, a_row, dtb_row, dsk_full):
    s = xbc.shape[0]
    nc = s // CHUNK
    ex, ext, sel, par, ones_blk = _ssd_consts()

    def body(xbc_ref, dtr_ref, a_ref, dtb_ref, dsk_ref, ex_ref, sel_ref, par_ref, y_ref, hs_ref, hst):
        @pl.when(pl.program_id(0) == 0)
        def _():
            hst[...] = jnp.zeros_like(hst)

        hs_ref[...] = hst[...]
        xs, dt, cs, cstp, dt_full, cs_full = _ssd_common(xbc_ref, dtr_ref, a_ref, dtb_ref, ex_ref, sel_ref, par_ref)
        cs_last = cs_full[CHUNK - 1:CHUNK, :]
        xdt = xs * dt_full
        causal, first = _pair_mask()
        for g in range(NG):
            gl = slice(g * GW, (g + 1) * GW)
            bg = xbc_ref[:, SSD_W + g * NSTATE:SSD_W + (g + 1) * NSTATE].astype(bf16)
            cg = xbc_ref[:, SSD_W + NG * NSTATE + g * NSTATE:SSD_W + NG * NSTATE + (g + 1) * NSTATE].astype(bf16)
            cb2 = lax.dot_general(cg, jnp.concatenate([bg, bg], axis=0), (((1,), (1,)), ((), ())), preferred_element_type=f32)
            hg = hst[g]
            y0 = jnp.dot(cg, hg.astype(bf16), preferred_element_type=f32)
            yoff = jnp.exp(cs_full[:, gl]) * y0
            for j in range(GW // 128):
                pair = g * (GW // 128) + j
                pl_ = slice(pair * 128, (pair + 1) * 128)
                seg = jnp.exp(jnp.where(causal, cs_full[:, pl_] - cstp[pair:pair + 1, :], -jnp.inf))
                m = (cb2 * seg).astype(bf16)
                yd = jnp.dot(m, _block_diag(xdt[:, pl_].astype(bf16), first), preferred_element_type=f32)
                y_ref[:, pl_] = yd + yoff[:, j * 128:(j + 1) * 128] + xs[:, pl_] * dsk_ref[:, pl_]
            xdec = (xdt[:, gl] * jnp.exp(cs_last[:, gl] - cs_full[:, gl])).astype(bf16)
            st = lax.dot_general(bg, xdec, (((0,), (0,)), ((), ())), preferred_element_type=f32)
            hst[g] = jnp.exp(cs_last[:, gl]) * hg + st

    const = lambda shape: pl.BlockSpec(shape, lambda c: tuple(0 for _ in shape))
    return pl.pallas_call(
        body, name="ssd_fwd", grid=(nc,),
        in_specs=[pl.BlockSpec((CHUNK, XBC), lambda c: (c, 0)), pl.BlockSpec((CHUNK, 128), lambda c: (c, DT_COL)),
                  const((1, 128)), const((1, 128)), const((1, SSD_W)), const((128, SSD_W)), const((8, 128)), const((128, 128))],
        out_specs=[pl.BlockSpec((CHUNK, SSD_W), lambda c: (c, 0)), pl.BlockSpec((None, NG, NSTATE, GW), lambda c: (c, 0, 0, 0))],
        out_shape=[_sds((s, SSD_W), f32), _sds((nc, NG, NSTATE, GW), f32)],
        scratch_shapes=[pltpu.VMEM((NG, NSTATE, GW), f32)], compiler_params=_params(("arbitrary",)),
    )(xbc, proj2, a_row, dtb_row, dsk_full, jnp.asarray(ex), jnp.asarray(sel), jnp.asarray(par))


def _ssd_bwd(xbc, proj2, dy, hsave, a_row, dtb_row, dsk_full):
    s = xbc.shape[0]
    nc = s // CHUNK
    ex, ext, sel, par, ones_blk = _ssd_consts()

    def body(xbc_ref, dtr_ref, dy_ref, hs_ref, a_ref, dtb_ref, dsk_ref, ex_ref, ext_ref, sel_ref, par_ref, ob_ref,
             dxbc_ref, ddtr_ref, dd_ref, da_ref, ddtb_ref, dh, a_dd, a_da, a_dtb, dcs_lane, dcs_b, dxdt):
        step = pl.program_id(0)

        @pl.when(step == 0)
        def _():
            dh[...] = jnp.zeros_like(dh)
            a_dd[...] = jnp.zeros_like(a_dd)
            a_da[...] = jnp.zeros_like(a_da)
            a_dtb[...] = jnp.zeros_like(a_dtb)

        xs, dt, cs, cstp, dt_full, cs_full = _ssd_common(xbc_ref, dtr_ref, a_ref, dtb_ref, ex_ref, sel_ref, par_ref)
        cs_last = cs_full[CHUNK - 1:CHUNK, :]
        xdt = xs * dt_full
        dyv = dy_ref[...]
        a_dd[...] += _fold8(dyv * xs)
        causal, first = _pair_mask()
        ones_l = jnp.ones((CHUNK, 128), f32)
        for g in range(NG):
            gl = slice(g * GW, (g + 1) * GW)
            bcol = slice(SSD_W + g * NSTATE, SSD_W + (g + 1) * NSTATE)
            ccol = slice(SSD_W + NG * NSTATE + g * NSTATE, SSD_W + NG * NSTATE + (g + 1) * NSTATE)
            bg = xbc_ref[:, bcol].astype(bf16)
            cg = xbc_ref[:, ccol].astype(bf16)
            bg2 = jnp.concatenate([bg, bg], axis=0)
            cb2 = lax.dot_general(cg, bg2, (((1,), (1,)), ((), ())), preferred_element_type=f32)
            hg = hs_ref[g]
            hgb = hg.astype(bf16)
            dhg = dh[g]
            dhgb = dhg.astype(bf16)
            eg = jnp.exp(cs_full[:, gl])
            dec = jnp.exp(cs_last[:, gl] - cs_full[:, gl])
            gam = jnp.exp(cs_last[:, gl])
            dyg = dyv[:, gl]
            xdt_g = xdt[:, gl]
            y0 = jnp.dot(cg, hgb, preferred_element_type=f32)
            dy0 = (eg * dyg).astype(bf16)
            dcm = lax.dot_general(dy0, hgb, (((1,), (1,)), ((), ())), preferred_element_type=f32)
            dh_prev = gam * dhg + lax.dot_general(cg, dy0, (((0,), (0,)), ((), ())), preferred_element_type=f32)
            dgam = jnp.sum(dhg * hg, axis=0, keepdims=True) * gam
            dxdec = jnp.dot(bg, dhgb, preferred_element_type=f32)
            dbm = lax.dot_general((xdt_g * dec).astype(bf16), dhgb, (((1,), (1,)), ((), ())), preferred_element_type=f32)
            t = dxdec * xdt_g * dec
            dcs_lane[:, gl] = dyg * eg * y0 - t
            dcs_lane[CHUNK - 1:CHUNK, gl] += jnp.sum(t, axis=0, keepdims=True) + dgam
            dxdt[:, gl] = dxdec * dec
            dcb2 = jnp.zeros((CHUNK, 128), f32)
            for j in range(GW // 128):
                pair = g * (GW // 128) + j
                pl_ = slice(pair * 128, (pair + 1) * 128)
                seg = jnp.exp(jnp.where(causal, cs_full[:, pl_] - cstp[pair:pair + 1, :], -jnp.inf))
                m = cb2 * seg
                mb = m.astype(bf16)
                rhs = _block_diag(xdt[:, pl_].astype(bf16), first)
                dyp = dyv[:, pl_].astype(bf16)
                dm = lax.dot_general(dyp, rhs, (((1,), (1,)), ((), ())), preferred_element_type=f32)
                tt = lax.dot_general(mb, dyp, (((0,), (0,)), ((), ())), preferred_element_type=f32)
                dxdt[:, pl_] += jnp.where(first, tt[0:CHUNK], tt[CHUNK:])
                dcb2 = dcb2 + dm * seg
                w = dm * m
                rsum = _dot01(w, ob_ref[...])
                t2 = _dot01(w, ones_l, ta=True)
                dcs_b[:, pl_] = rsum - jnp.where(first, t2[0:CHUNK], t2[CHUNK:])
            dcb2b = dcb2.astype(bf16)
            dcm = dcm + jnp.dot(dcb2b, bg2, preferred_element_type=f32)
            t3 = lax.dot_general(dcb2b, cg, (((0,), (0,)), ((), ())), preferred_element_type=f32)
            dxbc_ref[:, bcol] = dbm + t3[0:CHUNK] + t3[CHUNK:]
            dxbc_ref[:, ccol] = dcm
            dh[g] = dh_prev
        dcs = _dot01(dcs_lane[...] + dcs_b[...] * (1.0 / HD), ext_ref[...])
        r_i = lax.broadcasted_iota(jnp.int32, (CHUNK, CHUNK), 0)
        c_i = lax.broadcasted_iota(jnp.int32, (CHUNK, CHUNK), 1)
        triu = (r_i <= c_i).astype(f32)
        da_ = _dot01(triu, dcs, exact="a")
        dxdtv = dxdt[...]
        ddt = da_ * a_ref[...] + _dot01(dxdtv * xs, ext_ref[...])
        a_da[...] += _fold8(da_ * dt)
        dxbc_ref[:, 0:SSD_W] = dyv * dsk_ref[...] + dxdtv * dt_full
        ddtr = ddt * _sigmoid(dtr_ref[...] + dtb_ref[...])
        ddtr_ref[...] = ddtr
        a_dtb[...] += _fold8(ddtr)

        @pl.when(step == nc - 1)
        def _():
            dd_ref[...] = jnp.sum(jnp.dot(a_dd[...], ext_ref[...], precision=HIGHEST, preferred_element_type=f32), axis=0, keepdims=True)
            da_ref[...] = jnp.sum(a_da[...], axis=0, keepdims=True)
            ddtb_ref[...] = jnp.sum(a_dtb[...], axis=0, keepdims=True)

    rev = lambda c: nc - 1 - c
    const = lambda shape: pl.BlockSpec(shape, lambda c: tuple(0 for _ in shape))
    return pl.pallas_call(
        body, name="ssd_bwd", grid=(nc,),
        in_specs=[pl.BlockSpec((CHUNK, XBC), lambda c: (rev(c), 0)), pl.BlockSpec((CHUNK, 128), lambda c: (rev(c), DT_COL)),
                  pl.BlockSpec((CHUNK, SSD_W), lambda c: (rev(c), 0)), pl.BlockSpec((None, NG, NSTATE, GW), lambda c: (rev(c), 0, 0, 0)),
                  const((1, 128)), const((1, 128)), const((1, SSD_W)), const((128, SSD_W)), const((SSD_W, 128)),
                  const((8, 128)), const((128, 128)), const((128, 128))],
        out_specs=[pl.BlockSpec((CHUNK, XBC), lambda c: (rev(c), 0)), pl.BlockSpec((CHUNK, 128), lambda c: (rev(c), 0)),
                   const((1, 128)), const((1, 128)), const((1, 128))],
        out_shape=[_sds((s, XBC), f32), _sds((s, 128), f32), _sds((1, 128), f32), _sds((1, 128), f32), _sds((1, 128), f32)],
        scratch_shapes=[pltpu.VMEM((NG, NSTATE, GW), f32), pltpu.VMEM((8, SSD_W), f32), pltpu.VMEM((8, 128), f32), pltpu.VMEM((8, 128), f32),
                        pltpu.VMEM((CHUNK, SSD_W), f32), pltpu.VMEM((CHUNK, SSD_W), f32), pltpu.VMEM((CHUNK, SSD_W), f32)],
        compiler_params=_params(("arbitrary",)),
    )(xbc, proj2, dy, hsave, a_row, dtb_row, dsk_full, jnp.asarray(ex), jnp.asarray(ext), jnp.asarray(sel), jnp.asarray(par),
      jnp.asarray(ones_blk))


def _local_step(x, tgt, mods, g_mix, rel, conv_w, conv_b, dt_bias, a_log, d_skip, g_att, g_ssd, g_ffn, g_final, weights):
    s = x.shape[0]
    tm_e = 256 if s % 256 == 0 else s
    tm_m = 512 if s % 512 == 0 else s
    tm_l = 1024 if s % 1024 == 0 else s
    tk = 2048 if s % 2048 == 0 else s
    sh1, sc1, gt1, sh2, sc2, gt2 = [mods[:, i * D:(i + 1) * D] for i in range(6)]

    h1b = _norm_mod("norm_mod_1", x, g_mix, sc1, sh1, tm_e)
    win, win_b = weights.w_in(h1b)
    qkv = _mm_nn_fullk("proj_qkv", h1b, win, tm_l, 768, bf16, n=IN_A)
    proj2 = _mm_nn_fullk("proj_zxbcdt", h1b, win_b, tm_l, 896, f32)
    bias = _expand_bias(rel)
    att = _attn_fwd(qkv, bias)
    xbc = _ssd_conv(proj2, conv_w, conv_b, tm_e)
    a_row = jnp.pad(-jnp.exp(a_log), ((0, 0), (0, 128 - NH)))
    dtb_row = jnp.pad(dt_bias, ((0, 0), (0, 128 - NH)))
    dsk_full = jnp.repeat(d_skip, HD, axis=1)
    y, hsave = _ssd_fwd(xbc, proj2, a_row, dtb_row, dsk_full)
    mixcat = _mix_pre(att, y, proj2, g_att, g_ssd, tm_e)
    wout = weights.w_out(mixcat)
    mix = _mm_nn_fullk("proj_out", mixcat, wout, tm_l, 1024, f32)
    x2, h2b = _resid_norm_mod(x, gt1, mix, g_ffn, sc2, sh2, tm_e)
    wg4, wu4, wd4 = weights.ffn(h2b)
    act, sil, ud = _ffn_up(h2b, wg4, wu4, tm_m)
    ffn = _ffn_down(act, wd4, tm_l)

    dx3, dffn, loss, dg_final, dgt2 = _final_fwd_bwd(x2, ffn, gt2, g_final, tgt, tm_e)
    tok = weights.grad(("w_down",), [_grad_wdown4(act, dffn, 1024, tk)])
    dgate, dup = _ffn_dact(dffn, wd4, sil, ud, tm_l, dep=tok)
    tok = weights.grad(("w_gate", "w_up"), [_grad_cols4("grad_w_gate", h2b, dgate, 1024, tk), _grad_cols4("grad_w_up", h2b, dup, 1024, tk)])
    dh2 = _ffn_dh(dgate, dup, wg4, wu4, tm_m, dep=tok)
    dx2, dmix, dsc2, dsh2, dg_ffn, dgt1 = _norm_mod_bwd("norm_mod_bwd_2", dh2, x2, g_ffn, sc2, dx3, tm_e, mix=mix, gt=gt1)
    tok = weights.grad(("w_out",), [_mm_tn("grad_w_out", mixcat, dmix, 1024, 1024, tk, bf16).reshape(NSH, D // NSH, D)])
    dmc = _mm_nt("dmixcat", dmix, wout, tm_l, 1024, D, f32, dep=tok)
    datt, dy, dz, dg_att, dg_ssd = _mix_pre_bwd(dmc, att, y, proj2, g_att, g_ssd, tm_e)
    dq, dk, dv, gband = _attn_bwd(qkv, datt, bias, jnp.transpose(bias, (0, 2, 1)))
    drel = _rel_bias_grad(gband.reshape(NH, CHUNK, BANDP))
    dxbc, ddtr, dd_row, da_row, ddtb_row = _ssd_bwd(xbc, proj2, dy, hsave, a_row, dtb_row, dsk_full)
    dxbc_raw, dconv_w, dconv_b = _ssd_conv_bwd(dxbc, proj2, conv_w, conv_b, tm_e)
    dproj = jnp.concatenate([dq, dk, dv, dz, dxbc_raw, ddtr.astype(bf16)], axis=1)
    gwin = _mm_tn("grad_w_in", h1b, dproj, 1024, 1152, tk, bf16)
    gwin4 = jnp.stack([jnp.pad(gwin[:, k * IN_SH:(k + 1) * IN_SH], ((0, 0), (0, IN_SHP - IN_SH))) for k in range(NSH)])
    tok = weights.grad(("w_in",), [gwin4])
    dh1 = _mm_nt("dh1", dproj, win, tm_l, 1024, 1920, f32, dep=tok)
    grad_x, dsc1, dsh1, dg_mix = _norm_mod_bwd("norm_mod_bwd_1", dh1, x, g_mix, sc1, dx2, tm_e)

    dmods = jnp.concatenate([dsh1, dsc1, dgt1, dsh2, dsc2, dgt2], axis=1)
    dd_skip = dd_row[:, :NH]
    da_log = da_row[:, :NH] * a_row[:, :NH]
    small = dict(g_mix=dg_mix, conv_b=dconv_b, dt_bias=ddtb_row[:, :NH], a_log=da_log, d_skip=dd_skip, g_att_out=dg_att,
                 g_ssd_out=dg_ssd, g_ffn=dg_ffn, g_final=dg_final, rel_bias=drel, conv_w=dconv_w)
    return loss[0, 0], grad_x, dmods, small


HBM = pl.BlockSpec(memory_space=pl.ANY)
VMEM = pl.BlockSpec(memory_space=pltpu.VMEM)


def _place():
    x, y, c = lax.axis_index("x"), lax.axis_index("y"), lax.axis_index("c")
    chips = [(1 - x, y), (x, 1 - y), (1 - x, 1 - y)]
    return x, y, c, chips


def _allgather8(name, payload, dep=None):
    r = payload.shape[0]
    deps = [] if dep is None else [dep]

    def body(x_ref, *rest):
        out_ref, send_sems, recv_sems, local_sem = rest[-4:]
        x, y, c, chips = _place()
        me, sibling = (x, y, c), (x, y, 1 - c)

        def slot(px, py, pc):
            return out_ref.at[4 * px + 2 * py + pc]

        def copy(k, block, to, src=None):
            return pltpu.make_async_remote_copy(
                src_ref=slot(*block) if src is None else src, dst_ref=slot(*block),
                send_sem=send_sems.at[k], recv_sem=recv_sems.at[k], device_id=to, device_id_type=MESH)

        mine = pltpu.make_async_copy(x_ref, slot(*me), local_sem)
        mine.start()
        first = [copy(0, me, sibling, src=x_ref)]
        first += [copy(1 + j, me, (*chip, c), src=x_ref) for j, chip in enumerate(chips)]
        for cp in first:
            cp.start()
        passed = [copy(4 + j, (*chip, c), sibling) for j, chip in enumerate(chips)]
        for j, chip in enumerate(chips):
            copy(1 + j, (*chip, c), me).wait_recv()
            passed[j].start()
        copy(0, sibling, me).wait_recv()
        for j, chip in enumerate(chips):
            copy(4 + j, (*chip, 1 - c), me).wait_recv()
        for cp in first + passed:
            cp.wait_send()
        mine.wait()

    return pl.pallas_call(
        body, name=name, out_shape=_sds((N_DEV, r, 128), f32), in_specs=[VMEM] * (1 + len(deps)), out_specs=VMEM,
        scratch_shapes=[pltpu.SemaphoreType.DMA((7,)), pltpu.SemaphoreType.DMA((7,)), pltpu.SemaphoreType.DMA],
    )(payload, *deps)


def _sum8(g):
    r = g.shape[1]

    def body(g_ref, o_ref):
        acc = g_ref[0]
        for i in range(1, N_DEV):
            acc = acc + g_ref[i]
        o_ref[...] = acc

    return pl.pallas_call(body, name="sum8", out_shape=_sds((r, 128), f32))(g)


SEM = pl.BlockSpec(memory_space=pltpu.SEMAPHORE)
EFFECT = pltpu.SideEffectType.DATAFLOW_SIDE_EFFECTING


def _gather_copies(ins, lands, send_sems, recv_sems):
    x, y, c, chips = _place()
    k = 2 * x + y
    starts, recvs = [], []
    for w in range(len(ins)):
        for j, (px, py) in enumerate(chips):
            def mk(dst):
                return pltpu.make_async_remote_copy(src_ref=ins[w].at[c], dst_ref=dst, send_sem=send_sems[w].at[j],
                                                    recv_sem=recv_sems[w].at[j], device_id=(px, py, c), device_id_type=MESH)
            starts.append(mk(lands[w].at[k, c]))
            recvs.append(mk(lands[w].at[2 * px + py, c]))
    return starts, recvs


def _reduce_copies(ins, lands, send_sems, recv_sems):
    x, y, c, chips = _place()
    k = 2 * x + y
    starts, recvs = [], []
    for w in range(len(ins)):
        for j, (px, py) in enumerate(chips):
            def mk(dst):
                return pltpu.make_async_remote_copy(src_ref=ins[w].at[2 * px + py], dst_ref=dst, send_sem=send_sems[w].at[j],
                                                    recv_sem=recv_sems[w].at[j], device_id=(px, py, c), device_id_type=MESH)
            starts.append(mk(lands[w].at[k]))
            recvs.append(mk(lands[w].at[2 * px + py]))
    return starts, recvs


def _split_start(name, copies, srcs, land_shapes):
    nw = len(srcs)

    def body(*refs):
        starts, _ = copies(refs[:nw], refs[nw:2 * nw], refs[2 * nw:3 * nw], refs[3 * nw:4 * nw])
        for cp in starts:
            cp.start()
        refs[6 * nw][...] = jnp.zeros((8, 128), f32)

    sems = [pltpu.SemaphoreType.DMA((3,))] * nw
    bufs = [pltpu.HBM(s.shape, bf16) for s in srcs] + [pltpu.HBM(s, bf16) for s in land_shapes]
    res = pl.pallas_call(
        body, name=name, out_shape=sems + sems + bufs + [_sds((8, 128), f32)],
        in_specs=[HBM] * (2 * nw), out_specs=[SEM] * (2 * nw) + [HBM] * (2 * nw) + [VMEM],
        input_output_aliases={i: 2 * nw + i for i in range(2 * nw)},
        compiler_params=pltpu.CompilerParams(has_side_effects=EFFECT),
    )(*[pltpu.with_memory_space_constraint(s, pltpu.HBM) for s in srcs],
      *[pltpu.with_memory_space_constraint(lax.empty(s, bf16), pltpu.HBM) for s in land_shapes])
    return res[:nw], res[nw:2 * nw], res[2 * nw:3 * nw], res[3 * nw:4 * nw], res[4 * nw]


def _split_wait(name, copies, send_sems, recv_sems, srcs, lands, after):
    nw = len(srcs)

    def body(*refs):
        starts, recvs = copies(refs[:nw], refs[nw:2 * nw], refs[2 * nw:3 * nw], refs[3 * nw:4 * nw])
        for s_, r_ in zip(starts, recvs):
            s_.wait_send()
            r_.wait_recv()

    bufs = [pltpu.HBM(s.shape, bf16) for s in srcs] + [pltpu.HBM(l.shape, bf16) for l in lands]
    res = pl.pallas_call(
        body, name=name, out_shape=bufs, in_specs=[HBM] * (2 * nw) + [SEM] * (2 * nw) + [HBM], out_specs=[HBM] * (2 * nw),
        input_output_aliases={i: i for i in range(2 * nw)},
        compiler_params=pltpu.CompilerParams(has_side_effects=EFFECT),
    )(*srcs, *lands, *send_sems, *recv_sems, after)
    return res[:nw], res[nw:]


def _gather_forward(name, shards, lands):
    nw = len(shards)

    def body(*refs):
        ins, lands_in, outs = refs[:nw], refs[nw:2 * nw], refs[2 * nw:3 * nw]
        st_a, st_b, st_c = refs[3 * nw:4 * nw], refs[4 * nw:5 * nw], refs[5 * nw:6 * nw]
        send_sems, recv_sems, load_sems, store_sems = refs[6 * nw:]
        x, y, c, chips = _place()
        k = 2 * x + y
        sibling = (x, y, 1 - c)
        ld_a = [pltpu.make_async_copy(ins[w].at[c], st_a[w], load_sems.at[w, 0]) for w in range(nw)]
        ld_b = [pltpu.make_async_copy(ins[w].at[1 - c], st_b[w], load_sems.at[w, 1]) for w in range(nw)]
        for cp in ld_a + ld_b:
            cp.start()
        st_own = []
        for w in range(nw):
            ld_a[w].wait()
            st_own.append(pltpu.make_async_copy(st_a[w], outs[w].at[k, c], store_sems.at[w, 0]))
            st_own[-1].start()
        for w in range(nw):
            ld_b[w].wait()
            st_own.append(pltpu.make_async_copy(st_b[w], outs[w].at[k, 1 - c], store_sems.at[w, 1]))
            st_own[-1].start()
        for cp in st_own:
            cp.wait()
        fwds = {}
        for j, (px, py) in enumerate(chips):
            kq = 2 * px + py
            for w in range(nw):
                slot = st_b[w] if j % 2 == 0 else st_c[w]
                if j == 2:
                    fwds[w, 0].wait_send()
                ld = pltpu.make_async_copy(lands_in[w].at[kq, c], slot, load_sems.at[w, 2 + j])
                ld.start()
                ld.wait()
                fwds[w, j] = pltpu.make_async_remote_copy(src_ref=slot, dst_ref=outs[w].at[kq, c], send_sem=send_sems.at[w, j],
                                                          recv_sem=recv_sems.at[w, j], device_id=sibling, device_id_type=MESH)
                fwds[w, j].start()
        for j, (px, py) in enumerate(chips):
            for w in range(nw):
                pltpu.make_async_remote_copy(src_ref=st_c[w], dst_ref=outs[w].at[2 * px + py, 1 - c], send_sem=send_sems.at[w, j],
                                             recv_sem=recv_sems.at[w, j], device_id=sibling, device_id_type=MESH).wait_recv()
        for w in range(nw):
            fwds[w, 1].wait_send()
            fwds[w, 2].wait_send()

    stage = [pltpu.VMEM(s.shape[1:], bf16) for s in shards]
    return pl.pallas_call(
        body, name=name, out_shape=[_sds(l.shape, bf16) for l in lands],
        in_specs=[HBM] * (2 * nw), out_specs=[HBM] * nw, input_output_aliases={nw + w: w for w in range(nw)},
        scratch_shapes=stage * 3 + [pltpu.SemaphoreType.DMA((nw, 3)), pltpu.SemaphoreType.DMA((nw, 3)), pltpu.SemaphoreType.DMA((nw, 5)),
                                    pltpu.SemaphoreType.DMA((nw, 2))],
        compiler_params=pltpu.CompilerParams(vmem_limit_bytes=VMEM_LIMIT),
    )(*shards, *lands)


def _rs_pair_exchange(name, grads):
    nw = len(grads)

    def body(*refs):
        ins, got, stage = refs[:nw], refs[nw:2 * nw], refs[2 * nw:3 * nw]
        send_sems, recv_sems, load_sems = refs[3 * nw:]
        x, y, c, _ = _place()

        def load(w, kk):
            return pltpu.make_async_copy(ins[w].at[kk, 1 - c], stage[w].at[kk % 2], load_sems.at[w, kk])

        def send(w, kk):
            return pltpu.make_async_remote_copy(src_ref=stage[w].at[kk % 2], dst_ref=got[w].at[kk], send_sem=send_sems.at[w, kk],
                                                recv_sem=recv_sems.at[w, kk], device_id=(x, y, 1 - c), device_id_type=MESH)

        for kk in range(2):
            for w in range(nw):
                load(w, kk).start()
        for kk in range(NSH):
            for w in range(nw):
                load(w, kk).wait()
                send(w, kk).start()
            if kk + 2 < NSH:
                for w in range(nw):
                    send(w, kk).wait_send()
                    load(w, kk + 2).start()
        for kk in range(NSH - 2, NSH):
            for w in range(nw):
                send(w, kk).wait_send()
        for kk in range(NSH):
            for w in range(nw):
                send(w, kk).wait_recv()

    return pl.pallas_call(
        body, name=name, out_shape=[_sds((NSH,) + g.shape[2:], bf16) for g in grads], in_specs=[HBM] * nw, out_specs=[HBM] * nw,
        scratch_shapes=[pltpu.VMEM((2,) + g.shape[2:], bf16) for g in grads]
        + [pltpu.SemaphoreType.DMA((nw, NSH)), pltpu.SemaphoreType.DMA((nw, NSH)), pltpu.SemaphoreType.DMA((nw, NSH))],
        compiler_params=pltpu.CompilerParams(vmem_limit_bytes=VMEM_LIMIT),
    )(*grads)


def _rs_pair_gather(name, halves):
    nw = len(halves)

    def body(*refs):
        ins, outs, stage = refs[:nw], refs[nw:2 * nw], refs[2 * nw:3 * nw]
        send_sems, recv_sems, local_sems, stage_sems = refs[3 * nw:]
        x, y, c, _ = _place()
        loads = [pltpu.make_async_copy(ins[w], stage[w], stage_sems.at[w]) for w in range(nw)]
        for cp in loads:
            cp.start()
        local, cps = [], []
        for w in range(nw):
            loads[w].wait()
            local.append(pltpu.make_async_copy(stage[w], outs[w].at[c], local_sems.at[w]))
            cps.append(pltpu.make_async_remote_copy(src_ref=stage[w], dst_ref=outs[w].at[c], send_sem=send_sems.at[w],
                                                    recv_sem=recv_sems.at[w], device_id=(x, y, 1 - c), device_id_type=MESH))
            local[w].start()
            cps[w].start()
        for w in range(nw):
            pltpu.make_async_remote_copy(src_ref=stage[w], dst_ref=outs[w].at[1 - c], send_sem=send_sems.at[w], recv_sem=recv_sems.at[w],
                                         device_id=(x, y, 1 - c), device_id_type=MESH).wait_recv()
        for cp in cps:
            cp.wait_send()
        for cp in local:
            cp.wait()

    return pl.pallas_call(
        body, name=name, out_shape=[_sds((2,) + h.shape, f32) for h in halves], in_specs=[HBM] * nw, out_specs=[HBM] * nw,
        scratch_shapes=[pltpu.VMEM(h.shape, f32) for h in halves]
        + [pltpu.SemaphoreType.DMA((nw,)), pltpu.SemaphoreType.DMA((nw,)), pltpu.SemaphoreType.DMA((nw,)), pltpu.SemaphoreType.DMA((nw,))],
        compiler_params=pltpu.CompilerParams(vmem_limit_bytes=VMEM_LIMIT),
    )(*halves)


def _row_tile(r, c, nbuf):
    budget = 24 * 1024 * 1024 // (2 * nbuf * 4 * c)
    t = 8
    while t * 2 <= budget and r % (t * 2) == 0:
        t *= 2
    return t


def _cast_bf16(name, a, dep=None):
    r, c = a.shape
    tr = _row_tile(r, c, 2)
    dep_specs, dep_ops = _dep_args(dep, 1)

    def body(a_ref, *rest):
        rest[-1][...] = a_ref[...].astype(bf16)

    spec = pl.BlockSpec((tr, c), lambda i: (i, 0))
    return pl.pallas_call(body, name=name, grid=(r // tr,), in_specs=[spec] + dep_specs, out_specs=spec, out_shape=_sds((r, c), bf16),
                          compiler_params=_params(("parallel",)))(a, *dep_ops)


def _w_in_columns(win4):
    tr = 256

    def body(a_ref, o_ref, ob_ref):
        for k in range(NSH):
            o_ref[:, IN_SH * k:IN_SH * (k + 1)] = a_ref[k][:, :IN_SH]
        o_ref[:, IN_COLS:] = jnp.zeros((tr, IN_P - IN_COLS), bf16)
        ob_ref[...] = o_ref[:, IN_A:]

    return pl.pallas_call(
        body, name="w_in_columns", grid=(D // tr,), in_specs=[pl.BlockSpec((NSH, tr, IN_SHP), lambda i: (0, i, 0))],
        out_specs=[pl.BlockSpec((tr, IN_P), lambda i: (i, 0)), pl.BlockSpec((tr, IN_B), lambda i: (i, 0))],
        out_shape=[_sds((D, IN_P), bf16), _sds((D, IN_B), bf16)], compiler_params=_params(("parallel",)))(win4)


def _pair_sum(name, core, grads, got):
    _, _, rh, c = grads.shape
    tr = _row_tile(rh, c, 2)

    def body(c_ref, a_ref, b_ref, o_ref):
        o_ref[...] = (a_ref[...].astype(f32) + b_ref[...].astype(f32)).astype(bf16)

    spec = pl.BlockSpec((None, tr, c), lambda k, i, c_ref: (k, i, 0))
    return pl.pallas_call(
        body, name=name, out_shape=_sds((NSH, rh, c), bf16),
        grid_spec=pltpu.PrefetchScalarGridSpec(
            num_scalar_prefetch=1, grid=(NSH, rh // tr),
            in_specs=[pl.BlockSpec((None, None, tr, c), lambda k, i, c_ref: (k, c_ref[0], i, 0)), spec], out_specs=spec),
        compiler_params=_params(("parallel", "parallel")))(core, grads, got)


def _chip_sum(name, chip, sums, lands):
    _, rh, c = sums.shape
    tr = _row_tile(rh, c, 4)

    def body(k_ref, own_ref, l_ref, o_ref):
        own = own_ref[...].astype(f32)
        acc = None
        for j in range(NSH):
            term = jnp.where(k_ref[0] == j, own, l_ref[j].astype(f32))
            acc = term if acc is None else acc + term
        o_ref[...] = acc

    return pl.pallas_call(
        body, name=name, out_shape=_sds((rh, c), f32),
        grid_spec=pltpu.PrefetchScalarGridSpec(
            num_scalar_prefetch=1, grid=(rh // tr,),
            in_specs=[pl.BlockSpec((None, tr, c), lambda i, k_ref: (k_ref[0], i, 0)), pl.BlockSpec((NSH, tr, c), lambda i, k_ref: (0, i, 0))],
            out_specs=pl.BlockSpec((tr, c), lambda i, k_ref: (i, 0))),
        compiler_params=_params(("parallel",)))(chip, sums, lands)


def _mods_part(cond16, w_ada, b_part):
    n = w_ada.shape[1]
    tn = 512

    def body(c_ref, w_ref, b_ref, o_ref):
        cv = c_ref[...]
        o_ref[...] = _dot(cv * _sigmoid(cv), w_ref[...]) + b_ref[...]

    return pl.pallas_call(
        body, name="mods_part", grid=(n // tn,),
        in_specs=[pl.BlockSpec((16, D), lambda j: (0, 0)), pl.BlockSpec((D, tn), lambda j: (0, j)), pl.BlockSpec((1, tn), lambda j: (0, j))],
        out_specs=pl.BlockSpec((16, tn), lambda j: (0, j)), out_shape=_sds((16, n), f32), compiler_params=_params(("parallel",)),
    )(cond16, w_ada, b_part)


def _grad_w_ada(cond16, dm16):
    n = dm16.shape[1]
    tr = 256

    def body(c_ref, d_ref, o_ref):
        cv = c_ref[...]
        o_ref[...] = _dot(cv * _sigmoid(cv), d_ref[...], ta=True)

    return pl.pallas_call(
        body, name="grad_w_ada", grid=(D // tr,),
        in_specs=[pl.BlockSpec((16, tr), lambda i: (0, i)), pl.BlockSpec((16, n), lambda i: (0, 0))],
        out_specs=pl.BlockSpec((tr, n), lambda i: (i, 0)), out_shape=_sds((D, n), f32), compiler_params=_params(("parallel",)),
    )(cond16, dm16)


def _adamw(name, w, g, m, v):
    r, c = w.shape
    tr = _row_tile(r, c, 7)

    def body(w_ref, g_ref, m_ref, v_ref, d_ref, nm_ref, nv_ref):
        gv = g_ref[...]
        nm = ADAM_B1 * m_ref[...] + (1.0 - ADAM_B1) * gv
        nv = ADAM_B2 * v_ref[...] + (1.0 - ADAM_B2) * (gv * gv)
        nm_ref[...] = nm
        nv_ref[...] = nv
        m_hat = nm / (1.0 - ADAM_B1 ** ADAM_STEP)
        v_hat = nv / (1.0 - ADAM_B2 ** ADAM_STEP)
        d_ref[...] = -ADAM_LR * (m_hat / (jnp.sqrt(v_hat) + ADAM_EPS) + ADAM_WD * w_ref[...])

    spec = pl.BlockSpec((tr, c), lambda i: (i, 0))
    return pl.pallas_call(body, name=name, grid=(r // tr,), in_specs=[spec] * 4, out_specs=[spec] * 3, out_shape=[_sds((r, c), f32)] * 3,
                          compiler_params=_params(("parallel",)))(w, g, m, v)


def _pack(parts, rows):
    flat = []
    for p in parts:
        p = p.reshape(-1)
        flat.append(jnp.pad(p, (0, (-p.shape[0]) % 128)))
    v = jnp.concatenate(flat)
    return jnp.pad(v, (0, rows * 128 - v.shape[0])).reshape(rows, 128)


def _unpack(packed, sizes):
    lead = packed.shape[:-2]
    flat = packed.reshape(lead + (-1,))
    out, off = [], 0
    for n in sizes:
        out.append(flat[..., off:off + n])
        off += n + (-n) % 128
    return out


BIG = ("w_in", "w_out", "w_gate", "w_up", "w_down")
SMALL = ("b_ada", "g_mix", "conv_b", "dt_bias", "a_log", "d_skip", "g_att_out", "g_ssd_out", "g_ffn", "g_final", "rel_bias", "conv_w")
ORDER = ("w_ada", "b_ada", "g_mix", "w_in", "rel_bias", "conv_w", "conv_b", "dt_bias", "a_log", "d_skip", "g_att_out", "g_ssd_out",
         "w_out", "g_ffn", "w_gate", "w_up", "w_down", "g_final")
REL_SH = N_REL // NSH
CONVW_SH = XBC // NSH
ADA_SH = 6 * D // NSH


class _Exchange:
    def __init__(self, core, chip):
        self.core, self.chip = core, chip
        self.gathered = {}
        self.pending = []

    def gather(self, names, shards):
        ssem, rsem, thru, lands, token = _split_start("gather_start_" + "_".join(names), _gather_copies, shards,
                                                      [(NSH,) + s.shape for s in shards])
        self.gathered.update({n: (ssem[i], rsem[i], thru[i], lands[i]) for i, n in enumerate(names)})
        return token

    def _whole(self, names, after):
        ssem, rsem, thru, lands = zip(*[self.gathered[n] for n in names])
        tag = "_".join(names)
        thru, lands = _split_wait("gather_wait_" + tag, _gather_copies, ssem, rsem, thru, lands, after)
        return _gather_forward("gather_forward_" + tag, thru, lands)

    def w_in(self, after):
        (win4,) = self._whole(("w_in",), after)
        return _w_in_columns(win4.reshape(NSH, D, IN_SHP))

    def w_out(self, after):
        (wout4,) = self._whole(("w_out",), after)
        return wout4.reshape(D, D)

    def ffn(self, after):
        wg4, wu4, wd4 = self._whole(("w_gate", "w_up", "w_down"), after)
        return wg4.reshape(NSH, D, FSH), wu4.reshape(NSH, D, FSH), wd4.reshape(NSH, FSH, D)

    def grad(self, names, grads):
        tag = "_".join(names)
        stacked = [g.reshape(NSH, 2, g.shape[1] // 2, g.shape[2]) for g in grads]
        got = _rs_pair_exchange("rs_pair_exchange_" + tag, stacked)
        sums = [_pair_sum("pair_sum_" + n, self.core, o, g) for n, o, g in zip(names, stacked, got)]
        self.pending.append((names, _split_start("rs_start_" + tag, _reduce_copies, sums, [s.shape for s in sums])))
        return self.pending[-1][1][4]

    def finish(self, after):
        grads = {}
        for names, (ssem, rsem, sums, lands, _) in self.pending:
            tag = "_".join(names)
            sums, lands = _split_wait("rs_wait_" + tag, _reduce_copies, ssem, rsem, sums, lands, after)
            halves = [_chip_sum("chip_sum_" + n, self.chip, sm, ld) for n, sm, ld in zip(names, sums, lands)]
            for n, f in zip(names, _rs_pair_gather("rs_pair_gather_" + tag, halves)):
                grads[n] = f.reshape(2 * f.shape[1], f.shape[2])
        return grads


def kernel(x, c, w_ada, b_ada, g_mix, w_in, rel_bias, conv_w, conv_b, dt_bias, a_log, d_skip, g_att_out, g_ssd_out, w_out, g_ffn, w_gate, w_up, w_down, g_final, loss_target, m_w_ada, m_b_ada, m_g_mix, m_w_in, m_rel_bias, m_conv_w, m_conv_b, m_dt_bias, m_a_log, m_d_skip, m_g_att_out, m_g_ssd_out, m_w_out, m_g_ffn, m_w_gate, m_w_up, m_w_down, m_g_final, v_w_ada, v_b_ada, v_g_mix, v_w_in, v_rel_bias, v_conv_w, v_conv_b, v_dt_bias, v_a_log, v_d_skip, v_g_att_out, v_g_ssd_out, v_w_out, v_g_ffn, v_w_gate, v_w_up, v_w_down, v_g_final):
    args = dict(locals())
    w = {n: args[n] for n in ORDER}
    m = {n: args["m_" + n] for n in ORDER}
    v = {n: args["v_" + n] for n in ORDER}
    ix, iy, ic = lax.axis_index("x"), lax.axis_index("y"), lax.axis_index("c")
    chip = 2 * ix + iy
    dev = 2 * chip + ic
    s = x.shape[1]

    g1 = _allgather8("gather_inputs", _pack([c[0], rel_bias[0], conv_w[0]], 40))
    c_all, rel_sh, convw_sh = _unpack(g1, [D, NH * REL_SH, 4 * CONVW_SH])
    rel_full = jnp.concatenate([rel_sh[2 * k].reshape(NH, REL_SH) for k in range(NSH)], axis=1)
    convw_full = jnp.concatenate([convw_sh[2 * k].reshape(4, CONVW_SH) for k in range(NSH)], axis=1)
    cond16 = jnp.pad(c_all, ((0, 8), (0, 0)))
    b_part = lax.dynamic_slice_in_dim(b_ada, chip * ADA_SH, ADA_SH, axis=1)
    mods_part = _mods_part(cond16, w_ada[0], b_part)[:N_DEV]
    g2 = _allgather8("gather_mods", mods_part.reshape(N_DEV * ADA_SH // 128, 128))
    mods_all = jnp.concatenate([g2[2 * k].reshape(N_DEV, ADA_SH) for k in range(NSH)], axis=1)
    mods = lax.dynamic_slice_in_dim(mods_all, dev, 1, axis=0)

    exchange = _Exchange(jnp.reshape(ic, (1,)).astype(jnp.int32), jnp.reshape(chip, (1,)).astype(jnp.int32))
    shard_in = _cast_bf16("cast_w_in", jnp.pad(w_in[0], ((0, 0), (0, IN_SHP - IN_SH))), dep=g2[0, :8]).reshape(2, D // 2, IN_SHP)
    tok = exchange.gather(("w_in",), [shard_in])
    tok = exchange.gather(("w_out", "w_gate", "w_up", "w_down"), [
        _cast_bf16("cast_w_out", w_out[0], dep=tok).reshape(2, D // NSH // 2, D),
        _cast_bf16("cast_w_gate", w_gate[0], dep=tok).reshape(2, D // 2, FSH),
        _cast_bf16("cast_w_up", w_up[0], dep=tok).reshape(2, D // 2, FSH),
        _cast_bf16("cast_w_down", w_down[0], dep=tok).reshape(2, FSH // 2, D)])
    mods = mods + tok[:1, :1]

    loss, grad_x, dmods, small = _local_step(
        x[0], loss_target[0], mods, g_mix, rel_full, convw_full, conv_b, dt_bias, a_log, d_skip, g_att_out, g_ssd_out, g_ffn,
        g_final[None, :], exchange)

    small_names = ("g_mix", "conv_b", "dt_bias", "a_log", "d_skip", "g_att_out", "g_ssd_out", "g_ffn", "g_final", "rel_bias", "conv_w")
    g3 = _allgather8("gather_small_grads", _pack([dmods] + [small[n] for n in small_names], 264))
    sizes = [6 * D] + [int(np.prod(small[n].shape)) for n in small_names]
    dmods_all = _unpack(g3, sizes)[0]
    summed = _unpack(_sum8(g3), sizes)
    grads = {"b_ada": summed[0].reshape(1, 6 * D)}
    for n, val in zip(small_names, summed[1:]):
        grads[n] = val.reshape(small[n].shape)
    grads["rel_bias"] = lax.dynamic_slice_in_dim(grads["rel_bias"], chip * REL_SH, REL_SH, axis=1)
    grads["conv_w"] = lax.dynamic_slice_in_dim(grads["conv_w"], chip * CONVW_SH, CONVW_SH, axis=1)
    grads["g_final"] = grads["g_final"].reshape(D)
    dm16 = jnp.pad(lax.dynamic_slice_in_dim(dmods_all, chip * ADA_SH, ADA_SH, axis=1), ((0, 8), (0, 0)))
    grads["w_ada"] = _grad_w_ada(cond16, dm16)

    delta, new_m, new_v = {}, {}, {}
    delta["w_ada"], new_m["w_ada"], new_v["w_ada"] = _adamw("adamw_w_ada", w_ada[0], grads["w_ada"], m_w_ada[0], v_w_ada[0])
    grads.update(exchange.finish(grad_x))
    grads["w_in"] = grads["w_in"][:, :IN_SH]
    for n in BIG:
        delta[n], new_m[n], new_v[n] = _adamw("adamw_" + n, w[n][0], grads[n], m[n][0], v[n][0])
    sw = _pack([w[n] for n in SMALL], 200)
    sg = _pack([grads[n] for n in SMALL], 200)
    sm = _pack([m[n] for n in SMALL], 200)
    sv = _pack([v[n] for n in SMALL], 200)
    ssz = [int(np.prod(w[n].shape)) for n in SMALL]
    for dst, packed in zip((delta, new_m, new_v), _adamw("adamw_small", sw, sg, sm, sv)):
        for n, val in zip(SMALL, _unpack(packed, ssz)):
            dst[n] = val

    def shaped(d, n):
        return d[n].reshape(w[n].shape)

    total = lax.psum(loss, ("x", "y", "c"))
    return (total, grad_x[None], *[shaped(grads, n) for n in ORDER], *[shaped(delta, n) for n in ORDER],
            *[shaped(new_m, n) for n in ORDER], *[shaped(new_v, n) for n in ORDER])
```

```python
import functools

import numpy as np
import jax
import jax.numpy as jnp
from jax import lax
from jax.experimental import pallas as pl
from jax.experimental.pallas import tpu as pltpu

f32 = jnp.float32
bf16 = jnp.bfloat16
HIGHEST = lax.Precision.HIGHEST
MESH = pl.DeviceIdType.MESH

D = 2048
CHUNK = 64
LEFT = 8
BAND = (LEFT + 1) * CHUNK
BANDP = 640
PADK = LEFT * CHUNK
NH = 16
HD = 64
ATT_W = NH * HD
SSD_W = 1024
NG = 2
NSTATE = 128
GW = SSD_W // NG
XBC = SSD_W + 2 * NG * NSTATE
N_REL = 320
REL_CLIP = 256
FFN = 5632
NSH = 4
FSH = FFN // NSH
IN_COLS = 5648
IN_SH = IN_COLS // NSH
IN_SHP = 1536
IN_A = 3 * ATT_W
IN_B = 2688
IN_P = IN_A + IN_B
EPS = 1e-6
N_DEV = 8

ADAM_LR = 0.001
ADAM_B1 = 0.9
ADAM_B2 = 0.999
ADAM_EPS = 1e-08
ADAM_WD = 0.01
ADAM_STEP = 10

VMEM_LIMIT = 56 * 1024 * 1024


def _params(sem):
    return pltpu.CompilerParams(dimension_semantics=sem, vmem_limit_bytes=VMEM_LIMIT)


def _sds(shape, dtype):
    return jax.ShapeDtypeStruct(shape, dtype)


def _fold8(v):
    r, w = v.shape
    return jnp.sum(v.reshape(r // 8, 8, w), axis=0)


STRIP = 16


def _strips(tm, fn):
    def step(j, carry):
        fn(pl.ds(pl.multiple_of(j * STRIP, STRIP), STRIP))
        return carry
    lax.fori_loop(0, tm // STRIP, step, 0, unroll=4)


def _sigmoid(v):
    return 1.0 / (1.0 + jnp.exp(-v))


def _softplus(v):
    return jnp.maximum(v, 0.0) + jnp.log(1.0 + jnp.exp(-jnp.abs(v)))


def _dot(a, b, ta=False, tb=False):
    dn = (((0 if ta else 1,), (1 if tb else 0,)), ((), ()))
    return lax.dot_general(a.astype(bf16), b.astype(bf16), dn, preferred_element_type=f32)


def _dep_args(dep, ngrid):
    if dep is None:
        return [], []
    return [pl.BlockSpec((8, 128), lambda *_: (0, 0))], [dep]


def _dot01(a, b, ta=False, tb=False, exact="b"):
    dn = (((0 if ta else 1,), (1 if tb else 0,)), ((), ()))
    x = a if exact == "b" else b
    hi = x.astype(bf16)
    r = x - hi.astype(f32)
    mid = r.astype(bf16)
    lo = (r - mid.astype(f32)).astype(bf16)
    if exact == "b":
        m = b.astype(bf16)
        return sum(lax.dot_general(p, m, dn, preferred_element_type=f32) for p in (hi, mid, lo))
    m = a.astype(bf16)
    return sum(lax.dot_general(m, p, dn, preferred_element_type=f32) for p in (hi, mid, lo))


def _matmul(name, a, b, *, grid, a_spec, b_spec, o_spec, o_shape, o_dtype, acc_shape, ta=False, tb=False, dep=None):
    nk = grid[2]
    dep_specs, dep_ops = _dep_args(dep, 3)

    def body(a_ref, b_ref, *rest):
        o_ref, acc_ref = rest[-2:]
        p = _dot(a_ref[...], b_ref[...], ta, tb)
        if nk == 1:
            o_ref[...] = p.astype(o_ref.dtype)
        else:
            k = pl.program_id(2)

            @pl.when(k == 0)
            def _():
                acc_ref[...] = p

            @pl.when(jnp.logical_and(k > 0, k < nk - 1))
            def _():
                acc_ref[...] += p

            @pl.when(k == nk - 1)
            def _():
                o_ref[...] = (acc_ref[...] + p).astype(o_ref.dtype)

    return pl.pallas_call(
        body, name=name, grid=grid, in_specs=[a_spec, b_spec] + dep_specs, out_specs=o_spec,
        out_shape=_sds(o_shape, o_dtype), scratch_shapes=[pltpu.VMEM(acc_shape if nk > 1 else (8, 128), f32)],
        compiler_params=_params(("parallel", "parallel", "arbitrary")),
    )(a, b, *dep_ops)


def _mm_nn_fullk(name, a, b, tm, tn, o_dtype, n=None):
    m, k = a.shape
    n = b.shape[1] if n is None else n
    return _matmul(name, a, b, grid=(m // tm, n // tn, 1),
                   a_spec=pl.BlockSpec((tm, k), lambda i, j, kk: (i, 0)),
                   b_spec=pl.BlockSpec((k, tn), lambda i, j, kk: (0, j)),
                   o_spec=pl.BlockSpec((tm, tn), lambda i, j, kk: (i, j)),
                   o_shape=(m, n), o_dtype=o_dtype, acc_shape=(tm, tn))


def _mm_nt(name, a, b, tm, tn, tk, o_dtype, dep=None):
    m, k = a.shape
    n = b.shape[0]
    return _matmul(name, a, b, grid=(m // tm, n // tn, k // tk), tb=True, dep=dep,
                   a_spec=pl.BlockSpec((tm, tk), lambda i, j, kk: (i, kk)),
                   b_spec=pl.BlockSpec((tn, tk), lambda i, j, kk: (j, kk)),
                   o_spec=pl.BlockSpec((tm, tn), lambda i, j, kk: (i, j)),
                   o_shape=(m, n), o_dtype=o_dtype, acc_shape=(tm, tn))


def _mm_tn(name, a, b, tm, tn, tk, o_dtype):
    k, m = a.shape
    n = b.shape[1]
    return _matmul(name, a, b, grid=(m // tm, n // tn, k // tk), ta=True,
                   a_spec=pl.BlockSpec((tk, tm), lambda i, j, kk: (kk, i)),
                   b_spec=pl.BlockSpec((tk, tn), lambda i, j, kk: (kk, j)),
                   o_spec=pl.BlockSpec((tm, tn), lambda i, j, kk: (i, j)),
                   o_shape=(m, n), o_dtype=o_dtype, acc_shape=(tm, tn))


FSH_PARTS = (slice(0, 640), slice(640, FSH))


def _ffn_up(h2b, wg4, wu4, tm):
    s = h2b.shape[0]

    def body(h_ref, wg_ref, wu_ref, a_ref, s_ref, ud_ref):
        h = h_ref[...]
        for cols in FSH_PARTS:
            g = _dot(h, wg_ref[:, cols])
            u = _dot(h, wu_ref[:, cols])
            sg = _sigmoid(g)
            sil = g * sg
            a_ref[:, cols] = (sil * u).astype(bf16)
            s_ref[:, cols] = sil.astype(bf16)
            ud_ref[:, cols] = (u * (sg * (1.0 + g * (1.0 - sg)))).astype(bf16)

    wspec = pl.BlockSpec((None, D, FSH), lambda k, i: (k, 0, 0))
    ospec = pl.BlockSpec((tm, FSH), lambda k, i: (i, k))
    return pl.pallas_call(
        body, name="ffn_up", grid=(NSH, s // tm),
        in_specs=[pl.BlockSpec((tm, D), lambda k, i: (i, 0)), wspec, wspec],
        out_specs=[ospec, ospec, ospec], out_shape=[_sds((s, FFN), bf16)] * 3,
        compiler_params=_params(("parallel", "parallel")),
    )(h2b, wg4, wu4)


def _ffn_down(act, wd4, tm):
    s = act.shape[0]
    return _matmul("ffn_down", act, wd4, grid=(s // tm, 1, NSH),
                   a_spec=pl.BlockSpec((tm, FSH), lambda i, j, k: (i, k)),
                   b_spec=pl.BlockSpec((None, FSH, D), lambda i, j, k: (k, 0, 0)),
                   o_spec=pl.BlockSpec((tm, D), lambda i, j, k: (i, 0)),
                   o_shape=(s, D), o_dtype=f32, acc_shape=(tm, D))


def _ffn_dact(dffn, wd4, sil, ud, tm, dep=None):
    s = dffn.shape[0]
    dep_specs, dep_ops = _dep_args(dep, 2)

    def body(d_ref, w_ref, s_ref, ud_ref, *rest):
        dg_ref, du_ref = rest[-2:]
        d = d_ref[...]
        for cols in FSH_PARTS:
            dact = _dot(d, w_ref[cols, :], tb=True)
            dg_ref[:, cols] = (dact * ud_ref[:, cols].astype(f32)).astype(bf16)
            du_ref[:, cols] = (dact * s_ref[:, cols].astype(f32)).astype(bf16)

    blk = pl.BlockSpec((tm, FSH), lambda k, i: (i, k))
    return pl.pallas_call(
        body, name="ffn_dact", grid=(NSH, s // tm),
        in_specs=[pl.BlockSpec((tm, D), lambda k, i: (i, 0)), pl.BlockSpec((None, FSH, D), lambda k, i: (k, 0, 0)), blk, blk] + dep_specs,
        out_specs=[blk, blk], out_shape=[_sds((s, FFN), bf16), _sds((s, FFN), bf16)],
        compiler_params=_params(("parallel", "parallel")),
    )(dffn, wd4, sil, ud, *dep_ops)


def _ffn_dh(dgate, dup, wg4, wu4, tm, dep=None):
    s = dgate.shape[0]
    dep_specs, dep_ops = _dep_args(dep, 2)

    def body(dg_ref, du_ref, wg_ref, wu_ref, *rest):
        o_ref, acc_ref = rest[-2:]
        k = pl.program_id(1)
        p = _dot(dg_ref[...], wg_ref[...], tb=True) + _dot(du_ref[...], wu_ref[...], tb=True)

        @pl.when(k == 0)
        def _():
            acc_ref[...] = p

        @pl.when(jnp.logical_and(k > 0, k < NSH - 1))
        def _():
            acc_ref[...] += p

        @pl.when(k == NSH - 1)
        def _():
            o_ref[...] = acc_ref[...] + p

    aspec = pl.BlockSpec((tm, FSH), lambda i, k: (i, k))
    wspec = pl.BlockSpec((None, D, FSH), lambda i, k: (k, 0, 0))
    return pl.pallas_call(
        body, name="ffn_dh", grid=(s // tm, NSH), in_specs=[aspec, aspec, wspec, wspec] + dep_specs,
        out_specs=pl.BlockSpec((tm, D), lambda i, k: (i, 0)), out_shape=_sds((s, D), f32),
        scratch_shapes=[pltpu.VMEM((tm, D), f32)], compiler_params=_params(("parallel", "arbitrary")),
    )(dgate, dup, wg4, wu4, *dep_ops)


def _grad_cols4(name, h, dy, tm, tk):
    s = h.shape[0]
    return _matmul(name, h, dy, grid=(NSH, D // tm, s // tk), ta=True,
                   a_spec=pl.BlockSpec((tk, tm), lambda k, i, kk: (kk, i)),
                   b_spec=pl.BlockSpec((tk, FSH), lambda k, i, kk: (kk, k)),
                   o_spec=pl.BlockSpec((None, tm, FSH), lambda k, i, kk: (k, i, 0)),
                   o_shape=(NSH, D, FSH), o_dtype=bf16, acc_shape=(tm, FSH))


def _grad_wdown4(act, dffn, tn, tk):
    s = act.shape[0]
    return _matmul("grad_w_down", act, dffn, grid=(NSH, D // tn, s // tk), ta=True,
                   a_spec=pl.BlockSpec((tk, FSH), lambda k, j, kk: (kk, k)),
                   b_spec=pl.BlockSpec((tk, tn), lambda k, j, kk: (kk, j)),
                   o_spec=pl.BlockSpec((None, FSH, tn), lambda k, j, kk: (k, 0, j)),
                   o_shape=(NSH, FSH, D), o_dtype=bf16, acc_shape=(FSH, tn))


def _row_spec(w):
    return pl.BlockSpec((1, w), lambda i: (0, 0))


def _tile_spec(tm, w, col=0):
    return pl.BlockSpec((tm, w), lambda i: (i, col))


def _norm_mod(name, x, g, sc, sh, tm):
    s = x.shape[0]

    def body(x_ref, g_ref, sc_ref, sh_ref, o_ref):
        def strip(rows):
            xv = x_ref[rows, :]
            r = lax.rsqrt(jnp.mean(xv * xv, axis=-1, keepdims=True) + EPS)
            o_ref[rows, :] = (xv * r * g_ref[...] * (1.0 + sc_ref[...]) + sh_ref[...]).astype(bf16)

        _strips(tm, strip)

    return pl.pallas_call(
        body, name=name, grid=(s // tm,), in_specs=[_tile_spec(tm, D), _row_spec(D), _row_spec(D), _row_spec(D)],
        out_specs=_tile_spec(tm, D), out_shape=_sds((s, D), bf16), compiler_params=_params(("parallel",)),
    )(x, g, sc, sh)


def _resid_norm_mod(x, gt, mix, g, sc, sh, tm):
    s = x.shape[0]

    def body(x_ref, gt_ref, m_ref, g_ref, sc_ref, sh_ref, x2_ref, h_ref):
        def strip(rows):
            xv = x_ref[rows, :] + gt_ref[...] * m_ref[rows, :]
            x2_ref[rows, :] = xv
            r = lax.rsqrt(jnp.mean(xv * xv, axis=-1, keepdims=True) + EPS)
            h_ref[rows, :] = (xv * r * g_ref[...] * (1.0 + sc_ref[...]) + sh_ref[...]).astype(bf16)

        _strips(tm, strip)

    return pl.pallas_call(
        body, name="resid_norm_mod", grid=(s // tm,),
        in_specs=[_tile_spec(tm, D), _row_spec(D), _tile_spec(tm, D), _row_spec(D), _row_spec(D), _row_spec(D)],
        out_specs=[_tile_spec(tm, D), _tile_spec(tm, D)], out_shape=[_sds((s, D), f32), _sds((s, D), bf16)],
        compiler_params=_params(("parallel",)),
    )(x, gt, mix, g, sc, sh)


def _final_fwd_bwd(x2, ffn, gt2, g, tgt, tm):
    s = x2.shape[0]
    n = s // tm

    def body(x_ref, f_ref, gt_ref, g_ref, t_ref, dx_ref, df_ref, loss_ref, dg_ref, dgt_ref, a_loss, a_dg, a_dgt):
        i = pl.program_id(0)

        @pl.when(i == 0)
        def _():
            a_loss[...] = jnp.zeros_like(a_loss)
            a_dg[...] = jnp.zeros_like(a_dg)
            a_dgt[...] = jnp.zeros_like(a_dgt)

        def strip(rows):
            fv = f_ref[rows, :]
            gt = gt_ref[...]
            gv = g_ref[...]
            xv = x_ref[rows, :] + gt * fv
            r = lax.rsqrt(jnp.mean(xv * xv, axis=-1, keepdims=True) + EPS)
            xh = xv * r
            e = xh * gv - t_ref[rows, :]
            a_loss[...] += _fold8(e * e)
            dy = e * (1.0 / D)
            a_dg[...] += _fold8(dy * xh)
            t = dy * gv
            dx = r * (t - xh * jnp.mean(t * xh, axis=-1, keepdims=True))
            dx_ref[rows, :] = dx
            a_dgt[...] += _fold8(dx * fv)
            df_ref[rows, :] = (dx * gt).astype(bf16)

        _strips(tm, strip)

        @pl.when(i == n - 1)
        def _():
            tot = jnp.sum(jnp.sum(a_loss[...], axis=0, keepdims=True), axis=1, keepdims=True) * (0.5 / D)
            loss_ref[...] = jnp.broadcast_to(tot, (1, 128))
            dg_ref[...] = jnp.sum(a_dg[...], axis=0, keepdims=True)
            dgt_ref[...] = jnp.sum(a_dgt[...], axis=0, keepdims=True)

    return pl.pallas_call(
        body, name="final_fwd_bwd", grid=(n,),
        in_specs=[_tile_spec(tm, D), _tile_spec(tm, D), _row_spec(D), _row_spec(D), _tile_spec(tm, D)],
        out_specs=[_tile_spec(tm, D), _tile_spec(tm, D), _row_spec(128), _row_spec(D), _row_spec(D)],
        out_shape=[_sds((s, D), f32), _sds((s, D), bf16), _sds((1, 128), f32), _sds((1, D), f32), _sds((1, D), f32)],
        scratch_shapes=[pltpu.VMEM((8, D), f32)] * 3, compiler_params=_params(("arbitrary",)),
    )(x2, ffn, gt2, g, tgt)


def _norm_mod_bwd(name, dh, xin, g, sc, dres, tm, mix=None, gt=None):
    s = dh.shape[0]
    n = s // tm
    with_mix = mix is not None

    def body(*refs):
        if with_mix:
            dh_ref, x_ref, g_ref, sc_ref, dr_ref, m_ref, gt_ref, dx_ref, dm_ref, dsc_ref, dsh_ref, dg_ref, dgt_ref, a_sc, a_sh, a_g, a_gt = refs
        else:
            dh_ref, x_ref, g_ref, sc_ref, dr_ref, dx_ref, dsc_ref, dsh_ref, dg_ref, a_sc, a_sh, a_g = refs
        i = pl.program_id(0)

        @pl.when(i == 0)
        def _():
            a_sc[...] = jnp.zeros_like(a_sc)
            a_sh[...] = jnp.zeros_like(a_sh)
            a_g[...] = jnp.zeros_like(a_g)
            if with_mix:
                a_gt[...] = jnp.zeros_like(a_gt)

        def strip(rows):
            dh = dh_ref[rows, :]
            xv = x_ref[rows, :]
            gv = g_ref[...]
            r = lax.rsqrt(jnp.mean(xv * xv, axis=-1, keepdims=True) + EPS)
            xh = xv * r
            a_sc[...] += _fold8(dh * xh * gv)
            a_sh[...] += _fold8(dh)
            dn = dh * (1.0 + sc_ref[...])
            a_g[...] += _fold8(dn * xh)
            t = dn * gv
            dx = dr_ref[rows, :] + r * (t - xh * jnp.mean(t * xh, axis=-1, keepdims=True))
            dx_ref[rows, :] = dx
            if with_mix:
                a_gt[...] += _fold8(dx * m_ref[rows, :])
                dm_ref[rows, :] = (dx * gt_ref[...]).astype(bf16)

        _strips(tm, strip)

        @pl.when(i == n - 1)
        def _():
            dsc_ref[...] = jnp.sum(a_sc[...], axis=0, keepdims=True)
            dsh_ref[...] = jnp.sum(a_sh[...], axis=0, keepdims=True)
            dg_ref[...] = jnp.sum(a_g[...], axis=0, keepdims=True)
            if with_mix:
                dgt_ref[...] = jnp.sum(a_gt[...], axis=0, keepdims=True)

    tile, row = _tile_spec(tm, D), _row_spec(D)
    if with_mix:
        ins, args = [tile, tile, row, row, tile, tile, row], (dh, xin, g, sc, dres, mix, gt)
        outs = [tile, tile, row, row, row, row]
        shapes = [_sds((s, D), f32), _sds((s, D), bf16)] + [_sds((1, D), f32)] * 4
        nacc = 4
    else:
        ins, args = [tile, tile, row, row, tile], (dh, xin, g, sc, dres)
        outs = [tile, row, row, row]
        shapes = [_sds((s, D), f32)] + [_sds((1, D), f32)] * 3
        nacc = 3
    return pl.pallas_call(
        body, name=name, grid=(n,), in_specs=ins, out_specs=outs, out_shape=shapes,
        scratch_shapes=[pltpu.VMEM((8, D), f32)] * nacc, compiler_params=_params(("arbitrary",)),
    )(*args)


def _mix_pre(att, y, proj2, g_att, g_ssd, tm):
    s = att.shape[0]

    def body(a_ref, y_ref, z_ref, ga_ref, gs_ref, o_ref):
        def strip(rows):
            a = a_ref[rows, :]
            ra = lax.rsqrt(jnp.mean(a * a, axis=-1, keepdims=True) + EPS)
            o_ref[rows, 0:ATT_W] = (a * ra * ga_ref[...]).astype(bf16)
            z = z_ref[rows, :]
            u = y_ref[rows, :] * (z * _sigmoid(z))
            ru = lax.rsqrt(jnp.mean(u * u, axis=-1, keepdims=True) + EPS)
            o_ref[rows, ATT_W:] = (u * ru * gs_ref[...]).astype(bf16)

        _strips(tm, strip)

    t = _tile_spec(tm, ATT_W)
    return pl.pallas_call(
        body, name="mix_pre", grid=(s // tm,), in_specs=[t, t, t, _row_spec(ATT_W), _row_spec(SSD_W)],
        out_specs=_tile_spec(tm, D), out_shape=_sds((s, D), bf16), compiler_params=_params(("parallel",)),
    )(att, y, proj2, g_att, g_ssd)


def _mix_pre_bwd(dmc, att, y, proj2, g_att, g_ssd, tm):
    s = att.shape[0]
    n = s // tm

    def body(da_ref, ds_ref, a_ref, y_ref, z_ref, ga_ref, gs_ref, datt_ref, dy_ref, dz_ref, dga_ref, dgs_ref, acc_a, acc_s):
        i = pl.program_id(0)

        @pl.when(i == 0)
        def _():
            acc_a[...] = jnp.zeros_like(acc_a)
            acc_s[...] = jnp.zeros_like(acc_s)

        def strip(rows):
            a = a_ref[rows, :]
            ra = lax.rsqrt(jnp.mean(a * a, axis=-1, keepdims=True) + EPS)
            ah = a * ra
            dan = da_ref[rows, :]
            acc_a[...] += _fold8(dan * ah)
            t = dan * ga_ref[...]
            datt_ref[rows, :] = (ra * (t - ah * jnp.mean(t * ah, axis=-1, keepdims=True))).astype(bf16)
            z = z_ref[rows, :]
            yv = y_ref[rows, :]
            sz = _sigmoid(z)
            sil = z * sz
            u = yv * sil
            ru = lax.rsqrt(jnp.mean(u * u, axis=-1, keepdims=True) + EPS)
            uh = u * ru
            dsn = ds_ref[rows, :]
            acc_s[...] += _fold8(dsn * uh)
            t2 = dsn * gs_ref[...]
            du = ru * (t2 - uh * jnp.mean(t2 * uh, axis=-1, keepdims=True))
            dy_ref[rows, :] = du * sil
            dz_ref[rows, :] = (du * yv * (sz * (1.0 + z * (1.0 - sz)))).astype(bf16)

        _strips(tm, strip)

        @pl.when(i == n - 1)
        def _():
            dga_ref[...] = jnp.sum(acc_a[...], axis=0, keepdims=True)
            dgs_ref[...] = jnp.sum(acc_s[...], axis=0, keepdims=True)

    t = _tile_spec(tm, ATT_W)
    row = _row_spec(ATT_W)
    return pl.pallas_call(
        body, name="mix_pre_bwd", grid=(n,),
        in_specs=[_tile_spec(tm, ATT_W, 0), _tile_spec(tm, ATT_W, 1), t, t, t, row, row],
        out_specs=[t, t, t, row, row],
        out_shape=[_sds((s, ATT_W), bf16), _sds((s, SSD_W), f32), _sds((s, SSD_W), bf16), _sds((1, ATT_W), f32), _sds((1, SSD_W), f32)],
        scratch_shapes=[pltpu.VMEM((8, ATT_W), f32)] * 2, compiler_params=_params(("arbitrary",)),
    )(dmc, dmc, att, y, proj2, g_att, g_ssd)


ATT_GROUP = 8
ATT_GROUP_FWD = 8


def _pair_rows(qc):
    two = jnp.concatenate([qc, qc], axis=0)
    r = lax.broadcasted_iota(jnp.int32, (2 * CHUNK, 128), 0)
    l = lax.broadcasted_iota(jnp.int32, (2 * CHUNK, 128), 1)
    return jnp.where((r < CHUNK) == (l < HD), two, jnp.zeros_like(two))


def _scaled(q):
    return q * jnp.asarray(HD ** -0.5, q.dtype)


def _pair_scores(wt, kb, bias, r0, masked):
    sc = lax.dot_general(wt, kb, (((1,), (1,)), ((), ())), preferred_element_type=f32) + bias
    if not masked:
        return sc
    kidx = lax.broadcasted_iota(jnp.int32, sc.shape, 1)
    return jnp.where(r0 + kidx >= PADK, sc, -jnp.inf)


def _softmax(sc, axis):
    e = jnp.exp(sc - jnp.max(sc, axis=axis, keepdims=True))
    return e * (1.0 / jnp.sum(e, axis=axis, keepdims=True))


def _chunk_loops(nc, group, per_trip):
    n_masked = min(-(-LEFT // per_trip), nc // per_trip)

    def run(masked):
        def step(g, carry):
            group(g, masked)
            return carry
        return step

    lax.fori_loop(0, n_masked, run(True), 0)
    lax.fori_loop(n_masked, nc // per_trip, run(False), 0)


def _pair_diag(r):
    lane = lax.broadcasted_iota(jnp.int32, (CHUNK, 128), 1)
    return jnp.where(lane < HD, r[0:CHUNK], r[CHUNK:])


def _pad_keys(k_ref, kp, s):
    kp[0:PADK, :] = jnp.zeros((PADK, 128), bf16)
    kp[PADK:PADK + s, :] = k_ref[...]
    kp[PADK + s:, :] = jnp.zeros((CHUNK, 128), bf16)


def _attn_fwd(qkv, bias2):
    s = qkv.shape[0]
    nc = s // CHUNK
    npair = NH // 2

    def body(q_ref, k_ref, v_ref, b_ref, o_ref, kp, vp):
        _pad_keys(k_ref, kp, s)
        _pad_keys(v_ref, vp, s)

        def group(g, masked):
            r0s = [pl.multiple_of((g * ATT_GROUP_FWD + u) * CHUNK, CHUNK) for u in range(ATT_GROUP_FWD)]
            scs = [_pair_scores(_pair_rows(_scaled(q_ref[pl.ds(r0, CHUNK), :])), kp[pl.ds(r0, BANDP), :], b_ref[...], r0, masked)
                   for r0 in r0s]
            ps = [_softmax(sc, -1).astype(bf16) for sc in scs]
            for r0, p in zip(r0s, ps):
                o_ref[pl.ds(r0, CHUNK), :] = _pair_diag(jnp.dot(p, vp[pl.ds(r0, BANDP), :], preferred_element_type=f32))

        _chunk_loops(nc, group, ATT_GROUP_FWD)

    return pl.pallas_call(
        body, name="attn_fwd", grid=(npair,),
        in_specs=[pl.BlockSpec((s, 128), lambda p: (0, p)), pl.BlockSpec((s, 128), lambda p: (0, npair + p)),
                  pl.BlockSpec((s, 128), lambda p: (0, 2 * npair + p)), pl.BlockSpec((None, 2 * CHUNK, BANDP), lambda p: (p, 0, 0))],
        out_specs=pl.BlockSpec((s, 128), lambda p: (0, p)), out_shape=_sds((s, ATT_W), f32),
        scratch_shapes=[pltpu.VMEM((PADK + s + CHUNK, 128), bf16)] * 2, compiler_params=_params(("parallel",)),
    )(qkv, qkv, qkv, bias2)


def _attn_bwd(qkv, datt, bias2):
    s = qkv.shape[0]
    nc = s // CHUNK
    npair = NH // 2
    rows = PADK + s + CHUNK
    nt = (((1,), (1,)), ((), ()))

    def body(q_ref, k_ref, v_ref, do_ref, b_ref, dq_ref, dk_ref, dv_ref, g_ref, kp, vp, dkp, dvp):
        _pad_keys(k_ref, kp, s)
        _pad_keys(v_ref, vp, s)
        dkp[...] = jnp.zeros_like(dkp)
        dvp[...] = jnp.zeros_like(dvp)
        g_ref[...] = jnp.zeros_like(g_ref)

        def group(g, masked):
            r0s = [pl.multiple_of((g * ATT_GROUP + u) * CHUNK, CHUNK) for u in range(ATT_GROUP)]
            wts = [_pair_rows(_scaled(q_ref[pl.ds(r0, CHUNK), :])) for r0 in r0s]
            dos = [_pair_rows(do_ref[pl.ds(r0, CHUNK), :]) for r0 in r0s]
            scs = [_pair_scores(wt, kp[pl.ds(r0, BANDP), :], b_ref[...], r0, masked) for wt, r0 in zip(wts, r0s)]
            dps = [lax.dot_general(do, vp[pl.ds(r0, BANDP), :], nt, preferred_element_type=f32) for do, r0 in zip(dos, r0s)]
            tn_ = (((0,), (0,)), ((), ()))
            for r0, wt, do, sc, dp in zip(r0s, wts, dos, scs, dps):
                p = _softmax(sc, -1)
                ds = p * (dp - jnp.sum(p * dp, axis=-1, keepdims=True))
                g_ref[...] += ds
                dsb = ds.astype(bf16)
                dq = jnp.dot(dsb, kp[pl.ds(r0, BANDP), :], preferred_element_type=f32)
                dq_ref[pl.ds(r0, CHUNK), :] = (_pair_diag(dq) * (HD ** -0.5)).astype(bf16)
                dkp[pl.ds(r0, BANDP), :] += lax.dot_general(dsb, wt, tn_, preferred_element_type=f32)
                dvp[pl.ds(r0, BANDP), :] += lax.dot_general(p.astype(bf16), do, tn_, preferred_element_type=f32)

        _chunk_loops(nc, group, ATT_GROUP)
        dk_ref[...] = dkp[PADK:PADK + s, :].astype(bf16)
        dv_ref[...] = dvp[PADK:PADK + s, :].astype(bf16)

    col = lambda off: pl.BlockSpec((s, 128), lambda p: (0, off + p))
    return pl.pallas_call(
        body, name="attn_bwd", grid=(npair,),
        in_specs=[col(0), col(npair), col(2 * npair), col(0), pl.BlockSpec((None, 2 * CHUNK, BANDP), lambda p: (p, 0, 0))],
        out_specs=[col(0), col(0), col(0), pl.BlockSpec((None, 2 * CHUNK, BANDP), lambda p: (p, 0, 0))],
        out_shape=[_sds((s, ATT_W), bf16)] * 3 + [_sds((npair, 2 * CHUNK, BANDP), f32)],
        scratch_shapes=[pltpu.VMEM((rows, 128), bf16)] * 2 + [pltpu.VMEM((rows, 128), f32)] * 2,
        compiler_params=_params(("parallel",)),
    )(qkv, qkv, qkv, datt, bias2)


def _rel_tables():
    onehot = np.zeros((BANDP, N_REL), np.float32)
    for j in range(BAND + CHUNK - 1):
        o = j - (CHUNK - 1)
        onehot[j, int(np.clip(PADK - o, -(CHUNK - 1), REL_CLIP)) + CHUNK - 1] = 1.0
    return onehot, np.ascontiguousarray(np.eye(CHUNK, dtype=np.float32)[::-1])


def _expand_bias(rel):
    ext = jnp.concatenate([jnp.broadcast_to(rel[:, N_REL - 1:], (NH, N_REL - 1)), rel[:, ::-1],
                           jnp.zeros((NH, BANDP - BAND + 1), f32)], axis=1)
    band = jnp.stack([ext[:, CHUNK - 1 - q:CHUNK - 1 - q + BANDP] for q in range(CHUNK)], axis=1)
    band = jnp.where(np.arange(BANDP) < BAND, band, -jnp.inf)
    return band.reshape(NH // 2, 2 * CHUNK, BANDP)


def _rel_bias_grad(gband):
    def body(g_ref, m_ref, flip_ref, o_ref, d2):
        for h in range(NH):
            rev = jnp.dot(flip_ref[...], g_ref[h], precision=HIGHEST, preferred_element_type=f32)
            rolled = pltpu.roll(rev, 0, 1, stride=1, stride_axis=0)
            d2[h:h + 1, :] = jnp.sum(rolled, axis=0, keepdims=True)
        o_ref[...] = jnp.dot(d2[...], m_ref[...], precision=HIGHEST, preferred_element_type=f32)

    onehot, flip = _rel_tables()
    return pl.pallas_call(
        body, name="rel_bias_grad", out_shape=_sds((NH, N_REL), f32), scratch_shapes=[pltpu.VMEM((NH, BANDP), f32)],
    )(gband, jnp.asarray(onehot), jnp.asarray(flip))


XBC_BLK = 512
XBC_COL0 = SSD_W // XBC_BLK
DT_COL = (SSD_W + XBC) // 128


def _conv_taps(ext, w_ref, b_ref, tm):
    n = ext.shape[0]
    pre = w_ref[3:4, :] * ext + b_ref[...]
    for j in range(3):
        pre = pre + w_ref[j:j + 1, :] * pltpu.roll(ext, 3 - j, 0)
    return pre


def _ssd_conv(proj2, conv_w, conv_b, tm):
    s = proj2.shape[0]
    nb = XBC // XBC_BLK

    def body(x_ref, p_ref, w_ref, b_ref, o_ref):
        i = pl.program_id(1)
        prev = jnp.where(i > 0, p_ref[...], 0.0)
        ext = jnp.concatenate([prev, x_ref[...]], axis=0)
        pre = _conv_taps(ext, w_ref, b_ref, tm)[8:8 + tm]
        o_ref[...] = pre * _sigmoid(pre)

    return pl.pallas_call(
        body, name="ssd_conv", grid=(nb, s // tm),
        in_specs=[pl.BlockSpec((tm, XBC_BLK), lambda j, i: (i, XBC_COL0 + j)),
                  pl.BlockSpec((8, XBC_BLK), lambda j, i: (jnp.maximum(i * (tm // 8) - 1, 0), XBC_COL0 + j)),
                  pl.BlockSpec((4, XBC_BLK), lambda j, i: (0, j)), pl.BlockSpec((1, XBC_BLK), lambda j, i: (0, j))],
        out_specs=pl.BlockSpec((tm, XBC_BLK), lambda j, i: (i, j)), out_shape=_sds((s, XBC), f32),
        compiler_params=_params(("parallel", "parallel")),
    )(proj2, proj2, conv_w, conv_b)


def _ssd_conv_bwd(dxbc, proj2, conv_w, conv_b, tm):
    s = proj2.shape[0]
    nb = XBC // XBC_BLK
    n = s // tm
    last8 = s // 8 - 1

    def body(x_ref, xp_ref, xn_ref, d_ref, dn_ref, w_ref, b_ref, o_ref, dw_ref, db_ref):
        i = pl.program_id(1)

        @pl.when(i == 0)
        def _():
            dw_ref[...] = jnp.zeros_like(dw_ref)
            db_ref[...] = jnp.zeros_like(db_ref)

        prev = jnp.where(i > 0, xp_ref[...], 0.0)
        ext = jnp.concatenate([prev, x_ref[...], xn_ref[...]], axis=0)
        pre = _conv_taps(ext, w_ref, b_ref, tm)
        sg = _sigmoid(pre)
        dnext = jnp.where(i < n - 1, dn_ref[...], 0.0)
        dext = jnp.concatenate([jnp.zeros((8, XBC_BLK), f32), d_ref[...], dnext], axis=0)
        dpre = dext * (sg * (1.0 + pre * (1.0 - sg)))
        rows = tm + 16
        dx = w_ref[3:4, :] * dpre
        for j in range(3):
            dx = dx + w_ref[j:j + 1, :] * pltpu.roll(dpre, rows - (3 - j), 0)
        o_ref[...] = dx[8:8 + tm].astype(bf16)
        dcur = dpre[8:8 + tm]
        db_ref[...] += jnp.sum(dcur, axis=0, keepdims=True)
        dw_ref[3:4, :] += jnp.sum(dcur * ext[8:8 + tm], axis=0, keepdims=True)
        for j in range(3):
            dw_ref[j:j + 1, :] += jnp.sum(dcur * pltpu.roll(ext, 3 - j, 0)[8:8 + tm], axis=0, keepdims=True)

    xcol = lambda j: XBC_COL0 + j
    return pl.pallas_call(
        body, name="ssd_conv_bwd", grid=(nb, n),
        in_specs=[pl.BlockSpec((tm, XBC_BLK), lambda j, i: (i, xcol(j))),
                  pl.BlockSpec((8, XBC_BLK), lambda j, i: (jnp.maximum(i * (tm // 8) - 1, 0), xcol(j))),
                  pl.BlockSpec((8, XBC_BLK), lambda j, i: (jnp.minimum((i + 1) * (tm // 8), last8), xcol(j))),
                  pl.BlockSpec((tm, XBC_BLK), lambda j, i: (i, j)),
                  pl.BlockSpec((8, XBC_BLK), lambda j, i: (jnp.minimum((i + 1) * (tm // 8), last8), j)),
                  pl.BlockSpec((4, XBC_BLK), lambda j, i: (0, j)), pl.BlockSpec((1, XBC_BLK), lambda j, i: (0, j))],
        out_specs=[pl.BlockSpec((tm, XBC_BLK), lambda j, i: (i, j)), pl.BlockSpec((4, XBC_BLK), lambda j, i: (0, j)),
                   pl.BlockSpec((1, XBC_BLK), lambda j, i: (0, j))],
        out_shape=[_sds((s, XBC), bf16), _sds((4, XBC), f32), _sds((1, XBC), f32)],
        compiler_params=_params(("parallel", "arbitrary")),
    )(proj2, proj2, proj2, dxbc, dxbc, conv_w, conv_b)


def _ssd_consts():
    ex = np.zeros((128, SSD_W), np.float32)
    for h in range(NH):
        ex[h, h * HD:(h + 1) * HD] = 1.0
    sel = np.zeros((8, 128), np.float32)
    for h in range(NH):
        sel[h // 2, h] = 1.0
    par = np.zeros((128, 128), np.float32)
    for r in range(128):
        for h in range(NH):
            par[r, h] = 1.0 if (h % 2) == (r // 64) else 0.0
    ones_blk = np.zeros((128, 128), np.float32)
    for r in range(128):
        ones_blk[r, (r // 64) * 64:(r // 64) * 64 + 64] = 1.0
    return ex, np.ascontiguousarray(ex.T), sel, par, ones_blk


def _ssd_common(xbc_ref, dtr_ref, a_ref, dtb_ref, ex_ref, sel_ref, par_ref):
    xs = xbc_ref[:, 0:SSD_W]
    dt = _softplus(dtr_ref[...] + dtb_ref[...])
    adt = dt * a_ref[...]
    r_i = lax.broadcasted_iota(jnp.int32, (CHUNK, CHUNK), 0)
    c_i = lax.broadcasted_iota(jnp.int32, (CHUNK, CHUNK), 1)
    tril = (r_i >= c_i).astype(f32)
    cs = _dot01(tril, adt, exact="a")
    cs2 = jnp.concatenate([cs, cs], axis=0) * par_ref[...]
    cstp = _dot01(sel_ref[...], cs2, tb=True, exact="a")
    ex = ex_ref[...]
    dt_full = _dot01(dt, ex)
    cs_full = _dot01(cs, ex)
    return xs, dt, cs, cstp, dt_full, cs_full


def _pair_mask():
    l_i = lax.broadcasted_iota(jnp.int32, (CHUNK, 128), 0)
    lane = lax.broadcasted_iota(jnp.int32, (CHUNK, 128), 1)
    return l_i >= (lane % CHUNK), lane < HD


def _block_diag(xp, first):
    z = jnp.zeros_like(xp)
    return jnp.concatenate([jnp.where(first, xp, z), jnp.where(first, z, xp)], axis=0)


def _ssd_fwd(xbc, proj2, a_row, dtb_row, dsk_full):
    s = xbc.shape[0]
    nc = s // CHUNK
    ex, ext, sel, par, ones_blk = _ssd_consts()

    def body(xbc_ref, dtr_ref, a_ref, dtb_ref, dsk_ref, ex_ref, sel_ref, par_ref, y_ref, hs_ref, hst):
        @pl.when(pl.program_id(0) == 0)
        def _():
            hst[...] = jnp.zeros_like(hst)

        hs_ref[...] = hst[...]
        xs, dt, cs, cstp, dt_full, cs_full = _ssd_common(xbc_ref, dtr_ref, a_ref, dtb_ref, ex_ref, sel_ref, par_ref)
        cs_last = cs_full[CHUNK - 1:CHUNK, :]
        xdt = xs * dt_full
        causal, first = _pair_mask()
        for g in range(NG):
            gl = slice(g * GW, (g + 1) * GW)
            bg = xbc_ref[:, SSD_W + g * NSTATE:SSD_W + (g + 1) * NSTATE].astype(bf16)
            cg = xbc_ref[:, SSD_W + NG * NSTATE + g * NSTATE:SSD_W + NG * NSTATE + (g + 1) * NSTATE].astype(bf16)
            cb2 = lax.dot_general(cg, jnp.concatenate([bg, bg], axis=0), (((1,), (1,)), ((), ())), preferred_element_type=f32)
            hg = hst[g]
            y0 = jnp.dot(cg, hg.astype(bf16), preferred_element_type=f32)
            yoff = jnp.exp(cs_full[:, gl]) * y0
            for j in range(GW // 128):
                pair = g * (GW // 128) + j
                pl_ = slice(pair * 128, (pair + 1) * 128)
                seg = jnp.exp(jnp.where(causal, cs_full[:, pl_] - cstp[pair:pair + 1, :], -jnp.inf))
                m = (cb2 * seg).astype(bf16)
                yd = jnp.dot(m, _block_diag(xdt[:, pl_].astype(bf16), first), preferred_element_type=f32)
                y_ref[:, pl_] = yd + yoff[:, j * 128:(j + 1) * 128] + xs[:, pl_] * dsk_ref[:, pl_]
            xdec = (xdt[:, gl] * jnp.exp(cs_last[:, gl] - cs_full[:, gl])).astype(bf16)
            st = lax.dot_general(bg, xdec, (((0,), (0,)), ((), ())), preferred_element_type=f32)
            hst[g] = jnp.exp(cs_last[:, gl]) * hg + st

    const = lambda shape: pl.BlockSpec(shape, lambda c: tuple(0 for _ in shape))
    return pl.pallas_call(
        body, name="ssd_fwd", grid=(nc,),
        in_specs=[pl.BlockSpec((CHUNK, XBC), lambda c: (c, 0)), pl.BlockSpec((CHUNK, 128), lambda c: (c, DT_COL)),
                  const((1, 128)), const((1, 128)), const((1, SSD_W)), const((128, SSD_W)), const((8, 128)), const((128, 128))],
        out_specs=[pl.BlockSpec((CHUNK, SSD_W), lambda c: (c, 0)), pl.BlockSpec((None, NG, NSTATE, GW), lambda c: (c, 0, 0, 0))],
        out_shape=[_sds((s, SSD_W), f32), _sds((nc, NG, NSTATE, GW), f32)],
        scratch_shapes=[pltpu.VMEM((NG, NSTATE, GW), f32)], compiler_params=_params(("arbitrary",)),
    )(xbc, proj2, a_row, dtb_row, dsk_full, jnp.asarray(ex), jnp.asarray(sel), jnp.asarray(par))


def _ssd_bwd(xbc, proj2, dy, hsave, a_row, dtb_row, dsk_full):
    s = xbc.shape[0]
    nc = s // CHUNK
    ex, ext, sel, par, ones_blk = _ssd_consts()

    def body(xbc_ref, dtr_ref, dy_ref, hs_ref, a_ref, dtb_ref, dsk_ref, ex_ref, ext_ref, sel_ref, par_ref, ob_ref,
             dxbc_ref, ddtr_ref, dd_ref, da_ref, ddtb_ref, dh, a_dd, a_da, a_dtb, dcs_lane, dcs_b, dxdt):
        step = pl.program_id(0)

        @pl.when(step == 0)
        def _():
            dh[...] = jnp.zeros_like(dh)
            a_dd[...] = jnp.zeros_like(a_dd)
            a_da[...] = jnp.zeros_like(a_da)
            a_dtb[...] = jnp.zeros_like(a_dtb)

        xs, dt, cs, cstp, dt_full, cs_full = _ssd_common(xbc_ref, dtr_ref, a_ref, dtb_ref, ex_ref, sel_ref, par_ref)
        cs_last = cs_full[CHUNK - 1:CHUNK, :]
        xdt = xs * dt_full
        dyv = dy_ref[...]
        a_dd[...] += _fold8(dyv * xs)
        causal, first = _pair_mask()
        ones_l = jnp.ones((CHUNK, 128), f32)
        for g in range(NG):
            gl = slice(g * GW, (g + 1) * GW)
            bcol = slice(SSD_W + g * NSTATE, SSD_W + (g + 1) * NSTATE)
            ccol = slice(SSD_W + NG * NSTATE + g * NSTATE, SSD_W + NG * NSTATE + (g + 1) * NSTATE)
            bg = xbc_ref[:, bcol].astype(bf16)
            cg = xbc_ref[:, ccol].astype(bf16)
            bg2 = jnp.concatenate([bg, bg], axis=0)
            cb2 = lax.dot_general(cg, bg2, (((1,), (1,)), ((), ())), preferred_element_type=f32)
            hg = hs_ref[g]
            hgb = hg.astype(bf16)
            dhg = dh[g]
            dhgb = dhg.astype(bf16)
            eg = jnp.exp(cs_full[:, gl])
            dec = jnp.exp(cs_last[:, gl] - cs_full[:, gl])
            gam = jnp.exp(cs_last[:, gl])
            dyg = dyv[:, gl]
            xdt_g = xdt[:, gl]
            y0 = jnp.dot(cg, hgb, preferred_element_type=f32)
            dy0 = (eg * dyg).astype(bf16)
            dcm = lax.dot_general(dy0, hgb, (((1,), (1,)), ((), ())), preferred_element_type=f32)
            dh_prev = gam * dhg + lax.dot_general(cg, dy0, (((0,), (0,)), ((), ())), preferred_element_type=f32)
            dgam = jnp.sum(dhg * hg, axis=0, keepdims=True) * gam
            dxdec = jnp.dot(bg, dhgb, preferred_element_type=f32)
            dbm = lax.dot_general((xdt_g * dec).astype(bf16), dhgb, (((1,), (1,)), ((), ())), preferred_element_type=f32)
            t = dxdec * xdt_g * dec
            dcs_lane[:, gl] = dyg * eg * y0 - t
            dcs_lane[CHUNK - 1:CHUNK, gl] += jnp.sum(t, axis=0, keepdims=True) + dgam
            dxdt[:, gl] = dxdec * dec
            dcb2 = jnp.zeros((CHUNK, 128), f32)
            for j in range(GW // 128):
                pair = g * (GW // 128) + j
                pl_ = slice(pair * 128, (pair + 1) * 128)
                seg = jnp.exp(jnp.where(causal, cs_full[:, pl_] - cstp[pair:pair + 1, :], -jnp.inf))
                m = cb2 * seg
                mb = m.astype(bf16)
                rhs = _block_diag(xdt[:, pl_].astype(bf16), first)
                dyp = dyv[:, pl_].astype(bf16)
                dm = lax.dot_general(dyp, rhs, (((1,), (1,)), ((), ())), preferred_element_type=f32)
                tt = lax.dot_general(mb, dyp, (((0,), (0,)), ((), ())), preferred_element_type=f32)
                dxdt[:, pl_] += jnp.where(first, tt[0:CHUNK], tt[CHUNK:])
                dcb2 = dcb2 + dm * seg
                w = dm * m
                rsum = _dot01(w, ob_ref[...])
                t2 = _dot01(w, ones_l, ta=True)
                dcs_b[:, pl_] = rsum - jnp.where(first, t2[0:CHUNK], t2[CHUNK:])
            dcb2b = dcb2.astype(bf16)
            dcm = dcm + jnp.dot(dcb2b, bg2, preferred_element_type=f32)
            t3 = lax.dot_general(dcb2b, cg, (((0,), (0,)), ((), ())), preferred_element_type=f32)
            dxbc_ref[:, bcol] = dbm + t3[0:CHUNK] + t3[CHUNK:]
            dxbc_ref[:, ccol] = dcm
            dh[g] = dh_prev
        dcs = _dot01(dcs_lane[...] + dcs_b[...] * (1.0 / HD), ext_ref[...])
        r_i = lax.broadcasted_iota(jnp.int32, (CHUNK, CHUNK), 0)
        c_i = lax.broadcasted_iota(jnp.int32, (CHUNK, CHUNK), 1)
        triu = (r_i <= c_i).astype(f32)
        da_ = _dot01(triu, dcs, exact="a")
        dxdtv = dxdt[...]
        ddt = da_ * a_ref[...] + _dot01(dxdtv * xs, ext_ref[...])
        a_da[...] += _fold8(da_ * dt)
        dxbc_ref[:, 0:SSD_W] = dyv * dsk_ref[...] + dxdtv * dt_full
        ddtr = ddt * _sigmoid(dtr_ref[...] + dtb_ref[...])
        ddtr_ref[...] = ddtr
        a_dtb[...] += _fold8(ddtr)

        @pl.when(step == nc - 1)
        def _():
            dd_ref[...] = jnp.sum(jnp.dot(a_dd[...], ext_ref[...], precision=HIGHEST, preferred_element_type=f32), axis=0, keepdims=True)
            da_ref[...] = jnp.sum(a_da[...], axis=0, keepdims=True)
            ddtb_ref[...] = jnp.sum(a_dtb[...], axis=0, keepdims=True)

    rev = lambda c: nc - 1 - c
    const = lambda shape: pl.BlockSpec(shape, lambda c: tuple(0 for _ in shape))
    return pl.pallas_call(
        body, name="ssd_bwd", grid=(nc,),
        in_specs=[pl.BlockSpec((CHUNK, XBC), lambda c: (rev(c), 0)), pl.BlockSpec((CHUNK, 128), lambda c: (rev(c), DT_COL)),
                  pl.BlockSpec((CHUNK, SSD_W), lambda c: (rev(c), 0)), pl.BlockSpec((None, NG, NSTATE, GW), lambda c: (rev(c), 0, 0, 0)),
                  const((1, 128)), const((1, 128)), const((1, SSD_W)), const((128, SSD_W)), const((SSD_W, 128)),
                  const((8, 128)), const((128, 128)), const((128, 128))],
        out_specs=[pl.BlockSpec((CHUNK, XBC), lambda c: (rev(c), 0)), pl.BlockSpec((CHUNK, 128), lambda c: (rev(c), 0)),
                   const((1, 128)), const((1, 128)), const((1, 128))],
        out_shape=[_sds((s, XBC), f32), _sds((s, 128), f32), _sds((1, 128), f32), _sds((1, 128), f32), _sds((1, 128), f32)],
        scratch_shapes=[pltpu.VMEM((NG, NSTATE, GW), f32), pltpu.VMEM((8, SSD_W), f32), pltpu.VMEM((8, 128), f32), pltpu.VMEM((8, 128), f32),
                        pltpu.VMEM((CHUNK, SSD_W), f32), pltpu.VMEM((CHUNK, SSD_W), f32), pltpu.VMEM((CHUNK, SSD_W), f32)],
        compiler_params=_params(("arbitrary",)),
    )(xbc, proj2, dy, hsave, a_row, dtb_row, dsk_full, jnp.asarray(ex), jnp.asarray(ext), jnp.asarray(sel), jnp.asarray(par),
      jnp.asarray(ones_blk))


def _local_step(x, tgt, mods, g_mix, rel, conv_w, conv_b, dt_bias, a_log, d_skip, g_att, g_ssd, g_ffn, g_final, weights):
    s = x.shape[0]
    tm_e = 256 if s % 256 == 0 else s
    tm_m = 512 if s % 512 == 0 else s
    tm_l = 1024 if s % 1024 == 0 else s
    tk = 2048 if s % 2048 == 0 else s
    sh1, sc1, gt1, sh2, sc2, gt2 = [mods[:, i * D:(i + 1) * D] for i in range(6)]

    h1b = _norm_mod("norm_mod_1", x, g_mix, sc1, sh1, tm_e)
    win, win_b = weights.w_in(h1b)
    qkv = _mm_nn_fullk("proj_qkv", h1b, win, tm_l, 768, bf16, n=IN_A)
    proj2 = _mm_nn_fullk("proj_zxbcdt", h1b, win_b, tm_l, 896, f32)
    bias = _expand_bias(rel)
    att = _attn_fwd(qkv, bias)
    xbc = _ssd_conv(proj2, conv_w, conv_b, tm_e)
    a_row = jnp.pad(-jnp.exp(a_log), ((0, 0), (0, 128 - NH)))
    dtb_row = jnp.pad(dt_bias, ((0, 0), (0, 128 - NH)))
    dsk_full = jnp.repeat(d_skip, HD, axis=1)
    y, hsave = _ssd_fwd(xbc, proj2, a_row, dtb_row, dsk_full)
    mixcat = _mix_pre(att, y, proj2, g_att, g_ssd, tm_e)
    wout = weights.w_out(mixcat)
    mix = _mm_nn_fullk("proj_out", mixcat, wout, tm_l, 1024, f32)
    x2, h2b = _resid_norm_mod(x, gt1, mix, g_ffn, sc2, sh2, tm_e)
    wg4, wu4, wd4 = weights.ffn(h2b)
    act, sil, ud = _ffn_up(h2b, wg4, wu4, tm_m)
    ffn = _ffn_down(act, wd4, tm_l)

    dx3, dffn, loss, dg_final, dgt2 = _final_fwd_bwd(x2, ffn, gt2, g_final, tgt, tm_e)
    tok = weights.grad(("w_down",), [_grad_wdown4(act, dffn, 1024, tk)])
    dgate, dup = _ffn_dact(dffn, wd4, sil, ud, tm_l, dep=tok)
    tok = weights.grad(("w_gate", "w_up"), [_grad_cols4("grad_w_gate", h2b, dgate, 1024, tk), _grad_cols4("grad_w_up", h2b, dup, 1024, tk)])
    dh2 = _ffn_dh(dgate, dup, wg4, wu4, tm_m, dep=tok)
    dx2, dmix, dsc2, dsh2, dg_ffn, dgt1 = _norm_mod_bwd("norm_mod_bwd_2", dh2, x2, g_ffn, sc2, dx3, tm_e, mix=mix, gt=gt1)
    tok = weights.grad(("w_out",), [_mm_tn("grad_w_out", mixcat, dmix, 1024, 1024, tk, bf16).reshape(NSH, D // NSH, D)])
    dmc = _mm_nt("dmixcat", dmix, wout, tm_l, 1024, D, f32, dep=tok)
    datt, dy, dz, dg_att, dg_ssd = _mix_pre_bwd(dmc, att, y, proj2, g_att, g_ssd, tm_e)
    dq, dk, dv, gband = _attn_bwd(qkv, datt, bias)
    drel = _rel_bias_grad(gband.reshape(NH, CHUNK, BANDP))
    dxbc, ddtr, dd_row, da_row, ddtb_row = _ssd_bwd(xbc, proj2, dy, hsave, a_row, dtb_row, dsk_full)
    dxbc_raw, dconv_w, dconv_b = _ssd_conv_bwd(dxbc, proj2, conv_w, conv_b, tm_e)
    dproj = jnp.concatenate([dq, dk, dv, dz, dxbc_raw, ddtr.astype(bf16)], axis=1)
    gwin = _mm_tn("grad_w_in", h1b, dproj, 1024, 1152, tk, bf16)
    gwin4 = jnp.stack([jnp.pad(gwin[:, k * IN_SH:(k + 1) * IN_SH], ((0, 0), (0, IN_SHP - IN_SH))) for k in range(NSH)])
    tok = weights.grad(("w_in",), [gwin4])
    dh1 = _mm_nt("dh1", dproj, win, tm_l, 1024, 1920, f32, dep=tok)
    grad_x, dsc1, dsh1, dg_mix = _norm_mod_bwd("norm_mod_bwd_1", dh1, x, g_mix, sc1, dx2, tm_e)

    dmods = jnp.concatenate([dsh1, dsc1, dgt1, dsh2, dsc2, dgt2], axis=1)
    dd_skip = dd_row[:, :NH]
    da_log = da_row[:, :NH] * a_row[:, :NH]
    small = dict(g_mix=dg_mix, conv_b=dconv_b, dt_bias=ddtb_row[:, :NH], a_log=da_log, d_skip=dd_skip, g_att_out=dg_att,
                 g_ssd_out=dg_ssd, g_ffn=dg_ffn, g_final=dg_final, rel_bias=drel, conv_w=dconv_w)
    return loss[0, 0], grad_x, dmods, small


HBM = pl.BlockSpec(memory_space=pl.ANY)
VMEM = pl.BlockSpec(memory_space=pltpu.VMEM)


def _place():
    x, y, c = lax.axis_index("x"), lax.axis_index("y"), lax.axis_index("c")
    chips = [(1 - x, y), (x, 1 - y), (1 - x, 1 - y)]
    return x, y, c, chips


def _allgather8(name, payload, dep=None):
    r = payload.shape[0]
    deps = [] if dep is None else [dep]

    def body(x_ref, *rest):
        out_ref, send_sems, recv_sems, local_sem = rest[-4:]
        x, y, c, chips = _place()
        me, sibling = (x, y, c), (x, y, 1 - c)

        def slot(px, py, pc):
            return out_ref.at[4 * px + 2 * py + pc]

        def copy(k, block, to, src=None):
            return pltpu.make_async_remote_copy(
                src_ref=slot(*block) if src is None else src, dst_ref=slot(*block),
                send_sem=send_sems.at[k], recv_sem=recv_sems.at[k], device_id=to, device_id_type=MESH)

        mine = pltpu.make_async_copy(x_ref, slot(*me), local_sem)
        mine.start()
        first = [copy(0, me, sibling, src=x_ref)]
        first += [copy(1 + j, me, (*chip, c), src=x_ref) for j, chip in enumerate(chips)]
        for cp in first:
            cp.start()
        passed = [copy(4 + j, (*chip, c), sibling) for j, chip in enumerate(chips)]
        for j, chip in enumerate(chips):
            copy(1 + j, (*chip, c), me).wait_recv()
            passed[j].start()
        copy(0, sibling, me).wait_recv()
        for j, chip in enumerate(chips):
            copy(4 + j, (*chip, 1 - c), me).wait_recv()
        for cp in first + passed:
            cp.wait_send()
        mine.wait()

    return pl.pallas_call(
        body, name=name, out_shape=_sds((N_DEV, r, 128), f32), in_specs=[VMEM] * (1 + len(deps)), out_specs=VMEM,
        scratch_shapes=[pltpu.SemaphoreType.DMA((7,)), pltpu.SemaphoreType.DMA((7,)), pltpu.SemaphoreType.DMA],
    )(payload, *deps)


def _sum8(g):
    r = g.shape[1]

    def body(g_ref, o_ref):
        acc = g_ref[0]
        for i in range(1, N_DEV):
            acc = acc + g_ref[i]
        o_ref[...] = acc

    return pl.pallas_call(body, name="sum8", out_shape=_sds((r, 128), f32))(g)


SEM = pl.BlockSpec(memory_space=pltpu.SEMAPHORE)
EFFECT = pltpu.SideEffectType.DATAFLOW_SIDE_EFFECTING


def _gather_copies(ins, lands, send_sems, recv_sems):
    x, y, c, chips = _place()
    k = 2 * x + y
    starts, recvs = [], []
    for w in range(len(ins)):
        for j, (px, py) in enumerate(chips):
            def mk(dst):
                return pltpu.make_async_remote_copy(src_ref=ins[w].at[c], dst_ref=dst, send_sem=send_sems[w].at[j],
                                                    recv_sem=recv_sems[w].at[j], device_id=(px, py, c), device_id_type=MESH)
            starts.append(mk(lands[w].at[k, c]))
            recvs.append(mk(lands[w].at[2 * px + py, c]))
    return starts, recvs


def _reduce_copies(ins, lands, send_sems, recv_sems):
    x, y, c, chips = _place()
    k = 2 * x + y
    starts, recvs = [], []
    for w in range(len(ins)):
        for j, (px, py) in enumerate(chips):
            def mk(dst):
                return pltpu.make_async_remote_copy(src_ref=ins[w].at[2 * px + py], dst_ref=dst, send_sem=send_sems[w].at[j],
                                                    recv_sem=recv_sems[w].at[j], device_id=(px, py, c), device_id_type=MESH)
            starts.append(mk(lands[w].at[k]))
            recvs.append(mk(lands[w].at[2 * px + py]))
    return starts, recvs


def _split_start(name, copies, srcs, land_shapes):
    nw = len(srcs)

    def body(*refs):
        starts, _ = copies(refs[:nw], refs[nw:2 * nw], refs[2 * nw:3 * nw], refs[3 * nw:4 * nw])
        for cp in starts:
            cp.start()
        refs[6 * nw][...] = jnp.zeros((8, 128), f32)

    sems = [pltpu.SemaphoreType.DMA((3,))] * nw
    bufs = [pltpu.HBM(s.shape, bf16) for s in srcs] + [pltpu.HBM(s, bf16) for s in land_shapes]
    res = pl.pallas_call(
        body, name=name, out_shape=sems + sems + bufs + [_sds((8, 128), f32)],
        in_specs=[HBM] * (2 * nw), out_specs=[SEM] * (2 * nw) + [HBM] * (2 * nw) + [VMEM],
        input_output_aliases={i: 2 * nw + i for i in range(2 * nw)},
        compiler_params=pltpu.CompilerParams(has_side_effects=EFFECT),
    )(*[pltpu.with_memory_space_constraint(s, pltpu.HBM) for s in srcs],
      *[pltpu.with_memory_space_constraint(lax.empty(s, bf16), pltpu.HBM) for s in land_shapes])
    return res[:nw], res[nw:2 * nw], res[2 * nw:3 * nw], res[3 * nw:4 * nw], res[4 * nw]


def _split_wait(name, copies, send_sems, recv_sems, srcs, lands, after):
    nw = len(srcs)

    def body(*refs):
        starts, recvs = copies(refs[:nw], refs[nw:2 * nw], refs[2 * nw:3 * nw], refs[3 * nw:4 * nw])
        for s_, r_ in zip(starts, recvs):
            s_.wait_send()
            r_.wait_recv()

    bufs = [pltpu.HBM(s.shape, bf16) for s in srcs] + [pltpu.HBM(l.shape, bf16) for l in lands]
    res = pl.pallas_call(
        body, name=name, out_shape=bufs, in_specs=[HBM] * (2 * nw) + [SEM] * (2 * nw) + [HBM], out_specs=[HBM] * (2 * nw),
        input_output_aliases={i: i for i in range(2 * nw)},
        compiler_params=pltpu.CompilerParams(has_side_effects=EFFECT),
    )(*srcs, *lands, *send_sems, *recv_sems, after)
    return res[:nw], res[nw:]


def _gather_forward(name, shards, lands):
    nw = len(shards)

    def body(*refs):
        ins, lands_in, outs = refs[:nw], refs[nw:2 * nw], refs[2 * nw:3 * nw]
        st_a, st_b, st_c = refs[3 * nw:4 * nw], refs[4 * nw:5 * nw], refs[5 * nw:6 * nw]
        send_sems, recv_sems, load_sems, store_sems = refs[6 * nw:]
        x, y, c, chips = _place()
        k = 2 * x + y
        sibling = (x, y, 1 - c)
        ld_a = [pltpu.make_async_copy(ins[w].at[c], st_a[w], load_sems.at[w, 0]) for w in range(nw)]
        ld_b = [pltpu.make_async_copy(ins[w].at[1 - c], st_b[w], load_sems.at[w, 1]) for w in range(nw)]
        for cp in ld_a + ld_b:
            cp.start()
        st_own = []
        for w in range(nw):
            ld_a[w].wait()
            st_own.append(pltpu.make_async_copy(st_a[w], outs[w].at[k, c], store_sems.at[w, 0]))
            st_own[-1].start()
        for w in range(nw):
            ld_b[w].wait()
            st_own.append(pltpu.make_async_copy(st_b[w], outs[w].at[k, 1 - c], store_sems.at[w, 1]))
            st_own[-1].start()
        for cp in st_own:
            cp.wait()
        fwds = {}
        for j, (px, py) in enumerate(chips):
            kq = 2 * px + py
            for w in range(nw):
                slot = st_b[w] if j % 2 == 0 else st_c[w]
                if j == 2:
                    fwds[w, 0].wait_send()
                ld = pltpu.make_async_copy(lands_in[w].at[kq, c], slot, load_sems.at[w, 2 + j])
                ld.start()
                ld.wait()
                fwds[w, j] = pltpu.make_async_remote_copy(src_ref=slot, dst_ref=outs[w].at[kq, c], send_sem=send_sems.at[w, j],
                                                          recv_sem=recv_sems.at[w, j], device_id=sibling, device_id_type=MESH)
                fwds[w, j].start()
        for j, (px, py) in enumerate(chips):
            for w in range(nw):
                pltpu.make_async_remote_copy(src_ref=st_c[w], dst_ref=outs[w].at[2 * px + py, 1 - c], send_sem=send_sems.at[w, j],
                                             recv_sem=recv_sems.at[w, j], device_id=sibling, device_id_type=MESH).wait_recv()
        for w in range(nw):
            fwds[w, 1].wait_send()
            fwds[w, 2].wait_send()

    stage = [pltpu.VMEM(s.shape[1:], bf16) for s in shards]
    return pl.pallas_call(
        body, name=name, out_shape=[_sds(l.shape, bf16) for l in lands],
        in_specs=[HBM] * (2 * nw), out_specs=[HBM] * nw, input_output_aliases={nw + w: w for w in range(nw)},
        scratch_shapes=stage * 3 + [pltpu.SemaphoreType.DMA((nw, 3)), pltpu.SemaphoreType.DMA((nw, 3)), pltpu.SemaphoreType.DMA((nw, 5)),
                                    pltpu.SemaphoreType.DMA((nw, 2))],
        compiler_params=pltpu.CompilerParams(vmem_limit_bytes=VMEM_LIMIT),
    )(*shards, *lands)


def _rs_pair_exchange(name, grads):
    nw = len(grads)

    def body(*refs):
        ins, got, stage = refs[:nw], refs[nw:2 * nw], refs[2 * nw:3 * nw]
        send_sems, recv_sems, load_sems = refs[3 * nw:]
        x, y, c, _ = _place()

        def load(w, kk):
            return pltpu.make_async_copy(ins[w].at[kk, 1 - c], stage[w].at[kk % 2], load_sems.at[w, kk])

        def send(w, kk):
            return pltpu.make_async_remote_copy(src_ref=stage[w].at[kk % 2], dst_ref=got[w].at[kk], send_sem=send_sems.at[w, kk],
                                                recv_sem=recv_sems.at[w, kk], device_id=(x, y, 1 - c), device_id_type=MESH)

        for kk in range(2):
            for w in range(nw):
                load(w, kk).start()
        for kk in range(NSH):
            for w in range(nw):
                load(w, kk).wait()
                send(w, kk).start()
            if kk + 2 < NSH:
                for w in range(nw):
                    send(w, kk).wait_send()
                    load(w, kk + 2).start()
        for kk in range(NSH - 2, NSH):
            for w in range(nw):
                send(w, kk).wait_send()
        for kk in range(NSH):
            for w in range(nw):
                send(w, kk).wait_recv()

    return pl.pallas_call(
        body, name=name, out_shape=[_sds((NSH,) + g.shape[2:], bf16) for g in grads], in_specs=[HBM] * nw, out_specs=[HBM] * nw,
        scratch_shapes=[pltpu.VMEM((2,) + g.shape[2:], bf16) for g in grads]
        + [pltpu.SemaphoreType.DMA((nw, NSH)), pltpu.SemaphoreType.DMA((nw, NSH)), pltpu.SemaphoreType.DMA((nw, NSH))],
        compiler_params=pltpu.CompilerParams(vmem_limit_bytes=VMEM_LIMIT),
    )(*grads)


def _rs_pair_gather(name, halves):
    nw = len(halves)

    def body(*refs):
        ins, outs, stage = refs[:nw], refs[nw:2 * nw], refs[2 * nw:3 * nw]
        send_sems, recv_sems, local_sems, stage_sems = refs[3 * nw:]
        x, y, c, _ = _place()
        loads = [pltpu.make_async_copy(ins[w], stage[w], stage_sems.at[w]) for w in range(nw)]
        for cp in loads:
            cp.start()
        local, cps = [], []
        for w in range(nw):
            loads[w].wait()
            local.append(pltpu.make_async_copy(stage[w], outs[w].at[c], local_sems.at[w]))
            cps.append(pltpu.make_async_remote_copy(src_ref=stage[w], dst_ref=outs[w].at[c], send_sem=send_sems.at[w],
                                                    recv_sem=recv_sems.at[w], device_id=(x, y, 1 - c), device_id_type=MESH))
            local[w].start()
            cps[w].start()
        for w in range(nw):
            pltpu.make_async_remote_copy(src_ref=stage[w], dst_ref=outs[w].at[1 - c], send_sem=send_sems.at[w], recv_sem=recv_sems.at[w],
                                         device_id=(x, y, 1 - c), device_id_type=MESH).wait_recv()
        for cp in cps:
            cp.wait_send()
        for cp in local:
            cp.wait()

    return pl.pallas_call(
        body, name=name, out_shape=[_sds((2,) + h.shape, f32) for h in halves], in_specs=[HBM] * nw, out_specs=[HBM] * nw,
        scratch_shapes=[pltpu.VMEM(h.shape, f32) for h in halves]
        + [pltpu.SemaphoreType.DMA((nw,)), pltpu.SemaphoreType.DMA((nw,)), pltpu.SemaphoreType.DMA((nw,)), pltpu.SemaphoreType.DMA((nw,))],
        compiler_params=pltpu.CompilerParams(vmem_limit_bytes=VMEM_LIMIT),
    )(*halves)


def _row_tile(r, c, nbuf):
    budget = 24 * 1024 * 1024 // (2 * nbuf * 4 * c)
    t = 8
    while t * 2 <= budget and r % (t * 2) == 0:
        t *= 2
    return t


def _cast_bf16(name, a, dep=None):
    r, c = a.shape
    tr = _row_tile(r, c, 2)
    dep_specs, dep_ops = _dep_args(dep, 1)

    def body(a_ref, *rest):
        rest[-1][...] = a_ref[...].astype(bf16)

    spec = pl.BlockSpec((tr, c), lambda i: (i, 0))
    return pl.pallas_call(body, name=name, grid=(r // tr,), in_specs=[spec] + dep_specs, out_specs=spec, out_shape=_sds((r, c), bf16),
                          compiler_params=_params(("parallel",)))(a, *dep_ops)


def _w_in_columns(win4):
    tr = 256

    def body(a_ref, o_ref, ob_ref):
        for k in range(NSH):
            o_ref[:, IN_SH * k:IN_SH * (k + 1)] = a_ref[k][:, :IN_SH]
        o_ref[:, IN_COLS:] = jnp.zeros((tr, IN_P - IN_COLS), bf16)
        ob_ref[...] = o_ref[:, IN_A:]

    return pl.pallas_call(
        body, name="w_in_columns", grid=(D // tr,), in_specs=[pl.BlockSpec((NSH, tr, IN_SHP), lambda i: (0, i, 0))],
        out_specs=[pl.BlockSpec((tr, IN_P), lambda i: (i, 0)), pl.BlockSpec((tr, IN_B), lambda i: (i, 0))],
        out_shape=[_sds((D, IN_P), bf16), _sds((D, IN_B), bf16)], compiler_params=_params(("parallel",)))(win4)


def _pair_sum(name, core, grads, got):
    _, _, rh, c = grads.shape
    tr = _row_tile(rh, c, 2)

    def body(c_ref, a_ref, b_ref, o_ref):
        o_ref[...] = (a_ref[...].astype(f32) + b_ref[...].astype(f32)).astype(bf16)

    spec = pl.BlockSpec((None, tr, c), lambda k, i, c_ref: (k, i, 0))
    return pl.pallas_call(
        body, name=name, out_shape=_sds((NSH, rh, c), bf16),
        grid_spec=pltpu.PrefetchScalarGridSpec(
            num_scalar_prefetch=1, grid=(NSH, rh // tr),
            in_specs=[pl.BlockSpec((None, None, tr, c), lambda k, i, c_ref: (k, c_ref[0], i, 0)), spec], out_specs=spec),
        compiler_params=_params(("parallel", "parallel")))(core, grads, got)


def _chip_sum(name, chip, sums, lands):
    _, rh, c = sums.shape
    tr = _row_tile(rh, c, 4)

    def body(k_ref, own_ref, l_ref, o_ref):
        own = own_ref[...].astype(f32)
        acc = None
        for j in range(NSH):
            term = jnp.where(k_ref[0] == j, own, l_ref[j].astype(f32))
            acc = term if acc is None else acc + term
        o_ref[...] = acc

    return pl.pallas_call(
        body, name=name, out_shape=_sds((rh, c), f32),
        grid_spec=pltpu.PrefetchScalarGridSpec(
            num_scalar_prefetch=1, grid=(rh // tr,),
            in_specs=[pl.BlockSpec((None, tr, c), lambda i, k_ref: (k_ref[0], i, 0)), pl.BlockSpec((NSH, tr, c), lambda i, k_ref: (0, i, 0))],
            out_specs=pl.BlockSpec((tr, c), lambda i, k_ref: (i, 0))),
        compiler_params=_params(("parallel",)))(chip, sums, lands)


def _mods_part(cond16, w_ada, b_part):
    n = w_ada.shape[1]
    tn = 512

    def body(c_ref, w_ref, b_ref, o_ref):
        cv = c_ref[...]
        o_ref[...] = _dot(cv * _sigmoid(cv), w_ref[...]) + b_ref[...]

    return pl.pallas_call(
        body, name="mods_part", grid=(n // tn,),
        in_specs=[pl.BlockSpec((16, D), lambda j: (0, 0)), pl.BlockSpec((D, tn), lambda j: (0, j)), pl.BlockSpec((1, tn), lambda j: (0, j))],
        out_specs=pl.BlockSpec((16, tn), lambda j: (0, j)), out_shape=_sds((16, n), f32), compiler_params=_params(("parallel",)),
    )(cond16, w_ada, b_part)


def _grad_w_ada(cond16, dm16):
    n = dm16.shape[1]
    tr = 256

    def body(c_ref, d_ref, o_ref):
        cv = c_ref[...]
        o_ref[...] = _dot(cv * _sigmoid(cv), d_ref[...], ta=True)

    return pl.pallas_call(
        body, name="grad_w_ada", grid=(D // tr,),
        in_specs=[pl.BlockSpec((16, tr), lambda i: (0, i)), pl.BlockSpec((16, n), lambda i: (0, 0))],
        out_specs=pl.BlockSpec((tr, n), lambda i: (i, 0)), out_shape=_sds((D, n), f32), compiler_params=_params(("parallel",)),
    )(cond16, dm16)


def _adamw(name, w, g, m, v):
    r, c = w.shape
    tr = _row_tile(r, c, 7)

    def body(w_ref, g_ref, m_ref, v_ref, d_ref, nm_ref, nv_ref):
        gv = g_ref[...]
        nm = ADAM_B1 * m_ref[...] + (1.0 - ADAM_B1) * gv
        nv = ADAM_B2 * v_ref[...] + (1.0 - ADAM_B2) * (gv * gv)
        nm_ref[...] = nm
        nv_ref[...] = nv
        m_hat = nm / (1.0 - ADAM_B1 ** ADAM_STEP)
        v_hat = nv / (1.0 - ADAM_B2 ** ADAM_STEP)
        d_ref[...] = -ADAM_LR * (m_hat / (jnp.sqrt(v_hat) + ADAM_EPS) + ADAM_WD * w_ref[...])

    spec = pl.BlockSpec((tr, c), lambda i: (i, 0))
    return pl.pallas_call(body, name=name, grid=(r // tr,), in_specs=[spec] * 4, out_specs=[spec] * 3, out_shape=[_sds((r, c), f32)] * 3,
                          compiler_params=_params(("parallel",)))(w, g, m, v)


def _pack(parts, rows):
    flat = []
    for p in parts:
        p = p.reshape(-1)
        flat.append(jnp.pad(p, (0, (-p.shape[0]) % 128)))
    v = jnp.concatenate(flat)
    return jnp.pad(v, (0, rows * 128 - v.shape[0])).reshape(rows, 128)


def _unpack(packed, sizes):
    lead = packed.shape[:-2]
    flat = packed.reshape(lead + (-1,))
    out, off = [], 0
    for n in sizes:
        out.append(flat[..., off:off + n])
        off += n + (-n) % 128
    return out


BIG = ("w_in", "w_out", "w_gate", "w_up", "w_down")
SMALL = ("b_ada", "g_mix", "conv_b", "dt_bias", "a_log", "d_skip", "g_att_out", "g_ssd_out", "g_ffn", "g_final", "rel_bias", "conv_w")
ORDER = ("w_ada", "b_ada", "g_mix", "w_in", "rel_bias", "conv_w", "conv_b", "dt_bias", "a_log", "d_skip", "g_att_out", "g_ssd_out",
         "w_out", "g_ffn", "w_gate", "w_up", "w_down", "g_final")
REL_SH = N_REL // NSH
CONVW_SH = XBC // NSH
ADA_SH = 6 * D // NSH


class _Exchange:
    def __init__(self, core, chip):
        self.core, self.chip = core, chip
        self.gathered = {}
        self.pending = []

    def gather(self, names, shards):
        ssem, rsem, thru, lands, token = _split_start("gather_start_" + "_".join(names), _gather_copies, shards,
                                                      [(NSH,) + s.shape for s in shards])
        self.gathered.update({n: (ssem[i], rsem[i], thru[i], lands[i]) for i, n in enumerate(names)})
        return token

    def _whole(self, names, after):
        ssem, rsem, thru, lands = zip(*[self.gathered[n] for n in names])
        tag = "_".join(names)
        thru, lands = _split_wait("gather_wait_" + tag, _gather_copies, ssem, rsem, thru, lands, after)
        return _gather_forward("gather_forward_" + tag, thru, lands)

    def w_in(self, after):
        (win4,) = self._whole(("w_in",), after)
        return _w_in_columns(win4.reshape(NSH, D, IN_SHP))

    def w_out(self, after):
        (wout4,) = self._whole(("w_out",), after)
        return wout4.reshape(D, D)

    def ffn(self, after):
        wg4, wu4, wd4 = self._whole(("w_gate", "w_up", "w_down"), after)
        return wg4.reshape(NSH, D, FSH), wu4.reshape(NSH, D, FSH), wd4.reshape(NSH, FSH, D)

    def grad(self, names, grads):
        tag = "_".join(names)
        stacked = [g.reshape(NSH, 2, g.shape[1] // 2, g.shape[2]) for g in grads]
        got = _rs_pair_exchange("rs_pair_exchange_" + tag, stacked)
        sums = [_pair_sum("pair_sum_" + n, self.core, o, g) for n, o, g in zip(names, stacked, got)]
        self.pending.append((names, _split_start("rs_start_" + tag, _reduce_copies, sums, [s.shape for s in sums])))
        return self.pending[-1][1][4]

    def finish(self, after):
        grads = {}
        for names, (ssem, rsem, sums, lands, _) in self.pending:
            tag = "_".join(names)
            sums, lands = _split_wait("rs_wait_" + tag, _reduce_copies, ssem, rsem, sums, lands, after)
            halves = [_chip_sum("chip_sum_" + n, self.chip, sm, ld) for n, sm, ld in zip(names, sums, lands)]
            for n, f in zip(names, _rs_pair_gather("rs_pair_gather_" + tag, halves)):
                grads[n] = f.reshape(2 * f.shape[1], f.shape[2])
        return grads


def kernel(x, c, w_ada, b_ada, g_mix, w_in, rel_bias, conv_w, conv_b, dt_bias, a_log, d_skip, g_att_out, g_ssd_out, w_out, g_ffn, w_gate, w_up, w_down, g_final, loss_target, m_w_ada, m_b_ada, m_g_mix, m_w_in, m_rel_bias, m_conv_w, m_conv_b, m_dt_bias, m_a_log, m_d_skip, m_g_att_out, m_g_ssd_out, m_w_out, m_g_ffn, m_w_gate, m_w_up, m_w_down, m_g_final, v_w_ada, v_b_ada, v_g_mix, v_w_in, v_rel_bias, v_conv_w, v_conv_b, v_dt_bias, v_a_log, v_d_skip, v_g_att_out, v_g_ssd_out, v_w_out, v_g_ffn, v_w_gate, v_w_up, v_w_down, v_g_final):
    args = dict(locals())
    w = {n: args[n] for n in ORDER}
    m = {n: args["m_" + n] for n in ORDER}
    v = {n: args["v_" + n] for n in ORDER}
    ix, iy, ic = lax.axis_index("x"), lax.axis_index("y"), lax.axis_index("c")
    chip = 2 * ix + iy
    dev = 2 * chip + ic
    s = x.shape[1]

    g1 = _allgather8("gather_inputs", _pack([c[0], rel_bias[0], conv_w[0]], 40))
    c_all, rel_sh, convw_sh = _unpack(g1, [D, NH * REL_SH, 4 * CONVW_SH])
    rel_full = jnp.concatenate([rel_sh[2 * k].reshape(NH, REL_SH) for k in range(NSH)], axis=1)
    convw_full = jnp.concatenate([convw_sh[2 * k].reshape(4, CONVW_SH) for k in range(NSH)], axis=1)
    cond16 = jnp.pad(c_all, ((0, 8), (0, 0)))
    b_part = lax.dynamic_slice_in_dim(b_ada, chip * ADA_SH, ADA_SH, axis=1)
    mods_part = _mods_part(cond16, w_ada[0], b_part)[:N_DEV]
    g2 = _allgather8("gather_mods", mods_part.reshape(N_DEV * ADA_SH // 128, 128))
    mods_all = jnp.concatenate([g2[2 * k].reshape(N_DEV, ADA_SH) for k in range(NSH)], axis=1)
    mods = lax.dynamic_slice_in_dim(mods_all, dev, 1, axis=0)

    exchange = _Exchange(jnp.reshape(ic, (1,)).astype(jnp.int32), jnp.reshape(chip, (1,)).astype(jnp.int32))
    shard_in = _cast_bf16("cast_w_in", jnp.pad(w_in[0], ((0, 0), (0, IN_SHP - IN_SH))), dep=g2[0, :8]).reshape(2, D // 2, IN_SHP)
    tok = exchange.gather(("w_in",), [shard_in])
    tok = exchange.gather(("w_out", "w_gate", "w_up", "w_down"), [
        _cast_bf16("cast_w_out", w_out[0], dep=tok).reshape(2, D // NSH // 2, D),
        _cast_bf16("cast_w_gate", w_gate[0], dep=tok).reshape(2, D // 2, FSH),
        _cast_bf16("cast_w_up", w_up[0], dep=tok).reshape(2, D // 2, FSH),
        _cast_bf16("cast_w_down", w_down[0], dep=tok).reshape(2, FSH // 2, D)])
    mods = mods + tok[:1, :1]

    loss, grad_x, dmods, small = _local_step(
        x[0], loss_target[0], mods, g_mix, rel_full, convw_full, conv_b, dt_bias, a_log, d_skip, g_att_out, g_ssd_out, g_ffn,
        g_final[None, :], exchange)

    small_names = ("g_mix", "conv_b", "dt_bias", "a_log", "d_skip", "g_att_out", "g_ssd_out", "g_ffn", "g_final", "rel_bias", "conv_w")
    g3 = _allgather8("gather_small_grads", _pack([dmods] + [small[n] for n in small_names], 264))
    sizes = [6 * D] + [int(np.prod(small[n].shape)) for n in small_names]
    dmods_all = _unpack(g3, sizes)[0]
    summed = _unpack(_sum8(g3), sizes)
    grads = {"b_ada": summed[0].reshape(1, 6 * D)}
    for n, val in zip(small_names, summed[1:]):
        grads[n] = val.reshape(small[n].shape)
    grads["rel_bias"] = lax.dynamic_slice_in_dim(grads["rel_bias"], chip * REL_SH, REL_SH, axis=1)
    grads["conv_w"] = lax.dynamic_slice_in_dim(grads["conv_w"], chip * CONVW_SH, CONVW_SH, axis=1)
    grads["g_final"] = grads["g_final"].reshape(D)
    dm16 = jnp.pad(lax.dynamic_slice_in_dim(dmods_all, chip * ADA_SH, ADA_SH, axis=1), ((0, 8), (0, 0)))
    grads["w_ada"] = _grad_w_ada(cond16, dm16)

    delta, new_m, new_v = {}, {}, {}
    delta["w_ada"], new_m["w_ada"], new_v["w_ada"] = _adamw("adamw_w_ada", w_ada[0], grads["w_ada"], m_w_ada[0], v_w_ada[0])
    grads.update(exchange.finish(grad_x))
    grads["w_in"] = grads["w_in"][:, :IN_SH]
    for n in BIG:
        delta[n], new_m[n], new_v[n] = _adamw("adamw_" + n, w[n][0], grads[n], m[n][0], v[n][0])
    sw = _pack([w[n] for n in SMALL], 200)
    sg = _pack([grads[n] for n in SMALL], 200)
    sm = _pack([m[n] for n in SMALL], 200)
    sv = _pack([v[n] for n in SMALL], 200)
    ssz = [int(np.prod(w[n].shape)) for n in SMALL]
    for dst, packed in zip((delta, new_m, new_v), _adamw("adamw_small", sw, sg, sm, sv)):
        for n, val in zip(SMALL, _unpack(packed, ssz)):
            dst[n] = val

    def shaped(d, n):
        return d[n].reshape(w[n].shape)

    total = lax.psum(loss, ("x", "y", "c"))
    return (total, grad_x[None], *[shaped(grads, n) for n in ORDER], *[shaped(delta, n) for n in ORDER],
            *[shaped(new_m, n) for n in ORDER], *[shaped(new_v, n) for n in ORDER])
```

```python
import functools

import numpy as np
import jax
import jax.numpy as jnp
from jax import lax
from jax.experimental import pallas as pl
from jax.experimental.pallas import tpu as pltpu

f32 = jnp.float32
bf16 = jnp.bfloat16
HIGHEST = lax.Precision.HIGHEST
MESH = pl.DeviceIdType.MESH

D = 2048
CHUNK = 64
LEFT = 8
BAND = (LEFT + 1) * CHUNK
BANDP = 640
PADK = LEFT * CHUNK
NH = 16
HD = 64
ATT_W = NH * HD
SSD_W = 1024
NG = 2
NSTATE = 128
GW = SSD_W // NG
XBC = SSD_W + 2 * NG * NSTATE
N_REL = 320
REL_CLIP = 256
FFN = 5632
NSH = 4
FSH = FFN // NSH
IN_COLS = 5648
IN_SH = IN_COLS // NSH
IN_SHP = 1536
IN_A = 3 * ATT_W
IN_B = 2688
IN_P = IN_A + IN_B
EPS = 1e-6
N_DEV = 8

ADAM_LR = 0.001
ADAM_B1 = 0.9
ADAM_B2 = 0.999
ADAM_EPS = 1e-08
ADAM_WD = 0.01
ADAM_STEP = 10

VMEM_LIMIT = 56 * 1024 * 1024


def _params(sem):
    return pltpu.CompilerParams(dimension_semantics=sem, vmem_limit_bytes=VMEM_LIMIT)


def _sds(shape, dtype):
    return jax.ShapeDtypeStruct(shape, dtype)


def _fold8(v):
    r, w = v.shape
    return jnp.sum(v.reshape(r // 8, 8, w), axis=0)


STRIP = 16


def _strips(tm, fn):
    def step(j, carry):
        fn(pl.ds(pl.multiple_of(j * STRIP, STRIP), STRIP))
        return carry
    lax.fori_loop(0, tm // STRIP, step, 0, unroll=4)


def _sigmoid(v):
    return 1.0 / (1.0 + jnp.exp(-v))


def _softplus(v):
    return jnp.maximum(v, 0.0) + jnp.log(1.0 + jnp.exp(-jnp.abs(v)))


def _dot(a, b, ta=False, tb=False):
    dn = (((0 if ta else 1,), (1 if tb else 0,)), ((), ()))
    return lax.dot_general(a.astype(bf16), b.astype(bf16), dn, preferred_element_type=f32)


def _dep_args(dep, ngrid):
    if dep is None:
        return [], []
    return [pl.BlockSpec((8, 128), lambda *_: (0, 0))], [dep]


def _dot01(a, b, ta=False, tb=False, exact="b"):
    dn = (((0 if ta else 1,), (1 if tb else 0,)), ((), ()))
    x = a if exact == "b" else b
    hi = x.astype(bf16)
    r = x - hi.astype(f32)
    mid = r.astype(bf16)
    lo = (r - mid.astype(f32)).astype(bf16)
    if exact == "b":
        m = b.astype(bf16)
        return sum(lax.dot_general(p, m, dn, preferred_element_type=f32) for p in (hi, mid, lo))
    m = a.astype(bf16)
    return sum(lax.dot_general(m, p, dn, preferred_element_type=f32) for p in (hi, mid, lo))


def _matmul(name, a, b, *, grid, a_spec, b_spec, o_spec, o_shape, o_dtype, acc_shape, ta=False, tb=False, dep=None):
    nk = grid[2]
    dep_specs, dep_ops = _dep_args(dep, 3)

    def body(a_ref, b_ref, *rest):
        o_ref, acc_ref = rest[-2:]
        p = _dot(a_ref[...], b_ref[...], ta, tb)
        if nk == 1:
            o_ref[...] = p.astype(o_ref.dtype)
        else:
            k = pl.program_id(2)

            @pl.when(k == 0)
            def _():
                acc_ref[...] = p

            @pl.when(jnp.logical_and(k > 0, k < nk - 1))
            def _():
                acc_ref[...] += p

            @pl.when(k == nk - 1)
            def _():
                o_ref[...] = (acc_ref[...] + p).astype(o_ref.dtype)

    return pl.pallas_call(
        body, name=name, grid=grid, in_specs=[a_spec, b_spec] + dep_specs, out_specs=o_spec,
        out_shape=_sds(o_shape, o_dtype), scratch_shapes=[pltpu.VMEM(acc_shape if nk > 1 else (8, 128), f32)],
        compiler_params=_params(("parallel", "parallel", "arbitrary")),
    )(a, b, *dep_ops)


def _mm_nn_fullk(name, a, b, tm, tn, o_dtype, n=None):
    m, k = a.shape
    n = b.shape[1] if n is None else n
    return _matmul(name, a, b, grid=(m // tm, n // tn, 1),
                   a_spec=pl.BlockSpec((tm, k), lambda i, j, kk: (i, 0)),
                   b_spec=pl.BlockSpec((k, tn), lambda i, j, kk: (0, j)),
                   o_spec=pl.BlockSpec((tm, tn), lambda i, j, kk: (i, j)),
                   o_shape=(m, n), o_dtype=o_dtype, acc_shape=(tm, tn))


def _mm_nt(name, a, b, tm, tn, tk, o_dtype, dep=None):
    m, k = a.shape
    n = b.shape[0]
    return _matmul(name, a, b, grid=(m // tm, n // tn, k // tk), tb=True, dep=dep,
                   a_spec=pl.BlockSpec((tm, tk), lambda i, j, kk: (i, kk)),
                   b_spec=pl.BlockSpec((tn, tk), lambda i, j, kk: (j, kk)),
                   o_spec=pl.BlockSpec((tm, tn), lambda i, j, kk: (i, j)),
                   o_shape=(m, n), o_dtype=o_dtype, acc_shape=(tm, tn))


def _mm_tn(name, a, b, tm, tn, tk, o_dtype):
    k, m = a.shape
    n = b.shape[1]
    return _matmul(name, a, b, grid=(m // tm, n // tn, k // tk), ta=True,
                   a_spec=pl.BlockSpec((tk, tm), lambda i, j, kk: (kk, i)),
                   b_spec=pl.BlockSpec((tk, tn), lambda i, j, kk: (kk, j)),
                   o_spec=pl.BlockSpec((tm, tn), lambda i, j, kk: (i, j)),
                   o_shape=(m, n), o_dtype=o_dtype, acc_shape=(tm, tn))


FSH_PARTS = (slice(0, 640), slice(640, FSH))


def _ffn_up(h2b, wg4, wu4, tm):
    s = h2b.shape[0]

    def body(h_ref, wg_ref, wu_ref, a_ref, s_ref, ud_ref):
        h = h_ref[...]
        for cols in FSH_PARTS:
            g = _dot(h, wg_ref[:, cols])
            u = _dot(h, wu_ref[:, cols])
            sg = _sigmoid(g)
            sil = g * sg
            a_ref[:, cols] = (sil * u).astype(bf16)
            s_ref[:, cols] = sil.astype(bf16)
            ud_ref[:, cols] = (u * (sg * (1.0 + g * (1.0 - sg)))).astype(bf16)

    wspec = pl.BlockSpec((None, D, FSH), lambda k, i: (k, 0, 0))
    ospec = pl.BlockSpec((tm, FSH), lambda k, i: (i, k))
    return pl.pallas_call(
        body, name="ffn_up", grid=(NSH, s // tm),
        in_specs=[pl.BlockSpec((tm, D), lambda k, i: (i, 0)), wspec, wspec],
        out_specs=[ospec, ospec, ospec], out_shape=[_sds((s, FFN), bf16)] * 3,
        compiler_params=_params(("parallel", "parallel")),
    )(h2b, wg4, wu4)


def _ffn_down(act, wd4, tm):
    s = act.shape[0]
    return _matmul("ffn_down", act, wd4, grid=(s // tm, 1, NSH),
                   a_spec=pl.BlockSpec((tm, FSH), lambda i, j, k: (i, k)),
                   b_spec=pl.BlockSpec((None, FSH, D), lambda i, j, k: (k, 0, 0)),
                   o_spec=pl.BlockSpec((tm, D), lambda i, j, k: (i, 0)),
                   o_shape=(s, D), o_dtype=f32, acc_shape=(tm, D))


def _ffn_dact(dffn, wd4, sil, ud, tm, dep=None):
    s = dffn.shape[0]
    dep_specs, dep_ops = _dep_args(dep, 2)

    def body(d_ref, w_ref, s_ref, ud_ref, *rest):
        dg_ref, du_ref = rest[-2:]
        d = d_ref[...]
        for cols in FSH_PARTS:
            dact = _dot(d, w_ref[cols, :], tb=True)
            dg_ref[:, cols] = (dact * ud_ref[:, cols].astype(f32)).astype(bf16)
            du_ref[:, cols] = (dact * s_ref[:, cols].astype(f32)).astype(bf16)

    blk = pl.BlockSpec((tm, FSH), lambda k, i: (i, k))
    return pl.pallas_call(
        body, name="ffn_dact", grid=(NSH, s // tm),
        in_specs=[pl.BlockSpec((tm, D), lambda k, i: (i, 0)), pl.BlockSpec((None, FSH, D), lambda k, i: (k, 0, 0)), blk, blk] + dep_specs,
        out_specs=[blk, blk], out_shape=[_sds((s, FFN), bf16), _sds((s, FFN), bf16)],
        compiler_params=_params(("parallel", "parallel")),
    )(dffn, wd4, sil, ud, *dep_ops)


def _ffn_dh(dgate, dup, wg4, wu4, tm, dep=None):
    s = dgate.shape[0]
    dep_specs, dep_ops = _dep_args(dep, 2)

    def body(dg_ref, du_ref, wg_ref, wu_ref, *rest):
        o_ref, acc_ref = rest[-2:]
        k = pl.program_id(1)
        p = _dot(dg_ref[...], wg_ref[...], tb=True) + _dot(du_ref[...], wu_ref[...], tb=True)

        @pl.when(k == 0)
        def _():
            acc_ref[...] = p

        @pl.when(jnp.logical_and(k > 0, k < NSH - 1))
        def _():
            acc_ref[...] += p

        @pl.when(k == NSH - 1)
        def _():
            o_ref[...] = acc_ref[...] + p

    aspec = pl.BlockSpec((tm, FSH), lambda i, k: (i, k))
    wspec = pl.BlockSpec((None, D, FSH), lambda i, k: (k, 0, 0))
    return pl.pallas_call(
        body, name="ffn_dh", grid=(s // tm, NSH), in_specs=[aspec, aspec, wspec, wspec] + dep_specs,
        out_specs=pl.BlockSpec((tm, D), lambda i, k: (i, 0)), out_shape=_sds((s, D), f32),
        scratch_shapes=[pltpu.VMEM((tm, D), f32)], compiler_params=_params(("parallel", "arbitrary")),
    )(dgate, dup, wg4, wu4, *dep_ops)


def _grad_cols4(name, h, dy, tm, tk):
    s = h.shape[0]
    return _matmul(name, h, dy, grid=(NSH, D // tm, s // tk), ta=True,
                   a_spec=pl.BlockSpec((tk, tm), lambda k, i, kk: (kk, i)),
                   b_spec=pl.BlockSpec((tk, FSH), lambda k, i, kk: (kk, k)),
                   o_spec=pl.BlockSpec((None, tm, FSH), lambda k, i, kk: (k, i, 0)),
                   o_shape=(NSH, D, FSH), o_dtype=bf16, acc_shape=(tm, FSH))


def _grad_wdown4(act, dffn, tn, tk):
    s = act.shape[0]
    return _matmul("grad_w_down", act, dffn, grid=(NSH, D // tn, s // tk), ta=True,
                   a_spec=pl.BlockSpec((tk, FSH), lambda k, j, kk: (kk, k)),
                   b_spec=pl.BlockSpec((tk, tn), lambda k, j, kk: (kk, j)),
                   o_spec=pl.BlockSpec((None, FSH, tn), lambda k, j, kk: (k, 0, j)),
                   o_shape=(NSH, FSH, D), o_dtype=bf16, acc_shape=(FSH, tn))


def _row_spec(w):
    return pl.BlockSpec((1, w), lambda i: (0, 0))


def _tile_spec(tm, w, col=0):
    return pl.BlockSpec((tm, w), lambda i: (i, col))


def _norm_mod(name, x, g, sc, sh, tm):
    s = x.shape[0]

    def body(x_ref, g_ref, sc_ref, sh_ref, o_ref):
        def strip(rows):
            xv = x_ref[rows, :]
            r = lax.rsqrt(jnp.mean(xv * xv, axis=-1, keepdims=True) + EPS)
            o_ref[rows, :] = (xv * r * g_ref[...] * (1.0 + sc_ref[...]) + sh_ref[...]).astype(bf16)

        _strips(tm, strip)

    return pl.pallas_call(
        body, name=name, grid=(s // tm,), in_specs=[_tile_spec(tm, D), _row_spec(D), _row_spec(D), _row_spec(D)],
        out_specs=_tile_spec(tm, D), out_shape=_sds((s, D), bf16), compiler_params=_params(("parallel",)),
    )(x, g, sc, sh)


def _resid_norm_mod(x, gt, mix, g, sc, sh, tm):
    s = x.shape[0]

    def body(x_ref, gt_ref, m_ref, g_ref, sc_ref, sh_ref, x2_ref, h_ref):
        def strip(rows):
            xv = x_ref[rows, :] + gt_ref[...] * m_ref[rows, :]
            x2_ref[rows, :] = xv
            r = lax.rsqrt(jnp.mean(xv * xv, axis=-1, keepdims=True) + EPS)
            h_ref[rows, :] = (xv * r * g_ref[...] * (1.0 + sc_ref[...]) + sh_ref[...]).astype(bf16)

        _strips(tm, strip)

    return pl.pallas_call(
        body, name="resid_norm_mod", grid=(s // tm,),
        in_specs=[_tile_spec(tm, D), _row_spec(D), _tile_spec(tm, D), _row_spec(D), _row_spec(D), _row_spec(D)],
        out_specs=[_tile_spec(tm, D), _tile_spec(tm, D)], out_shape=[_sds((s, D), f32), _sds((s, D), bf16)],
        compiler_params=_params(("parallel",)),
    )(x, gt, mix, g, sc, sh)


def _final_fwd_bwd(x2, ffn, gt2, g, tgt, tm):
    s = x2.shape[0]
    n = s // tm

    def body(x_ref, f_ref, gt_ref, g_ref, t_ref, dx_ref, df_ref, loss_ref, dg_ref, dgt_ref, a_loss, a_dg, a_dgt):
        i = pl.program_id(0)

        @pl.when(i == 0)
        def _():
            a_loss[...] = jnp.zeros_like(a_loss)
            a_dg[...] = jnp.zeros_like(a_dg)
            a_dgt[...] = jnp.zeros_like(a_dgt)

        def strip(rows):
            fv = f_ref[rows, :]
            gt = gt_ref[...]
            gv = g_ref[...]
            xv = x_ref[rows, :] + gt * fv
            r = lax.rsqrt(jnp.mean(xv * xv, axis=-1, keepdims=True) + EPS)
            xh = xv * r
            e = xh * gv - t_ref[rows, :]
            a_loss[...] += _fold8(e * e)
            dy = e * (1.0 / D)
            a_dg[...] += _fold8(dy * xh)
            t = dy * gv
            dx = r * (t - xh * jnp.mean(t * xh, axis=-1, keepdims=True))
            dx_ref[rows, :] = dx
            a_dgt[...] += _fold8(dx * fv)
            df_ref[rows, :] = (dx * gt).astype(bf16)

        _strips(tm, strip)

        @pl.when(i == n - 1)
        def _():
            tot = jnp.sum(jnp.sum(a_loss[...], axis=0, keepdims=True), axis=1, keepdims=True) * (0.5 / D)
            loss_ref[...] = jnp.broadcast_to(tot, (1, 128))
            dg_ref[...] = jnp.sum(a_dg[...], axis=0, keepdims=True)
            dgt_ref[...] = jnp.sum(a_dgt[...], axis=0, keepdims=True)

    return pl.pallas_call(
        body, name="final_fwd_bwd", grid=(n,),
        in_specs=[_tile_spec(tm, D), _tile_spec(tm, D), _row_spec(D), _row_spec(D), _tile_spec(tm, D)],
        out_specs=[_tile_spec(tm, D), _tile_spec(tm, D), _row_spec(128), _row_spec(D), _row_spec(D)],
        out_shape=[_sds((s, D), f32), _sds((s, D), bf16), _sds((1, 128), f32), _sds((1, D), f32), _sds((1, D), f32)],
        scratch_shapes=[pltpu.VMEM((8, D), f32)] * 3, compiler_params=_params(("arbitrary",)),
    )(x2, ffn, gt2, g, tgt)


def _norm_mod_bwd(name, dh, xin, g, sc, dres, tm, mix=None, gt=None):
    s = dh.shape[0]
    n = s // tm
    with_mix = mix is not None

    def body(*refs):
        if with_mix:
            dh_ref, x_ref, g_ref, sc_ref, dr_ref, m_ref, gt_ref, dx_ref, dm_ref, dsc_ref, dsh_ref, dg_ref, dgt_ref, a_sc, a_sh, a_g, a_gt = refs
        else:
            dh_ref, x_ref, g_ref, sc_ref, dr_ref, dx_ref, dsc_ref, dsh_ref, dg_ref, a_sc, a_sh, a_g = refs
        i = pl.program_id(0)

        @pl.when(i == 0)
        def _():
            a_sc[...] = jnp.zeros_like(a_sc)
            a_sh[...] = jnp.zeros_like(a_sh)
            a_g[...] = jnp.zeros_like(a_g)
            if with_mix:
                a_gt[...] = jnp.zeros_like(a_gt)

        def strip(rows):
            dh = dh_ref[rows, :]
            xv = x_ref[rows, :]
            gv = g_ref[...]
            r = lax.rsqrt(jnp.mean(xv * xv, axis=-1, keepdims=True) + EPS)
            xh = xv * r
            a_sc[...] += _fold8(dh * xh * gv)
            a_sh[...] += _fold8(dh)
            dn = dh * (1.0 + sc_ref[...])
            a_g[...] += _fold8(dn * xh)
            t = dn * gv
            dx = dr_ref[rows, :] + r * (t - xh * jnp.mean(t * xh, axis=-1, keepdims=True))
            dx_ref[rows, :] = dx
            if with_mix:
                a_gt[...] += _fold8(dx * m_ref[rows, :])
                dm_ref[rows, :] = (dx * gt_ref[...]).astype(bf16)

        _strips(tm, strip)

        @pl.when(i == n - 1)
        def _():
            dsc_ref[...] = jnp.sum(a_sc[...], axis=0, keepdims=True)
            dsh_ref[...] = jnp.sum(a_sh[...], axis=0, keepdims=True)
            dg_ref[...] = jnp.sum(a_g[...], axis=0, keepdims=True)
            if with_mix:
                dgt_ref[...] = jnp.sum(a_gt[...], axis=0, keepdims=True)

    tile, row = _tile_spec(tm, D), _row_spec(D)
    if with_mix:
        ins, args = [tile, tile, row, row, tile, tile, row], (dh, xin, g, sc, dres, mix, gt)
        outs = [tile, tile, row, row, row, row]
        shapes = [_sds((s, D), f32), _sds((s, D), bf16)] + [_sds((1, D), f32)] * 4
        nacc = 4
    else:
        ins, args = [tile, tile, row, row, tile], (dh, xin, g, sc, dres)
        outs = [tile, row, row, row]
        shapes = [_sds((s, D), f32)] + [_sds((1, D), f32)] * 3
        nacc = 3
    return pl.pallas_call(
        body, name=name, grid=(n,), in_specs=ins, out_specs=outs, out_shape=shapes,
        scratch_shapes=[pltpu.VMEM((8, D), f32)] * nacc, compiler_params=_params(("arbitrary",)),
    )(*args)


def _mix_pre(att, y, proj2, g_att, g_ssd, tm):
    s = att.shape[0]

    def body(a_ref, y_ref, z_ref, ga_ref, gs_ref, o_ref):
        def strip(rows):
            a = a_ref[rows, :]
            ra = lax.rsqrt(jnp.mean(a * a, axis=-1, keepdims=True) + EPS)
            o_ref[rows, 0:ATT_W] = (a * ra * ga_ref[...]).astype(bf16)
            z = z_ref[rows, :]
            u = y_ref[rows, :] * (z * _sigmoid(z))
            ru = lax.rsqrt(jnp.mean(u * u, axis=-1, keepdims=True) + EPS)
            o_ref[rows, ATT_W:] = (u * ru * gs_ref[...]).astype(bf16)

        _strips(tm, strip)

    t = _tile_spec(tm, ATT_W)
    return pl.pallas_call(
        body, name="mix_pre", grid=(s // tm,), in_specs=[t, t, t, _row_spec(ATT_W), _row_spec(SSD_W)],
        out_specs=_tile_spec(tm, D), out_shape=_sds((s, D), bf16), compiler_params=_params(("parallel",)),
    )(att, y, proj2, g_att, g_ssd)


def _mix_pre_bwd(dmc, att, y, proj2, g_att, g_ssd, tm):
    s = att.shape[0]
    n = s // tm

    def body(da_ref, ds_ref, a_ref, y_ref, z_ref, ga_ref, gs_ref, datt_ref, dy_ref, dz_ref, dga_ref, dgs_ref, acc_a, acc_s):
        i = pl.program_id(0)

        @pl.when(i == 0)
        def _():
            acc_a[...] = jnp.zeros_like(acc_a)
            acc_s[...] = jnp.zeros_like(acc_s)

        def strip(rows):
            a = a_ref[rows, :]
            ra = lax.rsqrt(jnp.mean(a * a, axis=-1, keepdims=True) + EPS)
            ah = a * ra
            dan = da_ref[rows, :]
            acc_a[...] += _fold8(dan * ah)
            t = dan * ga_ref[...]
            datt_ref[rows, :] = (ra * (t - ah * jnp.mean(t * ah, axis=-1, keepdims=True))).astype(bf16)
            z = z_ref[rows, :]
            yv = y_ref[rows, :]
            sz = _sigmoid(z)
            sil = z * sz
            u = yv * sil
            ru = lax.rsqrt(jnp.mean(u * u, axis=-1, keepdims=True) + EPS)
            uh = u * ru
            dsn = ds_ref[rows, :]
            acc_s[...] += _fold8(dsn * uh)
            t2 = dsn * gs_ref[...]
            du = ru * (t2 - uh * jnp.mean(t2 * uh, axis=-1, keepdims=True))
            dy_ref[rows, :] = du * sil
            dz_ref[rows, :] = (du * yv * (sz * (1.0 + z * (1.0 - sz)))).astype(bf16)

        _strips(tm, strip)

        @pl.when(i == n - 1)
        def _():
            dga_ref[...] = jnp.sum(acc_a[...], axis=0, keepdims=True)
            dgs_ref[...] = jnp.sum(acc_s[...], axis=0, keepdims=True)

    t = _tile_spec(tm, ATT_W)
    row = _row_spec(ATT_W)
    return pl.pallas_call(
        body, name="mix_pre_bwd", grid=(n,),
        in_specs=[_tile_spec(tm, ATT_W, 0), _tile_spec(tm, ATT_W, 1), t, t, t, row, row],
        out_specs=[t, t, t, row, row],
        out_shape=[_sds((s, ATT_W), bf16), _sds((s, SSD_W), f32), _sds((s, SSD_W), bf16), _sds((1, ATT_W), f32), _sds((1, SSD_W), f32)],
        scratch_shapes=[pltpu.VMEM((8, ATT_W), f32)] * 2, compiler_params=_params(("arbitrary",)),
    )(dmc, dmc, att, y, proj2, g_att, g_ssd)


ATT_GROUP = 8
ATT_GROUP_FWD = 8


def _pair_rows(qc):
    two = jnp.concatenate([qc, qc], axis=0)
    r = lax.broadcasted_iota(jnp.int32, (2 * CHUNK, 128), 0)
    l = lax.broadcasted_iota(jnp.int32, (2 * CHUNK, 128), 1)
    return jnp.where((r < CHUNK) == (l < HD), two, jnp.zeros_like(two))


def _scaled(q):
    return q * jnp.asarray(HD ** -0.5, q.dtype)


def _pair_scores(wt, kb, bias, r0, masked):
    sc = lax.dot_general(wt, kb, (((1,), (1,)), ((), ())), preferred_element_type=f32) + bias
    if not masked:
        return sc
    kidx = lax.broadcasted_iota(jnp.int32, sc.shape, 1)
    return jnp.where(r0 + kidx >= PADK, sc, -jnp.inf)


def _softmax(sc, axis):
    e = jnp.exp(sc - jnp.max(sc, axis=axis, keepdims=True))
    return e * (1.0 / jnp.sum(e, axis=axis, keepdims=True))


def _chunk_loops(nc, group, per_trip):
    n_masked = min(-(-LEFT // per_trip), nc // per_trip)

    def run(masked):
        def step(g, carry):
            group(g, masked)
            return carry
        return step

    lax.fori_loop(0, n_masked, run(True), 0)
    lax.fori_loop(n_masked, nc // per_trip, run(False), 0)


def _pair_diag(r):
    lane = lax.broadcasted_iota(jnp.int32, (CHUNK, 128), 1)
    return jnp.where(lane < HD, r[0:CHUNK], r[CHUNK:])


def _pad_keys(k_ref, kp, s):
    kp[0:PADK, :] = jnp.zeros((PADK, 128), bf16)
    kp[PADK:PADK + s, :] = k_ref[...]
    kp[PADK + s:, :] = jnp.zeros((CHUNK, 128), bf16)


def _attn_fwd(qkv, bias2):
    s = qkv.shape[0]
    nc = s // CHUNK
    npair = NH // 2

    def body(q_ref, k_ref, v_ref, b_ref, o_ref, kp, vp):
        _pad_keys(k_ref, kp, s)
        _pad_keys(v_ref, vp, s)

        def group(g, masked):
            r0s = [pl.multiple_of((g * ATT_GROUP_FWD + u) * CHUNK, CHUNK) for u in range(ATT_GROUP_FWD)]
            scs = [_pair_scores(_pair_rows(_scaled(q_ref[pl.ds(r0, CHUNK), :])), kp[pl.ds(r0, BANDP), :], b_ref[...], r0, masked)
                   for r0 in r0s]
            ps = [_softmax(sc, -1).astype(bf16) for sc in scs]
            for r0, p in zip(r0s, ps):
                o_ref[pl.ds(r0, CHUNK), :] = _pair_diag(jnp.dot(p, vp[pl.ds(r0, BANDP), :], preferred_element_type=f32))

        _chunk_loops(nc, group, ATT_GROUP_FWD)

    return pl.pallas_call(
        body, name="attn_fwd", grid=(npair,),
        in_specs=[pl.BlockSpec((s, 128), lambda p: (0, p)), pl.BlockSpec((s, 128), lambda p: (0, npair + p)),
                  pl.BlockSpec((s, 128), lambda p: (0, 2 * npair + p)), pl.BlockSpec((None, 2 * CHUNK, BANDP), lambda p: (p, 0, 0))],
        out_specs=pl.BlockSpec((s, 128), lambda p: (0, p)), out_shape=_sds((s, ATT_W), f32),
        scratch_shapes=[pltpu.VMEM((PADK + s + CHUNK, 128), bf16)] * 2, compiler_params=_params(("parallel",)),
    )(qkv, qkv, qkv, bias2)


def _attn_bwd(qkv, datt, bias2):
    s = qkv.shape[0]
    nc = s // CHUNK
    npair = NH // 2
    rows = PADK + s + CHUNK
    nt = (((1,), (1,)), ((), ()))

    def body(q_ref, k_ref, v_ref, do_ref, b_ref, dq_ref, dk_ref, dv_ref, g_ref, kp, vp, dkp, dvp):
        _pad_keys(k_ref, kp, s)
        _pad_keys(v_ref, vp, s)
        dkp[...] = jnp.zeros_like(dkp)
        dvp[...] = jnp.zeros_like(dvp)
        g_ref[...] = jnp.zeros_like(g_ref)

        def group(g, masked):
            r0s = [pl.multiple_of((g * ATT_GROUP + u) * CHUNK, CHUNK) for u in range(ATT_GROUP)]
            wts = [_pair_rows(_scaled(q_ref[pl.ds(r0, CHUNK), :])) for r0 in r0s]
            dos = [_pair_rows(do_ref[pl.ds(r0, CHUNK), :]) for r0 in r0s]
            scs = [_pair_scores(wt, kp[pl.ds(r0, BANDP), :], b_ref[...], r0, masked) for wt, r0 in zip(wts, r0s)]
            dps = [lax.dot_general(do, vp[pl.ds(r0, BANDP), :], nt, preferred_element_type=f32) for do, r0 in zip(dos, r0s)]
            tn_ = (((0,), (0,)), ((), ()))
            for r0, wt, do, sc, dp in zip(r0s, wts, dos, scs, dps):
                p = _softmax(sc, -1)
                ds = p * (dp - jnp.sum(p * dp, axis=-1, keepdims=True))
                g_ref[...] += ds
                dsb = ds.astype(bf16)
                dq = jnp.dot(dsb, kp[pl.ds(r0, BANDP), :], preferred_element_type=f32)
                dq_ref[pl.ds(r0, CHUNK), :] = (_pair_diag(dq) * (HD ** -0.5)).astype(bf16)
                dkp[pl.ds(r0, BANDP), :] += lax.dot_general(dsb, wt, tn_, preferred_element_type=f32)
                dvp[pl.ds(r0, BANDP), :] += lax.dot_general(p.astype(bf16), do, tn_, preferred_element_type=f32)

        _chunk_loops(nc, group, ATT_GROUP)
        dk_ref[...] = dkp[PADK:PADK + s, :].astype(bf16)
        dv_ref[...] = dvp[PADK:PADK + s, :].astype(bf16)

    col = lambda off: pl.BlockSpec((s, 128), lambda p: (0, off + p))
    return pl.pallas_call(
        body, name="attn_bwd", grid=(npair,),
        in_specs=[col(0), col(npair), col(2 * npair), col(0), pl.BlockSpec((None, 2 * CHUNK, BANDP), lambda p: (p, 0, 0))],
        out_specs=[col(0), col(0), col(0), pl.BlockSpec((None, 2 * CHUNK, BANDP), lambda p: (p, 0, 0))],
        out_shape=[_sds((s, ATT_W), bf16)] * 3 + [_sds((npair, 2 * CHUNK, BANDP), f32)],
        scratch_shapes=[pltpu.VMEM((rows, 128), bf16)] * 2 + [pltpu.VMEM((rows, 128), f32)] * 2,
        compiler_params=_params(("parallel",)),
    )(qkv, qkv, qkv, datt, bias2)


def _rel_tables():
    onehot = np.zeros((BANDP, N_REL), np.float32)
    for j in range(BAND + CHUNK - 1):
        o = j - (CHUNK - 1)
        onehot[j, int(np.clip(PADK - o, -(CHUNK - 1), REL_CLIP)) + CHUNK - 1] = 1.0
    return onehot, np.ascontiguousarray(np.eye(CHUNK, dtype=np.float32)[::-1])


def _expand_bias(rel):
    ext = jnp.concatenate([jnp.broadcast_to(rel[:, N_REL - 1:], (NH, N_REL - 1)), rel[:, ::-1],
                           jnp.zeros((NH, BANDP - BAND + 1), f32)], axis=1)
    band = jnp.stack([ext[:, CHUNK - 1 - q:CHUNK - 1 - q + BANDP] for q in range(CHUNK)], axis=1)
    band = jnp.where(np.arange(BANDP) < BAND, band, -jnp.inf)
    return band.reshape(NH // 2, 2 * CHUNK, BANDP)


def _rel_bias_grad(gband):
    def body(g_ref, m_ref, flip_ref, o_ref, d2):
        for h in range(NH):
            rev = jnp.dot(flip_ref[...], g_ref[h], precision=HIGHEST, preferred_element_type=f32)
            rolled = pltpu.roll(rev, 0, 1, stride=1, stride_axis=0)
            d2[h:h + 1, :] = jnp.sum(rolled, axis=0, keepdims=True)
        o_ref[...] = jnp.dot(d2[...], m_ref[...], precision=HIGHEST, preferred_element_type=f32)

    onehot, flip = _rel_tables()
    return pl.pallas_call(
        body, name="rel_bias_grad", out_shape=_sds((NH, N_REL), f32), scratch_shapes=[pltpu.VMEM((NH, BANDP), f32)],
    )(gband, jnp.asarray(onehot), jnp.asarray(flip))


XBC_BLK = 512
XBC_COL0 = SSD_W // XBC_BLK
DT_COL = (SSD_W + XBC) // 128


def _conv_taps(ext, w_ref, b_ref, tm):
    n = ext.shape[0]
    pre = w_ref[3:4, :] * ext + b_ref[...]
    for j in range(3):
        pre = pre + w_ref[j:j + 1, :] * pltpu.roll(ext, 3 - j, 0)
    return pre


def _ssd_conv(proj2, conv_w, conv_b, tm):
    s = proj2.shape[0]
    nb = XBC // XBC_BLK

    def body(x_ref, p_ref, w_ref, b_ref, o_ref):
        i = pl.program_id(1)
        prev = jnp.where(i > 0, p_ref[...], 0.0)
        ext = jnp.concatenate([prev, x_ref[...]], axis=0)
        pre = _conv_taps(ext, w_ref, b_ref, tm)[8:8 + tm]
        o_ref[...] = pre * _sigmoid(pre)

    return pl.pallas_call(
        body, name="ssd_conv", grid=(nb, s // tm),
        in_specs=[pl.BlockSpec((tm, XBC_BLK), lambda j, i: (i, XBC_COL0 + j)),
                  pl.BlockSpec((8, XBC_BLK), lambda j, i: (jnp.maximum(i * (tm // 8) - 1, 0), XBC_COL0 + j)),
                  pl.BlockSpec((4, XBC_BLK), lambda j, i: (0, j)), pl.BlockSpec((1, XBC_BLK), lambda j, i: (0, j))],
        out_specs=pl.BlockSpec((tm, XBC_BLK), lambda j, i: (i, j)), out_shape=_sds((s, XBC), f32),
        compiler_params=_params(("parallel", "parallel")),
    )(proj2, proj2, conv_w, conv_b)


def _ssd_conv_bwd(dxbc, proj2, conv_w, conv_b, tm):
    s = proj2.shape[0]
    nb = XBC // XBC_BLK
    n = s // tm
    last8 = s // 8 - 1

    def body(x_ref, xp_ref, xn_ref, d_ref, dn_ref, w_ref, b_ref, o_ref, dw_ref, db_ref):
        i = pl.program_id(1)

        @pl.when(i == 0)
        def _():
            dw_ref[...] = jnp.zeros_like(dw_ref)
            db_ref[...] = jnp.zeros_like(db_ref)

        prev = jnp.where(i > 0, xp_ref[...], 0.0)
        ext = jnp.concatenate([prev, x_ref[...], xn_ref[...]], axis=0)
        pre = _conv_taps(ext, w_ref, b_ref, tm)
        sg = _sigmoid(pre)
        dnext = jnp.where(i < n - 1, dn_ref[...], 0.0)
        dext = jnp.concatenate([jnp.zeros((8, XBC_BLK), f32), d_ref[...], dnext], axis=0)
        dpre = dext * (sg * (1.0 + pre * (1.0 - sg)))
        rows = tm + 16
        dx = w_ref[3:4, :] * dpre
        for j in range(3):
            dx = dx + w_ref[j:j + 1, :] * pltpu.roll(dpre, rows - (3 - j), 0)
        o_ref[...] = dx[8:8 + tm].astype(bf16)
        dcur = dpre[8:8 + tm]
        db_ref[...] += jnp.sum(dcur, axis=0, keepdims=True)
        dw_ref[3:4, :] += jnp.sum(dcur * ext[8:8 + tm], axis=0, keepdims=True)
        for j in range(3):
            dw_ref[j:j + 1, :] += jnp.sum(dcur * pltpu.roll(ext, 3 - j, 0)[8:8 + tm], axis=0, keepdims=True)

    xcol = lambda j: XBC_COL0 + j
    return pl.pallas_call(
        body, name="ssd_conv_bwd", grid=(nb, n),
        in_specs=[pl.BlockSpec((tm, XBC_BLK), lambda j, i: (i, xcol(j))),
                  pl.BlockSpec((8, XBC_BLK), lambda j, i: (jnp.maximum(i * (tm // 8) - 1, 0), xcol(j))),
                  pl.BlockSpec((8, XBC_BLK), lambda j, i: (jnp.minimum((i + 1) * (tm // 8), last8), xcol(j))),
                  pl.BlockSpec((tm, XBC_BLK), lambda j, i: (i, j)),
                  pl.BlockSpec((8, XBC_BLK), lambda j, i: (jnp.minimum((i + 1) * (tm // 8), last8), j)),
                  pl.BlockSpec((4, XBC_BLK), lambda j, i: (0, j)), pl.BlockSpec((1, XBC_BLK), lambda j, i: (0, j))],
        out_specs=[pl.BlockSpec((tm, XBC_BLK), lambda j, i: (i, j)), pl.BlockSpec((4, XBC_BLK), lambda j, i: (0, j)),
                   pl.BlockSpec((1, XBC_BLK), lambda j, i: (0, j))],
        out_shape=[_sds((s, XBC), bf16), _sds((4, XBC), f32), _sds((1, XBC), f32)],
        compiler_params=_params(("parallel", "arbitrary")),
    )(proj2, proj2, proj2, dxbc, dxbc, conv_w, conv_b)


def _ssd_consts():
    ex = np.zeros((128, SSD_W), np.float32)
    for h in range(NH):
        ex[h, h * HD:(h + 1) * HD] = 1.0
    sel = np.zeros((8, 128), np.float32)
    for h in range(NH):
        sel[h // 2, h] = 1.0
    par = np.zeros((128, 128), np.float32)
    for r in range(128):
        for h in range(NH):
            par[r, h] = 1.0 if (h % 2) == (r // 64) else 0.0
    ones_blk = np.zeros((128, 128), np.float32)
    for r in range(128):
        ones_blk[r, (r // 64) * 64:(r // 64) * 64 + 64] = 1.0
    return ex, np.ascontiguousarray(ex.T), sel, par, ones_blk


SSD_SUB = 4


def _ssd_common(rs, xbc_ref, dtr_ref, a_ref, dtb_ref, ex_ref, sel_ref, par_ref):
    xs = xbc_ref[rs, 0:SSD_W]
    dt = _softplus(dtr_ref[rs, :] + dtb_ref[...])
    adt = dt * a_ref[...]
    r_i = lax.broadcasted_iota(jnp.int32, (CHUNK, CHUNK), 0)
    c_i = lax.broadcasted_iota(jnp.int32, (CHUNK, CHUNK), 1)
    tril = (r_i >= c_i).astype(f32)
    cs = _dot01(tril, adt, exact="a")
    cs2 = jnp.concatenate([cs, cs], axis=0) * par_ref[...]
    cstp = _dot01(sel_ref[...], cs2, tb=True, exact="a")
    ex = ex_ref[...]
    dt_full = _dot01(dt, ex)
    cs_full = _dot01(cs, ex)
    return xs, dt, cs, cstp, dt_full, cs_full


def _pair_mask():
    l_i = lax.broadcasted_iota(jnp.int32, (CHUNK, 128), 0)
    lane = lax.broadcasted_iota(jnp.int32, (CHUNK, 128), 1)
    return l_i >= (lane % CHUNK), lane < HD


def _block_diag(xp, first):
    z = jnp.zeros_like(xp)
    return jnp.concatenate([jnp.where(first, xp, z), jnp.where(first, z, xp)], axis=0)


def _ssd_fwd(xbc, proj2, a_row, dtb_row, dsk_full):
    s = xbc.shape[0]
    nc = s // CHUNK
    ex, ext, sel, par, ones_blk = _ssd_consts()

    def one_chunk(sub, states, refs):
        xbc_ref, dtr_ref, a_ref, dtb_ref, dsk_ref, ex_ref, sel_ref, par_ref, y_ref, hs_ref = refs
        rs = slice(sub * CHUNK, (sub + 1) * CHUNK)
        xs, dt, cs, cstp, dt_full, cs_full = _ssd_common(rs, xbc_ref, dtr_ref, a_ref, dtb_ref, ex_ref, sel_ref, par_ref)
        cs_last = cs_full[CHUNK - 1:CHUNK, :]
        xdt = xs * dt_full
        causal, first = _pair_mask()
        out = []
        for g in range(NG):
            gl = slice(g * GW, (g + 1) * GW)
            bg = xbc_ref[rs, SSD_W + g * NSTATE:SSD_W + (g + 1) * NSTATE].astype(bf16)
            cg = xbc_ref[rs, SSD_W + NG * NSTATE + g * NSTATE:SSD_W + NG * NSTATE + (g + 1) * NSTATE].astype(bf16)
            cb2 = lax.dot_general(cg, jnp.concatenate([bg, bg], axis=0), (((1,), (1,)), ((), ())), preferred_element_type=f32)
            hg = states[g]
            hs_ref[sub, g] = hg
            y0 = jnp.dot(cg, hg.astype(bf16), preferred_element_type=f32)
            yoff = jnp.exp(cs_full[:, gl]) * y0
            for j in range(GW // 128):
                pair = g * (GW // 128) + j
                pl_ = slice(pair * 128, (pair + 1) * 128)
                seg = jnp.exp(jnp.where(causal, cs_full[:, pl_] - cstp[pair:pair + 1, :], -jnp.inf))
                m = (cb2 * seg).astype(bf16)
                yd = jnp.dot(m, _block_diag(xdt[:, pl_].astype(bf16), first), preferred_element_type=f32)
                y_ref[rs, pl_] = yd + yoff[:, j * 128:(j + 1) * 128] + xs[:, pl_] * dsk_ref[:, pl_]
            xdec = (xdt[:, gl] * jnp.exp(cs_last[:, gl] - cs_full[:, gl])).astype(bf16)
            st = lax.dot_general(bg, xdec, (((0,), (0,)), ((), ())), preferred_element_type=f32)
            out.append(jnp.exp(cs_last[:, gl]) * hg + st)
        return out

    def body(*refs):
        hst = refs[-1]

        @pl.when(pl.program_id(0) == 0)
        def _():
            hst[...] = jnp.zeros_like(hst)

        states = [hst[g] for g in range(NG)]
        for sub in range(SSD_SUB):
            states = one_chunk(sub, states, refs[:-1])
        for g in range(NG):
            hst[g] = states[g]

    rows = SSD_SUB * CHUNK
    const = lambda shape: pl.BlockSpec(shape, lambda c: tuple(0 for _ in shape))
    return pl.pallas_call(
        body, name="ssd_fwd", grid=(nc // SSD_SUB,),
        in_specs=[pl.BlockSpec((rows, XBC), lambda c: (c, 0)), pl.BlockSpec((rows, 128), lambda c: (c, DT_COL)),
                  const((1, 128)), const((1, 128)), const((1, SSD_W)), const((128, SSD_W)), const((8, 128)), const((128, 128))],
        out_specs=[pl.BlockSpec((rows, SSD_W), lambda c: (c, 0)), pl.BlockSpec((SSD_SUB, NG, NSTATE, GW), lambda c: (c, 0, 0, 0))],
        out_shape=[_sds((s, SSD_W), f32), _sds((nc, NG, NSTATE, GW), f32)],
        scratch_shapes=[pltpu.VMEM((NG, NSTATE, GW), f32)], compiler_params=_params(("arbitrary",)),
    )(xbc, proj2, a_row, dtb_row, dsk_full, jnp.asarray(ex), jnp.asarray(sel), jnp.asarray(par))


def _ssd_bwd(xbc, proj2, dy, hsave, a_row, dtb_row, dsk_full):
    s = xbc.shape[0]
    nc = s // CHUNK
    ex, ext, sel, par, ones_blk = _ssd_consts()

    def one_chunk(sub, dhs, refs):
        (xbc_ref, dtr_ref, dy_ref, hs_ref, a_ref, dtb_ref, dsk_ref, ex_ref, ext_ref, sel_ref, par_ref, ob_ref,
         dxbc_ref, ddtr_ref, dd_ref, da_ref, ddtb_ref, dh, a_dd, a_da, a_dtb, dcs_lane, dcs_b, dxdt) = refs
        rs = slice(sub * CHUNK, (sub + 1) * CHUNK)
        dcs_lane, dcs_b, dxdt = dcs_lane.at[sub], dcs_b.at[sub], dxdt.at[sub]
        xs, dt, cs, cstp, dt_full, cs_full = _ssd_common(rs, xbc_ref, dtr_ref, a_ref, dtb_ref, ex_ref, sel_ref, par_ref)
        cs_last = cs_full[CHUNK - 1:CHUNK, :]
        xdt = xs * dt_full
        dyv = dy_ref[rs, :]
        a_dd[...] += _fold8(dyv * xs)
        causal, first = _pair_mask()
        ones_l = jnp.ones((CHUNK, 128), f32)
        dh_out = []
        for g in range(NG):
            gl = slice(g * GW, (g + 1) * GW)
            bcol = slice(SSD_W + g * NSTATE, SSD_W + (g + 1) * NSTATE)
            ccol = slice(SSD_W + NG * NSTATE + g * NSTATE, SSD_W + NG * NSTATE + (g + 1) * NSTATE)
            bg = xbc_ref[rs, bcol].astype(bf16)
            cg = xbc_ref[rs, ccol].astype(bf16)
            bg2 = jnp.concatenate([bg, bg], axis=0)
            cb2 = lax.dot_general(cg, bg2, (((1,), (1,)), ((), ())), preferred_element_type=f32)
            hg = hs_ref[sub, g]
            hgb = hg.astype(bf16)
            dhg = dhs[g]
            dhgb = dhg.astype(bf16)
            eg = jnp.exp(cs_full[:, gl])
            dec = jnp.exp(cs_last[:, gl] - cs_full[:, gl])
            gam = jnp.exp(cs_last[:, gl])
            dyg = dyv[:, gl]
            xdt_g = xdt[:, gl]
            y0 = jnp.dot(cg, hgb, preferred_element_type=f32)
            dy0 = (eg * dyg).astype(bf16)
            dcm = lax.dot_general(dy0, hgb, (((1,), (1,)), ((), ())), preferred_element_type=f32)
            dh_prev = gam * dhg + lax.dot_general(cg, dy0, (((0,), (0,)), ((), ())), preferred_element_type=f32)
            dgam = jnp.sum(dhg * hg, axis=0, keepdims=True) * gam
            dxdec = jnp.dot(bg, dhgb, preferred_element_type=f32)
            dbm = lax.dot_general((xdt_g * dec).astype(bf16), dhgb, (((1,), (1,)), ((), ())), preferred_element_type=f32)
            t = dxdec * xdt_g * dec
            dcs_lane[:, gl] = dyg * eg * y0 - t
            dcs_lane[CHUNK - 1:CHUNK, gl] += jnp.sum(t, axis=0, keepdims=True) + dgam
            dxdt[:, gl] = dxdec * dec
            dcb2 = jnp.zeros((CHUNK, 128), f32)
            for j in range(GW // 128):
                pair = g * (GW // 128) + j
                pl_ = slice(pair * 128, (pair + 1) * 128)
                seg = jnp.exp(jnp.where(causal, cs_full[:, pl_] - cstp[pair:pair + 1, :], -jnp.inf))
                m = cb2 * seg
                mb = m.astype(bf16)
                rhs = _block_diag(xdt[:, pl_].astype(bf16), first)
                dyp = dyv[:, pl_].astype(bf16)
                dm = lax.dot_general(dyp, rhs, (((1,), (1,)), ((), ())), preferred_element_type=f32)
                tt = lax.dot_general(mb, dyp, (((0,), (0,)), ((), ())), preferred_element_type=f32)
                dxdt[:, pl_] += jnp.where(first, tt[0:CHUNK], tt[CHUNK:])
                dcb2 = dcb2 + dm * seg
                w = dm * m
                rsum = _dot01(w, ob_ref[...])
                t2 = _dot01(w, ones_l, ta=True)
                dcs_b[:, pl_] = rsum - jnp.where(first, t2[0:CHUNK], t2[CHUNK:])
            dcb2b = dcb2.astype(bf16)
            dcm = dcm + jnp.dot(dcb2b, bg2, preferred_element_type=f32)
            t3 = lax.dot_general(dcb2b, cg, (((0,), (0,)), ((), ())), preferred_element_type=f32)
            dxbc_ref[rs, bcol] = dbm + t3[0:CHUNK] + t3[CHUNK:]
            dxbc_ref[rs, ccol] = dcm
            dh_out.append(dh_prev)
        dcs = _dot01(dcs_lane[...] + dcs_b[...] * (1.0 / HD), ext_ref[...])
        r_i = lax.broadcasted_iota(jnp.int32, (CHUNK, CHUNK), 0)
        c_i = lax.broadcasted_iota(jnp.int32, (CHUNK, CHUNK), 1)
        triu = (r_i <= c_i).astype(f32)
        da_ = _dot01(triu, dcs, exact="a")
        dxdtv = dxdt[...]
        ddt = da_ * a_ref[...] + _dot01(dxdtv * xs, ext_ref[...])
        a_da[...] += _fold8(da_ * dt)
        dxbc_ref[rs, 0:SSD_W] = dyv * dsk_ref[...] + dxdtv * dt_full
        ddtr = ddt * _sigmoid(dtr_ref[rs, :] + dtb_ref[...])
        ddtr_ref[rs, :] = ddtr
        a_dtb[...] += _fold8(ddtr)
        return dh_out

    nsteps = nc // SSD_SUB

    def body(*refs):
        dd_ref, da_ref, ddtb_ref, dh, a_dd, a_da, a_dtb = refs[14:21]
        ext_ref = refs[8]
        step = pl.program_id(0)

        @pl.when(step == 0)
        def _():
            dh[...] = jnp.zeros_like(dh)
            a_dd[...] = jnp.zeros_like(a_dd)
            a_da[...] = jnp.zeros_like(a_da)
            a_dtb[...] = jnp.zeros_like(a_dtb)

        dhs = [dh[g] for g in range(NG)]
        for sub in reversed(range(SSD_SUB)):
            dhs = one_chunk(sub, dhs, refs)
        for g in range(NG):
            dh[g] = dhs[g]

        @pl.when(step == nsteps - 1)
        def _():
            dd_ref[...] = jnp.sum(jnp.dot(a_dd[...], ext_ref[...], precision=HIGHEST, preferred_element_type=f32), axis=0, keepdims=True)
            da_ref[...] = jnp.sum(a_da[...], axis=0, keepdims=True)
            ddtb_ref[...] = jnp.sum(a_dtb[...], axis=0, keepdims=True)

    rev = lambda c: nsteps - 1 - c
    rows = SSD_SUB * CHUNK
    const = lambda shape: pl.BlockSpec(shape, lambda c: tuple(0 for _ in shape))
    return pl.pallas_call(
        body, name="ssd_bwd", grid=(nsteps,),
        in_specs=[pl.BlockSpec((rows, XBC), lambda c: (rev(c), 0)), pl.BlockSpec((rows, 128), lambda c: (rev(c), DT_COL)),
                  pl.BlockSpec((rows, SSD_W), lambda c: (rev(c), 0)), pl.BlockSpec((SSD_SUB, NG, NSTATE, GW), lambda c: (rev(c), 0, 0, 0)),
                  const((1, 128)), const((1, 128)), const((1, SSD_W)), const((128, SSD_W)), const((SSD_W, 128)),
                  const((8, 128)), const((128, 128)), const((128, 128))],
        out_specs=[pl.BlockSpec((rows, XBC), lambda c: (rev(c), 0)), pl.BlockSpec((rows, 128), lambda c: (rev(c), 0)),
                   const((1, 128)), const((1, 128)), const((1, 128))],
        out_shape=[_sds((s, XBC), f32), _sds((s, 128), f32), _sds((1, 128), f32), _sds((1, 128), f32), _sds((1, 128), f32)],
        scratch_shapes=[pltpu.VMEM((NG, NSTATE, GW), f32), pltpu.VMEM((8, SSD_W), f32), pltpu.VMEM((8, 128), f32), pltpu.VMEM((8, 128), f32)]
        + [pltpu.VMEM((SSD_SUB, CHUNK, SSD_W), f32)] * 3,
        compiler_params=_params(("arbitrary",)),
    )(xbc, proj2, dy, hsave, a_row, dtb_row, dsk_full, jnp.asarray(ex), jnp.asarray(ext), jnp.asarray(sel), jnp.asarray(par),
      jnp.asarray(ones_blk))


def _local_step(x, tgt, mods, g_mix, rel, conv_w, conv_b, dt_bias, a_log, d_skip, g_att, g_ssd, g_ffn, g_final, weights):
    s = x.shape[0]
    tm_e = 256 if s % 256 == 0 else s
    tm_m = 512 if s % 512 == 0 else s
    tm_l = 1024 if s % 1024 == 0 else s
    tk = 2048 if s % 2048 == 0 else s
    sh1, sc1, gt1, sh2, sc2, gt2 = [mods[:, i * D:(i + 1) * D] for i in range(6)]

    h1b = _norm_mod("norm_mod_1", x, g_mix, sc1, sh1, tm_e)
    win, win_b = weights.w_in(h1b)
    qkv = _mm_nn_fullk("proj_qkv", h1b, win, tm_l, 768, bf16, n=IN_A)
    proj2 = _mm_nn_fullk("proj_zxbcdt", h1b, win_b, tm_l, 896, f32)
    bias = _expand_bias(rel)
    att = _attn_fwd(qkv, bias)
    xbc = _ssd_conv(proj2, conv_w, conv_b, tm_e)
    a_row = jnp.pad(-jnp.exp(a_log), ((0, 0), (0, 128 - NH)))
    dtb_row = jnp.pad(dt_bias, ((0, 0), (0, 128 - NH)))
    dsk_full = jnp.repeat(d_skip, HD, axis=1)
    y, hsave = _ssd_fwd(xbc, proj2, a_row, dtb_row, dsk_full)
    mixcat = _mix_pre(att, y, proj2, g_att, g_ssd, tm_e)
    wout = weights.w_out(mixcat)
    mix = _mm_nn_fullk("proj_out", mixcat, wout, tm_l, 1024, f32)
    x2, h2b = _resid_norm_mod(x, gt1, mix, g_ffn, sc2, sh2, tm_e)
    wg4, wu4, wd4 = weights.ffn(h2b)
    act, sil, ud = _ffn_up(h2b, wg4, wu4, tm_m)
    ffn = _ffn_down(act, wd4, tm_l)

    dx3, dffn, loss, dg_final, dgt2 = _final_fwd_bwd(x2, ffn, gt2, g_final, tgt, tm_e)
    tok = weights.grad(("w_down",), [_grad_wdown4(act, dffn, 1024, tk)])
    dgate, dup = _ffn_dact(dffn, wd4, sil, ud, tm_l, dep=tok)
    tok = weights.grad(("w_gate", "w_up"), [_grad_cols4("grad_w_gate", h2b, dgate, 1024, tk), _grad_cols4("grad_w_up", h2b, dup, 1024, tk)])
    dh2 = _ffn_dh(dgate, dup, wg4, wu4, tm_m, dep=tok)
    dx2, dmix, dsc2, dsh2, dg_ffn, dgt1 = _norm_mod_bwd("norm_mod_bwd_2", dh2, x2, g_ffn, sc2, dx3, tm_e, mix=mix, gt=gt1)
    tok = weights.grad(("w_out",), [_mm_tn("grad_w_out", mixcat, dmix, 1024, 1024, tk, bf16).reshape(NSH, D // NSH, D)])
    dmc = _mm_nt("dmixcat", dmix, wout, tm_l, 1024, D, f32, dep=tok)
    datt, dy, dz, dg_att, dg_ssd = _mix_pre_bwd(dmc, att, y, proj2, g_att, g_ssd, tm_e)
    dq, dk, dv, gband = _attn_bwd(qkv, datt, bias)
    drel = _rel_bias_grad(gband.reshape(NH, CHUNK, BANDP))
    dxbc, ddtr, dd_row, da_row, ddtb_row = _ssd_bwd(xbc, proj2, dy, hsave, a_row, dtb_row, dsk_full)
    dxbc_raw, dconv_w, dconv_b = _ssd_conv_bwd(dxbc, proj2, conv_w, conv_b, tm_e)
    dproj = jnp.concatenate([dq, dk, dv, dz, dxbc_raw, ddtr.astype(bf16)], axis=1)
    gwin = _mm_tn("grad_w_in", h1b, dproj, 1024, 1152, tk, bf16)
    gwin4 = jnp.stack([jnp.pad(gwin[:, k * IN_SH:(k + 1) * IN_SH], ((0, 0), (0, IN_SHP - IN_SH))) for k in range(NSH)])
    tok = weights.grad(("w_in",), [gwin4])
    dh1 = _mm_nt("dh1", dproj, win, tm_l, 1024, 1920, f32, dep=tok)
    grad_x, dsc1, dsh1, dg_mix = _norm_mod_bwd("norm_mod_bwd_1", dh1, x, g_mix, sc1, dx2, tm_e)

    dmods = jnp.concatenate([dsh1, dsc1, dgt1, dsh2, dsc2, dgt2], axis=1)
    dd_skip = dd_row[:, :NH]
    da_log = da_row[:, :NH] * a_row[:, :NH]
    small = dict(g_mix=dg_mix, conv_b=dconv_b, dt_bias=ddtb_row[:, :NH], a_log=da_log, d_skip=dd_skip, g_att_out=dg_att,
                 g_ssd_out=dg_ssd, g_ffn=dg_ffn, g_final=dg_final, rel_bias=drel, conv_w=dconv_w)
    return loss[0, 0], grad_x, dmods, small


HBM = pl.BlockSpec(memory_space=pl.ANY)
VMEM = pl.BlockSpec(memory_space=pltpu.VMEM)


def _place():
    x, y, c = lax.axis_index("x"), lax.axis_index("y"), lax.axis_index("c")
    chips = [(1 - x, y), (x, 1 - y), (1 - x, 1 - y)]
    return x, y, c, chips


def _allgather8(name, payload, dep=None):
    r = payload.shape[0]
    deps = [] if dep is None else [dep]

    def body(x_ref, *rest):
        out_ref, send_sems, recv_sems, local_sem = rest[-4:]
        x, y, c, chips = _place()
        me, sibling = (x, y, c), (x, y, 1 - c)

        def slot(px, py, pc):
            return out_ref.at[4 * px + 2 * py + pc]

        def copy(k, block, to, src=None):
            return pltpu.make_async_remote_copy(
                src_ref=slot(*block) if src is None else src, dst_ref=slot(*block),
                send_sem=send_sems.at[k], recv_sem=recv_sems.at[k], device_id=to, device_id_type=MESH)

        mine = pltpu.make_async_copy(x_ref, slot(*me), local_sem)
        mine.start()
        first = [copy(0, me, sibling, src=x_ref)]
        first += [copy(1 + j, me, (*chip, c), src=x_ref) for j, chip in enumerate(chips)]
        for cp in first:
            cp.start()
        passed = [copy(4 + j, (*chip, c), sibling) for j, chip in enumerate(chips)]
        for j, chip in enumerate(chips):
            copy(1 + j, (*chip, c), me).wait_recv()
            passed[j].start()
        copy(0, sibling, me).wait_recv()
        for j, chip in enumerate(chips):
            copy(4 + j, (*chip, 1 - c), me).wait_recv()
        for cp in first + passed:
            cp.wait_send()
        mine.wait()

    return pl.pallas_call(
        body, name=name, out_shape=_sds((N_DEV, r, 128), f32), in_specs=[VMEM] * (1 + len(deps)), out_specs=VMEM,
        scratch_shapes=[pltpu.SemaphoreType.DMA((7,)), pltpu.SemaphoreType.DMA((7,)), pltpu.SemaphoreType.DMA],
    )(payload, *deps)


def _sum8(g):
    r = g.shape[1]

    def body(g_ref, o_ref):
        acc = g_ref[0]
        for i in range(1, N_DEV):
            acc = acc + g_ref[i]
        o_ref[...] = acc

    return pl.pallas_call(body, name="sum8", out_shape=_sds((r, 128), f32))(g)


SEM = pl.BlockSpec(memory_space=pltpu.SEMAPHORE)
EFFECT = pltpu.SideEffectType.DATAFLOW_SIDE_EFFECTING


def _gather_copies(ins, lands, send_sems, recv_sems):
    x, y, c, chips = _place()
    k = 2 * x + y
    starts, recvs = [], []
    for w in range(len(ins)):
        for j, (px, py) in enumerate(chips):
            def mk(dst):
                return pltpu.make_async_remote_copy(src_ref=ins[w].at[c], dst_ref=dst, send_sem=send_sems[w].at[j],
                                                    recv_sem=recv_sems[w].at[j], device_id=(px, py, c), device_id_type=MESH)
            starts.append(mk(lands[w].at[k, c]))
            recvs.append(mk(lands[w].at[2 * px + py, c]))
    return starts, recvs


def _reduce_copies(ins, lands, send_sems, recv_sems):
    x, y, c, chips = _place()
    k = 2 * x + y
    starts, recvs = [], []
    for w in range(len(ins)):
        for j, (px, py) in enumerate(chips):
            def mk(dst):
                return pltpu.make_async_remote_copy(src_ref=ins[w].at[2 * px + py], dst_ref=dst, send_sem=send_sems[w].at[j],
                                                    recv_sem=recv_sems[w].at[j], device_id=(px, py, c), device_id_type=MESH)
            starts.append(mk(lands[w].at[k]))
            recvs.append(mk(lands[w].at[2 * px + py]))
    return starts, recvs


def _split_start(name, copies, srcs, land_shapes):
    nw = len(srcs)

    def body(*refs):
        starts, _ = copies(refs[:nw], refs[nw:2 * nw], refs[2 * nw:3 * nw], refs[3 * nw:4 * nw])
        for cp in starts:
            cp.start()
        refs[6 * nw][...] = jnp.zeros((8, 128), f32)

    sems = [pltpu.SemaphoreType.DMA((3,))] * nw
    bufs = [pltpu.HBM(s.shape, bf16) for s in srcs] + [pltpu.HBM(s, bf16) for s in land_shapes]
    res = pl.pallas_call(
        body, name=name, out_shape=sems + sems + bufs + [_sds((8, 128), f32)],
        in_specs=[HBM] * (2 * nw), out_specs=[SEM] * (2 * nw) + [HBM] * (2 * nw) + [VMEM],
        input_output_aliases={i: 2 * nw + i for i in range(2 * nw)},
        compiler_params=pltpu.CompilerParams(has_side_effects=EFFECT),
    )(*[pltpu.with_memory_space_constraint(s, pltpu.HBM) for s in srcs],
      *[pltpu.with_memory_space_constraint(lax.empty(s, bf16), pltpu.HBM) for s in land_shapes])
    return res[:nw], res[nw:2 * nw], res[2 * nw:3 * nw], res[3 * nw:4 * nw], res[4 * nw]


def _split_wait(name, copies, send_sems, recv_sems, srcs, lands, after):
    nw = len(srcs)

    def body(*refs):
        starts, recvs = copies(refs[:nw], refs[nw:2 * nw], refs[2 * nw:3 * nw], refs[3 * nw:4 * nw])
        for s_, r_ in zip(starts, recvs):
            s_.wait_send()
            r_.wait_recv()

    bufs = [pltpu.HBM(s.shape, bf16) for s in srcs] + [pltpu.HBM(l.shape, bf16) for l in lands]
    res = pl.pallas_call(
        body, name=name, out_shape=bufs, in_specs=[HBM] * (2 * nw) + [SEM] * (2 * nw) + [HBM], out_specs=[HBM] * (2 * nw),
        input_output_aliases={i: i for i in range(2 * nw)},
        compiler_params=pltpu.CompilerParams(has_side_effects=EFFECT),
    )(*srcs, *lands, *send_sems, *recv_sems, after)
    return res[:nw], res[nw:]


def _gather_forward(name, shards, lands):
    nw = len(shards)

    def body(*refs):
        ins, lands_in, outs = refs[:nw], refs[nw:2 * nw], refs[2 * nw:3 * nw]
        st_a, st_b, st_c = refs[3 * nw:4 * nw], refs[4 * nw:5 * nw], refs[5 * nw:6 * nw]
        send_sems, recv_sems, load_sems, store_sems = refs[6 * nw:]
        x, y, c, chips = _place()
        k = 2 * x + y
        sibling = (x, y, 1 - c)
        ld_a = [pltpu.make_async_copy(ins[w].at[c], st_a[w], load_sems.at[w, 0]) for w in range(nw)]
        ld_b = [pltpu.make_async_copy(ins[w].at[1 - c], st_b[w], load_sems.at[w, 1]) for w in range(nw)]
        for cp in ld_a + ld_b:
            cp.start()
        st_own = []
        for w in range(nw):
            ld_a[w].wait()
            st_own.append(pltpu.make_async_copy(st_a[w], outs[w].at[k, c], store_sems.at[w, 0]))
            st_own[-1].start()
        for w in range(nw):
            ld_b[w].wait()
            st_own.append(pltpu.make_async_copy(st_b[w], outs[w].at[k, 1 - c], store_sems.at[w, 1]))
            st_own[-1].start()
        for cp in st_own:
            cp.wait()
        fwds = {}
        for j, (px, py) in enumerate(chips):
            kq = 2 * px + py
            for w in range(nw):
                slot = st_b[w] if j % 2 == 0 else st_c[w]
                if j == 2:
                    fwds[w, 0].wait_send()
                ld = pltpu.make_async_copy(lands_in[w].at[kq, c], slot, load_sems.at[w, 2 + j])
                ld.start()
                ld.wait()
                fwds[w, j] = pltpu.make_async_remote_copy(src_ref=slot, dst_ref=outs[w].at[kq, c], send_sem=send_sems.at[w, j],
                                                          recv_sem=recv_sems.at[w, j], device_id=sibling, device_id_type=MESH)
                fwds[w, j].start()
        for j, (px, py) in enumerate(chips):
            for w in range(nw):
                pltpu.make_async_remote_copy(src_ref=st_c[w], dst_ref=outs[w].at[2 * px + py, 1 - c], send_sem=send_sems.at[w, j],
                                             recv_sem=recv_sems.at[w, j], device_id=sibling, device_id_type=MESH).wait_recv()
        for w in range(nw):
            fwds[w, 1].wait_send()
            fwds[w, 2].wait_send()

    stage = [pltpu.VMEM(s.shape[1:], bf16) for s in shards]
    return pl.pallas_call(
        body, name=name, out_shape=[_sds(l.shape, bf16) for l in lands],
        in_specs=[HBM] * (2 * nw), out_specs=[HBM] * nw, input_output_aliases={nw + w: w for w in range(nw)},
        scratch_shapes=stage * 3 + [pltpu.SemaphoreType.DMA((nw, 3)), pltpu.SemaphoreType.DMA((nw, 3)), pltpu.SemaphoreType.DMA((nw, 5)),
                                    pltpu.SemaphoreType.DMA((nw, 2))],
        compiler_params=pltpu.CompilerParams(vmem_limit_bytes=VMEM_LIMIT),
    )(*shards, *lands)


def _rs_pair_exchange(name, grads):
    nw = len(grads)

    def body(*refs):
        ins, got, stage = refs[:nw], refs[nw:2 * nw], refs[2 * nw:3 * nw]
        send_sems, recv_sems, load_sems = refs[3 * nw:]
        x, y, c, _ = _place()

        def load(w, kk):
            return pltpu.make_async_copy(ins[w].at[kk, 1 - c], stage[w].at[kk % 2], load_sems.at[w, kk])

        def send(w, kk):
            return pltpu.make_async_remote_copy(src_ref=stage[w].at[kk % 2], dst_ref=got[w].at[kk], send_sem=send_sems.at[w, kk],
                                                recv_sem=recv_sems.at[w, kk], device_id=(x, y, 1 - c), device_id_type=MESH)

        for kk in range(2):
            for w in range(nw):
                load(w, kk).start()
        for kk in range(NSH):
            for w in range(nw):
                load(w, kk).wait()
                send(w, kk).start()
            if kk + 2 < NSH:
                for w in range(nw):
                    send(w, kk).wait_send()
                    load(w, kk + 2).start()
        for kk in range(NSH - 2, NSH):
            for w in range(nw):
                send(w, kk).wait_send()
        for kk in range(NSH):
            for w in range(nw):
                send(w, kk).wait_recv()

    return pl.pallas_call(
        body, name=name, out_shape=[_sds((NSH,) + g.shape[2:], bf16) for g in grads], in_specs=[HBM] * nw, out_specs=[HBM] * nw,
        scratch_shapes=[pltpu.VMEM((2,) + g.shape[2:], bf16) for g in grads]
        + [pltpu.SemaphoreType.DMA((nw, NSH)), pltpu.SemaphoreType.DMA((nw, NSH)), pltpu.SemaphoreType.DMA((nw, NSH))],
        compiler_params=pltpu.CompilerParams(vmem_limit_bytes=VMEM_LIMIT),
    )(*grads)


def _rs_pair_gather(name, halves):
    nw = len(halves)

    def body(*refs):
        ins, outs, stage = refs[:nw], refs[nw:2 * nw], refs[2 * nw:3 * nw]
        send_sems, recv_sems, local_sems, stage_sems = refs[3 * nw:]
        x, y, c, _ = _place()
        loads = [pltpu.make_async_copy(ins[w], stage[w], stage_sems.at[w]) for w in range(nw)]
        for cp in loads:
            cp.start()
        local, cps = [], []
        for w in range(nw):
            loads[w].wait()
            local.append(pltpu.make_async_copy(stage[w], outs[w].at[c], local_sems.at[w]))
            cps.append(pltpu.make_async_remote_copy(src_ref=stage[w], dst_ref=outs[w].at[c], send_sem=send_sems.at[w],
                                                    recv_sem=recv_sems.at[w], device_id=(x, y, 1 - c), device_id_type=MESH))
            local[w].start()
            cps[w].start()
        for w in range(nw):
            pltpu.make_async_remote_copy(src_ref=stage[w], dst_ref=outs[w].at[1 - c], send_sem=send_sems.at[w], recv_sem=recv_sems.at[w],
                                         device_id=(x, y, 1 - c), device_id_type=MESH).wait_recv()
        for cp in cps:
            cp.wait_send()
        for cp in local:
            cp.wait()

    return pl.pallas_call(
        body, name=name, out_shape=[_sds((2,) + h.shape, f32) for h in halves], in_specs=[HBM] * nw, out_specs=[HBM] * nw,
        scratch_shapes=[pltpu.VMEM(h.shape, f32) for h in halves]
        + [pltpu.SemaphoreType.DMA((nw,)), pltpu.SemaphoreType.DMA((nw,)), pltpu.SemaphoreType.DMA((nw,)), pltpu.SemaphoreType.DMA((nw,))],
        compiler_params=pltpu.CompilerParams(vmem_limit_bytes=VMEM_LIMIT),
    )(*halves)


def _row_tile(r, c, nbuf):
    budget = 24 * 1024 * 1024 // (2 * nbuf * 4 * c)
    t = 8
    while t * 2 <= budget and r % (t * 2) == 0:
        t *= 2
    return t


def _cast_bf16(name, a, dep=None):
    r, c = a.shape
    tr = _row_tile(r, c, 2)
    dep_specs, dep_ops = _dep_args(dep, 1)

    def body(a_ref, *rest):
        rest[-1][...] = a_ref[...].astype(bf16)

    spec = pl.BlockSpec((tr, c), lambda i: (i, 0))
    return pl.pallas_call(body, name=name, grid=(r // tr,), in_specs=[spec] + dep_specs, out_specs=spec, out_shape=_sds((r, c), bf16),
                          compiler_params=_params(("parallel",)))(a, *dep_ops)


def _w_in_columns(win4):
    tr = 256

    def body(a_ref, o_ref, ob_ref):
        for k in range(NSH):
            o_ref[:, IN_SH * k:IN_SH * (k + 1)] = a_ref[k][:, :IN_SH]
        o_ref[:, IN_COLS:] = jnp.zeros((tr, IN_P - IN_COLS), bf16)
        ob_ref[...] = o_ref[:, IN_A:]

    return pl.pallas_call(
        body, name="w_in_columns", grid=(D // tr,), in_specs=[pl.BlockSpec((NSH, tr, IN_SHP), lambda i: (0, i, 0))],
        out_specs=[pl.BlockSpec((tr, IN_P), lambda i: (i, 0)), pl.BlockSpec((tr, IN_B), lambda i: (i, 0))],
        out_shape=[_sds((D, IN_P), bf16), _sds((D, IN_B), bf16)], compiler_params=_params(("parallel",)))(win4)


def _pair_sum(name, core, grads, got):
    _, _, rh, c = grads.shape
    tr = _row_tile(rh, c, 2)

    def body(c_ref, a_ref, b_ref, o_ref):
        o_ref[...] = (a_ref[...].astype(f32) + b_ref[...].astype(f32)).astype(bf16)

    spec = pl.BlockSpec((None, tr, c), lambda k, i, c_ref: (k, i, 0))
    return pl.pallas_call(
        body, name=name, out_shape=_sds((NSH, rh, c), bf16),
        grid_spec=pltpu.PrefetchScalarGridSpec(
            num_scalar_prefetch=1, grid=(NSH, rh // tr),
            in_specs=[pl.BlockSpec((None, None, tr, c), lambda k, i, c_ref: (k, c_ref[0], i, 0)), spec], out_specs=spec),
        compiler_params=_params(("parallel", "parallel")))(core, grads, got)


def _chip_sum(name, chip, sums, lands):
    _, rh, c = sums.shape
    tr = _row_tile(rh, c, 4)

    def body(k_ref, own_ref, l_ref, o_ref):
        own = own_ref[...].astype(f32)
        acc = None
        for j in range(NSH):
            term = jnp.where(k_ref[0] == j, own, l_ref[j].astype(f32))
            acc = term if acc is None else acc + term
        o_ref[...] = acc

    return pl.pallas_call(
        body, name=name, out_shape=_sds((rh, c), f32),
        grid_spec=pltpu.PrefetchScalarGridSpec(
            num_scalar_prefetch=1, grid=(rh // tr,),
            in_specs=[pl.BlockSpec((None, tr, c), lambda i, k_ref: (k_ref[0], i, 0)), pl.BlockSpec((NSH, tr, c), lambda i, k_ref: (0, i, 0))],
            out_specs=pl.BlockSpec((tr, c), lambda i, k_ref: (i, 0))),
        compiler_params=_params(("parallel",)))(chip, sums, lands)


def _mods_part(cond16, w_ada, b_part):
    n = w_ada.shape[1]
    tn = 512

    def body(c_ref, w_ref, b_ref, o_ref):
        cv = c_ref[...]
        o_ref[...] = _dot(cv * _sigmoid(cv), w_ref[...]) + b_ref[...]

    return pl.pallas_call(
        body, name="mods_part", grid=(n // tn,),
        in_specs=[pl.BlockSpec((16, D), lambda j: (0, 0)), pl.BlockSpec((D, tn), lambda j: (0, j)), pl.BlockSpec((1, tn), lambda j: (0, j))],
        out_specs=pl.BlockSpec((16, tn), lambda j: (0, j)), out_shape=_sds((16, n), f32), compiler_params=_params(("parallel",)),
    )(cond16, w_ada, b_part)


def _grad_w_ada(cond16, dm16):
    n = dm16.shape[1]
    tr = 256

    def body(c_ref, d_ref, o_ref):
        cv = c_ref[...]
        o_ref[...] = _dot(cv * _sigmoid(cv), d_ref[...], ta=True)

    return pl.pallas_call(
        body, name="grad_w_ada", grid=(D // tr,),
        in_specs=[pl.BlockSpec((16, tr), lambda i: (0, i)), pl.BlockSpec((16, n), lambda i: (0, 0))],
        out_specs=pl.BlockSpec((tr, n), lambda i: (i, 0)), out_shape=_sds((D, n), f32), compiler_params=_params(("parallel",)),
    )(cond16, dm16)


def _adamw(name, w, g, m, v):
    r, c = w.shape
    tr = _row_tile(r, c, 7)

    def body(w_ref, g_ref, m_ref, v_ref, d_ref, nm_ref, nv_ref):
        gv = g_ref[...]
        nm = ADAM_B1 * m_ref[...] + (1.0 - ADAM_B1) * gv
        nv = ADAM_B2 * v_ref[...] + (1.0 - ADAM_B2) * (gv * gv)
        nm_ref[...] = nm
        nv_ref[...] = nv
        m_hat = nm / (1.0 - ADAM_B1 ** ADAM_STEP)
        v_hat = nv / (1.0 - ADAM_B2 ** ADAM_STEP)
        d_ref[...] = -ADAM_LR * (m_hat / (jnp.sqrt(v_hat) + ADAM_EPS) + ADAM_WD * w_ref[...])

    spec = pl.BlockSpec((tr, c), lambda i: (i, 0))
    return pl.pallas_call(body, name=name, grid=(r // tr,), in_specs=[spec] * 4, out_specs=[spec] * 3, out_shape=[_sds((r, c), f32)] * 3,
                          compiler_params=_params(("parallel",)))(w, g, m, v)


def _pack(parts, rows):
    flat = []
    for p in parts:
        p = p.reshape(-1)
        flat.append(jnp.pad(p, (0, (-p.shape[0]) % 128)))
    v = jnp.concatenate(flat)
    return jnp.pad(v, (0, rows * 128 - v.shape[0])).reshape(rows, 128)


def _unpack(packed, sizes):
    lead = packed.shape[:-2]
    flat = packed.reshape(lead + (-1,))
    out, off = [], 0
    for n in sizes:
        out.append(flat[..., off:off + n])
        off += n + (-n) % 128
    return out


BIG = ("w_in", "w_out", "w_gate", "w_up", "w_down")
SMALL = ("b_ada", "g_mix", "conv_b", "dt_bias", "a_log", "d_skip", "g_att_out", "g_ssd_out", "g_ffn", "g_final", "rel_bias", "conv_w")
ORDER = ("w_ada", "b_ada", "g_mix", "w_in", "rel_bias", "conv_w", "conv_b", "dt_bias", "a_log", "d_skip", "g_att_out", "g_ssd_out",
         "w_out", "g_ffn", "w_gate", "w_up", "w_down", "g_final")
REL_SH = N_REL // NSH
CONVW_SH = XBC // NSH
ADA_SH = 6 * D // NSH


class _Exchange:
    def __init__(self, core, chip):
        self.core, self.chip = core, chip
        self.gathered = {}
        self.pending = []

    def gather(self, names, shards):
        ssem, rsem, thru, lands, token = _split_start("gather_start_" + "_".join(names), _gather_copies, shards,
                                                      [(NSH,) + s.shape for s in shards])
        self.gathered.update({n: (ssem[i], rsem[i], thru[i], lands[i]) for i, n in enumerate(names)})
        return token

    def _whole(self, names, after):
        ssem, rsem, thru, lands = zip(*[self.gathered[n] for n in names])
        tag = "_".join(names)
        thru, lands = _split_wait("gather_wait_" + tag, _gather_copies, ssem, rsem, thru, lands, after)
        return _gather_forward("gather_forward_" + tag, thru, lands)

    def w_in(self, after):
        (win4,) = self._whole(("w_in",), after)
        return _w_in_columns(win4.reshape(NSH, D, IN_SHP))

    def w_out(self, after):
        (wout4,) = self._whole(("w_out",), after)
        return wout4.reshape(D, D)

    def ffn(self, after):
        wg4, wu4, wd4 = self._whole(("w_gate", "w_up", "w_down"), after)
        return wg4.reshape(NSH, D, FSH), wu4.reshape(NSH, D, FSH), wd4.reshape(NSH, FSH, D)

    def grad(self, names, grads):
        tag = "_".join(names)
        stacked = [g.reshape(NSH, 2, g.shape[1] // 2, g.shape[2]) for g in grads]
        got = _rs_pair_exchange("rs_pair_exchange_" + tag, stacked)
        sums = [_pair_sum("pair_sum_" + n, self.core, o, g) for n, o, g in zip(names, stacked, got)]
        self.pending.append((names, _split_start("rs_start_" + tag, _reduce_copies, sums, [s.shape for s in sums])))
        return self.pending[-1][1][4]

    def finish(self, after):
        grads = {}
        for names, (ssem, rsem, sums, lands, _) in self.pending:
            tag = "_".join(names)
            sums, lands = _split_wait("rs_wait_" + tag, _reduce_copies, ssem, rsem, sums, lands, after)
            halves = [_chip_sum("chip_sum_" + n, self.chip, sm, ld) for n, sm, ld in zip(names, sums, lands)]
            for n, f in zip(names, _rs_pair_gather("rs_pair_gather_" + tag, halves)):
                grads[n] = f.reshape(2 * f.shape[1], f.shape[2])
        return grads


def kernel(x, c, w_ada, b_ada, g_mix, w_in, rel_bias, conv_w, conv_b, dt_bias, a_log, d_skip, g_att_out, g_ssd_out, w_out, g_ffn, w_gate, w_up, w_down, g_final, loss_target, m_w_ada, m_b_ada, m_g_mix, m_w_in, m_rel_bias, m_conv_w, m_conv_b, m_dt_bias, m_a_log, m_d_skip, m_g_att_out, m_g_ssd_out, m_w_out, m_g_ffn, m_w_gate, m_w_up, m_w_down, m_g_final, v_w_ada, v_b_ada, v_g_mix, v_w_in, v_rel_bias, v_conv_w, v_conv_b, v_dt_bias, v_a_log, v_d_skip, v_g_att_out, v_g_ssd_out, v_w_out, v_g_ffn, v_w_gate, v_w_up, v_w_down, v_g_final):
    args = dict(locals())
    w = {n: args[n] for n in ORDER}
    m = {n: args["m_" + n] for n in ORDER}
    v = {n: args["v_" + n] for n in ORDER}
    ix, iy, ic = lax.axis_index("x"), lax.axis_index("y"), lax.axis_index("c")
    chip = 2 * ix + iy
    dev = 2 * chip + ic
    s = x.shape[1]

    g1 = _allgather8("gather_inputs", _pack([c[0], rel_bias[0], conv_w[0]], 40))
    c_all, rel_sh, convw_sh = _unpack(g1, [D, NH * REL_SH, 4 * CONVW_SH])
    rel_full = jnp.concatenate([rel_sh[2 * k].reshape(NH, REL_SH) for k in range(NSH)], axis=1)
    convw_full = jnp.concatenate([convw_sh[2 * k].reshape(4, CONVW_SH) for k in range(NSH)], axis=1)
    cond16 = jnp.pad(c_all, ((0, 8), (0, 0)))
    b_part = lax.dynamic_slice_in_dim(b_ada, chip * ADA_SH, ADA_SH, axis=1)
    mods_part = _mods_part(cond16, w_ada[0], b_part)[:N_DEV]
    g2 = _allgather8("gather_mods", mods_part.reshape(N_DEV * ADA_SH // 128, 128))
    mods_all = jnp.concatenate([g2[2 * k].reshape(N_DEV, ADA_SH) for k in range(NSH)], axis=1)
    mods = lax.dynamic_slice_in_dim(mods_all, dev, 1, axis=0)

    exchange = _Exchange(jnp.reshape(ic, (1,)).astype(jnp.int32), jnp.reshape(chip, (1,)).astype(jnp.int32))
    shard_in = _cast_bf16("cast_w_in", jnp.pad(w_in[0], ((0, 0), (0, IN_SHP - IN_SH))), dep=g2[0, :8]).reshape(2, D // 2, IN_SHP)
    tok = exchange.gather(("w_in",), [shard_in])
    tok = exchange.gather(("w_out", "w_gate", "w_up", "w_down"), [
        _cast_bf16("cast_w_out", w_out[0], dep=tok).reshape(2, D // NSH // 2, D),
        _cast_bf16("cast_w_gate", w_gate[0], dep=tok).reshape(2, D // 2, FSH),
        _cast_bf16("cast_w_up", w_up[0], dep=tok).reshape(2, D // 2, FSH),
        _cast_bf16("cast_w_down", w_down[0], dep=tok).reshape(2, FSH // 2, D)])
    mods = mods + tok[:1, :1]

    loss, grad_x, dmods, small = _local_step(
        x[0], loss_target[0], mods, g_mix, rel_full, convw_full, conv_b, dt_bias, a_log, d_skip, g_att_out, g_ssd_out, g_ffn,
        g_final[None, :], exchange)

    small_names = ("g_mix", "conv_b", "dt_bias", "a_log", "d_skip", "g_att_out", "g_ssd_out", "g_ffn", "g_final", "rel_bias", "conv_w")
    g3 = _allgather8("gather_small_grads", _pack([dmods] + [small[n] for n in small_names], 264))
    sizes = [6 * D] + [int(np.prod(small[n].shape)) for n in small_names]
    dmods_all = _unpack(g3, sizes)[0]
    summed = _unpack(_sum8(g3), sizes)
    grads = {"b_ada": summed[0].reshape(1, 6 * D)}
    for n, val in zip(small_names, summed[1:]):
        grads[n] = val.reshape(small[n].shape)
    grads["rel_bias"] = lax.dynamic_slice_in_dim(grads["rel_bias"], chip * REL_SH, REL_SH, axis=1)
    grads["conv_w"] = lax.dynamic_slice_in_dim(grads["conv_w"], chip * CONVW_SH, CONVW_SH, axis=1)
    grads["g_final"] = grads["g_final"].reshape(D)
    dm16 = jnp.pad(lax.dynamic_slice_in_dim(dmods_all, chip * ADA_SH, ADA_SH, axis=1), ((0, 8), (0, 0)))
    grads["w_ada"] = _grad_w_ada(cond16, dm16)

    delta, new_m, new_v = {}, {}, {}
    delta["w_ada"], new_m["w_ada"], new_v["w_ada"] = _adamw("adamw_w_ada", w_ada[0], grads["w_ada"], m_w_ada[0], v_w_ada[0])
    grads.update(exchange.finish(grad_x))
    grads["w_in"] = grads["w_in"][:, :IN_SH]
    for n in BIG:
        delta[n], new_m[n], new_v[n] = _adamw("adamw_" + n, w[n][0], grads[n], m[n][0], v[n][0])
    sw = _pack([w[n] for n in SMALL], 200)
    sg = _pack([grads[n] for n in SMALL], 200)
    sm = _pack([m[n] for n in SMALL], 200)
    sv = _pack([v[n] for n in SMALL], 200)
    ssz = [int(np.prod(w[n].shape)) for n in SMALL]
    for dst, packed in zip((delta, new_m, new_v), _adamw("adamw_small", sw, sg, sm, sv)):
        for n, val in zip(SMALL, _unpack(packed, ssz)):
            dst[n] = val

    def shaped(d, n):
        return d[n].reshape(w[n].shape)

    total = lax.psum(loss, ("x", "y", "c"))
    return (total, grad_x[None], *[shaped(grads, n) for n in ORDER], *[shaped(delta, n) for n in ORDER],
            *[shaped(new_m, n) for n in ORDER], *[shaped(new_v, n) for n in ORDER])
```

```python
import functools

import numpy as np
import jax
import jax.numpy as jnp
from jax import lax
from jax.experimental import pallas as pl
from jax.experimental.pallas import tpu as pltpu

f32 = jnp.float32
bf16 = jnp.bfloat16
HIGHEST = lax.Precision.HIGHEST
MESH = pl.DeviceIdType.MESH

D = 2048
CHUNK = 64
LEFT = 8
BAND = (LEFT + 1) * CHUNK
BANDP = 640
PADK = LEFT * CHUNK
NH = 16
HD = 64
ATT_W = NH * HD
SSD_W = 1024
NG = 2
NSTATE = 128
GW = SSD_W // NG
XBC = SSD_W + 2 * NG * NSTATE
N_REL = 320
REL_CLIP = 256
FFN = 5632
NSH = 4
FSH = FFN // NSH
IN_COLS = 5648
IN_SH = IN_COLS // NSH
IN_SHP = 1536
IN_A = 3 * ATT_W
IN_B = 2688
IN_P = IN_A + IN_B
EPS = 1e-6
N_DEV = 8

ADAM_LR = 0.001
ADAM_B1 = 0.9
ADAM_B2 = 0.999
ADAM_EPS = 1e-08
ADAM_WD = 0.01
ADAM_STEP = 10

VMEM_LIMIT = 56 * 1024 * 1024


def _params(sem):
    return pltpu.CompilerParams(dimension_semantics=sem, vmem_limit_bytes=VMEM_LIMIT)


def _sds(shape, dtype):
    return jax.ShapeDtypeStruct(shape, dtype)


def _fold8(v):
    r, w = v.shape
    return jnp.sum(v.reshape(r // 8, 8, w), axis=0)


STRIP = 16


def _strips(tm, fn):
    def step(j, carry):
        fn(pl.ds(pl.multiple_of(j * STRIP, STRIP), STRIP))
        return carry
    lax.fori_loop(0, tm // STRIP, step, 0, unroll=4)


def _sigmoid(v):
    return 1.0 / (1.0 + jnp.exp(-v))


def _softplus(v):
    return jnp.maximum(v, 0.0) + jnp.log(1.0 + jnp.exp(-jnp.abs(v)))


def _dot(a, b, ta=False, tb=False):
    dn = (((0 if ta else 1,), (1 if tb else 0,)), ((), ()))
    return lax.dot_general(a.astype(bf16), b.astype(bf16), dn, preferred_element_type=f32)


def _dep_args(dep, ngrid):
    if dep is None:
        return [], []
    return [pl.BlockSpec((8, 128), lambda *_: (0, 0))], [dep]


def _dot01(a, b, ta=False, tb=False, exact="b"):
    dn = (((0 if ta else 1,), (1 if tb else 0,)), ((), ()))
    x = a if exact == "b" else b
    hi = x.astype(bf16)
    r = x - hi.astype(f32)
    mid = r.astype(bf16)
    lo = (r - mid.astype(f32)).astype(bf16)
    if exact == "b":
        m = b.astype(bf16)
        return sum(lax.dot_general(p, m, dn, preferred_element_type=f32) for p in (hi, mid, lo))
    m = a.astype(bf16)
    return sum(lax.dot_general(m, p, dn, preferred_element_type=f32) for p in (hi, mid, lo))


def _matmul(name, a, b, *, grid, a_spec, b_spec, o_spec, o_shape, o_dtype, acc_shape, ta=False, tb=False, dep=None):
    nk = grid[2]
    dep_specs, dep_ops = _dep_args(dep, 3)

    def body(a_ref, b_ref, *rest):
        o_ref, acc_ref = rest[-2:]
        p = _dot(a_ref[...], b_ref[...], ta, tb)
        if nk == 1:
            o_ref[...] = p.astype(o_ref.dtype)
        else:
            k = pl.program_id(2)

            @pl.when(k == 0)
            def _():
                acc_ref[...] = p

            @pl.when(jnp.logical_and(k > 0, k < nk - 1))
            def _():
                acc_ref[...] += p

            @pl.when(k == nk - 1)
            def _():
                o_ref[...] = (acc_ref[...] + p).astype(o_ref.dtype)

    return pl.pallas_call(
        body, name=name, grid=grid, in_specs=[a_spec, b_spec] + dep_specs, out_specs=o_spec,
        out_shape=_sds(o_shape, o_dtype), scratch_shapes=[pltpu.VMEM(acc_shape if nk > 1 else (8, 128), f32)],
        compiler_params=_params(("parallel", "parallel", "arbitrary")),
    )(a, b, *dep_ops)


def _mm_nn_fullk(name, a, b, tm, tn, o_dtype, n=None):
    m, k = a.shape
    n = b.shape[1] if n is None else n
    return _matmul(name, a, b, grid=(m // tm, n // tn, 1),
                   a_spec=pl.BlockSpec((tm, k), lambda i, j, kk: (i, 0)),
                   b_spec=pl.BlockSpec((k, tn), lambda i, j, kk: (0, j)),
                   o_spec=pl.BlockSpec((tm, tn), lambda i, j, kk: (i, j)),
                   o_shape=(m, n), o_dtype=o_dtype, acc_shape=(tm, tn))


def _mm_nt(name, a, b, tm, tn, tk, o_dtype, dep=None):
    m, k = a.shape
    n = b.shape[0]
    return _matmul(name, a, b, grid=(m // tm, n // tn, k // tk), tb=True, dep=dep,
                   a_spec=pl.BlockSpec((tm, tk), lambda i, j, kk: (i, kk)),
                   b_spec=pl.BlockSpec((tn, tk), lambda i, j, kk: (j, kk)),
                   o_spec=pl.BlockSpec((tm, tn), lambda i, j, kk: (i, j)),
                   o_shape=(m, n), o_dtype=o_dtype, acc_shape=(tm, tn))


def _mm_tn(name, a, b, tm, tn, tk, o_dtype):
    k, m = a.shape
    n = b.shape[1]
    return _matmul(name, a, b, grid=(m // tm, n // tn, k // tk), ta=True,
                   a_spec=pl.BlockSpec((tk, tm), lambda i, j, kk: (kk, i)),
                   b_spec=pl.BlockSpec((tk, tn), lambda i, j, kk: (kk, j)),
                   o_spec=pl.BlockSpec((tm, tn), lambda i, j, kk: (i, j)),
                   o_shape=(m, n), o_dtype=o_dtype, acc_shape=(tm, tn))


FSH_PARTS = (slice(0, 640), slice(640, FSH))


def _ffn_up(h2b, wg4, wu4, tm):
    s = h2b.shape[0]

    def body(h_ref, wg_ref, wu_ref, a_ref, s_ref, ud_ref):
        h = h_ref[...]
        for cols in FSH_PARTS:
            g = _dot(h, wg_ref[:, cols])
            u = _dot(h, wu_ref[:, cols])
            sg = _sigmoid(g)
            sil = g * sg
            a_ref[:, cols] = (sil * u).astype(bf16)
            s_ref[:, cols] = sil.astype(bf16)
            ud_ref[:, cols] = (u * (sg * (1.0 + g * (1.0 - sg)))).astype(bf16)

    wspec = pl.BlockSpec((None, D, FSH), lambda k, i: (k, 0, 0))
    ospec = pl.BlockSpec((tm, FSH), lambda k, i: (i, k))
    return pl.pallas_call(
        body, name="ffn_up", grid=(NSH, s // tm),
        in_specs=[pl.BlockSpec((tm, D), lambda k, i: (i, 0)), wspec, wspec],
        out_specs=[ospec, ospec, ospec], out_shape=[_sds((s, FFN), bf16)] * 3,
        compiler_params=_params(("parallel", "parallel")),
    )(h2b, wg4, wu4)


def _ffn_down(act, wd4, tm):
    s = act.shape[0]
    return _matmul("ffn_down", act, wd4, grid=(s // tm, 1, NSH),
                   a_spec=pl.BlockSpec((tm, FSH), lambda i, j, k: (i, k)),
                   b_spec=pl.BlockSpec((None, FSH, D), lambda i, j, k: (k, 0, 0)),
                   o_spec=pl.BlockSpec((tm, D), lambda i, j, k: (i, 0)),
                   o_shape=(s, D), o_dtype=f32, acc_shape=(tm, D))


def _ffn_dact(dffn, wd4, sil, ud, tm, dep=None):
    s = dffn.shape[0]
    dep_specs, dep_ops = _dep_args(dep, 2)

    def body(d_ref, w_ref, s_ref, ud_ref, *rest):
        dg_ref, du_ref = rest[-2:]
        d = d_ref[...]
        for cols in FSH_PARTS:
            dact = _dot(d, w_ref[cols, :], tb=True)
            dg_ref[:, cols] = (dact * ud_ref[:, cols].astype(f32)).astype(bf16)
            du_ref[:, cols] = (dact * s_ref[:, cols].astype(f32)).astype(bf16)

    blk = pl.BlockSpec((tm, FSH), lambda k, i: (i, k))
    return pl.pallas_call(
        body, name="ffn_dact", grid=(NSH, s // tm),
        in_specs=[pl.BlockSpec((tm, D), lambda k, i: (i, 0)), pl.BlockSpec((None, FSH, D), lambda k, i: (k, 0, 0)), blk, blk] + dep_specs,
        out_specs=[blk, blk], out_shape=[_sds((s, FFN), bf16), _sds((s, FFN), bf16)],
        compiler_params=_params(("parallel", "parallel")),
    )(dffn, wd4, sil, ud, *dep_ops)


def _ffn_dh(dgate, dup, wg4, wu4, tm, dep=None):
    s = dgate.shape[0]
    dep_specs, dep_ops = _dep_args(dep, 2)

    def body(dg_ref, du_ref, wg_ref, wu_ref, *rest):
        o_ref, acc_ref = rest[-2:]
        k = pl.program_id(1)
        p = _dot(dg_ref[...], wg_ref[...], tb=True) + _dot(du_ref[...], wu_ref[...], tb=True)

        @pl.when(k == 0)
        def _():
            acc_ref[...] = p

        @pl.when(jnp.logical_and(k > 0, k < NSH - 1))
        def _():
            acc_ref[...] += p

        @pl.when(k == NSH - 1)
        def _():
            o_ref[...] = acc_ref[...] + p

    aspec = pl.BlockSpec((tm, FSH), lambda i, k: (i, k))
    wspec = pl.BlockSpec((None, D, FSH), lambda i, k: (k, 0, 0))
    return pl.pallas_call(
        body, name="ffn_dh", grid=(s // tm, NSH), in_specs=[aspec, aspec, wspec, wspec] + dep_specs,
        out_specs=pl.BlockSpec((tm, D), lambda i, k: (i, 0)), out_shape=_sds((s, D), f32),
        scratch_shapes=[pltpu.VMEM((tm, D), f32)], compiler_params=_params(("parallel", "arbitrary")),
    )(dgate, dup, wg4, wu4, *dep_ops)


def _grad_cols4(name, h, dy, tm, tk):
    s = h.shape[0]
    return _matmul(name, h, dy, grid=(NSH, D // tm, s // tk), ta=True,
                   a_spec=pl.BlockSpec((tk, tm), lambda k, i, kk: (kk, i)),
                   b_spec=pl.BlockSpec((tk, FSH), lambda k, i, kk: (kk, k)),
                   o_spec=pl.BlockSpec((None, tm, FSH), lambda k, i, kk: (k, i, 0)),
                   o_shape=(NSH, D, FSH), o_dtype=bf16, acc_shape=(tm, FSH))


def _grad_wdown4(act, dffn, tn, tk):
    s = act.shape[0]
    return _matmul("grad_w_down", act, dffn, grid=(NSH, D // tn, s // tk), ta=True,
                   a_spec=pl.BlockSpec((tk, FSH), lambda k, j, kk: (kk, k)),
                   b_spec=pl.BlockSpec((tk, tn), lambda k, j, kk: (kk, j)),
                   o_spec=pl.BlockSpec((None, FSH, tn), lambda k, j, kk: (k, 0, j)),
                   o_shape=(NSH, FSH, D), o_dtype=bf16, acc_shape=(FSH, tn))


def _row_spec(w):
    return pl.BlockSpec((1, w), lambda i: (0, 0))


def _tile_spec(tm, w, col=0):
    return pl.BlockSpec((tm, w), lambda i: (i, col))


def _norm_mod(name, x, g, sc, sh, tm):
    s = x.shape[0]

    def body(x_ref, g_ref, sc_ref, sh_ref, o_ref):
        def strip(rows):
            xv = x_ref[rows, :]
            r = lax.rsqrt(jnp.mean(xv * xv, axis=-1, keepdims=True) + EPS)
            o_ref[rows, :] = (xv * r * g_ref[...] * (1.0 + sc_ref[...]) + sh_ref[...]).astype(bf16)

        _strips(tm, strip)

    return pl.pallas_call(
        body, name=name, grid=(s // tm,), in_specs=[_tile_spec(tm, D), _row_spec(D), _row_spec(D), _row_spec(D)],
        out_specs=_tile_spec(tm, D), out_shape=_sds((s, D), bf16), compiler_params=_params(("parallel",)),
    )(x, g, sc, sh)


def _resid_norm_mod(x, gt, mix, g, sc, sh, tm):
    s = x.shape[0]

    def body(x_ref, gt_ref, m_ref, g_ref, sc_ref, sh_ref, x2_ref, h_ref):
        def strip(rows):
            xv = x_ref[rows, :] + gt_ref[...] * m_ref[rows, :]
            x2_ref[rows, :] = xv
            r = lax.rsqrt(jnp.mean(xv * xv, axis=-1, keepdims=True) + EPS)
            h_ref[rows, :] = (xv * r * g_ref[...] * (1.0 + sc_ref[...]) + sh_ref[...]).astype(bf16)

        _strips(tm, strip)

    return pl.pallas_call(
        body, name="resid_norm_mod", grid=(s // tm,),
        in_specs=[_tile_spec(tm, D), _row_spec(D), _tile_spec(tm, D), _row_spec(D), _row_spec(D), _row_spec(D)],
        out_specs=[_tile_spec(tm, D), _tile_spec(tm, D)], out_shape=[_sds((s, D), f32), _sds((s, D), bf16)],
        compiler_params=_params(("parallel",)),
    )(x, gt, mix, g, sc, sh)


def _final_fwd_bwd(x2, ffn, gt2, g, tgt, tm):
    s = x2.shape[0]
    n = s // tm

    def body(x_ref, f_ref, gt_ref, g_ref, t_ref, dx_ref, df_ref, loss_ref, dg_ref, dgt_ref, a_loss, a_dg, a_dgt):
        i = pl.program_id(0)

        @pl.when(i == 0)
        def _():
            a_loss[...] = jnp.zeros_like(a_loss)
            a_dg[...] = jnp.zeros_like(a_dg)
            a_dgt[...] = jnp.zeros_like(a_dgt)

        def strip(rows):
            fv = f_ref[rows, :]
            gt = gt_ref[...]
            gv = g_ref[...]
            xv = x_ref[rows, :] + gt * fv
            r = lax.rsqrt(jnp.mean(xv * xv, axis=-1, keepdims=True) + EPS)
            xh = xv * r
            e = xh * gv - t_ref[rows, :]
            a_loss[...] += _fold8(e * e)
            dy = e * (1.0 / D)
            a_dg[...] += _fold8(dy * xh)
            t = dy * gv
            dx = r * (t - xh * jnp.mean(t * xh, axis=-1, keepdims=True))
            dx_ref[rows, :] = dx
            a_dgt[...] += _fold8(dx * fv)
            df_ref[rows, :] = (dx * gt).astype(bf16)

        _strips(tm, strip)

        @pl.when(i == n - 1)
        def _():
            tot = jnp.sum(jnp.sum(a_loss[...], axis=0, keepdims=True), axis=1, keepdims=True) * (0.5 / D)
            loss_ref[...] = jnp.broadcast_to(tot, (1, 128))
            dg_ref[...] = jnp.sum(a_dg[...], axis=0, keepdims=True)
            dgt_ref[...] = jnp.sum(a_dgt[...], axis=0, keepdims=True)

    return pl.pallas_call(
        body, name="final_fwd_bwd", grid=(n,),
        in_specs=[_tile_spec(tm, D), _tile_spec(tm, D), _row_spec(D), _row_spec(D), _tile_spec(tm, D)],
        out_specs=[_tile_spec(tm, D), _tile_spec(tm, D), _row_spec(128), _row_spec(D), _row_spec(D)],
        out_shape=[_sds((s, D), f32), _sds((s, D), bf16), _sds((1, 128), f32), _sds((1, D), f32), _sds((1, D), f32)],
        scratch_shapes=[pltpu.VMEM((8, D), f32)] * 3, compiler_params=_params(("arbitrary",)),
    )(x2, ffn, gt2, g, tgt)


def _norm_mod_bwd(name, dh, xin, g, sc, dres, tm, mix=None, gt=None):
    s = dh.shape[0]
    n = s // tm
    with_mix = mix is not None

    def body(*refs):
        if with_mix:
            dh_ref, x_ref, g_ref, sc_ref, dr_ref, m_ref, gt_ref, dx_ref, dm_ref, dsc_ref, dsh_ref, dg_ref, dgt_ref, a_sc, a_sh, a_g, a_gt = refs
        else:
            dh_ref, x_ref, g_ref, sc_ref, dr_ref, dx_ref, dsc_ref, dsh_ref, dg_ref, a_sc, a_sh, a_g = refs
        i = pl.program_id(0)

        @pl.when(i == 0)
        def _():
            a_sc[...] = jnp.zeros_like(a_sc)
            a_sh[...] = jnp.zeros_like(a_sh)
            a_g[...] = jnp.zeros_like(a_g)
            if with_mix:
                a_gt[...] = jnp.zeros_like(a_gt)

        def strip(rows):
            dh = dh_ref[rows, :]
            xv = x_ref[rows, :]
            gv = g_ref[...]
            r = lax.rsqrt(jnp.mean(xv * xv, axis=-1, keepdims=True) + EPS)
            xh = xv * r
            a_sc[...] += _fold8(dh * xh * gv)
            a_sh[...] += _fold8(dh)
            dn = dh * (1.0 + sc_ref[...])
            a_g[...] += _fold8(dn * xh)
            t = dn * gv
            dx = dr_ref[rows, :] + r * (t - xh * jnp.mean(t * xh, axis=-1, keepdims=True))
            dx_ref[rows, :] = dx
            if with_mix:
                a_gt[...] += _fold8(dx * m_ref[rows, :])
                dm_ref[rows, :] = (dx * gt_ref[...]).astype(bf16)

        _strips(tm, strip)

        @pl.when(i == n - 1)
        def _():
            dsc_ref[...] = jnp.sum(a_sc[...], axis=0, keepdims=True)
            dsh_ref[...] = jnp.sum(a_sh[...], axis=0, keepdims=True)
            dg_ref[...] = jnp.sum(a_g[...], axis=0, keepdims=True)
            if with_mix:
                dgt_ref[...] = jnp.sum(a_gt[...], axis=0, keepdims=True)

    tile, row = _tile_spec(tm, D), _row_spec(D)
    if with_mix:
        ins, args = [tile, tile, row, row, tile, tile, row], (dh, xin, g, sc, dres, mix, gt)
        outs = [tile, tile, row, row, row, row]
        shapes = [_sds((s, D), f32), _sds((s, D), bf16)] + [_sds((1, D), f32)] * 4
        nacc = 4
    else:
        ins, args = [tile, tile, row, row, tile], (dh, xin, g, sc, dres)
        outs = [tile, row, row, row]
        shapes = [_sds((s, D), f32)] + [_sds((1, D), f32)] * 3
        nacc = 3
    return pl.pallas_call(
        body, name=name, grid=(n,), in_specs=ins, out_specs=outs, out_shape=shapes,
        scratch_shapes=[pltpu.VMEM((8, D), f32)] * nacc, compiler_params=_params(("arbitrary",)),
    )(*args)


def _mix_pre(att, y, proj2, g_att, g_ssd, tm):
    s = att.shape[0]

    def body(a_ref, y_ref, z_ref, ga_ref, gs_ref, o_ref):
        def strip(rows):
            a = a_ref[rows, :]
            ra = lax.rsqrt(jnp.mean(a * a, axis=-1, keepdims=True) + EPS)
            o_ref[rows, 0:ATT_W] = (a * ra * ga_ref[...]).astype(bf16)
            z = z_ref[rows, :]
            u = y_ref[rows, :] * (z * _sigmoid(z))
            ru = lax.rsqrt(jnp.mean(u * u, axis=-1, keepdims=True) + EPS)
            o_ref[rows, ATT_W:] = (u * ru * gs_ref[...]).astype(bf16)

        _strips(tm, strip)

    t = _tile_spec(tm, ATT_W)
    return pl.pallas_call(
        body, name="mix_pre", grid=(s // tm,), in_specs=[t, t, t, _row_spec(ATT_W), _row_spec(SSD_W)],
        out_specs=_tile_spec(tm, D), out_shape=_sds((s, D), bf16), compiler_params=_params(("parallel",)),
    )(att, y, proj2, g_att, g_ssd)


def _mix_pre_bwd(dmc, att, y, proj2, g_att, g_ssd, tm):
    s = att.shape[0]
    n = s // tm

    def body(da_ref, ds_ref, a_ref, y_ref, z_ref, ga_ref, gs_ref, datt_ref, dy_ref, dz_ref, dga_ref, dgs_ref, acc_a, acc_s):
        i = pl.program_id(0)

        @pl.when(i == 0)
        def _():
            acc_a[...] = jnp.zeros_like(acc_a)
            acc_s[...] = jnp.zeros_like(acc_s)

        def strip(rows):
            a = a_ref[rows, :]
            ra = lax.rsqrt(jnp.mean(a * a, axis=-1, keepdims=True) + EPS)
            ah = a * ra
            dan = da_ref[rows, :]
            acc_a[...] += _fold8(dan * ah)
            t = dan * ga_ref[...]
            datt_ref[rows, :] = (ra * (t - ah * jnp.mean(t * ah, axis=-1, keepdims=True))).astype(bf16)
            z = z_ref[rows, :]
            yv = y_ref[rows, :]
            sz = _sigmoid(z)
            sil = z * sz
            u = yv * sil
            ru = lax.rsqrt(jnp.mean(u * u, axis=-1, keepdims=True) + EPS)
            uh = u * ru
            dsn = ds_ref[rows, :]
            acc_s[...] += _fold8(dsn * uh)
            t2 = dsn * gs_ref[...]
            du = ru * (t2 - uh * jnp.mean(t2 * uh, axis=-1, keepdims=True))
            dy_ref[rows, :] = du * sil
            dz_ref[rows, :] = (du * yv * (sz * (1.0 + z * (1.0 - sz)))).astype(bf16)

        _strips(tm, strip)

        @pl.when(i == n - 1)
        def _():
            dga_ref[...] = jnp.sum(acc_a[...], axis=0, keepdims=True)
            dgs_ref[...] = jnp.sum(acc_s[...], axis=0, keepdims=True)

    t = _tile_spec(tm, ATT_W)
    row = _row_spec(ATT_W)
    return pl.pallas_call(
        body, name="mix_pre_bwd", grid=(n,),
        in_specs=[_tile_spec(tm, ATT_W, 0), _tile_spec(tm, ATT_W, 1), t, t, t, row, row],
        out_specs=[t, t, t, row, row],
        out_shape=[_sds((s, ATT_W), bf16), _sds((s, SSD_W), f32), _sds((s, SSD_W), bf16), _sds((1, ATT_W), f32), _sds((1, SSD_W), f32)],
        scratch_shapes=[pltpu.VMEM((8, ATT_W), f32)] * 2, compiler_params=_params(("arbitrary",)),
    )(dmc, dmc, att, y, proj2, g_att, g_ssd)


ATT_GROUP = 8
ATT_GROUP_FWD = 8


def _pair_rows(qc):
    two = jnp.concatenate([qc, qc], axis=0)
    r = lax.broadcasted_iota(jnp.int32, (2 * CHUNK, 128), 0)
    l = lax.broadcasted_iota(jnp.int32, (2 * CHUNK, 128), 1)
    return jnp.where((r < CHUNK) == (l < HD), two, jnp.zeros_like(two))


def _scaled(q):
    return q * jnp.asarray(HD ** -0.5, q.dtype)


def _pair_scores(wt, kb, bias, r0, masked):
    sc = lax.dot_general(wt, kb, (((1,), (1,)), ((), ())), preferred_element_type=f32) + bias
    if not masked:
        return sc
    kidx = lax.broadcasted_iota(jnp.int32, sc.shape, 1)
    return jnp.where(r0 + kidx >= PADK, sc, -jnp.inf)


def _softmax(sc, axis):
    e = jnp.exp(sc - jnp.max(sc, axis=axis, keepdims=True))
    return e * (1.0 / jnp.sum(e, axis=axis, keepdims=True))


def _chunk_loops(nc, group, per_trip):
    n_masked = min(-(-LEFT // per_trip), nc // per_trip)

    def run(masked):
        def step(g, carry):
            group(g, masked)
            return carry
        return step

    lax.fori_loop(0, n_masked, run(True), 0)
    lax.fori_loop(n_masked, nc // per_trip, run(False), 0)


def _pair_diag(r):
    lane = lax.broadcasted_iota(jnp.int32, (CHUNK, 128), 1)
    return jnp.where(lane < HD, r[0:CHUNK], r[CHUNK:])


def _pad_keys(k_ref, kp, s):
    kp[0:PADK, :] = jnp.zeros((PADK, 128), bf16)
    kp[PADK:PADK + s, :] = k_ref[...]
    kp[PADK + s:, :] = jnp.zeros((CHUNK, 128), bf16)


def _attn_fwd(qkv, bias2):
    s = qkv.shape[0]
    nc = s // CHUNK
    npair = NH // 2

    def body(q_ref, k_ref, v_ref, b_ref, o_ref, kp, vp):
        _pad_keys(k_ref, kp, s)
        _pad_keys(v_ref, vp, s)

        def group(g, masked):
            r0s = [pl.multiple_of((g * ATT_GROUP_FWD + u) * CHUNK, CHUNK) for u in range(ATT_GROUP_FWD)]
            scs = [_pair_scores(_pair_rows(_scaled(q_ref[pl.ds(r0, CHUNK), :])), kp[pl.ds(r0, BANDP), :], b_ref[...], r0, masked)
                   for r0 in r0s]
            ps = [_softmax(sc, -1).astype(bf16) for sc in scs]
            for r0, p in zip(r0s, ps):
                o_ref[pl.ds(r0, CHUNK), :] = _pair_diag(jnp.dot(p, vp[pl.ds(r0, BANDP), :], preferred_element_type=f32))

        _chunk_loops(nc, group, ATT_GROUP_FWD)

    return pl.pallas_call(
        body, name="attn_fwd", grid=(npair,),
        in_specs=[pl.BlockSpec((s, 128), lambda p: (0, p)), pl.BlockSpec((s, 128), lambda p: (0, npair + p)),
                  pl.BlockSpec((s, 128), lambda p: (0, 2 * npair + p)), pl.BlockSpec((None, 2 * CHUNK, BANDP), lambda p: (p, 0, 0))],
        out_specs=pl.BlockSpec((s, 128), lambda p: (0, p)), out_shape=_sds((s, ATT_W), f32),
        scratch_shapes=[pltpu.VMEM((PADK + s + CHUNK, 128), bf16)] * 2, compiler_params=_params(("parallel",)),
    )(qkv, qkv, qkv, bias2)


def _attn_bwd(qkv, datt, bias2):
    s = qkv.shape[0]
    nc = s // CHUNK
    npair = NH // 2
    rows = PADK + s + CHUNK
    nt = (((1,), (1,)), ((), ()))

    def body(q_ref, k_ref, v_ref, do_ref, b_ref, dq_ref, dk_ref, dv_ref, g_ref, kp, vp, dkp, dvp):
        _pad_keys(k_ref, kp, s)
        _pad_keys(v_ref, vp, s)
        dkp[...] = jnp.zeros_like(dkp)
        dvp[...] = jnp.zeros_like(dvp)
        g_ref[...] = jnp.zeros_like(g_ref)

        def group(g, masked):
            r0s = [pl.multiple_of((g * ATT_GROUP + u) * CHUNK, CHUNK) for u in range(ATT_GROUP)]
            wts = [_pair_rows(_scaled(q_ref[pl.ds(r0, CHUNK), :])) for r0 in r0s]
            dos = [_pair_rows(do_ref[pl.ds(r0, CHUNK), :]) for r0 in r0s]
            scs = [_pair_scores(wt, kp[pl.ds(r0, BANDP), :], b_ref[...], r0, masked) for wt, r0 in zip(wts, r0s)]
            dps = [lax.dot_general(do, vp[pl.ds(r0, BANDP), :], nt, preferred_element_type=f32) for do, r0 in zip(dos, r0s)]
            tn_ = (((0,), (0,)), ((), ()))
            for r0, wt, do, sc, dp in zip(r0s, wts, dos, scs, dps):
                p = _softmax(sc, -1)
                ds = p * (dp - jnp.sum(p * dp, axis=-1, keepdims=True))
                g_ref[...] += ds
                dsb = ds.astype(bf16)
                dq = jnp.dot(dsb, kp[pl.ds(r0, BANDP), :], preferred_element_type=f32)
                dq_ref[pl.ds(r0, CHUNK), :] = (_pair_diag(dq) * (HD ** -0.5)).astype(bf16)
                dkp[pl.ds(r0, BANDP), :] += lax.dot_general(dsb, wt, tn_, preferred_element_type=f32)
                dvp[pl.ds(r0, BANDP), :] += lax.dot_general(p.astype(bf16), do, tn_, preferred_element_type=f32)

        _chunk_loops(nc, group, ATT_GROUP)
        dk_ref[...] = dkp[PADK:PADK + s, :].astype(bf16)
        dv_ref[...] = dvp[PADK:PADK + s, :].astype(bf16)

    col = lambda off: pl.BlockSpec((s, 128), lambda p: (0, off + p))
    return pl.pallas_call(
        body, name="attn_bwd", grid=(npair,),
        in_specs=[col(0), col(npair), col(2 * npair), col(0), pl.BlockSpec((None, 2 * CHUNK, BANDP), lambda p: (p, 0, 0))],
        out_specs=[col(0), col(0), col(0), pl.BlockSpec((None, 2 * CHUNK, BANDP), lambda p: (p, 0, 0))],
        out_shape=[_sds((s, ATT_W), bf16)] * 3 + [_sds((npair, 2 * CHUNK, BANDP), f32)],
        scratch_shapes=[pltpu.VMEM((rows, 128), bf16)] * 2 + [pltpu.VMEM((rows, 128), f32)] * 2,
        compiler_params=_params(("parallel",)),
    )(qkv, qkv, qkv, datt, bias2)


def _rel_tables():
    onehot = np.zeros((BANDP, N_REL), np.float32)
    for j in range(BAND + CHUNK - 1):
        o = j - (CHUNK - 1)
        onehot[j, int(np.clip(PADK - o, -(CHUNK - 1), REL_CLIP)) + CHUNK - 1] = 1.0
    return onehot, np.ascontiguousarray(np.eye(CHUNK, dtype=np.float32)[::-1])


def _expand_bias(rel):
    ext = jnp.concatenate([jnp.broadcast_to(rel[:, N_REL - 1:], (NH, N_REL - 1)), rel[:, ::-1],
                           jnp.zeros((NH, BANDP - BAND + 1), f32)], axis=1)
    band = jnp.stack([ext[:, CHUNK - 1 - q:CHUNK - 1 - q + BANDP] for q in range(CHUNK)], axis=1)
    band = jnp.where(np.arange(BANDP) < BAND, band, -jnp.inf)
    return band.reshape(NH // 2, 2 * CHUNK, BANDP)


def _rel_bias_grad(gband):
    def body(g_ref, m_ref, flip_ref, o_ref, d2):
        for h in range(NH):
            rev = jnp.dot(flip_ref[...], g_ref[h], precision=HIGHEST, preferred_element_type=f32)
            rolled = pltpu.roll(rev, 0, 1, stride=1, stride_axis=0)
            d2[h:h + 1, :] = jnp.sum(rolled, axis=0, keepdims=True)
        o_ref[...] = jnp.dot(d2[...], m_ref[...], precision=HIGHEST, preferred_element_type=f32)

    onehot, flip = _rel_tables()
    return pl.pallas_call(
        body, name="rel_bias_grad", out_shape=_sds((NH, N_REL), f32), scratch_shapes=[pltpu.VMEM((NH, BANDP), f32)],
    )(gband, jnp.asarray(onehot), jnp.asarray(flip))


XBC_BLK = 512
XBC_COL0 = SSD_W // XBC_BLK
DT_COL = (SSD_W + XBC) // 128


def _conv_taps(ext, w_ref, b_ref, tm):
    n = ext.shape[0]
    pre = w_ref[3:4, :] * ext + b_ref[...]
    for j in range(3):
        pre = pre + w_ref[j:j + 1, :] * pltpu.roll(ext, 3 - j, 0)
    return pre


def _ssd_conv(proj2, conv_w, conv_b, tm):
    s = proj2.shape[0]
    nb = XBC // XBC_BLK

    def body(x_ref, p_ref, w_ref, b_ref, o_ref):
        i = pl.program_id(1)
        prev = jnp.where(i > 0, p_ref[...], 0.0)
        ext = jnp.concatenate([prev, x_ref[...]], axis=0)
        pre = _conv_taps(ext, w_ref, b_ref, tm)[8:8 + tm]
        o_ref[...] = pre * _sigmoid(pre)

    return pl.pallas_call(
        body, name="ssd_conv", grid=(nb, s // tm),
        in_specs=[pl.BlockSpec((tm, XBC_BLK), lambda j, i: (i, XBC_COL0 + j)),
                  pl.BlockSpec((8, XBC_BLK), lambda j, i: (jnp.maximum(i * (tm // 8) - 1, 0), XBC_COL0 + j)),
                  pl.BlockSpec((4, XBC_BLK), lambda j, i: (0, j)), pl.BlockSpec((1, XBC_BLK), lambda j, i: (0, j))],
        out_specs=pl.BlockSpec((tm, XBC_BLK), lambda j, i: (i, j)), out_shape=_sds((s, XBC), f32),
        compiler_params=_params(("parallel", "parallel")),
    )(proj2, proj2, conv_w, conv_b)


def _ssd_conv_bwd(dxbc, proj2, conv_w, conv_b, tm):
    s = proj2.shape[0]
    nb = XBC // XBC_BLK
    n = s // tm
    last8 = s // 8 - 1

    def body(x_ref, xp_ref, xn_ref, d_ref, dn_ref, w_ref, b_ref, o_ref, dw_ref, db_ref):
        i = pl.program_id(1)

        @pl.when(i == 0)
        def _():
            dw_ref[...] = jnp.zeros_like(dw_ref)
            db_ref[...] = jnp.zeros_like(db_ref)

        prev = jnp.where(i > 0, xp_ref[...], 0.0)
        ext = jnp.concatenate([prev, x_ref[...], xn_ref[...]], axis=0)
        pre = _conv_taps(ext, w_ref, b_ref, tm)
        sg = _sigmoid(pre)
        dnext = jnp.where(i < n - 1, dn_ref[...], 0.0)
        dext = jnp.concatenate([jnp.zeros((8, XBC_BLK), f32), d_ref[...], dnext], axis=0)
        dpre = dext * (sg * (1.0 + pre * (1.0 - sg)))
        rows = tm + 16
        dx = w_ref[3:4, :] * dpre
        for j in range(3):
            dx = dx + w_ref[j:j + 1, :] * pltpu.roll(dpre, rows - (3 - j), 0)
        o_ref[...] = dx[8:8 + tm].astype(bf16)
        dcur = dpre[8:8 + tm]
        db_ref[...] += jnp.sum(dcur, axis=0, keepdims=True)
        dw_ref[3:4, :] += jnp.sum(dcur * ext[8:8 + tm], axis=0, keepdims=True)
        for j in range(3):
            dw_ref[j:j + 1, :] += jnp.sum(dcur * pltpu.roll(ext, 3 - j, 0)[8:8 + tm], axis=0, keepdims=True)

    xcol = lambda j: XBC_COL0 + j
    return pl.pallas_call(
        body, name="ssd_conv_bwd", grid=(nb, n),
        in_specs=[pl.BlockSpec((tm, XBC_BLK), lambda j, i: (i, xcol(j))),
                  pl.BlockSpec((8, XBC_BLK), lambda j, i: (jnp.maximum(i * (tm // 8) - 1, 0), xcol(j))),
                  pl.BlockSpec((8, XBC_BLK), lambda j, i: (jnp.minimum((i + 1) * (tm // 8), last8), xcol(j))),
                  pl.BlockSpec((tm, XBC_BLK), lambda j, i: (i, j)),
                  pl.BlockSpec((8, XBC_BLK), lambda j, i: (jnp.minimum((i + 1) * (tm // 8), last8), j)),
                  pl.BlockSpec((4, XBC_BLK), lambda j, i: (0, j)), pl.BlockSpec((1, XBC_BLK), lambda j, i: (0, j))],
        out_specs=[pl.BlockSpec((tm, XBC_BLK), lambda j, i: (i, j)), pl.BlockSpec((4, XBC_BLK), lambda j, i: (0, j)),
                   pl.BlockSpec((1, XBC_BLK), lambda j, i: (0, j))],
        out_shape=[_sds((s, XBC), bf16), _sds((4, XBC), f32), _sds((1, XBC), f32)],
        compiler_params=_params(("parallel", "arbitrary")),
    )(proj2, proj2, proj2, dxbc, dxbc, conv_w, conv_b)


def _ssd_consts():
    ex = np.zeros((128, SSD_W), np.float32)
    for h in range(NH):
        ex[h, h * HD:(h + 1) * HD] = 1.0
    sel = np.zeros((8, 128), np.float32)
    for h in range(NH):
        sel[h // 2, h] = 1.0
    par = np.zeros((128, 128), np.float32)
    for r in range(128):
        for h in range(NH):
            par[r, h] = 1.0 if (h % 2) == (r // 64) else 0.0
    ones_blk = np.zeros((128, 128), np.float32)
    for r in range(128):
        ones_blk[r, (r // 64) * 64:(r // 64) * 64 + 64] = 1.0
    return ex, np.ascontiguousarray(ex.T), sel, par, ones_blk


SSD_SUB = 4


def _ssd_common(rs, xbc_ref, dtr_ref, a_ref, dtb_ref, ex_ref, sel_ref, par_ref):
    xs = xbc_ref[rs, 0:SSD_W]
    dt = _softplus(dtr_ref[rs, :] + dtb_ref[...])
    adt = dt * a_ref[...]
    r_i = lax.broadcasted_iota(jnp.int32, (CHUNK, CHUNK), 0)
    c_i = lax.broadcasted_iota(jnp.int32, (CHUNK, CHUNK), 1)
    tril = (r_i >= c_i).astype(f32)
    cs = _dot01(tril, adt, exact="a")
    cs2 = jnp.concatenate([cs, cs], axis=0) * par_ref[...]
    cstp = _dot01(sel_ref[...], cs2, tb=True, exact="a")
    both = _dot01(jnp.concatenate([dt, cs], axis=0), ex_ref[...])
    return xs, dt, cs, cstp, both[0:CHUNK], both[CHUNK:]


def _pair_mask():
    l_i = lax.broadcasted_iota(jnp.int32, (CHUNK, 128), 0)
    lane = lax.broadcasted_iota(jnp.int32, (CHUNK, 128), 1)
    return l_i >= (lane % CHUNK), lane < HD


def _block_diag(xp, first):
    z = jnp.zeros_like(xp)
    return jnp.concatenate([jnp.where(first, xp, z), jnp.where(first, z, xp)], axis=0)


def _ssd_fwd(xbc, proj2, a_row, dtb_row, dsk_full):
    s = xbc.shape[0]
    nc = s // CHUNK
    ex, ext, sel, par, ones_blk = _ssd_consts()

    def one_chunk(sub, states, refs):
        xbc_ref, dtr_ref, a_ref, dtb_ref, dsk_ref, ex_ref, sel_ref, par_ref, y_ref, hs_ref = refs
        rs = slice(sub * CHUNK, (sub + 1) * CHUNK)
        xs, dt, cs, cstp, dt_full, cs_full = _ssd_common(rs, xbc_ref, dtr_ref, a_ref, dtb_ref, ex_ref, sel_ref, par_ref)
        cs_last = cs_full[CHUNK - 1:CHUNK, :]
        xdt = xs * dt_full
        causal, first = _pair_mask()
        out = []
        for g in range(NG):
            gl = slice(g * GW, (g + 1) * GW)
            bg = xbc_ref[rs, SSD_W + g * NSTATE:SSD_W + (g + 1) * NSTATE].astype(bf16)
            cg = xbc_ref[rs, SSD_W + NG * NSTATE + g * NSTATE:SSD_W + NG * NSTATE + (g + 1) * NSTATE].astype(bf16)
            cb2 = lax.dot_general(cg, jnp.concatenate([bg, bg], axis=0), (((1,), (1,)), ((), ())), preferred_element_type=f32)
            hg = states[g]
            hs_ref[sub, g] = hg
            y0 = jnp.dot(cg, hg.astype(bf16), preferred_element_type=f32)
            yoff = jnp.exp(cs_full[:, gl]) * y0
            for j in range(GW // 128):
                pair = g * (GW // 128) + j
                pl_ = slice(pair * 128, (pair + 1) * 128)
                seg = jnp.exp(jnp.where(causal, cs_full[:, pl_] - cstp[pair:pair + 1, :], -jnp.inf))
                m = (cb2 * seg).astype(bf16)
                yd = jnp.dot(m, _block_diag(xdt[:, pl_].astype(bf16), first), preferred_element_type=f32)
                y_ref[rs, pl_] = yd + yoff[:, j * 128:(j + 1) * 128] + xs[:, pl_] * dsk_ref[:, pl_]
            xdec = (xdt[:, gl] * jnp.exp(cs_last[:, gl] - cs_full[:, gl])).astype(bf16)
            st = lax.dot_general(bg, xdec, (((0,), (0,)), ((), ())), preferred_element_type=f32)
            out.append(jnp.exp(cs_last[:, gl]) * hg + st)
        return out

    def body(*refs):
        hst = refs[-1]

        @pl.when(pl.program_id(0) == 0)
        def _():
            hst[...] = jnp.zeros_like(hst)

        states = [hst[g] for g in range(NG)]
        for sub in range(SSD_SUB):
            states = one_chunk(sub, states, refs[:-1])
        for g in range(NG):
            hst[g] = states[g]

    rows = SSD_SUB * CHUNK
    const = lambda shape: pl.BlockSpec(shape, lambda c: tuple(0 for _ in shape))
    return pl.pallas_call(
        body, name="ssd_fwd", grid=(nc // SSD_SUB,),
        in_specs=[pl.BlockSpec((rows, XBC), lambda c: (c, 0)), pl.BlockSpec((rows, 128), lambda c: (c, DT_COL)),
                  const((1, 128)), const((1, 128)), const((1, SSD_W)), const((128, SSD_W)), const((8, 128)), const((128, 128))],
        out_specs=[pl.BlockSpec((rows, SSD_W), lambda c: (c, 0)), pl.BlockSpec((SSD_SUB, NG, NSTATE, GW), lambda c: (c, 0, 0, 0))],
        out_shape=[_sds((s, SSD_W), f32), _sds((nc, NG, NSTATE, GW), f32)],
        scratch_shapes=[pltpu.VMEM((NG, NSTATE, GW), f32)], compiler_params=_params(("arbitrary",)),
    )(xbc, proj2, a_row, dtb_row, dsk_full, jnp.asarray(ex), jnp.asarray(sel), jnp.asarray(par))


def _ssd_bwd(xbc, proj2, dy, hsave, a_row, dtb_row, dsk_full):
    s = xbc.shape[0]
    nc = s // CHUNK
    ex, ext, sel, par, ones_blk = _ssd_consts()

    def one_chunk(sub, dhs, refs):
        (xbc_ref, dtr_ref, dy_ref, hs_ref, a_ref, dtb_ref, dsk_ref, ex_ref, ext_ref, sel_ref, par_ref, ob_ref,
         dxbc_ref, ddtr_ref, dd_ref, da_ref, ddtb_ref, dh, a_dd, a_da, a_dtb, dcs_lane, dcs_b, dxdt) = refs
        rs = slice(sub * CHUNK, (sub + 1) * CHUNK)
        dcs_lane, dcs_b, dxdt = dcs_lane.at[sub], dcs_b.at[sub], dxdt.at[sub]
        xs, dt, cs, cstp, dt_full, cs_full = _ssd_common(rs, xbc_ref, dtr_ref, a_ref, dtb_ref, ex_ref, sel_ref, par_ref)
        cs_last = cs_full[CHUNK - 1:CHUNK, :]
        xdt = xs * dt_full
        dyv = dy_ref[rs, :]
        a_dd[...] += _fold8(dyv * xs)
        causal, first = _pair_mask()
        diag = lax.broadcasted_iota(jnp.int32, (CHUNK, 128), 0) == lax.broadcasted_iota(jnp.int32, (CHUNK, 128), 1) % CHUNK
        dh_out = []
        for g in range(NG):
            gl = slice(g * GW, (g + 1) * GW)
            bcol = slice(SSD_W + g * NSTATE, SSD_W + (g + 1) * NSTATE)
            ccol = slice(SSD_W + NG * NSTATE + g * NSTATE, SSD_W + NG * NSTATE + (g + 1) * NSTATE)
            bg = xbc_ref[rs, bcol].astype(bf16)
            cg = xbc_ref[rs, ccol].astype(bf16)
            bg2 = jnp.concatenate([bg, bg], axis=0)
            cb2 = lax.dot_general(cg, bg2, (((1,), (1,)), ((), ())), preferred_element_type=f32)
            hg = hs_ref[sub, g]
            hgb = hg.astype(bf16)
            dhg = dhs[g]
            dhgb = dhg.astype(bf16)
            eg = jnp.exp(cs_full[:, gl])
            dec = jnp.exp(cs_last[:, gl] - cs_full[:, gl])
            gam = jnp.exp(cs_last[:, gl])
            dyg = dyv[:, gl]
            xdt_g = xdt[:, gl]
            y0 = jnp.dot(cg, hgb, preferred_element_type=f32)
            dy0 = (eg * dyg).astype(bf16)
            dcm = lax.dot_general(dy0, hgb, (((1,), (1,)), ((), ())), preferred_element_type=f32)
            dh_prev = gam * dhg + lax.dot_general(cg, dy0, (((0,), (0,)), ((), ())), preferred_element_type=f32)
            dgam = jnp.sum(dhg * hg, axis=0, keepdims=True) * gam
            dxdec = jnp.dot(bg, dhgb, preferred_element_type=f32)
            dbm = lax.dot_general((xdt_g * dec).astype(bf16), dhgb, (((1,), (1,)), ((), ())), preferred_element_type=f32)
            t = dxdec * xdt_g * dec
            dcs_lane[:, gl] = dyg * eg * y0 - t
            dcs_lane[CHUNK - 1:CHUNK, gl] += jnp.sum(t, axis=0, keepdims=True) + dgam
            dxdt[:, gl] = dxdec * dec
            dcb2 = jnp.zeros((CHUNK, 128), f32)
            for j in range(GW // 128):
                pair = g * (GW // 128) + j
                pl_ = slice(pair * 128, (pair + 1) * 128)
                seg = jnp.exp(jnp.where(causal, cs_full[:, pl_] - cstp[pair:pair + 1, :], -jnp.inf))
                m = cb2 * seg
                mb = m.astype(bf16)
                rhs = _block_diag(xdt[:, pl_].astype(bf16), first)
                dyp = dyv[:, pl_].astype(bf16)
                dm = lax.dot_general(dyp, rhs, (((1,), (1,)), ((), ())), preferred_element_type=f32)
                tt = lax.dot_general(mb, dyp, (((0,), (0,)), ((), ())), preferred_element_type=f32)
                dxdt[:, pl_] += jnp.where(first, tt[0:CHUNK], tt[CHUNK:])
                dcb2 = dcb2 + dm * seg
                w = dm * m
                colsum = jnp.sum(w, axis=0, keepdims=True)
                dcs_b[:, pl_] = _dot01(w - jnp.where(diag, colsum, 0.0), ob_ref[...])
            dcb2b = dcb2.astype(bf16)
            dcm = dcm + jnp.dot(dcb2b, bg2, preferred_element_type=f32)
            t3 = lax.dot_general(dcb2b, cg, (((0,), (0,)), ((), ())), preferred_element_type=f32)
            dxbc_ref[rs, bcol] = dbm + t3[0:CHUNK] + t3[CHUNK:]
            dxbc_ref[rs, ccol] = dcm
            dh_out.append(dh_prev)
        dxdtv = dxdt[...]
        both = _dot01(jnp.concatenate([dcs_lane[...] + dcs_b[...] * (1.0 / HD), dxdtv * xs], axis=0), ext_ref[...])
        dcs = both[0:CHUNK]
        r_i = lax.broadcasted_iota(jnp.int32, (CHUNK, CHUNK), 0)
        c_i = lax.broadcasted_iota(jnp.int32, (CHUNK, CHUNK), 1)
        triu = (r_i <= c_i).astype(f32)
        da_ = _dot01(triu, dcs, exact="a")
        ddt = da_ * a_ref[...] + both[CHUNK:]
        a_da[...] += _fold8(da_ * dt)
        dxbc_ref[rs, 0:SSD_W] = dyv * dsk_ref[...] + dxdtv * dt_full
        ddtr = ddt * _sigmoid(dtr_ref[rs, :] + dtb_ref[...])
        ddtr_ref[rs, :] = ddtr
        a_dtb[...] += _fold8(ddtr)
        return dh_out

    nsteps = nc // SSD_SUB

    def body(*refs):
        dd_ref, da_ref, ddtb_ref, dh, a_dd, a_da, a_dtb = refs[14:21]
        ext_ref = refs[8]
        step = pl.program_id(0)

        @pl.when(step == 0)
        def _():
            dh[...] = jnp.zeros_like(dh)
            a_dd[...] = jnp.zeros_like(a_dd)
            a_da[...] = jnp.zeros_like(a_da)
            a_dtb[...] = jnp.zeros_like(a_dtb)

        dhs = [dh[g] for g in range(NG)]
        for sub in reversed(range(SSD_SUB)):
            dhs = one_chunk(sub, dhs, refs)
        for g in range(NG):
            dh[g] = dhs[g]

        @pl.when(step == nsteps - 1)
        def _():
            dd_ref[...] = jnp.sum(jnp.dot(a_dd[...], ext_ref[...], precision=HIGHEST, preferred_element_type=f32), axis=0, keepdims=True)
            da_ref[...] = jnp.sum(a_da[...], axis=0, keepdims=True)
            ddtb_ref[...] = jnp.sum(a_dtb[...], axis=0, keepdims=True)

    rev = lambda c: nsteps - 1 - c
    rows = SSD_SUB * CHUNK
    const = lambda shape: pl.BlockSpec(shape, lambda c: tuple(0 for _ in shape))
    return pl.pallas_call(
        body, name="ssd_bwd", grid=(nsteps,),
        in_specs=[pl.BlockSpec((rows, XBC), lambda c: (rev(c), 0)), pl.BlockSpec((rows, 128), lambda c: (rev(c), DT_COL)),
                  pl.BlockSpec((rows, SSD_W), lambda c: (rev(c), 0)), pl.BlockSpec((SSD_SUB, NG, NSTATE, GW), lambda c: (rev(c), 0, 0, 0)),
                  const((1, 128)), const((1, 128)), const((1, SSD_W)), const((128, SSD_W)), const((SSD_W, 128)),
                  const((8, 128)), const((128, 128)), const((128, 128))],
        out_specs=[pl.BlockSpec((rows, XBC), lambda c: (rev(c), 0)), pl.BlockSpec((rows, 128), lambda c: (rev(c), 0)),
                   const((1, 128)), const((1, 128)), const((1, 128))],
        out_shape=[_sds((s, XBC), f32), _sds((s, 128), f32), _sds((1, 128), f32), _sds((1, 128), f32), _sds((1, 128), f32)],
        scratch_shapes=[pltpu.VMEM((NG, NSTATE, GW), f32), pltpu.VMEM((8, SSD_W), f32), pltpu.VMEM((8, 128), f32), pltpu.VMEM((8, 128), f32)]
        + [pltpu.VMEM((SSD_SUB, CHUNK, SSD_W), f32)] * 3,
        compiler_params=_params(("arbitrary",)),
    )(xbc, proj2, dy, hsave, a_row, dtb_row, dsk_full, jnp.asarray(ex), jnp.asarray(ext), jnp.asarray(sel), jnp.asarray(par),
      jnp.asarray(ones_blk))


def _local_step(x, tgt, mods, g_mix, rel, conv_w, conv_b, dt_bias, a_log, d_skip, g_att, g_ssd, g_ffn, g_final, weights):
    s = x.shape[0]
    tm_e = 256 if s % 256 == 0 else s
    tm_m = 512 if s % 512 == 0 else s
    tm_l = 1024 if s % 1024 == 0 else s
    tk = 2048 if s % 2048 == 0 else s
    sh1, sc1, gt1, sh2, sc2, gt2 = [mods[:, i * D:(i + 1) * D] for i in range(6)]

    h1b = _norm_mod("norm_mod_1", x, g_mix, sc1, sh1, tm_e)
    win, win_b = weights.w_in(h1b)
    qkv = _mm_nn_fullk("proj_qkv", h1b, win, tm_l, 768, bf16, n=IN_A)
    proj2 = _mm_nn_fullk("proj_zxbcdt", h1b, win_b, tm_l, 896, f32)
    bias = _expand_bias(rel)
    att = _attn_fwd(qkv, bias)
    xbc = _ssd_conv(proj2, conv_w, conv_b, tm_e)
    a_row = jnp.pad(-jnp.exp(a_log), ((0, 0), (0, 128 - NH)))
    dtb_row = jnp.pad(dt_bias, ((0, 0), (0, 128 - NH)))
    dsk_full = jnp.repeat(d_skip, HD, axis=1)
    y, hsave = _ssd_fwd(xbc, proj2, a_row, dtb_row, dsk_full)
    mixcat = _mix_pre(att, y, proj2, g_att, g_ssd, tm_e)
    wout = weights.w_out(mixcat)
    mix = _mm_nn_fullk("proj_out", mixcat, wout, tm_l, 1024, f32)
    x2, h2b = _resid_norm_mod(x, gt1, mix, g_ffn, sc2, sh2, tm_e)
    wg4, wu4, wd4 = weights.ffn(h2b)
    act, sil, ud = _ffn_up(h2b, wg4, wu4, tm_m)
    ffn = _ffn_down(act, wd4, tm_l)

    dx3, dffn, loss, dg_final, dgt2 = _final_fwd_bwd(x2, ffn, gt2, g_final, tgt, tm_e)
    tok = weights.grad(("w_down",), [_grad_wdown4(act, dffn, 1024, tk)])
    dgate, dup = _ffn_dact(dffn, wd4, sil, ud, tm_l, dep=tok)
    tok = weights.grad(("w_gate", "w_up"), [_grad_cols4("grad_w_gate", h2b, dgate, 1024, tk), _grad_cols4("grad_w_up", h2b, dup, 1024, tk)])
    dh2 = _ffn_dh(dgate, dup, wg4, wu4, tm_m, dep=tok)
    dx2, dmix, dsc2, dsh2, dg_ffn, dgt1 = _norm_mod_bwd("norm_mod_bwd_2", dh2, x2, g_ffn, sc2, dx3, tm_e, mix=mix, gt=gt1)
    tok = weights.grad(("w_out",), [_mm_tn("grad_w_out", mixcat, dmix, 1024, 1024, tk, bf16).reshape(NSH, D // NSH, D)])
    dmc = _mm_nt("dmixcat", dmix, wout, tm_l, 1024, D, f32, dep=tok)
    datt, dy, dz, dg_att, dg_ssd = _mix_pre_bwd(dmc, att, y, proj2, g_att, g_ssd, tm_e)
    dq, dk, dv, gband = _attn_bwd(qkv, datt, bias)
    drel = _rel_bias_grad(gband.reshape(NH, CHUNK, BANDP))
    dxbc, ddtr, dd_row, da_row, ddtb_row = _ssd_bwd(xbc, proj2, dy, hsave, a_row, dtb_row, dsk_full)
    dxbc_raw, dconv_w, dconv_b = _ssd_conv_bwd(dxbc, proj2, conv_w, conv_b, tm_e)
    dproj = jnp.concatenate([dq, dk, dv, dz, dxbc_raw, ddtr.astype(bf16)], axis=1)
    gwin = _mm_tn("grad_w_in", h1b, dproj, 1024, 1152, tk, bf16)
    gwin4 = jnp.stack([jnp.pad(gwin[:, k * IN_SH:(k + 1) * IN_SH], ((0, 0), (0, IN_SHP - IN_SH))) for k in range(NSH)])
    tok = weights.grad(("w_in",), [gwin4])
    dh1 = _mm_nt("dh1", dproj, win, tm_l, 1024, 1920, f32, dep=tok)
    grad_x, dsc1, dsh1, dg_mix = _norm_mod_bwd("norm_mod_bwd_1", dh1, x, g_mix, sc1, dx2, tm_e)

    dmods = jnp.concatenate([dsh1, dsc1, dgt1, dsh2, dsc2, dgt2], axis=1)
    dd_skip = dd_row[:, :NH]
    da_log = da_row[:, :NH] * a_row[:, :NH]
    small = dict(g_mix=dg_mix, conv_b=dconv_b, dt_bias=ddtb_row[:, :NH], a_log=da_log, d_skip=dd_skip, g_att_out=dg_att,
                 g_ssd_out=dg_ssd, g_ffn=dg_ffn, g_final=dg_final, rel_bias=drel, conv_w=dconv_w)
    return loss[0, 0], grad_x, dmods, small


HBM = pl.BlockSpec(memory_space=pl.ANY)
VMEM = pl.BlockSpec(memory_space=pltpu.VMEM)


def _place():
    x, y, c = lax.axis_index("x"), lax.axis_index("y"), lax.axis_index("c")
    chips = [(1 - x, y), (x, 1 - y), (1 - x, 1 - y)]
    return x, y, c, chips


def _allgather8(name, payload, dep=None):
    r = payload.shape[0]
    deps = [] if dep is None else [dep]

    def body(x_ref, *rest):
        out_ref, send_sems, recv_sems, local_sem = rest[-4:]
        x, y, c, chips = _place()
        me, sibling = (x, y, c), (x, y, 1 - c)

        def slot(px, py, pc):
            return out_ref.at[4 * px + 2 * py + pc]

        def copy(k, block, to, src=None):
            return pltpu.make_async_remote_copy(
                src_ref=slot(*block) if src is None else src, dst_ref=slot(*block),
                send_sem=send_sems.at[k], recv_sem=recv_sems.at[k], device_id=to, device_id_type=MESH)

        mine = pltpu.make_async_copy(x_ref, slot(*me), local_sem)
        mine.start()
        first = [copy(0, me, sibling, src=x_ref)]
        first += [copy(1 + j, me, (*chip, c), src=x_ref) for j, chip in enumerate(chips)]
        for cp in first:
            cp.start()
        passed = [copy(4 + j, (*chip, c), sibling) for j, chip in enumerate(chips)]
        for j, chip in enumerate(chips):
            copy(1 + j, (*chip, c), me).wait_recv()
            passed[j].start()
        copy(0, sibling, me).wait_recv()
        for j, chip in enumerate(chips):
            copy(4 + j, (*chip, 1 - c), me).wait_recv()
        for cp in first + passed:
            cp.wait_send()
        mine.wait()

    return pl.pallas_call(
        body, name=name, out_shape=_sds((N_DEV, r, 128), f32), in_specs=[VMEM] * (1 + len(deps)), out_specs=VMEM,
        scratch_shapes=[pltpu.SemaphoreType.DMA((7,)), pltpu.SemaphoreType.DMA((7,)), pltpu.SemaphoreType.DMA],
    )(payload, *deps)


def _sum8(g):
    r = g.shape[1]

    def body(g_ref, o_ref):
        acc = g_ref[0]
        for i in range(1, N_DEV):
            acc = acc + g_ref[i]
        o_ref[...] = acc

    return pl.pallas_call(body, name="sum8", out_shape=_sds((r, 128), f32))(g)


SEM = pl.BlockSpec(memory_space=pltpu.SEMAPHORE)
EFFECT = pltpu.SideEffectType.DATAFLOW_SIDE_EFFECTING


def _gather_copies(ins, lands, send_sems, recv_sems):
    x, y, c, chips = _place()
    k = 2 * x + y
    starts, recvs = [], []
    for w in range(len(ins)):
        for j, (px, py) in enumerate(chips):
            def mk(dst):
                return pltpu.make_async_remote_copy(src_ref=ins[w].at[c], dst_ref=dst, send_sem=send_sems[w].at[j],
                                                    recv_sem=recv_sems[w].at[j], device_id=(px, py, c), device_id_type=MESH)
            starts.append(mk(lands[w].at[k, c]))
            recvs.append(mk(lands[w].at[2 * px + py, c]))
    return starts, recvs


def _reduce_copies(ins, lands, send_sems, recv_sems):
    x, y, c, chips = _place()
    k = 2 * x + y
    starts, recvs = [], []
    for w in range(len(ins)):
        for j, (px, py) in enumerate(chips):
            def mk(dst):
                return pltpu.make_async_remote_copy(src_ref=ins[w].at[2 * px + py], dst_ref=dst, send_sem=send_sems[w].at[j],
                                                    recv_sem=recv_sems[w].at[j], device_id=(px, py, c), device_id_type=MESH)
            starts.append(mk(lands[w].at[k]))
            recvs.append(mk(lands[w].at[2 * px + py]))
    return starts, recvs


def _split_start(name, copies, srcs, land_shapes):
    nw = len(srcs)

    def body(*refs):
        starts, _ = copies(refs[:nw], refs[nw:2 * nw], refs[2 * nw:3 * nw], refs[3 * nw:4 * nw])
        for cp in starts:
            cp.start()
        refs[6 * nw][...] = jnp.zeros((8, 128), f32)

    sems = [pltpu.SemaphoreType.DMA((3,))] * nw
    bufs = [pltpu.HBM(s.shape, bf16) for s in srcs] + [pltpu.HBM(s, bf16) for s in land_shapes]
    res = pl.pallas_call(
        body, name=name, out_shape=sems + sems + bufs + [_sds((8, 128), f32)],
        in_specs=[HBM] * (2 * nw), out_specs=[SEM] * (2 * nw) + [HBM] * (2 * nw) + [VMEM],
        input_output_aliases={i: 2 * nw + i for i in range(2 * nw)},
        compiler_params=pltpu.CompilerParams(has_side_effects=EFFECT),
    )(*[pltpu.with_memory_space_constraint(s, pltpu.HBM) for s in srcs],
      *[pltpu.with_memory_space_constraint(lax.empty(s, bf16), pltpu.HBM) for s in land_shapes])
    return res[:nw], res[nw:2 * nw], res[2 * nw:3 * nw], res[3 * nw:4 * nw], res[4 * nw]


def _split_wait(name, copies, send_sems, recv_sems, srcs, lands, after):
    nw = len(srcs)

    def body(*refs):
        starts, recvs = copies(refs[:nw], refs[nw:2 * nw], refs[2 * nw:3 * nw], refs[3 * nw:4 * nw])
        for s_, r_ in zip(starts, recvs):
            s_.wait_send()
            r_.wait_recv()

    bufs = [pltpu.HBM(s.shape, bf16) for s in srcs] + [pltpu.HBM(l.shape, bf16) for l in lands]
    res = pl.pallas_call(
        body, name=name, out_shape=bufs, in_specs=[HBM] * (2 * nw) + [SEM] * (2 * nw) + [HBM], out_specs=[HBM] * (2 * nw),
        input_output_aliases={i: i for i in range(2 * nw)},
        compiler_params=pltpu.CompilerParams(has_side_effects=EFFECT),
    )(*srcs, *lands, *send_sems, *recv_sems, after)
    return res[:nw], res[nw:]


def _gather_forward(name, shards, lands):
    nw = len(shards)

    def body(*refs):
        ins, lands_in, outs = refs[:nw], refs[nw:2 * nw], refs[2 * nw:3 * nw]
        st_a, st_b, st_c = refs[3 * nw:4 * nw], refs[4 * nw:5 * nw], refs[5 * nw:6 * nw]
        send_sems, recv_sems, load_sems, store_sems = refs[6 * nw:]
        x, y, c, chips = _place()
        k = 2 * x + y
        sibling = (x, y, 1 - c)
        ld_a = [pltpu.make_async_copy(ins[w].at[c], st_a[w], load_sems.at[w, 0]) for w in range(nw)]
        ld_b = [pltpu.make_async_copy(ins[w].at[1 - c], st_b[w], load_sems.at[w, 1]) for w in range(nw)]
        for cp in ld_a + ld_b:
            cp.start()
        st_own = []
        for w in range(nw):
            ld_a[w].wait()
            st_own.append(pltpu.make_async_copy(st_a[w], outs[w].at[k, c], store_sems.at[w, 0]))
            st_own[-1].start()
        for w in range(nw):
            ld_b[w].wait()
            st_own.append(pltpu.make_async_copy(st_b[w], outs[w].at[k, 1 - c], store_sems.at[w, 1]))
            st_own[-1].start()
        for cp in st_own:
            cp.wait()
        fwds = {}
        for j, (px, py) in enumerate(chips):
            kq = 2 * px + py
            for w in range(nw):
                slot = st_b[w] if j % 2 == 0 else st_c[w]
                if j == 2:
                    fwds[w, 0].wait_send()
                ld = pltpu.make_async_copy(lands_in[w].at[kq, c], slot, load_sems.at[w, 2 + j])
                ld.start()
                ld.wait()
                fwds[w, j] = pltpu.make_async_remote_copy(src_ref=slot, dst_ref=outs[w].at[kq, c], send_sem=send_sems.at[w, j],
                                                          recv_sem=recv_sems.at[w, j], device_id=sibling, device_id_type=MESH)
                fwds[w, j].start()
        for j, (px, py) in enumerate(chips):
            for w in range(nw):
                pltpu.make_async_remote_copy(src_ref=st_c[w], dst_ref=outs[w].at[2 * px + py, 1 - c], send_sem=send_sems.at[w, j],
                                             recv_sem=recv_sems.at[w, j], device_id=sibling, device_id_type=MESH).wait_recv()
        for w in range(nw):
            fwds[w, 1].wait_send()
            fwds[w, 2].wait_send()

    stage = [pltpu.VMEM(s.shape[1:], bf16) for s in shards]
    return pl.pallas_call(
        body, name=name, out_shape=[_sds(l.shape, bf16) for l in lands],
        in_specs=[HBM] * (2 * nw), out_specs=[HBM] * nw, input_output_aliases={nw + w: w for w in range(nw)},
        scratch_shapes=stage * 3 + [pltpu.SemaphoreType.DMA((nw, 3)), pltpu.SemaphoreType.DMA((nw, 3)), pltpu.SemaphoreType.DMA((nw, 5)),
                                    pltpu.SemaphoreType.DMA((nw, 2))],
        compiler_params=pltpu.CompilerParams(vmem_limit_bytes=VMEM_LIMIT),
    )(*shards, *lands)


def _rs_pair_exchange(name, grads):
    nw = len(grads)

    def body(*refs):
        ins, got, stage = refs[:nw], refs[nw:2 * nw], refs[2 * nw:3 * nw]
        send_sems, recv_sems, load_sems = refs[3 * nw:]
        x, y, c, _ = _place()

        def load(w, kk):
            return pltpu.make_async_copy(ins[w].at[kk, 1 - c], stage[w].at[kk % 2], load_sems.at[w, kk])

        def send(w, kk):
            return pltpu.make_async_remote_copy(src_ref=stage[w].at[kk % 2], dst_ref=got[w].at[kk], send_sem=send_sems.at[w, kk],
                                                recv_sem=recv_sems.at[w, kk], device_id=(x, y, 1 - c), device_id_type=MESH)

        for kk in range(2):
            for w in range(nw):
                load(w, kk).start()
        for kk in range(NSH):
            for w in range(nw):
                load(w, kk).wait()
                send(w, kk).start()
            if kk + 2 < NSH:
                for w in range(nw):
                    send(w, kk).wait_send()
                    load(w, kk + 2).start()
        for kk in range(NSH - 2, NSH):
            for w in range(nw):
                send(w, kk).wait_send()
        for kk in range(NSH):
            for w in range(nw):
                send(w, kk).wait_recv()

    return pl.pallas_call(
        body, name=name, out_shape=[_sds((NSH,) + g.shape[2:], bf16) for g in grads], in_specs=[HBM] * nw, out_specs=[HBM] * nw,
        scratch_shapes=[pltpu.VMEM((2,) + g.shape[2:], bf16) for g in grads]
        + [pltpu.SemaphoreType.DMA((nw, NSH)), pltpu.SemaphoreType.DMA((nw, NSH)), pltpu.SemaphoreType.DMA((nw, NSH))],
        compiler_params=pltpu.CompilerParams(vmem_limit_bytes=VMEM_LIMIT),
    )(*grads)


def _rs_pair_gather(name, halves):
    nw = len(halves)

    def body(*refs):
        ins, outs, stage = refs[:nw], refs[nw:2 * nw], refs[2 * nw:3 * nw]
        send_sems, recv_sems, local_sems, stage_sems = refs[3 * nw:]
        x, y, c, _ = _place()
        loads = [pltpu.make_async_copy(ins[w], stage[w], stage_sems.at[w]) for w in range(nw)]
        for cp in loads:
            cp.start()
        local, cps = [], []
        for w in range(nw):
            loads[w].wait()
            local.append(pltpu.make_async_copy(stage[w], outs[w].at[c], local_sems.at[w]))
            cps.append(pltpu.make_async_remote_copy(src_ref=stage[w], dst_ref=outs[w].at[c], send_sem=send_sems.at[w],
                                                    recv_sem=recv_sems.at[w], device_id=(x, y, 1 - c), device_id_type=MESH))
            local[w].start()
            cps[w].start()
        for w in range(nw):
            pltpu.make_async_remote_copy(src_ref=stage[w], dst_ref=outs[w].at[1 - c], send_sem=send_sems.at[w], recv_sem=recv_sems.at[w],
                                         device_id=(x, y, 1 - c), device_id_type=MESH).wait_recv()
        for cp in cps:
            cp.wait_send()
        for cp in local:
            cp.wait()

    return pl.pallas_call(
        body, name=name, out_shape=[_sds((2,) + h.shape, f32) for h in halves], in_specs=[HBM] * nw, out_specs=[HBM] * nw,
        scratch_shapes=[pltpu.VMEM(h.shape, f32) for h in halves]
        + [pltpu.SemaphoreType.DMA((nw,)), pltpu.SemaphoreType.DMA((nw,)), pltpu.SemaphoreType.DMA((nw,)), pltpu.SemaphoreType.DMA((nw,))],
        compiler_params=pltpu.CompilerParams(vmem_limit_bytes=VMEM_LIMIT),
    )(*halves)


def _row_tile(r, c, nbuf):
    budget = 24 * 1024 * 1024 // (2 * nbuf * 4 * c)
    t = 8
    while t * 2 <= budget and r % (t * 2) == 0:
        t *= 2
    return t


def _cast_bf16(name, a, dep=None):
    r, c = a.shape
    tr = _row_tile(r, c, 2)
    dep_specs, dep_ops = _dep_args(dep, 1)

    def body(a_ref, *rest):
        rest[-1][...] = a_ref[...].astype(bf16)

    spec = pl.BlockSpec((tr, c), lambda i: (i, 0))
    return pl.pallas_call(body, name=name, grid=(r // tr,), in_specs=[spec] + dep_specs, out_specs=spec, out_shape=_sds((r, c), bf16),
                          compiler_params=_params(("parallel",)))(a, *dep_ops)


def _w_in_columns(win4):
    tr = 256

    def body(a_ref, o_ref, ob_ref):
        for k in range(NSH):
            o_ref[:, IN_SH * k:IN_SH * (k + 1)] = a_ref[k][:, :IN_SH]
        o_ref[:, IN_COLS:] = jnp.zeros((tr, IN_P - IN_COLS), bf16)
        ob_ref[...] = o_ref[:, IN_A:]

    return pl.pallas_call(
        body, name="w_in_columns", grid=(D // tr,), in_specs=[pl.BlockSpec((NSH, tr, IN_SHP), lambda i: (0, i, 0))],
        out_specs=[pl.BlockSpec((tr, IN_P), lambda i: (i, 0)), pl.BlockSpec((tr, IN_B), lambda i: (i, 0))],
        out_shape=[_sds((D, IN_P), bf16), _sds((D, IN_B), bf16)], compiler_params=_params(("parallel",)))(win4)


def _pair_sum(name, core, grads, got):
    _, _, rh, c = grads.shape
    tr = _row_tile(rh, c, 2)

    def body(c_ref, a_ref, b_ref, o_ref):
        o_ref[...] = (a_ref[...].astype(f32) + b_ref[...].astype(f32)).astype(bf16)

    spec = pl.BlockSpec((None, tr, c), lambda k, i, c_ref: (k, i, 0))
    return pl.pallas_call(
        body, name=name, out_shape=_sds((NSH, rh, c), bf16),
        grid_spec=pltpu.PrefetchScalarGridSpec(
            num_scalar_prefetch=1, grid=(NSH, rh // tr),
            in_specs=[pl.BlockSpec((None, None, tr, c), lambda k, i, c_ref: (k, c_ref[0], i, 0)), spec], out_specs=spec),
        compiler_params=_params(("parallel", "parallel")))(core, grads, got)


def _chip_sum(name, chip, sums, lands):
    _, rh, c = sums.shape
    tr = _row_tile(rh, c, 4)

    def body(k_ref, own_ref, l_ref, o_ref):
        own = own_ref[...].astype(f32)
        acc = None
        for j in range(NSH):
            term = jnp.where(k_ref[0] == j, own, l_ref[j].astype(f32))
            acc = term if acc is None else acc + term
        o_ref[...] = acc

    return pl.pallas_call(
        body, name=name, out_shape=_sds((rh, c), f32),
        grid_spec=pltpu.PrefetchScalarGridSpec(
            num_scalar_prefetch=1, grid=(rh // tr,),
            in_specs=[pl.BlockSpec((None, tr, c), lambda i, k_ref: (k_ref[0], i, 0)), pl.BlockSpec((NSH, tr, c), lambda i, k_ref: (0, i, 0))],
            out_specs=pl.BlockSpec((tr, c), lambda i, k_ref: (i, 0))),
        compiler_params=_params(("parallel",)))(chip, sums, lands)


def _mods_part(cond16, w_ada, b_part):
    n = w_ada.shape[1]
    tn = 512

    def body(c_ref, w_ref, b_ref, o_ref):
        cv = c_ref[...]
        o_ref[...] = _dot(cv * _sigmoid(cv), w_ref[...]) + b_ref[...]

    return pl.pallas_call(
        body, name="mods_part", grid=(n // tn,),
        in_specs=[pl.BlockSpec((16, D), lambda j: (0, 0)), pl.BlockSpec((D, tn), lambda j: (0, j)), pl.BlockSpec((1, tn), lambda j: (0, j))],
        out_specs=pl.BlockSpec((16, tn), lambda j: (0, j)), out_shape=_sds((16, n), f32), compiler_params=_params(("parallel",)),
    )(cond16, w_ada, b_part)


def _grad_w_ada(cond16, dm16):
    n = dm16.shape[1]
    tr = 256

    def body(c_ref, d_ref, o_ref):
        cv = c_ref[...]
        o_ref[...] = _dot(cv * _sigmoid(cv), d_ref[...], ta=True)

    return pl.pallas_call(
        body, name="grad_w_ada", grid=(D // tr,),
        in_specs=[pl.BlockSpec((16, tr), lambda i: (0, i)), pl.BlockSpec((16, n), lambda i: (0, 0))],
        out_specs=pl.BlockSpec((tr, n), lambda i: (i, 0)), out_shape=_sds((D, n), f32), compiler_params=_params(("parallel",)),
    )(cond16, dm16)


def _adamw(name, w, g, m, v):
    r, c = w.shape
    tr = _row_tile(r, c, 7)

    def body(w_ref, g_ref, m_ref, v_ref, d_ref, nm_ref, nv_ref):
        gv = g_ref[...]
        nm = ADAM_B1 * m_ref[...] + (1.0 - ADAM_B1) * gv
        nv = ADAM_B2 * v_ref[...] + (1.0 - ADAM_B2) * (gv * gv)
        nm_ref[...] = nm
        nv_ref[...] = nv
        m_hat = nm / (1.0 - ADAM_B1 ** ADAM_STEP)
        v_hat = nv / (1.0 - ADAM_B2 ** ADAM_STEP)
        d_ref[...] = -ADAM_LR * (m_hat / (jnp.sqrt(v_hat) + ADAM_EPS) + ADAM_WD * w_ref[...])

    spec = pl.BlockSpec((tr, c), lambda i: (i, 0))
    return pl.pallas_call(body, name=name, grid=(r // tr,), in_specs=[spec] * 4, out_specs=[spec] * 3, out_shape=[_sds((r, c), f32)] * 3,
                          compiler_params=_params(("parallel",)))(w, g, m, v)


def _pack(parts, rows):
    flat = []
    for p in parts:
        p = p.reshape(-1)
        flat.append(jnp.pad(p, (0, (-p.shape[0]) % 128)))
    v = jnp.concatenate(flat)
    return jnp.pad(v, (0, rows * 128 - v.shape[0])).reshape(rows, 128)


def _unpack(packed, sizes):
    lead = packed.shape[:-2]
    flat = packed.reshape(lead + (-1,))
    out, off = [], 0
    for n in sizes:
        out.append(flat[..., off:off + n])
        off += n + (-n) % 128
    return out


BIG = ("w_in", "w_out", "w_gate", "w_up", "w_down")
SMALL = ("b_ada", "g_mix", "conv_b", "dt_bias", "a_log", "d_skip", "g_att_out", "g_ssd_out", "g_ffn", "g_final", "rel_bias", "conv_w")
ORDER = ("w_ada", "b_ada", "g_mix", "w_in", "rel_bias", "conv_w", "conv_b", "dt_bias", "a_log", "d_skip", "g_att_out", "g_ssd_out",
         "w_out", "g_ffn", "w_gate", "w_up", "w_down", "g_final")
REL_SH = N_REL // NSH
CONVW_SH = XBC // NSH
ADA_SH = 6 * D // NSH


class _Exchange:
    def __init__(self, core, chip):
        self.core, self.chip = core, chip
        self.gathered = {}
        self.pending = []

    def gather(self, names, shards):
        ssem, rsem, thru, lands, token = _split_start("gather_start_" + "_".join(names), _gather_copies, shards,
                                                      [(NSH,) + s.shape for s in shards])
        self.gathered.update({n: (ssem[i], rsem[i], thru[i], lands[i]) for i, n in enumerate(names)})
        return token

    def _whole(self, names, after):
        ssem, rsem, thru, lands = zip(*[self.gathered[n] for n in names])
        tag = "_".join(names)
        thru, lands = _split_wait("gather_wait_" + tag, _gather_copies, ssem, rsem, thru, lands, after)
        return _gather_forward("gather_forward_" + tag, thru, lands)

    def w_in(self, after):
        (win4,) = self._whole(("w_in",), after)
        return _w_in_columns(win4.reshape(NSH, D, IN_SHP))

    def w_out(self, after):
        (wout4,) = self._whole(("w_out",), after)
        return wout4.reshape(D, D)

    def ffn(self, after):
        wg4, wu4, wd4 = self._whole(("w_gate", "w_up", "w_down"), after)
        return wg4.reshape(NSH, D, FSH), wu4.reshape(NSH, D, FSH), wd4.reshape(NSH, FSH, D)

    def grad(self, names, grads):
        tag = "_".join(names)
        stacked = [g.reshape(NSH, 2, g.shape[1] // 2, g.shape[2]) for g in grads]
        got = _rs_pair_exchange("rs_pair_exchange_" + tag, stacked)
        sums = [_pair_sum("pair_sum_" + n, self.core, o, g) for n, o, g in zip(names, stacked, got)]
        self.pending.append((names, _split_start("rs_start_" + tag, _reduce_copies, sums, [s.shape for s in sums])))
        return self.pending[-1][1][4]

    def finish(self, after):
        grads = {}
        for names, (ssem, rsem, sums, lands, _) in self.pending:
            tag = "_".join(names)
            sums, lands = _split_wait("rs_wait_" + tag, _reduce_copies, ssem, rsem, sums, lands, after)
            halves = [_chip_sum("chip_sum_" + n, self.chip, sm, ld) for n, sm, ld in zip(names, sums, lands)]
            for n, f in zip(names, _rs_pair_gather("rs_pair_gather_" + tag, halves)):
                grads[n] = f.reshape(2 * f.shape[1], f.shape[2])
        return grads


def kernel(x, c, w_ada, b_ada, g_mix, w_in, rel_bias, conv_w, conv_b, dt_bias, a_log, d_skip, g_att_out, g_ssd_out, w_out, g_ffn, w_gate, w_up, w_down, g_final, loss_target, m_w_ada, m_b_ada, m_g_mix, m_w_in, m_rel_bias, m_conv_w, m_conv_b, m_dt_bias, m_a_log, m_d_skip, m_g_att_out, m_g_ssd_out, m_w_out, m_g_ffn, m_w_gate, m_w_up, m_w_down, m_g_final, v_w_ada, v_b_ada, v_g_mix, v_w_in, v_rel_bias, v_conv_w, v_conv_b, v_dt_bias, v_a_log, v_d_skip, v_g_att_out, v_g_ssd_out, v_w_out, v_g_ffn, v_w_gate, v_w_up, v_w_down, v_g_final):
    args = dict(locals())
    w = {n: args[n] for n in ORDER}
    m = {n: args["m_" + n] for n in ORDER}
    v = {n: args["v_" + n] for n in ORDER}
    ix, iy, ic = lax.axis_index("x"), lax.axis_index("y"), lax.axis_index("c")
    chip = 2 * ix + iy
    dev = 2 * chip + ic
    s = x.shape[1]

    g1 = _allgather8("gather_inputs", _pack([c[0], rel_bias[0], conv_w[0]], 40))
    c_all, rel_sh, convw_sh = _unpack(g1, [D, NH * REL_SH, 4 * CONVW_SH])
    rel_full = jnp.concatenate([rel_sh[2 * k].reshape(NH, REL_SH) for k in range(NSH)], axis=1)
    convw_full = jnp.concatenate([convw_sh[2 * k].reshape(4, CONVW_SH) for k in range(NSH)], axis=1)
    cond16 = jnp.pad(c_all, ((0, 8), (0, 0)))
    b_part = lax.dynamic_slice_in_dim(b_ada, chip * ADA_SH, ADA_SH, axis=1)
    mods_part = _mods_part(cond16, w_ada[0], b_part)[:N_DEV]
    g2 = _allgather8("gather_mods", mods_part.reshape(N_DEV * ADA_SH // 128, 128))
    mods_all = jnp.concatenate([g2[2 * k].reshape(N_DEV, ADA_SH) for k in range(NSH)], axis=1)
    mods = lax.dynamic_slice_in_dim(mods_all, dev, 1, axis=0)

    exchange = _Exchange(jnp.reshape(ic, (1,)).astype(jnp.int32), jnp.reshape(chip, (1,)).astype(jnp.int32))
    shard_in = _cast_bf16("cast_w_in", jnp.pad(w_in[0], ((0, 0), (0, IN_SHP - IN_SH))), dep=g2[0, :8]).reshape(2, D // 2, IN_SHP)
    tok = exchange.gather(("w_in",), [shard_in])
    tok = exchange.gather(("w_out", "w_gate", "w_up", "w_down"), [
        _cast_bf16("cast_w_out", w_out[0], dep=tok).reshape(2, D // NSH // 2, D),
        _cast_bf16("cast_w_gate", w_gate[0], dep=tok).reshape(2, D // 2, FSH),
        _cast_bf16("cast_w_up", w_up[0], dep=tok).reshape(2, D // 2, FSH),
        _cast_bf16("cast_w_down", w_down[0], dep=tok).reshape(2, FSH // 2, D)])
    mods = mods + tok[:1, :1]

    loss, grad_x, dmods, small = _local_step(
        x[0], loss_target[0], mods, g_mix, rel_full, convw_full, conv_b, dt_bias, a_log, d_skip, g_att_out, g_ssd_out, g_ffn,
        g_final[None, :], exchange)

    small_names = ("g_mix", "conv_b", "dt_bias", "a_log", "d_skip", "g_att_out", "g_ssd_out", "g_ffn", "g_final", "rel_bias", "conv_w")
    g3 = _allgather8("gather_small_grads", _pack([dmods] + [small[n] for n in small_names], 264))
    sizes = [6 * D] + [int(np.prod(small[n].shape)) for n in small_names]
    dmods_all = _unpack(g3, sizes)[0]
    summed = _unpack(_sum8(g3), sizes)
    grads = {"b_ada": summed[0].reshape(1, 6 * D)}
    for n, val in zip(small_names, summed[1:]):
        grads[n] = val.reshape(small[n].shape)
    grads["rel_bias"] = lax.dynamic_slice_in_dim(grads["rel_bias"], chip * REL_SH, REL_SH, axis=1)
    grads["conv_w"] = lax.dynamic_slice_in_dim(grads["conv_w"], chip * CONVW_SH, CONVW_SH, axis=1)
    grads["g_final"] = grads["g_final"].reshape(D)
    dm16 = jnp.pad(lax.dynamic_slice_in_dim(dmods_all, chip * ADA_SH, ADA_SH, axis=1), ((0, 8), (0, 0)))
    grads["w_ada"] = _grad_w_ada(cond16, dm16)

    delta, new_m, new_v = {}, {}, {}
    delta["w_ada"], new_m["w_ada"], new_v["w_ada"] = _adamw("adamw_w_ada", w_ada[0], grads["w_ada"], m_w_ada[0], v_w_ada[0])
    grads.update(exchange.finish(grad_x))
    grads["w_in"] = grads["w_in"][:, :IN_SH]
    for n in BIG:
        delta[n], new_m[n], new_v[n] = _adamw("adamw_" + n, w[n][0], grads[n], m[n][0], v[n][0])
    sw = _pack([w[n] for n in SMALL], 200)
    sg = _pack([grads[n] for n in SMALL], 200)
    sm = _pack([m[n] for n in SMALL], 200)
    sv = _pack([v[n] for n in SMALL], 200)
    ssz = [int(np.prod(w[n].shape)) for n in SMALL]
    for dst, packed in zip((delta, new_m, new_v), _adamw("adamw_small", sw, sg, sm, sv)):
        for n, val in zip(SMALL, _unpack(packed, ssz)):
            dst[n] = val

    def shaped(d, n):
        return d[n].reshape(w[n].shape)

    total = lax.psum(loss, ("x", "y", "c"))
    return (total, grad_x[None], *[shaped(grads, n) for n in ORDER], *[shaped(delta, n) for n in ORDER],
            *[shaped(new_m, n) for n in ORDER], *[shaped(new_v, n) for n in ORDER])
```

```python
import functools

import numpy as np
import jax
import jax.numpy as jnp
from jax import lax
from jax.experimental import pallas as pl
from jax.experimental.pallas import tpu as pltpu

f32 = jnp.float32
bf16 = jnp.bfloat16
HIGHEST = lax.Precision.HIGHEST
MESH = pl.DeviceIdType.MESH

D = 2048
CHUNK = 64
LEFT = 8
BAND = (LEFT + 1) * CHUNK
BANDP = 640
PADK = LEFT * CHUNK
NH = 16
HD = 64
ATT_W = NH * HD
SSD_W = 1024
NG = 2
NSTATE = 128
GW = SSD_W // NG
XBC = SSD_W + 2 * NG * NSTATE
N_REL = 320
REL_CLIP = 256
FFN = 5632
NSH = 4
FSH = FFN // NSH
IN_COLS = 5648
IN_SH = IN_COLS // NSH
IN_SHP = 1536
IN_A = 3 * ATT_W
IN_B = 2688
IN_P = IN_A + IN_B
EPS = 1e-6
N_DEV = 8

ADAM_LR = 0.001
ADAM_B1 = 0.9
ADAM_B2 = 0.999
ADAM_EPS = 1e-08
ADAM_WD = 0.01
ADAM_STEP = 10

VMEM_LIMIT = 56 * 1024 * 1024


def _params(sem):
    return pltpu.CompilerParams(dimension_semantics=sem, vmem_limit_bytes=VMEM_LIMIT)


def _sds(shape, dtype):
    return jax.ShapeDtypeStruct(shape, dtype)


def _fold8(v):
    r, w = v.shape
    return jnp.sum(v.reshape(r // 8, 8, w), axis=0)


STRIP = 16


def _strips(tm, fn):
    def step(j, carry):
        fn(pl.ds(pl.multiple_of(j * STRIP, STRIP), STRIP))
        return carry
    lax.fori_loop(0, tm // STRIP, step, 0, unroll=4)


def _sigmoid(v):
    return 1.0 / (1.0 + jnp.exp(-v))


def _softplus(v):
    return jnp.maximum(v, 0.0) + jnp.log(1.0 + jnp.exp(-jnp.abs(v)))


def _dot(a, b, ta=False, tb=False):
    dn = (((0 if ta else 1,), (1 if tb else 0,)), ((), ()))
    return lax.dot_general(a.astype(bf16), b.astype(bf16), dn, preferred_element_type=f32)


def _dep_args(dep, ngrid):
    if dep is None:
        return [], []
    return [pl.BlockSpec((8, 128), lambda *_: (0, 0))], [dep]


def _dot01(a, b, ta=False, tb=False, exact="b"):
    dn = (((0 if ta else 1,), (1 if tb else 0,)), ((), ()))
    x = a if exact == "b" else b
    hi = x.astype(bf16)
    r = x - hi.astype(f32)
    mid = r.astype(bf16)
    lo = (r - mid.astype(f32)).astype(bf16)
    if exact == "b":
        m = b.astype(bf16)
        return sum(lax.dot_general(p, m, dn, preferred_element_type=f32) for p in (hi, mid, lo))
    m = a.astype(bf16)
    return sum(lax.dot_general(m, p, dn, preferred_element_type=f32) for p in (hi, mid, lo))


def _matmul(name, a, b, *, grid, a_spec, b_spec, o_spec, o_shape, o_dtype, acc_shape, ta=False, tb=False, dep=None):
    nk = grid[2]
    dep_specs, dep_ops = _dep_args(dep, 3)

    def body(a_ref, b_ref, *rest):
        o_ref, acc_ref = rest[-2:]
        p = _dot(a_ref[...], b_ref[...], ta, tb)
        if nk == 1:
            o_ref[...] = p.astype(o_ref.dtype)
        else:
            k = pl.program_id(2)

            @pl.when(k == 0)
            def _():
                acc_ref[...] = p

            @pl.when(jnp.logical_and(k > 0, k < nk - 1))
            def _():
                acc_ref[...] += p

            @pl.when(k == nk - 1)
            def _():
                o_ref[...] = (acc_ref[...] + p).astype(o_ref.dtype)

    return pl.pallas_call(
        body, name=name, grid=grid, in_specs=[a_spec, b_spec] + dep_specs, out_specs=o_spec,
        out_shape=_sds(o_shape, o_dtype), scratch_shapes=[pltpu.VMEM(acc_shape if nk > 1 else (8, 128), f32)],
        compiler_params=_params(("parallel", "parallel", "arbitrary")),
    )(a, b, *dep_ops)


def _mm_nn_fullk(name, a, b, tm, tn, o_dtype, n=None):
    m, k = a.shape
    n = b.shape[1] if n is None else n
    return _matmul(name, a, b, grid=(m // tm, n // tn, 1),
                   a_spec=pl.BlockSpec((tm, k), lambda i, j, kk: (i, 0)),
                   b_spec=pl.BlockSpec((k, tn), lambda i, j, kk: (0, j)),
                   o_spec=pl.BlockSpec((tm, tn), lambda i, j, kk: (i, j)),
                   o_shape=(m, n), o_dtype=o_dtype, acc_shape=(tm, tn))


def _mm_nt(name, a, b, tm, tn, tk, o_dtype, dep=None):
    m, k = a.shape
    n = b.shape[0]
    return _matmul(name, a, b, grid=(m // tm, n // tn, k // tk), tb=True, dep=dep,
                   a_spec=pl.BlockSpec((tm, tk), lambda i, j, kk: (i, kk)),
                   b_spec=pl.BlockSpec((tn, tk), lambda i, j, kk: (j, kk)),
                   o_spec=pl.BlockSpec((tm, tn), lambda i, j, kk: (i, j)),
                   o_shape=(m, n), o_dtype=o_dtype, acc_shape=(tm, tn))


def _mm_tn(name, a, b, tm, tn, tk, o_dtype):
    k, m = a.shape
    n = b.shape[1]
    return _matmul(name, a, b, grid=(m // tm, n // tn, k // tk), ta=True,
                   a_spec=pl.BlockSpec((tk, tm), lambda i, j, kk: (kk, i)),
                   b_spec=pl.BlockSpec((tk, tn), lambda i, j, kk: (kk, j)),
                   o_spec=pl.BlockSpec((tm, tn), lambda i, j, kk: (i, j)),
                   o_shape=(m, n), o_dtype=o_dtype, acc_shape=(tm, tn))


FSH_PARTS = (slice(0, 640), slice(640, FSH))


def _ffn_up(h2b, wg4, wu4, tm):
    s = h2b.shape[0]

    def body(h_ref, wg_ref, wu_ref, a_ref, s_ref, ud_ref):
        h = h_ref[...]
        for cols in FSH_PARTS:
            g = _dot(h, wg_ref[:, cols])
            u = _dot(h, wu_ref[:, cols])
            sg = _sigmoid(g)
            sil = g * sg
            a_ref[:, cols] = (sil * u).astype(bf16)
            s_ref[:, cols] = sil.astype(bf16)
            ud_ref[:, cols] = (u * (sg * (1.0 + g * (1.0 - sg)))).astype(bf16)

    wspec = pl.BlockSpec((None, D, FSH), lambda k, i: (k, 0, 0))
    ospec = pl.BlockSpec((tm, FSH), lambda k, i: (i, k))
    return pl.pallas_call(
        body, name="ffn_up", grid=(NSH, s // tm),
        in_specs=[pl.BlockSpec((tm, D), lambda k, i: (i, 0)), wspec, wspec],
        out_specs=[ospec, ospec, ospec], out_shape=[_sds((s, FFN), bf16)] * 3,
        compiler_params=_params(("parallel", "parallel")),
    )(h2b, wg4, wu4)


def _ffn_down(act, wd4, tm):
    s = act.shape[0]
    return _matmul("ffn_down", act, wd4, grid=(s // tm, 1, NSH),
                   a_spec=pl.BlockSpec((tm, FSH), lambda i, j, k: (i, k)),
                   b_spec=pl.BlockSpec((None, FSH, D), lambda i, j, k: (k, 0, 0)),
                   o_spec=pl.BlockSpec((tm, D), lambda i, j, k: (i, 0)),
                   o_shape=(s, D), o_dtype=f32, acc_shape=(tm, D))


def _ffn_dact(dffn, wd4, sil, ud, tm, dep=None):
    s = dffn.shape[0]
    dep_specs, dep_ops = _dep_args(dep, 2)

    def body(d_ref, w_ref, s_ref, ud_ref, *rest):
        dg_ref, du_ref = rest[-2:]
        d = d_ref[...]
        for cols in FSH_PARTS:
            dact = _dot(d, w_ref[cols, :], tb=True)
            dg_ref[:, cols] = (dact * ud_ref[:, cols].astype(f32)).astype(bf16)
            du_ref[:, cols] = (dact * s_ref[:, cols].astype(f32)).astype(bf16)

    blk = pl.BlockSpec((tm, FSH), lambda k, i: (i, k))
    return pl.pallas_call(
        body, name="ffn_dact", grid=(NSH, s // tm),
        in_specs=[pl.BlockSpec((tm, D), lambda k, i: (i, 0)), pl.BlockSpec((None, FSH, D), lambda k, i: (k, 0, 0)), blk, blk] + dep_specs,
        out_specs=[blk, blk], out_shape=[_sds((s, FFN), bf16), _sds((s, FFN), bf16)],
        compiler_params=_params(("parallel", "parallel")),
    )(dffn, wd4, sil, ud, *dep_ops)


def _ffn_dh(dgate, dup, wg4, wu4, tm, dep=None):
    s = dgate.shape[0]
    dep_specs, dep_ops = _dep_args(dep, 2)

    def body(dg_ref, du_ref, wg_ref, wu_ref, *rest):
        o_ref, acc_ref = rest[-2:]
        k = pl.program_id(1)
        p = _dot(dg_ref[...], wg_ref[...], tb=True) + _dot(du_ref[...], wu_ref[...], tb=True)

        @pl.when(k == 0)
        def _():
            acc_ref[...] = p

        @pl.when(jnp.logical_and(k > 0, k < NSH - 1))
        def _():
            acc_ref[...] += p

        @pl.when(k == NSH - 1)
        def _():
            o_ref[...] = acc_ref[...] + p

    aspec = pl.BlockSpec((tm, FSH), lambda i, k: (i, k))
    wspec = pl.BlockSpec((None, D, FSH), lambda i, k: (k, 0, 0))
    return pl.pallas_call(
        body, name="ffn_dh", grid=(s // tm, NSH), in_specs=[aspec, aspec, wspec, wspec] + dep_specs,
        out_specs=pl.BlockSpec((tm, D), lambda i, k: (i, 0)), out_shape=_sds((s, D), f32),
        scratch_shapes=[pltpu.VMEM((tm, D), f32)], compiler_params=_params(("parallel", "arbitrary")),
    )(dgate, dup, wg4, wu4, *dep_ops)


def _grad_cols4(name, h, dy, tm, tk):
    s = h.shape[0]
    return _matmul(name, h, dy, grid=(NSH, D // tm, s // tk), ta=True,
                   a_spec=pl.BlockSpec((tk, tm), lambda k, i, kk: (kk, i)),
                   b_spec=pl.BlockSpec((tk, FSH), lambda k, i, kk: (kk, k)),
                   o_spec=pl.BlockSpec((None, tm, FSH), lambda k, i, kk: (k, i, 0)),
                   o_shape=(NSH, D, FSH), o_dtype=bf16, acc_shape=(tm, FSH))


def _grad_wdown4(act, dffn, tn, tk):
    s = act.shape[0]
    return _matmul("grad_w_down", act, dffn, grid=(NSH, D // tn, s // tk), ta=True,
                   a_spec=pl.BlockSpec((tk, FSH), lambda k, j, kk: (kk, k)),
                   b_spec=pl.BlockSpec((tk, tn), lambda k, j, kk: (kk, j)),
                   o_spec=pl.BlockSpec((None, FSH, tn), lambda k, j, kk: (k, 0, j)),
                   o_shape=(NSH, FSH, D), o_dtype=bf16, acc_shape=(FSH, tn))


def _row_spec(w):
    return pl.BlockSpec((1, w), lambda i: (0, 0))


def _tile_spec(tm, w, col=0):
    return pl.BlockSpec((tm, w), lambda i: (i, col))


def _norm_mod(name, x, g, sc, sh, tm):
    s = x.shape[0]

    def body(x_ref, g_ref, sc_ref, sh_ref, o_ref):
        def strip(rows):
            xv = x_ref[rows, :]
            r = lax.rsqrt(jnp.mean(xv * xv, axis=-1, keepdims=True) + EPS)
            o_ref[rows, :] = (xv * r * g_ref[...] * (1.0 + sc_ref[...]) + sh_ref[...]).astype(bf16)

        _strips(tm, strip)

    return pl.pallas_call(
        body, name=name, grid=(s // tm,), in_specs=[_tile_spec(tm, D), _row_spec(D), _row_spec(D), _row_spec(D)],
        out_specs=_tile_spec(tm, D), out_shape=_sds((s, D), bf16), compiler_params=_params(("parallel",)),
    )(x, g, sc, sh)


def _resid_norm_mod(x, gt, mix, g, sc, sh, tm):
    s = x.shape[0]

    def body(x_ref, gt_ref, m_ref, g_ref, sc_ref, sh_ref, x2_ref, h_ref):
        def strip(rows):
            xv = x_ref[rows, :] + gt_ref[...] * m_ref[rows, :]
            x2_ref[rows, :] = xv
            r = lax.rsqrt(jnp.mean(xv * xv, axis=-1, keepdims=True) + EPS)
            h_ref[rows, :] = (xv * r * g_ref[...] * (1.0 + sc_ref[...]) + sh_ref[...]).astype(bf16)

        _strips(tm, strip)

    return pl.pallas_call(
        body, name="resid_norm_mod", grid=(s // tm,),
        in_specs=[_tile_spec(tm, D), _row_spec(D), _tile_spec(tm, D), _row_spec(D), _row_spec(D), _row_spec(D)],
        out_specs=[_tile_spec(tm, D), _tile_spec(tm, D)], out_shape=[_sds((s, D), f32), _sds((s, D), bf16)],
        compiler_params=_params(("parallel",)),
    )(x, gt, mix, g, sc, sh)


def _final_fwd_bwd(x2, ffn, gt2, g, tgt, tm):
    s = x2.shape[0]
    n = s // tm

    def body(x_ref, f_ref, gt_ref, g_ref, t_ref, dx_ref, df_ref, loss_ref, dg_ref, dgt_ref, a_loss, a_dg, a_dgt):
        i = pl.program_id(0)

        @pl.when(i == 0)
        def _():
            a_loss[...] = jnp.zeros_like(a_loss)
            a_dg[...] = jnp.zeros_like(a_dg)
            a_dgt[...] = jnp.zeros_like(a_dgt)

        def strip(rows):
            fv = f_ref[rows, :]
            gt = gt_ref[...]
            gv = g_ref[...]
            xv = x_ref[rows, :] + gt * fv
            r = lax.rsqrt(jnp.mean(xv * xv, axis=-1, keepdims=True) + EPS)
            xh = xv * r
            e = xh * gv - t_ref[rows, :]
            a_loss[...] += _fold8(e * e)
            dy = e * (1.0 / D)
            a_dg[...] += _fold8(dy * xh)
            t = dy * gv
            dx = r * (t - xh * jnp.mean(t * xh, axis=-1, keepdims=True))
            dx_ref[rows, :] = dx
            a_dgt[...] += _fold8(dx * fv)
            df_ref[rows, :] = (dx * gt).astype(bf16)

        _strips(tm, strip)

        @pl.when(i == n - 1)
        def _():
            tot = jnp.sum(jnp.sum(a_loss[...], axis=0, keepdims=True), axis=1, keepdims=True) * (0.5 / D)
            loss_ref[...] = jnp.broadcast_to(tot, (1, 128))
            dg_ref[...] = jnp.sum(a_dg[...], axis=0, keepdims=True)
            dgt_ref[...] = jnp.sum(a_dgt[...], axis=0, keepdims=True)

    return pl.pallas_call(
        body, name="final_fwd_bwd", grid=(n,),
        in_specs=[_tile_spec(tm, D), _tile_spec(tm, D), _row_spec(D), _row_spec(D), _tile_spec(tm, D)],
        out_specs=[_tile_spec(tm, D), _tile_spec(tm, D), _row_spec(128), _row_spec(D), _row_spec(D)],
        out_shape=[_sds((s, D), f32), _sds((s, D), bf16), _sds((1, 128), f32), _sds((1, D), f32), _sds((1, D), f32)],
        scratch_shapes=[pltpu.VMEM((8, D), f32)] * 3, compiler_params=_params(("arbitrary",)),
    )(x2, ffn, gt2, g, tgt)


def _norm_mod_bwd(name, dh, xin, g, sc, dres, tm, mix=None, gt=None):
    s = dh.shape[0]
    n = s // tm
    with_mix = mix is not None

    def body(*refs):
        if with_mix:
            dh_ref, x_ref, g_ref, sc_ref, dr_ref, m_ref, gt_ref, dx_ref, dm_ref, dsc_ref, dsh_ref, dg_ref, dgt_ref, a_sc, a_sh, a_g, a_gt = refs
        else:
            dh_ref, x_ref, g_ref, sc_ref, dr_ref, dx_ref, dsc_ref, dsh_ref, dg_ref, a_sc, a_sh, a_g = refs
        i = pl.program_id(0)

        @pl.when(i == 0)
        def _():
            a_sc[...] = jnp.zeros_like(a_sc)
            a_sh[...] = jnp.zeros_like(a_sh)
            a_g[...] = jnp.zeros_like(a_g)
            if with_mix:
                a_gt[...] = jnp.zeros_like(a_gt)

        def strip(rows):
            dh = dh_ref[rows, :]
            xv = x_ref[rows, :]
            gv = g_ref[...]
            r = lax.rsqrt(jnp.mean(xv * xv, axis=-1, keepdims=True) + EPS)
            xh = xv * r
            a_sc[...] += _fold8(dh * xh * gv)
            a_sh[...] += _fold8(dh)
            dn = dh * (1.0 + sc_ref[...])
            a_g[...] += _fold8(dn * xh)
            t = dn * gv
            dx = dr_ref[rows, :] + r * (t - xh * jnp.mean(t * xh, axis=-1, keepdims=True))
            dx_ref[rows, :] = dx
            if with_mix:
                a_gt[...] += _fold8(dx * m_ref[rows, :])
                dm_ref[rows, :] = (dx * gt_ref[...]).astype(bf16)

        _strips(tm, strip)

        @pl.when(i == n - 1)
        def _():
            dsc_ref[...] = jnp.sum(a_sc[...], axis=0, keepdims=True)
            dsh_ref[...] = jnp.sum(a_sh[...], axis=0, keepdims=True)
            dg_ref[...] = jnp.sum(a_g[...], axis=0, keepdims=True)
            if with_mix:
                dgt_ref[...] = jnp.sum(a_gt[...], axis=0, keepdims=True)

    tile, row = _tile_spec(tm, D), _row_spec(D)
    if with_mix:
        ins, args = [tile, tile, row, row, tile, tile, row], (dh, xin, g, sc, dres, mix, gt)
        outs = [tile, tile, row, row, row, row]
        shapes = [_sds((s, D), f32), _sds((s, D), bf16)] + [_sds((1, D), f32)] * 4
        nacc = 4
    else:
        ins, args = [tile, tile, row, row, tile], (dh, xin, g, sc, dres)
        outs = [tile, row, row, row]
        shapes = [_sds((s, D), f32)] + [_sds((1, D), f32)] * 3
        nacc = 3
    return pl.pallas_call(
        body, name=name, grid=(n,), in_specs=ins, out_specs=outs, out_shape=shapes,
        scratch_shapes=[pltpu.VMEM((8, D), f32)] * nacc, compiler_params=_params(("arbitrary",)),
    )(*args)


def _mix_pre(att, y, proj2, g_att, g_ssd, tm):
    s = att.shape[0]

    def body(a_ref, y_ref, z_ref, ga_ref, gs_ref, o_ref):
        def strip(rows):
            a = a_ref[rows, :]
            ra = lax.rsqrt(jnp.mean(a * a, axis=-1, keepdims=True) + EPS)
            o_ref[rows, 0:ATT_W] = (a * ra * ga_ref[...]).astype(bf16)
            z = z_ref[rows, :]
            u = y_ref[rows, :] * (z * _sigmoid(z))
            ru = lax.rsqrt(jnp.mean(u * u, axis=-1, keepdims=True) + EPS)
            o_ref[rows, ATT_W:] = (u * ru * gs_ref[...]).astype(bf16)

        _strips(tm, strip)

    t = _tile_spec(tm, ATT_W)
    return pl.pallas_call(
        body, name="mix_pre", grid=(s // tm,), in_specs=[t, t, t, _row_spec(ATT_W), _row_spec(SSD_W)],
        out_specs=_tile_spec(tm, D), out_shape=_sds((s, D), bf16), compiler_params=_params(("parallel",)),
    )(att, y, proj2, g_att, g_ssd)


def _mix_pre_bwd(dmc, att, y, proj2, g_att, g_ssd, tm):
    s = att.shape[0]
    n = s // tm

    def body(da_ref, ds_ref, a_ref, y_ref, z_ref, ga_ref, gs_ref, datt_ref, dy_ref, dz_ref, dga_ref, dgs_ref, acc_a, acc_s):
        i = pl.program_id(0)

        @pl.when(i == 0)
        def _():
            acc_a[...] = jnp.zeros_like(acc_a)
            acc_s[...] = jnp.zeros_like(acc_s)

        def strip(rows):
            a = a_ref[rows, :]
            ra = lax.rsqrt(jnp.mean(a * a, axis=-1, keepdims=True) + EPS)
            ah = a * ra
            dan = da_ref[rows, :]
            acc_a[...] += _fold8(dan * ah)
            t = dan * ga_ref[...]
            datt_ref[rows, :] = (ra * (t - ah * jnp.mean(t * ah, axis=-1, keepdims=True))).astype(bf16)
            z = z_ref[rows, :]
            yv = y_ref[rows, :]
            sz = _sigmoid(z)
            sil = z * sz
            u = yv * sil
            ru = lax.rsqrt(jnp.mean(u * u, axis=-1, keepdims=True) + EPS)
            uh = u * ru
            dsn = ds_ref[rows, :]
            acc_s[...] += _fold8(dsn * uh)
            t2 = dsn * gs_ref[...]
            du = ru * (t2 - uh * jnp.mean(t2 * uh, axis=-1, keepdims=True))
            dy_ref[rows, :] = du * sil
            dz_ref[rows, :] = (du * yv * (sz * (1.0 + z * (1.0 - sz)))).astype(bf16)

        _strips(tm, strip)

        @pl.when(i == n - 1)
        def _():
            dga_ref[...] = jnp.sum(acc_a[...], axis=0, keepdims=True)
            dgs_ref[...] = jnp.sum(acc_s[...], axis=0, keepdims=True)

    t = _tile_spec(tm, ATT_W)
    row = _row_spec(ATT_W)
    return pl.pallas_call(
        body, name="mix_pre_bwd", grid=(n,),
        in_specs=[_tile_spec(tm, ATT_W, 0), _tile_spec(tm, ATT_W, 1), t, t, t, row, row],
        out_specs=[t, t, t, row, row],
        out_shape=[_sds((s, ATT_W), bf16), _sds((s, SSD_W), f32), _sds((s, SSD_W), bf16), _sds((1, ATT_W), f32), _sds((1, SSD_W), f32)],
        scratch_shapes=[pltpu.VMEM((8, ATT_W), f32)] * 2, compiler_params=_params(("arbitrary",)),
    )(dmc, dmc, att, y, proj2, g_att, g_ssd)


ATT_GROUP = 8
ATT_GROUP_FWD = 8


def _pair_rows(qc):
    two = jnp.concatenate([qc, qc], axis=0)
    r = lax.broadcasted_iota(jnp.int32, (2 * CHUNK, 128), 0)
    l = lax.broadcasted_iota(jnp.int32, (2 * CHUNK, 128), 1)
    return jnp.where((r < CHUNK) == (l < HD), two, jnp.zeros_like(two))


def _scaled(q):
    return q * jnp.asarray(HD ** -0.5, q.dtype)


def _pair_scores(wt, kb, bias, r0, masked):
    sc = lax.dot_general(wt, kb, (((1,), (1,)), ((), ())), preferred_element_type=f32) + bias
    if not masked:
        return sc
    kidx = lax.broadcasted_iota(jnp.int32, sc.shape, 1)
    return jnp.where(r0 + kidx >= PADK, sc, -jnp.inf)


def _softmax(sc, axis):
    e = jnp.exp(sc - jnp.max(sc, axis=axis, keepdims=True))
    return e * (1.0 / jnp.sum(e, axis=axis, keepdims=True))


def _chunk_loops(nc, group, per_trip):
    n_masked = min(-(-LEFT // per_trip), nc // per_trip)

    def run(masked):
        def step(g, carry):
            group(g, masked)
            return carry
        return step

    lax.fori_loop(0, n_masked, run(True), 0)
    lax.fori_loop(n_masked, nc // per_trip, run(False), 0)


def _pair_diag(r):
    lane = lax.broadcasted_iota(jnp.int32, (CHUNK, 128), 1)
    return jnp.where(lane < HD, r[0:CHUNK], r[CHUNK:])


def _pad_keys(k_ref, kp, s):
    kp[0:PADK, :] = jnp.zeros((PADK, 128), bf16)
    kp[PADK:PADK + s, :] = k_ref[...]
    kp[PADK + s:, :] = jnp.zeros((CHUNK, 128), bf16)


def _attn_fwd(qkv, bias2):
    s = qkv.shape[0]
    nc = s // CHUNK
    npair = NH // 2

    def body(q_ref, k_ref, v_ref, b_ref, o_ref, kp, vp):
        _pad_keys(k_ref, kp, s)
        _pad_keys(v_ref, vp, s)

        def group(g, masked):
            r0s = [pl.multiple_of((g * ATT_GROUP_FWD + u) * CHUNK, CHUNK) for u in range(ATT_GROUP_FWD)]
            scs = [_pair_scores(_pair_rows(_scaled(q_ref[pl.ds(r0, CHUNK), :])), kp[pl.ds(r0, BANDP), :], b_ref[...], r0, masked)
                   for r0 in r0s]
            ps = [_softmax(sc, -1).astype(bf16) for sc in scs]
            for r0, p in zip(r0s, ps):
                o_ref[pl.ds(r0, CHUNK), :] = _pair_diag(jnp.dot(p, vp[pl.ds(r0, BANDP), :], preferred_element_type=f32))

        _chunk_loops(nc, group, ATT_GROUP_FWD)

    return pl.pallas_call(
        body, name="attn_fwd", grid=(npair,),
        in_specs=[pl.BlockSpec((s, 128), lambda p: (0, p)), pl.BlockSpec((s, 128), lambda p: (0, npair + p)),
                  pl.BlockSpec((s, 128), lambda p: (0, 2 * npair + p)), pl.BlockSpec((None, 2 * CHUNK, BANDP), lambda p: (p, 0, 0))],
        out_specs=pl.BlockSpec((s, 128), lambda p: (0, p)), out_shape=_sds((s, ATT_W), f32),
        scratch_shapes=[pltpu.VMEM((PADK + s + CHUNK, 128), bf16)] * 2, compiler_params=_params(("parallel",)),
    )(qkv, qkv, qkv, bias2)


def _attn_bwd(qkv, datt, bias2):
    s = qkv.shape[0]
    nc = s // CHUNK
    npair = NH // 2
    rows = PADK + s + CHUNK
    nt = (((1,), (1,)), ((), ()))

    def body(q_ref, k_ref, v_ref, do_ref, b_ref, dq_ref, dk_ref, dv_ref, g_ref, kp, vp, dkp, dvp):
        _pad_keys(k_ref, kp, s)
        _pad_keys(v_ref, vp, s)
        dkp[...] = jnp.zeros_like(dkp)
        dvp[...] = jnp.zeros_like(dvp)
        g_ref[...] = jnp.zeros_like(g_ref)

        def group(g, masked):
            r0s = [pl.multiple_of((g * ATT_GROUP + u) * CHUNK, CHUNK) for u in range(ATT_GROUP)]
            wts = [_pair_rows(_scaled(q_ref[pl.ds(r0, CHUNK), :])) for r0 in r0s]
            dos = [_pair_rows(do_ref[pl.ds(r0, CHUNK), :]) for r0 in r0s]
            scs = [_pair_scores(wt, kp[pl.ds(r0, BANDP), :], b_ref[...], r0, masked) for wt, r0 in zip(wts, r0s)]
            dps = [lax.dot_general(do, vp[pl.ds(r0, BANDP), :], nt, preferred_element_type=f32) for do, r0 in zip(dos, r0s)]
            tn_ = (((0,), (0,)), ((), ()))
            for r0, wt, do, sc, dp in zip(r0s, wts, dos, scs, dps):
                p = _softmax(sc, -1)
                ds = p * (dp - jnp.sum(p * dp, axis=-1, keepdims=True))
                g_ref[...] += ds
                dsb = ds.astype(bf16)
                dq = jnp.dot(dsb, kp[pl.ds(r0, BANDP), :], preferred_element_type=f32)
                dq_ref[pl.ds(r0, CHUNK), :] = (_pair_diag(dq) * (HD ** -0.5)).astype(bf16)
                dkp[pl.ds(r0, BANDP), :] += lax.dot_general(dsb, wt, tn_, preferred_element_type=f32)
                dvp[pl.ds(r0, BANDP), :] += lax.dot_general(p.astype(bf16), do, tn_, preferred_element_type=f32)

        _chunk_loops(nc, group, ATT_GROUP)
        dk_ref[...] = dkp[PADK:PADK + s, :].astype(bf16)
        dv_ref[...] = dvp[PADK:PADK + s, :].astype(bf16)

    col = lambda off: pl.BlockSpec((s, 128), lambda p: (0, off + p))
    return pl.pallas_call(
        body, name="attn_bwd", grid=(npair,),
        in_specs=[col(0), col(npair), col(2 * npair), col(0), pl.BlockSpec((None, 2 * CHUNK, BANDP), lambda p: (p, 0, 0))],
        out_specs=[col(0), col(0), col(0), pl.BlockSpec((None, 2 * CHUNK, BANDP), lambda p: (p, 0, 0))],
        out_shape=[_sds((s, ATT_W), bf16)] * 3 + [_sds((npair, 2 * CHUNK, BANDP), f32)],
        scratch_shapes=[pltpu.VMEM((rows, 128), bf16)] * 2 + [pltpu.VMEM((rows, 128), f32)] * 2,
        compiler_params=_params(("parallel",)),
    )(qkv, qkv, qkv, datt, bias2)


def _rel_tables():
    onehot = np.zeros((BANDP, N_REL), np.float32)
    for j in range(BAND + CHUNK - 1):
        o = j - (CHUNK - 1)
        onehot[j, int(np.clip(PADK - o, -(CHUNK - 1), REL_CLIP)) + CHUNK - 1] = 1.0
    return onehot, np.ascontiguousarray(np.eye(CHUNK, dtype=np.float32)[::-1])


def _expand_bias(rel):
    ext = jnp.concatenate([jnp.broadcast_to(rel[:, N_REL - 1:], (NH, N_REL - 1)), rel[:, ::-1],
                           jnp.zeros((NH, BANDP - BAND + 1), f32)], axis=1)
    band = jnp.stack([ext[:, CHUNK - 1 - q:CHUNK - 1 - q + BANDP] for q in range(CHUNK)], axis=1)
    band = jnp.where(np.arange(BANDP) < BAND, band, -jnp.inf)
    return band.reshape(NH // 2, 2 * CHUNK, BANDP)


def _rel_bias_grad(gband):
    def body(g_ref, m_ref, flip_ref, o_ref, d2):
        for h in range(NH):
            rev = jnp.dot(flip_ref[...], g_ref[h], precision=HIGHEST, preferred_element_type=f32)
            rolled = pltpu.roll(rev, 0, 1, stride=1, stride_axis=0)
            d2[h:h + 1, :] = jnp.sum(rolled, axis=0, keepdims=True)
        o_ref[...] = jnp.dot(d2[...], m_ref[...], precision=HIGHEST, preferred_element_type=f32)

    onehot, flip = _rel_tables()
    return pl.pallas_call(
        body, name="rel_bias_grad", out_shape=_sds((NH, N_REL), f32), scratch_shapes=[pltpu.VMEM((NH, BANDP), f32)],
    )(gband, jnp.asarray(onehot), jnp.asarray(flip))


XBC_BLK = 512
XBC_COL0 = SSD_W // XBC_BLK
DT_COL = (SSD_W + XBC) // 128


def _conv_taps(ext, w_ref, b_ref, tm):
    n = ext.shape[0]
    pre = w_ref[3:4, :] * ext + b_ref[...]
    for j in range(3):
        pre = pre + w_ref[j:j + 1, :] * pltpu.roll(ext, 3 - j, 0)
    return pre


def _ssd_conv(proj2, conv_w, conv_b, tm):
    s = proj2.shape[0]
    nb = XBC // XBC_BLK

    def body(x_ref, p_ref, w_ref, b_ref, o_ref):
        i = pl.program_id(1)
        prev = jnp.where(i > 0, p_ref[...], 0.0)
        ext = jnp.concatenate([prev, x_ref[...]], axis=0)
        pre = _conv_taps(ext, w_ref, b_ref, tm)[8:8 + tm]
        o_ref[...] = pre * _sigmoid(pre)

    return pl.pallas_call(
        body, name="ssd_conv", grid=(nb, s // tm),
        in_specs=[pl.BlockSpec((tm, XBC_BLK), lambda j, i: (i, XBC_COL0 + j)),
                  pl.BlockSpec((8, XBC_BLK), lambda j, i: (jnp.maximum(i * (tm // 8) - 1, 0), XBC_COL0 + j)),
                  pl.BlockSpec((4, XBC_BLK), lambda j, i: (0, j)), pl.BlockSpec((1, XBC_BLK), lambda j, i: (0, j))],
        out_specs=pl.BlockSpec((tm, XBC_BLK), lambda j, i: (i, j)), out_shape=_sds((s, XBC), f32),
        compiler_params=_params(("parallel", "parallel")),
    )(proj2, proj2, conv_w, conv_b)


def _ssd_conv_bwd(dxbc, proj2, conv_w, conv_b, tm):
    s = proj2.shape[0]
    nb = XBC // XBC_BLK
    n = s // tm
    last8 = s // 8 - 1

    def body(x_ref, xp_ref, xn_ref, d_ref, dn_ref, w_ref, b_ref, o_ref, dw_ref, db_ref):
        i = pl.program_id(1)

        @pl.when(i == 0)
        def _():
            dw_ref[...] = jnp.zeros_like(dw_ref)
            db_ref[...] = jnp.zeros_like(db_ref)

        prev = jnp.where(i > 0, xp_ref[...], 0.0)
        ext = jnp.concatenate([prev, x_ref[...], xn_ref[...]], axis=0)
        pre = _conv_taps(ext, w_ref, b_ref, tm)
        sg = _sigmoid(pre)
        dnext = jnp.where(i < n - 1, dn_ref[...], 0.0)
        dext = jnp.concatenate([jnp.zeros((8, XBC_BLK), f32), d_ref[...], dnext], axis=0)
        dpre = dext * (sg * (1.0 + pre * (1.0 - sg)))
        rows = tm + 16
        dx = w_ref[3:4, :] * dpre
        for j in range(3):
            dx = dx + w_ref[j:j + 1, :] * pltpu.roll(dpre, rows - (3 - j), 0)
        o_ref[...] = dx[8:8 + tm].astype(bf16)
        dcur = dpre[8:8 + tm]
        db_ref[...] += jnp.sum(dcur, axis=0, keepdims=True)
        dw_ref[3:4, :] += jnp.sum(dcur * ext[8:8 + tm], axis=0, keepdims=True)
        for j in range(3):
            dw_ref[j:j + 1, :] += jnp.sum(dcur * pltpu.roll(ext, 3 - j, 0)[8:8 + tm], axis=0, keepdims=True)

    xcol = lambda j: XBC_COL0 + j
    return pl.pallas_call(
        body, name="ssd_conv_bwd", grid=(nb, n),
        in_specs=[pl.BlockSpec((tm, XBC_BLK), lambda j, i: (i, xcol(j))),
                  pl.BlockSpec((8, XBC_BLK), lambda j, i: (jnp.maximum(i * (tm // 8) - 1, 0), xcol(j))),
                  pl.BlockSpec((8, XBC_BLK), lambda j, i: (jnp.minimum((i + 1) * (tm // 8), last8), xcol(j))),
                  pl.BlockSpec((tm, XBC_BLK), lambda j, i: (i, j)),
                  pl.BlockSpec((8, XBC_BLK), lambda j, i: (jnp.minimum((i + 1) * (tm // 8), last8), j)),
                  pl.BlockSpec((4, XBC_BLK), lambda j, i: (0, j)), pl.BlockSpec((1, XBC_BLK), lambda j, i: (0, j))],
        out_specs=[pl.BlockSpec((tm, XBC_BLK), lambda j, i: (i, j)), pl.BlockSpec((4, XBC_BLK), lambda j, i: (0, j)),
                   pl.BlockSpec((1, XBC_BLK), lambda j, i: (0, j))],
        out_shape=[_sds((s, XBC), bf16), _sds((4, XBC), f32), _sds((1, XBC), f32)],
        compiler_params=_params(("parallel", "arbitrary")),
    )(proj2, proj2, proj2, dxbc, dxbc, conv_w, conv_b)


def _ssd_consts():
    ex = np.zeros((128, SSD_W), np.float32)
    for h in range(NH):
        ex[h, h * HD:(h + 1) * HD] = 1.0
    sel = np.zeros((8, 128), np.float32)
    for h in range(NH):
        sel[h // 2, h] = 1.0
    par = np.zeros((128, 128), np.float32)
    for r in range(128):
        for h in range(NH):
            par[r, h] = 1.0 if (h % 2) == (r // 64) else 0.0
    ones_blk = np.zeros((128, 128), np.float32)
    for r in range(128):
        ones_blk[r, (r // 64) * 64:(r // 64) * 64 + 64] = 1.0
    return ex, np.ascontiguousarray(ex.T), sel, par, ones_blk


SSD_SUB = 4


def _ssd_common(rs, xbc_ref, dtr_ref, a_ref, dtb_ref, ex_ref, sel_ref, par_ref):
    xs = xbc_ref[rs, 0:SSD_W]
    dt = _softplus(dtr_ref[rs, :] + dtb_ref[...])
    adt = dt * a_ref[...]
    r_i = lax.broadcasted_iota(jnp.int32, (CHUNK, CHUNK), 0)
    c_i = lax.broadcasted_iota(jnp.int32, (CHUNK, CHUNK), 1)
    tril = (r_i >= c_i).astype(f32)
    cs = _dot01(tril, adt, exact="a")
    cs2 = jnp.concatenate([cs, cs], axis=0) * par_ref[...]
    cstp = _dot01(sel_ref[...], cs2, tb=True, exact="a")
    both = _dot01(jnp.concatenate([dt, cs], axis=0), ex_ref[...])
    return xs, dt, cs, cstp, both[0:CHUNK], both[CHUNK:]


def _pair_mask():
    l_i = lax.broadcasted_iota(jnp.int32, (CHUNK, 128), 0)
    lane = lax.broadcasted_iota(jnp.int32, (CHUNK, 128), 1)
    return l_i >= (lane % CHUNK), lane < HD


def _block_diag(xp, first):
    z = jnp.zeros_like(xp)
    return jnp.concatenate([jnp.where(first, xp, z), jnp.where(first, z, xp)], axis=0)


def _ssd_fwd(xbc, proj2, a_row, dtb_row, dsk_full):
    s = xbc.shape[0]
    nc = s // CHUNK
    ex, ext, sel, par, ones_blk = _ssd_consts()

    def one_chunk(sub, states, refs):
        xbc_ref, dtr_ref, a_ref, dtb_ref, dsk_ref, ex_ref, sel_ref, par_ref, y_ref, hs_ref = refs
        rs = slice(sub * CHUNK, (sub + 1) * CHUNK)
        xs, dt, cs, cstp, dt_full, cs_full = _ssd_common(rs, xbc_ref, dtr_ref, a_ref, dtb_ref, ex_ref, sel_ref, par_ref)
        cs_last = cs_full[CHUNK - 1:CHUNK, :]
        xdt = xs * dt_full
        causal, first = _pair_mask()
        out = []
        for g in range(NG):
            gl = slice(g * GW, (g + 1) * GW)
            bg = xbc_ref[rs, SSD_W + g * NSTATE:SSD_W + (g + 1) * NSTATE].astype(bf16)
            cg = xbc_ref[rs, SSD_W + NG * NSTATE + g * NSTATE:SSD_W + NG * NSTATE + (g + 1) * NSTATE].astype(bf16)
            cb2 = lax.dot_general(cg, jnp.concatenate([bg, bg], axis=0), (((1,), (1,)), ((), ())), preferred_element_type=f32)
            hg = states[g]
            hs_ref[sub, g] = hg
            y0 = jnp.dot(cg, hg.astype(bf16), preferred_element_type=f32)
            yoff = jnp.exp(cs_full[:, gl]) * y0
            for j in range(GW // 128):
                pair = g * (GW // 128) + j
                pl_ = slice(pair * 128, (pair + 1) * 128)
                seg = jnp.exp(jnp.where(causal, cs_full[:, pl_] - cstp[pair:pair + 1, :], -jnp.inf))
                m = (cb2 * seg).astype(bf16)
                yd = jnp.dot(m, _block_diag(xdt[:, pl_].astype(bf16), first), preferred_element_type=f32)
                y_ref[rs, pl_] = yd + yoff[:, j * 128:(j + 1) * 128] + xs[:, pl_] * dsk_ref[:, pl_]
            xdec = (xdt[:, gl] * jnp.exp(cs_last[:, gl] - cs_full[:, gl])).astype(bf16)
            st = lax.dot_general(bg, xdec, (((0,), (0,)), ((), ())), preferred_element_type=f32)
            out.append(jnp.exp(cs_last[:, gl]) * hg + st)
        return out

    def body(*refs):
        hst = refs[-1]

        @pl.when(pl.program_id(0) == 0)
        def _():
            hst[...] = jnp.zeros_like(hst)

        states = [hst[g] for g in range(NG)]
        for sub in range(SSD_SUB):
            states = one_chunk(sub, states, refs[:-1])
        for g in range(NG):
            hst[g] = states[g]

    rows = SSD_SUB * CHUNK
    const = lambda shape: pl.BlockSpec(shape, lambda c: tuple(0 for _ in shape))
    return pl.pallas_call(
        body, name="ssd_fwd", grid=(nc // SSD_SUB,),
        in_specs=[pl.BlockSpec((rows, XBC), lambda c: (c, 0)), pl.BlockSpec((rows, 128), lambda c: (c, DT_COL)),
                  const((1, 128)), const((1, 128)), const((1, SSD_W)), const((128, SSD_W)), const((8, 128)), const((128, 128))],
        out_specs=[pl.BlockSpec((rows, SSD_W), lambda c: (c, 0)), pl.BlockSpec((SSD_SUB, NG, NSTATE, GW), lambda c: (c, 0, 0, 0))],
        out_shape=[_sds((s, SSD_W), f32), _sds((nc, NG, NSTATE, GW), f32)],
        scratch_shapes=[pltpu.VMEM((NG, NSTATE, GW), f32)], compiler_params=_params(("arbitrary",)),
    )(xbc, proj2, a_row, dtb_row, dsk_full, jnp.asarray(ex), jnp.asarray(sel), jnp.asarray(par))


def _ssd_bwd(xbc, proj2, dy, hsave, a_row, dtb_row, dsk_full):
    s = xbc.shape[0]
    nc = s // CHUNK
    ex, ext, sel, par, ones_blk = _ssd_consts()

    def one_chunk(sub, dhs, refs):
        (xbc_ref, dtr_ref, dy_ref, hs_ref, a_ref, dtb_ref, dsk_ref, ex_ref, ext_ref, sel_ref, par_ref, ob_ref,
         dxbc_ref, ddtr_ref, dd_ref, da_ref, ddtb_ref, dh, a_dd, a_da, a_dtb, dcs_lane, dcs_b, dxdt) = refs
        rs = slice(sub * CHUNK, (sub + 1) * CHUNK)
        dcs_lane, dcs_b, dxdt = dcs_lane.at[sub], dcs_b.at[sub], dxdt.at[sub]
        xs, dt, cs, cstp, dt_full, cs_full = _ssd_common(rs, xbc_ref, dtr_ref, a_ref, dtb_ref, ex_ref, sel_ref, par_ref)
        cs_last = cs_full[CHUNK - 1:CHUNK, :]
        xdt = xs * dt_full
        dyv = dy_ref[rs, :]
        a_dd[...] += _fold8(dyv * xs)
        causal, first = _pair_mask()
        diag = lax.broadcasted_iota(jnp.int32, (CHUNK, 128), 0) == lax.broadcasted_iota(jnp.int32, (CHUNK, 128), 1) % CHUNK
        dh_out = []
        for g in range(NG):
            gl = slice(g * GW, (g + 1) * GW)
            bcol = slice(SSD_W + g * NSTATE, SSD_W + (g + 1) * NSTATE)
            ccol = slice(SSD_W + NG * NSTATE + g * NSTATE, SSD_W + NG * NSTATE + (g + 1) * NSTATE)
            bg = xbc_ref[rs, bcol].astype(bf16)
            cg = xbc_ref[rs, ccol].astype(bf16)
            bg2 = jnp.concatenate([bg, bg], axis=0)
            cb2 = lax.dot_general(cg, bg2, (((1,), (1,)), ((), ())), preferred_element_type=f32)
            hg = hs_ref[sub, g]
            hgb = hg.astype(bf16)
            dhg = dhs[g]
            dhgb = dhg.astype(bf16)
            eg = jnp.exp(cs_full[:, gl])
            dec = jnp.exp(cs_last[:, gl] - cs_full[:, gl])
            gam = jnp.exp(cs_last[:, gl])
            dyg = dyv[:, gl]
            xdt_g = xdt[:, gl]
            y0 = jnp.dot(cg, hgb, preferred_element_type=f32)
            dy0 = (eg * dyg).astype(bf16)
            dcm = lax.dot_general(dy0, hgb, (((1,), (1,)), ((), ())), preferred_element_type=f32)
            dh_prev = gam * dhg + lax.dot_general(cg, dy0, (((0,), (0,)), ((), ())), preferred_element_type=f32)
            dgam = jnp.sum(dhg * hg, axis=0, keepdims=True) * gam
            dxdec = jnp.dot(bg, dhgb, preferred_element_type=f32)
            dbm = lax.dot_general((xdt_g * dec).astype(bf16), dhgb, (((1,), (1,)), ((), ())), preferred_element_type=f32)
            t = dxdec * xdt_g * dec
            dcs_lane[:, gl] = dyg * eg * y0 - t
            dcs_lane[CHUNK - 1:CHUNK, gl] += jnp.sum(t, axis=0, keepdims=True) + dgam
            dxdt[:, gl] = dxdec * dec
            dcb2 = jnp.zeros((CHUNK, 128), f32)
            for j in range(GW // 128):
                pair = g * (GW // 128) + j
                pl_ = slice(pair * 128, (pair + 1) * 128)
                seg = jnp.exp(jnp.where(causal, cs_full[:, pl_] - cstp[pair:pair + 1, :], -jnp.inf))
                m = cb2 * seg
                mb = m.astype(bf16)
                rhs = _block_diag(xdt[:, pl_].astype(bf16), first)
                dyp = dyv[:, pl_].astype(bf16)
                dm = lax.dot_general(dyp, rhs, (((1,), (1,)), ((), ())), preferred_element_type=f32)
                tt = lax.dot_general(mb, dyp, (((0,), (0,)), ((), ())), preferred_element_type=f32)
                dxdt[:, pl_] += jnp.where(first, tt[0:CHUNK], tt[CHUNK:])
                dcb2 = dcb2 + dm * seg
                w = dm * m
                colsum = jnp.sum(w, axis=0, keepdims=True)
                dcs_b[:, pl_] = _dot01(w - jnp.where(diag, colsum, 0.0), ob_ref[...])
            dcb2b = dcb2.astype(bf16)
            dcm = dcm + jnp.dot(dcb2b, bg2, preferred_element_type=f32)
            t3 = lax.dot_general(dcb2b, cg, (((0,), (0,)), ((), ())), preferred_element_type=f32)
            dxbc_ref[rs, bcol] = dbm + t3[0:CHUNK] + t3[CHUNK:]
            dxbc_ref[rs, ccol] = dcm
            dh_out.append(dh_prev)
        dxdtv = dxdt[...]
        both = _dot01(jnp.concatenate([dcs_lane[...] + dcs_b[...] * (1.0 / HD), dxdtv * xs], axis=0), ext_ref[...])
        dcs = both[0:CHUNK]
        r_i = lax.broadcasted_iota(jnp.int32, (CHUNK, CHUNK), 0)
        c_i = lax.broadcasted_iota(jnp.int32, (CHUNK, CHUNK), 1)
        triu = (r_i <= c_i).astype(f32)
        da_ = _dot01(triu, dcs, exact="a")
        ddt = da_ * a_ref[...] + both[CHUNK:]
        a_da[...] += _fold8(da_ * dt)
        dxbc_ref[rs, 0:SSD_W] = dyv * dsk_ref[...] + dxdtv * dt_full
        ddtr = ddt * _sigmoid(dtr_ref[rs, :] + dtb_ref[...])
        ddtr_ref[rs, :] = ddtr
        a_dtb[...] += _fold8(ddtr)
        return dh_out

    nsteps = nc // SSD_SUB

    def body(*refs):
        dd_ref, da_ref, ddtb_ref, dh, a_dd, a_da, a_dtb = refs[14:21]
        ext_ref = refs[8]
        step = pl.program_id(0)

        @pl.when(step == 0)
        def _():
            dh[...] = jnp.zeros_like(dh)
            a_dd[...] = jnp.zeros_like(a_dd)
            a_da[...] = jnp.zeros_like(a_da)
            a_dtb[...] = jnp.zeros_like(a_dtb)

        dhs = [dh[g] for g in range(NG)]
        for sub in reversed(range(SSD_SUB)):
            dhs = one_chunk(sub, dhs, refs)
        for g in range(NG):
            dh[g] = dhs[g]

        @pl.when(step == nsteps - 1)
        def _():
            dd_ref[...] = jnp.sum(jnp.dot(a_dd[...], ext_ref[...], precision=HIGHEST, preferred_element_type=f32), axis=0, keepdims=True)
            da_ref[...] = jnp.sum(a_da[...], axis=0, keepdims=True)
            ddtb_ref[...] = jnp.sum(a_dtb[...], axis=0, keepdims=True)

    rev = lambda c: nsteps - 1 - c
    rows = SSD_SUB * CHUNK
    const = lambda shape: pl.BlockSpec(shape, lambda c: tuple(0 for _ in shape))
    return pl.pallas_call(
        body, name="ssd_bwd", grid=(nsteps,),
        in_specs=[pl.BlockSpec((rows, XBC), lambda c: (rev(c), 0)), pl.BlockSpec((rows, 128), lambda c: (rev(c), DT_COL)),
                  pl.BlockSpec((rows, SSD_W), lambda c: (rev(c), 0)), pl.BlockSpec((SSD_SUB, NG, NSTATE, GW), lambda c: (rev(c), 0, 0, 0)),
                  const((1, 128)), const((1, 128)), const((1, SSD_W)), const((128, SSD_W)), const((SSD_W, 128)),
                  const((8, 128)), const((128, 128)), const((128, 128))],
        out_specs=[pl.BlockSpec((rows, XBC), lambda c: (rev(c), 0)), pl.BlockSpec((rows, 128), lambda c: (rev(c), 0)),
                   const((1, 128)), const((1, 128)), const((1, 128))],
        out_shape=[_sds((s, XBC), f32), _sds((s, 128), f32), _sds((1, 128), f32), _sds((1, 128), f32), _sds((1, 128), f32)],
        scratch_shapes=[pltpu.VMEM((NG, NSTATE, GW), f32), pltpu.VMEM((8, SSD_W), f32), pltpu.VMEM((8, 128), f32), pltpu.VMEM((8, 128), f32)]
        + [pltpu.VMEM((SSD_SUB, CHUNK, SSD_W), f32)] * 3,
        compiler_params=_params(("arbitrary",)),
    )(xbc, proj2, dy, hsave, a_row, dtb_row, dsk_full, jnp.asarray(ex), jnp.asarray(ext), jnp.asarray(sel), jnp.asarray(par),
      jnp.asarray(ones_blk))


def _local_step(x, tgt, mods, g_mix, rel, conv_w, conv_b, dt_bias, a_log, d_skip, g_att, g_ssd, g_ffn, g_final, weights):
    s = x.shape[0]
    tm_e = 512 if s % 512 == 0 else s
    tm_m = 512 if s % 512 == 0 else s
    tm_l = 1024 if s % 1024 == 0 else s
    tk = 2048 if s % 2048 == 0 else s
    sh1, sc1, gt1, sh2, sc2, gt2 = [mods[:, i * D:(i + 1) * D] for i in range(6)]

    h1b = _norm_mod("norm_mod_1", x, g_mix, sc1, sh1, tm_e)
    win, win_b = weights.w_in(h1b)
    qkv = _mm_nn_fullk("proj_qkv", h1b, win, tm_l, 768, bf16, n=IN_A)
    proj2 = _mm_nn_fullk("proj_zxbcdt", h1b, win_b, tm_l, 896, f32)
    bias = _expand_bias(rel)
    att = _attn_fwd(qkv, bias)
    xbc = _ssd_conv(proj2, conv_w, conv_b, tm_l)
    a_row = jnp.pad(-jnp.exp(a_log), ((0, 0), (0, 128 - NH)))
    dtb_row = jnp.pad(dt_bias, ((0, 0), (0, 128 - NH)))
    dsk_full = jnp.repeat(d_skip, HD, axis=1)
    y, hsave = _ssd_fwd(xbc, proj2, a_row, dtb_row, dsk_full)
    mixcat = _mix_pre(att, y, proj2, g_att, g_ssd, tm_e)
    wout = weights.w_out(mixcat)
    mix = _mm_nn_fullk("proj_out", mixcat, wout, tm_l, 1024, f32)
    x2, h2b = _resid_norm_mod(x, gt1, mix, g_ffn, sc2, sh2, tm_e)
    wg4, wu4, wd4 = weights.ffn(h2b)
    act, sil, ud = _ffn_up(h2b, wg4, wu4, tm_m)
    ffn = _ffn_down(act, wd4, tm_l)

    dx3, dffn, loss, dg_final, dgt2 = _final_fwd_bwd(x2, ffn, gt2, g_final, tgt, tm_e)
    tok = weights.grad(("w_down",), [_grad_wdown4(act, dffn, 1024, tk)])
    dgate, dup = _ffn_dact(dffn, wd4, sil, ud, tm_l, dep=tok)
    tok = weights.grad(("w_gate", "w_up"), [_grad_cols4("grad_w_gate", h2b, dgate, 1024, tk), _grad_cols4("grad_w_up", h2b, dup, 1024, tk)])
    dh2 = _ffn_dh(dgate, dup, wg4, wu4, tm_m, dep=tok)
    dx2, dmix, dsc2, dsh2, dg_ffn, dgt1 = _norm_mod_bwd("norm_mod_bwd_2", dh2, x2, g_ffn, sc2, dx3, tm_e, mix=mix, gt=gt1)
    tok = weights.grad(("w_out",), [_mm_tn("grad_w_out", mixcat, dmix, 1024, 1024, tk, bf16).reshape(NSH, D // NSH, D)])
    dmc = _mm_nt("dmixcat", dmix, wout, tm_l, 1024, D, f32, dep=tok)
    datt, dy, dz, dg_att, dg_ssd = _mix_pre_bwd(dmc, att, y, proj2, g_att, g_ssd, tm_e)
    dq, dk, dv, gband = _attn_bwd(qkv, datt, bias)
    drel = _rel_bias_grad(gband.reshape(NH, CHUNK, BANDP))
    dxbc, ddtr, dd_row, da_row, ddtb_row = _ssd_bwd(xbc, proj2, dy, hsave, a_row, dtb_row, dsk_full)
    dxbc_raw, dconv_w, dconv_b = _ssd_conv_bwd(dxbc, proj2, conv_w, conv_b, tm_e)
    dproj = jnp.concatenate([dq, dk, dv, dz, dxbc_raw, ddtr.astype(bf16)], axis=1)
    gwin = _mm_tn("grad_w_in", h1b, dproj, 1024, 1152, tk, bf16)
    gwin4 = jnp.stack([jnp.pad(gwin[:, k * IN_SH:(k + 1) * IN_SH], ((0, 0), (0, IN_SHP - IN_SH))) for k in range(NSH)])
    tok = weights.grad(("w_in",), [gwin4])
    dh1 = _mm_nt("dh1", dproj, win, tm_l, 1024, 1920, f32, dep=tok)
    grad_x, dsc1, dsh1, dg_mix = _norm_mod_bwd("norm_mod_bwd_1", dh1, x, g_mix, sc1, dx2, tm_e)

    dmods = jnp.concatenate([dsh1, dsc1, dgt1, dsh2, dsc2, dgt2], axis=1)
    dd_skip = dd_row[:, :NH]
    da_log = da_row[:, :NH] * a_row[:, :NH]
    small = dict(g_mix=dg_mix, conv_b=dconv_b, dt_bias=ddtb_row[:, :NH], a_log=da_log, d_skip=dd_skip, g_att_out=dg_att,
                 g_ssd_out=dg_ssd, g_ffn=dg_ffn, g_final=dg_final, rel_bias=drel, conv_w=dconv_w)
    return loss[0, 0], grad_x, dmods, small


HBM = pl.BlockSpec(memory_space=pl.ANY)
VMEM = pl.BlockSpec(memory_space=pltpu.VMEM)


def _place():
    x, y, c = lax.axis_index("x"), lax.axis_index("y"), lax.axis_index("c")
    chips = [(1 - x, y), (x, 1 - y), (1 - x, 1 - y)]
    return x, y, c, chips


def _allgather8(name, payload, dep=None):
    r = payload.shape[0]
    deps = [] if dep is None else [dep]

    def body(x_ref, *rest):
        out_ref, send_sems, recv_sems, local_sem = rest[-4:]
        x, y, c, chips = _place()
        me, sibling = (x, y, c), (x, y, 1 - c)

        def slot(px, py, pc):
            return out_ref.at[4 * px + 2 * py + pc]

        def copy(k, block, to, src=None):
            return pltpu.make_async_remote_copy(
                src_ref=slot(*block) if src is None else src, dst_ref=slot(*block),
                send_sem=send_sems.at[k], recv_sem=recv_sems.at[k], device_id=to, device_id_type=MESH)

        mine = pltpu.make_async_copy(x_ref, slot(*me), local_sem)
        mine.start()
        first = [copy(0, me, sibling, src=x_ref)]
        first += [copy(1 + j, me, (*chip, c), src=x_ref) for j, chip in enumerate(chips)]
        for cp in first:
            cp.start()
        passed = [copy(4 + j, (*chip, c), sibling) for j, chip in enumerate(chips)]
        for j, chip in enumerate(chips):
            copy(1 + j, (*chip, c), me).wait_recv()
            passed[j].start()
        copy(0, sibling, me).wait_recv()
        for j, chip in enumerate(chips):
            copy(4 + j, (*chip, 1 - c), me).wait_recv()
        for cp in first + passed:
            cp.wait_send()
        mine.wait()

    return pl.pallas_call(
        body, name=name, out_shape=_sds((N_DEV, r, 128), f32), in_specs=[VMEM] * (1 + len(deps)), out_specs=VMEM,
        scratch_shapes=[pltpu.SemaphoreType.DMA((7,)), pltpu.SemaphoreType.DMA((7,)), pltpu.SemaphoreType.DMA],
    )(payload, *deps)


def _sum8(g):
    r = g.shape[1]

    def body(g_ref, o_ref):
        acc = g_ref[0]
        for i in range(1, N_DEV):
            acc = acc + g_ref[i]
        o_ref[...] = acc

    return pl.pallas_call(body, name="sum8", out_shape=_sds((r, 128), f32))(g)


SEM = pl.BlockSpec(memory_space=pltpu.SEMAPHORE)
EFFECT = pltpu.SideEffectType.DATAFLOW_SIDE_EFFECTING


def _gather_copies(ins, lands, send_sems, recv_sems):
    x, y, c, chips = _place()
    k = 2 * x + y
    starts, recvs = [], []
    for w in range(len(ins)):
        for j, (px, py) in enumerate(chips):
            def mk(dst):
                return pltpu.make_async_remote_copy(src_ref=ins[w].at[c], dst_ref=dst, send_sem=send_sems[w].at[j],
                                                    recv_sem=recv_sems[w].at[j], device_id=(px, py, c), device_id_type=MESH)
            starts.append(mk(lands[w].at[k, c]))
            recvs.append(mk(lands[w].at[2 * px + py, c]))
    return starts, recvs


def _reduce_copies(ins, lands, send_sems, recv_sems):
    x, y, c, chips = _place()
    k = 2 * x + y
    starts, recvs = [], []
    for w in range(len(ins)):
        for j, (px, py) in enumerate(chips):
            def mk(dst):
                return pltpu.make_async_remote_copy(src_ref=ins[w].at[2 * px + py], dst_ref=dst, send_sem=send_sems[w].at[j],
                                                    recv_sem=recv_sems[w].at[j], device_id=(px, py, c), device_id_type=MESH)
            starts.append(mk(lands[w].at[k]))
            recvs.append(mk(lands[w].at[2 * px + py]))
    return starts, recvs


def _split_start(name, copies, srcs, land_shapes):
    nw = len(srcs)

    def body(*refs):
        starts, _ = copies(refs[:nw], refs[nw:2 * nw], refs[2 * nw:3 * nw], refs[3 * nw:4 * nw])
        for cp in starts:
            cp.start()
        refs[6 * nw][...] = jnp.zeros((8, 128), f32)

    sems = [pltpu.SemaphoreType.DMA((3,))] * nw
    bufs = [pltpu.HBM(s.shape, bf16) for s in srcs] + [pltpu.HBM(s, bf16) for s in land_shapes]
    res = pl.pallas_call(
        body, name=name, out_shape=sems + sems + bufs + [_sds((8, 128), f32)],
        in_specs=[HBM] * (2 * nw), out_specs=[SEM] * (2 * nw) + [HBM] * (2 * nw) + [VMEM],
        input_output_aliases={i: 2 * nw + i for i in range(2 * nw)},
        compiler_params=pltpu.CompilerParams(has_side_effects=EFFECT),
    )(*[pltpu.with_memory_space_constraint(s, pltpu.HBM) for s in srcs],
      *[pltpu.with_memory_space_constraint(lax.empty(s, bf16), pltpu.HBM) for s in land_shapes])
    return res[:nw], res[nw:2 * nw], res[2 * nw:3 * nw], res[3 * nw:4 * nw], res[4 * nw]


def _split_wait(name, copies, send_sems, recv_sems, srcs, lands, after):
    nw = len(srcs)

    def body(*refs):
        starts, recvs = copies(refs[:nw], refs[nw:2 * nw], refs[2 * nw:3 * nw], refs[3 * nw:4 * nw])
        for s_, r_ in zip(starts, recvs):
            s_.wait_send()
            r_.wait_recv()

    bufs = [pltpu.HBM(s.shape, bf16) for s in srcs] + [pltpu.HBM(l.shape, bf16) for l in lands]
    res = pl.pallas_call(
        body, name=name, out_shape=bufs, in_specs=[HBM] * (2 * nw) + [SEM] * (2 * nw) + [HBM], out_specs=[HBM] * (2 * nw),
        input_output_aliases={i: i for i in range(2 * nw)},
        compiler_params=pltpu.CompilerParams(has_side_effects=EFFECT),
    )(*srcs, *lands, *send_sems, *recv_sems, after)
    return res[:nw], res[nw:]


def _gather_forward(name, shards, lands):
    nw = len(shards)

    def body(*refs):
        ins, lands_in, outs = refs[:nw], refs[nw:2 * nw], refs[2 * nw:3 * nw]
        st_a, st_b, st_c = refs[3 * nw:4 * nw], refs[4 * nw:5 * nw], refs[5 * nw:6 * nw]
        send_sems, recv_sems, load_sems, store_sems = refs[6 * nw:]
        x, y, c, chips = _place()
        k = 2 * x + y
        sibling = (x, y, 1 - c)
        ld_a = [pltpu.make_async_copy(ins[w].at[c], st_a[w], load_sems.at[w, 0]) for w in range(nw)]
        ld_b = [pltpu.make_async_copy(ins[w].at[1 - c], st_b[w], load_sems.at[w, 1]) for w in range(nw)]
        for cp in ld_a + ld_b:
            cp.start()
        st_own = []
        for w in range(nw):
            ld_a[w].wait()
            st_own.append(pltpu.make_async_copy(st_a[w], outs[w].at[k, c], store_sems.at[w, 0]))
            st_own[-1].start()
        for w in range(nw):
            ld_b[w].wait()
            st_own.append(pltpu.make_async_copy(st_b[w], outs[w].at[k, 1 - c], store_sems.at[w, 1]))
            st_own[-1].start()
        for cp in st_own:
            cp.wait()
        fwds = {}
        for j, (px, py) in enumerate(chips):
            kq = 2 * px + py
            for w in range(nw):
                slot = st_b[w] if j % 2 == 0 else st_c[w]
                if j == 2:
                    fwds[w, 0].wait_send()
                ld = pltpu.make_async_copy(lands_in[w].at[kq, c], slot, load_sems.at[w, 2 + j])
                ld.start()
                ld.wait()
                fwds[w, j] = pltpu.make_async_remote_copy(src_ref=slot, dst_ref=outs[w].at[kq, c], send_sem=send_sems.at[w, j],
                                                          recv_sem=recv_sems.at[w, j], device_id=sibling, device_id_type=MESH)
                fwds[w, j].start()
        for j, (px, py) in enumerate(chips):
            for w in range(nw):
                pltpu.make_async_remote_copy(src_ref=st_c[w], dst_ref=outs[w].at[2 * px + py, 1 - c], send_sem=send_sems.at[w, j],
                                             recv_sem=recv_sems.at[w, j], device_id=sibling, device_id_type=MESH).wait_recv()
        for w in range(nw):
            fwds[w, 1].wait_send()
            fwds[w, 2].wait_send()

    stage = [pltpu.VMEM(s.shape[1:], bf16) for s in shards]
    return pl.pallas_call(
        body, name=name, out_shape=[_sds(l.shape, bf16) for l in lands],
        in_specs=[HBM] * (2 * nw), out_specs=[HBM] * nw, input_output_aliases={nw + w: w for w in range(nw)},
        scratch_shapes=stage * 3 + [pltpu.SemaphoreType.DMA((nw, 3)), pltpu.SemaphoreType.DMA((nw, 3)), pltpu.SemaphoreType.DMA((nw, 5)),
                                    pltpu.SemaphoreType.DMA((nw, 2))],
        compiler_params=pltpu.CompilerParams(vmem_limit_bytes=VMEM_LIMIT),
    )(*shards, *lands)


def _rs_pair_exchange(name, grads):
    nw = len(grads)

    def body(*refs):
        ins, got, stage = refs[:nw], refs[nw:2 * nw], refs[2 * nw:3 * nw]
        send_sems, recv_sems, load_sems = refs[3 * nw:]
        x, y, c, _ = _place()

        def load(w, kk):
            return pltpu.make_async_copy(ins[w].at[kk, 1 - c], stage[w].at[kk % 2], load_sems.at[w, kk])

        def send(w, kk):
            return pltpu.make_async_remote_copy(src_ref=stage[w].at[kk % 2], dst_ref=got[w].at[kk], send_sem=send_sems.at[w, kk],
                                                recv_sem=recv_sems.at[w, kk], device_id=(x, y, 1 - c), device_id_type=MESH)

        for kk in range(2):
            for w in range(nw):
                load(w, kk).start()
        for kk in range(NSH):
            for w in range(nw):
                load(w, kk).wait()
                send(w, kk).start()
            if kk + 2 < NSH:
                for w in range(nw):
                    send(w, kk).wait_send()
                    load(w, kk + 2).start()
        for kk in range(NSH - 2, NSH):
            for w in range(nw):
                send(w, kk).wait_send()
        for kk in range(NSH):
            for w in range(nw):
                send(w, kk).wait_recv()

    return pl.pallas_call(
        body, name=name, out_shape=[_sds((NSH,) + g.shape[2:], bf16) for g in grads], in_specs=[HBM] * nw, out_specs=[HBM] * nw,
        scratch_shapes=[pltpu.VMEM((2,) + g.shape[2:], bf16) for g in grads]
        + [pltpu.SemaphoreType.DMA((nw, NSH)), pltpu.SemaphoreType.DMA((nw, NSH)), pltpu.SemaphoreType.DMA((nw, NSH))],
        compiler_params=pltpu.CompilerParams(vmem_limit_bytes=VMEM_LIMIT),
    )(*grads)


def _rs_pair_gather(name, halves):
    nw = len(halves)

    def body(*refs):
        ins, outs, stage = refs[:nw], refs[nw:2 * nw], refs[2 * nw:3 * nw]
        send_sems, recv_sems, local_sems, stage_sems = refs[3 * nw:]
        x, y, c, _ = _place()
        loads = [pltpu.make_async_copy(ins[w], stage[w], stage_sems.at[w]) for w in range(nw)]
        for cp in loads:
            cp.start()
        local, cps = [], []
        for w in range(nw):
            loads[w].wait()
            local.append(pltpu.make_async_copy(stage[w], outs[w].at[c], local_sems.at[w]))
            cps.append(pltpu.make_async_remote_copy(src_ref=stage[w], dst_ref=outs[w].at[c], send_sem=send_sems.at[w],
                                                    recv_sem=recv_sems.at[w], device_id=(x, y, 1 - c), device_id_type=MESH))
            local[w].start()
            cps[w].start()
        for w in range(nw):
            pltpu.make_async_remote_copy(src_ref=stage[w], dst_ref=outs[w].at[1 - c], send_sem=send_sems.at[w], recv_sem=recv_sems.at[w],
                                         device_id=(x, y, 1 - c), device_id_type=MESH).wait_recv()
        for cp in cps:
            cp.wait_send()
        for cp in local:
            cp.wait()

    return pl.pallas_call(
        body, name=name, out_shape=[_sds((2,) + h.shape, f32) for h in halves], in_specs=[HBM] * nw, out_specs=[HBM] * nw,
        scratch_shapes=[pltpu.VMEM(h.shape, f32) for h in halves]
        + [pltpu.SemaphoreType.DMA((nw,)), pltpu.SemaphoreType.DMA((nw,)), pltpu.SemaphoreType.DMA((nw,)), pltpu.SemaphoreType.DMA((nw,))],
        compiler_params=pltpu.CompilerParams(vmem_limit_bytes=VMEM_LIMIT),
    )(*halves)


def _row_tile(r, c, nbuf):
    budget = 24 * 1024 * 1024 // (2 * nbuf * 4 * c)
    t = 8
    while t * 2 <= budget and r % (t * 2) == 0:
        t *= 2
    return t


def _cast_bf16(name, a, dep=None):
    r, c = a.shape
    tr = _row_tile(r, c, 2)
    dep_specs, dep_ops = _dep_args(dep, 1)

    def body(a_ref, *rest):
        rest[-1][...] = a_ref[...].astype(bf16)

    spec = pl.BlockSpec((tr, c), lambda i: (i, 0))
    return pl.pallas_call(body, name=name, grid=(r // tr,), in_specs=[spec] + dep_specs, out_specs=spec, out_shape=_sds((r, c), bf16),
                          compiler_params=_params(("parallel",)))(a, *dep_ops)


def _w_in_columns(win4):
    tr = 256

    def body(a_ref, o_ref, ob_ref):
        for k in range(NSH):
            o_ref[:, IN_SH * k:IN_SH * (k + 1)] = a_ref[k][:, :IN_SH]
        o_ref[:, IN_COLS:] = jnp.zeros((tr, IN_P - IN_COLS), bf16)
        ob_ref[...] = o_ref[:, IN_A:]

    return pl.pallas_call(
        body, name="w_in_columns", grid=(D // tr,), in_specs=[pl.BlockSpec((NSH, tr, IN_SHP), lambda i: (0, i, 0))],
        out_specs=[pl.BlockSpec((tr, IN_P), lambda i: (i, 0)), pl.BlockSpec((tr, IN_B), lambda i: (i, 0))],
        out_shape=[_sds((D, IN_P), bf16), _sds((D, IN_B), bf16)], compiler_params=_params(("parallel",)))(win4)


def _pair_sum(name, core, grads, got):
    _, _, rh, c = grads.shape
    tr = _row_tile(rh, c, 2)

    def body(c_ref, a_ref, b_ref, o_ref):
        o_ref[...] = (a_ref[...].astype(f32) + b_ref[...].astype(f32)).astype(bf16)

    spec = pl.BlockSpec((None, tr, c), lambda k, i, c_ref: (k, i, 0))
    return pl.pallas_call(
        body, name=name, out_shape=_sds((NSH, rh, c), bf16),
        grid_spec=pltpu.PrefetchScalarGridSpec(
            num_scalar_prefetch=1, grid=(NSH, rh // tr),
            in_specs=[pl.BlockSpec((None, None, tr, c), lambda k, i, c_ref: (k, c_ref[0], i, 0)), spec], out_specs=spec),
        compiler_params=_params(("parallel", "parallel")))(core, grads, got)


def _chip_sum(name, chip, sums, lands):
    _, rh, c = sums.shape
    tr = _row_tile(rh, c, 4)

    def body(k_ref, own_ref, l_ref, o_ref):
        own = own_ref[...].astype(f32)
        acc = None
        for j in range(NSH):
            term = jnp.where(k_ref[0] == j, own, l_ref[j].astype(f32))
            acc = term if acc is None else acc + term
        o_ref[...] = acc

    return pl.pallas_call(
        body, name=name, out_shape=_sds((rh, c), f32),
        grid_spec=pltpu.PrefetchScalarGridSpec(
            num_scalar_prefetch=1, grid=(rh // tr,),
            in_specs=[pl.BlockSpec((None, tr, c), lambda i, k_ref: (k_ref[0], i, 0)), pl.BlockSpec((NSH, tr, c), lambda i, k_ref: (0, i, 0))],
            out_specs=pl.BlockSpec((tr, c), lambda i, k_ref: (i, 0))),
        compiler_params=_params(("parallel",)))(chip, sums, lands)


def _mods_part(cond16, w_ada, b_part):
    n = w_ada.shape[1]
    tn = 512

    def body(c_ref, w_ref, b_ref, o_ref):
        cv = c_ref[...]
        o_ref[...] = _dot(cv * _sigmoid(cv), w_ref[...]) + b_ref[...]

    return pl.pallas_call(
        body, name="mods_part", grid=(n // tn,),
        in_specs=[pl.BlockSpec((16, D), lambda j: (0, 0)), pl.BlockSpec((D, tn), lambda j: (0, j)), pl.BlockSpec((1, tn), lambda j: (0, j))],
        out_specs=pl.BlockSpec((16, tn), lambda j: (0, j)), out_shape=_sds((16, n), f32), compiler_params=_params(("parallel",)),
    )(cond16, w_ada, b_part)


def _grad_w_ada(cond16, dm16):
    n = dm16.shape[1]
    tr = 256

    def body(c_ref, d_ref, o_ref):
        cv = c_ref[...]
        o_ref[...] = _dot(cv * _sigmoid(cv), d_ref[...], ta=True)

    return pl.pallas_call(
        body, name="grad_w_ada", grid=(D // tr,),
        in_specs=[pl.BlockSpec((16, tr), lambda i: (0, i)), pl.BlockSpec((16, n), lambda i: (0, 0))],
        out_specs=pl.BlockSpec((tr, n), lambda i: (i, 0)), out_shape=_sds((D, n), f32), compiler_params=_params(("parallel",)),
    )(cond16, dm16)


def _adamw(name, w, g, m, v):
    r, c = w.shape
    tr = _row_tile(r, c, 7)
    spec = pl.BlockSpec((tr, c), lambda i: (i, 0))
    grid = (r // tr,)

    def body(w_ref, g_ref, m_ref, v_ref, d_ref, nm_ref, nv_ref):
        gv = g_ref[...]
        nm = ADAM_B1 * m_ref[...] + (1.0 - ADAM_B1) * gv
        nv = ADAM_B2 * v_ref[...] + (1.0 - ADAM_B2) * (gv * gv)
        nm_ref[...] = nm
        nv_ref[...] = nv
        m_hat = nm / (1.0 - ADAM_B1 ** ADAM_STEP)
        v_hat = nv / (1.0 - ADAM_B2 ** ADAM_STEP)
        d_ref[...] = -ADAM_LR * (m_hat / (jnp.sqrt(v_hat) + ADAM_EPS) + ADAM_WD * w_ref[...])

    return pl.pallas_call(body, name=name, grid=grid, in_specs=[spec] * 4, out_specs=[spec] * 3, out_shape=[_sds(w.shape, f32)] * 3,
                          compiler_params=_params(("parallel",)))(w, g, m, v)


def _pack(parts, rows):
    flat = []
    for p in parts:
        p = p.reshape(-1)
        flat.append(jnp.pad(p, (0, (-p.shape[0]) % 128)))
    v = jnp.concatenate(flat)
    return jnp.pad(v, (0, rows * 128 - v.shape[0])).reshape(rows, 128)


def _unpack(packed, sizes):
    lead = packed.shape[:-2]
    flat = packed.reshape(lead + (-1,))
    out, off = [], 0
    for n in sizes:
        out.append(flat[..., off:off + n])
        off += n + (-n) % 128
    return out


BIG = ("w_in", "w_out", "w_gate", "w_up", "w_down")
SMALL = ("b_ada", "g_mix", "conv_b", "dt_bias", "a_log", "d_skip", "g_att_out", "g_ssd_out", "g_ffn", "g_final", "rel_bias", "conv_w")
ORDER = ("w_ada", "b_ada", "g_mix", "w_in", "rel_bias", "conv_w", "conv_b", "dt_bias", "a_log", "d_skip", "g_att_out", "g_ssd_out",
         "w_out", "g_ffn", "w_gate", "w_up", "w_down", "g_final")
REL_SH = N_REL // NSH
CONVW_SH = XBC // NSH
ADA_SH = 6 * D // NSH


class _Exchange:
    def __init__(self, core, chip):
        self.core, self.chip = core, chip
        self.gathered = {}
        self.pending = []

    def gather(self, names, shards):
        ssem, rsem, thru, lands, token = _split_start("gather_start_" + "_".join(names), _gather_copies, shards,
                                                      [(NSH,) + s.shape for s in shards])
        self.gathered.update({n: (ssem[i], rsem[i], thru[i], lands[i]) for i, n in enumerate(names)})
        return token

    def _whole(self, names, after):
        ssem, rsem, thru, lands = zip(*[self.gathered[n] for n in names])
        tag = "_".join(names)
        thru, lands = _split_wait("gather_wait_" + tag, _gather_copies, ssem, rsem, thru, lands, after)
        return _gather_forward("gather_forward_" + tag, thru, lands)

    def w_in(self, after):
        (win4,) = self._whole(("w_in",), after)
        return _w_in_columns(win4.reshape(NSH, D, IN_SHP))

    def w_out(self, after):
        (wout4,) = self._whole(("w_out",), after)
        return wout4.reshape(D, D)

    def ffn(self, after):
        wg4, wu4, wd4 = self._whole(("w_gate", "w_up", "w_down"), after)
        return wg4.reshape(NSH, D, FSH), wu4.reshape(NSH, D, FSH), wd4.reshape(NSH, FSH, D)

    def grad(self, names, grads):
        tag = "_".join(names)
        stacked = [g.reshape(NSH, 2, g.shape[1] // 2, g.shape[2]) for g in grads]
        got = _rs_pair_exchange("rs_pair_exchange_" + tag, stacked)
        sums = [_pair_sum("pair_sum_" + n, self.core, o, g) for n, o, g in zip(names, stacked, got)]
        self.pending.append((names, _split_start("rs_start_" + tag, _reduce_copies, sums, [s.shape for s in sums])))
        return self.pending[-1][1][4]

    def finish(self, after):
        grads = {}
        for names, (ssem, rsem, sums, lands, _) in self.pending:
            tag = "_".join(names)
            sums, lands = _split_wait("rs_wait_" + tag, _reduce_copies, ssem, rsem, sums, lands, after)
            halves = [_chip_sum("chip_sum_" + n, self.chip, sm, ld) for n, sm, ld in zip(names, sums, lands)]
            for n, f in zip(names, _rs_pair_gather("rs_pair_gather_" + tag, halves)):
                grads[n] = f.reshape(2 * f.shape[1], f.shape[2])
        return grads


def kernel(x, c, w_ada, b_ada, g_mix, w_in, rel_bias, conv_w, conv_b, dt_bias, a_log, d_skip, g_att_out, g_ssd_out, w_out, g_ffn, w_gate, w_up, w_down, g_final, loss_target, m_w_ada, m_b_ada, m_g_mix, m_w_in, m_rel_bias, m_conv_w, m_conv_b, m_dt_bias, m_a_log, m_d_skip, m_g_att_out, m_g_ssd_out, m_w_out, m_g_ffn, m_w_gate, m_w_up, m_w_down, m_g_final, v_w_ada, v_b_ada, v_g_mix, v_w_in, v_rel_bias, v_conv_w, v_conv_b, v_dt_bias, v_a_log, v_d_skip, v_g_att_out, v_g_ssd_out, v_w_out, v_g_ffn, v_w_gate, v_w_up, v_w_down, v_g_final):
    args = dict(locals())
    w = {n: args[n] for n in ORDER}
    m = {n: args["m_" + n] for n in ORDER}
    v = {n: args["v_" + n] for n in ORDER}
    ix, iy, ic = lax.axis_index("x"), lax.axis_index("y"), lax.axis_index("c")
    chip = 2 * ix + iy
    dev = 2 * chip + ic
    s = x.shape[1]

    g1 = _allgather8("gather_inputs", _pack([c[0], rel_bias[0], conv_w[0]], 40))
    c_all, rel_sh, convw_sh = _unpack(g1, [D, NH * REL_SH, 4 * CONVW_SH])
    rel_full = jnp.concatenate([rel_sh[2 * k].reshape(NH, REL_SH) for k in range(NSH)], axis=1)
    convw_full = jnp.concatenate([convw_sh[2 * k].reshape(4, CONVW_SH) for k in range(NSH)], axis=1)
    cond16 = jnp.pad(c_all, ((0, 8), (0, 0)))
    b_part = lax.dynamic_slice_in_dim(b_ada, chip * ADA_SH, ADA_SH, axis=1)
    mods_part = _mods_part(cond16, w_ada[0], b_part)[:N_DEV]
    g2 = _allgather8("gather_mods", mods_part.reshape(N_DEV * ADA_SH // 128, 128))
    mods_all = jnp.concatenate([g2[2 * k].reshape(N_DEV, ADA_SH) for k in range(NSH)], axis=1)
    mods = lax.dynamic_slice_in_dim(mods_all, dev, 1, axis=0)

    exchange = _Exchange(jnp.reshape(ic, (1,)).astype(jnp.int32), jnp.reshape(chip, (1,)).astype(jnp.int32))
    shard_in = _cast_bf16("cast_w_in", jnp.pad(w_in[0], ((0, 0), (0, IN_SHP - IN_SH))), dep=g2[0, :8]).reshape(2, D // 2, IN_SHP)
    tok = exchange.gather(("w_in",), [shard_in])
    tok = exchange.gather(("w_out", "w_gate", "w_up", "w_down"), [
        _cast_bf16("cast_w_out", w_out[0], dep=tok).reshape(2, D // NSH // 2, D),
        _cast_bf16("cast_w_gate", w_gate[0], dep=tok).reshape(2, D // 2, FSH),
        _cast_bf16("cast_w_up", w_up[0], dep=tok).reshape(2, D // 2, FSH),
        _cast_bf16("cast_w_down", w_down[0], dep=tok).reshape(2, FSH // 2, D)])
    mods = mods + tok[:1, :1]

    loss, grad_x, dmods, small = _local_step(
        x[0], loss_target[0], mods, g_mix, rel_full, convw_full, conv_b, dt_bias, a_log, d_skip, g_att_out, g_ssd_out, g_ffn,
        g_final[None, :], exchange)

    small_names = ("g_mix", "conv_b", "dt_bias", "a_log", "d_skip", "g_att_out", "g_ssd_out", "g_ffn", "g_final", "rel_bias", "conv_w")
    g3 = _allgather8("gather_small_grads", _pack([dmods] + [small[n] for n in small_names], 264))
    sizes = [6 * D] + [int(np.prod(small[n].shape)) for n in small_names]
    dmods_all = _unpack(g3, sizes)[0]
    summed = _unpack(_sum8(g3), sizes)
    grads = {"b_ada": summed[0].reshape(1, 6 * D)}
    for n, val in zip(small_names, summed[1:]):
        grads[n] = val.reshape(small[n].shape)
    grads["rel_bias"] = lax.dynamic_slice_in_dim(grads["rel_bias"], chip * REL_SH, REL_SH, axis=1)
    grads["conv_w"] = lax.dynamic_slice_in_dim(grads["conv_w"], chip * CONVW_SH, CONVW_SH, axis=1)
    grads["g_final"] = grads["g_final"].reshape(D)
    dm16 = jnp.pad(lax.dynamic_slice_in_dim(dmods_all, chip * ADA_SH, ADA_SH, axis=1), ((0, 8), (0, 0)))
    grads["w_ada"] = _grad_w_ada(cond16, dm16)

    delta, new_m, new_v = {}, {}, {}
    delta["w_ada"], new_m["w_ada"], new_v["w_ada"] = _adamw("adamw_w_ada", w_ada[0], grads["w_ada"], m_w_ada[0], v_w_ada[0])
    grads.update(exchange.finish(grad_x))
    grads["w_in"] = grads["w_in"][:, :IN_SH]
    for n in BIG:
        delta[n], new_m[n], new_v[n] = _adamw("adamw_" + n, w[n][0], grads[n], m[n][0], v[n][0])
    sw = _pack([w[n] for n in SMALL], 200)
    sg = _pack([grads[n] for n in SMALL], 200)
    sm = _pack([m[n] for n in SMALL], 200)
    sv = _pack([v[n] for n in SMALL], 200)
    ssz = [int(np.prod(w[n].shape)) for n in SMALL]
    for dst, packed in zip((delta, new_m, new_v), _adamw("adamw_small", sw, sg, sm, sv)):
        for n, val in zip(SMALL, _unpack(packed, ssz)):
            dst[n] = val

    def shaped(d, n):
        return d[n].reshape(w[n].shape)

    total = lax.psum(loss, ("x", "y", "c"))
    return (total, grad_x[None], *[shaped(grads, n) for n in ORDER], *[shaped(delta, n) for n in ORDER],
            *[shaped(new_m, n) for n in ORDER], *[shaped(new_v, n) for n in ORDER])
```

```python
import functools

import numpy as np
import jax
import jax.numpy as jnp
from jax import lax
from jax.experimental import pallas as pl
from jax.experimental.pallas import tpu as pltpu

f32 = jnp.float32
bf16 = jnp.bfloat16
HIGHEST = lax.Precision.HIGHEST
MESH = pl.DeviceIdType.MESH

D = 2048
CHUNK = 64
LEFT = 8
BAND = (LEFT + 1) * CHUNK
BANDP = 640
PADK = LEFT * CHUNK
NH = 16
HD = 64
ATT_W = NH * HD
SSD_W = 1024
NG = 2
NSTATE = 128
GW = SSD_W // NG
XBC = SSD_W + 2 * NG * NSTATE
N_REL = 320
REL_CLIP = 256
FFN = 5632
NSH = 4
FSH = FFN // NSH
IN_COLS = 5648
IN_SH = IN_COLS // NSH
IN_SHP = 1536
IN_A = 3 * ATT_W
IN_B = 2688
IN_P = IN_A + IN_B
EPS = 1e-6
N_DEV = 8

ADAM_LR = 0.001
ADAM_B1 = 0.9
ADAM_B2 = 0.999
ADAM_EPS = 1e-08
ADAM_WD = 0.01
ADAM_STEP = 10

VMEM_LIMIT = 56 * 1024 * 1024


def _params(sem):
    return pltpu.CompilerParams(dimension_semantics=sem, vmem_limit_bytes=VMEM_LIMIT)


def _sds(shape, dtype):
    return jax.ShapeDtypeStruct(shape, dtype)


def _fold8(v):
    r, w = v.shape
    return jnp.sum(v.reshape(r // 8, 8, w), axis=0)


STRIP = 16


def _strips(tm, fn):
    def step(j, carry):
        fn(pl.ds(pl.multiple_of(j * STRIP, STRIP), STRIP))
        return carry
    lax.fori_loop(0, tm // STRIP, step, 0, unroll=4)


def _sigmoid(v):
    return 1.0 / (1.0 + jnp.exp(-v))


def _softplus(v):
    return jnp.maximum(v, 0.0) + jnp.log(1.0 + jnp.exp(-jnp.abs(v)))


def _dot(a, b, ta=False, tb=False):
    dn = (((0 if ta else 1,), (1 if tb else 0,)), ((), ()))
    return lax.dot_general(a.astype(bf16), b.astype(bf16), dn, preferred_element_type=f32)


def _dep_args(dep, ngrid):
    if dep is None:
        return [], []
    return [pl.BlockSpec((8, 128), lambda *_: (0, 0))], [dep]


def _dot01(a, b, ta=False, tb=False, exact="b"):
    dn = (((0 if ta else 1,), (1 if tb else 0,)), ((), ()))
    x = a if exact == "b" else b
    hi = x.astype(bf16)
    r = x - hi.astype(f32)
    mid = r.astype(bf16)
    lo = (r - mid.astype(f32)).astype(bf16)
    if exact == "b":
        m = b.astype(bf16)
        return sum(lax.dot_general(p, m, dn, preferred_element_type=f32) for p in (hi, mid, lo))
    m = a.astype(bf16)
    return sum(lax.dot_general(m, p, dn, preferred_element_type=f32) for p in (hi, mid, lo))


def _matmul(name, a, b, *, grid, a_spec, b_spec, o_spec, o_shape, o_dtype, acc_shape, ta=False, tb=False, dep=None):
    nk = grid[2]
    dep_specs, dep_ops = _dep_args(dep, 3)

    def body(a_ref, b_ref, *rest):
        o_ref, acc_ref = rest[-2:]
        p = _dot(a_ref[...], b_ref[...], ta, tb)
        if nk == 1:
            o_ref[...] = p.astype(o_ref.dtype)
        else:
            k = pl.program_id(2)

            @pl.when(k == 0)
            def _():
                acc_ref[...] = p

            @pl.when(jnp.logical_and(k > 0, k < nk - 1))
            def _():
                acc_ref[...] += p

            @pl.when(k == nk - 1)
            def _():
                o_ref[...] = (acc_ref[...] + p).astype(o_ref.dtype)

    return pl.pallas_call(
        body, name=name, grid=grid, in_specs=[a_spec, b_spec] + dep_specs, out_specs=o_spec,
        out_shape=_sds(o_shape, o_dtype), scratch_shapes=[pltpu.VMEM(acc_shape if nk > 1 else (8, 128), f32)],
        compiler_params=_params(("parallel", "parallel", "arbitrary")),
    )(a, b, *dep_ops)


def _mm_nn_fullk(name, a, b, tm, tn, o_dtype, n=None):
    m, k = a.shape
    n = b.shape[1] if n is None else n
    return _matmul(name, a, b, grid=(m // tm, n // tn, 1),
                   a_spec=pl.BlockSpec((tm, k), lambda i, j, kk: (i, 0)),
                   b_spec=pl.BlockSpec((k, tn), lambda i, j, kk: (0, j)),
                   o_spec=pl.BlockSpec((tm, tn), lambda i, j, kk: (i, j)),
                   o_shape=(m, n), o_dtype=o_dtype, acc_shape=(tm, tn))


def _mm_nt(name, a, b, tm, tn, tk, o_dtype, dep=None):
    m, k = a.shape
    n = b.shape[0]
    return _matmul(name, a, b, grid=(m // tm, n // tn, k // tk), tb=True, dep=dep,
                   a_spec=pl.BlockSpec((tm, tk), lambda i, j, kk: (i, kk)),
                   b_spec=pl.BlockSpec((tn, tk), lambda i, j, kk: (j, kk)),
                   o_spec=pl.BlockSpec((tm, tn), lambda i, j, kk: (i, j)),
                   o_shape=(m, n), o_dtype=o_dtype, acc_shape=(tm, tn))


def _mm_tn(name, a, b, tm, tn, tk, o_dtype):
    k, m = a.shape
    n = b.shape[1]
    return _matmul(name, a, b, grid=(m // tm, n // tn, k // tk), ta=True,
                   a_spec=pl.BlockSpec((tk, tm), lambda i, j, kk: (kk, i)),
                   b_spec=pl.BlockSpec((tk, tn), lambda i, j, kk: (kk, j)),
                   o_spec=pl.BlockSpec((tm, tn), lambda i, j, kk: (i, j)),
                   o_shape=(m, n), o_dtype=o_dtype, acc_shape=(tm, tn))


FSH_PARTS = (slice(0, 640), slice(640, FSH))


def _ffn_up(h2b, wg4, wu4, tm):
    s = h2b.shape[0]

    def body(h_ref, wg_ref, wu_ref, a_ref, s_ref, ud_ref):
        h = h_ref[...]
        for cols in FSH_PARTS:
            g = _dot(h, wg_ref[:, cols])
            u = _dot(h, wu_ref[:, cols])
            sg = _sigmoid(g)
            sil = g * sg
            a_ref[:, cols] = (sil * u).astype(bf16)
            s_ref[:, cols] = sil.astype(bf16)
            ud_ref[:, cols] = (u * (sg * (1.0 + g * (1.0 - sg)))).astype(bf16)

    wspec = pl.BlockSpec((None, D, FSH), lambda k, i: (k, 0, 0))
    ospec = pl.BlockSpec((tm, FSH), lambda k, i: (i, k))
    return pl.pallas_call(
        body, name="ffn_up", grid=(NSH, s // tm),
        in_specs=[pl.BlockSpec((tm, D), lambda k, i: (i, 0)), wspec, wspec],
        out_specs=[ospec, ospec, ospec], out_shape=[_sds((s, FFN), bf16)] * 3,
        compiler_params=_params(("parallel", "parallel")),
    )(h2b, wg4, wu4)


def _ffn_down(act, wd4, tm):
    s = act.shape[0]
    return _matmul("ffn_down", act, wd4, grid=(s // tm, 1, NSH),
                   a_spec=pl.BlockSpec((tm, FSH), lambda i, j, k: (i, k)),
                   b_spec=pl.BlockSpec((None, FSH, D), lambda i, j, k: (k, 0, 0)),
                   o_spec=pl.BlockSpec((tm, D), lambda i, j, k: (i, 0)),
                   o_shape=(s, D), o_dtype=f32, acc_shape=(tm, D))


def _ffn_dact(dffn, wd4, sil, ud, tm, dep=None):
    s = dffn.shape[0]
    dep_specs, dep_ops = _dep_args(dep, 2)

    def body(d_ref, w_ref, s_ref, ud_ref, *rest):
        dg_ref, du_ref = rest[-2:]
        d = d_ref[...]
        for cols in FSH_PARTS:
            dact = _dot(d, w_ref[cols, :], tb=True)
            dg_ref[:, cols] = (dact * ud_ref[:, cols].astype(f32)).astype(bf16)
            du_ref[:, cols] = (dact * s_ref[:, cols].astype(f32)).astype(bf16)

    blk = pl.BlockSpec((tm, FSH), lambda k, i: (i, k))
    return pl.pallas_call(
        body, name="ffn_dact", grid=(NSH, s // tm),
        in_specs=[pl.BlockSpec((tm, D), lambda k, i: (i, 0)), pl.BlockSpec((None, FSH, D), lambda k, i: (k, 0, 0)), blk, blk] + dep_specs,
        out_specs=[blk, blk], out_shape=[_sds((s, FFN), bf16), _sds((s, FFN), bf16)],
        compiler_params=_params(("parallel", "parallel")),
    )(dffn, wd4, sil, ud, *dep_ops)


def _ffn_dh(dgate, dup, wg4, wu4, tm, dep=None):
    s = dgate.shape[0]
    dep_specs, dep_ops = _dep_args(dep, 2)

    def body(dg_ref, du_ref, wg_ref, wu_ref, *rest):
        o_ref, acc_ref = rest[-2:]
        k = pl.program_id(1)
        p = _dot(dg_ref[...], wg_ref[...], tb=True) + _dot(du_ref[...], wu_ref[...], tb=True)

        @pl.when(k == 0)
        def _():
            acc_ref[...] = p

        @pl.when(jnp.logical_and(k > 0, k < NSH - 1))
        def _():
            acc_ref[...] += p

        @pl.when(k == NSH - 1)
        def _():
            o_ref[...] = acc_ref[...] + p

    aspec = pl.BlockSpec((tm, FSH), lambda i, k: (i, k))
    wspec = pl.BlockSpec((None, D, FSH), lambda i, k: (k, 0, 0))
    return pl.pallas_call(
        body, name="ffn_dh", grid=(s // tm, NSH), in_specs=[aspec, aspec, wspec, wspec] + dep_specs,
        out_specs=pl.BlockSpec((tm, D), lambda i, k: (i, 0)), out_shape=_sds((s, D), f32),
        scratch_shapes=[pltpu.VMEM((tm, D), f32)], compiler_params=_params(("parallel", "arbitrary")),
    )(dgate, dup, wg4, wu4, *dep_ops)


def _grad_cols4(name, h, dy, tm, tk):
    s = h.shape[0]
    return _matmul(name, h, dy, grid=(NSH, D // tm, s // tk), ta=True,
                   a_spec=pl.BlockSpec((tk, tm), lambda k, i, kk: (kk, i)),
                   b_spec=pl.BlockSpec((tk, FSH), lambda k, i, kk: (kk, k)),
                   o_spec=pl.BlockSpec((None, tm, FSH), lambda k, i, kk: (k, i, 0)),
                   o_shape=(NSH, D, FSH), o_dtype=bf16, acc_shape=(tm, FSH))


def _grad_wdown4(act, dffn, tn, tk):
    s = act.shape[0]
    return _matmul("grad_w_down", act, dffn, grid=(NSH, D // tn, s // tk), ta=True,
                   a_spec=pl.BlockSpec((tk, FSH), lambda k, j, kk: (kk, k)),
                   b_spec=pl.BlockSpec((tk, tn), lambda k, j, kk: (kk, j)),
                   o_spec=pl.BlockSpec((None, FSH, tn), lambda k, j, kk: (k, 0, j)),
                   o_shape=(NSH, FSH, D), o_dtype=bf16, acc_shape=(FSH, tn))


def _row_spec(w):
    return pl.BlockSpec((1, w), lambda i: (0, 0))


def _tile_spec(tm, w, col=0):
    return pl.BlockSpec((tm, w), lambda i: (i, col))


def _norm_mod(name, x, g, sc, sh, tm):
    s = x.shape[0]

    def body(x_ref, g_ref, sc_ref, sh_ref, o_ref):
        def strip(rows):
            xv = x_ref[rows, :]
            r = lax.rsqrt(jnp.mean(xv * xv, axis=-1, keepdims=True) + EPS)
            o_ref[rows, :] = (xv * r * g_ref[...] * (1.0 + sc_ref[...]) + sh_ref[...]).astype(bf16)

        _strips(tm, strip)

    return pl.pallas_call(
        body, name=name, grid=(s // tm,), in_specs=[_tile_spec(tm, D), _row_spec(D), _row_spec(D), _row_spec(D)],
        out_specs=_tile_spec(tm, D), out_shape=_sds((s, D), bf16), compiler_params=_params(("parallel",)),
    )(x, g, sc, sh)


def _resid_norm_mod(x, gt, mix, g, sc, sh, tm):
    s = x.shape[0]

    def body(x_ref, gt_ref, m_ref, g_ref, sc_ref, sh_ref, x2_ref, h_ref):
        def strip(rows):
            xv = x_ref[rows, :] + gt_ref[...] * m_ref[rows, :]
            x2_ref[rows, :] = xv
            r = lax.rsqrt(jnp.mean(xv * xv, axis=-1, keepdims=True) + EPS)
            h_ref[rows, :] = (xv * r * g_ref[...] * (1.0 + sc_ref[...]) + sh_ref[...]).astype(bf16)

        _strips(tm, strip)

    return pl.pallas_call(
        body, name="resid_norm_mod", grid=(s // tm,),
        in_specs=[_tile_spec(tm, D), _row_spec(D), _tile_spec(tm, D), _row_spec(D), _row_spec(D), _row_spec(D)],
        out_specs=[_tile_spec(tm, D), _tile_spec(tm, D)], out_shape=[_sds((s, D), f32), _sds((s, D), bf16)],
        compiler_params=_params(("parallel",)),
    )(x, gt, mix, g, sc, sh)


def _final_fwd_bwd(x2, ffn, gt2, g, tgt, tm):
    s = x2.shape[0]
    n = s // tm

    def body(x_ref, f_ref, gt_ref, g_ref, t_ref, dx_ref, df_ref, loss_ref, dg_ref, dgt_ref, a_loss, a_dg, a_dgt):
        i = pl.program_id(0)

        @pl.when(i == 0)
        def _():
            a_loss[...] = jnp.zeros_like(a_loss)
            a_dg[...] = jnp.zeros_like(a_dg)
            a_dgt[...] = jnp.zeros_like(a_dgt)

        def strip(rows):
            fv = f_ref[rows, :]
            gt = gt_ref[...]
            gv = g_ref[...]
            xv = x_ref[rows, :] + gt * fv
            r = lax.rsqrt(jnp.mean(xv * xv, axis=-1, keepdims=True) + EPS)
            xh = xv * r
            e = xh * gv - t_ref[rows, :]
            a_loss[...] += _fold8(e * e)
            dy = e * (1.0 / D)
            a_dg[...] += _fold8(dy * xh)
            t = dy * gv
            dx = r * (t - xh * jnp.mean(t * xh, axis=-1, keepdims=True))
            dx_ref[rows, :] = dx
            a_dgt[...] += _fold8(dx * fv)
            df_ref[rows, :] = (dx * gt).astype(bf16)

        _strips(tm, strip)

        @pl.when(i == n - 1)
        def _():
            tot = jnp.sum(jnp.sum(a_loss[...], axis=0, keepdims=True), axis=1, keepdims=True) * (0.5 / D)
            loss_ref[...] = jnp.broadcast_to(tot, (1, 128))
            dg_ref[...] = jnp.sum(a_dg[...], axis=0, keepdims=True)
            dgt_ref[...] = jnp.sum(a_dgt[...], axis=0, keepdims=True)

    return pl.pallas_call(
        body, name="final_fwd_bwd", grid=(n,),
        in_specs=[_tile_spec(tm, D), _tile_spec(tm, D), _row_spec(D), _row_spec(D), _tile_spec(tm, D)],
        out_specs=[_tile_spec(tm, D), _tile_spec(tm, D), _row_spec(128), _row_spec(D), _row_spec(D)],
        out_shape=[_sds((s, D), f32), _sds((s, D), bf16), _sds((1, 128), f32), _sds((1, D), f32), _sds((1, D), f32)],
        scratch_shapes=[pltpu.VMEM((8, D), f32)] * 3, compiler_params=_params(("arbitrary",)),
    )(x2, ffn, gt2, g, tgt)


def _norm_mod_bwd(name, dh, xin, g, sc, dres, tm, mix=None, gt=None):
    s = dh.shape[0]
    n = s // tm
    with_mix = mix is not None

    def body(*refs):
        if with_mix:
            dh_ref, x_ref, g_ref, sc_ref, dr_ref, m_ref, gt_ref, dx_ref, dm_ref, dsc_ref, dsh_ref, dg_ref, dgt_ref, a_sc, a_sh, a_g, a_gt = refs
        else:
            dh_ref, x_ref, g_ref, sc_ref, dr_ref, dx_ref, dsc_ref, dsh_ref, dg_ref, a_sc, a_sh, a_g = refs
        i = pl.program_id(0)

        @pl.when(i == 0)
        def _():
            a_sc[...] = jnp.zeros_like(a_sc)
            a_sh[...] = jnp.zeros_like(a_sh)
            a_g[...] = jnp.zeros_like(a_g)
            if with_mix:
                a_gt[...] = jnp.zeros_like(a_gt)

        def strip(rows):
            dh = dh_ref[rows, :]
            xv = x_ref[rows, :]
            gv = g_ref[...]
            r = lax.rsqrt(jnp.mean(xv * xv, axis=-1, keepdims=True) + EPS)
            xh = xv * r
            a_sc[...] += _fold8(dh * xh * gv)
            a_sh[...] += _fold8(dh)
            dn = dh * (1.0 + sc_ref[...])
            a_g[...] += _fold8(dn * xh)
            t = dn * gv
            dx = dr_ref[rows, :] + r * (t - xh * jnp.mean(t * xh, axis=-1, keepdims=True))
            dx_ref[rows, :] = dx
            if with_mix:
                a_gt[...] += _fold8(dx * m_ref[rows, :])
                dm_ref[rows, :] = (dx * gt_ref[...]).astype(bf16)

        _strips(tm, strip)

        @pl.when(i == n - 1)
        def _():
            dsc_ref[...] = jnp.sum(a_sc[...], axis=0, keepdims=True)
            dsh_ref[...] = jnp.sum(a_sh[...], axis=0, keepdims=True)
            dg_ref[...] = jnp.sum(a_g[...], axis=0, keepdims=True)
            if with_mix:
                dgt_ref[...] = jnp.sum(a_gt[...], axis=0, keepdims=True)

    tile, row = _tile_spec(tm, D), _row_spec(D)
    if with_mix:
        ins, args = [tile, tile, row, row, tile, tile, row], (dh, xin, g, sc, dres, mix, gt)
        outs = [tile, tile, row, row, row, row]
        shapes = [_sds((s, D), f32), _sds((s, D), bf16)] + [_sds((1, D), f32)] * 4
        nacc = 4
    else:
        ins, args = [tile, tile, row, row, tile], (dh, xin, g, sc, dres)
        outs = [tile, row, row, row]
        shapes = [_sds((s, D), f32)] + [_sds((1, D), f32)] * 3
        nacc = 3
    return pl.pallas_call(
        body, name=name, grid=(n,), in_specs=ins, out_specs=outs, out_shape=shapes,
        scratch_shapes=[pltpu.VMEM((8, D), f32)] * nacc, compiler_params=_params(("arbitrary",)),
    )(*args)


def _mix_pre(att, y, proj2, g_att, g_ssd, tm):
    s = att.shape[0]

    def body(a_ref, y_ref, z_ref, ga_ref, gs_ref, o_ref):
        def strip(rows):
            a = a_ref[rows, :]
            ra = lax.rsqrt(jnp.mean(a * a, axis=-1, keepdims=True) + EPS)
            o_ref[rows, 0:ATT_W] = (a * ra * ga_ref[...]).astype(bf16)
            z = z_ref[rows, :]
            u = y_ref[rows, :] * (z * _sigmoid(z))
            ru = lax.rsqrt(jnp.mean(u * u, axis=-1, keepdims=True) + EPS)
            o_ref[rows, ATT_W:] = (u * ru * gs_ref[...]).astype(bf16)

        _strips(tm, strip)

    t = _tile_spec(tm, ATT_W)
    return pl.pallas_call(
        body, name="mix_pre", grid=(s // tm,), in_specs=[t, t, t, _row_spec(ATT_W), _row_spec(SSD_W)],
        out_specs=_tile_spec(tm, D), out_shape=_sds((s, D), bf16), compiler_params=_params(("parallel",)),
    )(att, y, proj2, g_att, g_ssd)


def _mix_pre_bwd(dmc, att, y, proj2, g_att, g_ssd, tm):
    s = att.shape[0]
    n = s // tm

    def body(da_ref, ds_ref, a_ref, y_ref, z_ref, ga_ref, gs_ref, datt_ref, dy_ref, dz_ref, dga_ref, dgs_ref, acc_a, acc_s):
        i = pl.program_id(0)

        @pl.when(i == 0)
        def _():
            acc_a[...] = jnp.zeros_like(acc_a)
            acc_s[...] = jnp.zeros_like(acc_s)

        def strip(rows):
            a = a_ref[rows, :]
            ra = lax.rsqrt(jnp.mean(a * a, axis=-1, keepdims=True) + EPS)
            ah = a * ra
            dan = da_ref[rows, :]
            acc_a[...] += _fold8(dan * ah)
            t = dan * ga_ref[...]
            datt_ref[rows, :] = (ra * (t - ah * jnp.mean(t * ah, axis=-1, keepdims=True))).astype(bf16)
            z = z_ref[rows, :]
            yv = y_ref[rows, :]
            sz = _sigmoid(z)
            sil = z * sz
            u = yv * sil
            ru = lax.rsqrt(jnp.mean(u * u, axis=-1, keepdims=True) + EPS)
            uh = u * ru
            dsn = ds_ref[rows, :]
            acc_s[...] += _fold8(dsn * uh)
            t2 = dsn * gs_ref[...]
            du = ru * (t2 - uh * jnp.mean(t2 * uh, axis=-1, keepdims=True))
            dy_ref[rows, :] = du * sil
            dz_ref[rows, :] = (du * yv * (sz * (1.0 + z * (1.0 - sz)))).astype(bf16)

        _strips(tm, strip)

        @pl.when(i == n - 1)
        def _():
            dga_ref[...] = jnp.sum(acc_a[...], axis=0, keepdims=True)
            dgs_ref[...] = jnp.sum(acc_s[...], axis=0, keepdims=True)

    t = _tile_spec(tm, ATT_W)
    row = _row_spec(ATT_W)
    return pl.pallas_call(
        body, name="mix_pre_bwd", grid=(n,),
        in_specs=[_tile_spec(tm, ATT_W, 0), _tile_spec(tm, ATT_W, 1), t, t, t, row, row],
        out_specs=[t, t, t, row, row],
        out_shape=[_sds((s, ATT_W), bf16), _sds((s, SSD_W), f32), _sds((s, SSD_W), bf16), _sds((1, ATT_W), f32), _sds((1, SSD_W), f32)],
        scratch_shapes=[pltpu.VMEM((8, ATT_W), f32)] * 2, compiler_params=_params(("arbitrary",)),
    )(dmc, dmc, att, y, proj2, g_att, g_ssd)


ATT_GROUP = 8
ATT_GROUP_FWD = 8


def _pair_rows(qc):
    two = jnp.concatenate([qc, qc], axis=0)
    r = lax.broadcasted_iota(jnp.int32, (2 * CHUNK, 128), 0)
    l = lax.broadcasted_iota(jnp.int32, (2 * CHUNK, 128), 1)
    return jnp.where((r < CHUNK) == (l < HD), two, jnp.zeros_like(two))


def _scaled(q):
    return q * jnp.asarray(HD ** -0.5, q.dtype)


def _pair_scores(wt, kb, bias, r0, masked):
    sc = lax.dot_general(wt, kb, (((1,), (1,)), ((), ())), preferred_element_type=f32) + bias
    if not masked:
        return sc
    kidx = lax.broadcasted_iota(jnp.int32, sc.shape, 1)
    return jnp.where(r0 + kidx >= PADK, sc, -jnp.inf)


def _softmax(sc, axis):
    e = jnp.exp(sc - jnp.max(sc, axis=axis, keepdims=True))
    return e * (1.0 / jnp.sum(e, axis=axis, keepdims=True))


def _chunk_loops(nc, group, per_trip):
    n_masked = min(-(-LEFT // per_trip), nc // per_trip)

    def run(masked):
        def step(g, carry):
            group(g, masked)
            return carry
        return step

    lax.fori_loop(0, n_masked, run(True), 0)
    lax.fori_loop(n_masked, nc // per_trip, run(False), 0)


def _pair_diag(r):
    lane = lax.broadcasted_iota(jnp.int32, (CHUNK, 128), 1)
    return jnp.where(lane < HD, r[0:CHUNK], r[CHUNK:])


def _pad_keys(k_ref, kp, s):
    kp[0:PADK, :] = jnp.zeros((PADK, 128), bf16)
    kp[PADK:PADK + s, :] = k_ref[...]
    kp[PADK + s:, :] = jnp.zeros((CHUNK, 128), bf16)


def _attn_fwd(qkv, bias2):
    s = qkv.shape[0]
    nc = s // CHUNK
    npair = NH // 2

    def body(q_ref, k_ref, v_ref, b_ref, o_ref, kp, vp):
        _pad_keys(k_ref, kp, s)
        _pad_keys(v_ref, vp, s)

        def group(g, masked):
            r0s = [pl.multiple_of((g * ATT_GROUP_FWD + u) * CHUNK, CHUNK) for u in range(ATT_GROUP_FWD)]
            scs = [_pair_scores(_pair_rows(_scaled(q_ref[pl.ds(r0, CHUNK), :])), kp[pl.ds(r0, BANDP), :], b_ref[...], r0, masked)
                   for r0 in r0s]
            ps = [_softmax(sc, -1).astype(bf16) for sc in scs]
            for r0, p in zip(r0s, ps):
                o_ref[pl.ds(r0, CHUNK), :] = _pair_diag(jnp.dot(p, vp[pl.ds(r0, BANDP), :], preferred_element_type=f32))

        _chunk_loops(nc, group, ATT_GROUP_FWD)

    return pl.pallas_call(
        body, name="attn_fwd", grid=(npair,),
        in_specs=[pl.BlockSpec((s, 128), lambda p: (0, p)), pl.BlockSpec((s, 128), lambda p: (0, npair + p)),
                  pl.BlockSpec((s, 128), lambda p: (0, 2 * npair + p)), pl.BlockSpec((None, 2 * CHUNK, BANDP), lambda p: (p, 0, 0))],
        out_specs=pl.BlockSpec((s, 128), lambda p: (0, p)), out_shape=_sds((s, ATT_W), f32),
        scratch_shapes=[pltpu.VMEM((PADK + s + CHUNK, 128), bf16)] * 2, compiler_params=_params(("parallel",)),
    )(qkv, qkv, qkv, bias2)


def _attn_bwd(qkv, datt, bias2):
    s = qkv.shape[0]
    nc = s // CHUNK
    npair = NH // 2
    rows = PADK + s + CHUNK
    nt = (((1,), (1,)), ((), ()))

    def body(q_ref, k_ref, v_ref, do_ref, b_ref, dq_ref, dk_ref, dv_ref, g_ref, kp, vp, dkp, dvp):
        _pad_keys(k_ref, kp, s)
        _pad_keys(v_ref, vp, s)
        dkp[...] = jnp.zeros_like(dkp)
        dvp[...] = jnp.zeros_like(dvp)
        g_ref[...] = jnp.zeros_like(g_ref)

        def group(g, masked):
            r0s = [pl.multiple_of((g * ATT_GROUP + u) * CHUNK, CHUNK) for u in range(ATT_GROUP)]
            wts = [_pair_rows(_scaled(q_ref[pl.ds(r0, CHUNK), :])) for r0 in r0s]
            dos = [_pair_rows(do_ref[pl.ds(r0, CHUNK), :]) for r0 in r0s]
            scs = [_pair_scores(wt, kp[pl.ds(r0, BANDP), :], b_ref[...], r0, masked) for wt, r0 in zip(wts, r0s)]
            dps = [lax.dot_general(do, vp[pl.ds(r0, BANDP), :], nt, preferred_element_type=f32) for do, r0 in zip(dos, r0s)]
            tn_ = (((0,), (0,)), ((), ()))
            for r0, wt, do, sc, dp in zip(r0s, wts, dos, scs, dps):
                p = _softmax(sc, -1)
                ds = p * (dp - jnp.sum(p * dp, axis=-1, keepdims=True))
                g_ref[...] += ds
                dsb = ds.astype(bf16)
                dq = jnp.dot(dsb, kp[pl.ds(r0, BANDP), :], preferred_element_type=f32)
                dq_ref[pl.ds(r0, CHUNK), :] = (_pair_diag(dq) * (HD ** -0.5)).astype(bf16)
                dkp[pl.ds(r0, BANDP), :] += lax.dot_general(dsb, wt, tn_, preferred_element_type=f32)
                dvp[pl.ds(r0, BANDP), :] += lax.dot_general(p.astype(bf16), do, tn_, preferred_element_type=f32)

        _chunk_loops(nc, group, ATT_GROUP)
        dk_ref[...] = dkp[PADK:PADK + s, :].astype(bf16)
        dv_ref[...] = dvp[PADK:PADK + s, :].astype(bf16)

    col = lambda off: pl.BlockSpec((s, 128), lambda p: (0, off + p))
    return pl.pallas_call(
        body, name="attn_bwd", grid=(npair,),
        in_specs=[col(0), col(npair), col(2 * npair), col(0), pl.BlockSpec((None, 2 * CHUNK, BANDP), lambda p: (p, 0, 0))],
        out_specs=[col(0), col(0), col(0), pl.BlockSpec((None, 2 * CHUNK, BANDP), lambda p: (p, 0, 0))],
        out_shape=[_sds((s, ATT_W), bf16)] * 3 + [_sds((npair, 2 * CHUNK, BANDP), f32)],
        scratch_shapes=[pltpu.VMEM((rows, 128), bf16)] * 2 + [pltpu.VMEM((rows, 128), f32)] * 2,
        compiler_params=_params(("parallel",)),
    )(qkv, qkv, qkv, datt, bias2)


def _rel_tables():
    onehot = np.zeros((BANDP, N_REL), np.float32)
    for j in range(BAND + CHUNK - 1):
        o = j - (CHUNK - 1)
        onehot[j, int(np.clip(PADK - o, -(CHUNK - 1), REL_CLIP)) + CHUNK - 1] = 1.0
    return onehot, np.ascontiguousarray(np.eye(CHUNK, dtype=np.float32)[::-1])


def _expand_bias(rel):
    ext = jnp.concatenate([jnp.broadcast_to(rel[:, N_REL - 1:], (NH, N_REL - 1)), rel[:, ::-1],
                           jnp.zeros((NH, BANDP - BAND + 1), f32)], axis=1)
    band = jnp.stack([ext[:, CHUNK - 1 - q:CHUNK - 1 - q + BANDP] for q in range(CHUNK)], axis=1)
    band = jnp.where(np.arange(BANDP) < BAND, band, -jnp.inf)
    return band.reshape(NH // 2, 2 * CHUNK, BANDP)


def _rel_bias_grad(gband):
    def body(g_ref, m_ref, flip_ref, o_ref, d2):
        for h in range(NH):
            rev = jnp.dot(flip_ref[...], g_ref[h], precision=HIGHEST, preferred_element_type=f32)
            rolled = pltpu.roll(rev, 0, 1, stride=1, stride_axis=0)
            d2[h:h + 1, :] = jnp.sum(rolled, axis=0, keepdims=True)
        o_ref[...] = jnp.dot(d2[...], m_ref[...], precision=HIGHEST, preferred_element_type=f32)

    onehot, flip = _rel_tables()
    return pl.pallas_call(
        body, name="rel_bias_grad", out_shape=_sds((NH, N_REL), f32), scratch_shapes=[pltpu.VMEM((NH, BANDP), f32)],
    )(gband, jnp.asarray(onehot), jnp.asarray(flip))


XBC_BLK = 512
XBC_COL0 = SSD_W // XBC_BLK
DT_COL = (SSD_W + XBC) // 128


def _conv_taps(ext, w_ref, b_ref, tm):
    n = ext.shape[0]
    pre = w_ref[3:4, :] * ext + b_ref[...]
    for j in range(3):
        pre = pre + w_ref[j:j + 1, :] * pltpu.roll(ext, 3 - j, 0)
    return pre


def _ssd_conv(proj2, conv_w, conv_b, tm):
    s = proj2.shape[0]
    nb = XBC // XBC_BLK

    def body(x_ref, p_ref, w_ref, b_ref, o_ref):
        i = pl.program_id(1)
        prev = jnp.where(i > 0, p_ref[...], 0.0)
        ext = jnp.concatenate([prev, x_ref[...]], axis=0)
        pre = _conv_taps(ext, w_ref, b_ref, tm)[8:8 + tm]
        o_ref[...] = pre * _sigmoid(pre)

    return pl.pallas_call(
        body, name="ssd_conv", grid=(nb, s // tm),
        in_specs=[pl.BlockSpec((tm, XBC_BLK), lambda j, i: (i, XBC_COL0 + j)),
                  pl.BlockSpec((8, XBC_BLK), lambda j, i: (jnp.maximum(i * (tm // 8) - 1, 0), XBC_COL0 + j)),
                  pl.BlockSpec((4, XBC_BLK), lambda j, i: (0, j)), pl.BlockSpec((1, XBC_BLK), lambda j, i: (0, j))],
        out_specs=pl.BlockSpec((tm, XBC_BLK), lambda j, i: (i, j)), out_shape=_sds((s, XBC), f32),
        compiler_params=_params(("parallel", "parallel")),
    )(proj2, proj2, conv_w, conv_b)


def _ssd_conv_bwd(dxbc, proj2, conv_w, conv_b, tm):
    s = proj2.shape[0]
    nb = XBC // XBC_BLK
    n = s // tm
    last8 = s // 8 - 1

    def body(x_ref, xp_ref, xn_ref, d_ref, dn_ref, w_ref, b_ref, o_ref, dw_ref, db_ref):
        i = pl.program_id(1)

        @pl.when(i == 0)
        def _():
            dw_ref[...] = jnp.zeros_like(dw_ref)
            db_ref[...] = jnp.zeros_like(db_ref)

        prev = jnp.where(i > 0, xp_ref[...], 0.0)
        ext = jnp.concatenate([prev, x_ref[...], xn_ref[...]], axis=0)
        pre = _conv_taps(ext, w_ref, b_ref, tm)
        sg = _sigmoid(pre)
        dnext = jnp.where(i < n - 1, dn_ref[...], 0.0)
        dext = jnp.concatenate([jnp.zeros((8, XBC_BLK), f32), d_ref[...], dnext], axis=0)
        dpre = dext * (sg * (1.0 + pre * (1.0 - sg)))
        rows = tm + 16
        dx = w_ref[3:4, :] * dpre
        for j in range(3):
            dx = dx + w_ref[j:j + 1, :] * pltpu.roll(dpre, rows - (3 - j), 0)
        o_ref[...] = dx[8:8 + tm].astype(bf16)
        dcur = dpre[8:8 + tm]
        db_ref[...] += jnp.sum(dcur, axis=0, keepdims=True)
        dw_ref[3:4, :] += jnp.sum(dcur * ext[8:8 + tm], axis=0, keepdims=True)
        for j in range(3):
            dw_ref[j:j + 1, :] += jnp.sum(dcur * pltpu.roll(ext, 3 - j, 0)[8:8 + tm], axis=0, keepdims=True)

    xcol = lambda j: XBC_COL0 + j
    return pl.pallas_call(
        body, name="ssd_conv_bwd", grid=(nb, n),
        in_specs=[pl.BlockSpec((tm, XBC_BLK), lambda j, i: (i, xcol(j))),
                  pl.BlockSpec((8, XBC_BLK), lambda j, i: (jnp.maximum(i * (tm // 8) - 1, 0), xcol(j))),
                  pl.BlockSpec((8, XBC_BLK), lambda j, i: (jnp.minimum((i + 1) * (tm // 8), last8), xcol(j))),
                  pl.BlockSpec((tm, XBC_BLK), lambda j, i: (i, j)),
                  pl.BlockSpec((8, XBC_BLK), lambda j, i: (jnp.minimum((i + 1) * (tm // 8), last8), j)),
                  pl.BlockSpec((4, XBC_BLK), lambda j, i: (0, j)), pl.BlockSpec((1, XBC_BLK), lambda j, i: (0, j))],
        out_specs=[pl.BlockSpec((tm, XBC_BLK), lambda j, i: (i, j)), pl.BlockSpec((4, XBC_BLK), lambda j, i: (0, j)),
                   pl.BlockSpec((1, XBC_BLK), lambda j, i: (0, j))],
        out_shape=[_sds((s, XBC), bf16), _sds((4, XBC), f32), _sds((1, XBC), f32)],
        compiler_params=_params(("parallel", "arbitrary")),
    )(proj2, proj2, proj2, dxbc, dxbc, conv_w, conv_b)


def _ssd_consts():
    ex = np.zeros((128, SSD_W), np.float32)
    for h in range(NH):
        ex[h, h * HD:(h + 1) * HD] = 1.0
    sel = np.zeros((8, 128), np.float32)
    for h in range(NH):
        sel[h // 2, h] = 1.0
    par = np.zeros((128, 128), np.float32)
    for r in range(128):
        for h in range(NH):
            par[r, h] = 1.0 if (h % 2) == (r // 64) else 0.0
    ones_blk = np.zeros((128, 128), np.float32)
    for r in range(128):
        ones_blk[r, (r // 64) * 64:(r // 64) * 64 + 64] = 1.0
    return ex, np.ascontiguousarray(ex.T), sel, par, ones_blk


SSD_SUB = 8


def _ssd_common(rs, xbc_ref, dtr_ref, a_ref, dtb_ref, ex_ref, sel_ref, par_ref):
    xs = xbc_ref[rs, 0:SSD_W]
    dt = _softplus(dtr_ref[rs, :] + dtb_ref[...])
    adt = dt * a_ref[...]
    r_i = lax.broadcasted_iota(jnp.int32, (CHUNK, CHUNK), 0)
    c_i = lax.broadcasted_iota(jnp.int32, (CHUNK, CHUNK), 1)
    tril = (r_i >= c_i).astype(f32)
    cs = _dot01(tril, adt, exact="a")
    cs2 = jnp.concatenate([cs, cs], axis=0) * par_ref[...]
    cstp = _dot01(sel_ref[...], cs2, tb=True, exact="a")
    both = _dot01(jnp.concatenate([dt, cs], axis=0), ex_ref[...])
    return xs, dt, cs, cstp, both[0:CHUNK], both[CHUNK:]


def _pair_mask():
    l_i = lax.broadcasted_iota(jnp.int32, (CHUNK, 128), 0)
    lane = lax.broadcasted_iota(jnp.int32, (CHUNK, 128), 1)
    return l_i >= (lane % CHUNK), lane < HD


def _block_diag(xp, first):
    z = jnp.zeros_like(xp)
    return jnp.concatenate([jnp.where(first, xp, z), jnp.where(first, z, xp)], axis=0)


def _ssd_fwd(xbc, proj2, a_row, dtb_row, dsk_full):
    s = xbc.shape[0]
    nc = s // CHUNK
    ex, ext, sel, par, ones_blk = _ssd_consts()

    def one_chunk(sub, states, refs):
        xbc_ref, dtr_ref, a_ref, dtb_ref, dsk_ref, ex_ref, sel_ref, par_ref, y_ref, hs_ref = refs
        rs = slice(sub * CHUNK, (sub + 1) * CHUNK)
        xs, dt, cs, cstp, dt_full, cs_full = _ssd_common(rs, xbc_ref, dtr_ref, a_ref, dtb_ref, ex_ref, sel_ref, par_ref)
        cs_last = cs_full[CHUNK - 1:CHUNK, :]
        xdt = xs * dt_full
        causal, first = _pair_mask()
        out = []
        for g in range(NG):
            gl = slice(g * GW, (g + 1) * GW)
            bg = xbc_ref[rs, SSD_W + g * NSTATE:SSD_W + (g + 1) * NSTATE].astype(bf16)
            cg = xbc_ref[rs, SSD_W + NG * NSTATE + g * NSTATE:SSD_W + NG * NSTATE + (g + 1) * NSTATE].astype(bf16)
            cb2 = lax.dot_general(cg, jnp.concatenate([bg, bg], axis=0), (((1,), (1,)), ((), ())), preferred_element_type=f32)
            hg = states[g]
            hs_ref[sub, g] = hg
            y0 = jnp.dot(cg, hg.astype(bf16), preferred_element_type=f32)
            yoff = jnp.exp(cs_full[:, gl]) * y0
            for j in range(GW // 128):
                pair = g * (GW // 128) + j
                pl_ = slice(pair * 128, (pair + 1) * 128)
                seg = jnp.exp(jnp.where(causal, cs_full[:, pl_] - cstp[pair:pair + 1, :], -jnp.inf))
                m = (cb2 * seg).astype(bf16)
                yd = jnp.dot(m, _block_diag(xdt[:, pl_].astype(bf16), first), preferred_element_type=f32)
                y_ref[rs, pl_] = yd + yoff[:, j * 128:(j + 1) * 128] + xs[:, pl_] * dsk_ref[:, pl_]
            xdec = (xdt[:, gl] * jnp.exp(cs_last[:, gl] - cs_full[:, gl])).astype(bf16)
            st = lax.dot_general(bg, xdec, (((0,), (0,)), ((), ())), preferred_element_type=f32)
            out.append(jnp.exp(cs_last[:, gl]) * hg + st)
        return out

    def body(*refs):
        hst = refs[-1]

        @pl.when(pl.program_id(0) == 0)
        def _():
            hst[...] = jnp.zeros_like(hst)

        states = [hst[g] for g in range(NG)]
        for sub in range(SSD_SUB):
            states = one_chunk(sub, states, refs[:-1])
        for g in range(NG):
            hst[g] = states[g]

    rows = SSD_SUB * CHUNK
    const = lambda shape: pl.BlockSpec(shape, lambda c: tuple(0 for _ in shape))
    return pl.pallas_call(
        body, name="ssd_fwd", grid=(nc // SSD_SUB,),
        in_specs=[pl.BlockSpec((rows, XBC), lambda c: (c, 0)), pl.BlockSpec((rows, 128), lambda c: (c, DT_COL)),
                  const((1, 128)), const((1, 128)), const((1, SSD_W)), const((128, SSD_W)), const((8, 128)), const((128, 128))],
        out_specs=[pl.BlockSpec((rows, SSD_W), lambda c: (c, 0)), pl.BlockSpec((SSD_SUB, NG, NSTATE, GW), lambda c: (c, 0, 0, 0))],
        out_shape=[_sds((s, SSD_W), f32), _sds((nc, NG, NSTATE, GW), f32)],
        scratch_shapes=[pltpu.VMEM((NG, NSTATE, GW), f32)], compiler_params=_params(("arbitrary",)),
    )(xbc, proj2, a_row, dtb_row, dsk_full, jnp.asarray(ex), jnp.asarray(sel), jnp.asarray(par))


def _ssd_bwd(xbc, proj2, dy, hsave, a_row, dtb_row, dsk_full):
    s = xbc.shape[0]
    nc = s // CHUNK
    ex, ext, sel, par, ones_blk = _ssd_consts()

    def one_chunk(sub, dhs, refs):
        (xbc_ref, dtr_ref, dy_ref, hs_ref, a_ref, dtb_ref, dsk_ref, ex_ref, ext_ref, sel_ref, par_ref, ob_ref,
         dxbc_ref, ddtr_ref, dd_ref, da_ref, ddtb_ref, dh, a_dd, a_da, a_dtb, dcs_lane, dcs_b, dxdt) = refs
        rs = slice(sub * CHUNK, (sub + 1) * CHUNK)
        dcs_lane, dcs_b, dxdt = dcs_lane.at[sub], dcs_b.at[sub], dxdt.at[sub]
        xs, dt, cs, cstp, dt_full, cs_full = _ssd_common(rs, xbc_ref, dtr_ref, a_ref, dtb_ref, ex_ref, sel_ref, par_ref)
        cs_last = cs_full[CHUNK - 1:CHUNK, :]
        xdt = xs * dt_full
        dyv = dy_ref[rs, :]
        a_dd[...] += _fold8(dyv * xs)
        causal, first = _pair_mask()
        diag = lax.broadcasted_iota(jnp.int32, (CHUNK, 128), 0) == lax.broadcasted_iota(jnp.int32, (CHUNK, 128), 1) % CHUNK
        dh_out = []
        for g in range(NG):
            gl = slice(g * GW, (g + 1) * GW)
            bcol = slice(SSD_W + g * NSTATE, SSD_W + (g + 1) * NSTATE)
            ccol = slice(SSD_W + NG * NSTATE + g * NSTATE, SSD_W + NG * NSTATE + (g + 1) * NSTATE)
            bg = xbc_ref[rs, bcol].astype(bf16)
            cg = xbc_ref[rs, ccol].astype(bf16)
            bg2 = jnp.concatenate([bg, bg], axis=0)
            cb2 = lax.dot_general(cg, bg2, (((1,), (1,)), ((), ())), preferred_element_type=f32)
            hg = hs_ref[sub, g]
            hgb = hg.astype(bf16)
            dhg = dhs[g]
            dhgb = dhg.astype(bf16)
            eg = jnp.exp(cs_full[:, gl])
            dec = jnp.exp(cs_last[:, gl] - cs_full[:, gl])
            gam = jnp.exp(cs_last[:, gl])
            dyg = dyv[:, gl]
            xdt_g = xdt[:, gl]
            y0 = jnp.dot(cg, hgb, preferred_element_type=f32)
            dy0 = (eg * dyg).astype(bf16)
            dcm = lax.dot_general(dy0, hgb, (((1,), (1,)), ((), ())), preferred_element_type=f32)
            dh_prev = gam * dhg + lax.dot_general(cg, dy0, (((0,), (0,)), ((), ())), preferred_element_type=f32)
            dgam = jnp.sum(dhg * hg, axis=0, keepdims=True) * gam
            dxdec = jnp.dot(bg, dhgb, preferred_element_type=f32)
            dbm = lax.dot_general((xdt_g * dec).astype(bf16), dhgb, (((1,), (1,)), ((), ())), preferred_element_type=f32)
            t = dxdec * xdt_g * dec
            dcs_lane[:, gl] = dyg * eg * y0 - t
            dcs_lane[CHUNK - 1:CHUNK, gl] += jnp.sum(t, axis=0, keepdims=True) + dgam
            dxdt[:, gl] = dxdec * dec
            dcb2 = jnp.zeros((CHUNK, 128), f32)
            for j in range(GW // 128):
                pair = g * (GW // 128) + j
                pl_ = slice(pair * 128, (pair + 1) * 128)
                seg = jnp.exp(jnp.where(causal, cs_full[:, pl_] - cstp[pair:pair + 1, :], -jnp.inf))
                m = cb2 * seg
                mb = m.astype(bf16)
                rhs = _block_diag(xdt[:, pl_].astype(bf16), first)
                dyp = dyv[:, pl_].astype(bf16)
                dm = lax.dot_general(dyp, rhs, (((1,), (1,)), ((), ())), preferred_element_type=f32)
                tt = lax.dot_general(mb, dyp, (((0,), (0,)), ((), ())), preferred_element_type=f32)
                dxdt[:, pl_] += jnp.where(first, tt[0:CHUNK], tt[CHUNK:])
                dcb2 = dcb2 + dm * seg
                w = dm * m
                colsum = jnp.sum(w, axis=0, keepdims=True)
                dcs_b[:, pl_] = _dot01(w - jnp.where(diag, colsum, 0.0), ob_ref[...])
            dcb2b = dcb2.astype(bf16)
            dcm = dcm + jnp.dot(dcb2b, bg2, preferred_element_type=f32)
            t3 = lax.dot_general(dcb2b, cg, (((0,), (0,)), ((), ())), preferred_element_type=f32)
            dxbc_ref[rs, bcol] = dbm + t3[0:CHUNK] + t3[CHUNK:]
            dxbc_ref[rs, ccol] = dcm
            dh_out.append(dh_prev)
        dxdtv = dxdt[...]
        both = _dot01(jnp.concatenate([dcs_lane[...] + dcs_b[...] * (1.0 / HD), dxdtv * xs], axis=0), ext_ref[...])
        dcs = both[0:CHUNK]
        r_i = lax.broadcasted_iota(jnp.int32, (CHUNK, CHUNK), 0)
        c_i = lax.broadcasted_iota(jnp.int32, (CHUNK, CHUNK), 1)
        triu = (r_i <= c_i).astype(f32)
        da_ = _dot01(triu, dcs, exact="a")
        ddt = da_ * a_ref[...] + both[CHUNK:]
        a_da[...] += _fold8(da_ * dt)
        dxbc_ref[rs, 0:SSD_W] = dyv * dsk_ref[...] + dxdtv * dt_full
        ddtr = ddt * _sigmoid(dtr_ref[rs, :] + dtb_ref[...])
        ddtr_ref[rs, :] = ddtr
        a_dtb[...] += _fold8(ddtr)
        return dh_out

    nsteps = nc // SSD_SUB

    def body(*refs):
        dd_ref, da_ref, ddtb_ref, dh, a_dd, a_da, a_dtb = refs[14:21]
        ext_ref = refs[8]
        step = pl.program_id(0)

        @pl.when(step == 0)
        def _():
            dh[...] = jnp.zeros_like(dh)
            a_dd[...] = jnp.zeros_like(a_dd)
            a_da[...] = jnp.zeros_like(a_da)
            a_dtb[...] = jnp.zeros_like(a_dtb)

        dhs = [dh[g] for g in range(NG)]
        for sub in reversed(range(SSD_SUB)):
            dhs = one_chunk(sub, dhs, refs)
        for g in range(NG):
            dh[g] = dhs[g]

        @pl.when(step == nsteps - 1)
        def _():
            dd_ref[...] = jnp.sum(jnp.dot(a_dd[...], ext_ref[...], precision=HIGHEST, preferred_element_type=f32), axis=0, keepdims=True)
            da_ref[...] = jnp.sum(a_da[...], axis=0, keepdims=True)
            ddtb_ref[...] = jnp.sum(a_dtb[...], axis=0, keepdims=True)

    rev = lambda c: nsteps - 1 - c
    rows = SSD_SUB * CHUNK
    const = lambda shape: pl.BlockSpec(shape, lambda c: tuple(0 for _ in shape))
    return pl.pallas_call(
        body, name="ssd_bwd", grid=(nsteps,),
        in_specs=[pl.BlockSpec((rows, XBC), lambda c: (rev(c), 0)), pl.BlockSpec((rows, 128), lambda c: (rev(c), DT_COL)),
                  pl.BlockSpec((rows, SSD_W), lambda c: (rev(c), 0)), pl.BlockSpec((SSD_SUB, NG, NSTATE, GW), lambda c: (rev(c), 0, 0, 0)),
                  const((1, 128)), const((1, 128)), const((1, SSD_W)), const((128, SSD_W)), const((SSD_W, 128)),
                  const((8, 128)), const((128, 128)), const((128, 128))],
        out_specs=[pl.BlockSpec((rows, XBC), lambda c: (rev(c), 0)), pl.BlockSpec((rows, 128), lambda c: (rev(c), 0)),
                   const((1, 128)), const((1, 128)), const((1, 128))],
        out_shape=[_sds((s, XBC), f32), _sds((s, 128), f32), _sds((1, 128), f32), _sds((1, 128), f32), _sds((1, 128), f32)],
        scratch_shapes=[pltpu.VMEM((NG, NSTATE, GW), f32), pltpu.VMEM((8, SSD_W), f32), pltpu.VMEM((8, 128), f32), pltpu.VMEM((8, 128), f32)]
        + [pltpu.VMEM((SSD_SUB, CHUNK, SSD_W), f32)] * 3,
        compiler_params=_params(("arbitrary",)),
    )(xbc, proj2, dy, hsave, a_row, dtb_row, dsk_full, jnp.asarray(ex), jnp.asarray(ext), jnp.asarray(sel), jnp.asarray(par),
      jnp.asarray(ones_blk))


def _local_step(x, tgt, mods, g_mix, rel, conv_w, conv_b, dt_bias, a_log, d_skip, g_att, g_ssd, g_ffn, g_final, weights):
    s = x.shape[0]
    tm_e = 512 if s % 512 == 0 else s
    tm_m = 512 if s % 512 == 0 else s
    tm_l = 1024 if s % 1024 == 0 else s
    tk = 2048 if s % 2048 == 0 else s
    sh1, sc1, gt1, sh2, sc2, gt2 = [mods[:, i * D:(i + 1) * D] for i in range(6)]

    h1b = _norm_mod("norm_mod_1", x, g_mix, sc1, sh1, tm_e)
    win, win_b = weights.w_in(h1b)
    qkv = _mm_nn_fullk("proj_qkv", h1b, win, tm_l, 1536, bf16, n=IN_A)
    proj2 = _mm_nn_fullk("proj_zxbcdt", h1b, win_b, tm_l, 896, f32)
    bias = _expand_bias(rel)
    att = _attn_fwd(qkv, bias)
    xbc = _ssd_conv(proj2, conv_w, conv_b, tm_l)
    a_row = jnp.pad(-jnp.exp(a_log), ((0, 0), (0, 128 - NH)))
    dtb_row = jnp.pad(dt_bias, ((0, 0), (0, 128 - NH)))
    dsk_full = jnp.repeat(d_skip, HD, axis=1)
    y, hsave = _ssd_fwd(xbc, proj2, a_row, dtb_row, dsk_full)
    mixcat = _mix_pre(att, y, proj2, g_att, g_ssd, tm_e)
    wout = weights.w_out(mixcat)
    mix = _mm_nn_fullk("proj_out", mixcat, wout, tm_l, D, f32)
    x2, h2b = _resid_norm_mod(x, gt1, mix, g_ffn, sc2, sh2, tm_e)
    wg4, wu4, wd4 = weights.ffn(h2b)
    act, sil, ud = _ffn_up(h2b, wg4, wu4, tm_m)
    ffn = _ffn_down(act, wd4, tm_l)

    dx3, dffn, loss, dg_final, dgt2 = _final_fwd_bwd(x2, ffn, gt2, g_final, tgt, tm_e)
    tok = weights.grad(("w_down",), [_grad_wdown4(act, dffn, 1024, tk)])
    dgate, dup = _ffn_dact(dffn, wd4, sil, ud, tm_l, dep=tok)
    tok = weights.grad(("w_gate", "w_up"), [_grad_cols4("grad_w_gate", h2b, dgate, 1024, tk), _grad_cols4("grad_w_up", h2b, dup, 1024, tk)])
    dh2 = _ffn_dh(dgate, dup, wg4, wu4, tm_m, dep=tok)
    dx2, dmix, dsc2, dsh2, dg_ffn, dgt1 = _norm_mod_bwd("norm_mod_bwd_2", dh2, x2, g_ffn, sc2, dx3, tm_e, mix=mix, gt=gt1)
    tok = weights.grad(("w_out",), [_mm_tn("grad_w_out", mixcat, dmix, 1024, 1024, tk, bf16).reshape(NSH, D // NSH, D)])
    dmc = _mm_nt("dmixcat", dmix, wout, tm_l, D, D, f32, dep=tok)
    datt, dy, dz, dg_att, dg_ssd = _mix_pre_bwd(dmc, att, y, proj2, g_att, g_ssd, tm_e)
    dq, dk, dv, gband = _attn_bwd(qkv, datt, bias)
    drel = _rel_bias_grad(gband.reshape(NH, CHUNK, BANDP))
    dxbc, ddtr, dd_row, da_row, ddtb_row = _ssd_bwd(xbc, proj2, dy, hsave, a_row, dtb_row, dsk_full)
    dxbc_raw, dconv_w, dconv_b = _ssd_conv_bwd(dxbc, proj2, conv_w, conv_b, tm_e)
    dproj = jnp.concatenate([dq, dk, dv, dz, dxbc_raw, ddtr.astype(bf16)], axis=1)
    gwin = _mm_tn("grad_w_in", h1b, dproj, 1024, 1152, tk, bf16)
    gwin4 = jnp.stack([jnp.pad(gwin[:, k * IN_SH:(k + 1) * IN_SH], ((0, 0), (0, IN_SHP - IN_SH))) for k in range(NSH)])
    tok = weights.grad(("w_in",), [gwin4])
    dh1 = _mm_nt("dh1", dproj, win, tm_l, 1024, 1920, f32, dep=tok)
    grad_x, dsc1, dsh1, dg_mix = _norm_mod_bwd("norm_mod_bwd_1", dh1, x, g_mix, sc1, dx2, tm_e)

    dmods = jnp.concatenate([dsh1, dsc1, dgt1, dsh2, dsc2, dgt2], axis=1)
    dd_skip = dd_row[:, :NH]
    da_log = da_row[:, :NH] * a_row[:, :NH]
    small = dict(g_mix=dg_mix, conv_b=dconv_b, dt_bias=ddtb_row[:, :NH], a_log=da_log, d_skip=dd_skip, g_att_out=dg_att,
                 g_ssd_out=dg_ssd, g_ffn=dg_ffn, g_final=dg_final, rel_bias=drel, conv_w=dconv_w)
    return loss[0, 0], grad_x, dmods, small


HBM = pl.BlockSpec(memory_space=pl.ANY)
VMEM = pl.BlockSpec(memory_space=pltpu.VMEM)


def _place():
    x, y, c = lax.axis_index("x"), lax.axis_index("y"), lax.axis_index("c")
    chips = [(1 - x, y), (x, 1 - y), (1 - x, 1 - y)]
    return x, y, c, chips


def _allgather8(name, payload, dep=None):
    r = payload.shape[0]
    deps = [] if dep is None else [dep]

    def body(x_ref, *rest):
        out_ref, send_sems, recv_sems, local_sem = rest[-4:]
        x, y, c, chips = _place()
        me, sibling = (x, y, c), (x, y, 1 - c)

        def slot(px, py, pc):
            return out_ref.at[4 * px + 2 * py + pc]

        def copy(k, block, to, src=None):
            return pltpu.make_async_remote_copy(
                src_ref=slot(*block) if src is None else src, dst_ref=slot(*block),
                send_sem=send_sems.at[k], recv_sem=recv_sems.at[k], device_id=to, device_id_type=MESH)

        mine = pltpu.make_async_copy(x_ref, slot(*me), local_sem)
        mine.start()
        first = [copy(0, me, sibling, src=x_ref)]
        first += [copy(1 + j, me, (*chip, c), src=x_ref) for j, chip in enumerate(chips)]
        for cp in first:
            cp.start()
        passed = [copy(4 + j, (*chip, c), sibling) for j, chip in enumerate(chips)]
        for j, chip in enumerate(chips):
            copy(1 + j, (*chip, c), me).wait_recv()
            passed[j].start()
        copy(0, sibling, me).wait_recv()
        for j, chip in enumerate(chips):
            copy(4 + j, (*chip, 1 - c), me).wait_recv()
        for cp in first + passed:
            cp.wait_send()
        mine.wait()

    return pl.pallas_call(
        body, name=name, out_shape=_sds((N_DEV, r, 128), f32), in_specs=[VMEM] * (1 + len(deps)), out_specs=VMEM,
        scratch_shapes=[pltpu.SemaphoreType.DMA((7,)), pltpu.SemaphoreType.DMA((7,)), pltpu.SemaphoreType.DMA],
    )(payload, *deps)


def _sum8(g):
    r = g.shape[1]

    def body(g_ref, o_ref):
        acc = g_ref[0]
        for i in range(1, N_DEV):
            acc = acc + g_ref[i]
        o_ref[...] = acc

    return pl.pallas_call(body, name="sum8", out_shape=_sds((r, 128), f32))(g)


SEM = pl.BlockSpec(memory_space=pltpu.SEMAPHORE)
EFFECT = pltpu.SideEffectType.DATAFLOW_SIDE_EFFECTING


def _gather_copies(ins, lands, send_sems, recv_sems):
    x, y, c, chips = _place()
    k = 2 * x + y
    starts, recvs = [], []
    for w in range(len(ins)):
        for j, (px, py) in enumerate(chips):
            def mk(dst):
                return pltpu.make_async_remote_copy(src_ref=ins[w].at[c], dst_ref=dst, send_sem=send_sems[w].at[j],
                                                    recv_sem=recv_sems[w].at[j], device_id=(px, py, c), device_id_type=MESH)
            starts.append(mk(lands[w].at[k, c]))
            recvs.append(mk(lands[w].at[2 * px + py, c]))
    return starts, recvs


def _reduce_copies(ins, lands, send_sems, recv_sems):
    x, y, c, chips = _place()
    k = 2 * x + y
    starts, recvs = [], []
    for w in range(len(ins)):
        for j, (px, py) in enumerate(chips):
            def mk(dst):
                return pltpu.make_async_remote_copy(src_ref=ins[w].at[2 * px + py], dst_ref=dst, send_sem=send_sems[w].at[j],
                                                    recv_sem=recv_sems[w].at[j], device_id=(px, py, c), device_id_type=MESH)
            starts.append(mk(lands[w].at[k]))
            recvs.append(mk(lands[w].at[2 * px + py]))
    return starts, recvs


def _split_start(name, copies, srcs, land_shapes):
    nw = len(srcs)

    def body(*refs):
        starts, _ = copies(refs[:nw], refs[nw:2 * nw], refs[2 * nw:3 * nw], refs[3 * nw:4 * nw])
        for cp in starts:
            cp.start()
        refs[6 * nw][...] = jnp.zeros((8, 128), f32)

    sems = [pltpu.SemaphoreType.DMA((3,))] * nw
    bufs = [pltpu.HBM(s.shape, bf16) for s in srcs] + [pltpu.HBM(s, bf16) for s in land_shapes]
    res = pl.pallas_call(
        body, name=name, out_shape=sems + sems + bufs + [_sds((8, 128), f32)],
        in_specs=[HBM] * (2 * nw), out_specs=[SEM] * (2 * nw) + [HBM] * (2 * nw) + [VMEM],
        input_output_aliases={i: 2 * nw + i for i in range(2 * nw)},
        compiler_params=pltpu.CompilerParams(has_side_effects=EFFECT),
    )(*[pltpu.with_memory_space_constraint(s, pltpu.HBM) for s in srcs],
      *[pltpu.with_memory_space_constraint(lax.empty(s, bf16), pltpu.HBM) for s in land_shapes])
    return res[:nw], res[nw:2 * nw], res[2 * nw:3 * nw], res[3 * nw:4 * nw], res[4 * nw]


def _split_wait(name, copies, send_sems, recv_sems, srcs, lands, after):
    nw = len(srcs)

    def body(*refs):
        starts, recvs = copies(refs[:nw], refs[nw:2 * nw], refs[2 * nw:3 * nw], refs[3 * nw:4 * nw])
        for s_, r_ in zip(starts, recvs):
            s_.wait_send()
            r_.wait_recv()

    bufs = [pltpu.HBM(s.shape, bf16) for s in srcs] + [pltpu.HBM(l.shape, bf16) for l in lands]
    res = pl.pallas_call(
        body, name=name, out_shape=bufs, in_specs=[HBM] * (2 * nw) + [SEM] * (2 * nw) + [HBM], out_specs=[HBM] * (2 * nw),
        input_output_aliases={i: i for i in range(2 * nw)},
        compiler_params=pltpu.CompilerParams(has_side_effects=EFFECT),
    )(*srcs, *lands, *send_sems, *recv_sems, after)
    return res[:nw], res[nw:]


def _gather_forward(name, shards, lands):
    nw = len(shards)

    def body(*refs):
        ins, lands_in, outs = refs[:nw], refs[nw:2 * nw], refs[2 * nw:3 * nw]
        st_a, st_b, st_c = refs[3 * nw:4 * nw], refs[4 * nw:5 * nw], refs[5 * nw:6 * nw]
        send_sems, recv_sems, load_sems, store_sems = refs[6 * nw:]
        x, y, c, chips = _place()
        k = 2 * x + y
        sibling = (x, y, 1 - c)
        ld_a = [pltpu.make_async_copy(ins[w].at[c], st_a[w], load_sems.at[w, 0]) for w in range(nw)]
        ld_b = [pltpu.make_async_copy(ins[w].at[1 - c], st_b[w], load_sems.at[w, 1]) for w in range(nw)]
        for cp in ld_a + ld_b:
            cp.start()
        st_own = []
        for w in range(nw):
            ld_a[w].wait()
            st_own.append(pltpu.make_async_copy(st_a[w], outs[w].at[k, c], store_sems.at[w, 0]))
            st_own[-1].start()
        for w in range(nw):
            ld_b[w].wait()
            st_own.append(pltpu.make_async_copy(st_b[w], outs[w].at[k, 1 - c], store_sems.at[w, 1]))
            st_own[-1].start()
        for cp in st_own:
            cp.wait()
        fwds = {}
        for j, (px, py) in enumerate(chips):
            kq = 2 * px + py
            for w in range(nw):
                slot = st_b[w] if j % 2 == 0 else st_c[w]
                if j == 2:
                    fwds[w, 0].wait_send()
                ld = pltpu.make_async_copy(lands_in[w].at[kq, c], slot, load_sems.at[w, 2 + j])
                ld.start()
                ld.wait()
                fwds[w, j] = pltpu.make_async_remote_copy(src_ref=slot, dst_ref=outs[w].at[kq, c], send_sem=send_sems.at[w, j],
                                                          recv_sem=recv_sems.at[w, j], device_id=sibling, device_id_type=MESH)
                fwds[w, j].start()
        for j, (px, py) in enumerate(chips):
            for w in range(nw):
                pltpu.make_async_remote_copy(src_ref=st_c[w], dst_ref=outs[w].at[2 * px + py, 1 - c], send_sem=send_sems.at[w, j],
                                             recv_sem=recv_sems.at[w, j], device_id=sibling, device_id_type=MESH).wait_recv()
        for w in range(nw):
            fwds[w, 1].wait_send()
            fwds[w, 2].wait_send()

    stage = [pltpu.VMEM(s.shape[1:], bf16) for s in shards]
    return pl.pallas_call(
        body, name=name, out_shape=[_sds(l.shape, bf16) for l in lands],
        in_specs=[HBM] * (2 * nw), out_specs=[HBM] * nw, input_output_aliases={nw + w: w for w in range(nw)},
        scratch_shapes=stage * 3 + [pltpu.SemaphoreType.DMA((nw, 3)), pltpu.SemaphoreType.DMA((nw, 3)), pltpu.SemaphoreType.DMA((nw, 5)),
                                    pltpu.SemaphoreType.DMA((nw, 2))],
        compiler_params=pltpu.CompilerParams(vmem_limit_bytes=VMEM_LIMIT),
    )(*shards, *lands)


def _rs_pair_exchange(name, grads):
    nw = len(grads)

    def body(*refs):
        ins, got, stage = refs[:nw], refs[nw:2 * nw], refs[2 * nw:3 * nw]
        send_sems, recv_sems, load_sems = refs[3 * nw:]
        x, y, c, _ = _place()

        def load(w, kk):
            return pltpu.make_async_copy(ins[w].at[kk, 1 - c], stage[w].at[kk % 2], load_sems.at[w, kk])

        def send(w, kk):
            return pltpu.make_async_remote_copy(src_ref=stage[w].at[kk % 2], dst_ref=got[w].at[kk], send_sem=send_sems.at[w, kk],
                                                recv_sem=recv_sems.at[w, kk], device_id=(x, y, 1 - c), device_id_type=MESH)

        for kk in range(2):
            for w in range(nw):
                load(w, kk).start()
        for kk in range(NSH):
            for w in range(nw):
                load(w, kk).wait()
                send(w, kk).start()
            if kk + 2 < NSH:
                for w in range(nw):
                    send(w, kk).wait_send()
                    load(w, kk + 2).start()
        for kk in range(NSH - 2, NSH):
            for w in range(nw):
                send(w, kk).wait_send()
        for kk in range(NSH):
            for w in range(nw):
                send(w, kk).wait_recv()

    return pl.pallas_call(
        body, name=name, out_shape=[_sds((NSH,) + g.shape[2:], bf16) for g in grads], in_specs=[HBM] * nw, out_specs=[HBM] * nw,
        scratch_shapes=[pltpu.VMEM((2,) + g.shape[2:], bf16) for g in grads]
        + [pltpu.SemaphoreType.DMA((nw, NSH)), pltpu.SemaphoreType.DMA((nw, NSH)), pltpu.SemaphoreType.DMA((nw, NSH))],
        compiler_params=pltpu.CompilerParams(vmem_limit_bytes=VMEM_LIMIT),
    )(*grads)


def _rs_pair_gather(name, halves):
    nw = len(halves)

    def body(*refs):
        ins, outs, stage = refs[:nw], refs[nw:2 * nw], refs[2 * nw:3 * nw]
        send_sems, recv_sems, local_sems, stage_sems = refs[3 * nw:]
        x, y, c, _ = _place()
        loads = [pltpu.make_async_copy(ins[w], stage[w], stage_sems.at[w]) for w in range(nw)]
        for cp in loads:
            cp.start()
        local, cps = [], []
        for w in range(nw):
            loads[w].wait()
            local.append(pltpu.make_async_copy(stage[w], outs[w].at[c], local_sems.at[w]))
            cps.append(pltpu.make_async_remote_copy(src_ref=stage[w], dst_ref=outs[w].at[c], send_sem=send_sems.at[w],
                                                    recv_sem=recv_sems.at[w], device_id=(x, y, 1 - c), device_id_type=MESH))
            local[w].start()
            cps[w].start()
        for w in range(nw):
            pltpu.make_async_remote_copy(src_ref=stage[w], dst_ref=outs[w].at[1 - c], send_sem=send_sems.at[w], recv_sem=recv_sems.at[w],
                                         device_id=(x, y, 1 - c), device_id_type=MESH).wait_recv()
        for cp in cps:
            cp.wait_send()
        for cp in local:
            cp.wait()

    return pl.pallas_call(
        body, name=name, out_shape=[_sds((2,) + h.shape, f32) for h in halves], in_specs=[HBM] * nw, out_specs=[HBM] * nw,
        scratch_shapes=[pltpu.VMEM(h.shape, f32) for h in halves]
        + [pltpu.SemaphoreType.DMA((nw,)), pltpu.SemaphoreType.DMA((nw,)), pltpu.SemaphoreType.DMA((nw,)), pltpu.SemaphoreType.DMA((nw,))],
        compiler_params=pltpu.CompilerParams(vmem_limit_bytes=VMEM_LIMIT),
    )(*halves)


def _row_tile(r, c, nbuf):
    budget = 24 * 1024 * 1024 // (2 * nbuf * 4 * c)
    t = 8
    while t * 2 <= budget and r % (t * 2) == 0:
        t *= 2
    return t


def _cast_bf16(name, a, dep=None):
    r, c = a.shape
    tr = _row_tile(r, c, 2)
    dep_specs, dep_ops = _dep_args(dep, 1)

    def body(a_ref, *rest):
        rest[-1][...] = a_ref[...].astype(bf16)

    spec = pl.BlockSpec((tr, c), lambda i: (i, 0))
    return pl.pallas_call(body, name=name, grid=(r // tr,), in_specs=[spec] + dep_specs, out_specs=spec, out_shape=_sds((r, c), bf16),
                          compiler_params=_params(("parallel",)))(a, *dep_ops)


def _w_in_columns(win4):
    tr = 256

    def body(a_ref, o_ref, ob_ref):
        for k in range(NSH):
            o_ref[:, IN_SH * k:IN_SH * (k + 1)] = a_ref[k][:, :IN_SH]
        o_ref[:, IN_COLS:] = jnp.zeros((tr, IN_P - IN_COLS), bf16)
        ob_ref[...] = o_ref[:, IN_A:]

    return pl.pallas_call(
        body, name="w_in_columns", grid=(D // tr,), in_specs=[pl.BlockSpec((NSH, tr, IN_SHP), lambda i: (0, i, 0))],
        out_specs=[pl.BlockSpec((tr, IN_P), lambda i: (i, 0)), pl.BlockSpec((tr, IN_B), lambda i: (i, 0))],
        out_shape=[_sds((D, IN_P), bf16), _sds((D, IN_B), bf16)], compiler_params=_params(("parallel",)))(win4)


def _pair_sum(name, core, grads, got):
    _, _, rh, c = grads.shape
    tr = _row_tile(rh, c, 2)

    def body(c_ref, a_ref, b_ref, o_ref):
        o_ref[...] = (a_ref[...].astype(f32) + b_ref[...].astype(f32)).astype(bf16)

    spec = pl.BlockSpec((None, tr, c), lambda k, i, c_ref: (k, i, 0))
    return pl.pallas_call(
        body, name=name, out_shape=_sds((NSH, rh, c), bf16),
        grid_spec=pltpu.PrefetchScalarGridSpec(
            num_scalar_prefetch=1, grid=(NSH, rh // tr),
            in_specs=[pl.BlockSpec((None, None, tr, c), lambda k, i, c_ref: (k, c_ref[0], i, 0)), spec], out_specs=spec),
        compiler_params=_params(("parallel", "parallel")))(core, grads, got)


def _chip_sum(name, chip, sums, lands):
    _, rh, c = sums.shape
    tr = _row_tile(rh, c, 4)

    def body(k_ref, own_ref, l_ref, o_ref):
        own = own_ref[...].astype(f32)
        acc = None
        for j in range(NSH):
            term = jnp.where(k_ref[0] == j, own, l_ref[j].astype(f32))
            acc = term if acc is None else acc + term
        o_ref[...] = acc

    return pl.pallas_call(
        body, name=name, out_shape=_sds((rh, c), f32),
        grid_spec=pltpu.PrefetchScalarGridSpec(
            num_scalar_prefetch=1, grid=(rh // tr,),
            in_specs=[pl.BlockSpec((None, tr, c), lambda i, k_ref: (k_ref[0], i, 0)), pl.BlockSpec((NSH, tr, c), lambda i, k_ref: (0, i, 0))],
            out_specs=pl.BlockSpec((tr, c), lambda i, k_ref: (i, 0))),
        compiler_params=_params(("parallel",)))(chip, sums, lands)


def _mods_part(cond16, w_ada, b_part):
    n = w_ada.shape[1]
    tn = 512

    def body(c_ref, w_ref, b_ref, o_ref):
        cv = c_ref[...]
        o_ref[...] = _dot(cv * _sigmoid(cv), w_ref[...]) + b_ref[...]

    return pl.pallas_call(
        body, name="mods_part", grid=(n // tn,),
        in_specs=[pl.BlockSpec((16, D), lambda j: (0, 0)), pl.BlockSpec((D, tn), lambda j: (0, j)), pl.BlockSpec((1, tn), lambda j: (0, j))],
        out_specs=pl.BlockSpec((16, tn), lambda j: (0, j)), out_shape=_sds((16, n), f32), compiler_params=_params(("parallel",)),
    )(cond16, w_ada, b_part)


def _grad_w_ada(cond16, dm16):
    n = dm16.shape[1]
    tr = 256

    def body(c_ref, d_ref, o_ref):
        cv = c_ref[...]
        o_ref[...] = _dot(cv * _sigmoid(cv), d_ref[...], ta=True)

    return pl.pallas_call(
        body, name="grad_w_ada", grid=(D // tr,),
        in_specs=[pl.BlockSpec((16, tr), lambda i: (0, i)), pl.BlockSpec((16, n), lambda i: (0, 0))],
        out_specs=pl.BlockSpec((tr, n), lambda i: (i, 0)), out_shape=_sds((D, n), f32), compiler_params=_params(("parallel",)),
    )(cond16, dm16)


def _adamw(name, w, g, m, v):
    r, c = w.shape
    tr = _row_tile(r, c, 7)
    spec = pl.BlockSpec((tr, c), lambda i: (i, 0))
    grid = (r // tr,)

    def body(w_ref, g_ref, m_ref, v_ref, d_ref, nm_ref, nv_ref):
        gv = g_ref[...]
        nm = ADAM_B1 * m_ref[...] + (1.0 - ADAM_B1) * gv
        nv = ADAM_B2 * v_ref[...] + (1.0 - ADAM_B2) * (gv * gv)
        nm_ref[...] = nm
        nv_ref[...] = nv
        m_hat = nm / (1.0 - ADAM_B1 ** ADAM_STEP)
        v_hat = nv / (1.0 - ADAM_B2 ** ADAM_STEP)
        d_ref[...] = -ADAM_LR * (m_hat / (jnp.sqrt(v_hat) + ADAM_EPS) + ADAM_WD * w_ref[...])

    return pl.pallas_call(body, name=name, grid=grid, in_specs=[spec] * 4, out_specs=[spec] * 3, out_shape=[_sds(w.shape, f32)] * 3,
                          compiler_params=_params(("parallel",)))(w, g, m, v)


def _pack(parts, rows):
    flat = []
    for p in parts:
        p = p.reshape(-1)
        flat.append(jnp.pad(p, (0, (-p.shape[0]) % 128)))
    v = jnp.concatenate(flat)
    return jnp.pad(v, (0, rows * 128 - v.shape[0])).reshape(rows, 128)


def _unpack(packed, sizes):
    lead = packed.shape[:-2]
    flat = packed.reshape(lead + (-1,))
    out, off = [], 0
    for n in sizes:
        out.append(flat[..., off:off + n])
        off += n + (-n) % 128
    return out


BIG = ("w_in", "w_out", "w_gate", "w_up", "w_down")
SMALL = ("b_ada", "g_mix", "conv_b", "dt_bias", "a_log", "d_skip", "g_att_out", "g_ssd_out", "g_ffn", "g_final", "rel_bias", "conv_w")
ORDER = ("w_ada", "b_ada", "g_mix", "w_in", "rel_bias", "conv_w", "conv_b", "dt_bias", "a_log", "d_skip", "g_att_out", "g_ssd_out",
         "w_out", "g_ffn", "w_gate", "w_up", "w_down", "g_final")
REL_SH = N_REL // NSH
CONVW_SH = XBC // NSH
ADA_SH = 6 * D // NSH


class _Exchange:
    def __init__(self, core, chip):
        self.core, self.chip = core, chip
        self.gathered = {}
        self.pending = []

    def gather(self, names, shards):
        ssem, rsem, thru, lands, token = _split_start("gather_start_" + "_".join(names), _gather_copies, shards,
                                                      [(NSH,) + s.shape for s in shards])
        self.gathered.update({n: (ssem[i], rsem[i], thru[i], lands[i]) for i, n in enumerate(names)})
        return token

    def _whole(self, names, after):
        ssem, rsem, thru, lands = zip(*[self.gathered[n] for n in names])
        tag = "_".join(names)
        thru, lands = _split_wait("gather_wait_" + tag, _gather_copies, ssem, rsem, thru, lands, after)
        return _gather_forward("gather_forward_" + tag, thru, lands)

    def w_in(self, after):
        (win4,) = self._whole(("w_in",), after)
        return _w_in_columns(win4.reshape(NSH, D, IN_SHP))

    def w_out(self, after):
        (wout4,) = self._whole(("w_out",), after)
        return wout4.reshape(D, D)

    def ffn(self, after):
        wg4, wu4, wd4 = self._whole(("w_gate", "w_up", "w_down"), after)
        return wg4.reshape(NSH, D, FSH), wu4.reshape(NSH, D, FSH), wd4.reshape(NSH, FSH, D)

    def grad(self, names, grads):
        tag = "_".join(names)
        stacked = [g.reshape(NSH, 2, g.shape[1] // 2, g.shape[2]) for g in grads]
        got = _rs_pair_exchange("rs_pair_exchange_" + tag, stacked)
        sums = [_pair_sum("pair_sum_" + n, self.core, o, g) for n, o, g in zip(names, stacked, got)]
        self.pending.append((names, _split_start("rs_start_" + tag, _reduce_copies, sums, [s.shape for s in sums])))
        return self.pending[-1][1][4]

    def finish(self, after):
        grads = {}
        for names, (ssem, rsem, sums, lands, _) in self.pending:
            tag = "_".join(names)
            sums, lands = _split_wait("rs_wait_" + tag, _reduce_copies, ssem, rsem, sums, lands, after)
            halves = [_chip_sum("chip_sum_" + n, self.chip, sm, ld) for n, sm, ld in zip(names, sums, lands)]
            for n, f in zip(names, _rs_pair_gather("rs_pair_gather_" + tag, halves)):
                grads[n] = f.reshape(2 * f.shape[1], f.shape[2])
        return grads


def kernel(x, c, w_ada, b_ada, g_mix, w_in, rel_bias, conv_w, conv_b, dt_bias, a_log, d_skip, g_att_out, g_ssd_out, w_out, g_ffn, w_gate, w_up, w_down, g_final, loss_target, m_w_ada, m_b_ada, m_g_mix, m_w_in, m_rel_bias, m_conv_w, m_conv_b, m_dt_bias, m_a_log, m_d_skip, m_g_att_out, m_g_ssd_out, m_w_out, m_g_ffn, m_w_gate, m_w_up, m_w_down, m_g_final, v_w_ada, v_b_ada, v_g_mix, v_w_in, v_rel_bias, v_conv_w, v_conv_b, v_dt_bias, v_a_log, v_d_skip, v_g_att_out, v_g_ssd_out, v_w_out, v_g_ffn, v_w_gate, v_w_up, v_w_down, v_g_final):
    args = dict(locals())
    w = {n: args[n] for n in ORDER}
    m = {n: args["m_" + n] for n in ORDER}
    v = {n: args["v_" + n] for n in ORDER}
    ix, iy, ic = lax.axis_index("x"), lax.axis_index("y"), lax.axis_index("c")
    chip = 2 * ix + iy
    dev = 2 * chip + ic
    s = x.shape[1]

    g1 = _allgather8("gather_inputs", _pack([c[0], rel_bias[0], conv_w[0]], 40))
    c_all, rel_sh, convw_sh = _unpack(g1, [D, NH * REL_SH, 4 * CONVW_SH])
    rel_full = jnp.concatenate([rel_sh[2 * k].reshape(NH, REL_SH) for k in range(NSH)], axis=1)
    convw_full = jnp.concatenate([convw_sh[2 * k].reshape(4, CONVW_SH) for k in range(NSH)], axis=1)
    cond16 = jnp.pad(c_all, ((0, 8), (0, 0)))
    b_part = lax.dynamic_slice_in_dim(b_ada, chip * ADA_SH, ADA_SH, axis=1)
    mods_part = _mods_part(cond16, w_ada[0], b_part)[:N_DEV]
    g2 = _allgather8("gather_mods", mods_part.reshape(N_DEV * ADA_SH // 128, 128))
    mods_all = jnp.concatenate([g2[2 * k].reshape(N_DEV, ADA_SH) for k in range(NSH)], axis=1)
    mods = lax.dynamic_slice_in_dim(mods_all, dev, 1, axis=0)

    exchange = _Exchange(jnp.reshape(ic, (1,)).astype(jnp.int32), jnp.reshape(chip, (1,)).astype(jnp.int32))
    shard_in = _cast_bf16("cast_w_in", jnp.pad(w_in[0], ((0, 0), (0, IN_SHP - IN_SH))), dep=g2[0, :8]).reshape(2, D // 2, IN_SHP)
    tok = exchange.gather(("w_in",), [shard_in])
    tok = exchange.gather(("w_out", "w_gate", "w_up", "w_down"), [
        _cast_bf16("cast_w_out", w_out[0], dep=tok).reshape(2, D // NSH // 2, D),
        _cast_bf16("cast_w_gate", w_gate[0], dep=tok).reshape(2, D // 2, FSH),
        _cast_bf16("cast_w_up", w_up[0], dep=tok).reshape(2, D // 2, FSH),
        _cast_bf16("cast_w_down", w_down[0], dep=tok).reshape(2, FSH // 2, D)])
    mods = mods + tok[:1, :1]

    loss, grad_x, dmods, small = _local_step(
        x[0], loss_target[0], mods, g_mix, rel_full, convw_full, conv_b, dt_bias, a_log, d_skip, g_att_out, g_ssd_out, g_ffn,
        g_final[None, :], exchange)

    small_names = ("g_mix", "conv_b", "dt_bias", "a_log", "d_skip", "g_att_out", "g_ssd_out", "g_ffn", "g_final", "rel_bias", "conv_w")
    g3 = _allgather8("gather_small_grads", _pack([dmods] + [small[n] for n in small_names], 264))
    sizes = [6 * D] + [int(np.prod(small[n].shape)) for n in small_names]
    dmods_all = _unpack(g3, sizes)[0]
    summed = _unpack(_sum8(g3), sizes)
    grads = {"b_ada": summed[0].reshape(1, 6 * D)}
    for n, val in zip(small_names, summed[1:]):
        grads[n] = val.reshape(small[n].shape)
    grads["rel_bias"] = lax.dynamic_slice_in_dim(grads["rel_bias"], chip * REL_SH, REL_SH, axis=1)
    grads["conv_w"] = lax.dynamic_slice_in_dim(grads["conv_w"], chip * CONVW_SH, CONVW_SH, axis=1)
    grads["g_final"] = grads["g_final"].reshape(D)
    dm16 = jnp.pad(lax.dynamic_slice_in_dim(dmods_all, chip * ADA_SH, ADA_SH, axis=1), ((0, 8), (0, 0)))
    grads["w_ada"] = _grad_w_ada(cond16, dm16)

    delta, new_m, new_v = {}, {}, {}
    delta["w_ada"], new_m["w_ada"], new_v["w_ada"] = _adamw("adamw_w_ada", w_ada[0], grads["w_ada"], m_w_ada[0], v_w_ada[0])
    grads.update(exchange.finish(grad_x))
    grads["w_in"] = grads["w_in"][:, :IN_SH]
    for n in BIG:
        delta[n], new_m[n], new_v[n] = _adamw("adamw_" + n, w[n][0], grads[n], m[n][0], v[n][0])
    sw = _pack([w[n] for n in SMALL], 200)
    sg = _pack([grads[n] for n in SMALL], 200)
    sm = _pack([m[n] for n in SMALL], 200)
    sv = _pack([v[n] for n in SMALL], 200)
    ssz = [int(np.prod(w[n].shape)) for n in SMALL]
    for dst, packed in zip((delta, new_m, new_v), _adamw("adamw_small", sw, sg, sm, sv)):
        for n, val in zip(SMALL, _unpack(packed, ssz)):
            dst[n] = val

    def shaped(d, n):
        return d[n].reshape(w[n].shape)

    total = lax.psum(loss, ("x", "y", "c"))
    return (total, grad_x[None], *[shaped(grads, n) for n in ORDER], *[shaped(delta, n) for n in ORDER],
            *[shaped(new_m, n) for n in ORDER], *[shaped(new_v, n) for n in ORDER])
```

```python
import functools

import numpy as np
import jax
import jax.numpy as jnp
from jax import lax
from jax.experimental import pallas as pl
from jax.experimental.pallas import tpu as pltpu

f32 = jnp.float32
bf16 = jnp.bfloat16
HIGHEST = lax.Precision.HIGHEST
MESH = pl.DeviceIdType.MESH

D = 2048
CHUNK = 64
LEFT = 8
BAND = (LEFT + 1) * CHUNK
BANDP = 640
PADK = LEFT * CHUNK
NH = 16
HD = 64
ATT_W = NH * HD
SSD_W = 1024
NG = 2
NSTATE = 128
GW = SSD_W // NG
XBC = SSD_W + 2 * NG * NSTATE
N_REL = 320
REL_CLIP = 256
FFN = 5632
NSH = 4
FSH = FFN // NSH
IN_COLS = 5648
IN_SH = IN_COLS // NSH
IN_SHP = 1536
IN_A = 3 * ATT_W
IN_B = 2688
IN_P = IN_A + IN_B
EPS = 1e-6
N_DEV = 8

ADAM_LR = 0.001
ADAM_B1 = 0.9
ADAM_B2 = 0.999
ADAM_EPS = 1e-08
ADAM_WD = 0.01
ADAM_STEP = 10

VMEM_LIMIT = 56 * 1024 * 1024


def _params(sem):
    return pltpu.CompilerParams(dimension_semantics=sem, vmem_limit_bytes=VMEM_LIMIT)


def _sds(shape, dtype):
    return jax.ShapeDtypeStruct(shape, dtype)


def _fold8(v):
    r, w = v.shape
    return jnp.sum(v.reshape(r // 8, 8, w), axis=0)


STRIP = 16


def _strips(tm, fn):
    def step(j, carry):
        fn(pl.ds(pl.multiple_of(j * STRIP, STRIP), STRIP))
        return carry
    lax.fori_loop(0, tm // STRIP, step, 0, unroll=4)


def _sigmoid(v):
    return 1.0 / (1.0 + jnp.exp(-v))


def _softplus(v):
    return jnp.maximum(v, 0.0) + jnp.log(1.0 + jnp.exp(-jnp.abs(v)))


def _dot(a, b, ta=False, tb=False):
    dn = (((0 if ta else 1,), (1 if tb else 0,)), ((), ()))
    return lax.dot_general(a.astype(bf16), b.astype(bf16), dn, preferred_element_type=f32)


def _dep_args(dep, ngrid):
    if dep is None:
        return [], []
    return [pl.BlockSpec((8, 128), lambda *_: (0, 0))], [dep]


def _dot01(a, b, ta=False, tb=False, exact="b"):
    dn = (((0 if ta else 1,), (1 if tb else 0,)), ((), ()))
    x = a if exact == "b" else b
    hi = x.astype(bf16)
    r = x - hi.astype(f32)
    mid = r.astype(bf16)
    lo = (r - mid.astype(f32)).astype(bf16)
    if exact == "b":
        m = b.astype(bf16)
        return sum(lax.dot_general(p, m, dn, preferred_element_type=f32) for p in (hi, mid, lo))
    m = a.astype(bf16)
    return sum(lax.dot_general(m, p, dn, preferred_element_type=f32) for p in (hi, mid, lo))


def _matmul(name, a, b, *, grid, a_spec, b_spec, o_spec, o_shape, o_dtype, acc_shape, ta=False, tb=False, dep=None):
    nk = grid[2]
    dep_specs, dep_ops = _dep_args(dep, 3)

    def body(a_ref, b_ref, *rest):
        o_ref, acc_ref = rest[-2:]
        p = _dot(a_ref[...], b_ref[...], ta, tb)
        if nk == 1:
            o_ref[...] = p.astype(o_ref.dtype)
        else:
            k = pl.program_id(2)

            @pl.when(k == 0)
            def _():
                acc_ref[...] = p

            @pl.when(jnp.logical_and(k > 0, k < nk - 1))
            def _():
                acc_ref[...] += p

            @pl.when(k == nk - 1)
            def _():
                o_ref[...] = (acc_ref[...] + p).astype(o_ref.dtype)

    return pl.pallas_call(
        body, name=name, grid=grid, in_specs=[a_spec, b_spec] + dep_specs, out_specs=o_spec,
        out_shape=_sds(o_shape, o_dtype), scratch_shapes=[pltpu.VMEM(acc_shape if nk > 1 else (8, 128), f32)],
        compiler_params=_params(("parallel", "parallel", "arbitrary")),
    )(a, b, *dep_ops)


def _mm_nn_fullk(name, a, b, tm, tn, o_dtype, n=None):
    m, k = a.shape
    n = b.shape[1] if n is None else n
    return _matmul(name, a, b, grid=(m // tm, n // tn, 1),
                   a_spec=pl.BlockSpec((tm, k), lambda i, j, kk: (i, 0)),
                   b_spec=pl.BlockSpec((k, tn), lambda i, j, kk: (0, j)),
                   o_spec=pl.BlockSpec((tm, tn), lambda i, j, kk: (i, j)),
                   o_shape=(m, n), o_dtype=o_dtype, acc_shape=(tm, tn))


def _mm_nt(name, a, b, tm, tn, tk, o_dtype, dep=None):
    m, k = a.shape
    n = b.shape[0]
    return _matmul(name, a, b, grid=(m // tm, n // tn, k // tk), tb=True, dep=dep,
                   a_spec=pl.BlockSpec((tm, tk), lambda i, j, kk: (i, kk)),
                   b_spec=pl.BlockSpec((tn, tk), lambda i, j, kk: (j, kk)),
                   o_spec=pl.BlockSpec((tm, tn), lambda i, j, kk: (i, j)),
                   o_shape=(m, n), o_dtype=o_dtype, acc_shape=(tm, tn))


def _mm_tn(name, a, b, tm, tn, tk, o_dtype):
    k, m = a.shape
    n = b.shape[1]
    return _matmul(name, a, b, grid=(m // tm, n // tn, k // tk), ta=True,
                   a_spec=pl.BlockSpec((tk, tm), lambda i, j, kk: (kk, i)),
                   b_spec=pl.BlockSpec((tk, tn), lambda i, j, kk: (kk, j)),
                   o_spec=pl.BlockSpec((tm, tn), lambda i, j, kk: (i, j)),
                   o_shape=(m, n), o_dtype=o_dtype, acc_shape=(tm, tn))


FSH_PARTS = (slice(0, 640), slice(640, FSH))


def _ffn_up(h2b, wg4, wu4, tm):
    s = h2b.shape[0]

    def body(h_ref, wg_ref, wu_ref, a_ref, s_ref, ud_ref):
        h = h_ref[...]
        for cols in FSH_PARTS:
            g = _dot(h, wg_ref[:, cols])
            u = _dot(h, wu_ref[:, cols])
            sg = _sigmoid(g)
            sil = g * sg
            a_ref[:, cols] = (sil * u).astype(bf16)
            s_ref[:, cols] = sil.astype(bf16)
            ud_ref[:, cols] = (u * (sg * (1.0 + g * (1.0 - sg)))).astype(bf16)

    wspec = pl.BlockSpec((None, D, FSH), lambda k, i: (k, 0, 0))
    ospec = pl.BlockSpec((tm, FSH), lambda k, i: (i, k))
    return pl.pallas_call(
        body, name="ffn_up", grid=(NSH, s // tm),
        in_specs=[pl.BlockSpec((tm, D), lambda k, i: (i, 0)), wspec, wspec],
        out_specs=[ospec, ospec, ospec], out_shape=[_sds((s, FFN), bf16)] * 3,
        compiler_params=_params(("parallel", "parallel")),
    )(h2b, wg4, wu4)


def _ffn_down(act, wd4, tm):
    s = act.shape[0]
    return _matmul("ffn_down", act, wd4, grid=(s // tm, 1, NSH),
                   a_spec=pl.BlockSpec((tm, FSH), lambda i, j, k: (i, k)),
                   b_spec=pl.BlockSpec((None, FSH, D), lambda i, j, k: (k, 0, 0)),
                   o_spec=pl.BlockSpec((tm, D), lambda i, j, k: (i, 0)),
                   o_shape=(s, D), o_dtype=f32, acc_shape=(tm, D))


def _ffn_dact(dffn, wd4, sil, ud, tm, dep=None):
    s = dffn.shape[0]
    dep_specs, dep_ops = _dep_args(dep, 2)

    def body(d_ref, w_ref, s_ref, ud_ref, *rest):
        dg_ref, du_ref = rest[-2:]
        d = d_ref[...]
        for cols in FSH_PARTS:
            dact = _dot(d, w_ref[cols, :], tb=True)
            dg_ref[:, cols] = (dact * ud_ref[:, cols].astype(f32)).astype(bf16)
            du_ref[:, cols] = (dact * s_ref[:, cols].astype(f32)).astype(bf16)

    blk = pl.BlockSpec((tm, FSH), lambda k, i: (i, k))
    return pl.pallas_call(
        body, name="ffn_dact", grid=(NSH, s // tm),
        in_specs=[pl.BlockSpec((tm, D), lambda k, i: (i, 0)), pl.BlockSpec((None, FSH, D), lambda k, i: (k, 0, 0)), blk, blk] + dep_specs,
        out_specs=[blk, blk], out_shape=[_sds((s, FFN), bf16), _sds((s, FFN), bf16)],
        compiler_params=_params(("parallel", "parallel")),
    )(dffn, wd4, sil, ud, *dep_ops)


def _ffn_dh(dgate, dup, wg4, wu4, tm, dep=None):
    s = dgate.shape[0]
    dep_specs, dep_ops = _dep_args(dep, 2)

    def body(dg_ref, du_ref, wg_ref, wu_ref, *rest):
        o_ref, acc_ref = rest[-2:]
        k = pl.program_id(1)
        p = _dot(dg_ref[...], wg_ref[...], tb=True) + _dot(du_ref[...], wu_ref[...], tb=True)

        @pl.when(k == 0)
        def _():
            acc_ref[...] = p

        @pl.when(jnp.logical_and(k > 0, k < NSH - 1))
        def _():
            acc_ref[...] += p

        @pl.when(k == NSH - 1)
        def _():
            o_ref[...] = acc_ref[...] + p

    aspec = pl.BlockSpec((tm, FSH), lambda i, k: (i, k))
    wspec = pl.BlockSpec((None, D, FSH), lambda i, k: (k, 0, 0))
    return pl.pallas_call(
        body, name="ffn_dh", grid=(s // tm, NSH), in_specs=[aspec, aspec, wspec, wspec] + dep_specs,
        out_specs=pl.BlockSpec((tm, D), lambda i, k: (i, 0)), out_shape=_sds((s, D), f32),
        scratch_shapes=[pltpu.VMEM((tm, D), f32)], compiler_params=_params(("parallel", "arbitrary")),
    )(dgate, dup, wg4, wu4, *dep_ops)


def _grad_cols4(name, h, dy, tm, tk):
    s = h.shape[0]
    return _matmul(name, h, dy, grid=(NSH, D // tm, s // tk), ta=True,
                   a_spec=pl.BlockSpec((tk, tm), lambda k, i, kk: (kk, i)),
                   b_spec=pl.BlockSpec((tk, FSH), lambda k, i, kk: (kk, k)),
                   o_spec=pl.BlockSpec((None, tm, FSH), lambda k, i, kk: (k, i, 0)),
                   o_shape=(NSH, D, FSH), o_dtype=bf16, acc_shape=(tm, FSH))


def _grad_wdown4(act, dffn, tn, tk):
    s = act.shape[0]
    return _matmul("grad_w_down", act, dffn, grid=(NSH, D // tn, s // tk), ta=True,
                   a_spec=pl.BlockSpec((tk, FSH), lambda k, j, kk: (kk, k)),
                   b_spec=pl.BlockSpec((tk, tn), lambda k, j, kk: (kk, j)),
                   o_spec=pl.BlockSpec((None, FSH, tn), lambda k, j, kk: (k, 0, j)),
                   o_shape=(NSH, FSH, D), o_dtype=bf16, acc_shape=(FSH, tn))


def _row_spec(w):
    return pl.BlockSpec((1, w), lambda i: (0, 0))


def _tile_spec(tm, w, col=0):
    return pl.BlockSpec((tm, w), lambda i: (i, col))


def _norm_mod(name, x, g, sc, sh, tm):
    s = x.shape[0]

    def body(x_ref, g_ref, sc_ref, sh_ref, o_ref):
        def strip(rows):
            xv = x_ref[rows, :]
            r = lax.rsqrt(jnp.mean(xv * xv, axis=-1, keepdims=True) + EPS)
            o_ref[rows, :] = (xv * r * g_ref[...] * (1.0 + sc_ref[...]) + sh_ref[...]).astype(bf16)

        _strips(tm, strip)

    return pl.pallas_call(
        body, name=name, grid=(s // tm,), in_specs=[_tile_spec(tm, D), _row_spec(D), _row_spec(D), _row_spec(D)],
        out_specs=_tile_spec(tm, D), out_shape=_sds((s, D), bf16), compiler_params=_params(("parallel",)),
    )(x, g, sc, sh)


def _resid_norm_mod(x, gt, mix, g, sc, sh, tm):
    s = x.shape[0]

    def body(x_ref, gt_ref, m_ref, g_ref, sc_ref, sh_ref, x2_ref, h_ref):
        def strip(rows):
            xv = x_ref[rows, :] + gt_ref[...] * m_ref[rows, :]
            x2_ref[rows, :] = xv
            r = lax.rsqrt(jnp.mean(xv * xv, axis=-1, keepdims=True) + EPS)
            h_ref[rows, :] = (xv * r * g_ref[...] * (1.0 + sc_ref[...]) + sh_ref[...]).astype(bf16)

        _strips(tm, strip)

    return pl.pallas_call(
        body, name="resid_norm_mod", grid=(s // tm,),
        in_specs=[_tile_spec(tm, D), _row_spec(D), _tile_spec(tm, D), _row_spec(D), _row_spec(D), _row_spec(D)],
        out_specs=[_tile_spec(tm, D), _tile_spec(tm, D)], out_shape=[_sds((s, D), f32), _sds((s, D), bf16)],
        compiler_params=_params(("parallel",)),
    )(x, gt, mix, g, sc, sh)


def _final_fwd_bwd(x2, ffn, gt2, g, tgt, tm):
    s = x2.shape[0]
    n = s // tm

    def body(x_ref, f_ref, gt_ref, g_ref, t_ref, dx_ref, df_ref, loss_ref, dg_ref, dgt_ref, a_loss, a_dg, a_dgt):
        i = pl.program_id(0)

        @pl.when(i == 0)
        def _():
            a_loss[...] = jnp.zeros_like(a_loss)
            a_dg[...] = jnp.zeros_like(a_dg)
            a_dgt[...] = jnp.zeros_like(a_dgt)

        def strip(rows):
            fv = f_ref[rows, :]
            gt = gt_ref[...]
            gv = g_ref[...]
            xv = x_ref[rows, :] + gt * fv
            r = lax.rsqrt(jnp.mean(xv * xv, axis=-1, keepdims=True) + EPS)
            xh = xv * r
            e = xh * gv - t_ref[rows, :]
            a_loss[...] += _fold8(e * e)
            dy = e * (1.0 / D)
            a_dg[...] += _fold8(dy * xh)
            t = dy * gv
            dx = r * (t - xh * jnp.mean(t * xh, axis=-1, keepdims=True))
            dx_ref[rows, :] = dx
            a_dgt[...] += _fold8(dx * fv)
            df_ref[rows, :] = (dx * gt).astype(bf16)

        _strips(tm, strip)

        @pl.when(i == n - 1)
        def _():
            tot = jnp.sum(jnp.sum(a_loss[...], axis=0, keepdims=True), axis=1, keepdims=True) * (0.5 / D)
            loss_ref[...] = jnp.broadcast_to(tot, (1, 128))
            dg_ref[...] = jnp.sum(a_dg[...], axis=0, keepdims=True)
            dgt_ref[...] = jnp.sum(a_dgt[...], axis=0, keepdims=True)

    return pl.pallas_call(
        body, name="final_fwd_bwd", grid=(n,),
        in_specs=[_tile_spec(tm, D), _tile_spec(tm, D), _row_spec(D), _row_spec(D), _tile_spec(tm, D)],
        out_specs=[_tile_spec(tm, D), _tile_spec(tm, D), _row_spec(128), _row_spec(D), _row_spec(D)],
        out_shape=[_sds((s, D), f32), _sds((s, D), bf16), _sds((1, 128), f32), _sds((1, D), f32), _sds((1, D), f32)],
        scratch_shapes=[pltpu.VMEM((8, D), f32)] * 3, compiler_params=_params(("arbitrary",)),
    )(x2, ffn, gt2, g, tgt)


def _norm_mod_bwd(name, dh, xin, g, sc, dres, tm, mix=None, gt=None):
    s = dh.shape[0]
    n = s // tm
    with_mix = mix is not None

    def body(*refs):
        if with_mix:
            dh_ref, x_ref, g_ref, sc_ref, dr_ref, m_ref, gt_ref, dx_ref, dm_ref, dsc_ref, dsh_ref, dg_ref, dgt_ref, a_sc, a_sh, a_g, a_gt = refs
        else:
            dh_ref, x_ref, g_ref, sc_ref, dr_ref, dx_ref, dsc_ref, dsh_ref, dg_ref, a_sc, a_sh, a_g = refs
        i = pl.program_id(0)

        @pl.when(i == 0)
        def _():
            a_sc[...] = jnp.zeros_like(a_sc)
            a_sh[...] = jnp.zeros_like(a_sh)
            a_g[...] = jnp.zeros_like(a_g)
            if with_mix:
                a_gt[...] = jnp.zeros_like(a_gt)

        def strip(rows):
            dh = dh_ref[rows, :]
            xv = x_ref[rows, :]
            gv = g_ref[...]
            r = lax.rsqrt(jnp.mean(xv * xv, axis=-1, keepdims=True) + EPS)
            xh = xv * r
            a_sc[...] += _fold8(dh * xh * gv)
            a_sh[...] += _fold8(dh)
            dn = dh * (1.0 + sc_ref[...])
            a_g[...] += _fold8(dn * xh)
            t = dn * gv
            dx = dr_ref[rows, :] + r * (t - xh * jnp.mean(t * xh, axis=-1, keepdims=True))
            dx_ref[rows, :] = dx
            if with_mix:
                a_gt[...] += _fold8(dx * m_ref[rows, :])
                dm_ref[rows, :] = (dx * gt_ref[...]).astype(bf16)

        _strips(tm, strip)

        @pl.when(i == n - 1)
        def _():
            dsc_ref[...] = jnp.sum(a_sc[...], axis=0, keepdims=True)
            dsh_ref[...] = jnp.sum(a_sh[...], axis=0, keepdims=True)
            dg_ref[...] = jnp.sum(a_g[...], axis=0, keepdims=True)
            if with_mix:
                dgt_ref[...] = jnp.sum(a_gt[...], axis=0, keepdims=True)

    tile, row = _tile_spec(tm, D), _row_spec(D)
    if with_mix:
        ins, args = [tile, tile, row, row, tile, tile, row], (dh, xin, g, sc, dres, mix, gt)
        outs = [tile, tile, row, row, row, row]
        shapes = [_sds((s, D), f32), _sds((s, D), bf16)] + [_sds((1, D), f32)] * 4
        nacc = 4
    else:
        ins, args = [tile, tile, row, row, tile], (dh, xin, g, sc, dres)
        outs = [tile, row, row, row]
        shapes = [_sds((s, D), f32)] + [_sds((1, D), f32)] * 3
        nacc = 3
    return pl.pallas_call(
        body, name=name, grid=(n,), in_specs=ins, out_specs=outs, out_shape=shapes,
        scratch_shapes=[pltpu.VMEM((8, D), f32)] * nacc, compiler_params=_params(("arbitrary",)),
    )(*args)


def _mix_pre(att, y, proj2, g_att, g_ssd, tm):
    s = att.shape[0]

    def body(a_ref, y_ref, z_ref, ga_ref, gs_ref, o_ref):
        def strip(rows):
            a = a_ref[rows, :]
            ra = lax.rsqrt(jnp.mean(a * a, axis=-1, keepdims=True) + EPS)
            o_ref[rows, 0:ATT_W] = (a * ra * ga_ref[...]).astype(bf16)
            z = z_ref[rows, :]
            u = y_ref[rows, :] * (z * _sigmoid(z))
            ru = lax.rsqrt(jnp.mean(u * u, axis=-1, keepdims=True) + EPS)
            o_ref[rows, ATT_W:] = (u * ru * gs_ref[...]).astype(bf16)

        _strips(tm, strip)

    t = _tile_spec(tm, ATT_W)
    return pl.pallas_call(
        body, name="mix_pre", grid=(s // tm,), in_specs=[t, t, t, _row_spec(ATT_W), _row_spec(SSD_W)],
        out_specs=_tile_spec(tm, D), out_shape=_sds((s, D), bf16), compiler_params=_params(("parallel",)),
    )(att, y, proj2, g_att, g_ssd)


def _mix_pre_bwd(dmc, att, y, proj2, g_att, g_ssd, tm):
    s = att.shape[0]
    n = s // tm

    def body(da_ref, ds_ref, a_ref, y_ref, z_ref, ga_ref, gs_ref, datt_ref, dy_ref, dz_ref, dga_ref, dgs_ref, acc_a, acc_s):
        i = pl.program_id(0)

        @pl.when(i == 0)
        def _():
            acc_a[...] = jnp.zeros_like(acc_a)
            acc_s[...] = jnp.zeros_like(acc_s)

        def strip(rows):
            a = a_ref[rows, :]
            ra = lax.rsqrt(jnp.mean(a * a, axis=-1, keepdims=True) + EPS)
            ah = a * ra
            dan = da_ref[rows, :]
            acc_a[...] += _fold8(dan * ah)
            t = dan * ga_ref[...]
            datt_ref[rows, :] = (ra * (t - ah * jnp.mean(t * ah, axis=-1, keepdims=True))).astype(bf16)
            z = z_ref[rows, :]
            yv = y_ref[rows, :]
            sz = _sigmoid(z)
            sil = z * sz
            u = yv * sil
            ru = lax.rsqrt(jnp.mean(u * u, axis=-1, keepdims=True) + EPS)
            uh = u * ru
            dsn = ds_ref[rows, :]
            acc_s[...] += _fold8(dsn * uh)
            t2 = dsn * gs_ref[...]
            du = ru * (t2 - uh * jnp.mean(t2 * uh, axis=-1, keepdims=True))
            dy_ref[rows, :] = du * sil
            dz_ref[rows, :] = (du * yv * (sz * (1.0 + z * (1.0 - sz)))).astype(bf16)

        _strips(tm, strip)

        @pl.when(i == n - 1)
        def _():
            dga_ref[...] = jnp.sum(acc_a[...], axis=0, keepdims=True)
            dgs_ref[...] = jnp.sum(acc_s[...], axis=0, keepdims=True)

    t = _tile_spec(tm, ATT_W)
    row = _row_spec(ATT_W)
    return pl.pallas_call(
        body, name="mix_pre_bwd", grid=(n,),
        in_specs=[_tile_spec(tm, ATT_W, 0), _tile_spec(tm, ATT_W, 1), t, t, t, row, row],
        out_specs=[t, t, t, row, row],
        out_shape=[_sds((s, ATT_W), bf16), _sds((s, SSD_W), f32), _sds((s, SSD_W), bf16), _sds((1, ATT_W), f32), _sds((1, SSD_W), f32)],
        scratch_shapes=[pltpu.VMEM((8, ATT_W), f32)] * 2, compiler_params=_params(("arbitrary",)),
    )(dmc, dmc, att, y, proj2, g_att, g_ssd)


ATT_GROUP = 8
ATT_GROUP_FWD = 8


def _pair_rows(qc):
    two = jnp.concatenate([qc, qc], axis=0)
    r = lax.broadcasted_iota(jnp.int32, (2 * CHUNK, 128), 0)
    l = lax.broadcasted_iota(jnp.int32, (2 * CHUNK, 128), 1)
    return jnp.where((r < CHUNK) == (l < HD), two, jnp.zeros_like(two))


def _scaled(q):
    return q * jnp.asarray(HD ** -0.5, q.dtype)


def _pair_scores(wt, kb, bias, r0, masked):
    sc = lax.dot_general(wt, kb, (((1,), (1,)), ((), ())), preferred_element_type=f32) + bias
    if not masked:
        return sc
    kidx = lax.broadcasted_iota(jnp.int32, sc.shape, 1)
    return jnp.where(r0 + kidx >= PADK, sc, -jnp.inf)


def _softmax(sc, axis):
    e = jnp.exp(sc - jnp.max(sc, axis=axis, keepdims=True))
    return e * (1.0 / jnp.sum(e, axis=axis, keepdims=True))


def _chunk_loops(nc, group, per_trip):
    n_masked = min(-(-LEFT // per_trip), nc // per_trip)

    def run(masked):
        def step(g, carry):
            group(g, masked)
            return carry
        return step

    lax.fori_loop(0, n_masked, run(True), 0)
    lax.fori_loop(n_masked, nc // per_trip, run(False), 0)


def _pair_diag(r):
    lane = lax.broadcasted_iota(jnp.int32, (CHUNK, 128), 1)
    return jnp.where(lane < HD, r[0:CHUNK], r[CHUNK:])


def _pad_keys(k_ref, kp, s):
    kp[0:PADK, :] = jnp.zeros((PADK, 128), bf16)
    kp[PADK:PADK + s, :] = k_ref[...]
    kp[PADK + s:, :] = jnp.zeros((CHUNK, 128), bf16)


def _attn_fwd(qkv, bias2):
    s = qkv.shape[0]
    nc = s // CHUNK
    npair = NH // 2

    def body(q_ref, k_ref, v_ref, b_ref, o_ref, kp, vp):
        _pad_keys(k_ref, kp, s)
        _pad_keys(v_ref, vp, s)

        def group(g, masked):
            r0s = [pl.multiple_of((g * ATT_GROUP_FWD + u) * CHUNK, CHUNK) for u in range(ATT_GROUP_FWD)]
            scs = [_pair_scores(_pair_rows(_scaled(q_ref[pl.ds(r0, CHUNK), :])), kp[pl.ds(r0, BANDP), :], b_ref[...], r0, masked)
                   for r0 in r0s]
            ps = [_softmax(sc, -1).astype(bf16) for sc in scs]
            for r0, p in zip(r0s, ps):
                o_ref[pl.ds(r0, CHUNK), :] = _pair_diag(jnp.dot(p, vp[pl.ds(r0, BANDP), :], preferred_element_type=f32))

        _chunk_loops(nc, group, ATT_GROUP_FWD)

    return pl.pallas_call(
        body, name="attn_fwd", grid=(npair,),
        in_specs=[pl.BlockSpec((s, 128), lambda p: (0, p)), pl.BlockSpec((s, 128), lambda p: (0, npair + p)),
                  pl.BlockSpec((s, 128), lambda p: (0, 2 * npair + p)), pl.BlockSpec((None, 2 * CHUNK, BANDP), lambda p: (p, 0, 0))],
        out_specs=pl.BlockSpec((s, 128), lambda p: (0, p)), out_shape=_sds((s, ATT_W), f32),
        scratch_shapes=[pltpu.VMEM((PADK + s + CHUNK, 128), bf16)] * 2, compiler_params=_params(("parallel",)),
    )(qkv, qkv, qkv, bias2)


def _attn_bwd(qkv, datt, bias2):
    s = qkv.shape[0]
    nc = s // CHUNK
    npair = NH // 2
    rows = PADK + s + CHUNK
    nt = (((1,), (1,)), ((), ()))

    def body(q_ref, k_ref, v_ref, do_ref, b_ref, dq_ref, dk_ref, dv_ref, g_ref, kp, vp, dkp, dvp):
        _pad_keys(k_ref, kp, s)
        _pad_keys(v_ref, vp, s)
        dkp[...] = jnp.zeros_like(dkp)
        dvp[...] = jnp.zeros_like(dvp)
        g_ref[...] = jnp.zeros_like(g_ref)

        def group(g, masked):
            r0s = [pl.multiple_of((g * ATT_GROUP + u) * CHUNK, CHUNK) for u in range(ATT_GROUP)]
            wts = [_pair_rows(_scaled(q_ref[pl.ds(r0, CHUNK), :])) for r0 in r0s]
            dos = [_pair_rows(do_ref[pl.ds(r0, CHUNK), :]) for r0 in r0s]
            scs = [_pair_scores(wt, kp[pl.ds(r0, BANDP), :], b_ref[...], r0, masked) for wt, r0 in zip(wts, r0s)]
            dps = [lax.dot_general(do, vp[pl.ds(r0, BANDP), :], nt, preferred_element_type=f32) for do, r0 in zip(dos, r0s)]
            tn_ = (((0,), (0,)), ((), ()))
            for r0, wt, do, sc, dp in zip(r0s, wts, dos, scs, dps):
                p = _softmax(sc, -1)
                ds = p * (dp - jnp.sum(p * dp, axis=-1, keepdims=True))
                g_ref[...] += ds
                dsb = ds.astype(bf16)
                dq = jnp.dot(dsb, kp[pl.ds(r0, BANDP), :], preferred_element_type=f32)
                dq_ref[pl.ds(r0, CHUNK), :] = (_pair_diag(dq) * (HD ** -0.5)).astype(bf16)
                dkp[pl.ds(r0, BANDP), :] += lax.dot_general(dsb, wt, tn_, preferred_element_type=f32)
                dvp[pl.ds(r0, BANDP), :] += lax.dot_general(p.astype(bf16), do, tn_, preferred_element_type=f32)

        _chunk_loops(nc, group, ATT_GROUP)
        dk_ref[...] = dkp[PADK:PADK + s, :].astype(bf16)
        dv_ref[...] = dvp[PADK:PADK + s, :].astype(bf16)

    col = lambda off: pl.BlockSpec((s, 128), lambda p: (0, off + p))
    return pl.pallas_call(
        body, name="attn_bwd", grid=(npair,),
        in_specs=[col(0), col(npair), col(2 * npair), col(0), pl.BlockSpec((None, 2 * CHUNK, BANDP), lambda p: (p, 0, 0))],
        out_specs=[col(0), col(0), col(0), pl.BlockSpec((None, 2 * CHUNK, BANDP), lambda p: (p, 0, 0))],
        out_shape=[_sds((s, ATT_W), bf16)] * 3 + [_sds((npair, 2 * CHUNK, BANDP), f32)],
        scratch_shapes=[pltpu.VMEM((rows, 128), bf16)] * 2 + [pltpu.VMEM((rows, 128), f32)] * 2,
        compiler_params=_params(("parallel",)),
    )(qkv, qkv, qkv, datt, bias2)


def _rel_tables():
    onehot = np.zeros((BANDP, N_REL), np.float32)
    for j in range(BAND + CHUNK - 1):
        o = j - (CHUNK - 1)
        onehot[j, int(np.clip(PADK - o, -(CHUNK - 1), REL_CLIP)) + CHUNK - 1] = 1.0
    return onehot, np.ascontiguousarray(np.eye(CHUNK, dtype=np.float32)[::-1])


def _expand_bias(rel):
    ext = jnp.concatenate([jnp.broadcast_to(rel[:, N_REL - 1:], (NH, N_REL - 1)), rel[:, ::-1],
                           jnp.zeros((NH, BANDP - BAND + 1), f32)], axis=1)
    band = jnp.stack([ext[:, CHUNK - 1 - q:CHUNK - 1 - q + BANDP] for q in range(CHUNK)], axis=1)
    band = jnp.where(np.arange(BANDP) < BAND, band, -jnp.inf)
    return band.reshape(NH // 2, 2 * CHUNK, BANDP)


def _rel_bias_grad(gband):
    def body(g_ref, m_ref, flip_ref, o_ref, d2):
        for h in range(NH):
            rev = jnp.dot(flip_ref[...], g_ref[h], precision=HIGHEST, preferred_element_type=f32)
            rolled = pltpu.roll(rev, 0, 1, stride=1, stride_axis=0)
            d2[h:h + 1, :] = jnp.sum(rolled, axis=0, keepdims=True)
        o_ref[...] = jnp.dot(d2[...], m_ref[...], precision=HIGHEST, preferred_element_type=f32)

    onehot, flip = _rel_tables()
    return pl.pallas_call(
        body, name="rel_bias_grad", out_shape=_sds((NH, N_REL), f32), scratch_shapes=[pltpu.VMEM((NH, BANDP), f32)],
    )(gband, jnp.asarray(onehot), jnp.asarray(flip))


XBC_BLK = 512
XBC_COL0 = SSD_W // XBC_BLK
DT_COL = (SSD_W + XBC) // 128


def _conv_taps(ext, w_ref, b_ref, tm):
    n = ext.shape[0]
    pre = w_ref[3:4, :] * ext + b_ref[...]
    for j in range(3):
        pre = pre + w_ref[j:j + 1, :] * pltpu.roll(ext, 3 - j, 0)
    return pre


def _ssd_conv(proj2, conv_w, conv_b, tm):
    s = proj2.shape[0]
    nb = XBC // XBC_BLK

    def body(x_ref, p_ref, w_ref, b_ref, o_ref):
        i = pl.program_id(1)
        prev = jnp.where(i > 0, p_ref[...], 0.0)
        ext = jnp.concatenate([prev, x_ref[...]], axis=0)
        pre = _conv_taps(ext, w_ref, b_ref, tm)[8:8 + tm]
        o_ref[...] = pre * _sigmoid(pre)

    return pl.pallas_call(
        body, name="ssd_conv", grid=(nb, s // tm),
        in_specs=[pl.BlockSpec((tm, XBC_BLK), lambda j, i: (i, XBC_COL0 + j)),
                  pl.BlockSpec((8, XBC_BLK), lambda j, i: (jnp.maximum(i * (tm // 8) - 1, 0), XBC_COL0 + j)),
                  pl.BlockSpec((4, XBC_BLK), lambda j, i: (0, j)), pl.BlockSpec((1, XBC_BLK), lambda j, i: (0, j))],
        out_specs=pl.BlockSpec((tm, XBC_BLK), lambda j, i: (i, j)), out_shape=_sds((s, XBC), f32),
        compiler_params=_params(("parallel", "parallel")),
    )(proj2, proj2, conv_w, conv_b)


def _ssd_conv_bwd(dxbc, proj2, conv_w, conv_b, tm):
    s = proj2.shape[0]
    nb = XBC // XBC_BLK
    n = s // tm
    last8 = s // 8 - 1

    def body(x_ref, xp_ref, xn_ref, d_ref, dn_ref, w_ref, b_ref, o_ref, dw_ref, db_ref, acc):
        i = pl.program_id(1)

        @pl.when(i == 0)
        def _():
            acc[...] = jnp.zeros_like(acc)

        prev = jnp.where(i > 0, xp_ref[...], 0.0)
        ext = jnp.concatenate([prev, x_ref[...], xn_ref[...]], axis=0)
        pre = _conv_taps(ext, w_ref, b_ref, tm)
        sg = _sigmoid(pre)
        dnext = jnp.where(i < n - 1, dn_ref[...], 0.0)
        dext = jnp.concatenate([jnp.zeros((8, XBC_BLK), f32), d_ref[...], dnext], axis=0)
        dpre = dext * (sg * (1.0 + pre * (1.0 - sg)))
        rows = tm + 16
        dx = w_ref[3:4, :] * dpre
        for j in range(3):
            dx = dx + w_ref[j:j + 1, :] * pltpu.roll(dpre, rows - (3 - j), 0)
        o_ref[...] = dx[8:8 + tm].astype(bf16)
        dcur = dpre[8:8 + tm]
        acc[4] += _fold8(dcur)
        acc[3] += _fold8(dcur * ext[8:8 + tm])
        for j in range(3):
            acc[j] += _fold8(dcur * pltpu.roll(ext, 3 - j, 0)[8:8 + tm])

        @pl.when(i == n - 1)
        def _():
            for j in range(4):
                dw_ref[j:j + 1, :] = jnp.sum(acc[j], axis=0, keepdims=True)
            db_ref[...] = jnp.sum(acc[4], axis=0, keepdims=True)

    xcol = lambda j: XBC_COL0 + j
    return pl.pallas_call(
        body, name="ssd_conv_bwd", grid=(nb, n),
        in_specs=[pl.BlockSpec((tm, XBC_BLK), lambda j, i: (i, xcol(j))),
                  pl.BlockSpec((8, XBC_BLK), lambda j, i: (jnp.maximum(i * (tm // 8) - 1, 0), xcol(j))),
                  pl.BlockSpec((8, XBC_BLK), lambda j, i: (jnp.minimum((i + 1) * (tm // 8), last8), xcol(j))),
                  pl.BlockSpec((tm, XBC_BLK), lambda j, i: (i, j)),
                  pl.BlockSpec((8, XBC_BLK), lambda j, i: (jnp.minimum((i + 1) * (tm // 8), last8), j)),
                  pl.BlockSpec((4, XBC_BLK), lambda j, i: (0, j)), pl.BlockSpec((1, XBC_BLK), lambda j, i: (0, j))],
        out_specs=[pl.BlockSpec((tm, XBC_BLK), lambda j, i: (i, j)), pl.BlockSpec((4, XBC_BLK), lambda j, i: (0, j)),
                   pl.BlockSpec((1, XBC_BLK), lambda j, i: (0, j))],
        out_shape=[_sds((s, XBC), bf16), _sds((4, XBC), f32), _sds((1, XBC), f32)],
        scratch_shapes=[pltpu.VMEM((5, 8, XBC_BLK), f32)], compiler_params=_params(("parallel", "arbitrary")),
    )(proj2, proj2, proj2, dxbc, dxbc, conv_w, conv_b)


def _ssd_consts():
    ex = np.zeros((128, SSD_W), np.float32)
    for h in range(NH):
        ex[h, h * HD:(h + 1) * HD] = 1.0
    sel = np.zeros((8, 128), np.float32)
    for h in range(NH):
        sel[h // 2, h] = 1.0
    par = np.zeros((128, 128), np.float32)
    for r in range(128):
        for h in range(NH):
            par[r, h] = 1.0 if (h % 2) == (r // 64) else 0.0
    ones_blk = np.zeros((128, 128), np.float32)
    for r in range(128):
        ones_blk[r, (r // 64) * 64:(r // 64) * 64 + 64] = 1.0
    return ex, np.ascontiguousarray(ex.T), sel, par, ones_blk


SSD_SUB = 8


def _ssd_common(rs, xbc_ref, dtr_ref, a_ref, dtb_ref, ex_ref, sel_ref, par_ref):
    xs = xbc_ref[rs, 0:SSD_W]
    dt = _softplus(dtr_ref[rs, :] + dtb_ref[...])
    adt = dt * a_ref[...]
    r_i = lax.broadcasted_iota(jnp.int32, (CHUNK, CHUNK), 0)
    c_i = lax.broadcasted_iota(jnp.int32, (CHUNK, CHUNK), 1)
    tril = (r_i >= c_i).astype(f32)
    cs = _dot01(tril, adt, exact="a")
    cs2 = jnp.concatenate([cs, cs], axis=0) * par_ref[...]
    cstp = _dot01(sel_ref[...], cs2, tb=True, exact="a")
    both = _dot01(jnp.concatenate([dt, cs], axis=0), ex_ref[...])
    return xs, dt, cs, cstp, both[0:CHUNK], both[CHUNK:]


def _pair_mask():
    l_i = lax.broadcasted_iota(jnp.int32, (CHUNK, 128), 0)
    lane = lax.broadcasted_iota(jnp.int32, (CHUNK, 128), 1)
    return l_i >= (lane % CHUNK), lane < HD


def _block_diag(xp, first):
    z = jnp.zeros_like(xp)
    return jnp.concatenate([jnp.where(first, xp, z), jnp.where(first, z, xp)], axis=0)


def _ssd_fwd(xbc, proj2, a_row, dtb_row, dsk_full):
    s = xbc.shape[0]
    nc = s // CHUNK
    ex, ext, sel, par, ones_blk = _ssd_consts()

    def one_chunk(sub, states, refs):
        xbc_ref, dtr_ref, a_ref, dtb_ref, dsk_ref, ex_ref, sel_ref, par_ref, y_ref, hs_ref = refs
        rs = slice(sub * CHUNK, (sub + 1) * CHUNK)
        xs, dt, cs, cstp, dt_full, cs_full = _ssd_common(rs, xbc_ref, dtr_ref, a_ref, dtb_ref, ex_ref, sel_ref, par_ref)
        cs_last = cs_full[CHUNK - 1:CHUNK, :]
        xdt = xs * dt_full
        causal, first = _pair_mask()
        out = []
        for g in range(NG):
            gl = slice(g * GW, (g + 1) * GW)
            bg = xbc_ref[rs, SSD_W + g * NSTATE:SSD_W + (g + 1) * NSTATE].astype(bf16)
            cg = xbc_ref[rs, SSD_W + NG * NSTATE + g * NSTATE:SSD_W + NG * NSTATE + (g + 1) * NSTATE].astype(bf16)
            cb2 = lax.dot_general(cg, jnp.concatenate([bg, bg], axis=0), (((1,), (1,)), ((), ())), preferred_element_type=f32)
            hg = states[g]
            hs_ref[sub, g] = hg
            y0 = jnp.dot(cg, hg.astype(bf16), preferred_element_type=f32)
            yoff = jnp.exp(cs_full[:, gl]) * y0
            for j in range(GW // 128):
                pair = g * (GW // 128) + j
                pl_ = slice(pair * 128, (pair + 1) * 128)
                seg = jnp.exp(jnp.where(causal, cs_full[:, pl_] - cstp[pair:pair + 1, :], -jnp.inf))
                m = (cb2 * seg).astype(bf16)
                yd = jnp.dot(m, _block_diag(xdt[:, pl_].astype(bf16), first), preferred_element_type=f32)
                y_ref[rs, pl_] = yd + yoff[:, j * 128:(j + 1) * 128] + xs[:, pl_] * dsk_ref[:, pl_]
            xdec = (xdt[:, gl] * jnp.exp(cs_last[:, gl] - cs_full[:, gl])).astype(bf16)
            st = lax.dot_general(bg, xdec, (((0,), (0,)), ((), ())), preferred_element_type=f32)
            out.append(jnp.exp(cs_last[:, gl]) * hg + st)
        return out

    def body(*refs):
        hst = refs[-1]

        @pl.when(pl.program_id(0) == 0)
        def _():
            hst[...] = jnp.zeros_like(hst)

        states = [hst[g] for g in range(NG)]
        for sub in range(SSD_SUB):
            states = one_chunk(sub, states, refs[:-1])
        for g in range(NG):
            hst[g] = states[g]

    rows = SSD_SUB * CHUNK
    const = lambda shape: pl.BlockSpec(shape, lambda c: tuple(0 for _ in shape))
    return pl.pallas_call(
        body, name="ssd_fwd", grid=(nc // SSD_SUB,),
        in_specs=[pl.BlockSpec((rows, XBC), lambda c: (c, 0)), pl.BlockSpec((rows, 128), lambda c: (c, DT_COL)),
                  const((1, 128)), const((1, 128)), const((1, SSD_W)), const((128, SSD_W)), const((8, 128)), const((128, 128))],
        out_specs=[pl.BlockSpec((rows, SSD_W), lambda c: (c, 0)), pl.BlockSpec((SSD_SUB, NG, NSTATE, GW), lambda c: (c, 0, 0, 0))],
        out_shape=[_sds((s, SSD_W), f32), _sds((nc, NG, NSTATE, GW), f32)],
        scratch_shapes=[pltpu.VMEM((NG, NSTATE, GW), f32)], compiler_params=_params(("arbitrary",)),
    )(xbc, proj2, a_row, dtb_row, dsk_full, jnp.asarray(ex), jnp.asarray(sel), jnp.asarray(par))


def _ssd_bwd(xbc, proj2, dy, hsave, a_row, dtb_row, dsk_full):
    s = xbc.shape[0]
    nc = s // CHUNK
    ex, ext, sel, par, ones_blk = _ssd_consts()

    def one_chunk(sub, dhs, refs):
        (xbc_ref, dtr_ref, dy_ref, hs_ref, a_ref, dtb_ref, dsk_ref, ex_ref, ext_ref, sel_ref, par_ref, ob_ref,
         dxbc_ref, ddtr_ref, dd_ref, da_ref, ddtb_ref, dh, a_dd, a_da, a_dtb, dcs_lane, dcs_b, dxdt) = refs
        rs = slice(sub * CHUNK, (sub + 1) * CHUNK)
        dcs_lane, dcs_b, dxdt = dcs_lane.at[sub], dcs_b.at[sub], dxdt.at[sub]
        xs, dt, cs, cstp, dt_full, cs_full = _ssd_common(rs, xbc_ref, dtr_ref, a_ref, dtb_ref, ex_ref, sel_ref, par_ref)
        cs_last = cs_full[CHUNK - 1:CHUNK, :]
        xdt = xs * dt_full
        dyv = dy_ref[rs, :]
        a_dd[...] += _fold8(dyv * xs)
        causal, first = _pair_mask()
        diag = lax.broadcasted_iota(jnp.int32, (CHUNK, 128), 0) == lax.broadcasted_iota(jnp.int32, (CHUNK, 128), 1) % CHUNK
        dh_out = []
        for g in range(NG):
            gl = slice(g * GW, (g + 1) * GW)
            bcol = slice(SSD_W + g * NSTATE, SSD_W + (g + 1) * NSTATE)
            ccol = slice(SSD_W + NG * NSTATE + g * NSTATE, SSD_W + NG * NSTATE + (g + 1) * NSTATE)
            bg = xbc_ref[rs, bcol].astype(bf16)
            cg = xbc_ref[rs, ccol].astype(bf16)
            bg2 = jnp.concatenate([bg, bg], axis=0)
            cb2 = lax.dot_general(cg, bg2, (((1,), (1,)), ((), ())), preferred_element_type=f32)
            hg = hs_ref[sub, g]
            hgb = hg.astype(bf16)
            dhg = dhs[g]
            dhgb = dhg.astype(bf16)
            eg = jnp.exp(cs_full[:, gl])
            dec = jnp.exp(cs_last[:, gl] - cs_full[:, gl])
            gam = jnp.exp(cs_last[:, gl])
            dyg = dyv[:, gl]
            xdt_g = xdt[:, gl]
            y0 = jnp.dot(cg, hgb, preferred_element_type=f32)
            dy0 = (eg * dyg).astype(bf16)
            dcm = lax.dot_general(dy0, hgb, (((1,), (1,)), ((), ())), preferred_element_type=f32)
            dh_prev = gam * dhg + lax.dot_general(cg, dy0, (((0,), (0,)), ((), ())), preferred_element_type=f32)
            dgam = jnp.sum(dhg * hg, axis=0, keepdims=True) * gam
            dxdec = jnp.dot(bg, dhgb, preferred_element_type=f32)
            dbm = lax.dot_general((xdt_g * dec).astype(bf16), dhgb, (((1,), (1,)), ((), ())), preferred_element_type=f32)
            t = dxdec * xdt_g * dec
            dcs_lane[:, gl] = dyg * eg * y0 - t
            dcs_lane[CHUNK - 1:CHUNK, gl] += jnp.sum(t, axis=0, keepdims=True) + dgam
            dxdt[:, gl] = dxdec * dec
            dcb2 = jnp.zeros((CHUNK, 128), f32)
            for j in range(GW // 128):
                pair = g * (GW // 128) + j
                pl_ = slice(pair * 128, (pair + 1) * 128)
                seg = jnp.exp(jnp.where(causal, cs_full[:, pl_] - cstp[pair:pair + 1, :], -jnp.inf))
                m = cb2 * seg
                mb = m.astype(bf16)
                rhs = _block_diag(xdt[:, pl_].astype(bf16), first)
                dyp = dyv[:, pl_].astype(bf16)
                dm = lax.dot_general(dyp, rhs, (((1,), (1,)), ((), ())), preferred_element_type=f32)
                tt = lax.dot_general(mb, dyp, (((0,), (0,)), ((), ())), preferred_element_type=f32)
                dxdt[:, pl_] += jnp.where(first, tt[0:CHUNK], tt[CHUNK:])
                dcb2 = dcb2 + dm * seg
                w = dm * m
                colsum = jnp.sum(w, axis=0, keepdims=True)
                dcs_b[:, pl_] = _dot01(w - jnp.where(diag, colsum, 0.0), ob_ref[...])
            dcb2b = dcb2.astype(bf16)
            dcm = dcm + jnp.dot(dcb2b, bg2, preferred_element_type=f32)
            t3 = lax.dot_general(dcb2b, cg, (((0,), (0,)), ((), ())), preferred_element_type=f32)
            dxbc_ref[rs, bcol] = dbm + t3[0:CHUNK] + t3[CHUNK:]
            dxbc_ref[rs, ccol] = dcm
            dh_out.append(dh_prev)
        dxdtv = dxdt[...]
        both = _dot01(jnp.concatenate([dcs_lane[...] + dcs_b[...] * (1.0 / HD), dxdtv * xs], axis=0), ext_ref[...])
        dcs = both[0:CHUNK]
        r_i = lax.broadcasted_iota(jnp.int32, (CHUNK, CHUNK), 0)
        c_i = lax.broadcasted_iota(jnp.int32, (CHUNK, CHUNK), 1)
        triu = (r_i <= c_i).astype(f32)
        da_ = _dot01(triu, dcs, exact="a")
        ddt = da_ * a_ref[...] + both[CHUNK:]
        a_da[...] += _fold8(da_ * dt)
        dxbc_ref[rs, 0:SSD_W] = dyv * dsk_ref[...] + dxdtv * dt_full
        ddtr = ddt * _sigmoid(dtr_ref[rs, :] + dtb_ref[...])
        ddtr_ref[rs, :] = ddtr
        a_dtb[...] += _fold8(ddtr)
        return dh_out

    nsteps = nc // SSD_SUB

    def body(*refs):
        dd_ref, da_ref, ddtb_ref, dh, a_dd, a_da, a_dtb = refs[14:21]
        ext_ref = refs[8]
        step = pl.program_id(0)

        @pl.when(step == 0)
        def _():
            dh[...] = jnp.zeros_like(dh)
            a_dd[...] = jnp.zeros_like(a_dd)
            a_da[...] = jnp.zeros_like(a_da)
            a_dtb[...] = jnp.zeros_like(a_dtb)

        dhs = [dh[g] for g in range(NG)]
        for sub in reversed(range(SSD_SUB)):
            dhs = one_chunk(sub, dhs, refs)
        for g in range(NG):
            dh[g] = dhs[g]

        @pl.when(step == nsteps - 1)
        def _():
            dd_ref[...] = jnp.sum(jnp.dot(a_dd[...], ext_ref[...], precision=HIGHEST, preferred_element_type=f32), axis=0, keepdims=True)
            da_ref[...] = jnp.sum(a_da[...], axis=0, keepdims=True)
            ddtb_ref[...] = jnp.sum(a_dtb[...], axis=0, keepdims=True)

    rev = lambda c: nsteps - 1 - c
    rows = SSD_SUB * CHUNK
    const = lambda shape: pl.BlockSpec(shape, lambda c: tuple(0 for _ in shape))
    return pl.pallas_call(
        body, name="ssd_bwd", grid=(nsteps,),
        in_specs=[pl.BlockSpec((rows, XBC), lambda c: (rev(c), 0)), pl.BlockSpec((rows, 128), lambda c: (rev(c), DT_COL)),
                  pl.BlockSpec((rows, SSD_W), lambda c: (rev(c), 0)), pl.BlockSpec((SSD_SUB, NG, NSTATE, GW), lambda c: (rev(c), 0, 0, 0)),
                  const((1, 128)), const((1, 128)), const((1, SSD_W)), const((128, SSD_W)), const((SSD_W, 128)),
                  const((8, 128)), const((128, 128)), const((128, 128))],
        out_specs=[pl.BlockSpec((rows, XBC), lambda c: (rev(c), 0)), pl.BlockSpec((rows, 128), lambda c: (rev(c), 0)),
                   const((1, 128)), const((1, 128)), const((1, 128))],
        out_shape=[_sds((s, XBC), f32), _sds((s, 128), f32), _sds((1, 128), f32), _sds((1, 128), f32), _sds((1, 128), f32)],
        scratch_shapes=[pltpu.VMEM((NG, NSTATE, GW), f32), pltpu.VMEM((8, SSD_W), f32), pltpu.VMEM((8, 128), f32), pltpu.VMEM((8, 128), f32)]
        + [pltpu.VMEM((SSD_SUB, CHUNK, SSD_W), f32)] * 3,
        compiler_params=_params(("arbitrary",)),
    )(xbc, proj2, dy, hsave, a_row, dtb_row, dsk_full, jnp.asarray(ex), jnp.asarray(ext), jnp.asarray(sel), jnp.asarray(par),
      jnp.asarray(ones_blk))


def _local_step(x, tgt, mods, g_mix, rel, conv_w, conv_b, dt_bias, a_log, d_skip, g_att, g_ssd, g_ffn, g_final, weights):
    s = x.shape[0]
    tm_e = 512 if s % 512 == 0 else s
    tm_m = 512 if s % 512 == 0 else s
    tm_l = 1024 if s % 1024 == 0 else s
    tk = 2048 if s % 2048 == 0 else s
    sh1, sc1, gt1, sh2, sc2, gt2 = [mods[:, i * D:(i + 1) * D] for i in range(6)]

    h1b = _norm_mod("norm_mod_1", x, g_mix, sc1, sh1, tm_e)
    win, win_b = weights.w_in(h1b)
    qkv = _mm_nn_fullk("proj_qkv", h1b, win, tm_l, 1536, bf16, n=IN_A)
    proj2 = _mm_nn_fullk("proj_zxbcdt", h1b, win_b, tm_l, 896, f32)
    bias = _expand_bias(rel)
    att = _attn_fwd(qkv, bias)
    xbc = _ssd_conv(proj2, conv_w, conv_b, tm_l)
    a_row = jnp.pad(-jnp.exp(a_log), ((0, 0), (0, 128 - NH)))
    dtb_row = jnp.pad(dt_bias, ((0, 0), (0, 128 - NH)))
    dsk_full = jnp.repeat(d_skip, HD, axis=1)
    y, hsave = _ssd_fwd(xbc, proj2, a_row, dtb_row, dsk_full)
    mixcat = _mix_pre(att, y, proj2, g_att, g_ssd, tm_e)
    wout = weights.w_out(mixcat)
    mix = _mm_nn_fullk("proj_out", mixcat, wout, tm_l, D, f32)
    x2, h2b = _resid_norm_mod(x, gt1, mix, g_ffn, sc2, sh2, tm_e)
    wg4, wu4, wd4 = weights.ffn(h2b)
    act, sil, ud = _ffn_up(h2b, wg4, wu4, tm_m)
    ffn = _ffn_down(act, wd4, tm_l)

    dx3, dffn, loss, dg_final, dgt2 = _final_fwd_bwd(x2, ffn, gt2, g_final, tgt, tm_e)
    tok = weights.grad(("w_down",), [_grad_wdown4(act, dffn, 1024, tk)])
    dgate, dup = _ffn_dact(dffn, wd4, sil, ud, tm_l, dep=tok)
    tok = weights.grad(("w_gate", "w_up"), [_grad_cols4("grad_w_gate", h2b, dgate, 1024, tk), _grad_cols4("grad_w_up", h2b, dup, 1024, tk)])
    dh2 = _ffn_dh(dgate, dup, wg4, wu4, tm_m, dep=tok)
    dx2, dmix, dsc2, dsh2, dg_ffn, dgt1 = _norm_mod_bwd("norm_mod_bwd_2", dh2, x2, g_ffn, sc2, dx3, tm_e, mix=mix, gt=gt1)
    tok = weights.grad(("w_out",), [_mm_tn("grad_w_out", mixcat, dmix, 1024, 1024, tk, bf16).reshape(NSH, D // NSH, D)])
    dmc = _mm_nt("dmixcat", dmix, wout, tm_l, D, D, f32, dep=tok)
    datt, dy, dz, dg_att, dg_ssd = _mix_pre_bwd(dmc, att, y, proj2, g_att, g_ssd, tm_e)
    dq, dk, dv, gband = _attn_bwd(qkv, datt, bias)
    drel = _rel_bias_grad(gband.reshape(NH, CHUNK, BANDP))
    dxbc, ddtr, dd_row, da_row, ddtb_row = _ssd_bwd(xbc, proj2, dy, hsave, a_row, dtb_row, dsk_full)
    dxbc_raw, dconv_w, dconv_b = _ssd_conv_bwd(dxbc, proj2, conv_w, conv_b, tm_e)
    dproj = jnp.concatenate([dq, dk, dv, dz, dxbc_raw, ddtr.astype(bf16)], axis=1)
    gwin = _mm_tn("grad_w_in", h1b, dproj, 1024, 1152, tk, bf16)
    gwin4 = jnp.stack([jnp.pad(gwin[:, k * IN_SH:(k + 1) * IN_SH], ((0, 0), (0, IN_SHP - IN_SH))) for k in range(NSH)])
    tok = weights.grad(("w_in",), [gwin4])
    dh1 = _mm_nt("dh1", dproj, win, tm_l, 1024, 1920, f32, dep=tok)
    grad_x, dsc1, dsh1, dg_mix = _norm_mod_bwd("norm_mod_bwd_1", dh1, x, g_mix, sc1, dx2, tm_e)

    dmods = jnp.concatenate([dsh1, dsc1, dgt1, dsh2, dsc2, dgt2], axis=1)
    dd_skip = dd_row[:, :NH]
    da_log = da_row[:, :NH] * a_row[:, :NH]
    small = dict(g_mix=dg_mix, conv_b=dconv_b, dt_bias=ddtb_row[:, :NH], a_log=da_log, d_skip=dd_skip, g_att_out=dg_att,
                 g_ssd_out=dg_ssd, g_ffn=dg_ffn, g_final=dg_final, rel_bias=drel, conv_w=dconv_w)
    return loss[0, 0], grad_x, dmods, small


HBM = pl.BlockSpec(memory_space=pl.ANY)
VMEM = pl.BlockSpec(memory_space=pltpu.VMEM)


def _place():
    x, y, c = lax.axis_index("x"), lax.axis_index("y"), lax.axis_index("c")
    chips = [(1 - x, y), (x, 1 - y), (1 - x, 1 - y)]
    return x, y, c, chips


def _allgather8(name, payload, dep=None):
    r = payload.shape[0]
    deps = [] if dep is None else [dep]

    def body(x_ref, *rest):
        out_ref, send_sems, recv_sems, local_sem = rest[-4:]
        x, y, c, chips = _place()
        me, sibling = (x, y, c), (x, y, 1 - c)

        def slot(px, py, pc):
            return out_ref.at[4 * px + 2 * py + pc]

        def copy(k, block, to, src=None):
            return pltpu.make_async_remote_copy(
                src_ref=slot(*block) if src is None else src, dst_ref=slot(*block),
                send_sem=send_sems.at[k], recv_sem=recv_sems.at[k], device_id=to, device_id_type=MESH)

        mine = pltpu.make_async_copy(x_ref, slot(*me), local_sem)
        mine.start()
        first = [copy(0, me, sibling, src=x_ref)]
        first += [copy(1 + j, me, (*chip, c), src=x_ref) for j, chip in enumerate(chips)]
        for cp in first:
            cp.start()
        passed = [copy(4 + j, (*chip, c), sibling) for j, chip in enumerate(chips)]
        for j, chip in enumerate(chips):
            copy(1 + j, (*chip, c), me).wait_recv()
            passed[j].start()
        copy(0, sibling, me).wait_recv()
        for j, chip in enumerate(chips):
            copy(4 + j, (*chip, 1 - c), me).wait_recv()
        for cp in first + passed:
            cp.wait_send()
        mine.wait()

    return pl.pallas_call(
        body, name=name, out_shape=_sds((N_DEV, r, 128), f32), in_specs=[VMEM] * (1 + len(deps)), out_specs=VMEM,
        scratch_shapes=[pltpu.SemaphoreType.DMA((7,)), pltpu.SemaphoreType.DMA((7,)), pltpu.SemaphoreType.DMA],
    )(payload, *deps)


def _sum8(g):
    r = g.shape[1]

    def body(g_ref, o_ref):
        acc = g_ref[0]
        for i in range(1, N_DEV):
            acc = acc + g_ref[i]
        o_ref[...] = acc

    return pl.pallas_call(body, name="sum8", out_shape=_sds((r, 128), f32))(g)


SEM = pl.BlockSpec(memory_space=pltpu.SEMAPHORE)
EFFECT = pltpu.SideEffectType.DATAFLOW_SIDE_EFFECTING


def _gather_copies(ins, lands, send_sems, recv_sems):
    x, y, c, chips = _place()
    k = 2 * x + y
    starts, recvs = [], []
    for w in range(len(ins)):
        for j, (px, py) in enumerate(chips):
            def mk(dst):
                return pltpu.make_async_remote_copy(src_ref=ins[w].at[c], dst_ref=dst, send_sem=send_sems[w].at[j],
                                                    recv_sem=recv_sems[w].at[j], device_id=(px, py, c), device_id_type=MESH)
            starts.append(mk(lands[w].at[k, c]))
            recvs.append(mk(lands[w].at[2 * px + py, c]))
    return starts, recvs


def _reduce_copies(ins, lands, send_sems, recv_sems):
    x, y, c, chips = _place()
    k = 2 * x + y
    starts, recvs = [], []
    for w in range(len(ins)):
        for j, (px, py) in enumerate(chips):
            def mk(dst):
                return pltpu.make_async_remote_copy(src_ref=ins[w].at[2 * px + py], dst_ref=dst, send_sem=send_sems[w].at[j],
                                                    recv_sem=recv_sems[w].at[j], device_id=(px, py, c), device_id_type=MESH)
            starts.append(mk(lands[w].at[k]))
            recvs.append(mk(lands[w].at[2 * px + py]))
    return starts, recvs


def _split_start(name, copies, srcs, land_shapes):
    nw = len(srcs)

    def body(*refs):
        starts, _ = copies(refs[:nw], refs[nw:2 * nw], refs[2 * nw:3 * nw], refs[3 * nw:4 * nw])
        for cp in starts:
            cp.start()
        refs[6 * nw][...] = jnp.zeros((8, 128), f32)

    sems = [pltpu.SemaphoreType.DMA((3,))] * nw
    bufs = [pltpu.HBM(s.shape, bf16) for s in srcs] + [pltpu.HBM(s, bf16) for s in land_shapes]
    res = pl.pallas_call(
        body, name=name, out_shape=sems + sems + bufs + [_sds((8, 128), f32)],
        in_specs=[HBM] * (2 * nw), out_specs=[SEM] * (2 * nw) + [HBM] * (2 * nw) + [VMEM],
        input_output_aliases={i: 2 * nw + i for i in range(2 * nw)},
        compiler_params=pltpu.CompilerParams(has_side_effects=EFFECT),
    )(*[pltpu.with_memory_space_constraint(s, pltpu.HBM) for s in srcs],
      *[pltpu.with_memory_space_constraint(lax.empty(s, bf16), pltpu.HBM) for s in land_shapes])
    return res[:nw], res[nw:2 * nw], res[2 * nw:3 * nw], res[3 * nw:4 * nw], res[4 * nw]


def _split_wait(name, copies, send_sems, recv_sems, srcs, lands, after):
    nw = len(srcs)

    def body(*refs):
        starts, recvs = copies(refs[:nw], refs[nw:2 * nw], refs[2 * nw:3 * nw], refs[3 * nw:4 * nw])
        for s_, r_ in zip(starts, recvs):
            s_.wait_send()
            r_.wait_recv()

    bufs = [pltpu.HBM(s.shape, bf16) for s in srcs] + [pltpu.HBM(l.shape, bf16) for l in lands]
    res = pl.pallas_call(
        body, name=name, out_shape=bufs, in_specs=[HBM] * (2 * nw) + [SEM] * (2 * nw) + [HBM], out_specs=[HBM] * (2 * nw),
        input_output_aliases={i: i for i in range(2 * nw)},
        compiler_params=pltpu.CompilerParams(has_side_effects=EFFECT),
    )(*srcs, *lands, *send_sems, *recv_sems, after)
    return res[:nw], res[nw:]


def _gather_forward(name, shards, lands):
    nw = len(shards)

    def body(*refs):
        ins, lands_in, outs = refs[:nw], refs[nw:2 * nw], refs[2 * nw:3 * nw]
        st_a, st_b, st_c = refs[3 * nw:4 * nw], refs[4 * nw:5 * nw], refs[5 * nw:6 * nw]
        send_sems, recv_sems, load_sems, store_sems = refs[6 * nw:]
        x, y, c, chips = _place()
        k = 2 * x + y
        sibling = (x, y, 1 - c)
        ld_a = [pltpu.make_async_copy(ins[w].at[c], st_a[w], load_sems.at[w, 0]) for w in range(nw)]
        ld_b = [pltpu.make_async_copy(ins[w].at[1 - c], st_b[w], load_sems.at[w, 1]) for w in range(nw)]
        for cp in ld_a + ld_b:
            cp.start()
        st_own = []
        for w in range(nw):
            ld_a[w].wait()
            st_own.append(pltpu.make_async_copy(st_a[w], outs[w].at[k, c], store_sems.at[w, 0]))
            st_own[-1].start()
        for w in range(nw):
            ld_b[w].wait()
            st_own.append(pltpu.make_async_copy(st_b[w], outs[w].at[k, 1 - c], store_sems.at[w, 1]))
            st_own[-1].start()
        for cp in st_own:
            cp.wait()
        fwds = {}
        for j, (px, py) in enumerate(chips):
            kq = 2 * px + py
            for w in range(nw):
                slot = st_b[w] if j % 2 == 0 else st_c[w]
                if j == 2:
                    fwds[w, 0].wait_send()
                ld = pltpu.make_async_copy(lands_in[w].at[kq, c], slot, load_sems.at[w, 2 + j])
                ld.start()
                ld.wait()
                fwds[w, j] = pltpu.make_async_remote_copy(src_ref=slot, dst_ref=outs[w].at[kq, c], send_sem=send_sems.at[w, j],
                                                          recv_sem=recv_sems.at[w, j], device_id=sibling, device_id_type=MESH)
                fwds[w, j].start()
        for j, (px, py) in enumerate(chips):
            for w in range(nw):
                pltpu.make_async_remote_copy(src_ref=st_c[w], dst_ref=outs[w].at[2 * px + py, 1 - c], send_sem=send_sems.at[w, j],
                                             recv_sem=recv_sems.at[w, j], device_id=sibling, device_id_type=MESH).wait_recv()
        for w in range(nw):
            fwds[w, 1].wait_send()
            fwds[w, 2].wait_send()

    stage = [pltpu.VMEM(s.shape[1:], bf16) for s in shards]
    return pl.pallas_call(
        body, name=name, out_shape=[_sds(l.shape, bf16) for l in lands],
        in_specs=[HBM] * (2 * nw), out_specs=[HBM] * nw, input_output_aliases={nw + w: w for w in range(nw)},
        scratch_shapes=stage * 3 + [pltpu.SemaphoreType.DMA((nw, 3)), pltpu.SemaphoreType.DMA((nw, 3)), pltpu.SemaphoreType.DMA((nw, 5)),
                                    pltpu.SemaphoreType.DMA((nw, 2))],
        compiler_params=pltpu.CompilerParams(vmem_limit_bytes=VMEM_LIMIT),
    )(*shards, *lands)


def _rs_pair_exchange(name, grads):
    nw = len(grads)

    def body(*refs):
        ins, got, stage = refs[:nw], refs[nw:2 * nw], refs[2 * nw:3 * nw]
        send_sems, recv_sems, load_sems = refs[3 * nw:]
        x, y, c, _ = _place()

        def load(w, kk):
            return pltpu.make_async_copy(ins[w].at[kk, 1 - c], stage[w].at[kk % 2], load_sems.at[w, kk])

        def send(w, kk):
            return pltpu.make_async_remote_copy(src_ref=stage[w].at[kk % 2], dst_ref=got[w].at[kk], send_sem=send_sems.at[w, kk],
                                                recv_sem=recv_sems.at[w, kk], device_id=(x, y, 1 - c), device_id_type=MESH)

        for kk in range(2):
            for w in range(nw):
                load(w, kk).start()
        for kk in range(NSH):
            for w in range(nw):
                load(w, kk).wait()
                send(w, kk).start()
            if kk + 2 < NSH:
                for w in range(nw):
                    send(w, kk).wait_send()
                    load(w, kk + 2).start()
        for kk in range(NSH - 2, NSH):
            for w in range(nw):
                send(w, kk).wait_send()
        for kk in range(NSH):
            for w in range(nw):
                send(w, kk).wait_recv()

    return pl.pallas_call(
        body, name=name, out_shape=[_sds((NSH,) + g.shape[2:], bf16) for g in grads], in_specs=[HBM] * nw, out_specs=[HBM] * nw,
        scratch_shapes=[pltpu.VMEM((2,) + g.shape[2:], bf16) for g in grads]
        + [pltpu.SemaphoreType.DMA((nw, NSH)), pltpu.SemaphoreType.DMA((nw, NSH)), pltpu.SemaphoreType.DMA((nw, NSH))],
        compiler_params=pltpu.CompilerParams(vmem_limit_bytes=VMEM_LIMIT),
    )(*grads)


def _rs_pair_gather(name, halves):
    nw = len(halves)

    def body(*refs):
        ins, outs, stage = refs[:nw], refs[nw:2 * nw], refs[2 * nw:3 * nw]
        send_sems, recv_sems, local_sems, stage_sems = refs[3 * nw:]
        x, y, c, _ = _place()
        loads = [pltpu.make_async_copy(ins[w], stage[w], stage_sems.at[w]) for w in range(nw)]
        for cp in loads:
            cp.start()
        local, cps = [], []
        for w in range(nw):
            loads[w].wait()
            local.append(pltpu.make_async_copy(stage[w], outs[w].at[c], local_sems.at[w]))
            cps.append(pltpu.make_async_remote_copy(src_ref=stage[w], dst_ref=outs[w].at[c], send_sem=send_sems.at[w],
                                                    recv_sem=recv_sems.at[w], device_id=(x, y, 1 - c), device_id_type=MESH))
            local[w].start()
            cps[w].start()
        for w in range(nw):
            pltpu.make_async_remote_copy(src_ref=stage[w], dst_ref=outs[w].at[1 - c], send_sem=send_sems.at[w], recv_sem=recv_sems.at[w],
                                         device_id=(x, y, 1 - c), device_id_type=MESH).wait_recv()
        for cp in cps:
            cp.wait_send()
        for cp in local:
            cp.wait()

    return pl.pallas_call(
        body, name=name, out_shape=[_sds((2,) + h.shape, f32) for h in halves], in_specs=[HBM] * nw, out_specs=[HBM] * nw,
        scratch_shapes=[pltpu.VMEM(h.shape, f32) for h in halves]
        + [pltpu.SemaphoreType.DMA((nw,)), pltpu.SemaphoreType.DMA((nw,)), pltpu.SemaphoreType.DMA((nw,)), pltpu.SemaphoreType.DMA((nw,))],
        compiler_params=pltpu.CompilerParams(vmem_limit_bytes=VMEM_LIMIT),
    )(*halves)


def _row_tile(r, c, nbuf):
    budget = 24 * 1024 * 1024 // (2 * nbuf * 4 * c)
    t = 8
    while t * 2 <= budget and r % (t * 2) == 0:
        t *= 2
    return t


def _cast_bf16(name, a, dep=None):
    r, c = a.shape
    tr = _row_tile(r, c, 2)
    dep_specs, dep_ops = _dep_args(dep, 1)

    def body(a_ref, *rest):
        rest[-1][...] = a_ref[...].astype(bf16)

    spec = pl.BlockSpec((tr, c), lambda i: (i, 0))
    return pl.pallas_call(body, name=name, grid=(r // tr,), in_specs=[spec] + dep_specs, out_specs=spec, out_shape=_sds((r, c), bf16),
                          compiler_params=_params(("parallel",)))(a, *dep_ops)


def _w_in_columns(win4):
    tr = 256

    def body(a_ref, o_ref, ob_ref):
        for k in range(NSH):
            o_ref[:, IN_SH * k:IN_SH * (k + 1)] = a_ref[k][:, :IN_SH]
        o_ref[:, IN_COLS:] = jnp.zeros((tr, IN_P - IN_COLS), bf16)
        ob_ref[...] = o_ref[:, IN_A:]

    return pl.pallas_call(
        body, name="w_in_columns", grid=(D // tr,), in_specs=[pl.BlockSpec((NSH, tr, IN_SHP), lambda i: (0, i, 0))],
        out_specs=[pl.BlockSpec((tr, IN_P), lambda i: (i, 0)), pl.BlockSpec((tr, IN_B), lambda i: (i, 0))],
        out_shape=[_sds((D, IN_P), bf16), _sds((D, IN_B), bf16)], compiler_params=_params(("parallel",)))(win4)


def _pair_sum(name, core, grads, got):
    _, _, rh, c = grads.shape
    tr = _row_tile(rh, c, 2)

    def body(c_ref, a_ref, b_ref, o_ref):
        o_ref[...] = (a_ref[...].astype(f32) + b_ref[...].astype(f32)).astype(bf16)

    spec = pl.BlockSpec((None, tr, c), lambda k, i, c_ref: (k, i, 0))
    return pl.pallas_call(
        body, name=name, out_shape=_sds((NSH, rh, c), bf16),
        grid_spec=pltpu.PrefetchScalarGridSpec(
            num_scalar_prefetch=1, grid=(NSH, rh // tr),
            in_specs=[pl.BlockSpec((None, None, tr, c), lambda k, i, c_ref: (k, c_ref[0], i, 0)), spec], out_specs=spec),
        compiler_params=_params(("parallel", "parallel")))(core, grads, got)


def _chip_sum(name, chip, sums, lands):
    _, rh, c = sums.shape
    tr = _row_tile(rh, c, 4)

    def body(k_ref, own_ref, l_ref, o_ref):
        own = own_ref[...].astype(f32)
        acc = None
        for j in range(NSH):
            term = jnp.where(k_ref[0] == j, own, l_ref[j].astype(f32))
            acc = term if acc is None else acc + term
        o_ref[...] = acc

    return pl.pallas_call(
        body, name=name, out_shape=_sds((rh, c), f32),
        grid_spec=pltpu.PrefetchScalarGridSpec(
            num_scalar_prefetch=1, grid=(rh // tr,),
            in_specs=[pl.BlockSpec((None, tr, c), lambda i, k_ref: (k_ref[0], i, 0)), pl.BlockSpec((NSH, tr, c), lambda i, k_ref: (0, i, 0))],
            out_specs=pl.BlockSpec((tr, c), lambda i, k_ref: (i, 0))),
        compiler_params=_params(("parallel",)))(chip, sums, lands)


def _mods_part(cond16, w_ada, b_part):
    n = w_ada.shape[1]
    tn = 512

    def body(c_ref, w_ref, b_ref, o_ref):
        cv = c_ref[...]
        o_ref[...] = _dot(cv * _sigmoid(cv), w_ref[...]) + b_ref[...]

    return pl.pallas_call(
        body, name="mods_part", grid=(n // tn,),
        in_specs=[pl.BlockSpec((16, D), lambda j: (0, 0)), pl.BlockSpec((D, tn), lambda j: (0, j)), pl.BlockSpec((1, tn), lambda j: (0, j))],
        out_specs=pl.BlockSpec((16, tn), lambda j: (0, j)), out_shape=_sds((16, n), f32), compiler_params=_params(("parallel",)),
    )(cond16, w_ada, b_part)


def _grad_w_ada(cond16, dm16):
    n = dm16.shape[1]
    tr = 256

    def body(c_ref, d_ref, o_ref):
        cv = c_ref[...]
        o_ref[...] = _dot(cv * _sigmoid(cv), d_ref[...], ta=True)

    return pl.pallas_call(
        body, name="grad_w_ada", grid=(D // tr,),
        in_specs=[pl.BlockSpec((16, tr), lambda i: (0, i)), pl.BlockSpec((16, n), lambda i: (0, 0))],
        out_specs=pl.BlockSpec((tr, n), lambda i: (i, 0)), out_shape=_sds((D, n), f32), compiler_params=_params(("parallel",)),
    )(cond16, dm16)


def _adamw(name, w, g, m, v):
    r, c = w.shape
    tr = _row_tile(r, c, 7)
    spec = pl.BlockSpec((tr, c), lambda i: (i, 0))
    grid = (r // tr,)

    def body(w_ref, g_ref, m_ref, v_ref, d_ref, nm_ref, nv_ref):
        gv = g_ref[...]
        nm = ADAM_B1 * m_ref[...] + (1.0 - ADAM_B1) * gv
        nv = ADAM_B2 * v_ref[...] + (1.0 - ADAM_B2) * (gv * gv)
        nm_ref[...] = nm
        nv_ref[...] = nv
        m_hat = nm / (1.0 - ADAM_B1 ** ADAM_STEP)
        v_hat = nv / (1.0 - ADAM_B2 ** ADAM_STEP)
        d_ref[...] = -ADAM_LR * (m_hat / (jnp.sqrt(v_hat) + ADAM_EPS) + ADAM_WD * w_ref[...])

    return pl.pallas_call(body, name=name, grid=grid, in_specs=[spec] * 4, out_specs=[spec] * 3, out_shape=[_sds(w.shape, f32)] * 3,
                          compiler_params=_params(("parallel",)))(w, g, m, v)


def _pack(parts, rows):
    flat = []
    for p in parts:
        p = p.reshape(-1)
        flat.append(jnp.pad(p, (0, (-p.shape[0]) % 128)))
    v = jnp.concatenate(flat)
    return jnp.pad(v, (0, rows * 128 - v.shape[0])).reshape(rows, 128)


def _unpack(packed, sizes):
    lead = packed.shape[:-2]
    flat = packed.reshape(lead + (-1,))
    out, off = [], 0
    for n in sizes:
        out.append(flat[..., off:off + n])
        off += n + (-n) % 128
    return out


BIG = ("w_in", "w_out", "w_gate", "w_up", "w_down")
SMALL = ("b_ada", "g_mix", "conv_b", "dt_bias", "a_log", "d_skip", "g_att_out", "g_ssd_out", "g_ffn", "g_final", "rel_bias", "conv_w")
ORDER = ("w_ada", "b_ada", "g_mix", "w_in", "rel_bias", "conv_w", "conv_b", "dt_bias", "a_log", "d_skip", "g_att_out", "g_ssd_out",
         "w_out", "g_ffn", "w_gate", "w_up", "w_down", "g_final")
REL_SH = N_REL // NSH
CONVW_SH = XBC // NSH
ADA_SH = 6 * D // NSH


class _Exchange:
    def __init__(self, core, chip):
        self.core, self.chip = core, chip
        self.gathered = {}
        self.pending = []

    def gather(self, names, shards):
        ssem, rsem, thru, lands, token = _split_start("gather_start_" + "_".join(names), _gather_copies, shards,
                                                      [(NSH,) + s.shape for s in shards])
        self.gathered.update({n: (ssem[i], rsem[i], thru[i], lands[i]) for i, n in enumerate(names)})
        return token

    def _whole(self, names, after):
        ssem, rsem, thru, lands = zip(*[self.gathered[n] for n in names])
        tag = "_".join(names)
        thru, lands = _split_wait("gather_wait_" + tag, _gather_copies, ssem, rsem, thru, lands, after)
        return _gather_forward("gather_forward_" + tag, thru, lands)

    def w_in(self, after):
        (win4,) = self._whole(("w_in",), after)
        return _w_in_columns(win4.reshape(NSH, D, IN_SHP))

    def w_out(self, after):
        (wout4,) = self._whole(("w_out",), after)
        return wout4.reshape(D, D)

    def ffn(self, after):
        wg4, wu4, wd4 = self._whole(("w_gate", "w_up", "w_down"), after)
        return wg4.reshape(NSH, D, FSH), wu4.reshape(NSH, D, FSH), wd4.reshape(NSH, FSH, D)

    def grad(self, names, grads):
        tag = "_".join(names)
        stacked = [g.reshape(NSH, 2, g.shape[1] // 2, g.shape[2]) for g in grads]
        got = _rs_pair_exchange("rs_pair_exchange_" + tag, stacked)
        sums = [_pair_sum("pair_sum_" + n, self.core, o, g) for n, o, g in zip(names, stacked, got)]
        self.pending.append((names, _split_start("rs_start_" + tag, _reduce_copies, sums, [s.shape for s in sums])))
        return self.pending[-1][1][4]

    def finish(self, after):
        grads = {}
        for names, (ssem, rsem, sums, lands, _) in self.pending:
            tag = "_".join(names)
            sums, lands = _split_wait("rs_wait_" + tag, _reduce_copies, ssem, rsem, sums, lands, after)
            halves = [_chip_sum("chip_sum_" + n, self.chip, sm, ld) for n, sm, ld in zip(names, sums, lands)]
            for n, f in zip(names, _rs_pair_gather("rs_pair_gather_" + tag, halves)):
                grads[n] = f.reshape(2 * f.shape[1], f.shape[2])
        return grads


def kernel(x, c, w_ada, b_ada, g_mix, w_in, rel_bias, conv_w, conv_b, dt_bias, a_log, d_skip, g_att_out, g_ssd_out, w_out, g_ffn, w_gate, w_up, w_down, g_final, loss_target, m_w_ada, m_b_ada, m_g_mix, m_w_in, m_rel_bias, m_conv_w, m_conv_b, m_dt_bias, m_a_log, m_d_skip, m_g_att_out, m_g_ssd_out, m_w_out, m_g_ffn, m_w_gate, m_w_up, m_w_down, m_g_final, v_w_ada, v_b_ada, v_g_mix, v_w_in, v_rel_bias, v_conv_w, v_conv_b, v_dt_bias, v_a_log, v_d_skip, v_g_att_out, v_g_ssd_out, v_w_out, v_g_ffn, v_w_gate, v_w_up, v_w_down, v_g_final):
    args = dict(locals())
    w = {n: args[n] for n in ORDER}
    m = {n: args["m_" + n] for n in ORDER}
    v = {n: args["v_" + n] for n in ORDER}
    ix, iy, ic = lax.axis_index("x"), lax.axis_index("y"), lax.axis_index("c")
    chip = 2 * ix + iy
    dev = 2 * chip + ic
    s = x.shape[1]

    g1 = _allgather8("gather_inputs", _pack([c[0], rel_bias[0], conv_w[0]], 40))
    c_all, rel_sh, convw_sh = _unpack(g1, [D, NH * REL_SH, 4 * CONVW_SH])
    rel_full = jnp.concatenate([rel_sh[2 * k].reshape(NH, REL_SH) for k in range(NSH)], axis=1)
    convw_full = jnp.concatenate([convw_sh[2 * k].reshape(4, CONVW_SH) for k in range(NSH)], axis=1)
    cond16 = jnp.pad(c_all, ((0, 8), (0, 0)))
    b_part = lax.dynamic_slice_in_dim(b_ada, chip * ADA_SH, ADA_SH, axis=1)
    mods_part = _mods_part(cond16, w_ada[0], b_part)[:N_DEV]
    g2 = _allgather8("gather_mods", mods_part.reshape(N_DEV * ADA_SH // 128, 128))
    mods_all = jnp.concatenate([g2[2 * k].reshape(N_DEV, ADA_SH) for k in range(NSH)], axis=1)
    mods = lax.dynamic_slice_in_dim(mods_all, dev, 1, axis=0)

    exchange = _Exchange(jnp.reshape(ic, (1,)).astype(jnp.int32), jnp.reshape(chip, (1,)).astype(jnp.int32))
    shard_in = _cast_bf16("cast_w_in", jnp.pad(w_in[0], ((0, 0), (0, IN_SHP - IN_SH))), dep=g2[0, :8]).reshape(2, D // 2, IN_SHP)
    tok = exchange.gather(("w_in",), [shard_in])
    tok = exchange.gather(("w_out", "w_gate", "w_up", "w_down"), [
        _cast_bf16("cast_w_out", w_out[0], dep=tok).reshape(2, D // NSH // 2, D),
        _cast_bf16("cast_w_gate", w_gate[0], dep=tok).reshape(2, D // 2, FSH),
        _cast_bf16("cast_w_up", w_up[0], dep=tok).reshape(2, D // 2, FSH),
        _cast_bf16("cast_w_down", w_down[0], dep=tok).reshape(2, FSH // 2, D)])
    mods = mods + tok[:1, :1]

    loss, grad_x, dmods, small = _local_step(
        x[0], loss_target[0], mods, g_mix, rel_full, convw_full, conv_b, dt_bias, a_log, d_skip, g_att_out, g_ssd_out, g_ffn,
        g_final[None, :], exchange)

    small_names = ("g_mix", "conv_b", "dt_bias", "a_log", "d_skip", "g_att_out", "g_ssd_out", "g_ffn", "g_final", "rel_bias", "conv_w")
    g3 = _allgather8("gather_small_grads", _pack([dmods] + [small[n] for n in small_names], 264))
    sizes = [6 * D] + [int(np.prod(small[n].shape)) for n in small_names]
    dmods_all = _unpack(g3, sizes)[0]
    summed = _unpack(_sum8(g3), sizes)
    grads = {"b_ada": summed[0].reshape(1, 6 * D)}
    for n, val in zip(small_names, summed[1:]):
        grads[n] = val.reshape(small[n].shape)
    grads["rel_bias"] = lax.dynamic_slice_in_dim(grads["rel_bias"], chip * REL_SH, REL_SH, axis=1)
    grads["conv_w"] = lax.dynamic_slice_in_dim(grads["conv_w"], chip * CONVW_SH, CONVW_SH, axis=1)
    grads["g_final"] = grads["g_final"].reshape(D)
    dm16 = jnp.pad(lax.dynamic_slice_in_dim(dmods_all, chip * ADA_SH, ADA_SH, axis=1), ((0, 8), (0, 0)))
    grads["w_ada"] = _grad_w_ada(cond16, dm16)

    delta, new_m, new_v = {}, {}, {}
    delta["w_ada"], new_m["w_ada"], new_v["w_ada"] = _adamw("adamw_w_ada", w_ada[0], grads["w_ada"], m_w_ada[0], v_w_ada[0])
    grads.update(exchange.finish(grad_x))
    grads["w_in"] = grads["w_in"][:, :IN_SH]
    for n in BIG:
        delta[n], new_m[n], new_v[n] = _adamw("adamw_" + n, w[n][0], grads[n], m[n][0], v[n][0])
    sw = _pack([w[n] for n in SMALL], 200)
    sg = _pack([grads[n] for n in SMALL], 200)
    sm = _pack([m[n] for n in SMALL], 200)
    sv = _pack([v[n] for n in SMALL], 200)
    ssz = [int(np.prod(w[n].shape)) for n in SMALL]
    for dst, packed in zip((delta, new_m, new_v), _adamw("adamw_small", sw, sg, sm, sv)):
        for n, val in zip(SMALL, _unpack(packed, ssz)):
            dst[n] = val

    def shaped(d, n):
        return d[n].reshape(w[n].shape)

    total = lax.psum(loss, ("x", "y", "c"))
    return (total, grad_x[None], *[shaped(grads, n) for n in ORDER], *[shaped(delta, n) for n in ORDER],
            *[shaped(new_m, n) for n in ORDER], *[shaped(new_v, n) for n in ORDER])
```

```python
import functools

import numpy as np
import jax
import jax.numpy as jnp
from jax import lax
from jax.experimental import pallas as pl
from jax.experimental.pallas import tpu as pltpu

f32 = jnp.float32
bf16 = jnp.bfloat16
HIGHEST = lax.Precision.HIGHEST
MESH = pl.DeviceIdType.MESH

D = 2048
CHUNK = 64
LEFT = 8
BAND = (LEFT + 1) * CHUNK
BANDP = 640
PADK = LEFT * CHUNK
NH = 16
HD = 64
ATT_W = NH * HD
SSD_W = 1024
NG = 2
NSTATE = 128
GW = SSD_W // NG
XBC = SSD_W + 2 * NG * NSTATE
N_REL = 320
REL_CLIP = 256
FFN = 5632
NSH = 4
FSH = FFN // NSH
IN_COLS = 5648
IN_SH = IN_COLS // NSH
IN_SHP = 1536
IN_A = 3 * ATT_W
IN_B = 2688
IN_P = IN_A + IN_B
EPS = 1e-6
N_DEV = 8

ADAM_LR = 0.001
ADAM_B1 = 0.9
ADAM_B2 = 0.999
ADAM_EPS = 1e-08
ADAM_WD = 0.01
ADAM_STEP = 10

VMEM_LIMIT = 56 * 1024 * 1024


def _params(sem):
    return pltpu.CompilerParams(dimension_semantics=sem, vmem_limit_bytes=VMEM_LIMIT)


def _sds(shape, dtype):
    return jax.ShapeDtypeStruct(shape, dtype)


def _fold8(v):
    r, w = v.shape
    return jnp.sum(v.reshape(r // 8, 8, w), axis=0)


STRIP = 16


def _strips(tm, fn):
    def step(j, carry):
        fn(pl.ds(pl.multiple_of(j * STRIP, STRIP), STRIP))
        return carry
    lax.fori_loop(0, tm // STRIP, step, 0, unroll=4)


def _sigmoid(v):
    return 1.0 / (1.0 + jnp.exp(-v))


def _softplus(v):
    return jnp.maximum(v, 0.0) + jnp.log(1.0 + jnp.exp(-jnp.abs(v)))


def _dot(a, b, ta=False, tb=False):
    dn = (((0 if ta else 1,), (1 if tb else 0,)), ((), ()))
    return lax.dot_general(a.astype(bf16), b.astype(bf16), dn, preferred_element_type=f32)


def _dep_args(dep, ngrid):
    if dep is None:
        return [], []
    return [pl.BlockSpec((8, 128), lambda *_: (0, 0))], [dep]


def _dot01(a, b, ta=False, tb=False, exact="b"):
    dn = (((0 if ta else 1,), (1 if tb else 0,)), ((), ()))
    x = a if exact == "b" else b
    hi = x.astype(bf16)
    r = x - hi.astype(f32)
    mid = r.astype(bf16)
    lo = (r - mid.astype(f32)).astype(bf16)
    if exact == "b":
        m = b.astype(bf16)
        return sum(lax.dot_general(p, m, dn, preferred_element_type=f32) for p in (hi, mid, lo))
    m = a.astype(bf16)
    return sum(lax.dot_general(m, p, dn, preferred_element_type=f32) for p in (hi, mid, lo))


def _matmul(name, a, b, *, grid, a_spec, b_spec, o_spec, o_shape, o_dtype, acc_shape, ta=False, tb=False, dep=None):
    nk = grid[2]
    dep_specs, dep_ops = _dep_args(dep, 3)

    def body(a_ref, b_ref, *rest):
        o_ref, acc_ref = rest[-2:]
        p = _dot(a_ref[...], b_ref[...], ta, tb)
        if nk == 1:
            o_ref[...] = p.astype(o_ref.dtype)
        else:
            k = pl.program_id(2)

            @pl.when(k == 0)
            def _():
                acc_ref[...] = p

            @pl.when(jnp.logical_and(k > 0, k < nk - 1))
            def _():
                acc_ref[...] += p

            @pl.when(k == nk - 1)
            def _():
                o_ref[...] = (acc_ref[...] + p).astype(o_ref.dtype)

    return pl.pallas_call(
        body, name=name, grid=grid, in_specs=[a_spec, b_spec] + dep_specs, out_specs=o_spec,
        out_shape=_sds(o_shape, o_dtype), scratch_shapes=[pltpu.VMEM(acc_shape if nk > 1 else (8, 128), f32)],
        compiler_params=_params(("parallel", "parallel", "arbitrary")),
    )(a, b, *dep_ops)


def _mm_nn_fullk(name, a, b, tm, tn, o_dtype, n=None):
    m, k = a.shape
    n = b.shape[1] if n is None else n
    return _matmul(name, a, b, grid=(m // tm, n // tn, 1),
                   a_spec=pl.BlockSpec((tm, k), lambda i, j, kk: (i, 0)),
                   b_spec=pl.BlockSpec((k, tn), lambda i, j, kk: (0, j)),
                   o_spec=pl.BlockSpec((tm, tn), lambda i, j, kk: (i, j)),
                   o_shape=(m, n), o_dtype=o_dtype, acc_shape=(tm, tn))


def _mm_nt(name, a, b, tm, tn, tk, o_dtype, dep=None):
    m, k = a.shape
    n = b.shape[0]
    return _matmul(name, a, b, grid=(m // tm, n // tn, k // tk), tb=True, dep=dep,
                   a_spec=pl.BlockSpec((tm, tk), lambda i, j, kk: (i, kk)),
                   b_spec=pl.BlockSpec((tn, tk), lambda i, j, kk: (j, kk)),
                   o_spec=pl.BlockSpec((tm, tn), lambda i, j, kk: (i, j)),
                   o_shape=(m, n), o_dtype=o_dtype, acc_shape=(tm, tn))


def _mm_tn(name, a, b, tm, tn, tk, o_dtype):
    k, m = a.shape
    n = b.shape[1]
    return _matmul(name, a, b, grid=(m // tm, n // tn, k // tk), ta=True,
                   a_spec=pl.BlockSpec((tk, tm), lambda i, j, kk: (kk, i)),
                   b_spec=pl.BlockSpec((tk, tn), lambda i, j, kk: (kk, j)),
                   o_spec=pl.BlockSpec((tm, tn), lambda i, j, kk: (i, j)),
                   o_shape=(m, n), o_dtype=o_dtype, acc_shape=(tm, tn))


FSH_PARTS = (slice(0, 640), slice(640, FSH))


def _ffn_up(h2b, wg4, wu4, tm):
    s = h2b.shape[0]

    def body(h_ref, wg_ref, wu_ref, a_ref, s_ref, ud_ref):
        h = h_ref[...]
        for cols in FSH_PARTS:
            g = _dot(h, wg_ref[:, cols])
            u = _dot(h, wu_ref[:, cols])
            sg = _sigmoid(g)
            sil = g * sg
            a_ref[:, cols] = (sil * u).astype(bf16)
            s_ref[:, cols] = sil.astype(bf16)
            ud_ref[:, cols] = (u * (sg * (1.0 + g * (1.0 - sg)))).astype(bf16)

    wspec = pl.BlockSpec((None, D, FSH), lambda k, i: (k, 0, 0))
    ospec = pl.BlockSpec((tm, FSH), lambda k, i: (i, k))
    return pl.pallas_call(
        body, name="ffn_up", grid=(NSH, s // tm),
        in_specs=[pl.BlockSpec((tm, D), lambda k, i: (i, 0)), wspec, wspec],
        out_specs=[ospec, ospec, ospec], out_shape=[_sds((s, FFN), bf16)] * 3,
        compiler_params=_params(("parallel", "parallel")),
    )(h2b, wg4, wu4)


def _ffn_down(act, wd4, tm):
    s = act.shape[0]
    return _matmul("ffn_down", act, wd4, grid=(s // tm, 1, NSH),
                   a_spec=pl.BlockSpec((tm, FSH), lambda i, j, k: (i, k)),
                   b_spec=pl.BlockSpec((None, FSH, D), lambda i, j, k: (k, 0, 0)),
                   o_spec=pl.BlockSpec((tm, D), lambda i, j, k: (i, 0)),
                   o_shape=(s, D), o_dtype=f32, acc_shape=(tm, D))


def _ffn_dact(dffn, wd4, sil, ud, tm, dep=None):
    s = dffn.shape[0]
    dep_specs, dep_ops = _dep_args(dep, 2)

    def body(d_ref, w_ref, s_ref, ud_ref, *rest):
        dg_ref, du_ref = rest[-2:]
        d = d_ref[...]
        for cols in FSH_PARTS:
            dact = _dot(d, w_ref[cols, :], tb=True)
            dg_ref[:, cols] = (dact * ud_ref[:, cols].astype(f32)).astype(bf16)
            du_ref[:, cols] = (dact * s_ref[:, cols].astype(f32)).astype(bf16)

    blk = pl.BlockSpec((tm, FSH), lambda k, i: (i, k))
    return pl.pallas_call(
        body, name="ffn_dact", grid=(NSH, s // tm),
        in_specs=[pl.BlockSpec((tm, D), lambda k, i: (i, 0)), pl.BlockSpec((None, FSH, D), lambda k, i: (k, 0, 0)), blk, blk] + dep_specs,
        out_specs=[blk, blk], out_shape=[_sds((s, FFN), bf16), _sds((s, FFN), bf16)],
        compiler_params=_params(("parallel", "parallel")),
    )(dffn, wd4, sil, ud, *dep_ops)


def _ffn_dh(dgate, dup, wg4, wu4, tm, dep=None):
    s = dgate.shape[0]
    dep_specs, dep_ops = _dep_args(dep, 2)

    def body(dg_ref, du_ref, wg_ref, wu_ref, *rest):
        o_ref, acc_ref = rest[-2:]
        k = pl.program_id(1)
        p = _dot(dg_ref[...], wg_ref[...], tb=True) + _dot(du_ref[...], wu_ref[...], tb=True)

        @pl.when(k == 0)
        def _():
            acc_ref[...] = p

        @pl.when(jnp.logical_and(k > 0, k < NSH - 1))
        def _():
            acc_ref[...] += p

        @pl.when(k == NSH - 1)
        def _():
            o_ref[...] = acc_ref[...] + p

    aspec = pl.BlockSpec((tm, FSH), lambda i, k: (i, k))
    wspec = pl.BlockSpec((None, D, FSH), lambda i, k: (k, 0, 0))
    return pl.pallas_call(
        body, name="ffn_dh", grid=(s // tm, NSH), in_specs=[aspec, aspec, wspec, wspec] + dep_specs,
        out_specs=pl.BlockSpec((tm, D), lambda i, k: (i, 0)), out_shape=_sds((s, D), f32),
        scratch_shapes=[pltpu.VMEM((tm, D), f32)], compiler_params=_params(("parallel", "arbitrary")),
    )(dgate, dup, wg4, wu4, *dep_ops)


def _grad_cols4(name, h, dy, tm, tk):
    s = h.shape[0]
    return _matmul(name, h, dy, grid=(NSH, D // tm, s // tk), ta=True,
                   a_spec=pl.BlockSpec((tk, tm), lambda k, i, kk: (kk, i)),
                   b_spec=pl.BlockSpec((tk, FSH), lambda k, i, kk: (kk, k)),
                   o_spec=pl.BlockSpec((None, tm, FSH), lambda k, i, kk: (k, i, 0)),
                   o_shape=(NSH, D, FSH), o_dtype=bf16, acc_shape=(tm, FSH))


def _grad_wdown4(act, dffn, tn, tk):
    s = act.shape[0]
    return _matmul("grad_w_down", act, dffn, grid=(NSH, D // tn, s // tk), ta=True,
                   a_spec=pl.BlockSpec((tk, FSH), lambda k, j, kk: (kk, k)),
                   b_spec=pl.BlockSpec((tk, tn), lambda k, j, kk: (kk, j)),
                   o_spec=pl.BlockSpec((None, FSH, tn), lambda k, j, kk: (k, 0, j)),
                   o_shape=(NSH, FSH, D), o_dtype=bf16, acc_shape=(FSH, tn))


def _row_spec(w):
    return pl.BlockSpec((1, w), lambda i: (0, 0))


def _tile_spec(tm, w, col=0):
    return pl.BlockSpec((tm, w), lambda i: (i, col))


def _norm_mod(name, x, g, sc, sh, tm):
    s = x.shape[0]

    def body(x_ref, g_ref, sc_ref, sh_ref, o_ref):
        def strip(rows):
            xv = x_ref[rows, :]
            r = lax.rsqrt(jnp.mean(xv * xv, axis=-1, keepdims=True) + EPS)
            o_ref[rows, :] = (xv * r * g_ref[...] * (1.0 + sc_ref[...]) + sh_ref[...]).astype(bf16)

        _strips(tm, strip)

    return pl.pallas_call(
        body, name=name, grid=(s // tm,), in_specs=[_tile_spec(tm, D), _row_spec(D), _row_spec(D), _row_spec(D)],
        out_specs=_tile_spec(tm, D), out_shape=_sds((s, D), bf16), compiler_params=_params(("parallel",)),
    )(x, g, sc, sh)


def _resid_norm_mod(x, gt, mix, g, sc, sh, tm):
    s = x.shape[0]

    def body(x_ref, gt_ref, m_ref, g_ref, sc_ref, sh_ref, x2_ref, h_ref):
        def strip(rows):
            xv = x_ref[rows, :] + gt_ref[...] * m_ref[rows, :]
            x2_ref[rows, :] = xv
            r = lax.rsqrt(jnp.mean(xv * xv, axis=-1, keepdims=True) + EPS)
            h_ref[rows, :] = (xv * r * g_ref[...] * (1.0 + sc_ref[...]) + sh_ref[...]).astype(bf16)

        _strips(tm, strip)

    return pl.pallas_call(
        body, name="resid_norm_mod", grid=(s // tm,),
        in_specs=[_tile_spec(tm, D), _row_spec(D), _tile_spec(tm, D), _row_spec(D), _row_spec(D), _row_spec(D)],
        out_specs=[_tile_spec(tm, D), _tile_spec(tm, D)], out_shape=[_sds((s, D), f32), _sds((s, D), bf16)],
        compiler_params=_params(("parallel",)),
    )(x, gt, mix, g, sc, sh)


def _final_fwd_bwd(x2, ffn, gt2, g, tgt, tm):
    s = x2.shape[0]
    n = s // tm

    def body(x_ref, f_ref, gt_ref, g_ref, t_ref, dx_ref, df_ref, loss_ref, dg_ref, dgt_ref, a_loss, a_dg, a_dgt):
        i = pl.program_id(0)

        @pl.when(i == 0)
        def _():
            a_loss[...] = jnp.zeros_like(a_loss)
            a_dg[...] = jnp.zeros_like(a_dg)
            a_dgt[...] = jnp.zeros_like(a_dgt)

        def strip(rows):
            fv = f_ref[rows, :]
            gt = gt_ref[...]
            gv = g_ref[...]
            xv = x_ref[rows, :] + gt * fv
            r = lax.rsqrt(jnp.mean(xv * xv, axis=-1, keepdims=True) + EPS)
            xh = xv * r
            e = xh * gv - t_ref[rows, :]
            a_loss[...] += _fold8(e * e)
            dy = e * (1.0 / D)
            a_dg[...] += _fold8(dy * xh)
            t = dy * gv
            dx = r * (t - xh * jnp.mean(t * xh, axis=-1, keepdims=True))
            dx_ref[rows, :] = dx
            a_dgt[...] += _fold8(dx * fv)
            df_ref[rows, :] = (dx * gt).astype(bf16)

        _strips(tm, strip)

        @pl.when(i == n - 1)
        def _():
            tot = jnp.sum(jnp.sum(a_loss[...], axis=0, keepdims=True), axis=1, keepdims=True) * (0.5 / D)
            loss_ref[...] = jnp.broadcast_to(tot, (1, 128))
            dg_ref[...] = jnp.sum(a_dg[...], axis=0, keepdims=True)
            dgt_ref[...] = jnp.sum(a_dgt[...], axis=0, keepdims=True)

    return pl.pallas_call(
        body, name="final_fwd_bwd", grid=(n,),
        in_specs=[_tile_spec(tm, D), _tile_spec(tm, D), _row_spec(D), _row_spec(D), _tile_spec(tm, D)],
        out_specs=[_tile_spec(tm, D), _tile_spec(tm, D), _row_spec(128), _row_spec(D), _row_spec(D)],
        out_shape=[_sds((s, D), f32), _sds((s, D), bf16), _sds((1, 128), f32), _sds((1, D), f32), _sds((1, D), f32)],
        scratch_shapes=[pltpu.VMEM((8, D), f32)] * 3, compiler_params=_params(("arbitrary",)),
    )(x2, ffn, gt2, g, tgt)


def _norm_mod_bwd(name, dh, xin, g, sc, dres, tm, mix=None, gt=None):
    s = dh.shape[0]
    n = s // tm
    with_mix = mix is not None

    def body(*refs):
        if with_mix:
            dh_ref, x_ref, g_ref, sc_ref, dr_ref, m_ref, gt_ref, dx_ref, dm_ref, dsc_ref, dsh_ref, dg_ref, dgt_ref, a_sc, a_sh, a_g, a_gt = refs
        else:
            dh_ref, x_ref, g_ref, sc_ref, dr_ref, dx_ref, dsc_ref, dsh_ref, dg_ref, a_sc, a_sh, a_g = refs
        i = pl.program_id(0)

        @pl.when(i == 0)
        def _():
            a_sc[...] = jnp.zeros_like(a_sc)
            a_sh[...] = jnp.zeros_like(a_sh)
            a_g[...] = jnp.zeros_like(a_g)
            if with_mix:
                a_gt[...] = jnp.zeros_like(a_gt)

        def strip(rows):
            dh = dh_ref[rows, :]
            xv = x_ref[rows, :]
            gv = g_ref[...]
            r = lax.rsqrt(jnp.mean(xv * xv, axis=-1, keepdims=True) + EPS)
            xh = xv * r
            a_sc[...] += _fold8(dh * xh * gv)
            a_sh[...] += _fold8(dh)
            dn = dh * (1.0 + sc_ref[...])
            a_g[...] += _fold8(dn * xh)
            t = dn * gv
            dx = dr_ref[rows, :] + r * (t - xh * jnp.mean(t * xh, axis=-1, keepdims=True))
            dx_ref[rows, :] = dx
            if with_mix:
                a_gt[...] += _fold8(dx * m_ref[rows, :])
                dm_ref[rows, :] = (dx * gt_ref[...]).astype(bf16)

        _strips(tm, strip)

        @pl.when(i == n - 1)
        def _():
            dsc_ref[...] = jnp.sum(a_sc[...], axis=0, keepdims=True)
            dsh_ref[...] = jnp.sum(a_sh[...], axis=0, keepdims=True)
            dg_ref[...] = jnp.sum(a_g[...], axis=0, keepdims=True)
            if with_mix:
                dgt_ref[...] = jnp.sum(a_gt[...], axis=0, keepdims=True)

    tile, row = _tile_spec(tm, D), _row_spec(D)
    if with_mix:
        ins, args = [tile, tile, row, row, tile, tile, row], (dh, xin, g, sc, dres, mix, gt)
        outs = [tile, tile, row, row, row, row]
        shapes = [_sds((s, D), f32), _sds((s, D), bf16)] + [_sds((1, D), f32)] * 4
        nacc = 4
    else:
        ins, args = [tile, tile, row, row, tile], (dh, xin, g, sc, dres)
        outs = [tile, row, row, row]
        shapes = [_sds((s, D), f32)] + [_sds((1, D), f32)] * 3
        nacc = 3
    return pl.pallas_call(
        body, name=name, grid=(n,), in_specs=ins, out_specs=outs, out_shape=shapes,
        scratch_shapes=[pltpu.VMEM((8, D), f32)] * nacc, compiler_params=_params(("arbitrary",)),
    )(*args)


def _mix_pre(att, y, proj2, g_att, g_ssd, tm):
    s = att.shape[0]

    def body(a_ref, y_ref, z_ref, ga_ref, gs_ref, o_ref):
        def strip(rows):
            a = a_ref[rows, :]
            ra = lax.rsqrt(jnp.mean(a * a, axis=-1, keepdims=True) + EPS)
            o_ref[rows, 0:ATT_W] = (a * ra * ga_ref[...]).astype(bf16)
            z = z_ref[rows, :]
            u = y_ref[rows, :] * (z * _sigmoid(z))
            ru = lax.rsqrt(jnp.mean(u * u, axis=-1, keepdims=True) + EPS)
            o_ref[rows, ATT_W:] = (u * ru * gs_ref[...]).astype(bf16)

        _strips(tm, strip)

    t = _tile_spec(tm, ATT_W)
    return pl.pallas_call(
        body, name="mix_pre", grid=(s // tm,), in_specs=[t, t, t, _row_spec(ATT_W), _row_spec(SSD_W)],
        out_specs=_tile_spec(tm, D), out_shape=_sds((s, D), bf16), compiler_params=_params(("parallel",)),
    )(att, y, proj2, g_att, g_ssd)


def _mix_pre_bwd(dmc, att, y, proj2, g_att, g_ssd, tm):
    s = att.shape[0]
    n = s // tm

    def body(da_ref, ds_ref, a_ref, y_ref, z_ref, ga_ref, gs_ref, datt_ref, dy_ref, dz_ref, dga_ref, dgs_ref, acc_a, acc_s):
        i = pl.program_id(0)

        @pl.when(i == 0)
        def _():
            acc_a[...] = jnp.zeros_like(acc_a)
            acc_s[...] = jnp.zeros_like(acc_s)

        def strip(rows):
            a = a_ref[rows, :]
            ra = lax.rsqrt(jnp.mean(a * a, axis=-1, keepdims=True) + EPS)
            ah = a * ra
            dan = da_ref[rows, :]
            acc_a[...] += _fold8(dan * ah)
            t = dan * ga_ref[...]
            datt_ref[rows, :] = (ra * (t - ah * jnp.mean(t * ah, axis=-1, keepdims=True))).astype(bf16)
            z = z_ref[rows, :]
            yv = y_ref[rows, :]
            sz = _sigmoid(z)
            sil = z * sz
            u = yv * sil
            ru = lax.rsqrt(jnp.mean(u * u, axis=-1, keepdims=True) + EPS)
            uh = u * ru
            dsn = ds_ref[rows, :]
            acc_s[...] += _fold8(dsn * uh)
            t2 = dsn * gs_ref[...]
            du = ru * (t2 - uh * jnp.mean(t2 * uh, axis=-1, keepdims=True))
            dy_ref[rows, :] = du * sil
            dz_ref[rows, :] = (du * yv * (sz * (1.0 + z * (1.0 - sz)))).astype(bf16)

        _strips(tm, strip)

        @pl.when(i == n - 1)
        def _():
            dga_ref[...] = jnp.sum(acc_a[...], axis=0, keepdims=True)
            dgs_ref[...] = jnp.sum(acc_s[...], axis=0, keepdims=True)

    t = _tile_spec(tm, ATT_W)
    row = _row_spec(ATT_W)
    return pl.pallas_call(
        body, name="mix_pre_bwd", grid=(n,),
        in_specs=[_tile_spec(tm, ATT_W, 0), _tile_spec(tm, ATT_W, 1), t, t, t, row, row],
        out_specs=[t, t, t, row, row],
        out_shape=[_sds((s, ATT_W), bf16), _sds((s, SSD_W), f32), _sds((s, SSD_W), bf16), _sds((1, ATT_W), f32), _sds((1, SSD_W), f32)],
        scratch_shapes=[pltpu.VMEM((8, ATT_W), f32)] * 2, compiler_params=_params(("arbitrary",)),
    )(dmc, dmc, att, y, proj2, g_att, g_ssd)


ATT_GROUP = 8
ATT_GROUP_FWD = 16


def _pair_rows(qc):
    two = jnp.concatenate([qc, qc], axis=0)
    r = lax.broadcasted_iota(jnp.int32, (2 * CHUNK, 128), 0)
    l = lax.broadcasted_iota(jnp.int32, (2 * CHUNK, 128), 1)
    return jnp.where((r < CHUNK) == (l < HD), two, jnp.zeros_like(two))


def _scaled(q):
    return q * jnp.asarray(HD ** -0.5, q.dtype)


def _pair_scores(wt, kb, bias, r0, masked):
    sc = lax.dot_general(wt, kb, (((1,), (1,)), ((), ())), preferred_element_type=f32) + bias
    if not masked:
        return sc
    kidx = lax.broadcasted_iota(jnp.int32, sc.shape, 1)
    return jnp.where(r0 + kidx >= PADK, sc, -jnp.inf)


def _softmax(sc, axis):
    e = jnp.exp(sc - jnp.max(sc, axis=axis, keepdims=True))
    return e * (1.0 / jnp.sum(e, axis=axis, keepdims=True))


def _chunk_loops(nc, group, per_trip):
    n_masked = min(-(-LEFT // per_trip), nc // per_trip)

    def run(masked):
        def step(g, carry):
            group(g, masked)
            return carry
        return step

    lax.fori_loop(0, n_masked, run(True), 0)
    lax.fori_loop(n_masked, nc // per_trip, run(False), 0)


def _pair_diag(r):
    lane = lax.broadcasted_iota(jnp.int32, (CHUNK, 128), 1)
    return jnp.where(lane < HD, r[0:CHUNK], r[CHUNK:])


def _pad_keys(k_ref, kp, s):
    kp[0:PADK, :] = jnp.zeros((PADK, 128), bf16)
    kp[PADK:PADK + s, :] = k_ref[...]
    kp[PADK + s:, :] = jnp.zeros((CHUNK, 128), bf16)


def _attn_fwd(qkv, bias2):
    s = qkv.shape[0]
    nc = s // CHUNK
    npair = NH // 2
    per_trip = min(ATT_GROUP_FWD, nc)

    def body(q_ref, k_ref, v_ref, b_ref, o_ref, kp, vp):
        _pad_keys(k_ref, kp, s)
        _pad_keys(v_ref, vp, s)

        def group(g, masked):
            r0s = [pl.multiple_of((g * per_trip + u) * CHUNK, CHUNK) for u in range(per_trip)]
            scs = [_pair_scores(_pair_rows(_scaled(q_ref[pl.ds(r0, CHUNK), :])), kp[pl.ds(r0, BANDP), :], b_ref[...], r0, masked)
                   for r0 in r0s]
            ps = [_softmax(sc, -1).astype(bf16) for sc in scs]
            for r0, p in zip(r0s, ps):
                o_ref[pl.ds(r0, CHUNK), :] = _pair_diag(jnp.dot(p, vp[pl.ds(r0, BANDP), :], preferred_element_type=f32))

        _chunk_loops(nc, group, per_trip)

    return pl.pallas_call(
        body, name="attn_fwd", grid=(npair,),
        in_specs=[pl.BlockSpec((s, 128), lambda p: (0, p)), pl.BlockSpec((s, 128), lambda p: (0, npair + p)),
                  pl.BlockSpec((s, 128), lambda p: (0, 2 * npair + p)), pl.BlockSpec((None, 2 * CHUNK, BANDP), lambda p: (p, 0, 0))],
        out_specs=pl.BlockSpec((s, 128), lambda p: (0, p)), out_shape=_sds((s, ATT_W), f32),
        scratch_shapes=[pltpu.VMEM((PADK + s + CHUNK, 128), bf16)] * 2, compiler_params=_params(("parallel",)),
    )(qkv, qkv, qkv, bias2)


def _attn_bwd(qkv, datt, bias2):
    s = qkv.shape[0]
    nc = s // CHUNK
    npair = NH // 2
    rows = PADK + s + CHUNK
    nt = (((1,), (1,)), ((), ()))

    def body(q_ref, k_ref, v_ref, do_ref, b_ref, dq_ref, dk_ref, dv_ref, g_ref, kp, vp, dkp, dvp):
        _pad_keys(k_ref, kp, s)
        _pad_keys(v_ref, vp, s)
        dkp[...] = jnp.zeros_like(dkp)
        dvp[...] = jnp.zeros_like(dvp)
        g_ref[...] = jnp.zeros_like(g_ref)

        def group(g, masked):
            r0s = [pl.multiple_of((g * ATT_GROUP + u) * CHUNK, CHUNK) for u in range(ATT_GROUP)]
            wts = [_pair_rows(_scaled(q_ref[pl.ds(r0, CHUNK), :])) for r0 in r0s]
            dos = [_pair_rows(do_ref[pl.ds(r0, CHUNK), :]) for r0 in r0s]
            scs = [_pair_scores(wt, kp[pl.ds(r0, BANDP), :], b_ref[...], r0, masked) for wt, r0 in zip(wts, r0s)]
            dps = [lax.dot_general(do, vp[pl.ds(r0, BANDP), :], nt, preferred_element_type=f32) for do, r0 in zip(dos, r0s)]
            tn_ = (((0,), (0,)), ((), ()))
            for r0, wt, do, sc, dp in zip(r0s, wts, dos, scs, dps):
                p = _softmax(sc, -1)
                ds = p * (dp - jnp.sum(p * dp, axis=-1, keepdims=True))
                g_ref[...] += ds
                dsb = ds.astype(bf16)
                dq = jnp.dot(dsb, kp[pl.ds(r0, BANDP), :], preferred_element_type=f32)
                dq_ref[pl.ds(r0, CHUNK), :] = (_pair_diag(dq) * (HD ** -0.5)).astype(bf16)
                dkp[pl.ds(r0, BANDP), :] += lax.dot_general(dsb, wt, tn_, preferred_element_type=f32)
                dvp[pl.ds(r0, BANDP), :] += lax.dot_general(p.astype(bf16), do, tn_, preferred_element_type=f32)

        _chunk_loops(nc, group, ATT_GROUP)
        dk_ref[...] = dkp[PADK:PADK + s, :].astype(bf16)
        dv_ref[...] = dvp[PADK:PADK + s, :].astype(bf16)

    col = lambda off: pl.BlockSpec((s, 128), lambda p: (0, off + p))
    return pl.pallas_call(
        body, name="attn_bwd", grid=(npair,),
        in_specs=[col(0), col(npair), col(2 * npair), col(0), pl.BlockSpec((None, 2 * CHUNK, BANDP), lambda p: (p, 0, 0))],
        out_specs=[col(0), col(0), col(0), pl.BlockSpec((None, 2 * CHUNK, BANDP), lambda p: (p, 0, 0))],
        out_shape=[_sds((s, ATT_W), bf16)] * 3 + [_sds((npair, 2 * CHUNK, BANDP), f32)],
        scratch_shapes=[pltpu.VMEM((rows, 128), bf16)] * 2 + [pltpu.VMEM((rows, 128), f32)] * 2,
        compiler_params=_params(("parallel",)),
    )(qkv, qkv, qkv, datt, bias2)


def _rel_tables():
    onehot = np.zeros((BANDP, N_REL), np.float32)
    for j in range(BAND + CHUNK - 1):
        o = j - (CHUNK - 1)
        onehot[j, int(np.clip(PADK - o, -(CHUNK - 1), REL_CLIP)) + CHUNK - 1] = 1.0
    return onehot, np.ascontiguousarray(np.eye(CHUNK, dtype=np.float32)[::-1])


def _expand_bias(rel):
    ext = jnp.concatenate([jnp.broadcast_to(rel[:, N_REL - 1:], (NH, N_REL - 1)), rel[:, ::-1],
                           jnp.zeros((NH, BANDP - BAND + 1), f32)], axis=1)
    band = jnp.stack([ext[:, CHUNK - 1 - q:CHUNK - 1 - q + BANDP] for q in range(CHUNK)], axis=1)
    band = jnp.where(np.arange(BANDP) < BAND, band, -jnp.inf)
    return band.reshape(NH // 2, 2 * CHUNK, BANDP)


def _rel_bias_grad(gband):
    def body(g_ref, m_ref, flip_ref, o_ref, d2):
        for h in range(NH):
            rev = jnp.dot(flip_ref[...], g_ref[h], precision=HIGHEST, preferred_element_type=f32)
            rolled = pltpu.roll(rev, 0, 1, stride=1, stride_axis=0)
            d2[h:h + 1, :] = jnp.sum(rolled, axis=0, keepdims=True)
        o_ref[...] = jnp.dot(d2[...], m_ref[...], precision=HIGHEST, preferred_element_type=f32)

    onehot, flip = _rel_tables()
    return pl.pallas_call(
        body, name="rel_bias_grad", out_shape=_sds((NH, N_REL), f32), scratch_shapes=[pltpu.VMEM((NH, BANDP), f32)],
    )(gband, jnp.asarray(onehot), jnp.asarray(flip))


XBC_BLK = 512
XBC_COL0 = SSD_W // XBC_BLK
DT_COL = (SSD_W + XBC) // 128


def _conv_taps(ext, w_ref, b_ref, tm):
    n = ext.shape[0]
    pre = w_ref[3:4, :] * ext + b_ref[...]
    for j in range(3):
        pre = pre + w_ref[j:j + 1, :] * pltpu.roll(ext, 3 - j, 0)
    return pre


def _ssd_conv(proj2, conv_w, conv_b, tm):
    s = proj2.shape[0]
    nb = XBC // XBC_BLK

    def body(x_ref, p_ref, w_ref, b_ref, o_ref):
        i = pl.program_id(1)
        prev = jnp.where(i > 0, p_ref[...], 0.0)
        ext = jnp.concatenate([prev, x_ref[...]], axis=0)
        pre = _conv_taps(ext, w_ref, b_ref, tm)[8:8 + tm]
        o_ref[...] = pre * _sigmoid(pre)

    return pl.pallas_call(
        body, name="ssd_conv", grid=(nb, s // tm),
        in_specs=[pl.BlockSpec((tm, XBC_BLK), lambda j, i: (i, XBC_COL0 + j)),
                  pl.BlockSpec((8, XBC_BLK), lambda j, i: (jnp.maximum(i * (tm // 8) - 1, 0), XBC_COL0 + j)),
                  pl.BlockSpec((4, XBC_BLK), lambda j, i: (0, j)), pl.BlockSpec((1, XBC_BLK), lambda j, i: (0, j))],
        out_specs=pl.BlockSpec((tm, XBC_BLK), lambda j, i: (i, j)), out_shape=_sds((s, XBC), f32),
        compiler_params=_params(("parallel", "parallel")),
    )(proj2, proj2, conv_w, conv_b)


def _ssd_conv_bwd(dxbc, proj2, conv_w, conv_b, tm):
    s = proj2.shape[0]
    nb = XBC // XBC_BLK
    n = s // tm
    last8 = s // 8 - 1

    def body(x_ref, xp_ref, xn_ref, d_ref, dn_ref, w_ref, b_ref, o_ref, dw_ref, db_ref, acc):
        i = pl.program_id(1)

        @pl.when(i == 0)
        def _():
            acc[...] = jnp.zeros_like(acc)

        prev = jnp.where(i > 0, xp_ref[...], 0.0)
        ext = jnp.concatenate([prev, x_ref[...], xn_ref[...]], axis=0)
        pre = _conv_taps(ext, w_ref, b_ref, tm)
        sg = _sigmoid(pre)
        dnext = jnp.where(i < n - 1, dn_ref[...], 0.0)
        dext = jnp.concatenate([jnp.zeros((8, XBC_BLK), f32), d_ref[...], dnext], axis=0)
        dpre = dext * (sg * (1.0 + pre * (1.0 - sg)))
        rows = tm + 16
        dx = w_ref[3:4, :] * dpre
        for j in range(3):
            dx = dx + w_ref[j:j + 1, :] * pltpu.roll(dpre, rows - (3 - j), 0)
        o_ref[...] = dx[8:8 + tm].astype(bf16)
        dcur = dpre[8:8 + tm]
        acc[4] += _fold8(dcur)
        acc[3] += _fold8(dcur * ext[8:8 + tm])
        for j in range(3):
            acc[j] += _fold8(dcur * pltpu.roll(ext, 3 - j, 0)[8:8 + tm])

        @pl.when(i == n - 1)
        def _():
            for j in range(4):
                dw_ref[j:j + 1, :] = jnp.sum(acc[j], axis=0, keepdims=True)
            db_ref[...] = jnp.sum(acc[4], axis=0, keepdims=True)

    xcol = lambda j: XBC_COL0 + j
    return pl.pallas_call(
        body, name="ssd_conv_bwd", grid=(nb, n),
        in_specs=[pl.BlockSpec((tm, XBC_BLK), lambda j, i: (i, xcol(j))),
                  pl.BlockSpec((8, XBC_BLK), lambda j, i: (jnp.maximum(i * (tm // 8) - 1, 0), xcol(j))),
                  pl.BlockSpec((8, XBC_BLK), lambda j, i: (jnp.minimum((i + 1) * (tm // 8), last8), xcol(j))),
                  pl.BlockSpec((tm, XBC_BLK), lambda j, i: (i, j)),
                  pl.BlockSpec((8, XBC_BLK), lambda j, i: (jnp.minimum((i + 1) * (tm // 8), last8), j)),
                  pl.BlockSpec((4, XBC_BLK), lambda j, i: (0, j)), pl.BlockSpec((1, XBC_BLK), lambda j, i: (0, j))],
        out_specs=[pl.BlockSpec((tm, XBC_BLK), lambda j, i: (i, j)), pl.BlockSpec((4, XBC_BLK), lambda j, i: (0, j)),
                   pl.BlockSpec((1, XBC_BLK), lambda j, i: (0, j))],
        out_shape=[_sds((s, XBC), bf16), _sds((4, XBC), f32), _sds((1, XBC), f32)],
        scratch_shapes=[pltpu.VMEM((5, 8, XBC_BLK), f32)], compiler_params=_params(("parallel", "arbitrary")),
    )(proj2, proj2, proj2, dxbc, dxbc, conv_w, conv_b)


def _ssd_consts():
    ex = np.zeros((128, SSD_W), np.float32)
    for h in range(NH):
        ex[h, h * HD:(h + 1) * HD] = 1.0
    sel = np.zeros((8, 128), np.float32)
    for h in range(NH):
        sel[h // 2, h] = 1.0
    par = np.zeros((128, 128), np.float32)
    for r in range(128):
        for h in range(NH):
            par[r, h] = 1.0 if (h % 2) == (r // 64) else 0.0
    ones_blk = np.zeros((128, 128), np.float32)
    for r in range(128):
        ones_blk[r, (r // 64) * 64:(r // 64) * 64 + 64] = 1.0
    return ex, np.ascontiguousarray(ex.T), sel, par, ones_blk


SSD_SUB = 8


def _ssd_common(rs, xbc_ref, dtr_ref, a_ref, dtb_ref, ex_ref, sel_ref, par_ref):
    xs = xbc_ref[rs, 0:SSD_W]
    dt = _softplus(dtr_ref[rs, :] + dtb_ref[...])
    adt = dt * a_ref[...]
    r_i = lax.broadcasted_iota(jnp.int32, (CHUNK, CHUNK), 0)
    c_i = lax.broadcasted_iota(jnp.int32, (CHUNK, CHUNK), 1)
    tril = (r_i >= c_i).astype(f32)
    cs = _dot01(tril, adt, exact="a")
    cs2 = jnp.concatenate([cs, cs], axis=0) * par_ref[...]
    cstp = _dot01(sel_ref[...], cs2, tb=True, exact="a")
    both = _dot01(jnp.concatenate([dt, cs], axis=0), ex_ref[...])
    return xs, dt, cs, cstp, both[0:CHUNK], both[CHUNK:]


def _pair_mask():
    l_i = lax.broadcasted_iota(jnp.int32, (CHUNK, 128), 0)
    lane = lax.broadcasted_iota(jnp.int32, (CHUNK, 128), 1)
    return l_i >= (lane % CHUNK), lane < HD


def _block_diag(xp, first):
    z = jnp.zeros_like(xp)
    return jnp.concatenate([jnp.where(first, xp, z), jnp.where(first, z, xp)], axis=0)


def _ssd_fwd(xbc, proj2, a_row, dtb_row, dsk_full):
    s = xbc.shape[0]
    nc = s // CHUNK
    ex, ext, sel, par, ones_blk = _ssd_consts()

    def one_chunk(sub, states, refs):
        xbc_ref, dtr_ref, a_ref, dtb_ref, dsk_ref, ex_ref, sel_ref, par_ref, y_ref, hs_ref = refs
        rs = slice(sub * CHUNK, (sub + 1) * CHUNK)
        xs, dt, cs, cstp, dt_full, cs_full = _ssd_common(rs, xbc_ref, dtr_ref, a_ref, dtb_ref, ex_ref, sel_ref, par_ref)
        cs_last = cs_full[CHUNK - 1:CHUNK, :]
        xdt = xs * dt_full
        causal, first = _pair_mask()
        out = []
        for g in range(NG):
            gl = slice(g * GW, (g + 1) * GW)
            bg = xbc_ref[rs, SSD_W + g * NSTATE:SSD_W + (g + 1) * NSTATE].astype(bf16)
            cg = xbc_ref[rs, SSD_W + NG * NSTATE + g * NSTATE:SSD_W + NG * NSTATE + (g + 1) * NSTATE].astype(bf16)
            cb2 = lax.dot_general(cg, jnp.concatenate([bg, bg], axis=0), (((1,), (1,)), ((), ())), preferred_element_type=f32)
            hg = states[g]
            hs_ref[sub, g] = hg
            y0 = jnp.dot(cg, hg.astype(bf16), preferred_element_type=f32)
            yoff = jnp.exp(cs_full[:, gl]) * y0
            for j in range(GW // 128):
                pair = g * (GW // 128) + j
                pl_ = slice(pair * 128, (pair + 1) * 128)
                seg = jnp.exp(jnp.where(causal, cs_full[:, pl_] - cstp[pair:pair + 1, :], -jnp.inf))
                m = (cb2 * seg).astype(bf16)
                yd = jnp.dot(m, _block_diag(xdt[:, pl_].astype(bf16), first), preferred_element_type=f32)
                y_ref[rs, pl_] = yd + yoff[:, j * 128:(j + 1) * 128] + xs[:, pl_] * dsk_ref[:, pl_]
            xdec = (xdt[:, gl] * jnp.exp(cs_last[:, gl] - cs_full[:, gl])).astype(bf16)
            st = lax.dot_general(bg, xdec, (((0,), (0,)), ((), ())), preferred_element_type=f32)
            out.append(jnp.exp(cs_last[:, gl]) * hg + st)
        return out

    def body(*refs):
        hst = refs[-1]

        @pl.when(pl.program_id(0) == 0)
        def _():
            hst[...] = jnp.zeros_like(hst)

        states = [hst[g] for g in range(NG)]
        for sub in range(SSD_SUB):
            states = one_chunk(sub, states, refs[:-1])
        for g in range(NG):
            hst[g] = states[g]

    rows = SSD_SUB * CHUNK
    const = lambda shape: pl.BlockSpec(shape, lambda c: tuple(0 for _ in shape))
    return pl.pallas_call(
        body, name="ssd_fwd", grid=(nc // SSD_SUB,),
        in_specs=[pl.BlockSpec((rows, XBC), lambda c: (c, 0)), pl.BlockSpec((rows, 128), lambda c: (c, DT_COL)),
                  const((1, 128)), const((1, 128)), const((1, SSD_W)), const((128, SSD_W)), const((8, 128)), const((128, 128))],
        out_specs=[pl.BlockSpec((rows, SSD_W), lambda c: (c, 0)), pl.BlockSpec((SSD_SUB, NG, NSTATE, GW), lambda c: (c, 0, 0, 0))],
        out_shape=[_sds((s, SSD_W), f32), _sds((nc, NG, NSTATE, GW), f32)],
        scratch_shapes=[pltpu.VMEM((NG, NSTATE, GW), f32)], compiler_params=_params(("arbitrary",)),
    )(xbc, proj2, a_row, dtb_row, dsk_full, jnp.asarray(ex), jnp.asarray(sel), jnp.asarray(par))


def _ssd_bwd(xbc, proj2, dy, hsave, a_row, dtb_row, dsk_full):
    s = xbc.shape[0]
    nc = s // CHUNK
    ex, ext, sel, par, ones_blk = _ssd_consts()

    def one_chunk(sub, dhs, refs):
        (xbc_ref, dtr_ref, dy_ref, hs_ref, a_ref, dtb_ref, dsk_ref, ex_ref, ext_ref, sel_ref, par_ref, ob_ref,
         dxbc_ref, ddtr_ref, dd_ref, da_ref, ddtb_ref, dh, a_dd, a_da, a_dtb, dcs_lane, dcs_b, dxdt) = refs
        rs = slice(sub * CHUNK, (sub + 1) * CHUNK)
        dcs_lane, dcs_b, dxdt = dcs_lane.at[sub], dcs_b.at[sub], dxdt.at[sub]
        xs, dt, cs, cstp, dt_full, cs_full = _ssd_common(rs, xbc_ref, dtr_ref, a_ref, dtb_ref, ex_ref, sel_ref, par_ref)
        cs_last = cs_full[CHUNK - 1:CHUNK, :]
        xdt = xs * dt_full
        dyv = dy_ref[rs, :]
        a_dd[...] += _fold8(dyv * xs)
        causal, first = _pair_mask()
        diag = lax.broadcasted_iota(jnp.int32, (CHUNK, 128), 0) == lax.broadcasted_iota(jnp.int32, (CHUNK, 128), 1) % CHUNK
        dh_out = []
        for g in range(NG):
            gl = slice(g * GW, (g + 1) * GW)
            bcol = slice(SSD_W + g * NSTATE, SSD_W + (g + 1) * NSTATE)
            ccol = slice(SSD_W + NG * NSTATE + g * NSTATE, SSD_W + NG * NSTATE + (g + 1) * NSTATE)
            bg = xbc_ref[rs, bcol].astype(bf16)
            cg = xbc_ref[rs, ccol].astype(bf16)
            bg2 = jnp.concatenate([bg, bg], axis=0)
            cb2 = lax.dot_general(cg, bg2, (((1,), (1,)), ((), ())), preferred_element_type=f32)
            hg = hs_ref[sub, g]
            hgb = hg.astype(bf16)
            dhg = dhs[g]
            dhgb = dhg.astype(bf16)
            eg = jnp.exp(cs_full[:, gl])
            dec = jnp.exp(cs_last[:, gl] - cs_full[:, gl])
            gam = jnp.exp(cs_last[:, gl])
            dyg = dyv[:, gl]
            xdt_g = xdt[:, gl]
            y0 = jnp.dot(cg, hgb, preferred_element_type=f32)
            dy0 = (eg * dyg).astype(bf16)
            dcm = lax.dot_general(dy0, hgb, (((1,), (1,)), ((), ())), preferred_element_type=f32)
            dh_prev = gam * dhg + lax.dot_general(cg, dy0, (((0,), (0,)), ((), ())), preferred_element_type=f32)
            dgam = jnp.sum(dhg * hg, axis=0, keepdims=True) * gam
            dxdec = jnp.dot(bg, dhgb, preferred_element_type=f32)
            dbm = lax.dot_general((xdt_g * dec).astype(bf16), dhgb, (((1,), (1,)), ((), ())), preferred_element_type=f32)
            t = dxdec * xdt_g * dec
            dcs_lane[:, gl] = dyg * eg * y0 - t
            dcs_lane[CHUNK - 1:CHUNK, gl] += jnp.sum(t, axis=0, keepdims=True) + dgam
            dxdt[:, gl] = dxdec * dec
            dcb2 = jnp.zeros((CHUNK, 128), f32)
            for j in range(GW // 128):
                pair = g * (GW // 128) + j
                pl_ = slice(pair * 128, (pair + 1) * 128)
                seg = jnp.exp(jnp.where(causal, cs_full[:, pl_] - cstp[pair:pair + 1, :], -jnp.inf))
                m = cb2 * seg
                mb = m.astype(bf16)
                rhs = _block_diag(xdt[:, pl_].astype(bf16), first)
                dyp = dyv[:, pl_].astype(bf16)
                dm = lax.dot_general(dyp, rhs, (((1,), (1,)), ((), ())), preferred_element_type=f32)
                tt = lax.dot_general(mb, dyp, (((0,), (0,)), ((), ())), preferred_element_type=f32)
                dxdt[:, pl_] += jnp.where(first, tt[0:CHUNK], tt[CHUNK:])
                dcb2 = dcb2 + dm * seg
                w = dm * m
                colsum = jnp.sum(w, axis=0, keepdims=True)
                dcs_b[:, pl_] = _dot01(w - jnp.where(diag, colsum, 0.0), ob_ref[...])
            dcb2b = dcb2.astype(bf16)
            dcm = dcm + jnp.dot(dcb2b, bg2, preferred_element_type=f32)
            t3 = lax.dot_general(dcb2b, cg, (((0,), (0,)), ((), ())), preferred_element_type=f32)
            dxbc_ref[rs, bcol] = dbm + t3[0:CHUNK] + t3[CHUNK:]
            dxbc_ref[rs, ccol] = dcm
            dh_out.append(dh_prev)
        dxdtv = dxdt[...]
        both = _dot01(jnp.concatenate([dcs_lane[...] + dcs_b[...] * (1.0 / HD), dxdtv * xs], axis=0), ext_ref[...])
        dcs = both[0:CHUNK]
        r_i = lax.broadcasted_iota(jnp.int32, (CHUNK, CHUNK), 0)
        c_i = lax.broadcasted_iota(jnp.int32, (CHUNK, CHUNK), 1)
        triu = (r_i <= c_i).astype(f32)
        da_ = _dot01(triu, dcs, exact="a")
        ddt = da_ * a_ref[...] + both[CHUNK:]
        a_da[...] += _fold8(da_ * dt)
        dxbc_ref[rs, 0:SSD_W] = dyv * dsk_ref[...] + dxdtv * dt_full
        ddtr = ddt * _sigmoid(dtr_ref[rs, :] + dtb_ref[...])
        ddtr_ref[rs, :] = ddtr
        a_dtb[...] += _fold8(ddtr)
        return dh_out

    nsteps = nc // SSD_SUB

    def body(*refs):
        dd_ref, da_ref, ddtb_ref, dh, a_dd, a_da, a_dtb = refs[14:21]
        ext_ref = refs[8]
        step = pl.program_id(0)

        @pl.when(step == 0)
        def _():
            dh[...] = jnp.zeros_like(dh)
            a_dd[...] = jnp.zeros_like(a_dd)
            a_da[...] = jnp.zeros_like(a_da)
            a_dtb[...] = jnp.zeros_like(a_dtb)

        dhs = [dh[g] for g in range(NG)]
        for sub in reversed(range(SSD_SUB)):
            dhs = one_chunk(sub, dhs, refs)
        for g in range(NG):
            dh[g] = dhs[g]

        @pl.when(step == nsteps - 1)
        def _():
            dd_ref[...] = jnp.sum(jnp.dot(a_dd[...], ext_ref[...], precision=HIGHEST, preferred_element_type=f32), axis=0, keepdims=True)
            da_ref[...] = jnp.sum(a_da[...], axis=0, keepdims=True)
            ddtb_ref[...] = jnp.sum(a_dtb[...], axis=0, keepdims=True)

    rev = lambda c: nsteps - 1 - c
    rows = SSD_SUB * CHUNK
    const = lambda shape: pl.BlockSpec(shape, lambda c: tuple(0 for _ in shape))
    return pl.pallas_call(
        body, name="ssd_bwd", grid=(nsteps,),
        in_specs=[pl.BlockSpec((rows, XBC), lambda c: (rev(c), 0)), pl.BlockSpec((rows, 128), lambda c: (rev(c), DT_COL)),
                  pl.BlockSpec((rows, SSD_W), lambda c: (rev(c), 0)), pl.BlockSpec((SSD_SUB, NG, NSTATE, GW), lambda c: (rev(c), 0, 0, 0)),
                  const((1, 128)), const((1, 128)), const((1, SSD_W)), const((128, SSD_W)), const((SSD_W, 128)),
                  const((8, 128)), const((128, 128)), const((128, 128))],
        out_specs=[pl.BlockSpec((rows, XBC), lambda c: (rev(c), 0)), pl.BlockSpec((rows, 128), lambda c: (rev(c), 0)),
                   const((1, 128)), const((1, 128)), const((1, 128))],
        out_shape=[_sds((s, XBC), f32), _sds((s, 128), f32), _sds((1, 128), f32), _sds((1, 128), f32), _sds((1, 128), f32)],
        scratch_shapes=[pltpu.VMEM((NG, NSTATE, GW), f32), pltpu.VMEM((8, SSD_W), f32), pltpu.VMEM((8, 128), f32), pltpu.VMEM((8, 128), f32)]
        + [pltpu.VMEM((SSD_SUB, CHUNK, SSD_W), f32)] * 3,
        compiler_params=_params(("arbitrary",)),
    )(xbc, proj2, dy, hsave, a_row, dtb_row, dsk_full, jnp.asarray(ex), jnp.asarray(ext), jnp.asarray(sel), jnp.asarray(par),
      jnp.asarray(ones_blk))


def _local_step(x, tgt, mods, g_mix, rel, conv_w, conv_b, dt_bias, a_log, d_skip, g_att, g_ssd, g_ffn, g_final, weights):
    s = x.shape[0]
    tm_e = 512 if s % 512 == 0 else s
    tm_m = 512 if s % 512 == 0 else s
    tm_l = 1024 if s % 1024 == 0 else s
    tk = 2048 if s % 2048 == 0 else s
    sh1, sc1, gt1, sh2, sc2, gt2 = [mods[:, i * D:(i + 1) * D] for i in range(6)]

    h1b = _norm_mod("norm_mod_1", x, g_mix, sc1, sh1, tm_e)
    win, win_b = weights.w_in(h1b)
    qkv = _mm_nn_fullk("proj_qkv", h1b, win, tm_l, 1536, bf16, n=IN_A)
    proj2 = _mm_nn_fullk("proj_zxbcdt", h1b, win_b, tm_l, 896, f32)
    bias = _expand_bias(rel)
    att = _attn_fwd(qkv, bias)
    xbc = _ssd_conv(proj2, conv_w, conv_b, tm_l)
    a_row = jnp.pad(-jnp.exp(a_log), ((0, 0), (0, 128 - NH)))
    dtb_row = jnp.pad(dt_bias, ((0, 0), (0, 128 - NH)))
    dsk_full = jnp.repeat(d_skip, HD, axis=1)
    y, hsave = _ssd_fwd(xbc, proj2, a_row, dtb_row, dsk_full)
    mixcat = _mix_pre(att, y, proj2, g_att, g_ssd, tm_e)
    wout = weights.w_out(mixcat)
    mix = _mm_nn_fullk("proj_out", mixcat, wout, tm_l, D, f32)
    x2, h2b = _resid_norm_mod(x, gt1, mix, g_ffn, sc2, sh2, tm_e)
    wg4, wu4, wd4 = weights.ffn(h2b)
    act, sil, ud = _ffn_up(h2b, wg4, wu4, tm_m)
    ffn = _ffn_down(act, wd4, tm_l)

    dx3, dffn, loss, dg_final, dgt2 = _final_fwd_bwd(x2, ffn, gt2, g_final, tgt, tm_e)
    tok = weights.grad(("w_down",), [_grad_wdown4(act, dffn, 1024, tk)])
    dgate, dup = _ffn_dact(dffn, wd4, sil, ud, tm_l, dep=tok)
    tok = weights.grad(("w_gate", "w_up"), [_grad_cols4("grad_w_gate", h2b, dgate, 1024, tk), _grad_cols4("grad_w_up", h2b, dup, 1024, tk)])
    dh2 = _ffn_dh(dgate, dup, wg4, wu4, tm_m, dep=tok)
    dx2, dmix, dsc2, dsh2, dg_ffn, dgt1 = _norm_mod_bwd("norm_mod_bwd_2", dh2, x2, g_ffn, sc2, dx3, tm_e, mix=mix, gt=gt1)
    tok = weights.grad(("w_out",), [_mm_tn("grad_w_out", mixcat, dmix, 1024, 1024, tk, bf16).reshape(NSH, D // NSH, D)])
    dmc = _mm_nt("dmixcat", dmix, wout, tm_l, D, D, f32, dep=tok)
    datt, dy, dz, dg_att, dg_ssd = _mix_pre_bwd(dmc, att, y, proj2, g_att, g_ssd, tm_e)
    dq, dk, dv, gband = _attn_bwd(qkv, datt, bias)
    drel = _rel_bias_grad(gband.reshape(NH, CHUNK, BANDP))
    dxbc, ddtr, dd_row, da_row, ddtb_row = _ssd_bwd(xbc, proj2, dy, hsave, a_row, dtb_row, dsk_full)
    dxbc_raw, dconv_w, dconv_b = _ssd_conv_bwd(dxbc, proj2, conv_w, conv_b, tm_e)
    dproj = jnp.concatenate([dq, dk, dv, dz, dxbc_raw, ddtr.astype(bf16)], axis=1)
    gwin = _mm_tn("grad_w_in", h1b, dproj, 1024, 1152, tk, bf16)
    gwin4 = jnp.stack([jnp.pad(gwin[:, k * IN_SH:(k + 1) * IN_SH], ((0, 0), (0, IN_SHP - IN_SH))) for k in range(NSH)])
    tok = weights.grad(("w_in",), [gwin4])
    dh1 = _mm_nt("dh1", dproj, win, tm_l, 1024, 1920, f32, dep=tok)
    grad_x, dsc1, dsh1, dg_mix = _norm_mod_bwd("norm_mod_bwd_1", dh1, x, g_mix, sc1, dx2, tm_e)

    dmods = jnp.concatenate([dsh1, dsc1, dgt1, dsh2, dsc2, dgt2], axis=1)
    dd_skip = dd_row[:, :NH]
    da_log = da_row[:, :NH] * a_row[:, :NH]
    small = dict(g_mix=dg_mix, conv_b=dconv_b, dt_bias=ddtb_row[:, :NH], a_log=da_log, d_skip=dd_skip, g_att_out=dg_att,
                 g_ssd_out=dg_ssd, g_ffn=dg_ffn, g_final=dg_final, rel_bias=drel, conv_w=dconv_w)
    return loss[0, 0], grad_x, dmods, small


HBM = pl.BlockSpec(memory_space=pl.ANY)
VMEM = pl.BlockSpec(memory_space=pltpu.VMEM)


def _place():
    x, y, c = lax.axis_index("x"), lax.axis_index("y"), lax.axis_index("c")
    chips = [(1 - x, y), (x, 1 - y), (1 - x, 1 - y)]
    return x, y, c, chips


def _allgather8(name, payload, dep=None):
    r = payload.shape[0]
    deps = [] if dep is None else [dep]

    def body(x_ref, *rest):
        out_ref, send_sems, recv_sems, local_sem = rest[-4:]
        x, y, c, chips = _place()
        me, sibling = (x, y, c), (x, y, 1 - c)

        def slot(px, py, pc):
            return out_ref.at[4 * px + 2 * py + pc]

        def copy(k, block, to, src=None):
            return pltpu.make_async_remote_copy(
                src_ref=slot(*block) if src is None else src, dst_ref=slot(*block),
                send_sem=send_sems.at[k], recv_sem=recv_sems.at[k], device_id=to, device_id_type=MESH)

        mine = pltpu.make_async_copy(x_ref, slot(*me), local_sem)
        mine.start()
        first = [copy(0, me, sibling, src=x_ref)]
        first += [copy(1 + j, me, (*chip, c), src=x_ref) for j, chip in enumerate(chips)]
        for cp in first:
            cp.start()
        passed = [copy(4 + j, (*chip, c), sibling) for j, chip in enumerate(chips)]
        for j, chip in enumerate(chips):
            copy(1 + j, (*chip, c), me).wait_recv()
            passed[j].start()
        copy(0, sibling, me).wait_recv()
        for j, chip in enumerate(chips):
            copy(4 + j, (*chip, 1 - c), me).wait_recv()
        for cp in first + passed:
            cp.wait_send()
        mine.wait()

    return pl.pallas_call(
        body, name=name, out_shape=_sds((N_DEV, r, 128), f32), in_specs=[VMEM] * (1 + len(deps)), out_specs=VMEM,
        scratch_shapes=[pltpu.SemaphoreType.DMA((7,)), pltpu.SemaphoreType.DMA((7,)), pltpu.SemaphoreType.DMA],
    )(payload, *deps)


def _sum8(g):
    r = g.shape[1]

    def body(g_ref, o_ref):
        acc = g_ref[0]
        for i in range(1, N_DEV):
            acc = acc + g_ref[i]
        o_ref[...] = acc

    return pl.pallas_call(body, name="sum8", out_shape=_sds((r, 128), f32))(g)


SEM = pl.BlockSpec(memory_space=pltpu.SEMAPHORE)
EFFECT = pltpu.SideEffectType.DATAFLOW_SIDE_EFFECTING


def _gather_copies(ins, lands, send_sems, recv_sems):
    x, y, c, chips = _place()
    k = 2 * x + y
    starts, recvs = [], []
    for w in range(len(ins)):
        for j, (px, py) in enumerate(chips):
            def mk(dst):
                return pltpu.make_async_remote_copy(src_ref=ins[w].at[c], dst_ref=dst, send_sem=send_sems[w].at[j],
                                                    recv_sem=recv_sems[w].at[j], device_id=(px, py, c), device_id_type=MESH)
            starts.append(mk(lands[w].at[k, c]))
            recvs.append(mk(lands[w].at[2 * px + py, c]))
    return starts, recvs


def _reduce_copies(ins, lands, send_sems, recv_sems):
    x, y, c, chips = _place()
    k = 2 * x + y
    starts, recvs = [], []
    for w in range(len(ins)):
        for j, (px, py) in enumerate(chips):
            def mk(dst):
                return pltpu.make_async_remote_copy(src_ref=ins[w].at[2 * px + py], dst_ref=dst, send_sem=send_sems[w].at[j],
                                                    recv_sem=recv_sems[w].at[j], device_id=(px, py, c), device_id_type=MESH)
            starts.append(mk(lands[w].at[k]))
            recvs.append(mk(lands[w].at[2 * px + py]))
    return starts, recvs


def _split_start(name, copies, srcs, land_shapes):
    nw = len(srcs)

    def body(*refs):
        starts, _ = copies(refs[:nw], refs[nw:2 * nw], refs[2 * nw:3 * nw], refs[3 * nw:4 * nw])
        for cp in starts:
            cp.start()
        refs[6 * nw][...] = jnp.zeros((8, 128), f32)

    sems = [pltpu.SemaphoreType.DMA((3,))] * nw
    bufs = [pltpu.HBM(s.shape, bf16) for s in srcs] + [pltpu.HBM(s, bf16) for s in land_shapes]
    res = pl.pallas_call(
        body, name=name, out_shape=sems + sems + bufs + [_sds((8, 128), f32)],
        in_specs=[HBM] * (2 * nw), out_specs=[SEM] * (2 * nw) + [HBM] * (2 * nw) + [VMEM],
        input_output_aliases={i: 2 * nw + i for i in range(2 * nw)},
        compiler_params=pltpu.CompilerParams(has_side_effects=EFFECT),
    )(*[pltpu.with_memory_space_constraint(s, pltpu.HBM) for s in srcs],
      *[pltpu.with_memory_space_constraint(lax.empty(s, bf16), pltpu.HBM) for s in land_shapes])
    return res[:nw], res[nw:2 * nw], res[2 * nw:3 * nw], res[3 * nw:4 * nw], res[4 * nw]


def _split_wait(name, copies, send_sems, recv_sems, srcs, lands, after):
    nw = len(srcs)

    def body(*refs):
        starts, recvs = copies(refs[:nw], refs[nw:2 * nw], refs[2 * nw:3 * nw], refs[3 * nw:4 * nw])
        for s_, r_ in zip(starts, recvs):
            s_.wait_send()
            r_.wait_recv()

    bufs = [pltpu.HBM(s.shape, bf16) for s in srcs] + [pltpu.HBM(l.shape, bf16) for l in lands]
    res = pl.pallas_call(
        body, name=name, out_shape=bufs, in_specs=[HBM] * (2 * nw) + [SEM] * (2 * nw) + [HBM], out_specs=[HBM] * (2 * nw),
        input_output_aliases={i: i for i in range(2 * nw)},
        compiler_params=pltpu.CompilerParams(has_side_effects=EFFECT),
    )(*srcs, *lands, *send_sems, *recv_sems, after)
    return res[:nw], res[nw:]


def _gather_forward(name, shards, lands):
    nw = len(shards)

    def body(*refs):
        ins, lands_in, outs = refs[:nw], refs[nw:2 * nw], refs[2 * nw:3 * nw]
        st_a, st_b, st_c = refs[3 * nw:4 * nw], refs[4 * nw:5 * nw], refs[5 * nw:6 * nw]
        send_sems, recv_sems, load_sems, store_sems = refs[6 * nw:]
        x, y, c, chips = _place()
        k = 2 * x + y
        sibling = (x, y, 1 - c)
        ld_a = [pltpu.make_async_copy(ins[w].at[c], st_a[w], load_sems.at[w, 0]) for w in range(nw)]
        ld_b = [pltpu.make_async_copy(ins[w].at[1 - c], st_b[w], load_sems.at[w, 1]) for w in range(nw)]
        for cp in ld_a + ld_b:
            cp.start()
        st_own = []
        for w in range(nw):
            ld_a[w].wait()
            st_own.append(pltpu.make_async_copy(st_a[w], outs[w].at[k, c], store_sems.at[w, 0]))
            st_own[-1].start()
        for w in range(nw):
            ld_b[w].wait()
            st_own.append(pltpu.make_async_copy(st_b[w], outs[w].at[k, 1 - c], store_sems.at[w, 1]))
            st_own[-1].start()
        for cp in st_own:
            cp.wait()
        fwds = {}
        for j, (px, py) in enumerate(chips):
            kq = 2 * px + py
            for w in range(nw):
                slot = st_b[w] if j % 2 == 0 else st_c[w]
                if j == 2:
                    fwds[w, 0].wait_send()
                ld = pltpu.make_async_copy(lands_in[w].at[kq, c], slot, load_sems.at[w, 2 + j])
                ld.start()
                ld.wait()
                fwds[w, j] = pltpu.make_async_remote_copy(src_ref=slot, dst_ref=outs[w].at[kq, c], send_sem=send_sems.at[w, j],
                                                          recv_sem=recv_sems.at[w, j], device_id=sibling, device_id_type=MESH)
                fwds[w, j].start()
        for j, (px, py) in enumerate(chips):
            for w in range(nw):
                pltpu.make_async_remote_copy(src_ref=st_c[w], dst_ref=outs[w].at[2 * px + py, 1 - c], send_sem=send_sems.at[w, j],
                                             recv_sem=recv_sems.at[w, j], device_id=sibling, device_id_type=MESH).wait_recv()
        for w in range(nw):
            fwds[w, 1].wait_send()
            fwds[w, 2].wait_send()

    stage = [pltpu.VMEM(s.shape[1:], bf16) for s in shards]
    return pl.pallas_call(
        body, name=name, out_shape=[_sds(l.shape, bf16) for l in lands],
        in_specs=[HBM] * (2 * nw), out_specs=[HBM] * nw, input_output_aliases={nw + w: w for w in range(nw)},
        scratch_shapes=stage * 3 + [pltpu.SemaphoreType.DMA((nw, 3)), pltpu.SemaphoreType.DMA((nw, 3)), pltpu.SemaphoreType.DMA((nw, 5)),
                                    pltpu.SemaphoreType.DMA((nw, 2))],
        compiler_params=pltpu.CompilerParams(vmem_limit_bytes=VMEM_LIMIT),
    )(*shards, *lands)


def _rs_pair_exchange(name, grads):
    nw = len(grads)

    def body(*refs):
        ins, got, stage = refs[:nw], refs[nw:2 * nw], refs[2 * nw:3 * nw]
        send_sems, recv_sems, load_sems = refs[3 * nw:]
        x, y, c, _ = _place()

        def load(w, kk):
            return pltpu.make_async_copy(ins[w].at[kk, 1 - c], stage[w].at[kk % 2], load_sems.at[w, kk])

        def send(w, kk):
            return pltpu.make_async_remote_copy(src_ref=stage[w].at[kk % 2], dst_ref=got[w].at[kk], send_sem=send_sems.at[w, kk],
                                                recv_sem=recv_sems.at[w, kk], device_id=(x, y, 1 - c), device_id_type=MESH)

        for kk in range(2):
            for w in range(nw):
                load(w, kk).start()
        for kk in range(NSH):
            for w in range(nw):
                load(w, kk).wait()
                send(w, kk).start()
            if kk + 2 < NSH:
                for w in range(nw):
                    send(w, kk).wait_send()
                    load(w, kk + 2).start()
        for kk in range(NSH - 2, NSH):
            for w in range(nw):
                send(w, kk).wait_send()
        for kk in range(NSH):
            for w in range(nw):
                send(w, kk).wait_recv()

    return pl.pallas_call(
        body, name=name, out_shape=[_sds((NSH,) + g.shape[2:], bf16) for g in grads], in_specs=[HBM] * nw, out_specs=[HBM] * nw,
        scratch_shapes=[pltpu.VMEM((2,) + g.shape[2:], bf16) for g in grads]
        + [pltpu.SemaphoreType.DMA((nw, NSH)), pltpu.SemaphoreType.DMA((nw, NSH)), pltpu.SemaphoreType.DMA((nw, NSH))],
        compiler_params=pltpu.CompilerParams(vmem_limit_bytes=VMEM_LIMIT),
    )(*grads)


def _rs_pair_gather(name, halves):
    nw = len(halves)

    def body(*refs):
        ins, outs, stage = refs[:nw], refs[nw:2 * nw], refs[2 * nw:3 * nw]
        send_sems, recv_sems, local_sems, stage_sems = refs[3 * nw:]
        x, y, c, _ = _place()
        loads = [pltpu.make_async_copy(ins[w], stage[w], stage_sems.at[w]) for w in range(nw)]
        for cp in loads:
            cp.start()
        local, cps = [], []
        for w in range(nw):
            loads[w].wait()
            local.append(pltpu.make_async_copy(stage[w], outs[w].at[c], local_sems.at[w]))
            cps.append(pltpu.make_async_remote_copy(src_ref=stage[w], dst_ref=outs[w].at[c], send_sem=send_sems.at[w],
                                                    recv_sem=recv_sems.at[w], device_id=(x, y, 1 - c), device_id_type=MESH))
            local[w].start()
            cps[w].start()
        for w in range(nw):
            pltpu.make_async_remote_copy(src_ref=stage[w], dst_ref=outs[w].at[1 - c], send_sem=send_sems.at[w], recv_sem=recv_sems.at[w],
                                         device_id=(x, y, 1 - c), device_id_type=MESH).wait_recv()
        for cp in cps:
            cp.wait_send()
        for cp in local:
            cp.wait()

    return pl.pallas_call(
        body, name=name, out_shape=[_sds((2,) + h.shape, f32) for h in halves], in_specs=[HBM] * nw, out_specs=[HBM] * nw,
        scratch_shapes=[pltpu.VMEM(h.shape, f32) for h in halves]
        + [pltpu.SemaphoreType.DMA((nw,)), pltpu.SemaphoreType.DMA((nw,)), pltpu.SemaphoreType.DMA((nw,)), pltpu.SemaphoreType.DMA((nw,))],
        compiler_params=pltpu.CompilerParams(vmem_limit_bytes=VMEM_LIMIT),
    )(*halves)


def _row_tile(r, c, nbuf):
    budget = 24 * 1024 * 1024 // (2 * nbuf * 4 * c)
    fits = [t for t in range(16, r + 1, 16) if r % t == 0 and t <= budget]
    return max(fits) if fits else r


def _cast_bf16(name, a, dep=None):
    r, c = a.shape
    tr = _row_tile(r, c, 2)
    dep_specs, dep_ops = _dep_args(dep, 1)

    def body(a_ref, *rest):
        rest[-1][...] = a_ref[...].astype(bf16)

    spec = pl.BlockSpec((tr, c), lambda i: (i, 0))
    return pl.pallas_call(body, name=name, grid=(r // tr,), in_specs=[spec] + dep_specs, out_specs=spec, out_shape=_sds((r, c), bf16),
                          compiler_params=_params(("parallel",)))(a, *dep_ops)


def _w_in_columns(win4):
    tr = 256

    def body(a_ref, o_ref, ob_ref):
        for k in range(NSH):
            o_ref[:, IN_SH * k:IN_SH * (k + 1)] = a_ref[k][:, :IN_SH]
        o_ref[:, IN_COLS:] = jnp.zeros((tr, IN_P - IN_COLS), bf16)
        ob_ref[...] = o_ref[:, IN_A:]

    return pl.pallas_call(
        body, name="w_in_columns", grid=(D // tr,), in_specs=[pl.BlockSpec((NSH, tr, IN_SHP), lambda i: (0, i, 0))],
        out_specs=[pl.BlockSpec((tr, IN_P), lambda i: (i, 0)), pl.BlockSpec((tr, IN_B), lambda i: (i, 0))],
        out_shape=[_sds((D, IN_P), bf16), _sds((D, IN_B), bf16)], compiler_params=_params(("parallel",)))(win4)


def _pair_sum(name, core, grads, got):
    _, _, rh, c = grads.shape
    tr = _row_tile(rh, c, 2)

    def body(c_ref, a_ref, b_ref, o_ref):
        o_ref[...] = (a_ref[...].astype(f32) + b_ref[...].astype(f32)).astype(bf16)

    spec = pl.BlockSpec((None, tr, c), lambda k, i, c_ref: (k, i, 0))
    return pl.pallas_call(
        body, name=name, out_shape=_sds((NSH, rh, c), bf16),
        grid_spec=pltpu.PrefetchScalarGridSpec(
            num_scalar_prefetch=1, grid=(NSH, rh // tr),
            in_specs=[pl.BlockSpec((None, None, tr, c), lambda k, i, c_ref: (k, c_ref[0], i, 0)), spec], out_specs=spec),
        compiler_params=_params(("parallel", "parallel")))(core, grads, got)


def _chip_sum(name, chip, sums, lands):
    _, rh, c = sums.shape
    tr = _row_tile(rh, c, 4)

    def body(k_ref, own_ref, l_ref, o_ref):
        own = own_ref[...].astype(f32)
        acc = None
        for j in range(NSH):
            term = jnp.where(k_ref[0] == j, own, l_ref[j].astype(f32))
            acc = term if acc is None else acc + term
        o_ref[...] = acc

    return pl.pallas_call(
        body, name=name, out_shape=_sds((rh, c), f32),
        grid_spec=pltpu.PrefetchScalarGridSpec(
            num_scalar_prefetch=1, grid=(rh // tr,),
            in_specs=[pl.BlockSpec((None, tr, c), lambda i, k_ref: (k_ref[0], i, 0)), pl.BlockSpec((NSH, tr, c), lambda i, k_ref: (0, i, 0))],
            out_specs=pl.BlockSpec((tr, c), lambda i, k_ref: (i, 0))),
        compiler_params=_params(("parallel",)))(chip, sums, lands)


def _mods_part(cond16, w_ada, b_part):
    n = w_ada.shape[1]
    tn = 512

    def body(c_ref, w_ref, b_ref, o_ref):
        cv = c_ref[...]
        o_ref[...] = _dot(cv * _sigmoid(cv), w_ref[...]) + b_ref[...]

    return pl.pallas_call(
        body, name="mods_part", grid=(n // tn,),
        in_specs=[pl.BlockSpec((16, D), lambda j: (0, 0)), pl.BlockSpec((D, tn), lambda j: (0, j)), pl.BlockSpec((1, tn), lambda j: (0, j))],
        out_specs=pl.BlockSpec((16, tn), lambda j: (0, j)), out_shape=_sds((16, n), f32), compiler_params=_params(("parallel",)),
    )(cond16, w_ada, b_part)


def _grad_w_ada(cond16, dm16):
    n = dm16.shape[1]
    tr = 256

    def body(c_ref, d_ref, o_ref):
        cv = c_ref[...]
        o_ref[...] = _dot(cv * _sigmoid(cv), d_ref[...], ta=True)

    return pl.pallas_call(
        body, name="grad_w_ada", grid=(D // tr,),
        in_specs=[pl.BlockSpec((16, tr), lambda i: (0, i)), pl.BlockSpec((16, n), lambda i: (0, 0))],
        out_specs=pl.BlockSpec((tr, n), lambda i: (i, 0)), out_shape=_sds((D, n), f32), compiler_params=_params(("parallel",)),
    )(cond16, dm16)


def _adamw(name, w, g, m, v):
    r, c = w.shape
    tr = _row_tile(r, c, 7)
    spec = pl.BlockSpec((tr, c), lambda i: (i, 0))
    grid = (r // tr,)

    def body(w_ref, g_ref, m_ref, v_ref, d_ref, nm_ref, nv_ref):
        gv = g_ref[...]
        nm = ADAM_B1 * m_ref[...] + (1.0 - ADAM_B1) * gv
        nv = ADAM_B2 * v_ref[...] + (1.0 - ADAM_B2) * (gv * gv)
        nm_ref[...] = nm
        nv_ref[...] = nv
        m_hat = nm / (1.0 - ADAM_B1 ** ADAM_STEP)
        v_hat = nv / (1.0 - ADAM_B2 ** ADAM_STEP)
        d_ref[...] = -ADAM_LR * (m_hat / (jnp.sqrt(v_hat) + ADAM_EPS) + ADAM_WD * w_ref[...])

    return pl.pallas_call(body, name=name, grid=grid, in_specs=[spec] * 4, out_specs=[spec] * 3, out_shape=[_sds(w.shape, f32)] * 3,
                          compiler_params=_params(("parallel",)))(w, g, m, v)


def _pack(parts, rows):
    flat = []
    for p in parts:
        p = p.reshape(-1)
        flat.append(jnp.pad(p, (0, (-p.shape[0]) % 128)))
    v = jnp.concatenate(flat)
    return jnp.pad(v, (0, rows * 128 - v.shape[0])).reshape(rows, 128)


def _unpack(packed, sizes):
    lead = packed.shape[:-2]
    flat = packed.reshape(lead + (-1,))
    out, off = [], 0
    for n in sizes:
        out.append(flat[..., off:off + n])
        off += n + (-n) % 128
    return out


BIG = ("w_in", "w_out", "w_gate", "w_up", "w_down")
SMALL = ("b_ada", "g_mix", "conv_b", "dt_bias", "a_log", "d_skip", "g_att_out", "g_ssd_out", "g_ffn", "g_final", "rel_bias", "conv_w")
ORDER = ("w_ada", "b_ada", "g_mix", "w_in", "rel_bias", "conv_w", "conv_b", "dt_bias", "a_log", "d_skip", "g_att_out", "g_ssd_out",
         "w_out", "g_ffn", "w_gate", "w_up", "w_down", "g_final")
REL_SH = N_REL // NSH
CONVW_SH = XBC // NSH
ADA_SH = 6 * D // NSH


class _Exchange:
    def __init__(self, core, chip):
        self.core, self.chip = core, chip
        self.gathered = {}
        self.pending = []

    def gather(self, names, shards):
        ssem, rsem, thru, lands, token = _split_start("gather_start_" + "_".join(names), _gather_copies, shards,
                                                      [(NSH,) + s.shape for s in shards])
        self.gathered.update({n: (ssem[i], rsem[i], thru[i], lands[i]) for i, n in enumerate(names)})
        return token

    def _whole(self, names, after):
        ssem, rsem, thru, lands = zip(*[self.gathered[n] for n in names])
        tag = "_".join(names)
        thru, lands = _split_wait("gather_wait_" + tag, _gather_copies, ssem, rsem, thru, lands, after)
        return _gather_forward("gather_forward_" + tag, thru, lands)

    def w_in(self, after):
        (win4,) = self._whole(("w_in",), after)
        return _w_in_columns(win4.reshape(NSH, D, IN_SHP))

    def w_out(self, after):
        (wout4,) = self._whole(("w_out",), after)
        return wout4.reshape(D, D)

    def ffn(self, after):
        wg4, wu4, wd4 = self._whole(("w_gate", "w_up", "w_down"), after)
        return wg4.reshape(NSH, D, FSH), wu4.reshape(NSH, D, FSH), wd4.reshape(NSH, FSH, D)

    def grad(self, names, grads):
        tag = "_".join(names)
        stacked = [g.reshape(NSH, 2, g.shape[1] // 2, g.shape[2]) for g in grads]
        got = _rs_pair_exchange("rs_pair_exchange_" + tag, stacked)
        sums = [_pair_sum("pair_sum_" + n, self.core, o, g) for n, o, g in zip(names, stacked, got)]
        self.pending.append((names, _split_start("rs_start_" + tag, _reduce_copies, sums, [s.shape for s in sums])))
        return self.pending[-1][1][4]

    def finish(self, after):
        grads = {}
        for names, (ssem, rsem, sums, lands, _) in self.pending:
            tag = "_".join(names)
            sums, lands = _split_wait("rs_wait_" + tag, _reduce_copies, ssem, rsem, sums, lands, after)
            halves = [_chip_sum("chip_sum_" + n, self.chip, sm, ld) for n, sm, ld in zip(names, sums, lands)]
            for n, f in zip(names, _rs_pair_gather("rs_pair_gather_" + tag, halves)):
                grads[n] = f.reshape(2 * f.shape[1], f.shape[2])
        return grads


def kernel(x, c, w_ada, b_ada, g_mix, w_in, rel_bias, conv_w, conv_b, dt_bias, a_log, d_skip, g_att_out, g_ssd_out, w_out, g_ffn, w_gate, w_up, w_down, g_final, loss_target, m_w_ada, m_b_ada, m_g_mix, m_w_in, m_rel_bias, m_conv_w, m_conv_b, m_dt_bias, m_a_log, m_d_skip, m_g_att_out, m_g_ssd_out, m_w_out, m_g_ffn, m_w_gate, m_w_up, m_w_down, m_g_final, v_w_ada, v_b_ada, v_g_mix, v_w_in, v_rel_bias, v_conv_w, v_conv_b, v_dt_bias, v_a_log, v_d_skip, v_g_att_out, v_g_ssd_out, v_w_out, v_g_ffn, v_w_gate, v_w_up, v_w_down, v_g_final):
    args = dict(locals())
    w = {n: args[n] for n in ORDER}
    m = {n: args["m_" + n] for n in ORDER}
    v = {n: args["v_" + n] for n in ORDER}
    ix, iy, ic = lax.axis_index("x"), lax.axis_index("y"), lax.axis_index("c")
    chip = 2 * ix + iy
    dev = 2 * chip + ic
    s = x.shape[1]

    g1 = _allgather8("gather_inputs", _pack([c[0], rel_bias[0], conv_w[0]], 40))
    c_all, rel_sh, convw_sh = _unpack(g1, [D, NH * REL_SH, 4 * CONVW_SH])
    rel_full = jnp.concatenate([rel_sh[2 * k].reshape(NH, REL_SH) for k in range(NSH)], axis=1)
    convw_full = jnp.concatenate([convw_sh[2 * k].reshape(4, CONVW_SH) for k in range(NSH)], axis=1)
    cond16 = jnp.pad(c_all, ((0, 8), (0, 0)))
    b_part = lax.dynamic_slice_in_dim(b_ada, chip * ADA_SH, ADA_SH, axis=1)
    mods_part = _mods_part(cond16, w_ada[0], b_part)[:N_DEV]
    g2 = _allgather8("gather_mods", mods_part.reshape(N_DEV * ADA_SH // 128, 128))
    mods_all = jnp.concatenate([g2[2 * k].reshape(N_DEV, ADA_SH) for k in range(NSH)], axis=1)
    mods = lax.dynamic_slice_in_dim(mods_all, dev, 1, axis=0)

    exchange = _Exchange(jnp.reshape(ic, (1,)).astype(jnp.int32), jnp.reshape(chip, (1,)).astype(jnp.int32))
    shard_in = _cast_bf16("cast_w_in", jnp.pad(w_in[0], ((0, 0), (0, IN_SHP - IN_SH))), dep=g2[0, :8]).reshape(2, D // 2, IN_SHP)
    tok = exchange.gather(("w_in",), [shard_in])
    tok = exchange.gather(("w_out", "w_gate", "w_up", "w_down"), [
        _cast_bf16("cast_w_out", w_out[0], dep=tok).reshape(2, D // NSH // 2, D),
        _cast_bf16("cast_w_gate", w_gate[0], dep=tok).reshape(2, D // 2, FSH),
        _cast_bf16("cast_w_up", w_up[0], dep=tok).reshape(2, D // 2, FSH),
        _cast_bf16("cast_w_down", w_down[0], dep=tok).reshape(2, FSH // 2, D)])
    mods = mods + tok[:1, :1]

    loss, grad_x, dmods, small = _local_step(
        x[0], loss_target[0], mods, g_mix, rel_full, convw_full, conv_b, dt_bias, a_log, d_skip, g_att_out, g_ssd_out, g_ffn,
        g_final[None, :], exchange)

    small_names = ("g_mix", "conv_b", "dt_bias", "a_log", "d_skip", "g_att_out", "g_ssd_out", "g_ffn", "g_final", "rel_bias", "conv_w")
    g3 = _allgather8("gather_small_grads", _pack([dmods] + [small[n] for n in small_names], 264))
    sizes = [6 * D] + [int(np.prod(small[n].shape)) for n in small_names]
    dmods_all = _unpack(g3, sizes)[0]
    summed = _unpack(_sum8(g3), sizes)
    grads = {"b_ada": summed[0].reshape(1, 6 * D)}
    for n, val in zip(small_names, summed[1:]):
        grads[n] = val.reshape(small[n].shape)
    grads["rel_bias"] = lax.dynamic_slice_in_dim(grads["rel_bias"], chip * REL_SH, REL_SH, axis=1)
    grads["conv_w"] = lax.dynamic_slice_in_dim(grads["conv_w"], chip * CONVW_SH, CONVW_SH, axis=1)
    grads["g_final"] = grads["g_final"].reshape(D)
    dm16 = jnp.pad(lax.dynamic_slice_in_dim(dmods_all, chip * ADA_SH, ADA_SH, axis=1), ((0, 8), (0, 0)))
    grads["w_ada"] = _grad_w_ada(cond16, dm16)

    delta, new_m, new_v = {}, {}, {}
    delta["w_ada"], new_m["w_ada"], new_v["w_ada"] = _adamw("adamw_w_ada", w_ada[0], grads["w_ada"], m_w_ada[0], v_w_ada[0])
    grads.update(exchange.finish(grad_x))
    grads["w_in"] = grads["w_in"][:, :IN_SH]
    for n in BIG:
        delta[n], new_m[n], new_v[n] = _adamw("adamw_" + n, w[n][0], grads[n], m[n][0], v[n][0])
    sw = _pack([w[n] for n in SMALL], 200)
    sg = _pack([grads[n] for n in SMALL], 200)
    sm = _pack([m[n] for n in SMALL], 200)
    sv = _pack([v[n] for n in SMALL], 200)
    ssz = [int(np.prod(w[n].shape)) for n in SMALL]
    for dst, packed in zip((delta, new_m, new_v), _adamw("adamw_small", sw, sg, sm, sv)):
        for n, val in zip(SMALL, _unpack(packed, ssz)):
            dst[n] = val

    def shaped(d, n):
        return d[n].reshape(w[n].shape)

    total = lax.psum(loss, ("x", "y", "c"))
    return (total, grad_x[None], *[shaped(grads, n) for n in ORDER], *[shaped(delta, n) for n in ORDER],
            *[shaped(new_m, n) for n in ORDER], *[shaped(new_v, n) for n in ORDER])
```

```python
import functools

import numpy as np
import jax
import jax.numpy as jnp
from jax import lax
from jax.experimental import pallas as pl
from jax.experimental.pallas import tpu as pltpu

f32 = jnp.float32
bf16 = jnp.bfloat16
HIGHEST = lax.Precision.HIGHEST
MESH = pl.DeviceIdType.MESH

D = 2048
CHUNK = 64
LEFT = 8
BAND = (LEFT + 1) * CHUNK
BANDP = 640
PADK = LEFT * CHUNK
NH = 16
HD = 64
ATT_W = NH * HD
SSD_W = 1024
NG = 2
NSTATE = 128
GW = SSD_W // NG
XBC = SSD_W + 2 * NG * NSTATE
N_REL = 320
REL_CLIP = 256
FFN = 5632
NSH = 4
FSH = FFN // NSH
IN_COLS = 5648
IN_SH = IN_COLS // NSH
IN_SHP = 1536
IN_A = 3 * ATT_W
IN_B = 2688
IN_P = IN_A + IN_B
EPS = 1e-6
N_DEV = 8

ADAM_LR = 0.001
ADAM_B1 = 0.9
ADAM_B2 = 0.999
ADAM_EPS = 1e-08
ADAM_WD = 0.01
ADAM_STEP = 10

VMEM_LIMIT = 56 * 1024 * 1024


def _params(sem):
    return pltpu.CompilerParams(dimension_semantics=sem, vmem_limit_bytes=VMEM_LIMIT)


def _sds(shape, dtype):
    return jax.ShapeDtypeStruct(shape, dtype)


def _fold8(v):
    r, w = v.shape
    return jnp.sum(v.reshape(r // 8, 8, w), axis=0)


STRIP = 16


def _strips(tm, fn):
    def step(j, carry):
        fn(pl.ds(pl.multiple_of(j * STRIP, STRIP), STRIP))
        return carry
    lax.fori_loop(0, tm // STRIP, step, 0, unroll=4)


def _sigmoid(v):
    return 1.0 / (1.0 + jnp.exp(-v))


def _softplus(v):
    return jnp.maximum(v, 0.0) + jnp.log(1.0 + jnp.exp(-jnp.abs(v)))


def _dot(a, b, ta=False, tb=False):
    dn = (((0 if ta else 1,), (1 if tb else 0,)), ((), ()))
    return lax.dot_general(a.astype(bf16), b.astype(bf16), dn, preferred_element_type=f32)


def _dep_args(dep, ngrid):
    if dep is None:
        return [], []
    return [pl.BlockSpec((8, 128), lambda *_: (0, 0))], [dep]


def _dot01(a, b, ta=False, tb=False, exact="b"):
    dn = (((0 if ta else 1,), (1 if tb else 0,)), ((), ()))
    x = a if exact == "b" else b
    hi = x.astype(bf16)
    r = x - hi.astype(f32)
    mid = r.astype(bf16)
    lo = (r - mid.astype(f32)).astype(bf16)
    if exact == "b":
        m = b.astype(bf16)
        return sum(lax.dot_general(p, m, dn, preferred_element_type=f32) for p in (hi, mid, lo))
    m = a.astype(bf16)
    return sum(lax.dot_general(m, p, dn, preferred_element_type=f32) for p in (hi, mid, lo))


def _matmul(name, a, b, *, grid, a_spec, b_spec, o_spec, o_shape, o_dtype, acc_shape, ta=False, tb=False, dep=None):
    nk = grid[2]
    dep_specs, dep_ops = _dep_args(dep, 3)

    def body(a_ref, b_ref, *rest):
        o_ref, acc_ref = rest[-2:]
        p = _dot(a_ref[...], b_ref[...], ta, tb)
        if nk == 1:
            o_ref[...] = p.astype(o_ref.dtype)
        else:
            k = pl.program_id(2)

            @pl.when(k == 0)
            def _():
                acc_ref[...] = p

            @pl.when(jnp.logical_and(k > 0, k < nk - 1))
            def _():
                acc_ref[...] += p

            @pl.when(k == nk - 1)
            def _():
                o_ref[...] = (acc_ref[...] + p).astype(o_ref.dtype)

    return pl.pallas_call(
        body, name=name, grid=grid, in_specs=[a_spec, b_spec] + dep_specs, out_specs=o_spec,
        out_shape=_sds(o_shape, o_dtype), scratch_shapes=[pltpu.VMEM(acc_shape if nk > 1 else (8, 128), f32)],
        compiler_params=_params(("parallel", "parallel", "arbitrary")),
    )(a, b, *dep_ops)


def _mm_nn_fullk(name, a, b, tm, tn, o_dtype, n=None):
    m, k = a.shape
    n = b.shape[1] if n is None else n
    return _matmul(name, a, b, grid=(m // tm, n // tn, 1),
                   a_spec=pl.BlockSpec((tm, k), lambda i, j, kk: (i, 0)),
                   b_spec=pl.BlockSpec((k, tn), lambda i, j, kk: (0, j)),
                   o_spec=pl.BlockSpec((tm, tn), lambda i, j, kk: (i, j)),
                   o_shape=(m, n), o_dtype=o_dtype, acc_shape=(tm, tn))


def _mm_nt(name, a, b, tm, tn, tk, o_dtype, dep=None):
    m, k = a.shape
    n = b.shape[0]
    return _matmul(name, a, b, grid=(m // tm, n // tn, k // tk), tb=True, dep=dep,
                   a_spec=pl.BlockSpec((tm, tk), lambda i, j, kk: (i, kk)),
                   b_spec=pl.BlockSpec((tn, tk), lambda i, j, kk: (j, kk)),
                   o_spec=pl.BlockSpec((tm, tn), lambda i, j, kk: (i, j)),
                   o_shape=(m, n), o_dtype=o_dtype, acc_shape=(tm, tn))


def _mm_tn(name, a, b, tm, tn, tk, o_dtype):
    k, m = a.shape
    n = b.shape[1]
    return _matmul(name, a, b, grid=(m // tm, n // tn, k // tk), ta=True,
                   a_spec=pl.BlockSpec((tk, tm), lambda i, j, kk: (kk, i)),
                   b_spec=pl.BlockSpec((tk, tn), lambda i, j, kk: (kk, j)),
                   o_spec=pl.BlockSpec((tm, tn), lambda i, j, kk: (i, j)),
                   o_shape=(m, n), o_dtype=o_dtype, acc_shape=(tm, tn))


FSH_PARTS = (slice(0, 640), slice(640, FSH))


def _ffn_up(h2b, wg4, wu4, tm):
    s = h2b.shape[0]

    def body(h_ref, wg_ref, wu_ref, a_ref, s_ref, ud_ref):
        h = h_ref[...]
        for cols in FSH_PARTS:
            g = _dot(h, wg_ref[:, cols])
            u = _dot(h, wu_ref[:, cols])
            sg = _sigmoid(g)
            sil = g * sg
            a_ref[:, cols] = (sil * u).astype(bf16)
            s_ref[:, cols] = sil.astype(bf16)
            ud_ref[:, cols] = (u * (sg * (1.0 + g * (1.0 - sg)))).astype(bf16)

    wspec = pl.BlockSpec((None, D, FSH), lambda k, i: (k, 0, 0))
    ospec = pl.BlockSpec((tm, FSH), lambda k, i: (i, k))
    return pl.pallas_call(
        body, name="ffn_up", grid=(NSH, s // tm),
        in_specs=[pl.BlockSpec((tm, D), lambda k, i: (i, 0)), wspec, wspec],
        out_specs=[ospec, ospec, ospec], out_shape=[_sds((s, FFN), bf16)] * 3,
        compiler_params=_params(("parallel", "parallel")),
    )(h2b, wg4, wu4)


def _ffn_down(act, wd4, tm):
    s = act.shape[0]
    tn = D // 2

    def body(a_ref, b_ref, o_ref):
        o_ref[...] = jnp.dot(a_ref[...], b_ref[...].reshape(FFN, tn), preferred_element_type=f32)

    return pl.pallas_call(
        body, name="ffn_down", grid=(s // tm, D // tn),
        in_specs=[pl.BlockSpec((tm, FFN), lambda i, j: (i, 0)), pl.BlockSpec((NSH, FSH, tn), lambda i, j: (0, 0, j))],
        out_specs=pl.BlockSpec((tm, tn), lambda i, j: (i, j)), out_shape=_sds((s, D), f32),
        compiler_params=_params(("parallel", "parallel")),
    )(act, wd4)


def _ffn_dact(dffn, wd4, sil, ud, tm, dep=None):
    s = dffn.shape[0]
    dep_specs, dep_ops = _dep_args(dep, 2)

    def body(d_ref, w_ref, s_ref, ud_ref, *rest):
        dg_ref, du_ref = rest[-2:]
        d = d_ref[...]
        for cols in FSH_PARTS:
            dact = _dot(d, w_ref[cols, :], tb=True)
            dg_ref[:, cols] = (dact * ud_ref[:, cols].astype(f32)).astype(bf16)
            du_ref[:, cols] = (dact * s_ref[:, cols].astype(f32)).astype(bf16)

    blk = pl.BlockSpec((tm, FSH), lambda k, i: (i, k))
    return pl.pallas_call(
        body, name="ffn_dact", grid=(NSH, s // tm),
        in_specs=[pl.BlockSpec((tm, D), lambda k, i: (i, 0)), pl.BlockSpec((None, FSH, D), lambda k, i: (k, 0, 0)), blk, blk] + dep_specs,
        out_specs=[blk, blk], out_shape=[_sds((s, FFN), bf16), _sds((s, FFN), bf16)],
        compiler_params=_params(("parallel", "parallel")),
    )(dffn, wd4, sil, ud, *dep_ops)


def _ffn_dh(dgate, dup, wg4, wu4, tm, dep=None):
    s = dgate.shape[0]
    dep_specs, dep_ops = _dep_args(dep, 2)

    def body(dg_ref, du_ref, wg_ref, wu_ref, *rest):
        o_ref, acc_ref = rest[-2:]
        k = pl.program_id(1)
        p = _dot(dg_ref[...], wg_ref[...], tb=True) + _dot(du_ref[...], wu_ref[...], tb=True)

        @pl.when(k == 0)
        def _():
            acc_ref[...] = p

        @pl.when(jnp.logical_and(k > 0, k < NSH - 1))
        def _():
            acc_ref[...] += p

        @pl.when(k == NSH - 1)
        def _():
            o_ref[...] = acc_ref[...] + p

    aspec = pl.BlockSpec((tm, FSH), lambda i, k: (i, k))
    wspec = pl.BlockSpec((None, D, FSH), lambda i, k: (k, 0, 0))
    return pl.pallas_call(
        body, name="ffn_dh", grid=(s // tm, NSH), in_specs=[aspec, aspec, wspec, wspec] + dep_specs,
        out_specs=pl.BlockSpec((tm, D), lambda i, k: (i, 0)), out_shape=_sds((s, D), f32),
        scratch_shapes=[pltpu.VMEM((tm, D), f32)], compiler_params=_params(("parallel", "arbitrary")),
    )(dgate, dup, wg4, wu4, *dep_ops)


def _grad_cols4(name, h, dy, tm, tk):
    s = h.shape[0]
    return _matmul(name, h, dy, grid=(NSH, D // tm, s // tk), ta=True,
                   a_spec=pl.BlockSpec((tk, tm), lambda k, i, kk: (kk, i)),
                   b_spec=pl.BlockSpec((tk, FSH), lambda k, i, kk: (kk, k)),
                   o_spec=pl.BlockSpec((None, tm, FSH), lambda k, i, kk: (k, i, 0)),
                   o_shape=(NSH, D, FSH), o_dtype=bf16, acc_shape=(tm, FSH))


def _grad_wdown4(act, dffn, tn, tk):
    s = act.shape[0]
    return _matmul("grad_w_down", act, dffn, grid=(NSH, D // tn, s // tk), ta=True,
                   a_spec=pl.BlockSpec((tk, FSH), lambda k, j, kk: (kk, k)),
                   b_spec=pl.BlockSpec((tk, tn), lambda k, j, kk: (kk, j)),
                   o_spec=pl.BlockSpec((None, FSH, tn), lambda k, j, kk: (k, 0, j)),
                   o_shape=(NSH, FSH, D), o_dtype=bf16, acc_shape=(FSH, tn))


def _row_spec(w):
    return pl.BlockSpec((1, w), lambda i: (0, 0))


def _tile_spec(tm, w, col=0):
    return pl.BlockSpec((tm, w), lambda i: (i, col))


def _norm_mod(name, x, g, sc, sh, tm):
    s = x.shape[0]

    def body(x_ref, g_ref, sc_ref, sh_ref, o_ref):
        def strip(rows):
            xv = x_ref[rows, :]
            r = lax.rsqrt(jnp.mean(xv * xv, axis=-1, keepdims=True) + EPS)
            o_ref[rows, :] = (xv * r * g_ref[...] * (1.0 + sc_ref[...]) + sh_ref[...]).astype(bf16)

        _strips(tm, strip)

    return pl.pallas_call(
        body, name=name, grid=(s // tm,), in_specs=[_tile_spec(tm, D), _row_spec(D), _row_spec(D), _row_spec(D)],
        out_specs=_tile_spec(tm, D), out_shape=_sds((s, D), bf16), compiler_params=_params(("parallel",)),
    )(x, g, sc, sh)


def _resid_norm_mod(x, gt, mix, g, sc, sh, tm):
    s = x.shape[0]

    def body(x_ref, gt_ref, m_ref, g_ref, sc_ref, sh_ref, x2_ref, h_ref):
        def strip(rows):
            xv = x_ref[rows, :] + gt_ref[...] * m_ref[rows, :]
            x2_ref[rows, :] = xv
            r = lax.rsqrt(jnp.mean(xv * xv, axis=-1, keepdims=True) + EPS)
            h_ref[rows, :] = (xv * r * g_ref[...] * (1.0 + sc_ref[...]) + sh_ref[...]).astype(bf16)

        _strips(tm, strip)

    return pl.pallas_call(
        body, name="resid_norm_mod", grid=(s // tm,),
        in_specs=[_tile_spec(tm, D), _row_spec(D), _tile_spec(tm, D), _row_spec(D), _row_spec(D), _row_spec(D)],
        out_specs=[_tile_spec(tm, D), _tile_spec(tm, D)], out_shape=[_sds((s, D), f32), _sds((s, D), bf16)],
        compiler_params=_params(("parallel",)),
    )(x, gt, mix, g, sc, sh)


def _final_fwd_bwd(x2, ffn, gt2, g, tgt, tm):
    s = x2.shape[0]
    n = s // tm

    def body(x_ref, f_ref, gt_ref, g_ref, t_ref, dx_ref, df_ref, loss_ref, dg_ref, dgt_ref, a_loss, a_dg, a_dgt):
        i = pl.program_id(0)

        @pl.when(i == 0)
        def _():
            a_loss[...] = jnp.zeros_like(a_loss)
            a_dg[...] = jnp.zeros_like(a_dg)
            a_dgt[...] = jnp.zeros_like(a_dgt)

        def strip(rows):
            fv = f_ref[rows, :]
            gt = gt_ref[...]
            gv = g_ref[...]
            xv = x_ref[rows, :] + gt * fv
            r = lax.rsqrt(jnp.mean(xv * xv, axis=-1, keepdims=True) + EPS)
            xh = xv * r
            e = xh * gv - t_ref[rows, :]
            a_loss[...] += _fold8(e * e)
            dy = e * (1.0 / D)
            a_dg[...] += _fold8(dy * xh)
            t = dy * gv
            dx = r * (t - xh * jnp.mean(t * xh, axis=-1, keepdims=True))
            dx_ref[rows, :] = dx
            a_dgt[...] += _fold8(dx * fv)
            df_ref[rows, :] = (dx * gt).astype(bf16)

        _strips(tm, strip)

        @pl.when(i == n - 1)
        def _():
            tot = jnp.sum(jnp.sum(a_loss[...], axis=0, keepdims=True), axis=1, keepdims=True) * (0.5 / D)
            loss_ref[...] = jnp.broadcast_to(tot, (1, 128))
            dg_ref[...] = jnp.sum(a_dg[...], axis=0, keepdims=True)
            dgt_ref[...] = jnp.sum(a_dgt[...], axis=0, keepdims=True)

    return pl.pallas_call(
        body, name="final_fwd_bwd", grid=(n,),
        in_specs=[_tile_spec(tm, D), _tile_spec(tm, D), _row_spec(D), _row_spec(D), _tile_spec(tm, D)],
        out_specs=[_tile_spec(tm, D), _tile_spec(tm, D), _row_spec(128), _row_spec(D), _row_spec(D)],
        out_shape=[_sds((s, D), f32), _sds((s, D), bf16), _sds((1, 128), f32), _sds((1, D), f32), _sds((1, D), f32)],
        scratch_shapes=[pltpu.VMEM((8, D), f32)] * 3, compiler_params=_params(("arbitrary",)),
    )(x2, ffn, gt2, g, tgt)


def _norm_mod_bwd(name, dh, xin, g, sc, dres, tm, mix=None, gt=None):
    s = dh.shape[0]
    n = s // tm
    with_mix = mix is not None

    def body(*refs):
        if with_mix:
            dh_ref, x_ref, g_ref, sc_ref, dr_ref, m_ref, gt_ref, dx_ref, dm_ref, dsc_ref, dsh_ref, dg_ref, dgt_ref, a_sc, a_sh, a_g, a_gt = refs
        else:
            dh_ref, x_ref, g_ref, sc_ref, dr_ref, dx_ref, dsc_ref, dsh_ref, dg_ref, a_sc, a_sh, a_g = refs
        i = pl.program_id(0)

        @pl.when(i == 0)
        def _():
            a_sc[...] = jnp.zeros_like(a_sc)
            a_sh[...] = jnp.zeros_like(a_sh)
            a_g[...] = jnp.zeros_like(a_g)
            if with_mix:
                a_gt[...] = jnp.zeros_like(a_gt)

        def strip(rows):
            dh = dh_ref[rows, :]
            xv = x_ref[rows, :]
            gv = g_ref[...]
            r = lax.rsqrt(jnp.mean(xv * xv, axis=-1, keepdims=True) + EPS)
            xh = xv * r
            a_sc[...] += _fold8(dh * xh * gv)
            a_sh[...] += _fold8(dh)
            dn = dh * (1.0 + sc_ref[...])
            a_g[...] += _fold8(dn * xh)
            t = dn * gv
            dx = dr_ref[rows, :] + r * (t - xh * jnp.mean(t * xh, axis=-1, keepdims=True))
            dx_ref[rows, :] = dx
            if with_mix:
                a_gt[...] += _fold8(dx * m_ref[rows, :])
                dm_ref[rows, :] = (dx * gt_ref[...]).astype(bf16)

        _strips(tm, strip)

        @pl.when(i == n - 1)
        def _():
            dsc_ref[...] = jnp.sum(a_sc[...], axis=0, keepdims=True)
            dsh_ref[...] = jnp.sum(a_sh[...], axis=0, keepdims=True)
            dg_ref[...] = jnp.sum(a_g[...], axis=0, keepdims=True)
            if with_mix:
                dgt_ref[...] = jnp.sum(a_gt[...], axis=0, keepdims=True)

    tile, row = _tile_spec(tm, D), _row_spec(D)
    if with_mix:
        ins, args = [tile, tile, row, row, tile, tile, row], (dh, xin, g, sc, dres, mix, gt)
        outs = [tile, tile, row, row, row, row]
        shapes = [_sds((s, D), f32), _sds((s, D), bf16)] + [_sds((1, D), f32)] * 4
        nacc = 4
    else:
        ins, args = [tile, tile, row, row, tile], (dh, xin, g, sc, dres)
        outs = [tile, row, row, row]
        shapes = [_sds((s, D), f32)] + [_sds((1, D), f32)] * 3
        nacc = 3
    return pl.pallas_call(
        body, name=name, grid=(n,), in_specs=ins, out_specs=outs, out_shape=shapes,
        scratch_shapes=[pltpu.VMEM((8, D), f32)] * nacc, compiler_params=_params(("arbitrary",)),
    )(*args)


def _mix_pre(att, y, proj2, g_att, g_ssd, tm):
    s = att.shape[0]

    def body(a_ref, y_ref, z_ref, ga_ref, gs_ref, o_ref):
        def strip(rows):
            a = a_ref[rows, :]
            ra = lax.rsqrt(jnp.mean(a * a, axis=-1, keepdims=True) + EPS)
            o_ref[rows, 0:ATT_W] = (a * ra * ga_ref[...]).astype(bf16)
            z = z_ref[rows, :]
            u = y_ref[rows, :] * (z * _sigmoid(z))
            ru = lax.rsqrt(jnp.mean(u * u, axis=-1, keepdims=True) + EPS)
            o_ref[rows, ATT_W:] = (u * ru * gs_ref[...]).astype(bf16)

        _strips(tm, strip)

    t = _tile_spec(tm, ATT_W)
    return pl.pallas_call(
        body, name="mix_pre", grid=(s // tm,), in_specs=[t, t, t, _row_spec(ATT_W), _row_spec(SSD_W)],
        out_specs=_tile_spec(tm, D), out_shape=_sds((s, D), bf16), compiler_params=_params(("parallel",)),
    )(att, y, proj2, g_att, g_ssd)


def _mix_pre_bwd(dmc, att, y, proj2, g_att, g_ssd, tm):
    s = att.shape[0]
    n = s // tm

    def body(da_ref, ds_ref, a_ref, y_ref, z_ref, ga_ref, gs_ref, datt_ref, dy_ref, dz_ref, dga_ref, dgs_ref, acc_a, acc_s):
        i = pl.program_id(0)

        @pl.when(i == 0)
        def _():
            acc_a[...] = jnp.zeros_like(acc_a)
            acc_s[...] = jnp.zeros_like(acc_s)

        def strip(rows):
            a = a_ref[rows, :]
            ra = lax.rsqrt(jnp.mean(a * a, axis=-1, keepdims=True) + EPS)
            ah = a * ra
            dan = da_ref[rows, :]
            acc_a[...] += _fold8(dan * ah)
            t = dan * ga_ref[...]
            datt_ref[rows, :] = (ra * (t - ah * jnp.mean(t * ah, axis=-1, keepdims=True))).astype(bf16)
            z = z_ref[rows, :]
            yv = y_ref[rows, :]
            sz = _sigmoid(z)
            sil = z * sz
            u = yv * sil
            ru = lax.rsqrt(jnp.mean(u * u, axis=-1, keepdims=True) + EPS)
            uh = u * ru
            dsn = ds_ref[rows, :]
            acc_s[...] += _fold8(dsn * uh)
            t2 = dsn * gs_ref[...]
            du = ru * (t2 - uh * jnp.mean(t2 * uh, axis=-1, keepdims=True))
            dy_ref[rows, :] = du * sil
            dz_ref[rows, :] = (du * yv * (sz * (1.0 + z * (1.0 - sz)))).astype(bf16)

        _strips(tm, strip)

        @pl.when(i == n - 1)
        def _():
            dga_ref[...] = jnp.sum(acc_a[...], axis=0, keepdims=True)
            dgs_ref[...] = jnp.sum(acc_s[...], axis=0, keepdims=True)

    t = _tile_spec(tm, ATT_W)
    row = _row_spec(ATT_W)
    return pl.pallas_call(
        body, name="mix_pre_bwd", grid=(n,),
        in_specs=[_tile_spec(tm, ATT_W, 0), _tile_spec(tm, ATT_W, 1), t, t, t, row, row],
        out_specs=[t, t, t, row, row],
        out_shape=[_sds((s, ATT_W), bf16), _sds((s, SSD_W), f32), _sds((s, SSD_W), bf16), _sds((1, ATT_W), f32), _sds((1, SSD_W), f32)],
        scratch_shapes=[pltpu.VMEM((8, ATT_W), f32)] * 2, compiler_params=_params(("arbitrary",)),
    )(dmc, dmc, att, y, proj2, g_att, g_ssd)


ATT_GROUP = 8
ATT_GROUP_FWD = 16


def _pair_rows(qc):
    two = jnp.concatenate([qc, qc], axis=0)
    r = lax.broadcasted_iota(jnp.int32, (2 * CHUNK, 128), 0)
    l = lax.broadcasted_iota(jnp.int32, (2 * CHUNK, 128), 1)
    return jnp.where((r < CHUNK) == (l < HD), two, jnp.zeros_like(two))


def _scaled(q):
    return q * jnp.asarray(HD ** -0.5, q.dtype)


def _pair_scores(wt, kb, bias, r0, masked):
    sc = lax.dot_general(wt, kb, (((1,), (1,)), ((), ())), preferred_element_type=f32) + bias
    if not masked:
        return sc
    kidx = lax.broadcasted_iota(jnp.int32, sc.shape, 1)
    return jnp.where(r0 + kidx >= PADK, sc, -jnp.inf)


def _softmax(sc, axis):
    e = jnp.exp(sc - jnp.max(sc, axis=axis, keepdims=True))
    return e * (1.0 / jnp.sum(e, axis=axis, keepdims=True))


def _chunk_loops(nc, group, per_trip):
    n_masked = min(-(-LEFT // per_trip), nc // per_trip)

    def run(masked):
        def step(g, carry):
            group(g, masked)
            return carry
        return step

    lax.fori_loop(0, n_masked, run(True), 0)
    lax.fori_loop(n_masked, nc // per_trip, run(False), 0)


def _pair_diag(r):
    lane = lax.broadcasted_iota(jnp.int32, (CHUNK, 128), 1)
    return jnp.where(lane < HD, r[0:CHUNK], r[CHUNK:])


def _pad_keys(k_ref, kp, s):
    kp[0:PADK, :] = jnp.zeros((PADK, 128), bf16)
    kp[PADK:PADK + s, :] = k_ref[...]
    kp[PADK + s:, :] = jnp.zeros((CHUNK, 128), bf16)


def _attn_fwd(qkv, bias2):
    s = qkv.shape[0]
    nc = s // CHUNK
    npair = NH // 2
    per_trip = min(ATT_GROUP_FWD, nc)

    def body(q_ref, k_ref, v_ref, b_ref, o_ref, kp, vp):
        _pad_keys(k_ref, kp, s)
        _pad_keys(v_ref, vp, s)

        def group(g, masked):
            r0s = [pl.multiple_of((g * per_trip + u) * CHUNK, CHUNK) for u in range(per_trip)]
            scs = [_pair_scores(_pair_rows(_scaled(q_ref[pl.ds(r0, CHUNK), :])), kp[pl.ds(r0, BANDP), :], b_ref[...], r0, masked)
                   for r0 in r0s]
            ps = [_softmax(sc, -1).astype(bf16) for sc in scs]
            for r0, p in zip(r0s, ps):
                o_ref[pl.ds(r0, CHUNK), :] = _pair_diag(jnp.dot(p, vp[pl.ds(r0, BANDP), :], preferred_element_type=f32))

        _chunk_loops(nc, group, per_trip)

    return pl.pallas_call(
        body, name="attn_fwd", grid=(npair,),
        in_specs=[pl.BlockSpec((s, 128), lambda p: (0, p)), pl.BlockSpec((s, 128), lambda p: (0, npair + p)),
                  pl.BlockSpec((s, 128), lambda p: (0, 2 * npair + p)), pl.BlockSpec((None, 2 * CHUNK, BANDP), lambda p: (p, 0, 0))],
        out_specs=pl.BlockSpec((s, 128), lambda p: (0, p)), out_shape=_sds((s, ATT_W), f32),
        scratch_shapes=[pltpu.VMEM((PADK + s + CHUNK, 128), bf16)] * 2, compiler_params=_params(("parallel",)),
    )(qkv, qkv, qkv, bias2)


def _attn_bwd(qkv, datt, bias2):
    s = qkv.shape[0]
    nc = s // CHUNK
    npair = NH // 2
    rows = PADK + s + CHUNK
    nt = (((1,), (1,)), ((), ()))

    def body(q_ref, k_ref, v_ref, do_ref, b_ref, dq_ref, dk_ref, dv_ref, g_ref, kp, vp, dkp, dvp):
        _pad_keys(k_ref, kp, s)
        _pad_keys(v_ref, vp, s)
        dkp[...] = jnp.zeros_like(dkp)
        dvp[...] = jnp.zeros_like(dvp)
        g_ref[...] = jnp.zeros_like(g_ref)

        def group(g, masked):
            r0s = [pl.multiple_of((g * ATT_GROUP + u) * CHUNK, CHUNK) for u in range(ATT_GROUP)]
            wts = [_pair_rows(_scaled(q_ref[pl.ds(r0, CHUNK), :])) for r0 in r0s]
            dos = [_pair_rows(do_ref[pl.ds(r0, CHUNK), :]) for r0 in r0s]
            scs = [_pair_scores(wt, kp[pl.ds(r0, BANDP), :], b_ref[...], r0, masked) for wt, r0 in zip(wts, r0s)]
            dps = [lax.dot_general(do, vp[pl.ds(r0, BANDP), :], nt, preferred_element_type=f32) for do, r0 in zip(dos, r0s)]
            tn_ = (((0,), (0,)), ((), ()))
            for r0, wt, do, sc, dp in zip(r0s, wts, dos, scs, dps):
                p = _softmax(sc, -1)
                ds = p * (dp - jnp.sum(p * dp, axis=-1, keepdims=True))
                g_ref[...] += ds
                dsb = ds.astype(bf16)
                dq = jnp.dot(dsb, kp[pl.ds(r0, BANDP), :], preferred_element_type=f32)
                dq_ref[pl.ds(r0, CHUNK), :] = (_pair_diag(dq) * (HD ** -0.5)).astype(bf16)
                dkp[pl.ds(r0, BANDP), :] += lax.dot_general(dsb, wt, tn_, preferred_element_type=f32)
                dvp[pl.ds(r0, BANDP), :] += lax.dot_general(p.astype(bf16), do, tn_, preferred_element_type=f32)

        _chunk_loops(nc, group, ATT_GROUP)
        dk_ref[...] = dkp[PADK:PADK + s, :].astype(bf16)
        dv_ref[...] = dvp[PADK:PADK + s, :].astype(bf16)

    col = lambda off: pl.BlockSpec((s, 128), lambda p: (0, off + p))
    return pl.pallas_call(
        body, name="attn_bwd", grid=(npair,),
        in_specs=[col(0), col(npair), col(2 * npair), col(0), pl.BlockSpec((None, 2 * CHUNK, BANDP), lambda p: (p, 0, 0))],
        out_specs=[col(0), col(0), col(0), pl.BlockSpec((None, 2 * CHUNK, BANDP), lambda p: (p, 0, 0))],
        out_shape=[_sds((s, ATT_W), bf16)] * 3 + [_sds((npair, 2 * CHUNK, BANDP), f32)],
        scratch_shapes=[pltpu.VMEM((rows, 128), bf16)] * 2 + [pltpu.VMEM((rows, 128), f32)] * 2,
        compiler_params=_params(("parallel",)),
    )(qkv, qkv, qkv, datt, bias2)


def _rel_tables():
    onehot = np.zeros((BANDP, N_REL), np.float32)
    for j in range(BAND + CHUNK - 1):
        o = j - (CHUNK - 1)
        onehot[j, int(np.clip(PADK - o, -(CHUNK - 1), REL_CLIP)) + CHUNK - 1] = 1.0
    return onehot, np.ascontiguousarray(np.eye(CHUNK, dtype=np.float32)[::-1])


def _expand_bias(rel):
    ext = jnp.concatenate([jnp.broadcast_to(rel[:, N_REL - 1:], (NH, N_REL - 1)), rel[:, ::-1],
                           jnp.zeros((NH, BANDP - BAND + 1), f32)], axis=1)
    band = jnp.stack([ext[:, CHUNK - 1 - q:CHUNK - 1 - q + BANDP] for q in range(CHUNK)], axis=1)
    band = jnp.where(np.arange(BANDP) < BAND, band, -jnp.inf)
    return band.reshape(NH // 2, 2 * CHUNK, BANDP)


def _rel_bias_grad(gband):
    def body(g_ref, m_ref, flip_ref, o_ref, d2):
        for h in range(NH):
            rev = jnp.dot(flip_ref[...], g_ref[h], precision=HIGHEST, preferred_element_type=f32)
            rolled = pltpu.roll(rev, 0, 1, stride=1, stride_axis=0)
            d2[h:h + 1, :] = jnp.sum(rolled, axis=0, keepdims=True)
        o_ref[...] = jnp.dot(d2[...], m_ref[...], precision=HIGHEST, preferred_element_type=f32)

    onehot, flip = _rel_tables()
    return pl.pallas_call(
        body, name="rel_bias_grad", out_shape=_sds((NH, N_REL), f32), scratch_shapes=[pltpu.VMEM((NH, BANDP), f32)],
    )(gband, jnp.asarray(onehot), jnp.asarray(flip))


XBC_BLK = 512
XBC_COL0 = SSD_W // XBC_BLK
DT_COL = (SSD_W + XBC) // 128


def _conv_taps(ext, w_ref, b_ref, tm):
    n = ext.shape[0]
    pre = w_ref[3:4, :] * ext + b_ref[...]
    for j in range(3):
        pre = pre + w_ref[j:j + 1, :] * pltpu.roll(ext, 3 - j, 0)
    return pre


def _ssd_conv(proj2, conv_w, conv_b, tm):
    s = proj2.shape[0]
    nb = XBC // XBC_BLK

    def body(x_ref, p_ref, w_ref, b_ref, o_ref):
        i = pl.program_id(1)
        prev = jnp.where(i > 0, p_ref[...], 0.0)
        ext = jnp.concatenate([prev, x_ref[...]], axis=0)
        pre = _conv_taps(ext, w_ref, b_ref, tm)[8:8 + tm]
        o_ref[...] = pre * _sigmoid(pre)

    return pl.pallas_call(
        body, name="ssd_conv", grid=(nb, s // tm),
        in_specs=[pl.BlockSpec((tm, XBC_BLK), lambda j, i: (i, XBC_COL0 + j)),
                  pl.BlockSpec((8, XBC_BLK), lambda j, i: (jnp.maximum(i * (tm // 8) - 1, 0), XBC_COL0 + j)),
                  pl.BlockSpec((4, XBC_BLK), lambda j, i: (0, j)), pl.BlockSpec((1, XBC_BLK), lambda j, i: (0, j))],
        out_specs=pl.BlockSpec((tm, XBC_BLK), lambda j, i: (i, j)), out_shape=_sds((s, XBC), f32),
        compiler_params=_params(("parallel", "parallel")),
    )(proj2, proj2, conv_w, conv_b)


def _ssd_conv_bwd(dxbc, proj2, conv_w, conv_b, tm):
    s = proj2.shape[0]
    nb = XBC // XBC_BLK
    n = s // tm
    last8 = s // 8 - 1

    def body(x_ref, xp_ref, xn_ref, d_ref, dn_ref, w_ref, b_ref, o_ref, dw_ref, db_ref, acc):
        i = pl.program_id(1)

        @pl.when(i == 0)
        def _():
            acc[...] = jnp.zeros_like(acc)

        prev = jnp.where(i > 0, xp_ref[...], 0.0)
        ext = jnp.concatenate([prev, x_ref[...], xn_ref[...]], axis=0)
        pre = _conv_taps(ext, w_ref, b_ref, tm)
        sg = _sigmoid(pre)
        dnext = jnp.where(i < n - 1, dn_ref[...], 0.0)
        dext = jnp.concatenate([jnp.zeros((8, XBC_BLK), f32), d_ref[...], dnext], axis=0)
        dpre = dext * (sg * (1.0 + pre * (1.0 - sg)))
        rows = tm + 16
        dx = w_ref[3:4, :] * dpre
        for j in range(3):
            dx = dx + w_ref[j:j + 1, :] * pltpu.roll(dpre, rows - (3 - j), 0)
        o_ref[...] = dx[8:8 + tm].astype(bf16)
        dcur = dpre[8:8 + tm]
        acc[4] += _fold8(dcur)
        acc[3] += _fold8(dcur * ext[8:8 + tm])
        for j in range(3):
            acc[j] += _fold8(dcur * pltpu.roll(ext, 3 - j, 0)[8:8 + tm])

        @pl.when(i == n - 1)
        def _():
            for j in range(4):
                dw_ref[j:j + 1, :] = jnp.sum(acc[j], axis=0, keepdims=True)
            db_ref[...] = jnp.sum(acc[4], axis=0, keepdims=True)

    xcol = lambda j: XBC_COL0 + j
    return pl.pallas_call(
        body, name="ssd_conv_bwd", grid=(nb, n),
        in_specs=[pl.BlockSpec((tm, XBC_BLK), lambda j, i: (i, xcol(j))),
                  pl.BlockSpec((8, XBC_BLK), lambda j, i: (jnp.maximum(i * (tm // 8) - 1, 0), xcol(j))),
                  pl.BlockSpec((8, XBC_BLK), lambda j, i: (jnp.minimum((i + 1) * (tm // 8), last8), xcol(j))),
                  pl.BlockSpec((tm, XBC_BLK), lambda j, i: (i, j)),
                  pl.BlockSpec((8, XBC_BLK), lambda j, i: (jnp.minimum((i + 1) * (tm // 8), last8), j)),
                  pl.BlockSpec((4, XBC_BLK), lambda j, i: (0, j)), pl.BlockSpec((1, XBC_BLK), lambda j, i: (0, j))],
        out_specs=[pl.BlockSpec((tm, XBC_BLK), lambda j, i: (i, j)), pl.BlockSpec((4, XBC_BLK), lambda j, i: (0, j)),
                   pl.BlockSpec((1, XBC_BLK), lambda j, i: (0, j))],
        out_shape=[_sds((s, XBC), bf16), _sds((4, XBC), f32), _sds((1, XBC), f32)],
        scratch_shapes=[pltpu.VMEM((5, 8, XBC_BLK), f32)], compiler_params=_params(("parallel", "arbitrary")),
    )(proj2, proj2, proj2, dxbc, dxbc, conv_w, conv_b)


def _ssd_consts():
    ex = np.zeros((128, SSD_W), np.float32)
    for h in range(NH):
        ex[h, h * HD:(h + 1) * HD] = 1.0
    sel = np.zeros((8, 128), np.float32)
    for h in range(NH):
        sel[h // 2, h] = 1.0
    par = np.zeros((128, 128), np.float32)
    for r in range(128):
        for h in range(NH):
            par[r, h] = 1.0 if (h % 2) == (r // 64) else 0.0
    ones_blk = np.zeros((128, 128), np.float32)
    for r in range(128):
        ones_blk[r, (r // 64) * 64:(r // 64) * 64 + 64] = 1.0
    return ex, np.ascontiguousarray(ex.T), sel, par, ones_blk


SSD_SUB = 8


def _ssd_common(rs, xbc_ref, dtr_ref, a_ref, dtb_ref, ex_ref, sel_ref, par_ref):
    xs = xbc_ref[rs, 0:SSD_W]
    dt = _softplus(dtr_ref[rs, :] + dtb_ref[...])
    adt = dt * a_ref[...]
    r_i = lax.broadcasted_iota(jnp.int32, (CHUNK, CHUNK), 0)
    c_i = lax.broadcasted_iota(jnp.int32, (CHUNK, CHUNK), 1)
    tril = (r_i >= c_i).astype(f32)
    cs = _dot01(tril, adt, exact="a")
    cs2 = jnp.concatenate([cs, cs], axis=0) * par_ref[...]
    cstp = _dot01(sel_ref[...], cs2, tb=True, exact="a")
    both = _dot01(jnp.concatenate([dt, cs], axis=0), ex_ref[...])
    return xs, dt, cs, cstp, both[0:CHUNK], both[CHUNK:]


def _pair_mask():
    l_i = lax.broadcasted_iota(jnp.int32, (CHUNK, 128), 0)
    lane = lax.broadcasted_iota(jnp.int32, (CHUNK, 128), 1)
    return l_i >= (lane % CHUNK), lane < HD


def _block_diag(xp, first):
    z = jnp.zeros_like(xp)
    return jnp.concatenate([jnp.where(first, xp, z), jnp.where(first, z, xp)], axis=0)


def _ssd_fwd(xbc, proj2, a_row, dtb_row, dsk_full):
    s = xbc.shape[0]
    nc = s // CHUNK
    ex, ext, sel, par, ones_blk = _ssd_consts()

    def one_chunk(sub, states, refs):
        xbc_ref, dtr_ref, a_ref, dtb_ref, dsk_ref, ex_ref, sel_ref, par_ref, y_ref, hs_ref = refs
        rs = slice(sub * CHUNK, (sub + 1) * CHUNK)
        xs, dt, cs, cstp, dt_full, cs_full = _ssd_common(rs, xbc_ref, dtr_ref, a_ref, dtb_ref, ex_ref, sel_ref, par_ref)
        cs_last = cs_full[CHUNK - 1:CHUNK, :]
        xdt = xs * dt_full
        causal, first = _pair_mask()
        out = []
        for g in range(NG):
            gl = slice(g * GW, (g + 1) * GW)
            bg = xbc_ref[rs, SSD_W + g * NSTATE:SSD_W + (g + 1) * NSTATE].astype(bf16)
            cg = xbc_ref[rs, SSD_W + NG * NSTATE + g * NSTATE:SSD_W + NG * NSTATE + (g + 1) * NSTATE].astype(bf16)
            cb2 = lax.dot_general(cg, jnp.concatenate([bg, bg], axis=0), (((1,), (1,)), ((), ())), preferred_element_type=f32)
            hg = states[g]
            hs_ref[sub, g] = hg
            y0 = jnp.dot(cg, hg.astype(bf16), preferred_element_type=f32)
            yoff = jnp.exp(cs_full[:, gl]) * y0
            for j in range(GW // 128):
                pair = g * (GW // 128) + j
                pl_ = slice(pair * 128, (pair + 1) * 128)
                seg = jnp.exp(jnp.where(causal, cs_full[:, pl_] - cstp[pair:pair + 1, :], -jnp.inf))
                m = (cb2 * seg).astype(bf16)
                yd = jnp.dot(m, _block_diag(xdt[:, pl_].astype(bf16), first), preferred_element_type=f32)
                y_ref[rs, pl_] = yd + yoff[:, j * 128:(j + 1) * 128] + xs[:, pl_] * dsk_ref[:, pl_]
            xdec = (xdt[:, gl] * jnp.exp(cs_last[:, gl] - cs_full[:, gl])).astype(bf16)
            st = lax.dot_general(bg, xdec, (((0,), (0,)), ((), ())), preferred_element_type=f32)
            out.append(jnp.exp(cs_last[:, gl]) * hg + st)
        return out

    def body(*refs):
        hst = refs[-1]

        @pl.when(pl.program_id(0) == 0)
        def _():
            hst[...] = jnp.zeros_like(hst)

        states = [hst[g] for g in range(NG)]
        for sub in range(SSD_SUB):
            states = one_chunk(sub, states, refs[:-1])
        for g in range(NG):
            hst[g] = states[g]

    rows = SSD_SUB * CHUNK
    const = lambda shape: pl.BlockSpec(shape, lambda c: tuple(0 for _ in shape))
    return pl.pallas_call(
        body, name="ssd_fwd", grid=(nc // SSD_SUB,),
        in_specs=[pl.BlockSpec((rows, XBC), lambda c: (c, 0)), pl.BlockSpec((rows, 128), lambda c: (c, DT_COL)),
                  const((1, 128)), const((1, 128)), const((1, SSD_W)), const((128, SSD_W)), const((8, 128)), const((128, 128))],
        out_specs=[pl.BlockSpec((rows, SSD_W), lambda c: (c, 0)), pl.BlockSpec((SSD_SUB, NG, NSTATE, GW), lambda c: (c, 0, 0, 0))],
        out_shape=[_sds((s, SSD_W), f32), _sds((nc, NG, NSTATE, GW), f32)],
        scratch_shapes=[pltpu.VMEM((NG, NSTATE, GW), f32)], compiler_params=_params(("arbitrary",)),
    )(xbc, proj2, a_row, dtb_row, dsk_full, jnp.asarray(ex), jnp.asarray(sel), jnp.asarray(par))


def _ssd_bwd(xbc, proj2, dy, hsave, a_row, dtb_row, dsk_full):
    s = xbc.shape[0]
    nc = s // CHUNK
    ex, ext, sel, par, ones_blk = _ssd_consts()

    def one_chunk(sub, dhs, refs):
        (xbc_ref, dtr_ref, dy_ref, hs_ref, a_ref, dtb_ref, dsk_ref, ex_ref, ext_ref, sel_ref, par_ref, ob_ref,
         dxbc_ref, ddtr_ref, dd_ref, da_ref, ddtb_ref, dh, a_dd, a_da, a_dtb, dcs_lane, dcs_b, dxdt) = refs
        rs = slice(sub * CHUNK, (sub + 1) * CHUNK)
        dcs_lane, dcs_b, dxdt = dcs_lane.at[sub], dcs_b.at[sub], dxdt.at[sub]
        xs, dt, cs, cstp, dt_full, cs_full = _ssd_common(rs, xbc_ref, dtr_ref, a_ref, dtb_ref, ex_ref, sel_ref, par_ref)
        cs_last = cs_full[CHUNK - 1:CHUNK, :]
        xdt = xs * dt_full
        dyv = dy_ref[rs, :]
        a_dd[...] += _fold8(dyv * xs)
        causal, first = _pair_mask()
        diag = lax.broadcasted_iota(jnp.int32, (CHUNK, 128), 0) == lax.broadcasted_iota(jnp.int32, (CHUNK, 128), 1) % CHUNK
        dh_out = []
        for g in range(NG):
            gl = slice(g * GW, (g + 1) * GW)
            bcol = slice(SSD_W + g * NSTATE, SSD_W + (g + 1) * NSTATE)
            ccol = slice(SSD_W + NG * NSTATE + g * NSTATE, SSD_W + NG * NSTATE + (g + 1) * NSTATE)
            bg = xbc_ref[rs, bcol].astype(bf16)
            cg = xbc_ref[rs, ccol].astype(bf16)
            bg2 = jnp.concatenate([bg, bg], axis=0)
            cb2 = lax.dot_general(cg, bg2, (((1,), (1,)), ((), ())), preferred_element_type=f32)
            hg = hs_ref[sub, g]
            hgb = hg.astype(bf16)
            dhg = dhs[g]
            dhgb = dhg.astype(bf16)
            eg = jnp.exp(cs_full[:, gl])
            dec = jnp.exp(cs_last[:, gl] - cs_full[:, gl])
            gam = jnp.exp(cs_last[:, gl])
            dyg = dyv[:, gl]
            xdt_g = xdt[:, gl]
            y0 = jnp.dot(cg, hgb, preferred_element_type=f32)
            dy0 = (eg * dyg).astype(bf16)
            dcm = lax.dot_general(dy0, hgb, (((1,), (1,)), ((), ())), preferred_element_type=f32)
            dh_prev = gam * dhg + lax.dot_general(cg, dy0, (((0,), (0,)), ((), ())), preferred_element_type=f32)
            dgam = jnp.sum(dhg * hg, axis=0, keepdims=True) * gam
            dxdec = jnp.dot(bg, dhgb, preferred_element_type=f32)
            dbm = lax.dot_general((xdt_g * dec).astype(bf16), dhgb, (((1,), (1,)), ((), ())), preferred_element_type=f32)
            t = dxdec * xdt_g * dec
            dcs_lane[:, gl] = dyg * eg * y0 - t
            dcs_lane[CHUNK - 1:CHUNK, gl] += jnp.sum(t, axis=0, keepdims=True) + dgam
            dxdt[:, gl] = dxdec * dec
            dcb2 = jnp.zeros((CHUNK, 128), f32)
            for j in range(GW // 128):
                pair = g * (GW // 128) + j
                pl_ = slice(pair * 128, (pair + 1) * 128)
                seg = jnp.exp(jnp.where(causal, cs_full[:, pl_] - cstp[pair:pair + 1, :], -jnp.inf))
                m = cb2 * seg
                mb = m.astype(bf16)
                rhs = _block_diag(xdt[:, pl_].astype(bf16), first)
                dyp = dyv[:, pl_].astype(bf16)
                dm = lax.dot_general(dyp, rhs, (((1,), (1,)), ((), ())), preferred_element_type=f32)
                tt = lax.dot_general(mb, dyp, (((0,), (0,)), ((), ())), preferred_element_type=f32)
                dxdt[:, pl_] += jnp.where(first, tt[0:CHUNK], tt[CHUNK:])
                dcb2 = dcb2 + dm * seg
                w = dm * m
                colsum = jnp.sum(w, axis=0, keepdims=True)
                dcs_b[:, pl_] = _dot01(w - jnp.where(diag, colsum, 0.0), ob_ref[...])
            dcb2b = dcb2.astype(bf16)
            dcm = dcm + jnp.dot(dcb2b, bg2, preferred_element_type=f32)
            t3 = lax.dot_general(dcb2b, cg, (((0,), (0,)), ((), ())), preferred_element_type=f32)
            dxbc_ref[rs, bcol] = dbm + t3[0:CHUNK] + t3[CHUNK:]
            dxbc_ref[rs, ccol] = dcm
            dh_out.append(dh_prev)
        dxdtv = dxdt[...]
        both = _dot01(jnp.concatenate([dcs_lane[...] + dcs_b[...] * (1.0 / HD), dxdtv * xs], axis=0), ext_ref[...])
        dcs = both[0:CHUNK]
        r_i = lax.broadcasted_iota(jnp.int32, (CHUNK, CHUNK), 0)
        c_i = lax.broadcasted_iota(jnp.int32, (CHUNK, CHUNK), 1)
        triu = (r_i <= c_i).astype(f32)
        da_ = _dot01(triu, dcs, exact="a")
        ddt = da_ * a_ref[...] + both[CHUNK:]
        a_da[...] += _fold8(da_ * dt)
        dxbc_ref[rs, 0:SSD_W] = dyv * dsk_ref[...] + dxdtv * dt_full
        ddtr = ddt * _sigmoid(dtr_ref[rs, :] + dtb_ref[...])
        ddtr_ref[rs, :] = ddtr
        a_dtb[...] += _fold8(ddtr)
        return dh_out

    nsteps = nc // SSD_SUB

    def body(*refs):
        dd_ref, da_ref, ddtb_ref, dh, a_dd, a_da, a_dtb = refs[14:21]
        ext_ref = refs[8]
        step = pl.program_id(0)

        @pl.when(step == 0)
        def _():
            dh[...] = jnp.zeros_like(dh)
            a_dd[...] = jnp.zeros_like(a_dd)
            a_da[...] = jnp.zeros_like(a_da)
            a_dtb[...] = jnp.zeros_like(a_dtb)

        dhs = [dh[g] for g in range(NG)]
        for sub in reversed(range(SSD_SUB)):
            dhs = one_chunk(sub, dhs, refs)
        for g in range(NG):
            dh[g] = dhs[g]

        @pl.when(step == nsteps - 1)
        def _():
            dd_ref[...] = jnp.sum(jnp.dot(a_dd[...], ext_ref[...], precision=HIGHEST, preferred_element_type=f32), axis=0, keepdims=True)
            da_ref[...] = jnp.sum(a_da[...], axis=0, keepdims=True)
            ddtb_ref[...] = jnp.sum(a_dtb[...], axis=0, keepdims=True)

    rev = lambda c: nsteps - 1 - c
    rows = SSD_SUB * CHUNK
    const = lambda shape: pl.BlockSpec(shape, lambda c: tuple(0 for _ in shape))
    return pl.pallas_call(
        body, name="ssd_bwd", grid=(nsteps,),
        in_specs=[pl.BlockSpec((rows, XBC), lambda c: (rev(c), 0)), pl.BlockSpec((rows, 128), lambda c: (rev(c), DT_COL)),
                  pl.BlockSpec((rows, SSD_W), lambda c: (rev(c), 0)), pl.BlockSpec((SSD_SUB, NG, NSTATE, GW), lambda c: (rev(c), 0, 0, 0)),
                  const((1, 128)), const((1, 128)), const((1, SSD_W)), const((128, SSD_W)), const((SSD_W, 128)),
                  const((8, 128)), const((128, 128)), const((128, 128))],
        out_specs=[pl.BlockSpec((rows, XBC), lambda c: (rev(c), 0)), pl.BlockSpec((rows, 128), lambda c: (rev(c), 0)),
                   const((1, 128)), const((1, 128)), const((1, 128))],
        out_shape=[_sds((s, XBC), f32), _sds((s, 128), f32), _sds((1, 128), f32), _sds((1, 128), f32), _sds((1, 128), f32)],
        scratch_shapes=[pltpu.VMEM((NG, NSTATE, GW), f32), pltpu.VMEM((8, SSD_W), f32), pltpu.VMEM((8, 128), f32), pltpu.VMEM((8, 128), f32)]
        + [pltpu.VMEM((SSD_SUB, CHUNK, SSD_W), f32)] * 3,
        compiler_params=_params(("arbitrary",)),
    )(xbc, proj2, dy, hsave, a_row, dtb_row, dsk_full, jnp.asarray(ex), jnp.asarray(ext), jnp.asarray(sel), jnp.asarray(par),
      jnp.asarray(ones_blk))


def _local_step(x, tgt, mods, g_mix, rel, conv_w, conv_b, dt_bias, a_log, d_skip, g_att, g_ssd, g_ffn, g_final, weights):
    s = x.shape[0]
    tm_e = 512 if s % 512 == 0 else s
    tm_m = 512 if s % 512 == 0 else s
    tm_l = 1024 if s % 1024 == 0 else s
    tk = 2048 if s % 2048 == 0 else s
    sh1, sc1, gt1, sh2, sc2, gt2 = [mods[:, i * D:(i + 1) * D] for i in range(6)]

    h1b = _norm_mod("norm_mod_1", x, g_mix, sc1, sh1, tm_e)
    win, win_b = weights.w_in(h1b)
    qkv = _mm_nn_fullk("proj_qkv", h1b, win, tm_l, 1536, bf16, n=IN_A)
    proj2 = _mm_nn_fullk("proj_zxbcdt", h1b, win_b, tm_l, 896, f32)
    bias = _expand_bias(rel)
    att = _attn_fwd(qkv, bias)
    xbc = _ssd_conv(proj2, conv_w, conv_b, tm_l)
    a_row = jnp.pad(-jnp.exp(a_log), ((0, 0), (0, 128 - NH)))
    dtb_row = jnp.pad(dt_bias, ((0, 0), (0, 128 - NH)))
    dsk_full = jnp.repeat(d_skip, HD, axis=1)
    y, hsave = _ssd_fwd(xbc, proj2, a_row, dtb_row, dsk_full)
    mixcat = _mix_pre(att, y, proj2, g_att, g_ssd, tm_e)
    wout = weights.w_out(mixcat)
    mix = _mm_nn_fullk("proj_out", mixcat, wout, tm_l, D, f32)
    x2, h2b = _resid_norm_mod(x, gt1, mix, g_ffn, sc2, sh2, tm_e)
    wg4, wu4, wd4 = weights.ffn(h2b)
    act, sil, ud = _ffn_up(h2b, wg4, wu4, tm_m)
    ffn = _ffn_down(act, wd4, tm_m)

    dx3, dffn, loss, dg_final, dgt2 = _final_fwd_bwd(x2, ffn, gt2, g_final, tgt, tm_e)
    tok = weights.grad(("w_down",), [_grad_wdown4(act, dffn, 1024, tk)])
    dgate, dup = _ffn_dact(dffn, wd4, sil, ud, tm_l, dep=tok)
    tk2 = 4096 if s % 4096 == 0 else s
    tok = weights.grad(("w_gate", "w_up"), [_grad_cols4("grad_w_gate", h2b, dgate, 512, tk2), _grad_cols4("grad_w_up", h2b, dup, 512, tk2)])
    dh2 = _ffn_dh(dgate, dup, wg4, wu4, tm_m, dep=tok)
    dx2, dmix, dsc2, dsh2, dg_ffn, dgt1 = _norm_mod_bwd("norm_mod_bwd_2", dh2, x2, g_ffn, sc2, dx3, tm_e, mix=mix, gt=gt1)
    tok = weights.grad(("w_out",), [_mm_tn("grad_w_out", mixcat, dmix, 512, 1024, tk2, bf16).reshape(NSH, D // NSH, D)])
    dmc = _mm_nt("dmixcat", dmix, wout, tm_l, D, D, f32, dep=tok)
    datt, dy, dz, dg_att, dg_ssd = _mix_pre_bwd(dmc, att, y, proj2, g_att, g_ssd, tm_e)
    dq, dk, dv, gband = _attn_bwd(qkv, datt, bias)
    drel = _rel_bias_grad(gband.reshape(NH, CHUNK, BANDP))
    dxbc, ddtr, dd_row, da_row, ddtb_row = _ssd_bwd(xbc, proj2, dy, hsave, a_row, dtb_row, dsk_full)
    dxbc_raw, dconv_w, dconv_b = _ssd_conv_bwd(dxbc, proj2, conv_w, conv_b, tm_e)
    dproj = jnp.concatenate([dq, dk, dv, dz, dxbc_raw, ddtr.astype(bf16)], axis=1)
    gwin = _mm_tn("grad_w_in", h1b, dproj, 512, 1152, tk2, bf16)
    gwin4 = jnp.stack([jnp.pad(gwin[:, k * IN_SH:(k + 1) * IN_SH], ((0, 0), (0, IN_SHP - IN_SH))) for k in range(NSH)])
    tok = weights.grad(("w_in",), [gwin4])
    dh1 = _mm_nt("dh1", dproj, win, tm_m, 1024, IN_P, f32, dep=tok)
    grad_x, dsc1, dsh1, dg_mix = _norm_mod_bwd("norm_mod_bwd_1", dh1, x, g_mix, sc1, dx2, tm_e)

    dmods = jnp.concatenate([dsh1, dsc1, dgt1, dsh2, dsc2, dgt2], axis=1)
    dd_skip = dd_row[:, :NH]
    da_log = da_row[:, :NH] * a_row[:, :NH]
    small = dict(g_mix=dg_mix, conv_b=dconv_b, dt_bias=ddtb_row[:, :NH], a_log=da_log, d_skip=dd_skip, g_att_out=dg_att,
                 g_ssd_out=dg_ssd, g_ffn=dg_ffn, g_final=dg_final, rel_bias=drel, conv_w=dconv_w)
    return loss[0, 0], grad_x, dmods, small


HBM = pl.BlockSpec(memory_space=pl.ANY)
VMEM = pl.BlockSpec(memory_space=pltpu.VMEM)


def _place():
    x, y, c = lax.axis_index("x"), lax.axis_index("y"), lax.axis_index("c")
    chips = [(1 - x, y), (x, 1 - y), (1 - x, 1 - y)]
    return x, y, c, chips


def _allgather8(name, payload, dep=None):
    r = payload.shape[0]
    deps = [] if dep is None else [dep]

    def body(x_ref, *rest):
        out_ref, send_sems, recv_sems, local_sem = rest[-4:]
        x, y, c, chips = _place()
        me, sibling = (x, y, c), (x, y, 1 - c)

        def slot(px, py, pc):
            return out_ref.at[4 * px + 2 * py + pc]

        def copy(k, block, to, src=None):
            return pltpu.make_async_remote_copy(
                src_ref=slot(*block) if src is None else src, dst_ref=slot(*block),
                send_sem=send_sems.at[k], recv_sem=recv_sems.at[k], device_id=to, device_id_type=MESH)

        mine = pltpu.make_async_copy(x_ref, slot(*me), local_sem)
        mine.start()
        first = [copy(0, me, sibling, src=x_ref)]
        first += [copy(1 + j, me, (*chip, c), src=x_ref) for j, chip in enumerate(chips)]
        for cp in first:
            cp.start()
        passed = [copy(4 + j, (*chip, c), sibling) for j, chip in enumerate(chips)]
        for j, chip in enumerate(chips):
            copy(1 + j, (*chip, c), me).wait_recv()
            passed[j].start()
        copy(0, sibling, me).wait_recv()
        for j, chip in enumerate(chips):
            copy(4 + j, (*chip, 1 - c), me).wait_recv()
        for cp in first + passed:
            cp.wait_send()
        mine.wait()

    return pl.pallas_call(
        body, name=name, out_shape=_sds((N_DEV, r, 128), f32), in_specs=[VMEM] * (1 + len(deps)), out_specs=VMEM,
        scratch_shapes=[pltpu.SemaphoreType.DMA((7,)), pltpu.SemaphoreType.DMA((7,)), pltpu.SemaphoreType.DMA],
    )(payload, *deps)


def _sum8(g):
    r = g.shape[1]

    def body(g_ref, o_ref):
        acc = g_ref[0]
        for i in range(1, N_DEV):
            acc = acc + g_ref[i]
        o_ref[...] = acc

    return pl.pallas_call(body, name="sum8", out_shape=_sds((r, 128), f32))(g)


SEM = pl.BlockSpec(memory_space=pltpu.SEMAPHORE)
EFFECT = pltpu.SideEffectType.DATAFLOW_SIDE_EFFECTING


def _gather_copies(ins, lands, send_sems, recv_sems):
    x, y, c, chips = _place()
    k = 2 * x + y
    starts, recvs = [], []
    for w in range(len(ins)):
        for j, (px, py) in enumerate(chips):
            def mk(dst):
                return pltpu.make_async_remote_copy(src_ref=ins[w].at[c], dst_ref=dst, send_sem=send_sems[w].at[j],
                                                    recv_sem=recv_sems[w].at[j], device_id=(px, py, c), device_id_type=MESH)
            starts.append(mk(lands[w].at[k, c]))
            recvs.append(mk(lands[w].at[2 * px + py, c]))
    return starts, recvs


def _reduce_copies(ins, lands, send_sems, recv_sems):
    x, y, c, chips = _place()
    k = 2 * x + y
    starts, recvs = [], []
    for w in range(len(ins)):
        for j, (px, py) in enumerate(chips):
            def mk(dst):
                return pltpu.make_async_remote_copy(src_ref=ins[w].at[2 * px + py], dst_ref=dst, send_sem=send_sems[w].at[j],
                                                    recv_sem=recv_sems[w].at[j], device_id=(px, py, c), device_id_type=MESH)
            starts.append(mk(lands[w].at[k]))
            recvs.append(mk(lands[w].at[2 * px + py]))
    return starts, recvs


def _split_start(name, copies, srcs, land_shapes):
    nw = len(srcs)

    def body(*refs):
        starts, _ = copies(refs[:nw], refs[nw:2 * nw], refs[2 * nw:3 * nw], refs[3 * nw:4 * nw])
        for cp in starts:
            cp.start()
        refs[6 * nw][...] = jnp.zeros((8, 128), f32)

    sems = [pltpu.SemaphoreType.DMA((3,))] * nw
    bufs = [pltpu.HBM(s.shape, bf16) for s in srcs] + [pltpu.HBM(s, bf16) for s in land_shapes]
    res = pl.pallas_call(
        body, name=name, out_shape=sems + sems + bufs + [_sds((8, 128), f32)],
        in_specs=[HBM] * (2 * nw), out_specs=[SEM] * (2 * nw) + [HBM] * (2 * nw) + [VMEM],
        input_output_aliases={i: 2 * nw + i for i in range(2 * nw)},
        compiler_params=pltpu.CompilerParams(has_side_effects=EFFECT),
    )(*[pltpu.with_memory_space_constraint(s, pltpu.HBM) for s in srcs],
      *[pltpu.with_memory_space_constraint(lax.empty(s, bf16), pltpu.HBM) for s in land_shapes])
    return res[:nw], res[nw:2 * nw], res[2 * nw:3 * nw], res[3 * nw:4 * nw], res[4 * nw]


def _split_wait(name, copies, send_sems, recv_sems, srcs, lands, after):
    nw = len(srcs)

    def body(*refs):
        starts, recvs = copies(refs[:nw], refs[nw:2 * nw], refs[2 * nw:3 * nw], refs[3 * nw:4 * nw])
        for s_, r_ in zip(starts, recvs):
            s_.wait_send()
            r_.wait_recv()

    bufs = [pltpu.HBM(s.shape, bf16) for s in srcs] + [pltpu.HBM(l.shape, bf16) for l in lands]
    res = pl.pallas_call(
        body, name=name, out_shape=bufs, in_specs=[HBM] * (2 * nw) + [SEM] * (2 * nw) + [HBM], out_specs=[HBM] * (2 * nw),
        input_output_aliases={i: i for i in range(2 * nw)},
        compiler_params=pltpu.CompilerParams(has_side_effects=EFFECT),
    )(*srcs, *lands, *send_sems, *recv_sems, after)
    return res[:nw], res[nw:]


def _gather_forward(name, shards, lands):
    nw = len(shards)

    def body(*refs):
        ins, lands_in, outs = refs[:nw], refs[nw:2 * nw], refs[2 * nw:3 * nw]
        st_a, st_b, st_c = refs[3 * nw:4 * nw], refs[4 * nw:5 * nw], refs[5 * nw:6 * nw]
        send_sems, recv_sems, load_sems, store_sems = refs[6 * nw:]
        x, y, c, chips = _place()
        k = 2 * x + y
        sibling = (x, y, 1 - c)
        ld_a = [pltpu.make_async_copy(ins[w].at[c], st_a[w], load_sems.at[w, 0]) for w in range(nw)]
        ld_b = [pltpu.make_async_copy(ins[w].at[1 - c], st_b[w], load_sems.at[w, 1]) for w in range(nw)]
        for cp in ld_a + ld_b:
            cp.start()
        st_own = []
        for w in range(nw):
            ld_a[w].wait()
            st_own.append(pltpu.make_async_copy(st_a[w], outs[w].at[k, c], store_sems.at[w, 0]))
            st_own[-1].start()
        for w in range(nw):
            ld_b[w].wait()
            st_own.append(pltpu.make_async_copy(st_b[w], outs[w].at[k, 1 - c], store_sems.at[w, 1]))
            st_own[-1].start()
        for cp in st_own:
            cp.wait()
        fwds = {}
        for j, (px, py) in enumerate(chips):
            kq = 2 * px + py
            for w in range(nw):
                slot = st_b[w] if j % 2 == 0 else st_c[w]
                if j == 2:
                    fwds[w, 0].wait_send()
                ld = pltpu.make_async_copy(lands_in[w].at[kq, c], slot, load_sems.at[w, 2 + j])
                ld.start()
                ld.wait()
                fwds[w, j] = pltpu.make_async_remote_copy(src_ref=slot, dst_ref=outs[w].at[kq, c], send_sem=send_sems.at[w, j],
                                                          recv_sem=recv_sems.at[w, j], device_id=sibling, device_id_type=MESH)
                fwds[w, j].start()
        for j, (px, py) in enumerate(chips):
            for w in range(nw):
                pltpu.make_async_remote_copy(src_ref=st_c[w], dst_ref=outs[w].at[2 * px + py, 1 - c], send_sem=send_sems.at[w, j],
                                             recv_sem=recv_sems.at[w, j], device_id=sibling, device_id_type=MESH).wait_recv()
        for w in range(nw):
            fwds[w, 1].wait_send()
            fwds[w, 2].wait_send()

    stage = [pltpu.VMEM(s.shape[1:], bf16) for s in shards]
    return pl.pallas_call(
        body, name=name, out_shape=[_sds(l.shape, bf16) for l in lands],
        in_specs=[HBM] * (2 * nw), out_specs=[HBM] * nw, input_output_aliases={nw + w: w for w in range(nw)},
        scratch_shapes=stage * 3 + [pltpu.SemaphoreType.DMA((nw, 3)), pltpu.SemaphoreType.DMA((nw, 3)), pltpu.SemaphoreType.DMA((nw, 5)),
                                    pltpu.SemaphoreType.DMA((nw, 2))],
        compiler_params=pltpu.CompilerParams(vmem_limit_bytes=VMEM_LIMIT),
    )(*shards, *lands)


def _rs_pair_exchange(name, grads):
    nw = len(grads)

    def body(*refs):
        ins, got, stage = refs[:nw], refs[nw:2 * nw], refs[2 * nw:3 * nw]
        send_sems, recv_sems, load_sems = refs[3 * nw:]
        x, y, c, _ = _place()

        def load(w, kk):
            return pltpu.make_async_copy(ins[w].at[kk, 1 - c], stage[w].at[kk % 2], load_sems.at[w, kk])

        def send(w, kk):
            return pltpu.make_async_remote_copy(src_ref=stage[w].at[kk % 2], dst_ref=got[w].at[kk], send_sem=send_sems.at[w, kk],
                                                recv_sem=recv_sems.at[w, kk], device_id=(x, y, 1 - c), device_id_type=MESH)

        for kk in range(2):
            for w in range(nw):
                load(w, kk).start()
        for kk in range(NSH):
            for w in range(nw):
                load(w, kk).wait()
                send(w, kk).start()
            if kk + 2 < NSH:
                for w in range(nw):
                    send(w, kk).wait_send()
                    load(w, kk + 2).start()
        for kk in range(NSH - 2, NSH):
            for w in range(nw):
                send(w, kk).wait_send()
        for kk in range(NSH):
            for w in range(nw):
                send(w, kk).wait_recv()

    return pl.pallas_call(
        body, name=name, out_shape=[_sds((NSH,) + g.shape[2:], bf16) for g in grads], in_specs=[HBM] * nw, out_specs=[HBM] * nw,
        scratch_shapes=[pltpu.VMEM((2,) + g.shape[2:], bf16) for g in grads]
        + [pltpu.SemaphoreType.DMA((nw, NSH)), pltpu.SemaphoreType.DMA((nw, NSH)), pltpu.SemaphoreType.DMA((nw, NSH))],
        compiler_params=pltpu.CompilerParams(vmem_limit_bytes=VMEM_LIMIT),
    )(*grads)


def _rs_pair_gather(name, halves):
    nw = len(halves)

    def body(*refs):
        ins, outs, stage = refs[:nw], refs[nw:2 * nw], refs[2 * nw:3 * nw]
        send_sems, recv_sems, local_sems, stage_sems = refs[3 * nw:]
        x, y, c, _ = _place()
        loads = [pltpu.make_async_copy(ins[w], stage[w], stage_sems.at[w]) for w in range(nw)]
        for cp in loads:
            cp.start()
        local, cps = [], []
        for w in range(nw):
            loads[w].wait()
            local.append(pltpu.make_async_copy(stage[w], outs[w].at[c], local_sems.at[w]))
            cps.append(pltpu.make_async_remote_copy(src_ref=stage[w], dst_ref=outs[w].at[c], send_sem=send_sems.at[w],
                                                    recv_sem=recv_sems.at[w], device_id=(x, y, 1 - c), device_id_type=MESH))
            local[w].start()
            cps[w].start()
        for w in range(nw):
            pltpu.make_async_remote_copy(src_ref=stage[w], dst_ref=outs[w].at[1 - c], send_sem=send_sems.at[w], recv_sem=recv_sems.at[w],
                                         device_id=(x, y, 1 - c), device_id_type=MESH).wait_recv()
        for cp in cps:
            cp.wait_send()
        for cp in local:
            cp.wait()

    return pl.pallas_call(
        body, name=name, out_shape=[_sds((2,) + h.shape, f32) for h in halves], in_specs=[HBM] * nw, out_specs=[HBM] * nw,
        scratch_shapes=[pltpu.VMEM(h.shape, f32) for h in halves]
        + [pltpu.SemaphoreType.DMA((nw,)), pltpu.SemaphoreType.DMA((nw,)), pltpu.SemaphoreType.DMA((nw,)), pltpu.SemaphoreType.DMA((nw,))],
        compiler_params=pltpu.CompilerParams(vmem_limit_bytes=VMEM_LIMIT),
    )(*halves)


def _row_tile(r, c, nbuf):
    budget = 24 * 1024 * 1024 // (2 * nbuf * 4 * c)
    fits = [t for t in range(16, r + 1, 16) if r % t == 0 and t <= budget]
    return max(fits) if fits else r


def _cast_bf16(name, a, dep=None):
    r, c = a.shape
    tr = _row_tile(r, c, 2)
    dep_specs, dep_ops = _dep_args(dep, 1)

    def body(a_ref, *rest):
        rest[-1][...] = a_ref[...].astype(bf16)

    spec = pl.BlockSpec((tr, c), lambda i: (i, 0))
    return pl.pallas_call(body, name=name, grid=(r // tr,), in_specs=[spec] + dep_specs, out_specs=spec, out_shape=_sds((r, c), bf16),
                          compiler_params=_params(("parallel",)))(a, *dep_ops)


def _w_in_columns(win4):
    tr = 256

    def body(a_ref, o_ref, ob_ref):
        for k in range(NSH):
            o_ref[:, IN_SH * k:IN_SH * (k + 1)] = a_ref[k][:, :IN_SH]
        o_ref[:, IN_COLS:] = jnp.zeros((tr, IN_P - IN_COLS), bf16)
        ob_ref[...] = o_ref[:, IN_A:]

    return pl.pallas_call(
        body, name="w_in_columns", grid=(D // tr,), in_specs=[pl.BlockSpec((NSH, tr, IN_SHP), lambda i: (0, i, 0))],
        out_specs=[pl.BlockSpec((tr, IN_P), lambda i: (i, 0)), pl.BlockSpec((tr, IN_B), lambda i: (i, 0))],
        out_shape=[_sds((D, IN_P), bf16), _sds((D, IN_B), bf16)], compiler_params=_params(("parallel",)))(win4)


def _pair_sum(name, core, grads, got):
    _, _, rh, c = grads.shape
    tr = _row_tile(rh, c, 2)

    def body(c_ref, a_ref, b_ref, o_ref):
        o_ref[...] = (a_ref[...].astype(f32) + b_ref[...].astype(f32)).astype(bf16)

    spec = pl.BlockSpec((None, tr, c), lambda k, i, c_ref: (k, i, 0))
    return pl.pallas_call(
        body, name=name, out_shape=_sds((NSH, rh, c), bf16),
        grid_spec=pltpu.PrefetchScalarGridSpec(
            num_scalar_prefetch=1, grid=(NSH, rh // tr),
            in_specs=[pl.BlockSpec((None, None, tr, c), lambda k, i, c_ref: (k, c_ref[0], i, 0)), spec], out_specs=spec),
        compiler_params=_params(("parallel", "parallel")))(core, grads, got)


def _chip_sum(name, chip, sums, lands):
    _, rh, c = sums.shape
    tr = _row_tile(rh, c, 4)

    def body(k_ref, own_ref, l_ref, o_ref):
        own = own_ref[...].astype(f32)
        acc = None
        for j in range(NSH):
            term = jnp.where(k_ref[0] == j, own, l_ref[j].astype(f32))
            acc = term if acc is None else acc + term
        o_ref[...] = acc

    return pl.pallas_call(
        body, name=name, out_shape=_sds((rh, c), f32),
        grid_spec=pltpu.PrefetchScalarGridSpec(
            num_scalar_prefetch=1, grid=(rh // tr,),
            in_specs=[pl.BlockSpec((None, tr, c), lambda i, k_ref: (k_ref[0], i, 0)), pl.BlockSpec((NSH, tr, c), lambda i, k_ref: (0, i, 0))],
            out_specs=pl.BlockSpec((tr, c), lambda i, k_ref: (i, 0))),
        compiler_params=_params(("parallel",)))(chip, sums, lands)


def _mods_part(cond16, w_ada, b_part):
    n = w_ada.shape[1]
    tn = 512

    def body(c_ref, w_ref, b_ref, o_ref):
        cv = c_ref[...]
        o_ref[...] = _dot(cv * _sigmoid(cv), w_ref[...]) + b_ref[...]

    return pl.pallas_call(
        body, name="mods_part", grid=(n // tn,),
        in_specs=[pl.BlockSpec((16, D), lambda j: (0, 0)), pl.BlockSpec((D, tn), lambda j: (0, j)), pl.BlockSpec((1, tn), lambda j: (0, j))],
        out_specs=pl.BlockSpec((16, tn), lambda j: (0, j)), out_shape=_sds((16, n), f32), compiler_params=_params(("parallel",)),
    )(cond16, w_ada, b_part)


def _grad_w_ada(cond16, dm16):
    n = dm16.shape[1]
    tr = 256

    def body(c_ref, d_ref, o_ref):
        cv = c_ref[...]
        o_ref[...] = _dot(cv * _sigmoid(cv), d_ref[...], ta=True)

    return pl.pallas_call(
        body, name="grad_w_ada", grid=(D // tr,),
        in_specs=[pl.BlockSpec((16, tr), lambda i: (0, i)), pl.BlockSpec((16, n), lambda i: (0, 0))],
        out_specs=pl.BlockSpec((tr, n), lambda i: (i, 0)), out_shape=_sds((D, n), f32), compiler_params=_params(("parallel",)),
    )(cond16, dm16)


def _adamw(name, w, g, m, v):
    r, c = w.shape
    tr = _row_tile(r, c, 7)
    spec = pl.BlockSpec((tr, c), lambda i: (i, 0))
    grid = (r // tr,)

    def body(w_ref, g_ref, m_ref, v_ref, d_ref, nm_ref, nv_ref):
        gv = g_ref[...]
        nm = ADAM_B1 * m_ref[...] + (1.0 - ADAM_B1) * gv
        nv = ADAM_B2 * v_ref[...] + (1.0 - ADAM_B2) * (gv * gv)
        nm_ref[...] = nm
        nv_ref[...] = nv
        m_hat = nm / (1.0 - ADAM_B1 ** ADAM_STEP)
        v_hat = nv / (1.0 - ADAM_B2 ** ADAM_STEP)
        d_ref[...] = -ADAM_LR * (m_hat / (jnp.sqrt(v_hat) + ADAM_EPS) + ADAM_WD * w_ref[...])

    return pl.pallas_call(body, name=name, grid=grid, in_specs=[spec] * 4, out_specs=[spec] * 3, out_shape=[_sds(w.shape, f32)] * 3,
                          compiler_params=_params(("parallel",)))(w, g, m, v)


def _pack(parts, rows):
    flat = []
    for p in parts:
        p = p.reshape(-1)
        flat.append(jnp.pad(p, (0, (-p.shape[0]) % 128)))
    v = jnp.concatenate(flat)
    return jnp.pad(v, (0, rows * 128 - v.shape[0])).reshape(rows, 128)


def _unpack(packed, sizes):
    lead = packed.shape[:-2]
    flat = packed.reshape(lead + (-1,))
    out, off = [], 0
    for n in sizes:
        out.append(flat[..., off:off + n])
        off += n + (-n) % 128
    return out


BIG = ("w_in", "w_out", "w_gate", "w_up", "w_down")
SMALL = ("b_ada", "g_mix", "conv_b", "dt_bias", "a_log", "d_skip", "g_att_out", "g_ssd_out", "g_ffn", "g_final", "rel_bias", "conv_w")
ORDER = ("w_ada", "b_ada", "g_mix", "w_in", "rel_bias", "conv_w", "conv_b", "dt_bias", "a_log", "d_skip", "g_att_out", "g_ssd_out",
         "w_out", "g_ffn", "w_gate", "w_up", "w_down", "g_final")
REL_SH = N_REL // NSH
CONVW_SH = XBC // NSH
ADA_SH = 6 * D // NSH


class _Exchange:
    def __init__(self, core, chip):
        self.core, self.chip = core, chip
        self.gathered = {}
        self.pending = []

    def gather(self, names, shards):
        ssem, rsem, thru, lands, token = _split_start("gather_start_" + "_".join(names), _gather_copies, shards,
                                                      [(NSH,) + s.shape for s in shards])
        self.gathered.update({n: (ssem[i], rsem[i], thru[i], lands[i]) for i, n in enumerate(names)})
        return token

    def _whole(self, names, after):
        ssem, rsem, thru, lands = zip(*[self.gathered[n] for n in names])
        tag = "_".join(names)
        thru, lands = _split_wait("gather_wait_" + tag, _gather_copies, ssem, rsem, thru, lands, after)
        return _gather_forward("gather_forward_" + tag, thru, lands)

    def w_in(self, after):
        (win4,) = self._whole(("w_in",), after)
        return _w_in_columns(win4.reshape(NSH, D, IN_SHP))

    def w_out(self, after):
        (wout4,) = self._whole(("w_out",), after)
        return wout4.reshape(D, D)

    def ffn(self, after):
        wg4, wu4, wd4 = self._whole(("w_gate", "w_up", "w_down"), after)
        return wg4.reshape(NSH, D, FSH), wu4.reshape(NSH, D, FSH), wd4.reshape(NSH, FSH, D)

    def grad(self, names, grads):
        tag = "_".join(names)
        stacked = [g.reshape(NSH, 2, g.shape[1] // 2, g.shape[2]) for g in grads]
        got = _rs_pair_exchange("rs_pair_exchange_" + tag, stacked)
        sums = [_pair_sum("pair_sum_" + n, self.core, o, g) for n, o, g in zip(names, stacked, got)]
        self.pending.append((names, _split_start("rs_start_" + tag, _reduce_copies, sums, [s.shape for s in sums])))
        return self.pending[-1][1][4]

    def finish(self, after):
        grads = {}
        for names, (ssem, rsem, sums, lands, _) in self.pending:
            tag = "_".join(names)
            sums, lands = _split_wait("rs_wait_" + tag, _reduce_copies, ssem, rsem, sums, lands, after)
            halves = [_chip_sum("chip_sum_" + n, self.chip, sm, ld) for n, sm, ld in zip(names, sums, lands)]
            for n, f in zip(names, _rs_pair_gather("rs_pair_gather_" + tag, halves)):
                grads[n] = f.reshape(2 * f.shape[1], f.shape[2])
        return grads


def kernel(x, c, w_ada, b_ada, g_mix, w_in, rel_bias, conv_w, conv_b, dt_bias, a_log, d_skip, g_att_out, g_ssd_out, w_out, g_ffn, w_gate, w_up, w_down, g_final, loss_target, m_w_ada, m_b_ada, m_g_mix, m_w_in, m_rel_bias, m_conv_w, m_conv_b, m_dt_bias, m_a_log, m_d_skip, m_g_att_out, m_g_ssd_out, m_w_out, m_g_ffn, m_w_gate, m_w_up, m_w_down, m_g_final, v_w_ada, v_b_ada, v_g_mix, v_w_in, v_rel_bias, v_conv_w, v_conv_b, v_dt_bias, v_a_log, v_d_skip, v_g_att_out, v_g_ssd_out, v_w_out, v_g_ffn, v_w_gate, v_w_up, v_w_down, v_g_final):
    args = dict(locals())
    w = {n: args[n] for n in ORDER}
    m = {n: args["m_" + n] for n in ORDER}
    v = {n: args["v_" + n] for n in ORDER}
    ix, iy, ic = lax.axis_index("x"), lax.axis_index("y"), lax.axis_index("c")
    chip = 2 * ix + iy
    dev = 2 * chip + ic
    s = x.shape[1]

    g1 = _allgather8("gather_inputs", _pack([c[0], rel_bias[0], conv_w[0]], 40))
    c_all, rel_sh, convw_sh = _unpack(g1, [D, NH * REL_SH, 4 * CONVW_SH])
    rel_full = jnp.concatenate([rel_sh[2 * k].reshape(NH, REL_SH) for k in range(NSH)], axis=1)
    convw_full = jnp.concatenate([convw_sh[2 * k].reshape(4, CONVW_SH) for k in range(NSH)], axis=1)
    cond16 = jnp.pad(c_all, ((0, 8), (0, 0)))
    b_part = lax.dynamic_slice_in_dim(b_ada, chip * ADA_SH, ADA_SH, axis=1)
    mods_part = _mods_part(cond16, w_ada[0], b_part)[:N_DEV]
    g2 = _allgather8("gather_mods", mods_part.reshape(N_DEV * ADA_SH // 128, 128))
    mods_all = jnp.concatenate([g2[2 * k].reshape(N_DEV, ADA_SH) for k in range(NSH)], axis=1)
    mods = lax.dynamic_slice_in_dim(mods_all, dev, 1, axis=0)

    exchange = _Exchange(jnp.reshape(ic, (1,)).astype(jnp.int32), jnp.reshape(chip, (1,)).astype(jnp.int32))
    shard_in = _cast_bf16("cast_w_in", jnp.pad(w_in[0], ((0, 0), (0, IN_SHP - IN_SH))), dep=g2[0, :8]).reshape(2, D // 2, IN_SHP)
    tok = exchange.gather(("w_in",), [shard_in])
    tok = exchange.gather(("w_out", "w_gate", "w_up", "w_down"), [
        _cast_bf16("cast_w_out", w_out[0], dep=tok).reshape(2, D // NSH // 2, D),
        _cast_bf16("cast_w_gate", w_gate[0], dep=tok).reshape(2, D // 2, FSH),
        _cast_bf16("cast_w_up", w_up[0], dep=tok).reshape(2, D // 2, FSH),
        _cast_bf16("cast_w_down", w_down[0], dep=tok).reshape(2, FSH // 2, D)])
    mods = mods + tok[:1, :1]

    loss, grad_x, dmods, small = _local_step(
        x[0], loss_target[0], mods, g_mix, rel_full, convw_full, conv_b, dt_bias, a_log, d_skip, g_att_out, g_ssd_out, g_ffn,
        g_final[None, :], exchange)

    small_names = ("g_mix", "conv_b", "dt_bias", "a_log", "d_skip", "g_att_out", "g_ssd_out", "g_ffn", "g_final", "rel_bias", "conv_w")
    g3 = _allgather8("gather_small_grads", _pack([dmods] + [small[n] for n in small_names], 264))
    sizes = [6 * D] + [int(np.prod(small[n].shape)) for n in small_names]
    dmods_all = _unpack(g3, sizes)[0]
    summed = _unpack(_sum8(g3), sizes)
    grads = {"b_ada": summed[0].reshape(1, 6 * D)}
    for n, val in zip(small_names, summed[1:]):
        grads[n] = val.reshape(small[n].shape)
    grads["rel_bias"] = lax.dynamic_slice_in_dim(grads["rel_bias"], chip * REL_SH, REL_SH, axis=1)
    grads["conv_w"] = lax.dynamic_slice_in_dim(grads["conv_w"], chip * CONVW_SH, CONVW_SH, axis=1)
    grads["g_final"] = grads["g_final"].reshape(D)
    dm16 = jnp.pad(lax.dynamic_slice_in_dim(dmods_all, chip * ADA_SH, ADA_SH, axis=1), ((0, 8), (0, 0)))
    grads["w_ada"] = _grad_w_ada(cond16, dm16)

    delta, new_m, new_v = {}, {}, {}
    delta["w_ada"], new_m["w_ada"], new_v["w_ada"] = _adamw("adamw_w_ada", w_ada[0], grads["w_ada"], m_w_ada[0], v_w_ada[0])
    grads.update(exchange.finish(grad_x))
    grads["w_in"] = grads["w_in"][:, :IN_SH]
    for n in BIG:
        delta[n], new_m[n], new_v[n] = _adamw("adamw_" + n, w[n][0], grads[n], m[n][0], v[n][0])
    sw = _pack([w[n] for n in SMALL], 200)
    sg = _pack([grads[n] for n in SMALL], 200)
    sm = _pack([m[n] for n in SMALL], 200)
    sv = _pack([v[n] for n in SMALL], 200)
    ssz = [int(np.prod(w[n].shape)) for n in SMALL]
    for dst, packed in zip((delta, new_m, new_v), _adamw("adamw_small", sw, sg, sm, sv)):
        for n, val in zip(SMALL, _unpack(packed, ssz)):
            dst[n] = val

    def shaped(d, n):
        return d[n].reshape(w[n].shape)

    total = lax.psum(loss, ("x", "y", "c"))
    return (total, grad_x[None], *[shaped(grads, n) for n in ORDER], *[shaped(delta, n) for n in ORDER],
            *[shaped(new_m, n) for n in ORDER], *[shaped(new_v, n) for n in ORDER])
```

```python
import functools

import numpy as np
import jax
import jax.numpy as jnp
from jax import lax
from jax.experimental import pallas as pl
from jax.experimental.pallas import tpu as pltpu

f32 = jnp.float32
bf16 = jnp.bfloat16
HIGHEST = lax.Precision.HIGHEST
MESH = pl.DeviceIdType.MESH

D = 2048
CHUNK = 64
LEFT = 8
BAND = (LEFT + 1) * CHUNK
BANDP = 640
PADK = LEFT * CHUNK
NH = 16
HD = 64
ATT_W = NH * HD
SSD_W = 1024
NG = 2
NSTATE = 128
GW = SSD_W // NG
XBC = SSD_W + 2 * NG * NSTATE
N_REL = 320
REL_CLIP = 256
FFN = 5632
NSH = 4
FSH = FFN // NSH
IN_COLS = 5648
IN_SH = IN_COLS // NSH
IN_SHP = 1536
IN_A = 3 * ATT_W
IN_B = 2688
IN_P = IN_A + IN_B
EPS = 1e-6
N_DEV = 8

ADAM_LR = 0.001
ADAM_B1 = 0.9
ADAM_B2 = 0.999
ADAM_EPS = 1e-08
ADAM_WD = 0.01
ADAM_STEP = 10

VMEM_LIMIT = 56 * 1024 * 1024


def _params(sem):
    return pltpu.CompilerParams(dimension_semantics=sem, vmem_limit_bytes=VMEM_LIMIT)


def _sds(shape, dtype):
    return jax.ShapeDtypeStruct(shape, dtype)


def _fold8(v):
    r, w = v.shape
    return jnp.sum(v.reshape(r // 8, 8, w), axis=0)


STRIP = 16


def _strips(tm, fn):
    def step(j, carry):
        fn(pl.ds(pl.multiple_of(j * STRIP, STRIP), STRIP))
        return carry
    lax.fori_loop(0, tm // STRIP, step, 0, unroll=4)


def _sigmoid(v):
    return 1.0 / (1.0 + jnp.exp(-v))


def _softplus(v):
    return jnp.maximum(v, 0.0) + jnp.log(1.0 + jnp.exp(-jnp.abs(v)))


def _dot(a, b, ta=False, tb=False):
    dn = (((0 if ta else 1,), (1 if tb else 0,)), ((), ()))
    return lax.dot_general(a.astype(bf16), b.astype(bf16), dn, preferred_element_type=f32)


def _dep_args(dep, ngrid):
    if dep is None:
        return [], []
    return [pl.BlockSpec((8, 128), lambda *_: (0, 0))], [dep]


def _dot01(a, b, ta=False, tb=False, exact="b"):
    dn = (((0 if ta else 1,), (1 if tb else 0,)), ((), ()))
    x = a if exact == "b" else b
    hi = x.astype(bf16)
    r = x - hi.astype(f32)
    mid = r.astype(bf16)
    lo = (r - mid.astype(f32)).astype(bf16)
    if exact == "b":
        m = b.astype(bf16)
        return sum(lax.dot_general(p, m, dn, preferred_element_type=f32) for p in (hi, mid, lo))
    m = a.astype(bf16)
    return sum(lax.dot_general(m, p, dn, preferred_element_type=f32) for p in (hi, mid, lo))


def _matmul(name, a, b, *, grid, a_spec, b_spec, o_spec, o_shape, o_dtype, acc_shape, ta=False, tb=False, dep=None):
    nk = grid[2]
    dep_specs, dep_ops = _dep_args(dep, 3)

    def body(a_ref, b_ref, *rest):
        o_ref, acc_ref = rest[-2:]
        p = _dot(a_ref[...], b_ref[...], ta, tb)
        if nk == 1:
            o_ref[...] = p.astype(o_ref.dtype)
        else:
            k = pl.program_id(2)

            @pl.when(k == 0)
            def _():
                acc_ref[...] = p

            @pl.when(jnp.logical_and(k > 0, k < nk - 1))
            def _():
                acc_ref[...] += p

            @pl.when(k == nk - 1)
            def _():
                o_ref[...] = (acc_ref[...] + p).astype(o_ref.dtype)

    return pl.pallas_call(
        body, name=name, grid=grid, in_specs=[a_spec, b_spec] + dep_specs, out_specs=o_spec,
        out_shape=_sds(o_shape, o_dtype), scratch_shapes=[pltpu.VMEM(acc_shape if nk > 1 else (8, 128), f32)],
        compiler_params=_params(("parallel", "parallel", "arbitrary")),
    )(a, b, *dep_ops)


def _mm_nn_fullk(name, a, b, tm, tn, o_dtype, n=None):
    m, k = a.shape
    n = b.shape[1] if n is None else n
    return _matmul(name, a, b, grid=(m // tm, n // tn, 1),
                   a_spec=pl.BlockSpec((tm, k), lambda i, j, kk: (i, 0)),
                   b_spec=pl.BlockSpec((k, tn), lambda i, j, kk: (0, j)),
                   o_spec=pl.BlockSpec((tm, tn), lambda i, j, kk: (i, j)),
                   o_shape=(m, n), o_dtype=o_dtype, acc_shape=(tm, tn))


def _mm_nt(name, a, b, tm, tn, tk, o_dtype, dep=None):
    m, k = a.shape
    n = b.shape[0]
    return _matmul(name, a, b, grid=(m // tm, n // tn, k // tk), tb=True, dep=dep,
                   a_spec=pl.BlockSpec((tm, tk), lambda i, j, kk: (i, kk)),
                   b_spec=pl.BlockSpec((tn, tk), lambda i, j, kk: (j, kk)),
                   o_spec=pl.BlockSpec((tm, tn), lambda i, j, kk: (i, j)),
                   o_shape=(m, n), o_dtype=o_dtype, acc_shape=(tm, tn))


def _mm_tn(name, a, b, tm, tn, tk, o_dtype):
    k, m = a.shape
    n = b.shape[1]
    return _matmul(name, a, b, grid=(m // tm, n // tn, k // tk), ta=True,
                   a_spec=pl.BlockSpec((tk, tm), lambda i, j, kk: (kk, i)),
                   b_spec=pl.BlockSpec((tk, tn), lambda i, j, kk: (kk, j)),
                   o_spec=pl.BlockSpec((tm, tn), lambda i, j, kk: (i, j)),
                   o_shape=(m, n), o_dtype=o_dtype, acc_shape=(tm, tn))


FSH_PARTS = (slice(0, 640), slice(640, FSH))


def _ffn_up(h2b, wg4, wu4, tm):
    s = h2b.shape[0]

    def body(h_ref, wg_ref, wu_ref, a_ref, s_ref, ud_ref):
        h = h_ref[...]
        for cols in FSH_PARTS:
            g = _dot(h, wg_ref[:, cols])
            u = _dot(h, wu_ref[:, cols])
            sg = _sigmoid(g)
            sil = g * sg
            a_ref[:, cols] = (sil * u).astype(bf16)
            s_ref[:, cols] = sil.astype(bf16)
            ud_ref[:, cols] = (u * (sg * (1.0 + g * (1.0 - sg)))).astype(bf16)

    wspec = pl.BlockSpec((None, D, FSH), lambda k, i: (k, 0, 0))
    ospec = pl.BlockSpec((tm, FSH), lambda k, i: (i, k))
    return pl.pallas_call(
        body, name="ffn_up", grid=(NSH, s // tm),
        in_specs=[pl.BlockSpec((tm, D), lambda k, i: (i, 0)), wspec, wspec],
        out_specs=[ospec, ospec, ospec], out_shape=[_sds((s, FFN), bf16)] * 3,
        compiler_params=_params(("parallel", "parallel")),
    )(h2b, wg4, wu4)


def _ffn_down(act, wd4, tm):
    s = act.shape[0]
    tn = D // 4

    def body(a_ref, b_ref, o_ref):
        o_ref[...] = jnp.dot(a_ref[...], b_ref[...].reshape(FFN, tn), preferred_element_type=f32)

    return pl.pallas_call(
        body, name="ffn_down", grid=(s // tm, D // tn),
        in_specs=[pl.BlockSpec((tm, FFN), lambda i, j: (i, 0)), pl.BlockSpec((NSH, FSH, tn), lambda i, j: (0, 0, j))],
        out_specs=pl.BlockSpec((tm, tn), lambda i, j: (i, j)), out_shape=_sds((s, D), f32),
        compiler_params=_params(("parallel", "parallel")),
    )(act, wd4)


def _ffn_dact(dffn, wd4, sil, ud, tm, dep=None):
    s = dffn.shape[0]
    dep_specs, dep_ops = _dep_args(dep, 2)

    def body(d_ref, w_ref, s_ref, ud_ref, *rest):
        dg_ref, du_ref = rest[-2:]
        d = d_ref[...]
        for cols in FSH_PARTS:
            dact = _dot(d, w_ref[cols, :], tb=True)
            dg_ref[:, cols] = (dact * ud_ref[:, cols].astype(f32)).astype(bf16)
            du_ref[:, cols] = (dact * s_ref[:, cols].astype(f32)).astype(bf16)

    blk = pl.BlockSpec((tm, FSH), lambda k, i: (i, k))
    return pl.pallas_call(
        body, name="ffn_dact", grid=(NSH, s // tm),
        in_specs=[pl.BlockSpec((tm, D), lambda k, i: (i, 0)), pl.BlockSpec((None, FSH, D), lambda k, i: (k, 0, 0)), blk, blk] + dep_specs,
        out_specs=[blk, blk], out_shape=[_sds((s, FFN), bf16), _sds((s, FFN), bf16)],
        compiler_params=_params(("parallel", "parallel")),
    )(dffn, wd4, sil, ud, *dep_ops)


def _ffn_dh(dgate, dup, wg4, wu4, tm, dep=None):
    s = dgate.shape[0]
    dep_specs, dep_ops = _dep_args(dep, 2)

    def body(dg_ref, du_ref, wg_ref, wu_ref, *rest):
        o_ref, acc_ref = rest[-2:]
        k = pl.program_id(1)
        p = _dot(dg_ref[...], wg_ref[...], tb=True) + _dot(du_ref[...], wu_ref[...], tb=True)

        @pl.when(k == 0)
        def _():
            acc_ref[...] = p

        @pl.when(jnp.logical_and(k > 0, k < NSH - 1))
        def _():
            acc_ref[...] += p

        @pl.when(k == NSH - 1)
        def _():
            o_ref[...] = acc_ref[...] + p

    aspec = pl.BlockSpec((tm, FSH), lambda i, k: (i, k))
    wspec = pl.BlockSpec((None, D, FSH), lambda i, k: (k, 0, 0))
    return pl.pallas_call(
        body, name="ffn_dh", grid=(s // tm, NSH), in_specs=[aspec, aspec, wspec, wspec] + dep_specs,
        out_specs=pl.BlockSpec((tm, D), lambda i, k: (i, 0)), out_shape=_sds((s, D), f32),
        scratch_shapes=[pltpu.VMEM((tm, D), f32)], compiler_params=_params(("parallel", "arbitrary")),
    )(dgate, dup, wg4, wu4, *dep_ops)


def _grad_cols4(name, h, dy, tm, tk):
    s = h.shape[0]
    return _matmul(name, h, dy, grid=(NSH, D // tm, s // tk), ta=True,
                   a_spec=pl.BlockSpec((tk, tm), lambda k, i, kk: (kk, i)),
                   b_spec=pl.BlockSpec((tk, FSH), lambda k, i, kk: (kk, k)),
                   o_spec=pl.BlockSpec((None, tm, FSH), lambda k, i, kk: (k, i, 0)),
                   o_shape=(NSH, D, FSH), o_dtype=bf16, acc_shape=(tm, FSH))


def _grad_wdown4(act, dffn, tn, tk):
    s = act.shape[0]
    return _matmul("grad_w_down", act, dffn, grid=(NSH, D // tn, s // tk), ta=True,
                   a_spec=pl.BlockSpec((tk, FSH), lambda k, j, kk: (kk, k)),
                   b_spec=pl.BlockSpec((tk, tn), lambda k, j, kk: (kk, j)),
                   o_spec=pl.BlockSpec((None, FSH, tn), lambda k, j, kk: (k, 0, j)),
                   o_shape=(NSH, FSH, D), o_dtype=bf16, acc_shape=(FSH, tn))


def _row_spec(w):
    return pl.BlockSpec((1, w), lambda i: (0, 0))


def _tile_spec(tm, w, col=0):
    return pl.BlockSpec((tm, w), lambda i: (i, col))


def _norm_mod(name, x, g, sc, sh, tm):
    s = x.shape[0]

    def body(x_ref, g_ref, sc_ref, sh_ref, o_ref):
        def strip(rows):
            xv = x_ref[rows, :]
            r = lax.rsqrt(jnp.mean(xv * xv, axis=-1, keepdims=True) + EPS)
            o_ref[rows, :] = (xv * r * g_ref[...] * (1.0 + sc_ref[...]) + sh_ref[...]).astype(bf16)

        _strips(tm, strip)

    return pl.pallas_call(
        body, name=name, grid=(s // tm,), in_specs=[_tile_spec(tm, D), _row_spec(D), _row_spec(D), _row_spec(D)],
        out_specs=_tile_spec(tm, D), out_shape=_sds((s, D), bf16), compiler_params=_params(("parallel",)),
    )(x, g, sc, sh)


def _resid_norm_mod(x, gt, mix, g, sc, sh, tm):
    s = x.shape[0]

    def body(x_ref, gt_ref, m_ref, g_ref, sc_ref, sh_ref, x2_ref, h_ref):
        def strip(rows):
            xv = x_ref[rows, :] + gt_ref[...] * m_ref[rows, :]
            x2_ref[rows, :] = xv
            r = lax.rsqrt(jnp.mean(xv * xv, axis=-1, keepdims=True) + EPS)
            h_ref[rows, :] = (xv * r * g_ref[...] * (1.0 + sc_ref[...]) + sh_ref[...]).astype(bf16)

        _strips(tm, strip)

    return pl.pallas_call(
        body, name="resid_norm_mod", grid=(s // tm,),
        in_specs=[_tile_spec(tm, D), _row_spec(D), _tile_spec(tm, D), _row_spec(D), _row_spec(D), _row_spec(D)],
        out_specs=[_tile_spec(tm, D), _tile_spec(tm, D)], out_shape=[_sds((s, D), f32), _sds((s, D), bf16)],
        compiler_params=_params(("parallel",)),
    )(x, gt, mix, g, sc, sh)


def _final_fwd_bwd(x2, ffn, gt2, g, tgt, tm):
    s = x2.shape[0]
    n = s // tm

    def body(x_ref, f_ref, gt_ref, g_ref, t_ref, dx_ref, df_ref, loss_ref, dg_ref, dgt_ref, a_loss, a_dg, a_dgt):
        i = pl.program_id(0)

        @pl.when(i == 0)
        def _():
            a_loss[...] = jnp.zeros_like(a_loss)
            a_dg[...] = jnp.zeros_like(a_dg)
            a_dgt[...] = jnp.zeros_like(a_dgt)

        def strip(rows):
            fv = f_ref[rows, :]
            gt = gt_ref[...]
            gv = g_ref[...]
            xv = x_ref[rows, :] + gt * fv
            r = lax.rsqrt(jnp.mean(xv * xv, axis=-1, keepdims=True) + EPS)
            xh = xv * r
            e = xh * gv - t_ref[rows, :]
            a_loss[...] += _fold8(e * e)
            dy = e * (1.0 / D)
            a_dg[...] += _fold8(dy * xh)
            t = dy * gv
            dx = r * (t - xh * jnp.mean(t * xh, axis=-1, keepdims=True))
            dx_ref[rows, :] = dx
            a_dgt[...] += _fold8(dx * fv)
            df_ref[rows, :] = (dx * gt).astype(bf16)

        _strips(tm, strip)

        @pl.when(i == n - 1)
        def _():
            tot = jnp.sum(jnp.sum(a_loss[...], axis=0, keepdims=True), axis=1, keepdims=True) * (0.5 / D)
            loss_ref[...] = jnp.broadcast_to(tot, (1, 128))
            dg_ref[...] = jnp.sum(a_dg[...], axis=0, keepdims=True)
            dgt_ref[...] = jnp.sum(a_dgt[...], axis=0, keepdims=True)

    return pl.pallas_call(
        body, name="final_fwd_bwd", grid=(n,),
        in_specs=[_tile_spec(tm, D), _tile_spec(tm, D), _row_spec(D), _row_spec(D), _tile_spec(tm, D)],
        out_specs=[_tile_spec(tm, D), _tile_spec(tm, D), _row_spec(128), _row_spec(D), _row_spec(D)],
        out_shape=[_sds((s, D), f32), _sds((s, D), bf16), _sds((1, 128), f32), _sds((1, D), f32), _sds((1, D), f32)],
        scratch_shapes=[pltpu.VMEM((8, D), f32)] * 3, compiler_params=_params(("arbitrary",)),
    )(x2, ffn, gt2, g, tgt)


def _norm_mod_bwd(name, dh, xin, g, sc, dres, tm, mix=None, gt=None):
    s = dh.shape[0]
    n = s // tm
    with_mix = mix is not None

    def body(*refs):
        if with_mix:
            dh_ref, x_ref, g_ref, sc_ref, dr_ref, m_ref, gt_ref, dx_ref, dm_ref, dsc_ref, dsh_ref, dg_ref, dgt_ref, a_sc, a_sh, a_g, a_gt = refs
        else:
            dh_ref, x_ref, g_ref, sc_ref, dr_ref, dx_ref, dsc_ref, dsh_ref, dg_ref, a_sc, a_sh, a_g = refs
        i = pl.program_id(0)

        @pl.when(i == 0)
        def _():
            a_sc[...] = jnp.zeros_like(a_sc)
            a_sh[...] = jnp.zeros_like(a_sh)
            a_g[...] = jnp.zeros_like(a_g)
            if with_mix:
                a_gt[...] = jnp.zeros_like(a_gt)

        def strip(rows):
            dh = dh_ref[rows, :]
            xv = x_ref[rows, :]
            gv = g_ref[...]
            r = lax.rsqrt(jnp.mean(xv * xv, axis=-1, keepdims=True) + EPS)
            xh = xv * r
            a_sc[...] += _fold8(dh * xh * gv)
            a_sh[...] += _fold8(dh)
            dn = dh * (1.0 + sc_ref[...])
            a_g[...] += _fold8(dn * xh)
            t = dn * gv
            dx = dr_ref[rows, :] + r * (t - xh * jnp.mean(t * xh, axis=-1, keepdims=True))
            dx_ref[rows, :] = dx
            if with_mix:
                a_gt[...] += _fold8(dx * m_ref[rows, :])
                dm_ref[rows, :] = (dx * gt_ref[...]).astype(bf16)

        _strips(tm, strip)

        @pl.when(i == n - 1)
        def _():
            dsc_ref[...] = jnp.sum(a_sc[...], axis=0, keepdims=True)
            dsh_ref[...] = jnp.sum(a_sh[...], axis=0, keepdims=True)
            dg_ref[...] = jnp.sum(a_g[...], axis=0, keepdims=True)
            if with_mix:
                dgt_ref[...] = jnp.sum(a_gt[...], axis=0, keepdims=True)

    tile, row = _tile_spec(tm, D), _row_spec(D)
    if with_mix:
        ins, args = [tile, tile, row, row, tile, tile, row], (dh, xin, g, sc, dres, mix, gt)
        outs = [tile, tile, row, row, row, row]
        shapes = [_sds((s, D), f32), _sds((s, D), bf16)] + [_sds((1, D), f32)] * 4
        nacc = 4
    else:
        ins, args = [tile, tile, row, row, tile], (dh, xin, g, sc, dres)
        outs = [tile, row, row, row]
        shapes = [_sds((s, D), f32)] + [_sds((1, D), f32)] * 3
        nacc = 3
    return pl.pallas_call(
        body, name=name, grid=(n,), in_specs=ins, out_specs=outs, out_shape=shapes,
        scratch_shapes=[pltpu.VMEM((8, D), f32)] * nacc, compiler_params=_params(("arbitrary",)),
    )(*args)


def _mix_pre(att, y, proj2, g_att, g_ssd, tm):
    s = att.shape[0]

    def body(a_ref, y_ref, z_ref, ga_ref, gs_ref, o_ref):
        def strip(rows):
            a = a_ref[rows, :]
            ra = lax.rsqrt(jnp.mean(a * a, axis=-1, keepdims=True) + EPS)
            o_ref[rows, 0:ATT_W] = (a * ra * ga_ref[...]).astype(bf16)
            z = z_ref[rows, :]
            u = y_ref[rows, :] * (z * _sigmoid(z))
            ru = lax.rsqrt(jnp.mean(u * u, axis=-1, keepdims=True) + EPS)
            o_ref[rows, ATT_W:] = (u * ru * gs_ref[...]).astype(bf16)

        _strips(tm, strip)

    t = _tile_spec(tm, ATT_W)
    return pl.pallas_call(
        body, name="mix_pre", grid=(s // tm,), in_specs=[t, t, t, _row_spec(ATT_W), _row_spec(SSD_W)],
        out_specs=_tile_spec(tm, D), out_shape=_sds((s, D), bf16), compiler_params=_params(("parallel",)),
    )(att, y, proj2, g_att, g_ssd)


def _mix_pre_bwd(dmc, att, y, proj2, g_att, g_ssd, tm):
    s = att.shape[0]
    n = s // tm

    def body(da_ref, ds_ref, a_ref, y_ref, z_ref, ga_ref, gs_ref, datt_ref, dy_ref, dz_ref, dga_ref, dgs_ref, acc_a, acc_s):
        i = pl.program_id(0)

        @pl.when(i == 0)
        def _():
            acc_a[...] = jnp.zeros_like(acc_a)
            acc_s[...] = jnp.zeros_like(acc_s)

        def strip(rows):
            a = a_ref[rows, :]
            ra = lax.rsqrt(jnp.mean(a * a, axis=-1, keepdims=True) + EPS)
            ah = a * ra
            dan = da_ref[rows, :]
            acc_a[...] += _fold8(dan * ah)
            t = dan * ga_ref[...]
            datt_ref[rows, :] = (ra * (t - ah * jnp.mean(t * ah, axis=-1, keepdims=True))).astype(bf16)
            z = z_ref[rows, :]
            yv = y_ref[rows, :]
            sz = _sigmoid(z)
            sil = z * sz
            u = yv * sil
            ru = lax.rsqrt(jnp.mean(u * u, axis=-1, keepdims=True) + EPS)
            uh = u * ru
            dsn = ds_ref[rows, :]
            acc_s[...] += _fold8(dsn * uh)
            t2 = dsn * gs_ref[...]
            du = ru * (t2 - uh * jnp.mean(t2 * uh, axis=-1, keepdims=True))
            dy_ref[rows, :] = du * sil
            dz_ref[rows, :] = (du * yv * (sz * (1.0 + z * (1.0 - sz)))).astype(bf16)

        _strips(tm, strip)

        @pl.when(i == n - 1)
        def _():
            dga_ref[...] = jnp.sum(acc_a[...], axis=0, keepdims=True)
            dgs_ref[...] = jnp.sum(acc_s[...], axis=0, keepdims=True)

    t = _tile_spec(tm, ATT_W)
    row = _row_spec(ATT_W)
    return pl.pallas_call(
        body, name="mix_pre_bwd", grid=(n,),
        in_specs=[_tile_spec(tm, ATT_W, 0), _tile_spec(tm, ATT_W, 1), t, t, t, row, row],
        out_specs=[t, t, t, row, row],
        out_shape=[_sds((s, ATT_W), bf16), _sds((s, SSD_W), f32), _sds((s, SSD_W), bf16), _sds((1, ATT_W), f32), _sds((1, SSD_W), f32)],
        scratch_shapes=[pltpu.VMEM((8, ATT_W), f32)] * 2, compiler_params=_params(("arbitrary",)),
    )(dmc, dmc, att, y, proj2, g_att, g_ssd)


ATT_GROUP = 8
ATT_GROUP_FWD = 16


def _pair_rows(qc):
    two = jnp.concatenate([qc, qc], axis=0)
    r = lax.broadcasted_iota(jnp.int32, (2 * CHUNK, 128), 0)
    l = lax.broadcasted_iota(jnp.int32, (2 * CHUNK, 128), 1)
    return jnp.where((r < CHUNK) == (l < HD), two, jnp.zeros_like(two))


def _scaled(q):
    return q * jnp.asarray(HD ** -0.5, q.dtype)


def _pair_scores(wt, kb, bias, r0, masked):
    sc = lax.dot_general(wt, kb, (((1,), (1,)), ((), ())), preferred_element_type=f32) + bias
    if not masked:
        return sc
    kidx = lax.broadcasted_iota(jnp.int32, sc.shape, 1)
    return jnp.where(r0 + kidx >= PADK, sc, -jnp.inf)


def _softmax(sc, axis):
    e = jnp.exp(sc - jnp.max(sc, axis=axis, keepdims=True))
    return e * (1.0 / jnp.sum(e, axis=axis, keepdims=True))


def _chunk_loops(nc, group, per_trip):
    n_masked = min(-(-LEFT // per_trip), nc // per_trip)

    def run(masked):
        def step(g, carry):
            group(g, masked)
            return carry
        return step

    lax.fori_loop(0, n_masked, run(True), 0)
    lax.fori_loop(n_masked, nc // per_trip, run(False), 0)


def _pair_diag(r):
    lane = lax.broadcasted_iota(jnp.int32, (CHUNK, 128), 1)
    return jnp.where(lane < HD, r[0:CHUNK], r[CHUNK:])


def _pad_keys(k_ref, kp, s):
    kp[0:PADK, :] = jnp.zeros((PADK, 128), bf16)
    kp[PADK:PADK + s, :] = k_ref[...]
    kp[PADK + s:, :] = jnp.zeros((CHUNK, 128), bf16)


def _attn_fwd(qkv, bias2):
    s = qkv.shape[0]
    nc = s // CHUNK
    npair = NH // 2
    per_trip = min(ATT_GROUP_FWD, nc)

    def body(q_ref, k_ref, v_ref, b_ref, o_ref, kp, vp):
        _pad_keys(k_ref, kp, s)
        _pad_keys(v_ref, vp, s)

        def group(g, masked):
            r0s = [pl.multiple_of((g * per_trip + u) * CHUNK, CHUNK) for u in range(per_trip)]
            scs = [_pair_scores(_pair_rows(_scaled(q_ref[pl.ds(r0, CHUNK), :])), kp[pl.ds(r0, BANDP), :], b_ref[...], r0, masked)
                   for r0 in r0s]
            ps = [_softmax(sc, -1).astype(bf16) for sc in scs]
            for r0, p in zip(r0s, ps):
                o_ref[pl.ds(r0, CHUNK), :] = _pair_diag(jnp.dot(p, vp[pl.ds(r0, BANDP), :], preferred_element_type=f32))

        _chunk_loops(nc, group, per_trip)

    return pl.pallas_call(
        body, name="attn_fwd", grid=(npair,),
        in_specs=[pl.BlockSpec((s, 128), lambda p: (0, p)), pl.BlockSpec((s, 128), lambda p: (0, npair + p)),
                  pl.BlockSpec((s, 128), lambda p: (0, 2 * npair + p)), pl.BlockSpec((None, 2 * CHUNK, BANDP), lambda p: (p, 0, 0))],
        out_specs=pl.BlockSpec((s, 128), lambda p: (0, p)), out_shape=_sds((s, ATT_W), f32),
        scratch_shapes=[pltpu.VMEM((PADK + s + CHUNK, 128), bf16)] * 2, compiler_params=_params(("parallel",)),
    )(qkv, qkv, qkv, bias2)


def _attn_bwd(qkv, datt, bias2):
    s = qkv.shape[0]
    nc = s // CHUNK
    npair = NH // 2
    rows = PADK + s + CHUNK
    nt = (((1,), (1,)), ((), ()))

    def body(q_ref, k_ref, v_ref, do_ref, b_ref, dq_ref, dk_ref, dv_ref, g_ref, kp, vp, dkp, dvp):
        _pad_keys(k_ref, kp, s)
        _pad_keys(v_ref, vp, s)
        dkp[...] = jnp.zeros_like(dkp)
        dvp[...] = jnp.zeros_like(dvp)
        g_ref[...] = jnp.zeros_like(g_ref)

        def group(g, masked):
            r0s = [pl.multiple_of((g * ATT_GROUP + u) * CHUNK, CHUNK) for u in range(ATT_GROUP)]
            wts = [_pair_rows(_scaled(q_ref[pl.ds(r0, CHUNK), :])) for r0 in r0s]
            dos = [_pair_rows(do_ref[pl.ds(r0, CHUNK), :]) for r0 in r0s]
            scs = [_pair_scores(wt, kp[pl.ds(r0, BANDP), :], b_ref[...], r0, masked) for wt, r0 in zip(wts, r0s)]
            dps = [lax.dot_general(do, vp[pl.ds(r0, BANDP), :], nt, preferred_element_type=f32) for do, r0 in zip(dos, r0s)]
            tn_ = (((0,), (0,)), ((), ()))
            for r0, wt, do, sc, dp in zip(r0s, wts, dos, scs, dps):
                p = _softmax(sc, -1)
                ds = p * (dp - jnp.sum(p * dp, axis=-1, keepdims=True))
                g_ref[...] += ds
                dsb = ds.astype(bf16)
                dq = jnp.dot(dsb, kp[pl.ds(r0, BANDP), :], preferred_element_type=f32)
                dq_ref[pl.ds(r0, CHUNK), :] = (_pair_diag(dq) * (HD ** -0.5)).astype(bf16)
                dkp[pl.ds(r0, BANDP), :] += lax.dot_general(dsb, wt, tn_, preferred_element_type=f32)
                dvp[pl.ds(r0, BANDP), :] += lax.dot_general(p.astype(bf16), do, tn_, preferred_element_type=f32)

        _chunk_loops(nc, group, ATT_GROUP)
        dk_ref[...] = dkp[PADK:PADK + s, :].astype(bf16)
        dv_ref[...] = dvp[PADK:PADK + s, :].astype(bf16)

    col = lambda off: pl.BlockSpec((s, 128), lambda p: (0, off + p))
    return pl.pallas_call(
        body, name="attn_bwd", grid=(npair,),
        in_specs=[col(0), col(npair), col(2 * npair), col(0), pl.BlockSpec((None, 2 * CHUNK, BANDP), lambda p: (p, 0, 0))],
        out_specs=[col(0), col(0), col(0), pl.BlockSpec((None, 2 * CHUNK, BANDP), lambda p: (p, 0, 0))],
        out_shape=[_sds((s, ATT_W), bf16)] * 3 + [_sds((npair, 2 * CHUNK, BANDP), f32)],
        scratch_shapes=[pltpu.VMEM((rows, 128), bf16)] * 2 + [pltpu.VMEM((rows, 128), f32)] * 2,
        compiler_params=_params(("parallel",)),
    )(qkv, qkv, qkv, datt, bias2)


def _rel_tables():
    onehot = np.zeros((BANDP, N_REL), np.float32)
    for j in range(BAND + CHUNK - 1):
        o = j - (CHUNK - 1)
        onehot[j, int(np.clip(PADK - o, -(CHUNK - 1), REL_CLIP)) + CHUNK - 1] = 1.0
    return onehot, np.ascontiguousarray(np.eye(CHUNK, dtype=np.float32)[::-1])


def _expand_bias(rel):
    ext = jnp.concatenate([jnp.broadcast_to(rel[:, N_REL - 1:], (NH, N_REL - 1)), rel[:, ::-1],
                           jnp.zeros((NH, BANDP - BAND + 1), f32)], axis=1)
    band = jnp.stack([ext[:, CHUNK - 1 - q:CHUNK - 1 - q + BANDP] for q in range(CHUNK)], axis=1)
    band = jnp.where(np.arange(BANDP) < BAND, band, -jnp.inf)
    return band.reshape(NH // 2, 2 * CHUNK, BANDP)


def _rel_bias_grad(gband):
    def body(g_ref, m_ref, flip_ref, o_ref, d2):
        for h in range(NH):
            rev = jnp.dot(flip_ref[...], g_ref[h], precision=HIGHEST, preferred_element_type=f32)
            rolled = pltpu.roll(rev, 0, 1, stride=1, stride_axis=0)
            d2[h:h + 1, :] = jnp.sum(rolled, axis=0, keepdims=True)
        o_ref[...] = jnp.dot(d2[...], m_ref[...], precision=HIGHEST, preferred_element_type=f32)

    onehot, flip = _rel_tables()
    return pl.pallas_call(
        body, name="rel_bias_grad", out_shape=_sds((NH, N_REL), f32), scratch_shapes=[pltpu.VMEM((NH, BANDP), f32)],
    )(gband, jnp.asarray(onehot), jnp.asarray(flip))


XBC_BLK = 512
XBC_COL0 = SSD_W // XBC_BLK
DT_COL = (SSD_W + XBC) // 128


def _conv_taps(ext, w_ref, b_ref, tm):
    n = ext.shape[0]
    pre = w_ref[3:4, :] * ext + b_ref[...]
    for j in range(3):
        pre = pre + w_ref[j:j + 1, :] * pltpu.roll(ext, 3 - j, 0)
    return pre


def _ssd_conv(proj2, conv_w, conv_b, tm):
    s = proj2.shape[0]
    nb = XBC // XBC_BLK

    def body(x_ref, p_ref, w_ref, b_ref, o_ref):
        i = pl.program_id(1)
        prev = jnp.where(i > 0, p_ref[...], 0.0)
        ext = jnp.concatenate([prev, x_ref[...]], axis=0)
        pre = _conv_taps(ext, w_ref, b_ref, tm)[8:8 + tm]
        o_ref[...] = pre * _sigmoid(pre)

    return pl.pallas_call(
        body, name="ssd_conv", grid=(nb, s // tm),
        in_specs=[pl.BlockSpec((tm, XBC_BLK), lambda j, i: (i, XBC_COL0 + j)),
                  pl.BlockSpec((8, XBC_BLK), lambda j, i: (jnp.maximum(i * (tm // 8) - 1, 0), XBC_COL0 + j)),
                  pl.BlockSpec((4, XBC_BLK), lambda j, i: (0, j)), pl.BlockSpec((1, XBC_BLK), lambda j, i: (0, j))],
        out_specs=pl.BlockSpec((tm, XBC_BLK), lambda j, i: (i, j)), out_shape=_sds((s, XBC), f32),
        compiler_params=_params(("parallel", "parallel")),
    )(proj2, proj2, conv_w, conv_b)


def _ssd_conv_bwd(dxbc, proj2, conv_w, conv_b, tm):
    s = proj2.shape[0]
    nb = XBC // XBC_BLK
    n = s // tm
    last8 = s // 8 - 1

    def body(x_ref, xp_ref, xn_ref, d_ref, dn_ref, w_ref, b_ref, o_ref, dw_ref, db_ref, acc):
        i = pl.program_id(1)

        @pl.when(i == 0)
        def _():
            acc[...] = jnp.zeros_like(acc)

        prev = jnp.where(i > 0, xp_ref[...], 0.0)
        ext = jnp.concatenate([prev, x_ref[...], xn_ref[...]], axis=0)
        pre = _conv_taps(ext, w_ref, b_ref, tm)
        sg = _sigmoid(pre)
        dnext = jnp.where(i < n - 1, dn_ref[...], 0.0)
        dext = jnp.concatenate([jnp.zeros((8, XBC_BLK), f32), d_ref[...], dnext], axis=0)
        dpre = dext * (sg * (1.0 + pre * (1.0 - sg)))
        rows = tm + 16
        dx = w_ref[3:4, :] * dpre
        for j in range(3):
            dx = dx + w_ref[j:j + 1, :] * pltpu.roll(dpre, rows - (3 - j), 0)
        o_ref[...] = dx[8:8 + tm].astype(bf16)
        dcur = dpre[8:8 + tm]
        acc[4] += _fold8(dcur)
        acc[3] += _fold8(dcur * ext[8:8 + tm])
        for j in range(3):
            acc[j] += _fold8(dcur * pltpu.roll(ext, 3 - j, 0)[8:8 + tm])

        @pl.when(i == n - 1)
        def _():
            for j in range(4):
                dw_ref[j:j + 1, :] = jnp.sum(acc[j], axis=0, keepdims=True)
            db_ref[...] = jnp.sum(acc[4], axis=0, keepdims=True)

    xcol = lambda j: XBC_COL0 + j
    return pl.pallas_call(
        body, name="ssd_conv_bwd", grid=(nb, n),
        in_specs=[pl.BlockSpec((tm, XBC_BLK), lambda j, i: (i, xcol(j))),
                  pl.BlockSpec((8, XBC_BLK), lambda j, i: (jnp.maximum(i * (tm // 8) - 1, 0), xcol(j))),
                  pl.BlockSpec((8, XBC_BLK), lambda j, i: (jnp.minimum((i + 1) * (tm // 8), last8), xcol(j))),
                  pl.BlockSpec((tm, XBC_BLK), lambda j, i: (i, j)),
                  pl.BlockSpec((8, XBC_BLK), lambda j, i: (jnp.minimum((i + 1) * (tm // 8), last8), j)),
                  pl.BlockSpec((4, XBC_BLK), lambda j, i: (0, j)), pl.BlockSpec((1, XBC_BLK), lambda j, i: (0, j))],
        out_specs=[pl.BlockSpec((tm, XBC_BLK), lambda j, i: (i, j)), pl.BlockSpec((4, XBC_BLK), lambda j, i: (0, j)),
                   pl.BlockSpec((1, XBC_BLK), lambda j, i: (0, j))],
        out_shape=[_sds((s, XBC), bf16), _sds((4, XBC), f32), _sds((1, XBC), f32)],
        scratch_shapes=[pltpu.VMEM((5, 8, XBC_BLK), f32)], compiler_params=_params(("parallel", "arbitrary")),
    )(proj2, proj2, proj2, dxbc, dxbc, conv_w, conv_b)


def _ssd_consts():
    ex = np.zeros((128, SSD_W), np.float32)
    for h in range(NH):
        ex[h, h * HD:(h + 1) * HD] = 1.0
    sel = np.zeros((8, 128), np.float32)
    for h in range(NH):
        sel[h // 2, h] = 1.0
    par = np.zeros((128, 128), np.float32)
    for r in range(128):
        for h in range(NH):
            par[r, h] = 1.0 if (h % 2) == (r // 64) else 0.0
    ones_blk = np.zeros((128, 128), np.float32)
    for r in range(128):
        ones_blk[r, (r // 64) * 64:(r // 64) * 64 + 64] = 1.0
    return ex, np.ascontiguousarray(ex.T), sel, par, ones_blk


SSD_SUB = 8


def _ssd_common(rs, xbc_ref, dtr_ref, a_ref, dtb_ref, ex_ref, sel_ref, par_ref):
    xs = xbc_ref[rs, 0:SSD_W]
    dt = _softplus(dtr_ref[rs, :] + dtb_ref[...])
    adt = dt * a_ref[...]
    r_i = lax.broadcasted_iota(jnp.int32, (CHUNK, CHUNK), 0)
    c_i = lax.broadcasted_iota(jnp.int32, (CHUNK, CHUNK), 1)
    tril = (r_i >= c_i).astype(f32)
    cs = _dot01(tril, adt, exact="a")
    cs2 = jnp.concatenate([cs, cs], axis=0) * par_ref[...]
    cstp = _dot01(sel_ref[...], cs2, tb=True, exact="a")
    both = _dot01(jnp.concatenate([dt, cs], axis=0), ex_ref[...])
    return xs, dt, cs, cstp, both[0:CHUNK], both[CHUNK:]


def _pair_mask():
    l_i = lax.broadcasted_iota(jnp.int32, (CHUNK, 128), 0)
    lane = lax.broadcasted_iota(jnp.int32, (CHUNK, 128), 1)
    return l_i >= (lane % CHUNK), lane < HD


def _block_diag(xp, first):
    z = jnp.zeros_like(xp)
    return jnp.concatenate([jnp.where(first, xp, z), jnp.where(first, z, xp)], axis=0)


def _ssd_fwd(xbc, proj2, a_row, dtb_row, dsk_full):
    s = xbc.shape[0]
    nc = s // CHUNK
    ex, ext, sel, par, ones_blk = _ssd_consts()

    def one_chunk(sub, states, refs):
        xbc_ref, dtr_ref, a_ref, dtb_ref, dsk_ref, ex_ref, sel_ref, par_ref, y_ref, hs_ref = refs
        rs = slice(sub * CHUNK, (sub + 1) * CHUNK)
        xs, dt, cs, cstp, dt_full, cs_full = _ssd_common(rs, xbc_ref, dtr_ref, a_ref, dtb_ref, ex_ref, sel_ref, par_ref)
        cs_last = cs_full[CHUNK - 1:CHUNK, :]
        xdt = xs * dt_full
        causal, first = _pair_mask()
        out = []
        for g in range(NG):
            gl = slice(g * GW, (g + 1) * GW)
            bg = xbc_ref[rs, SSD_W + g * NSTATE:SSD_W + (g + 1) * NSTATE].astype(bf16)
            cg = xbc_ref[rs, SSD_W + NG * NSTATE + g * NSTATE:SSD_W + NG * NSTATE + (g + 1) * NSTATE].astype(bf16)
            cb2 = lax.dot_general(cg, jnp.concatenate([bg, bg], axis=0), (((1,), (1,)), ((), ())), preferred_element_type=f32)
            hg = states[g]
            hs_ref[sub, g] = hg
            y0 = jnp.dot(cg, hg.astype(bf16), preferred_element_type=f32)
            yoff = jnp.exp(cs_full[:, gl]) * y0
            for j in range(GW // 128):
                pair = g * (GW // 128) + j
                pl_ = slice(pair * 128, (pair + 1) * 128)
                seg = jnp.exp(jnp.where(causal, cs_full[:, pl_] - cstp[pair:pair + 1, :], -jnp.inf))
                m = (cb2 * seg).astype(bf16)
                yd = jnp.dot(m, _block_diag(xdt[:, pl_].astype(bf16), first), preferred_element_type=f32)
                y_ref[rs, pl_] = yd + yoff[:, j * 128:(j + 1) * 128] + xs[:, pl_] * dsk_ref[:, pl_]
            xdec = (xdt[:, gl] * jnp.exp(cs_last[:, gl] - cs_full[:, gl])).astype(bf16)
            st = lax.dot_general(bg, xdec, (((0,), (0,)), ((), ())), preferred_element_type=f32)
            out.append(jnp.exp(cs_last[:, gl]) * hg + st)
        return out

    def body(*refs):
        hst = refs[-1]

        @pl.when(pl.program_id(0) == 0)
        def _():
            hst[...] = jnp.zeros_like(hst)

        states = [hst[g] for g in range(NG)]
        for sub in range(SSD_SUB):
            states = one_chunk(sub, states, refs[:-1])
        for g in range(NG):
            hst[g] = states[g]

    rows = SSD_SUB * CHUNK
    const = lambda shape: pl.BlockSpec(shape, lambda c: tuple(0 for _ in shape))
    return pl.pallas_call(
        body, name="ssd_fwd", grid=(nc // SSD_SUB,),
        in_specs=[pl.BlockSpec((rows, XBC), lambda c: (c, 0)), pl.BlockSpec((rows, 128), lambda c: (c, DT_COL)),
                  const((1, 128)), const((1, 128)), const((1, SSD_W)), const((128, SSD_W)), const((8, 128)), const((128, 128))],
        out_specs=[pl.BlockSpec((rows, SSD_W), lambda c: (c, 0)), pl.BlockSpec((SSD_SUB, NG, NSTATE, GW), lambda c: (c, 0, 0, 0))],
        out_shape=[_sds((s, SSD_W), f32), _sds((nc, NG, NSTATE, GW), f32)],
        scratch_shapes=[pltpu.VMEM((NG, NSTATE, GW), f32)], compiler_params=_params(("arbitrary",)),
    )(xbc, proj2, a_row, dtb_row, dsk_full, jnp.asarray(ex), jnp.asarray(sel), jnp.asarray(par))


def _ssd_bwd(xbc, proj2, dy, hsave, a_row, dtb_row, dsk_full):
    s = xbc.shape[0]
    nc = s // CHUNK
    ex, ext, sel, par, ones_blk = _ssd_consts()

    def one_chunk(sub, dhs, refs):
        (xbc_ref, dtr_ref, dy_ref, hs_ref, a_ref, dtb_ref, dsk_ref, ex_ref, ext_ref, sel_ref, par_ref, ob_ref,
         dxbc_ref, ddtr_ref, dd_ref, da_ref, ddtb_ref, dh, a_dd, a_da, a_dtb, dcs_lane, dcs_b, dxdt) = refs
        rs = slice(sub * CHUNK, (sub + 1) * CHUNK)
        dcs_lane, dcs_b, dxdt = dcs_lane.at[sub], dcs_b.at[sub], dxdt.at[sub]
        xs, dt, cs, cstp, dt_full, cs_full = _ssd_common(rs, xbc_ref, dtr_ref, a_ref, dtb_ref, ex_ref, sel_ref, par_ref)
        cs_last = cs_full[CHUNK - 1:CHUNK, :]
        xdt = xs * dt_full
        dyv = dy_ref[rs, :]
        a_dd[...] += _fold8(dyv * xs)
        causal, first = _pair_mask()
        diag = lax.broadcasted_iota(jnp.int32, (CHUNK, 128), 0) == lax.broadcasted_iota(jnp.int32, (CHUNK, 128), 1) % CHUNK
        dh_out = []
        for g in range(NG):
            gl = slice(g * GW, (g + 1) * GW)
            bcol = slice(SSD_W + g * NSTATE, SSD_W + (g + 1) * NSTATE)
            ccol = slice(SSD_W + NG * NSTATE + g * NSTATE, SSD_W + NG * NSTATE + (g + 1) * NSTATE)
            bg = xbc_ref[rs, bcol].astype(bf16)
            cg = xbc_ref[rs, ccol].astype(bf16)
            bg2 = jnp.concatenate([bg, bg], axis=0)
            cb2 = lax.dot_general(cg, bg2, (((1,), (1,)), ((), ())), preferred_element_type=f32)
            hg = hs_ref[sub, g]
            hgb = hg.astype(bf16)
            dhg = dhs[g]
            dhgb = dhg.astype(bf16)
            eg = jnp.exp(cs_full[:, gl])
            dec = jnp.exp(cs_last[:, gl] - cs_full[:, gl])
            gam = jnp.exp(cs_last[:, gl])
            dyg = dyv[:, gl]
            xdt_g = xdt[:, gl]
            y0 = jnp.dot(cg, hgb, preferred_element_type=f32)
            dy0 = (eg * dyg).astype(bf16)
            dcm = lax.dot_general(dy0, hgb, (((1,), (1,)), ((), ())), preferred_element_type=f32)
            dh_prev = gam * dhg + lax.dot_general(cg, dy0, (((0,), (0,)), ((), ())), preferred_element_type=f32)
            dgam = jnp.sum(dhg * hg, axis=0, keepdims=True) * gam
            dxdec = jnp.dot(bg, dhgb, preferred_element_type=f32)
            dbm = lax.dot_general((xdt_g * dec).astype(bf16), dhgb, (((1,), (1,)), ((), ())), preferred_element_type=f32)
            t = dxdec * xdt_g * dec
            dcs_lane[:, gl] = dyg * eg * y0 - t
            dcs_lane[CHUNK - 1:CHUNK, gl] += jnp.sum(t, axis=0, keepdims=True) + dgam
            dxdt[:, gl] = dxdec * dec
            dcb2 = jnp.zeros((CHUNK, 128), f32)
            for j in range(GW // 128):
                pair = g * (GW // 128) + j
                pl_ = slice(pair * 128, (pair + 1) * 128)
                seg = jnp.exp(jnp.where(causal, cs_full[:, pl_] - cstp[pair:pair + 1, :], -jnp.inf))
                m = cb2 * seg
                mb = m.astype(bf16)
                rhs = _block_diag(xdt[:, pl_].astype(bf16), first)
                dyp = dyv[:, pl_].astype(bf16)
                dm = lax.dot_general(dyp, rhs, (((1,), (1,)), ((), ())), preferred_element_type=f32)
                tt = lax.dot_general(mb, dyp, (((0,), (0,)), ((), ())), preferred_element_type=f32)
                dxdt[:, pl_] += jnp.where(first, tt[0:CHUNK], tt[CHUNK:])
                dcb2 = dcb2 + dm * seg
                w = dm * m
                colsum = jnp.sum(w, axis=0, keepdims=True)
                dcs_b[:, pl_] = _dot01(w - jnp.where(diag, colsum, 0.0), ob_ref[...])
            dcb2b = dcb2.astype(bf16)
            dcm = dcm + jnp.dot(dcb2b, bg2, preferred_element_type=f32)
            t3 = lax.dot_general(dcb2b, cg, (((0,), (0,)), ((), ())), preferred_element_type=f32)
            dxbc_ref[rs, bcol] = dbm + t3[0:CHUNK] + t3[CHUNK:]
            dxbc_ref[rs, ccol] = dcm
            dh_out.append(dh_prev)
        dxdtv = dxdt[...]
        both = _dot01(jnp.concatenate([dcs_lane[...] + dcs_b[...] * (1.0 / HD), dxdtv * xs], axis=0), ext_ref[...])
        dcs = both[0:CHUNK]
        r_i = lax.broadcasted_iota(jnp.int32, (CHUNK, CHUNK), 0)
        c_i = lax.broadcasted_iota(jnp.int32, (CHUNK, CHUNK), 1)
        triu = (r_i <= c_i).astype(f32)
        da_ = _dot01(triu, dcs, exact="a")
        ddt = da_ * a_ref[...] + both[CHUNK:]
        a_da[...] += _fold8(da_ * dt)
        dxbc_ref[rs, 0:SSD_W] = dyv * dsk_ref[...] + dxdtv * dt_full
        ddtr = ddt * _sigmoid(dtr_ref[rs, :] + dtb_ref[...])
        ddtr_ref[rs, :] = ddtr
        a_dtb[...] += _fold8(ddtr)
        return dh_out

    nsteps = nc // SSD_SUB

    def body(*refs):
        dd_ref, da_ref, ddtb_ref, dh, a_dd, a_da, a_dtb = refs[14:21]
        ext_ref = refs[8]
        step = pl.program_id(0)

        @pl.when(step == 0)
        def _():
            dh[...] = jnp.zeros_like(dh)
            a_dd[...] = jnp.zeros_like(a_dd)
            a_da[...] = jnp.zeros_like(a_da)
            a_dtb[...] = jnp.zeros_like(a_dtb)

        dhs = [dh[g] for g in range(NG)]
        for sub in reversed(range(SSD_SUB)):
            dhs = one_chunk(sub, dhs, refs)
        for g in range(NG):
            dh[g] = dhs[g]

        @pl.when(step == nsteps - 1)
        def _():
            dd_ref[...] = jnp.sum(jnp.dot(a_dd[...], ext_ref[...], precision=HIGHEST, preferred_element_type=f32), axis=0, keepdims=True)
            da_ref[...] = jnp.sum(a_da[...], axis=0, keepdims=True)
            ddtb_ref[...] = jnp.sum(a_dtb[...], axis=0, keepdims=True)

    rev = lambda c: nsteps - 1 - c
    rows = SSD_SUB * CHUNK
    const = lambda shape: pl.BlockSpec(shape, lambda c: tuple(0 for _ in shape))
    return pl.pallas_call(
        body, name="ssd_bwd", grid=(nsteps,),
        in_specs=[pl.BlockSpec((rows, XBC), lambda c: (rev(c), 0)), pl.BlockSpec((rows, 128), lambda c: (rev(c), DT_COL)),
                  pl.BlockSpec((rows, SSD_W), lambda c: (rev(c), 0)), pl.BlockSpec((SSD_SUB, NG, NSTATE, GW), lambda c: (rev(c), 0, 0, 0)),
                  const((1, 128)), const((1, 128)), const((1, SSD_W)), const((128, SSD_W)), const((SSD_W, 128)),
                  const((8, 128)), const((128, 128)), const((128, 128))],
        out_specs=[pl.BlockSpec((rows, XBC), lambda c: (rev(c), 0)), pl.BlockSpec((rows, 128), lambda c: (rev(c), 0)),
                   const((1, 128)), const((1, 128)), const((1, 128))],
        out_shape=[_sds((s, XBC), f32), _sds((s, 128), f32), _sds((1, 128), f32), _sds((1, 128), f32), _sds((1, 128), f32)],
        scratch_shapes=[pltpu.VMEM((NG, NSTATE, GW), f32), pltpu.VMEM((8, SSD_W), f32), pltpu.VMEM((8, 128), f32), pltpu.VMEM((8, 128), f32)]
        + [pltpu.VMEM((SSD_SUB, CHUNK, SSD_W), f32)] * 3,
        compiler_params=_params(("arbitrary",)),
    )(xbc, proj2, dy, hsave, a_row, dtb_row, dsk_full, jnp.asarray(ex), jnp.asarray(ext), jnp.asarray(sel), jnp.asarray(par),
      jnp.asarray(ones_blk))


def _local_step(x, tgt, mods, g_mix, rel, conv_w, conv_b, dt_bias, a_log, d_skip, g_att, g_ssd, g_ffn, g_final, weights):
    s = x.shape[0]
    tm_e = 512 if s % 512 == 0 else s
    tm_m = 512 if s % 512 == 0 else s
    tm_l = 1024 if s % 1024 == 0 else s
    tk = 2048 if s % 2048 == 0 else s
    sh1, sc1, gt1, sh2, sc2, gt2 = [mods[:, i * D:(i + 1) * D] for i in range(6)]

    h1b = _norm_mod("norm_mod_1", x, g_mix, sc1, sh1, tm_e)
    win, win_b = weights.w_in(h1b)
    qkv = _mm_nn_fullk("proj_qkv", h1b, win, tm_l, 1536, bf16, n=IN_A)
    proj2 = _mm_nn_fullk("proj_zxbcdt", h1b, win_b, tm_l, 896, f32)
    bias = _expand_bias(rel)
    att = _attn_fwd(qkv, bias)
    xbc = _ssd_conv(proj2, conv_w, conv_b, tm_l)
    a_row = jnp.pad(-jnp.exp(a_log), ((0, 0), (0, 128 - NH)))
    dtb_row = jnp.pad(dt_bias, ((0, 0), (0, 128 - NH)))
    dsk_full = jnp.repeat(d_skip, HD, axis=1)
    y, hsave = _ssd_fwd(xbc, proj2, a_row, dtb_row, dsk_full)
    mixcat = _mix_pre(att, y, proj2, g_att, g_ssd, tm_e)
    wout = weights.w_out(mixcat)
    mix = _mm_nn_fullk("proj_out", mixcat, wout, tm_l, D, f32)
    x2, h2b = _resid_norm_mod(x, gt1, mix, g_ffn, sc2, sh2, tm_e)
    wg4, wu4, wd4 = weights.ffn(h2b)
    act, sil, ud = _ffn_up(h2b, wg4, wu4, tm_m)
    ffn = _ffn_down(act, wd4, tm_l)

    dx3, dffn, loss, dg_final, dgt2 = _final_fwd_bwd(x2, ffn, gt2, g_final, tgt, tm_e)
    tok = weights.grad(("w_down",), [_grad_wdown4(act, dffn, 1024, tk)])
    dgate, dup = _ffn_dact(dffn, wd4, sil, ud, tm_l, dep=tok)
    tk2 = 4096 if s % 4096 == 0 else s
    tok = weights.grad(("w_gate", "w_up"), [_grad_cols4("grad_w_gate", h2b, dgate, 512, tk2), _grad_cols4("grad_w_up", h2b, dup, 512, tk2)])
    dh2 = _ffn_dh(dgate, dup, wg4, wu4, tm_m, dep=tok)
    dx2, dmix, dsc2, dsh2, dg_ffn, dgt1 = _norm_mod_bwd("norm_mod_bwd_2", dh2, x2, g_ffn, sc2, dx3, tm_e, mix=mix, gt=gt1)
    tok = weights.grad(("w_out",), [_mm_tn("grad_w_out", mixcat, dmix, 512, 1024, tk2, bf16).reshape(NSH, D // NSH, D)])
    dmc = _mm_nt("dmixcat", dmix, wout, tm_l, D, D, f32, dep=tok)
    datt, dy, dz, dg_att, dg_ssd = _mix_pre_bwd(dmc, att, y, proj2, g_att, g_ssd, tm_e)
    dq, dk, dv, gband = _attn_bwd(qkv, datt, bias)
    drel = _rel_bias_grad(gband.reshape(NH, CHUNK, BANDP))
    dxbc, ddtr, dd_row, da_row, ddtb_row = _ssd_bwd(xbc, proj2, dy, hsave, a_row, dtb_row, dsk_full)
    dxbc_raw, dconv_w, dconv_b = _ssd_conv_bwd(dxbc, proj2, conv_w, conv_b, tm_e)
    dproj = jnp.concatenate([dq, dk, dv, dz, dxbc_raw, ddtr.astype(bf16)], axis=1)
    gwin = _mm_tn("grad_w_in", h1b, dproj, 512, 1152, tk2, bf16)
    gwin4 = jnp.stack([jnp.pad(gwin[:, k * IN_SH:(k + 1) * IN_SH], ((0, 0), (0, IN_SHP - IN_SH))) for k in range(NSH)])
    tok = weights.grad(("w_in",), [gwin4])
    dh1 = _mm_nt("dh1", dproj, win, tm_l, 512, IN_P, f32, dep=tok)
    grad_x, dsc1, dsh1, dg_mix = _norm_mod_bwd("norm_mod_bwd_1", dh1, x, g_mix, sc1, dx2, tm_e)

    dmods = jnp.concatenate([dsh1, dsc1, dgt1, dsh2, dsc2, dgt2], axis=1)
    dd_skip = dd_row[:, :NH]
    da_log = da_row[:, :NH] * a_row[:, :NH]
    small = dict(g_mix=dg_mix, conv_b=dconv_b, dt_bias=ddtb_row[:, :NH], a_log=da_log, d_skip=dd_skip, g_att_out=dg_att,
                 g_ssd_out=dg_ssd, g_ffn=dg_ffn, g_final=dg_final, rel_bias=drel, conv_w=dconv_w)
    return loss[0, 0], grad_x, dmods, small


HBM = pl.BlockSpec(memory_space=pl.ANY)
VMEM = pl.BlockSpec(memory_space=pltpu.VMEM)


def _place():
    x, y, c = lax.axis_index("x"), lax.axis_index("y"), lax.axis_index("c")
    chips = [(1 - x, y), (x, 1 - y), (1 - x, 1 - y)]
    return x, y, c, chips


def _allgather8(name, payload, dep=None):
    r = payload.shape[0]
    deps = [] if dep is None else [dep]

    def body(x_ref, *rest):
        out_ref, send_sems, recv_sems, local_sem = rest[-4:]
        x, y, c, chips = _place()
        me, sibling = (x, y, c), (x, y, 1 - c)

        def slot(px, py, pc):
            return out_ref.at[4 * px + 2 * py + pc]

        def copy(k, block, to, src=None):
            return pltpu.make_async_remote_copy(
                src_ref=slot(*block) if src is None else src, dst_ref=slot(*block),
                send_sem=send_sems.at[k], recv_sem=recv_sems.at[k], device_id=to, device_id_type=MESH)

        mine = pltpu.make_async_copy(x_ref, slot(*me), local_sem)
        mine.start()
        first = [copy(0, me, sibling, src=x_ref)]
        first += [copy(1 + j, me, (*chip, c), src=x_ref) for j, chip in enumerate(chips)]
        for cp in first:
            cp.start()
        passed = [copy(4 + j, (*chip, c), sibling) for j, chip in enumerate(chips)]
        for j, chip in enumerate(chips):
            copy(1 + j, (*chip, c), me).wait_recv()
            passed[j].start()
        copy(0, sibling, me).wait_recv()
        for j, chip in enumerate(chips):
            copy(4 + j, (*chip, 1 - c), me).wait_recv()
        for cp in first + passed:
            cp.wait_send()
        mine.wait()

    return pl.pallas_call(
        body, name=name, out_shape=_sds((N_DEV, r, 128), f32), in_specs=[VMEM] * (1 + len(deps)), out_specs=VMEM,
        scratch_shapes=[pltpu.SemaphoreType.DMA((7,)), pltpu.SemaphoreType.DMA((7,)), pltpu.SemaphoreType.DMA],
    )(payload, *deps)


def _sum8(g):
    r = g.shape[1]

    def body(g_ref, o_ref):
        acc = g_ref[0]
        for i in range(1, N_DEV):
            acc = acc + g_ref[i]
        o_ref[...] = acc

    return pl.pallas_call(body, name="sum8", out_shape=_sds((r, 128), f32))(g)


SEM = pl.BlockSpec(memory_space=pltpu.SEMAPHORE)
EFFECT = pltpu.SideEffectType.DATAFLOW_SIDE_EFFECTING


def _gather_copies(ins, lands, send_sems, recv_sems):
    x, y, c, chips = _place()
    k = 2 * x + y
    starts, recvs = [], []
    for w in range(len(ins)):
        for j, (px, py) in enumerate(chips):
            def mk(dst):
                return pltpu.make_async_remote_copy(src_ref=ins[w].at[c], dst_ref=dst, send_sem=send_sems[w].at[j],
                                                    recv_sem=recv_sems[w].at[j], device_id=(px, py, c), device_id_type=MESH)
            starts.append(mk(lands[w].at[k, c]))
            recvs.append(mk(lands[w].at[2 * px + py, c]))
    return starts, recvs


def _reduce_copies(ins, lands, send_sems, recv_sems):
    x, y, c, chips = _place()
    k = 2 * x + y
    starts, recvs = [], []
    for w in range(len(ins)):
        for j, (px, py) in enumerate(chips):
            def mk(dst):
                return pltpu.make_async_remote_copy(src_ref=ins[w].at[2 * px + py], dst_ref=dst, send_sem=send_sems[w].at[j],
                                                    recv_sem=recv_sems[w].at[j], device_id=(px, py, c), device_id_type=MESH)
            starts.append(mk(lands[w].at[k]))
            recvs.append(mk(lands[w].at[2 * px + py]))
    return starts, recvs


def _split_start(name, copies, srcs, land_shapes):
    nw = len(srcs)

    def body(*refs):
        starts, _ = copies(refs[:nw], refs[nw:2 * nw], refs[2 * nw:3 * nw], refs[3 * nw:4 * nw])
        for cp in starts:
            cp.start()
        refs[6 * nw][...] = jnp.zeros((8, 128), f32)

    sems = [pltpu.SemaphoreType.DMA((3,))] * nw
    bufs = [pltpu.HBM(s.shape, bf16) for s in srcs] + [pltpu.HBM(s, bf16) for s in land_shapes]
    res = pl.pallas_call(
        body, name=name, out_shape=sems + sems + bufs + [_sds((8, 128), f32)],
        in_specs=[HBM] * (2 * nw), out_specs=[SEM] * (2 * nw) + [HBM] * (2 * nw) + [VMEM],
        input_output_aliases={i: 2 * nw + i for i in range(2 * nw)},
        compiler_params=pltpu.CompilerParams(has_side_effects=EFFECT),
    )(*[pltpu.with_memory_space_constraint(s, pltpu.HBM) for s in srcs],
      *[pltpu.with_memory_space_constraint(lax.empty(s, bf16), pltpu.HBM) for s in land_shapes])
    return res[:nw], res[nw:2 * nw], res[2 * nw:3 * nw], res[3 * nw:4 * nw], res[4 * nw]


def _split_wait(name, copies, send_sems, recv_sems, srcs, lands, after):
    nw = len(srcs)

    def body(*refs):
        starts, recvs = copies(refs[:nw], refs[nw:2 * nw], refs[2 * nw:3 * nw], refs[3 * nw:4 * nw])
        for s_, r_ in zip(starts, recvs):
            s_.wait_send()
            r_.wait_recv()

    bufs = [pltpu.HBM(s.shape, bf16) for s in srcs] + [pltpu.HBM(l.shape, bf16) for l in lands]
    res = pl.pallas_call(
        body, name=name, out_shape=bufs, in_specs=[HBM] * (2 * nw) + [SEM] * (2 * nw) + [HBM], out_specs=[HBM] * (2 * nw),
        input_output_aliases={i: i for i in range(2 * nw)},
        compiler_params=pltpu.CompilerParams(has_side_effects=EFFECT),
    )(*srcs, *lands, *send_sems, *recv_sems, after)
    return res[:nw], res[nw:]


def _gather_forward(name, shards, lands):
    nw = len(shards)

    def body(*refs):
        ins, lands_in, outs = refs[:nw], refs[nw:2 * nw], refs[2 * nw:3 * nw]
        st_a, st_b, st_c = refs[3 * nw:4 * nw], refs[4 * nw:5 * nw], refs[5 * nw:6 * nw]
        send_sems, recv_sems, load_sems, store_sems = refs[6 * nw:]
        x, y, c, chips = _place()
        k = 2 * x + y
        sibling = (x, y, 1 - c)
        ld_a = [pltpu.make_async_copy(ins[w].at[c], st_a[w], load_sems.at[w, 0]) for w in range(nw)]
        ld_b = [pltpu.make_async_copy(ins[w].at[1 - c], st_b[w], load_sems.at[w, 1]) for w in range(nw)]
        for cp in ld_a + ld_b:
            cp.start()
        st_own = []
        for w in range(nw):
            ld_a[w].wait()
            st_own.append(pltpu.make_async_copy(st_a[w], outs[w].at[k, c], store_sems.at[w, 0]))
            st_own[-1].start()
        for w in range(nw):
            ld_b[w].wait()
            st_own.append(pltpu.make_async_copy(st_b[w], outs[w].at[k, 1 - c], store_sems.at[w, 1]))
            st_own[-1].start()
        for cp in st_own:
            cp.wait()
        fwds = {}
        for j, (px, py) in enumerate(chips):
            kq = 2 * px + py
            for w in range(nw):
                slot = st_b[w] if j % 2 == 0 else st_c[w]
                if j == 2:
                    fwds[w, 0].wait_send()
                ld = pltpu.make_async_copy(lands_in[w].at[kq, c], slot, load_sems.at[w, 2 + j])
                ld.start()
                ld.wait()
                fwds[w, j] = pltpu.make_async_remote_copy(src_ref=slot, dst_ref=outs[w].at[kq, c], send_sem=send_sems.at[w, j],
                                                          recv_sem=recv_sems.at[w, j], device_id=sibling, device_id_type=MESH)
                fwds[w, j].start()
        for j, (px, py) in enumerate(chips):
            for w in range(nw):
                pltpu.make_async_remote_copy(src_ref=st_c[w], dst_ref=outs[w].at[2 * px + py, 1 - c], send_sem=send_sems.at[w, j],
                                             recv_sem=recv_sems.at[w, j], device_id=sibling, device_id_type=MESH).wait_recv()
        for w in range(nw):
            fwds[w, 1].wait_send()
            fwds[w, 2].wait_send()

    stage = [pltpu.VMEM(s.shape[1:], bf16) for s in shards]
    return pl.pallas_call(
        body, name=name, out_shape=[_sds(l.shape, bf16) for l in lands],
        in_specs=[HBM] * (2 * nw), out_specs=[HBM] * nw, input_output_aliases={nw + w: w for w in range(nw)},
        scratch_shapes=stage * 3 + [pltpu.SemaphoreType.DMA((nw, 3)), pltpu.SemaphoreType.DMA((nw, 3)), pltpu.SemaphoreType.DMA((nw, 5)),
                                    pltpu.SemaphoreType.DMA((nw, 2))],
        compiler_params=pltpu.CompilerParams(vmem_limit_bytes=VMEM_LIMIT),
    )(*shards, *lands)


def _rs_pair_exchange(name, grads):
    nw = len(grads)

    def body(*refs):
        ins, got, stage = refs[:nw], refs[nw:2 * nw], refs[2 * nw:3 * nw]
        send_sems, recv_sems, load_sems = refs[3 * nw:]
        x, y, c, _ = _place()

        def load(w, kk):
            return pltpu.make_async_copy(ins[w].at[kk, 1 - c], stage[w].at[kk % 2], load_sems.at[w, kk])

        def send(w, kk):
            return pltpu.make_async_remote_copy(src_ref=stage[w].at[kk % 2], dst_ref=got[w].at[kk], send_sem=send_sems.at[w, kk],
                                                recv_sem=recv_sems.at[w, kk], device_id=(x, y, 1 - c), device_id_type=MESH)

        for kk in range(2):
            for w in range(nw):
                load(w, kk).start()
        for kk in range(NSH):
            for w in range(nw):
                load(w, kk).wait()
                send(w, kk).start()
            if kk + 2 < NSH:
                for w in range(nw):
                    send(w, kk).wait_send()
                    load(w, kk + 2).start()
        for kk in range(NSH - 2, NSH):
            for w in range(nw):
                send(w, kk).wait_send()
        for kk in range(NSH):
            for w in range(nw):
                send(w, kk).wait_recv()

    return pl.pallas_call(
        body, name=name, out_shape=[_sds((NSH,) + g.shape[2:], bf16) for g in grads], in_specs=[HBM] * nw, out_specs=[HBM] * nw,
        scratch_shapes=[pltpu.VMEM((2,) + g.shape[2:], bf16) for g in grads]
        + [pltpu.SemaphoreType.DMA((nw, NSH)), pltpu.SemaphoreType.DMA((nw, NSH)), pltpu.SemaphoreType.DMA((nw, NSH))],
        compiler_params=pltpu.CompilerParams(vmem_limit_bytes=VMEM_LIMIT),
    )(*grads)


def _rs_pair_gather(name, halves):
    nw = len(halves)

    def body(*refs):
        ins, outs, stage = refs[:nw], refs[nw:2 * nw], refs[2 * nw:3 * nw]
        send_sems, recv_sems, local_sems, stage_sems = refs[3 * nw:]
        x, y, c, _ = _place()
        loads = [pltpu.make_async_copy(ins[w], stage[w], stage_sems.at[w]) for w in range(nw)]
        for cp in loads:
            cp.start()
        local, cps = [], []
        for w in range(nw):
            loads[w].wait()
            local.append(pltpu.make_async_copy(stage[w], outs[w].at[c], local_sems.at[w]))
            cps.append(pltpu.make_async_remote_copy(src_ref=stage[w], dst_ref=outs[w].at[c], send_sem=send_sems.at[w],
                                                    recv_sem=recv_sems.at[w], device_id=(x, y, 1 - c), device_id_type=MESH))
            local[w].start()
            cps[w].start()
        for w in range(nw):
            pltpu.make_async_remote_copy(src_ref=stage[w], dst_ref=outs[w].at[1 - c], send_sem=send_sems.at[w], recv_sem=recv_sems.at[w],
                                         device_id=(x, y, 1 - c), device_id_type=MESH).wait_recv()
        for cp in cps:
            cp.wait_send()
        for cp in local:
            cp.wait()

    return pl.pallas_call(
        body, name=name, out_shape=[_sds((2,) + h.shape, f32) for h in halves], in_specs=[HBM] * nw, out_specs=[HBM] * nw,
        scratch_shapes=[pltpu.VMEM(h.shape, f32) for h in halves]
        + [pltpu.SemaphoreType.DMA((nw,)), pltpu.SemaphoreType.DMA((nw,)), pltpu.SemaphoreType.DMA((nw,)), pltpu.SemaphoreType.DMA((nw,))],
        compiler_params=pltpu.CompilerParams(vmem_limit_bytes=VMEM_LIMIT),
    )(*halves)


def _row_tile(r, c, nbuf):
    budget = 24 * 1024 * 1024 // (2 * nbuf * 4 * c)
    fits = [t for t in range(16, r + 1, 16) if r % t == 0 and t <= budget]
    return max(fits) if fits else r


def _cast_bf16(name, a, dep=None):
    r, c = a.shape
    tr = _row_tile(r, c, 2)
    dep_specs, dep_ops = _dep_args(dep, 1)

    def body(a_ref, *rest):
        rest[-1][...] = a_ref[...].astype(bf16)

    spec = pl.BlockSpec((tr, c), lambda i: (i, 0))
    return pl.pallas_call(body, name=name, grid=(r // tr,), in_specs=[spec] + dep_specs, out_specs=spec, out_shape=_sds((r, c), bf16),
                          compiler_params=_params(("parallel",)))(a, *dep_ops)


def _w_in_columns(win4):
    tr = 256

    def body(a_ref, o_ref, ob_ref):
        for k in range(NSH):
            o_ref[:, IN_SH * k:IN_SH * (k + 1)] = a_ref[k][:, :IN_SH]
        o_ref[:, IN_COLS:] = jnp.zeros((tr, IN_P - IN_COLS), bf16)
        ob_ref[...] = o_ref[:, IN_A:]

    return pl.pallas_call(
        body, name="w_in_columns", grid=(D // tr,), in_specs=[pl.BlockSpec((NSH, tr, IN_SHP), lambda i: (0, i, 0))],
        out_specs=[pl.BlockSpec((tr, IN_P), lambda i: (i, 0)), pl.BlockSpec((tr, IN_B), lambda i: (i, 0))],
        out_shape=[_sds((D, IN_P), bf16), _sds((D, IN_B), bf16)], compiler_params=_params(("parallel",)))(win4)


def _pair_sum(name, core, grads, got):
    _, _, rh, c = grads.shape
    tr = _row_tile(rh, c, 2)

    def body(c_ref, a_ref, b_ref, o_ref):
        o_ref[...] = (a_ref[...].astype(f32) + b_ref[...].astype(f32)).astype(bf16)

    spec = pl.BlockSpec((None, tr, c), lambda k, i, c_ref: (k, i, 0))
    return pl.pallas_call(
        body, name=name, out_shape=_sds((NSH, rh, c), bf16),
        grid_spec=pltpu.PrefetchScalarGridSpec(
            num_scalar_prefetch=1, grid=(NSH, rh // tr),
            in_specs=[pl.BlockSpec((None, None, tr, c), lambda k, i, c_ref: (k, c_ref[0], i, 0)), spec], out_specs=spec),
        compiler_params=_params(("parallel", "parallel")))(core, grads, got)


def _chip_sum(name, chip, sums, lands):
    _, rh, c = sums.shape
    tr = _row_tile(rh, c, 4)

    def body(k_ref, own_ref, l_ref, o_ref):
        own = own_ref[...].astype(f32)
        acc = None
        for j in range(NSH):
            term = jnp.where(k_ref[0] == j, own, l_ref[j].astype(f32))
            acc = term if acc is None else acc + term
        o_ref[...] = acc

    return pl.pallas_call(
        body, name=name, out_shape=_sds((rh, c), f32),
        grid_spec=pltpu.PrefetchScalarGridSpec(
            num_scalar_prefetch=1, grid=(rh // tr,),
            in_specs=[pl.BlockSpec((None, tr, c), lambda i, k_ref: (k_ref[0], i, 0)), pl.BlockSpec((NSH, tr, c), lambda i, k_ref: (0, i, 0))],
            out_specs=pl.BlockSpec((tr, c), lambda i, k_ref: (i, 0))),
        compiler_params=_params(("parallel",)))(chip, sums, lands)


def _mods_part(cond16, w_ada, b_part):
    n = w_ada.shape[1]
    tn = 512

    def body(c_ref, w_ref, b_ref, o_ref):
        cv = c_ref[...]
        o_ref[...] = _dot(cv * _sigmoid(cv), w_ref[...]) + b_ref[...]

    return pl.pallas_call(
        body, name="mods_part", grid=(n // tn,),
        in_specs=[pl.BlockSpec((16, D), lambda j: (0, 0)), pl.BlockSpec((D, tn), lambda j: (0, j)), pl.BlockSpec((1, tn), lambda j: (0, j))],
        out_specs=pl.BlockSpec((16, tn), lambda j: (0, j)), out_shape=_sds((16, n), f32), compiler_params=_params(("parallel",)),
    )(cond16, w_ada, b_part)


def _grad_w_ada(cond16, dm16):
    n = dm16.shape[1]
    tr = 256

    def body(c_ref, d_ref, o_ref):
        cv = c_ref[...]
        o_ref[...] = _dot(cv * _sigmoid(cv), d_ref[...], ta=True)

    return pl.pallas_call(
        body, name="grad_w_ada", grid=(D // tr,),
        in_specs=[pl.BlockSpec((16, tr), lambda i: (0, i)), pl.BlockSpec((16, n), lambda i: (0, 0))],
        out_specs=pl.BlockSpec((tr, n), lambda i: (i, 0)), out_shape=_sds((D, n), f32), compiler_params=_params(("parallel",)),
    )(cond16, dm16)


def _adamw(name, w, g, m, v):
    r, c = w.shape
    tr = _row_tile(r, c, 7)
    spec = pl.BlockSpec((tr, c), lambda i: (i, 0))
    grid = (r // tr,)

    def body(w_ref, g_ref, m_ref, v_ref, d_ref, nm_ref, nv_ref):
        gv = g_ref[...]
        nm = ADAM_B1 * m_ref[...] + (1.0 - ADAM_B1) * gv
        nv = ADAM_B2 * v_ref[...] + (1.0 - ADAM_B2) * (gv * gv)
        nm_ref[...] = nm
        nv_ref[...] = nv
        m_hat = nm / (1.0 - ADAM_B1 ** ADAM_STEP)
        v_hat = nv / (1.0 - ADAM_B2 ** ADAM_STEP)
        d_ref[...] = -ADAM_LR * (m_hat / (jnp.sqrt(v_hat) + ADAM_EPS) + ADAM_WD * w_ref[...])

    return pl.pallas_call(body, name=name, grid=grid, in_specs=[spec] * 4, out_specs=[spec] * 3, out_shape=[_sds(w.shape, f32)] * 3,
                          compiler_params=_params(("parallel",)))(w, g, m, v)


def _pack(parts, rows):
    flat = []
    for p in parts:
        p = p.reshape(-1)
        flat.append(jnp.pad(p, (0, (-p.shape[0]) % 128)))
    v = jnp.concatenate(flat)
    return jnp.pad(v, (0, rows * 128 - v.shape[0])).reshape(rows, 128)


def _unpack(packed, sizes):
    lead = packed.shape[:-2]
    flat = packed.reshape(lead + (-1,))
    out, off = [], 0
    for n in sizes:
        out.append(flat[..., off:off + n])
        off += n + (-n) % 128
    return out


BIG = ("w_in", "w_out", "w_gate", "w_up", "w_down")
SMALL = ("b_ada", "g_mix", "conv_b", "dt_bias", "a_log", "d_skip", "g_att_out", "g_ssd_out", "g_ffn", "g_final", "rel_bias", "conv_w")
ORDER = ("w_ada", "b_ada", "g_mix", "w_in", "rel_bias", "conv_w", "conv_b", "dt_bias", "a_log", "d_skip", "g_att_out", "g_ssd_out",
         "w_out", "g_ffn", "w_gate", "w_up", "w_down", "g_final")
REL_SH = N_REL // NSH
CONVW_SH = XBC // NSH
ADA_SH = 6 * D // NSH


class _Exchange:
    def __init__(self, core, chip):
        self.core, self.chip = core, chip
        self.gathered = {}
        self.pending = []

    def gather(self, names, shards):
        ssem, rsem, thru, lands, token = _split_start("gather_start_" + "_".join(names), _gather_copies, shards,
                                                      [(NSH,) + s.shape for s in shards])
        self.gathered.update({n: (ssem[i], rsem[i], thru[i], lands[i]) for i, n in enumerate(names)})
        return token

    def _whole(self, names, after):
        ssem, rsem, thru, lands = zip(*[self.gathered[n] for n in names])
        tag = "_".join(names)
        thru, lands = _split_wait("gather_wait_" + tag, _gather_copies, ssem, rsem, thru, lands, after)
        return _gather_forward("gather_forward_" + tag, thru, lands)

    def w_in(self, after):
        (win4,) = self._whole(("w_in",), after)
        return _w_in_columns(win4.reshape(NSH, D, IN_SHP))

    def w_out(self, after):
        (wout4,) = self._whole(("w_out",), after)
        return wout4.reshape(D, D)

    def ffn(self, after):
        wg4, wu4, wd4 = self._whole(("w_gate", "w_up", "w_down"), after)
        return wg4.reshape(NSH, D, FSH), wu4.reshape(NSH, D, FSH), wd4.reshape(NSH, FSH, D)

    def grad(self, names, grads):
        tag = "_".join(names)
        stacked = [g.reshape(NSH, 2, g.shape[1] // 2, g.shape[2]) for g in grads]
        got = _rs_pair_exchange("rs_pair_exchange_" + tag, stacked)
        sums = [_pair_sum("pair_sum_" + n, self.core, o, g) for n, o, g in zip(names, stacked, got)]
        self.pending.append((names, _split_start("rs_start_" + tag, _reduce_copies, sums, [s.shape for s in sums])))
        return self.pending[-1][1][4]

    def finish(self, after):
        grads = {}
        for names, (ssem, rsem, sums, lands, _) in self.pending:
            tag = "_".join(names)
            sums, lands = _split_wait("rs_wait_" + tag, _reduce_copies, ssem, rsem, sums, lands, after)
            halves = [_chip_sum("chip_sum_" + n, self.chip, sm, ld) for n, sm, ld in zip(names, sums, lands)]
            for n, f in zip(names, _rs_pair_gather("rs_pair_gather_" + tag, halves)):
                grads[n] = f.reshape(2 * f.shape[1], f.shape[2])
        return grads


def kernel(x, c, w_ada, b_ada, g_mix, w_in, rel_bias, conv_w, conv_b, dt_bias, a_log, d_skip, g_att_out, g_ssd_out, w_out, g_ffn, w_gate, w_up, w_down, g_final, loss_target, m_w_ada, m_b_ada, m_g_mix, m_w_in, m_rel_bias, m_conv_w, m_conv_b, m_dt_bias, m_a_log, m_d_skip, m_g_att_out, m_g_ssd_out, m_w_out, m_g_ffn, m_w_gate, m_w_up, m_w_down, m_g_final, v_w_ada, v_b_ada, v_g_mix, v_w_in, v_rel_bias, v_conv_w, v_conv_b, v_dt_bias, v_a_log, v_d_skip, v_g_att_out, v_g_ssd_out, v_w_out, v_g_ffn, v_w_gate, v_w_up, v_w_down, v_g_final):
    args = dict(locals())
    w = {n: args[n] for n in ORDER}
    m = {n: args["m_" + n] for n in ORDER}
    v = {n: args["v_" + n] for n in ORDER}
    ix, iy, ic = lax.axis_index("x"), lax.axis_index("y"), lax.axis_index("c")
    chip = 2 * ix + iy
    dev = 2 * chip + ic
    s = x.shape[1]

    g1 = _allgather8("gather_inputs", _pack([c[0], rel_bias[0], conv_w[0]], 40))
    c_all, rel_sh, convw_sh = _unpack(g1, [D, NH * REL_SH, 4 * CONVW_SH])
    rel_full = jnp.concatenate([rel_sh[2 * k].reshape(NH, REL_SH) for k in range(NSH)], axis=1)
    convw_full = jnp.concatenate([convw_sh[2 * k].reshape(4, CONVW_SH) for k in range(NSH)], axis=1)
    cond16 = jnp.pad(c_all, ((0, 8), (0, 0)))
    b_part = lax.dynamic_slice_in_dim(b_ada, chip * ADA_SH, ADA_SH, axis=1)
    mods_part = _mods_part(cond16, w_ada[0], b_part)[:N_DEV]
    g2 = _allgather8("gather_mods", mods_part.reshape(N_DEV * ADA_SH // 128, 128))
    mods_all = jnp.concatenate([g2[2 * k].reshape(N_DEV, ADA_SH) for k in range(NSH)], axis=1)
    mods = lax.dynamic_slice_in_dim(mods_all, dev, 1, axis=0)

    exchange = _Exchange(jnp.reshape(ic, (1,)).astype(jnp.int32), jnp.reshape(chip, (1,)).astype(jnp.int32))
    shard_in = _cast_bf16("cast_w_in", jnp.pad(w_in[0], ((0, 0), (0, IN_SHP - IN_SH))), dep=g2[0, :8]).reshape(2, D // 2, IN_SHP)
    tok = exchange.gather(("w_in",), [shard_in])
    tok = exchange.gather(("w_out", "w_gate", "w_up", "w_down"), [
        _cast_bf16("cast_w_out", w_out[0], dep=tok).reshape(2, D // NSH // 2, D),
        _cast_bf16("cast_w_gate", w_gate[0], dep=tok).reshape(2, D // 2, FSH),
        _cast_bf16("cast_w_up", w_up[0], dep=tok).reshape(2, D // 2, FSH),
        _cast_bf16("cast_w_down", w_down[0], dep=tok).reshape(2, FSH // 2, D)])
    mods = mods + tok[:1, :1]

    loss, grad_x, dmods, small = _local_step(
        x[0], loss_target[0], mods, g_mix, rel_full, convw_full, conv_b, dt_bias, a_log, d_skip, g_att_out, g_ssd_out, g_ffn,
        g_final[None, :], exchange)

    small_names = ("g_mix", "conv_b", "dt_bias", "a_log", "d_skip", "g_att_out", "g_ssd_out", "g_ffn", "g_final", "rel_bias", "conv_w")
    g3 = _allgather8("gather_small_grads", _pack([dmods] + [small[n] for n in small_names], 264))
    sizes = [6 * D] + [int(np.prod(small[n].shape)) for n in small_names]
    dmods_all = _unpack(g3, sizes)[0]
    summed = _unpack(_sum8(g3), sizes)
    grads = {"b_ada": summed[0].reshape(1, 6 * D)}
    for n, val in zip(small_names, summed[1:]):
        grads[n] = val.reshape(small[n].shape)
    grads["rel_bias"] = lax.dynamic_slice_in_dim(grads["rel_bias"], chip * REL_SH, REL_SH, axis=1)
    grads["conv_w"] = lax.dynamic_slice_in_dim(grads["conv_w"], chip * CONVW_SH, CONVW_SH, axis=1)
    grads["g_final"] = grads["g_final"].reshape(D)
    dm16 = jnp.pad(lax.dynamic_slice_in_dim(dmods_all, chip * ADA_SH, ADA_SH, axis=1), ((0, 8), (0, 0)))
    grads["w_ada"] = _grad_w_ada(cond16, dm16)

    delta, new_m, new_v = {}, {}, {}
    delta["w_ada"], new_m["w_ada"], new_v["w_ada"] = _adamw("adamw_w_ada", w_ada[0], grads["w_ada"], m_w_ada[0], v_w_ada[0])
    grads.update(exchange.finish(grad_x))
    grads["w_in"] = grads["w_in"][:, :IN_SH]
    for n in BIG:
        delta[n], new_m[n], new_v[n] = _adamw("adamw_" + n, w[n][0], grads[n], m[n][0], v[n][0])
    sw = _pack([w[n] for n in SMALL], 200)
    sg = _pack([grads[n] for n in SMALL], 200)
    sm = _pack([m[n] for n in SMALL], 200)
    sv = _pack([v[n] for n in SMALL], 200)
    ssz = [int(np.prod(w[n].shape)) for n in SMALL]
    for dst, packed in zip((delta, new_m, new_v), _adamw("adamw_small", sw, sg, sm, sv)):
        for n, val in zip(SMALL, _unpack(packed, ssz)):
            dst[n] = val

    def shaped(d, n):
        return d[n].reshape(w[n].shape)

    total = lax.psum(loss, ("x", "y", "c"))
    return (total, grad_x[None], *[shaped(grads, n) for n in ORDER], *[shaped(delta, n) for n in ORDER],
            *[shaped(new_m, n) for n in ORDER], *[shaped(new_v, n) for n in ORDER])
```

```python
import functools

import numpy as np
import jax
import jax.numpy as jnp
from jax import lax
from jax.experimental import pallas as pl
from jax.experimental.pallas import tpu as pltpu

f32 = jnp.float32
bf16 = jnp.bfloat16
HIGHEST = lax.Precision.HIGHEST
MESH = pl.DeviceIdType.MESH

D = 2048
CHUNK = 64
LEFT = 8
BAND = (LEFT + 1) * CHUNK
BANDP = 640
PADK = LEFT * CHUNK
NH = 16
HD = 64
ATT_W = NH * HD
SSD_W = 1024
NG = 2
NSTATE = 128
GW = SSD_W // NG
XBC = SSD_W + 2 * NG * NSTATE
N_REL = 320
REL_CLIP = 256
FFN = 5632
NSH = 4
FSH = FFN // NSH
IN_COLS = 5648
IN_SH = IN_COLS // NSH
IN_SHP = 1536
IN_A = 3 * ATT_W
IN_B = 2688
IN_P = IN_A + IN_B
EPS = 1e-6
N_DEV = 8

ADAM_LR = 0.001
ADAM_B1 = 0.9
ADAM_B2 = 0.999
ADAM_EPS = 1e-08
ADAM_WD = 0.01
ADAM_STEP = 10

VMEM_LIMIT = 56 * 1024 * 1024


def _params(sem):
    return pltpu.CompilerParams(dimension_semantics=sem, vmem_limit_bytes=VMEM_LIMIT)


def _sds(shape, dtype):
    return jax.ShapeDtypeStruct(shape, dtype)


def _fold8(v):
    r, w = v.shape
    return jnp.sum(v.reshape(r // 8, 8, w), axis=0)


STRIP = 16


def _strips(tm, fn):
    def step(j, carry):
        fn(pl.ds(pl.multiple_of(j * STRIP, STRIP), STRIP))
        return carry
    lax.fori_loop(0, tm // STRIP, step, 0, unroll=4)


def _sigmoid(v):
    return 1.0 / (1.0 + jnp.exp(-v))


def _softplus(v):
    return jnp.maximum(v, 0.0) + jnp.log(1.0 + jnp.exp(-jnp.abs(v)))


def _dot(a, b, ta=False, tb=False):
    dn = (((0 if ta else 1,), (1 if tb else 0,)), ((), ()))
    return lax.dot_general(a.astype(bf16), b.astype(bf16), dn, preferred_element_type=f32)


def _dep_args(dep, ngrid):
    if dep is None:
        return [], []
    return [pl.BlockSpec((8, 128), lambda *_: (0, 0))], [dep]


def _dot01(a, b, ta=False, tb=False, exact="b"):
    dn = (((0 if ta else 1,), (1 if tb else 0,)), ((), ()))
    x = a if exact == "b" else b
    hi = x.astype(bf16)
    r = x - hi.astype(f32)
    mid = r.astype(bf16)
    lo = (r - mid.astype(f32)).astype(bf16)
    if exact == "b":
        m = b.astype(bf16)
        return sum(lax.dot_general(p, m, dn, preferred_element_type=f32) for p in (hi, mid, lo))
    m = a.astype(bf16)
    return sum(lax.dot_general(m, p, dn, preferred_element_type=f32) for p in (hi, mid, lo))


def _matmul(name, a, b, *, grid, a_spec, b_spec, o_spec, o_shape, o_dtype, acc_shape, ta=False, tb=False, dep=None):
    nk = grid[2]
    dep_specs, dep_ops = _dep_args(dep, 3)

    def body(a_ref, b_ref, *rest):
        o_ref, acc_ref = rest[-2:]
        p = _dot(a_ref[...], b_ref[...], ta, tb)
        if nk == 1:
            o_ref[...] = p.astype(o_ref.dtype)
        else:
            k = pl.program_id(2)

            @pl.when(k == 0)
            def _():
                acc_ref[...] = p

            @pl.when(jnp.logical_and(k > 0, k < nk - 1))
            def _():
                acc_ref[...] += p

            @pl.when(k == nk - 1)
            def _():
                o_ref[...] = (acc_ref[...] + p).astype(o_ref.dtype)

    return pl.pallas_call(
        body, name=name, grid=grid, in_specs=[a_spec, b_spec] + dep_specs, out_specs=o_spec,
        out_shape=_sds(o_shape, o_dtype), scratch_shapes=[pltpu.VMEM(acc_shape if nk > 1 else (8, 128), f32)],
        compiler_params=_params(("parallel", "parallel", "arbitrary")),
    )(a, b, *dep_ops)


def _mm_nn_fullk(name, a, b, tm, tn, o_dtype, n=None):
    m, k = a.shape
    n = b.shape[1] if n is None else n
    return _matmul(name, a, b, grid=(m // tm, n // tn, 1),
                   a_spec=pl.BlockSpec((tm, k), lambda i, j, kk: (i, 0)),
                   b_spec=pl.BlockSpec((k, tn), lambda i, j, kk: (0, j)),
                   o_spec=pl.BlockSpec((tm, tn), lambda i, j, kk: (i, j)),
                   o_shape=(m, n), o_dtype=o_dtype, acc_shape=(tm, tn))


def _mm_nt(name, a, b, tm, tn, tk, o_dtype, dep=None):
    m, k = a.shape
    n = b.shape[0]
    return _matmul(name, a, b, grid=(m // tm, n // tn, k // tk), tb=True, dep=dep,
                   a_spec=pl.BlockSpec((tm, tk), lambda i, j, kk: (i, kk)),
                   b_spec=pl.BlockSpec((tn, tk), lambda i, j, kk: (j, kk)),
                   o_spec=pl.BlockSpec((tm, tn), lambda i, j, kk: (i, j)),
                   o_shape=(m, n), o_dtype=o_dtype, acc_shape=(tm, tn))


def _mm_tn(name, a, b, tm, tn, tk, o_dtype):
    k, m = a.shape
    n = b.shape[1]
    return _matmul(name, a, b, grid=(m // tm, n // tn, k // tk), ta=True,
                   a_spec=pl.BlockSpec((tk, tm), lambda i, j, kk: (kk, i)),
                   b_spec=pl.BlockSpec((tk, tn), lambda i, j, kk: (kk, j)),
                   o_spec=pl.BlockSpec((tm, tn), lambda i, j, kk: (i, j)),
                   o_shape=(m, n), o_dtype=o_dtype, acc_shape=(tm, tn))


FSH_PARTS = (slice(0, 640), slice(640, FSH))


def _ffn_up(h2b, wg4, wu4, tm):
    s = h2b.shape[0]

    def body(h_ref, wg_ref, wu_ref, a_ref, s_ref, ud_ref):
        h = h_ref[...]
        for cols in FSH_PARTS:
            g = _dot(h, wg_ref[:, cols])
            u = _dot(h, wu_ref[:, cols])
            sg = _sigmoid(g)
            sil = g * sg
            a_ref[:, cols] = (sil * u).astype(bf16)
            s_ref[:, cols] = sil.astype(bf16)
            ud_ref[:, cols] = (u * (sg * (1.0 + g * (1.0 - sg)))).astype(bf16)

    wspec = pl.BlockSpec((None, D, FSH), lambda k, i: (k, 0, 0))
    ospec = pl.BlockSpec((tm, FSH), lambda k, i: (i, k))
    return pl.pallas_call(
        body, name="ffn_up", grid=(NSH, s // tm),
        in_specs=[pl.BlockSpec((tm, D), lambda k, i: (i, 0)), wspec, wspec],
        out_specs=[ospec, ospec, ospec], out_shape=[_sds((s, FFN), bf16)] * 3,
        compiler_params=_params(("parallel", "parallel")),
    )(h2b, wg4, wu4)


def _ffn_down(act, wd4, tm):
    s = act.shape[0]
    tn = D // 2

    def body(a_ref, b_ref, o_ref):
        o_ref[...] = jnp.dot(a_ref[...], b_ref[...].reshape(FFN, tn), preferred_element_type=f32)

    return pl.pallas_call(
        body, name="ffn_down", grid=(s // tm, D // tn),
        in_specs=[pl.BlockSpec((tm, FFN), lambda i, j: (i, 0)), pl.BlockSpec((NSH, FSH, tn), lambda i, j: (0, 0, j))],
        out_specs=pl.BlockSpec((tm, tn), lambda i, j: (i, j)), out_shape=_sds((s, D), f32),
        compiler_params=_params(("parallel", "parallel")),
    )(act, wd4)


def _ffn_dact(dffn, wd4, sil, ud, tm, dep=None):
    s = dffn.shape[0]
    dep_specs, dep_ops = _dep_args(dep, 2)

    def body(d_ref, w_ref, s_ref, ud_ref, *rest):
        dg_ref, du_ref = rest[-2:]
        d = d_ref[...]
        for cols in FSH_PARTS:
            dact = _dot(d, w_ref[cols, :], tb=True)
            dg_ref[:, cols] = (dact * ud_ref[:, cols].astype(f32)).astype(bf16)
            du_ref[:, cols] = (dact * s_ref[:, cols].astype(f32)).astype(bf16)

    blk = pl.BlockSpec((tm, FSH), lambda k, i: (i, k))
    return pl.pallas_call(
        body, name="ffn_dact", grid=(NSH, s // tm),
        in_specs=[pl.BlockSpec((tm, D), lambda k, i: (i, 0)), pl.BlockSpec((None, FSH, D), lambda k, i: (k, 0, 0)), blk, blk] + dep_specs,
        out_specs=[blk, blk], out_shape=[_sds((s, FFN), bf16), _sds((s, FFN), bf16)],
        compiler_params=_params(("parallel", "parallel")),
    )(dffn, wd4, sil, ud, *dep_ops)


def _ffn_dh(dgate, dup, wg4, wu4, tm, dep=None):
    s = dgate.shape[0]
    dep_specs, dep_ops = _dep_args(dep, 2)

    def body(dg_ref, du_ref, wg_ref, wu_ref, *rest):
        o_ref, acc_ref = rest[-2:]
        k = pl.program_id(1)
        p = _dot(dg_ref[...], wg_ref[...], tb=True) + _dot(du_ref[...], wu_ref[...], tb=True)

        @pl.when(k == 0)
        def _():
            acc_ref[...] = p

        @pl.when(jnp.logical_and(k > 0, k < NSH - 1))
        def _():
            acc_ref[...] += p

        @pl.when(k == NSH - 1)
        def _():
            o_ref[...] = acc_ref[...] + p

    aspec = pl.BlockSpec((tm, FSH), lambda i, k: (i, k))
    wspec = pl.BlockSpec((None, D, FSH), lambda i, k: (k, 0, 0))
    return pl.pallas_call(
        body, name="ffn_dh", grid=(s // tm, NSH), in_specs=[aspec, aspec, wspec, wspec] + dep_specs,
        out_specs=pl.BlockSpec((tm, D), lambda i, k: (i, 0)), out_shape=_sds((s, D), f32),
        scratch_shapes=[pltpu.VMEM((tm, D), f32)], compiler_params=_params(("parallel", "arbitrary")),
    )(dgate, dup, wg4, wu4, *dep_ops)


def _grad_cols4(name, h, dy, tm, tk):
    s = h.shape[0]
    return _matmul(name, h, dy, grid=(NSH, D // tm, s // tk), ta=True,
                   a_spec=pl.BlockSpec((tk, tm), lambda k, i, kk: (kk, i)),
                   b_spec=pl.BlockSpec((tk, FSH), lambda k, i, kk: (kk, k)),
                   o_spec=pl.BlockSpec((None, tm, FSH), lambda k, i, kk: (k, i, 0)),
                   o_shape=(NSH, D, FSH), o_dtype=bf16, acc_shape=(tm, FSH))


def _grad_wdown4(act, dffn, tn, tk):
    s = act.shape[0]
    return _matmul("grad_w_down", act, dffn, grid=(NSH, D // tn, s // tk), ta=True,
                   a_spec=pl.BlockSpec((tk, FSH), lambda k, j, kk: (kk, k)),
                   b_spec=pl.BlockSpec((tk, tn), lambda k, j, kk: (kk, j)),
                   o_spec=pl.BlockSpec((None, FSH, tn), lambda k, j, kk: (k, 0, j)),
                   o_shape=(NSH, FSH, D), o_dtype=bf16, acc_shape=(FSH, tn))


def _row_spec(w):
    return pl.BlockSpec((1, w), lambda i: (0, 0))


def _tile_spec(tm, w, col=0):
    return pl.BlockSpec((tm, w), lambda i: (i, col))


def _norm_mod(name, x, g, sc, sh, tm):
    s = x.shape[0]

    def body(x_ref, g_ref, sc_ref, sh_ref, o_ref):
        def strip(rows):
            xv = x_ref[rows, :]
            r = lax.rsqrt(jnp.mean(xv * xv, axis=-1, keepdims=True) + EPS)
            o_ref[rows, :] = (xv * r * g_ref[...] * (1.0 + sc_ref[...]) + sh_ref[...]).astype(bf16)

        _strips(tm, strip)

    return pl.pallas_call(
        body, name=name, grid=(s // tm,), in_specs=[_tile_spec(tm, D), _row_spec(D), _row_spec(D), _row_spec(D)],
        out_specs=_tile_spec(tm, D), out_shape=_sds((s, D), bf16), compiler_params=_params(("parallel",)),
    )(x, g, sc, sh)


def _resid_norm_mod(x, gt, mix, g, sc, sh, tm):
    s = x.shape[0]

    def body(x_ref, gt_ref, m_ref, g_ref, sc_ref, sh_ref, x2_ref, h_ref):
        def strip(rows):
            xv = x_ref[rows, :] + gt_ref[...] * m_ref[rows, :]
            x2_ref[rows, :] = xv
            r = lax.rsqrt(jnp.mean(xv * xv, axis=-1, keepdims=True) + EPS)
            h_ref[rows, :] = (xv * r * g_ref[...] * (1.0 + sc_ref[...]) + sh_ref[...]).astype(bf16)

        _strips(tm, strip)

    return pl.pallas_call(
        body, name="resid_norm_mod", grid=(s // tm,),
        in_specs=[_tile_spec(tm, D), _row_spec(D), _tile_spec(tm, D), _row_spec(D), _row_spec(D), _row_spec(D)],
        out_specs=[_tile_spec(tm, D), _tile_spec(tm, D)], out_shape=[_sds((s, D), f32), _sds((s, D), bf16)],
        compiler_params=_params(("parallel",)),
    )(x, gt, mix, g, sc, sh)


def _final_fwd_bwd(x2, ffn, gt2, g, tgt, tm):
    s = x2.shape[0]
    n = s // tm

    def body(x_ref, f_ref, gt_ref, g_ref, t_ref, dx_ref, df_ref, loss_ref, dg_ref, dgt_ref, a_loss, a_dg, a_dgt):
        i = pl.program_id(0)

        @pl.when(i == 0)
        def _():
            a_loss[...] = jnp.zeros_like(a_loss)
            a_dg[...] = jnp.zeros_like(a_dg)
            a_dgt[...] = jnp.zeros_like(a_dgt)

        def strip(rows):
            fv = f_ref[rows, :]
            gt = gt_ref[...]
            gv = g_ref[...]
            xv = x_ref[rows, :] + gt * fv
            r = lax.rsqrt(jnp.mean(xv * xv, axis=-1, keepdims=True) + EPS)
            xh = xv * r
            e = xh * gv - t_ref[rows, :]
            a_loss[...] += _fold8(e * e)
            dy = e * (1.0 / D)
            a_dg[...] += _fold8(dy * xh)
            t = dy * gv
            dx = r * (t - xh * jnp.mean(t * xh, axis=-1, keepdims=True))
            dx_ref[rows, :] = dx
            a_dgt[...] += _fold8(dx * fv)
            df_ref[rows, :] = (dx * gt).astype(bf16)

        _strips(tm, strip)

        @pl.when(i == n - 1)
        def _():
            tot = jnp.sum(jnp.sum(a_loss[...], axis=0, keepdims=True), axis=1, keepdims=True) * (0.5 / D)
            loss_ref[...] = jnp.broadcast_to(tot, (1, 128))
            dg_ref[...] = jnp.sum(a_dg[...], axis=0, keepdims=True)
            dgt_ref[...] = jnp.sum(a_dgt[...], axis=0, keepdims=True)

    return pl.pallas_call(
        body, name="final_fwd_bwd", grid=(n,),
        in_specs=[_tile_spec(tm, D), _tile_spec(tm, D), _row_spec(D), _row_spec(D), _tile_spec(tm, D)],
        out_specs=[_tile_spec(tm, D), _tile_spec(tm, D), _row_spec(128), _row_spec(D), _row_spec(D)],
        out_shape=[_sds((s, D), f32), _sds((s, D), bf16), _sds((1, 128), f32), _sds((1, D), f32), _sds((1, D), f32)],
        scratch_shapes=[pltpu.VMEM((8, D), f32)] * 3, compiler_params=_params(("arbitrary",)),
    )(x2, ffn, gt2, g, tgt)


def _norm_mod_bwd(name, dh, xin, g, sc, dres, tm, mix=None, gt=None):
    s = dh.shape[0]
    n = s // tm
    with_mix = mix is not None

    def body(*refs):
        if with_mix:
            dh_ref, x_ref, g_ref, sc_ref, dr_ref, m_ref, gt_ref, dx_ref, dm_ref, dsc_ref, dsh_ref, dg_ref, dgt_ref, a_sc, a_sh, a_g, a_gt = refs
        else:
            dh_ref, x_ref, g_ref, sc_ref, dr_ref, dx_ref, dsc_ref, dsh_ref, dg_ref, a_sc, a_sh, a_g = refs
        i = pl.program_id(0)

        @pl.when(i == 0)
        def _():
            a_sc[...] = jnp.zeros_like(a_sc)
            a_sh[...] = jnp.zeros_like(a_sh)
            a_g[...] = jnp.zeros_like(a_g)
            if with_mix:
                a_gt[...] = jnp.zeros_like(a_gt)

        def strip(rows):
            dh = dh_ref[rows, :]
            xv = x_ref[rows, :]
            gv = g_ref[...]
            r = lax.rsqrt(jnp.mean(xv * xv, axis=-1, keepdims=True) + EPS)
            xh = xv * r
            a_sc[...] += _fold8(dh * xh * gv)
            a_sh[...] += _fold8(dh)
            dn = dh * (1.0 + sc_ref[...])
            a_g[...] += _fold8(dn * xh)
            t = dn * gv
            dx = dr_ref[rows, :] + r * (t - xh * jnp.mean(t * xh, axis=-1, keepdims=True))
            dx_ref[rows, :] = dx
            if with_mix:
                a_gt[...] += _fold8(dx * m_ref[rows, :])
                dm_ref[rows, :] = (dx * gt_ref[...]).astype(bf16)

        _strips(tm, strip)

        @pl.when(i == n - 1)
        def _():
            dsc_ref[...] = jnp.sum(a_sc[...], axis=0, keepdims=True)
            dsh_ref[...] = jnp.sum(a_sh[...], axis=0, keepdims=True)
            dg_ref[...] = jnp.sum(a_g[...], axis=0, keepdims=True)
            if with_mix:
                dgt_ref[...] = jnp.sum(a_gt[...], axis=0, keepdims=True)

    tile, row = _tile_spec(tm, D), _row_spec(D)
    if with_mix:
        ins, args = [tile, tile, row, row, tile, tile, row], (dh, xin, g, sc, dres, mix, gt)
        outs = [tile, tile, row, row, row, row]
        shapes = [_sds((s, D), f32), _sds((s, D), bf16)] + [_sds((1, D), f32)] * 4
        nacc = 4
    else:
        ins, args = [tile, tile, row, row, tile], (dh, xin, g, sc, dres)
        outs = [tile, row, row, row]
        shapes = [_sds((s, D), f32)] + [_sds((1, D), f32)] * 3
        nacc = 3
    return pl.pallas_call(
        body, name=name, grid=(n,), in_specs=ins, out_specs=outs, out_shape=shapes,
        scratch_shapes=[pltpu.VMEM((8, D), f32)] * nacc, compiler_params=_params(("arbitrary",)),
    )(*args)


def _mix_pre(att, y, proj2, g_att, g_ssd, tm):
    s = att.shape[0]

    def body(a_ref, y_ref, z_ref, ga_ref, gs_ref, o_ref):
        def strip(rows):
            a = a_ref[rows, :]
            ra = lax.rsqrt(jnp.mean(a * a, axis=-1, keepdims=True) + EPS)
            o_ref[rows, 0:ATT_W] = (a * ra * ga_ref[...]).astype(bf16)
            z = z_ref[rows, :]
            u = y_ref[rows, :] * (z * _sigmoid(z))
            ru = lax.rsqrt(jnp.mean(u * u, axis=-1, keepdims=True) + EPS)
            o_ref[rows, ATT_W:] = (u * ru * gs_ref[...]).astype(bf16)

        _strips(tm, strip)

    t = _tile_spec(tm, ATT_W)
    return pl.pallas_call(
        body, name="mix_pre", grid=(s // tm,), in_specs=[t, t, t, _row_spec(ATT_W), _row_spec(SSD_W)],
        out_specs=_tile_spec(tm, D), out_shape=_sds((s, D), bf16), compiler_params=_params(("parallel",)),
    )(att, y, proj2, g_att, g_ssd)


def _mix_pre_bwd(dmc, att, y, proj2, g_att, g_ssd, tm):
    s = att.shape[0]
    n = s // tm

    def body(da_ref, ds_ref, a_ref, y_ref, z_ref, ga_ref, gs_ref, datt_ref, dy_ref, dz_ref, dga_ref, dgs_ref, acc_a, acc_s):
        i = pl.program_id(0)

        @pl.when(i == 0)
        def _():
            acc_a[...] = jnp.zeros_like(acc_a)
            acc_s[...] = jnp.zeros_like(acc_s)

        def strip(rows):
            a = a_ref[rows, :]
            ra = lax.rsqrt(jnp.mean(a * a, axis=-1, keepdims=True) + EPS)
            ah = a * ra
            dan = da_ref[rows, :]
            acc_a[...] += _fold8(dan * ah)
            t = dan * ga_ref[...]
            datt_ref[rows, :] = (ra * (t - ah * jnp.mean(t * ah, axis=-1, keepdims=True))).astype(bf16)
            z = z_ref[rows, :]
            yv = y_ref[rows, :]
            sz = _sigmoid(z)
            sil = z * sz
            u = yv * sil
            ru = lax.rsqrt(jnp.mean(u * u, axis=-1, keepdims=True) + EPS)
            uh = u * ru
            dsn = ds_ref[rows, :]
            acc_s[...] += _fold8(dsn * uh)
            t2 = dsn * gs_ref[...]
            du = ru * (t2 - uh * jnp.mean(t2 * uh, axis=-1, keepdims=True))
            dy_ref[rows, :] = du * sil
            dz_ref[rows, :] = (du * yv * (sz * (1.0 + z * (1.0 - sz)))).astype(bf16)

        _strips(tm, strip)

        @pl.when(i == n - 1)
        def _():
            dga_ref[...] = jnp.sum(acc_a[...], axis=0, keepdims=True)
            dgs_ref[...] = jnp.sum(acc_s[...], axis=0, keepdims=True)

    t = _tile_spec(tm, ATT_W)
    row = _row_spec(ATT_W)
    return pl.pallas_call(
        body, name="mix_pre_bwd", grid=(n,),
        in_specs=[_tile_spec(tm, ATT_W, 0), _tile_spec(tm, ATT_W, 1), t, t, t, row, row],
        out_specs=[t, t, t, row, row],
        out_shape=[_sds((s, ATT_W), bf16), _sds((s, SSD_W), f32), _sds((s, SSD_W), bf16), _sds((1, ATT_W), f32), _sds((1, SSD_W), f32)],
        scratch_shapes=[pltpu.VMEM((8, ATT_W), f32)] * 2, compiler_params=_params(("arbitrary",)),
    )(dmc, dmc, att, y, proj2, g_att, g_ssd)


ATT_GROUP = 8
ATT_GROUP_FWD = 16


def _pair_rows(qc):
    two = jnp.concatenate([qc, qc], axis=0)
    r = lax.broadcasted_iota(jnp.int32, (2 * CHUNK, 128), 0)
    l = lax.broadcasted_iota(jnp.int32, (2 * CHUNK, 128), 1)
    return jnp.where((r < CHUNK) == (l < HD), two, jnp.zeros_like(two))


def _scaled(q):
    return q * jnp.asarray(HD ** -0.5, q.dtype)


def _pair_scores(wt, kb, bias, r0, masked):
    sc = lax.dot_general(wt, kb, (((1,), (1,)), ((), ())), preferred_element_type=f32) + bias
    if not masked:
        return sc
    kidx = lax.broadcasted_iota(jnp.int32, sc.shape, 1)
    return jnp.where(r0 + kidx >= PADK, sc, -jnp.inf)


def _softmax(sc, axis):
    e = jnp.exp(sc - jnp.max(sc, axis=axis, keepdims=True))
    return e * (1.0 / jnp.sum(e, axis=axis, keepdims=True))


def _chunk_loops(nc, group, per_trip):
    n_masked = min(-(-LEFT // per_trip), nc // per_trip)

    def run(masked):
        def step(g, carry):
            group(g, masked)
            return carry
        return step

    lax.fori_loop(0, n_masked, run(True), 0)
    lax.fori_loop(n_masked, nc // per_trip, run(False), 0)


def _pair_diag(r):
    lane = lax.broadcasted_iota(jnp.int32, (CHUNK, 128), 1)
    return jnp.where(lane < HD, r[0:CHUNK], r[CHUNK:])


def _pad_keys(k_ref, kp, s):
    kp[0:PADK, :] = jnp.zeros((PADK, 128), bf16)
    kp[PADK:PADK + s, :] = k_ref[...]
    kp[PADK + s:, :] = jnp.zeros((CHUNK, 128), bf16)


def _attn_fwd(qkv, bias2):
    s = qkv.shape[0]
    nc = s // CHUNK
    npair = NH // 2
    per_trip = min(ATT_GROUP_FWD, nc)

    def body(q_ref, k_ref, v_ref, b_ref, o_ref, kp, vp):
        _pad_keys(k_ref, kp, s)
        _pad_keys(v_ref, vp, s)

        def group(g, masked):
            r0s = [pl.multiple_of((g * per_trip + u) * CHUNK, CHUNK) for u in range(per_trip)]
            scs = [_pair_scores(_pair_rows(_scaled(q_ref[pl.ds(r0, CHUNK), :])), kp[pl.ds(r0, BANDP), :], b_ref[...], r0, masked)
                   for r0 in r0s]
            ps = [_softmax(sc, -1).astype(bf16) for sc in scs]
            for r0, p in zip(r0s, ps):
                o_ref[pl.ds(r0, CHUNK), :] = _pair_diag(jnp.dot(p, vp[pl.ds(r0, BANDP), :], preferred_element_type=f32))

        _chunk_loops(nc, group, per_trip)

    return pl.pallas_call(
        body, name="attn_fwd", grid=(npair,),
        in_specs=[pl.BlockSpec((s, 128), lambda p: (0, p)), pl.BlockSpec((s, 128), lambda p: (0, npair + p)),
                  pl.BlockSpec((s, 128), lambda p: (0, 2 * npair + p)), pl.BlockSpec((None, 2 * CHUNK, BANDP), lambda p: (p, 0, 0))],
        out_specs=pl.BlockSpec((s, 128), lambda p: (0, p)), out_shape=_sds((s, ATT_W), f32),
        scratch_shapes=[pltpu.VMEM((PADK + s + CHUNK, 128), bf16)] * 2, compiler_params=_params(("parallel",)),
    )(qkv, qkv, qkv, bias2)


def _attn_bwd(qkv, datt, bias2):
    s = qkv.shape[0]
    nc = s // CHUNK
    npair = NH // 2
    rows = PADK + s + CHUNK
    nt = (((1,), (1,)), ((), ()))

    def body(q_ref, k_ref, v_ref, do_ref, b_ref, dq_ref, dk_ref, dv_ref, g_ref, kp, vp, dkp, dvp):
        _pad_keys(k_ref, kp, s)
        _pad_keys(v_ref, vp, s)
        dkp[...] = jnp.zeros_like(dkp)
        dvp[...] = jnp.zeros_like(dvp)
        g_ref[...] = jnp.zeros_like(g_ref)

        def group(g, masked):
            r0s = [pl.multiple_of((g * ATT_GROUP + u) * CHUNK, CHUNK) for u in range(ATT_GROUP)]
            wts = [_pair_rows(_scaled(q_ref[pl.ds(r0, CHUNK), :])) for r0 in r0s]
            dos = [_pair_rows(do_ref[pl.ds(r0, CHUNK), :]) for r0 in r0s]
            scs = [_pair_scores(wt, kp[pl.ds(r0, BANDP), :], b_ref[...], r0, masked) for wt, r0 in zip(wts, r0s)]
            dps = [lax.dot_general(do, vp[pl.ds(r0, BANDP), :], nt, preferred_element_type=f32) for do, r0 in zip(dos, r0s)]
            tn_ = (((0,), (0,)), ((), ()))
            for r0, wt, do, sc, dp in zip(r0s, wts, dos, scs, dps):
                p = _softmax(sc, -1)
                ds = p * (dp - jnp.sum(p * dp, axis=-1, keepdims=True))
                g_ref[...] += ds
                dsb = ds.astype(bf16)
                dq = jnp.dot(dsb, kp[pl.ds(r0, BANDP), :], preferred_element_type=f32)
                dq_ref[pl.ds(r0, CHUNK), :] = (_pair_diag(dq) * (HD ** -0.5)).astype(bf16)
                dkp[pl.ds(r0, BANDP), :] += lax.dot_general(dsb, wt, tn_, preferred_element_type=f32)
                dvp[pl.ds(r0, BANDP), :] += lax.dot_general(p.astype(bf16), do, tn_, preferred_element_type=f32)

        _chunk_loops(nc, group, ATT_GROUP)
        dk_ref[...] = dkp[PADK:PADK + s, :].astype(bf16)
        dv_ref[...] = dvp[PADK:PADK + s, :].astype(bf16)

    col = lambda off: pl.BlockSpec((s, 128), lambda p: (0, off + p))
    return pl.pallas_call(
        body, name="attn_bwd", grid=(npair,),
        in_specs=[col(0), col(npair), col(2 * npair), col(0), pl.BlockSpec((None, 2 * CHUNK, BANDP), lambda p: (p, 0, 0))],
        out_specs=[col(0), col(0), col(0), pl.BlockSpec((None, 2 * CHUNK, BANDP), lambda p: (p, 0, 0))],
        out_shape=[_sds((s, ATT_W), bf16)] * 3 + [_sds((npair, 2 * CHUNK, BANDP), f32)],
        scratch_shapes=[pltpu.VMEM((rows, 128), bf16)] * 2 + [pltpu.VMEM((rows, 128), f32)] * 2,
        compiler_params=_params(("parallel",)),
    )(qkv, qkv, qkv, datt, bias2)


def _rel_tables():
    onehot = np.zeros((BANDP, N_REL), np.float32)
    for j in range(BAND + CHUNK - 1):
        o = j - (CHUNK - 1)
        onehot[j, int(np.clip(PADK - o, -(CHUNK - 1), REL_CLIP)) + CHUNK - 1] = 1.0
    return onehot, np.ascontiguousarray(np.eye(CHUNK, dtype=np.float32)[::-1])


def _expand_bias(rel):
    ext = jnp.concatenate([jnp.broadcast_to(rel[:, N_REL - 1:], (NH, N_REL - 1)), rel[:, ::-1],
                           jnp.zeros((NH, BANDP - BAND + 1), f32)], axis=1)
    band = jnp.stack([ext[:, CHUNK - 1 - q:CHUNK - 1 - q + BANDP] for q in range(CHUNK)], axis=1)
    band = jnp.where(np.arange(BANDP) < BAND, band, -jnp.inf)
    return band.reshape(NH // 2, 2 * CHUNK, BANDP)


def _rel_bias_grad(gband):
    def body(g_ref, m_ref, flip_ref, o_ref, d2):
        for h in range(NH):
            rev = jnp.dot(flip_ref[...], g_ref[h], precision=HIGHEST, preferred_element_type=f32)
            rolled = pltpu.roll(rev, 0, 1, stride=1, stride_axis=0)
            d2[h:h + 1, :] = jnp.sum(rolled, axis=0, keepdims=True)
        o_ref[...] = jnp.dot(d2[...], m_ref[...], precision=HIGHEST, preferred_element_type=f32)

    onehot, flip = _rel_tables()
    return pl.pallas_call(
        body, name="rel_bias_grad", out_shape=_sds((NH, N_REL), f32), scratch_shapes=[pltpu.VMEM((NH, BANDP), f32)],
    )(gband, jnp.asarray(onehot), jnp.asarray(flip))


XBC_BLK = 512
XBC_COL0 = SSD_W // XBC_BLK
DT_COL = (SSD_W + XBC) // 128


def _conv_taps(ext, w_ref, b_ref, tm):
    n = ext.shape[0]
    pre = w_ref[3:4, :] * ext + b_ref[...]
    for j in range(3):
        pre = pre + w_ref[j:j + 1, :] * pltpu.roll(ext, 3 - j, 0)
    return pre


def _ssd_conv(proj2, conv_w, conv_b, tm):
    s = proj2.shape[0]
    nb = XBC // XBC_BLK

    def body(x_ref, p_ref, w_ref, b_ref, o_ref):
        i = pl.program_id(1)
        prev = jnp.where(i > 0, p_ref[...], 0.0)
        ext = jnp.concatenate([prev, x_ref[...]], axis=0)
        pre = _conv_taps(ext, w_ref, b_ref, tm)[8:8 + tm]
        o_ref[...] = pre * _sigmoid(pre)

    return pl.pallas_call(
        body, name="ssd_conv", grid=(nb, s // tm),
        in_specs=[pl.BlockSpec((tm, XBC_BLK), lambda j, i: (i, XBC_COL0 + j)),
                  pl.BlockSpec((8, XBC_BLK), lambda j, i: (jnp.maximum(i * (tm // 8) - 1, 0), XBC_COL0 + j)),
                  pl.BlockSpec((4, XBC_BLK), lambda j, i: (0, j)), pl.BlockSpec((1, XBC_BLK), lambda j, i: (0, j))],
        out_specs=pl.BlockSpec((tm, XBC_BLK), lambda j, i: (i, j)), out_shape=_sds((s, XBC), f32),
        compiler_params=_params(("parallel", "parallel")),
    )(proj2, proj2, conv_w, conv_b)


def _ssd_conv_bwd(dxbc, proj2, conv_w, conv_b, tm):
    s = proj2.shape[0]
    nb = XBC // XBC_BLK
    n = s // tm
    last8 = s // 8 - 1

    def body(x_ref, xp_ref, xn_ref, d_ref, dn_ref, w_ref, b_ref, o_ref, dw_ref, db_ref, acc):
        i = pl.program_id(1)

        @pl.when(i == 0)
        def _():
            acc[...] = jnp.zeros_like(acc)

        prev = jnp.where(i > 0, xp_ref[...], 0.0)
        ext = jnp.concatenate([prev, x_ref[...], xn_ref[...]], axis=0)
        pre = _conv_taps(ext, w_ref, b_ref, tm)
        sg = _sigmoid(pre)
        dnext = jnp.where(i < n - 1, dn_ref[...], 0.0)
        dext = jnp.concatenate([jnp.zeros((8, XBC_BLK), f32), d_ref[...], dnext], axis=0)
        dpre = dext * (sg * (1.0 + pre * (1.0 - sg)))
        rows = tm + 16
        dx = w_ref[3:4, :] * dpre
        for j in range(3):
            dx = dx + w_ref[j:j + 1, :] * pltpu.roll(dpre, rows - (3 - j), 0)
        o_ref[...] = dx[8:8 + tm].astype(bf16)
        dcur = dpre[8:8 + tm]
        acc[4] += _fold8(dcur)
        acc[3] += _fold8(dcur * ext[8:8 + tm])
        for j in range(3):
            acc[j] += _fold8(dcur * pltpu.roll(ext, 3 - j, 0)[8:8 + tm])

        @pl.when(i == n - 1)
        def _():
            for j in range(4):
                dw_ref[j:j + 1, :] = jnp.sum(acc[j], axis=0, keepdims=True)
            db_ref[...] = jnp.sum(acc[4], axis=0, keepdims=True)

    xcol = lambda j: XBC_COL0 + j
    return pl.pallas_call(
        body, name="ssd_conv_bwd", grid=(nb, n),
        in_specs=[pl.BlockSpec((tm, XBC_BLK), lambda j, i: (i, xcol(j))),
                  pl.BlockSpec((8, XBC_BLK), lambda j, i: (jnp.maximum(i * (tm // 8) - 1, 0), xcol(j))),
                  pl.BlockSpec((8, XBC_BLK), lambda j, i: (jnp.minimum((i + 1) * (tm // 8), last8), xcol(j))),
                  pl.BlockSpec((tm, XBC_BLK), lambda j, i: (i, j)),
                  pl.BlockSpec((8, XBC_BLK), lambda j, i: (jnp.minimum((i + 1) * (tm // 8), last8), j)),
                  pl.BlockSpec((4, XBC_BLK), lambda j, i: (0, j)), pl.BlockSpec((1, XBC_BLK), lambda j, i: (0, j))],
        out_specs=[pl.BlockSpec((tm, XBC_BLK), lambda j, i: (i, j)), pl.BlockSpec((4, XBC_BLK), lambda j, i: (0, j)),
                   pl.BlockSpec((1, XBC_BLK), lambda j, i: (0, j))],
        out_shape=[_sds((s, XBC), bf16), _sds((4, XBC), f32), _sds((1, XBC), f32)],
        scratch_shapes=[pltpu.VMEM((5, 8, XBC_BLK), f32)], compiler_params=_params(("parallel", "arbitrary")),
    )(proj2, proj2, proj2, dxbc, dxbc, conv_w, conv_b)


def _ssd_consts():
    ex = np.zeros((128, SSD_W), np.float32)
    for h in range(NH):
        ex[h, h * HD:(h + 1) * HD] = 1.0
    sel = np.zeros((8, 128), np.float32)
    for h in range(NH):
        sel[h // 2, h] = 1.0
    par = np.zeros((128, 128), np.float32)
    for r in range(128):
        for h in range(NH):
            par[r, h] = 1.0 if (h % 2) == (r // 64) else 0.0
    ones_blk = np.zeros((128, 128), np.float32)
    for r in range(128):
        ones_blk[r, (r // 64) * 64:(r // 64) * 64 + 64] = 1.0
    return ex, np.ascontiguousarray(ex.T), sel, par, ones_blk


SSD_SUB = 8


def _ssd_common(rs, xbc_ref, dtr_ref, a_ref, dtb_ref, ex_ref, sel_ref, par_ref):
    xs = xbc_ref[rs, 0:SSD_W]
    dt = _softplus(dtr_ref[rs, :] + dtb_ref[...])
    adt = dt * a_ref[...]
    r_i = lax.broadcasted_iota(jnp.int32, (CHUNK, CHUNK), 0)
    c_i = lax.broadcasted_iota(jnp.int32, (CHUNK, CHUNK), 1)
    tril = (r_i >= c_i).astype(f32)
    cs = _dot01(tril, adt, exact="a")
    cs2 = jnp.concatenate([cs, cs], axis=0) * par_ref[...]
    cstp = _dot01(sel_ref[...], cs2, tb=True, exact="a")
    both = _dot01(jnp.concatenate([dt, cs], axis=0), ex_ref[...])
    return xs, dt, cs, cstp, both[0:CHUNK], both[CHUNK:]


def _pair_mask():
    l_i = lax.broadcasted_iota(jnp.int32, (CHUNK, 128), 0)
    lane = lax.broadcasted_iota(jnp.int32, (CHUNK, 128), 1)
    return l_i >= (lane % CHUNK), lane < HD


def _block_diag(xp, first):
    z = jnp.zeros_like(xp)
    return jnp.concatenate([jnp.where(first, xp, z), jnp.where(first, z, xp)], axis=0)


def _ssd_fwd(xbc, proj2, a_row, dtb_row, dsk_full):
    s = xbc.shape[0]
    nc = s // CHUNK
    ex, ext, sel, par, ones_blk = _ssd_consts()

    def one_chunk(sub, states, refs):
        xbc_ref, dtr_ref, a_ref, dtb_ref, dsk_ref, ex_ref, sel_ref, par_ref, y_ref, hs_ref = refs
        rs = slice(sub * CHUNK, (sub + 1) * CHUNK)
        xs, dt, cs, cstp, dt_full, cs_full = _ssd_common(rs, xbc_ref, dtr_ref, a_ref, dtb_ref, ex_ref, sel_ref, par_ref)
        cs_last = cs_full[CHUNK - 1:CHUNK, :]
        xdt = xs * dt_full
        causal, first = _pair_mask()
        out = []
        for g in range(NG):
            gl = slice(g * GW, (g + 1) * GW)
            bg = xbc_ref[rs, SSD_W + g * NSTATE:SSD_W + (g + 1) * NSTATE].astype(bf16)
            cg = xbc_ref[rs, SSD_W + NG * NSTATE + g * NSTATE:SSD_W + NG * NSTATE + (g + 1) * NSTATE].astype(bf16)
            cb2 = lax.dot_general(cg, jnp.concatenate([bg, bg], axis=0), (((1,), (1,)), ((), ())), preferred_element_type=f32)
            hg = states[g]
            hs_ref[sub, g] = hg
            y0 = jnp.dot(cg, hg.astype(bf16), preferred_element_type=f32)
            yoff = jnp.exp(cs_full[:, gl]) * y0
            for j in range(GW // 128):
                pair = g * (GW // 128) + j
                pl_ = slice(pair * 128, (pair + 1) * 128)
                seg = jnp.exp(jnp.where(causal, cs_full[:, pl_] - cstp[pair:pair + 1, :], -jnp.inf))
                m = (cb2 * seg).astype(bf16)
                yd = jnp.dot(m, _block_diag(xdt[:, pl_].astype(bf16), first), preferred_element_type=f32)
                y_ref[rs, pl_] = yd + yoff[:, j * 128:(j + 1) * 128] + xs[:, pl_] * dsk_ref[:, pl_]
            xdec = (xdt[:, gl] * jnp.exp(cs_last[:, gl] - cs_full[:, gl])).astype(bf16)
            st = lax.dot_general(bg, xdec, (((0,), (0,)), ((), ())), preferred_element_type=f32)
            out.append(jnp.exp(cs_last[:, gl]) * hg + st)
        return out

    def body(*refs):
        hst = refs[-1]

        @pl.when(pl.program_id(0) == 0)
        def _():
            hst[...] = jnp.zeros_like(hst)

        states = [hst[g] for g in range(NG)]
        for sub in range(SSD_SUB):
            states = one_chunk(sub, states, refs[:-1])
        for g in range(NG):
            hst[g] = states[g]

    rows = SSD_SUB * CHUNK
    const = lambda shape: pl.BlockSpec(shape, lambda c: tuple(0 for _ in shape))
    return pl.pallas_call(
        body, name="ssd_fwd", grid=(nc // SSD_SUB,),
        in_specs=[pl.BlockSpec((rows, XBC), lambda c: (c, 0)), pl.BlockSpec((rows, 128), lambda c: (c, DT_COL)),
                  const((1, 128)), const((1, 128)), const((1, SSD_W)), const((128, SSD_W)), const((8, 128)), const((128, 128))],
        out_specs=[pl.BlockSpec((rows, SSD_W), lambda c: (c, 0)), pl.BlockSpec((SSD_SUB, NG, NSTATE, GW), lambda c: (c, 0, 0, 0))],
        out_shape=[_sds((s, SSD_W), f32), _sds((nc, NG, NSTATE, GW), f32)],
        scratch_shapes=[pltpu.VMEM((NG, NSTATE, GW), f32)], compiler_params=_params(("arbitrary",)),
    )(xbc, proj2, a_row, dtb_row, dsk_full, jnp.asarray(ex), jnp.asarray(sel), jnp.asarray(par))


def _ssd_bwd(xbc, proj2, dy, hsave, a_row, dtb_row, dsk_full):
    s = xbc.shape[0]
    nc = s // CHUNK
    ex, ext, sel, par, ones_blk = _ssd_consts()

    def one_chunk(sub, dhs, refs):
        (xbc_ref, dtr_ref, dy_ref, hs_ref, a_ref, dtb_ref, dsk_ref, ex_ref, ext_ref, sel_ref, par_ref, ob_ref,
         dxbc_ref, ddtr_ref, dd_ref, da_ref, ddtb_ref, dh, a_dd, a_da, a_dtb, dcs_lane, dcs_b, dxdt) = refs
        rs = slice(sub * CHUNK, (sub + 1) * CHUNK)
        dcs_lane, dcs_b, dxdt = dcs_lane.at[sub], dcs_b.at[sub], dxdt.at[sub]
        xs, dt, cs, cstp, dt_full, cs_full = _ssd_common(rs, xbc_ref, dtr_ref, a_ref, dtb_ref, ex_ref, sel_ref, par_ref)
        cs_last = cs_full[CHUNK - 1:CHUNK, :]
        xdt = xs * dt_full
        dyv = dy_ref[rs, :]
        a_dd[...] += _fold8(dyv * xs)
        causal, first = _pair_mask()
        diag = lax.broadcasted_iota(jnp.int32, (CHUNK, 128), 0) == lax.broadcasted_iota(jnp.int32, (CHUNK, 128), 1) % CHUNK
        dh_out = []
        for g in range(NG):
            gl = slice(g * GW, (g + 1) * GW)
            bcol = slice(SSD_W + g * NSTATE, SSD_W + (g + 1) * NSTATE)
            ccol = slice(SSD_W + NG * NSTATE + g * NSTATE, SSD_W + NG * NSTATE + (g + 1) * NSTATE)
            bg = xbc_ref[rs, bcol].astype(bf16)
            cg = xbc_ref[rs, ccol].astype(bf16)
            bg2 = jnp.concatenate([bg, bg], axis=0)
            cb2 = lax.dot_general(cg, bg2, (((1,), (1,)), ((), ())), preferred_element_type=f32)
            hg = hs_ref[sub, g]
            hgb = hg.astype(bf16)
            dhg = dhs[g]
            dhgb = dhg.astype(bf16)
            eg = jnp.exp(cs_full[:, gl])
            dec = jnp.exp(cs_last[:, gl] - cs_full[:, gl])
            gam = jnp.exp(cs_last[:, gl])
            dyg = dyv[:, gl]
            xdt_g = xdt[:, gl]
            y0 = jnp.dot(cg, hgb, preferred_element_type=f32)
            dy0 = (eg * dyg).astype(bf16)
            dcm = lax.dot_general(dy0, hgb, (((1,), (1,)), ((), ())), preferred_element_type=f32)
            dh_prev = gam * dhg + lax.dot_general(cg, dy0, (((0,), (0,)), ((), ())), preferred_element_type=f32)
            dgam = jnp.sum(dhg * hg, axis=0, keepdims=True) * gam
            dxdec = jnp.dot(bg, dhgb, preferred_element_type=f32)
            dbm = lax.dot_general((xdt_g * dec).astype(bf16), dhgb, (((1,), (1,)), ((), ())), preferred_element_type=f32)
            t = dxdec * xdt_g * dec
            dcs_lane[:, gl] = dyg * eg * y0 - t
            dcs_lane[CHUNK - 1:CHUNK, gl] += jnp.sum(t, axis=0, keepdims=True) + dgam
            dxdt[:, gl] = dxdec * dec
            dcb2 = jnp.zeros((CHUNK, 128), f32)
            for j in range(GW // 128):
                pair = g * (GW // 128) + j
                pl_ = slice(pair * 128, (pair + 1) * 128)
                seg = jnp.exp(jnp.where(causal, cs_full[:, pl_] - cstp[pair:pair + 1, :], -jnp.inf))
                m = cb2 * seg
                mb = m.astype(bf16)
                rhs = _block_diag(xdt[:, pl_].astype(bf16), first)
                dyp = dyv[:, pl_].astype(bf16)
                dm = lax.dot_general(dyp, rhs, (((1,), (1,)), ((), ())), preferred_element_type=f32)
                tt = lax.dot_general(mb, dyp, (((0,), (0,)), ((), ())), preferred_element_type=f32)
                dxdt[:, pl_] += jnp.where(first, tt[0:CHUNK], tt[CHUNK:])
                dcb2 = dcb2 + dm * seg
                w = dm * m
                colsum = jnp.sum(w, axis=0, keepdims=True)
                dcs_b[:, pl_] = _dot01(w - jnp.where(diag, colsum, 0.0), ob_ref[...])
            dcb2b = dcb2.astype(bf16)
            dcm = dcm + jnp.dot(dcb2b, bg2, preferred_element_type=f32)
            t3 = lax.dot_general(dcb2b, cg, (((0,), (0,)), ((), ())), preferred_element_type=f32)
            dxbc_ref[rs, bcol] = dbm + t3[0:CHUNK] + t3[CHUNK:]
            dxbc_ref[rs, ccol] = dcm
            dh_out.append(dh_prev)
        dxdtv = dxdt[...]
        both = _dot01(jnp.concatenate([dcs_lane[...] + dcs_b[...] * (1.0 / HD), dxdtv * xs], axis=0), ext_ref[...])
        dcs = both[0:CHUNK]
        r_i = lax.broadcasted_iota(jnp.int32, (CHUNK, CHUNK), 0)
        c_i = lax.broadcasted_iota(jnp.int32, (CHUNK, CHUNK), 1)
        triu = (r_i <= c_i).astype(f32)
        da_ = _dot01(triu, dcs, exact="a")
        ddt = da_ * a_ref[...] + both[CHUNK:]
        a_da[...] += _fold8(da_ * dt)
        dxbc_ref[rs, 0:SSD_W] = dyv * dsk_ref[...] + dxdtv * dt_full
        ddtr = ddt * _sigmoid(dtr_ref[rs, :] + dtb_ref[...])
        ddtr_ref[rs, :] = ddtr
        a_dtb[...] += _fold8(ddtr)
        return dh_out

    nsteps = nc // SSD_SUB

    def body(*refs):
        dd_ref, da_ref, ddtb_ref, dh, a_dd, a_da, a_dtb = refs[14:21]
        ext_ref = refs[8]
        step = pl.program_id(0)

        @pl.when(step == 0)
        def _():
            dh[...] = jnp.zeros_like(dh)
            a_dd[...] = jnp.zeros_like(a_dd)
            a_da[...] = jnp.zeros_like(a_da)
            a_dtb[...] = jnp.zeros_like(a_dtb)

        dhs = [dh[g] for g in range(NG)]
        for sub in reversed(range(SSD_SUB)):
            dhs = one_chunk(sub, dhs, refs)
        for g in range(NG):
            dh[g] = dhs[g]

        @pl.when(step == nsteps - 1)
        def _():
            dd_ref[...] = jnp.sum(jnp.dot(a_dd[...], ext_ref[...], precision=HIGHEST, preferred_element_type=f32), axis=0, keepdims=True)
            da_ref[...] = jnp.sum(a_da[...], axis=0, keepdims=True)
            ddtb_ref[...] = jnp.sum(a_dtb[...], axis=0, keepdims=True)

    rev = lambda c: nsteps - 1 - c
    rows = SSD_SUB * CHUNK
    const = lambda shape: pl.BlockSpec(shape, lambda c: tuple(0 for _ in shape))
    return pl.pallas_call(
        body, name="ssd_bwd", grid=(nsteps,),
        in_specs=[pl.BlockSpec((rows, XBC), lambda c: (rev(c), 0)), pl.BlockSpec((rows, 128), lambda c: (rev(c), DT_COL)),
                  pl.BlockSpec((rows, SSD_W), lambda c: (rev(c), 0)), pl.BlockSpec((SSD_SUB, NG, NSTATE, GW), lambda c: (rev(c), 0, 0, 0)),
                  const((1, 128)), const((1, 128)), const((1, SSD_W)), const((128, SSD_W)), const((SSD_W, 128)),
                  const((8, 128)), const((128, 128)), const((128, 128))],
        out_specs=[pl.BlockSpec((rows, XBC), lambda c: (rev(c), 0)), pl.BlockSpec((rows, 128), lambda c: (rev(c), 0)),
                   const((1, 128)), const((1, 128)), const((1, 128))],
        out_shape=[_sds((s, XBC), f32), _sds((s, 128), f32), _sds((1, 128), f32), _sds((1, 128), f32), _sds((1, 128), f32)],
        scratch_shapes=[pltpu.VMEM((NG, NSTATE, GW), f32), pltpu.VMEM((8, SSD_W), f32), pltpu.VMEM((8, 128), f32), pltpu.VMEM((8, 128), f32)]
        + [pltpu.VMEM((SSD_SUB, CHUNK, SSD_W), f32)] * 3,
        compiler_params=_params(("arbitrary",)),
    )(xbc, proj2, dy, hsave, a_row, dtb_row, dsk_full, jnp.asarray(ex), jnp.asarray(ext), jnp.asarray(sel), jnp.asarray(par),
      jnp.asarray(ones_blk))


def _local_step(x, tgt, mods, g_mix, rel, conv_w, conv_b, dt_bias, a_log, d_skip, g_att, g_ssd, g_ffn, g_final, weights):
    s = x.shape[0]
    tm_e = 512 if s % 512 == 0 else s
    tm_m = 512 if s % 512 == 0 else s
    tm_l = 1024 if s % 1024 == 0 else s
    tk = 2048 if s % 2048 == 0 else s
    sh1, sc1, gt1, sh2, sc2, gt2 = [mods[:, i * D:(i + 1) * D] for i in range(6)]

    h1b = _norm_mod("norm_mod_1", x, g_mix, sc1, sh1, tm_e)
    win, win_b = weights.w_in(h1b)
    qkv = _mm_nn_fullk("proj_qkv", h1b, win, tm_l, 1536, bf16, n=IN_A)
    proj2 = _mm_nn_fullk("proj_zxbcdt", h1b, win_b, tm_l, 896, f32)
    bias = _expand_bias(rel)
    att = _attn_fwd(qkv, bias)
    xbc = _ssd_conv(proj2, conv_w, conv_b, tm_l)
    a_row = jnp.pad(-jnp.exp(a_log), ((0, 0), (0, 128 - NH)))
    dtb_row = jnp.pad(dt_bias, ((0, 0), (0, 128 - NH)))
    dsk_full = jnp.repeat(d_skip, HD, axis=1)
    y, hsave = _ssd_fwd(xbc, proj2, a_row, dtb_row, dsk_full)
    mixcat = _mix_pre(att, y, proj2, g_att, g_ssd, tm_e)
    wout = weights.w_out(mixcat)
    mix = _mm_nn_fullk("proj_out", mixcat, wout, tm_l, D, f32)
    x2, h2b = _resid_norm_mod(x, gt1, mix, g_ffn, sc2, sh2, tm_e)
    wg4, wu4, wd4 = weights.ffn(h2b)
    act, sil, ud = _ffn_up(h2b, wg4, wu4, tm_m)
    ffn = _ffn_down(act, wd4, tm_m)

    dx3, dffn, loss, dg_final, dgt2 = _final_fwd_bwd(x2, ffn, gt2, g_final, tgt, tm_e)
    gwd4 = _grad_wdown4(act, dffn, 1024, tk)
    dgate, dup = _ffn_dact(dffn, wd4, sil, ud, tm_l)
    tk2 = 4096 if s % 4096 == 0 else s
    tok = weights.grad(("w_down", "w_gate", "w_up"),
                       [gwd4, _grad_cols4("grad_w_gate", h2b, dgate, 512, tk2), _grad_cols4("grad_w_up", h2b, dup, 512, tk2)])
    dh2 = _ffn_dh(dgate, dup, wg4, wu4, tm_m, dep=tok)
    dx2, dmix, dsc2, dsh2, dg_ffn, dgt1 = _norm_mod_bwd("norm_mod_bwd_2", dh2, x2, g_ffn, sc2, dx3, tm_e, mix=mix, gt=gt1)
    gwout4 = _mm_tn("grad_w_out", mixcat, dmix, 512, 1024, tk2, bf16).reshape(NSH, D // NSH, D)
    dmc = _mm_nt("dmixcat", dmix, wout, tm_l, D, D, f32)
    datt, dy, dz, dg_att, dg_ssd = _mix_pre_bwd(dmc, att, y, proj2, g_att, g_ssd, tm_e)
    dq, dk, dv, gband = _attn_bwd(qkv, datt, bias)
    drel = _rel_bias_grad(gband.reshape(NH, CHUNK, BANDP))
    dxbc, ddtr, dd_row, da_row, ddtb_row = _ssd_bwd(xbc, proj2, dy, hsave, a_row, dtb_row, dsk_full)
    dxbc_raw, dconv_w, dconv_b = _ssd_conv_bwd(dxbc, proj2, conv_w, conv_b, tm_e)
    dproj = jnp.concatenate([dq, dk, dv, dz, dxbc_raw, ddtr.astype(bf16)], axis=1)
    gwin = _mm_tn("grad_w_in", h1b, dproj, 512, 1152, tk2, bf16)
    gwin4 = jnp.stack([jnp.pad(gwin[:, k * IN_SH:(k + 1) * IN_SH], ((0, 0), (0, IN_SHP - IN_SH))) for k in range(NSH)])
    tok = weights.grad(("w_out", "w_in"), [gwout4, gwin4])
    dh1 = _mm_nt("dh1", dproj, win, tm_m, 1024, IN_P, f32, dep=tok)
    grad_x, dsc1, dsh1, dg_mix = _norm_mod_bwd("norm_mod_bwd_1", dh1, x, g_mix, sc1, dx2, tm_e)

    dmods = jnp.concatenate([dsh1, dsc1, dgt1, dsh2, dsc2, dgt2], axis=1)
    dd_skip = dd_row[:, :NH]
    da_log = da_row[:, :NH] * a_row[:, :NH]
    small = dict(g_mix=dg_mix, conv_b=dconv_b, dt_bias=ddtb_row[:, :NH], a_log=da_log, d_skip=dd_skip, g_att_out=dg_att,
                 g_ssd_out=dg_ssd, g_ffn=dg_ffn, g_final=dg_final, rel_bias=drel, conv_w=dconv_w)
    return loss[0, 0], grad_x, dmods, small


HBM = pl.BlockSpec(memory_space=pl.ANY)
VMEM = pl.BlockSpec(memory_space=pltpu.VMEM)


def _place():
    x, y, c = lax.axis_index("x"), lax.axis_index("y"), lax.axis_index("c")
    chips = [(1 - x, y), (x, 1 - y), (1 - x, 1 - y)]
    return x, y, c, chips


def _allgather8(name, payload, dep=None):
    r = payload.shape[0]
    deps = [] if dep is None else [dep]

    def body(x_ref, *rest):
        out_ref, send_sems, recv_sems, local_sem = rest[-4:]
        x, y, c, chips = _place()
        me, sibling = (x, y, c), (x, y, 1 - c)

        def slot(px, py, pc):
            return out_ref.at[4 * px + 2 * py + pc]

        def copy(k, block, to, src=None):
            return pltpu.make_async_remote_copy(
                src_ref=slot(*block) if src is None else src, dst_ref=slot(*block),
                send_sem=send_sems.at[k], recv_sem=recv_sems.at[k], device_id=to, device_id_type=MESH)

        mine = pltpu.make_async_copy(x_ref, slot(*me), local_sem)
        mine.start()
        first = [copy(0, me, sibling, src=x_ref)]
        first += [copy(1 + j, me, (*chip, c), src=x_ref) for j, chip in enumerate(chips)]
        for cp in first:
            cp.start()
        passed = [copy(4 + j, (*chip, c), sibling) for j, chip in enumerate(chips)]
        for j, chip in enumerate(chips):
            copy(1 + j, (*chip, c), me).wait_recv()
            passed[j].start()
        copy(0, sibling, me).wait_recv()
        for j, chip in enumerate(chips):
            copy(4 + j, (*chip, 1 - c), me).wait_recv()
        for cp in first + passed:
            cp.wait_send()
        mine.wait()

    return pl.pallas_call(
        body, name=name, out_shape=_sds((N_DEV, r, 128), f32), in_specs=[VMEM] * (1 + len(deps)), out_specs=VMEM,
        scratch_shapes=[pltpu.SemaphoreType.DMA((7,)), pltpu.SemaphoreType.DMA((7,)), pltpu.SemaphoreType.DMA],
    )(payload, *deps)


def _sum8(g):
    r = g.shape[1]

    def body(g_ref, o_ref):
        acc = g_ref[0]
        for i in range(1, N_DEV):
            acc = acc + g_ref[i]
        o_ref[...] = acc

    return pl.pallas_call(body, name="sum8", out_shape=_sds((r, 128), f32))(g)


SEM = pl.BlockSpec(memory_space=pltpu.SEMAPHORE)
EFFECT = pltpu.SideEffectType.DATAFLOW_SIDE_EFFECTING


def _gather_copies(ins, lands, send_sems, recv_sems):
    x, y, c, chips = _place()
    k = 2 * x + y
    starts, recvs = [], []
    for w in range(len(ins)):
        for j, (px, py) in enumerate(chips):
            def mk(dst):
                return pltpu.make_async_remote_copy(src_ref=ins[w].at[c], dst_ref=dst, send_sem=send_sems[w].at[j],
                                                    recv_sem=recv_sems[w].at[j], device_id=(px, py, c), device_id_type=MESH)
            starts.append(mk(lands[w].at[k, c]))
            recvs.append(mk(lands[w].at[2 * px + py, c]))
    return starts, recvs


def _reduce_copies(ins, lands, send_sems, recv_sems):
    x, y, c, chips = _place()
    k = 2 * x + y
    starts, recvs = [], []
    for w in range(len(ins)):
        for j, (px, py) in enumerate(chips):
            def mk(dst):
                return pltpu.make_async_remote_copy(src_ref=ins[w].at[2 * px + py], dst_ref=dst, send_sem=send_sems[w].at[j],
                                                    recv_sem=recv_sems[w].at[j], device_id=(px, py, c), device_id_type=MESH)
            starts.append(mk(lands[w].at[k]))
            recvs.append(mk(lands[w].at[2 * px + py]))
    return starts, recvs


def _split_start(name, copies, srcs, land_shapes):
    nw = len(srcs)

    def body(*refs):
        starts, _ = copies(refs[:nw], refs[nw:2 * nw], refs[2 * nw:3 * nw], refs[3 * nw:4 * nw])
        for cp in starts:
            cp.start()
        refs[6 * nw][...] = jnp.zeros((8, 128), f32)

    sems = [pltpu.SemaphoreType.DMA((3,))] * nw
    bufs = [pltpu.HBM(s.shape, bf16) for s in srcs] + [pltpu.HBM(s, bf16) for s in land_shapes]
    res = pl.pallas_call(
        body, name=name, out_shape=sems + sems + bufs + [_sds((8, 128), f32)],
        in_specs=[HBM] * (2 * nw), out_specs=[SEM] * (2 * nw) + [HBM] * (2 * nw) + [VMEM],
        input_output_aliases={i: 2 * nw + i for i in range(2 * nw)},
        compiler_params=pltpu.CompilerParams(has_side_effects=EFFECT),
    )(*[pltpu.with_memory_space_constraint(s, pltpu.HBM) for s in srcs],
      *[pltpu.with_memory_space_constraint(lax.empty(s, bf16), pltpu.HBM) for s in land_shapes])
    return res[:nw], res[nw:2 * nw], res[2 * nw:3 * nw], res[3 * nw:4 * nw], res[4 * nw]


def _split_wait(name, copies, send_sems, recv_sems, srcs, lands, after):
    nw = len(srcs)

    def body(*refs):
        starts, recvs = copies(refs[:nw], refs[nw:2 * nw], refs[2 * nw:3 * nw], refs[3 * nw:4 * nw])
        for s_, r_ in zip(starts, recvs):
            s_.wait_send()
            r_.wait_recv()

    bufs = [pltpu.HBM(s.shape, bf16) for s in srcs] + [pltpu.HBM(l.shape, bf16) for l in lands]
    res = pl.pallas_call(
        body, name=name, out_shape=bufs, in_specs=[HBM] * (2 * nw) + [SEM] * (2 * nw) + [HBM], out_specs=[HBM] * (2 * nw),
        input_output_aliases={i: i for i in range(2 * nw)},
        compiler_params=pltpu.CompilerParams(has_side_effects=EFFECT),
    )(*srcs, *lands, *send_sems, *recv_sems, after)
    return res[:nw], res[nw:]


def _gather_forward(name, shards, lands):
    nw = len(shards)

    def body(*refs):
        ins, lands_in, outs = refs[:nw], refs[nw:2 * nw], refs[2 * nw:3 * nw]
        st_a, st_b, st_c = refs[3 * nw:4 * nw], refs[4 * nw:5 * nw], refs[5 * nw:6 * nw]
        send_sems, recv_sems, load_sems, store_sems = refs[6 * nw:]
        x, y, c, chips = _place()
        k = 2 * x + y
        sibling = (x, y, 1 - c)
        ld_a = [pltpu.make_async_copy(ins[w].at[c], st_a[w], load_sems.at[w, 0]) for w in range(nw)]
        ld_b = [pltpu.make_async_copy(ins[w].at[1 - c], st_b[w], load_sems.at[w, 1]) for w in range(nw)]
        for cp in ld_a + ld_b:
            cp.start()
        st_own = []
        for w in range(nw):
            ld_a[w].wait()
            st_own.append(pltpu.make_async_copy(st_a[w], outs[w].at[k, c], store_sems.at[w, 0]))
            st_own[-1].start()
        for w in range(nw):
            ld_b[w].wait()
            st_own.append(pltpu.make_async_copy(st_b[w], outs[w].at[k, 1 - c], store_sems.at[w, 1]))
            st_own[-1].start()
        for cp in st_own:
            cp.wait()
        fwds = {}
        for j, (px, py) in enumerate(chips):
            kq = 2 * px + py
            for w in range(nw):
                slot = st_b[w] if j % 2 == 0 else st_c[w]
                if j == 2:
                    fwds[w, 0].wait_send()
                ld = pltpu.make_async_copy(lands_in[w].at[kq, c], slot, load_sems.at[w, 2 + j])
                ld.start()
                ld.wait()
                fwds[w, j] = pltpu.make_async_remote_copy(src_ref=slot, dst_ref=outs[w].at[kq, c], send_sem=send_sems.at[w, j],
                                                          recv_sem=recv_sems.at[w, j], device_id=sibling, device_id_type=MESH)
                fwds[w, j].start()
        for j, (px, py) in enumerate(chips):
            for w in range(nw):
                pltpu.make_async_remote_copy(src_ref=st_c[w], dst_ref=outs[w].at[2 * px + py, 1 - c], send_sem=send_sems.at[w, j],
                                             recv_sem=recv_sems.at[w, j], device_id=sibling, device_id_type=MESH).wait_recv()
        for w in range(nw):
            fwds[w, 1].wait_send()
            fwds[w, 2].wait_send()

    stage = [pltpu.VMEM(s.shape[1:], bf16) for s in shards]
    return pl.pallas_call(
        body, name=name, out_shape=[_sds(l.shape, bf16) for l in lands],
        in_specs=[HBM] * (2 * nw), out_specs=[HBM] * nw, input_output_aliases={nw + w: w for w in range(nw)},
        scratch_shapes=stage * 3 + [pltpu.SemaphoreType.DMA((nw, 3)), pltpu.SemaphoreType.DMA((nw, 3)), pltpu.SemaphoreType.DMA((nw, 5)),
                                    pltpu.SemaphoreType.DMA((nw, 2))],
        compiler_params=pltpu.CompilerParams(vmem_limit_bytes=VMEM_LIMIT),
    )(*shards, *lands)


def _rs_pair_exchange(name, grads):
    nw = len(grads)

    def body(*refs):
        ins, got, stage = refs[:nw], refs[nw:2 * nw], refs[2 * nw:3 * nw]
        send_sems, recv_sems, load_sems = refs[3 * nw:]
        x, y, c, _ = _place()

        def load(w, kk):
            return pltpu.make_async_copy(ins[w].at[kk, 1 - c], stage[w].at[kk % 2], load_sems.at[w, kk])

        def send(w, kk):
            return pltpu.make_async_remote_copy(src_ref=stage[w].at[kk % 2], dst_ref=got[w].at[kk], send_sem=send_sems.at[w, kk],
                                                recv_sem=recv_sems.at[w, kk], device_id=(x, y, 1 - c), device_id_type=MESH)

        for kk in range(2):
            for w in range(nw):
                load(w, kk).start()
        for kk in range(NSH):
            for w in range(nw):
                load(w, kk).wait()
                send(w, kk).start()
            if kk + 2 < NSH:
                for w in range(nw):
                    send(w, kk).wait_send()
                    load(w, kk + 2).start()
        for kk in range(NSH - 2, NSH):
            for w in range(nw):
                send(w, kk).wait_send()
        for kk in range(NSH):
            for w in range(nw):
                send(w, kk).wait_recv()

    return pl.pallas_call(
        body, name=name, out_shape=[_sds((NSH,) + g.shape[2:], bf16) for g in grads], in_specs=[HBM] * nw, out_specs=[HBM] * nw,
        scratch_shapes=[pltpu.VMEM((2,) + g.shape[2:], bf16) for g in grads]
        + [pltpu.SemaphoreType.DMA((nw, NSH)), pltpu.SemaphoreType.DMA((nw, NSH)), pltpu.SemaphoreType.DMA((nw, NSH))],
        compiler_params=pltpu.CompilerParams(vmem_limit_bytes=VMEM_LIMIT),
    )(*grads)


def _rs_pair_gather(name, halves):
    nw = len(halves)

    def body(*refs):
        ins, outs, stage = refs[:nw], refs[nw:2 * nw], refs[2 * nw:3 * nw]
        send_sems, recv_sems, local_sems, stage_sems = refs[3 * nw:]
        x, y, c, _ = _place()
        loads = [pltpu.make_async_copy(ins[w], stage[w], stage_sems.at[w]) for w in range(nw)]
        for cp in loads:
            cp.start()
        local, cps = [], []
        for w in range(nw):
            loads[w].wait()
            local.append(pltpu.make_async_copy(stage[w], outs[w].at[c], local_sems.at[w]))
            cps.append(pltpu.make_async_remote_copy(src_ref=stage[w], dst_ref=outs[w].at[c], send_sem=send_sems.at[w],
                                                    recv_sem=recv_sems.at[w], device_id=(x, y, 1 - c), device_id_type=MESH))
            local[w].start()
            cps[w].start()
        for w in range(nw):
            pltpu.make_async_remote_copy(src_ref=stage[w], dst_ref=outs[w].at[1 - c], send_sem=send_sems.at[w], recv_sem=recv_sems.at[w],
                                         device_id=(x, y, 1 - c), device_id_type=MESH).wait_recv()
        for cp in cps:
            cp.wait_send()
        for cp in local:
            cp.wait()

    return pl.pallas_call(
        body, name=name, out_shape=[_sds((2,) + h.shape, f32) for h in halves], in_specs=[HBM] * nw, out_specs=[HBM] * nw,
        scratch_shapes=[pltpu.VMEM(h.shape, f32) for h in halves]
        + [pltpu.SemaphoreType.DMA((nw,)), pltpu.SemaphoreType.DMA((nw,)), pltpu.SemaphoreType.DMA((nw,)), pltpu.SemaphoreType.DMA((nw,))],
        compiler_params=pltpu.CompilerParams(vmem_limit_bytes=VMEM_LIMIT),
    )(*halves)


def _row_tile(r, c, nbuf):
    budget = 24 * 1024 * 1024 // (2 * nbuf * 4 * c)
    fits = [t for t in range(16, r + 1, 16) if r % t == 0 and t <= budget]
    return max(fits) if fits else r


def _cast_bf16(name, a, dep=None):
    r, c = a.shape
    tr = _row_tile(r, c, 2)
    dep_specs, dep_ops = _dep_args(dep, 1)

    def body(a_ref, *rest):
        rest[-1][...] = a_ref[...].astype(bf16)

    spec = pl.BlockSpec((tr, c), lambda i: (i, 0))
    return pl.pallas_call(body, name=name, grid=(r // tr,), in_specs=[spec] + dep_specs, out_specs=spec, out_shape=_sds((r, c), bf16),
                          compiler_params=_params(("parallel",)))(a, *dep_ops)


def _w_in_columns(win4):
    tr = 256

    def body(a_ref, o_ref, ob_ref):
        for k in range(NSH):
            o_ref[:, IN_SH * k:IN_SH * (k + 1)] = a_ref[k][:, :IN_SH]
        o_ref[:, IN_COLS:] = jnp.zeros((tr, IN_P - IN_COLS), bf16)
        ob_ref[...] = o_ref[:, IN_A:]

    return pl.pallas_call(
        body, name="w_in_columns", grid=(D // tr,), in_specs=[pl.BlockSpec((NSH, tr, IN_SHP), lambda i: (0, i, 0))],
        out_specs=[pl.BlockSpec((tr, IN_P), lambda i: (i, 0)), pl.BlockSpec((tr, IN_B), lambda i: (i, 0))],
        out_shape=[_sds((D, IN_P), bf16), _sds((D, IN_B), bf16)], compiler_params=_params(("parallel",)))(win4)


def _pair_sum(name, core, grads, got):
    _, _, rh, c = grads.shape
    tr = _row_tile(rh, c, 2)

    def body(c_ref, a_ref, b_ref, o_ref):
        o_ref[...] = (a_ref[...].astype(f32) + b_ref[...].astype(f32)).astype(bf16)

    spec = pl.BlockSpec((None, tr, c), lambda k, i, c_ref: (k, i, 0))
    return pl.pallas_call(
        body, name=name, out_shape=_sds((NSH, rh, c), bf16),
        grid_spec=pltpu.PrefetchScalarGridSpec(
            num_scalar_prefetch=1, grid=(NSH, rh // tr),
            in_specs=[pl.BlockSpec((None, None, tr, c), lambda k, i, c_ref: (k, c_ref[0], i, 0)), spec], out_specs=spec),
        compiler_params=_params(("parallel", "parallel")))(core, grads, got)


def _chip_sum(name, chip, sums, lands):
    _, rh, c = sums.shape
    tr = _row_tile(rh, c, 4)

    def body(k_ref, own_ref, l_ref, o_ref):
        own = own_ref[...].astype(f32)
        acc = None
        for j in range(NSH):
            term = jnp.where(k_ref[0] == j, own, l_ref[j].astype(f32))
            acc = term if acc is None else acc + term
        o_ref[...] = acc

    return pl.pallas_call(
        body, name=name, out_shape=_sds((rh, c), f32),
        grid_spec=pltpu.PrefetchScalarGridSpec(
            num_scalar_prefetch=1, grid=(rh // tr,),
            in_specs=[pl.BlockSpec((None, tr, c), lambda i, k_ref: (k_ref[0], i, 0)), pl.BlockSpec((NSH, tr, c), lambda i, k_ref: (0, i, 0))],
            out_specs=pl.BlockSpec((tr, c), lambda i, k_ref: (i, 0))),
        compiler_params=_params(("parallel",)))(chip, sums, lands)


def _mods_part(cond16, w_ada, b_part):
    n = w_ada.shape[1]
    tn = 512

    def body(c_ref, w_ref, b_ref, o_ref):
        cv = c_ref[...]
        o_ref[...] = _dot(cv * _sigmoid(cv), w_ref[...]) + b_ref[...]

    return pl.pallas_call(
        body, name="mods_part", grid=(n // tn,),
        in_specs=[pl.BlockSpec((16, D), lambda j: (0, 0)), pl.BlockSpec((D, tn), lambda j: (0, j)), pl.BlockSpec((1, tn), lambda j: (0, j))],
        out_specs=pl.BlockSpec((16, tn), lambda j: (0, j)), out_shape=_sds((16, n), f32), compiler_params=_params(("parallel",)),
    )(cond16, w_ada, b_part)


def _grad_w_ada(cond16, dm16):
    n = dm16.shape[1]
    tr = 256

    def body(c_ref, d_ref, o_ref):
        cv = c_ref[...]
        o_ref[...] = _dot(cv * _sigmoid(cv), d_ref[...], ta=True)

    return pl.pallas_call(
        body, name="grad_w_ada", grid=(D // tr,),
        in_specs=[pl.BlockSpec((16, tr), lambda i: (0, i)), pl.BlockSpec((16, n), lambda i: (0, 0))],
        out_specs=pl.BlockSpec((tr, n), lambda i: (i, 0)), out_shape=_sds((D, n), f32), compiler_params=_params(("parallel",)),
    )(cond16, dm16)


def _adamw(name, w, g, m, v):
    r, c = w.shape
    tr = _row_tile(r, c, 7)
    spec = pl.BlockSpec((tr, c), lambda i: (i, 0))
    grid = (r // tr,)

    def body(w_ref, g_ref, m_ref, v_ref, d_ref, nm_ref, nv_ref):
        gv = g_ref[...]
        nm = ADAM_B1 * m_ref[...] + (1.0 - ADAM_B1) * gv
        nv = ADAM_B2 * v_ref[...] + (1.0 - ADAM_B2) * (gv * gv)
        nm_ref[...] = nm
        nv_ref[...] = nv
        m_hat = nm / (1.0 - ADAM_B1 ** ADAM_STEP)
        v_hat = nv / (1.0 - ADAM_B2 ** ADAM_STEP)
        d_ref[...] = -ADAM_LR * (m_hat / (jnp.sqrt(v_hat) + ADAM_EPS) + ADAM_WD * w_ref[...])

    return pl.pallas_call(body, name=name, grid=grid, in_specs=[spec] * 4, out_specs=[spec] * 3, out_shape=[_sds(w.shape, f32)] * 3,
                          compiler_params=_params(("parallel",)))(w, g, m, v)


def _pack(parts, rows):
    flat = []
    for p in parts:
        p = p.reshape(-1)
        flat.append(jnp.pad(p, (0, (-p.shape[0]) % 128)))
    v = jnp.concatenate(flat)
    return jnp.pad(v, (0, rows * 128 - v.shape[0])).reshape(rows, 128)


def _unpack(packed, sizes):
    lead = packed.shape[:-2]
    flat = packed.reshape(lead + (-1,))
    out, off = [], 0
    for n in sizes:
        out.append(flat[..., off:off + n])
        off += n + (-n) % 128
    return out


BIG = ("w_in", "w_out", "w_gate", "w_up", "w_down")
SMALL = ("b_ada", "g_mix", "conv_b", "dt_bias", "a_log", "d_skip", "g_att_out", "g_ssd_out", "g_ffn", "g_final", "rel_bias", "conv_w")
ORDER = ("w_ada", "b_ada", "g_mix", "w_in", "rel_bias", "conv_w", "conv_b", "dt_bias", "a_log", "d_skip", "g_att_out", "g_ssd_out",
         "w_out", "g_ffn", "w_gate", "w_up", "w_down", "g_final")
REL_SH = N_REL // NSH
CONVW_SH = XBC // NSH
ADA_SH = 6 * D // NSH


class _Exchange:
    def __init__(self, core, chip):
        self.core, self.chip = core, chip
        self.gathered = {}
        self.pending = []

    def gather(self, names, shards):
        ssem, rsem, thru, lands, token = _split_start("gather_start_" + "_".join(names), _gather_copies, shards,
                                                      [(NSH,) + s.shape for s in shards])
        self.gathered.update({n: (ssem[i], rsem[i], thru[i], lands[i]) for i, n in enumerate(names)})
        return token

    def _whole(self, names, after):
        ssem, rsem, thru, lands = zip(*[self.gathered[n] for n in names])
        tag = "_".join(names)
        thru, lands = _split_wait("gather_wait_" + tag, _gather_copies, ssem, rsem, thru, lands, after)
        return _gather_forward("gather_forward_" + tag, thru, lands)

    def w_in(self, after):
        (win4,) = self._whole(("w_in",), after)
        return _w_in_columns(win4.reshape(NSH, D, IN_SHP))

    def w_out(self, after):
        (wout4,) = self._whole(("w_out",), after)
        return wout4.reshape(D, D)

    def ffn(self, after):
        wg4, wu4, wd4 = self._whole(("w_gate", "w_up", "w_down"), after)
        return wg4.reshape(NSH, D, FSH), wu4.reshape(NSH, D, FSH), wd4.reshape(NSH, FSH, D)

    def grad(self, names, grads):
        tag = "_".join(names)
        stacked = [g.reshape(NSH, 2, g.shape[1] // 2, g.shape[2]) for g in grads]
        got = _rs_pair_exchange("rs_pair_exchange_" + tag, stacked)
        sums = [_pair_sum("pair_sum_" + n, self.core, o, g) for n, o, g in zip(names, stacked, got)]
        self.pending.append((names, _split_start("rs_start_" + tag, _reduce_copies, sums, [s.shape for s in sums])))
        return self.pending[-1][1][4]

    def finish(self, after):
        grads = {}
        for names, (ssem, rsem, sums, lands, _) in self.pending:
            tag = "_".join(names)
            sums, lands = _split_wait("rs_wait_" + tag, _reduce_copies, ssem, rsem, sums, lands, after)
            halves = [_chip_sum("chip_sum_" + n, self.chip, sm, ld) for n, sm, ld in zip(names, sums, lands)]
            for n, f in zip(names, _rs_pair_gather("rs_pair_gather_" + tag, halves)):
                grads[n] = f.reshape(2 * f.shape[1], f.shape[2])
        return grads


def kernel(x, c, w_ada, b_ada, g_mix, w_in, rel_bias, conv_w, conv_b, dt_bias, a_log, d_skip, g_att_out, g_ssd_out, w_out, g_ffn, w_gate, w_up, w_down, g_final, loss_target, m_w_ada, m_b_ada, m_g_mix, m_w_in, m_rel_bias, m_conv_w, m_conv_b, m_dt_bias, m_a_log, m_d_skip, m_g_att_out, m_g_ssd_out, m_w_out, m_g_ffn, m_w_gate, m_w_up, m_w_down, m_g_final, v_w_ada, v_b_ada, v_g_mix, v_w_in, v_rel_bias, v_conv_w, v_conv_b, v_dt_bias, v_a_log, v_d_skip, v_g_att_out, v_g_ssd_out, v_w_out, v_g_ffn, v_w_gate, v_w_up, v_w_down, v_g_final):
    args = dict(locals())
    w = {n: args[n] for n in ORDER}
    m = {n: args["m_" + n] for n in ORDER}
    v = {n: args["v_" + n] for n in ORDER}
    ix, iy, ic = lax.axis_index("x"), lax.axis_index("y"), lax.axis_index("c")
    chip = 2 * ix + iy
    dev = 2 * chip + ic
    s = x.shape[1]

    g1 = _allgather8("gather_inputs", _pack([c[0], rel_bias[0], conv_w[0]], 40))
    c_all, rel_sh, convw_sh = _unpack(g1, [D, NH * REL_SH, 4 * CONVW_SH])
    rel_full = jnp.concatenate([rel_sh[2 * k].reshape(NH, REL_SH) for k in range(NSH)], axis=1)
    convw_full = jnp.concatenate([convw_sh[2 * k].reshape(4, CONVW_SH) for k in range(NSH)], axis=1)
    cond16 = jnp.pad(c_all, ((0, 8), (0, 0)))
    b_part = lax.dynamic_slice_in_dim(b_ada, chip * ADA_SH, ADA_SH, axis=1)
    mods_part = _mods_part(cond16, w_ada[0], b_part)[:N_DEV]
    g2 = _allgather8("gather_mods", mods_part.reshape(N_DEV * ADA_SH // 128, 128))
    mods_all = jnp.concatenate([g2[2 * k].reshape(N_DEV, ADA_SH) for k in range(NSH)], axis=1)
    mods = lax.dynamic_slice_in_dim(mods_all, dev, 1, axis=0)

    exchange = _Exchange(jnp.reshape(ic, (1,)).astype(jnp.int32), jnp.reshape(chip, (1,)).astype(jnp.int32))
    shard_in = _cast_bf16("cast_w_in", jnp.pad(w_in[0], ((0, 0), (0, IN_SHP - IN_SH))), dep=g2[0, :8]).reshape(2, D // 2, IN_SHP)
    tok = exchange.gather(("w_in",), [shard_in])
    tok = exchange.gather(("w_out", "w_gate", "w_up", "w_down"), [
        _cast_bf16("cast_w_out", w_out[0], dep=tok).reshape(2, D // NSH // 2, D),
        _cast_bf16("cast_w_gate", w_gate[0], dep=tok).reshape(2, D // 2, FSH),
        _cast_bf16("cast_w_up", w_up[0], dep=tok).reshape(2, D // 2, FSH),
        _cast_bf16("cast_w_down", w_down[0], dep=tok).reshape(2, FSH // 2, D)])
    mods = mods + tok[:1, :1]

    loss, grad_x, dmods, small = _local_step(
        x[0], loss_target[0], mods, g_mix, rel_full, convw_full, conv_b, dt_bias, a_log, d_skip, g_att_out, g_ssd_out, g_ffn,
        g_final[None, :], exchange)

    small_names = ("g_mix", "conv_b", "dt_bias", "a_log", "d_skip", "g_att_out", "g_ssd_out", "g_ffn", "g_final", "rel_bias", "conv_w")
    g3 = _allgather8("gather_small_grads", _pack([dmods] + [small[n] for n in small_names], 264))
    sizes = [6 * D] + [int(np.prod(small[n].shape)) for n in small_names]
    dmods_all = _unpack(g3, sizes)[0]
    summed = _unpack(_sum8(g3), sizes)
    grads = {"b_ada": summed[0].reshape(1, 6 * D)}
    for n, val in zip(small_names, summed[1:]):
        grads[n] = val.reshape(small[n].shape)
    grads["rel_bias"] = lax.dynamic_slice_in_dim(grads["rel_bias"], chip * REL_SH, REL_SH, axis=1)
    grads["conv_w"] = lax.dynamic_slice_in_dim(grads["conv_w"], chip * CONVW_SH, CONVW_SH, axis=1)
    grads["g_final"] = grads["g_final"].reshape(D)
    dm16 = jnp.pad(lax.dynamic_slice_in_dim(dmods_all, chip * ADA_SH, ADA_SH, axis=1), ((0, 8), (0, 0)))
    grads["w_ada"] = _grad_w_ada(cond16, dm16)

    delta, new_m, new_v = {}, {}, {}
    delta["w_ada"], new_m["w_ada"], new_v["w_ada"] = _adamw("adamw_w_ada", w_ada[0], grads["w_ada"], m_w_ada[0], v_w_ada[0])
    grads.update(exchange.finish(grad_x))
    grads["w_in"] = grads["w_in"][:, :IN_SH]
    for n in BIG:
        delta[n], new_m[n], new_v[n] = _adamw("adamw_" + n, w[n][0], grads[n], m[n][0], v[n][0])
    sw = _pack([w[n] for n in SMALL], 200)
    sg = _pack([grads[n] for n in SMALL], 200)
    sm = _pack([m[n] for n in SMALL], 200)
    sv = _pack([v[n] for n in SMALL], 200)
    ssz = [int(np.prod(w[n].shape)) for n in SMALL]
    for dst, packed in zip((delta, new_m, new_v), _adamw("adamw_small", sw, sg, sm, sv)):
        for n, val in zip(SMALL, _unpack(packed, ssz)):
            dst[n] = val

    def shaped(d, n):
        return d[n].reshape(w[n].shape)

    total = lax.psum(loss, ("x", "y", "c"))
    return (total, grad_x[None], *[shaped(grads, n) for n in ORDER], *[shaped(delta, n) for n in ORDER],
            *[shaped(new_m, n) for n in ORDER], *[shaped(new_v, n) for n in ORDER])
```

```python
import functools

import numpy as np
import jax
import jax.numpy as jnp
from jax import lax
from jax.experimental import pallas as pl
from jax.experimental.pallas import tpu as pltpu

f32 = jnp.float32
bf16 = jnp.bfloat16
HIGHEST = lax.Precision.HIGHEST
MESH = pl.DeviceIdType.MESH

D = 2048
CHUNK = 64
LEFT = 8
BAND = (LEFT + 1) * CHUNK
BANDP = 640
PADK = LEFT * CHUNK
NH = 16
HD = 64
ATT_W = NH * HD
SSD_W = 1024
NG = 2
NSTATE = 128
GW = SSD_W // NG
XBC = SSD_W + 2 * NG * NSTATE
N_REL = 320
REL_CLIP = 256
FFN = 5632
NSH = 4
FSH = FFN // NSH
IN_COLS = 5648
IN_SH = IN_COLS // NSH
IN_SHP = 1536
IN_A = 3 * ATT_W
IN_B = 2688
IN_P = IN_A + IN_B
EPS = 1e-6
N_DEV = 8

ADAM_LR = 0.001
ADAM_B1 = 0.9
ADAM_B2 = 0.999
ADAM_EPS = 1e-08
ADAM_WD = 0.01
ADAM_STEP = 10

VMEM_LIMIT = 56 * 1024 * 1024


def _params(sem):
    return pltpu.CompilerParams(dimension_semantics=sem, vmem_limit_bytes=VMEM_LIMIT)


def _sds(shape, dtype):
    return jax.ShapeDtypeStruct(shape, dtype)


def _fold8(v):
    r, w = v.shape
    return jnp.sum(v.reshape(r // 8, 8, w), axis=0)


STRIP = 16


def _strips(tm, fn):
    def step(j, carry):
        fn(pl.ds(pl.multiple_of(j * STRIP, STRIP), STRIP))
        return carry
    lax.fori_loop(0, tm // STRIP, step, 0, unroll=4)


def _sigmoid(v):
    return 1.0 / (1.0 + jnp.exp(-v))


def _softplus(v):
    return jnp.maximum(v, 0.0) + jnp.log(1.0 + jnp.exp(-jnp.abs(v)))


def _dot(a, b, ta=False, tb=False):
    dn = (((0 if ta else 1,), (1 if tb else 0,)), ((), ()))
    return lax.dot_general(a.astype(bf16), b.astype(bf16), dn, preferred_element_type=f32)


def _dep_args(dep, ngrid):
    if dep is None:
        return [], []
    return [pl.BlockSpec((8, 128), lambda *_: (0, 0))], [dep]


def _dot01(a, b, ta=False, tb=False, exact="b"):
    dn = (((0 if ta else 1,), (1 if tb else 0,)), ((), ()))
    x = a if exact == "b" else b
    hi = x.astype(bf16)
    r = x - hi.astype(f32)
    mid = r.astype(bf16)
    lo = (r - mid.astype(f32)).astype(bf16)
    if exact == "b":
        m = b.astype(bf16)
        return sum(lax.dot_general(p, m, dn, preferred_element_type=f32) for p in (hi, mid, lo))
    m = a.astype(bf16)
    return sum(lax.dot_general(m, p, dn, preferred_element_type=f32) for p in (hi, mid, lo))


def _matmul(name, a, b, *, grid, a_spec, b_spec, o_spec, o_shape, o_dtype, acc_shape, ta=False, tb=False, dep=None):
    nk = grid[2]
    dep_specs, dep_ops = _dep_args(dep, 3)

    def body(a_ref, b_ref, *rest):
        o_ref, acc_ref = rest[-2:]
        p = _dot(a_ref[...], b_ref[...], ta, tb)
        if nk == 1:
            o_ref[...] = p.astype(o_ref.dtype)
        else:
            k = pl.program_id(2)

            @pl.when(k == 0)
            def _():
                acc_ref[...] = p

            @pl.when(jnp.logical_and(k > 0, k < nk - 1))
            def _():
                acc_ref[...] += p

            @pl.when(k == nk - 1)
            def _():
                o_ref[...] = (acc_ref[...] + p).astype(o_ref.dtype)

    return pl.pallas_call(
        body, name=name, grid=grid, in_specs=[a_spec, b_spec] + dep_specs, out_specs=o_spec,
        out_shape=_sds(o_shape, o_dtype), scratch_shapes=[pltpu.VMEM(acc_shape if nk > 1 else (8, 128), f32)],
        compiler_params=_params(("parallel", "parallel", "arbitrary")),
    )(a, b, *dep_ops)


def _mm_nn_fullk(name, a, b, tm, tn, o_dtype, n=None):
    m, k = a.shape
    n = b.shape[1] if n is None else n
    return _matmul(name, a, b, grid=(m // tm, n // tn, 1),
                   a_spec=pl.BlockSpec((tm, k), lambda i, j, kk: (i, 0)),
                   b_spec=pl.BlockSpec((k, tn), lambda i, j, kk: (0, j)),
                   o_spec=pl.BlockSpec((tm, tn), lambda i, j, kk: (i, j)),
                   o_shape=(m, n), o_dtype=o_dtype, acc_shape=(tm, tn))


def _mm_nt(name, a, b, tm, tn, tk, o_dtype, dep=None):
    m, k = a.shape
    n = b.shape[0]
    return _matmul(name, a, b, grid=(m // tm, n // tn, k // tk), tb=True, dep=dep,
                   a_spec=pl.BlockSpec((tm, tk), lambda i, j, kk: (i, kk)),
                   b_spec=pl.BlockSpec((tn, tk), lambda i, j, kk: (j, kk)),
                   o_spec=pl.BlockSpec((tm, tn), lambda i, j, kk: (i, j)),
                   o_shape=(m, n), o_dtype=o_dtype, acc_shape=(tm, tn))


def _mm_tn(name, a, b, tm, tn, tk, o_dtype):
    k, m = a.shape
    n = b.shape[1]
    return _matmul(name, a, b, grid=(m // tm, n // tn, k // tk), ta=True,
                   a_spec=pl.BlockSpec((tk, tm), lambda i, j, kk: (kk, i)),
                   b_spec=pl.BlockSpec((tk, tn), lambda i, j, kk: (kk, j)),
                   o_spec=pl.BlockSpec((tm, tn), lambda i, j, kk: (i, j)),
                   o_shape=(m, n), o_dtype=o_dtype, acc_shape=(tm, tn))


FSH_PARTS = (slice(0, 640), slice(640, FSH))


def _ffn_up(h2b, wg4, wu4, tm):
    s = h2b.shape[0]

    def body(h_ref, wg_ref, wu_ref, a_ref, s_ref, ud_ref):
        h = h_ref[...]
        for cols in FSH_PARTS:
            g = _dot(h, wg_ref[:, cols])
            u = _dot(h, wu_ref[:, cols])
            sg = _sigmoid(g)
            sil = g * sg
            a_ref[:, cols] = (sil * u).astype(bf16)
            s_ref[:, cols] = sil.astype(bf16)
            ud_ref[:, cols] = (u * (sg * (1.0 + g * (1.0 - sg)))).astype(bf16)

    wspec = pl.BlockSpec((None, D, FSH), lambda k, i: (k, 0, 0))
    ospec = pl.BlockSpec((tm, FSH), lambda k, i: (i, k))
    return pl.pallas_call(
        body, name="ffn_up", grid=(NSH, s // tm),
        in_specs=[pl.BlockSpec((tm, D), lambda k, i: (i, 0)), wspec, wspec],
        out_specs=[ospec, ospec, ospec], out_shape=[_sds((s, FFN), bf16)] * 3,
        compiler_params=_params(("parallel", "parallel")),
    )(h2b, wg4, wu4)


def _ffn_down(act, wd4, tm):
    s = act.shape[0]
    tn = D // 2

    def body(a_ref, b_ref, o_ref):
        o_ref[...] = jnp.dot(a_ref[...], b_ref[...].reshape(FFN, tn), preferred_element_type=f32)

    return pl.pallas_call(
        body, name="ffn_down", grid=(s // tm, D // tn),
        in_specs=[pl.BlockSpec((tm, FFN), lambda i, j: (i, 0)), pl.BlockSpec((NSH, FSH, tn), lambda i, j: (0, 0, j))],
        out_specs=pl.BlockSpec((tm, tn), lambda i, j: (i, j)), out_shape=_sds((s, D), f32),
        compiler_params=_params(("parallel", "parallel")),
    )(act, wd4)


def _ffn_dact(dffn, wd4, sil, ud, tm, dep=None):
    s = dffn.shape[0]
    dep_specs, dep_ops = _dep_args(dep, 2)

    def body(d_ref, w_ref, s_ref, ud_ref, *rest):
        dg_ref, du_ref = rest[-2:]
        d = d_ref[...]
        for cols in FSH_PARTS:
            dact = _dot(d, w_ref[cols, :], tb=True)
            dg_ref[:, cols] = (dact * ud_ref[:, cols].astype(f32)).astype(bf16)
            du_ref[:, cols] = (dact * s_ref[:, cols].astype(f32)).astype(bf16)

    blk = pl.BlockSpec((tm, FSH), lambda k, i: (i, k))
    return pl.pallas_call(
        body, name="ffn_dact", grid=(NSH, s // tm),
        in_specs=[pl.BlockSpec((tm, D), lambda k, i: (i, 0)), pl.BlockSpec((None, FSH, D), lambda k, i: (k, 0, 0)), blk, blk] + dep_specs,
        out_specs=[blk, blk], out_shape=[_sds((s, FFN), bf16), _sds((s, FFN), bf16)],
        compiler_params=_params(("parallel", "parallel")),
    )(dffn, wd4, sil, ud, *dep_ops)


def _ffn_dh(dgate, dup, wg4, wu4, tm, dep=None):
    s = dgate.shape[0]
    dep_specs, dep_ops = _dep_args(dep, 2)

    def body(dg_ref, du_ref, wg_ref, wu_ref, *rest):
        o_ref, acc_ref = rest[-2:]
        k = pl.program_id(1)
        p = _dot(dg_ref[...], wg_ref[...], tb=True) + _dot(du_ref[...], wu_ref[...], tb=True)

        @pl.when(k == 0)
        def _():
            acc_ref[...] = p

        @pl.when(jnp.logical_and(k > 0, k < NSH - 1))
        def _():
            acc_ref[...] += p

        @pl.when(k == NSH - 1)
        def _():
            o_ref[...] = acc_ref[...] + p

    aspec = pl.BlockSpec((tm, FSH), lambda i, k: (i, k))
    wspec = pl.BlockSpec((None, D, FSH), lambda i, k: (k, 0, 0))
    return pl.pallas_call(
        body, name="ffn_dh", grid=(s // tm, NSH), in_specs=[aspec, aspec, wspec, wspec] + dep_specs,
        out_specs=pl.BlockSpec((tm, D), lambda i, k: (i, 0)), out_shape=_sds((s, D), f32),
        scratch_shapes=[pltpu.VMEM((tm, D), f32)], compiler_params=_params(("parallel", "arbitrary")),
    )(dgate, dup, wg4, wu4, *dep_ops)


def _grad_cols4(name, h, dy, tm, tk):
    s = h.shape[0]
    return _matmul(name, h, dy, grid=(NSH, D // tm, s // tk), ta=True,
                   a_spec=pl.BlockSpec((tk, tm), lambda k, i, kk: (kk, i)),
                   b_spec=pl.BlockSpec((tk, FSH), lambda k, i, kk: (kk, k)),
                   o_spec=pl.BlockSpec((None, tm, FSH), lambda k, i, kk: (k, i, 0)),
                   o_shape=(NSH, D, FSH), o_dtype=bf16, acc_shape=(tm, FSH))


def _grad_wdown4(act, dffn, tn, tk):
    s = act.shape[0]
    return _matmul("grad_w_down", act, dffn, grid=(NSH, D // tn, s // tk), ta=True,
                   a_spec=pl.BlockSpec((tk, FSH), lambda k, j, kk: (kk, k)),
                   b_spec=pl.BlockSpec((tk, tn), lambda k, j, kk: (kk, j)),
                   o_spec=pl.BlockSpec((None, FSH, tn), lambda k, j, kk: (k, 0, j)),
                   o_shape=(NSH, FSH, D), o_dtype=bf16, acc_shape=(FSH, tn))


def _row_spec(w):
    return pl.BlockSpec((1, w), lambda i: (0, 0))


def _tile_spec(tm, w, col=0):
    return pl.BlockSpec((tm, w), lambda i: (i, col))


def _norm_mod(name, x, g, sc, sh, tm):
    s = x.shape[0]

    def body(x_ref, g_ref, sc_ref, sh_ref, o_ref):
        def strip(rows):
            xv = x_ref[rows, :]
            r = lax.rsqrt(jnp.mean(xv * xv, axis=-1, keepdims=True) + EPS)
            o_ref[rows, :] = (xv * r * g_ref[...] * (1.0 + sc_ref[...]) + sh_ref[...]).astype(bf16)

        _strips(tm, strip)

    return pl.pallas_call(
        body, name=name, grid=(s // tm,), in_specs=[_tile_spec(tm, D), _row_spec(D), _row_spec(D), _row_spec(D)],
        out_specs=_tile_spec(tm, D), out_shape=_sds((s, D), bf16), compiler_params=_params(("parallel",)),
    )(x, g, sc, sh)


def _resid_norm_mod(x, gt, mix, g, sc, sh, tm):
    s = x.shape[0]

    def body(x_ref, gt_ref, m_ref, g_ref, sc_ref, sh_ref, x2_ref, h_ref):
        def strip(rows):
            xv = x_ref[rows, :] + gt_ref[...] * m_ref[rows, :]
            x2_ref[rows, :] = xv
            r = lax.rsqrt(jnp.mean(xv * xv, axis=-1, keepdims=True) + EPS)
            h_ref[rows, :] = (xv * r * g_ref[...] * (1.0 + sc_ref[...]) + sh_ref[...]).astype(bf16)

        _strips(tm, strip)

    return pl.pallas_call(
        body, name="resid_norm_mod", grid=(s // tm,),
        in_specs=[_tile_spec(tm, D), _row_spec(D), _tile_spec(tm, D), _row_spec(D), _row_spec(D), _row_spec(D)],
        out_specs=[_tile_spec(tm, D), _tile_spec(tm, D)], out_shape=[_sds((s, D), f32), _sds((s, D), bf16)],
        compiler_params=_params(("parallel",)),
    )(x, gt, mix, g, sc, sh)


def _final_fwd_bwd(x2, ffn, gt2, g, tgt, tm):
    s = x2.shape[0]
    n = s // tm

    def body(x_ref, f_ref, gt_ref, g_ref, t_ref, dx_ref, df_ref, loss_ref, dg_ref, dgt_ref, a_loss, a_dg, a_dgt):
        i = pl.program_id(0)

        @pl.when(i == 0)
        def _():
            a_loss[...] = jnp.zeros_like(a_loss)
            a_dg[...] = jnp.zeros_like(a_dg)
            a_dgt[...] = jnp.zeros_like(a_dgt)

        def strip(rows):
            fv = f_ref[rows, :]
            gt = gt_ref[...]
            gv = g_ref[...]
            xv = x_ref[rows, :] + gt * fv
            r = lax.rsqrt(jnp.mean(xv * xv, axis=-1, keepdims=True) + EPS)
            xh = xv * r
            e = xh * gv - t_ref[rows, :]
            a_loss[...] += _fold8(e * e)
            dy = e * (1.0 / D)
            a_dg[...] += _fold8(dy * xh)
            t = dy * gv
            dx = r * (t - xh * jnp.mean(t * xh, axis=-1, keepdims=True))
            dx_ref[rows, :] = dx
            a_dgt[...] += _fold8(dx * fv)
            df_ref[rows, :] = (dx * gt).astype(bf16)

        _strips(tm, strip)

        @pl.when(i == n - 1)
        def _():
            tot = jnp.sum(jnp.sum(a_loss[...], axis=0, keepdims=True), axis=1, keepdims=True) * (0.5 / D)
            loss_ref[...] = jnp.broadcast_to(tot, (1, 128))
            dg_ref[...] = jnp.sum(a_dg[...], axis=0, keepdims=True)
            dgt_ref[...] = jnp.sum(a_dgt[...], axis=0, keepdims=True)

    return pl.pallas_call(
        body, name="final_fwd_bwd", grid=(n,),
        in_specs=[_tile_spec(tm, D), _tile_spec(tm, D), _row_spec(D), _row_spec(D), _tile_spec(tm, D)],
        out_specs=[_tile_spec(tm, D), _tile_spec(tm, D), _row_spec(128), _row_spec(D), _row_spec(D)],
        out_shape=[_sds((s, D), f32), _sds((s, D), bf16), _sds((1, 128), f32), _sds((1, D), f32), _sds((1, D), f32)],
        scratch_shapes=[pltpu.VMEM((8, D), f32)] * 3, compiler_params=_params(("arbitrary",)),
    )(x2, ffn, gt2, g, tgt)


def _norm_mod_bwd(name, dh, xin, g, sc, dres, tm, mix=None, gt=None):
    s = dh.shape[0]
    n = s // tm
    with_mix = mix is not None

    def body(*refs):
        if with_mix:
            dh_ref, x_ref, g_ref, sc_ref, dr_ref, m_ref, gt_ref, dx_ref, dm_ref, dsc_ref, dsh_ref, dg_ref, dgt_ref, a_sc, a_sh, a_g, a_gt = refs
        else:
            dh_ref, x_ref, g_ref, sc_ref, dr_ref, dx_ref, dsc_ref, dsh_ref, dg_ref, a_sc, a_sh, a_g = refs
        i = pl.program_id(0)

        @pl.when(i == 0)
        def _():
            a_sc[...] = jnp.zeros_like(a_sc)
            a_sh[...] = jnp.zeros_like(a_sh)
            a_g[...] = jnp.zeros_like(a_g)
            if with_mix:
                a_gt[...] = jnp.zeros_like(a_gt)

        def strip(rows):
            dh = dh_ref[rows, :]
            xv = x_ref[rows, :]
            gv = g_ref[...]
            r = lax.rsqrt(jnp.mean(xv * xv, axis=-1, keepdims=True) + EPS)
            xh = xv * r
            a_sc[...] += _fold8(dh * xh * gv)
            a_sh[...] += _fold8(dh)
            dn = dh * (1.0 + sc_ref[...])
            a_g[...] += _fold8(dn * xh)
            t = dn * gv
            dx = dr_ref[rows, :] + r * (t - xh * jnp.mean(t * xh, axis=-1, keepdims=True))
            dx_ref[rows, :] = dx
            if with_mix:
                a_gt[...] += _fold8(dx * m_ref[rows, :])
                dm_ref[rows, :] = (dx * gt_ref[...]).astype(bf16)

        _strips(tm, strip)

        @pl.when(i == n - 1)
        def _():
            dsc_ref[...] = jnp.sum(a_sc[...], axis=0, keepdims=True)
            dsh_ref[...] = jnp.sum(a_sh[...], axis=0, keepdims=True)
            dg_ref[...] = jnp.sum(a_g[...], axis=0, keepdims=True)
            if with_mix:
                dgt_ref[...] = jnp.sum(a_gt[...], axis=0, keepdims=True)

    tile, row = _tile_spec(tm, D), _row_spec(D)
    if with_mix:
        ins, args = [tile, tile, row, row, tile, tile, row], (dh, xin, g, sc, dres, mix, gt)
        outs = [tile, tile, row, row, row, row]
        shapes = [_sds((s, D), f32), _sds((s, D), bf16)] + [_sds((1, D), f32)] * 4
        nacc = 4
    else:
        ins, args = [tile, tile, row, row, tile], (dh, xin, g, sc, dres)
        outs = [tile, row, row, row]
        shapes = [_sds((s, D), f32)] + [_sds((1, D), f32)] * 3
        nacc = 3
    return pl.pallas_call(
        body, name=name, grid=(n,), in_specs=ins, out_specs=outs, out_shape=shapes,
        scratch_shapes=[pltpu.VMEM((8, D), f32)] * nacc, compiler_params=_params(("arbitrary",)),
    )(*args)


def _mix_pre(att, y, proj2, g_att, g_ssd, tm):
    s = att.shape[0]

    def body(a_ref, y_ref, z_ref, ga_ref, gs_ref, o_ref):
        def strip(rows):
            a = a_ref[rows, :]
            ra = lax.rsqrt(jnp.mean(a * a, axis=-1, keepdims=True) + EPS)
            o_ref[rows, 0:ATT_W] = (a * ra * ga_ref[...]).astype(bf16)
            z = z_ref[rows, :]
            u = y_ref[rows, :] * (z * _sigmoid(z))
            ru = lax.rsqrt(jnp.mean(u * u, axis=-1, keepdims=True) + EPS)
            o_ref[rows, ATT_W:] = (u * ru * gs_ref[...]).astype(bf16)

        _strips(tm, strip)

    t = _tile_spec(tm, ATT_W)
    return pl.pallas_call(
        body, name="mix_pre", grid=(s // tm,), in_specs=[t, t, t, _row_spec(ATT_W), _row_spec(SSD_W)],
        out_specs=_tile_spec(tm, D), out_shape=_sds((s, D), bf16), compiler_params=_params(("parallel",)),
    )(att, y, proj2, g_att, g_ssd)


def _mix_pre_bwd(dmc, att, y, proj2, g_att, g_ssd, tm):
    s = att.shape[0]
    n = s // tm

    def body(da_ref, ds_ref, a_ref, y_ref, z_ref, ga_ref, gs_ref, datt_ref, dy_ref, dz_ref, dga_ref, dgs_ref, acc_a, acc_s):
        i = pl.program_id(0)

        @pl.when(i == 0)
        def _():
            acc_a[...] = jnp.zeros_like(acc_a)
            acc_s[...] = jnp.zeros_like(acc_s)

        def strip(rows):
            a = a_ref[rows, :]
            ra = lax.rsqrt(jnp.mean(a * a, axis=-1, keepdims=True) + EPS)
            ah = a * ra
            dan = da_ref[rows, :]
            acc_a[...] += _fold8(dan * ah)
            t = dan * ga_ref[...]
            datt_ref[rows, :] = (ra * (t - ah * jnp.mean(t * ah, axis=-1, keepdims=True))).astype(bf16)
            z = z_ref[rows, :]
            yv = y_ref[rows, :]
            sz = _sigmoid(z)
            sil = z * sz
            u = yv * sil
            ru = lax.rsqrt(jnp.mean(u * u, axis=-1, keepdims=True) + EPS)
            uh = u * ru
            dsn = ds_ref[rows, :]
            acc_s[...] += _fold8(dsn * uh)
            t2 = dsn * gs_ref[...]
            du = ru * (t2 - uh * jnp.mean(t2 * uh, axis=-1, keepdims=True))
            dy_ref[rows, :] = du * sil
            dz_ref[rows, :] = (du * yv * (sz * (1.0 + z * (1.0 - sz)))).astype(bf16)

        _strips(tm, strip)

        @pl.when(i == n - 1)
        def _():
            dga_ref[...] = jnp.sum(acc_a[...], axis=0, keepdims=True)
            dgs_ref[...] = jnp.sum(acc_s[...], axis=0, keepdims=True)

    t = _tile_spec(tm, ATT_W)
    row = _row_spec(ATT_W)
    return pl.pallas_call(
        body, name="mix_pre_bwd", grid=(n,),
        in_specs=[_tile_spec(tm, ATT_W, 0), _tile_spec(tm, ATT_W, 1), t, t, t, row, row],
        out_specs=[t, t, t, row, row],
        out_shape=[_sds((s, ATT_W), bf16), _sds((s, SSD_W), f32), _sds((s, SSD_W), bf16), _sds((1, ATT_W), f32), _sds((1, SSD_W), f32)],
        scratch_shapes=[pltpu.VMEM((8, ATT_W), f32)] * 2, compiler_params=_params(("arbitrary",)),
    )(dmc, dmc, att, y, proj2, g_att, g_ssd)


ATT_GROUP = 8
ATT_GROUP_FWD = 16


def _pair_rows(qc):
    two = jnp.concatenate([qc, qc], axis=0)
    r = lax.broadcasted_iota(jnp.int32, (2 * CHUNK, 128), 0)
    l = lax.broadcasted_iota(jnp.int32, (2 * CHUNK, 128), 1)
    return jnp.where((r < CHUNK) == (l < HD), two, jnp.zeros_like(two))


def _scaled(q):
    return q * jnp.asarray(HD ** -0.5, q.dtype)


def _pair_scores(wt, kb, bias, r0, masked):
    sc = lax.dot_general(wt, kb, (((1,), (1,)), ((), ())), preferred_element_type=f32) + bias
    if not masked:
        return sc
    kidx = lax.broadcasted_iota(jnp.int32, sc.shape, 1)
    return jnp.where(r0 + kidx >= PADK, sc, -jnp.inf)


def _softmax(sc, axis):
    e = jnp.exp(sc - jnp.max(sc, axis=axis, keepdims=True))
    return e * (1.0 / jnp.sum(e, axis=axis, keepdims=True))


def _chunk_loops(nc, group, per_trip):
    n_masked = min(-(-LEFT // per_trip), nc // per_trip)

    def run(masked):
        def step(g, carry):
            group(g, masked)
            return carry
        return step

    lax.fori_loop(0, n_masked, run(True), 0)
    lax.fori_loop(n_masked, nc // per_trip, run(False), 0)


def _pair_diag(r):
    lane = lax.broadcasted_iota(jnp.int32, (CHUNK, 128), 1)
    return jnp.where(lane < HD, r[0:CHUNK], r[CHUNK:])


def _pad_keys(k_ref, kp, s):
    kp[0:PADK, :] = jnp.zeros((PADK, 128), bf16)
    kp[PADK:PADK + s, :] = k_ref[...]
    kp[PADK + s:, :] = jnp.zeros((CHUNK, 128), bf16)


def _attn_fwd(qkv, bias2):
    s = qkv.shape[0]
    nc = s // CHUNK
    npair = NH // 2
    per_trip = min(ATT_GROUP_FWD, nc)

    def body(q_ref, k_ref, v_ref, b_ref, o_ref, kp, vp):
        _pad_keys(k_ref, kp, s)
        _pad_keys(v_ref, vp, s)

        def group(g, masked):
            r0s = [pl.multiple_of((g * per_trip + u) * CHUNK, CHUNK) for u in range(per_trip)]
            scs = [_pair_scores(_pair_rows(_scaled(q_ref[pl.ds(r0, CHUNK), :])), kp[pl.ds(r0, BANDP), :], b_ref[...], r0, masked)
                   for r0 in r0s]
            ps = [_softmax(sc, -1).astype(bf16) for sc in scs]
            for r0, p in zip(r0s, ps):
                o_ref[pl.ds(r0, CHUNK), :] = _pair_diag(jnp.dot(p, vp[pl.ds(r0, BANDP), :], preferred_element_type=f32))

        _chunk_loops(nc, group, per_trip)

    return pl.pallas_call(
        body, name="attn_fwd", grid=(npair,),
        in_specs=[pl.BlockSpec((s, 128), lambda p: (0, p)), pl.BlockSpec((s, 128), lambda p: (0, npair + p)),
                  pl.BlockSpec((s, 128), lambda p: (0, 2 * npair + p)), pl.BlockSpec((None, 2 * CHUNK, BANDP), lambda p: (p, 0, 0))],
        out_specs=pl.BlockSpec((s, 128), lambda p: (0, p)), out_shape=_sds((s, ATT_W), f32),
        scratch_shapes=[pltpu.VMEM((PADK + s + CHUNK, 128), bf16)] * 2, compiler_params=_params(("parallel",)),
    )(qkv, qkv, qkv, bias2)


def _attn_bwd(qkv, datt, bias2):
    s = qkv.shape[0]
    nc = s // CHUNK
    npair = NH // 2
    rows = PADK + s + CHUNK
    nt = (((1,), (1,)), ((), ()))

    def body(q_ref, k_ref, v_ref, do_ref, b_ref, dq_ref, dk_ref, dv_ref, g_ref, kp, vp, dkp, dvp):
        _pad_keys(k_ref, kp, s)
        _pad_keys(v_ref, vp, s)
        dkp[...] = jnp.zeros_like(dkp)
        dvp[...] = jnp.zeros_like(dvp)
        g_ref[...] = jnp.zeros_like(g_ref)

        def group(g, masked):
            r0s = [pl.multiple_of((g * ATT_GROUP + u) * CHUNK, CHUNK) for u in range(ATT_GROUP)]
            wts = [_pair_rows(_scaled(q_ref[pl.ds(r0, CHUNK), :])) for r0 in r0s]
            dos = [_pair_rows(do_ref[pl.ds(r0, CHUNK), :]) for r0 in r0s]
            scs = [_pair_scores(wt, kp[pl.ds(r0, BANDP), :], b_ref[...], r0, masked) for wt, r0 in zip(wts, r0s)]
            dps = [lax.dot_general(do, vp[pl.ds(r0, BANDP), :], nt, preferred_element_type=f32) for do, r0 in zip(dos, r0s)]
            tn_ = (((0,), (0,)), ((), ()))
            for r0, wt, do, sc, dp in zip(r0s, wts, dos, scs, dps):
                p = _softmax(sc, -1)
                ds = p * (dp - jnp.sum(p * dp, axis=-1, keepdims=True))
                g_ref[...] += ds
                dsb = ds.astype(bf16)
                dq = jnp.dot(dsb, kp[pl.ds(r0, BANDP), :], preferred_element_type=f32)
                dq_ref[pl.ds(r0, CHUNK), :] = (_pair_diag(dq) * (HD ** -0.5)).astype(bf16)
                dkp[pl.ds(r0, BANDP), :] += lax.dot_general(dsb, wt, tn_, preferred_element_type=f32)
                dvp[pl.ds(r0, BANDP), :] += lax.dot_general(p.astype(bf16), do, tn_, preferred_element_type=f32)

        _chunk_loops(nc, group, ATT_GROUP)
        dk_ref[...] = dkp[PADK:PADK + s, :].astype(bf16)
        dv_ref[...] = dvp[PADK:PADK + s, :].astype(bf16)

    col = lambda off: pl.BlockSpec((s, 128), lambda p: (0, off + p))
    return pl.pallas_call(
        body, name="attn_bwd", grid=(npair,),
        in_specs=[col(0), col(npair), col(2 * npair), col(0), pl.BlockSpec((None, 2 * CHUNK, BANDP), lambda p: (p, 0, 0))],
        out_specs=[col(0), col(0), col(0), pl.BlockSpec((None, 2 * CHUNK, BANDP), lambda p: (p, 0, 0))],
        out_shape=[_sds((s, ATT_W), bf16)] * 3 + [_sds((npair, 2 * CHUNK, BANDP), f32)],
        scratch_shapes=[pltpu.VMEM((rows, 128), bf16)] * 2 + [pltpu.VMEM((rows, 128), f32)] * 2,
        compiler_params=_params(("parallel",)),
    )(qkv, qkv, qkv, datt, bias2)


def _rel_tables():
    onehot = np.zeros((BANDP, N_REL), np.float32)
    for j in range(BAND + CHUNK - 1):
        o = j - (CHUNK - 1)
        onehot[j, int(np.clip(PADK - o, -(CHUNK - 1), REL_CLIP)) + CHUNK - 1] = 1.0
    return onehot, np.ascontiguousarray(np.eye(CHUNK, dtype=np.float32)[::-1])


def _expand_bias(rel):
    ext = jnp.concatenate([jnp.broadcast_to(rel[:, N_REL - 1:], (NH, N_REL - 1)), rel[:, ::-1],
                           jnp.zeros((NH, BANDP - BAND + 1), f32)], axis=1)
    band = jnp.stack([ext[:, CHUNK - 1 - q:CHUNK - 1 - q + BANDP] for q in range(CHUNK)], axis=1)
    band = jnp.where(np.arange(BANDP) < BAND, band, -jnp.inf)
    return band.reshape(NH // 2, 2 * CHUNK, BANDP)


def _rel_bias_grad(gband):
    def body(g_ref, m_ref, flip_ref, o_ref, d2):
        for h in range(NH):
            rev = jnp.dot(flip_ref[...], g_ref[h], precision=HIGHEST, preferred_element_type=f32)
            rolled = pltpu.roll(rev, 0, 1, stride=1, stride_axis=0)
            d2[h:h + 1, :] = jnp.sum(rolled, axis=0, keepdims=True)
        o_ref[...] = jnp.dot(d2[...], m_ref[...], precision=HIGHEST, preferred_element_type=f32)

    onehot, flip = _rel_tables()
    return pl.pallas_call(
        body, name="rel_bias_grad", out_shape=_sds((NH, N_REL), f32), scratch_shapes=[pltpu.VMEM((NH, BANDP), f32)],
    )(gband, jnp.asarray(onehot), jnp.asarray(flip))


XBC_BLK = 512
XBC_COL0 = SSD_W // XBC_BLK
DT_COL = (SSD_W + XBC) // 128


def _conv_taps(ext, w_ref, b_ref, tm):
    n = ext.shape[0]
    pre = w_ref[3:4, :] * ext + b_ref[...]
    for j in range(3):
        pre = pre + w_ref[j:j + 1, :] * pltpu.roll(ext, 3 - j, 0)
    return pre


def _ssd_conv(proj2, conv_w, conv_b, tm):
    s = proj2.shape[0]
    nb = XBC // XBC_BLK

    def body(x_ref, p_ref, w_ref, b_ref, o_ref):
        i = pl.program_id(1)
        prev = jnp.where(i > 0, p_ref[...], 0.0)
        ext = jnp.concatenate([prev, x_ref[...]], axis=0)
        pre = _conv_taps(ext, w_ref, b_ref, tm)[8:8 + tm]
        o_ref[...] = pre * _sigmoid(pre)

    return pl.pallas_call(
        body, name="ssd_conv", grid=(nb, s // tm),
        in_specs=[pl.BlockSpec((tm, XBC_BLK), lambda j, i: (i, XBC_COL0 + j)),
                  pl.BlockSpec((8, XBC_BLK), lambda j, i: (jnp.maximum(i * (tm // 8) - 1, 0), XBC_COL0 + j)),
                  pl.BlockSpec((4, XBC_BLK), lambda j, i: (0, j)), pl.BlockSpec((1, XBC_BLK), lambda j, i: (0, j))],
        out_specs=pl.BlockSpec((tm, XBC_BLK), lambda j, i: (i, j)), out_shape=_sds((s, XBC), f32),
        compiler_params=_params(("parallel", "parallel")),
    )(proj2, proj2, conv_w, conv_b)


def _ssd_conv_bwd(dxbc, proj2, conv_w, conv_b, tm):
    s = proj2.shape[0]
    nb = XBC // XBC_BLK
    n = s // tm
    last8 = s // 8 - 1

    def body(x_ref, xp_ref, xn_ref, d_ref, dn_ref, w_ref, b_ref, o_ref, dw_ref, db_ref, acc):
        i = pl.program_id(1)

        @pl.when(i == 0)
        def _():
            acc[...] = jnp.zeros_like(acc)

        prev = jnp.where(i > 0, xp_ref[...], 0.0)
        ext = jnp.concatenate([prev, x_ref[...], xn_ref[...]], axis=0)
        pre = _conv_taps(ext, w_ref, b_ref, tm)
        sg = _sigmoid(pre)
        dnext = jnp.where(i < n - 1, dn_ref[...], 0.0)
        dext = jnp.concatenate([jnp.zeros((8, XBC_BLK), f32), d_ref[...], dnext], axis=0)
        dpre = dext * (sg * (1.0 + pre * (1.0 - sg)))
        rows = tm + 16
        dx = w_ref[3:4, :] * dpre
        for j in range(3):
            dx = dx + w_ref[j:j + 1, :] * pltpu.roll(dpre, rows - (3 - j), 0)
        o_ref[...] = dx[8:8 + tm].astype(bf16)
        dcur = dpre[8:8 + tm]
        acc[4] += _fold8(dcur)
        acc[3] += _fold8(dcur * ext[8:8 + tm])
        for j in range(3):
            acc[j] += _fold8(dcur * pltpu.roll(ext, 3 - j, 0)[8:8 + tm])

        @pl.when(i == n - 1)
        def _():
            for j in range(4):
                dw_ref[j:j + 1, :] = jnp.sum(acc[j], axis=0, keepdims=True)
            db_ref[...] = jnp.sum(acc[4], axis=0, keepdims=True)

    xcol = lambda j: XBC_COL0 + j
    return pl.pallas_call(
        body, name="ssd_conv_bwd", grid=(nb, n),
        in_specs=[pl.BlockSpec((tm, XBC_BLK), lambda j, i: (i, xcol(j))),
                  pl.BlockSpec((8, XBC_BLK), lambda j, i: (jnp.maximum(i * (tm // 8) - 1, 0), xcol(j))),
                  pl.BlockSpec((8, XBC_BLK), lambda j, i: (jnp.minimum((i + 1) * (tm // 8), last8), xcol(j))),
                  pl.BlockSpec((tm, XBC_BLK), lambda j, i: (i, j)),
                  pl.BlockSpec((8, XBC_BLK), lambda j, i: (jnp.minimum((i + 1) * (tm // 8), last8), j)),
                  pl.BlockSpec((4, XBC_BLK), lambda j, i: (0, j)), pl.BlockSpec((1, XBC_BLK), lambda j, i: (0, j))],
        out_specs=[pl.BlockSpec((tm, XBC_BLK), lambda j, i: (i, j)), pl.BlockSpec((4, XBC_BLK), lambda j, i: (0, j)),
                   pl.BlockSpec((1, XBC_BLK), lambda j, i: (0, j))],
        out_shape=[_sds((s, XBC), bf16), _sds((4, XBC), f32), _sds((1, XBC), f32)],
        scratch_shapes=[pltpu.VMEM((5, 8, XBC_BLK), f32)], compiler_params=_params(("parallel", "arbitrary")),
    )(proj2, proj2, proj2, dxbc, dxbc, conv_w, conv_b)


def _ssd_consts():
    ex = np.zeros((128, SSD_W), np.float32)
    for h in range(NH):
        ex[h, h * HD:(h + 1) * HD] = 1.0
    sel = np.zeros((8, 128), np.float32)
    for h in range(NH):
        sel[h // 2, h] = 1.0
    par = np.zeros((128, 128), np.float32)
    for r in range(128):
        for h in range(NH):
            par[r, h] = 1.0 if (h % 2) == (r // 64) else 0.0
    ones_blk = np.zeros((128, 128), np.float32)
    for r in range(128):
        ones_blk[r, (r // 64) * 64:(r // 64) * 64 + 64] = 1.0
    return ex, np.ascontiguousarray(ex.T), sel, par, ones_blk


SSD_SUB = 8


def _ssd_common(rs, xbc_ref, dtr_ref, a_ref, dtb_ref, ex_ref, sel_ref, par_ref):
    xs = xbc_ref[rs, 0:SSD_W]
    dt = _softplus(dtr_ref[rs, :] + dtb_ref[...])
    adt = dt * a_ref[...]
    r_i = lax.broadcasted_iota(jnp.int32, (CHUNK, CHUNK), 0)
    c_i = lax.broadcasted_iota(jnp.int32, (CHUNK, CHUNK), 1)
    tril = (r_i >= c_i).astype(f32)
    cs = _dot01(tril, adt, exact="a")
    cs2 = jnp.concatenate([cs, cs], axis=0) * par_ref[...]
    cstp = _dot01(sel_ref[...], cs2, tb=True, exact="a")
    both = _dot01(jnp.concatenate([dt, cs], axis=0), ex_ref[...])
    return xs, dt, cs, cstp, both[0:CHUNK], both[CHUNK:]


def _pair_mask():
    l_i = lax.broadcasted_iota(jnp.int32, (CHUNK, 128), 0)
    lane = lax.broadcasted_iota(jnp.int32, (CHUNK, 128), 1)
    return l_i >= (lane % CHUNK), lane < HD


def _block_diag(xp, first):
    z = jnp.zeros_like(xp)
    return jnp.concatenate([jnp.where(first, xp, z), jnp.where(first, z, xp)], axis=0)


def _ssd_fwd(xbc, proj2, a_row, dtb_row, dsk_full):
    s = xbc.shape[0]
    nc = s // CHUNK
    ex, ext, sel, par, ones_blk = _ssd_consts()

    def one_chunk(sub, states, refs):
        xbc_ref, dtr_ref, a_ref, dtb_ref, dsk_ref, ex_ref, sel_ref, par_ref, y_ref, hs_ref = refs
        rs = slice(sub * CHUNK, (sub + 1) * CHUNK)
        xs, dt, cs, cstp, dt_full, cs_full = _ssd_common(rs, xbc_ref, dtr_ref, a_ref, dtb_ref, ex_ref, sel_ref, par_ref)
        cs_last = cs_full[CHUNK - 1:CHUNK, :]
        xdt = xs * dt_full
        causal, first = _pair_mask()
        out = []
        for g in range(NG):
            gl = slice(g * GW, (g + 1) * GW)
            bg = xbc_ref[rs, SSD_W + g * NSTATE:SSD_W + (g + 1) * NSTATE].astype(bf16)
            cg = xbc_ref[rs, SSD_W + NG * NSTATE + g * NSTATE:SSD_W + NG * NSTATE + (g + 1) * NSTATE].astype(bf16)
            cb2 = lax.dot_general(cg, jnp.concatenate([bg, bg], axis=0), (((1,), (1,)), ((), ())), preferred_element_type=f32)
            hg = states[g]
            hs_ref[sub, g] = hg
            y0 = jnp.dot(cg, hg.astype(bf16), preferred_element_type=f32)
            yoff = jnp.exp(cs_full[:, gl]) * y0
            for j in range(GW // 128):
                pair = g * (GW // 128) + j
                pl_ = slice(pair * 128, (pair + 1) * 128)
                seg = jnp.exp(jnp.where(causal, cs_full[:, pl_] - cstp[pair:pair + 1, :], -jnp.inf))
                m = (cb2 * seg).astype(bf16)
                yd = jnp.dot(m, _block_diag(xdt[:, pl_].astype(bf16), first), preferred_element_type=f32)
                y_ref[rs, pl_] = yd + yoff[:, j * 128:(j + 1) * 128] + xs[:, pl_] * dsk_ref[:, pl_]
            xdec = (xdt[:, gl] * jnp.exp(cs_last[:, gl] - cs_full[:, gl])).astype(bf16)
            st = lax.dot_general(bg, xdec, (((0,), (0,)), ((), ())), preferred_element_type=f32)
            out.append(jnp.exp(cs_last[:, gl]) * hg + st)
        return out

    def body(*refs):
        hst = refs[-1]

        @pl.when(pl.program_id(0) == 0)
        def _():
            hst[...] = jnp.zeros_like(hst)

        states = [hst[g] for g in range(NG)]
        for sub in range(SSD_SUB):
            states = one_chunk(sub, states, refs[:-1])
        for g in range(NG):
            hst[g] = states[g]

    rows = SSD_SUB * CHUNK
    const = lambda shape: pl.BlockSpec(shape, lambda c: tuple(0 for _ in shape))
    return pl.pallas_call(
        body, name="ssd_fwd", grid=(nc // SSD_SUB,),
        in_specs=[pl.BlockSpec((rows, XBC), lambda c: (c, 0)), pl.BlockSpec((rows, 128), lambda c: (c, DT_COL)),
                  const((1, 128)), const((1, 128)), const((1, SSD_W)), const((128, SSD_W)), const((8, 128)), const((128, 128))],
        out_specs=[pl.BlockSpec((rows, SSD_W), lambda c: (c, 0)), pl.BlockSpec((SSD_SUB, NG, NSTATE, GW), lambda c: (c, 0, 0, 0))],
        out_shape=[_sds((s, SSD_W), f32), _sds((nc, NG, NSTATE, GW), f32)],
        scratch_shapes=[pltpu.VMEM((NG, NSTATE, GW), f32)], compiler_params=_params(("arbitrary",)),
    )(xbc, proj2, a_row, dtb_row, dsk_full, jnp.asarray(ex), jnp.asarray(sel), jnp.asarray(par))


def _ssd_bwd(xbc, proj2, dy, hsave, a_row, dtb_row, dsk_full):
    s = xbc.shape[0]
    nc = s // CHUNK
    ex, ext, sel, par, ones_blk = _ssd_consts()

    def one_chunk(sub, dhs, refs):
        (xbc_ref, dtr_ref, dy_ref, hs_ref, a_ref, dtb_ref, dsk_ref, ex_ref, ext_ref, sel_ref, par_ref, ob_ref,
         dxbc_ref, ddtr_ref, dd_ref, da_ref, ddtb_ref, dh, a_dd, a_da, a_dtb, dcs_lane, dcs_b, dxdt) = refs
        rs = slice(sub * CHUNK, (sub + 1) * CHUNK)
        dcs_lane, dcs_b, dxdt = dcs_lane.at[sub], dcs_b.at[sub], dxdt.at[sub]
        xs, dt, cs, cstp, dt_full, cs_full = _ssd_common(rs, xbc_ref, dtr_ref, a_ref, dtb_ref, ex_ref, sel_ref, par_ref)
        cs_last = cs_full[CHUNK - 1:CHUNK, :]
        xdt = xs * dt_full
        dyv = dy_ref[rs, :]
        a_dd[...] += _fold8(dyv * xs)
        causal, first = _pair_mask()
        diag = lax.broadcasted_iota(jnp.int32, (CHUNK, 128), 0) == lax.broadcasted_iota(jnp.int32, (CHUNK, 128), 1) % CHUNK
        dh_out = []
        for g in range(NG):
            gl = slice(g * GW, (g + 1) * GW)
            bcol = slice(SSD_W + g * NSTATE, SSD_W + (g + 1) * NSTATE)
            ccol = slice(SSD_W + NG * NSTATE + g * NSTATE, SSD_W + NG * NSTATE + (g + 1) * NSTATE)
            bg = xbc_ref[rs, bcol].astype(bf16)
            cg = xbc_ref[rs, ccol].astype(bf16)
            bg2 = jnp.concatenate([bg, bg], axis=0)
            cb2 = lax.dot_general(cg, bg2, (((1,), (1,)), ((), ())), preferred_element_type=f32)
            hg = hs_ref[sub, g]
            hgb = hg.astype(bf16)
            dhg = dhs[g]
            dhgb = dhg.astype(bf16)
            eg = jnp.exp(cs_full[:, gl])
            dec = jnp.exp(cs_last[:, gl] - cs_full[:, gl])
            gam = jnp.exp(cs_last[:, gl])
            dyg = dyv[:, gl]
            xdt_g = xdt[:, gl]
            y0 = jnp.dot(cg, hgb, preferred_element_type=f32)
            dy0 = (eg * dyg).astype(bf16)
            dcm = lax.dot_general(dy0, hgb, (((1,), (1,)), ((), ())), preferred_element_type=f32)
            dh_prev = gam * dhg + lax.dot_general(cg, dy0, (((0,), (0,)), ((), ())), preferred_element_type=f32)
            dgam = jnp.sum(dhg * hg, axis=0, keepdims=True) * gam
            dxdec = jnp.dot(bg, dhgb, preferred_element_type=f32)
            dbm = lax.dot_general((xdt_g * dec).astype(bf16), dhgb, (((1,), (1,)), ((), ())), preferred_element_type=f32)
            t = dxdec * xdt_g * dec
            dcs_lane[:, gl] = dyg * eg * y0 - t
            dcs_lane[CHUNK - 1:CHUNK, gl] += jnp.sum(t, axis=0, keepdims=True) + dgam
            dxdt[:, gl] = dxdec * dec
            dcb2 = jnp.zeros((CHUNK, 128), f32)
            for j in range(GW // 128):
                pair = g * (GW // 128) + j
                pl_ = slice(pair * 128, (pair + 1) * 128)
                seg = jnp.exp(jnp.where(causal, cs_full[:, pl_] - cstp[pair:pair + 1, :], -jnp.inf))
                m = cb2 * seg
                mb = m.astype(bf16)
                rhs = _block_diag(xdt[:, pl_].astype(bf16), first)
                dyp = dyv[:, pl_].astype(bf16)
                dm = lax.dot_general(dyp, rhs, (((1,), (1,)), ((), ())), preferred_element_type=f32)
                tt = lax.dot_general(mb, dyp, (((0,), (0,)), ((), ())), preferred_element_type=f32)
                dxdt[:, pl_] += jnp.where(first, tt[0:CHUNK], tt[CHUNK:])
                dcb2 = dcb2 + dm * seg
                w = dm * m
                colsum = jnp.sum(w, axis=0, keepdims=True)
                dcs_b[:, pl_] = _dot01(w - jnp.where(diag, colsum, 0.0), ob_ref[...])
            dcb2b = dcb2.astype(bf16)
            dcm = dcm + jnp.dot(dcb2b, bg2, preferred_element_type=f32)
            t3 = lax.dot_general(dcb2b, cg, (((0,), (0,)), ((), ())), preferred_element_type=f32)
            dxbc_ref[rs, bcol] = dbm + t3[0:CHUNK] + t3[CHUNK:]
            dxbc_ref[rs, ccol] = dcm
            dh_out.append(dh_prev)
        dxdtv = dxdt[...]
        both = _dot01(jnp.concatenate([dcs_lane[...] + dcs_b[...] * (1.0 / HD), dxdtv * xs], axis=0), ext_ref[...])
        dcs = both[0:CHUNK]
        r_i = lax.broadcasted_iota(jnp.int32, (CHUNK, CHUNK), 0)
        c_i = lax.broadcasted_iota(jnp.int32, (CHUNK, CHUNK), 1)
        triu = (r_i <= c_i).astype(f32)
        da_ = _dot01(triu, dcs, exact="a")
        ddt = da_ * a_ref[...] + both[CHUNK:]
        a_da[...] += _fold8(da_ * dt)
        dxbc_ref[rs, 0:SSD_W] = dyv * dsk_ref[...] + dxdtv * dt_full
        ddtr = ddt * _sigmoid(dtr_ref[rs, :] + dtb_ref[...])
        ddtr_ref[rs, :] = ddtr
        a_dtb[...] += _fold8(ddtr)
        return dh_out

    nsteps = nc // SSD_SUB

    def body(*refs):
        dd_ref, da_ref, ddtb_ref, dh, a_dd, a_da, a_dtb = refs[14:21]
        ext_ref = refs[8]
        step = pl.program_id(0)

        @pl.when(step == 0)
        def _():
            dh[...] = jnp.zeros_like(dh)
            a_dd[...] = jnp.zeros_like(a_dd)
            a_da[...] = jnp.zeros_like(a_da)
            a_dtb[...] = jnp.zeros_like(a_dtb)

        dhs = [dh[g] for g in range(NG)]
        for sub in reversed(range(SSD_SUB)):
            dhs = one_chunk(sub, dhs, refs)
        for g in range(NG):
            dh[g] = dhs[g]

        @pl.when(step == nsteps - 1)
        def _():
            dd_ref[...] = jnp.sum(jnp.dot(a_dd[...], ext_ref[...], precision=HIGHEST, preferred_element_type=f32), axis=0, keepdims=True)
            da_ref[...] = jnp.sum(a_da[...], axis=0, keepdims=True)
            ddtb_ref[...] = jnp.sum(a_dtb[...], axis=0, keepdims=True)

    rev = lambda c: nsteps - 1 - c
    rows = SSD_SUB * CHUNK
    const = lambda shape: pl.BlockSpec(shape, lambda c: tuple(0 for _ in shape))
    return pl.pallas_call(
        body, name="ssd_bwd", grid=(nsteps,),
        in_specs=[pl.BlockSpec((rows, XBC), lambda c: (rev(c), 0)), pl.BlockSpec((rows, 128), lambda c: (rev(c), DT_COL)),
                  pl.BlockSpec((rows, SSD_W), lambda c: (rev(c), 0)), pl.BlockSpec((SSD_SUB, NG, NSTATE, GW), lambda c: (rev(c), 0, 0, 0)),
                  const((1, 128)), const((1, 128)), const((1, SSD_W)), const((128, SSD_W)), const((SSD_W, 128)),
                  const((8, 128)), const((128, 128)), const((128, 128))],
        out_specs=[pl.BlockSpec((rows, XBC), lambda c: (rev(c), 0)), pl.BlockSpec((rows, 128), lambda c: (rev(c), 0)),
                   const((1, 128)), const((1, 128)), const((1, 128))],
        out_shape=[_sds((s, XBC), f32), _sds((s, 128), f32), _sds((1, 128), f32), _sds((1, 128), f32), _sds((1, 128), f32)],
        scratch_shapes=[pltpu.VMEM((NG, NSTATE, GW), f32), pltpu.VMEM((8, SSD_W), f32), pltpu.VMEM((8, 128), f32), pltpu.VMEM((8, 128), f32)]
        + [pltpu.VMEM((SSD_SUB, CHUNK, SSD_W), f32)] * 3,
        compiler_params=_params(("arbitrary",)),
    )(xbc, proj2, dy, hsave, a_row, dtb_row, dsk_full, jnp.asarray(ex), jnp.asarray(ext), jnp.asarray(sel), jnp.asarray(par),
      jnp.asarray(ones_blk))


def _local_step(x, tgt, mods, g_mix, rel, conv_w, conv_b, dt_bias, a_log, d_skip, g_att, g_ssd, g_ffn, g_final, weights):
    s = x.shape[0]
    tm_e = 512 if s % 512 == 0 else s
    tm_m = 512 if s % 512 == 0 else s
    tm_l = 1024 if s % 1024 == 0 else s
    tk = 2048 if s % 2048 == 0 else s
    sh1, sc1, gt1, sh2, sc2, gt2 = [mods[:, i * D:(i + 1) * D] for i in range(6)]

    h1b = _norm_mod("norm_mod_1", x, g_mix, sc1, sh1, tm_e)
    win, win_b = weights.w_in(h1b)
    qkv = _mm_nn_fullk("proj_qkv", h1b, win, tm_l, 1536, bf16, n=IN_A)
    proj2 = _mm_nn_fullk("proj_zxbcdt", h1b, win_b, tm_l, 896, f32)
    bias = _expand_bias(rel)
    att = _attn_fwd(qkv, bias)
    xbc = _ssd_conv(proj2, conv_w, conv_b, tm_l)
    a_row = jnp.pad(-jnp.exp(a_log), ((0, 0), (0, 128 - NH)))
    dtb_row = jnp.pad(dt_bias, ((0, 0), (0, 128 - NH)))
    dsk_full = jnp.repeat(d_skip, HD, axis=1)
    y, hsave = _ssd_fwd(xbc, proj2, a_row, dtb_row, dsk_full)
    mixcat = _mix_pre(att, y, proj2, g_att, g_ssd, tm_e)
    wout = weights.w_out(mixcat)
    mix = _mm_nn_fullk("proj_out", mixcat, wout, tm_l, D, f32)
    x2, h2b = _resid_norm_mod(x, gt1, mix, g_ffn, sc2, sh2, tm_e)
    wg4, wu4, wd4 = weights.ffn(h2b)
    act, sil, ud = _ffn_up(h2b, wg4, wu4, tm_m)
    ffn = _ffn_down(act, wd4, tm_m)

    dx3, dffn, loss, dg_final, dgt2 = _final_fwd_bwd(x2, ffn, gt2, g_final, tgt, tm_e)
    gwd4 = _grad_wdown4(act, dffn, 1024, tk)
    dgate, dup = _ffn_dact(dffn, wd4, sil, ud, tm_l)
    tk2 = 4096 if s % 4096 == 0 else s
    tok = weights.grad(("w_down", "w_gate", "w_up"),
                       [gwd4, _grad_cols4("grad_w_gate", h2b, dgate, 512, tk2), _grad_cols4("grad_w_up", h2b, dup, 512, tk2)])
    dh2 = _ffn_dh(dgate, dup, wg4, wu4, tm_m, dep=tok)
    dx2, dmix, dsc2, dsh2, dg_ffn, dgt1 = _norm_mod_bwd("norm_mod_bwd_2", dh2, x2, g_ffn, sc2, dx3, tm_e, mix=mix, gt=gt1)
    gwout4 = _mm_tn("grad_w_out", mixcat, dmix, 512, 1024, tk2, bf16).reshape(NSH, D // NSH, D)
    dmc = _mm_nt("dmixcat", dmix, wout, tm_l, D, D, f32)
    datt, dy, dz, dg_att, dg_ssd = _mix_pre_bwd(dmc, att, y, proj2, g_att, g_ssd, tm_e)
    dq, dk, dv, gband = _attn_bwd(qkv, datt, bias)
    drel = _rel_bias_grad(gband.reshape(NH, CHUNK, BANDP))
    dxbc, ddtr, dd_row, da_row, ddtb_row = _ssd_bwd(xbc, proj2, dy, hsave, a_row, dtb_row, dsk_full)
    dxbc_raw, dconv_w, dconv_b = _ssd_conv_bwd(dxbc, proj2, conv_w, conv_b, tm_e)
    dproj = jnp.concatenate([dq, dk, dv, dz, dxbc_raw, ddtr.astype(bf16)], axis=1)
    gwin = _mm_tn("grad_w_in", h1b, dproj, 512, 1152, tk2, bf16)
    gwin4 = jnp.stack([jnp.pad(gwin[:, k * IN_SH:(k + 1) * IN_SH], ((0, 0), (0, IN_SHP - IN_SH))) for k in range(NSH)])
    tok = weights.grad(("w_out", "w_in"), [gwout4, gwin4])
    dh1 = _mm_nt("dh1", dproj, win, tm_m, 1024, IN_P, f32, dep=tok)
    grad_x, dsc1, dsh1, dg_mix = _norm_mod_bwd("norm_mod_bwd_1", dh1, x, g_mix, sc1, dx2, tm_e)

    dmods = jnp.concatenate([dsh1, dsc1, dgt1, dsh2, dsc2, dgt2], axis=1)
    dd_skip = dd_row[:, :NH]
    da_log = da_row[:, :NH] * a_row[:, :NH]
    small = dict(g_mix=dg_mix, conv_b=dconv_b, dt_bias=ddtb_row[:, :NH], a_log=da_log, d_skip=dd_skip, g_att_out=dg_att,
                 g_ssd_out=dg_ssd, g_ffn=dg_ffn, g_final=dg_final, rel_bias=drel, conv_w=dconv_w)
    return loss[0, 0], grad_x, dmods, small


HBM = pl.BlockSpec(memory_space=pl.ANY)
VMEM = pl.BlockSpec(memory_space=pltpu.VMEM)


def _place():
    x, y, c = lax.axis_index("x"), lax.axis_index("y"), lax.axis_index("c")
    chips = [(1 - x, y), (x, 1 - y), (1 - x, 1 - y)]
    return x, y, c, chips


def _allgather8(name, payload, dep=None):
    r = payload.shape[0]
    deps = [] if dep is None else [dep]

    def body(x_ref, *rest):
        out_ref, send_sems, recv_sems, local_sem = rest[-4:]
        x, y, c, chips = _place()
        me, sibling = (x, y, c), (x, y, 1 - c)

        def slot(px, py, pc):
            return out_ref.at[4 * px + 2 * py + pc]

        def copy(k, block, to, src=None):
            return pltpu.make_async_remote_copy(
                src_ref=slot(*block) if src is None else src, dst_ref=slot(*block),
                send_sem=send_sems.at[k], recv_sem=recv_sems.at[k], device_id=to, device_id_type=MESH)

        mine = pltpu.make_async_copy(x_ref, slot(*me), local_sem)
        mine.start()
        first = [copy(0, me, sibling, src=x_ref)]
        first += [copy(1 + j, me, (*chip, c), src=x_ref) for j, chip in enumerate(chips)]
        for cp in first:
            cp.start()
        passed = [copy(4 + j, (*chip, c), sibling) for j, chip in enumerate(chips)]
        for j, chip in enumerate(chips):
            copy(1 + j, (*chip, c), me).wait_recv()
            passed[j].start()
        copy(0, sibling, me).wait_recv()
        for j, chip in enumerate(chips):
            copy(4 + j, (*chip, 1 - c), me).wait_recv()
        for cp in first + passed:
            cp.wait_send()
        mine.wait()

    return pl.pallas_call(
        body, name=name, out_shape=_sds((N_DEV, r, 128), f32), in_specs=[VMEM] * (1 + len(deps)), out_specs=VMEM,
        scratch_shapes=[pltpu.SemaphoreType.DMA((7,)), pltpu.SemaphoreType.DMA((7,)), pltpu.SemaphoreType.DMA],
    )(payload, *deps)


def _sum8(g):
    r = g.shape[1]

    def body(g_ref, o_ref):
        acc = g_ref[0]
        for i in range(1, N_DEV):
            acc = acc + g_ref[i]
        o_ref[...] = acc

    return pl.pallas_call(body, name="sum8", out_shape=_sds((r, 128), f32))(g)


SEM = pl.BlockSpec(memory_space=pltpu.SEMAPHORE)
EFFECT = pltpu.SideEffectType.DATAFLOW_SIDE_EFFECTING


def _gather_copies(ins, lands, send_sems, recv_sems):
    x, y, c, chips = _place()
    k = 2 * x + y
    starts, recvs = [], []
    for w in range(len(ins)):
        for j, (px, py) in enumerate(chips):
            def mk(dst):
                return pltpu.make_async_remote_copy(src_ref=ins[w].at[c], dst_ref=dst, send_sem=send_sems[w].at[j],
                                                    recv_sem=recv_sems[w].at[j], device_id=(px, py, c), device_id_type=MESH)
            starts.append(mk(lands[w].at[k, c]))
            recvs.append(mk(lands[w].at[2 * px + py, c]))
    return starts, recvs


def _reduce_copies(ins, lands, send_sems, recv_sems):
    x, y, c, chips = _place()
    k = 2 * x + y
    starts, recvs = [], []
    for w in range(len(ins)):
        for j, (px, py) in enumerate(chips):
            def mk(dst):
                return pltpu.make_async_remote_copy(src_ref=ins[w].at[2 * px + py], dst_ref=dst, send_sem=send_sems[w].at[j],
                                                    recv_sem=recv_sems[w].at[j], device_id=(px, py, c), device_id_type=MESH)
            starts.append(mk(lands[w].at[k]))
            recvs.append(mk(lands[w].at[2 * px + py]))
    return starts, recvs


def _split_start(name, copies, srcs, land_shapes):
    nw = len(srcs)

    def body(*refs):
        starts, _ = copies(refs[:nw], refs[nw:2 * nw], refs[2 * nw:3 * nw], refs[3 * nw:4 * nw])
        for cp in starts:
            cp.start()
        refs[6 * nw][...] = jnp.zeros((8, 128), f32)

    sems = [pltpu.SemaphoreType.DMA((3,))] * nw
    bufs = [pltpu.HBM(s.shape, bf16) for s in srcs] + [pltpu.HBM(s, bf16) for s in land_shapes]
    res = pl.pallas_call(
        body, name=name, out_shape=sems + sems + bufs + [_sds((8, 128), f32)],
        in_specs=[HBM] * (2 * nw), out_specs=[SEM] * (2 * nw) + [HBM] * (2 * nw) + [VMEM],
        input_output_aliases={i: 2 * nw + i for i in range(2 * nw)},
        compiler_params=pltpu.CompilerParams(has_side_effects=EFFECT),
    )(*[pltpu.with_memory_space_constraint(s, pltpu.HBM) for s in srcs],
      *[pltpu.with_memory_space_constraint(lax.empty(s, bf16), pltpu.HBM) for s in land_shapes])
    return res[:nw], res[nw:2 * nw], res[2 * nw:3 * nw], res[3 * nw:4 * nw], res[4 * nw]


def _split_wait(name, copies, send_sems, recv_sems, srcs, lands, after):
    nw = len(srcs)

    def body(*refs):
        starts, recvs = copies(refs[:nw], refs[nw:2 * nw], refs[2 * nw:3 * nw], refs[3 * nw:4 * nw])
        for s_, r_ in zip(starts, recvs):
            s_.wait_send()
            r_.wait_recv()

    bufs = [pltpu.HBM(s.shape, bf16) for s in srcs] + [pltpu.HBM(l.shape, bf16) for l in lands]
    res = pl.pallas_call(
        body, name=name, out_shape=bufs, in_specs=[HBM] * (2 * nw) + [SEM] * (2 * nw) + [HBM], out_specs=[HBM] * (2 * nw),
        input_output_aliases={i: i for i in range(2 * nw)},
        compiler_params=pltpu.CompilerParams(has_side_effects=EFFECT),
    )(*srcs, *lands, *send_sems, *recv_sems, after)
    return res[:nw], res[nw:]


def _gather_forward(name, shards, lands):
    nw = len(shards)

    def body(*refs):
        ins, lands_in, outs = refs[:nw], refs[nw:2 * nw], refs[2 * nw:3 * nw]
        st_a, st_b, st_c = refs[3 * nw:4 * nw], refs[4 * nw:5 * nw], refs[5 * nw:6 * nw]
        send_sems, recv_sems, load_sems, store_sems = refs[6 * nw:]
        x, y, c, chips = _place()
        k = 2 * x + y
        sibling = (x, y, 1 - c)
        ld_a = [pltpu.make_async_copy(ins[w].at[c], st_a[w], load_sems.at[w, 0]) for w in range(nw)]
        ld_b = [pltpu.make_async_copy(ins[w].at[1 - c], st_b[w], load_sems.at[w, 1]) for w in range(nw)]
        for cp in ld_a + ld_b:
            cp.start()
        st_own = []
        for w in range(nw):
            ld_a[w].wait()
            st_own.append(pltpu.make_async_copy(st_a[w], outs[w].at[k, c], store_sems.at[w, 0]))
            st_own[-1].start()
        for w in range(nw):
            ld_b[w].wait()
            st_own.append(pltpu.make_async_copy(st_b[w], outs[w].at[k, 1 - c], store_sems.at[w, 1]))
            st_own[-1].start()
        for cp in st_own:
            cp.wait()
        fwds = {}
        for j, (px, py) in enumerate(chips):
            kq = 2 * px + py
            for w in range(nw):
                slot = st_b[w] if j % 2 == 0 else st_c[w]
                if j == 2:
                    fwds[w, 0].wait_send()
                ld = pltpu.make_async_copy(lands_in[w].at[kq, c], slot, load_sems.at[w, 2 + j])
                ld.start()
                ld.wait()
                fwds[w, j] = pltpu.make_async_remote_copy(src_ref=slot, dst_ref=outs[w].at[kq, c], send_sem=send_sems.at[w, j],
                                                          recv_sem=recv_sems.at[w, j], device_id=sibling, device_id_type=MESH)
                fwds[w, j].start()
        for j, (px, py) in enumerate(chips):
            for w in range(nw):
                pltpu.make_async_remote_copy(src_ref=st_c[w], dst_ref=outs[w].at[2 * px + py, 1 - c], send_sem=send_sems.at[w, j],
                                             recv_sem=recv_sems.at[w, j], device_id=sibling, device_id_type=MESH).wait_recv()
        for w in range(nw):
            fwds[w, 1].wait_send()
            fwds[w, 2].wait_send()

    stage = [pltpu.VMEM(s.shape[1:], bf16) for s in shards]
    return pl.pallas_call(
        body, name=name, out_shape=[_sds(l.shape, bf16) for l in lands],
        in_specs=[HBM] * (2 * nw), out_specs=[HBM] * nw, input_output_aliases={nw + w: w for w in range(nw)},
        scratch_shapes=stage * 3 + [pltpu.SemaphoreType.DMA((nw, 3)), pltpu.SemaphoreType.DMA((nw, 3)), pltpu.SemaphoreType.DMA((nw, 5)),
                                    pltpu.SemaphoreType.DMA((nw, 2))],
        compiler_params=pltpu.CompilerParams(vmem_limit_bytes=VMEM_LIMIT),
    )(*shards, *lands)


def _rs_pair_exchange(name, grads):
    nw = len(grads)

    def body(*refs):
        ins, got, stage = refs[:nw], refs[nw:2 * nw], refs[2 * nw:3 * nw]
        send_sems, recv_sems, load_sems = refs[3 * nw:]
        x, y, c, _ = _place()

        def load(w, kk):
            return pltpu.make_async_copy(ins[w].at[kk, 1 - c], stage[w].at[kk % 2], load_sems.at[w, kk])

        def send(w, kk):
            return pltpu.make_async_remote_copy(src_ref=stage[w].at[kk % 2], dst_ref=got[w].at[kk], send_sem=send_sems.at[w, kk],
                                                recv_sem=recv_sems.at[w, kk], device_id=(x, y, 1 - c), device_id_type=MESH)

        for kk in range(2):
            for w in range(nw):
                load(w, kk).start()
        for kk in range(NSH):
            for w in range(nw):
                load(w, kk).wait()
                send(w, kk).start()
            if kk + 2 < NSH:
                for w in range(nw):
                    send(w, kk).wait_send()
                    load(w, kk + 2).start()
        for kk in range(NSH - 2, NSH):
            for w in range(nw):
                send(w, kk).wait_send()
        for kk in range(NSH):
            for w in range(nw):
                send(w, kk).wait_recv()

    return pl.pallas_call(
        body, name=name, out_shape=[_sds((NSH,) + g.shape[2:], bf16) for g in grads], in_specs=[HBM] * nw, out_specs=[HBM] * nw,
        scratch_shapes=[pltpu.VMEM((2,) + g.shape[2:], bf16) for g in grads]
        + [pltpu.SemaphoreType.DMA((nw, NSH)), pltpu.SemaphoreType.DMA((nw, NSH)), pltpu.SemaphoreType.DMA((nw, NSH))],
        compiler_params=pltpu.CompilerParams(vmem_limit_bytes=VMEM_LIMIT),
    )(*grads)


def _rs_pair_gather(name, halves):
    nw = len(halves)

    def body(*refs):
        ins, outs, stage = refs[:nw], refs[nw:2 * nw], refs[2 * nw:3 * nw]
        send_sems, recv_sems, local_sems, stage_sems = refs[3 * nw:]
        x, y, c, _ = _place()
        loads = [pltpu.make_async_copy(ins[w], stage[w], stage_sems.at[w]) for w in range(nw)]
        for cp in loads:
            cp.start()
        local, cps = [], []
        for w in range(nw):
            loads[w].wait()
            local.append(pltpu.make_async_copy(stage[w], outs[w].at[c], local_sems.at[w]))
            cps.append(pltpu.make_async_remote_copy(src_ref=stage[w], dst_ref=outs[w].at[c], send_sem=send_sems.at[w],
                                                    recv_sem=recv_sems.at[w], device_id=(x, y, 1 - c), device_id_type=MESH))
            local[w].start()
            cps[w].start()
        for w in range(nw):
            pltpu.make_async_remote_copy(src_ref=stage[w], dst_ref=outs[w].at[1 - c], send_sem=send_sems.at[w], recv_sem=recv_sems.at[w],
                                         device_id=(x, y, 1 - c), device_id_type=MESH).wait_recv()
        for cp in cps:
            cp.wait_send()
        for cp in local:
            cp.wait()

    return pl.pallas_call(
        body, name=name, out_shape=[_sds((2,) + h.shape, f32) for h in halves], in_specs=[HBM] * nw, out_specs=[HBM] * nw,
        scratch_shapes=[pltpu.VMEM(h.shape, f32) for h in halves]
        + [pltpu.SemaphoreType.DMA((nw,)), pltpu.SemaphoreType.DMA((nw,)), pltpu.SemaphoreType.DMA((nw,)), pltpu.SemaphoreType.DMA((nw,))],
        compiler_params=pltpu.CompilerParams(vmem_limit_bytes=VMEM_LIMIT),
    )(*halves)


def _row_tile(r, c, nbuf):
    budget = 24 * 1024 * 1024 // (2 * nbuf * 4 * c)
    fits = [t for t in range(16, r + 1, 16) if r % t == 0 and t <= budget]
    return max(fits) if fits else r


def _cast_bf16(name, a, dep=None):
    r, c = a.shape
    tr = _row_tile(r, c, 2)
    dep_specs, dep_ops = _dep_args(dep, 1)

    def body(a_ref, *rest):
        rest[-1][...] = a_ref[...].astype(bf16)

    spec = pl.BlockSpec((tr, c), lambda i: (i, 0))
    return pl.pallas_call(body, name=name, grid=(r // tr,), in_specs=[spec] + dep_specs, out_specs=spec, out_shape=_sds((r, c), bf16),
                          compiler_params=_params(("parallel",)))(a, *dep_ops)


def _w_in_columns(win4):
    tr = 256

    def body(a_ref, o_ref, ob_ref):
        for k in range(NSH):
            o_ref[:, IN_SH * k:IN_SH * (k + 1)] = a_ref[k][:, :IN_SH]
        o_ref[:, IN_COLS:] = jnp.zeros((tr, IN_P - IN_COLS), bf16)
        ob_ref[...] = o_ref[:, IN_A:]

    return pl.pallas_call(
        body, name="w_in_columns", grid=(D // tr,), in_specs=[pl.BlockSpec((NSH, tr, IN_SHP), lambda i: (0, i, 0))],
        out_specs=[pl.BlockSpec((tr, IN_P), lambda i: (i, 0)), pl.BlockSpec((tr, IN_B), lambda i: (i, 0))],
        out_shape=[_sds((D, IN_P), bf16), _sds((D, IN_B), bf16)], compiler_params=_params(("parallel",)))(win4)


def _pair_sum(name, core, grads, got):
    _, _, rh, c = grads.shape
    tr = _row_tile(rh, c, 2)

    def body(c_ref, a_ref, b_ref, o_ref):
        o_ref[...] = (a_ref[...].astype(f32) + b_ref[...].astype(f32)).astype(bf16)

    spec = pl.BlockSpec((None, tr, c), lambda k, i, c_ref: (k, i, 0))
    return pl.pallas_call(
        body, name=name, out_shape=_sds((NSH, rh, c), bf16),
        grid_spec=pltpu.PrefetchScalarGridSpec(
            num_scalar_prefetch=1, grid=(NSH, rh // tr),
            in_specs=[pl.BlockSpec((None, None, tr, c), lambda k, i, c_ref: (k, c_ref[0], i, 0)), spec], out_specs=spec),
        compiler_params=_params(("parallel", "parallel")))(core, grads, got)


def _chip_sum(name, chip, sums, lands):
    _, rh, c = sums.shape
    tr = _row_tile(rh, c, 4)

    def body(k_ref, own_ref, l_ref, o_ref):
        own = own_ref[...].astype(f32)
        acc = None
        for j in range(NSH):
            term = jnp.where(k_ref[0] == j, own, l_ref[j].astype(f32))
            acc = term if acc is None else acc + term
        o_ref[...] = acc

    return pl.pallas_call(
        body, name=name, out_shape=_sds((rh, c), f32),
        grid_spec=pltpu.PrefetchScalarGridSpec(
            num_scalar_prefetch=1, grid=(rh // tr,),
            in_specs=[pl.BlockSpec((None, tr, c), lambda i, k_ref: (k_ref[0], i, 0)), pl.BlockSpec((NSH, tr, c), lambda i, k_ref: (0, i, 0))],
            out_specs=pl.BlockSpec((tr, c), lambda i, k_ref: (i, 0))),
        compiler_params=_params(("parallel",)))(chip, sums, lands)


def _mods_part(cond16, w_ada, b_part):
    n = w_ada.shape[1]
    tn = 512

    def body(c_ref, w_ref, b_ref, o_ref):
        cv = c_ref[...]
        o_ref[...] = _dot(cv * _sigmoid(cv), w_ref[...]) + b_ref[...]

    return pl.pallas_call(
        body, name="mods_part", grid=(n // tn,),
        in_specs=[pl.BlockSpec((16, D), lambda j: (0, 0)), pl.BlockSpec((D, tn), lambda j: (0, j)), pl.BlockSpec((1, tn), lambda j: (0, j))],
        out_specs=pl.BlockSpec((16, tn), lambda j: (0, j)), out_shape=_sds((16, n), f32), compiler_params=_params(("parallel",)),
    )(cond16, w_ada, b_part)


def _grad_w_ada(cond16, dm16):
    n = dm16.shape[1]
    tr = 256

    def body(c_ref, d_ref, o_ref):
        cv = c_ref[...]
        o_ref[...] = _dot(cv * _sigmoid(cv), d_ref[...], ta=True)

    return pl.pallas_call(
        body, name="grad_w_ada", grid=(D // tr,),
        in_specs=[pl.BlockSpec((16, tr), lambda i: (0, i)), pl.BlockSpec((16, n), lambda i: (0, 0))],
        out_specs=pl.BlockSpec((tr, n), lambda i: (i, 0)), out_shape=_sds((D, n), f32), compiler_params=_params(("parallel",)),
    )(cond16, dm16)


def _adamw(name, w, g, m, v):
    r, c = w.shape
    tr = _row_tile(r, c, 7)
    spec = pl.BlockSpec((tr, c), lambda i: (i, 0))
    grid = (r // tr,)

    def body(w_ref, g_ref, m_ref, v_ref, d_ref, nm_ref, nv_ref):
        gv = g_ref[...]
        nm = ADAM_B1 * m_ref[...] + (1.0 - ADAM_B1) * gv
        nv = ADAM_B2 * v_ref[...] + (1.0 - ADAM_B2) * (gv * gv)
        nm_ref[...] = nm
        nv_ref[...] = nv
        m_hat = nm / (1.0 - ADAM_B1 ** ADAM_STEP)
        v_hat = nv / (1.0 - ADAM_B2 ** ADAM_STEP)
        d_ref[...] = -ADAM_LR * (m_hat / (jnp.sqrt(v_hat) + ADAM_EPS) + ADAM_WD * w_ref[...])

    return pl.pallas_call(body, name=name, grid=grid, in_specs=[spec] * 4, out_specs=[spec] * 3, out_shape=[_sds(w.shape, f32)] * 3,
                          compiler_params=_params(("parallel",)))(w, g, m, v)


def _pack(parts, rows):
    flat = []
    for p in parts:
        p = p.reshape(-1)
        flat.append(jnp.pad(p, (0, (-p.shape[0]) % 128)))
    v = jnp.concatenate(flat)
    return jnp.pad(v, (0, rows * 128 - v.shape[0])).reshape(rows, 128)


def _unpack(packed, sizes):
    lead = packed.shape[:-2]
    flat = packed.reshape(lead + (-1,))
    out, off = [], 0
    for n in sizes:
        out.append(flat[..., off:off + n])
        off += n + (-n) % 128
    return out


BIG = ("w_in", "w_out", "w_gate", "w_up", "w_down")
SMALL = ("b_ada", "g_mix", "conv_b", "dt_bias", "a_log", "d_skip", "g_att_out", "g_ssd_out", "g_ffn", "g_final", "rel_bias", "conv_w")
ORDER = ("w_ada", "b_ada", "g_mix", "w_in", "rel_bias", "conv_w", "conv_b", "dt_bias", "a_log", "d_skip", "g_att_out", "g_ssd_out",
         "w_out", "g_ffn", "w_gate", "w_up", "w_down", "g_final")
REL_SH = N_REL // NSH
CONVW_SH = XBC // NSH
ADA_SH = 6 * D // NSH


class _Exchange:
    def __init__(self, core, chip):
        self.core, self.chip = core, chip
        self.gathered = {}
        self.pending = []

    def gather(self, names, shards):
        ssem, rsem, thru, lands, token = _split_start("gather_start_" + "_".join(names), _gather_copies, shards,
                                                      [(NSH,) + s.shape for s in shards])
        self.gathered.update({n: (ssem[i], rsem[i], thru[i], lands[i]) for i, n in enumerate(names)})
        return token

    def _whole(self, names, after):
        ssem, rsem, thru, lands = zip(*[self.gathered[n] for n in names])
        tag = "_".join(names)
        thru, lands = _split_wait("gather_wait_" + tag, _gather_copies, ssem, rsem, thru, lands, after)
        return _gather_forward("gather_forward_" + tag, thru, lands)

    def w_in(self, after):
        (win4,) = self._whole(("w_in",), after)
        return _w_in_columns(win4.reshape(NSH, D, IN_SHP))

    def w_out(self, after):
        wout4, wg4, wu4, wd4 = self._whole(("w_out", "w_gate", "w_up", "w_down"), after)
        self.ffn_weights = wg4.reshape(NSH, D, FSH), wu4.reshape(NSH, D, FSH), wd4.reshape(NSH, FSH, D)
        return wout4.reshape(D, D)

    def ffn(self, after):
        return self.ffn_weights

    def grad(self, names, grads):
        tag = "_".join(names)
        stacked = [g.reshape(NSH, 2, g.shape[1] // 2, g.shape[2]) for g in grads]
        got = _rs_pair_exchange("rs_pair_exchange_" + tag, stacked)
        sums = [_pair_sum("pair_sum_" + n, self.core, o, g) for n, o, g in zip(names, stacked, got)]
        self.pending.append((names, _split_start("rs_start_" + tag, _reduce_copies, sums, [s.shape for s in sums])))
        return self.pending[-1][1][4]

    def finish(self, after):
        grads = {}
        for names, (ssem, rsem, sums, lands, _) in self.pending:
            tag = "_".join(names)
            sums, lands = _split_wait("rs_wait_" + tag, _reduce_copies, ssem, rsem, sums, lands, after)
            halves = [_chip_sum("chip_sum_" + n, self.chip, sm, ld) for n, sm, ld in zip(names, sums, lands)]
            for n, f in zip(names, _rs_pair_gather("rs_pair_gather_" + tag, halves)):
                grads[n] = f.reshape(2 * f.shape[1], f.shape[2])
        return grads


def kernel(x, c, w_ada, b_ada, g_mix, w_in, rel_bias, conv_w, conv_b, dt_bias, a_log, d_skip, g_att_out, g_ssd_out, w_out, g_ffn, w_gate, w_up, w_down, g_final, loss_target, m_w_ada, m_b_ada, m_g_mix, m_w_in, m_rel_bias, m_conv_w, m_conv_b, m_dt_bias, m_a_log, m_d_skip, m_g_att_out, m_g_ssd_out, m_w_out, m_g_ffn, m_w_gate, m_w_up, m_w_down, m_g_final, v_w_ada, v_b_ada, v_g_mix, v_w_in, v_rel_bias, v_conv_w, v_conv_b, v_dt_bias, v_a_log, v_d_skip, v_g_att_out, v_g_ssd_out, v_w_out, v_g_ffn, v_w_gate, v_w_up, v_w_down, v_g_final):
    args = dict(locals())
    w = {n: args[n] for n in ORDER}
    m = {n: args["m_" + n] for n in ORDER}
    v = {n: args["v_" + n] for n in ORDER}
    ix, iy, ic = lax.axis_index("x"), lax.axis_index("y"), lax.axis_index("c")
    chip = 2 * ix + iy
    dev = 2 * chip + ic
    s = x.shape[1]

    g1 = _allgather8("gather_inputs", _pack([c[0], rel_bias[0], conv_w[0]], 40))
    c_all, rel_sh, convw_sh = _unpack(g1, [D, NH * REL_SH, 4 * CONVW_SH])
    rel_full = jnp.concatenate([rel_sh[2 * k].reshape(NH, REL_SH) for k in range(NSH)], axis=1)
    convw_full = jnp.concatenate([convw_sh[2 * k].reshape(4, CONVW_SH) for k in range(NSH)], axis=1)
    cond16 = jnp.pad(c_all, ((0, 8), (0, 0)))
    b_part = lax.dynamic_slice_in_dim(b_ada, chip * ADA_SH, ADA_SH, axis=1)
    mods_part = _mods_part(cond16, w_ada[0], b_part)[:N_DEV]
    g2 = _allgather8("gather_mods", mods_part.reshape(N_DEV * ADA_SH // 128, 128))
    mods_all = jnp.concatenate([g2[2 * k].reshape(N_DEV, ADA_SH) for k in range(NSH)], axis=1)
    mods = lax.dynamic_slice_in_dim(mods_all, dev, 1, axis=0)

    exchange = _Exchange(jnp.reshape(ic, (1,)).astype(jnp.int32), jnp.reshape(chip, (1,)).astype(jnp.int32))
    shard_in = _cast_bf16("cast_w_in", jnp.pad(w_in[0], ((0, 0), (0, IN_SHP - IN_SH))), dep=g2[0, :8]).reshape(2, D // 2, IN_SHP)
    tok = exchange.gather(("w_in",), [shard_in])
    tok = exchange.gather(("w_out", "w_gate", "w_up", "w_down"), [
        _cast_bf16("cast_w_out", w_out[0], dep=tok).reshape(2, D // NSH // 2, D),
        _cast_bf16("cast_w_gate", w_gate[0], dep=tok).reshape(2, D // 2, FSH),
        _cast_bf16("cast_w_up", w_up[0], dep=tok).reshape(2, D // 2, FSH),
        _cast_bf16("cast_w_down", w_down[0], dep=tok).reshape(2, FSH // 2, D)])
    mods = mods + tok[:1, :1]

    loss, grad_x, dmods, small = _local_step(
        x[0], loss_target[0], mods, g_mix, rel_full, convw_full, conv_b, dt_bias, a_log, d_skip, g_att_out, g_ssd_out, g_ffn,
        g_final[None, :], exchange)

    small_names = ("g_mix", "conv_b", "dt_bias", "a_log", "d_skip", "g_att_out", "g_ssd_out", "g_ffn", "g_final", "rel_bias", "conv_w")
    g3 = _allgather8("gather_small_grads", _pack([dmods] + [small[n] for n in small_names], 264))
    sizes = [6 * D] + [int(np.prod(small[n].shape)) for n in small_names]
    dmods_all = _unpack(g3, sizes)[0]
    summed = _unpack(_sum8(g3), sizes)
    grads = {"b_ada": summed[0].reshape(1, 6 * D)}
    for n, val in zip(small_names, summed[1:]):
        grads[n] = val.reshape(small[n].shape)
    grads["rel_bias"] = lax.dynamic_slice_in_dim(grads["rel_bias"], chip * REL_SH, REL_SH, axis=1)
    grads["conv_w"] = lax.dynamic_slice_in_dim(grads["conv_w"], chip * CONVW_SH, CONVW_SH, axis=1)
    grads["g_final"] = grads["g_final"].reshape(D)
    dm16 = jnp.pad(lax.dynamic_slice_in_dim(dmods_all, chip * ADA_SH, ADA_SH, axis=1), ((0, 8), (0, 0)))
    grads["w_ada"] = _grad_w_ada(cond16, dm16)

    delta, new_m, new_v = {}, {}, {}
    delta["w_ada"], new_m["w_ada"], new_v["w_ada"] = _adamw("adamw_w_ada", w_ada[0], grads["w_ada"], m_w_ada[0], v_w_ada[0])
    grads.update(exchange.finish(grad_x))
    grads["w_in"] = grads["w_in"][:, :IN_SH]
    for n in BIG:
        delta[n], new_m[n], new_v[n] = _adamw("adamw_" + n, w[n][0], grads[n], m[n][0], v[n][0])
    sw = _pack([w[n] for n in SMALL], 200)
    sg = _pack([grads[n] for n in SMALL], 200)
    sm = _pack([m[n] for n in SMALL], 200)
    sv = _pack([v[n] for n in SMALL], 200)
    ssz = [int(np.prod(w[n].shape)) for n in SMALL]
    for dst, packed in zip((delta, new_m, new_v), _adamw("adamw_small", sw, sg, sm, sv)):
        for n, val in zip(SMALL, _unpack(packed, ssz)):
            dst[n] = val

    def shaped(d, n):
        return d[n].reshape(w[n].shape)

    total = lax.psum(loss, ("x", "y", "c"))
    return (total, grad_x[None], *[shaped(grads, n) for n in ORDER], *[shaped(delta, n) for n in ORDER],
            *[shaped(new_m, n) for n in ORDER], *[shaped(new_v, n) for n in ORDER])
```

```python
import functools

import numpy as np
import jax
import jax.numpy as jnp
from jax import lax
from jax.experimental import pallas as pl
from jax.experimental.pallas import tpu as pltpu

f32 = jnp.float32
bf16 = jnp.bfloat16
HIGHEST = lax.Precision.HIGHEST
MESH = pl.DeviceIdType.MESH

D = 2048
CHUNK = 64
LEFT = 8
BAND = (LEFT + 1) * CHUNK
BANDP = 640
PADK = LEFT * CHUNK
NH = 16
HD = 64
ATT_W = NH * HD
SSD_W = 1024
NG = 2
NSTATE = 128
GW = SSD_W // NG
XBC = SSD_W + 2 * NG * NSTATE
N_REL = 320
REL_CLIP = 256
FFN = 5632
NSH = 4
FSH = FFN // NSH
IN_COLS = 5648
IN_SH = IN_COLS // NSH
IN_SHP = 1536
IN_A = 3 * ATT_W
IN_B = 2688
IN_P = IN_A + IN_B
EPS = 1e-6
N_DEV = 8

ADAM_LR = 0.001
ADAM_B1 = 0.9
ADAM_B2 = 0.999
ADAM_EPS = 1e-08
ADAM_WD = 0.01
ADAM_STEP = 10

VMEM_LIMIT = 56 * 1024 * 1024


def _params(sem):
    return pltpu.CompilerParams(dimension_semantics=sem, vmem_limit_bytes=VMEM_LIMIT)


def _sds(shape, dtype):
    return jax.ShapeDtypeStruct(shape, dtype)


def _fold8(v):
    r, w = v.shape
    return jnp.sum(v.reshape(r // 8, 8, w), axis=0)


STRIP = 16


def _strips(tm, fn):
    def step(j, carry):
        fn(pl.ds(pl.multiple_of(j * STRIP, STRIP), STRIP))
        return carry
    lax.fori_loop(0, tm // STRIP, step, 0, unroll=4)


def _sigmoid(v):
    return 1.0 / (1.0 + jnp.exp(-v))


def _softplus(v):
    return jnp.maximum(v, 0.0) + jnp.log(1.0 + jnp.exp(-jnp.abs(v)))


def _dot(a, b, ta=False, tb=False):
    dn = (((0 if ta else 1,), (1 if tb else 0,)), ((), ()))
    return lax.dot_general(a.astype(bf16), b.astype(bf16), dn, preferred_element_type=f32)


def _dep_args(dep, ngrid):
    if dep is None:
        return [], []
    return [pl.BlockSpec((8, 128), lambda *_: (0, 0))], [dep]


def _dot01(a, b, ta=False, tb=False, exact="b"):
    dn = (((0 if ta else 1,), (1 if tb else 0,)), ((), ()))
    x = a if exact == "b" else b
    hi = x.astype(bf16)
    r = x - hi.astype(f32)
    mid = r.astype(bf16)
    lo = (r - mid.astype(f32)).astype(bf16)
    if exact == "b":
        m = b.astype(bf16)
        return sum(lax.dot_general(p, m, dn, preferred_element_type=f32) for p in (hi, mid, lo))
    m = a.astype(bf16)
    return sum(lax.dot_general(m, p, dn, preferred_element_type=f32) for p in (hi, mid, lo))


def _matmul(name, a, b, *, grid, a_spec, b_spec, o_spec, o_shape, o_dtype, acc_shape, ta=False, tb=False, dep=None):
    nk = grid[2]
    dep_specs, dep_ops = _dep_args(dep, 3)

    def body(a_ref, b_ref, *rest):
        o_ref, acc_ref = rest[-2:]
        p = _dot(a_ref[...], b_ref[...], ta, tb)
        if nk == 1:
            o_ref[...] = p.astype(o_ref.dtype)
        else:
            k = pl.program_id(2)

            @pl.when(k == 0)
            def _():
                acc_ref[...] = p

            @pl.when(jnp.logical_and(k > 0, k < nk - 1))
            def _():
                acc_ref[...] += p

            @pl.when(k == nk - 1)
            def _():
                o_ref[...] = (acc_ref[...] + p).astype(o_ref.dtype)

    return pl.pallas_call(
        body, name=name, grid=grid, in_specs=[a_spec, b_spec] + dep_specs, out_specs=o_spec,
        out_shape=_sds(o_shape, o_dtype), scratch_shapes=[pltpu.VMEM(acc_shape if nk > 1 else (8, 128), f32)],
        compiler_params=_params(("parallel", "parallel", "arbitrary")),
    )(a, b, *dep_ops)


def _mm_nn_fullk(name, a, b, tm, tn, o_dtype, n=None):
    m, k = a.shape
    n = b.shape[1] if n is None else n
    return _matmul(name, a, b, grid=(m // tm, n // tn, 1),
                   a_spec=pl.BlockSpec((tm, k), lambda i, j, kk: (i, 0)),
                   b_spec=pl.BlockSpec((k, tn), lambda i, j, kk: (0, j)),
                   o_spec=pl.BlockSpec((tm, tn), lambda i, j, kk: (i, j)),
                   o_shape=(m, n), o_dtype=o_dtype, acc_shape=(tm, tn))


def _mm_nt(name, a, b, tm, tn, tk, o_dtype, dep=None):
    m, k = a.shape
    n = b.shape[0]
    return _matmul(name, a, b, grid=(m // tm, n // tn, k // tk), tb=True, dep=dep,
                   a_spec=pl.BlockSpec((tm, tk), lambda i, j, kk: (i, kk)),
                   b_spec=pl.BlockSpec((tn, tk), lambda i, j, kk: (j, kk)),
                   o_spec=pl.BlockSpec((tm, tn), lambda i, j, kk: (i, j)),
                   o_shape=(m, n), o_dtype=o_dtype, acc_shape=(tm, tn))


def _mm_tn(name, a, b, tm, tn, tk, o_dtype):
    k, m = a.shape
    n = b.shape[1]
    return _matmul(name, a, b, grid=(m // tm, n // tn, k // tk), ta=True,
                   a_spec=pl.BlockSpec((tk, tm), lambda i, j, kk: (kk, i)),
                   b_spec=pl.BlockSpec((tk, tn), lambda i, j, kk: (kk, j)),
                   o_spec=pl.BlockSpec((tm, tn), lambda i, j, kk: (i, j)),
                   o_shape=(m, n), o_dtype=o_dtype, acc_shape=(tm, tn))


FSH_PARTS = (slice(0, 640), slice(640, FSH))


def _ffn_up(h2b, wg4, wu4, tm):
    s = h2b.shape[0]

    def body(h_ref, wg_ref, wu_ref, a_ref, s_ref, ud_ref):
        h = h_ref[...]
        for cols in FSH_PARTS:
            g = _dot(h, wg_ref[:, cols])
            u = _dot(h, wu_ref[:, cols])
            sg = _sigmoid(g)
            sil = g * sg
            a_ref[:, cols] = (sil * u).astype(bf16)
            s_ref[:, cols] = sil.astype(bf16)
            ud_ref[:, cols] = (u * (sg * (1.0 + g * (1.0 - sg)))).astype(bf16)

    wspec = pl.BlockSpec((None, D, FSH), lambda k, i: (k, 0, 0))
    ospec = pl.BlockSpec((tm, FSH), lambda k, i: (i, k))
    return pl.pallas_call(
        body, name="ffn_up", grid=(NSH, s // tm),
        in_specs=[pl.BlockSpec((tm, D), lambda k, i: (i, 0)), wspec, wspec],
        out_specs=[ospec, ospec, ospec], out_shape=[_sds((s, FFN), bf16)] * 3,
        compiler_params=_params(("parallel", "parallel")),
    )(h2b, wg4, wu4)


def _ffn_down(act, wd4, tm):
    s = act.shape[0]
    tn = D // 2

    def body(a_ref, b_ref, o_ref):
        o_ref[...] = jnp.dot(a_ref[...], b_ref[...].reshape(FFN, tn), preferred_element_type=f32)

    return pl.pallas_call(
        body, name="ffn_down", grid=(s // tm, D // tn),
        in_specs=[pl.BlockSpec((tm, FFN), lambda i, j: (i, 0)), pl.BlockSpec((NSH, FSH, tn), lambda i, j: (0, 0, j))],
        out_specs=pl.BlockSpec((tm, tn), lambda i, j: (i, j)), out_shape=_sds((s, D), f32),
        compiler_params=_params(("parallel", "parallel")),
    )(act, wd4)


def _ffn_dact(dffn, wd4, sil, ud, tm, dep=None):
    s = dffn.shape[0]
    dep_specs, dep_ops = _dep_args(dep, 2)

    def body(d_ref, w_ref, s_ref, ud_ref, *rest):
        dg_ref, du_ref = rest[-2:]
        d = d_ref[...]
        for cols in FSH_PARTS:
            dact = _dot(d, w_ref[cols, :], tb=True)
            dg_ref[:, cols] = (dact * ud_ref[:, cols].astype(f32)).astype(bf16)
            du_ref[:, cols] = (dact * s_ref[:, cols].astype(f32)).astype(bf16)

    blk = pl.BlockSpec((tm, FSH), lambda k, i: (i, k))
    return pl.pallas_call(
        body, name="ffn_dact", grid=(NSH, s // tm),
        in_specs=[pl.BlockSpec((tm, D), lambda k, i: (i, 0)), pl.BlockSpec((None, FSH, D), lambda k, i: (k, 0, 0)), blk, blk] + dep_specs,
        out_specs=[blk, blk], out_shape=[_sds((s, FFN), bf16), _sds((s, FFN), bf16)],
        compiler_params=_params(("parallel", "parallel")),
    )(dffn, wd4, sil, ud, *dep_ops)


def _ffn_dh(dgate, dup, wg4, wu4, tm, dep=None):
    s = dgate.shape[0]
    dep_specs, dep_ops = _dep_args(dep, 2)

    def body(dg_ref, du_ref, wg_ref, wu_ref, *rest):
        o_ref, acc_ref = rest[-2:]
        k = pl.program_id(1)
        p = _dot(dg_ref[...], wg_ref[...], tb=True) + _dot(du_ref[...], wu_ref[...], tb=True)

        @pl.when(k == 0)
        def _():
            acc_ref[...] = p

        @pl.when(jnp.logical_and(k > 0, k < NSH - 1))
        def _():
            acc_ref[...] += p

        @pl.when(k == NSH - 1)
        def _():
            o_ref[...] = acc_ref[...] + p

    aspec = pl.BlockSpec((tm, FSH), lambda i, k: (i, k))
    wspec = pl.BlockSpec((None, D, FSH), lambda i, k: (k, 0, 0))
    return pl.pallas_call(
        body, name="ffn_dh", grid=(s // tm, NSH), in_specs=[aspec, aspec, wspec, wspec] + dep_specs,
        out_specs=pl.BlockSpec((tm, D), lambda i, k: (i, 0)), out_shape=_sds((s, D), f32),
        scratch_shapes=[pltpu.VMEM((tm, D), f32)], compiler_params=_params(("parallel", "arbitrary")),
    )(dgate, dup, wg4, wu4, *dep_ops)


def _grad_cols4(name, h, dy, tm, tk):
    s = h.shape[0]
    return _matmul(name, h, dy, grid=(NSH, D // tm, s // tk), ta=True,
                   a_spec=pl.BlockSpec((tk, tm), lambda k, i, kk: (kk, i)),
                   b_spec=pl.BlockSpec((tk, FSH), lambda k, i, kk: (kk, k)),
                   o_spec=pl.BlockSpec((None, tm, FSH), lambda k, i, kk: (k, i, 0)),
                   o_shape=(NSH, D, FSH), o_dtype=bf16, acc_shape=(tm, FSH))


def _grad_wdown4(act, dffn, tn, tk):
    s = act.shape[0]
    return _matmul("grad_w_down", act, dffn, grid=(NSH, D // tn, s // tk), ta=True,
                   a_spec=pl.BlockSpec((tk, FSH), lambda k, j, kk: (kk, k)),
                   b_spec=pl.BlockSpec((tk, tn), lambda k, j, kk: (kk, j)),
                   o_spec=pl.BlockSpec((None, FSH, tn), lambda k, j, kk: (k, 0, j)),
                   o_shape=(NSH, FSH, D), o_dtype=bf16, acc_shape=(FSH, tn))


def _row_spec(w):
    return pl.BlockSpec((1, w), lambda i: (0, 0))


def _tile_spec(tm, w, col=0):
    return pl.BlockSpec((tm, w), lambda i: (i, col))


def _norm_mod(name, x, g, sc, sh, tm):
    s = x.shape[0]

    def body(x_ref, g_ref, sc_ref, sh_ref, o_ref):
        def strip(rows):
            xv = x_ref[rows, :]
            r = lax.rsqrt(jnp.mean(xv * xv, axis=-1, keepdims=True) + EPS)
            o_ref[rows, :] = (xv * r * g_ref[...] * (1.0 + sc_ref[...]) + sh_ref[...]).astype(bf16)

        _strips(tm, strip)

    return pl.pallas_call(
        body, name=name, grid=(s // tm,), in_specs=[_tile_spec(tm, D), _row_spec(D), _row_spec(D), _row_spec(D)],
        out_specs=_tile_spec(tm, D), out_shape=_sds((s, D), bf16), compiler_params=_params(("parallel",)),
    )(x, g, sc, sh)


def _resid_norm_mod(x, gt, mix, g, sc, sh, tm):
    s = x.shape[0]

    def body(x_ref, gt_ref, m_ref, g_ref, sc_ref, sh_ref, x2_ref, h_ref):
        def strip(rows):
            xv = x_ref[rows, :] + gt_ref[...] * m_ref[rows, :]
            x2_ref[rows, :] = xv
            r = lax.rsqrt(jnp.mean(xv * xv, axis=-1, keepdims=True) + EPS)
            h_ref[rows, :] = (xv * r * g_ref[...] * (1.0 + sc_ref[...]) + sh_ref[...]).astype(bf16)

        _strips(tm, strip)

    return pl.pallas_call(
        body, name="resid_norm_mod", grid=(s // tm,),
        in_specs=[_tile_spec(tm, D), _row_spec(D), _tile_spec(tm, D), _row_spec(D), _row_spec(D), _row_spec(D)],
        out_specs=[_tile_spec(tm, D), _tile_spec(tm, D)], out_shape=[_sds((s, D), f32), _sds((s, D), bf16)],
        compiler_params=_params(("parallel",)),
    )(x, gt, mix, g, sc, sh)


def _final_fwd_bwd(x2, ffn, gt2, g, tgt, tm):
    s = x2.shape[0]
    n = s // tm

    def body(x_ref, f_ref, gt_ref, g_ref, t_ref, dx_ref, df_ref, loss_ref, dg_ref, dgt_ref, a_loss, a_dg, a_dgt):
        i = pl.program_id(0)

        @pl.when(i == 0)
        def _():
            a_loss[...] = jnp.zeros_like(a_loss)
            a_dg[...] = jnp.zeros_like(a_dg)
            a_dgt[...] = jnp.zeros_like(a_dgt)

        def strip(rows):
            fv = f_ref[rows, :]
            gt = gt_ref[...]
            gv = g_ref[...]
            xv = x_ref[rows, :] + gt * fv
            r = lax.rsqrt(jnp.mean(xv * xv, axis=-1, keepdims=True) + EPS)
            xh = xv * r
            e = xh * gv - t_ref[rows, :]
            a_loss[...] += _fold8(e * e)
            dy = e * (1.0 / D)
            a_dg[...] += _fold8(dy * xh)
            t = dy * gv
            dx = r * (t - xh * jnp.mean(t * xh, axis=-1, keepdims=True))
            dx_ref[rows, :] = dx
            a_dgt[...] += _fold8(dx * fv)
            df_ref[rows, :] = (dx * gt).astype(bf16)

        _strips(tm, strip)

        @pl.when(i == n - 1)
        def _():
            tot = jnp.sum(jnp.sum(a_loss[...], axis=0, keepdims=True), axis=1, keepdims=True) * (0.5 / D)
            loss_ref[...] = jnp.broadcast_to(tot, (1, 128))
            dg_ref[...] = jnp.sum(a_dg[...], axis=0, keepdims=True)
            dgt_ref[...] = jnp.sum(a_dgt[...], axis=0, keepdims=True)

    return pl.pallas_call(
        body, name="final_fwd_bwd", grid=(n,),
        in_specs=[_tile_spec(tm, D), _tile_spec(tm, D), _row_spec(D), _row_spec(D), _tile_spec(tm, D)],
        out_specs=[_tile_spec(tm, D), _tile_spec(tm, D), _row_spec(128), _row_spec(D), _row_spec(D)],
        out_shape=[_sds((s, D), f32), _sds((s, D), bf16), _sds((1, 128), f32), _sds((1, D), f32), _sds((1, D), f32)],
        scratch_shapes=[pltpu.VMEM((8, D), f32)] * 3, compiler_params=_params(("arbitrary",)),
    )(x2, ffn, gt2, g, tgt)


def _norm_mod_bwd(name, dh, xin, g, sc, dres, tm, mix=None, gt=None):
    s = dh.shape[0]
    n = s // tm
    with_mix = mix is not None

    def body(*refs):
        if with_mix:
            dh_ref, x_ref, g_ref, sc_ref, dr_ref, m_ref, gt_ref, dx_ref, dm_ref, dsc_ref, dsh_ref, dg_ref, dgt_ref, a_sc, a_sh, a_g, a_gt = refs
        else:
            dh_ref, x_ref, g_ref, sc_ref, dr_ref, dx_ref, dsc_ref, dsh_ref, dg_ref, a_sc, a_sh, a_g = refs
        i = pl.program_id(0)

        @pl.when(i == 0)
        def _():
            a_sc[...] = jnp.zeros_like(a_sc)
            a_sh[...] = jnp.zeros_like(a_sh)
            a_g[...] = jnp.zeros_like(a_g)
            if with_mix:
                a_gt[...] = jnp.zeros_like(a_gt)

        def strip(rows):
            dh = dh_ref[rows, :]
            xv = x_ref[rows, :]
            gv = g_ref[...]
            r = lax.rsqrt(jnp.mean(xv * xv, axis=-1, keepdims=True) + EPS)
            xh = xv * r
            a_sc[...] += _fold8(dh * xh * gv)
            a_sh[...] += _fold8(dh)
            dn = dh * (1.0 + sc_ref[...])
            a_g[...] += _fold8(dn * xh)
            t = dn * gv
            dx = dr_ref[rows, :] + r * (t - xh * jnp.mean(t * xh, axis=-1, keepdims=True))
            dx_ref[rows, :] = dx
            if with_mix:
                a_gt[...] += _fold8(dx * m_ref[rows, :])
                dm_ref[rows, :] = (dx * gt_ref[...]).astype(bf16)

        _strips(tm, strip)

        @pl.when(i == n - 1)
        def _():
            dsc_ref[...] = jnp.sum(a_sc[...], axis=0, keepdims=True)
            dsh_ref[...] = jnp.sum(a_sh[...], axis=0, keepdims=True)
            dg_ref[...] = jnp.sum(a_g[...], axis=0, keepdims=True)
            if with_mix:
                dgt_ref[...] = jnp.sum(a_gt[...], axis=0, keepdims=True)

    tile, row = _tile_spec(tm, D), _row_spec(D)
    if with_mix:
        ins, args = [tile, tile, row, row, tile, tile, row], (dh, xin, g, sc, dres, mix, gt)
        outs = [tile, tile, row, row, row, row]
        shapes = [_sds((s, D), f32), _sds((s, D), bf16)] + [_sds((1, D), f32)] * 4
        nacc = 4
    else:
        ins, args = [tile, tile, row, row, tile], (dh, xin, g, sc, dres)
        outs = [tile, row, row, row]
        shapes = [_sds((s, D), f32)] + [_sds((1, D), f32)] * 3
        nacc = 3
    return pl.pallas_call(
        body, name=name, grid=(n,), in_specs=ins, out_specs=outs, out_shape=shapes,
        scratch_shapes=[pltpu.VMEM((8, D), f32)] * nacc, compiler_params=_params(("arbitrary",)),
    )(*args)


def _mix_pre(att, y, proj2, g_att, g_ssd, tm):
    s = att.shape[0]

    def body(a_ref, y_ref, z_ref, ga_ref, gs_ref, o_ref):
        def strip(rows):
            a = a_ref[rows, :]
            ra = lax.rsqrt(jnp.mean(a * a, axis=-1, keepdims=True) + EPS)
            o_ref[rows, 0:ATT_W] = (a * ra * ga_ref[...]).astype(bf16)
            z = z_ref[rows, :]
            u = y_ref[rows, :] * (z * _sigmoid(z))
            ru = lax.rsqrt(jnp.mean(u * u, axis=-1, keepdims=True) + EPS)
            o_ref[rows, ATT_W:] = (u * ru * gs_ref[...]).astype(bf16)

        _strips(tm, strip)

    t = _tile_spec(tm, ATT_W)
    return pl.pallas_call(
        body, name="mix_pre", grid=(s // tm,), in_specs=[t, t, t, _row_spec(ATT_W), _row_spec(SSD_W)],
        out_specs=_tile_spec(tm, D), out_shape=_sds((s, D), bf16), compiler_params=_params(("parallel",)),
    )(att, y, proj2, g_att, g_ssd)


def _mix_pre_bwd(dmc, att, y, proj2, g_att, g_ssd, tm):
    s = att.shape[0]
    n = s // tm

    def body(da_ref, ds_ref, a_ref, y_ref, z_ref, ga_ref, gs_ref, datt_ref, dy_ref, dz_ref, dga_ref, dgs_ref, acc_a, acc_s):
        i = pl.program_id(0)

        @pl.when(i == 0)
        def _():
            acc_a[...] = jnp.zeros_like(acc_a)
            acc_s[...] = jnp.zeros_like(acc_s)

        def strip(rows):
            a = a_ref[rows, :]
            ra = lax.rsqrt(jnp.mean(a * a, axis=-1, keepdims=True) + EPS)
            ah = a * ra
            dan = da_ref[rows, :]
            acc_a[...] += _fold8(dan * ah)
            t = dan * ga_ref[...]
            datt_ref[rows, :] = (ra * (t - ah * jnp.mean(t * ah, axis=-1, keepdims=True))).astype(bf16)
            z = z_ref[rows, :]
            yv = y_ref[rows, :]
            sz = _sigmoid(z)
            sil = z * sz
            u = yv * sil
            ru = lax.rsqrt(jnp.mean(u * u, axis=-1, keepdims=True) + EPS)
            uh = u * ru
            dsn = ds_ref[rows, :]
            acc_s[...] += _fold8(dsn * uh)
            t2 = dsn * gs_ref[...]
            du = ru * (t2 - uh * jnp.mean(t2 * uh, axis=-1, keepdims=True))
            dy_ref[rows, :] = du * sil
            dz_ref[rows, :] = (du * yv * (sz * (1.0 + z * (1.0 - sz)))).astype(bf16)

        _strips(tm, strip)

        @pl.when(i == n - 1)
        def _():
            dga_ref[...] = jnp.sum(acc_a[...], axis=0, keepdims=True)
            dgs_ref[...] = jnp.sum(acc_s[...], axis=0, keepdims=True)

    t = _tile_spec(tm, ATT_W)
    row = _row_spec(ATT_W)
    return pl.pallas_call(
        body, name="mix_pre_bwd", grid=(n,),
        in_specs=[_tile_spec(tm, ATT_W, 0), _tile_spec(tm, ATT_W, 1), t, t, t, row, row],
        out_specs=[t, t, t, row, row],
        out_shape=[_sds((s, ATT_W), bf16), _sds((s, SSD_W), f32), _sds((s, SSD_W), bf16), _sds((1, ATT_W), f32), _sds((1, SSD_W), f32)],
        scratch_shapes=[pltpu.VMEM((8, ATT_W), f32)] * 2, compiler_params=_params(("arbitrary",)),
    )(dmc, dmc, att, y, proj2, g_att, g_ssd)


ATT_GROUP = 8
ATT_GROUP_FWD = 16


def _pair_rows(qc):
    two = jnp.concatenate([qc, qc], axis=0)
    r = lax.broadcasted_iota(jnp.int32, (2 * CHUNK, 128), 0)
    l = lax.broadcasted_iota(jnp.int32, (2 * CHUNK, 128), 1)
    return jnp.where((r < CHUNK) == (l < HD), two, jnp.zeros_like(two))


def _scaled(q):
    return q * jnp.asarray(HD ** -0.5, q.dtype)


def _pair_scores(wt, kb, bias, r0, masked):
    sc = lax.dot_general(wt, kb, (((1,), (1,)), ((), ())), preferred_element_type=f32) + bias
    if not masked:
        return sc
    kidx = lax.broadcasted_iota(jnp.int32, sc.shape, 1)
    return jnp.where(r0 + kidx >= PADK, sc, -jnp.inf)


def _softmax(sc, axis):
    e = jnp.exp(sc - jnp.max(sc, axis=axis, keepdims=True))
    return e * (1.0 / jnp.sum(e, axis=axis, keepdims=True))


def _chunk_loops(nc, group, per_trip):
    n_masked = min(-(-LEFT // per_trip), nc // per_trip)

    def run(masked):
        def step(g, carry):
            group(g, masked)
            return carry
        return step

    lax.fori_loop(0, n_masked, run(True), 0)
    lax.fori_loop(n_masked, nc // per_trip, run(False), 0)


def _pair_diag(r):
    lane = lax.broadcasted_iota(jnp.int32, (CHUNK, 128), 1)
    return jnp.where(lane < HD, r[0:CHUNK], r[CHUNK:])


def _pad_keys(k_ref, kp, s):
    kp[0:PADK, :] = jnp.zeros((PADK, 128), bf16)
    kp[PADK:PADK + s, :] = k_ref[...]
    kp[PADK + s:, :] = jnp.zeros((CHUNK, 128), bf16)


def _attn_fwd(qkv, bias2):
    s = qkv.shape[0]
    nc = s // CHUNK
    npair = NH // 2
    per_trip = min(ATT_GROUP_FWD, nc)

    def body(q_ref, k_ref, v_ref, b_ref, o_ref, kp, vp):
        _pad_keys(k_ref, kp, s)
        _pad_keys(v_ref, vp, s)

        def group(g, masked):
            r0s = [pl.multiple_of((g * per_trip + u) * CHUNK, CHUNK) for u in range(per_trip)]
            scs = [_pair_scores(_pair_rows(_scaled(q_ref[pl.ds(r0, CHUNK), :])), kp[pl.ds(r0, BANDP), :], b_ref[...], r0, masked)
                   for r0 in r0s]
            ps = [_softmax(sc, -1).astype(bf16) for sc in scs]
            for r0, p in zip(r0s, ps):
                o_ref[pl.ds(r0, CHUNK), :] = _pair_diag(jnp.dot(p, vp[pl.ds(r0, BANDP), :], preferred_element_type=f32))

        _chunk_loops(nc, group, per_trip)

    return pl.pallas_call(
        body, name="attn_fwd", grid=(npair,),
        in_specs=[pl.BlockSpec((s, 128), lambda p: (0, p)), pl.BlockSpec((s, 128), lambda p: (0, npair + p)),
                  pl.BlockSpec((s, 128), lambda p: (0, 2 * npair + p)), pl.BlockSpec((None, 2 * CHUNK, BANDP), lambda p: (p, 0, 0))],
        out_specs=pl.BlockSpec((s, 128), lambda p: (0, p)), out_shape=_sds((s, ATT_W), f32),
        scratch_shapes=[pltpu.VMEM((PADK + s + CHUNK, 128), bf16)] * 2, compiler_params=_params(("parallel",)),
    )(qkv, qkv, qkv, bias2)


def _attn_bwd(qkv, datt, bias2):
    s = qkv.shape[0]
    nc = s // CHUNK
    npair = NH // 2
    rows = PADK + s + CHUNK
    nt = (((1,), (1,)), ((), ()))

    def body(q_ref, k_ref, v_ref, do_ref, b_ref, dq_ref, dk_ref, dv_ref, g_ref, kp, vp, dkp, dvp):
        _pad_keys(k_ref, kp, s)
        _pad_keys(v_ref, vp, s)
        dkp[...] = jnp.zeros_like(dkp)
        dvp[...] = jnp.zeros_like(dvp)
        g_ref[...] = jnp.zeros_like(g_ref)

        def group(g, masked):
            r0s = [pl.multiple_of((g * ATT_GROUP + u) * CHUNK, CHUNK) for u in range(ATT_GROUP)]
            wts = [_pair_rows(_scaled(q_ref[pl.ds(r0, CHUNK), :])) for r0 in r0s]
            dos = [_pair_rows(do_ref[pl.ds(r0, CHUNK), :]) for r0 in r0s]
            scs = [_pair_scores(wt, kp[pl.ds(r0, BANDP), :], b_ref[...], r0, masked) for wt, r0 in zip(wts, r0s)]
            dps = [lax.dot_general(do, vp[pl.ds(r0, BANDP), :], nt, preferred_element_type=f32) for do, r0 in zip(dos, r0s)]
            tn_ = (((0,), (0,)), ((), ()))
            for r0, wt, do, sc, dp in zip(r0s, wts, dos, scs, dps):
                p = _softmax(sc, -1)
                ds = p * (dp - jnp.sum(p * dp, axis=-1, keepdims=True))
                g_ref[...] += ds
                dsb = ds.astype(bf16)
                dq = jnp.dot(dsb, kp[pl.ds(r0, BANDP), :], preferred_element_type=f32)
                dq_ref[pl.ds(r0, CHUNK), :] = (_pair_diag(dq) * (HD ** -0.5)).astype(bf16)
                dkp[pl.ds(r0, BANDP), :] += lax.dot_general(dsb, wt, tn_, preferred_element_type=f32)
                dvp[pl.ds(r0, BANDP), :] += lax.dot_general(p.astype(bf16), do, tn_, preferred_element_type=f32)

        _chunk_loops(nc, group, ATT_GROUP)
        dk_ref[...] = dkp[PADK:PADK + s, :].astype(bf16)
        dv_ref[...] = dvp[PADK:PADK + s, :].astype(bf16)

    col = lambda off: pl.BlockSpec((s, 128), lambda p: (0, off + p))
    return pl.pallas_call(
        body, name="attn_bwd", grid=(npair,),
        in_specs=[col(0), col(npair), col(2 * npair), col(0), pl.BlockSpec((None, 2 * CHUNK, BANDP), lambda p: (p, 0, 0))],
        out_specs=[col(0), col(0), col(0), pl.BlockSpec((None, 2 * CHUNK, BANDP), lambda p: (p, 0, 0))],
        out_shape=[_sds((s, ATT_W), bf16)] * 3 + [_sds((npair, 2 * CHUNK, BANDP), f32)],
        scratch_shapes=[pltpu.VMEM((rows, 128), bf16)] * 2 + [pltpu.VMEM((rows, 128), f32)] * 2,
        compiler_params=_params(("parallel",)),
    )(qkv, qkv, qkv, datt, bias2)


def _rel_tables():
    onehot = np.zeros((BANDP, N_REL), np.float32)
    for j in range(BAND + CHUNK - 1):
        o = j - (CHUNK - 1)
        onehot[j, int(np.clip(PADK - o, -(CHUNK - 1), REL_CLIP)) + CHUNK - 1] = 1.0
    return onehot, np.ascontiguousarray(np.eye(CHUNK, dtype=np.float32)[::-1])


def _expand_bias(rel):
    ext = jnp.concatenate([jnp.broadcast_to(rel[:, N_REL - 1:], (NH, N_REL - 1)), rel[:, ::-1],
                           jnp.zeros((NH, BANDP - BAND + 1), f32)], axis=1)
    band = jnp.stack([ext[:, CHUNK - 1 - q:CHUNK - 1 - q + BANDP] for q in range(CHUNK)], axis=1)
    band = jnp.where(np.arange(BANDP) < BAND, band, -jnp.inf)
    return band.reshape(NH // 2, 2 * CHUNK, BANDP)


def _rel_bias_grad(gband):
    def body(g_ref, m_ref, flip_ref, o_ref, d2):
        for h in range(NH):
            rev = jnp.dot(flip_ref[...], g_ref[h], precision=HIGHEST, preferred_element_type=f32)
            rolled = pltpu.roll(rev, 0, 1, stride=1, stride_axis=0)
            d2[h:h + 1, :] = jnp.sum(rolled, axis=0, keepdims=True)
        o_ref[...] = jnp.dot(d2[...], m_ref[...], precision=HIGHEST, preferred_element_type=f32)

    onehot, flip = _rel_tables()
    return pl.pallas_call(
        body, name="rel_bias_grad", out_shape=_sds((NH, N_REL), f32), scratch_shapes=[pltpu.VMEM((NH, BANDP), f32)],
    )(gband, jnp.asarray(onehot), jnp.asarray(flip))


XBC_BLK = 512
XBC_COL0 = SSD_W // XBC_BLK
DT_COL = (SSD_W + XBC) // 128


def _conv_taps(ext, w_ref, b_ref, tm):
    n = ext.shape[0]
    pre = w_ref[3:4, :] * ext + b_ref[...]
    for j in range(3):
        pre = pre + w_ref[j:j + 1, :] * pltpu.roll(ext, 3 - j, 0)
    return pre


def _ssd_conv(proj2, conv_w, conv_b, tm):
    s = proj2.shape[0]
    nb = XBC // XBC_BLK

    def body(x_ref, p_ref, w_ref, b_ref, o_ref):
        i = pl.program_id(1)
        prev = jnp.where(i > 0, p_ref[...], 0.0)
        ext = jnp.concatenate([prev, x_ref[...]], axis=0)
        pre = _conv_taps(ext, w_ref, b_ref, tm)[8:8 + tm]
        o_ref[...] = pre * _sigmoid(pre)

    return pl.pallas_call(
        body, name="ssd_conv", grid=(nb, s // tm),
        in_specs=[pl.BlockSpec((tm, XBC_BLK), lambda j, i: (i, XBC_COL0 + j)),
                  pl.BlockSpec((8, XBC_BLK), lambda j, i: (jnp.maximum(i * (tm // 8) - 1, 0), XBC_COL0 + j)),
                  pl.BlockSpec((4, XBC_BLK), lambda j, i: (0, j)), pl.BlockSpec((1, XBC_BLK), lambda j, i: (0, j))],
        out_specs=pl.BlockSpec((tm, XBC_BLK), lambda j, i: (i, j)), out_shape=_sds((s, XBC), f32),
        compiler_params=_params(("parallel", "parallel")),
    )(proj2, proj2, conv_w, conv_b)


def _ssd_conv_bwd(dxbc, proj2, conv_w, conv_b, tm):
    s = proj2.shape[0]
    nb = XBC // XBC_BLK
    n = s // tm
    last8 = s // 8 - 1

    def body(x_ref, xp_ref, xn_ref, d_ref, dn_ref, w_ref, b_ref, o_ref, dw_ref, db_ref, acc):
        i = pl.program_id(1)

        @pl.when(i == 0)
        def _():
            acc[...] = jnp.zeros_like(acc)

        prev = jnp.where(i > 0, xp_ref[...], 0.0)
        ext = jnp.concatenate([prev, x_ref[...], xn_ref[...]], axis=0)
        pre = _conv_taps(ext, w_ref, b_ref, tm)
        sg = _sigmoid(pre)
        dnext = jnp.where(i < n - 1, dn_ref[...], 0.0)
        dext = jnp.concatenate([jnp.zeros((8, XBC_BLK), f32), d_ref[...], dnext], axis=0)
        dpre = dext * (sg * (1.0 + pre * (1.0 - sg)))
        rows = tm + 16
        dx = w_ref[3:4, :] * dpre
        for j in range(3):
            dx = dx + w_ref[j:j + 1, :] * pltpu.roll(dpre, rows - (3 - j), 0)
        o_ref[...] = dx[8:8 + tm].astype(bf16)
        dcur = dpre[8:8 + tm]
        acc[4] += _fold8(dcur)
        acc[3] += _fold8(dcur * ext[8:8 + tm])
        for j in range(3):
            acc[j] += _fold8(dcur * pltpu.roll(ext, 3 - j, 0)[8:8 + tm])

        @pl.when(i == n - 1)
        def _():
            for j in range(4):
                dw_ref[j:j + 1, :] = jnp.sum(acc[j], axis=0, keepdims=True)
            db_ref[...] = jnp.sum(acc[4], axis=0, keepdims=True)

    xcol = lambda j: XBC_COL0 + j
    return pl.pallas_call(
        body, name="ssd_conv_bwd", grid=(nb, n),
        in_specs=[pl.BlockSpec((tm, XBC_BLK), lambda j, i: (i, xcol(j))),
                  pl.BlockSpec((8, XBC_BLK), lambda j, i: (jnp.maximum(i * (tm // 8) - 1, 0), xcol(j))),
                  pl.BlockSpec((8, XBC_BLK), lambda j, i: (jnp.minimum((i + 1) * (tm // 8), last8), xcol(j))),
                  pl.BlockSpec((tm, XBC_BLK), lambda j, i: (i, j)),
                  pl.BlockSpec((8, XBC_BLK), lambda j, i: (jnp.minimum((i + 1) * (tm // 8), last8), j)),
                  pl.BlockSpec((4, XBC_BLK), lambda j, i: (0, j)), pl.BlockSpec((1, XBC_BLK), lambda j, i: (0, j))],
        out_specs=[pl.BlockSpec((tm, XBC_BLK), lambda j, i: (i, j)), pl.BlockSpec((4, XBC_BLK), lambda j, i: (0, j)),
                   pl.BlockSpec((1, XBC_BLK), lambda j, i: (0, j))],
        out_shape=[_sds((s, XBC), bf16), _sds((4, XBC), f32), _sds((1, XBC), f32)],
        scratch_shapes=[pltpu.VMEM((5, 8, XBC_BLK), f32)], compiler_params=_params(("parallel", "arbitrary")),
    )(proj2, proj2, proj2, dxbc, dxbc, conv_w, conv_b)


def _ssd_consts():
    ex = np.zeros((128, SSD_W), np.float32)
    for h in range(NH):
        ex[h, h * HD:(h + 1) * HD] = 1.0
    sel = np.zeros((8, 128), np.float32)
    for h in range(NH):
        sel[h // 2, h] = 1.0
    par = np.zeros((128, 128), np.float32)
    for r in range(128):
        for h in range(NH):
            par[r, h] = 1.0 if (h % 2) == (r // 64) else 0.0
    ones_blk = np.zeros((128, 128), np.float32)
    for r in range(128):
        ones_blk[r, (r // 64) * 64:(r // 64) * 64 + 64] = 1.0
    return ex, np.ascontiguousarray(ex.T), sel, par, ones_blk


SSD_SUB = 8


def _ssd_common(rs, xbc_ref, dtr_ref, a_ref, dtb_ref, ex_ref, sel_ref, par_ref):
    xs = xbc_ref[rs, 0:SSD_W]
    dt = _softplus(dtr_ref[rs, :] + dtb_ref[...])
    adt = dt * a_ref[...]
    r_i = lax.broadcasted_iota(jnp.int32, (CHUNK, CHUNK), 0)
    c_i = lax.broadcasted_iota(jnp.int32, (CHUNK, CHUNK), 1)
    tril = (r_i >= c_i).astype(f32)
    cs = _dot01(tril, adt, exact="a")
    cs2 = jnp.concatenate([cs, cs], axis=0) * par_ref[...]
    cstp = _dot01(sel_ref[...], cs2, tb=True, exact="a")
    both = _dot01(jnp.concatenate([dt, cs], axis=0), ex_ref[...])
    return xs, dt, cs, cstp, both[0:CHUNK], both[CHUNK:]


def _pair_mask():
    l_i = lax.broadcasted_iota(jnp.int32, (CHUNK, 128), 0)
    lane = lax.broadcasted_iota(jnp.int32, (CHUNK, 128), 1)
    return l_i >= (lane % CHUNK), lane < HD


def _block_diag(xp, first):
    z = jnp.zeros_like(xp)
    return jnp.concatenate([jnp.where(first, xp, z), jnp.where(first, z, xp)], axis=0)


def _ssd_fwd(xbc, proj2, a_row, dtb_row, dsk_full):
    s = xbc.shape[0]
    nc = s // CHUNK
    ex, ext, sel, par, ones_blk = _ssd_consts()

    def one_chunk(sub, states, refs):
        xbc_ref, dtr_ref, a_ref, dtb_ref, dsk_ref, ex_ref, sel_ref, par_ref, y_ref, hs_ref = refs
        rs = slice(sub * CHUNK, (sub + 1) * CHUNK)
        xs, dt, cs, cstp, dt_full, cs_full = _ssd_common(rs, xbc_ref, dtr_ref, a_ref, dtb_ref, ex_ref, sel_ref, par_ref)
        cs_last = cs_full[CHUNK - 1:CHUNK, :]
        xdt = xs * dt_full
        causal, first = _pair_mask()
        out = []
        for g in range(NG):
            gl = slice(g * GW, (g + 1) * GW)
            bg = xbc_ref[rs, SSD_W + g * NSTATE:SSD_W + (g + 1) * NSTATE].astype(bf16)
            cg = xbc_ref[rs, SSD_W + NG * NSTATE + g * NSTATE:SSD_W + NG * NSTATE + (g + 1) * NSTATE].astype(bf16)
            cb2 = lax.dot_general(cg, jnp.concatenate([bg, bg], axis=0), (((1,), (1,)), ((), ())), preferred_element_type=f32)
            hg = states[g]
            hs_ref[sub, g] = hg
            y0 = jnp.dot(cg, hg.astype(bf16), preferred_element_type=f32)
            yoff = jnp.exp(cs_full[:, gl]) * y0
            for j in range(GW // 128):
                pair = g * (GW // 128) + j
                pl_ = slice(pair * 128, (pair + 1) * 128)
                seg = jnp.exp(jnp.where(causal, cs_full[:, pl_] - cstp[pair:pair + 1, :], -jnp.inf))
                m = (cb2 * seg).astype(bf16)
                yd = jnp.dot(m, _block_diag(xdt[:, pl_].astype(bf16), first), preferred_element_type=f32)
                y_ref[rs, pl_] = yd + yoff[:, j * 128:(j + 1) * 128] + xs[:, pl_] * dsk_ref[:, pl_]
            xdec = (xdt[:, gl] * jnp.exp(cs_last[:, gl] - cs_full[:, gl])).astype(bf16)
            st = lax.dot_general(bg, xdec, (((0,), (0,)), ((), ())), preferred_element_type=f32)
            out.append(jnp.exp(cs_last[:, gl]) * hg + st)
        return out

    def body(*refs):
        hst = refs[-1]

        @pl.when(pl.program_id(0) == 0)
        def _():
            hst[...] = jnp.zeros_like(hst)

        states = [hst[g] for g in range(NG)]
        for sub in range(SSD_SUB):
            states = one_chunk(sub, states, refs[:-1])
        for g in range(NG):
            hst[g] = states[g]

    rows = SSD_SUB * CHUNK
    const = lambda shape: pl.BlockSpec(shape, lambda c: tuple(0 for _ in shape))
    return pl.pallas_call(
        body, name="ssd_fwd", grid=(nc // SSD_SUB,),
        in_specs=[pl.BlockSpec((rows, XBC), lambda c: (c, 0)), pl.BlockSpec((rows, 128), lambda c: (c, DT_COL)),
                  const((1, 128)), const((1, 128)), const((1, SSD_W)), const((128, SSD_W)), const((8, 128)), const((128, 128))],
        out_specs=[pl.BlockSpec((rows, SSD_W), lambda c: (c, 0)), pl.BlockSpec((SSD_SUB, NG, NSTATE, GW), lambda c: (c, 0, 0, 0))],
        out_shape=[_sds((s, SSD_W), f32), _sds((nc, NG, NSTATE, GW), f32)],
        scratch_shapes=[pltpu.VMEM((NG, NSTATE, GW), f32)], compiler_params=_params(("arbitrary",)),
    )(xbc, proj2, a_row, dtb_row, dsk_full, jnp.asarray(ex), jnp.asarray(sel), jnp.asarray(par))


def _ssd_bwd(xbc, proj2, dy, hsave, a_row, dtb_row, dsk_full):
    s = xbc.shape[0]
    nc = s // CHUNK
    ex, ext, sel, par, ones_blk = _ssd_consts()

    def one_chunk(sub, dhs, refs):
        (xbc_ref, dtr_ref, dy_ref, hs_ref, a_ref, dtb_ref, dsk_ref, ex_ref, ext_ref, sel_ref, par_ref, ob_ref,
         dxbc_ref, ddtr_ref, dd_ref, da_ref, ddtb_ref, dh, a_dd, a_da, a_dtb, dcs_lane, dcs_b, dxdt) = refs
        rs = slice(sub * CHUNK, (sub + 1) * CHUNK)
        dcs_lane, dcs_b, dxdt = dcs_lane.at[sub], dcs_b.at[sub], dxdt.at[sub]
        xs, dt, cs, cstp, dt_full, cs_full = _ssd_common(rs, xbc_ref, dtr_ref, a_ref, dtb_ref, ex_ref, sel_ref, par_ref)
        cs_last = cs_full[CHUNK - 1:CHUNK, :]
        xdt = xs * dt_full
        dyv = dy_ref[rs, :]
        a_dd[...] += _fold8(dyv * xs)
        causal, first = _pair_mask()
        diag = lax.broadcasted_iota(jnp.int32, (CHUNK, 128), 0) == lax.broadcasted_iota(jnp.int32, (CHUNK, 128), 1) % CHUNK
        dh_out = []
        for g in range(NG):
            gl = slice(g * GW, (g + 1) * GW)
            bcol = slice(SSD_W + g * NSTATE, SSD_W + (g + 1) * NSTATE)
            ccol = slice(SSD_W + NG * NSTATE + g * NSTATE, SSD_W + NG * NSTATE + (g + 1) * NSTATE)
            bg = xbc_ref[rs, bcol].astype(bf16)
            cg = xbc_ref[rs, ccol].astype(bf16)
            bg2 = jnp.concatenate([bg, bg], axis=0)
            cb2 = lax.dot_general(cg, bg2, (((1,), (1,)), ((), ())), preferred_element_type=f32)
            hg = hs_ref[sub, g]
            hgb = hg.astype(bf16)
            dhg = dhs[g]
            dhgb = dhg.astype(bf16)
            eg = jnp.exp(cs_full[:, gl])
            dec = jnp.exp(cs_last[:, gl] - cs_full[:, gl])
            gam = jnp.exp(cs_last[:, gl])
            dyg = dyv[:, gl]
            xdt_g = xdt[:, gl]
            y0 = jnp.dot(cg, hgb, preferred_element_type=f32)
            dy0 = (eg * dyg).astype(bf16)
            dcm = lax.dot_general(dy0, hgb, (((1,), (1,)), ((), ())), preferred_element_type=f32)
            dh_prev = gam * dhg + lax.dot_general(cg, dy0, (((0,), (0,)), ((), ())), preferred_element_type=f32)
            dgam = jnp.sum(dhg * hg, axis=0, keepdims=True) * gam
            dxdec = jnp.dot(bg, dhgb, preferred_element_type=f32)
            dbm = lax.dot_general((xdt_g * dec).astype(bf16), dhgb, (((1,), (1,)), ((), ())), preferred_element_type=f32)
            t = dxdec * xdt_g * dec
            dcs_lane[:, gl] = dyg * eg * y0 - t
            dcs_lane[CHUNK - 1:CHUNK, gl] += jnp.sum(t, axis=0, keepdims=True) + dgam
            dxdt[:, gl] = dxdec * dec
            dcb2 = jnp.zeros((CHUNK, 128), f32)
            for j in range(GW // 128):
                pair = g * (GW // 128) + j
                pl_ = slice(pair * 128, (pair + 1) * 128)
                seg = jnp.exp(jnp.where(causal, cs_full[:, pl_] - cstp[pair:pair + 1, :], -jnp.inf))
                m = cb2 * seg
                mb = m.astype(bf16)
                rhs = _block_diag(xdt[:, pl_].astype(bf16), first)
                dyp = dyv[:, pl_].astype(bf16)
                dm = lax.dot_general(dyp, rhs, (((1,), (1,)), ((), ())), preferred_element_type=f32)
                tt = lax.dot_general(mb, dyp, (((0,), (0,)), ((), ())), preferred_element_type=f32)
                dxdt[:, pl_] += jnp.where(first, tt[0:CHUNK], tt[CHUNK:])
                dcb2 = dcb2 + dm * seg
                w = dm * m
                colsum = jnp.sum(w, axis=0, keepdims=True)
                dcs_b[:, pl_] = _dot01(w - jnp.where(diag, colsum, 0.0), ob_ref[...])
            dcb2b = dcb2.astype(bf16)
            dcm = dcm + jnp.dot(dcb2b, bg2, preferred_element_type=f32)
            t3 = lax.dot_general(dcb2b, cg, (((0,), (0,)), ((), ())), preferred_element_type=f32)
            dxbc_ref[rs, bcol] = dbm + t3[0:CHUNK] + t3[CHUNK:]
            dxbc_ref[rs, ccol] = dcm
            dh_out.append(dh_prev)
        dxdtv = dxdt[...]
        both = _dot01(jnp.concatenate([dcs_lane[...] + dcs_b[...] * (1.0 / HD), dxdtv * xs], axis=0), ext_ref[...])
        dcs = both[0:CHUNK]
        r_i = lax.broadcasted_iota(jnp.int32, (CHUNK, CHUNK), 0)
        c_i = lax.broadcasted_iota(jnp.int32, (CHUNK, CHUNK), 1)
        triu = (r_i <= c_i).astype(f32)
        da_ = _dot01(triu, dcs, exact="a")
        ddt = da_ * a_ref[...] + both[CHUNK:]
        a_da[...] += _fold8(da_ * dt)
        dxbc_ref[rs, 0:SSD_W] = dyv * dsk_ref[...] + dxdtv * dt_full
        ddtr = ddt * _sigmoid(dtr_ref[rs, :] + dtb_ref[...])
        ddtr_ref[rs, :] = ddtr
        a_dtb[...] += _fold8(ddtr)
        return dh_out

    nsteps = nc // SSD_SUB

    def body(*refs):
        dd_ref, da_ref, ddtb_ref, dh, a_dd, a_da, a_dtb = refs[14:21]
        ext_ref = refs[8]
        step = pl.program_id(0)

        @pl.when(step == 0)
        def _():
            dh[...] = jnp.zeros_like(dh)
            a_dd[...] = jnp.zeros_like(a_dd)
            a_da[...] = jnp.zeros_like(a_da)
            a_dtb[...] = jnp.zeros_like(a_dtb)

        dhs = [dh[g] for g in range(NG)]
        for sub in reversed(range(SSD_SUB)):
            dhs = one_chunk(sub, dhs, refs)
        for g in range(NG):
            dh[g] = dhs[g]

        @pl.when(step == nsteps - 1)
        def _():
            dd_ref[...] = jnp.sum(jnp.dot(a_dd[...], ext_ref[...], precision=HIGHEST, preferred_element_type=f32), axis=0, keepdims=True)
            da_ref[...] = jnp.sum(a_da[...], axis=0, keepdims=True)
            ddtb_ref[...] = jnp.sum(a_dtb[...], axis=0, keepdims=True)

    rev = lambda c: nsteps - 1 - c
    rows = SSD_SUB * CHUNK
    const = lambda shape: pl.BlockSpec(shape, lambda c: tuple(0 for _ in shape))
    return pl.pallas_call(
        body, name="ssd_bwd", grid=(nsteps,),
        in_specs=[pl.BlockSpec((rows, XBC), lambda c: (rev(c), 0)), pl.BlockSpec((rows, 128), lambda c: (rev(c), DT_COL)),
                  pl.BlockSpec((rows, SSD_W), lambda c: (rev(c), 0)), pl.BlockSpec((SSD_SUB, NG, NSTATE, GW), lambda c: (rev(c), 0, 0, 0)),
                  const((1, 128)), const((1, 128)), const((1, SSD_W)), const((128, SSD_W)), const((SSD_W, 128)),
                  const((8, 128)), const((128, 128)), const((128, 128))],
        out_specs=[pl.BlockSpec((rows, XBC), lambda c: (rev(c), 0)), pl.BlockSpec((rows, 128), lambda c: (rev(c), 0)),
                   const((1, 128)), const((1, 128)), const((1, 128))],
        out_shape=[_sds((s, XBC), f32), _sds((s, 128), f32), _sds((1, 128), f32), _sds((1, 128), f32), _sds((1, 128), f32)],
        scratch_shapes=[pltpu.VMEM((NG, NSTATE, GW), f32), pltpu.VMEM((8, SSD_W), f32), pltpu.VMEM((8, 128), f32), pltpu.VMEM((8, 128), f32)]
        + [pltpu.VMEM((SSD_SUB, CHUNK, SSD_W), f32)] * 3,
        compiler_params=_params(("arbitrary",)),
    )(xbc, proj2, dy, hsave, a_row, dtb_row, dsk_full, jnp.asarray(ex), jnp.asarray(ext), jnp.asarray(sel), jnp.asarray(par),
      jnp.asarray(ones_blk))


def _local_step(x, tgt, mods, g_mix, rel, conv_w, conv_b, dt_bias, a_log, d_skip, g_att, g_ssd, g_ffn, g_final, weights):
    s = x.shape[0]
    tm_e = 512 if s % 512 == 0 else s
    tm_m = 512 if s % 512 == 0 else s
    tm_l = 1024 if s % 1024 == 0 else s
    tk = 2048 if s % 2048 == 0 else s
    sh1, sc1, gt1, sh2, sc2, gt2 = [mods[:, i * D:(i + 1) * D] for i in range(6)]

    h1b = _norm_mod("norm_mod_1", x, g_mix, sc1, sh1, tm_e)
    win, win_b = weights.w_in(h1b)
    qkv = _mm_nn_fullk("proj_qkv", h1b, win, tm_l, 1536, bf16, n=IN_A)
    proj2 = _mm_nn_fullk("proj_zxbcdt", h1b, win_b, tm_l, 896, f32)
    bias = _expand_bias(rel)
    att = _attn_fwd(qkv, bias)
    xbc = _ssd_conv(proj2, conv_w, conv_b, tm_l)
    a_row = jnp.pad(-jnp.exp(a_log), ((0, 0), (0, 128 - NH)))
    dtb_row = jnp.pad(dt_bias, ((0, 0), (0, 128 - NH)))
    dsk_full = jnp.repeat(d_skip, HD, axis=1)
    y, hsave = _ssd_fwd(xbc, proj2, a_row, dtb_row, dsk_full)
    mixcat = _mix_pre(att, y, proj2, g_att, g_ssd, tm_e)
    wout = weights.w_out(mixcat)
    mix = _mm_nn_fullk("proj_out", mixcat, wout, tm_l, D, f32)
    x2, h2b = _resid_norm_mod(x, gt1, mix, g_ffn, sc2, sh2, tm_e)
    wg4, wu4, wd4 = weights.ffn(h2b)
    act, sil, ud = _ffn_up(h2b, wg4, wu4, tm_m)
    ffn = _ffn_down(act, wd4, tm_m)

    dx3, dffn, loss, dg_final, dgt2 = _final_fwd_bwd(x2, ffn, gt2, g_final, tgt, tm_e)
    gwd4 = _grad_wdown4(act, dffn, 1024, tk)
    dgate, dup = _ffn_dact(dffn, wd4, sil, ud, tm_l)
    tk2 = 4096 if s % 4096 == 0 else s
    tok = weights.grad(("w_down", "w_gate", "w_up"),
                       [gwd4, _grad_cols4("grad_w_gate", h2b, dgate, 512, tk2), _grad_cols4("grad_w_up", h2b, dup, 512, tk2)])
    dh2 = _ffn_dh(dgate, dup, wg4, wu4, tm_m, dep=tok)
    dx2, dmix, dsc2, dsh2, dg_ffn, dgt1 = _norm_mod_bwd("norm_mod_bwd_2", dh2, x2, g_ffn, sc2, dx3, tm_e, mix=mix, gt=gt1)
    gwout4 = _mm_tn("grad_w_out", mixcat, dmix, 512, 1024, tk2, bf16).reshape(NSH, D // NSH, D)
    dmc = _mm_nt("dmixcat", dmix, wout, tm_l, D, D, f32)
    datt, dy, dz, dg_att, dg_ssd = _mix_pre_bwd(dmc, att, y, proj2, g_att, g_ssd, tm_e)
    dq, dk, dv, gband = _attn_bwd(qkv, datt, bias)
    drel = _rel_bias_grad(gband.reshape(NH, CHUNK, BANDP))
    dxbc, ddtr, dd_row, da_row, ddtb_row = _ssd_bwd(xbc, proj2, dy, hsave, a_row, dtb_row, dsk_full)
    dxbc_raw, dconv_w, dconv_b = _ssd_conv_bwd(dxbc, proj2, conv_w, conv_b, tm_l)
    dproj = jnp.concatenate([dq, dk, dv, dz, dxbc_raw, ddtr.astype(bf16)], axis=1)
    gwin = _mm_tn("grad_w_in", h1b, dproj, 512, 1152, tk2, bf16)
    gwin4 = jnp.stack([jnp.pad(gwin[:, k * IN_SH:(k + 1) * IN_SH], ((0, 0), (0, IN_SHP - IN_SH))) for k in range(NSH)])
    tok = weights.grad(("w_out", "w_in"), [gwout4, gwin4])
    dh1 = _mm_nt("dh1", dproj, win, tm_m, 1024, IN_P, f32, dep=tok)
    grad_x, dsc1, dsh1, dg_mix = _norm_mod_bwd("norm_mod_bwd_1", dh1, x, g_mix, sc1, dx2, tm_e)

    dmods = jnp.concatenate([dsh1, dsc1, dgt1, dsh2, dsc2, dgt2], axis=1)
    dd_skip = dd_row[:, :NH]
    da_log = da_row[:, :NH] * a_row[:, :NH]
    small = dict(g_mix=dg_mix, conv_b=dconv_b, dt_bias=ddtb_row[:, :NH], a_log=da_log, d_skip=dd_skip, g_att_out=dg_att,
                 g_ssd_out=dg_ssd, g_ffn=dg_ffn, g_final=dg_final, rel_bias=drel, conv_w=dconv_w)
    return loss[0, 0], grad_x, dmods, small


HBM = pl.BlockSpec(memory_space=pl.ANY)
VMEM = pl.BlockSpec(memory_space=pltpu.VMEM)


def _place():
    x, y, c = lax.axis_index("x"), lax.axis_index("y"), lax.axis_index("c")
    chips = [(1 - x, y), (x, 1 - y), (1 - x, 1 - y)]
    return x, y, c, chips


def _allgather8(name, payload, dep=None):
    r = payload.shape[0]
    deps = [] if dep is None else [dep]

    def body(x_ref, *rest):
        out_ref, send_sems, recv_sems, local_sem = rest[-4:]
        x, y, c, chips = _place()
        me, sibling = (x, y, c), (x, y, 1 - c)

        def slot(px, py, pc):
            return out_ref.at[4 * px + 2 * py + pc]

        def copy(k, block, to, src=None):
            return pltpu.make_async_remote_copy(
                src_ref=slot(*block) if src is None else src, dst_ref=slot(*block),
                send_sem=send_sems.at[k], recv_sem=recv_sems.at[k], device_id=to, device_id_type=MESH)

        mine = pltpu.make_async_copy(x_ref, slot(*me), local_sem)
        mine.start()
        first = [copy(0, me, sibling, src=x_ref)]
        first += [copy(1 + j, me, (*chip, c), src=x_ref) for j, chip in enumerate(chips)]
        for cp in first:
            cp.start()
        passed = [copy(4 + j, (*chip, c), sibling) for j, chip in enumerate(chips)]
        for j, chip in enumerate(chips):
            copy(1 + j, (*chip, c), me).wait_recv()
            passed[j].start()
        copy(0, sibling, me).wait_recv()
        for j, chip in enumerate(chips):
            copy(4 + j, (*chip, 1 - c), me).wait_recv()
        for cp in first + passed:
            cp.wait_send()
        mine.wait()

    return pl.pallas_call(
        body, name=name, out_shape=_sds((N_DEV, r, 128), f32), in_specs=[VMEM] * (1 + len(deps)), out_specs=VMEM,
        scratch_shapes=[pltpu.SemaphoreType.DMA((7,)), pltpu.SemaphoreType.DMA((7,)), pltpu.SemaphoreType.DMA],
    )(payload, *deps)


def _sum8(g):
    r = g.shape[1]

    def body(g_ref, o_ref):
        acc = g_ref[0]
        for i in range(1, N_DEV):
            acc = acc + g_ref[i]
        o_ref[...] = acc

    return pl.pallas_call(body, name="sum8", out_shape=_sds((r, 128), f32))(g)


SEM = pl.BlockSpec(memory_space=pltpu.SEMAPHORE)
EFFECT = pltpu.SideEffectType.DATAFLOW_SIDE_EFFECTING


def _gather_copies(ins, lands, send_sems, recv_sems):
    x, y, c, chips = _place()
    k = 2 * x + y
    starts, recvs = [], []
    for w in range(len(ins)):
        for j, (px, py) in enumerate(chips):
            def mk(dst):
                return pltpu.make_async_remote_copy(src_ref=ins[w].at[c], dst_ref=dst, send_sem=send_sems[w].at[j],
                                                    recv_sem=recv_sems[w].at[j], device_id=(px, py, c), device_id_type=MESH)
            starts.append(mk(lands[w].at[k, c]))
            recvs.append(mk(lands[w].at[2 * px + py, c]))
    return starts, recvs


def _reduce_copies(ins, lands, send_sems, recv_sems):
    x, y, c, chips = _place()
    k = 2 * x + y
    starts, recvs = [], []
    for w in range(len(ins)):
        for j, (px, py) in enumerate(chips):
            def mk(dst):
                return pltpu.make_async_remote_copy(src_ref=ins[w].at[2 * px + py], dst_ref=dst, send_sem=send_sems[w].at[j],
                                                    recv_sem=recv_sems[w].at[j], device_id=(px, py, c), device_id_type=MESH)
            starts.append(mk(lands[w].at[k]))
            recvs.append(mk(lands[w].at[2 * px + py]))
    return starts, recvs


def _split_start(name, copies, srcs, land_shapes):
    nw = len(srcs)

    def body(*refs):
        starts, _ = copies(refs[:nw], refs[nw:2 * nw], refs[2 * nw:3 * nw], refs[3 * nw:4 * nw])
        for cp in starts:
            cp.start()
        refs[6 * nw][...] = jnp.zeros((8, 128), f32)

    sems = [pltpu.SemaphoreType.DMA((3,))] * nw
    bufs = [pltpu.HBM(s.shape, bf16) for s in srcs] + [pltpu.HBM(s, bf16) for s in land_shapes]
    res = pl.pallas_call(
        body, name=name, out_shape=sems + sems + bufs + [_sds((8, 128), f32)],
        in_specs=[HBM] * (2 * nw), out_specs=[SEM] * (2 * nw) + [HBM] * (2 * nw) + [VMEM],
        input_output_aliases={i: 2 * nw + i for i in range(2 * nw)},
        compiler_params=pltpu.CompilerParams(has_side_effects=EFFECT),
    )(*[pltpu.with_memory_space_constraint(s, pltpu.HBM) for s in srcs],
      *[pltpu.with_memory_space_constraint(lax.empty(s, bf16), pltpu.HBM) for s in land_shapes])
    return res[:nw], res[nw:2 * nw], res[2 * nw:3 * nw], res[3 * nw:4 * nw], res[4 * nw]


def _split_wait(name, copies, send_sems, recv_sems, srcs, lands, after):
    nw = len(srcs)

    def body(*refs):
        starts, recvs = copies(refs[:nw], refs[nw:2 * nw], refs[2 * nw:3 * nw], refs[3 * nw:4 * nw])
        for s_, r_ in zip(starts, recvs):
            s_.wait_send()
            r_.wait_recv()

    bufs = [pltpu.HBM(s.shape, bf16) for s in srcs] + [pltpu.HBM(l.shape, bf16) for l in lands]
    res = pl.pallas_call(
        body, name=name, out_shape=bufs, in_specs=[HBM] * (2 * nw) + [SEM] * (2 * nw) + [HBM], out_specs=[HBM] * (2 * nw),
        input_output_aliases={i: i for i in range(2 * nw)},
        compiler_params=pltpu.CompilerParams(has_side_effects=EFFECT),
    )(*srcs, *lands, *send_sems, *recv_sems, after)
    return res[:nw], res[nw:]


def _gather_forward(name, shards, lands):
    nw = len(shards)

    def body(*refs):
        ins, lands_in, outs = refs[:nw], refs[nw:2 * nw], refs[2 * nw:3 * nw]
        st_a, st_b, st_c = refs[3 * nw:4 * nw], refs[4 * nw:5 * nw], refs[5 * nw:6 * nw]
        send_sems, recv_sems, load_sems, store_sems = refs[6 * nw:]
        x, y, c, chips = _place()
        k = 2 * x + y
        sibling = (x, y, 1 - c)
        ld_a = [pltpu.make_async_copy(ins[w].at[c], st_a[w], load_sems.at[w, 0]) for w in range(nw)]
        ld_b = [pltpu.make_async_copy(ins[w].at[1 - c], st_b[w], load_sems.at[w, 1]) for w in range(nw)]
        for cp in ld_a + ld_b:
            cp.start()
        st_own = []
        for w in range(nw):
            ld_a[w].wait()
            st_own.append(pltpu.make_async_copy(st_a[w], outs[w].at[k, c], store_sems.at[w, 0]))
            st_own[-1].start()
        for w in range(nw):
            ld_b[w].wait()
            st_own.append(pltpu.make_async_copy(st_b[w], outs[w].at[k, 1 - c], store_sems.at[w, 1]))
            st_own[-1].start()
        for cp in st_own:
            cp.wait()
        fwds = {}
        for j, (px, py) in enumerate(chips):
            kq = 2 * px + py
            for w in range(nw):
                slot = st_b[w] if j % 2 == 0 else st_c[w]
                if j == 2:
                    fwds[w, 0].wait_send()
                ld = pltpu.make_async_copy(lands_in[w].at[kq, c], slot, load_sems.at[w, 2 + j])
                ld.start()
                ld.wait()
                fwds[w, j] = pltpu.make_async_remote_copy(src_ref=slot, dst_ref=outs[w].at[kq, c], send_sem=send_sems.at[w, j],
                                                          recv_sem=recv_sems.at[w, j], device_id=sibling, device_id_type=MESH)
                fwds[w, j].start()
        for j, (px, py) in enumerate(chips):
            for w in range(nw):
                pltpu.make_async_remote_copy(src_ref=st_c[w], dst_ref=outs[w].at[2 * px + py, 1 - c], send_sem=send_sems.at[w, j],
                                             recv_sem=recv_sems.at[w, j], device_id=sibling, device_id_type=MESH).wait_recv()
        for w in range(nw):
            fwds[w, 1].wait_send()
            fwds[w, 2].wait_send()

    stage = [pltpu.VMEM(s.shape[1:], bf16) for s in shards]
    return pl.pallas_call(
        body, name=name, out_shape=[_sds(l.shape, bf16) for l in lands],
        in_specs=[HBM] * (2 * nw), out_specs=[HBM] * nw, input_output_aliases={nw + w: w for w in range(nw)},
        scratch_shapes=stage * 3 + [pltpu.SemaphoreType.DMA((nw, 3)), pltpu.SemaphoreType.DMA((nw, 3)), pltpu.SemaphoreType.DMA((nw, 5)),
                                    pltpu.SemaphoreType.DMA((nw, 2))],
        compiler_params=pltpu.CompilerParams(vmem_limit_bytes=VMEM_LIMIT),
    )(*shards, *lands)


def _rs_pair_exchange(name, grads):
    nw = len(grads)

    def body(*refs):
        ins, got, stage = refs[:nw], refs[nw:2 * nw], refs[2 * nw:3 * nw]
        send_sems, recv_sems, load_sems = refs[3 * nw:]
        x, y, c, _ = _place()

        def load(w, kk):
            return pltpu.make_async_copy(ins[w].at[kk, 1 - c], stage[w].at[kk % 2], load_sems.at[w, kk])

        def send(w, kk):
            return pltpu.make_async_remote_copy(src_ref=stage[w].at[kk % 2], dst_ref=got[w].at[kk], send_sem=send_sems.at[w, kk],
                                                recv_sem=recv_sems.at[w, kk], device_id=(x, y, 1 - c), device_id_type=MESH)

        for kk in range(2):
            for w in range(nw):
                load(w, kk).start()
        for kk in range(NSH):
            for w in range(nw):
                load(w, kk).wait()
                send(w, kk).start()
            if kk + 2 < NSH:
                for w in range(nw):
                    send(w, kk).wait_send()
                    load(w, kk + 2).start()
        for kk in range(NSH - 2, NSH):
            for w in range(nw):
                send(w, kk).wait_send()
        for kk in range(NSH):
            for w in range(nw):
                send(w, kk).wait_recv()

    return pl.pallas_call(
        body, name=name, out_shape=[_sds((NSH,) + g.shape[2:], bf16) for g in grads], in_specs=[HBM] * nw, out_specs=[HBM] * nw,
        scratch_shapes=[pltpu.VMEM((2,) + g.shape[2:], bf16) for g in grads]
        + [pltpu.SemaphoreType.DMA((nw, NSH)), pltpu.SemaphoreType.DMA((nw, NSH)), pltpu.SemaphoreType.DMA((nw, NSH))],
        compiler_params=pltpu.CompilerParams(vmem_limit_bytes=VMEM_LIMIT),
    )(*grads)


def _rs_pair_gather(name, halves):
    nw = len(halves)

    def body(*refs):
        ins, outs, stage = refs[:nw], refs[nw:2 * nw], refs[2 * nw:3 * nw]
        send_sems, recv_sems, local_sems, stage_sems = refs[3 * nw:]
        x, y, c, _ = _place()
        loads = [pltpu.make_async_copy(ins[w], stage[w], stage_sems.at[w]) for w in range(nw)]
        for cp in loads:
            cp.start()
        local, cps = [], []
        for w in range(nw):
            loads[w].wait()
            local.append(pltpu.make_async_copy(stage[w], outs[w].at[c], local_sems.at[w]))
            cps.append(pltpu.make_async_remote_copy(src_ref=stage[w], dst_ref=outs[w].at[c], send_sem=send_sems.at[w],
                                                    recv_sem=recv_sems.at[w], device_id=(x, y, 1 - c), device_id_type=MESH))
            local[w].start()
            cps[w].start()
        for w in range(nw):
            pltpu.make_async_remote_copy(src_ref=stage[w], dst_ref=outs[w].at[1 - c], send_sem=send_sems.at[w], recv_sem=recv_sems.at[w],
                                         device_id=(x, y, 1 - c), device_id_type=MESH).wait_recv()
        for cp in cps:
            cp.wait_send()
        for cp in local:
            cp.wait()

    return pl.pallas_call(
        body, name=name, out_shape=[_sds((2,) + h.shape, f32) for h in halves], in_specs=[HBM] * nw, out_specs=[HBM] * nw,
        scratch_shapes=[pltpu.VMEM(h.shape, f32) for h in halves]
        + [pltpu.SemaphoreType.DMA((nw,)), pltpu.SemaphoreType.DMA((nw,)), pltpu.SemaphoreType.DMA((nw,)), pltpu.SemaphoreType.DMA((nw,))],
        compiler_params=pltpu.CompilerParams(vmem_limit_bytes=VMEM_LIMIT),
    )(*halves)


def _row_tile(r, c, nbuf):
    budget = 24 * 1024 * 1024 // (2 * nbuf * 4 * c)
    fits = [t for t in range(16, r + 1, 16) if r % t == 0 and t <= budget]
    return max(fits) if fits else r


def _cast_bf16(name, a, dep=None):
    r, c = a.shape
    tr = _row_tile(r, c, 2)
    dep_specs, dep_ops = _dep_args(dep, 1)

    def body(a_ref, *rest):
        rest[-1][...] = a_ref[...].astype(bf16)

    spec = pl.BlockSpec((tr, c), lambda i: (i, 0))
    return pl.pallas_call(body, name=name, grid=(r // tr,), in_specs=[spec] + dep_specs, out_specs=spec, out_shape=_sds((r, c), bf16),
                          compiler_params=_params(("parallel",)))(a, *dep_ops)


def _w_in_columns(win4):
    tr = 256

    def body(a_ref, o_ref, ob_ref):
        for k in range(NSH):
            o_ref[:, IN_SH * k:IN_SH * (k + 1)] = a_ref[k][:, :IN_SH]
        o_ref[:, IN_COLS:] = jnp.zeros((tr, IN_P - IN_COLS), bf16)
        ob_ref[...] = o_ref[:, IN_A:]

    return pl.pallas_call(
        body, name="w_in_columns", grid=(D // tr,), in_specs=[pl.BlockSpec((NSH, tr, IN_SHP), lambda i: (0, i, 0))],
        out_specs=[pl.BlockSpec((tr, IN_P), lambda i: (i, 0)), pl.BlockSpec((tr, IN_B), lambda i: (i, 0))],
        out_shape=[_sds((D, IN_P), bf16), _sds((D, IN_B), bf16)], compiler_params=_params(("parallel",)))(win4)


def _pair_sum(name, core, grads, got):
    _, _, rh, c = grads.shape
    tr = _row_tile(rh, c, 2)

    def body(c_ref, a_ref, b_ref, o_ref):
        o_ref[...] = (a_ref[...].astype(f32) + b_ref[...].astype(f32)).astype(bf16)

    spec = pl.BlockSpec((None, tr, c), lambda k, i, c_ref: (k, i, 0))
    return pl.pallas_call(
        body, name=name, out_shape=_sds((NSH, rh, c), bf16),
        grid_spec=pltpu.PrefetchScalarGridSpec(
            num_scalar_prefetch=1, grid=(NSH, rh // tr),
            in_specs=[pl.BlockSpec((None, None, tr, c), lambda k, i, c_ref: (k, c_ref[0], i, 0)), spec], out_specs=spec),
        compiler_params=_params(("parallel", "parallel")))(core, grads, got)


def _chip_sum(name, chip, sums, lands):
    _, rh, c = sums.shape
    tr = _row_tile(rh, c, 4)

    def body(k_ref, own_ref, l_ref, o_ref):
        own = own_ref[...].astype(f32)
        acc = None
        for j in range(NSH):
            term = jnp.where(k_ref[0] == j, own, l_ref[j].astype(f32))
            acc = term if acc is None else acc + term
        o_ref[...] = acc

    return pl.pallas_call(
        body, name=name, out_shape=_sds((rh, c), f32),
        grid_spec=pltpu.PrefetchScalarGridSpec(
            num_scalar_prefetch=1, grid=(rh // tr,),
            in_specs=[pl.BlockSpec((None, tr, c), lambda i, k_ref: (k_ref[0], i, 0)), pl.BlockSpec((NSH, tr, c), lambda i, k_ref: (0, i, 0))],
            out_specs=pl.BlockSpec((tr, c), lambda i, k_ref: (i, 0))),
        compiler_params=_params(("parallel",)))(chip, sums, lands)


def _mods_part(cond16, w_ada, b_part):
    n = w_ada.shape[1]
    tn = 512

    def body(c_ref, w_ref, b_ref, o_ref):
        cv = c_ref[...]
        o_ref[...] = _dot(cv * _sigmoid(cv), w_ref[...]) + b_ref[...]

    return pl.pallas_call(
        body, name="mods_part", grid=(n // tn,),
        in_specs=[pl.BlockSpec((16, D), lambda j: (0, 0)), pl.BlockSpec((D, tn), lambda j: (0, j)), pl.BlockSpec((1, tn), lambda j: (0, j))],
        out_specs=pl.BlockSpec((16, tn), lambda j: (0, j)), out_shape=_sds((16, n), f32), compiler_params=_params(("parallel",)),
    )(cond16, w_ada, b_part)


def _grad_w_ada(cond16, dm16):
    n = dm16.shape[1]
    tr = 256

    def body(c_ref, d_ref, o_ref):
        cv = c_ref[...]
        o_ref[...] = _dot(cv * _sigmoid(cv), d_ref[...], ta=True)

    return pl.pallas_call(
        body, name="grad_w_ada", grid=(D // tr,),
        in_specs=[pl.BlockSpec((16, tr), lambda i: (0, i)), pl.BlockSpec((16, n), lambda i: (0, 0))],
        out_specs=pl.BlockSpec((tr, n), lambda i: (i, 0)), out_shape=_sds((D, n), f32), compiler_params=_params(("parallel",)),
    )(cond16, dm16)


def _adamw(name, w, g, m, v):
    r, c = w.shape
    tr = _row_tile(r, c, 7)
    spec = pl.BlockSpec((tr, c), lambda i: (i, 0))
    grid = (r // tr,)

    def body(w_ref, g_ref, m_ref, v_ref, d_ref, nm_ref, nv_ref):
        gv = g_ref[...]
        nm = ADAM_B1 * m_ref[...] + (1.0 - ADAM_B1) * gv
        nv = ADAM_B2 * v_ref[...] + (1.0 - ADAM_B2) * (gv * gv)
        nm_ref[...] = nm
        nv_ref[...] = nv
        m_hat = nm / (1.0 - ADAM_B1 ** ADAM_STEP)
        v_hat = nv / (1.0 - ADAM_B2 ** ADAM_STEP)
        d_ref[...] = -ADAM_LR * (m_hat / (jnp.sqrt(v_hat) + ADAM_EPS) + ADAM_WD * w_ref[...])

    return pl.pallas_call(body, name=name, grid=grid, in_specs=[spec] * 4, out_specs=[spec] * 3, out_shape=[_sds(w.shape, f32)] * 3,
                          compiler_params=_params(("parallel",)))(w, g, m, v)


def _pack(parts, rows):
    flat = []
    for p in parts:
        p = p.reshape(-1)
        flat.append(jnp.pad(p, (0, (-p.shape[0]) % 128)))
    v = jnp.concatenate(flat)
    return jnp.pad(v, (0, rows * 128 - v.shape[0])).reshape(rows, 128)


def _unpack(packed, sizes):
    lead = packed.shape[:-2]
    flat = packed.reshape(lead + (-1,))
    out, off = [], 0
    for n in sizes:
        out.append(flat[..., off:off + n])
        off += n + (-n) % 128
    return out


BIG = ("w_in", "w_out", "w_gate", "w_up", "w_down")
SMALL = ("b_ada", "g_mix", "conv_b", "dt_bias", "a_log", "d_skip", "g_att_out", "g_ssd_out", "g_ffn", "g_final", "rel_bias", "conv_w")
ORDER = ("w_ada", "b_ada", "g_mix", "w_in", "rel_bias", "conv_w", "conv_b", "dt_bias", "a_log", "d_skip", "g_att_out", "g_ssd_out",
         "w_out", "g_ffn", "w_gate", "w_up", "w_down", "g_final")
REL_SH = N_REL // NSH
CONVW_SH = XBC // NSH
ADA_SH = 6 * D // NSH


class _Exchange:
    def __init__(self, core, chip):
        self.core, self.chip = core, chip
        self.gathered = {}
        self.pending = []

    def gather(self, names, shards):
        ssem, rsem, thru, lands, token = _split_start("gather_start_" + "_".join(names), _gather_copies, shards,
                                                      [(NSH,) + s.shape for s in shards])
        self.gathered.update({n: (ssem[i], rsem[i], thru[i], lands[i]) for i, n in enumerate(names)})
        return token

    def _whole(self, names, after):
        ssem, rsem, thru, lands = zip(*[self.gathered[n] for n in names])
        tag = "_".join(names)
        thru, lands = _split_wait("gather_wait_" + tag, _gather_copies, ssem, rsem, thru, lands, after)
        return _gather_forward("gather_forward_" + tag, thru, lands)

    def w_in(self, after):
        (win4,) = self._whole(("w_in",), after)
        return _w_in_columns(win4.reshape(NSH, D, IN_SHP))

    def w_out(self, after):
        wout4, wg4, wu4, wd4 = self._whole(("w_out", "w_gate", "w_up", "w_down"), after)
        self.ffn_weights = wg4.reshape(NSH, D, FSH), wu4.reshape(NSH, D, FSH), wd4.reshape(NSH, FSH, D)
        return wout4.reshape(D, D)

    def ffn(self, after):
        return self.ffn_weights

    def grad(self, names, grads):
        tag = "_".join(names)
        stacked = [g.reshape(NSH, 2, g.shape[1] // 2, g.shape[2]) for g in grads]
        got = _rs_pair_exchange("rs_pair_exchange_" + tag, stacked)
        sums = [_pair_sum("pair_sum_" + n, self.core, o, g) for n, o, g in zip(names, stacked, got)]
        self.pending.append((names, _split_start("rs_start_" + tag, _reduce_copies, sums, [s.shape for s in sums])))
        return self.pending[-1][1][4]

    def finish(self, after):
        grads = {}
        for names, (ssem, rsem, sums, lands, _) in self.pending:
            tag = "_".join(names)
            sums, lands = _split_wait("rs_wait_" + tag, _reduce_copies, ssem, rsem, sums, lands, after)
            halves = [_chip_sum("chip_sum_" + n, self.chip, sm, ld) for n, sm, ld in zip(names, sums, lands)]
            for n, f in zip(names, _rs_pair_gather("rs_pair_gather_" + tag, halves)):
                grads[n] = f.reshape(2 * f.shape[1], f.shape[2])
        return grads


def kernel(x, c, w_ada, b_ada, g_mix, w_in, rel_bias, conv_w, conv_b, dt_bias, a_log, d_skip, g_att_out, g_ssd_out, w_out, g_ffn, w_gate, w_up, w_down, g_final, loss_target, m_w_ada, m_b_ada, m_g_mix, m_w_in, m_rel_bias, m_conv_w, m_conv_b, m_dt_bias, m_a_log, m_d_skip, m_g_att_out, m_g_ssd_out, m_w_out, m_g_ffn, m_w_gate, m_w_up, m_w_down, m_g_final, v_w_ada, v_b_ada, v_g_mix, v_w_in, v_rel_bias, v_conv_w, v_conv_b, v_dt_bias, v_a_log, v_d_skip, v_g_att_out, v_g_ssd_out, v_w_out, v_g_ffn, v_w_gate, v_w_up, v_w_down, v_g_final):
    args = dict(locals())
    w = {n: args[n] for n in ORDER}
    m = {n: args["m_" + n] for n in ORDER}
    v = {n: args["v_" + n] for n in ORDER}
    ix, iy, ic = lax.axis_index("x"), lax.axis_index("y"), lax.axis_index("c")
    chip = 2 * ix + iy
    dev = 2 * chip + ic
    s = x.shape[1]

    g1 = _allgather8("gather_inputs", _pack([c[0], rel_bias[0], conv_w[0]], 40))
    c_all, rel_sh, convw_sh = _unpack(g1, [D, NH * REL_SH, 4 * CONVW_SH])
    rel_full = jnp.concatenate([rel_sh[2 * k].reshape(NH, REL_SH) for k in range(NSH)], axis=1)
    convw_full = jnp.concatenate([convw_sh[2 * k].reshape(4, CONVW_SH) for k in range(NSH)], axis=1)
    cond16 = jnp.pad(c_all, ((0, 8), (0, 0)))
    b_part = lax.dynamic_slice_in_dim(b_ada, chip * ADA_SH, ADA_SH, axis=1)
    mods_part = _mods_part(cond16, w_ada[0], b_part)[:N_DEV]
    g2 = _allgather8("gather_mods", mods_part.reshape(N_DEV * ADA_SH // 128, 128))
    mods_all = jnp.concatenate([g2[2 * k].reshape(N_DEV, ADA_SH) for k in range(NSH)], axis=1)
    mods = lax.dynamic_slice_in_dim(mods_all, dev, 1, axis=0)

    exchange = _Exchange(jnp.reshape(ic, (1,)).astype(jnp.int32), jnp.reshape(chip, (1,)).astype(jnp.int32))
    shard_in = _cast_bf16("cast_w_in", jnp.pad(w_in[0], ((0, 0), (0, IN_SHP - IN_SH))), dep=g2[0, :8]).reshape(2, D // 2, IN_SHP)
    tok = exchange.gather(("w_in",), [shard_in])
    tok = exchange.gather(("w_out", "w_gate", "w_up", "w_down"), [
        _cast_bf16("cast_w_out", w_out[0], dep=tok).reshape(2, D // NSH // 2, D),
        _cast_bf16("cast_w_gate", w_gate[0], dep=tok).reshape(2, D // 2, FSH),
        _cast_bf16("cast_w_up", w_up[0], dep=tok).reshape(2, D // 2, FSH),
        _cast_bf16("cast_w_down", w_down[0], dep=tok).reshape(2, FSH // 2, D)])
    mods = mods + tok[:1, :1]

    loss, grad_x, dmods, small = _local_step(
        x[0], loss_target[0], mods, g_mix, rel_full, convw_full, conv_b, dt_bias, a_log, d_skip, g_att_out, g_ssd_out, g_ffn,
        g_final[None, :], exchange)

    small_names = ("g_mix", "conv_b", "dt_bias", "a_log", "d_skip", "g_att_out", "g_ssd_out", "g_ffn", "g_final", "rel_bias", "conv_w")
    g3 = _allgather8("gather_small_grads", _pack([dmods] + [small[n] for n in small_names], 264))
    sizes = [6 * D] + [int(np.prod(small[n].shape)) for n in small_names]
    dmods_all = _unpack(g3, sizes)[0]
    summed = _unpack(_sum8(g3), sizes)
    grads = {"b_ada": summed[0].reshape(1, 6 * D)}
    for n, val in zip(small_names, summed[1:]):
        grads[n] = val.reshape(small[n].shape)
    grads["rel_bias"] = lax.dynamic_slice_in_dim(grads["rel_bias"], chip * REL_SH, REL_SH, axis=1)
    grads["conv_w"] = lax.dynamic_slice_in_dim(grads["conv_w"], chip * CONVW_SH, CONVW_SH, axis=1)
    grads["g_final"] = grads["g_final"].reshape(D)
    dm16 = jnp.pad(lax.dynamic_slice_in_dim(dmods_all, chip * ADA_SH, ADA_SH, axis=1), ((0, 8), (0, 0)))
    grads["w_ada"] = _grad_w_ada(cond16, dm16)

    delta, new_m, new_v = {}, {}, {}
    delta["w_ada"], new_m["w_ada"], new_v["w_ada"] = _adamw("adamw_w_ada", w_ada[0], grads["w_ada"], m_w_ada[0], v_w_ada[0])
    grads.update(exchange.finish(grad_x))
    grads["w_in"] = grads["w_in"][:, :IN_SH]
    for n in BIG:
        delta[n], new_m[n], new_v[n] = _adamw("adamw_" + n, w[n][0], grads[n], m[n][0], v[n][0])
    sw = _pack([w[n] for n in SMALL], 200)
    sg = _pack([grads[n] for n in SMALL], 200)
    sm = _pack([m[n] for n in SMALL], 200)
    sv = _pack([v[n] for n in SMALL], 200)
    ssz = [int(np.prod(w[n].shape)) for n in SMALL]
    for dst, packed in zip((delta, new_m, new_v), _adamw("adamw_small", sw, sg, sm, sv)):
        for n, val in zip(SMALL, _unpack(packed, ssz)):
            dst[n] = val

    def shaped(d, n):
        return d[n].reshape(w[n].shape)

    total = lax.psum(loss, ("x", "y", "c"))
    return (total, grad_x[None], *[shaped(grads, n) for n in ORDER], *[shaped(delta, n) for n in ORDER],
            *[shaped(new_m, n) for n in ORDER], *[shaped(new_v, n) for n in ORDER])
```

```python
import functools

import numpy as np
import jax
import jax.numpy as jnp
from jax import lax
from jax.experimental import pallas as pl
from jax.experimental.pallas import tpu as pltpu

f32 = jnp.float32
bf16 = jnp.bfloat16
HIGHEST = lax.Precision.HIGHEST
MESH = pl.DeviceIdType.MESH

D = 2048
CHUNK = 64
LEFT = 8
BAND = (LEFT + 1) * CHUNK
BANDP = 640
PADK = LEFT * CHUNK
NH = 16
HD = 64
ATT_W = NH * HD
SSD_W = 1024
NG = 2
NSTATE = 128
GW = SSD_W // NG
XBC = SSD_W + 2 * NG * NSTATE
N_REL = 320
REL_CLIP = 256
FFN = 5632
NSH = 4
FSH = FFN // NSH
IN_COLS = 5648
IN_SH = IN_COLS // NSH
IN_SHP = 1536
IN_A = 3 * ATT_W
IN_B = 2688
IN_P = IN_A + IN_B
EPS = 1e-6
N_DEV = 8

ADAM_LR = 0.001
ADAM_B1 = 0.9
ADAM_B2 = 0.999
ADAM_EPS = 1e-08
ADAM_WD = 0.01
ADAM_STEP = 10

VMEM_LIMIT = 56 * 1024 * 1024


def _params(sem):
    return pltpu.CompilerParams(dimension_semantics=sem, vmem_limit_bytes=VMEM_LIMIT)


def _sds(shape, dtype):
    return jax.ShapeDtypeStruct(shape, dtype)


def _fold8(v):
    r, w = v.shape
    return jnp.sum(v.reshape(r // 8, 8, w), axis=0)


STRIP = 16


def _strips(tm, fn):
    def step(j, carry):
        fn(pl.ds(pl.multiple_of(j * STRIP, STRIP), STRIP))
        return carry
    lax.fori_loop(0, tm // STRIP, step, 0, unroll=4)


def _sigmoid(v):
    return 1.0 / (1.0 + jnp.exp(-v))


def _softplus(v):
    return jnp.maximum(v, 0.0) + jnp.log(1.0 + jnp.exp(-jnp.abs(v)))


def _dot(a, b, ta=False, tb=False):
    dn = (((0 if ta else 1,), (1 if tb else 0,)), ((), ()))
    return lax.dot_general(a.astype(bf16), b.astype(bf16), dn, preferred_element_type=f32)


def _dep_args(dep, ngrid):
    if dep is None:
        return [], []
    return [pl.BlockSpec((8, 128), lambda *_: (0, 0))], [dep]


def _dot01(a, b, ta=False, tb=False, exact="b"):
    dn = (((0 if ta else 1,), (1 if tb else 0,)), ((), ()))
    x = a if exact == "b" else b
    hi = x.astype(bf16)
    r = x - hi.astype(f32)
    mid = r.astype(bf16)
    lo = (r - mid.astype(f32)).astype(bf16)
    if exact == "b":
        m = b.astype(bf16)
        return sum(lax.dot_general(p, m, dn, preferred_element_type=f32) for p in (hi, mid, lo))
    m = a.astype(bf16)
    return sum(lax.dot_general(m, p, dn, preferred_element_type=f32) for p in (hi, mid, lo))


def _matmul(name, a, b, *, grid, a_spec, b_spec, o_spec, o_shape, o_dtype, acc_shape, ta=False, tb=False, dep=None):
    nk = grid[2]
    dep_specs, dep_ops = _dep_args(dep, 3)

    def body(a_ref, b_ref, *rest):
        o_ref, acc_ref = rest[-2:]
        p = _dot(a_ref[...], b_ref[...], ta, tb)
        if nk == 1:
            o_ref[...] = p.astype(o_ref.dtype)
        else:
            k = pl.program_id(2)

            @pl.when(k == 0)
            def _():
                acc_ref[...] = p

            @pl.when(jnp.logical_and(k > 0, k < nk - 1))
            def _():
                acc_ref[...] += p

            @pl.when(k == nk - 1)
            def _():
                o_ref[...] = (acc_ref[...] + p).astype(o_ref.dtype)

    return pl.pallas_call(
        body, name=name, grid=grid, in_specs=[a_spec, b_spec] + dep_specs, out_specs=o_spec,
        out_shape=_sds(o_shape, o_dtype), scratch_shapes=[pltpu.VMEM(acc_shape if nk > 1 else (8, 128), f32)],
        compiler_params=_params(("parallel", "parallel", "arbitrary")),
    )(a, b, *dep_ops)


def _mm_nn_fullk(name, a, b, tm, tn, o_dtype, n=None):
    m, k = a.shape
    n = b.shape[1] if n is None else n
    return _matmul(name, a, b, grid=(m // tm, n // tn, 1),
                   a_spec=pl.BlockSpec((tm, k), lambda i, j, kk: (i, 0)),
                   b_spec=pl.BlockSpec((k, tn), lambda i, j, kk: (0, j)),
                   o_spec=pl.BlockSpec((tm, tn), lambda i, j, kk: (i, j)),
                   o_shape=(m, n), o_dtype=o_dtype, acc_shape=(tm, tn))


def _mm_nt(name, a, b, tm, tn, tk, o_dtype, dep=None):
    m, k = a.shape
    n = b.shape[0]
    return _matmul(name, a, b, grid=(m // tm, n // tn, k // tk), tb=True, dep=dep,
                   a_spec=pl.BlockSpec((tm, tk), lambda i, j, kk: (i, kk)),
                   b_spec=pl.BlockSpec((tn, tk), lambda i, j, kk: (j, kk)),
                   o_spec=pl.BlockSpec((tm, tn), lambda i, j, kk: (i, j)),
                   o_shape=(m, n), o_dtype=o_dtype, acc_shape=(tm, tn))


def _mm_tn(name, a, b, tm, tn, tk, o_dtype):
    k, m = a.shape
    n = b.shape[1]
    return _matmul(name, a, b, grid=(m // tm, n // tn, k // tk), ta=True,
                   a_spec=pl.BlockSpec((tk, tm), lambda i, j, kk: (kk, i)),
                   b_spec=pl.BlockSpec((tk, tn), lambda i, j, kk: (kk, j)),
                   o_spec=pl.BlockSpec((tm, tn), lambda i, j, kk: (i, j)),
                   o_shape=(m, n), o_dtype=o_dtype, acc_shape=(tm, tn))


FSH_PARTS = (slice(0, 640), slice(640, FSH))


def _ffn_up(h2b, wg4, wu4, tm):
    s = h2b.shape[0]

    def body(h_ref, wg_ref, wu_ref, a_ref, s_ref, ud_ref, at_ref):
        h = h_ref[...]
        for cols in FSH_PARTS:
            g = _dot(h, wg_ref[:, cols])
            u = _dot(h, wu_ref[:, cols])
            sg = _sigmoid(g)
            sil = g * sg
            a = sil * u
            a_ref[:, cols] = a.astype(bf16)
            at_ref[cols, :] = a.T.astype(bf16)
            s_ref[:, cols] = sil.astype(bf16)
            ud_ref[:, cols] = (u * (sg * (1.0 + g * (1.0 - sg)))).astype(bf16)

    wspec = pl.BlockSpec((None, D, FSH), lambda k, i: (k, 0, 0))
    ospec = pl.BlockSpec((tm, FSH), lambda k, i: (i, k))
    return pl.pallas_call(
        body, name="ffn_up", grid=(NSH, s // tm),
        in_specs=[pl.BlockSpec((tm, D), lambda k, i: (i, 0)), wspec, wspec],
        out_specs=[ospec, ospec, ospec, pl.BlockSpec((None, FSH, tm), lambda k, i: (k, 0, i))],
        out_shape=[_sds((s, FFN), bf16)] * 3 + [_sds((NSH, FSH, s), bf16)],
        compiler_params=_params(("parallel", "parallel")),
    )(h2b, wg4, wu4)


def _ffn_down(act, wd4, tm):
    s = act.shape[0]
    tn = D // 2

    def body(a_ref, b_ref, o_ref):
        o_ref[...] = jnp.dot(a_ref[...], b_ref[...].reshape(FFN, tn), preferred_element_type=f32)

    return pl.pallas_call(
        body, name="ffn_down", grid=(s // tm, D // tn),
        in_specs=[pl.BlockSpec((tm, FFN), lambda i, j: (i, 0)), pl.BlockSpec((NSH, FSH, tn), lambda i, j: (0, 0, j))],
        out_specs=pl.BlockSpec((tm, tn), lambda i, j: (i, j)), out_shape=_sds((s, D), f32),
        compiler_params=_params(("parallel", "parallel")),
    )(act, wd4)


def _ffn_dact(dffn, wd4, sil, ud, tm, dep=None):
    s = dffn.shape[0]
    dep_specs, dep_ops = _dep_args(dep, 2)

    def body(d_ref, w_ref, s_ref, ud_ref, *rest):
        dg_ref, du_ref = rest[-2:]
        d = d_ref[...]
        for cols in FSH_PARTS:
            dact = _dot(d, w_ref[cols, :], tb=True)
            dg_ref[:, cols] = (dact * ud_ref[:, cols].astype(f32)).astype(bf16)
            du_ref[:, cols] = (dact * s_ref[:, cols].astype(f32)).astype(bf16)

    blk = pl.BlockSpec((tm, FSH), lambda k, i: (i, k))
    return pl.pallas_call(
        body, name="ffn_dact", grid=(NSH, s // tm),
        in_specs=[pl.BlockSpec((tm, D), lambda k, i: (i, 0)), pl.BlockSpec((None, FSH, D), lambda k, i: (k, 0, 0)), blk, blk] + dep_specs,
        out_specs=[blk, blk], out_shape=[_sds((s, FFN), bf16), _sds((s, FFN), bf16)],
        compiler_params=_params(("parallel", "parallel")),
    )(dffn, wd4, sil, ud, *dep_ops)


def _ffn_dh(dgate, dup, wg4, wu4, tm, dep=None):
    s = dgate.shape[0]
    dep_specs, dep_ops = _dep_args(dep, 2)

    def body(dg_ref, du_ref, wg_ref, wu_ref, *rest):
        o_ref, acc_ref = rest[-2:]
        k = pl.program_id(1)
        p = _dot(dg_ref[...], wg_ref[...], tb=True) + _dot(du_ref[...], wu_ref[...], tb=True)

        @pl.when(k == 0)
        def _():
            acc_ref[...] = p

        @pl.when(jnp.logical_and(k > 0, k < NSH - 1))
        def _():
            acc_ref[...] += p

        @pl.when(k == NSH - 1)
        def _():
            o_ref[...] = acc_ref[...] + p

    aspec = pl.BlockSpec((tm, FSH), lambda i, k: (i, k))
    wspec = pl.BlockSpec((None, D, FSH), lambda i, k: (k, 0, 0))
    return pl.pallas_call(
        body, name="ffn_dh", grid=(s // tm, NSH), in_specs=[aspec, aspec, wspec, wspec] + dep_specs,
        out_specs=pl.BlockSpec((tm, D), lambda i, k: (i, 0)), out_shape=_sds((s, D), f32),
        scratch_shapes=[pltpu.VMEM((tm, D), f32)], compiler_params=_params(("parallel", "arbitrary")),
    )(dgate, dup, wg4, wu4, *dep_ops)


def _grad_cols4(name, h, dy, tm, tk):
    s = h.shape[0]
    return _matmul(name, h, dy, grid=(NSH, D // tm, s // tk), ta=True,
                   a_spec=pl.BlockSpec((tk, tm), lambda k, i, kk: (kk, i)),
                   b_spec=pl.BlockSpec((tk, FSH), lambda k, i, kk: (kk, k)),
                   o_spec=pl.BlockSpec((None, tm, FSH), lambda k, i, kk: (k, i, 0)),
                   o_shape=(NSH, D, FSH), o_dtype=bf16, acc_shape=(tm, FSH))


def _grad_wdown4(act_t, dffn, tn, tk):
    s = act_t.shape[2]
    return _matmul("grad_w_down", act_t, dffn, grid=(NSH, D // tn, s // tk),
                   a_spec=pl.BlockSpec((None, FSH, tk), lambda k, j, kk: (k, 0, kk)),
                   b_spec=pl.BlockSpec((tk, tn), lambda k, j, kk: (kk, j)),
                   o_spec=pl.BlockSpec((None, FSH, tn), lambda k, j, kk: (k, 0, j)),
                   o_shape=(NSH, FSH, D), o_dtype=bf16, acc_shape=(FSH, tn))


def _row_spec(w):
    return pl.BlockSpec((1, w), lambda i: (0, 0))


def _tile_spec(tm, w, col=0):
    return pl.BlockSpec((tm, w), lambda i: (i, col))


def _norm_mod(name, x, g, sc, sh, tm):
    s = x.shape[0]

    def body(x_ref, g_ref, sc_ref, sh_ref, o_ref):
        def strip(rows):
            xv = x_ref[rows, :]
            r = lax.rsqrt(jnp.mean(xv * xv, axis=-1, keepdims=True) + EPS)
            o_ref[rows, :] = (xv * r * g_ref[...] * (1.0 + sc_ref[...]) + sh_ref[...]).astype(bf16)

        _strips(tm, strip)

    return pl.pallas_call(
        body, name=name, grid=(s // tm,), in_specs=[_tile_spec(tm, D), _row_spec(D), _row_spec(D), _row_spec(D)],
        out_specs=_tile_spec(tm, D), out_shape=_sds((s, D), bf16), compiler_params=_params(("parallel",)),
    )(x, g, sc, sh)


def _resid_norm_mod(x, gt, mix, g, sc, sh, tm):
    s = x.shape[0]

    def body(x_ref, gt_ref, m_ref, g_ref, sc_ref, sh_ref, x2_ref, h_ref):
        def strip(rows):
            xv = x_ref[rows, :] + gt_ref[...] * m_ref[rows, :]
            x2_ref[rows, :] = xv
            r = lax.rsqrt(jnp.mean(xv * xv, axis=-1, keepdims=True) + EPS)
            h_ref[rows, :] = (xv * r * g_ref[...] * (1.0 + sc_ref[...]) + sh_ref[...]).astype(bf16)

        _strips(tm, strip)

    return pl.pallas_call(
        body, name="resid_norm_mod", grid=(s // tm,),
        in_specs=[_tile_spec(tm, D), _row_spec(D), _tile_spec(tm, D), _row_spec(D), _row_spec(D), _row_spec(D)],
        out_specs=[_tile_spec(tm, D), _tile_spec(tm, D)], out_shape=[_sds((s, D), f32), _sds((s, D), bf16)],
        compiler_params=_params(("parallel",)),
    )(x, gt, mix, g, sc, sh)


def _final_fwd_bwd(x2, ffn, gt2, g, tgt, tm):
    s = x2.shape[0]
    n = s // tm

    def body(x_ref, f_ref, gt_ref, g_ref, t_ref, dx_ref, df_ref, loss_ref, dg_ref, dgt_ref, a_loss, a_dg, a_dgt):
        i = pl.program_id(0)

        @pl.when(i == 0)
        def _():
            a_loss[...] = jnp.zeros_like(a_loss)
            a_dg[...] = jnp.zeros_like(a_dg)
            a_dgt[...] = jnp.zeros_like(a_dgt)

        def strip(rows):
            fv = f_ref[rows, :]
            gt = gt_ref[...]
            gv = g_ref[...]
            xv = x_ref[rows, :] + gt * fv
            r = lax.rsqrt(jnp.mean(xv * xv, axis=-1, keepdims=True) + EPS)
            xh = xv * r
            e = xh * gv - t_ref[rows, :]
            a_loss[...] += _fold8(e * e)
            dy = e * (1.0 / D)
            a_dg[...] += _fold8(dy * xh)
            t = dy * gv
            dx = r * (t - xh * jnp.mean(t * xh, axis=-1, keepdims=True))
            dx_ref[rows, :] = dx
            a_dgt[...] += _fold8(dx * fv)
            df_ref[rows, :] = (dx * gt).astype(bf16)

        _strips(tm, strip)

        @pl.when(i == n - 1)
        def _():
            tot = jnp.sum(jnp.sum(a_loss[...], axis=0, keepdims=True), axis=1, keepdims=True) * (0.5 / D)
            loss_ref[...] = jnp.broadcast_to(tot, (1, 128))
            dg_ref[...] = jnp.sum(a_dg[...], axis=0, keepdims=True)
            dgt_ref[...] = jnp.sum(a_dgt[...], axis=0, keepdims=True)

    return pl.pallas_call(
        body, name="final_fwd_bwd", grid=(n,),
        in_specs=[_tile_spec(tm, D), _tile_spec(tm, D), _row_spec(D), _row_spec(D), _tile_spec(tm, D)],
        out_specs=[_tile_spec(tm, D), _tile_spec(tm, D), _row_spec(128), _row_spec(D), _row_spec(D)],
        out_shape=[_sds((s, D), f32), _sds((s, D), bf16), _sds((1, 128), f32), _sds((1, D), f32), _sds((1, D), f32)],
        scratch_shapes=[pltpu.VMEM((8, D), f32)] * 3, compiler_params=_params(("arbitrary",)),
    )(x2, ffn, gt2, g, tgt)


def _norm_mod_bwd(name, dh, xin, g, sc, dres, tm, mix=None, gt=None):
    s = dh.shape[0]
    n = s // tm
    with_mix = mix is not None

    def body(*refs):
        if with_mix:
            dh_ref, x_ref, g_ref, sc_ref, dr_ref, m_ref, gt_ref, dx_ref, dm_ref, dsc_ref, dsh_ref, dg_ref, dgt_ref, a_sc, a_sh, a_g, a_gt = refs
        else:
            dh_ref, x_ref, g_ref, sc_ref, dr_ref, dx_ref, dsc_ref, dsh_ref, dg_ref, a_sc, a_sh, a_g = refs
        i = pl.program_id(0)

        @pl.when(i == 0)
        def _():
            a_sc[...] = jnp.zeros_like(a_sc)
            a_sh[...] = jnp.zeros_like(a_sh)
            a_g[...] = jnp.zeros_like(a_g)
            if with_mix:
                a_gt[...] = jnp.zeros_like(a_gt)

        def strip(rows):
            dh = dh_ref[rows, :]
            xv = x_ref[rows, :]
            gv = g_ref[...]
            r = lax.rsqrt(jnp.mean(xv * xv, axis=-1, keepdims=True) + EPS)
            xh = xv * r
            a_sc[...] += _fold8(dh * xh * gv)
            a_sh[...] += _fold8(dh)
            dn = dh * (1.0 + sc_ref[...])
            a_g[...] += _fold8(dn * xh)
            t = dn * gv
            dx = dr_ref[rows, :] + r * (t - xh * jnp.mean(t * xh, axis=-1, keepdims=True))
            dx_ref[rows, :] = dx
            if with_mix:
                a_gt[...] += _fold8(dx * m_ref[rows, :])
                dm_ref[rows, :] = (dx * gt_ref[...]).astype(bf16)

        _strips(tm, strip)

        @pl.when(i == n - 1)
        def _():
            dsc_ref[...] = jnp.sum(a_sc[...], axis=0, keepdims=True)
            dsh_ref[...] = jnp.sum(a_sh[...], axis=0, keepdims=True)
            dg_ref[...] = jnp.sum(a_g[...], axis=0, keepdims=True)
            if with_mix:
                dgt_ref[...] = jnp.sum(a_gt[...], axis=0, keepdims=True)

    tile, row = _tile_spec(tm, D), _row_spec(D)
    if with_mix:
        ins, args = [tile, tile, row, row, tile, tile, row], (dh, xin, g, sc, dres, mix, gt)
        outs = [tile, tile, row, row, row, row]
        shapes = [_sds((s, D), f32), _sds((s, D), bf16)] + [_sds((1, D), f32)] * 4
        nacc = 4
    else:
        ins, args = [tile, tile, row, row, tile], (dh, xin, g, sc, dres)
        outs = [tile, row, row, row]
        shapes = [_sds((s, D), f32)] + [_sds((1, D), f32)] * 3
        nacc = 3
    return pl.pallas_call(
        body, name=name, grid=(n,), in_specs=ins, out_specs=outs, out_shape=shapes,
        scratch_shapes=[pltpu.VMEM((8, D), f32)] * nacc, compiler_params=_params(("arbitrary",)),
    )(*args)


def _mix_pre(att, y, proj2, g_att, g_ssd, tm):
    s = att.shape[0]

    def body(a_ref, y_ref, z_ref, ga_ref, gs_ref, o_ref):
        def strip(rows):
            a = a_ref[rows, :]
            ra = lax.rsqrt(jnp.mean(a * a, axis=-1, keepdims=True) + EPS)
            o_ref[rows, 0:ATT_W] = (a * ra * ga_ref[...]).astype(bf16)
            z = z_ref[rows, :]
            u = y_ref[rows, :] * (z * _sigmoid(z))
            ru = lax.rsqrt(jnp.mean(u * u, axis=-1, keepdims=True) + EPS)
            o_ref[rows, ATT_W:] = (u * ru * gs_ref[...]).astype(bf16)

        _strips(tm, strip)

    t = _tile_spec(tm, ATT_W)
    return pl.pallas_call(
        body, name="mix_pre", grid=(s // tm,), in_specs=[t, t, t, _row_spec(ATT_W), _row_spec(SSD_W)],
        out_specs=_tile_spec(tm, D), out_shape=_sds((s, D), bf16), compiler_params=_params(("parallel",)),
    )(att, y, proj2, g_att, g_ssd)


def _mix_pre_bwd(dmc, att, y, proj2, g_att, g_ssd, tm):
    s = att.shape[0]
    n = s // tm

    def body(da_ref, ds_ref, a_ref, y_ref, z_ref, ga_ref, gs_ref, datt_ref, dy_ref, dz_ref, dga_ref, dgs_ref, acc_a, acc_s):
        i = pl.program_id(0)

        @pl.when(i == 0)
        def _():
            acc_a[...] = jnp.zeros_like(acc_a)
            acc_s[...] = jnp.zeros_like(acc_s)

        def strip(rows):
            a = a_ref[rows, :]
            ra = lax.rsqrt(jnp.mean(a * a, axis=-1, keepdims=True) + EPS)
            ah = a * ra
            dan = da_ref[rows, :]
            acc_a[...] += _fold8(dan * ah)
            t = dan * ga_ref[...]
            datt_ref[rows, :] = (ra * (t - ah * jnp.mean(t * ah, axis=-1, keepdims=True))).astype(bf16)
            z = z_ref[rows, :]
            yv = y_ref[rows, :]
            sz = _sigmoid(z)
            sil = z * sz
            u = yv * sil
            ru = lax.rsqrt(jnp.mean(u * u, axis=-1, keepdims=True) + EPS)
            uh = u * ru
            dsn = ds_ref[rows, :]
            acc_s[...] += _fold8(dsn * uh)
            t2 = dsn * gs_ref[...]
            du = ru * (t2 - uh * jnp.mean(t2 * uh, axis=-1, keepdims=True))
            dy_ref[rows, :] = du * sil
            dz_ref[rows, :] = (du * yv * (sz * (1.0 + z * (1.0 - sz)))).astype(bf16)

        _strips(tm, strip)

        @pl.when(i == n - 1)
        def _():
            dga_ref[...] = jnp.sum(acc_a[...], axis=0, keepdims=True)
            dgs_ref[...] = jnp.sum(acc_s[...], axis=0, keepdims=True)

    t = _tile_spec(tm, ATT_W)
    row = _row_spec(ATT_W)
    return pl.pallas_call(
        body, name="mix_pre_bwd", grid=(n,),
        in_specs=[_tile_spec(tm, ATT_W, 0), _tile_spec(tm, ATT_W, 1), t, t, t, row, row],
        out_specs=[t, t, t, row, row],
        out_shape=[_sds((s, ATT_W), bf16), _sds((s, SSD_W), f32), _sds((s, SSD_W), bf16), _sds((1, ATT_W), f32), _sds((1, SSD_W), f32)],
        scratch_shapes=[pltpu.VMEM((8, ATT_W), f32)] * 2, compiler_params=_params(("arbitrary",)),
    )(dmc, dmc, att, y, proj2, g_att, g_ssd)


ATT_GROUP = 8
ATT_GROUP_FWD = 16


def _pair_rows(qc):
    two = jnp.concatenate([qc, qc], axis=0)
    r = lax.broadcasted_iota(jnp.int32, (2 * CHUNK, 128), 0)
    l = lax.broadcasted_iota(jnp.int32, (2 * CHUNK, 128), 1)
    return jnp.where((r < CHUNK) == (l < HD), two, jnp.zeros_like(two))


def _scaled(q):
    return q * jnp.asarray(HD ** -0.5, q.dtype)


def _pair_scores(wt, kb, bias, r0, masked):
    sc = lax.dot_general(wt, kb, (((1,), (1,)), ((), ())), preferred_element_type=f32) + bias
    if not masked:
        return sc
    kidx = lax.broadcasted_iota(jnp.int32, sc.shape, 1)
    return jnp.where(r0 + kidx >= PADK, sc, -jnp.inf)


def _softmax(sc, axis):
    e = jnp.exp(sc - jnp.max(sc, axis=axis, keepdims=True))
    return e * (1.0 / jnp.sum(e, axis=axis, keepdims=True))


def _chunk_loops(nc, group, per_trip):
    n_masked = min(-(-LEFT // per_trip), nc // per_trip)

    def run(masked):
        def step(g, carry):
            group(g, masked)
            return carry
        return step

    lax.fori_loop(0, n_masked, run(True), 0)
    lax.fori_loop(n_masked, nc // per_trip, run(False), 0)


def _pair_diag(r):
    lane = lax.broadcasted_iota(jnp.int32, (CHUNK, 128), 1)
    return jnp.where(lane < HD, r[0:CHUNK], r[CHUNK:])


def _pad_keys(k_ref, kp, s):
    kp[0:PADK, :] = jnp.zeros((PADK, 128), bf16)
    kp[PADK:PADK + s, :] = k_ref[...]
    kp[PADK + s:, :] = jnp.zeros((CHUNK, 128), bf16)


def _attn_fwd(qkv, bias2):
    s = qkv.shape[0]
    nc = s // CHUNK
    npair = NH // 2
    per_trip = min(ATT_GROUP_FWD, nc)

    def body(q_ref, k_ref, v_ref, b_ref, o_ref, kp, vp):
        _pad_keys(k_ref, kp, s)
        _pad_keys(v_ref, vp, s)

        def group(g, masked):
            r0s = [pl.multiple_of((g * per_trip + u) * CHUNK, CHUNK) for u in range(per_trip)]
            scs = [_pair_scores(_pair_rows(_scaled(q_ref[pl.ds(r0, CHUNK), :])), kp[pl.ds(r0, BANDP), :], b_ref[...], r0, masked)
                   for r0 in r0s]
            ps = [_softmax(sc, -1).astype(bf16) for sc in scs]
            for r0, p in zip(r0s, ps):
                o_ref[pl.ds(r0, CHUNK), :] = _pair_diag(jnp.dot(p, vp[pl.ds(r0, BANDP), :], preferred_element_type=f32))

        _chunk_loops(nc, group, per_trip)

    return pl.pallas_call(
        body, name="attn_fwd", grid=(npair,),
        in_specs=[pl.BlockSpec((s, 128), lambda p: (0, p)), pl.BlockSpec((s, 128), lambda p: (0, npair + p)),
                  pl.BlockSpec((s, 128), lambda p: (0, 2 * npair + p)), pl.BlockSpec((None, 2 * CHUNK, BANDP), lambda p: (p, 0, 0))],
        out_specs=pl.BlockSpec((s, 128), lambda p: (0, p)), out_shape=_sds((s, ATT_W), f32),
        scratch_shapes=[pltpu.VMEM((PADK + s + CHUNK, 128), bf16)] * 2, compiler_params=_params(("parallel",)),
    )(qkv, qkv, qkv, bias2)


def _attn_bwd(qkv, datt, bias2):
    s = qkv.shape[0]
    nc = s // CHUNK
    npair = NH // 2
    rows = PADK + s + CHUNK
    nt = (((1,), (1,)), ((), ()))

    def body(q_ref, k_ref, v_ref, do_ref, b_ref, dq_ref, dk_ref, dv_ref, g_ref, kp, vp, dkp, dvp):
        _pad_keys(k_ref, kp, s)
        _pad_keys(v_ref, vp, s)
        dkp[...] = jnp.zeros_like(dkp)
        dvp[...] = jnp.zeros_like(dvp)
        g_ref[...] = jnp.zeros_like(g_ref)

        def group(g, masked):
            r0s = [pl.multiple_of((g * ATT_GROUP + u) * CHUNK, CHUNK) for u in range(ATT_GROUP)]
            wts = [_pair_rows(_scaled(q_ref[pl.ds(r0, CHUNK), :])) for r0 in r0s]
            dos = [_pair_rows(do_ref[pl.ds(r0, CHUNK), :]) for r0 in r0s]
            scs = [_pair_scores(wt, kp[pl.ds(r0, BANDP), :], b_ref[...], r0, masked) for wt, r0 in zip(wts, r0s)]
            dps = [lax.dot_general(do, vp[pl.ds(r0, BANDP), :], nt, preferred_element_type=f32) for do, r0 in zip(dos, r0s)]
            tn_ = (((0,), (0,)), ((), ()))
            for r0, wt, do, sc, dp in zip(r0s, wts, dos, scs, dps):
                p = _softmax(sc, -1)
                ds = p * (dp - jnp.sum(p * dp, axis=-1, keepdims=True))
                g_ref[...] += ds
                dsb = ds.astype(bf16)
                dq = jnp.dot(dsb, kp[pl.ds(r0, BANDP), :], preferred_element_type=f32)
                dq_ref[pl.ds(r0, CHUNK), :] = (_pair_diag(dq) * (HD ** -0.5)).astype(bf16)
                dkp[pl.ds(r0, BANDP), :] += lax.dot_general(dsb, wt, tn_, preferred_element_type=f32)
                dvp[pl.ds(r0, BANDP), :] += lax.dot_general(p.astype(bf16), do, tn_, preferred_element_type=f32)

        _chunk_loops(nc, group, ATT_GROUP)
        dk_ref[...] = dkp[PADK:PADK + s, :].astype(bf16)
        dv_ref[...] = dvp[PADK:PADK + s, :].astype(bf16)

    col = lambda off: pl.BlockSpec((s, 128), lambda p: (0, off + p))
    return pl.pallas_call(
        body, name="attn_bwd", grid=(npair,),
        in_specs=[col(0), col(npair), col(2 * npair), col(0), pl.BlockSpec((None, 2 * CHUNK, BANDP), lambda p: (p, 0, 0))],
        out_specs=[col(0), col(0), col(0), pl.BlockSpec((None, 2 * CHUNK, BANDP), lambda p: (p, 0, 0))],
        out_shape=[_sds((s, ATT_W), bf16)] * 3 + [_sds((npair, 2 * CHUNK, BANDP), f32)],
        scratch_shapes=[pltpu.VMEM((rows, 128), bf16)] * 2 + [pltpu.VMEM((rows, 128), f32)] * 2,
        compiler_params=_params(("parallel",)),
    )(qkv, qkv, qkv, datt, bias2)


def _rel_tables():
    onehot = np.zeros((BANDP, N_REL), np.float32)
    for j in range(BAND + CHUNK - 1):
        o = j - (CHUNK - 1)
        onehot[j, int(np.clip(PADK - o, -(CHUNK - 1), REL_CLIP)) + CHUNK - 1] = 1.0
    return onehot, np.ascontiguousarray(np.eye(CHUNK, dtype=np.float32)[::-1])


def _expand_bias(rel):
    ext = jnp.concatenate([jnp.broadcast_to(rel[:, N_REL - 1:], (NH, N_REL - 1)), rel[:, ::-1],
                           jnp.zeros((NH, BANDP - BAND + 1), f32)], axis=1)
    band = jnp.stack([ext[:, CHUNK - 1 - q:CHUNK - 1 - q + BANDP] for q in range(CHUNK)], axis=1)
    band = jnp.where(np.arange(BANDP) < BAND, band, -jnp.inf)
    return band.reshape(NH // 2, 2 * CHUNK, BANDP)


def _rel_bias_grad(gband):
    def body(g_ref, m_ref, flip_ref, o_ref, d2):
        for h in range(NH):
            rev = jnp.dot(flip_ref[...], g_ref[h], precision=HIGHEST, preferred_element_type=f32)
            rolled = pltpu.roll(rev, 0, 1, stride=1, stride_axis=0)
            d2[h:h + 1, :] = jnp.sum(rolled, axis=0, keepdims=True)
        o_ref[...] = jnp.dot(d2[...], m_ref[...], precision=HIGHEST, preferred_element_type=f32)

    onehot, flip = _rel_tables()
    return pl.pallas_call(
        body, name="rel_bias_grad", out_shape=_sds((NH, N_REL), f32), scratch_shapes=[pltpu.VMEM((NH, BANDP), f32)],
    )(gband, jnp.asarray(onehot), jnp.asarray(flip))


XBC_BLK = 512
XBC_COL0 = SSD_W // XBC_BLK
DT_COL = (SSD_W + XBC) // 128


def _conv_taps(ext, w_ref, b_ref, tm):
    n = ext.shape[0]
    pre = w_ref[3:4, :] * ext + b_ref[...]
    for j in range(3):
        pre = pre + w_ref[j:j + 1, :] * pltpu.roll(ext, 3 - j, 0)
    return pre


def _ssd_conv(proj2, conv_w, conv_b, tm):
    s = proj2.shape[0]
    nb = XBC // XBC_BLK

    def body(x_ref, p_ref, w_ref, b_ref, o_ref):
        i = pl.program_id(1)
        prev = jnp.where(i > 0, p_ref[...], 0.0)
        ext = jnp.concatenate([prev, x_ref[...]], axis=0)
        pre = _conv_taps(ext, w_ref, b_ref, tm)[8:8 + tm]
        o_ref[...] = pre * _sigmoid(pre)

    return pl.pallas_call(
        body, name="ssd_conv", grid=(nb, s // tm),
        in_specs=[pl.BlockSpec((tm, XBC_BLK), lambda j, i: (i, XBC_COL0 + j)),
                  pl.BlockSpec((8, XBC_BLK), lambda j, i: (jnp.maximum(i * (tm // 8) - 1, 0), XBC_COL0 + j)),
                  pl.BlockSpec((4, XBC_BLK), lambda j, i: (0, j)), pl.BlockSpec((1, XBC_BLK), lambda j, i: (0, j))],
        out_specs=pl.BlockSpec((tm, XBC_BLK), lambda j, i: (i, j)), out_shape=_sds((s, XBC), f32),
        compiler_params=_params(("parallel", "parallel")),
    )(proj2, proj2, conv_w, conv_b)


def _ssd_conv_bwd(dxbc, proj2, conv_w, conv_b, tm):
    s = proj2.shape[0]
    nb = XBC // XBC_BLK
    n = s // tm
    last8 = s // 8 - 1

    def body(x_ref, xp_ref, xn_ref, d_ref, dn_ref, w_ref, b_ref, o_ref, dw_ref, db_ref, acc):
        i = pl.program_id(1)

        @pl.when(i == 0)
        def _():
            acc[...] = jnp.zeros_like(acc)

        prev = jnp.where(i > 0, xp_ref[...], 0.0)
        ext = jnp.concatenate([prev, x_ref[...], xn_ref[...]], axis=0)
        pre = _conv_taps(ext, w_ref, b_ref, tm)
        sg = _sigmoid(pre)
        dnext = jnp.where(i < n - 1, dn_ref[...], 0.0)
        dext = jnp.concatenate([jnp.zeros((8, XBC_BLK), f32), d_ref[...], dnext], axis=0)
        dpre = dext * (sg * (1.0 + pre * (1.0 - sg)))
        rows = tm + 16
        dx = w_ref[3:4, :] * dpre
        for j in range(3):
            dx = dx + w_ref[j:j + 1, :] * pltpu.roll(dpre, rows - (3 - j), 0)
        o_ref[...] = dx[8:8 + tm].astype(bf16)
        dcur = dpre[8:8 + tm]
        acc[4] += _fold8(dcur)
        acc[3] += _fold8(dcur * ext[8:8 + tm])
        for j in range(3):
            acc[j] += _fold8(dcur * pltpu.roll(ext, 3 - j, 0)[8:8 + tm])

        @pl.when(i == n - 1)
        def _():
            for j in range(4):
                dw_ref[j:j + 1, :] = jnp.sum(acc[j], axis=0, keepdims=True)
            db_ref[...] = jnp.sum(acc[4], axis=0, keepdims=True)

    xcol = lambda j: XBC_COL0 + j
    return pl.pallas_call(
        body, name="ssd_conv_bwd", grid=(nb, n),
        in_specs=[pl.BlockSpec((tm, XBC_BLK), lambda j, i: (i, xcol(j))),
                  pl.BlockSpec((8, XBC_BLK), lambda j, i: (jnp.maximum(i * (tm // 8) - 1, 0), xcol(j))),
                  pl.BlockSpec((8, XBC_BLK), lambda j, i: (jnp.minimum((i + 1) * (tm // 8), last8), xcol(j))),
                  pl.BlockSpec((tm, XBC_BLK), lambda j, i: (i, j)),
                  pl.BlockSpec((8, XBC_BLK), lambda j, i: (jnp.minimum((i + 1) * (tm // 8), last8), j)),
                  pl.BlockSpec((4, XBC_BLK), lambda j, i: (0, j)), pl.BlockSpec((1, XBC_BLK), lambda j, i: (0, j))],
        out_specs=[pl.BlockSpec((tm, XBC_BLK), lambda j, i: (i, j)), pl.BlockSpec((4, XBC_BLK), lambda j, i: (0, j)),
                   pl.BlockSpec((1, XBC_BLK), lambda j, i: (0, j))],
        out_shape=[_sds((s, XBC), bf16), _sds((4, XBC), f32), _sds((1, XBC), f32)],
        scratch_shapes=[pltpu.VMEM((5, 8, XBC_BLK), f32)], compiler_params=_params(("parallel", "arbitrary")),
    )(proj2, proj2, proj2, dxbc, dxbc, conv_w, conv_b)


def _ssd_consts():
    ex = np.zeros((128, SSD_W), np.float32)
    for h in range(NH):
        ex[h, h * HD:(h + 1) * HD] = 1.0
    sel = np.zeros((8, 128), np.float32)
    for h in range(NH):
        sel[h // 2, h] = 1.0
    par = np.zeros((128, 128), np.float32)
    for r in range(128):
        for h in range(NH):
            par[r, h] = 1.0 if (h % 2) == (r // 64) else 0.0
    ones_blk = np.zeros((128, 128), np.float32)
    for r in range(128):
        ones_blk[r, (r // 64) * 64:(r // 64) * 64 + 64] = 1.0
    return ex, np.ascontiguousarray(ex.T), sel, par, ones_blk


SSD_SUB = 8


def _ssd_common(rs, xbc_ref, dtr_ref, a_ref, dtb_ref, ex_ref, sel_ref, par_ref):
    xs = xbc_ref[rs, 0:SSD_W]
    dt = _softplus(dtr_ref[rs, :] + dtb_ref[...])
    adt = dt * a_ref[...]
    r_i = lax.broadcasted_iota(jnp.int32, (CHUNK, CHUNK), 0)
    c_i = lax.broadcasted_iota(jnp.int32, (CHUNK, CHUNK), 1)
    tril = (r_i >= c_i).astype(f32)
    cs = _dot01(tril, adt, exact="a")
    cs2 = jnp.concatenate([cs, cs], axis=0) * par_ref[...]
    cstp = _dot01(sel_ref[...], cs2, tb=True, exact="a")
    both = _dot01(jnp.concatenate([dt, cs], axis=0), ex_ref[...])
    return xs, dt, cs, cstp, both[0:CHUNK], both[CHUNK:]


def _pair_mask():
    l_i = lax.broadcasted_iota(jnp.int32, (CHUNK, 128), 0)
    lane = lax.broadcasted_iota(jnp.int32, (CHUNK, 128), 1)
    return l_i >= (lane % CHUNK), lane < HD


def _block_diag(xp, first):
    z = jnp.zeros_like(xp)
    return jnp.concatenate([jnp.where(first, xp, z), jnp.where(first, z, xp)], axis=0)


def _ssd_fwd(xbc, proj2, a_row, dtb_row, dsk_full):
    s = xbc.shape[0]
    nc = s // CHUNK
    ex, ext, sel, par, ones_blk = _ssd_consts()

    def one_chunk(sub, states, refs):
        xbc_ref, dtr_ref, a_ref, dtb_ref, dsk_ref, ex_ref, sel_ref, par_ref, y_ref, hs_ref = refs
        rs = slice(sub * CHUNK, (sub + 1) * CHUNK)
        xs, dt, cs, cstp, dt_full, cs_full = _ssd_common(rs, xbc_ref, dtr_ref, a_ref, dtb_ref, ex_ref, sel_ref, par_ref)
        cs_last = cs_full[CHUNK - 1:CHUNK, :]
        xdt = xs * dt_full
        causal, first = _pair_mask()
        out = []
        for g in range(NG):
            gl = slice(g * GW, (g + 1) * GW)
            bg = xbc_ref[rs, SSD_W + g * NSTATE:SSD_W + (g + 1) * NSTATE].astype(bf16)
            cg = xbc_ref[rs, SSD_W + NG * NSTATE + g * NSTATE:SSD_W + NG * NSTATE + (g + 1) * NSTATE].astype(bf16)
            cb2 = lax.dot_general(cg, jnp.concatenate([bg, bg], axis=0), (((1,), (1,)), ((), ())), preferred_element_type=f32)
            hg = states[g]
            hs_ref[sub, g] = hg
            y0 = jnp.dot(cg, hg.astype(bf16), preferred_element_type=f32)
            yoff = jnp.exp(cs_full[:, gl]) * y0
            for j in range(GW // 128):
                pair = g * (GW // 128) + j
                pl_ = slice(pair * 128, (pair + 1) * 128)
                seg = jnp.exp(jnp.where(causal, cs_full[:, pl_] - cstp[pair:pair + 1, :], -jnp.inf))
                m = (cb2 * seg).astype(bf16)
                yd = jnp.dot(m, _block_diag(xdt[:, pl_].astype(bf16), first), preferred_element_type=f32)
                y_ref[rs, pl_] = yd + yoff[:, j * 128:(j + 1) * 128] + xs[:, pl_] * dsk_ref[:, pl_]
            xdec = (xdt[:, gl] * jnp.exp(cs_last[:, gl] - cs_full[:, gl])).astype(bf16)
            st = lax.dot_general(bg, xdec, (((0,), (0,)), ((), ())), preferred_element_type=f32)
            out.append(jnp.exp(cs_last[:, gl]) * hg + st)
        return out

    def body(*refs):
        hst = refs[-1]

        @pl.when(pl.program_id(0) == 0)
        def _():
            hst[...] = jnp.zeros_like(hst)

        states = [hst[g] for g in range(NG)]
        for sub in range(SSD_SUB):
            states = one_chunk(sub, states, refs[:-1])
        for g in range(NG):
            hst[g] = states[g]

    rows = SSD_SUB * CHUNK
    const = lambda shape: pl.BlockSpec(shape, lambda c: tuple(0 for _ in shape))
    return pl.pallas_call(
        body, name="ssd_fwd", grid=(nc // SSD_SUB,),
        in_specs=[pl.BlockSpec((rows, XBC), lambda c: (c, 0)), pl.BlockSpec((rows, 128), lambda c: (c, DT_COL)),
                  const((1, 128)), const((1, 128)), const((1, SSD_W)), const((128, SSD_W)), const((8, 128)), const((128, 128))],
        out_specs=[pl.BlockSpec((rows, SSD_W), lambda c: (c, 0)), pl.BlockSpec((SSD_SUB, NG, NSTATE, GW), lambda c: (c, 0, 0, 0))],
        out_shape=[_sds((s, SSD_W), f32), _sds((nc, NG, NSTATE, GW), f32)],
        scratch_shapes=[pltpu.VMEM((NG, NSTATE, GW), f32)], compiler_params=_params(("arbitrary",)),
    )(xbc, proj2, a_row, dtb_row, dsk_full, jnp.asarray(ex), jnp.asarray(sel), jnp.asarray(par))


def _ssd_bwd(xbc, proj2, dy, hsave, a_row, dtb_row, dsk_full):
    s = xbc.shape[0]
    nc = s // CHUNK
    ex, ext, sel, par, ones_blk = _ssd_consts()

    def one_chunk(sub, dhs, refs):
        (xbc_ref, dtr_ref, dy_ref, hs_ref, a_ref, dtb_ref, dsk_ref, ex_ref, ext_ref, sel_ref, par_ref, ob_ref,
         dxbc_ref, ddtr_ref, dd_ref, da_ref, ddtb_ref, dh, a_dd, a_da, a_dtb, dcs_lane, dcs_b, dxdt) = refs
        rs = slice(sub * CHUNK, (sub + 1) * CHUNK)
        dcs_lane, dcs_b, dxdt = dcs_lane.at[sub], dcs_b.at[sub], dxdt.at[sub]
        xs, dt, cs, cstp, dt_full, cs_full = _ssd_common(rs, xbc_ref, dtr_ref, a_ref, dtb_ref, ex_ref, sel_ref, par_ref)
        cs_last = cs_full[CHUNK - 1:CHUNK, :]
        xdt = xs * dt_full
        dyv = dy_ref[rs, :]
        a_dd[...] += _fold8(dyv * xs)
        causal, first = _pair_mask()
        diag = lax.broadcasted_iota(jnp.int32, (CHUNK, 128), 0) == lax.broadcasted_iota(jnp.int32, (CHUNK, 128), 1) % CHUNK
        dh_out = []
        for g in range(NG):
            gl = slice(g * GW, (g + 1) * GW)
            bcol = slice(SSD_W + g * NSTATE, SSD_W + (g + 1) * NSTATE)
            ccol = slice(SSD_W + NG * NSTATE + g * NSTATE, SSD_W + NG * NSTATE + (g + 1) * NSTATE)
            bg = xbc_ref[rs, bcol].astype(bf16)
            cg = xbc_ref[rs, ccol].astype(bf16)
            bg2 = jnp.concatenate([bg, bg], axis=0)
            cb2 = lax.dot_general(cg, bg2, (((1,), (1,)), ((), ())), preferred_element_type=f32)
            hg = hs_ref[sub, g]
            hgb = hg.astype(bf16)
            dhg = dhs[g]
            dhgb = dhg.astype(bf16)
            eg = jnp.exp(cs_full[:, gl])
            dec = jnp.exp(cs_last[:, gl] - cs_full[:, gl])
            gam = jnp.exp(cs_last[:, gl])
            dyg = dyv[:, gl]
            xdt_g = xdt[:, gl]
            y0 = jnp.dot(cg, hgb, preferred_element_type=f32)
            dy0 = (eg * dyg).astype(bf16)
            dcm = lax.dot_general(dy0, hgb, (((1,), (1,)), ((), ())), preferred_element_type=f32)
            dh_prev = gam * dhg + lax.dot_general(cg, dy0, (((0,), (0,)), ((), ())), preferred_element_type=f32)
            dgam = jnp.sum(dhg * hg, axis=0, keepdims=True) * gam
            dxdec = jnp.dot(bg, dhgb, preferred_element_type=f32)
            dbm = lax.dot_general((xdt_g * dec).astype(bf16), dhgb, (((1,), (1,)), ((), ())), preferred_element_type=f32)
            t = dxdec * xdt_g * dec
            dcs_lane[:, gl] = dyg * eg * y0 - t
            dcs_lane[CHUNK - 1:CHUNK, gl] += jnp.sum(t, axis=0, keepdims=True) + dgam
            dxdt[:, gl] = dxdec * dec
            dcb2 = jnp.zeros((CHUNK, 128), f32)
            for j in range(GW // 128):
                pair = g * (GW // 128) + j
                pl_ = slice(pair * 128, (pair + 1) * 128)
                seg = jnp.exp(jnp.where(causal, cs_full[:, pl_] - cstp[pair:pair + 1, :], -jnp.inf))
                m = cb2 * seg
                mb = m.astype(bf16)
                rhs = _block_diag(xdt[:, pl_].astype(bf16), first)
                dyp = dyv[:, pl_].astype(bf16)
                dm = lax.dot_general(dyp, rhs, (((1,), (1,)), ((), ())), preferred_element_type=f32)
                tt = lax.dot_general(mb, dyp, (((0,), (0,)), ((), ())), preferred_element_type=f32)
                dxdt[:, pl_] += jnp.where(first, tt[0:CHUNK], tt[CHUNK:])
                dcb2 = dcb2 + dm * seg
                w = dm * m
                colsum = jnp.sum(w, axis=0, keepdims=True)
                dcs_b[:, pl_] = _dot01(w - jnp.where(diag, colsum, 0.0), ob_ref[...])
            dcb2b = dcb2.astype(bf16)
            dcm = dcm + jnp.dot(dcb2b, bg2, preferred_element_type=f32)
            t3 = lax.dot_general(dcb2b, cg, (((0,), (0,)), ((), ())), preferred_element_type=f32)
            dxbc_ref[rs, bcol] = dbm + t3[0:CHUNK] + t3[CHUNK:]
            dxbc_ref[rs, ccol] = dcm
            dh_out.append(dh_prev)
        dxdtv = dxdt[...]
        both = _dot01(jnp.concatenate([dcs_lane[...] + dcs_b[...] * (1.0 / HD), dxdtv * xs], axis=0), ext_ref[...])
        dcs = both[0:CHUNK]
        r_i = lax.broadcasted_iota(jnp.int32, (CHUNK, CHUNK), 0)
        c_i = lax.broadcasted_iota(jnp.int32, (CHUNK, CHUNK), 1)
        triu = (r_i <= c_i).astype(f32)
        da_ = _dot01(triu, dcs, exact="a")
        ddt = da_ * a_ref[...] + both[CHUNK:]
        a_da[...] += _fold8(da_ * dt)
        dxbc_ref[rs, 0:SSD_W] = dyv * dsk_ref[...] + dxdtv * dt_full
        ddtr = ddt * _sigmoid(dtr_ref[rs, :] + dtb_ref[...])
        ddtr_ref[rs, :] = ddtr
        a_dtb[...] += _fold8(ddtr)
        return dh_out

    nsteps = nc // SSD_SUB

    def body(*refs):
        dd_ref, da_ref, ddtb_ref, dh, a_dd, a_da, a_dtb = refs[14:21]
        ext_ref = refs[8]
        step = pl.program_id(0)

        @pl.when(step == 0)
        def _():
            dh[...] = jnp.zeros_like(dh)
            a_dd[...] = jnp.zeros_like(a_dd)
            a_da[...] = jnp.zeros_like(a_da)
            a_dtb[...] = jnp.zeros_like(a_dtb)

        dhs = [dh[g] for g in range(NG)]
        for sub in reversed(range(SSD_SUB)):
            dhs = one_chunk(sub, dhs, refs)
        for g in range(NG):
            dh[g] = dhs[g]

        @pl.when(step == nsteps - 1)
        def _():
            dd_ref[...] = jnp.sum(jnp.dot(a_dd[...], ext_ref[...], precision=HIGHEST, preferred_element_type=f32), axis=0, keepdims=True)
            da_ref[...] = jnp.sum(a_da[...], axis=0, keepdims=True)
            ddtb_ref[...] = jnp.sum(a_dtb[...], axis=0, keepdims=True)

    rev = lambda c: nsteps - 1 - c
    rows = SSD_SUB * CHUNK
    const = lambda shape: pl.BlockSpec(shape, lambda c: tuple(0 for _ in shape))
    return pl.pallas_call(
        body, name="ssd_bwd", grid=(nsteps,),
        in_specs=[pl.BlockSpec((rows, XBC), lambda c: (rev(c), 0)), pl.BlockSpec((rows, 128), lambda c: (rev(c), DT_COL)),
                  pl.BlockSpec((rows, SSD_W), lambda c: (rev(c), 0)), pl.BlockSpec((SSD_SUB, NG, NSTATE, GW), lambda c: (rev(c), 0, 0, 0)),
                  const((1, 128)), const((1, 128)), const((1, SSD_W)), const((128, SSD_W)), const((SSD_W, 128)),
                  const((8, 128)), const((128, 128)), const((128, 128))],
        out_specs=[pl.BlockSpec((rows, XBC), lambda c: (rev(c), 0)), pl.BlockSpec((rows, 128), lambda c: (rev(c), 0)),
                   const((1, 128)), const((1, 128)), const((1, 128))],
        out_shape=[_sds((s, XBC), f32), _sds((s, 128), f32), _sds((1, 128), f32), _sds((1, 128), f32), _sds((1, 128), f32)],
        scratch_shapes=[pltpu.VMEM((NG, NSTATE, GW), f32), pltpu.VMEM((8, SSD_W), f32), pltpu.VMEM((8, 128), f32), pltpu.VMEM((8, 128), f32)]
        + [pltpu.VMEM((SSD_SUB, CHUNK, SSD_W), f32)] * 3,
        compiler_params=_params(("arbitrary",)),
    )(xbc, proj2, dy, hsave, a_row, dtb_row, dsk_full, jnp.asarray(ex), jnp.asarray(ext), jnp.asarray(sel), jnp.asarray(par),
      jnp.asarray(ones_blk))


def _local_step(x, tgt, mods, g_mix, rel, conv_w, conv_b, dt_bias, a_log, d_skip, g_att, g_ssd, g_ffn, g_final, weights):
    s = x.shape[0]
    tm_e = 512 if s % 512 == 0 else s
    tm_m = 512 if s % 512 == 0 else s
    tm_l = 1024 if s % 1024 == 0 else s
    tk = 2048 if s % 2048 == 0 else s
    sh1, sc1, gt1, sh2, sc2, gt2 = [mods[:, i * D:(i + 1) * D] for i in range(6)]

    h1b = _norm_mod("norm_mod_1", x, g_mix, sc1, sh1, tm_e)
    win, win_b = weights.w_in(h1b)
    qkv = _mm_nn_fullk("proj_qkv", h1b, win, tm_l, 1536, bf16, n=IN_A)
    proj2 = _mm_nn_fullk("proj_zxbcdt", h1b, win_b, tm_l, 896, f32)
    bias = _expand_bias(rel)
    att = _attn_fwd(qkv, bias)
    xbc = _ssd_conv(proj2, conv_w, conv_b, tm_l)
    a_row = jnp.pad(-jnp.exp(a_log), ((0, 0), (0, 128 - NH)))
    dtb_row = jnp.pad(dt_bias, ((0, 0), (0, 128 - NH)))
    dsk_full = jnp.repeat(d_skip, HD, axis=1)
    y, hsave = _ssd_fwd(xbc, proj2, a_row, dtb_row, dsk_full)
    mixcat = _mix_pre(att, y, proj2, g_att, g_ssd, tm_e)
    wout = weights.w_out(mixcat)
    mix = _mm_nn_fullk("proj_out", mixcat, wout, tm_l, D, f32)
    x2, h2b = _resid_norm_mod(x, gt1, mix, g_ffn, sc2, sh2, tm_e)
    wg4, wu4, wd4 = weights.ffn(h2b)
    act, sil, ud, act_t = _ffn_up(h2b, wg4, wu4, tm_m)
    ffn = _ffn_down(act, wd4, tm_m)

    dx3, dffn, loss, dg_final, dgt2 = _final_fwd_bwd(x2, ffn, gt2, g_final, tgt, tm_e)
    gwd4 = _grad_wdown4(act_t, dffn, 1024, tk)
    dgate, dup = _ffn_dact(dffn, wd4, sil, ud, tm_l)
    tk2 = 4096 if s % 4096 == 0 else s
    tok = weights.grad(("w_down", "w_gate", "w_up"),
                       [gwd4, _grad_cols4("grad_w_gate", h2b, dgate, 512, tk2), _grad_cols4("grad_w_up", h2b, dup, 512, tk2)])
    dh2 = _ffn_dh(dgate, dup, wg4, wu4, tm_m, dep=tok)
    dx2, dmix, dsc2, dsh2, dg_ffn, dgt1 = _norm_mod_bwd("norm_mod_bwd_2", dh2, x2, g_ffn, sc2, dx3, tm_e, mix=mix, gt=gt1)
    gwout4 = _mm_tn("grad_w_out", mixcat, dmix, 512, 1024, tk2, bf16).reshape(NSH, D // NSH, D)
    dmc = _mm_nt("dmixcat", dmix, wout, tm_l, D, D, f32)
    datt, dy, dz, dg_att, dg_ssd = _mix_pre_bwd(dmc, att, y, proj2, g_att, g_ssd, tm_e)
    dq, dk, dv, gband = _attn_bwd(qkv, datt, bias)
    drel = _rel_bias_grad(gband.reshape(NH, CHUNK, BANDP))
    dxbc, ddtr, dd_row, da_row, ddtb_row = _ssd_bwd(xbc, proj2, dy, hsave, a_row, dtb_row, dsk_full)
    dxbc_raw, dconv_w, dconv_b = _ssd_conv_bwd(dxbc, proj2, conv_w, conv_b, tm_e)
    dproj = jnp.concatenate([dq, dk, dv, dz, dxbc_raw, ddtr.astype(bf16)], axis=1)
    gwin = _mm_tn("grad_w_in", h1b, dproj, 512, 1152, tk2, bf16)
    gwin4 = jnp.stack([jnp.pad(gwin[:, k * IN_SH:(k + 1) * IN_SH], ((0, 0), (0, IN_SHP - IN_SH))) for k in range(NSH)])
    tok = weights.grad(("w_out", "w_in"), [gwout4, gwin4])
    dh1 = _mm_nt("dh1", dproj, win, tm_m, 1024, IN_P, f32, dep=tok)
    grad_x, dsc1, dsh1, dg_mix = _norm_mod_bwd("norm_mod_bwd_1", dh1, x, g_mix, sc1, dx2, tm_e)

    dmods = jnp.concatenate([dsh1, dsc1, dgt1, dsh2, dsc2, dgt2], axis=1)
    dd_skip = dd_row[:, :NH]
    da_log = da_row[:, :NH] * a_row[:, :NH]
    small = dict(g_mix=dg_mix, conv_b=dconv_b, dt_bias=ddtb_row[:, :NH], a_log=da_log, d_skip=dd_skip, g_att_out=dg_att,
                 g_ssd_out=dg_ssd, g_ffn=dg_ffn, g_final=dg_final, rel_bias=drel, conv_w=dconv_w)
    return loss[0, 0], grad_x, dmods, small


HBM = pl.BlockSpec(memory_space=pl.ANY)
VMEM = pl.BlockSpec(memory_space=pltpu.VMEM)


def _place():
    x, y, c = lax.axis_index("x"), lax.axis_index("y"), lax.axis_index("c")
    chips = [(1 - x, y), (x, 1 - y), (1 - x, 1 - y)]
    return x, y, c, chips


def _allgather8(name, payload, dep=None):
    r = payload.shape[0]
    deps = [] if dep is None else [dep]

    def body(x_ref, *rest):
        out_ref, send_sems, recv_sems, local_sem = rest[-4:]
        x, y, c, chips = _place()
        me, sibling = (x, y, c), (x, y, 1 - c)

        def slot(px, py, pc):
            return out_ref.at[4 * px + 2 * py + pc]

        def copy(k, block, to, src=None):
            return pltpu.make_async_remote_copy(
                src_ref=slot(*block) if src is None else src, dst_ref=slot(*block),
                send_sem=send_sems.at[k], recv_sem=recv_sems.at[k], device_id=to, device_id_type=MESH)

        mine = pltpu.make_async_copy(x_ref, slot(*me), local_sem)
        mine.start()
        first = [copy(0, me, sibling, src=x_ref)]
        first += [copy(1 + j, me, (*chip, c), src=x_ref) for j, chip in enumerate(chips)]
        for cp in first:
            cp.start()
        passed = [copy(4 + j, (*chip, c), sibling) for j, chip in enumerate(chips)]
        for j, chip in enumerate(chips):
            copy(1 + j, (*chip, c), me).wait_recv()
            passed[j].start()
        copy(0, sibling, me).wait_recv()
        for j, chip in enumerate(chips):
            copy(4 + j, (*chip, 1 - c), me).wait_recv()
        for cp in first + passed:
            cp.wait_send()
        mine.wait()

    return pl.pallas_call(
        body, name=name, out_shape=_sds((N_DEV, r, 128), f32), in_specs=[VMEM] * (1 + len(deps)), out_specs=VMEM,
        scratch_shapes=[pltpu.SemaphoreType.DMA((7,)), pltpu.SemaphoreType.DMA((7,)), pltpu.SemaphoreType.DMA],
    )(payload, *deps)


def _sum8(g):
    r = g.shape[1]

    def body(g_ref, o_ref):
        acc = g_ref[0]
        for i in range(1, N_DEV):
            acc = acc + g_ref[i]
        o_ref[...] = acc

    return pl.pallas_call(body, name="sum8", out_shape=_sds((r, 128), f32))(g)


SEM = pl.BlockSpec(memory_space=pltpu.SEMAPHORE)
EFFECT = pltpu.SideEffectType.DATAFLOW_SIDE_EFFECTING


def _gather_copies(ins, lands, send_sems, recv_sems):
    x, y, c, chips = _place()
    k = 2 * x + y
    starts, recvs = [], []
    for w in range(len(ins)):
        for j, (px, py) in enumerate(chips):
            def mk(dst):
                return pltpu.make_async_remote_copy(src_ref=ins[w].at[c], dst_ref=dst, send_sem=send_sems[w].at[j],
                                                    recv_sem=recv_sems[w].at[j], device_id=(px, py, c), device_id_type=MESH)
            starts.append(mk(lands[w].at[k, c]))
            recvs.append(mk(lands[w].at[2 * px + py, c]))
    return starts, recvs


def _reduce_copies(ins, lands, send_sems, recv_sems):
    x, y, c, chips = _place()
    k = 2 * x + y
    starts, recvs = [], []
    for w in range(len(ins)):
        for j, (px, py) in enumerate(chips):
            def mk(dst):
                return pltpu.make_async_remote_copy(src_ref=ins[w].at[2 * px + py], dst_ref=dst, send_sem=send_sems[w].at[j],
                                                    recv_sem=recv_sems[w].at[j], device_id=(px, py, c), device_id_type=MESH)
            starts.append(mk(lands[w].at[k]))
            recvs.append(mk(lands[w].at[2 * px + py]))
    return starts, recvs


def _split_start(name, copies, srcs, land_shapes):
    nw = len(srcs)

    def body(*refs):
        starts, _ = copies(refs[:nw], refs[nw:2 * nw], refs[2 * nw:3 * nw], refs[3 * nw:4 * nw])
        for cp in starts:
            cp.start()
        refs[6 * nw][...] = jnp.zeros((8, 128), f32)

    sems = [pltpu.SemaphoreType.DMA((3,))] * nw
    bufs = [pltpu.HBM(s.shape, bf16) for s in srcs] + [pltpu.HBM(s, bf16) for s in land_shapes]
    res = pl.pallas_call(
        body, name=name, out_shape=sems + sems + bufs + [_sds((8, 128), f32)],
        in_specs=[HBM] * (2 * nw), out_specs=[SEM] * (2 * nw) + [HBM] * (2 * nw) + [VMEM],
        input_output_aliases={i: 2 * nw + i for i in range(2 * nw)},
        compiler_params=pltpu.CompilerParams(has_side_effects=EFFECT),
    )(*[pltpu.with_memory_space_constraint(s, pltpu.HBM) for s in srcs],
      *[pltpu.with_memory_space_constraint(lax.empty(s, bf16), pltpu.HBM) for s in land_shapes])
    return res[:nw], res[nw:2 * nw], res[2 * nw:3 * nw], res[3 * nw:4 * nw], res[4 * nw]


def _split_wait(name, copies, send_sems, recv_sems, srcs, lands, after):
    nw = len(srcs)

    def body(*refs):
        starts, recvs = copies(refs[:nw], refs[nw:2 * nw], refs[2 * nw:3 * nw], refs[3 * nw:4 * nw])
        for s_, r_ in zip(starts, recvs):
            s_.wait_send()
            r_.wait_recv()

    bufs = [pltpu.HBM(s.shape, bf16) for s in srcs] + [pltpu.HBM(l.shape, bf16) for l in lands]
    res = pl.pallas_call(
        body, name=name, out_shape=bufs, in_specs=[HBM] * (2 * nw) + [SEM] * (2 * nw) + [HBM], out_specs=[HBM] * (2 * nw),
        input_output_aliases={i: i for i in range(2 * nw)},
        compiler_params=pltpu.CompilerParams(has_side_effects=EFFECT),
    )(*srcs, *lands, *send_sems, *recv_sems, after)
    return res[:nw], res[nw:]


def _gather_forward(name, shards, lands):
    nw = len(shards)

    def body(*refs):
        ins, lands_in, outs = refs[:nw], refs[nw:2 * nw], refs[2 * nw:3 * nw]
        st_a, st_b, st_c = refs[3 * nw:4 * nw], refs[4 * nw:5 * nw], refs[5 * nw:6 * nw]
        send_sems, recv_sems, load_sems, store_sems = refs[6 * nw:]
        x, y, c, chips = _place()
        k = 2 * x + y
        sibling = (x, y, 1 - c)
        ld_a = [pltpu.make_async_copy(ins[w].at[c], st_a[w], load_sems.at[w, 0]) for w in range(nw)]
        ld_b = [pltpu.make_async_copy(ins[w].at[1 - c], st_b[w], load_sems.at[w, 1]) for w in range(nw)]
        for cp in ld_a + ld_b:
            cp.start()
        st_own = []
        for w in range(nw):
            ld_a[w].wait()
            st_own.append(pltpu.make_async_copy(st_a[w], outs[w].at[k, c], store_sems.at[w, 0]))
            st_own[-1].start()
        for w in range(nw):
            ld_b[w].wait()
            st_own.append(pltpu.make_async_copy(st_b[w], outs[w].at[k, 1 - c], store_sems.at[w, 1]))
            st_own[-1].start()
        for cp in st_own:
            cp.wait()
        fwds = {}
        for j, (px, py) in enumerate(chips):
            kq = 2 * px + py
            for w in range(nw):
                slot = st_b[w] if j % 2 == 0 else st_c[w]
                if j == 2:
                    fwds[w, 0].wait_send()
                ld = pltpu.make_async_copy(lands_in[w].at[kq, c], slot, load_sems.at[w, 2 + j])
                ld.start()
                ld.wait()
                fwds[w, j] = pltpu.make_async_remote_copy(src_ref=slot, dst_ref=outs[w].at[kq, c], send_sem=send_sems.at[w, j],
                                                          recv_sem=recv_sems.at[w, j], device_id=sibling, device_id_type=MESH)
                fwds[w, j].start()
        for j, (px, py) in enumerate(chips):
            for w in range(nw):
                pltpu.make_async_remote_copy(src_ref=st_c[w], dst_ref=outs[w].at[2 * px + py, 1 - c], send_sem=send_sems.at[w, j],
                                             recv_sem=recv_sems.at[w, j], device_id=sibling, device_id_type=MESH).wait_recv()
        for w in range(nw):
            fwds[w, 1].wait_send()
            fwds[w, 2].wait_send()

    stage = [pltpu.VMEM(s.shape[1:], bf16) for s in shards]
    return pl.pallas_call(
        body, name=name, out_shape=[_sds(l.shape, bf16) for l in lands],
        in_specs=[HBM] * (2 * nw), out_specs=[HBM] * nw, input_output_aliases={nw + w: w for w in range(nw)},
        scratch_shapes=stage * 3 + [pltpu.SemaphoreType.DMA((nw, 3)), pltpu.SemaphoreType.DMA((nw, 3)), pltpu.SemaphoreType.DMA((nw, 5)),
                                    pltpu.SemaphoreType.DMA((nw, 2))],
        compiler_params=pltpu.CompilerParams(vmem_limit_bytes=VMEM_LIMIT),
    )(*shards, *lands)


def _rs_pair_exchange(name, grads):
    nw = len(grads)

    def body(*refs):
        ins, got, stage = refs[:nw], refs[nw:2 * nw], refs[2 * nw:3 * nw]
        send_sems, recv_sems, load_sems = refs[3 * nw:]
        x, y, c, _ = _place()

        def load(w, kk):
            return pltpu.make_async_copy(ins[w].at[kk, 1 - c], stage[w].at[kk % 2], load_sems.at[w, kk])

        def send(w, kk):
            return pltpu.make_async_remote_copy(src_ref=stage[w].at[kk % 2], dst_ref=got[w].at[kk], send_sem=send_sems.at[w, kk],
                                                recv_sem=recv_sems.at[w, kk], device_id=(x, y, 1 - c), device_id_type=MESH)

        for kk in range(2):
            for w in range(nw):
                load(w, kk).start()
        for kk in range(NSH):
            for w in range(nw):
                load(w, kk).wait()
                send(w, kk).start()
            if kk + 2 < NSH:
                for w in range(nw):
                    send(w, kk).wait_send()
                    load(w, kk + 2).start()
        for kk in range(NSH - 2, NSH):
            for w in range(nw):
                send(w, kk).wait_send()
        for kk in range(NSH):
            for w in range(nw):
                send(w, kk).wait_recv()

    return pl.pallas_call(
        body, name=name, out_shape=[_sds((NSH,) + g.shape[2:], bf16) for g in grads], in_specs=[HBM] * nw, out_specs=[HBM] * nw,
        scratch_shapes=[pltpu.VMEM((2,) + g.shape[2:], bf16) for g in grads]
        + [pltpu.SemaphoreType.DMA((nw, NSH)), pltpu.SemaphoreType.DMA((nw, NSH)), pltpu.SemaphoreType.DMA((nw, NSH))],
        compiler_params=pltpu.CompilerParams(vmem_limit_bytes=VMEM_LIMIT),
    )(*grads)


def _rs_pair_gather(name, halves):
    nw = len(halves)

    def body(*refs):
        ins, outs, stage = refs[:nw], refs[nw:2 * nw], refs[2 * nw:3 * nw]
        send_sems, recv_sems, local_sems, stage_sems = refs[3 * nw:]
        x, y, c, _ = _place()
        loads = [pltpu.make_async_copy(ins[w], stage[w], stage_sems.at[w]) for w in range(nw)]
        for cp in loads:
            cp.start()
        local, cps = [], []
        for w in range(nw):
            loads[w].wait()
            local.append(pltpu.make_async_copy(stage[w], outs[w].at[c], local_sems.at[w]))
            cps.append(pltpu.make_async_remote_copy(src_ref=stage[w], dst_ref=outs[w].at[c], send_sem=send_sems.at[w],
                                                    recv_sem=recv_sems.at[w], device_id=(x, y, 1 - c), device_id_type=MESH))
            local[w].start()
            cps[w].start()
        for w in range(nw):
            pltpu.make_async_remote_copy(src_ref=stage[w], dst_ref=outs[w].at[1 - c], send_sem=send_sems.at[w], recv_sem=recv_sems.at[w],
                                         device_id=(x, y, 1 - c), device_id_type=MESH).wait_recv()
        for cp in cps:
            cp.wait_send()
        for cp in local:
            cp.wait()

    return pl.pallas_call(
        body, name=name, out_shape=[_sds((2,) + h.shape, f32) for h in halves], in_specs=[HBM] * nw, out_specs=[HBM] * nw,
        scratch_shapes=[pltpu.VMEM(h.shape, f32) for h in halves]
        + [pltpu.SemaphoreType.DMA((nw,)), pltpu.SemaphoreType.DMA((nw,)), pltpu.SemaphoreType.DMA((nw,)), pltpu.SemaphoreType.DMA((nw,))],
        compiler_params=pltpu.CompilerParams(vmem_limit_bytes=VMEM_LIMIT),
    )(*halves)


def _row_tile(r, c, nbuf):
    budget = 24 * 1024 * 1024 // (2 * nbuf * 4 * c)
    fits = [t for t in range(16, r + 1, 16) if r % t == 0 and t <= budget]
    return max(fits) if fits else r


def _cast_bf16(name, a, dep=None):
    r, c = a.shape
    tr = _row_tile(r, c, 2)
    dep_specs, dep_ops = _dep_args(dep, 1)

    def body(a_ref, *rest):
        rest[-1][...] = a_ref[...].astype(bf16)

    spec = pl.BlockSpec((tr, c), lambda i: (i, 0))
    return pl.pallas_call(body, name=name, grid=(r // tr,), in_specs=[spec] + dep_specs, out_specs=spec, out_shape=_sds((r, c), bf16),
                          compiler_params=_params(("parallel",)))(a, *dep_ops)


def _w_in_columns(win4):
    tr = 256

    def body(a_ref, o_ref, ob_ref):
        for k in range(NSH):
            o_ref[:, IN_SH * k:IN_SH * (k + 1)] = a_ref[k][:, :IN_SH]
        o_ref[:, IN_COLS:] = jnp.zeros((tr, IN_P - IN_COLS), bf16)
        ob_ref[...] = o_ref[:, IN_A:]

    return pl.pallas_call(
        body, name="w_in_columns", grid=(D // tr,), in_specs=[pl.BlockSpec((NSH, tr, IN_SHP), lambda i: (0, i, 0))],
        out_specs=[pl.BlockSpec((tr, IN_P), lambda i: (i, 0)), pl.BlockSpec((tr, IN_B), lambda i: (i, 0))],
        out_shape=[_sds((D, IN_P), bf16), _sds((D, IN_B), bf16)], compiler_params=_params(("parallel",)))(win4)


def _pair_sum(name, core, grads, got):
    _, _, rh, c = grads.shape
    tr = _row_tile(rh, c, 2)

    def body(c_ref, a_ref, b_ref, o_ref):
        o_ref[...] = (a_ref[...].astype(f32) + b_ref[...].astype(f32)).astype(bf16)

    spec = pl.BlockSpec((None, tr, c), lambda k, i, c_ref: (k, i, 0))
    return pl.pallas_call(
        body, name=name, out_shape=_sds((NSH, rh, c), bf16),
        grid_spec=pltpu.PrefetchScalarGridSpec(
            num_scalar_prefetch=1, grid=(NSH, rh // tr),
            in_specs=[pl.BlockSpec((None, None, tr, c), lambda k, i, c_ref: (k, c_ref[0], i, 0)), spec], out_specs=spec),
        compiler_params=_params(("parallel", "parallel")))(core, grads, got)


def _chip_sum(name, chip, sums, lands):
    _, rh, c = sums.shape
    tr = _row_tile(rh, c, 4)

    def body(k_ref, own_ref, l_ref, o_ref):
        own = own_ref[...].astype(f32)
        acc = None
        for j in range(NSH):
            term = jnp.where(k_ref[0] == j, own, l_ref[j].astype(f32))
            acc = term if acc is None else acc + term
        o_ref[...] = acc

    return pl.pallas_call(
        body, name=name, out_shape=_sds((rh, c), f32),
        grid_spec=pltpu.PrefetchScalarGridSpec(
            num_scalar_prefetch=1, grid=(rh // tr,),
            in_specs=[pl.BlockSpec((None, tr, c), lambda i, k_ref: (k_ref[0], i, 0)), pl.BlockSpec((NSH, tr, c), lambda i, k_ref: (0, i, 0))],
            out_specs=pl.BlockSpec((tr, c), lambda i, k_ref: (i, 0))),
        compiler_params=_params(("parallel",)))(chip, sums, lands)


def _mods_part(cond16, w_ada, b_part):
    n = w_ada.shape[1]
    tn = 512

    def body(c_ref, w_ref, b_ref, o_ref):
        cv = c_ref[...]
        o_ref[...] = _dot(cv * _sigmoid(cv), w_ref[...]) + b_ref[...]

    return pl.pallas_call(
        body, name="mods_part", grid=(n // tn,),
        in_specs=[pl.BlockSpec((16, D), lambda j: (0, 0)), pl.BlockSpec((D, tn), lambda j: (0, j)), pl.BlockSpec((1, tn), lambda j: (0, j))],
        out_specs=pl.BlockSpec((16, tn), lambda j: (0, j)), out_shape=_sds((16, n), f32), compiler_params=_params(("parallel",)),
    )(cond16, w_ada, b_part)


def _grad_w_ada(cond16, dm16):
    n = dm16.shape[1]
    tr = 256

    def body(c_ref, d_ref, o_ref):
        cv = c_ref[...]
        o_ref[...] = _dot(cv * _sigmoid(cv), d_ref[...], ta=True)

    return pl.pallas_call(
        body, name="grad_w_ada", grid=(D // tr,),
        in_specs=[pl.BlockSpec((16, tr), lambda i: (0, i)), pl.BlockSpec((16, n), lambda i: (0, 0))],
        out_specs=pl.BlockSpec((tr, n), lambda i: (i, 0)), out_shape=_sds((D, n), f32), compiler_params=_params(("parallel",)),
    )(cond16, dm16)


def _adamw(name, w, g, m, v):
    r, c = w.shape
    tr = _row_tile(r, c, 7)
    spec = pl.BlockSpec((tr, c), lambda i: (i, 0))
    grid = (r // tr,)

    def body(w_ref, g_ref, m_ref, v_ref, d_ref, nm_ref, nv_ref):
        gv = g_ref[...]
        nm = ADAM_B1 * m_ref[...] + (1.0 - ADAM_B1) * gv
        nv = ADAM_B2 * v_ref[...] + (1.0 - ADAM_B2) * (gv * gv)
        nm_ref[...] = nm
        nv_ref[...] = nv
        m_hat = nm / (1.0 - ADAM_B1 ** ADAM_STEP)
        v_hat = nv / (1.0 - ADAM_B2 ** ADAM_STEP)
        d_ref[...] = -ADAM_LR * (m_hat / (jnp.sqrt(v_hat) + ADAM_EPS) + ADAM_WD * w_ref[...])

    return pl.pallas_call(body, name=name, grid=grid, in_specs=[spec] * 4, out_specs=[spec] * 3, out_shape=[_sds(w.shape, f32)] * 3,
                          compiler_params=_params(("parallel",)))(w, g, m, v)


def _pack(parts, rows):
    flat = []
    for p in parts:
        p = p.reshape(-1)
        flat.append(jnp.pad(p, (0, (-p.shape[0]) % 128)))
    v = jnp.concatenate(flat)
    return jnp.pad(v, (0, rows * 128 - v.shape[0])).reshape(rows, 128)


def _unpack(packed, sizes):
    lead = packed.shape[:-2]
    flat = packed.reshape(lead + (-1,))
    out, off = [], 0
    for n in sizes:
        out.append(flat[..., off:off + n])
        off += n + (-n) % 128
    return out


BIG = ("w_in", "w_out", "w_gate", "w_up", "w_down")
SMALL = ("b_ada", "g_mix", "conv_b", "dt_bias", "a_log", "d_skip", "g_att_out", "g_ssd_out", "g_ffn", "g_final", "rel_bias", "conv_w")
ORDER = ("w_ada", "b_ada", "g_mix", "w_in", "rel_bias", "conv_w", "conv_b", "dt_bias", "a_log", "d_skip", "g_att_out", "g_ssd_out",
         "w_out", "g_ffn", "w_gate", "w_up", "w_down", "g_final")
REL_SH = N_REL // NSH
CONVW_SH = XBC // NSH
ADA_SH = 6 * D // NSH


class _Exchange:
    def __init__(self, core, chip):
        self.core, self.chip = core, chip
        self.gathered = {}
        self.pending = []

    def gather(self, names, shards):
        ssem, rsem, thru, lands, token = _split_start("gather_start_" + "_".join(names), _gather_copies, shards,
                                                      [(NSH,) + s.shape for s in shards])
        self.gathered.update({n: (ssem[i], rsem[i], thru[i], lands[i]) for i, n in enumerate(names)})
        return token

    def _whole(self, names, after):
        ssem, rsem, thru, lands = zip(*[self.gathered[n] for n in names])
        tag = "_".join(names)
        thru, lands = _split_wait("gather_wait_" + tag, _gather_copies, ssem, rsem, thru, lands, after)
        return _gather_forward("gather_forward_" + tag, thru, lands)

    def w_in(self, after):
        (win4,) = self._whole(("w_in",), after)
        return _w_in_columns(win4.reshape(NSH, D, IN_SHP))

    def w_out(self, after):
        wout4, wg4, wu4, wd4 = self._whole(("w_out", "w_gate", "w_up", "w_down"), after)
        self.ffn_weights = wg4.reshape(NSH, D, FSH), wu4.reshape(NSH, D, FSH), wd4.reshape(NSH, FSH, D)
        return wout4.reshape(D, D)

    def ffn(self, after):
        return self.ffn_weights

    def grad(self, names, grads):
        tag = "_".join(names)
        stacked = [g.reshape(NSH, 2, g.shape[1] // 2, g.shape[2]) for g in grads]
        got = _rs_pair_exchange("rs_pair_exchange_" + tag, stacked)
        sums = [_pair_sum("pair_sum_" + n, self.core, o, g) for n, o, g in zip(names, stacked, got)]
        self.pending.append((names, _split_start("rs_start_" + tag, _reduce_copies, sums, [s.shape for s in sums])))
        return self.pending[-1][1][4]

    def finish(self, after):
        grads = {}
        for names, (ssem, rsem, sums, lands, _) in self.pending:
            tag = "_".join(names)
            sums, lands = _split_wait("rs_wait_" + tag, _reduce_copies, ssem, rsem, sums, lands, after)
            halves = [_chip_sum("chip_sum_" + n, self.chip, sm, ld) for n, sm, ld in zip(names, sums, lands)]
            for n, f in zip(names, _rs_pair_gather("rs_pair_gather_" + tag, halves)):
                grads[n] = f.reshape(2 * f.shape[1], f.shape[2])
        return grads


def kernel(x, c, w_ada, b_ada, g_mix, w_in, rel_bias, conv_w, conv_b, dt_bias, a_log, d_skip, g_att_out, g_ssd_out, w_out, g_ffn, w_gate, w_up, w_down, g_final, loss_target, m_w_ada, m_b_ada, m_g_mix, m_w_in, m_rel_bias, m_conv_w, m_conv_b, m_dt_bias, m_a_log, m_d_skip, m_g_att_out, m_g_ssd_out, m_w_out, m_g_ffn, m_w_gate, m_w_up, m_w_down, m_g_final, v_w_ada, v_b_ada, v_g_mix, v_w_in, v_rel_bias, v_conv_w, v_conv_b, v_dt_bias, v_a_log, v_d_skip, v_g_att_out, v_g_ssd_out, v_w_out, v_g_ffn, v_w_gate, v_w_up, v_w_down, v_g_final):
    args = dict(locals())
    w = {n: args[n] for n in ORDER}
    m = {n: args["m_" + n] for n in ORDER}
    v = {n: args["v_" + n] for n in ORDER}
    ix, iy, ic = lax.axis_index("x"), lax.axis_index("y"), lax.axis_index("c")
    chip = 2 * ix + iy
    dev = 2 * chip + ic
    s = x.shape[1]

    g1 = _allgather8("gather_inputs", _pack([c[0], rel_bias[0], conv_w[0]], 40))
    c_all, rel_sh, convw_sh = _unpack(g1, [D, NH * REL_SH, 4 * CONVW_SH])
    rel_full = jnp.concatenate([rel_sh[2 * k].reshape(NH, REL_SH) for k in range(NSH)], axis=1)
    convw_full = jnp.concatenate([convw_sh[2 * k].reshape(4, CONVW_SH) for k in range(NSH)], axis=1)
    cond16 = jnp.pad(c_all, ((0, 8), (0, 0)))
    b_part = lax.dynamic_slice_in_dim(b_ada, chip * ADA_SH, ADA_SH, axis=1)
    mods_part = _mods_part(cond16, w_ada[0], b_part)[:N_DEV]
    g2 = _allgather8("gather_mods", mods_part.reshape(N_DEV * ADA_SH // 128, 128))
    mods_all = jnp.concatenate([g2[2 * k].reshape(N_DEV, ADA_SH) for k in range(NSH)], axis=1)
    mods = lax.dynamic_slice_in_dim(mods_all, dev, 1, axis=0)

    exchange = _Exchange(jnp.reshape(ic, (1,)).astype(jnp.int32), jnp.reshape(chip, (1,)).astype(jnp.int32))
    shard_in = _cast_bf16("cast_w_in", jnp.pad(w_in[0], ((0, 0), (0, IN_SHP - IN_SH))), dep=g2[0, :8]).reshape(2, D // 2, IN_SHP)
    tok = exchange.gather(("w_in",), [shard_in])
    tok = exchange.gather(("w_out", "w_gate", "w_up", "w_down"), [
        _cast_bf16("cast_w_out", w_out[0], dep=tok).reshape(2, D // NSH // 2, D),
        _cast_bf16("cast_w_gate", w_gate[0], dep=tok).reshape(2, D // 2, FSH),
        _cast_bf16("cast_w_up", w_up[0], dep=tok).reshape(2, D // 2, FSH),
        _cast_bf16("cast_w_down", w_down[0], dep=tok).reshape(2, FSH // 2, D)])
    mods = mods + tok[:1, :1]

    loss, grad_x, dmods, small = _local_step(
        x[0], loss_target[0], mods, g_mix, rel_full, convw_full, conv_b, dt_bias, a_log, d_skip, g_att_out, g_ssd_out, g_ffn,
        g_final[None, :], exchange)

    small_names = ("g_mix", "conv_b", "dt_bias", "a_log", "d_skip", "g_att_out", "g_ssd_out", "g_ffn", "g_final", "rel_bias", "conv_w")
    g3 = _allgather8("gather_small_grads", _pack([dmods] + [small[n] for n in small_names], 264))
    sizes = [6 * D] + [int(np.prod(small[n].shape)) for n in small_names]
    dmods_all = _unpack(g3, sizes)[0]
    summed = _unpack(_sum8(g3), sizes)
    grads = {"b_ada": summed[0].reshape(1, 6 * D)}
    for n, val in zip(small_names, summed[1:]):
        grads[n] = val.reshape(small[n].shape)
    grads["rel_bias"] = lax.dynamic_slice_in_dim(grads["rel_bias"], chip * REL_SH, REL_SH, axis=1)
    grads["conv_w"] = lax.dynamic_slice_in_dim(grads["conv_w"], chip * CONVW_SH, CONVW_SH, axis=1)
    grads["g_final"] = grads["g_final"].reshape(D)
    dm16 = jnp.pad(lax.dynamic_slice_in_dim(dmods_all, chip * ADA_SH, ADA_SH, axis=1), ((0, 8), (0, 0)))
    grads["w_ada"] = _grad_w_ada(cond16, dm16)

    delta, new_m, new_v = {}, {}, {}
    delta["w_ada"], new_m["w_ada"], new_v["w_ada"] = _adamw("adamw_w_ada", w_ada[0], grads["w_ada"], m_w_ada[0], v_w_ada[0])
    grads.update(exchange.finish(grad_x))
    grads["w_in"] = grads["w_in"][:, :IN_SH]
    for n in BIG:
        delta[n], new_m[n], new_v[n] = _adamw("adamw_" + n, w[n][0], grads[n], m[n][0], v[n][0])
    sw = _pack([w[n] for n in SMALL], 200)
    sg = _pack([grads[n] for n in SMALL], 200)
    sm = _pack([m[n] for n in SMALL], 200)
    sv = _pack([v[n] for n in SMALL], 200)
    ssz = [int(np.prod(w[n].shape)) for n in SMALL]
    for dst, packed in zip((delta, new_m, new_v), _adamw("adamw_small", sw, sg, sm, sv)):
        for n, val in zip(SMALL, _unpack(packed, ssz)):
            dst[n] = val

    def shaped(d, n):
        return d[n].reshape(w[n].shape)

    total = lax.psum(loss, ("x", "y", "c"))
    return (total, grad_x[None], *[shaped(grads, n) for n in ORDER], *[shaped(delta, n) for n in ORDER],
            *[shaped(new_m, n) for n in ORDER], *[shaped(new_v, n) for n in ORDER])
```
